```python
import math
import jax, jax.numpy as jnp
from jax import lax
import numpy as np

D_MODEL = 1024
BATCH = 8
SEQ = 2048
DEPTH = 1

ATTN_HEADS = 8
ATTN_KV_HEADS = 2
HEAD_DIM = 64
ATTN_WIDTH = ATTN_HEADS * HEAD_DIM
KV_WIDTH = ATTN_KV_HEADS * HEAD_DIM
WINDOW = 128
BLOCK = 128
SSM_CH_PER_GROUP = 16
SSM_WIDTH = D_MODEL - ATTN_WIDTH
SSM_GROUPS = SSM_WIDTH // SSM_CH_PER_GROUP
SSM_STATE = 64
DT_MIN = 1e-3
DT_MAX = 1e-1
MIX_WIDTH = ATTN_WIDTH + SSM_WIDTH
IN_WIDTH = ATTN_WIDTH + 2 * KV_WIDTH + SSM_WIDTH
D_FF = 2816
EPS = 1e-6
NEG_INF = -1e30
LAMBDA_RE_MAX = -1e-4

kernel_name = "hymba_swa_s5_macaron_block"


def _alibi_slopes(n_heads):
    return jnp.asarray(2.0 ** (-8.0 * (np.arange(n_heads) + 1) / n_heads), dtype=jnp.float32)


def _rmsnorm(x, g):
    x32 = x.astype(jnp.float32)
    y = x32 * lax.rsqrt(jnp.mean(x32 * x32, axis=-1, keepdims=True) + EPS)
    return (y * g.astype(jnp.float32)).astype(x.dtype)


def _swiglu(x, w_gate, w_up, w_down):
    return (jax.nn.silu(x @ w_gate) * (x @ w_up)) @ w_down


def _window_attention(q, k, v, sinks):
    b, l = q.shape[0], q.shape[1]
    nb = l // BLOCK
    gq = ATTN_HEADS // ATTN_KV_HEADS
    qb = q.reshape(b, nb, BLOCK, ATTN_KV_HEADS, gq, HEAD_DIM)

    def band(t):
        tp = jnp.pad(t, ((0, 0), (BLOCK, BLOCK), (0, 0), (0, 0)))
        tp = tp.reshape(b, nb + 2, BLOCK, ATTN_KV_HEADS, HEAD_DIM)
        return jnp.concatenate([tp[:, :-2], tp[:, 1:-1], tp[:, 2:]], axis=2)

    kw, vw = band(k), band(v)
    scores = jnp.einsum('bnqkgd,bnskd->bnkgqs', qb, kw).astype(jnp.float32) * (HEAD_DIM ** -0.5)
    qi = jnp.arange(BLOCK)[:, None]
    kj = jnp.arange(3 * BLOCK)[None, :]
    rel = kj - BLOCK - qi
    key_pos = jnp.arange(nb)[:, None, None] * BLOCK - BLOCK + kj[None]
    valid = (jnp.abs(rel) <= WINDOW)[None] & (key_pos >= 0) & (key_pos < l)
    slopes = _alibi_slopes(ATTN_HEADS).reshape(ATTN_KV_HEADS, gq)
    alibi = -slopes[:, :, None, None] * jnp.abs(rel).astype(jnp.float32)
    scores = jnp.where(valid[None, :, None, None], scores + alibi, NEG_INF)
    sink = jnp.broadcast_to(sinks.astype(jnp.float32).reshape(1, 1, ATTN_KV_HEADS, gq, 1, 1),
                            scores.shape[:-1] + (1,))
    probs = jax.nn.softmax(jnp.concatenate([scores, sink], axis=-1), axis=-1)[..., :-1]
    out = jnp.einsum('bnkgqs,bnskd->bnqkgd', probs.astype(v.dtype), vw)
    return out.reshape(b, l, ATTN_WIDTH)


def _s5_direction(u, lam_re, lam_im, log_dt, b_re, b_im, c_re, c_im, reverse):
    f32 = jnp.float32
    lr = jnp.minimum(lam_re.astype(f32), LAMBDA_RE_MAX)
    li = lam_im.astype(f32)
    dt = jnp.exp(log_dt.astype(f32))[:, None]
    mag = jnp.exp(lr * dt)
    a_re = mag * jnp.cos(li * dt)
    a_im = mag * jnp.sin(li * dt)
    den = lr * lr + li * li
    coef_re = ((a_re - 1.0) * lr + a_im * li) / den
    coef_im = (a_im * lr - (a_re - 1.0) * li) / den
    br, bi = b_re.astype(f32), b_im.astype(f32)
    bb_re = coef_re[..., None] * br - coef_im[..., None] * bi
    bb_im = coef_re[..., None] * bi + coef_im[..., None] * br
    bu_re = jnp.einsum('blgh,gph->blgp', u, bb_re)
    bu_im = jnp.einsum('blgh,gph->blgp', u, bb_im)
    ar = jnp.broadcast_to(a_re, bu_re.shape)
    ai = jnp.broadcast_to(a_im, bu_re.shape)

    def combine(e1, e2):
        ar1, ai1, xr1, xi1 = e1
        ar2, ai2, xr2, xi2 = e2
        return (ar2 * ar1 - ai2 * ai1,
                ar2 * ai1 + ai2 * ar1,
                ar2 * xr1 - ai2 * xi1 + xr2,
                ar2 * xi1 + ai2 * xr1 + xi2)

    _, _, xr, xi = lax.associative_scan(combine, (ar, ai, bu_re, bu_im), reverse=reverse, axis=1)
    return (jnp.einsum('blgp,ghp->blgh', xr, c_re.astype(f32))
            - jnp.einsum('blgp,ghp->blgh', xi, c_im.astype(f32)))


def _s5_mixer(u, lam_re, lam_im, log_dt, b_re, b_im, c_re, c_im, d_skip, glu_w, glu_b):
    b, l, _ = u.shape
    ug = u.astype(jnp.float32).reshape(b, l, SSM_GROUPS, SSM_CH_PER_GROUP)
    y = d_skip.astype(jnp.float32) * ug
    for direction in range(2):
        y = y + _s5_direction(ug, lam_re[direction], lam_im[direction], log_dt[direction],
                              b_re[direction], b_im[direction], c_re[direction], c_im[direction],
                              reverse=(direction == 1))
    y = jax.nn.gelu(y.reshape(b, l, SSM_WIDTH)).astype(u.dtype)
    return y * jax.nn.sigmoid(y @ glu_w + glu_b)


def _fwd_setup_inputs(seed: int = 0) -> dict:
    key = jax.random.key(seed)
    ks = iter(jax.random.split(key, 40))
    f32 = jnp.float32

    def nrm(shape, scale):
        return jax.random.normal(next(ks), shape, f32) * scale

    def gain(shape):
        return 1.0 + 0.01 * jax.random.normal(next(ks), shape, f32)

    L, G, P, Hc = DEPTH, SSM_GROUPS, SSM_STATE, SSM_CH_PER_GROUP
    lam_im_init = jnp.pi * jnp.arange(P, dtype=f32)
    return {
        "x": jax.random.normal(next(ks), (BATCH, SEQ, D_MODEL), f32),
        "norm_ffn1": gain((L, D_MODEL)),
        "ffn1_w_gate": nrm((L, D_MODEL, D_FF), D_MODEL ** -0.5),
        "ffn1_w_up": nrm((L, D_MODEL, D_FF), D_MODEL ** -0.5),
        "ffn1_w_down": nrm((L, D_FF, D_MODEL), D_FF ** -0.5),
        "norm_mix": gain((L, D_MODEL)),
        "w_in": nrm((L, D_MODEL, IN_WIDTH), D_MODEL ** -0.5),
        "attn_sinks": nrm((L, ATTN_HEADS), 0.5),
        "ssm_lambda_re": -0.5 + nrm((L, 2, G, P), 0.01),
        "ssm_lambda_im": lam_im_init + nrm((L, 2, G, P), 0.01),
        "ssm_log_dt": jax.random.uniform(next(ks), (L, 2, G), f32,
                                         minval=math.log(DT_MIN), maxval=math.log(DT_MAX)),
        "ssm_b_re": nrm((L, 2, G, P, Hc), (2.0 * Hc) ** -0.5),
        "ssm_b_im": nrm((L, 2, G, P, Hc), (2.0 * Hc) ** -0.5),
        "ssm_c_re": nrm((L, 2, G, Hc, P), (2.0 * P) ** -0.5),
        "ssm_c_im": nrm((L, 2, G, Hc, P), (2.0 * P) ** -0.5),
        "ssm_d": 1.0 + nrm((L, G, Hc), 0.1),
        "ssm_glu_w": nrm((L, SSM_WIDTH, SSM_WIDTH), SSM_WIDTH ** -0.5),
        "ssm_glu_b": nrm((L, SSM_WIDTH), 0.01),
        "attn_out_norm": gain((L, ATTN_WIDTH)),
        "ssm_out_norm": gain((L, SSM_WIDTH)),
        "w_out": nrm((L, MIX_WIDTH, D_MODEL), MIX_WIDTH ** -0.5),
        "norm_ffn2": gain((L, D_MODEL)),
        "ffn2_w_gate": nrm((L, D_MODEL, D_FF), D_MODEL ** -0.5),
        "ffn2_w_up": nrm((L, D_MODEL, D_FF), D_MODEL ** -0.5),
        "ffn2_w_down": nrm((L, D_FF, D_MODEL), D_FF ** -0.5),
        "final_norm": gain((D_MODEL,)),
    }


def _fwd_reference(x, norm_ffn1, ffn1_w_gate, ffn1_w_up, ffn1_w_down, norm_mix, w_in, attn_sinks,
              ssm_lambda_re, ssm_lambda_im, ssm_log_dt, ssm_b_re, ssm_b_im, ssm_c_re, ssm_c_im,
              ssm_d, ssm_glu_w, ssm_glu_b, attn_out_norm, ssm_out_norm, w_out,
              norm_ffn2, ffn2_w_gate, ffn2_w_up, ffn2_w_down, final_norm):
    b, l, _ = x.shape
    for layer in range(DEPTH):
        x = x + 0.5 * _swiglu(_rmsnorm(x, norm_ffn1[layer]),
                              ffn1_w_gate[layer], ffn1_w_up[layer], ffn1_w_down[layer])
        h = _rmsnorm(x, norm_mix[layer])
        proj = h @ w_in[layer]
        q, k, v, u = jnp.split(proj, [ATTN_WIDTH, ATTN_WIDTH + KV_WIDTH,
                                      ATTN_WIDTH + 2 * KV_WIDTH], axis=-1)
        attn = _window_attention(q.reshape(b, l, ATTN_HEADS, HEAD_DIM),
                                 k.reshape(b, l, ATTN_KV_HEADS, HEAD_DIM),
                                 v.reshape(b, l, ATTN_KV_HEADS, HEAD_DIM),
                                 attn_sinks[layer])
        ssm = _s5_mixer(u, ssm_lambda_re[layer], ssm_lambda_im[layer], ssm_log_dt[layer],
                        ssm_b_re[layer], ssm_b_im[layer], ssm_c_re[layer], ssm_c_im[layer],
                        ssm_d[layer], ssm_glu_w[layer], ssm_glu_b[layer])
        mixed = jnp.concatenate([_rmsnorm(attn, attn_out_norm[layer]),
                                 _rmsnorm(ssm, ssm_out_norm[layer])], axis=-1)
        x = x + mixed @ w_out[layer]
        x = x + 0.5 * _swiglu(_rmsnorm(x, norm_ffn2[layer]),
                              ffn2_w_gate[layer], ffn2_w_up[layer], ffn2_w_down[layer])
    return _rmsnorm(x, final_norm)


import jax as _jax
import jax.numpy as _jnp

TWIN_FORMAT = 'train_step'
FWD_PARAMS = ['x', 'norm_ffn1', 'ffn1_w_gate', 'ffn1_w_up', 'ffn1_w_down', 'norm_mix', 'w_in', 'attn_sinks', 'ssm_lambda_re', 'ssm_lambda_im', 'ssm_log_dt', 'ssm_b_re', 'ssm_b_im', 'ssm_c_re', 'ssm_c_im', 'ssm_d', 'ssm_glu_w', 'ssm_glu_b', 'attn_out_norm', 'ssm_out_norm', 'w_out', 'norm_ffn2', 'ffn2_w_gate', 'ffn2_w_up', 'ffn2_w_down', 'final_norm']
TWIN_WEIGHTS = ['norm_ffn1', 'ffn1_w_gate', 'ffn1_w_up', 'ffn1_w_down', 'norm_mix', 'w_in', 'attn_sinks', 'ssm_lambda_re', 'ssm_lambda_im', 'ssm_log_dt', 'ssm_b_re', 'ssm_b_im', 'ssm_c_re', 'ssm_c_im', 'ssm_d', 'ssm_glu_w', 'ssm_glu_b', 'attn_out_norm', 'ssm_out_norm', 'w_out', 'norm_ffn2', 'ffn2_w_gate', 'ffn2_w_up', 'ffn2_w_down', 'final_norm']
TWIN_DIFF_INPUT = 'x'
TWIN_INPUTS = ['x', 'norm_ffn1', 'ffn1_w_gate', 'ffn1_w_up', 'ffn1_w_down', 'norm_mix', 'w_in', 'attn_sinks', 'ssm_lambda_re', 'ssm_lambda_im', 'ssm_log_dt', 'ssm_b_re', 'ssm_b_im', 'ssm_c_re', 'ssm_c_im', 'ssm_d', 'ssm_glu_w', 'ssm_glu_b', 'attn_out_norm', 'ssm_out_norm', 'w_out', 'norm_ffn2', 'ffn2_w_gate', 'ffn2_w_up', 'ffn2_w_down', 'final_norm', 'loss_target', 'm_norm_ffn1', 'm_ffn1_w_gate', 'm_ffn1_w_up', 'm_ffn1_w_down', 'm_norm_mix', 'm_w_in', 'm_attn_sinks', 'm_ssm_lambda_re', 'm_ssm_lambda_im', 'm_ssm_log_dt', 'm_ssm_b_re', 'm_ssm_b_im', 'm_ssm_c_re', 'm_ssm_c_im', 'm_ssm_d', 'm_ssm_glu_w', 'm_ssm_glu_b', 'm_attn_out_norm', 'm_ssm_out_norm', 'm_w_out', 'm_norm_ffn2', 'm_ffn2_w_gate', 'm_ffn2_w_up', 'm_ffn2_w_down', 'm_final_norm', 'v_norm_ffn1', 'v_ffn1_w_gate', 'v_ffn1_w_up', 'v_ffn1_w_down', 'v_norm_mix', 'v_w_in', 'v_attn_sinks', 'v_ssm_lambda_re', 'v_ssm_lambda_im', 'v_ssm_log_dt', 'v_ssm_b_re', 'v_ssm_b_im', 'v_ssm_c_re', 'v_ssm_c_im', 'v_ssm_d', 'v_ssm_glu_w', 'v_ssm_glu_b', 'v_attn_out_norm', 'v_ssm_out_norm', 'v_w_out', 'v_norm_ffn2', 'v_ffn2_w_gate', 'v_ffn2_w_up', 'v_ffn2_w_down', 'v_final_norm']
TWIN_OUTPUTS = ['loss', 'grad_x', 'grad_norm_ffn1', 'grad_ffn1_w_gate', 'grad_ffn1_w_up', 'grad_ffn1_w_down', 'grad_norm_mix', 'grad_w_in', 'grad_attn_sinks', 'grad_ssm_lambda_re', 'grad_ssm_lambda_im', 'grad_ssm_log_dt', 'grad_ssm_b_re', 'grad_ssm_b_im', 'grad_ssm_c_re', 'grad_ssm_c_im', 'grad_ssm_d', 'grad_ssm_glu_w', 'grad_ssm_glu_b', 'grad_attn_out_norm', 'grad_ssm_out_norm', 'grad_w_out', 'grad_norm_ffn2', 'grad_ffn2_w_gate', 'grad_ffn2_w_up', 'grad_ffn2_w_down', 'grad_final_norm', 'delta_norm_ffn1', 'delta_ffn1_w_gate', 'delta_ffn1_w_up', 'delta_ffn1_w_down', 'delta_norm_mix', 'delta_w_in', 'delta_attn_sinks', 'delta_ssm_lambda_re', 'delta_ssm_lambda_im', 'delta_ssm_log_dt', 'delta_ssm_b_re', 'delta_ssm_b_im', 'delta_ssm_c_re', 'delta_ssm_c_im', 'delta_ssm_d', 'delta_ssm_glu_w', 'delta_ssm_glu_b', 'delta_attn_out_norm', 'delta_ssm_out_norm', 'delta_w_out', 'delta_norm_ffn2', 'delta_ffn2_w_gate', 'delta_ffn2_w_up', 'delta_ffn2_w_down', 'delta_final_norm', 'new_m_norm_ffn1', 'new_m_ffn1_w_gate', 'new_m_ffn1_w_up', 'new_m_ffn1_w_down', 'new_m_norm_mix', 'new_m_w_in', 'new_m_attn_sinks', 'new_m_ssm_lambda_re', 'new_m_ssm_lambda_im', 'new_m_ssm_log_dt', 'new_m_ssm_b_re', 'new_m_ssm_b_im', 'new_m_ssm_c_re', 'new_m_ssm_c_im', 'new_m_ssm_d', 'new_m_ssm_glu_w', 'new_m_ssm_glu_b', 'new_m_attn_out_norm', 'new_m_ssm_out_norm', 'new_m_w_out', 'new_m_norm_ffn2', 'new_m_ffn2_w_gate', 'new_m_ffn2_w_up', 'new_m_ffn2_w_down', 'new_m_final_norm', 'new_v_norm_ffn1', 'new_v_ffn1_w_gate', 'new_v_ffn1_w_up', 'new_v_ffn1_w_down', 'new_v_norm_mix', 'new_v_w_in', 'new_v_attn_sinks', 'new_v_ssm_lambda_re', 'new_v_ssm_lambda_im', 'new_v_ssm_log_dt', 'new_v_ssm_b_re', 'new_v_ssm_b_im', 'new_v_ssm_c_re', 'new_v_ssm_c_im', 'new_v_ssm_d', 'new_v_ssm_glu_w', 'new_v_ssm_glu_b', 'new_v_attn_out_norm', 'new_v_ssm_out_norm', 'new_v_w_out', 'new_v_norm_ffn2', 'new_v_ffn2_w_gate', 'new_v_ffn2_w_up', 'new_v_ffn2_w_down', 'new_v_final_norm']
TWIN_LEAF_KINDS = {'loss': 'loss', 'grad_x': 'grad_x', 'grad_norm_ffn1': 'grad_w', 'grad_ffn1_w_gate': 'grad_w', 'grad_ffn1_w_up': 'grad_w', 'grad_ffn1_w_down': 'grad_w', 'grad_norm_mix': 'grad_w', 'grad_w_in': 'grad_w', 'grad_attn_sinks': 'grad_w', 'grad_ssm_lambda_re': 'grad_w', 'grad_ssm_lambda_im': 'grad_w', 'grad_ssm_log_dt': 'grad_w', 'grad_ssm_b_re': 'grad_w', 'grad_ssm_b_im': 'grad_w', 'grad_ssm_c_re': 'grad_w', 'grad_ssm_c_im': 'grad_w', 'grad_ssm_d': 'grad_w', 'grad_ssm_glu_w': 'grad_w', 'grad_ssm_glu_b': 'grad_w', 'grad_attn_out_norm': 'grad_w', 'grad_ssm_out_norm': 'grad_w', 'grad_w_out': 'grad_w', 'grad_norm_ffn2': 'grad_w', 'grad_ffn2_w_gate': 'grad_w', 'grad_ffn2_w_up': 'grad_w', 'grad_ffn2_w_down': 'grad_w', 'grad_final_norm': 'grad_w', 'delta_norm_ffn1': 'delta_w', 'delta_ffn1_w_gate': 'delta_w', 'delta_ffn1_w_up': 'delta_w', 'delta_ffn1_w_down': 'delta_w', 'delta_norm_mix': 'delta_w', 'delta_w_in': 'delta_w', 'delta_attn_sinks': 'delta_w', 'delta_ssm_lambda_re': 'delta_w', 'delta_ssm_lambda_im': 'delta_w', 'delta_ssm_log_dt': 'delta_w', 'delta_ssm_b_re': 'delta_w', 'delta_ssm_b_im': 'delta_w', 'delta_ssm_c_re': 'delta_w', 'delta_ssm_c_im': 'delta_w', 'delta_ssm_d': 'delta_w', 'delta_ssm_glu_w': 'delta_w', 'delta_ssm_glu_b': 'delta_w', 'delta_attn_out_norm': 'delta_w', 'delta_ssm_out_norm': 'delta_w', 'delta_w_out': 'delta_w', 'delta_norm_ffn2': 'delta_w', 'delta_ffn2_w_gate': 'delta_w', 'delta_ffn2_w_up': 'delta_w', 'delta_ffn2_w_down': 'delta_w', 'delta_final_norm': 'delta_w', 'new_m_norm_ffn1': 'new_m', 'new_m_ffn1_w_gate': 'new_m', 'new_m_ffn1_w_up': 'new_m', 'new_m_ffn1_w_down': 'new_m', 'new_m_norm_mix': 'new_m', 'new_m_w_in': 'new_m', 'new_m_attn_sinks': 'new_m', 'new_m_ssm_lambda_re': 'new_m', 'new_m_ssm_lambda_im': 'new_m', 'new_m_ssm_log_dt': 'new_m', 'new_m_ssm_b_re': 'new_m', 'new_m_ssm_b_im': 'new_m', 'new_m_ssm_c_re': 'new_m', 'new_m_ssm_c_im': 'new_m', 'new_m_ssm_d': 'new_m', 'new_m_ssm_glu_w': 'new_m', 'new_m_ssm_glu_b': 'new_m', 'new_m_attn_out_norm': 'new_m', 'new_m_ssm_out_norm': 'new_m', 'new_m_w_out': 'new_m', 'new_m_norm_ffn2': 'new_m', 'new_m_ffn2_w_gate': 'new_m', 'new_m_ffn2_w_up': 'new_m', 'new_m_ffn2_w_down': 'new_m', 'new_m_final_norm': 'new_m', 'new_v_norm_ffn1': 'new_v', 'new_v_ffn1_w_gate': 'new_v', 'new_v_ffn1_w_up': 'new_v', 'new_v_ffn1_w_down': 'new_v', 'new_v_norm_mix': 'new_v', 'new_v_w_in': 'new_v', 'new_v_attn_sinks': 'new_v', 'new_v_ssm_lambda_re': 'new_v', 'new_v_ssm_lambda_im': 'new_v', 'new_v_ssm_log_dt': 'new_v', 'new_v_ssm_b_re': 'new_v', 'new_v_ssm_b_im': 'new_v', 'new_v_ssm_c_re': 'new_v', 'new_v_ssm_c_im': 'new_v', 'new_v_ssm_d': 'new_v', 'new_v_ssm_glu_w': 'new_v', 'new_v_ssm_glu_b': 'new_v', 'new_v_attn_out_norm': 'new_v', 'new_v_ssm_out_norm': 'new_v', 'new_v_w_out': 'new_v', 'new_v_norm_ffn2': 'new_v', 'new_v_ffn2_w_gate': 'new_v', 'new_v_ffn2_w_up': 'new_v', 'new_v_ffn2_w_down': 'new_v', 'new_v_final_norm': 'new_v'}


def _forward(args):
    return _fwd_reference(*[args[k] for k in FWD_PARAMS])


def _output_shape():
    out = _jax.eval_shape(lambda: _forward(_fwd_setup_inputs(0)))
    return out.shape, out.dtype

N_MICROBATCH = 1
ADAM_LR = 0.001
ADAM_B1 = 0.9
ADAM_B2 = 0.999
ADAM_EPS = 1e-08
ADAM_WD = 0.01
ADAM_STEP = 10
PER_EXAMPLE_BATCH_AXIS = {'x': 0, 'loss_target': 0}
SHARED_INPUTS = []
_WEIGHT_DTYPES = {'norm_ffn1': _jnp.float32, 'ffn1_w_gate': _jnp.float32, 'ffn1_w_up': _jnp.float32, 'ffn1_w_down': _jnp.float32, 'norm_mix': _jnp.float32, 'w_in': _jnp.float32, 'attn_sinks': _jnp.float32, 'ssm_lambda_re': _jnp.float32, 'ssm_lambda_im': _jnp.float32, 'ssm_log_dt': _jnp.float32, 'ssm_b_re': _jnp.float32, 'ssm_b_im': _jnp.float32, 'ssm_c_re': _jnp.float32, 'ssm_c_im': _jnp.float32, 'ssm_d': _jnp.float32, 'ssm_glu_w': _jnp.float32, 'ssm_glu_b': _jnp.float32, 'attn_out_norm': _jnp.float32, 'ssm_out_norm': _jnp.float32, 'w_out': _jnp.float32, 'norm_ffn2': _jnp.float32, 'ffn2_w_gate': _jnp.float32, 'ffn2_w_up': _jnp.float32, 'ffn2_w_down': _jnp.float32, 'final_norm': _jnp.float32}
MOMENT_SCALE = {'norm_ffn1': 6.740086e-02, 'ffn1_w_gate': 2.684155e-02, 'ffn1_w_up': 2.596321e-02, 'ffn1_w_down': 4.312645e-02, 'norm_mix': 1.225462e-01, 'w_in': 1.071596e-01, 'attn_sinks': 5.831571e-02, 'ssm_lambda_re': 4.728951e-03, 'ssm_lambda_im': 4.642548e-03, 'ssm_log_dt': 3.031115e+00, 'ssm_b_re': 3.202385e-03, 'ssm_b_im': 3.188936e-03, 'ssm_c_re': 6.557480e-03, 'ssm_c_im': 6.480027e-03, 'ssm_d': 1.070505e-01, 'ssm_glu_w': 2.543870e-02, 'ssm_glu_b': 3.839865e-02, 'attn_out_norm': 9.439466e-02, 'ssm_out_norm': 8.866319e-02, 'w_out': 8.881190e-02, 'norm_ffn2': 3.695547e-02, 'ffn2_w_gate': 1.579602e-02, 'ffn2_w_up': 1.533072e-02, 'ffn2_w_down': 2.551041e-02, 'final_norm': 1.602719e+01}


def _to_microbatches(a, axis):
    t = _jnp.moveaxis(a, axis, 0)
    t = t.reshape((N_MICROBATCH, t.shape[0] // N_MICROBATCH) + t.shape[1:])
    return _jnp.moveaxis(t, 1, axis + 1)


def setup_inputs(seed: int = 0) -> dict:
    inp = _fwd_setup_inputs(seed)
    key = _jax.random.fold_in(_jax.random.key(seed), 7919)
    shape, _ = _output_shape()
    out = dict(inp)
    out["loss_target"] = _jax.random.normal(_jax.random.fold_in(key, 0), shape, _jnp.float32)
    for i, name in enumerate(TWIN_WEIGHTS):
        w = inp[name].astype(_jnp.float32)
        if MOMENT_SCALE is None:
            s = _jnp.sqrt(_jnp.mean(_jnp.square(w)) + 1e-30)
        else:
            s = MOMENT_SCALE[name]
        km, kv = _jax.random.split(_jax.random.fold_in(key, i + 1))
        out[name] = w
        out["m_" + name] = s * _jax.random.normal(km, w.shape, _jnp.float32)
        out["v_" + name] = (s * s) * _jax.random.uniform(kv, w.shape, _jnp.float32, 0.5, 1.5)
    if N_MICROBATCH > 1:
        for name, axis in PER_EXAMPLE_BATCH_AXIS.items():
            out[name] = _to_microbatches(out[name], axis)
    return {'x': out['x'], 'norm_ffn1': out['norm_ffn1'], 'ffn1_w_gate': out['ffn1_w_gate'], 'ffn1_w_up': out['ffn1_w_up'], 'ffn1_w_down': out['ffn1_w_down'], 'norm_mix': out['norm_mix'], 'w_in': out['w_in'], 'attn_sinks': out['attn_sinks'], 'ssm_lambda_re': out['ssm_lambda_re'], 'ssm_lambda_im': out['ssm_lambda_im'], 'ssm_log_dt': out['ssm_log_dt'], 'ssm_b_re': out['ssm_b_re'], 'ssm_b_im': out['ssm_b_im'], 'ssm_c_re': out['ssm_c_re'], 'ssm_c_im': out['ssm_c_im'], 'ssm_d': out['ssm_d'], 'ssm_glu_w': out['ssm_glu_w'], 'ssm_glu_b': out['ssm_glu_b'], 'attn_out_norm': out['attn_out_norm'], 'ssm_out_norm': out['ssm_out_norm'], 'w_out': out['w_out'], 'norm_ffn2': out['norm_ffn2'], 'ffn2_w_gate': out['ffn2_w_gate'], 'ffn2_w_up': out['ffn2_w_up'], 'ffn2_w_down': out['ffn2_w_down'], 'final_norm': out['final_norm'], 'loss_target': out['loss_target'], 'm_norm_ffn1': out['m_norm_ffn1'], 'm_ffn1_w_gate': out['m_ffn1_w_gate'], 'm_ffn1_w_up': out['m_ffn1_w_up'], 'm_ffn1_w_down': out['m_ffn1_w_down'], 'm_norm_mix': out['m_norm_mix'], 'm_w_in': out['m_w_in'], 'm_attn_sinks': out['m_attn_sinks'], 'm_ssm_lambda_re': out['m_ssm_lambda_re'], 'm_ssm_lambda_im': out['m_ssm_lambda_im'], 'm_ssm_log_dt': out['m_ssm_log_dt'], 'm_ssm_b_re': out['m_ssm_b_re'], 'm_ssm_b_im': out['m_ssm_b_im'], 'm_ssm_c_re': out['m_ssm_c_re'], 'm_ssm_c_im': out['m_ssm_c_im'], 'm_ssm_d': out['m_ssm_d'], 'm_ssm_glu_w': out['m_ssm_glu_w'], 'm_ssm_glu_b': out['m_ssm_glu_b'], 'm_attn_out_norm': out['m_attn_out_norm'], 'm_ssm_out_norm': out['m_ssm_out_norm'], 'm_w_out': out['m_w_out'], 'm_norm_ffn2': out['m_norm_ffn2'], 'm_ffn2_w_gate': out['m_ffn2_w_gate'], 'm_ffn2_w_up': out['m_ffn2_w_up'], 'm_ffn2_w_down': out['m_ffn2_w_down'], 'm_final_norm': out['m_final_norm'], 'v_norm_ffn1': out['v_norm_ffn1'], 'v_ffn1_w_gate': out['v_ffn1_w_gate'], 'v_ffn1_w_up': out['v_ffn1_w_up'], 'v_ffn1_w_down': out['v_ffn1_w_down'], 'v_norm_mix': out['v_norm_mix'], 'v_w_in': out['v_w_in'], 'v_attn_sinks': out['v_attn_sinks'], 'v_ssm_lambda_re': out['v_ssm_lambda_re'], 'v_ssm_lambda_im': out['v_ssm_lambda_im'], 'v_ssm_log_dt': out['v_ssm_log_dt'], 'v_ssm_b_re': out['v_ssm_b_re'], 'v_ssm_b_im': out['v_ssm_b_im'], 'v_ssm_c_re': out['v_ssm_c_re'], 'v_ssm_c_im': out['v_ssm_c_im'], 'v_ssm_d': out['v_ssm_d'], 'v_ssm_glu_w': out['v_ssm_glu_w'], 'v_ssm_glu_b': out['v_ssm_glu_b'], 'v_attn_out_norm': out['v_attn_out_norm'], 'v_ssm_out_norm': out['v_ssm_out_norm'], 'v_w_out': out['v_w_out'], 'v_norm_ffn2': out['v_norm_ffn2'], 'v_ffn2_w_gate': out['v_ffn2_w_gate'], 'v_ffn2_w_up': out['v_ffn2_w_up'], 'v_ffn2_w_down': out['v_ffn2_w_down'], 'v_final_norm': out['v_final_norm']}


def _loss(weights, diff, rest, loss_target):
    with _jax.named_scope("forward"):
        args = {**rest, TWIN_DIFF_INPUT: diff, **{k: w.astype(_WEIGHT_DTYPES[k]) for k, w in weights.items()}}
        y = _forward(args)
    with _jax.named_scope("loss_head"):
        err = _jnp.square(y.astype(_jnp.float32) - loss_target)
        return 0.5 * _jnp.sum(_jnp.mean(err, axis=-1)) if err.ndim else 0.5 * err


def _adamw(w, g, m, v):
    m = ADAM_B1 * m + (1.0 - ADAM_B1) * g
    v = ADAM_B2 * v + (1.0 - ADAM_B2) * _jnp.square(g)
    m_hat = m / (1.0 - ADAM_B1 ** ADAM_STEP)
    v_hat = v / (1.0 - ADAM_B2 ** ADAM_STEP)
    delta = -ADAM_LR * (m_hat / (_jnp.sqrt(v_hat) + ADAM_EPS) + ADAM_WD * w)
    return delta, m, v


def reference(x, norm_ffn1, ffn1_w_gate, ffn1_w_up, ffn1_w_down, norm_mix, w_in, attn_sinks, ssm_lambda_re, ssm_lambda_im, ssm_log_dt, ssm_b_re, ssm_b_im, ssm_c_re, ssm_c_im, ssm_d, ssm_glu_w, ssm_glu_b, attn_out_norm, ssm_out_norm, w_out, norm_ffn2, ffn2_w_gate, ffn2_w_up, ffn2_w_down, final_norm, loss_target, m_norm_ffn1, m_ffn1_w_gate, m_ffn1_w_up, m_ffn1_w_down, m_norm_mix, m_w_in, m_attn_sinks, m_ssm_lambda_re, m_ssm_lambda_im, m_ssm_log_dt, m_ssm_b_re, m_ssm_b_im, m_ssm_c_re, m_ssm_c_im, m_ssm_d, m_ssm_glu_w, m_ssm_glu_b, m_attn_out_norm, m_ssm_out_norm, m_w_out, m_norm_ffn2, m_ffn2_w_gate, m_ffn2_w_up, m_ffn2_w_down, m_final_norm, v_norm_ffn1, v_ffn1_w_gate, v_ffn1_w_up, v_ffn1_w_down, v_norm_mix, v_w_in, v_attn_sinks, v_ssm_lambda_re, v_ssm_lambda_im, v_ssm_log_dt, v_ssm_b_re, v_ssm_b_im, v_ssm_c_re, v_ssm_c_im, v_ssm_d, v_ssm_glu_w, v_ssm_glu_b, v_attn_out_norm, v_ssm_out_norm, v_w_out, v_norm_ffn2, v_ffn2_w_gate, v_ffn2_w_up, v_ffn2_w_down, v_final_norm):
    given = dict(x=x, norm_ffn1=norm_ffn1, ffn1_w_gate=ffn1_w_gate, ffn1_w_up=ffn1_w_up, ffn1_w_down=ffn1_w_down, norm_mix=norm_mix, w_in=w_in, attn_sinks=attn_sinks, ssm_lambda_re=ssm_lambda_re, ssm_lambda_im=ssm_lambda_im, ssm_log_dt=ssm_log_dt, ssm_b_re=ssm_b_re, ssm_b_im=ssm_b_im, ssm_c_re=ssm_c_re, ssm_c_im=ssm_c_im, ssm_d=ssm_d, ssm_glu_w=ssm_glu_w, ssm_glu_b=ssm_glu_b, attn_out_norm=attn_out_norm, ssm_out_norm=ssm_out_norm, w_out=w_out, norm_ffn2=norm_ffn2, ffn2_w_gate=ffn2_w_gate, ffn2_w_up=ffn2_w_up, ffn2_w_down=ffn2_w_down, final_norm=final_norm, loss_target=loss_target, m_norm_ffn1=m_norm_ffn1, m_ffn1_w_gate=m_ffn1_w_gate, m_ffn1_w_up=m_ffn1_w_up, m_ffn1_w_down=m_ffn1_w_down, m_norm_mix=m_norm_mix, m_w_in=m_w_in, m_attn_sinks=m_attn_sinks, m_ssm_lambda_re=m_ssm_lambda_re, m_ssm_lambda_im=m_ssm_lambda_im, m_ssm_log_dt=m_ssm_log_dt, m_ssm_b_re=m_ssm_b_re, m_ssm_b_im=m_ssm_b_im, m_ssm_c_re=m_ssm_c_re, m_ssm_c_im=m_ssm_c_im, m_ssm_d=m_ssm_d, m_ssm_glu_w=m_ssm_glu_w, m_ssm_glu_b=m_ssm_glu_b, m_attn_out_norm=m_attn_out_norm, m_ssm_out_norm=m_ssm_out_norm, m_w_out=m_w_out, m_norm_ffn2=m_norm_ffn2, m_ffn2_w_gate=m_ffn2_w_gate, m_ffn2_w_up=m_ffn2_w_up, m_ffn2_w_down=m_ffn2_w_down, m_final_norm=m_final_norm, v_norm_ffn1=v_norm_ffn1, v_ffn1_w_gate=v_ffn1_w_gate, v_ffn1_w_up=v_ffn1_w_up, v_ffn1_w_down=v_ffn1_w_down, v_norm_mix=v_norm_mix, v_w_in=v_w_in, v_attn_sinks=v_attn_sinks, v_ssm_lambda_re=v_ssm_lambda_re, v_ssm_lambda_im=v_ssm_lambda_im, v_ssm_log_dt=v_ssm_log_dt, v_ssm_b_re=v_ssm_b_re, v_ssm_b_im=v_ssm_b_im, v_ssm_c_re=v_ssm_c_re, v_ssm_c_im=v_ssm_c_im, v_ssm_d=v_ssm_d, v_ssm_glu_w=v_ssm_glu_w, v_ssm_glu_b=v_ssm_glu_b, v_attn_out_norm=v_attn_out_norm, v_ssm_out_norm=v_ssm_out_norm, v_w_out=v_w_out, v_norm_ffn2=v_norm_ffn2, v_ffn2_w_gate=v_ffn2_w_gate, v_ffn2_w_up=v_ffn2_w_up, v_ffn2_w_down=v_ffn2_w_down, v_final_norm=v_final_norm)
    weights = {n: given[n] for n in TWIN_WEIGHTS}
    shared = {n: given[n] for n in SHARED_INPUTS}
    per_example = {n: given[n] for n in ['x']}
    grad_fn = _jax.value_and_grad(_loss, argnums=(0, 1))

    def one_microbatch(ex, loss_target):
        ex = dict(ex)
        diff = ex.pop(TWIN_DIFF_INPUT)
        return grad_fn(weights, diff, {**shared, **ex}, loss_target)

    if N_MICROBATCH == 1:
        loss, (grad_w, grad_x) = one_microbatch(per_example, given["loss_target"])
    else:
        def body(carry, xs):
            loss_sum, grad_sum = carry
            l_k, (gw_k, gx_k) = one_microbatch(xs[0], xs[1])
            with _jax.named_scope("update"):
                return (loss_sum + l_k, _jax.tree.map(_jnp.add, grad_sum, gw_k)), gx_k

        init = (_jnp.zeros((), _jnp.float32), _jax.tree.map(_jnp.zeros_like, weights))
        (loss, grad_w), grad_x = _jax.lax.scan(body, init, (per_example, given["loss_target"]))
    with _jax.named_scope("update"):
        delta_w, new_m, new_v = {}, {}, {}
        for n in TWIN_WEIGHTS:
            delta_w[n], new_m[n], new_v[n] = _adamw(weights[n], grad_w[n], given["m_" + n], given["v_" + n])
    return (loss, grad_x, *[grad_w[n] for n in TWIN_WEIGHTS], *[delta_w[n] for n in TWIN_WEIGHTS],
            *[new_m[n] for n in TWIN_WEIGHTS], *[new_v[n] for n in TWIN_WEIGHTS])
```

```python
import functools

import jax
import jax.numpy as jnp
from jax import lax
from jax.experimental import pallas as pl
from jax.experimental.pallas import tpu as pltpu

F32 = jnp.float32
BF16 = jnp.bfloat16

N_DEV = 8
SEQ = 2048
D_MODEL = 1024
D_FF = 2816
ATTN_HEADS = 8
KV_HEADS = 2
HEAD_DIM = 64
ATTN_WIDTH = 512
KV_WIDTH = 128
WINDOW = 128
SSM_WIDTH = 512
IN_WIDTH = 1280
EPS = 1e-6
NEG_INF = -1e30
LAMBDA_RE_MAX = -1e-4
LANES = 128
N_LANE_BLOCKS = 16
SCAN_CHUNK = SEQ // 8

ADAM_LR = 0.001
ADAM_B1 = 0.9
ADAM_B2 = 0.999
ADAM_EPS = 1e-08
ADAM_WD = 0.01
ADAM_STEP = 10

VMEM_LIMIT = 56 * 1024 * 1024
MESH_ID = pl.DeviceIdType.MESH


def _cparams(sem=None):
    return pltpu.CompilerParams(dimension_semantics=sem, vmem_limit_bytes=VMEM_LIMIT)


def _dot(a, b):
    return jnp.dot(a, b, preferred_element_type=F32)


def _dot_nt(a, b):
    return lax.dot_general(a, b, (((1,), (1,)), ((), ())), preferred_element_type=F32)


def _dot_tn(a, b):
    return lax.dot_general(a, b, (((0,), (0,)), ((), ())), preferred_element_type=F32)


def _rms_fwd(x, g):
    r = lax.rsqrt(jnp.mean(x * x, axis=-1, keepdims=True) + EPS)
    return x * r * g


def _rms_bwd(dh, x, g):
    r = lax.rsqrt(jnp.mean(x * x, axis=-1, keepdims=True) + EPS)
    xh = x * r
    dg = jnp.sum(dh * xh, axis=0, keepdims=True)
    dxh = dh * g
    dx = r * (dxh - xh * jnp.mean(dxh * xh, axis=-1, keepdims=True))
    return dx, dg


def _sigmoid(x):
    return 1.0 / (1.0 + jnp.exp(-x))


FFN_TM = 512
FFN_TF = 256


def _ffn_fwd(x, g, wgt, wut, wd, name):
    tm, tf = FFN_TM, FFN_TF
    nj = D_FF // tf

    def body(x_ref, g_ref, wg_ref, wu_ref, wd_ref, xo_ref, h_ref, a_ref, b_ref, h_s, acc):
        j = pl.program_id(1)

        @pl.when(j == 0)
        def _():
            h = _rms_fwd(x_ref[...], g_ref[...]).astype(BF16)
            h_s[...] = h
            h_ref[...] = h
            acc[...] = jnp.zeros_like(acc)

        h = h_s[...]
        a = _dot_nt(h, wg_ref[...])
        b = _dot_nt(h, wu_ref[...])
        a_ref[...] = a.astype(BF16)
        b_ref[...] = b.astype(BF16)
        s = (a * _sigmoid(a) * b).astype(BF16)
        acc[...] += _dot(s, wd_ref[...])

        @pl.when(j == nj - 1)
        def _():
            xo_ref[...] = x_ref[...] + 0.5 * acc[...]

    return pl.pallas_call(
        body, name=name, grid=(SEQ // tm, nj),
        in_specs=[pl.BlockSpec((tm, D_MODEL), lambda i, j: (i, 0)),
                  pl.BlockSpec((1, D_MODEL), lambda i, j: (0, 0)),
                  pl.BlockSpec((tf, D_MODEL), lambda i, j: (j, 0)),
                  pl.BlockSpec((tf, D_MODEL), lambda i, j: (j, 0)),
                  pl.BlockSpec((tf, D_MODEL), lambda i, j: (j, 0))],
        out_specs=[pl.BlockSpec((tm, D_MODEL), lambda i, j: (i, 0)),
                   pl.BlockSpec((tm, D_MODEL), lambda i, j: (i, 0)),
                   pl.BlockSpec((tm, tf), lambda i, j: (i, j)),
                   pl.BlockSpec((tm, tf), lambda i, j: (i, j))],
        out_shape=[jax.ShapeDtypeStruct((SEQ, D_MODEL), F32), jax.ShapeDtypeStruct((SEQ, D_MODEL), BF16),
                   jax.ShapeDtypeStruct((SEQ, D_FF), BF16), jax.ShapeDtypeStruct((SEQ, D_FF), BF16)],
        scratch_shapes=[pltpu.VMEM((tm, D_MODEL), BF16), pltpu.VMEM((tm, D_MODEL), F32)],
        compiler_params=_cparams(("parallel", "arbitrary")),
    )(x, g, wgt, wut, wd)


def _ffn_bwd_act(dxo, x, g, a, b, wgt, wut, wd, name):
    tm, tf = FFN_TM, FFN_TF
    nj = D_FF // tf

    def body(dxo_ref, x_ref, g_ref, a_ref, b_ref, wg_ref, wu_ref, wd_ref,
             dx_ref, da_ref, db_ref, s_ref, df_ref, dg_ref, df_s, acc):
        i = pl.program_id(0)
        j = pl.program_id(1)

        @pl.when(j == 0)
        def _():
            df = (0.5 * dxo_ref[...]).astype(BF16)
            df_s[...] = df
            df_ref[...] = df
            acc[...] = jnp.zeros_like(acc)

        ds = _dot_nt(df_s[...], wd_ref[...])
        av = a_ref[...].astype(F32)
        bv = b_ref[...].astype(F32)
        sig = _sigmoid(av)
        sl = av * sig
        s_ref[...] = (sl * bv).astype(BF16)
        db = (ds * sl).astype(BF16)
        da = (ds * bv * (sig * (1.0 + av * (1.0 - sig)))).astype(BF16)
        da_ref[...] = da
        db_ref[...] = db
        acc[...] += _dot(da, wg_ref[...]) + _dot(db, wu_ref[...])

        @pl.when(j == nj - 1)
        def _():
            dx, dg = _rms_bwd(acc[...], x_ref[...], g_ref[...])
            dx_ref[...] = dxo_ref[...] + dx

            @pl.when(i == 0)
            def _():
                dg_ref[...] = dg

            @pl.when(i != 0)
            def _():
                dg_ref[...] += dg

    row = lambda i, j: (i, 0)
    col = lambda i, j: (j, 0)
    tile = lambda i, j: (i, j)
    return pl.pallas_call(
        body, name=name, grid=(SEQ // tm, nj),
        in_specs=[pl.BlockSpec((tm, D_MODEL), row), pl.BlockSpec((tm, D_MODEL), row),
                  pl.BlockSpec((1, D_MODEL), lambda i, j: (0, 0)),
                  pl.BlockSpec((tm, tf), tile), pl.BlockSpec((tm, tf), tile),
                  pl.BlockSpec((tf, D_MODEL), col), pl.BlockSpec((tf, D_MODEL), col), pl.BlockSpec((tf, D_MODEL), col)],
        out_specs=[pl.BlockSpec((tm, D_MODEL), row),
                   pl.BlockSpec((tm, tf), tile), pl.BlockSpec((tm, tf), tile), pl.BlockSpec((tm, tf), tile),
                   pl.BlockSpec((tm, D_MODEL), row),
                   pl.BlockSpec((1, D_MODEL), lambda i, j: (0, 0))],
        out_shape=[jax.ShapeDtypeStruct((SEQ, D_MODEL), F32),
                   jax.ShapeDtypeStruct((SEQ, D_FF), BF16), jax.ShapeDtypeStruct((SEQ, D_FF), BF16),
                   jax.ShapeDtypeStruct((SEQ, D_FF), BF16),
                   jax.ShapeDtypeStruct((SEQ, D_MODEL), BF16),
                   jax.ShapeDtypeStruct((1, D_MODEL), F32)],
        scratch_shapes=[pltpu.VMEM((tm, D_MODEL), BF16), pltpu.VMEM((tm, D_MODEL), F32)],
        compiler_params=_cparams(("arbitrary", "arbitrary")),
    )(dxo, x, g, a, b, wgt, wut, wd)


def _mm_tn(pairs, name, tmm=256):
    m = pairs[0][0].shape[1]
    n_pairs = len(pairs)

    def body(*refs):
        ins, outs = refs[:2 * n_pairs], refs[2 * n_pairs:]
        for p in range(n_pairs):
            outs[p][...] = _dot_tn(ins[2 * p][...], ins[2 * p + 1][...]).astype(BF16)

    in_specs, out_specs, out_shape, args = [], [], [], []
    for a, b in pairs:
        n = b.shape[1]
        in_specs += [pl.BlockSpec((SEQ, tmm), lambda i: (0, i)), pl.BlockSpec((SEQ, n), lambda i: (0, 0))]
        out_specs.append(pl.BlockSpec((tmm, n), lambda i: (i, 0)))
        out_shape.append(jax.ShapeDtypeStruct((m, n), BF16))
        args += [a, b]
    return pl.pallas_call(body, name=name, grid=(m // tmm,), in_specs=in_specs, out_specs=out_specs,
                          out_shape=out_shape, compiler_params=_cparams(("parallel",)))(*args)


MIX_TM = 256


def _mixin_fwd(x, g, wint):
    tm = MIX_TM

    def body(x_ref, g_ref, w_ref, h_ref, q_ref, k_ref, v_ref, u_ref):
        h = _rms_fwd(x_ref[...], g_ref[...]).astype(BF16)
        h_ref[...] = h
        proj = _dot_nt(h, w_ref[...])
        q_ref[...] = proj[:, :ATTN_WIDTH]
        k_ref[...] = proj[:, ATTN_WIDTH:ATTN_WIDTH + KV_WIDTH]
        v_ref[...] = proj[:, ATTN_WIDTH + KV_WIDTH:ATTN_WIDTH + 2 * KV_WIDTH]
        u_ref[...] = proj[:, ATTN_WIDTH + 2 * KV_WIDTH:]

    row = lambda i: (i, 0)
    return pl.pallas_call(
        body, name="mixin_fwd", grid=(SEQ // tm,),
        in_specs=[pl.BlockSpec((tm, D_MODEL), row), pl.BlockSpec((1, D_MODEL), lambda i: (0, 0)),
                  pl.BlockSpec((IN_WIDTH, D_MODEL), lambda i: (0, 0))],
        out_specs=[pl.BlockSpec((tm, D_MODEL), row), pl.BlockSpec((tm, ATTN_WIDTH), row),
                   pl.BlockSpec((tm, KV_WIDTH), row), pl.BlockSpec((tm, KV_WIDTH), row),
                   pl.BlockSpec((tm, SSM_WIDTH), row)],
        out_shape=[jax.ShapeDtypeStruct((SEQ, D_MODEL), BF16), jax.ShapeDtypeStruct((SEQ, ATTN_WIDTH), F32),
                   jax.ShapeDtypeStruct((SEQ, KV_WIDTH), F32), jax.ShapeDtypeStruct((SEQ, KV_WIDTH), F32),
                   jax.ShapeDtypeStruct((SEQ, SSM_WIDTH), F32)],
        compiler_params=_cparams(("parallel",)),
    )(x, g, wint)


def _mixin_bwd(dq, dk, dv, du, wint, x, g, dres):
    tm = MIX_TM

    def body(dq_ref, dk_ref, dv_ref, du_ref, w_ref, x_ref, g_ref, dres_ref, dx_ref, dp_ref, dg_ref):
        i = pl.program_id(0)
        dp = jnp.concatenate([dq_ref[...], dk_ref[...], dv_ref[...], du_ref[...]], axis=-1).astype(BF16)
        dp_ref[...] = dp
        dh = _dot(dp, w_ref[...])
        dx, dg = _rms_bwd(dh, x_ref[...], g_ref[...])
        dx_ref[...] = dres_ref[...] + dx

        @pl.when(i == 0)
        def _():
            dg_ref[...] = dg

        @pl.when(i != 0)
        def _():
            dg_ref[...] += dg

    row = lambda i: (i, 0)
    const = lambda i: (0, 0)
    return pl.pallas_call(
        body, name="mixin_bwd", grid=(SEQ // tm,),
        in_specs=[pl.BlockSpec((tm, ATTN_WIDTH), row), pl.BlockSpec((tm, KV_WIDTH), row),
                  pl.BlockSpec((tm, KV_WIDTH), row), pl.BlockSpec((tm, SSM_WIDTH), row),
                  pl.BlockSpec((IN_WIDTH, D_MODEL), const), pl.BlockSpec((tm, D_MODEL), row),
                  pl.BlockSpec((1, D_MODEL), const), pl.BlockSpec((tm, D_MODEL), row)],
        out_specs=[pl.BlockSpec((tm, D_MODEL), row), pl.BlockSpec((tm, IN_WIDTH), row),
                   pl.BlockSpec((1, D_MODEL), const)],
        out_shape=[jax.ShapeDtypeStruct((SEQ, D_MODEL), F32), jax.ShapeDtypeStruct((SEQ, IN_WIDTH), BF16),
                   jax.ShapeDtypeStruct((1, D_MODEL), F32)],
        compiler_params=_cparams(("arbitrary",)),
    )(dq, dk, dv, du, wint, x, g, dres)


N_QBLOCKS = SEQ // WINDOW
GROUP = ATTN_HEADS // KV_HEADS
SCALE = HEAD_DIM ** -0.5


def _alibi_slope(h):
    return 2.0 ** (-8.0 * (h + 1) / ATTN_HEADS)


def _window_masks(n):
    t_idx = lax.broadcasted_iota(jnp.int32, (WINDOW, 3 * WINDOW), 0)
    s_idx = lax.broadcasted_iota(jnp.int32, (WINDOW, 3 * WINDOW), 1)
    rel = s_idx - WINDOW - t_idx
    absrel = jnp.abs(rel)
    key_pos = n * WINDOW - WINDOW + s_idx
    valid = (absrel <= WINDOW) & (key_pos >= 0) & (key_pos < SEQ)
    return absrel.astype(F32), valid


def _head_probs(qh, kw, absrel, valid, slope, sink):
    s = _dot_nt(qh, kw) * SCALE
    s = jnp.where(valid, s - slope * absrel, NEG_INF)
    m = jnp.maximum(jnp.max(s, axis=-1, keepdims=True), sink)
    p = jnp.exp(s - m)
    ps = jnp.exp(sink - m)
    inv = 1.0 / (jnp.sum(p, axis=-1, keepdims=True) + ps)
    return p * inv, ps * inv


def _attn_fwd(q, kp, vp, sinks):
    def body(sk_ref, q_ref, kp_ref, vp_ref, o_ref):
        def blk(n, carry):
            r0 = pl.multiple_of(n * WINDOW, WINDOW)
            absrel, valid = _window_masks(n)
            for gi in range(KV_HEADS):
                kw = kp_ref[pl.ds(r0, 3 * WINDOW), gi * HEAD_DIM:(gi + 1) * HEAD_DIM].astype(BF16)
                vw = vp_ref[pl.ds(r0, 3 * WINDOW), gi * HEAD_DIM:(gi + 1) * HEAD_DIM].astype(BF16)
                for hh in range(GROUP):
                    h = gi * GROUP + hh
                    cols = slice(h * HEAD_DIM, (h + 1) * HEAD_DIM)
                    qh = q_ref[pl.ds(r0, WINDOW), cols].astype(BF16)
                    pr, _ = _head_probs(qh, kw, absrel, valid, _alibi_slope(h), sk_ref[0, h])
                    o_ref[pl.ds(r0, WINDOW), cols] = _dot(pr.astype(BF16), vw)
            return carry

        lax.fori_loop(0, N_QBLOCKS, blk, 0)

    vmem = pl.BlockSpec(memory_space=pltpu.VMEM)
    return pl.pallas_call(
        body, name="attn_fwd",
        in_specs=[pl.BlockSpec(memory_space=pltpu.SMEM), vmem, vmem, vmem], out_specs=vmem,
        out_shape=jax.ShapeDtypeStruct((SEQ, ATTN_WIDTH), F32),
        compiler_params=_cparams(),
    )(sinks, q, kp, vp)


def _attn_bwd(q, kp, vp, sinks, do):
    def body(sk_ref, q_ref, kp_ref, vp_ref, do_ref, dq_ref, dkp_ref, dvp_ref, dsk_ref, dsk_acc):
        dkp_ref[...] = jnp.zeros_like(dkp_ref)
        dvp_ref[...] = jnp.zeros_like(dvp_ref)
        dsk_acc[...] = jnp.zeros_like(dsk_acc)

        def blk(n, carry):
            r0 = pl.multiple_of(n * WINDOW, WINDOW)
            absrel, valid = _window_masks(n)
            for gi in range(KV_HEADS):
                gcols = slice(gi * HEAD_DIM, (gi + 1) * HEAD_DIM)
                kw = kp_ref[pl.ds(r0, 3 * WINDOW), gcols].astype(BF16)
                vw = vp_ref[pl.ds(r0, 3 * WINDOW), gcols].astype(BF16)
                dkw = jnp.zeros((3 * WINDOW, HEAD_DIM), F32)
                dvw = jnp.zeros((3 * WINDOW, HEAD_DIM), F32)
                for hh in range(GROUP):
                    h = gi * GROUP + hh
                    cols = slice(h * HEAD_DIM, (h + 1) * HEAD_DIM)
                    qh = q_ref[pl.ds(r0, WINDOW), cols].astype(BF16)
                    doh = do_ref[pl.ds(r0, WINDOW), cols].astype(BF16)
                    pr, psink = _head_probs(qh, kw, absrel, valid, _alibi_slope(h), sk_ref[0, h])
                    dp = _dot_nt(doh, vw)
                    delta = jnp.sum(pr * dp, axis=-1, keepdims=True)
                    ds = (pr * (dp - delta)).astype(BF16)
                    dsk_acc[:, h:h + 1] += -(psink * delta)
                    dq_ref[pl.ds(r0, WINDOW), cols] = _dot(ds, kw) * SCALE
                    dkw = dkw + _dot_tn(ds, qh) * SCALE
                    dvw = dvw + _dot_tn(pr.astype(BF16), doh)
                dkp_ref[pl.ds(r0, 3 * WINDOW), gcols] += dkw
                dvp_ref[pl.ds(r0, 3 * WINDOW), gcols] += dvw
            return carry

        lax.fori_loop(0, N_QBLOCKS, blk, 0)
        dsk_ref[...] = jnp.sum(dsk_acc[...], axis=0, keepdims=True)

    vmem = pl.BlockSpec(memory_space=pltpu.VMEM)
    return pl.pallas_call(
        body, name="attn_bwd",
        in_specs=[pl.BlockSpec(memory_space=pltpu.SMEM), vmem, vmem, vmem, vmem],
        out_specs=[vmem, vmem, vmem, vmem],
        out_shape=[jax.ShapeDtypeStruct((SEQ, ATTN_WIDTH), F32),
                   jax.ShapeDtypeStruct((SEQ + 2 * WINDOW, KV_WIDTH), F32),
                   jax.ShapeDtypeStruct((SEQ + 2 * WINDOW, KV_WIDTH), F32),
                   jax.ShapeDtypeStruct((1, ATTN_HEADS), F32)],
        scratch_shapes=[pltpu.VMEM((WINDOW, ATTN_HEADS), F32)],
        compiler_params=_cparams(),
    )(sinks, q, kp, vp, do)


def _ssm_prep(lam_re, lam_im, log_dt, bpad_re, bpad_im):
    nb = 2 * N_LANE_BLOCKS

    def body(lr_ref, li_ref, ldt_ref, br_ref, bi_ref, ar_ref, ai_ref, bbr_ref, bbi_ref):
        lr = jnp.minimum(lr_ref[...], LAMBDA_RE_MAX)
        li = li_ref[...]
        dt = jnp.exp(ldt_ref[...])
        mag = jnp.exp(lr * dt)
        ar = mag * jnp.cos(li * dt)
        ai = mag * jnp.sin(li * dt)
        den = lr * lr + li * li
        cr = ((ar - 1.0) * lr + ai * li) / den
        ci = (ai * lr - (ar - 1.0) * li) / den
        ar_ref[...] = ar
        ai_ref[...] = ai
        for i in range(nb):
            br = br_ref[i]
            bi = bi_ref[i]
            cri, cii = cr[i:i + 1, :], ci[i:i + 1, :]
            bbr_ref[i] = (cri * br - cii * bi).astype(BF16)
            bbi_ref[i] = (cri * bi + cii * br).astype(BF16)

    return pl.pallas_call(
        body, name="ssm_prep",
        out_shape=[jax.ShapeDtypeStruct((nb, LANES), F32), jax.ShapeDtypeStruct((nb, LANES), F32),
                   jax.ShapeDtypeStruct((nb, LANES, LANES), BF16), jax.ShapeDtypeStruct((nb, LANES, LANES), BF16)],
        compiler_params=_cparams(),
    )(lam_re, lam_im, log_dt, bpad_re, bpad_im)


def _ssm_prep_bwd(lam_re, lam_im, log_dt, bpad_re, bpad_im, dar, dai, dbbr, dbbi):
    nb = 2 * N_LANE_BLOCKS

    def body(lr_ref, li_ref, ldt_ref, br_ref, bi_ref, dar_ref, dai_ref, dbbr_ref, dbbi_ref,
             glr_ref, gli_ref, gdt_ref, gbr_ref, gbi_ref, gcr_s, gci_s):
        lam = lr_ref[...]
        lr = jnp.minimum(lam, LAMBDA_RE_MAX)
        li = li_ref[...]
        dt = jnp.exp(ldt_ref[...])
        mag = jnp.exp(lr * dt)
        cs = jnp.cos(li * dt)
        sn = jnp.sin(li * dt)
        ar = mag * cs
        ai = mag * sn
        den = lr * lr + li * li
        nr = (ar - 1.0) * lr + ai * li
        ni = ai * lr - (ar - 1.0) * li
        cr = nr / den
        ci = ni / den
        for i in range(nb):
            br = br_ref[i]
            bi = bi_ref[i]
            gbbr = dbbr_ref[i]
            gbbi = dbbi_ref[i]
            cri, cii = cr[i:i + 1, :], ci[i:i + 1, :]
            gcr_s[i:i + 1, :] = jnp.sum(gbbr * br + gbbi * bi, axis=0, keepdims=True)
            gci_s[i:i + 1, :] = jnp.sum(gbbi * br - gbbr * bi, axis=0, keepdims=True)
            gbr_ref[i] = cri * gbbr + cii * gbbi
            gbi_ref[i] = cri * gbbi - cii * gbbr
        g_cr = gcr_s[...]
        g_ci = gci_s[...]
        g_nr = g_cr / den
        g_ni = g_ci / den
        g_den = -(g_cr * nr + g_ci * ni) / (den * den)
        g_ar = dar_ref[...] + g_nr * lr - g_ni * li
        g_ai = dai_ref[...] + g_nr * li + g_ni * lr
        g_lr = g_nr * (ar - 1.0) + g_ni * ai + g_den * 2.0 * lr
        g_li = g_nr * ai - g_ni * (ar - 1.0) + g_den * 2.0 * li
        g_mag = g_ar * cs + g_ai * sn
        g_th = (g_ai * cs - g_ar * sn) * mag
        g_lr = g_lr + g_mag * mag * dt
        g_li = g_li + g_th * dt
        g_dt = g_mag * mag * lr + g_th * li
        glr_ref[...] = jnp.where(lam < LAMBDA_RE_MAX, g_lr, 0.0)
        gli_ref[...] = g_li
        gl = g_dt * dt
        half = LANES // 2
        gdt_ref[:, 0:1] = jnp.sum(gl[:, :half], axis=1, keepdims=True)
        gdt_ref[:, 1:2] = jnp.sum(gl[:, half:], axis=1, keepdims=True)

    return pl.pallas_call(
        body, name="ssm_prep_bwd",
        out_shape=[jax.ShapeDtypeStruct((nb, LANES), F32), jax.ShapeDtypeStruct((nb, LANES), F32),
                   jax.ShapeDtypeStruct((nb, 2), F32),
                   jax.ShapeDtypeStruct((nb, LANES, LANES), F32), jax.ShapeDtypeStruct((nb, LANES, LANES), F32)],
        scratch_shapes=[pltpu.VMEM((nb, LANES), F32), pltpu.VMEM((nb, LANES), F32)],
        compiler_params=_cparams(),
    )(lam_re, lam_im, log_dt, bpad_re, bpad_im, dar, dai, dbbr, dbbi)


def _cmul(ar, ai, br, bi):
    return ar * br - ai * bi, ar * bi + ai * br


def _scan_inplace(re_ref, im_ref, a_re, a_im, reverse):
    nq = len(a_re)
    ch = SCAN_CHUNK
    ab_re = [jnp.broadcast_to(a, (8, LANES)) for a in a_re]
    ab_im = [jnp.broadcast_to(a, (8, LANES)) for a in a_im]

    def rows(j):
        jj = (ch - 1 - j) if reverse else j
        return pl.ds(jj, 8, stride=ch)

    def sweep(init, store):
        def step(j, st):
            out = []
            r = rows(j)
            for qi in range(nq):
                xr, xi = st[2 * qi], st[2 * qi + 1]
                pr, pi = _cmul(ab_re[qi], ab_im[qi], xr, xi)
                xr = pr + re_ref[qi, r, :]
                xi = pi + im_ref[qi, r, :]
                if store:
                    re_ref[qi, r, :] = xr
                    im_ref[qi, r, :] = xi
                out += [xr, xi]
            return tuple(out)
        return lax.fori_loop(0, ch, step, tuple(init))

    zeros = [jnp.zeros((8, LANES), F32)] * (2 * nq)
    finals = sweep(zeros, store=False)

    row_id = lax.broadcasted_iota(jnp.int32, (8, LANES), 0)
    carries = []
    for qi in range(nq):
        pr, pi = ab_re[qi], ab_im[qi]
        for _ in range(8):
            pr, pi = _cmul(pr, pi, pr, pi)
        fr, fi = finals[2 * qi], finals[2 * qi + 1]
        sr = jnp.zeros((8, LANES), F32)
        si = jnp.zeros((8, LANES), F32)
        for _ in range(7):
            tr, ti = _cmul(pr, pi, sr, si)
            tr, ti = tr + fr, ti + fi
            if reverse:
                sr = jnp.where(row_id == 7, 0.0, pltpu.roll(tr, 7, axis=0))
                si = jnp.where(row_id == 7, 0.0, pltpu.roll(ti, 7, axis=0))
            else:
                sr = jnp.where(row_id == 0, 0.0, pltpu.roll(tr, 1, axis=0))
                si = jnp.where(row_id == 0, 0.0, pltpu.roll(ti, 1, axis=0))
        carries += [sr, si]
    sweep(carries, store=True)


SSM_Q = 4


def _ssm_fwd(u, are, aim, bbr, bbi, cre, cim, dskip):
    nq = SSM_Q

    def body(u_ref, ar_ref, ai_ref, bbr_ref, bbi_ref, cr_ref, ci_ref, d_ref, y_ref, xr_ref, xi_ref, sre, sim):
        uf = u_ref[...]
        ub = uf.astype(BF16)
        y_ref[...] = d_ref[...] * uf
        for d in range(2):
            for qi in range(nq):
                sre[qi] = _dot(ub, bbr_ref[d, qi])
                sim[qi] = _dot(ub, bbi_ref[d, qi])
            _scan_inplace(sre, sim, [ar_ref[d, qi] for qi in range(nq)], [ai_ref[d, qi] for qi in range(nq)],
                          reverse=(d == 1))
            for qi in range(nq):
                xrb = sre[qi].astype(BF16)
                xib = sim[qi].astype(BF16)
                xr_ref[d, qi] = xrb
                xi_ref[d, qi] = xib
                y_ref[...] += _dot(xrb, cr_ref[d, qi]) - _dot(xib, ci_ref[d, qi])

    blk4 = lambda k: (0, k, 0, 0)
    return pl.pallas_call(
        body, name="ssm_fwd", grid=(SSM_WIDTH // LANES,),
        in_specs=[pl.BlockSpec((SEQ, LANES), lambda k: (0, k)),
                  pl.BlockSpec((2, nq, 1, LANES), blk4), pl.BlockSpec((2, nq, 1, LANES), blk4),
                  pl.BlockSpec((2, nq, LANES, LANES), blk4), pl.BlockSpec((2, nq, LANES, LANES), blk4),
                  pl.BlockSpec((2, nq, LANES, LANES), blk4), pl.BlockSpec((2, nq, LANES, LANES), blk4),
                  pl.BlockSpec((1, LANES), lambda k: (0, k))],
        out_specs=[pl.BlockSpec((SEQ, LANES), lambda k: (0, k)),
                   pl.BlockSpec((2, nq, SEQ, LANES), blk4), pl.BlockSpec((2, nq, SEQ, LANES), blk4)],
        out_shape=[jax.ShapeDtypeStruct((SEQ, SSM_WIDTH), F32),
                   jax.ShapeDtypeStruct((2, N_LANE_BLOCKS, SEQ, LANES), BF16),
                   jax.ShapeDtypeStruct((2, N_LANE_BLOCKS, SEQ, LANES), BF16)],
        scratch_shapes=[pltpu.VMEM((nq, SEQ, LANES), F32), pltpu.VMEM((nq, SEQ, LANES), F32)],
        compiler_params=_cparams(("parallel",)),
    )(u, are, aim, bbr, bbi, cre, cim, dskip)


def _ssm_bwd(dy, u, xr, xi, are, aim, bbr, bbi, cre, cim, dskip):
    nq = SSM_Q

    def body(dy_ref, u_ref, xr_ref, xi_ref, ar_ref, ai_ref, bbr_ref, bbi_ref, cr_ref, ci_ref, d_ref,
             du_ref, dd_ref, dcr_ref, dci_ref, dbr_ref, dbi_ref, dar_ref, dai_ref, sre, sim):
        dyf = dy_ref[...]
        uf = u_ref[...]
        dyb = dyf.astype(BF16)
        ub = uf.astype(BF16)
        dd_ref[...] = jnp.sum(dyf * uf, axis=0, keepdims=True)
        du_ref[...] = d_ref[...] * dyf
        row = lax.broadcasted_iota(jnp.int32, (SEQ, LANES), 0)
        for d in range(2):
            for qi in range(nq):
                sre[qi] = _dot_nt(dyb, cr_ref[d, qi])
                sim[qi] = -_dot_nt(dyb, ci_ref[d, qi])
                dcr_ref[d, qi] = _dot_tn(xr_ref[d, qi], dyb)
                dci_ref[d, qi] = -_dot_tn(xi_ref[d, qi], dyb)
            _scan_inplace(sre, sim, [ar_ref[d, qi] for qi in range(nq)], [-ai_ref[d, qi] for qi in range(nq)],
                          reverse=(d == 0))
            for qi in range(nq):
                gr = sre[qi]
                gi = sim[qi]
                xrf = xr_ref[d, qi].astype(F32)
                xif = xi_ref[d, qi].astype(F32)
                if d == 0:
                    xpr = jnp.where(row == 0, 0.0, pltpu.roll(xrf, 1, axis=0))
                    xpi = jnp.where(row == 0, 0.0, pltpu.roll(xif, 1, axis=0))
                else:
                    xpr = jnp.where(row == SEQ - 1, 0.0, pltpu.roll(xrf, SEQ - 1, axis=0))
                    xpi = jnp.where(row == SEQ - 1, 0.0, pltpu.roll(xif, SEQ - 1, axis=0))
                dar_ref[d, qi] = jnp.sum(gr * xpr + gi * xpi, axis=0, keepdims=True)
                dai_ref[d, qi] = jnp.sum(gi * xpr - gr * xpi, axis=0, keepdims=True)
                grb = gr.astype(BF16)
                gib = gi.astype(BF16)
                du_ref[...] += _dot_nt(grb, bbr_ref[d, qi]) + _dot_nt(gib, bbi_ref[d, qi])
                dbr_ref[d, qi] = _dot_tn(ub, grb)
                dbi_ref[d, qi] = _dot_tn(ub, gib)

    blk4 = lambda k: (0, k, 0, 0)
    col = lambda k: (0, k)
    w_spec = pl.BlockSpec((2, nq, LANES, LANES), blk4)
    a_spec = pl.BlockSpec((2, nq, 1, LANES), blk4)
    x_spec = pl.BlockSpec((2, nq, SEQ, LANES), blk4)
    w_shape = jax.ShapeDtypeStruct((2, N_LANE_BLOCKS, LANES, LANES), F32)
    a_shape = jax.ShapeDtypeStruct((2, N_LANE_BLOCKS, 1, LANES), F32)
    return pl.pallas_call(
        body, name="ssm_bwd", grid=(SSM_WIDTH // LANES,),
        in_specs=[pl.BlockSpec((SEQ, LANES), col), pl.BlockSpec((SEQ, LANES), col), x_spec, x_spec,
                  a_spec, a_spec, w_spec, w_spec, w_spec, w_spec, pl.BlockSpec((1, LANES), col)],
        out_specs=[pl.BlockSpec((SEQ, LANES), col), pl.BlockSpec((1, LANES), col),
                   w_spec, w_spec, w_spec, w_spec, a_spec, a_spec],
        out_shape=[jax.ShapeDtypeStruct((SEQ, SSM_WIDTH), F32), jax.ShapeDtypeStruct((1, SSM_WIDTH), F32),
                   w_shape, w_shape, w_shape, w_shape, a_shape, a_shape],
        scratch_shapes=[pltpu.VMEM((nq, SEQ, LANES), F32), pltpu.VMEM((nq, SEQ, LANES), F32)],
        compiler_params=_cparams(("parallel",)),
    )(dy, u, xr, xi, are, aim, bbr, bbi, cre, cim, dskip)


GELU_C = 0.7978845608028654
GELU_K = 0.044715


def _gelu(y):
    return 0.5 * y * (1.0 + jnp.tanh(GELU_C * (y + GELU_K * y * y * y)))


def _gelu_grad(y):
    t = jnp.tanh(GELU_C * (y + GELU_K * y * y * y))
    return 0.5 * (1.0 + t) + 0.5 * y * (1.0 - t * t) * GELU_C * (1.0 + 3.0 * GELU_K * y * y)


def _mixout_fwd(o, y, glu_w, glu_b, gan, gsn, wout, x1):
    tm = MIX_TM

    def body(o_ref, y_ref, gw_ref, gb_ref, gan_ref, gsn_ref, w_ref, x1_ref, x2_ref, mx_ref):
        yg = _gelu(y_ref[...])
        z = _dot(yg.astype(BF16), gw_ref[...]) + gb_ref[...]
        so = yg * _sigmoid(z)
        na = _rms_fwd(o_ref[...], gan_ref[...])
        ns = _rms_fwd(so, gsn_ref[...])
        mixed = jnp.concatenate([na, ns], axis=-1).astype(BF16)
        mx_ref[...] = mixed
        x2_ref[...] = x1_ref[...] + _dot(mixed, w_ref[...])

    row = lambda i: (i, 0)
    const = lambda i: (0, 0)
    return pl.pallas_call(
        body, name="mixout_fwd", grid=(SEQ // tm,),
        in_specs=[pl.BlockSpec((tm, ATTN_WIDTH), row), pl.BlockSpec((tm, SSM_WIDTH), row),
                  pl.BlockSpec((SSM_WIDTH, SSM_WIDTH), const), pl.BlockSpec((1, SSM_WIDTH), const),
                  pl.BlockSpec((1, ATTN_WIDTH), const), pl.BlockSpec((1, SSM_WIDTH), const),
                  pl.BlockSpec((D_MODEL, D_MODEL), const), pl.BlockSpec((tm, D_MODEL), row)],
        out_specs=[pl.BlockSpec((tm, D_MODEL), row), pl.BlockSpec((tm, D_MODEL), row)],
        out_shape=[jax.ShapeDtypeStruct((SEQ, D_MODEL), F32), jax.ShapeDtypeStruct((SEQ, D_MODEL), BF16)],
        compiler_params=_cparams(("parallel",)),
    )(o, y, glu_w, glu_b, gan, gsn, wout, x1)


def _mixout_bwd(dx2, o, y, glu_w, glu_b, gan, gsn, wout):
    tm = MIX_TM

    def body(dx2_ref, o_ref, y_ref, gw_ref, gb_ref, gan_ref, gsn_ref, w_ref,
             do_ref, dy_ref, dz_ref, yg_ref, dxb_ref, dgan_ref, dgsn_ref, dgb_ref):
        i = pl.program_id(0)
        dxb = dx2_ref[...].astype(BF16)
        dxb_ref[...] = dxb
        dmixed = _dot_nt(dxb, w_ref[...])
        do, dgan = _rms_bwd(dmixed[:, :ATTN_WIDTH], o_ref[...], gan_ref[...])
        do_ref[...] = do
        yv = y_ref[...]
        yg = _gelu(yv)
        ygb = yg.astype(BF16)
        yg_ref[...] = ygb
        sg = _sigmoid(_dot(ygb, gw_ref[...]) + gb_ref[...])
        dso, dgsn = _rms_bwd(dmixed[:, ATTN_WIDTH:], yg * sg, gsn_ref[...])
        dz = dso * yg * sg * (1.0 - sg)
        dzb = dz.astype(BF16)
        dz_ref[...] = dzb
        dyg = dso * sg + _dot_nt(dzb, gw_ref[...])
        dy_ref[...] = dyg * _gelu_grad(yv)
        dgb = jnp.sum(dz, axis=0, keepdims=True)

        @pl.when(i == 0)
        def _():
            dgan_ref[...] = dgan
            dgsn_ref[...] = dgsn
            dgb_ref[...] = dgb

        @pl.when(i != 0)
        def _():
            dgan_ref[...] += dgan
            dgsn_ref[...] += dgsn
            dgb_ref[...] += dgb

    row = lambda i: (i, 0)
    const = lambda i: (0, 0)
    return pl.pallas_call(
        body, name="mixout_bwd", grid=(SEQ // tm,),
        in_specs=[pl.BlockSpec((tm, D_MODEL), row), pl.BlockSpec((tm, ATTN_WIDTH), row),
                  pl.BlockSpec((tm, SSM_WIDTH), row),
                  pl.BlockSpec((SSM_WIDTH, SSM_WIDTH), const), pl.BlockSpec((1, SSM_WIDTH), const),
                  pl.BlockSpec((1, ATTN_WIDTH), const), pl.BlockSpec((1, SSM_WIDTH), const),
                  pl.BlockSpec((D_MODEL, D_MODEL), const)],
        out_specs=[pl.BlockSpec((tm, ATTN_WIDTH), row), pl.BlockSpec((tm, SSM_WIDTH), row),
                   pl.BlockSpec((tm, SSM_WIDTH), row), pl.BlockSpec((tm, SSM_WIDTH), row),
                   pl.BlockSpec((tm, D_MODEL), row),
                   pl.BlockSpec((1, ATTN_WIDTH), const), pl.BlockSpec((1, SSM_WIDTH), const),
                   pl.BlockSpec((1, SSM_WIDTH), const)],
        out_shape=[jax.ShapeDtypeStruct((SEQ, ATTN_WIDTH), F32), jax.ShapeDtypeStruct((SEQ, SSM_WIDTH), F32),
                   jax.ShapeDtypeStruct((SEQ, SSM_WIDTH), BF16), jax.ShapeDtypeStruct((SEQ, SSM_WIDTH), BF16),
                   jax.ShapeDtypeStruct((SEQ, D_MODEL), BF16),
                   jax.ShapeDtypeStruct((1, ATTN_WIDTH), F32), jax.ShapeDtypeStruct((1, SSM_WIDTH), F32),
                   jax.ShapeDtypeStruct((1, SSM_WIDTH), F32)],
        compiler_params=_cparams(("arbitrary",)),
    )(dx2, o, y, glu_w, glu_b, gan, gsn, wout)


def _loss_head(x, g, target):
    tm = MIX_TM

    def body(x_ref, g_ref, t_ref, loss_ref, dx_ref, dg_ref):
        i = pl.program_id(0)
        xv = x_ref[...]
        gv = g_ref[...]
        err = _rms_fwd(xv, gv) - t_ref[...]
        part = jnp.broadcast_to(0.5 * jnp.sum(err * err) / D_MODEL, (1, LANES))
        dx, dg = _rms_bwd(err * (1.0 / D_MODEL), xv, gv)
        dx_ref[...] = dx

        @pl.when(i == 0)
        def _():
            loss_ref[...] = part
            dg_ref[...] = dg

        @pl.when(i != 0)
        def _():
            loss_ref[...] += part
            dg_ref[...] += dg

    row = lambda i: (i, 0)
    const = lambda i: (0, 0)
    return pl.pallas_call(
        body, name="loss_head", grid=(SEQ // tm,),
        in_specs=[pl.BlockSpec((tm, D_MODEL), row), pl.BlockSpec((1, D_MODEL), const),
                  pl.BlockSpec((tm, D_MODEL), row)],
        out_specs=[pl.BlockSpec((1, LANES), const), pl.BlockSpec((tm, D_MODEL), row),
                   pl.BlockSpec((1, D_MODEL), const)],
        out_shape=[jax.ShapeDtypeStruct((1, LANES), F32), jax.ShapeDtypeStruct((SEQ, D_MODEL), F32),
                   jax.ShapeDtypeStruct((1, D_MODEL), F32)],
        compiler_params=_cparams(("arbitrary",)),
    )(x, g, target)


def _embed_blocks(b):
    b5 = b.reshape(2, 16, 2, 64, 16).transpose(0, 1, 2, 4, 3)
    eye2 = jnp.eye(2, dtype=b.dtype)
    m = (b5[:, :, :, :, None, :] * eye2[None, None, :, None, :, None]).reshape(2, 16, 32, LANES)
    sel = (jnp.arange(16)[:, None] % 4 == jnp.arange(4)[None, :]).astype(b.dtype)
    return (m[:, :, None, :, :] * sel[None, :, :, None, None]).reshape(2, 16, LANES, LANES)


def _extract_blocks(m):
    sel = (jnp.arange(16)[:, None] % 4 == jnp.arange(4)[None, :]).astype(m.dtype)
    m = jnp.sum(m.reshape(2, 16, 4, 32, LANES) * sel[None, :, :, None, None], axis=2)
    eye2 = jnp.eye(2, dtype=m.dtype)
    m = jnp.sum(m.reshape(2, 16, 2, 16, 2, 64) * eye2[None, None, :, None, :, None], axis=4)
    return m.transpose(0, 1, 2, 4, 3).reshape(2, 32, 64, 16)


def _local_step(x, target, w, p):
    x1, h1, a1, b1 = _ffn_fwd(x, p["norm_ffn1"], w["wgt1"], w["wut1"], w["wd1"], "ffn1_fwd")
    h2, q, k, v, u = _mixin_fwd(x1, p["norm_mix"], w["wint"])
    kp = jnp.pad(k, ((WINDOW, WINDOW), (0, 0)))
    vp = jnp.pad(v, ((WINDOW, WINDOW), (0, 0)))
    o = _attn_fwd(q, kp, vp, p["attn_sinks"])

    lam_re = p["ssm_lambda_re"].reshape(2 * N_LANE_BLOCKS, LANES)
    lam_im = p["ssm_lambda_im"].reshape(2 * N_LANE_BLOCKS, LANES)
    log_dt = jnp.repeat(p["ssm_log_dt"].reshape(2, 32), 64, axis=-1).reshape(2 * N_LANE_BLOCKS, LANES)
    bpad_re = _embed_blocks(p["ssm_b_re"].reshape(2, 32, 64, 16)).reshape(2 * N_LANE_BLOCKS, LANES, LANES)
    bpad_im = _embed_blocks(p["ssm_b_im"].reshape(2, 32, 64, 16)).reshape(2 * N_LANE_BLOCKS, LANES, LANES)
    c_t = lambda c: _embed_blocks(c.reshape(2, 32, 16, 64).transpose(0, 1, 3, 2)).transpose(0, 1, 3, 2)
    cre = c_t(p["ssm_c_re"]).astype(BF16)
    cim = c_t(p["ssm_c_im"]).astype(BF16)
    a_re, a_im, bbr, bbi = _ssm_prep(lam_re, lam_im, log_dt, bpad_re, bpad_im)
    shape_a = (2, N_LANE_BLOCKS, 1, LANES)
    shape_w = (2, N_LANE_BLOCKS, LANES, LANES)
    a_re4, a_im4 = a_re.reshape(shape_a), a_im.reshape(shape_a)
    bbr4, bbi4 = bbr.reshape(shape_w), bbi.reshape(shape_w)
    dskip = p["ssm_d"].reshape(1, SSM_WIDTH)
    y, xr, xi = _ssm_fwd(u, a_re4, a_im4, bbr4, bbi4, cre, cim, dskip)

    x2, mixed = _mixout_fwd(o, y, w["glu"], p["ssm_glu_b"], p["attn_out_norm"], p["ssm_out_norm"], w["wout"], x1)
    x3, h3, a3, b3 = _ffn_fwd(x2, p["norm_ffn2"], w["wgt2"], w["wut2"], w["wd2"], "ffn2_fwd")

    loss, dx3, d_final = _loss_head(x3, p["final_norm"], target)
    dx2, da3, db3, s3, df3, d_n2 = _ffn_bwd_act(dx3, x2, p["norm_ffn2"], a3, b3, w["wgt2"], w["wut2"], w["wd2"],
                                                "ffn2_bwd_act")
    g_wgt2, g_wut2, g_wd2 = _mm_tn([(da3, h3), (db3, h3), (s3, df3)], "ffn2_bwd_w")

    do, dy, dz, ygb, dx2b, d_gan, d_gsn, d_glub = _mixout_bwd(
        dx2, o, y, w["glu"], p["ssm_glu_b"], p["attn_out_norm"], p["ssm_out_norm"], w["wout"])
    (g_wout,) = _mm_tn([(mixed, dx2b)], "wout_bwd_w")
    (g_glu,) = _mm_tn([(ygb, dz)], "glu_bwd_w")

    du, d_dskip, dcre, dcim, dbbr, dbbi, dar, dai = _ssm_bwd(dy, u, xr, xi, a_re4, a_im4, bbr4, bbi4, cre, cim, dskip)
    nb = 2 * N_LANE_BLOCKS
    g_lre, g_lim, g_ldt, g_bpr, g_bpi = _ssm_prep_bwd(
        lam_re, lam_im, log_dt, bpad_re, bpad_im, dar.reshape(nb, LANES), dai.reshape(nb, LANES),
        dbbr.reshape(nb, LANES, LANES), dbbi.reshape(nb, LANES, LANES))

    dq, dkp, dvp, d_sinks = _attn_bwd(q, kp, vp, p["attn_sinks"], do)
    dk = dkp[WINDOW:WINDOW + SEQ]
    dv = dvp[WINDOW:WINDOW + SEQ]
    dx1, dproj, d_nmix = _mixin_bwd(dq, dk, dv, du, w["wint"], x1, p["norm_mix"], dx2)
    (g_wint,) = _mm_tn([(dproj, h2)], "win_bwd_w")

    dx0, da1, db1, s1, df1, d_n1 = _ffn_bwd_act(dx1, x, p["norm_ffn1"], a1, b1, w["wgt1"], w["wut1"], w["wd1"],
                                                "ffn1_bwd_act")
    g_wgt1, g_wut1, g_wd1 = _mm_tn([(da1, h1), (db1, h1), (s1, df1)], "ffn1_bwd_w")

    c_back = lambda g: _extract_blocks(g.transpose(0, 1, 3, 2)).transpose(0, 1, 3, 2)
    big = dict(wgt1=g_wgt1, wut1=g_wut1, wd1=g_wd1, wint=g_wint, glu=g_glu, wout=g_wout,
               wgt2=g_wgt2, wut2=g_wut2, wd2=g_wd2)
    small = dict(
        norm_ffn1=d_n1, norm_mix=d_nmix, attn_sinks=d_sinks,
        ssm_lambda_re=g_lre.reshape(64, 64), ssm_lambda_im=g_lim.reshape(64, 64),
        ssm_log_dt=g_ldt.reshape(2, 32),
        ssm_b_re=_extract_blocks(g_bpr.reshape(shape_w)).reshape(4096, 16),
        ssm_b_im=_extract_blocks(g_bpi.reshape(shape_w)).reshape(4096, 16),
        ssm_c_re=c_back(dcre).reshape(1024, 64), ssm_c_im=c_back(dcim).reshape(1024, 64),
        ssm_d=d_dskip.reshape(32, 16), ssm_glu_b=d_glub, attn_out_norm=d_gan, ssm_out_norm=d_gsn,
        norm_ffn2=d_n2, final_norm=d_final)
    return loss, dx0, big, small


BIG = dict(
    wgt1=("ffn1_w_gate", 352, 1024, True), wut1=("ffn1_w_up", 352, 1024, True), wd1=("ffn1_w_down", 352, 1024, False),
    wint=("w_in", 160, 1024, True), glu=("ssm_glu_w", 64, 512, False), wout=("w_out", 128, 1024, False),
    wgt2=("ffn2_w_gate", 352, 1024, True), wut2=("ffn2_w_up", 352, 1024, True), wd2=("ffn2_w_down", 352, 1024, False))

SMALL = dict(
    norm_ffn1=(1, 1024), norm_mix=(1, 1024), attn_sinks=(1, 8), ssm_lambda_re=(64, 64), ssm_lambda_im=(64, 64),
    ssm_log_dt=(2, 32), ssm_b_re=(4096, 16), ssm_b_im=(4096, 16), ssm_c_re=(1024, 64), ssm_c_im=(1024, 64),
    ssm_d=(32, 16), ssm_glu_b=(1, 512), attn_out_norm=(1, 512), ssm_out_norm=(1, 512), norm_ffn2=(1, 1024),
    final_norm=(1, 1024))
SMALL_TOTAL = sum(r * c for r, c in SMALL.values())
SMALL_ROWS = -(-SMALL_TOTAL // (8 * LANES)) * 8


def _pad_to(n, mult):
    return -(-n // mult) * mult


def _transpose_2d(x, rows_out, cols_out):
    r_in, c_in = x.shape
    rp, cp = _pad_to(r_in, LANES), _pad_to(c_in, LANES)
    if cp != c_in:
        x = jnp.concatenate([x, jnp.zeros((r_in, cp - c_in), x.dtype)], axis=1)
    if rp != r_in:
        x = jnp.concatenate([x, jnp.zeros((rp - r_in, cp), x.dtype)], axis=0)
    return x.T[:rows_out, :cols_out]


def _cast_shards(shards):
    names = list(BIG)

    def body(*refs):
        ins, outs = refs[:len(names)], refs[len(names):]
        for idx, n in enumerate(names):
            _, rows, cols, transposed = BIG[n]
            v = ins[idx][...]
            if transposed:
                v = _transpose_2d(v, rows, cols)
            outs[idx][...] = v.astype(BF16)

    return pl.pallas_call(
        body, name="cast_shards",
        out_shape=[jax.ShapeDtypeStruct((BIG[n][1], BIG[n][2]), BF16) for n in names],
        compiler_params=_cparams(),
    )(*[shards[n] for n in names])


def _peer(x, y, c, r):
    px = 1 - x if r & 4 else x
    py = 1 - y if r & 2 else y
    pc = 1 - c if r & 1 else c
    return px, py, pc


def _all_gather(shards):
    names = list(BIG)
    nk = len(names)

    def body(*refs):
        ins, outs = refs[:nk], refs[nk:2 * nk]
        send_sems, recv_sems, local_sems = refs[2 * nk:]
        x, y, c = lax.axis_index("x"), lax.axis_index("y"), lax.axis_index("c")
        me = 4 * x + 2 * y + c
        copies = []
        for k in range(nk):
            mine = pltpu.make_async_copy(ins[k], outs[k].at[me], local_sems.at[k])
            mine.start()
            copies.append(mine)
            for r in range(1, N_DEV):
                cp = pltpu.make_async_remote_copy(
                    src_ref=ins[k], dst_ref=outs[k].at[me], send_sem=send_sems.at[k, r - 1],
                    recv_sem=recv_sems.at[k, r - 1], device_id=_peer(x, y, c, r), device_id_type=MESH_ID)
                cp.start()
                copies.append(cp)
        for cp in copies:
            cp.wait()

    hbm = pl.BlockSpec(memory_space=pl.ANY)
    return pl.pallas_call(
        body, name="all_gather",
        in_specs=[hbm] * nk, out_specs=[hbm] * nk,
        out_shape=[jax.ShapeDtypeStruct((N_DEV, BIG[n][1], BIG[n][2]), BF16) for n in names],
        scratch_shapes=[pltpu.SemaphoreType.DMA((nk, N_DEV - 1)), pltpu.SemaphoreType.DMA((nk, N_DEV - 1)),
                        pltpu.SemaphoreType.DMA((nk,))],
        compiler_params=pltpu.CompilerParams(has_side_effects=True),
    )(*[shards[n] for n in names])


def _exchange_grads(big, small_packed):
    names = list(BIG)
    nk = len(names) + 1

    def body(*refs):
        ins, outs = refs[:nk], refs[nk:2 * nk]
        send_sems, recv_sems, local_sems = refs[2 * nk:]
        x, y, c = lax.axis_index("x"), lax.axis_index("y"), lax.axis_index("c")
        me = 4 * x + 2 * y + c
        copies = []
        for k in range(nk):
            is_small = k == nk - 1
            own = ins[k] if is_small else ins[k].at[me]
            mine = pltpu.make_async_copy(own, outs[k].at[me], local_sems.at[k])
            mine.start()
            copies.append(mine)
            for r in range(1, N_DEV):
                px, py, pc = _peer(x, y, c, r)
                src = ins[k] if is_small else ins[k].at[4 * px + 2 * py + pc]
                cp = pltpu.make_async_remote_copy(
                    src_ref=src, dst_ref=outs[k].at[me], send_sem=send_sems.at[k, r - 1],
                    recv_sem=recv_sems.at[k, r - 1], device_id=(px, py, pc), device_id_type=MESH_ID)
                cp.start()
                copies.append(cp)
        for cp in copies:
            cp.wait()

    hbm = pl.BlockSpec(memory_space=pl.ANY)
    out_shape = [jax.ShapeDtypeStruct((N_DEV, BIG[n][1], BIG[n][2]), BF16) for n in names]
    out_shape.append(jax.ShapeDtypeStruct((N_DEV, SMALL_ROWS, LANES), F32))
    return pl.pallas_call(
        body, name="exchange_grads",
        in_specs=[hbm] * nk, out_specs=[hbm] * nk, out_shape=out_shape,
        scratch_shapes=[pltpu.SemaphoreType.DMA((nk, N_DEV - 1)), pltpu.SemaphoreType.DMA((nk, N_DEV - 1)),
                        pltpu.SemaphoreType.DMA((nk,))],
        compiler_params=pltpu.CompilerParams(has_side_effects=True),
    )(*[big[n] for n in names], small_packed)


def _adamw_math(w, g, m, v):
    m2 = ADAM_B1 * m + (1.0 - ADAM_B1) * g
    v2 = ADAM_B2 * v + (1.0 - ADAM_B2) * (g * g)
    m_hat = m2 / (1.0 - ADAM_B1 ** ADAM_STEP)
    v_hat = v2 / (1.0 - ADAM_B2 ** ADAM_STEP)
    delta = -ADAM_LR * (m_hat / (jnp.sqrt(v_hat) + ADAM_EPS) + ADAM_WD * w)
    return delta, m2, v2


def _adamw_big(parts, w, m, v, name, transposed):
    shape = w.shape

    def body(p_ref, w_ref, m_ref, v_ref, g_ref, d_ref, m2_ref, v2_ref):
        g = p_ref[0].astype(F32)
        for i in range(1, N_DEV):
            g = g + p_ref[i].astype(F32)
        if transposed:
            g = _transpose_2d(g, shape[0], shape[1])
        delta, m2, v2 = _adamw_math(w_ref[...], g, m_ref[...], v_ref[...])
        g_ref[...] = g
        d_ref[...] = delta
        m2_ref[...] = m2
        v2_ref[...] = v2

    return pl.pallas_call(
        body, name=name, out_shape=[jax.ShapeDtypeStruct(shape, F32)] * 4, compiler_params=_cparams(),
    )(parts, w, m, v)


def _adamw_small(parts, w, m, v):
    def body(p_ref, w_ref, m_ref, v_ref, g_ref, d_ref, m2_ref, v2_ref):
        g = p_ref[0]
        for i in range(1, N_DEV):
            g = g + p_ref[i]
        delta, m2, v2 = _adamw_math(w_ref[...], g, m_ref[...], v_ref[...])
        g_ref[...] = g
        d_ref[...] = delta
        m2_ref[...] = m2
        v2_ref[...] = v2

    return pl.pallas_call(
        body, name="adamw_small", out_shape=[jax.ShapeDtypeStruct((SMALL_ROWS, LANES), F32)] * 4,
        compiler_params=_cparams(),
    )(parts, w, m, v)


def _pack_small(d):
    flat = jnp.concatenate([d[n].reshape(-1) for n in SMALL])
    flat = jnp.pad(flat, (0, SMALL_ROWS * LANES - SMALL_TOTAL))
    return flat.reshape(SMALL_ROWS, LANES)


def _unpack_small(packed, shapes):
    flat = packed.reshape(-1)
    out, off = {}, 0
    for n, (r, c) in SMALL.items():
        out[n] = flat[off:off + r * c].reshape(shapes[n])
        off += r * c
    return out


WEIGHT_NAMES = ['norm_ffn1', 'ffn1_w_gate', 'ffn1_w_up', 'ffn1_w_down', 'norm_mix', 'w_in', 'attn_sinks',
                'ssm_lambda_re', 'ssm_lambda_im', 'ssm_log_dt', 'ssm_b_re', 'ssm_b_im', 'ssm_c_re', 'ssm_c_im',
                'ssm_d', 'ssm_glu_w', 'ssm_glu_b', 'attn_out_norm', 'ssm_out_norm', 'w_out', 'norm_ffn2',
                'ffn2_w_gate', 'ffn2_w_up', 'ffn2_w_down', 'final_norm']


def kernel(x, norm_ffn1, ffn1_w_gate, ffn1_w_up, ffn1_w_down, norm_mix, w_in, attn_sinks, ssm_lambda_re, ssm_lambda_im, ssm_log_dt, ssm_b_re, ssm_b_im, ssm_c_re, ssm_c_im, ssm_d, ssm_glu_w, ssm_glu_b, attn_out_norm, ssm_out_norm, w_out, norm_ffn2, ffn2_w_gate, ffn2_w_up, ffn2_w_down, final_norm, loss_target, m_norm_ffn1, m_ffn1_w_gate, m_ffn1_w_up, m_ffn1_w_down, m_norm_mix, m_w_in, m_attn_sinks, m_ssm_lambda_re, m_ssm_lambda_im, m_ssm_log_dt, m_ssm_b_re, m_ssm_b_im, m_ssm_c_re, m_ssm_c_im, m_ssm_d, m_ssm_glu_w, m_ssm_glu_b, m_attn_out_norm, m_ssm_out_norm, m_w_out, m_norm_ffn2, m_ffn2_w_gate, m_ffn2_w_up, m_ffn2_w_down, m_final_norm, v_norm_ffn1, v_ffn1_w_gate, v_ffn1_w_up, v_ffn1_w_down, v_norm_mix, v_w_in, v_attn_sinks, v_ssm_lambda_re, v_ssm_lambda_im, v_ssm_log_dt, v_ssm_b_re, v_ssm_b_im, v_ssm_c_re, v_ssm_c_im, v_ssm_d, v_ssm_glu_w, v_ssm_glu_b, v_attn_out_norm, v_ssm_out_norm, v_w_out, v_norm_ffn2, v_ffn2_w_gate, v_ffn2_w_up, v_ffn2_w_down, v_final_norm):
    args = dict(locals())
    weights = {n: args[n] for n in WEIGHT_NAMES}
    moms = {n: args["m_" + n] for n in WEIGHT_NAMES}
    vars_ = {n: args["v_" + n] for n in WEIGHT_NAMES}
    big_of = {BIG[k][0]: k for k in BIG}

    def shard2d(a):
        return a.reshape(a.shape[-2], a.shape[-1])

    shards = _cast_shards({k: shard2d(weights[BIG[k][0]]) for k in BIG})
    gathered = _all_gather(dict(zip(BIG, shards)))
    w = {k: g.reshape(N_DEV * BIG[k][1], BIG[k][2]) for k, g in zip(BIG, gathered)}

    small_p = {n: weights[n].reshape(SMALL[n]) for n in SMALL}
    loss, grad_x, g_big, g_small = _local_step(x.reshape(SEQ, D_MODEL), loss_target.reshape(SEQ, D_MODEL), w, small_p)

    exch = _exchange_grads({k: g_big[k].reshape(N_DEV, BIG[k][1], BIG[k][2]) for k in BIG}, _pack_small(g_small))
    outs = {}
    for k, parts in zip(BIG, exch[:-1]):
        n = BIG[k][0]
        outs[n] = [o.reshape(weights[n].shape) for o in
                   _adamw_big(parts, shard2d(weights[n]), shard2d(moms[n]), shard2d(vars_[n]),
                              "adamw_" + n, BIG[k][3])]
    small_shapes = {n: weights[n].shape for n in SMALL}
    packed = _adamw_small(exch[-1], _pack_small({n: weights[n] for n in SMALL}),
                          _pack_small({n: moms[n] for n in SMALL}), _pack_small({n: vars_[n] for n in SMALL}))
    unpacked = [_unpack_small(pk, small_shapes) for pk in packed]
    for n in SMALL:
        outs[n] = [u[n] for u in unpacked]

    total_loss = lax.psum(loss[0, 0], ("x", "y", "c"))
    result = [total_loss, grad_x.reshape(x.shape)]
    for i in range(4):
        result += [outs[n][i] for n in WEIGHT_NAMES]
    return tuple(result)
```

```python
import functools

import jax
import jax.numpy as jnp
from jax import lax
from jax.experimental import pallas as pl
from jax.experimental.pallas import tpu as pltpu

F32 = jnp.float32
BF16 = jnp.bfloat16

N_DEV = 8
SEQ = 2048
D_MODEL = 1024
D_FF = 2816
ATTN_HEADS = 8
KV_HEADS = 2
HEAD_DIM = 64
ATTN_WIDTH = 512
KV_WIDTH = 128
WINDOW = 128
SSM_WIDTH = 512
IN_WIDTH = 1280
EPS = 1e-6
NEG_INF = -1e30
LAMBDA_RE_MAX = -1e-4
LANES = 128
N_LANE_BLOCKS = 16
SCAN_CHUNK = SEQ // 8

ADAM_LR = 0.001
ADAM_B1 = 0.9
ADAM_B2 = 0.999
ADAM_EPS = 1e-08
ADAM_WD = 0.01
ADAM_STEP = 10

VMEM_LIMIT = 56 * 1024 * 1024
MESH_ID = pl.DeviceIdType.MESH


def _cparams(sem=None):
    return pltpu.CompilerParams(dimension_semantics=sem, vmem_limit_bytes=VMEM_LIMIT)


def _dot(a, b):
    return jnp.dot(a, b, preferred_element_type=F32)


def _dot_nt(a, b):
    return lax.dot_general(a, b, (((1,), (1,)), ((), ())), preferred_element_type=F32)


def _dot_tn(a, b):
    return lax.dot_general(a, b, (((0,), (0,)), ((), ())), preferred_element_type=F32)


def _rms_fwd(x, g):
    r = lax.rsqrt(jnp.mean(x * x, axis=-1, keepdims=True) + EPS)
    return x * r * g


def _rms_bwd(dh, x, g):
    r = lax.rsqrt(jnp.mean(x * x, axis=-1, keepdims=True) + EPS)
    xh = x * r
    dg = jnp.sum(dh * xh, axis=0, keepdims=True)
    dxh = dh * g
    dx = r * (dxh - xh * jnp.mean(dxh * xh, axis=-1, keepdims=True))
    return dx, dg


def _sigmoid(x):
    return 1.0 / (1.0 + jnp.exp(-x))


FFN_TM = 512
FFN_TF = 256


def _ffn_fwd(x, g, wgt, wut, wd, name):
    tm, tf = FFN_TM, FFN_TF
    nj = D_FF // tf

    def body(x_ref, g_ref, wg_ref, wu_ref, wd_ref, xo_ref, h_ref, a_ref, b_ref, h_s, acc):
        j = pl.program_id(1)

        @pl.when(j == 0)
        def _():
            h = _rms_fwd(x_ref[...], g_ref[...]).astype(BF16)
            h_s[...] = h
            h_ref[...] = h
            acc[...] = jnp.zeros_like(acc)

        h = h_s[...]
        a = _dot_nt(h, wg_ref[...])
        b = _dot_nt(h, wu_ref[...])
        a_ref[...] = a.astype(BF16)
        b_ref[...] = b.astype(BF16)
        s = (a * _sigmoid(a) * b).astype(BF16)
        acc[...] += _dot(s, wd_ref[...])

        @pl.when(j == nj - 1)
        def _():
            xo_ref[...] = x_ref[...] + 0.5 * acc[...]

    return pl.pallas_call(
        body, name=name, grid=(SEQ // tm, nj),
        in_specs=[pl.BlockSpec((tm, D_MODEL), lambda i, j: (i, 0)),
                  pl.BlockSpec((1, D_MODEL), lambda i, j: (0, 0)),
                  pl.BlockSpec((tf, D_MODEL), lambda i, j: (j, 0)),
                  pl.BlockSpec((tf, D_MODEL), lambda i, j: (j, 0)),
                  pl.BlockSpec((tf, D_MODEL), lambda i, j: (j, 0))],
        out_specs=[pl.BlockSpec((tm, D_MODEL), lambda i, j: (i, 0)),
                   pl.BlockSpec((tm, D_MODEL), lambda i, j: (i, 0)),
                   pl.BlockSpec((tm, tf), lambda i, j: (i, j)),
                   pl.BlockSpec((tm, tf), lambda i, j: (i, j))],
        out_shape=[jax.ShapeDtypeStruct((SEQ, D_MODEL), F32), jax.ShapeDtypeStruct((SEQ, D_MODEL), BF16),
                   jax.ShapeDtypeStruct((SEQ, D_FF), BF16), jax.ShapeDtypeStruct((SEQ, D_FF), BF16)],
        scratch_shapes=[pltpu.VMEM((tm, D_MODEL), BF16), pltpu.VMEM((tm, D_MODEL), F32)],
        compiler_params=_cparams(("parallel", "arbitrary")),
    )(x, g, wgt, wut, wd)


def _ffn_bwd_act(dxo, x, g, a, b, wgt, wut, wd, name):
    tm, tf = FFN_TM, FFN_TF
    nj = D_FF // tf

    def body(dxo_ref, x_ref, g_ref, a_ref, b_ref, wg_ref, wu_ref, wd_ref,
             dx_ref, da_ref, db_ref, s_ref, df_ref, dg_ref, df_s, acc):
        i = pl.program_id(0)
        j = pl.program_id(1)

        @pl.when(j == 0)
        def _():
            df = (0.5 * dxo_ref[...]).astype(BF16)
            df_s[...] = df
            df_ref[...] = df
            acc[...] = jnp.zeros_like(acc)

        ds = _dot_nt(df_s[...], wd_ref[...])
        av = a_ref[...].astype(F32)
        bv = b_ref[...].astype(F32)
        sig = _sigmoid(av)
        sl = av * sig
        s_ref[...] = (sl * bv).astype(BF16)
        db = (ds * sl).astype(BF16)
        da = (ds * bv * (sig * (1.0 + av * (1.0 - sig)))).astype(BF16)
        da_ref[...] = da
        db_ref[...] = db
        acc[...] += _dot(da, wg_ref[...]) + _dot(db, wu_ref[...])

        @pl.when(j == nj - 1)
        def _():
            dx, dg = _rms_bwd(acc[...], x_ref[...], g_ref[...])
            dx_ref[...] = dxo_ref[...] + dx

            @pl.when(i == 0)
            def _():
                dg_ref[...] = dg

            @pl.when(i != 0)
            def _():
                dg_ref[...] += dg

    row = lambda i, j: (i, 0)
    col = lambda i, j: (j, 0)
    tile = lambda i, j: (i, j)
    return pl.pallas_call(
        body, name=name, grid=(SEQ // tm, nj),
        in_specs=[pl.BlockSpec((tm, D_MODEL), row), pl.BlockSpec((tm, D_MODEL), row),
                  pl.BlockSpec((1, D_MODEL), lambda i, j: (0, 0)),
                  pl.BlockSpec((tm, tf), tile), pl.BlockSpec((tm, tf), tile),
                  pl.BlockSpec((tf, D_MODEL), col), pl.BlockSpec((tf, D_MODEL), col), pl.BlockSpec((tf, D_MODEL), col)],
        out_specs=[pl.BlockSpec((tm, D_MODEL), row),
                   pl.BlockSpec((tm, tf), tile), pl.BlockSpec((tm, tf), tile), pl.BlockSpec((tm, tf), tile),
                   pl.BlockSpec((tm, D_MODEL), row),
                   pl.BlockSpec((1, D_MODEL), lambda i, j: (0, 0))],
        out_shape=[jax.ShapeDtypeStruct((SEQ, D_MODEL), F32),
                   jax.ShapeDtypeStruct((SEQ, D_FF), BF16), jax.ShapeDtypeStruct((SEQ, D_FF), BF16),
                   jax.ShapeDtypeStruct((SEQ, D_FF), BF16),
                   jax.ShapeDtypeStruct((SEQ, D_MODEL), BF16),
                   jax.ShapeDtypeStruct((1, D_MODEL), F32)],
        scratch_shapes=[pltpu.VMEM((tm, D_MODEL), BF16), pltpu.VMEM((tm, D_MODEL), F32)],
        compiler_params=_cparams(("arbitrary", "arbitrary")),
    )(dxo, x, g, a, b, wgt, wut, wd)


def _mm_tn(pairs, name, tmm=256):
    m = pairs[0][0].shape[1]
    n_pairs = len(pairs)

    def body(*refs):
        ins, outs = refs[:2 * n_pairs], refs[2 * n_pairs:]
        for p in range(n_pairs):
            outs[p][...] = _dot_tn(ins[2 * p][...], ins[2 * p + 1][...]).astype(BF16)

    in_specs, out_specs, out_shape, args = [], [], [], []
    for a, b in pairs:
        n = b.shape[1]
        in_specs += [pl.BlockSpec((SEQ, tmm), lambda i: (0, i)), pl.BlockSpec((SEQ, n), lambda i: (0, 0))]
        out_specs.append(pl.BlockSpec((tmm, n), lambda i: (i, 0)))
        out_shape.append(jax.ShapeDtypeStruct((m, n), BF16))
        args += [a, b]
    return pl.pallas_call(body, name=name, grid=(m // tmm,), in_specs=in_specs, out_specs=out_specs,
                          out_shape=out_shape, compiler_params=_cparams(("parallel",)))(*args)


MIX_TM = 256


def _mixin_fwd(x, g, wint):
    tm = MIX_TM

    def body(x_ref, g_ref, w_ref, h_ref, q_ref, k_ref, v_ref, u_ref):
        h = _rms_fwd(x_ref[...], g_ref[...]).astype(BF16)
        h_ref[...] = h
        proj = _dot_nt(h, w_ref[...])
        q_ref[...] = proj[:, :ATTN_WIDTH]
        k_ref[...] = proj[:, ATTN_WIDTH:ATTN_WIDTH + KV_WIDTH]
        v_ref[...] = proj[:, ATTN_WIDTH + KV_WIDTH:ATTN_WIDTH + 2 * KV_WIDTH]
        u_ref[...] = proj[:, ATTN_WIDTH + 2 * KV_WIDTH:]

    row = lambda i: (i, 0)
    return pl.pallas_call(
        body, name="mixin_fwd", grid=(SEQ // tm,),
        in_specs=[pl.BlockSpec((tm, D_MODEL), row), pl.BlockSpec((1, D_MODEL), lambda i: (0, 0)),
                  pl.BlockSpec((IN_WIDTH, D_MODEL), lambda i: (0, 0))],
        out_specs=[pl.BlockSpec((tm, D_MODEL), row), pl.BlockSpec((tm, ATTN_WIDTH), row),
                   pl.BlockSpec((tm, KV_WIDTH), row), pl.BlockSpec((tm, KV_WIDTH), row),
                   pl.BlockSpec((tm, SSM_WIDTH), row)],
        out_shape=[jax.ShapeDtypeStruct((SEQ, D_MODEL), BF16), jax.ShapeDtypeStruct((SEQ, ATTN_WIDTH), F32),
                   jax.ShapeDtypeStruct((SEQ, KV_WIDTH), F32), jax.ShapeDtypeStruct((SEQ, KV_WIDTH), F32),
                   jax.ShapeDtypeStruct((SEQ, SSM_WIDTH), F32)],
        compiler_params=_cparams(("parallel",)),
    )(x, g, wint)


def _mixin_bwd(dq, dk, dv, du, wint, x, g, dres):
    tm = MIX_TM

    def body(dq_ref, dk_ref, dv_ref, du_ref, w_ref, x_ref, g_ref, dres_ref, dx_ref, dp_ref, dg_ref):
        i = pl.program_id(0)
        dp = jnp.concatenate([dq_ref[...], dk_ref[...], dv_ref[...], du_ref[...]], axis=-1).astype(BF16)
        dp_ref[...] = dp
        dh = _dot(dp, w_ref[...])
        dx, dg = _rms_bwd(dh, x_ref[...], g_ref[...])
        dx_ref[...] = dres_ref[...] + dx

        @pl.when(i == 0)
        def _():
            dg_ref[...] = dg

        @pl.when(i != 0)
        def _():
            dg_ref[...] += dg

    row = lambda i: (i, 0)
    const = lambda i: (0, 0)
    return pl.pallas_call(
        body, name="mixin_bwd", grid=(SEQ // tm,),
        in_specs=[pl.BlockSpec((tm, ATTN_WIDTH), row), pl.BlockSpec((tm, KV_WIDTH), row),
                  pl.BlockSpec((tm, KV_WIDTH), row), pl.BlockSpec((tm, SSM_WIDTH), row),
                  pl.BlockSpec((IN_WIDTH, D_MODEL), const), pl.BlockSpec((tm, D_MODEL), row),
                  pl.BlockSpec((1, D_MODEL), const), pl.BlockSpec((tm, D_MODEL), row)],
        out_specs=[pl.BlockSpec((tm, D_MODEL), row), pl.BlockSpec((tm, IN_WIDTH), row),
                   pl.BlockSpec((1, D_MODEL), const)],
        out_shape=[jax.ShapeDtypeStruct((SEQ, D_MODEL), F32), jax.ShapeDtypeStruct((SEQ, IN_WIDTH), BF16),
                   jax.ShapeDtypeStruct((1, D_MODEL), F32)],
        compiler_params=_cparams(("arbitrary",)),
    )(dq, dk, dv, du, wint, x, g, dres)


N_QBLOCKS = SEQ // WINDOW
GROUP = ATTN_HEADS // KV_HEADS
SCALE = HEAD_DIM ** -0.5


def _alibi_slope(h):
    return 2.0 ** (-8.0 * (h + 1) / ATTN_HEADS)


def _window_masks(n):
    t_idx = lax.broadcasted_iota(jnp.int32, (WINDOW, 3 * WINDOW), 0)
    s_idx = lax.broadcasted_iota(jnp.int32, (WINDOW, 3 * WINDOW), 1)
    rel = s_idx - WINDOW - t_idx
    absrel = jnp.abs(rel)
    key_pos = n * WINDOW - WINDOW + s_idx
    valid = (absrel <= WINDOW) & (key_pos >= 0) & (key_pos < SEQ)
    return absrel.astype(F32), valid


def _head_probs(qh, kw, absrel, valid, slope, sink):
    s = _dot_nt(qh, kw) * SCALE
    s = jnp.where(valid, s - slope * absrel, NEG_INF)
    m = jnp.maximum(jnp.max(s, axis=-1, keepdims=True), sink)
    p = jnp.exp(s - m)
    ps = jnp.exp(sink - m)
    inv = 1.0 / (jnp.sum(p, axis=-1, keepdims=True) + ps)
    return p * inv, ps * inv


def _attn_fwd(q, kp, vp, sinks):
    def body(sk_ref, q_ref, kp_ref, vp_ref, o_ref):
        def blk(n, carry):
            r0 = pl.multiple_of(n * WINDOW, WINDOW)
            absrel, valid = _window_masks(n)
            for gi in range(KV_HEADS):
                kw = kp_ref[pl.ds(r0, 3 * WINDOW), gi * HEAD_DIM:(gi + 1) * HEAD_DIM].astype(BF16)
                vw = vp_ref[pl.ds(r0, 3 * WINDOW), gi * HEAD_DIM:(gi + 1) * HEAD_DIM].astype(BF16)
                for hh in range(GROUP):
                    h = gi * GROUP + hh
                    cols = slice(h * HEAD_DIM, (h + 1) * HEAD_DIM)
                    qh = q_ref[pl.ds(r0, WINDOW), cols].astype(BF16)
                    pr, _ = _head_probs(qh, kw, absrel, valid, _alibi_slope(h), sk_ref[0, h])
                    o_ref[pl.ds(r0, WINDOW), cols] = _dot(pr.astype(BF16), vw)
            return carry

        lax.fori_loop(0, N_QBLOCKS, blk, 0)

    vmem = pl.BlockSpec(memory_space=pltpu.VMEM)
    return pl.pallas_call(
        body, name="attn_fwd",
        in_specs=[pl.BlockSpec(memory_space=pltpu.SMEM), vmem, vmem, vmem], out_specs=vmem,
        out_shape=jax.ShapeDtypeStruct((SEQ, ATTN_WIDTH), F32),
        compiler_params=_cparams(),
    )(sinks, q, kp, vp)


def _attn_bwd(q, kp, vp, sinks, do):
    def body(sk_ref, q_ref, kp_ref, vp_ref, do_ref, dq_ref, dkp_ref, dvp_ref, dsk_ref, dsk_acc):
        dkp_ref[...] = jnp.zeros_like(dkp_ref)
        dvp_ref[...] = jnp.zeros_like(dvp_ref)
        dsk_acc[...] = jnp.zeros_like(dsk_acc)

        def blk(n, carry):
            r0 = pl.multiple_of(n * WINDOW, WINDOW)
            absrel, valid = _window_masks(n)
            for gi in range(KV_HEADS):
                gcols = slice(gi * HEAD_DIM, (gi + 1) * HEAD_DIM)
                kw = kp_ref[pl.ds(r0, 3 * WINDOW), gcols].astype(BF16)
                vw = vp_ref[pl.ds(r0, 3 * WINDOW), gcols].astype(BF16)
                dkw = jnp.zeros((3 * WINDOW, HEAD_DIM), F32)
                dvw = jnp.zeros((3 * WINDOW, HEAD_DIM), F32)
                for hh in range(GROUP):
                    h = gi * GROUP + hh
                    cols = slice(h * HEAD_DIM, (h + 1) * HEAD_DIM)
                    qh = q_ref[pl.ds(r0, WINDOW), cols].astype(BF16)
                    doh = do_ref[pl.ds(r0, WINDOW), cols].astype(BF16)
                    pr, psink = _head_probs(qh, kw, absrel, valid, _alibi_slope(h), sk_ref[0, h])
                    dp = _dot_nt(doh, vw)
                    delta = jnp.sum(pr * dp, axis=-1, keepdims=True)
                    ds = (pr * (dp - delta)).astype(BF16)
                    dsk_acc[:, h:h + 1] += -(psink * delta)
                    dq_ref[pl.ds(r0, WINDOW), cols] = _dot(ds, kw) * SCALE
                    dkw = dkw + _dot_tn(ds, qh) * SCALE
                    dvw = dvw + _dot_tn(pr.astype(BF16), doh)
                dkp_ref[pl.ds(r0, 3 * WINDOW), gcols] += dkw
                dvp_ref[pl.ds(r0, 3 * WINDOW), gcols] += dvw
            return carry

        lax.fori_loop(0, N_QBLOCKS, blk, 0)
        dsk_ref[...] = jnp.sum(dsk_acc[...], axis=0, keepdims=True)

    vmem = pl.BlockSpec(memory_space=pltpu.VMEM)
    return pl.pallas_call(
        body, name="attn_bwd",
        in_specs=[pl.BlockSpec(memory_space=pltpu.SMEM), vmem, vmem, vmem, vmem],
        out_specs=[vmem, vmem, vmem, vmem],
        out_shape=[jax.ShapeDtypeStruct((SEQ, ATTN_WIDTH), F32),
                   jax.ShapeDtypeStruct((SEQ + 2 * WINDOW, KV_WIDTH), F32),
                   jax.ShapeDtypeStruct((SEQ + 2 * WINDOW, KV_WIDTH), F32),
                   jax.ShapeDtypeStruct((1, ATTN_HEADS), F32)],
        scratch_shapes=[pltpu.VMEM((WINDOW, ATTN_HEADS), F32)],
        compiler_params=_cparams(),
    )(sinks, q, kp, vp, do)


def _ssm_prep(lam_re, lam_im, log_dt, bpad_re, bpad_im):
    nb = 2 * N_LANE_BLOCKS

    def body(lr_ref, li_ref, ldt_ref, br_ref, bi_ref, ar_ref, ai_ref, bbr_ref, bbi_ref):
        lr = jnp.minimum(lr_ref[...], LAMBDA_RE_MAX)
        li = li_ref[...]
        dt = jnp.exp(ldt_ref[...])
        mag = jnp.exp(lr * dt)
        ar = mag * jnp.cos(li * dt)
        ai = mag * jnp.sin(li * dt)
        den = lr * lr + li * li
        cr = ((ar - 1.0) * lr + ai * li) / den
        ci = (ai * lr - (ar - 1.0) * li) / den
        ar_ref[...] = ar
        ai_ref[...] = ai
        for i in range(nb):
            br = br_ref[i]
            bi = bi_ref[i]
            cri, cii = cr[i:i + 1, :], ci[i:i + 1, :]
            bbr_ref[i] = (cri * br - cii * bi).astype(BF16)
            bbi_ref[i] = (cri * bi + cii * br).astype(BF16)

    return pl.pallas_call(
        body, name="ssm_prep",
        out_shape=[jax.ShapeDtypeStruct((nb, LANES), F32), jax.ShapeDtypeStruct((nb, LANES), F32),
                   jax.ShapeDtypeStruct((nb, LANES, LANES), BF16), jax.ShapeDtypeStruct((nb, LANES, LANES), BF16)],
        compiler_params=_cparams(),
    )(lam_re, lam_im, log_dt, bpad_re, bpad_im)


def _ssm_prep_bwd(lam_re, lam_im, log_dt, bpad_re, bpad_im, dar, dai, dbbr, dbbi):
    nb = 2 * N_LANE_BLOCKS

    def body(lr_ref, li_ref, ldt_ref, br_ref, bi_ref, dar_ref, dai_ref, dbbr_ref, dbbi_ref,
             glr_ref, gli_ref, gdt_ref, gbr_ref, gbi_ref, gcr_s, gci_s):
        lam = lr_ref[...]
        lr = jnp.minimum(lam, LAMBDA_RE_MAX)
        li = li_ref[...]
        dt = jnp.exp(ldt_ref[...])
        mag = jnp.exp(lr * dt)
        cs = jnp.cos(li * dt)
        sn = jnp.sin(li * dt)
        ar = mag * cs
        ai = mag * sn
        den = lr * lr + li * li
        nr = (ar - 1.0) * lr + ai * li
        ni = ai * lr - (ar - 1.0) * li
        cr = nr / den
        ci = ni / den
        for i in range(nb):
            br = br_ref[i]
            bi = bi_ref[i]
            gbbr = dbbr_ref[i]
            gbbi = dbbi_ref[i]
            cri, cii = cr[i:i + 1, :], ci[i:i + 1, :]
            gcr_s[i:i + 1, :] = jnp.sum(gbbr * br + gbbi * bi, axis=0, keepdims=True)
            gci_s[i:i + 1, :] = jnp.sum(gbbi * br - gbbr * bi, axis=0, keepdims=True)
            gbr_ref[i] = cri * gbbr + cii * gbbi
            gbi_ref[i] = cri * gbbi - cii * gbbr
        g_cr = gcr_s[...]
        g_ci = gci_s[...]
        g_nr = g_cr / den
        g_ni = g_ci / den
        g_den = -(g_cr * nr + g_ci * ni) / (den * den)
        g_ar = dar_ref[...] + g_nr * lr - g_ni * li
        g_ai = dai_ref[...] + g_nr * li + g_ni * lr
        g_lr = g_nr * (ar - 1.0) + g_ni * ai + g_den * 2.0 * lr
        g_li = g_nr * ai - g_ni * (ar - 1.0) + g_den * 2.0 * li
        g_mag = g_ar * cs + g_ai * sn
        g_th = (g_ai * cs - g_ar * sn) * mag
        g_lr = g_lr + g_mag * mag * dt
        g_li = g_li + g_th * dt
        g_dt = g_mag * mag * lr + g_th * li
        glr_ref[...] = jnp.where(lam < LAMBDA_RE_MAX, g_lr, 0.0)
        gli_ref[...] = g_li
        gl = g_dt * dt
        half = LANES // 2
        gdt_ref[:, 0:1] = jnp.sum(gl[:, :half], axis=1, keepdims=True)
        gdt_ref[:, 1:2] = jnp.sum(gl[:, half:], axis=1, keepdims=True)

    return pl.pallas_call(
        body, name="ssm_prep_bwd",
        out_shape=[jax.ShapeDtypeStruct((nb, LANES), F32), jax.ShapeDtypeStruct((nb, LANES), F32),
                   jax.ShapeDtypeStruct((nb, 2), F32),
                   jax.ShapeDtypeStruct((nb, LANES, LANES), F32), jax.ShapeDtypeStruct((nb, LANES, LANES), F32)],
        scratch_shapes=[pltpu.VMEM((nb, LANES), F32), pltpu.VMEM((nb, LANES), F32)],
        compiler_params=_cparams(),
    )(lam_re, lam_im, log_dt, bpad_re, bpad_im, dar, dai, dbbr, dbbi)


def _cmul(ar, ai, br, bi):
    return ar * br - ai * bi, ar * bi + ai * br


def _interleave_rows(src_ref, dst_ref):
    def step(j, carry):
        dst_ref[pl.ds(pl.multiple_of(j * 8, 8), 8), :] = src_ref[pl.ds(j, 8, stride=SCAN_CHUNK), :]
        return carry
    lax.fori_loop(0, SCAN_CHUNK, step, 0, unroll=4)


def _deinterleave_rows(src_ref, dst_ref):
    def step(j, carry):
        dst_ref[pl.ds(j, 8, stride=SCAN_CHUNK), :] = src_ref[pl.ds(pl.multiple_of(j * 8, 8), 8), :]
        return carry
    lax.fori_loop(0, SCAN_CHUNK, step, 0, unroll=4)


def _scan_inplace(re_ref, im_ref, a_re, a_im, reverse):
    nq = len(a_re)
    ch = SCAN_CHUNK
    ab_re = [jnp.broadcast_to(a, (8, LANES)) for a in a_re]
    ab_im = [jnp.broadcast_to(a, (8, LANES)) for a in a_im]

    def rows(j):
        jj = (ch - 1 - j) if reverse else j
        return pl.ds(pl.multiple_of(jj * 8, 8), 8)

    def sweep(init, store):
        def step(j, st):
            out = []
            r = rows(j)
            for qi in range(nq):
                xr, xi = st[2 * qi], st[2 * qi + 1]
                pr, pi = _cmul(ab_re[qi], ab_im[qi], xr, xi)
                xr = pr + re_ref[qi, r, :]
                xi = pi + im_ref[qi, r, :]
                if store:
                    re_ref[qi, r, :] = xr
                    im_ref[qi, r, :] = xi
                out += [xr, xi]
            return tuple(out)
        return lax.fori_loop(0, ch, step, tuple(init), unroll=2)

    zeros = [jnp.zeros((8, LANES), F32)] * (2 * nq)
    finals = sweep(zeros, store=False)

    row_id = lax.broadcasted_iota(jnp.int32, (8, LANES), 0)
    carries = []
    for qi in range(nq):
        pr, pi = ab_re[qi], ab_im[qi]
        for _ in range(8):
            pr, pi = _cmul(pr, pi, pr, pi)
        fr, fi = finals[2 * qi], finals[2 * qi + 1]
        sr = jnp.zeros((8, LANES), F32)
        si = jnp.zeros((8, LANES), F32)
        for _ in range(7):
            tr, ti = _cmul(pr, pi, sr, si)
            tr, ti = tr + fr, ti + fi
            if reverse:
                sr = jnp.where(row_id == 7, 0.0, pltpu.roll(tr, 7, axis=0))
                si = jnp.where(row_id == 7, 0.0, pltpu.roll(ti, 7, axis=0))
            else:
                sr = jnp.where(row_id == 0, 0.0, pltpu.roll(tr, 1, axis=0))
                si = jnp.where(row_id == 0, 0.0, pltpu.roll(ti, 1, axis=0))
        carries += [sr, si]
    sweep(carries, store=True)


SSM_Q = 4


def _ssm_fwd(u, are, aim, bbr, bbi, cre, cim, dskip):
    nq = SSM_Q

    def body(u_ref, ar_ref, ai_ref, bbr_ref, bbi_ref, cr_ref, ci_ref, d_ref, y_ref, xr_ref, xi_ref,
             sre, sim, up, yp):
        _interleave_rows(u_ref, up)
        uf = up[...]
        ub = uf.astype(BF16)
        yp[...] = d_ref[...] * uf
        for d in range(2):
            for qi in range(nq):
                sre[qi] = _dot(ub, bbr_ref[d, qi])
                sim[qi] = _dot(ub, bbi_ref[d, qi])
            _scan_inplace(sre, sim, [ar_ref[d, qi] for qi in range(nq)], [ai_ref[d, qi] for qi in range(nq)],
                          reverse=(d == 1))
            for qi in range(nq):
                xrb = sre[qi].astype(BF16)
                xib = sim[qi].astype(BF16)
                xr_ref[d, qi] = xrb
                xi_ref[d, qi] = xib
                yp[...] += _dot(xrb, cr_ref[d, qi]) - _dot(xib, ci_ref[d, qi])
        _deinterleave_rows(yp, y_ref)

    blk4 = lambda k: (0, k, 0, 0)
    return pl.pallas_call(
        body, name="ssm_fwd", grid=(SSM_WIDTH // LANES,),
        in_specs=[pl.BlockSpec((SEQ, LANES), lambda k: (0, k)),
                  pl.BlockSpec((2, nq, 1, LANES), blk4), pl.BlockSpec((2, nq, 1, LANES), blk4),
                  pl.BlockSpec((2, nq, LANES, LANES), blk4), pl.BlockSpec((2, nq, LANES, LANES), blk4),
                  pl.BlockSpec((2, nq, LANES, LANES), blk4), pl.BlockSpec((2, nq, LANES, LANES), blk4),
                  pl.BlockSpec((1, LANES), lambda k: (0, k))],
        out_specs=[pl.BlockSpec((SEQ, LANES), lambda k: (0, k)),
                   pl.BlockSpec((2, nq, SEQ, LANES), blk4), pl.BlockSpec((2, nq, SEQ, LANES), blk4)],
        out_shape=[jax.ShapeDtypeStruct((SEQ, SSM_WIDTH), F32),
                   jax.ShapeDtypeStruct((2, N_LANE_BLOCKS, SEQ, LANES), BF16),
                   jax.ShapeDtypeStruct((2, N_LANE_BLOCKS, SEQ, LANES), BF16)],
        scratch_shapes=[pltpu.VMEM((nq, SEQ, LANES), F32), pltpu.VMEM((nq, SEQ, LANES), F32),
                        pltpu.VMEM((SEQ, LANES), F32), pltpu.VMEM((SEQ, LANES), F32)],
        compiler_params=_cparams(("parallel",)),
    )(u, are, aim, bbr, bbi, cre, cim, dskip)


def _ssm_bwd(dy, u, xr, xi, are, aim, bbr, bbi, cre, cim, dskip):
    nq = SSM_Q
    body_rows = SEQ - 8

    def body(dy_ref, u_ref, xr_ref, xi_ref, ar_ref, ai_ref, bbr_ref, bbi_ref, cr_ref, ci_ref, d_ref,
             du_ref, dd_ref, dcr_ref, dci_ref, dbr_ref, dbi_ref, dar_ref, dai_ref, sre, sim, up, dyp, dup):
        _interleave_rows(u_ref, up)
        _interleave_rows(dy_ref, dyp)
        dyf = dyp[...]
        uf = up[...]
        dyb = dyf.astype(BF16)
        ub = uf.astype(BF16)
        dd_ref[...] = jnp.sum(dyf * uf, axis=0, keepdims=True)
        dup[...] = d_ref[...] * dyf
        row8 = lax.broadcasted_iota(jnp.int32, (8, LANES), 0)
        for d in range(2):
            for qi in range(nq):
                sre[qi] = _dot_nt(dyb, cr_ref[d, qi])
                sim[qi] = -_dot_nt(dyb, ci_ref[d, qi])
                dcr_ref[d, qi] = _dot_tn(xr_ref[d, qi], dyb)
                dci_ref[d, qi] = -_dot_tn(xi_ref[d, qi], dyb)
            _scan_inplace(sre, sim, [ar_ref[d, qi] for qi in range(nq)], [-ai_ref[d, qi] for qi in range(nq)],
                          reverse=(d == 0))
            for qi in range(nq):
                gr = sre[qi]
                gi = sim[qi]
                xrf = xr_ref[d, qi].astype(F32)
                xif = xi_ref[d, qi].astype(F32)
                if d == 0:
                    g_main_r, g_main_i = gr[8:], gi[8:]
                    x_main_r, x_main_i = xrf[:body_rows], xif[:body_rows]
                    g_edge_r, g_edge_i = gr[:8], gi[:8]
                    x_edge_r = jnp.where(row8 == 0, 0.0, pltpu.roll(xrf[body_rows:], 1, axis=0))
                    x_edge_i = jnp.where(row8 == 0, 0.0, pltpu.roll(xif[body_rows:], 1, axis=0))
                else:
                    g_main_r, g_main_i = gr[:body_rows], gi[:body_rows]
                    x_main_r, x_main_i = xrf[8:], xif[8:]
                    g_edge_r, g_edge_i = gr[body_rows:], gi[body_rows:]
                    x_edge_r = jnp.where(row8 == 7, 0.0, pltpu.roll(xrf[:8], 7, axis=0))
                    x_edge_i = jnp.where(row8 == 7, 0.0, pltpu.roll(xif[:8], 7, axis=0))
                dar_ref[d, qi] = (jnp.sum(g_main_r * x_main_r + g_main_i * x_main_i, axis=0, keepdims=True)
                                  + jnp.sum(g_edge_r * x_edge_r + g_edge_i * x_edge_i, axis=0, keepdims=True))
                dai_ref[d, qi] = (jnp.sum(g_main_i * x_main_r - g_main_r * x_main_i, axis=0, keepdims=True)
                                  + jnp.sum(g_edge_i * x_edge_r - g_edge_r * x_edge_i, axis=0, keepdims=True))
                grb = gr.astype(BF16)
                gib = gi.astype(BF16)
                dup[...] += _dot_nt(grb, bbr_ref[d, qi]) + _dot_nt(gib, bbi_ref[d, qi])
                dbr_ref[d, qi] = _dot_tn(ub, grb)
                dbi_ref[d, qi] = _dot_tn(ub, gib)
        _deinterleave_rows(dup, du_ref)

    blk4 = lambda k: (0, k, 0, 0)
    col = lambda k: (0, k)
    w_spec = pl.BlockSpec((2, nq, LANES, LANES), blk4)
    a_spec = pl.BlockSpec((2, nq, 1, LANES), blk4)
    x_spec = pl.BlockSpec((2, nq, SEQ, LANES), blk4)
    w_shape = jax.ShapeDtypeStruct((2, N_LANE_BLOCKS, LANES, LANES), F32)
    a_shape = jax.ShapeDtypeStruct((2, N_LANE_BLOCKS, 1, LANES), F32)
    return pl.pallas_call(
        body, name="ssm_bwd", grid=(SSM_WIDTH // LANES,),
        in_specs=[pl.BlockSpec((SEQ, LANES), col), pl.BlockSpec((SEQ, LANES), col), x_spec, x_spec,
                  a_spec, a_spec, w_spec, w_spec, w_spec, w_spec, pl.BlockSpec((1, LANES), col)],
        out_specs=[pl.BlockSpec((SEQ, LANES), col), pl.BlockSpec((1, LANES), col),
                   w_spec, w_spec, w_spec, w_spec, a_spec, a_spec],
        out_shape=[jax.ShapeDtypeStruct((SEQ, SSM_WIDTH), F32), jax.ShapeDtypeStruct((1, SSM_WIDTH), F32),
                   w_shape, w_shape, w_shape, w_shape, a_shape, a_shape],
        scratch_shapes=[pltpu.VMEM((nq, SEQ, LANES), F32), pltpu.VMEM((nq, SEQ, LANES), F32),
                        pltpu.VMEM((SEQ, LANES), F32), pltpu.VMEM((SEQ, LANES), F32), pltpu.VMEM((SEQ, LANES), F32)],
        compiler_params=_cparams(("parallel",)),
    )(dy, u, xr, xi, are, aim, bbr, bbi, cre, cim, dskip)


GELU_C = 0.7978845608028654
GELU_K = 0.044715


def _gelu(y):
    return 0.5 * y * (1.0 + jnp.tanh(GELU_C * (y + GELU_K * y * y * y)))


def _gelu_grad(y):
    t = jnp.tanh(GELU_C * (y + GELU_K * y * y * y))
    return 0.5 * (1.0 + t) + 0.5 * y * (1.0 - t * t) * GELU_C * (1.0 + 3.0 * GELU_K * y * y)


def _mixout_fwd(o, y, glu_w, glu_b, gan, gsn, wout, x1):
    tm = MIX_TM

    def body(o_ref, y_ref, gw_ref, gb_ref, gan_ref, gsn_ref, w_ref, x1_ref, x2_ref, mx_ref):
        yg = _gelu(y_ref[...])
        z = _dot(yg.astype(BF16), gw_ref[...]) + gb_ref[...]
        so = yg * _sigmoid(z)
        na = _rms_fwd(o_ref[...], gan_ref[...])
        ns = _rms_fwd(so, gsn_ref[...])
        mixed = jnp.concatenate([na, ns], axis=-1).astype(BF16)
        mx_ref[...] = mixed
        x2_ref[...] = x1_ref[...] + _dot(mixed, w_ref[...])

    row = lambda i: (i, 0)
    const = lambda i: (0, 0)
    return pl.pallas_call(
        body, name="mixout_fwd", grid=(SEQ // tm,),
        in_specs=[pl.BlockSpec((tm, ATTN_WIDTH), row), pl.BlockSpec((tm, SSM_WIDTH), row),
                  pl.BlockSpec((SSM_WIDTH, SSM_WIDTH), const), pl.BlockSpec((1, SSM_WIDTH), const),
                  pl.BlockSpec((1, ATTN_WIDTH), const), pl.BlockSpec((1, SSM_WIDTH), const),
                  pl.BlockSpec((D_MODEL, D_MODEL), const), pl.BlockSpec((tm, D_MODEL), row)],
        out_specs=[pl.BlockSpec((tm, D_MODEL), row), pl.BlockSpec((tm, D_MODEL), row)],
        out_shape=[jax.ShapeDtypeStruct((SEQ, D_MODEL), F32), jax.ShapeDtypeStruct((SEQ, D_MODEL), BF16)],
        compiler_params=_cparams(("parallel",)),
    )(o, y, glu_w, glu_b, gan, gsn, wout, x1)


def _mixout_bwd(dx2, o, y, glu_w, glu_b, gan, gsn, wout):
    tm = MIX_TM

    def body(dx2_ref, o_ref, y_ref, gw_ref, gb_ref, gan_ref, gsn_ref, w_ref,
             do_ref, dy_ref, dz_ref, yg_ref, dxb_ref, dgan_ref, dgsn_ref, dgb_ref):
        i = pl.program_id(0)
        dxb = dx2_ref[...].astype(BF16)
        dxb_ref[...] = dxb
        dmixed = _dot_nt(dxb, w_ref[...])
        do, dgan = _rms_bwd(dmixed[:, :ATTN_WIDTH], o_ref[...], gan_ref[...])
        do_ref[...] = do
        yv = y_ref[...]
        yg = _gelu(yv)
        ygb = yg.astype(BF16)
        yg_ref[...] = ygb
        sg = _sigmoid(_dot(ygb, gw_ref[...]) + gb_ref[...])
        dso, dgsn = _rms_bwd(dmixed[:, ATTN_WIDTH:], yg * sg, gsn_ref[...])
        dz = dso * yg * sg * (1.0 - sg)
        dzb = dz.astype(BF16)
        dz_ref[...] = dzb
        dyg = dso * sg + _dot_nt(dzb, gw_ref[...])
        dy_ref[...] = dyg * _gelu_grad(yv)
        dgb = jnp.sum(dz, axis=0, keepdims=True)

        @pl.when(i == 0)
        def _():
            dgan_ref[...] = dgan
            dgsn_ref[...] = dgsn
            dgb_ref[...] = dgb

        @pl.when(i != 0)
        def _():
            dgan_ref[...] += dgan
            dgsn_ref[...] += dgsn
            dgb_ref[...] += dgb

    row = lambda i: (i, 0)
    const = lambda i: (0, 0)
    return pl.pallas_call(
        body, name="mixout_bwd", grid=(SEQ // tm,),
        in_specs=[pl.BlockSpec((tm, D_MODEL), row), pl.BlockSpec((tm, ATTN_WIDTH), row),
                  pl.BlockSpec((tm, SSM_WIDTH), row),
                  pl.BlockSpec((SSM_WIDTH, SSM_WIDTH), const), pl.BlockSpec((1, SSM_WIDTH), const),
                  pl.BlockSpec((1, ATTN_WIDTH), const), pl.BlockSpec((1, SSM_WIDTH), const),
                  pl.BlockSpec((D_MODEL, D_MODEL), const)],
        out_specs=[pl.BlockSpec((tm, ATTN_WIDTH), row), pl.BlockSpec((tm, SSM_WIDTH), row),
                   pl.BlockSpec((tm, SSM_WIDTH), row), pl.BlockSpec((tm, SSM_WIDTH), row),
                   pl.BlockSpec((tm, D_MODEL), row),
                   pl.BlockSpec((1, ATTN_WIDTH), const), pl.BlockSpec((1, SSM_WIDTH), const),
                   pl.BlockSpec((1, SSM_WIDTH), const)],
        out_shape=[jax.ShapeDtypeStruct((SEQ, ATTN_WIDTH), F32), jax.ShapeDtypeStruct((SEQ, SSM_WIDTH), F32),
                   jax.ShapeDtypeStruct((SEQ, SSM_WIDTH), BF16), jax.ShapeDtypeStruct((SEQ, SSM_WIDTH), BF16),
                   jax.ShapeDtypeStruct((SEQ, D_MODEL), BF16),
                   jax.ShapeDtypeStruct((1, ATTN_WIDTH), F32), jax.ShapeDtypeStruct((1, SSM_WIDTH), F32),
                   jax.ShapeDtypeStruct((1, SSM_WIDTH), F32)],
        compiler_params=_cparams(("arbitrary",)),
    )(dx2, o, y, glu_w, glu_b, gan, gsn, wout)


def _loss_head(x, g, target):
    tm = MIX_TM

    def body(x_ref, g_ref, t_ref, loss_ref, dx_ref, dg_ref):
        i = pl.program_id(0)
        xv = x_ref[...]
        gv = g_ref[...]
        err = _rms_fwd(xv, gv) - t_ref[...]
        part = jnp.broadcast_to(0.5 * jnp.sum(err * err) / D_MODEL, (1, LANES))
        dx, dg = _rms_bwd(err * (1.0 / D_MODEL), xv, gv)
        dx_ref[...] = dx

        @pl.when(i == 0)
        def _():
            loss_ref[...] = part
            dg_ref[...] = dg

        @pl.when(i != 0)
        def _():
            loss_ref[...] += part
            dg_ref[...] += dg

    row = lambda i: (i, 0)
    const = lambda i: (0, 0)
    return pl.pallas_call(
        body, name="loss_head", grid=(SEQ // tm,),
        in_specs=[pl.BlockSpec((tm, D_MODEL), row), pl.BlockSpec((1, D_MODEL), const),
                  pl.BlockSpec((tm, D_MODEL), row)],
        out_specs=[pl.BlockSpec((1, LANES), const), pl.BlockSpec((tm, D_MODEL), row),
                   pl.BlockSpec((1, D_MODEL), const)],
        out_shape=[jax.ShapeDtypeStruct((1, LANES), F32), jax.ShapeDtypeStruct((SEQ, D_MODEL), F32),
                   jax.ShapeDtypeStruct((1, D_MODEL), F32)],
        compiler_params=_cparams(("arbitrary",)),
    )(x, g, target)


def _embed_blocks(b):
    b5 = b.reshape(2, 16, 2, 64, 16).transpose(0, 1, 2, 4, 3)
    eye2 = jnp.eye(2, dtype=b.dtype)
    m = (b5[:, :, :, :, None, :] * eye2[None, None, :, None, :, None]).reshape(2, 16, 32, LANES)
    sel = (jnp.arange(16)[:, None] % 4 == jnp.arange(4)[None, :]).astype(b.dtype)
    return (m[:, :, None, :, :] * sel[None, :, :, None, None]).reshape(2, 16, LANES, LANES)


def _extract_blocks(m):
    sel = (jnp.arange(16)[:, None] % 4 == jnp.arange(4)[None, :]).astype(m.dtype)
    m = jnp.sum(m.reshape(2, 16, 4, 32, LANES) * sel[None, :, :, None, None], axis=2)
    eye2 = jnp.eye(2, dtype=m.dtype)
    m = jnp.sum(m.reshape(2, 16, 2, 16, 2, 64) * eye2[None, None, :, None, :, None], axis=4)
    return m.transpose(0, 1, 2, 4, 3).reshape(2, 32, 64, 16)


def _local_step(x, target, w, p):
    x1, h1, a1, b1 = _ffn_fwd(x, p["norm_ffn1"], w["wgt1"], w["wut1"], w["wd1"], "ffn1_fwd")
    h2, q, k, v, u = _mixin_fwd(x1, p["norm_mix"], w["wint"])
    kp = jnp.pad(k, ((WINDOW, WINDOW), (0, 0)))
    vp = jnp.pad(v, ((WINDOW, WINDOW), (0, 0)))
    o = _attn_fwd(q, kp, vp, p["attn_sinks"])

    lam_re = p["ssm_lambda_re"].reshape(2 * N_LANE_BLOCKS, LANES)
    lam_im = p["ssm_lambda_im"].reshape(2 * N_LANE_BLOCKS, LANES)
    log_dt = jnp.repeat(p["ssm_log_dt"].reshape(2, 32), 64, axis=-1).reshape(2 * N_LANE_BLOCKS, LANES)
    bpad_re = _embed_blocks(p["ssm_b_re"].reshape(2, 32, 64, 16)).reshape(2 * N_LANE_BLOCKS, LANES, LANES)
    bpad_im = _embed_blocks(p["ssm_b_im"].reshape(2, 32, 64, 16)).reshape(2 * N_LANE_BLOCKS, LANES, LANES)
    c_t = lambda c: _embed_blocks(c.reshape(2, 32, 16, 64).transpose(0, 1, 3, 2)).transpose(0, 1, 3, 2)
    cre = c_t(p["ssm_c_re"]).astype(BF16)
    cim = c_t(p["ssm_c_im"]).astype(BF16)
    a_re, a_im, bbr, bbi = _ssm_prep(lam_re, lam_im, log_dt, bpad_re, bpad_im)
    shape_a = (2, N_LANE_BLOCKS, 1, LANES)
    shape_w = (2, N_LANE_BLOCKS, LANES, LANES)
    a_re4, a_im4 = a_re.reshape(shape_a), a_im.reshape(shape_a)
    bbr4, bbi4 = bbr.reshape(shape_w), bbi.reshape(shape_w)
    dskip = p["ssm_d"].reshape(1, SSM_WIDTH)
    y, xr, xi = _ssm_fwd(u, a_re4, a_im4, bbr4, bbi4, cre, cim, dskip)

    x2, mixed = _mixout_fwd(o, y, w["glu"], p["ssm_glu_b"], p["attn_out_norm"], p["ssm_out_norm"], w["wout"], x1)
    x3, h3, a3, b3 = _ffn_fwd(x2, p["norm_ffn2"], w["wgt2"], w["wut2"], w["wd2"], "ffn2_fwd")

    loss, dx3, d_final = _loss_head(x3, p["final_norm"], target)
    dx2, da3, db3, s3, df3, d_n2 = _ffn_bwd_act(dx3, x2, p["norm_ffn2"], a3, b3, w["wgt2"], w["wut2"], w["wd2"],
                                                "ffn2_bwd_act")
    g_wgt2, g_wut2, g_wd2 = _mm_tn([(da3, h3), (db3, h3), (s3, df3)], "ffn2_bwd_w")

    do, dy, dz, ygb, dx2b, d_gan, d_gsn, d_glub = _mixout_bwd(
        dx2, o, y, w["glu"], p["ssm_glu_b"], p["attn_out_norm"], p["ssm_out_norm"], w["wout"])
    (g_wout,) = _mm_tn([(mixed, dx2b)], "wout_bwd_w")
    (g_glu,) = _mm_tn([(ygb, dz)], "glu_bwd_w")

    du, d_dskip, dcre, dcim, dbbr, dbbi, dar, dai = _ssm_bwd(dy, u, xr, xi, a_re4, a_im4, bbr4, bbi4, cre, cim, dskip)
    nb = 2 * N_LANE_BLOCKS
    g_lre, g_lim, g_ldt, g_bpr, g_bpi = _ssm_prep_bwd(
        lam_re, lam_im, log_dt, bpad_re, bpad_im, dar.reshape(nb, LANES), dai.reshape(nb, LANES),
        dbbr.reshape(nb, LANES, LANES), dbbi.reshape(nb, LANES, LANES))

    dq, dkp, dvp, d_sinks = _attn_bwd(q, kp, vp, p["attn_sinks"], do)
    dk = dkp[WINDOW:WINDOW + SEQ]
    dv = dvp[WINDOW:WINDOW + SEQ]
    dx1, dproj, d_nmix = _mixin_bwd(dq, dk, dv, du, w["wint"], x1, p["norm_mix"], dx2)
    (g_wint,) = _mm_tn([(dproj, h2)], "win_bwd_w")

    dx0, da1, db1, s1, df1, d_n1 = _ffn_bwd_act(dx1, x, p["norm_ffn1"], a1, b1, w["wgt1"], w["wut1"], w["wd1"],
                                                "ffn1_bwd_act")
    g_wgt1, g_wut1, g_wd1 = _mm_tn([(da1, h1), (db1, h1), (s1, df1)], "ffn1_bwd_w")

    c_back = lambda g: _extract_blocks(g.transpose(0, 1, 3, 2)).transpose(0, 1, 3, 2)
    big = dict(wgt1=g_wgt1, wut1=g_wut1, wd1=g_wd1, wint=g_wint, glu=g_glu, wout=g_wout,
               wgt2=g_wgt2, wut2=g_wut2, wd2=g_wd2)
    small = dict(
        norm_ffn1=d_n1, norm_mix=d_nmix, attn_sinks=d_sinks,
        ssm_lambda_re=g_lre.reshape(64, 64), ssm_lambda_im=g_lim.reshape(64, 64),
        ssm_log_dt=g_ldt.reshape(2, 32),
        ssm_b_re=_extract_blocks(g_bpr.reshape(shape_w)).reshape(4096, 16),
        ssm_b_im=_extract_blocks(g_bpi.reshape(shape_w)).reshape(4096, 16),
        ssm_c_re=c_back(dcre).reshape(1024, 64), ssm_c_im=c_back(dcim).reshape(1024, 64),
        ssm_d=d_dskip.reshape(32, 16), ssm_glu_b=d_glub, attn_out_norm=d_gan, ssm_out_norm=d_gsn,
        norm_ffn2=d_n2, final_norm=d_final)
    return loss, dx0, big, small


BIG = dict(
    wgt1=("ffn1_w_gate", 352, 1024, True), wut1=("ffn1_w_up", 352, 1024, True), wd1=("ffn1_w_down", 352, 1024, False),
    wint=("w_in", 160, 1024, True), glu=("ssm_glu_w", 64, 512, False), wout=("w_out", 128, 1024, False),
    wgt2=("ffn2_w_gate", 352, 1024, True), wut2=("ffn2_w_up", 352, 1024, True), wd2=("ffn2_w_down", 352, 1024, False))

SMALL = dict(
    norm_ffn1=(1, 1024), norm_mix=(1, 1024), attn_sinks=(1, 8), ssm_lambda_re=(64, 64), ssm_lambda_im=(64, 64),
    ssm_log_dt=(2, 32), ssm_b_re=(4096, 16), ssm_b_im=(4096, 16), ssm_c_re=(1024, 64), ssm_c_im=(1024, 64),
    ssm_d=(32, 16), ssm_glu_b=(1, 512), attn_out_norm=(1, 512), ssm_out_norm=(1, 512), norm_ffn2=(1, 1024),
    final_norm=(1, 1024))
SMALL_TOTAL = sum(r * c for r, c in SMALL.values())
SMALL_ROWS = -(-SMALL_TOTAL // (8 * LANES)) * 8


def _pad_to(n, mult):
    return -(-n // mult) * mult


def _transpose_2d(x, rows_out, cols_out):
    r_in, c_in = x.shape
    rp, cp = _pad_to(r_in, LANES), _pad_to(c_in, LANES)
    if cp != c_in:
        x = jnp.concatenate([x, jnp.zeros((r_in, cp - c_in), x.dtype)], axis=1)
    if rp != r_in:
        x = jnp.concatenate([x, jnp.zeros((rp - r_in, cp), x.dtype)], axis=0)
    return x.T[:rows_out, :cols_out]


def _cast_shards(shards):
    names = list(BIG)

    def body(*refs):
        ins, outs = refs[:len(names)], refs[len(names):]
        for idx, n in enumerate(names):
            _, rows, cols, transposed = BIG[n]
            v = ins[idx][...]
            if transposed:
                v = _transpose_2d(v, rows, cols)
            outs[idx][...] = v.astype(BF16)

    return pl.pallas_call(
        body, name="cast_shards",
        out_shape=[jax.ShapeDtypeStruct((BIG[n][1], BIG[n][2]), BF16) for n in names],
        compiler_params=_cparams(),
    )(*[shards[n] for n in names])


def _peer(x, y, c, r):
    px = 1 - x if r & 4 else x
    py = 1 - y if r & 2 else y
    pc = 1 - c if r & 1 else c
    return px, py, pc


def _all_gather(shards):
    names = list(BIG)
    nk = len(names)

    def body(*refs):
        ins, outs = refs[:nk], refs[nk:2 * nk]
        send_sems, recv_sems, local_sems = refs[2 * nk:]
        x, y, c = lax.axis_index("x"), lax.axis_index("y"), lax.axis_index("c")
        me = 4 * x + 2 * y + c
        copies = []
        for k in range(nk):
            mine = pltpu.make_async_copy(ins[k], outs[k].at[me], local_sems.at[k])
            mine.start()
            copies.append(mine)
            for r in range(1, N_DEV):
                cp = pltpu.make_async_remote_copy(
                    src_ref=ins[k], dst_ref=outs[k].at[me], send_sem=send_sems.at[k, r - 1],
                    recv_sem=recv_sems.at[k, r - 1], device_id=_peer(x, y, c, r), device_id_type=MESH_ID)
                cp.start()
                copies.append(cp)
        for cp in copies:
            cp.wait()

    hbm = pl.BlockSpec(memory_space=pl.ANY)
    return pl.pallas_call(
        body, name="all_gather",
        in_specs=[hbm] * nk, out_specs=[hbm] * nk,
        out_shape=[jax.ShapeDtypeStruct((N_DEV, BIG[n][1], BIG[n][2]), BF16) for n in names],
        scratch_shapes=[pltpu.SemaphoreType.DMA((nk, N_DEV - 1)), pltpu.SemaphoreType.DMA((nk, N_DEV - 1)),
                        pltpu.SemaphoreType.DMA((nk,))],
        compiler_params=pltpu.CompilerParams(has_side_effects=True),
    )(*[shards[n] for n in names])


def _exchange_grads(big, small_packed):
    names = list(BIG)
    nk = len(names) + 1

    def body(*refs):
        ins, outs = refs[:nk], refs[nk:2 * nk]
        send_sems, recv_sems, local_sems = refs[2 * nk:]
        x, y, c = lax.axis_index("x"), lax.axis_index("y"), lax.axis_index("c")
        me = 4 * x + 2 * y + c
        copies = []
        for k in range(nk):
            is_small = k == nk - 1
            own = ins[k] if is_small else ins[k].at[me]
            mine = pltpu.make_async_copy(own, outs[k].at[me], local_sems.at[k])
            mine.start()
            copies.append(mine)
            for r in range(1, N_DEV):
                px, py, pc = _peer(x, y, c, r)
                src = ins[k] if is_small else ins[k].at[4 * px + 2 * py + pc]
                cp = pltpu.make_async_remote_copy(
                    src_ref=src, dst_ref=outs[k].at[me], send_sem=send_sems.at[k, r - 1],
                    recv_sem=recv_sems.at[k, r - 1], device_id=(px, py, pc), device_id_type=MESH_ID)
                cp.start()
                copies.append(cp)
        for cp in copies:
            cp.wait()

    hbm = pl.BlockSpec(memory_space=pl.ANY)
    out_shape = [jax.ShapeDtypeStruct((N_DEV, BIG[n][1], BIG[n][2]), BF16) for n in names]
    out_shape.append(jax.ShapeDtypeStruct((N_DEV, SMALL_ROWS, LANES), F32))
    return pl.pallas_call(
        body, name="exchange_grads",
        in_specs=[hbm] * nk, out_specs=[hbm] * nk, out_shape=out_shape,
        scratch_shapes=[pltpu.SemaphoreType.DMA((nk, N_DEV - 1)), pltpu.SemaphoreType.DMA((nk, N_DEV - 1)),
                        pltpu.SemaphoreType.DMA((nk,))],
        compiler_params=pltpu.CompilerParams(has_side_effects=True),
    )(*[big[n] for n in names], small_packed)


def _adamw_math(w, g, m, v):
    m2 = ADAM_B1 * m + (1.0 - ADAM_B1) * g
    v2 = ADAM_B2 * v + (1.0 - ADAM_B2) * (g * g)
    m_hat = m2 / (1.0 - ADAM_B1 ** ADAM_STEP)
    v_hat = v2 / (1.0 - ADAM_B2 ** ADAM_STEP)
    delta = -ADAM_LR * (m_hat / (jnp.sqrt(v_hat) + ADAM_EPS) + ADAM_WD * w)
    return delta, m2, v2


def _adamw_big(parts, w, m, v, name, transposed):
    shape = w.shape

    def body(p_ref, w_ref, m_ref, v_ref, g_ref, d_ref, m2_ref, v2_ref):
        g = p_ref[0].astype(F32)
        for i in range(1, N_DEV):
            g = g + p_ref[i].astype(F32)
        if transposed:
            g = _transpose_2d(g, shape[0], shape[1])
        delta, m2, v2 = _adamw_math(w_ref[...], g, m_ref[...], v_ref[...])
        g_ref[...] = g
        d_ref[...] = delta
        m2_ref[...] = m2
        v2_ref[...] = v2

    return pl.pallas_call(
        body, name=name, out_shape=[jax.ShapeDtypeStruct(shape, F32)] * 4, compiler_params=_cparams(),
    )(parts, w, m, v)


def _adamw_small(parts, w, m, v):
    def body(p_ref, w_ref, m_ref, v_ref, g_ref, d_ref, m2_ref, v2_ref):
        g = p_ref[0]
        for i in range(1, N_DEV):
            g = g + p_ref[i]
        delta, m2, v2 = _adamw_math(w_ref[...], g, m_ref[...], v_ref[...])
        g_ref[...] = g
        d_ref[...] = delta
        m2_ref[...] = m2
        v2_ref[...] = v2

    return pl.pallas_call(
        body, name="adamw_small", out_shape=[jax.ShapeDtypeStruct((SMALL_ROWS, LANES), F32)] * 4,
        compiler_params=_cparams(),
    )(parts, w, m, v)


def _pack_small(d):
    flat = jnp.concatenate([d[n].reshape(-1) for n in SMALL])
    flat = jnp.pad(flat, (0, SMALL_ROWS * LANES - SMALL_TOTAL))
    return flat.reshape(SMALL_ROWS, LANES)


def _unpack_small(packed, shapes):
    flat = packed.reshape(-1)
    out, off = {}, 0
    for n, (r, c) in SMALL.items():
        out[n] = flat[off:off + r * c].reshape(shapes[n])
        off += r * c
    return out


WEIGHT_NAMES = ['norm_ffn1', 'ffn1_w_gate', 'ffn1_w_up', 'ffn1_w_down', 'norm_mix', 'w_in', 'attn_sinks',
                'ssm_lambda_re', 'ssm_lambda_im', 'ssm_log_dt', 'ssm_b_re', 'ssm_b_im', 'ssm_c_re', 'ssm_c_im',
                'ssm_d', 'ssm_glu_w', 'ssm_glu_b', 'attn_out_norm', 'ssm_out_norm', 'w_out', 'norm_ffn2',
                'ffn2_w_gate', 'ffn2_w_up', 'ffn2_w_down', 'final_norm']


def kernel(x, norm_ffn1, ffn1_w_gate, ffn1_w_up, ffn1_w_down, norm_mix, w_in, attn_sinks, ssm_lambda_re, ssm_lambda_im, ssm_log_dt, ssm_b_re, ssm_b_im, ssm_c_re, ssm_c_im, ssm_d, ssm_glu_w, ssm_glu_b, attn_out_norm, ssm_out_norm, w_out, norm_ffn2, ffn2_w_gate, ffn2_w_up, ffn2_w_down, final_norm, loss_target, m_norm_ffn1, m_ffn1_w_gate, m_ffn1_w_up, m_ffn1_w_down, m_norm_mix, m_w_in, m_attn_sinks, m_ssm_lambda_re, m_ssm_lambda_im, m_ssm_log_dt, m_ssm_b_re, m_ssm_b_im, m_ssm_c_re, m_ssm_c_im, m_ssm_d, m_ssm_glu_w, m_ssm_glu_b, m_attn_out_norm, m_ssm_out_norm, m_w_out, m_norm_ffn2, m_ffn2_w_gate, m_ffn2_w_up, m_ffn2_w_down, m_final_norm, v_norm_ffn1, v_ffn1_w_gate, v_ffn1_w_up, v_ffn1_w_down, v_norm_mix, v_w_in, v_attn_sinks, v_ssm_lambda_re, v_ssm_lambda_im, v_ssm_log_dt, v_ssm_b_re, v_ssm_b_im, v_ssm_c_re, v_ssm_c_im, v_ssm_d, v_ssm_glu_w, v_ssm_glu_b, v_attn_out_norm, v_ssm_out_norm, v_w_out, v_norm_ffn2, v_ffn2_w_gate, v_ffn2_w_up, v_ffn2_w_down, v_final_norm):
    args = dict(locals())
    weights = {n: args[n] for n in WEIGHT_NAMES}
    moms = {n: args["m_" + n] for n in WEIGHT_NAMES}
    vars_ = {n: args["v_" + n] for n in WEIGHT_NAMES}
    big_of = {BIG[k][0]: k for k in BIG}

    def shard2d(a):
        return a.reshape(a.shape[-2], a.shape[-1])

    shards = _cast_shards({k: shard2d(weights[BIG[k][0]]) for k in BIG})
    gathered = _all_gather(dict(zip(BIG, shards)))
    w = {k: g.reshape(N_DEV * BIG[k][1], BIG[k][2]) for k, g in zip(BIG, gathered)}

    small_p = {n: weights[n].reshape(SMALL[n]) for n in SMALL}
    loss, grad_x, g_big, g_small = _local_step(x.reshape(SEQ, D_MODEL), loss_target.reshape(SEQ, D_MODEL), w, small_p)

    exch = _exchange_grads({k: g_big[k].reshape(N_DEV, BIG[k][1], BIG[k][2]) for k in BIG}, _pack_small(g_small))
    outs = {}
    for k, parts in zip(BIG, exch[:-1]):
        n = BIG[k][0]
        outs[n] = [o.reshape(weights[n].shape) for o in
                   _adamw_big(parts, shard2d(weights[n]), shard2d(moms[n]), shard2d(vars_[n]),
                              "adamw_" + n, BIG[k][3])]
    small_shapes = {n: weights[n].shape for n in SMALL}
    packed = _adamw_small(exch[-1], _pack_small({n: weights[n] for n in SMALL}),
                          _pack_small({n: moms[n] for n in SMALL}), _pack_small({n: vars_[n] for n in SMALL}))
    unpacked = [_unpack_small(pk, small_shapes) for pk in packed]
    for n in SMALL:
        outs[n] = [u[n] for u in unpacked]

    total_loss = lax.psum(loss[0, 0], ("x", "y", "c"))
    result = [total_loss, grad_x.reshape(x.shape)]
    for i in range(4):
        result += [outs[n][i] for n in WEIGHT_NAMES]
    return tuple(result)
```

```python
import functools

import jax
import jax.numpy as jnp
from jax import lax
from jax.experimental import pallas as pl
from jax.experimental.pallas import tpu as pltpu

F32 = jnp.float32
BF16 = jnp.bfloat16

N_DEV = 8
SEQ = 2048
D_MODEL = 1024
D_FF = 2816
ATTN_HEADS = 8
KV_HEADS = 2
HEAD_DIM = 64
ATTN_WIDTH = 512
KV_WIDTH = 128
WINDOW = 128
SSM_WIDTH = 512
IN_WIDTH = 1280
EPS = 1e-6
NEG_INF = -1e30
LAMBDA_RE_MAX = -1e-4
LANES = 128
N_LANE_BLOCKS = 16
SCAN_CHUNK = SEQ // 8

ADAM_LR = 0.001
ADAM_B1 = 0.9
ADAM_B2 = 0.999
ADAM_EPS = 1e-08
ADAM_WD = 0.01
ADAM_STEP = 10

VMEM_LIMIT = 56 * 1024 * 1024
MESH_ID = pl.DeviceIdType.MESH


def _cparams(sem=None):
    return pltpu.CompilerParams(dimension_semantics=sem, vmem_limit_bytes=VMEM_LIMIT)


def _dot(a, b):
    return jnp.dot(a, b, preferred_element_type=F32)


def _dot_nt(a, b):
    return lax.dot_general(a, b, (((1,), (1,)), ((), ())), preferred_element_type=F32)


def _dot_tn(a, b):
    return lax.dot_general(a, b, (((0,), (0,)), ((), ())), preferred_element_type=F32)


def _rms_fwd(x, g):
    r = lax.rsqrt(jnp.mean(x * x, axis=-1, keepdims=True) + EPS)
    return x * r * g


def _rms_bwd(dh, x, g):
    r = lax.rsqrt(jnp.mean(x * x, axis=-1, keepdims=True) + EPS)
    xh = x * r
    dg = jnp.sum(dh * xh, axis=0, keepdims=True)
    dxh = dh * g
    dx = r * (dxh - xh * jnp.mean(dxh * xh, axis=-1, keepdims=True))
    return dx, dg


def _sigmoid(x):
    return 1.0 / (1.0 + jnp.exp(-x))


FFN_TM = 512
FFN_TF = 256


def _ffn_fwd(x, g, wgt, wut, wd, name, after=None):
    tm, tf = FFN_TM, FFN_TF
    nj = D_FF // tf
    deps = [] if after is None else [after]

    def body(x_ref, g_ref, wg_ref, wu_ref, wd_ref, *rest):
        xo_ref, h_ref, a_ref, b_ref, h_s, acc = rest[len(deps):]
        j = pl.program_id(1)

        @pl.when(j == 0)
        def _():
            h = _rms_fwd(x_ref[...], g_ref[...]).astype(BF16)
            h_s[...] = h
            h_ref[...] = h
            acc[...] = jnp.zeros_like(acc)

        h = h_s[...]
        a = _dot_nt(h, wg_ref[...])
        b = _dot_nt(h, wu_ref[...])
        a_ref[...] = a.astype(BF16)
        b_ref[...] = b.astype(BF16)
        s = (a * _sigmoid(a) * b).astype(BF16)
        acc[...] += _dot(s, wd_ref[...])

        @pl.when(j == nj - 1)
        def _():
            xo_ref[...] = x_ref[...] + 0.5 * acc[...]

    return pl.pallas_call(
        body, name=name, grid=(SEQ // tm, nj),
        in_specs=[pl.BlockSpec((tm, D_MODEL), lambda i, j: (i, 0)),
                  pl.BlockSpec((1, D_MODEL), lambda i, j: (0, 0)),
                  pl.BlockSpec((tf, D_MODEL), lambda i, j: (j, 0)),
                  pl.BlockSpec((tf, D_MODEL), lambda i, j: (j, 0)),
                  pl.BlockSpec((tf, D_MODEL), lambda i, j: (j, 0))] + [pl.BlockSpec(memory_space=pl.ANY)] * len(deps),
        out_specs=[pl.BlockSpec((tm, D_MODEL), lambda i, j: (i, 0)),
                   pl.BlockSpec((tm, D_MODEL), lambda i, j: (i, 0)),
                   pl.BlockSpec((tm, tf), lambda i, j: (i, j)),
                   pl.BlockSpec((tm, tf), lambda i, j: (i, j))],
        out_shape=[jax.ShapeDtypeStruct((SEQ, D_MODEL), F32), jax.ShapeDtypeStruct((SEQ, D_MODEL), BF16),
                   jax.ShapeDtypeStruct((SEQ, D_FF), BF16), jax.ShapeDtypeStruct((SEQ, D_FF), BF16)],
        scratch_shapes=[pltpu.VMEM((tm, D_MODEL), BF16), pltpu.VMEM((tm, D_MODEL), F32)],
        compiler_params=_cparams(("parallel", "arbitrary")),
    )(x, g, wgt, wut, wd, *deps)


def _ffn_bwd_act(dxo, x, g, a, b, wgt, wut, wd, name):
    tm, tf = FFN_TM, FFN_TF
    nj = D_FF // tf

    def body(dxo_ref, x_ref, g_ref, a_ref, b_ref, wg_ref, wu_ref, wd_ref,
             dx_ref, da_ref, db_ref, s_ref, df_ref, dg_ref, df_s, acc):
        i = pl.program_id(0)
        j = pl.program_id(1)

        @pl.when(j == 0)
        def _():
            df = (0.5 * dxo_ref[...]).astype(BF16)
            df_s[...] = df
            df_ref[...] = df
            acc[...] = jnp.zeros_like(acc)

        ds = _dot_nt(df_s[...], wd_ref[...])
        av = a_ref[...].astype(F32)
        bv = b_ref[...].astype(F32)
        sig = _sigmoid(av)
        sl = av * sig
        s_ref[...] = (sl * bv).astype(BF16)
        db = (ds * sl).astype(BF16)
        da = (ds * bv * (sig * (1.0 + av * (1.0 - sig)))).astype(BF16)
        da_ref[...] = da
        db_ref[...] = db
        acc[...] += _dot(da, wg_ref[...]) + _dot(db, wu_ref[...])

        @pl.when(j == nj - 1)
        def _():
            dx, dg = _rms_bwd(acc[...], x_ref[...], g_ref[...])
            dx_ref[...] = dxo_ref[...] + dx

            @pl.when(i == 0)
            def _():
                dg_ref[...] = dg

            @pl.when(i != 0)
            def _():
                dg_ref[...] += dg

    row = lambda i, j: (i, 0)
    col = lambda i, j: (j, 0)
    tile = lambda i, j: (i, j)
    return pl.pallas_call(
        body, name=name, grid=(SEQ // tm, nj),
        in_specs=[pl.BlockSpec((tm, D_MODEL), row), pl.BlockSpec((tm, D_MODEL), row),
                  pl.BlockSpec((1, D_MODEL), lambda i, j: (0, 0)),
                  pl.BlockSpec((tm, tf), tile), pl.BlockSpec((tm, tf), tile),
                  pl.BlockSpec((tf, D_MODEL), col), pl.BlockSpec((tf, D_MODEL), col), pl.BlockSpec((tf, D_MODEL), col)],
        out_specs=[pl.BlockSpec((tm, D_MODEL), row),
                   pl.BlockSpec((tm, tf), tile), pl.BlockSpec((tm, tf), tile), pl.BlockSpec((tm, tf), tile),
                   pl.BlockSpec((tm, D_MODEL), row),
                   pl.BlockSpec((1, D_MODEL), lambda i, j: (0, 0))],
        out_shape=[jax.ShapeDtypeStruct((SEQ, D_MODEL), F32),
                   jax.ShapeDtypeStruct((SEQ, D_FF), BF16), jax.ShapeDtypeStruct((SEQ, D_FF), BF16),
                   jax.ShapeDtypeStruct((SEQ, D_FF), BF16),
                   jax.ShapeDtypeStruct((SEQ, D_MODEL), BF16),
                   jax.ShapeDtypeStruct((1, D_MODEL), F32)],
        scratch_shapes=[pltpu.VMEM((tm, D_MODEL), BF16), pltpu.VMEM((tm, D_MODEL), F32)],
        compiler_params=_cparams(("arbitrary", "arbitrary")),
    )(dxo, x, g, a, b, wgt, wut, wd)


def _mm_tn(pairs, name, tmm=256):
    m = pairs[0][0].shape[1]
    n_pairs = len(pairs)

    def body(*refs):
        ins, outs = refs[:2 * n_pairs], refs[2 * n_pairs:]
        for p in range(n_pairs):
            outs[p][...] = _dot_tn(ins[2 * p][...], ins[2 * p + 1][...]).astype(BF16)

    in_specs, out_specs, out_shape, args = [], [], [], []
    for a, b in pairs:
        n = b.shape[1]
        in_specs += [pl.BlockSpec((SEQ, tmm), lambda i: (0, i)), pl.BlockSpec((SEQ, n), lambda i: (0, 0))]
        out_specs.append(pl.BlockSpec((tmm, n), lambda i: (i, 0)))
        out_shape.append(jax.ShapeDtypeStruct((m, n), BF16))
        args += [a, b]
    return pl.pallas_call(body, name=name, grid=(m // tmm,), in_specs=in_specs, out_specs=out_specs,
                          out_shape=out_shape, compiler_params=_cparams(("parallel",)))(*args)


MIX_TM = 256


def _mixin_fwd(x, g, wint):
    tm = MIX_TM

    def body(x_ref, g_ref, w_ref, h_ref, q_ref, k_ref, v_ref, u_ref):
        h = _rms_fwd(x_ref[...], g_ref[...]).astype(BF16)
        h_ref[...] = h
        proj = _dot_nt(h, w_ref[...])
        q_ref[...] = proj[:, :ATTN_WIDTH]
        k_ref[...] = proj[:, ATTN_WIDTH:ATTN_WIDTH + KV_WIDTH]
        v_ref[...] = proj[:, ATTN_WIDTH + KV_WIDTH:ATTN_WIDTH + 2 * KV_WIDTH]
        u_ref[...] = proj[:, ATTN_WIDTH + 2 * KV_WIDTH:]

    row = lambda i: (i, 0)
    return pl.pallas_call(
        body, name="mixin_fwd", grid=(SEQ // tm,),
        in_specs=[pl.BlockSpec((tm, D_MODEL), row), pl.BlockSpec((1, D_MODEL), lambda i: (0, 0)),
                  pl.BlockSpec((IN_WIDTH, D_MODEL), lambda i: (0, 0))],
        out_specs=[pl.BlockSpec((tm, D_MODEL), row), pl.BlockSpec((tm, ATTN_WIDTH), row),
                   pl.BlockSpec((tm, KV_WIDTH), row), pl.BlockSpec((tm, KV_WIDTH), row),
                   pl.BlockSpec((tm, SSM_WIDTH), row)],
        out_shape=[jax.ShapeDtypeStruct((SEQ, D_MODEL), BF16), jax.ShapeDtypeStruct((SEQ, ATTN_WIDTH), F32),
                   jax.ShapeDtypeStruct((SEQ, KV_WIDTH), F32), jax.ShapeDtypeStruct((SEQ, KV_WIDTH), F32),
                   jax.ShapeDtypeStruct((SEQ, SSM_WIDTH), F32)],
        compiler_params=_cparams(("parallel",)),
    )(x, g, wint)


def _mixin_bwd(dq, dk, dv, du, wint, x, g, dres):
    tm = MIX_TM

    def body(dq_ref, dk_ref, dv_ref, du_ref, w_ref, x_ref, g_ref, dres_ref, dx_ref, dp_ref, dg_ref):
        i = pl.program_id(0)
        dp = jnp.concatenate([dq_ref[...], dk_ref[...], dv_ref[...], du_ref[...]], axis=-1).astype(BF16)
        dp_ref[...] = dp
        dh = _dot(dp, w_ref[...])
        dx, dg = _rms_bwd(dh, x_ref[...], g_ref[...])
        dx_ref[...] = dres_ref[...] + dx

        @pl.when(i == 0)
        def _():
            dg_ref[...] = dg

        @pl.when(i != 0)
        def _():
            dg_ref[...] += dg

    row = lambda i: (i, 0)
    const = lambda i: (0, 0)
    return pl.pallas_call(
        body, name="mixin_bwd", grid=(SEQ // tm,),
        in_specs=[pl.BlockSpec((tm, ATTN_WIDTH), row), pl.BlockSpec((tm, KV_WIDTH), row),
                  pl.BlockSpec((tm, KV_WIDTH), row), pl.BlockSpec((tm, SSM_WIDTH), row),
                  pl.BlockSpec((IN_WIDTH, D_MODEL), const), pl.BlockSpec((tm, D_MODEL), row),
                  pl.BlockSpec((1, D_MODEL), const), pl.BlockSpec((tm, D_MODEL), row)],
        out_specs=[pl.BlockSpec((tm, D_MODEL), row), pl.BlockSpec((tm, IN_WIDTH), row),
                   pl.BlockSpec((1, D_MODEL), const)],
        out_shape=[jax.ShapeDtypeStruct((SEQ, D_MODEL), F32), jax.ShapeDtypeStruct((SEQ, IN_WIDTH), BF16),
                   jax.ShapeDtypeStruct((1, D_MODEL), F32)],
        compiler_params=_cparams(("arbitrary",)),
    )(dq, dk, dv, du, wint, x, g, dres)


N_QBLOCKS = SEQ // WINDOW
GROUP = ATTN_HEADS // KV_HEADS
SCALE = HEAD_DIM ** -0.5


def _alibi_slope(h):
    return 2.0 ** (-8.0 * (h + 1) / ATTN_HEADS)


def _window_masks(n):
    t_idx = lax.broadcasted_iota(jnp.int32, (WINDOW, 3 * WINDOW), 0)
    s_idx = lax.broadcasted_iota(jnp.int32, (WINDOW, 3 * WINDOW), 1)
    rel = s_idx - WINDOW - t_idx
    absrel = jnp.abs(rel)
    key_pos = n * WINDOW - WINDOW + s_idx
    valid = (absrel <= WINDOW) & (key_pos >= 0) & (key_pos < SEQ)
    return absrel.astype(F32), valid


def _head_probs(qh, kw, absrel, valid, slope, sink):
    s = _dot_nt(qh, kw) * SCALE
    s = jnp.where(valid, s - slope * absrel, NEG_INF)
    m = jnp.maximum(jnp.max(s, axis=-1, keepdims=True), sink)
    p = jnp.exp(s - m)
    ps = jnp.exp(sink - m)
    inv = 1.0 / (jnp.sum(p, axis=-1, keepdims=True) + ps)
    return p * inv, ps * inv


def _attn_fwd(q, kp, vp, sinks):
    def body(sk_ref, q_ref, kp_ref, vp_ref, o_ref):
        def blk(n, carry):
            r0 = pl.multiple_of(n * WINDOW, WINDOW)
            absrel, valid = _window_masks(n)
            for gi in range(KV_HEADS):
                kw = kp_ref[pl.ds(r0, 3 * WINDOW), gi * HEAD_DIM:(gi + 1) * HEAD_DIM].astype(BF16)
                vw = vp_ref[pl.ds(r0, 3 * WINDOW), gi * HEAD_DIM:(gi + 1) * HEAD_DIM].astype(BF16)
                for hh in range(GROUP):
                    h = gi * GROUP + hh
                    cols = slice(h * HEAD_DIM, (h + 1) * HEAD_DIM)
                    qh = q_ref[pl.ds(r0, WINDOW), cols].astype(BF16)
                    pr, _ = _head_probs(qh, kw, absrel, valid, _alibi_slope(h), sk_ref[0, h])
                    o_ref[pl.ds(r0, WINDOW), cols] = _dot(pr.astype(BF16), vw)
            return carry

        lax.fori_loop(0, N_QBLOCKS, blk, 0)

    vmem = pl.BlockSpec(memory_space=pltpu.VMEM)
    return pl.pallas_call(
        body, name="attn_fwd",
        in_specs=[pl.BlockSpec(memory_space=pltpu.SMEM), vmem, vmem, vmem], out_specs=vmem,
        out_shape=jax.ShapeDtypeStruct((SEQ, ATTN_WIDTH), F32),
        compiler_params=_cparams(),
    )(sinks, q, kp, vp)


def _attn_bwd(q, kp, vp, sinks, do):
    def body(sk_ref, q_ref, kp_ref, vp_ref, do_ref, dq_ref, dkp_ref, dvp_ref, dsk_ref, dsk_acc):
        dkp_ref[...] = jnp.zeros_like(dkp_ref)
        dvp_ref[...] = jnp.zeros_like(dvp_ref)
        dsk_acc[...] = jnp.zeros_like(dsk_acc)

        def blk(n, carry):
            r0 = pl.multiple_of(n * WINDOW, WINDOW)
            absrel, valid = _window_masks(n)
            for gi in range(KV_HEADS):
                gcols = slice(gi * HEAD_DIM, (gi + 1) * HEAD_DIM)
                kw = kp_ref[pl.ds(r0, 3 * WINDOW), gcols].astype(BF16)
                vw = vp_ref[pl.ds(r0, 3 * WINDOW), gcols].astype(BF16)
                dkw = jnp.zeros((3 * WINDOW, HEAD_DIM), F32)
                dvw = jnp.zeros((3 * WINDOW, HEAD_DIM), F32)
                for hh in range(GROUP):
                    h = gi * GROUP + hh
                    cols = slice(h * HEAD_DIM, (h + 1) * HEAD_DIM)
                    qh = q_ref[pl.ds(r0, WINDOW), cols].astype(BF16)
                    doh = do_ref[pl.ds(r0, WINDOW), cols].astype(BF16)
                    pr, psink = _head_probs(qh, kw, absrel, valid, _alibi_slope(h), sk_ref[0, h])
                    dp = _dot_nt(doh, vw)
                    delta = jnp.sum(pr * dp, axis=-1, keepdims=True)
                    ds = (pr * (dp - delta)).astype(BF16)
                    dsk_acc[:, h:h + 1] += -(psink * delta)
                    dq_ref[pl.ds(r0, WINDOW), cols] = _dot(ds, kw) * SCALE
                    dkw = dkw + _dot_tn(ds, qh) * SCALE
                    dvw = dvw + _dot_tn(pr.astype(BF16), doh)
                dkp_ref[pl.ds(r0, 3 * WINDOW), gcols] += dkw
                dvp_ref[pl.ds(r0, 3 * WINDOW), gcols] += dvw
            return carry

        lax.fori_loop(0, N_QBLOCKS, blk, 0)
        dsk_ref[...] = jnp.sum(dsk_acc[...], axis=0, keepdims=True)

    vmem = pl.BlockSpec(memory_space=pltpu.VMEM)
    return pl.pallas_call(
        body, name="attn_bwd",
        in_specs=[pl.BlockSpec(memory_space=pltpu.SMEM), vmem, vmem, vmem, vmem],
        out_specs=[vmem, vmem, vmem, vmem],
        out_shape=[jax.ShapeDtypeStruct((SEQ, ATTN_WIDTH), F32),
                   jax.ShapeDtypeStruct((SEQ + 2 * WINDOW, KV_WIDTH), F32),
                   jax.ShapeDtypeStruct((SEQ + 2 * WINDOW, KV_WIDTH), F32),
                   jax.ShapeDtypeStruct((1, ATTN_HEADS), F32)],
        scratch_shapes=[pltpu.VMEM((WINDOW, ATTN_HEADS), F32)],
        compiler_params=_cparams(),
    )(sinks, q, kp, vp, do)


def _ssm_prep(lam_re, lam_im, log_dt, bpad_re, bpad_im):
    nb = 2 * N_LANE_BLOCKS

    def body(lr_ref, li_ref, ldt_ref, br_ref, bi_ref, ar_ref, ai_ref, bbr_ref, bbi_ref):
        lr = jnp.minimum(lr_ref[...], LAMBDA_RE_MAX)
        li = li_ref[...]
        dt = jnp.exp(ldt_ref[...])
        mag = jnp.exp(lr * dt)
        ar = mag * jnp.cos(li * dt)
        ai = mag * jnp.sin(li * dt)
        den = lr * lr + li * li
        cr = ((ar - 1.0) * lr + ai * li) / den
        ci = (ai * lr - (ar - 1.0) * li) / den
        ar_ref[...] = ar
        ai_ref[...] = ai
        for i in range(nb):
            br = br_ref[i]
            bi = bi_ref[i]
            cri, cii = cr[i:i + 1, :], ci[i:i + 1, :]
            bbr_ref[i] = (cri * br - cii * bi).astype(BF16)
            bbi_ref[i] = (cri * bi + cii * br).astype(BF16)

    return pl.pallas_call(
        body, name="ssm_prep",
        out_shape=[jax.ShapeDtypeStruct((nb, LANES), F32), jax.ShapeDtypeStruct((nb, LANES), F32),
                   jax.ShapeDtypeStruct((nb, LANES, LANES), BF16), jax.ShapeDtypeStruct((nb, LANES, LANES), BF16)],
        compiler_params=_cparams(),
    )(lam_re, lam_im, log_dt, bpad_re, bpad_im)


def _ssm_prep_bwd(lam_re, lam_im, log_dt, bpad_re, bpad_im, dar, dai, dbbr, dbbi):
    nb = 2 * N_LANE_BLOCKS

    def body(lr_ref, li_ref, ldt_ref, br_ref, bi_ref, dar_ref, dai_ref, dbbr_ref, dbbi_ref,
             glr_ref, gli_ref, gdt_ref, gbr_ref, gbi_ref, gcr_s, gci_s):
        lam = lr_ref[...]
        lr = jnp.minimum(lam, LAMBDA_RE_MAX)
        li = li_ref[...]
        dt = jnp.exp(ldt_ref[...])
        mag = jnp.exp(lr * dt)
        cs = jnp.cos(li * dt)
        sn = jnp.sin(li * dt)
        ar = mag * cs
        ai = mag * sn
        den = lr * lr + li * li
        nr = (ar - 1.0) * lr + ai * li
        ni = ai * lr - (ar - 1.0) * li
        cr = nr / den
        ci = ni / den
        for i in range(nb):
            br = br_ref[i]
            bi = bi_ref[i]
            gbbr = dbbr_ref[i]
            gbbi = dbbi_ref[i]
            cri, cii = cr[i:i + 1, :], ci[i:i + 1, :]
            gcr_s[i:i + 1, :] = jnp.sum(gbbr * br + gbbi * bi, axis=0, keepdims=True)
            gci_s[i:i + 1, :] = jnp.sum(gbbi * br - gbbr * bi, axis=0, keepdims=True)
            gbr_ref[i] = cri * gbbr + cii * gbbi
            gbi_ref[i] = cri * gbbi - cii * gbbr
        g_cr = gcr_s[...]
        g_ci = gci_s[...]
        g_nr = g_cr / den
        g_ni = g_ci / den
        g_den = -(g_cr * nr + g_ci * ni) / (den * den)
        g_ar = dar_ref[...] + g_nr * lr - g_ni * li
        g_ai = dai_ref[...] + g_nr * li + g_ni * lr
        g_lr = g_nr * (ar - 1.0) + g_ni * ai + g_den * 2.0 * lr
        g_li = g_nr * ai - g_ni * (ar - 1.0) + g_den * 2.0 * li
        g_mag = g_ar * cs + g_ai * sn
        g_th = (g_ai * cs - g_ar * sn) * mag
        g_lr = g_lr + g_mag * mag * dt
        g_li = g_li + g_th * dt
        g_dt = g_mag * mag * lr + g_th * li
        glr_ref[...] = jnp.where(lam < LAMBDA_RE_MAX, g_lr, 0.0)
        gli_ref[...] = g_li
        gl = g_dt * dt
        half = LANES // 2
        gdt_ref[:, 0:1] = jnp.sum(gl[:, :half], axis=1, keepdims=True)
        gdt_ref[:, 1:2] = jnp.sum(gl[:, half:], axis=1, keepdims=True)

    return pl.pallas_call(
        body, name="ssm_prep_bwd",
        out_shape=[jax.ShapeDtypeStruct((nb, LANES), F32), jax.ShapeDtypeStruct((nb, LANES), F32),
                   jax.ShapeDtypeStruct((nb, 2), F32),
                   jax.ShapeDtypeStruct((nb, LANES, LANES), F32), jax.ShapeDtypeStruct((nb, LANES, LANES), F32)],
        scratch_shapes=[pltpu.VMEM((nb, LANES), F32), pltpu.VMEM((nb, LANES), F32)],
        compiler_params=_cparams(),
    )(lam_re, lam_im, log_dt, bpad_re, bpad_im, dar, dai, dbbr, dbbi)


def _cmul(ar, ai, br, bi):
    return ar * br - ai * bi, ar * bi + ai * br


def _interleave_rows(src_ref, dst_ref):
    def step(j, carry):
        dst_ref[pl.ds(pl.multiple_of(j * 8, 8), 8), :] = src_ref[pl.ds(j, 8, stride=SCAN_CHUNK), :]
        return carry
    lax.fori_loop(0, SCAN_CHUNK, step, 0, unroll=4)


def _deinterleave_rows(src_ref, dst_ref):
    def step(j, carry):
        dst_ref[pl.ds(j, 8, stride=SCAN_CHUNK), :] = src_ref[pl.ds(pl.multiple_of(j * 8, 8), 8), :]
        return carry
    lax.fori_loop(0, SCAN_CHUNK, step, 0, unroll=4)


def _scan_inplace(re_ref, im_ref, a_re, a_im, reverse):
    nq = len(a_re)
    ch = SCAN_CHUNK
    ab_re = [jnp.broadcast_to(a, (8, LANES)) for a in a_re]
    ab_im = [jnp.broadcast_to(a, (8, LANES)) for a in a_im]

    def rows(j):
        jj = (ch - 1 - j) if reverse else j
        return pl.ds(pl.multiple_of(jj * 8, 8), 8)

    def sweep(init, store):
        def step(j, st):
            out = []
            r = rows(j)
            for qi in range(nq):
                xr, xi = st[2 * qi], st[2 * qi + 1]
                pr, pi = _cmul(ab_re[qi], ab_im[qi], xr, xi)
                xr = pr + re_ref[qi, r, :]
                xi = pi + im_ref[qi, r, :]
                if store:
                    re_ref[qi, r, :] = xr
                    im_ref[qi, r, :] = xi
                out += [xr, xi]
            return tuple(out)
        return lax.fori_loop(0, ch, step, tuple(init), unroll=2)

    zeros = [jnp.zeros((8, LANES), F32)] * (2 * nq)
    finals = sweep(zeros, store=False)

    row_id = lax.broadcasted_iota(jnp.int32, (8, LANES), 0)
    carries = []
    for qi in range(nq):
        pr, pi = ab_re[qi], ab_im[qi]
        for _ in range(8):
            pr, pi = _cmul(pr, pi, pr, pi)
        fr, fi = finals[2 * qi], finals[2 * qi + 1]
        sr = jnp.zeros((8, LANES), F32)
        si = jnp.zeros((8, LANES), F32)
        for _ in range(7):
            tr, ti = _cmul(pr, pi, sr, si)
            tr, ti = tr + fr, ti + fi
            if reverse:
                sr = jnp.where(row_id == 7, 0.0, pltpu.roll(tr, 7, axis=0))
                si = jnp.where(row_id == 7, 0.0, pltpu.roll(ti, 7, axis=0))
            else:
                sr = jnp.where(row_id == 0, 0.0, pltpu.roll(tr, 1, axis=0))
                si = jnp.where(row_id == 0, 0.0, pltpu.roll(ti, 1, axis=0))
        carries += [sr, si]
    sweep(carries, store=True)


SSM_Q = 4


def _ssm_fwd(u, are, aim, bbr, bbi, cre, cim, dskip):
    nq = SSM_Q

    def body(u_ref, ar_ref, ai_ref, bbr_ref, bbi_ref, cr_ref, ci_ref, d_ref, y_ref, xr_ref, xi_ref,
             sre, sim, up, yp):
        _interleave_rows(u_ref, up)
        uf = up[...]
        ub = uf.astype(BF16)
        yp[...] = d_ref[...] * uf
        for d in range(2):
            for qi in range(nq):
                sre[qi] = _dot(ub, bbr_ref[d, qi])
                sim[qi] = _dot(ub, bbi_ref[d, qi])
            _scan_inplace(sre, sim, [ar_ref[d, qi] for qi in range(nq)], [ai_ref[d, qi] for qi in range(nq)],
                          reverse=(d == 1))
            for qi in range(nq):
                xrb = sre[qi].astype(BF16)
                xib = sim[qi].astype(BF16)
                xr_ref[d, qi] = xrb
                xi_ref[d, qi] = xib
                yp[...] += _dot(xrb, cr_ref[d, qi]) - _dot(xib, ci_ref[d, qi])
        _deinterleave_rows(yp, y_ref)

    blk4 = lambda k: (0, k, 0, 0)
    return pl.pallas_call(
        body, name="ssm_fwd", grid=(SSM_WIDTH // LANES,),
        in_specs=[pl.BlockSpec((SEQ, LANES), lambda k: (0, k)),
                  pl.BlockSpec((2, nq, 1, LANES), blk4), pl.BlockSpec((2, nq, 1, LANES), blk4),
                  pl.BlockSpec((2, nq, LANES, LANES), blk4), pl.BlockSpec((2, nq, LANES, LANES), blk4),
                  pl.BlockSpec((2, nq, LANES, LANES), blk4), pl.BlockSpec((2, nq, LANES, LANES), blk4),
                  pl.BlockSpec((1, LANES), lambda k: (0, k))],
        out_specs=[pl.BlockSpec((SEQ, LANES), lambda k: (0, k)),
                   pl.BlockSpec((2, nq, SEQ, LANES), blk4), pl.BlockSpec((2, nq, SEQ, LANES), blk4)],
        out_shape=[jax.ShapeDtypeStruct((SEQ, SSM_WIDTH), F32),
                   jax.ShapeDtypeStruct((2, N_LANE_BLOCKS, SEQ, LANES), BF16),
                   jax.ShapeDtypeStruct((2, N_LANE_BLOCKS, SEQ, LANES), BF16)],
        scratch_shapes=[pltpu.VMEM((nq, SEQ, LANES), F32), pltpu.VMEM((nq, SEQ, LANES), F32),
                        pltpu.VMEM((SEQ, LANES), F32), pltpu.VMEM((SEQ, LANES), F32)],
        compiler_params=_cparams(("parallel",)),
    )(u, are, aim, bbr, bbi, cre, cim, dskip)


def _ssm_bwd(dy, u, xr, xi, are, aim, bbr, bbi, cre, cim, dskip, after=None):
    nq = SSM_Q
    body_rows = SEQ - 8
    deps = [] if after is None else [after]

    def body(dy_ref, u_ref, xr_ref, xi_ref, ar_ref, ai_ref, bbr_ref, bbi_ref, cr_ref, ci_ref, d_ref, *rest):
        (du_ref, dd_ref, dcr_ref, dci_ref, dbr_ref, dbi_ref, dar_ref, dai_ref,
         sre, sim, up, dyp, dup) = rest[len(deps):]
        _interleave_rows(u_ref, up)
        _interleave_rows(dy_ref, dyp)
        dyf = dyp[...]
        uf = up[...]
        dyb = dyf.astype(BF16)
        ub = uf.astype(BF16)
        dd_ref[...] = jnp.sum(dyf * uf, axis=0, keepdims=True)
        dup[...] = d_ref[...] * dyf
        row8 = lax.broadcasted_iota(jnp.int32, (8, LANES), 0)
        for d in range(2):
            for qi in range(nq):
                sre[qi] = _dot_nt(dyb, cr_ref[d, qi])
                sim[qi] = -_dot_nt(dyb, ci_ref[d, qi])
                dcr_ref[d, qi] = _dot_tn(xr_ref[d, qi], dyb)
                dci_ref[d, qi] = -_dot_tn(xi_ref[d, qi], dyb)
            _scan_inplace(sre, sim, [ar_ref[d, qi] for qi in range(nq)], [-ai_ref[d, qi] for qi in range(nq)],
                          reverse=(d == 0))
            for qi in range(nq):
                gr = sre[qi]
                gi = sim[qi]
                xrf = xr_ref[d, qi].astype(F32)
                xif = xi_ref[d, qi].astype(F32)
                if d == 0:
                    g_main_r, g_main_i = gr[8:], gi[8:]
                    x_main_r, x_main_i = xrf[:body_rows], xif[:body_rows]
                    g_edge_r, g_edge_i = gr[:8], gi[:8]
                    x_edge_r = jnp.where(row8 == 0, 0.0, pltpu.roll(xrf[body_rows:], 1, axis=0))
                    x_edge_i = jnp.where(row8 == 0, 0.0, pltpu.roll(xif[body_rows:], 1, axis=0))
                else:
                    g_main_r, g_main_i = gr[:body_rows], gi[:body_rows]
                    x_main_r, x_main_i = xrf[8:], xif[8:]
                    g_edge_r, g_edge_i = gr[body_rows:], gi[body_rows:]
                    x_edge_r = jnp.where(row8 == 7, 0.0, pltpu.roll(xrf[:8], 7, axis=0))
                    x_edge_i = jnp.where(row8 == 7, 0.0, pltpu.roll(xif[:8], 7, axis=0))
                dar_ref[d, qi] = (jnp.sum(g_main_r * x_main_r + g_main_i * x_main_i, axis=0, keepdims=True)
                                  + jnp.sum(g_edge_r * x_edge_r + g_edge_i * x_edge_i, axis=0, keepdims=True))
                dai_ref[d, qi] = (jnp.sum(g_main_i * x_main_r - g_main_r * x_main_i, axis=0, keepdims=True)
                                  + jnp.sum(g_edge_i * x_edge_r - g_edge_r * x_edge_i, axis=0, keepdims=True))
                grb = gr.astype(BF16)
                gib = gi.astype(BF16)
                dup[...] += _dot_nt(grb, bbr_ref[d, qi]) + _dot_nt(gib, bbi_ref[d, qi])
                dbr_ref[d, qi] = _dot_tn(ub, grb)
                dbi_ref[d, qi] = _dot_tn(ub, gib)
        _deinterleave_rows(dup, du_ref)

    blk4 = lambda k: (0, k, 0, 0)
    col = lambda k: (0, k)
    w_spec = pl.BlockSpec((2, nq, LANES, LANES), blk4)
    a_spec = pl.BlockSpec((2, nq, 1, LANES), blk4)
    x_spec = pl.BlockSpec((2, nq, SEQ, LANES), blk4)
    w_shape = jax.ShapeDtypeStruct((2, N_LANE_BLOCKS, LANES, LANES), F32)
    a_shape = jax.ShapeDtypeStruct((2, N_LANE_BLOCKS, 1, LANES), F32)
    return pl.pallas_call(
        body, name="ssm_bwd", grid=(SSM_WIDTH // LANES,),
        in_specs=[pl.BlockSpec((SEQ, LANES), col), pl.BlockSpec((SEQ, LANES), col), x_spec, x_spec,
                  a_spec, a_spec, w_spec, w_spec, w_spec, w_spec, pl.BlockSpec((1, LANES), col)]
        + [pl.BlockSpec(memory_space=pl.ANY)] * len(deps),
        out_specs=[pl.BlockSpec((SEQ, LANES), col), pl.BlockSpec((1, LANES), col),
                   w_spec, w_spec, w_spec, w_spec, a_spec, a_spec],
        out_shape=[jax.ShapeDtypeStruct((SEQ, SSM_WIDTH), F32), jax.ShapeDtypeStruct((1, SSM_WIDTH), F32),
                   w_shape, w_shape, w_shape, w_shape, a_shape, a_shape],
        scratch_shapes=[pltpu.VMEM((nq, SEQ, LANES), F32), pltpu.VMEM((nq, SEQ, LANES), F32),
                        pltpu.VMEM((SEQ, LANES), F32), pltpu.VMEM((SEQ, LANES), F32), pltpu.VMEM((SEQ, LANES), F32)],
        compiler_params=_cparams(("parallel",)),
    )(dy, u, xr, xi, are, aim, bbr, bbi, cre, cim, dskip, *deps)


GELU_C = 0.7978845608028654
GELU_K = 0.044715


def _gelu(y):
    return 0.5 * y * (1.0 + jnp.tanh(GELU_C * (y + GELU_K * y * y * y)))


def _gelu_grad(y):
    t = jnp.tanh(GELU_C * (y + GELU_K * y * y * y))
    return 0.5 * (1.0 + t) + 0.5 * y * (1.0 - t * t) * GELU_C * (1.0 + 3.0 * GELU_K * y * y)


def _mixout_fwd(o, y, glu_w, glu_b, gan, gsn, wout, x1):
    tm = MIX_TM

    def body(o_ref, y_ref, gw_ref, gb_ref, gan_ref, gsn_ref, w_ref, x1_ref, x2_ref, mx_ref):
        yg = _gelu(y_ref[...])
        z = _dot(yg.astype(BF16), gw_ref[...]) + gb_ref[...]
        so = yg * _sigmoid(z)
        na = _rms_fwd(o_ref[...], gan_ref[...])
        ns = _rms_fwd(so, gsn_ref[...])
        mixed = jnp.concatenate([na, ns], axis=-1).astype(BF16)
        mx_ref[...] = mixed
        x2_ref[...] = x1_ref[...] + _dot(mixed, w_ref[...])

    row = lambda i: (i, 0)
    const = lambda i: (0, 0)
    return pl.pallas_call(
        body, name="mixout_fwd", grid=(SEQ // tm,),
        in_specs=[pl.BlockSpec((tm, ATTN_WIDTH), row), pl.BlockSpec((tm, SSM_WIDTH), row),
                  pl.BlockSpec((SSM_WIDTH, SSM_WIDTH), const), pl.BlockSpec((1, SSM_WIDTH), const),
                  pl.BlockSpec((1, ATTN_WIDTH), const), pl.BlockSpec((1, SSM_WIDTH), const),
                  pl.BlockSpec((D_MODEL, D_MODEL), const), pl.BlockSpec((tm, D_MODEL), row)],
        out_specs=[pl.BlockSpec((tm, D_MODEL), row), pl.BlockSpec((tm, D_MODEL), row)],
        out_shape=[jax.ShapeDtypeStruct((SEQ, D_MODEL), F32), jax.ShapeDtypeStruct((SEQ, D_MODEL), BF16)],
        compiler_params=_cparams(("parallel",)),
    )(o, y, glu_w, glu_b, gan, gsn, wout, x1)


def _mixout_bwd(dx2, o, y, glu_w, glu_b, gan, gsn, wout):
    tm = MIX_TM

    def body(dx2_ref, o_ref, y_ref, gw_ref, gb_ref, gan_ref, gsn_ref, w_ref,
             do_ref, dy_ref, dz_ref, yg_ref, dxb_ref, dgan_ref, dgsn_ref, dgb_ref):
        i = pl.program_id(0)
        dxb = dx2_ref[...].astype(BF16)
        dxb_ref[...] = dxb
        dmixed = _dot_nt(dxb, w_ref[...])
        do, dgan = _rms_bwd(dmixed[:, :ATTN_WIDTH], o_ref[...], gan_ref[...])
        do_ref[...] = do
        yv = y_ref[...]
        yg = _gelu(yv)
        ygb = yg.astype(BF16)
        yg_ref[...] = ygb
        sg = _sigmoid(_dot(ygb, gw_ref[...]) + gb_ref[...])
        dso, dgsn = _rms_bwd(dmixed[:, ATTN_WIDTH:], yg * sg, gsn_ref[...])
        dz = dso * yg * sg * (1.0 - sg)
        dzb = dz.astype(BF16)
        dz_ref[...] = dzb
        dyg = dso * sg + _dot_nt(dzb, gw_ref[...])
        dy_ref[...] = dyg * _gelu_grad(yv)
        dgb = jnp.sum(dz, axis=0, keepdims=True)

        @pl.when(i == 0)
        def _():
            dgan_ref[...] = dgan
            dgsn_ref[...] = dgsn
            dgb_ref[...] = dgb

        @pl.when(i != 0)
        def _():
            dgan_ref[...] += dgan
            dgsn_ref[...] += dgsn
            dgb_ref[...] += dgb

    row = lambda i: (i, 0)
    const = lambda i: (0, 0)
    return pl.pallas_call(
        body, name="mixout_bwd", grid=(SEQ // tm,),
        in_specs=[pl.BlockSpec((tm, D_MODEL), row), pl.BlockSpec((tm, ATTN_WIDTH), row),
                  pl.BlockSpec((tm, SSM_WIDTH), row),
                  pl.BlockSpec((SSM_WIDTH, SSM_WIDTH), const), pl.BlockSpec((1, SSM_WIDTH), const),
                  pl.BlockSpec((1, ATTN_WIDTH), const), pl.BlockSpec((1, SSM_WIDTH), const),
                  pl.BlockSpec((D_MODEL, D_MODEL), const)],
        out_specs=[pl.BlockSpec((tm, ATTN_WIDTH), row), pl.BlockSpec((tm, SSM_WIDTH), row),
                   pl.BlockSpec((tm, SSM_WIDTH), row), pl.BlockSpec((tm, SSM_WIDTH), row),
                   pl.BlockSpec((tm, D_MODEL), row),
                   pl.BlockSpec((1, ATTN_WIDTH), const), pl.BlockSpec((1, SSM_WIDTH), const),
                   pl.BlockSpec((1, SSM_WIDTH), const)],
        out_shape=[jax.ShapeDtypeStruct((SEQ, ATTN_WIDTH), F32), jax.ShapeDtypeStruct((SEQ, SSM_WIDTH), F32),
                   jax.ShapeDtypeStruct((SEQ, SSM_WIDTH), BF16), jax.ShapeDtypeStruct((SEQ, SSM_WIDTH), BF16),
                   jax.ShapeDtypeStruct((SEQ, D_MODEL), BF16),
                   jax.ShapeDtypeStruct((1, ATTN_WIDTH), F32), jax.ShapeDtypeStruct((1, SSM_WIDTH), F32),
                   jax.ShapeDtypeStruct((1, SSM_WIDTH), F32)],
        compiler_params=_cparams(("arbitrary",)),
    )(dx2, o, y, glu_w, glu_b, gan, gsn, wout)


def _loss_head(x, g, target):
    tm = MIX_TM

    def body(x_ref, g_ref, t_ref, loss_ref, dx_ref, dg_ref):
        i = pl.program_id(0)
        xv = x_ref[...]
        gv = g_ref[...]
        err = _rms_fwd(xv, gv) - t_ref[...]
        part = jnp.broadcast_to(0.5 * jnp.sum(err * err) / D_MODEL, (1, LANES))
        dx, dg = _rms_bwd(err * (1.0 / D_MODEL), xv, gv)
        dx_ref[...] = dx

        @pl.when(i == 0)
        def _():
            loss_ref[...] = part
            dg_ref[...] = dg

        @pl.when(i != 0)
        def _():
            loss_ref[...] += part
            dg_ref[...] += dg

    row = lambda i: (i, 0)
    const = lambda i: (0, 0)
    return pl.pallas_call(
        body, name="loss_head", grid=(SEQ // tm,),
        in_specs=[pl.BlockSpec((tm, D_MODEL), row), pl.BlockSpec((1, D_MODEL), const),
                  pl.BlockSpec((tm, D_MODEL), row)],
        out_specs=[pl.BlockSpec((1, LANES), const), pl.BlockSpec((tm, D_MODEL), row),
                   pl.BlockSpec((1, D_MODEL), const)],
        out_shape=[jax.ShapeDtypeStruct((1, LANES), F32), jax.ShapeDtypeStruct((SEQ, D_MODEL), F32),
                   jax.ShapeDtypeStruct((1, D_MODEL), F32)],
        compiler_params=_cparams(("arbitrary",)),
    )(x, g, target)


def _embed_blocks(b):
    b5 = b.reshape(2, 16, 2, 64, 16).transpose(0, 1, 2, 4, 3)
    eye2 = jnp.eye(2, dtype=b.dtype)
    m = (b5[:, :, :, :, None, :] * eye2[None, None, :, None, :, None]).reshape(2, 16, 32, LANES)
    sel = (jnp.arange(16)[:, None] % 4 == jnp.arange(4)[None, :]).astype(b.dtype)
    return (m[:, :, None, :, :] * sel[None, :, :, None, None]).reshape(2, 16, LANES, LANES)


def _extract_blocks(m):
    sel = (jnp.arange(16)[:, None] % 4 == jnp.arange(4)[None, :]).astype(m.dtype)
    m = jnp.sum(m.reshape(2, 16, 4, 32, LANES) * sel[None, :, :, None, None], axis=2)
    eye2 = jnp.eye(2, dtype=m.dtype)
    m = jnp.sum(m.reshape(2, 16, 2, 16, 2, 64) * eye2[None, None, :, None, :, None], axis=4)
    return m.transpose(0, 1, 2, 4, 3).reshape(2, 32, 64, 16)


def _local_step(x, target, w, p, late_weights, early_grads, after=None):
    x1, h1, a1, b1 = _ffn_fwd(x, p["norm_ffn1"], w["wgt1"], w["wut1"], w["wd1"], "ffn1_fwd", after=after)
    h2, q, k, v, u = _mixin_fwd(x1, p["norm_mix"], w["wint"])
    kp = jnp.pad(k, ((WINDOW, WINDOW), (0, 0)))
    vp = jnp.pad(v, ((WINDOW, WINDOW), (0, 0)))
    o = _attn_fwd(q, kp, vp, p["attn_sinks"])

    lam_re = p["ssm_lambda_re"].reshape(2 * N_LANE_BLOCKS, LANES)
    lam_im = p["ssm_lambda_im"].reshape(2 * N_LANE_BLOCKS, LANES)
    log_dt = jnp.repeat(p["ssm_log_dt"].reshape(2, 32), 64, axis=-1).reshape(2 * N_LANE_BLOCKS, LANES)
    bpad_re = _embed_blocks(p["ssm_b_re"].reshape(2, 32, 64, 16)).reshape(2 * N_LANE_BLOCKS, LANES, LANES)
    bpad_im = _embed_blocks(p["ssm_b_im"].reshape(2, 32, 64, 16)).reshape(2 * N_LANE_BLOCKS, LANES, LANES)
    c_t = lambda c: _embed_blocks(c.reshape(2, 32, 16, 64).transpose(0, 1, 3, 2)).transpose(0, 1, 3, 2)
    cre = c_t(p["ssm_c_re"]).astype(BF16)
    cim = c_t(p["ssm_c_im"]).astype(BF16)
    a_re, a_im, bbr, bbi = _ssm_prep(lam_re, lam_im, log_dt, bpad_re, bpad_im)
    shape_a = (2, N_LANE_BLOCKS, 1, LANES)
    shape_w = (2, N_LANE_BLOCKS, LANES, LANES)
    a_re4, a_im4 = a_re.reshape(shape_a), a_im.reshape(shape_a)
    bbr4, bbi4 = bbr.reshape(shape_w), bbi.reshape(shape_w)
    dskip = p["ssm_d"].reshape(1, SSM_WIDTH)
    y, xr, xi = _ssm_fwd(u, a_re4, a_im4, bbr4, bbi4, cre, cim, dskip)

    w2 = late_weights(y)
    x2, mixed = _mixout_fwd(o, y, w2["glu"], p["ssm_glu_b"], p["attn_out_norm"], p["ssm_out_norm"], w2["wout"], x1)
    x3, h3, a3, b3 = _ffn_fwd(x2, p["norm_ffn2"], w2["wgt2"], w2["wut2"], w2["wd2"], "ffn2_fwd")

    loss, dx3, d_final = _loss_head(x3, p["final_norm"], target)
    dx2, da3, db3, s3, df3, d_n2 = _ffn_bwd_act(dx3, x2, p["norm_ffn2"], a3, b3, w2["wgt2"], w2["wut2"], w2["wd2"],
                                                "ffn2_bwd_act")
    g_wgt2, g_wut2, g_wd2 = _mm_tn([(da3, h3), (db3, h3), (s3, df3)], "ffn2_bwd_w")

    do, dy, dz, ygb, dx2b, d_gan, d_gsn, d_glub = _mixout_bwd(
        dx2, o, y, w2["glu"], p["ssm_glu_b"], p["attn_out_norm"], p["ssm_out_norm"], w2["wout"])
    (g_wout,) = _mm_tn([(mixed, dx2b)], "wout_bwd_w")
    (g_glu,) = _mm_tn([(ygb, dz)], "glu_bwd_w")
    sent = early_grads(dict(glu=g_glu, wout=g_wout, wgt2=g_wgt2, wut2=g_wut2, wd2=g_wd2))

    du, d_dskip, dcre, dcim, dbbr, dbbi, dar, dai = _ssm_bwd(dy, u, xr, xi, a_re4, a_im4, bbr4, bbi4, cre, cim, dskip,
                                                             after=sent)
    nb = 2 * N_LANE_BLOCKS
    g_lre, g_lim, g_ldt, g_bpr, g_bpi = _ssm_prep_bwd(
        lam_re, lam_im, log_dt, bpad_re, bpad_im, dar.reshape(nb, LANES), dai.reshape(nb, LANES),
        dbbr.reshape(nb, LANES, LANES), dbbi.reshape(nb, LANES, LANES))

    dq, dkp, dvp, d_sinks = _attn_bwd(q, kp, vp, p["attn_sinks"], do)
    dk = dkp[WINDOW:WINDOW + SEQ]
    dv = dvp[WINDOW:WINDOW + SEQ]
    dx1, dproj, d_nmix = _mixin_bwd(dq, dk, dv, du, w["wint"], x1, p["norm_mix"], dx2)
    (g_wint,) = _mm_tn([(dproj, h2)], "win_bwd_w")

    dx0, da1, db1, s1, df1, d_n1 = _ffn_bwd_act(dx1, x, p["norm_ffn1"], a1, b1, w["wgt1"], w["wut1"], w["wd1"],
                                                "ffn1_bwd_act")
    g_wgt1, g_wut1, g_wd1 = _mm_tn([(da1, h1), (db1, h1), (s1, df1)], "ffn1_bwd_w")

    c_back = lambda g: _extract_blocks(g.transpose(0, 1, 3, 2)).transpose(0, 1, 3, 2)
    big = dict(wgt1=g_wgt1, wut1=g_wut1, wd1=g_wd1, wint=g_wint)
    small = dict(
        norm_ffn1=d_n1, norm_mix=d_nmix, attn_sinks=d_sinks,
        ssm_lambda_re=g_lre.reshape(64, 64), ssm_lambda_im=g_lim.reshape(64, 64),
        ssm_log_dt=g_ldt.reshape(2, 32),
        ssm_b_re=_extract_blocks(g_bpr.reshape(shape_w)).reshape(4096, 16),
        ssm_b_im=_extract_blocks(g_bpi.reshape(shape_w)).reshape(4096, 16),
        ssm_c_re=c_back(dcre).reshape(1024, 64), ssm_c_im=c_back(dcim).reshape(1024, 64),
        ssm_d=d_dskip.reshape(32, 16), ssm_glu_b=d_glub, attn_out_norm=d_gan, ssm_out_norm=d_gsn,
        norm_ffn2=d_n2, final_norm=d_final)
    return loss, dx0, big, small


BIG = dict(
    wgt1=("ffn1_w_gate", 352, 1024, True), wut1=("ffn1_w_up", 352, 1024, True), wd1=("ffn1_w_down", 352, 1024, False),
    wint=("w_in", 160, 1024, True), glu=("ssm_glu_w", 64, 512, False), wout=("w_out", 128, 1024, False),
    wgt2=("ffn2_w_gate", 352, 1024, True), wut2=("ffn2_w_up", 352, 1024, True), wd2=("ffn2_w_down", 352, 1024, False))

SMALL = dict(
    norm_ffn1=(1, 1024), norm_mix=(1, 1024), attn_sinks=(1, 8), ssm_lambda_re=(64, 64), ssm_lambda_im=(64, 64),
    ssm_log_dt=(2, 32), ssm_b_re=(4096, 16), ssm_b_im=(4096, 16), ssm_c_re=(1024, 64), ssm_c_im=(1024, 64),
    ssm_d=(32, 16), ssm_glu_b=(1, 512), attn_out_norm=(1, 512), ssm_out_norm=(1, 512), norm_ffn2=(1, 1024),
    final_norm=(1, 1024))
SMALL_TOTAL = sum(r * c for r, c in SMALL.values())
SMALL_ROWS = -(-SMALL_TOTAL // (8 * LANES)) * 8


def _pad_to(n, mult):
    return -(-n // mult) * mult


def _transpose_2d(x, rows_out, cols_out):
    r_in, c_in = x.shape
    rp, cp = _pad_to(r_in, LANES), _pad_to(c_in, LANES)
    if cp != c_in:
        x = jnp.concatenate([x, jnp.zeros((r_in, cp - c_in), x.dtype)], axis=1)
    if rp != r_in:
        x = jnp.concatenate([x, jnp.zeros((rp - r_in, cp), x.dtype)], axis=0)
    return x.T[:rows_out, :cols_out]


def _cast_shards(shards):
    names = list(BIG)

    def body(*refs):
        ins, outs = refs[:len(names)], refs[len(names):]
        for idx, n in enumerate(names):
            _, rows, cols, transposed = BIG[n]
            v = ins[idx][...]
            if transposed:
                v = _transpose_2d(v, rows, cols)
            outs[idx][...] = v.astype(BF16)

    return pl.pallas_call(
        body, name="cast_shards",
        out_shape=[jax.ShapeDtypeStruct((BIG[n][1], BIG[n][2]), BF16) for n in names],
        compiler_params=_cparams(),
    )(*[shards[n] for n in names])


def _peer(x, y, c, r):
    px = 1 - x if r & 4 else x
    py = 1 - y if r & 2 else y
    pc = 1 - c if r & 1 else c
    return px, py, pc


FIRST_GROUP = ("wgt1", "wut1", "wd1", "wint")
LATE_GROUP = ("glu", "wout", "wgt2", "wut2", "wd2")
N_PEERS = N_DEV - 1
ANY_SPEC = pl.BlockSpec(memory_space=pl.ANY)
HBM_SPEC = pl.BlockSpec(memory_space=pltpu.HBM)
SEM_SPEC = pl.BlockSpec(memory_space=pltpu.SEMAPHORE)
DATAFLOW_EFFECT = pltpu.SideEffectType.DATAFLOW_SIDE_EFFECTING


def _mesh_pos():
    x, y, c = lax.axis_index("x"), lax.axis_index("y"), lax.axis_index("c")
    return x, y, c, 4 * x + 2 * y + c


def _gather_first(first, late):
    nf, nl = len(first), len(late)

    def body(*refs):
        f_in, l_in = refs[:nf], refs[nf:nf + nl]
        f_out, l_out = refs[nf + nl:2 * nf + nl], refs[2 * nf + nl:2 * (nf + nl)]
        send_sems, recv_sems, local_sems = refs[2 * (nf + nl):]
        x, y, c, me = _mesh_pos()
        copies = []
        for k in range(nf):
            for r in range(1, N_DEV):
                cp = pltpu.make_async_remote_copy(
                    src_ref=f_in[k], dst_ref=f_out[k].at[me], send_sem=send_sems.at[k, r - 1],
                    recv_sem=recv_sems.at[k, r - 1], device_id=_peer(x, y, c, r), device_id_type=MESH_ID)
                cp.start()
                copies.append(cp)
        for k in range(nf + nl):
            src, dst = (f_in[k], f_out[k]) if k < nf else (l_in[k - nf], l_out[k - nf])
            mine = pltpu.make_async_copy(src, dst.at[me], local_sems.at[k])
            mine.start()
            copies.append(mine)
        for cp in copies:
            cp.wait()

    return pl.pallas_call(
        body, name="gather_first",
        in_specs=[ANY_SPEC] * (nf + nl), out_specs=[ANY_SPEC] * (nf + nl),
        out_shape=[jax.ShapeDtypeStruct((N_DEV,) + s.shape, s.dtype) for s in list(first) + list(late)],
        scratch_shapes=[pltpu.SemaphoreType.DMA((nf, N_PEERS)), pltpu.SemaphoreType.DMA((nf, N_PEERS)),
                        pltpu.SemaphoreType.DMA((nf + nl,))],
        compiler_params=pltpu.CompilerParams(has_side_effects=True),
    )(*first, *late)


def _split_copy(src_refs, land_refs, send_sems, recv_sems, k, r, pos, scatter, receiving):
    x, y, c, me = pos
    px, py, pc = _peer(x, y, c, r)
    peer_idx = 4 * px + 2 * py + pc
    if scatter:
        src, dst = src_refs[k].at[peer_idx], land_refs[k].at[r - 1]
    else:
        src, dst = src_refs[k], land_refs[k].at[peer_idx if receiving else me]
    return pltpu.make_async_remote_copy(
        src_ref=src, dst_ref=dst, send_sem=send_sems.at[k * N_PEERS + r - 1],
        recv_sem=recv_sems.at[k * N_PEERS + r - 1], device_id=(px, py, pc), device_id_type=MESH_ID)


def _split_start(name, srcs, lands, scatter):
    n = len(srcs)

    def body(*refs):
        src_refs, land_refs = refs[:n], refs[n:2 * n]
        send_sems, recv_sems = refs[2 * n], refs[2 * n + 1]
        token = refs[-1]
        pos = _mesh_pos()
        for k in range(n):
            for r in range(1, N_DEV):
                _split_copy(src_refs, land_refs, send_sems, recv_sems, k, r, pos, scatter, False).start()
        token[...] = jnp.zeros_like(token)

    thru = [pltpu.HBM(a.shape, a.dtype) for a in list(srcs) + list(lands)]
    outs = pl.pallas_call(
        body, name=name,
        in_specs=[HBM_SPEC] * (2 * n),
        out_specs=[SEM_SPEC, SEM_SPEC] + [HBM_SPEC] * (2 * n) + [pl.BlockSpec(memory_space=pltpu.VMEM)],
        out_shape=[pltpu.SemaphoreType.DMA((n * N_PEERS,)), pltpu.SemaphoreType.DMA((n * N_PEERS,))] + thru
        + [jax.ShapeDtypeStruct((8, LANES), F32)],
        input_output_aliases={i: 2 + i for i in range(2 * n)},
        compiler_params=pltpu.CompilerParams(has_side_effects=DATAFLOW_EFFECT),
    )(*[pltpu.with_memory_space_constraint(a, pltpu.HBM) for a in list(srcs) + list(lands)])
    return outs[0], outs[1], outs[2:2 + n], outs[2 + n:2 + 2 * n], outs[-1]


def _split_wait(name, send_sems, recv_sems, srcs, lands, scatter, after):
    n = len(srcs)

    def body(*refs):
        src_refs, land_refs = refs[:n], refs[n:2 * n]
        send, recv = refs[2 * n], refs[2 * n + 1]
        pos = _mesh_pos()
        for k in range(n):
            for r in range(1, N_DEV):
                cp = _split_copy(src_refs, land_refs, send, recv, k, r, pos, scatter, True)
                cp.wait_send()
                cp.wait_recv()

    thru = [pltpu.HBM(a.shape, a.dtype) for a in list(srcs) + list(lands)]
    outs = pl.pallas_call(
        body, name=name,
        in_specs=[HBM_SPEC] * (2 * n) + [SEM_SPEC, SEM_SPEC, ANY_SPEC],
        out_specs=[HBM_SPEC] * (2 * n), out_shape=thru,
        input_output_aliases={i: i for i in range(2 * n)},
        compiler_params=pltpu.CompilerParams(has_side_effects=DATAFLOW_EFFECT),
    )(*srcs, *lands, send_sems, recv_sems, after)
    return outs[:n], outs[n:]


def _exchange_last(grads, small_packed):
    ng = len(grads)

    def body(*refs):
        g_in, s_in = refs[:ng], refs[ng]
        g_out, s_out = refs[ng + 1:2 * ng + 1], refs[2 * ng + 1]
        send_sems, recv_sems, local_sem = refs[2 * ng + 2:]
        x, y, c, me = _mesh_pos()
        copies = []
        for k in range(ng + 1):
            for r in range(1, N_DEV):
                px, py, pc = _peer(x, y, c, r)
                if k < ng:
                    src, dst = g_in[k].at[4 * px + 2 * py + pc], g_out[k].at[r - 1]
                else:
                    src, dst = s_in, s_out.at[me]
                cp = pltpu.make_async_remote_copy(
                    src_ref=src, dst_ref=dst, send_sem=send_sems.at[k, r - 1], recv_sem=recv_sems.at[k, r - 1],
                    device_id=(px, py, pc), device_id_type=MESH_ID)
                cp.start()
                copies.append(cp)
        mine = pltpu.make_async_copy(s_in, s_out.at[me], local_sem)
        mine.start()
        copies.append(mine)
        for cp in copies:
            cp.wait()

    out_shape = [jax.ShapeDtypeStruct((N_PEERS,) + g.shape[1:], g.dtype) for g in grads]
    out_shape.append(jax.ShapeDtypeStruct((N_DEV, SMALL_ROWS, LANES), F32))
    return pl.pallas_call(
        body, name="exchange_last",
        in_specs=[ANY_SPEC] * (ng + 1), out_specs=[ANY_SPEC] * (ng + 1), out_shape=out_shape,
        scratch_shapes=[pltpu.SemaphoreType.DMA((ng + 1, N_PEERS)), pltpu.SemaphoreType.DMA((ng + 1, N_PEERS)),
                        pltpu.SemaphoreType.DMA(())],
        compiler_params=pltpu.CompilerParams(has_side_effects=True),
    )(*grads, small_packed)


def _adamw_math(w, g, m, v):
    m2 = ADAM_B1 * m + (1.0 - ADAM_B1) * g
    v2 = ADAM_B2 * v + (1.0 - ADAM_B2) * (g * g)
    m_hat = m2 / (1.0 - ADAM_B1 ** ADAM_STEP)
    v_hat = v2 / (1.0 - ADAM_B2 ** ADAM_STEP)
    delta = -ADAM_LR * (m_hat / (jnp.sqrt(v_hat) + ADAM_EPS) + ADAM_WD * w)
    return delta, m2, v2


def _adamw_big(own, parts, w, m, v, name, transposed):
    shape = w.shape

    def body(own_ref, p_ref, w_ref, m_ref, v_ref, g_ref, d_ref, m2_ref, v2_ref, own_s, sem):
        me = _mesh_pos()[3]
        cp = pltpu.make_async_copy(own_ref.at[me], own_s, sem)
        cp.start()
        cp.wait()
        g = own_s[...].astype(F32)
        for i in range(N_PEERS):
            g = g + p_ref[i].astype(F32)
        if transposed:
            g = _transpose_2d(g, shape[0], shape[1])
        delta, m2, v2 = _adamw_math(w_ref[...], g, m_ref[...], v_ref[...])
        g_ref[...] = g
        d_ref[...] = delta
        m2_ref[...] = m2
        v2_ref[...] = v2

    vmem = pl.BlockSpec(memory_space=pltpu.VMEM)
    return pl.pallas_call(
        body, name=name, in_specs=[ANY_SPEC, vmem, vmem, vmem, vmem], out_specs=[vmem] * 4,
        out_shape=[jax.ShapeDtypeStruct(shape, F32)] * 4,
        scratch_shapes=[pltpu.VMEM(own.shape[1:], own.dtype), pltpu.SemaphoreType.DMA(())],
        compiler_params=_cparams(),
    )(own, parts, w, m, v)


def _adamw_small(parts, w, m, v):
    def body(p_ref, w_ref, m_ref, v_ref, g_ref, d_ref, m2_ref, v2_ref):
        g = p_ref[0]
        for i in range(1, N_DEV):
            g = g + p_ref[i]
        delta, m2, v2 = _adamw_math(w_ref[...], g, m_ref[...], v_ref[...])
        g_ref[...] = g
        d_ref[...] = delta
        m2_ref[...] = m2
        v2_ref[...] = v2

    return pl.pallas_call(
        body, name="adamw_small", out_shape=[jax.ShapeDtypeStruct((SMALL_ROWS, LANES), F32)] * 4,
        compiler_params=_cparams(),
    )(parts, w, m, v)


def _pack_small(d):
    flat = jnp.concatenate([d[n].reshape(-1) for n in SMALL])
    flat = jnp.pad(flat, (0, SMALL_ROWS * LANES - SMALL_TOTAL))
    return flat.reshape(SMALL_ROWS, LANES)


def _unpack_small(packed, shapes):
    flat = packed.reshape(-1)
    out, off = {}, 0
    for n, (r, c) in SMALL.items():
        out[n] = flat[off:off + r * c].reshape(shapes[n])
        off += r * c
    return out


WEIGHT_NAMES = ['norm_ffn1', 'ffn1_w_gate', 'ffn1_w_up', 'ffn1_w_down', 'norm_mix', 'w_in', 'attn_sinks',
                'ssm_lambda_re', 'ssm_lambda_im', 'ssm_log_dt', 'ssm_b_re', 'ssm_b_im', 'ssm_c_re', 'ssm_c_im',
                'ssm_d', 'ssm_glu_w', 'ssm_glu_b', 'attn_out_norm', 'ssm_out_norm', 'w_out', 'norm_ffn2',
                'ffn2_w_gate', 'ffn2_w_up', 'ffn2_w_down', 'final_norm']


def kernel(x, norm_ffn1, ffn1_w_gate, ffn1_w_up, ffn1_w_down, norm_mix, w_in, attn_sinks, ssm_lambda_re, ssm_lambda_im, ssm_log_dt, ssm_b_re, ssm_b_im, ssm_c_re, ssm_c_im, ssm_d, ssm_glu_w, ssm_glu_b, attn_out_norm, ssm_out_norm, w_out, norm_ffn2, ffn2_w_gate, ffn2_w_up, ffn2_w_down, final_norm, loss_target, m_norm_ffn1, m_ffn1_w_gate, m_ffn1_w_up, m_ffn1_w_down, m_norm_mix, m_w_in, m_attn_sinks, m_ssm_lambda_re, m_ssm_lambda_im, m_ssm_log_dt, m_ssm_b_re, m_ssm_b_im, m_ssm_c_re, m_ssm_c_im, m_ssm_d, m_ssm_glu_w, m_ssm_glu_b, m_attn_out_norm, m_ssm_out_norm, m_w_out, m_norm_ffn2, m_ffn2_w_gate, m_ffn2_w_up, m_ffn2_w_down, m_final_norm, v_norm_ffn1, v_ffn1_w_gate, v_ffn1_w_up, v_ffn1_w_down, v_norm_mix, v_w_in, v_attn_sinks, v_ssm_lambda_re, v_ssm_lambda_im, v_ssm_log_dt, v_ssm_b_re, v_ssm_b_im, v_ssm_c_re, v_ssm_c_im, v_ssm_d, v_ssm_glu_w, v_ssm_glu_b, v_attn_out_norm, v_ssm_out_norm, v_w_out, v_norm_ffn2, v_ffn2_w_gate, v_ffn2_w_up, v_ffn2_w_down, v_final_norm):
    args = dict(locals())
    weights = {n: args[n] for n in WEIGHT_NAMES}
    moms = {n: args["m_" + n] for n in WEIGHT_NAMES}
    vars_ = {n: args["v_" + n] for n in WEIGHT_NAMES}

    def shard2d(a):
        return a.reshape(a.shape[-2], a.shape[-1])

    def blocks(g, k):
        return g.reshape(N_DEV, BIG[k][1], BIG[k][2])

    def full(g, k):
        return g.reshape(N_DEV * BIG[k][1], BIG[k][2])

    shards = dict(zip(BIG, _cast_shards({k: shard2d(weights[BIG[k][0]]) for k in BIG})))
    nf = len(FIRST_GROUP)
    got = _gather_first([shards[k] for k in FIRST_GROUP], [shards[k] for k in LATE_GROUP])
    w_first = {k: full(g, k) for k, g in zip(FIRST_GROUP, got[:nf])}
    w_send, w_recv, w_srcs, w_lands, w_token = _split_start(
        "gather_late_start", [shards[k] for k in LATE_GROUP], got[nf:], scatter=False)

    def late_weights(dep):
        _, lands = _split_wait("gather_late_wait", w_send, w_recv, w_srcs, w_lands, False, dep)
        return {k: full(g, k) for k, g in zip(LATE_GROUP, lands)}

    early = {}

    def early_grads(g):
        srcs = [blocks(g[k], k) for k in LATE_GROUP]
        lands = [lax.empty((N_PEERS, BIG[k][1], BIG[k][2]), BF16) for k in LATE_GROUP]
        early["send"], early["recv"], early["srcs"], early["lands"], token = _split_start(
            "grads_late_start", srcs, lands, scatter=True)
        return token

    small_p = {n: weights[n].reshape(SMALL[n]) for n in SMALL}
    loss, grad_x, g_first, g_small = _local_step(
        x.reshape(SEQ, D_MODEL), loss_target.reshape(SEQ, D_MODEL), w_first, small_p, late_weights, early_grads,
        after=w_token)

    own_first = [blocks(g_first[k], k) for k in FIRST_GROUP]
    exch = _exchange_last(own_first, _pack_small(g_small))
    own_late, late_parts = _split_wait("grads_late_wait", early["send"], early["recv"], early["srcs"],
                                       early["lands"], True, exch[-1])
    own = dict(zip(FIRST_GROUP + LATE_GROUP, own_first + list(own_late)))
    parts = dict(zip(FIRST_GROUP + LATE_GROUP, list(exch[:-1]) + list(late_parts)))
    outs = {}
    for k in BIG:
        n = BIG[k][0]
        outs[n] = [o.reshape(weights[n].shape) for o in
                   _adamw_big(own[k], parts[k], shard2d(weights[n]), shard2d(moms[n]), shard2d(vars_[n]),
                              "adamw_" + n, BIG[k][3])]
    small_shapes = {n: weights[n].shape for n in SMALL}
    packed = _adamw_small(exch[-1], _pack_small({n: weights[n] for n in SMALL}),
                          _pack_small({n: moms[n] for n in SMALL}), _pack_small({n: vars_[n] for n in SMALL}))
    unpacked = [_unpack_small(pk, small_shapes) for pk in packed]
    for n in SMALL:
        outs[n] = [u[n] for u in unpacked]

    total_loss = lax.psum(loss[0, 0], ("x", "y", "c"))
    result = [total_loss, grad_x.reshape(x.shape)]
    for i in range(4):
        result += [outs[n][i] for n in WEIGHT_NAMES]
    return tuple(result)
```

```python
import functools

import jax
import jax.numpy as jnp
from jax import lax
from jax.experimental import pallas as pl
from jax.experimental.pallas import tpu as pltpu

F32 = jnp.float32
BF16 = jnp.bfloat16

N_DEV = 8
SEQ = 2048
D_MODEL = 1024
D_FF = 2816
ATTN_HEADS = 8
KV_HEADS = 2
HEAD_DIM = 64
ATTN_WIDTH = 512
KV_WIDTH = 128
WINDOW = 128
SSM_WIDTH = 512
IN_WIDTH = 1280
EPS = 1e-6
NEG_INF = -1e30
LAMBDA_RE_MAX = -1e-4
LANES = 128
N_LANE_BLOCKS = 16
SCAN_CHUNK = SEQ // 8

ADAM_LR = 0.001
ADAM_B1 = 0.9
ADAM_B2 = 0.999
ADAM_EPS = 1e-08
ADAM_WD = 0.01
ADAM_STEP = 10

VMEM_LIMIT = 56 * 1024 * 1024
MESH_ID = pl.DeviceIdType.MESH


def _cparams(sem=None):
    return pltpu.CompilerParams(dimension_semantics=sem, vmem_limit_bytes=VMEM_LIMIT)


def _dot(a, b):
    return jnp.dot(a, b, preferred_element_type=F32)


def _dot_nt(a, b):
    return lax.dot_general(a, b, (((1,), (1,)), ((), ())), preferred_element_type=F32)


def _dot_tn(a, b):
    return lax.dot_general(a, b, (((0,), (0,)), ((), ())), preferred_element_type=F32)


def _rms_fwd(x, g):
    r = lax.rsqrt(jnp.mean(x * x, axis=-1, keepdims=True) + EPS)
    return x * r * g


def _rms_bwd(dh, x, g):
    r = lax.rsqrt(jnp.mean(x * x, axis=-1, keepdims=True) + EPS)
    xh = x * r
    dg = jnp.sum(dh * xh, axis=0, keepdims=True)
    dxh = dh * g
    dx = r * (dxh - xh * jnp.mean(dxh * xh, axis=-1, keepdims=True))
    return dx, dg


def _sigmoid(x):
    return 1.0 / (1.0 + jnp.exp(-x))


FFN_TM = 512
FFN_TF = 256


def _ffn_fwd(x, g, wgt, wut, wd, name, after=None):
    tm, tf = FFN_TM, FFN_TF
    nj = D_FF // tf
    deps = [] if after is None else [after]

    def body(x_ref, g_ref, wg_ref, wu_ref, wd_ref, *rest):
        xo_ref, h_ref, a_ref, b_ref, h_s, acc = rest[len(deps):]
        j = pl.program_id(1)

        @pl.when(j == 0)
        def _():
            h = _rms_fwd(x_ref[...], g_ref[...]).astype(BF16)
            h_s[...] = h
            h_ref[...] = h
            acc[...] = jnp.zeros_like(acc)

        h = h_s[...]
        a = _dot_nt(h, wg_ref[...])
        b = _dot_nt(h, wu_ref[...])
        a_ref[...] = a.astype(BF16)
        b_ref[...] = b.astype(BF16)
        s = (a * _sigmoid(a) * b).astype(BF16)
        acc[...] += _dot(s, wd_ref[...])

        @pl.when(j == nj - 1)
        def _():
            xo_ref[...] = x_ref[...] + 0.5 * acc[...]

    return pl.pallas_call(
        body, name=name, grid=(SEQ // tm, nj),
        in_specs=[pl.BlockSpec((tm, D_MODEL), lambda i, j: (i, 0)),
                  pl.BlockSpec((1, D_MODEL), lambda i, j: (0, 0)),
                  pl.BlockSpec((tf, D_MODEL), lambda i, j: (j, 0)),
                  pl.BlockSpec((tf, D_MODEL), lambda i, j: (j, 0)),
                  pl.BlockSpec((tf, D_MODEL), lambda i, j: (j, 0))] + [pl.BlockSpec(memory_space=pl.ANY)] * len(deps),
        out_specs=[pl.BlockSpec((tm, D_MODEL), lambda i, j: (i, 0)),
                   pl.BlockSpec((tm, D_MODEL), lambda i, j: (i, 0)),
                   pl.BlockSpec((tm, tf), lambda i, j: (i, j)),
                   pl.BlockSpec((tm, tf), lambda i, j: (i, j))],
        out_shape=[jax.ShapeDtypeStruct((SEQ, D_MODEL), F32), jax.ShapeDtypeStruct((SEQ, D_MODEL), BF16),
                   jax.ShapeDtypeStruct((SEQ, D_FF), BF16), jax.ShapeDtypeStruct((SEQ, D_FF), BF16)],
        scratch_shapes=[pltpu.VMEM((tm, D_MODEL), BF16), pltpu.VMEM((tm, D_MODEL), F32)],
        compiler_params=_cparams(("parallel", "arbitrary")),
    )(x, g, wgt, wut, wd, *deps)


def _ffn_bwd_act(dxo, x, g, a, b, wgt, wut, wd, name):
    tm, tf = FFN_TM, FFN_TF
    nj = D_FF // tf

    def body(dxo_ref, x_ref, g_ref, a_ref, b_ref, wg_ref, wu_ref, wd_ref,
             dx_ref, da_ref, db_ref, s_ref, df_ref, dg_ref, df_s, acc):
        i = pl.program_id(0)
        j = pl.program_id(1)

        @pl.when(j == 0)
        def _():
            df = (0.5 * dxo_ref[...]).astype(BF16)
            df_s[...] = df
            df_ref[...] = df
            acc[...] = jnp.zeros_like(acc)

        ds = _dot_nt(df_s[...], wd_ref[...])
        av = a_ref[...].astype(F32)
        bv = b_ref[...].astype(F32)
        sig = _sigmoid(av)
        sl = av * sig
        s_ref[...] = (sl * bv).astype(BF16)
        db = (ds * sl).astype(BF16)
        da = (ds * bv * (sig * (1.0 + av * (1.0 - sig)))).astype(BF16)
        da_ref[...] = da
        db_ref[...] = db
        acc[...] += _dot(da, wg_ref[...]) + _dot(db, wu_ref[...])

        @pl.when(j == nj - 1)
        def _():
            dx, dg = _rms_bwd(acc[...], x_ref[...], g_ref[...])
            dx_ref[...] = dxo_ref[...] + dx

            @pl.when(i == 0)
            def _():
                dg_ref[...] = dg

            @pl.when(i != 0)
            def _():
                dg_ref[...] += dg

    row = lambda i, j: (i, 0)
    col = lambda i, j: (j, 0)
    tile = lambda i, j: (i, j)
    return pl.pallas_call(
        body, name=name, grid=(SEQ // tm, nj),
        in_specs=[pl.BlockSpec((tm, D_MODEL), row), pl.BlockSpec((tm, D_MODEL), row),
                  pl.BlockSpec((1, D_MODEL), lambda i, j: (0, 0)),
                  pl.BlockSpec((tm, tf), tile), pl.BlockSpec((tm, tf), tile),
                  pl.BlockSpec((tf, D_MODEL), col), pl.BlockSpec((tf, D_MODEL), col), pl.BlockSpec((tf, D_MODEL), col)],
        out_specs=[pl.BlockSpec((tm, D_MODEL), row),
                   pl.BlockSpec((tm, tf), tile), pl.BlockSpec((tm, tf), tile), pl.BlockSpec((tm, tf), tile),
                   pl.BlockSpec((tm, D_MODEL), row),
                   pl.BlockSpec((1, D_MODEL), lambda i, j: (0, 0))],
        out_shape=[jax.ShapeDtypeStruct((SEQ, D_MODEL), F32),
                   jax.ShapeDtypeStruct((SEQ, D_FF), BF16), jax.ShapeDtypeStruct((SEQ, D_FF), BF16),
                   jax.ShapeDtypeStruct((SEQ, D_FF), BF16),
                   jax.ShapeDtypeStruct((SEQ, D_MODEL), BF16),
                   jax.ShapeDtypeStruct((1, D_MODEL), F32)],
        scratch_shapes=[pltpu.VMEM((tm, D_MODEL), BF16), pltpu.VMEM((tm, D_MODEL), F32)],
        compiler_params=_cparams(("arbitrary", "arbitrary")),
    )(dxo, x, g, a, b, wgt, wut, wd)


def _mm_tn(pairs, name, tmm=256):
    m = pairs[0][0].shape[1]
    n_pairs = len(pairs)

    def body(*refs):
        ins, outs = refs[:2 * n_pairs], refs[2 * n_pairs:]
        for p in range(n_pairs):
            outs[p][...] = _dot_tn(ins[2 * p][...], ins[2 * p + 1][...]).astype(BF16)

    in_specs, out_specs, out_shape, args = [], [], [], []
    for a, b in pairs:
        n = b.shape[1]
        in_specs += [pl.BlockSpec((SEQ, tmm), lambda i: (0, i)), pl.BlockSpec((SEQ, n), lambda i: (0, 0))]
        out_specs.append(pl.BlockSpec((tmm, n), lambda i: (i, 0)))
        out_shape.append(jax.ShapeDtypeStruct((m, n), BF16))
        args += [a, b]
    return pl.pallas_call(body, name=name, grid=(m // tmm,), in_specs=in_specs, out_specs=out_specs,
                          out_shape=out_shape, compiler_params=_cparams(("parallel",)))(*args)


MIX_TM = 256


def _mixin_fwd(x, g, wint):
    tm = MIX_TM

    def body(x_ref, g_ref, w_ref, h_ref, q_ref, k_ref, v_ref, u_ref):
        h = _rms_fwd(x_ref[...], g_ref[...]).astype(BF16)
        h_ref[...] = h
        proj = _dot_nt(h, w_ref[...])
        q_ref[...] = proj[:, :ATTN_WIDTH]
        k_ref[...] = proj[:, ATTN_WIDTH:ATTN_WIDTH + KV_WIDTH]
        v_ref[...] = proj[:, ATTN_WIDTH + KV_WIDTH:ATTN_WIDTH + 2 * KV_WIDTH]
        u_ref[...] = proj[:, ATTN_WIDTH + 2 * KV_WIDTH:]

    row = lambda i: (i, 0)
    return pl.pallas_call(
        body, name="mixin_fwd", grid=(SEQ // tm,),
        in_specs=[pl.BlockSpec((tm, D_MODEL), row), pl.BlockSpec((1, D_MODEL), lambda i: (0, 0)),
                  pl.BlockSpec((IN_WIDTH, D_MODEL), lambda i: (0, 0))],
        out_specs=[pl.BlockSpec((tm, D_MODEL), row), pl.BlockSpec((tm, ATTN_WIDTH), row),
                   pl.BlockSpec((tm, KV_WIDTH), row), pl.BlockSpec((tm, KV_WIDTH), row),
                   pl.BlockSpec((tm, SSM_WIDTH), row)],
        out_shape=[jax.ShapeDtypeStruct((SEQ, D_MODEL), BF16), jax.ShapeDtypeStruct((SEQ, ATTN_WIDTH), F32),
                   jax.ShapeDtypeStruct((SEQ, KV_WIDTH), F32), jax.ShapeDtypeStruct((SEQ, KV_WIDTH), F32),
                   jax.ShapeDtypeStruct((SEQ, SSM_WIDTH), F32)],
        compiler_params=_cparams(("parallel",)),
    )(x, g, wint)


def _mixin_bwd(dq, dk, dv, du, wint, x, g, dres):
    tm = MIX_TM

    def body(dq_ref, dk_ref, dv_ref, du_ref, w_ref, x_ref, g_ref, dres_ref, dx_ref, dp_ref, dg_ref):
        i = pl.program_id(0)
        dp = jnp.concatenate([dq_ref[...], dk_ref[...], dv_ref[...], du_ref[...]], axis=-1).astype(BF16)
        dp_ref[...] = dp
        dh = _dot(dp, w_ref[...])
        dx, dg = _rms_bwd(dh, x_ref[...], g_ref[...])
        dx_ref[...] = dres_ref[...] + dx

        @pl.when(i == 0)
        def _():
            dg_ref[...] = dg

        @pl.when(i != 0)
        def _():
            dg_ref[...] += dg

    row = lambda i: (i, 0)
    const = lambda i: (0, 0)
    return pl.pallas_call(
        body, name="mixin_bwd", grid=(SEQ // tm,),
        in_specs=[pl.BlockSpec((tm, ATTN_WIDTH), row), pl.BlockSpec((tm, KV_WIDTH), row),
                  pl.BlockSpec((tm, KV_WIDTH), row), pl.BlockSpec((tm, SSM_WIDTH), row),
                  pl.BlockSpec((IN_WIDTH, D_MODEL), const), pl.BlockSpec((tm, D_MODEL), row),
                  pl.BlockSpec((1, D_MODEL), const), pl.BlockSpec((tm, D_MODEL), row)],
        out_specs=[pl.BlockSpec((tm, D_MODEL), row), pl.BlockSpec((tm, IN_WIDTH), row),
                   pl.BlockSpec((1, D_MODEL), const)],
        out_shape=[jax.ShapeDtypeStruct((SEQ, D_MODEL), F32), jax.ShapeDtypeStruct((SEQ, IN_WIDTH), BF16),
                   jax.ShapeDtypeStruct((1, D_MODEL), F32)],
        compiler_params=_cparams(("arbitrary",)),
    )(dq, dk, dv, du, wint, x, g, dres)


N_QBLOCKS = SEQ // WINDOW
GROUP = ATTN_HEADS // KV_HEADS
SCALE = HEAD_DIM ** -0.5


def _alibi_slope(h):
    return 2.0 ** (-8.0 * (h + 1) / ATTN_HEADS)


def _window_masks(n):
    t_idx = lax.broadcasted_iota(jnp.int32, (WINDOW, 3 * WINDOW), 0)
    s_idx = lax.broadcasted_iota(jnp.int32, (WINDOW, 3 * WINDOW), 1)
    rel = s_idx - WINDOW - t_idx
    absrel = jnp.abs(rel)
    key_pos = n * WINDOW - WINDOW + s_idx
    valid = (absrel <= WINDOW) & (key_pos >= 0) & (key_pos < SEQ)
    return absrel.astype(F32), valid


def _head_probs(qh, kw, absrel, valid, slope, sink):
    s = _dot_nt(qh, kw) * SCALE
    s = jnp.where(valid, s - slope * absrel, NEG_INF)
    m = jnp.maximum(jnp.max(s, axis=-1, keepdims=True), sink)
    p = jnp.exp(s - m)
    ps = jnp.exp(sink - m)
    inv = 1.0 / (jnp.sum(p, axis=-1, keepdims=True) + ps)
    return p * inv, ps * inv


def _attn_fwd(q, kp, vp, sinks):
    def body(sk_ref, q_ref, kp_ref, vp_ref, o_ref):
        def blk(n, carry):
            r0 = pl.multiple_of(n * WINDOW, WINDOW)
            absrel, valid = _window_masks(n)
            for gi in range(KV_HEADS):
                kw = kp_ref[pl.ds(r0, 3 * WINDOW), gi * HEAD_DIM:(gi + 1) * HEAD_DIM].astype(BF16)
                vw = vp_ref[pl.ds(r0, 3 * WINDOW), gi * HEAD_DIM:(gi + 1) * HEAD_DIM].astype(BF16)
                for hh in range(GROUP):
                    h = gi * GROUP + hh
                    cols = slice(h * HEAD_DIM, (h + 1) * HEAD_DIM)
                    qh = q_ref[pl.ds(r0, WINDOW), cols].astype(BF16)
                    pr, _ = _head_probs(qh, kw, absrel, valid, _alibi_slope(h), sk_ref[0, h])
                    o_ref[pl.ds(r0, WINDOW), cols] = _dot(pr.astype(BF16), vw)
            return carry

        lax.fori_loop(0, N_QBLOCKS, blk, 0)

    vmem = pl.BlockSpec(memory_space=pltpu.VMEM)
    return pl.pallas_call(
        body, name="attn_fwd",
        in_specs=[pl.BlockSpec(memory_space=pltpu.SMEM), vmem, vmem, vmem], out_specs=vmem,
        out_shape=jax.ShapeDtypeStruct((SEQ, ATTN_WIDTH), F32),
        compiler_params=_cparams(),
    )(sinks, q, kp, vp)


def _attn_bwd(q, kp, vp, sinks, do):
    def body(sk_ref, q_ref, kp_ref, vp_ref, do_ref, dq_ref, dkp_ref, dvp_ref, dsk_ref, dsk_acc):
        dkp_ref[...] = jnp.zeros_like(dkp_ref)
        dvp_ref[...] = jnp.zeros_like(dvp_ref)
        dsk_acc[...] = jnp.zeros_like(dsk_acc)

        def blk(n, carry):
            r0 = pl.multiple_of(n * WINDOW, WINDOW)
            absrel, valid = _window_masks(n)
            for gi in range(KV_HEADS):
                gcols = slice(gi * HEAD_DIM, (gi + 1) * HEAD_DIM)
                kw = kp_ref[pl.ds(r0, 3 * WINDOW), gcols].astype(BF16)
                vw = vp_ref[pl.ds(r0, 3 * WINDOW), gcols].astype(BF16)
                dkw = jnp.zeros((3 * WINDOW, HEAD_DIM), F32)
                dvw = jnp.zeros((3 * WINDOW, HEAD_DIM), F32)
                for hh in range(GROUP):
                    h = gi * GROUP + hh
                    cols = slice(h * HEAD_DIM, (h + 1) * HEAD_DIM)
                    qh = q_ref[pl.ds(r0, WINDOW), cols].astype(BF16)
                    doh = do_ref[pl.ds(r0, WINDOW), cols].astype(BF16)
                    pr, psink = _head_probs(qh, kw, absrel, valid, _alibi_slope(h), sk_ref[0, h])
                    dp = _dot_nt(doh, vw)
                    delta = jnp.sum(pr * dp, axis=-1, keepdims=True)
                    ds = (pr * (dp - delta)).astype(BF16)
                    dsk_acc[:, h:h + 1] += -(psink * delta)
                    dq_ref[pl.ds(r0, WINDOW), cols] = _dot(ds, kw) * SCALE
                    dkw = dkw + _dot_tn(ds, qh) * SCALE
                    dvw = dvw + _dot_tn(pr.astype(BF16), doh)
                dkp_ref[pl.ds(r0, 3 * WINDOW), gcols] += dkw
                dvp_ref[pl.ds(r0, 3 * WINDOW), gcols] += dvw
            return carry

        lax.fori_loop(0, N_QBLOCKS, blk, 0)
        dsk_ref[...] = jnp.sum(dsk_acc[...], axis=0, keepdims=True)

    vmem = pl.BlockSpec(memory_space=pltpu.VMEM)
    return pl.pallas_call(
        body, name="attn_bwd",
        in_specs=[pl.BlockSpec(memory_space=pltpu.SMEM), vmem, vmem, vmem, vmem],
        out_specs=[vmem, vmem, vmem, vmem],
        out_shape=[jax.ShapeDtypeStruct((SEQ, ATTN_WIDTH), F32),
                   jax.ShapeDtypeStruct((SEQ + 2 * WINDOW, KV_WIDTH), F32),
                   jax.ShapeDtypeStruct((SEQ + 2 * WINDOW, KV_WIDTH), F32),
                   jax.ShapeDtypeStruct((1, ATTN_HEADS), F32)],
        scratch_shapes=[pltpu.VMEM((WINDOW, ATTN_HEADS), F32)],
        compiler_params=_cparams(),
    )(sinks, q, kp, vp, do)


def _ssm_prep(lam_re, lam_im, log_dt, bpad_re, bpad_im):
    nb = 2 * N_LANE_BLOCKS

    def body(lr_ref, li_ref, ldt_ref, br_ref, bi_ref, ar_ref, ai_ref, bbr_ref, bbi_ref):
        lr = jnp.minimum(lr_ref[...], LAMBDA_RE_MAX)
        li = li_ref[...]
        dt = jnp.exp(ldt_ref[...])
        mag = jnp.exp(lr * dt)
        ar = mag * jnp.cos(li * dt)
        ai = mag * jnp.sin(li * dt)
        den = lr * lr + li * li
        cr = ((ar - 1.0) * lr + ai * li) / den
        ci = (ai * lr - (ar - 1.0) * li) / den
        ar_ref[...] = ar
        ai_ref[...] = ai
        for i in range(nb):
            br = br_ref[i]
            bi = bi_ref[i]
            cri, cii = cr[i:i + 1, :], ci[i:i + 1, :]
            bbr_ref[i] = (cri * br - cii * bi).astype(BF16)
            bbi_ref[i] = (cri * bi + cii * br).astype(BF16)

    return pl.pallas_call(
        body, name="ssm_prep",
        out_shape=[jax.ShapeDtypeStruct((nb, LANES), F32), jax.ShapeDtypeStruct((nb, LANES), F32),
                   jax.ShapeDtypeStruct((nb, LANES, LANES), BF16), jax.ShapeDtypeStruct((nb, LANES, LANES), BF16)],
        compiler_params=_cparams(),
    )(lam_re, lam_im, log_dt, bpad_re, bpad_im)


def _ssm_prep_bwd(lam_re, lam_im, log_dt, bpad_re, bpad_im, dar, dai, dbbr, dbbi):
    nb = 2 * N_LANE_BLOCKS

    def body(lr_ref, li_ref, ldt_ref, br_ref, bi_ref, dar_ref, dai_ref, dbbr_ref, dbbi_ref,
             glr_ref, gli_ref, gdt_ref, gbr_ref, gbi_ref, gcr_s, gci_s):
        lam = lr_ref[...]
        lr = jnp.minimum(lam, LAMBDA_RE_MAX)
        li = li_ref[...]
        dt = jnp.exp(ldt_ref[...])
        mag = jnp.exp(lr * dt)
        cs = jnp.cos(li * dt)
        sn = jnp.sin(li * dt)
        ar = mag * cs
        ai = mag * sn
        den = lr * lr + li * li
        nr = (ar - 1.0) * lr + ai * li
        ni = ai * lr - (ar - 1.0) * li
        cr = nr / den
        ci = ni / den
        for i in range(nb):
            br = br_ref[i]
            bi = bi_ref[i]
            gbbr = dbbr_ref[i]
            gbbi = dbbi_ref[i]
            cri, cii = cr[i:i + 1, :], ci[i:i + 1, :]
            gcr_s[i:i + 1, :] = jnp.sum(gbbr * br + gbbi * bi, axis=0, keepdims=True)
            gci_s[i:i + 1, :] = jnp.sum(gbbi * br - gbbr * bi, axis=0, keepdims=True)
            gbr_ref[i] = cri * gbbr + cii * gbbi
            gbi_ref[i] = cri * gbbi - cii * gbbr
        g_cr = gcr_s[...]
        g_ci = gci_s[...]
        g_nr = g_cr / den
        g_ni = g_ci / den
        g_den = -(g_cr * nr + g_ci * ni) / (den * den)
        g_ar = dar_ref[...] + g_nr * lr - g_ni * li
        g_ai = dai_ref[...] + g_nr * li + g_ni * lr
        g_lr = g_nr * (ar - 1.0) + g_ni * ai + g_den * 2.0 * lr
        g_li = g_nr * ai - g_ni * (ar - 1.0) + g_den * 2.0 * li
        g_mag = g_ar * cs + g_ai * sn
        g_th = (g_ai * cs - g_ar * sn) * mag
        g_lr = g_lr + g_mag * mag * dt
        g_li = g_li + g_th * dt
        g_dt = g_mag * mag * lr + g_th * li
        glr_ref[...] = jnp.where(lam < LAMBDA_RE_MAX, g_lr, 0.0)
        gli_ref[...] = g_li
        gl = g_dt * dt
        half = LANES // 2
        gdt_ref[:, 0:1] = jnp.sum(gl[:, :half], axis=1, keepdims=True)
        gdt_ref[:, 1:2] = jnp.sum(gl[:, half:], axis=1, keepdims=True)

    return pl.pallas_call(
        body, name="ssm_prep_bwd",
        out_shape=[jax.ShapeDtypeStruct((nb, LANES), F32), jax.ShapeDtypeStruct((nb, LANES), F32),
                   jax.ShapeDtypeStruct((nb, 2), F32),
                   jax.ShapeDtypeStruct((nb, LANES, LANES), F32), jax.ShapeDtypeStruct((nb, LANES, LANES), F32)],
        scratch_shapes=[pltpu.VMEM((nb, LANES), F32), pltpu.VMEM((nb, LANES), F32)],
        compiler_params=_cparams(),
    )(lam_re, lam_im, log_dt, bpad_re, bpad_im, dar, dai, dbbr, dbbi)


def _cmul(ar, ai, br, bi):
    return ar * br - ai * bi, ar * bi + ai * br


def _interleave_rows(src_ref, dst_ref):
    def step(j, carry):
        dst_ref[pl.ds(pl.multiple_of(j * 8, 8), 8), :] = src_ref[pl.ds(j, 8, stride=SCAN_CHUNK), :]
        return carry
    lax.fori_loop(0, SCAN_CHUNK, step, 0, unroll=4)


def _deinterleave_rows(src_ref, dst_ref):
    def step(j, carry):
        dst_ref[pl.ds(j, 8, stride=SCAN_CHUNK), :] = src_ref[pl.ds(pl.multiple_of(j * 8, 8), 8), :]
        return carry
    lax.fori_loop(0, SCAN_CHUNK, step, 0, unroll=4)


def _scan_inplace(re_ref, im_ref, a_re, a_im, reverse):
    nq = len(a_re)
    ch = SCAN_CHUNK
    ab_re = [jnp.broadcast_to(a, (8, LANES)) for a in a_re]
    ab_im = [jnp.broadcast_to(a, (8, LANES)) for a in a_im]

    def rows(j):
        jj = (ch - 1 - j) if reverse else j
        return pl.ds(pl.multiple_of(jj * 8, 8), 8)

    def sweep(init, store):
        def step(j, st):
            out = []
            r = rows(j)
            for qi in range(nq):
                xr, xi = st[2 * qi], st[2 * qi + 1]
                pr, pi = _cmul(ab_re[qi], ab_im[qi], xr, xi)
                xr = pr + re_ref[qi, r, :]
                xi = pi + im_ref[qi, r, :]
                if store:
                    re_ref[qi, r, :] = xr
                    im_ref[qi, r, :] = xi
                out += [xr, xi]
            return tuple(out)
        return lax.fori_loop(0, ch, step, tuple(init), unroll=2)

    zeros = [jnp.zeros((8, LANES), F32)] * (2 * nq)
    finals = sweep(zeros, store=False)

    row_id = lax.broadcasted_iota(jnp.int32, (8, LANES), 0)
    carries = []
    for qi in range(nq):
        pr, pi = ab_re[qi], ab_im[qi]
        for _ in range(8):
            pr, pi = _cmul(pr, pi, pr, pi)
        fr, fi = finals[2 * qi], finals[2 * qi + 1]
        sr = jnp.zeros((8, LANES), F32)
        si = jnp.zeros((8, LANES), F32)
        for _ in range(7):
            tr, ti = _cmul(pr, pi, sr, si)
            tr, ti = tr + fr, ti + fi
            if reverse:
                sr = jnp.where(row_id == 7, 0.0, pltpu.roll(tr, 7, axis=0))
                si = jnp.where(row_id == 7, 0.0, pltpu.roll(ti, 7, axis=0))
            else:
                sr = jnp.where(row_id == 0, 0.0, pltpu.roll(tr, 1, axis=0))
                si = jnp.where(row_id == 0, 0.0, pltpu.roll(ti, 1, axis=0))
        carries += [sr, si]
    sweep(carries, store=True)


SSM_Q = 4


def _ssm_fwd(u, are, aim, bbr, bbi, cre, cim, dskip):
    nq = SSM_Q

    def body(u_ref, ar_ref, ai_ref, bbr_ref, bbi_ref, cr_ref, ci_ref, d_ref, y_ref, xr_ref, xi_ref,
             sre, sim, up, yp):
        _interleave_rows(u_ref, up)
        uf = up[...]
        ub = uf.astype(BF16)
        yp[...] = d_ref[...] * uf
        for d in range(2):
            for qi in range(nq):
                sre[qi] = _dot(ub, bbr_ref[d, qi])
                sim[qi] = _dot(ub, bbi_ref[d, qi])
            _scan_inplace(sre, sim, [ar_ref[d, qi] for qi in range(nq)], [ai_ref[d, qi] for qi in range(nq)],
                          reverse=(d == 1))
            for qi in range(nq):
                xrb = sre[qi].astype(BF16)
                xib = sim[qi].astype(BF16)
                xr_ref[d, qi] = xrb
                xi_ref[d, qi] = xib
                yp[...] += _dot(xrb, cr_ref[d, qi]) - _dot(xib, ci_ref[d, qi])
        _deinterleave_rows(yp, y_ref)

    blk4 = lambda k: (0, k, 0, 0)
    return pl.pallas_call(
        body, name="ssm_fwd", grid=(SSM_WIDTH // LANES,),
        in_specs=[pl.BlockSpec((SEQ, LANES), lambda k: (0, k)),
                  pl.BlockSpec((2, nq, 1, LANES), blk4), pl.BlockSpec((2, nq, 1, LANES), blk4),
                  pl.BlockSpec((2, nq, LANES, LANES), blk4), pl.BlockSpec((2, nq, LANES, LANES), blk4),
                  pl.BlockSpec((2, nq, LANES, LANES), blk4), pl.BlockSpec((2, nq, LANES, LANES), blk4),
                  pl.BlockSpec((1, LANES), lambda k: (0, k))],
        out_specs=[pl.BlockSpec((SEQ, LANES), lambda k: (0, k)),
                   pl.BlockSpec((2, nq, SEQ, LANES), blk4), pl.BlockSpec((2, nq, SEQ, LANES), blk4)],
        out_shape=[jax.ShapeDtypeStruct((SEQ, SSM_WIDTH), F32),
                   jax.ShapeDtypeStruct((2, N_LANE_BLOCKS, SEQ, LANES), BF16),
                   jax.ShapeDtypeStruct((2, N_LANE_BLOCKS, SEQ, LANES), BF16)],
        scratch_shapes=[pltpu.VMEM((nq, SEQ, LANES), F32), pltpu.VMEM((nq, SEQ, LANES), F32),
                        pltpu.VMEM((SEQ, LANES), F32), pltpu.VMEM((SEQ, LANES), F32)],
        compiler_params=_cparams(("parallel",)),
    )(u, are, aim, bbr, bbi, cre, cim, dskip)


def _ssm_bwd(dy, u, xr, xi, are, aim, bbr, bbi, cre, cim, dskip, after=None):
    nq = SSM_Q
    body_rows = SEQ - 8
    deps = [] if after is None else [after]

    def body(dy_ref, u_ref, xr_ref, xi_ref, ar_ref, ai_ref, bbr_ref, bbi_ref, cr_ref, ci_ref, d_ref, *rest):
        (du_ref, dd_ref, dcr_ref, dci_ref, dbr_ref, dbi_ref, dar_ref, dai_ref,
         sre, sim, up, dyp, dup) = rest[len(deps):]
        _interleave_rows(u_ref, up)
        _interleave_rows(dy_ref, dyp)
        dyf = dyp[...]
        uf = up[...]
        dyb = dyf.astype(BF16)
        ub = uf.astype(BF16)
        dd_ref[...] = jnp.sum(dyf * uf, axis=0, keepdims=True)
        dup[...] = d_ref[...] * dyf
        row8 = lax.broadcasted_iota(jnp.int32, (8, LANES), 0)
        for d in range(2):
            for qi in range(nq):
                sre[qi] = _dot_nt(dyb, cr_ref[d, qi])
                sim[qi] = -_dot_nt(dyb, ci_ref[d, qi])
                dcr_ref[d, qi] = _dot_tn(xr_ref[d, qi], dyb)
                dci_ref[d, qi] = -_dot_tn(xi_ref[d, qi], dyb)
            _scan_inplace(sre, sim, [ar_ref[d, qi] for qi in range(nq)], [-ai_ref[d, qi] for qi in range(nq)],
                          reverse=(d == 0))
            for qi in range(nq):
                gr = sre[qi]
                gi = sim[qi]
                xrf = xr_ref[d, qi].astype(F32)
                xif = xi_ref[d, qi].astype(F32)
                if d == 0:
                    g_main_r, g_main_i = gr[8:], gi[8:]
                    x_main_r, x_main_i = xrf[:body_rows], xif[:body_rows]
                    g_edge_r, g_edge_i = gr[:8], gi[:8]
                    x_edge_r = jnp.where(row8 == 0, 0.0, pltpu.roll(xrf[body_rows:], 1, axis=0))
                    x_edge_i = jnp.where(row8 == 0, 0.0, pltpu.roll(xif[body_rows:], 1, axis=0))
                else:
                    g_main_r, g_main_i = gr[:body_rows], gi[:body_rows]
                    x_main_r, x_main_i = xrf[8:], xif[8:]
                    g_edge_r, g_edge_i = gr[body_rows:], gi[body_rows:]
                    x_edge_r = jnp.where(row8 == 7, 0.0, pltpu.roll(xrf[:8], 7, axis=0))
                    x_edge_i = jnp.where(row8 == 7, 0.0, pltpu.roll(xif[:8], 7, axis=0))
                dar_ref[d, qi] = (jnp.sum(g_main_r * x_main_r + g_main_i * x_main_i, axis=0, keepdims=True)
                                  + jnp.sum(g_edge_r * x_edge_r + g_edge_i * x_edge_i, axis=0, keepdims=True))
                dai_ref[d, qi] = (jnp.sum(g_main_i * x_main_r - g_main_r * x_main_i, axis=0, keepdims=True)
                                  + jnp.sum(g_edge_i * x_edge_r - g_edge_r * x_edge_i, axis=0, keepdims=True))
                grb = gr.astype(BF16)
                gib = gi.astype(BF16)
                dup[...] += _dot_nt(grb, bbr_ref[d, qi]) + _dot_nt(gib, bbi_ref[d, qi])
                dbr_ref[d, qi] = _dot_tn(ub, grb)
                dbi_ref[d, qi] = _dot_tn(ub, gib)
        _deinterleave_rows(dup, du_ref)

    blk4 = lambda k: (0, k, 0, 0)
    col = lambda k: (0, k)
    w_spec = pl.BlockSpec((2, nq, LANES, LANES), blk4)
    a_spec = pl.BlockSpec((2, nq, 1, LANES), blk4)
    x_spec = pl.BlockSpec((2, nq, SEQ, LANES), blk4)
    w_shape = jax.ShapeDtypeStruct((2, N_LANE_BLOCKS, LANES, LANES), F32)
    a_shape = jax.ShapeDtypeStruct((2, N_LANE_BLOCKS, 1, LANES), F32)
    return pl.pallas_call(
        body, name="ssm_bwd", grid=(SSM_WIDTH // LANES,),
        in_specs=[pl.BlockSpec((SEQ, LANES), col), pl.BlockSpec((SEQ, LANES), col), x_spec, x_spec,
                  a_spec, a_spec, w_spec, w_spec, w_spec, w_spec, pl.BlockSpec((1, LANES), col)]
        + [pl.BlockSpec(memory_space=pl.ANY)] * len(deps),
        out_specs=[pl.BlockSpec((SEQ, LANES), col), pl.BlockSpec((1, LANES), col),
                   w_spec, w_spec, w_spec, w_spec, a_spec, a_spec],
        out_shape=[jax.ShapeDtypeStruct((SEQ, SSM_WIDTH), F32), jax.ShapeDtypeStruct((1, SSM_WIDTH), F32),
                   w_shape, w_shape, w_shape, w_shape, a_shape, a_shape],
        scratch_shapes=[pltpu.VMEM((nq, SEQ, LANES), F32), pltpu.VMEM((nq, SEQ, LANES), F32),
                        pltpu.VMEM((SEQ, LANES), F32), pltpu.VMEM((SEQ, LANES), F32), pltpu.VMEM((SEQ, LANES), F32)],
        compiler_params=_cparams(("parallel",)),
    )(dy, u, xr, xi, are, aim, bbr, bbi, cre, cim, dskip, *deps)


GELU_C = 0.7978845608028654
GELU_K = 0.044715


def _gelu(y):
    return 0.5 * y * (1.0 + jnp.tanh(GELU_C * (y + GELU_K * y * y * y)))


def _gelu_grad(y):
    t = jnp.tanh(GELU_C * (y + GELU_K * y * y * y))
    return 0.5 * (1.0 + t) + 0.5 * y * (1.0 - t * t) * GELU_C * (1.0 + 3.0 * GELU_K * y * y)


def _mixout_fwd(o, y, glu_w, glu_b, gan, gsn, wout, x1):
    tm = MIX_TM

    def body(o_ref, y_ref, gw_ref, gb_ref, gan_ref, gsn_ref, w_ref, x1_ref, x2_ref, mx_ref):
        yg = _gelu(y_ref[...])
        z = _dot(yg.astype(BF16), gw_ref[...]) + gb_ref[...]
        so = yg * _sigmoid(z)
        na = _rms_fwd(o_ref[...], gan_ref[...])
        ns = _rms_fwd(so, gsn_ref[...])
        mixed = jnp.concatenate([na, ns], axis=-1).astype(BF16)
        mx_ref[...] = mixed
        x2_ref[...] = x1_ref[...] + _dot(mixed, w_ref[...])

    row = lambda i: (i, 0)
    const = lambda i: (0, 0)
    return pl.pallas_call(
        body, name="mixout_fwd", grid=(SEQ // tm,),
        in_specs=[pl.BlockSpec((tm, ATTN_WIDTH), row), pl.BlockSpec((tm, SSM_WIDTH), row),
                  pl.BlockSpec((SSM_WIDTH, SSM_WIDTH), const), pl.BlockSpec((1, SSM_WIDTH), const),
                  pl.BlockSpec((1, ATTN_WIDTH), const), pl.BlockSpec((1, SSM_WIDTH), const),
                  pl.BlockSpec((D_MODEL, D_MODEL), const), pl.BlockSpec((tm, D_MODEL), row)],
        out_specs=[pl.BlockSpec((tm, D_MODEL), row), pl.BlockSpec((tm, D_MODEL), row)],
        out_shape=[jax.ShapeDtypeStruct((SEQ, D_MODEL), F32), jax.ShapeDtypeStruct((SEQ, D_MODEL), BF16)],
        compiler_params=_cparams(("parallel",)),
    )(o, y, glu_w, glu_b, gan, gsn, wout, x1)


def _mixout_bwd(dx2, o, y, glu_w, glu_b, gan, gsn, wout):
    tm = MIX_TM

    def body(dx2_ref, o_ref, y_ref, gw_ref, gb_ref, gan_ref, gsn_ref, w_ref,
             do_ref, dy_ref, dz_ref, yg_ref, dxb_ref, dgan_ref, dgsn_ref, dgb_ref):
        i = pl.program_id(0)
        dxb = dx2_ref[...].astype(BF16)
        dxb_ref[...] = dxb
        dmixed = _dot_nt(dxb, w_ref[...])
        do, dgan = _rms_bwd(dmixed[:, :ATTN_WIDTH], o_ref[...], gan_ref[...])
        do_ref[...] = do
        yv = y_ref[...]
        yg = _gelu(yv)
        ygb = yg.astype(BF16)
        yg_ref[...] = ygb
        sg = _sigmoid(_dot(ygb, gw_ref[...]) + gb_ref[...])
        dso, dgsn = _rms_bwd(dmixed[:, ATTN_WIDTH:], yg * sg, gsn_ref[...])
        dz = dso * yg * sg * (1.0 - sg)
        dzb = dz.astype(BF16)
        dz_ref[...] = dzb
        dyg = dso * sg + _dot_nt(dzb, gw_ref[...])
        dy_ref[...] = dyg * _gelu_grad(yv)
        dgb = jnp.sum(dz, axis=0, keepdims=True)

        @pl.when(i == 0)
        def _():
            dgan_ref[...] = dgan
            dgsn_ref[...] = dgsn
            dgb_ref[...] = dgb

        @pl.when(i != 0)
        def _():
            dgan_ref[...] += dgan
            dgsn_ref[...] += dgsn
            dgb_ref[...] += dgb

    row = lambda i: (i, 0)
    const = lambda i: (0, 0)
    return pl.pallas_call(
        body, name="mixout_bwd", grid=(SEQ // tm,),
        in_specs=[pl.BlockSpec((tm, D_MODEL), row), pl.BlockSpec((tm, ATTN_WIDTH), row),
                  pl.BlockSpec((tm, SSM_WIDTH), row),
                  pl.BlockSpec((SSM_WIDTH, SSM_WIDTH), const), pl.BlockSpec((1, SSM_WIDTH), const),
                  pl.BlockSpec((1, ATTN_WIDTH), const), pl.BlockSpec((1, SSM_WIDTH), const),
                  pl.BlockSpec((D_MODEL, D_MODEL), const)],
        out_specs=[pl.BlockSpec((tm, ATTN_WIDTH), row), pl.BlockSpec((tm, SSM_WIDTH), row),
                   pl.BlockSpec((tm, SSM_WIDTH), row), pl.BlockSpec((tm, SSM_WIDTH), row),
                   pl.BlockSpec((tm, D_MODEL), row),
                   pl.BlockSpec((1, ATTN_WIDTH), const), pl.BlockSpec((1, SSM_WIDTH), const),
                   pl.BlockSpec((1, SSM_WIDTH), const)],
        out_shape=[jax.ShapeDtypeStruct((SEQ, ATTN_WIDTH), F32), jax.ShapeDtypeStruct((SEQ, SSM_WIDTH), F32),
                   jax.ShapeDtypeStruct((SEQ, SSM_WIDTH), BF16), jax.ShapeDtypeStruct((SEQ, SSM_WIDTH), BF16),
                   jax.ShapeDtypeStruct((SEQ, D_MODEL), BF16),
                   jax.ShapeDtypeStruct((1, ATTN_WIDTH), F32), jax.ShapeDtypeStruct((1, SSM_WIDTH), F32),
                   jax.ShapeDtypeStruct((1, SSM_WIDTH), F32)],
        compiler_params=_cparams(("arbitrary",)),
    )(dx2, o, y, glu_w, glu_b, gan, gsn, wout)


def _loss_head(x, g, target):
    tm = MIX_TM

    def body(x_ref, g_ref, t_ref, loss_ref, dx_ref, dg_ref):
        i = pl.program_id(0)
        xv = x_ref[...]
        gv = g_ref[...]
        err = _rms_fwd(xv, gv) - t_ref[...]
        part = jnp.broadcast_to(0.5 * jnp.sum(err * err) / D_MODEL, (1, LANES))
        dx, dg = _rms_bwd(err * (1.0 / D_MODEL), xv, gv)
        dx_ref[...] = dx

        @pl.when(i == 0)
        def _():
            loss_ref[...] = part
            dg_ref[...] = dg

        @pl.when(i != 0)
        def _():
            loss_ref[...] += part
            dg_ref[...] += dg

    row = lambda i: (i, 0)
    const = lambda i: (0, 0)
    return pl.pallas_call(
        body, name="loss_head", grid=(SEQ // tm,),
        in_specs=[pl.BlockSpec((tm, D_MODEL), row), pl.BlockSpec((1, D_MODEL), const),
                  pl.BlockSpec((tm, D_MODEL), row)],
        out_specs=[pl.BlockSpec((1, LANES), const), pl.BlockSpec((tm, D_MODEL), row),
                   pl.BlockSpec((1, D_MODEL), const)],
        out_shape=[jax.ShapeDtypeStruct((1, LANES), F32), jax.ShapeDtypeStruct((SEQ, D_MODEL), F32),
                   jax.ShapeDtypeStruct((1, D_MODEL), F32)],
        compiler_params=_cparams(("arbitrary",)),
    )(x, g, target)


def _embed_blocks(b):
    b5 = b.reshape(2, 16, 2, 64, 16).transpose(0, 1, 2, 4, 3)
    eye2 = jnp.eye(2, dtype=b.dtype)
    m = (b5[:, :, :, :, None, :] * eye2[None, None, :, None, :, None]).reshape(2, 16, 32, LANES)
    sel = (jnp.arange(16)[:, None] % 4 == jnp.arange(4)[None, :]).astype(b.dtype)
    return (m[:, :, None, :, :] * sel[None, :, :, None, None]).reshape(2, 16, LANES, LANES)


def _extract_blocks(m):
    sel = (jnp.arange(16)[:, None] % 4 == jnp.arange(4)[None, :]).astype(m.dtype)
    m = jnp.sum(m.reshape(2, 16, 4, 32, LANES) * sel[None, :, :, None, None], axis=2)
    eye2 = jnp.eye(2, dtype=m.dtype)
    m = jnp.sum(m.reshape(2, 16, 2, 16, 2, 64) * eye2[None, None, :, None, :, None], axis=4)
    return m.transpose(0, 1, 2, 4, 3).reshape(2, 32, 64, 16)


def _local_step(x, target, w, p, late_weights, early_grads, after=None):
    x1, h1, a1, b1 = _ffn_fwd(x, p["norm_ffn1"], w["wgt1"], w["wut1"], w["wd1"], "ffn1_fwd", after=after)
    h2, q, k, v, u = _mixin_fwd(x1, p["norm_mix"], w["wint"])
    kp = jnp.pad(k, ((WINDOW, WINDOW), (0, 0)))
    vp = jnp.pad(v, ((WINDOW, WINDOW), (0, 0)))
    o = _attn_fwd(q, kp, vp, p["attn_sinks"])

    lam_re = p["ssm_lambda_re"].reshape(2 * N_LANE_BLOCKS, LANES)
    lam_im = p["ssm_lambda_im"].reshape(2 * N_LANE_BLOCKS, LANES)
    log_dt = jnp.repeat(p["ssm_log_dt"].reshape(2, 32), 64, axis=-1).reshape(2 * N_LANE_BLOCKS, LANES)
    bpad_re = _embed_blocks(p["ssm_b_re"].reshape(2, 32, 64, 16)).reshape(2 * N_LANE_BLOCKS, LANES, LANES)
    bpad_im = _embed_blocks(p["ssm_b_im"].reshape(2, 32, 64, 16)).reshape(2 * N_LANE_BLOCKS, LANES, LANES)
    c_t = lambda c: _embed_blocks(c.reshape(2, 32, 16, 64).transpose(0, 1, 3, 2)).transpose(0, 1, 3, 2)
    cre = c_t(p["ssm_c_re"]).astype(BF16)
    cim = c_t(p["ssm_c_im"]).astype(BF16)
    a_re, a_im, bbr, bbi = _ssm_prep(lam_re, lam_im, log_dt, bpad_re, bpad_im)
    shape_a = (2, N_LANE_BLOCKS, 1, LANES)
    shape_w = (2, N_LANE_BLOCKS, LANES, LANES)
    a_re4, a_im4 = a_re.reshape(shape_a), a_im.reshape(shape_a)
    bbr4, bbi4 = bbr.reshape(shape_w), bbi.reshape(shape_w)
    dskip = p["ssm_d"].reshape(1, SSM_WIDTH)
    y, xr, xi = _ssm_fwd(u, a_re4, a_im4, bbr4, bbi4, cre, cim, dskip)

    w2 = late_weights(y)
    x2, mixed = _mixout_fwd(o, y, w2["glu"], p["ssm_glu_b"], p["attn_out_norm"], p["ssm_out_norm"], w2["wout"], x1)
    x3, h3, a3, b3 = _ffn_fwd(x2, p["norm_ffn2"], w2["wgt2"], w2["wut2"], w2["wd2"], "ffn2_fwd")

    loss, dx3, d_final = _loss_head(x3, p["final_norm"], target)
    dx2, da3, db3, s3, df3, d_n2 = _ffn_bwd_act(dx3, x2, p["norm_ffn2"], a3, b3, w2["wgt2"], w2["wut2"], w2["wd2"],
                                                "ffn2_bwd_act")
    g_wgt2, g_wut2, g_wd2 = _mm_tn([(da3, h3), (db3, h3), (s3, df3)], "ffn2_bwd_w")

    do, dy, dz, ygb, dx2b, d_gan, d_gsn, d_glub = _mixout_bwd(
        dx2, o, y, w2["glu"], p["ssm_glu_b"], p["attn_out_norm"], p["ssm_out_norm"], w2["wout"])
    (g_wout,) = _mm_tn([(mixed, dx2b)], "wout_bwd_w")
    (g_glu,) = _mm_tn([(ygb, dz)], "glu_bwd_w")
    sent = early_grads(dict(glu=g_glu, wout=g_wout, wgt2=g_wgt2, wut2=g_wut2, wd2=g_wd2))

    du, d_dskip, dcre, dcim, dbbr, dbbi, dar, dai = _ssm_bwd(dy, u, xr, xi, a_re4, a_im4, bbr4, bbi4, cre, cim, dskip,
                                                             after=sent)
    nb = 2 * N_LANE_BLOCKS
    g_lre, g_lim, g_ldt, g_bpr, g_bpi = _ssm_prep_bwd(
        lam_re, lam_im, log_dt, bpad_re, bpad_im, dar.reshape(nb, LANES), dai.reshape(nb, LANES),
        dbbr.reshape(nb, LANES, LANES), dbbi.reshape(nb, LANES, LANES))

    dq, dkp, dvp, d_sinks = _attn_bwd(q, kp, vp, p["attn_sinks"], do)
    dk = dkp[WINDOW:WINDOW + SEQ]
    dv = dvp[WINDOW:WINDOW + SEQ]
    dx1, dproj, d_nmix = _mixin_bwd(dq, dk, dv, du, w["wint"], x1, p["norm_mix"], dx2)
    (g_wint,) = _mm_tn([(dproj, h2)], "win_bwd_w")

    dx0, da1, db1, s1, df1, d_n1 = _ffn_bwd_act(dx1, x, p["norm_ffn1"], a1, b1, w["wgt1"], w["wut1"], w["wd1"],
                                                "ffn1_bwd_act")
    g_wgt1, g_wut1, g_wd1 = _mm_tn([(da1, h1), (db1, h1), (s1, df1)], "ffn1_bwd_w")

    c_back = lambda g: _extract_blocks(g.transpose(0, 1, 3, 2)).transpose(0, 1, 3, 2)
    big = dict(wgt1=g_wgt1, wut1=g_wut1, wd1=g_wd1, wint=g_wint)
    small = dict(
        norm_ffn1=d_n1, norm_mix=d_nmix, attn_sinks=d_sinks,
        ssm_lambda_re=g_lre.reshape(64, 64), ssm_lambda_im=g_lim.reshape(64, 64),
        ssm_log_dt=g_ldt.reshape(2, 32),
        ssm_b_re=_extract_blocks(g_bpr.reshape(shape_w)).reshape(4096, 16),
        ssm_b_im=_extract_blocks(g_bpi.reshape(shape_w)).reshape(4096, 16),
        ssm_c_re=c_back(dcre).reshape(1024, 64), ssm_c_im=c_back(dcim).reshape(1024, 64),
        ssm_d=d_dskip.reshape(32, 16), ssm_glu_b=d_glub, attn_out_norm=d_gan, ssm_out_norm=d_gsn,
        norm_ffn2=d_n2, final_norm=d_final)
    return loss, dx0, big, small


BIG = dict(
    wgt1=("ffn1_w_gate", 352, 1024, True), wut1=("ffn1_w_up", 352, 1024, True), wd1=("ffn1_w_down", 352, 1024, False),
    wint=("w_in", 160, 1024, True), glu=("ssm_glu_w", 64, 512, False), wout=("w_out", 128, 1024, False),
    wgt2=("ffn2_w_gate", 352, 1024, True), wut2=("ffn2_w_up", 352, 1024, True), wd2=("ffn2_w_down", 352, 1024, False))

SMALL = dict(
    norm_ffn1=(1, 1024), norm_mix=(1, 1024), attn_sinks=(1, 8), ssm_lambda_re=(64, 64), ssm_lambda_im=(64, 64),
    ssm_log_dt=(2, 32), ssm_b_re=(4096, 16), ssm_b_im=(4096, 16), ssm_c_re=(1024, 64), ssm_c_im=(1024, 64),
    ssm_d=(32, 16), ssm_glu_b=(1, 512), attn_out_norm=(1, 512), ssm_out_norm=(1, 512), norm_ffn2=(1, 1024),
    final_norm=(1, 1024))
SMALL_TOTAL = sum(r * c for r, c in SMALL.values())
SMALL_ROWS = -(-SMALL_TOTAL // (8 * N_DEV * LANES)) * 8 * N_DEV


def _pad_to(n, mult):
    return -(-n // mult) * mult


def _transpose_2d(x, rows_out, cols_out):
    r_in, c_in = x.shape
    rp, cp = _pad_to(r_in, LANES), _pad_to(c_in, LANES)
    if cp != c_in:
        x = jnp.concatenate([x, jnp.zeros((r_in, cp - c_in), x.dtype)], axis=1)
    if rp != r_in:
        x = jnp.concatenate([x, jnp.zeros((rp - r_in, cp), x.dtype)], axis=0)
    return x.T[:rows_out, :cols_out]


def _cast_shards(shards):
    names = list(BIG)

    def body(*refs):
        ins, outs = refs[:len(names)], refs[len(names):]
        for idx, n in enumerate(names):
            _, rows, cols, transposed = BIG[n]
            v = ins[idx][...]
            if transposed:
                v = _transpose_2d(v, rows, cols)
            outs[idx][...] = v.astype(BF16)

    return pl.pallas_call(
        body, name="cast_shards",
        out_shape=[jax.ShapeDtypeStruct((BIG[n][1], BIG[n][2]), BF16) for n in names],
        compiler_params=_cparams(),
    )(*[shards[n] for n in names])


def _peer(x, y, c, r):
    px = 1 - x if r & 4 else x
    py = 1 - y if r & 2 else y
    pc = 1 - c if r & 1 else c
    return px, py, pc


FIRST_GROUP = ("wgt1", "wut1", "wd1", "wint")
LATE_GROUP = ("glu", "wout", "wgt2", "wut2", "wd2")
N_PEERS = N_DEV - 1
ANY_SPEC = pl.BlockSpec(memory_space=pl.ANY)
HBM_SPEC = pl.BlockSpec(memory_space=pltpu.HBM)
SEM_SPEC = pl.BlockSpec(memory_space=pltpu.SEMAPHORE)
DATAFLOW_EFFECT = pltpu.SideEffectType.DATAFLOW_SIDE_EFFECTING


def _mesh_pos():
    x, y, c = lax.axis_index("x"), lax.axis_index("y"), lax.axis_index("c")
    return x, y, c, 4 * x + 2 * y + c


def _gather_first(first, late):
    nf, nl = len(first), len(late)

    def body(*refs):
        f_in, l_in = refs[:nf], refs[nf:nf + nl]
        f_out, l_out = refs[nf + nl:2 * nf + nl], refs[2 * nf + nl:2 * (nf + nl)]
        send_sems, recv_sems, local_sems = refs[2 * (nf + nl):]
        x, y, c, me = _mesh_pos()
        sibling = (x, y, 1 - c)
        chips = [(x, 1 - y), (1 - x, y), (1 - x, 1 - y)]

        def idx(px, py, pc):
            return 4 * px + 2 * py + pc

        def copy(k, s, block, to, src=None):
            slot = f_out[k].at[block]
            return pltpu.make_async_remote_copy(
                src_ref=slot if src is None else src, dst_ref=slot, send_sem=send_sems.at[k, s],
                recv_sem=recv_sems.at[k, s], device_id=to, device_id_type=MESH_ID)

        local = []
        for k in range(nf + nl):
            src, dst = (f_in[k], f_out[k]) if k < nf else (l_in[k - nf], l_out[k - nf])
            mine = pltpu.make_async_copy(src, dst.at[me], local_sems.at[k])
            mine.start()
            local.append(mine)
        sends = []
        for j, chip in enumerate(chips):
            for k in range(nf):
                sends.append(copy(k, 1 + j, me, (*chip, c), src=f_in[k]))
                sends[-1].start()
        for k in range(nf):
            sends.append(copy(k, 0, me, sibling, src=f_in[k]))
            sends[-1].start()
        for j, chip in enumerate(chips):
            for k in range(nf):
                copy(k, 1 + j, idx(*chip, c), (*chip, c)).wait_recv()
                sends.append(copy(k, 4 + j, idx(*chip, c), sibling))
                sends[-1].start()
        for k in range(nf):
            copy(k, 0, idx(*sibling), sibling).wait_recv()
        for j, chip in enumerate(chips):
            for k in range(nf):
                copy(k, 4 + j, idx(*chip, 1 - c), sibling).wait_recv()
        for cp in sends:
            cp.wait_send()
        for cp in local:
            cp.wait()

    return pl.pallas_call(
        body, name="gather_first",
        in_specs=[ANY_SPEC] * (nf + nl), out_specs=[ANY_SPEC] * (nf + nl),
        out_shape=[jax.ShapeDtypeStruct((N_DEV,) + s.shape, s.dtype) for s in list(first) + list(late)],
        scratch_shapes=[pltpu.SemaphoreType.DMA((nf, N_PEERS)), pltpu.SemaphoreType.DMA((nf, N_PEERS)),
                        pltpu.SemaphoreType.DMA((nf + nl,))],
        compiler_params=pltpu.CompilerParams(has_side_effects=True),
    )(*first, *late)


def _split_copy(src_refs, land_refs, send_sems, recv_sems, k, r, pos, scatter, receiving):
    x, y, c, me = pos
    px, py, pc = _peer(x, y, c, r)
    peer_idx = 4 * px + 2 * py + pc
    if scatter:
        src, dst = src_refs[k].at[peer_idx], land_refs[k].at[r - 1]
    else:
        src, dst = src_refs[k], land_refs[k].at[peer_idx if receiving else me]
    return pltpu.make_async_remote_copy(
        src_ref=src, dst_ref=dst, send_sem=send_sems.at[k * N_PEERS + r - 1],
        recv_sem=recv_sems.at[k * N_PEERS + r - 1], device_id=(px, py, pc), device_id_type=MESH_ID)


def _split_start(name, srcs, lands, scatter):
    n = len(srcs)

    def body(*refs):
        src_refs, land_refs = refs[:n], refs[n:2 * n]
        send_sems, recv_sems = refs[2 * n], refs[2 * n + 1]
        token = refs[-1]
        pos = _mesh_pos()
        for k in range(n):
            for r in range(1, N_DEV):
                _split_copy(src_refs, land_refs, send_sems, recv_sems, k, r, pos, scatter, False).start()
        token[...] = jnp.zeros_like(token)

    thru = [pltpu.HBM(a.shape, a.dtype) for a in list(srcs) + list(lands)]
    outs = pl.pallas_call(
        body, name=name,
        in_specs=[HBM_SPEC] * (2 * n),
        out_specs=[SEM_SPEC, SEM_SPEC] + [HBM_SPEC] * (2 * n) + [pl.BlockSpec(memory_space=pltpu.VMEM)],
        out_shape=[pltpu.SemaphoreType.DMA((n * N_PEERS,)), pltpu.SemaphoreType.DMA((n * N_PEERS,))] + thru
        + [jax.ShapeDtypeStruct((8, LANES), F32)],
        input_output_aliases={i: 2 + i for i in range(2 * n)},
        compiler_params=pltpu.CompilerParams(has_side_effects=DATAFLOW_EFFECT),
    )(*[pltpu.with_memory_space_constraint(a, pltpu.HBM) for a in list(srcs) + list(lands)])
    return outs[0], outs[1], outs[2:2 + n], outs[2 + n:2 + 2 * n], outs[-1]


def _split_wait(name, send_sems, recv_sems, srcs, lands, scatter, after):
    n = len(srcs)

    def body(*refs):
        src_refs, land_refs = refs[:n], refs[n:2 * n]
        send, recv = refs[2 * n], refs[2 * n + 1]
        pos = _mesh_pos()
        for k in range(n):
            for r in range(1, N_DEV):
                cp = _split_copy(src_refs, land_refs, send, recv, k, r, pos, scatter, True)
                cp.wait_send()
                cp.wait_recv()

    thru = [pltpu.HBM(a.shape, a.dtype) for a in list(srcs) + list(lands)]
    outs = pl.pallas_call(
        body, name=name,
        in_specs=[HBM_SPEC] * (2 * n) + [SEM_SPEC, SEM_SPEC, ANY_SPEC],
        out_specs=[HBM_SPEC] * (2 * n), out_shape=thru,
        input_output_aliases={i: i for i in range(2 * n)},
        compiler_params=pltpu.CompilerParams(has_side_effects=DATAFLOW_EFFECT),
    )(*srcs, *lands, send_sems, recv_sems, after)
    return outs[:n], outs[n:]


def _exchange_last(grads, small_packed):
    ng = len(grads)
    ch = SMALL_ROWS // N_DEV
    max_rows = max(g.shape[1] for g in grads)
    cols = grads[0].shape[2]

    def body(*refs):
        g_in, s_in = refs[:ng], refs[ng]
        outs = refs[ng + 1:]
        own_out, land, stage = outs[:ng], outs[ng:2 * ng], outs[2 * ng:3 * ng]
        s_red, s_stage = outs[3 * ng], outs[3 * ng + 1]
        (va, vb, vo, vs, sm_in, sm_out, d2d_send, d2d_recv, ici_send, ici_recv, s1_send, s1_recv, s2_send, s2_recv,
         local_sems) = outs[3 * ng + 2:]
        x, y, c, me = _mesh_pos()
        sibling = (x, y, 1 - c)
        chips = [(x, y), (x, 1 - y), (1 - x, y), (1 - x, 1 - y)]

        def idx(chip, core):
            return 4 * chip[0] + 2 * chip[1] + core

        def d2d(k, j):
            return pltpu.make_async_remote_copy(
                src_ref=g_in[k].at[idx(chips[j], 1 - c)], dst_ref=stage[k].at[j], send_sem=d2d_send.at[k, j],
                recv_sem=d2d_recv.at[k, j], device_id=sibling, device_id_type=MESH_ID)

        def ici(k, j, slot):
            rows = g_in[k].shape[1]
            return pltpu.make_async_remote_copy(
                src_ref=vo.at[slot, pl.ds(0, rows)], dst_ref=land[k].at[j - 1], send_sem=ici_send.at[k, j - 1],
                recv_sem=ici_recv.at[k, j - 1], device_id=(*chips[j], c), device_id_type=MESH_ID)

        def small_scatter(r):
            px, py, pc = _peer(x, y, c, r)
            return pltpu.make_async_remote_copy(
                src_ref=s_in.at[pl.ds(pl.multiple_of((4 * px + 2 * py + pc) * ch, 8), ch)], dst_ref=s_stage.at[me],
                send_sem=s1_send.at[r - 1], recv_sem=s1_recv.at[r - 1], device_id=(px, py, pc), device_id_type=MESH_ID)

        def small_gather(r):
            return pltpu.make_async_remote_copy(
                src_ref=sm_out, dst_ref=s_red.at[me], send_sem=s2_send.at[r - 1], recv_sem=s2_recv.at[r - 1],
                device_id=_peer(x, y, c, r), device_id_type=MESH_ID)

        for r in range(1, N_DEV):
            small_scatter(r).start()
        mine = pltpu.make_async_copy(s_in.at[pl.ds(pl.multiple_of(me * ch, 8), ch)], s_stage.at[me], local_sems.at[0])
        mine.start()
        for j in (1, 2, 3, 0):
            for k in range(ng):
                d2d(k, j).start()

        for r in range(1, N_DEV):
            small_scatter(r).wait_recv()
        mine.wait()
        load = pltpu.make_async_copy(s_stage, sm_in, local_sems.at[1])
        load.start()
        load.wait()
        total = sm_in[0]
        for i in range(1, N_DEV):
            total = total + sm_in[i]
        sm_out[...] = total
        for r in range(1, N_DEV):
            small_gather(r).start()
        keep = pltpu.make_async_copy(sm_out, s_red.at[me], local_sems.at[2])
        keep.start()

        pairs = [(k, j) for j in (1, 2, 3, 0) for k in range(ng)]
        in_flight = {}
        for i, (k, j) in enumerate(pairs):
            slot = i % 2
            rows = g_in[k].shape[1]
            if slot in in_flight:
                in_flight.pop(slot).wait_send()
            d2d(k, j).wait_recv()
            la = pltpu.make_async_copy(g_in[k].at[idx(chips[j], c)], va.at[slot, pl.ds(0, rows)], local_sems.at[3])
            lb = pltpu.make_async_copy(stage[k].at[j], vb.at[slot, pl.ds(0, rows)], local_sems.at[4])
            la.start()
            lb.start()
            la.wait()
            lb.wait()
            total = va[slot, pl.ds(0, rows)].astype(F32) + vb[slot, pl.ds(0, rows)].astype(F32)
            if j == 0:
                vs[pl.ds(0, rows)] = total
                st = pltpu.make_async_copy(vs.at[pl.ds(0, rows)], own_out[k], local_sems.at[5])
                st.start()
                st.wait()
            else:
                vo[slot, pl.ds(0, rows)] = total.astype(BF16)
                cp = ici(k, j, slot)
                cp.start()
                in_flight[slot] = cp
        for cp in in_flight.values():
            cp.wait_send()

        for j in (1, 2, 3, 0):
            for k in range(ng):
                d2d(k, j).wait_send()
        for j in (1, 2, 3):
            for k in range(ng):
                ici(k, j, 0).wait_recv()
        for r in range(1, N_DEV):
            small_scatter(r).wait_send()
            small_gather(r).wait_send()
            small_gather(r).wait_recv()
        keep.wait()

    out_shape = [jax.ShapeDtypeStruct(g.shape[1:], F32) for g in grads]
    out_shape += [jax.ShapeDtypeStruct((3,) + g.shape[1:], BF16) for g in grads]
    out_shape += [jax.ShapeDtypeStruct((4,) + g.shape[1:], BF16) for g in grads]
    out_shape += [jax.ShapeDtypeStruct((N_DEV, ch, LANES), F32), jax.ShapeDtypeStruct((N_DEV, ch, LANES), F32)]
    outs = pl.pallas_call(
        body, name="exchange_last",
        in_specs=[ANY_SPEC] * (ng + 1), out_specs=[ANY_SPEC] * len(out_shape), out_shape=out_shape,
        scratch_shapes=[pltpu.VMEM((2, max_rows, cols), BF16), pltpu.VMEM((2, max_rows, cols), BF16),
                        pltpu.VMEM((2, max_rows, cols), BF16), pltpu.VMEM((max_rows, cols), F32),
                        pltpu.VMEM((N_DEV, ch, LANES), F32), pltpu.VMEM((ch, LANES), F32),
                        pltpu.SemaphoreType.DMA((ng, 4)), pltpu.SemaphoreType.DMA((ng, 4)),
                        pltpu.SemaphoreType.DMA((ng, 3)), pltpu.SemaphoreType.DMA((ng, 3)),
                        pltpu.SemaphoreType.DMA((N_PEERS,)), pltpu.SemaphoreType.DMA((N_PEERS,)),
                        pltpu.SemaphoreType.DMA((N_PEERS,)), pltpu.SemaphoreType.DMA((N_PEERS,)),
                        pltpu.SemaphoreType.DMA((6,))],
        compiler_params=pltpu.CompilerParams(has_side_effects=True, vmem_limit_bytes=VMEM_LIMIT),
    )(*grads, small_packed)
    return outs[:ng], outs[ng:2 * ng], outs[3 * ng].reshape(SMALL_ROWS, LANES)


def _adamw_math(w, g, m, v):
    m2 = ADAM_B1 * m + (1.0 - ADAM_B1) * g
    v2 = ADAM_B2 * v + (1.0 - ADAM_B2) * (g * g)
    m_hat = m2 / (1.0 - ADAM_B1 ** ADAM_STEP)
    v_hat = v2 / (1.0 - ADAM_B2 ** ADAM_STEP)
    delta = -ADAM_LR * (m_hat / (jnp.sqrt(v_hat) + ADAM_EPS) + ADAM_WD * w)
    return delta, m2, v2


def _adamw_big(own, parts, w, m, v, name, transposed):
    shape = w.shape
    own_is_blocks = own.ndim == 3

    def body(own_ref, p_ref, w_ref, m_ref, v_ref, g_ref, d_ref, m2_ref, v2_ref, own_s, sem):
        if own_is_blocks:
            cp = pltpu.make_async_copy(own_ref.at[_mesh_pos()[3]], own_s, sem)
        else:
            cp = pltpu.make_async_copy(own_ref, own_s, sem)
        cp.start()
        cp.wait()
        g = own_s[...].astype(F32)
        for i in range(parts.shape[0]):
            g = g + p_ref[i].astype(F32)
        if transposed:
            g = _transpose_2d(g, shape[0], shape[1])
        delta, m2, v2 = _adamw_math(w_ref[...], g, m_ref[...], v_ref[...])
        g_ref[...] = g
        d_ref[...] = delta
        m2_ref[...] = m2
        v2_ref[...] = v2

    vmem = pl.BlockSpec(memory_space=pltpu.VMEM)
    return pl.pallas_call(
        body, name=name, in_specs=[ANY_SPEC, vmem, vmem, vmem, vmem], out_specs=[vmem] * 4,
        out_shape=[jax.ShapeDtypeStruct(shape, F32)] * 4,
        scratch_shapes=[pltpu.VMEM(own.shape[-2:], own.dtype), pltpu.SemaphoreType.DMA(())],
        compiler_params=_cparams(),
    )(own, parts, w, m, v)


def _adamw_small(g, w, m, v):
    def body(g_ref, w_ref, m_ref, v_ref, d_ref, m2_ref, v2_ref):
        delta, m2, v2 = _adamw_math(w_ref[...], g_ref[...], m_ref[...], v_ref[...])
        d_ref[...] = delta
        m2_ref[...] = m2
        v2_ref[...] = v2

    return pl.pallas_call(
        body, name="adamw_small", out_shape=[jax.ShapeDtypeStruct((SMALL_ROWS, LANES), F32)] * 3,
        compiler_params=_cparams(),
    )(g, w, m, v)


def _pack_small(d):
    flat = jnp.concatenate([d[n].reshape(-1) for n in SMALL])
    flat = jnp.pad(flat, (0, SMALL_ROWS * LANES - SMALL_TOTAL))
    return flat.reshape(SMALL_ROWS, LANES)


def _unpack_small(packed, shapes):
    flat = packed.reshape(-1)
    out, off = {}, 0
    for n, (r, c) in SMALL.items():
        out[n] = flat[off:off + r * c].reshape(shapes[n])
        off += r * c
    return out


WEIGHT_NAMES = ['norm_ffn1', 'ffn1_w_gate', 'ffn1_w_up', 'ffn1_w_down', 'norm_mix', 'w_in', 'attn_sinks',
                'ssm_lambda_re', 'ssm_lambda_im', 'ssm_log_dt', 'ssm_b_re', 'ssm_b_im', 'ssm_c_re', 'ssm_c_im',
                'ssm_d', 'ssm_glu_w', 'ssm_glu_b', 'attn_out_norm', 'ssm_out_norm', 'w_out', 'norm_ffn2',
                'ffn2_w_gate', 'ffn2_w_up', 'ffn2_w_down', 'final_norm']


def kernel(x, norm_ffn1, ffn1_w_gate, ffn1_w_up, ffn1_w_down, norm_mix, w_in, attn_sinks, ssm_lambda_re, ssm_lambda_im, ssm_log_dt, ssm_b_re, ssm_b_im, ssm_c_re, ssm_c_im, ssm_d, ssm_glu_w, ssm_glu_b, attn_out_norm, ssm_out_norm, w_out, norm_ffn2, ffn2_w_gate, ffn2_w_up, ffn2_w_down, final_norm, loss_target, m_norm_ffn1, m_ffn1_w_gate, m_ffn1_w_up, m_ffn1_w_down, m_norm_mix, m_w_in, m_attn_sinks, m_ssm_lambda_re, m_ssm_lambda_im, m_ssm_log_dt, m_ssm_b_re, m_ssm_b_im, m_ssm_c_re, m_ssm_c_im, m_ssm_d, m_ssm_glu_w, m_ssm_glu_b, m_attn_out_norm, m_ssm_out_norm, m_w_out, m_norm_ffn2, m_ffn2_w_gate, m_ffn2_w_up, m_ffn2_w_down, m_final_norm, v_norm_ffn1, v_ffn1_w_gate, v_ffn1_w_up, v_ffn1_w_down, v_norm_mix, v_w_in, v_attn_sinks, v_ssm_lambda_re, v_ssm_lambda_im, v_ssm_log_dt, v_ssm_b_re, v_ssm_b_im, v_ssm_c_re, v_ssm_c_im, v_ssm_d, v_ssm_glu_w, v_ssm_glu_b, v_attn_out_norm, v_ssm_out_norm, v_w_out, v_norm_ffn2, v_ffn2_w_gate, v_ffn2_w_up, v_ffn2_w_down, v_final_norm):
    args = dict(locals())
    weights = {n: args[n] for n in WEIGHT_NAMES}
    moms = {n: args["m_" + n] for n in WEIGHT_NAMES}
    vars_ = {n: args["v_" + n] for n in WEIGHT_NAMES}

    def shard2d(a):
        return a.reshape(a.shape[-2], a.shape[-1])

    def blocks(g, k):
        return g.reshape(N_DEV, BIG[k][1], BIG[k][2])

    def full(g, k):
        return g.reshape(N_DEV * BIG[k][1], BIG[k][2])

    shards = dict(zip(BIG, _cast_shards({k: shard2d(weights[BIG[k][0]]) for k in BIG})))
    nf = len(FIRST_GROUP)
    got = _gather_first([shards[k] for k in FIRST_GROUP], [shards[k] for k in LATE_GROUP])
    w_first = {k: full(g, k) for k, g in zip(FIRST_GROUP, got[:nf])}
    w_send, w_recv, w_srcs, w_lands, w_token = _split_start(
        "gather_late_start", [shards[k] for k in LATE_GROUP], got[nf:], scatter=False)

    def late_weights(dep):
        _, lands = _split_wait("gather_late_wait", w_send, w_recv, w_srcs, w_lands, False, dep)
        return {k: full(g, k) for k, g in zip(LATE_GROUP, lands)}

    early = {}

    def early_grads(g):
        srcs = [blocks(g[k], k) for k in LATE_GROUP]
        lands = [lax.empty((N_PEERS, BIG[k][1], BIG[k][2]), BF16) for k in LATE_GROUP]
        early["send"], early["recv"], early["srcs"], early["lands"], token = _split_start(
            "grads_late_start", srcs, lands, scatter=True)
        return token

    small_p = {n: weights[n].reshape(SMALL[n]) for n in SMALL}
    loss, grad_x, g_first, g_small = _local_step(
        x.reshape(SEQ, D_MODEL), loss_target.reshape(SEQ, D_MODEL), w_first, small_p, late_weights, early_grads,
        after=w_token)

    own_sums, first_parts, small_grad = _exchange_last([blocks(g_first[k], k) for k in FIRST_GROUP],
                                                       _pack_small(g_small))
    own_late, late_parts = _split_wait("grads_late_wait", early["send"], early["recv"], early["srcs"],
                                       early["lands"], True, small_grad)
    own = dict(zip(FIRST_GROUP + LATE_GROUP, list(own_sums) + list(own_late)))
    parts = dict(zip(FIRST_GROUP + LATE_GROUP, list(first_parts) + list(late_parts)))
    outs = {}
    for k in BIG:
        n = BIG[k][0]
        outs[n] = [o.reshape(weights[n].shape) for o in
                   _adamw_big(own[k], parts[k], shard2d(weights[n]), shard2d(moms[n]), shard2d(vars_[n]),
                              "adamw_" + n, BIG[k][3])]
    small_shapes = {n: weights[n].shape for n in SMALL}
    packed = [small_grad] + list(_adamw_small(
        small_grad, _pack_small({n: weights[n] for n in SMALL}), _pack_small({n: moms[n] for n in SMALL}),
        _pack_small({n: vars_[n] for n in SMALL})))
    unpacked = [_unpack_small(pk, small_shapes) for pk in packed]
    for n in SMALL:
        outs[n] = [u[n] for u in unpacked]

    total_loss = lax.psum(loss[0, 0], ("x", "y", "c"))
    result = [total_loss, grad_x.reshape(x.shape)]
    for i in range(4):
        result += [outs[n][i] for n in WEIGHT_NAMES]
    return tuple(result)
```

```python
import functools

import jax
import jax.numpy as jnp
from jax import lax
from jax.experimental import pallas as pl
from jax.experimental.pallas import tpu as pltpu

F32 = jnp.float32
BF16 = jnp.bfloat16

N_DEV = 8
SEQ = 2048
D_MODEL = 1024
D_FF = 2816
ATTN_HEADS = 8
KV_HEADS = 2
HEAD_DIM = 64
ATTN_WIDTH = 512
KV_WIDTH = 128
WINDOW = 128
SSM_WIDTH = 512
IN_WIDTH = 1280
EPS = 1e-6
NEG_INF = -1e30
LAMBDA_RE_MAX = -1e-4
LANES = 128
N_LANE_BLOCKS = 16
SCAN_CHUNK = SEQ // 8

ADAM_LR = 0.001
ADAM_B1 = 0.9
ADAM_B2 = 0.999
ADAM_EPS = 1e-08
ADAM_WD = 0.01
ADAM_STEP = 10

VMEM_LIMIT = 56 * 1024 * 1024
MESH_ID = pl.DeviceIdType.MESH


def _cparams(sem=None):
    return pltpu.CompilerParams(dimension_semantics=sem, vmem_limit_bytes=VMEM_LIMIT)


def _dot(a, b):
    return jnp.dot(a, b, preferred_element_type=F32)


def _dot_nt(a, b):
    return lax.dot_general(a, b, (((1,), (1,)), ((), ())), preferred_element_type=F32)


def _dot_tn(a, b):
    return lax.dot_general(a, b, (((0,), (0,)), ((), ())), preferred_element_type=F32)


def _rms_fwd(x, g):
    r = lax.rsqrt(jnp.mean(x * x, axis=-1, keepdims=True) + EPS)
    return x * r * g


def _rms_bwd(dh, x, g):
    r = lax.rsqrt(jnp.mean(x * x, axis=-1, keepdims=True) + EPS)
    xh = x * r
    dg = jnp.sum(dh * xh, axis=0, keepdims=True)
    dxh = dh * g
    dx = r * (dxh - xh * jnp.mean(dxh * xh, axis=-1, keepdims=True))
    return dx, dg


def _sigmoid(x):
    return 1.0 / (1.0 + jnp.exp(-x))


FFN_TM = 512
FFN_TF = 1408


def _ffn_fwd(x, g, wgt, wut, wd, name, after=None):
    tm, tf = FFN_TM, FFN_TF
    nj = D_FF // tf
    deps = [] if after is None else [after]

    def body(x_ref, g_ref, wg_ref, wu_ref, wd_ref, *rest):
        xo_ref, h_ref, a_ref, b_ref, h_s, acc = rest[len(deps):]
        j = pl.program_id(1)

        @pl.when(j == 0)
        def _():
            h = _rms_fwd(x_ref[...], g_ref[...]).astype(BF16)
            h_s[...] = h
            h_ref[...] = h
            acc[...] = jnp.zeros_like(acc)

        h = h_s[...]
        a = _dot_nt(h, wg_ref[...])
        b = _dot_nt(h, wu_ref[...])
        a_ref[...] = a.astype(BF16)
        b_ref[...] = b.astype(BF16)
        s = (a * _sigmoid(a) * b).astype(BF16)
        acc[...] += _dot(s, wd_ref[...])

        @pl.when(j == nj - 1)
        def _():
            xo_ref[...] = x_ref[...] + 0.5 * acc[...]

    return pl.pallas_call(
        body, name=name, grid=(SEQ // tm, nj),
        in_specs=[pl.BlockSpec((tm, D_MODEL), lambda i, j: (i, 0)),
                  pl.BlockSpec((1, D_MODEL), lambda i, j: (0, 0)),
                  pl.BlockSpec((tf, D_MODEL), lambda i, j: (j, 0)),
                  pl.BlockSpec((tf, D_MODEL), lambda i, j: (j, 0)),
                  pl.BlockSpec((tf, D_MODEL), lambda i, j: (j, 0))] + [pl.BlockSpec(memory_space=pl.ANY)] * len(deps),
        out_specs=[pl.BlockSpec((tm, D_MODEL), lambda i, j: (i, 0)),
                   pl.BlockSpec((tm, D_MODEL), lambda i, j: (i, 0)),
                   pl.BlockSpec((tm, tf), lambda i, j: (i, j)),
                   pl.BlockSpec((tm, tf), lambda i, j: (i, j))],
        out_shape=[jax.ShapeDtypeStruct((SEQ, D_MODEL), F32), jax.ShapeDtypeStruct((SEQ, D_MODEL), BF16),
                   jax.ShapeDtypeStruct((SEQ, D_FF), BF16), jax.ShapeDtypeStruct((SEQ, D_FF), BF16)],
        scratch_shapes=[pltpu.VMEM((tm, D_MODEL), BF16), pltpu.VMEM((tm, D_MODEL), F32)],
        compiler_params=_cparams(("parallel", "arbitrary")),
    )(x, g, wgt, wut, wd, *deps)


def _ffn_bwd_act(dxo, x, g, a, b, wgt, wut, wd, name):
    tm, tf = FFN_TM // 2, FFN_TF
    nj = D_FF // tf

    def body(dxo_ref, x_ref, g_ref, a_ref, b_ref, wg_ref, wu_ref, wd_ref,
             dx_ref, da_ref, db_ref, s_ref, df_ref, dg_ref, df_s, acc):
        i = pl.program_id(0)
        j = pl.program_id(1)

        @pl.when(j == 0)
        def _():
            df = (0.5 * dxo_ref[...]).astype(BF16)
            df_s[...] = df
            df_ref[...] = df
            acc[...] = jnp.zeros_like(acc)

        ds = _dot_nt(df_s[...], wd_ref[...])
        av = a_ref[...].astype(F32)
        bv = b_ref[...].astype(F32)
        sig = _sigmoid(av)
        sl = av * sig
        s_ref[...] = (sl * bv).astype(BF16)
        db = (ds * sl).astype(BF16)
        da = (ds * bv * (sig * (1.0 + av * (1.0 - sig)))).astype(BF16)
        da_ref[...] = da
        db_ref[...] = db
        acc[...] += _dot(da, wg_ref[...]) + _dot(db, wu_ref[...])

        @pl.when(j == nj - 1)
        def _():
            dx, dg = _rms_bwd(acc[...], x_ref[...], g_ref[...])
            dx_ref[...] = dxo_ref[...] + dx

            @pl.when(i == 0)
            def _():
                dg_ref[...] = dg

            @pl.when(i != 0)
            def _():
                dg_ref[...] += dg

    row = lambda i, j: (i, 0)
    col = lambda i, j: (j, 0)
    tile = lambda i, j: (i, j)
    return pl.pallas_call(
        body, name=name, grid=(SEQ // tm, nj),
        in_specs=[pl.BlockSpec((tm, D_MODEL), row), pl.BlockSpec((tm, D_MODEL), row),
                  pl.BlockSpec((1, D_MODEL), lambda i, j: (0, 0)),
                  pl.BlockSpec((tm, tf), tile), pl.BlockSpec((tm, tf), tile),
                  pl.BlockSpec((tf, D_MODEL), col), pl.BlockSpec((tf, D_MODEL), col), pl.BlockSpec((tf, D_MODEL), col)],
        out_specs=[pl.BlockSpec((tm, D_MODEL), row),
                   pl.BlockSpec((tm, tf), tile), pl.BlockSpec((tm, tf), tile), pl.BlockSpec((tm, tf), tile),
                   pl.BlockSpec((tm, D_MODEL), row),
                   pl.BlockSpec((1, D_MODEL), lambda i, j: (0, 0))],
        out_shape=[jax.ShapeDtypeStruct((SEQ, D_MODEL), F32),
                   jax.ShapeDtypeStruct((SEQ, D_FF), BF16), jax.ShapeDtypeStruct((SEQ, D_FF), BF16),
                   jax.ShapeDtypeStruct((SEQ, D_FF), BF16),
                   jax.ShapeDtypeStruct((SEQ, D_MODEL), BF16),
                   jax.ShapeDtypeStruct((1, D_MODEL), F32)],
        scratch_shapes=[pltpu.VMEM((tm, D_MODEL), BF16), pltpu.VMEM((tm, D_MODEL), F32)],
        compiler_params=_cparams(("arbitrary", "arbitrary")),
    )(dxo, x, g, a, b, wgt, wut, wd)


def _mm_tn(pairs, name, tmm=256):
    m = pairs[0][0].shape[1]
    n_pairs = len(pairs)

    def body(*refs):
        ins, outs = refs[:2 * n_pairs], refs[2 * n_pairs:]
        for p in range(n_pairs):
            outs[p][...] = _dot_tn(ins[2 * p][...], ins[2 * p + 1][...]).astype(BF16)

    in_specs, out_specs, out_shape, args = [], [], [], []
    for a, b in pairs:
        n = b.shape[1]
        in_specs += [pl.BlockSpec((SEQ, tmm), lambda i: (0, i)), pl.BlockSpec((SEQ, n), lambda i: (0, 0))]
        out_specs.append(pl.BlockSpec((tmm, n), lambda i: (i, 0)))
        out_shape.append(jax.ShapeDtypeStruct((m, n), BF16))
        args += [a, b]
    return pl.pallas_call(body, name=name, grid=(m // tmm,), in_specs=in_specs, out_specs=out_specs,
                          out_shape=out_shape, compiler_params=_cparams(("parallel",)))(*args)


MIX_TM = 256


def _mixin_fwd(x, g, wint):
    tm = MIX_TM

    def body(x_ref, g_ref, w_ref, h_ref, q_ref, k_ref, v_ref, u_ref):
        h = _rms_fwd(x_ref[...], g_ref[...]).astype(BF16)
        h_ref[...] = h
        proj = _dot_nt(h, w_ref[...])
        q_ref[...] = proj[:, :ATTN_WIDTH]
        k_ref[...] = proj[:, ATTN_WIDTH:ATTN_WIDTH + KV_WIDTH]
        v_ref[...] = proj[:, ATTN_WIDTH + KV_WIDTH:ATTN_WIDTH + 2 * KV_WIDTH]
        u_ref[...] = proj[:, ATTN_WIDTH + 2 * KV_WIDTH:]

    row = lambda i: (i, 0)
    return pl.pallas_call(
        body, name="mixin_fwd", grid=(SEQ // tm,),
        in_specs=[pl.BlockSpec((tm, D_MODEL), row), pl.BlockSpec((1, D_MODEL), lambda i: (0, 0)),
                  pl.BlockSpec((IN_WIDTH, D_MODEL), lambda i: (0, 0))],
        out_specs=[pl.BlockSpec((tm, D_MODEL), row), pl.BlockSpec((tm, ATTN_WIDTH), row),
                   pl.BlockSpec((tm, KV_WIDTH), row), pl.BlockSpec((tm, KV_WIDTH), row),
                   pl.BlockSpec((tm, SSM_WIDTH), row)],
        out_shape=[jax.ShapeDtypeStruct((SEQ, D_MODEL), BF16), jax.ShapeDtypeStruct((SEQ, ATTN_WIDTH), F32),
                   jax.ShapeDtypeStruct((SEQ, KV_WIDTH), F32), jax.ShapeDtypeStruct((SEQ, KV_WIDTH), F32),
                   jax.ShapeDtypeStruct((SEQ, SSM_WIDTH), F32)],
        compiler_params=_cparams(("parallel",)),
    )(x, g, wint)


def _mixin_bwd(dq, dk, dv, du, wint, x, g, dres):
    tm = MIX_TM

    def body(dq_ref, dk_ref, dv_ref, du_ref, w_ref, x_ref, g_ref, dres_ref, dx_ref, dp_ref, dg_ref):
        i = pl.program_id(0)
        dp = jnp.concatenate([dq_ref[...], dk_ref[...], dv_ref[...], du_ref[...]], axis=-1).astype(BF16)
        dp_ref[...] = dp
        dh = _dot(dp, w_ref[...])
        dx, dg = _rms_bwd(dh, x_ref[...], g_ref[...])
        dx_ref[...] = dres_ref[...] + dx

        @pl.when(i == 0)
        def _():
            dg_ref[...] = dg

        @pl.when(i != 0)
        def _():
            dg_ref[...] += dg

    row = lambda i: (i, 0)
    const = lambda i: (0, 0)
    return pl.pallas_call(
        body, name="mixin_bwd", grid=(SEQ // tm,),
        in_specs=[pl.BlockSpec((tm, ATTN_WIDTH), row), pl.BlockSpec((tm, KV_WIDTH), row),
                  pl.BlockSpec((tm, KV_WIDTH), row), pl.BlockSpec((tm, SSM_WIDTH), row),
                  pl.BlockSpec((IN_WIDTH, D_MODEL), const), pl.BlockSpec((tm, D_MODEL), row),
                  pl.BlockSpec((1, D_MODEL), const), pl.BlockSpec((tm, D_MODEL), row)],
        out_specs=[pl.BlockSpec((tm, D_MODEL), row), pl.BlockSpec((tm, IN_WIDTH), row),
                   pl.BlockSpec((1, D_MODEL), const)],
        out_shape=[jax.ShapeDtypeStruct((SEQ, D_MODEL), F32), jax.ShapeDtypeStruct((SEQ, IN_WIDTH), BF16),
                   jax.ShapeDtypeStruct((1, D_MODEL), F32)],
        compiler_params=_cparams(("arbitrary",)),
    )(dq, dk, dv, du, wint, x, g, dres)


N_QBLOCKS = SEQ // WINDOW
GROUP = ATTN_HEADS // KV_HEADS
SCALE = HEAD_DIM ** -0.5


def _alibi_slope(h):
    return 2.0 ** (-8.0 * (h + 1) / ATTN_HEADS)


def _window_masks(n):
    t_idx = lax.broadcasted_iota(jnp.int32, (WINDOW, 3 * WINDOW), 0)
    s_idx = lax.broadcasted_iota(jnp.int32, (WINDOW, 3 * WINDOW), 1)
    rel = s_idx - WINDOW - t_idx
    absrel = jnp.abs(rel)
    key_pos = n * WINDOW - WINDOW + s_idx
    valid = (absrel <= WINDOW) & (key_pos >= 0) & (key_pos < SEQ)
    return absrel.astype(F32), valid


def _head_probs(qh, kw, absrel, valid, slope, sink):
    s = _dot_nt(qh, kw) * SCALE
    s = jnp.where(valid, s - slope * absrel, NEG_INF)
    m = jnp.maximum(jnp.max(s, axis=-1, keepdims=True), sink)
    p = jnp.exp(s - m)
    ps = jnp.exp(sink - m)
    inv = 1.0 / (jnp.sum(p, axis=-1, keepdims=True) + ps)
    return p * inv, ps * inv


def _attn_fwd(q, kp, vp, sinks):
    def body(sk_ref, q_ref, kp_ref, vp_ref, o_ref):
        def blk(n, carry):
            r0 = pl.multiple_of(n * WINDOW, WINDOW)
            absrel, valid = _window_masks(n)
            for gi in range(KV_HEADS):
                kw = kp_ref[pl.ds(r0, 3 * WINDOW), gi * HEAD_DIM:(gi + 1) * HEAD_DIM].astype(BF16)
                vw = vp_ref[pl.ds(r0, 3 * WINDOW), gi * HEAD_DIM:(gi + 1) * HEAD_DIM].astype(BF16)
                for hh in range(GROUP):
                    h = gi * GROUP + hh
                    cols = slice(h * HEAD_DIM, (h + 1) * HEAD_DIM)
                    qh = q_ref[pl.ds(r0, WINDOW), cols].astype(BF16)
                    pr, _ = _head_probs(qh, kw, absrel, valid, _alibi_slope(h), sk_ref[0, h])
                    o_ref[pl.ds(r0, WINDOW), cols] = _dot(pr.astype(BF16), vw)
            return carry

        lax.fori_loop(0, N_QBLOCKS, blk, 0)

    vmem = pl.BlockSpec(memory_space=pltpu.VMEM)
    return pl.pallas_call(
        body, name="attn_fwd",
        in_specs=[pl.BlockSpec(memory_space=pltpu.SMEM), vmem, vmem, vmem], out_specs=vmem,
        out_shape=jax.ShapeDtypeStruct((SEQ, ATTN_WIDTH), F32),
        compiler_params=_cparams(),
    )(sinks, q, kp, vp)


def _attn_bwd(q, kp, vp, sinks, do):
    def body(sk_ref, q_ref, kp_ref, vp_ref, do_ref, dq_ref, dkp_ref, dvp_ref, dsk_ref, dsk_acc):
        dkp_ref[...] = jnp.zeros_like(dkp_ref)
        dvp_ref[...] = jnp.zeros_like(dvp_ref)
        dsk_acc[...] = jnp.zeros_like(dsk_acc)

        def blk(n, carry):
            r0 = pl.multiple_of(n * WINDOW, WINDOW)
            absrel, valid = _window_masks(n)
            for gi in range(KV_HEADS):
                gcols = slice(gi * HEAD_DIM, (gi + 1) * HEAD_DIM)
                kw = kp_ref[pl.ds(r0, 3 * WINDOW), gcols].astype(BF16)
                vw = vp_ref[pl.ds(r0, 3 * WINDOW), gcols].astype(BF16)
                dkw = jnp.zeros((3 * WINDOW, HEAD_DIM), F32)
                dvw = jnp.zeros((3 * WINDOW, HEAD_DIM), F32)
                for hh in range(GROUP):
                    h = gi * GROUP + hh
                    cols = slice(h * HEAD_DIM, (h + 1) * HEAD_DIM)
                    qh = q_ref[pl.ds(r0, WINDOW), cols].astype(BF16)
                    doh = do_ref[pl.ds(r0, WINDOW), cols].astype(BF16)
                    pr, psink = _head_probs(qh, kw, absrel, valid, _alibi_slope(h), sk_ref[0, h])
                    dp = _dot_nt(doh, vw)
                    delta = jnp.sum(pr * dp, axis=-1, keepdims=True)
                    ds = (pr * (dp - delta)).astype(BF16)
                    dsk_acc[:, h:h + 1] += -(psink * delta)
                    dq_ref[pl.ds(r0, WINDOW), cols] = _dot(ds, kw) * SCALE
                    dkw = dkw + _dot_tn(ds, qh) * SCALE
                    dvw = dvw + _dot_tn(pr.astype(BF16), doh)
                dkp_ref[pl.ds(r0, 3 * WINDOW), gcols] += dkw
                dvp_ref[pl.ds(r0, 3 * WINDOW), gcols] += dvw
            return carry

        lax.fori_loop(0, N_QBLOCKS, blk, 0)
        dsk_ref[...] = jnp.sum(dsk_acc[...], axis=0, keepdims=True)

    vmem = pl.BlockSpec(memory_space=pltpu.VMEM)
    return pl.pallas_call(
        body, name="attn_bwd",
        in_specs=[pl.BlockSpec(memory_space=pltpu.SMEM), vmem, vmem, vmem, vmem],
        out_specs=[vmem, vmem, vmem, vmem],
        out_shape=[jax.ShapeDtypeStruct((SEQ, ATTN_WIDTH), F32),
                   jax.ShapeDtypeStruct((SEQ + 2 * WINDOW, KV_WIDTH), F32),
                   jax.ShapeDtypeStruct((SEQ + 2 * WINDOW, KV_WIDTH), F32),
                   jax.ShapeDtypeStruct((1, ATTN_HEADS), F32)],
        scratch_shapes=[pltpu.VMEM((WINDOW, ATTN_HEADS), F32)],
        compiler_params=_cparams(),
    )(sinks, q, kp, vp, do)


HALF_LANES = LANES // 2
BLOCK_ROWS = 32


def _embed_block(bt, q):
    z = jnp.zeros((16, HALF_LANES), bt.dtype)
    blk = jnp.concatenate([jnp.concatenate([bt[:16], z], axis=1), jnp.concatenate([z, bt[16:]], axis=1)], axis=0)
    parts = [jnp.zeros((BLOCK_ROWS * q, LANES), bt.dtype)] if q else []
    parts.append(blk)
    if q < 3:
        parts.append(jnp.zeros((BLOCK_ROWS * (3 - q), LANES), bt.dtype))
    return jnp.concatenate(parts, axis=0)


def _extract_block(m, q):
    blk = m[BLOCK_ROWS * q:BLOCK_ROWS * (q + 1)]
    return jnp.concatenate([blk[:16, :HALF_LANES], blk[16:, HALF_LANES:]], axis=0)


def _ssm_prep(lam_re, lam_im, log_dt, bt_re, bt_im, c_re, c_im):
    nb = 2 * N_LANE_BLOCKS

    def body(lr_ref, li_ref, ldt_ref, btr_ref, bti_ref, ctr_ref, cti_ref,
             ar_ref, ai_ref, bbr_ref, bbi_ref, cpr_ref, cpi_ref):
        lr = jnp.minimum(lr_ref[...], LAMBDA_RE_MAX)
        li = li_ref[...]
        dt = jnp.exp(ldt_ref[...])
        mag = jnp.exp(lr * dt)
        ar = mag * jnp.cos(li * dt)
        ai = mag * jnp.sin(li * dt)
        den = lr * lr + li * li
        cr = ((ar - 1.0) * lr + ai * li) / den
        ci = (ai * lr - (ar - 1.0) * li) / den
        ar_ref[...] = ar
        ai_ref[...] = ai
        for i in range(nb):
            q = i % 4
            rows = slice(BLOCK_ROWS * i, BLOCK_ROWS * (i + 1))
            br = _embed_block(btr_ref[rows, :], q)
            bi = _embed_block(bti_ref[rows, :], q)
            cri, cii = cr[i:i + 1, :], ci[i:i + 1, :]
            bbr_ref[i] = (cri * br - cii * bi).astype(BF16)
            bbi_ref[i] = (cri * bi + cii * br).astype(BF16)
            cpr_ref[i] = _embed_block(ctr_ref[rows, :], q).T.astype(BF16)
            cpi_ref[i] = _embed_block(cti_ref[rows, :], q).T.astype(BF16)

    w_shape = jax.ShapeDtypeStruct((nb, LANES, LANES), BF16)
    return pl.pallas_call(
        body, name="ssm_prep",
        out_shape=[jax.ShapeDtypeStruct((nb, LANES), F32), jax.ShapeDtypeStruct((nb, LANES), F32),
                   w_shape, w_shape, w_shape, w_shape],
        compiler_params=_cparams(),
    )(lam_re, lam_im, log_dt, bt_re, bt_im, c_re, c_im)


def _ssm_prep_bwd(lam_re, lam_im, log_dt, bt_re, bt_im, dar, dai, dbbr, dbbi, dcr, dci):
    nb = 2 * N_LANE_BLOCKS

    def body(lr_ref, li_ref, ldt_ref, btr_ref, bti_ref, dar_ref, dai_ref, dbbr_ref, dbbi_ref, dcr_ref, dci_ref,
             glr_ref, gli_ref, gdt_ref, gbr_ref, gbi_ref, gcre_ref, gcim_ref, gcr_s, gci_s):
        lam = lr_ref[...]
        lr = jnp.minimum(lam, LAMBDA_RE_MAX)
        li = li_ref[...]
        dt = jnp.exp(ldt_ref[...])
        mag = jnp.exp(lr * dt)
        cs = jnp.cos(li * dt)
        sn = jnp.sin(li * dt)
        ar = mag * cs
        ai = mag * sn
        den = lr * lr + li * li
        nr = (ar - 1.0) * lr + ai * li
        ni = ai * lr - (ar - 1.0) * li
        cr = nr / den
        ci = ni / den
        for i in range(nb):
            q = i % 4
            rows = slice(BLOCK_ROWS * i, BLOCK_ROWS * (i + 1))
            br = _embed_block(btr_ref[rows, :], q)
            bi = _embed_block(bti_ref[rows, :], q)
            gbbr = dbbr_ref[i]
            gbbi = dbbi_ref[i]
            cri, cii = cr[i:i + 1, :], ci[i:i + 1, :]
            gcr_s[i:i + 1, :] = jnp.sum(gbbr * br + gbbi * bi, axis=0, keepdims=True)
            gci_s[i:i + 1, :] = jnp.sum(gbbi * br - gbbr * bi, axis=0, keepdims=True)
            gbr_ref[rows, :] = _extract_block(cri * gbbr + cii * gbbi, q)
            gbi_ref[rows, :] = _extract_block(cri * gbbi - cii * gbbr, q)
            gcre_ref[rows, :] = _extract_block(dcr_ref[i].T, q)
            gcim_ref[rows, :] = _extract_block(dci_ref[i].T, q)
        g_cr = gcr_s[...]
        g_ci = gci_s[...]
        g_nr = g_cr / den
        g_ni = g_ci / den
        g_den = -(g_cr * nr + g_ci * ni) / (den * den)
        g_ar = dar_ref[...] + g_nr * lr - g_ni * li
        g_ai = dai_ref[...] + g_nr * li + g_ni * lr
        g_lr = g_nr * (ar - 1.0) + g_ni * ai + g_den * 2.0 * lr
        g_li = g_nr * ai - g_ni * (ar - 1.0) + g_den * 2.0 * li
        g_mag = g_ar * cs + g_ai * sn
        g_th = (g_ai * cs - g_ar * sn) * mag
        g_lr = g_lr + g_mag * mag * dt
        g_li = g_li + g_th * dt
        g_dt = g_mag * mag * lr + g_th * li
        glr_ref[...] = jnp.where(lam < LAMBDA_RE_MAX, g_lr, 0.0)
        gli_ref[...] = g_li
        gl = g_dt * dt
        half = LANES // 2
        gdt_ref[:, 0:1] = jnp.sum(gl[:, :half], axis=1, keepdims=True)
        gdt_ref[:, 1:2] = jnp.sum(gl[:, half:], axis=1, keepdims=True)

    rows_shape = jax.ShapeDtypeStruct((nb * BLOCK_ROWS, HALF_LANES), F32)
    return pl.pallas_call(
        body, name="ssm_prep_bwd",
        out_shape=[jax.ShapeDtypeStruct((nb, LANES), F32), jax.ShapeDtypeStruct((nb, LANES), F32),
                   jax.ShapeDtypeStruct((nb, 2), F32), rows_shape, rows_shape, rows_shape, rows_shape],
        scratch_shapes=[pltpu.VMEM((nb, LANES), F32), pltpu.VMEM((nb, LANES), F32)],
        compiler_params=_cparams(),
    )(lam_re, lam_im, log_dt, bt_re, bt_im, dar, dai, dbbr, dbbi, dcr, dci)


def _cmul(ar, ai, br, bi):
    return ar * br - ai * bi, ar * bi + ai * br


def _interleave_rows(src_ref, dst_ref):
    def step(j, carry):
        dst_ref[pl.ds(pl.multiple_of(j * 8, 8), 8), :] = src_ref[pl.ds(j, 8, stride=SCAN_CHUNK), :]
        return carry
    lax.fori_loop(0, SCAN_CHUNK, step, 0, unroll=4)


def _deinterleave_rows(src_ref, dst_ref):
    def step(j, carry):
        dst_ref[pl.ds(j, 8, stride=SCAN_CHUNK), :] = src_ref[pl.ds(pl.multiple_of(j * 8, 8), 8), :]
        return carry
    lax.fori_loop(0, SCAN_CHUNK, step, 0, unroll=4)


def _scan_inplace(re_ref, im_ref, a_re, a_im, reverse):
    nq = len(a_re)
    ch = SCAN_CHUNK
    ab_re = [jnp.broadcast_to(a, (8, LANES)) for a in a_re]
    ab_im = [jnp.broadcast_to(a, (8, LANES)) for a in a_im]

    def rows(j):
        jj = (ch - 1 - j) if reverse else j
        return pl.ds(pl.multiple_of(jj * 8, 8), 8)

    def sweep(init, store):
        def step(j, st):
            out = []
            r = rows(j)
            for qi in range(nq):
                xr, xi = st[2 * qi], st[2 * qi + 1]
                pr, pi = _cmul(ab_re[qi], ab_im[qi], xr, xi)
                xr = pr + re_ref[qi, r, :]
                xi = pi + im_ref[qi, r, :]
                if store:
                    re_ref[qi, r, :] = xr
                    im_ref[qi, r, :] = xi
                out += [xr, xi]
            return tuple(out)
        return lax.fori_loop(0, ch, step, tuple(init), unroll=2)

    zeros = [jnp.zeros((8, LANES), F32)] * (2 * nq)
    finals = sweep(zeros, store=False)

    row_id = lax.broadcasted_iota(jnp.int32, (8, LANES), 0)
    carries = []
    for qi in range(nq):
        pr, pi = ab_re[qi], ab_im[qi]
        for _ in range(8):
            pr, pi = _cmul(pr, pi, pr, pi)
        fr, fi = finals[2 * qi], finals[2 * qi + 1]
        sr = jnp.zeros((8, LANES), F32)
        si = jnp.zeros((8, LANES), F32)
        for _ in range(7):
            tr, ti = _cmul(pr, pi, sr, si)
            tr, ti = tr + fr, ti + fi
            if reverse:
                sr = jnp.where(row_id == 7, 0.0, pltpu.roll(tr, 7, axis=0))
                si = jnp.where(row_id == 7, 0.0, pltpu.roll(ti, 7, axis=0))
            else:
                sr = jnp.where(row_id == 0, 0.0, pltpu.roll(tr, 1, axis=0))
                si = jnp.where(row_id == 0, 0.0, pltpu.roll(ti, 1, axis=0))
        carries += [sr, si]
    sweep(carries, store=True)


SSM_Q = 4


def _ssm_fwd(u, are, aim, bbr, bbi, cre, cim, dskip):
    nq = SSM_Q

    def body(u_ref, ar_ref, ai_ref, bbr_ref, bbi_ref, cr_ref, ci_ref, d_ref, y_ref, xr_ref, xi_ref,
             sre, sim, up, yp):
        _interleave_rows(u_ref, up)
        uf = up[...]
        ub = uf.astype(BF16)
        yp[...] = d_ref[...] * uf
        for d in range(2):
            for qi in range(nq):
                sre[qi] = _dot(ub, bbr_ref[d, qi])
                sim[qi] = _dot(ub, bbi_ref[d, qi])
            _scan_inplace(sre, sim, [ar_ref[d, qi] for qi in range(nq)], [ai_ref[d, qi] for qi in range(nq)],
                          reverse=(d == 1))
            for qi in range(nq):
                xrb = sre[qi].astype(BF16)
                xib = sim[qi].astype(BF16)
                xr_ref[d, qi] = xrb
                xi_ref[d, qi] = xib
                yp[...] += _dot(xrb, cr_ref[d, qi]) - _dot(xib, ci_ref[d, qi])
        _deinterleave_rows(yp, y_ref)

    blk4 = lambda k: (0, k, 0, 0)
    return pl.pallas_call(
        body, name="ssm_fwd", grid=(SSM_WIDTH // LANES,),
        in_specs=[pl.BlockSpec((SEQ, LANES), lambda k: (0, k)),
                  pl.BlockSpec((2, nq, 1, LANES), blk4), pl.BlockSpec((2, nq, 1, LANES), blk4),
                  pl.BlockSpec((2, nq, LANES, LANES), blk4), pl.BlockSpec((2, nq, LANES, LANES), blk4),
                  pl.BlockSpec((2, nq, LANES, LANES), blk4), pl.BlockSpec((2, nq, LANES, LANES), blk4),
                  pl.BlockSpec((1, LANES), lambda k: (0, k))],
        out_specs=[pl.BlockSpec((SEQ, LANES), lambda k: (0, k)),
                   pl.BlockSpec((2, nq, SEQ, LANES), blk4), pl.BlockSpec((2, nq, SEQ, LANES), blk4)],
        out_shape=[jax.ShapeDtypeStruct((SEQ, SSM_WIDTH), F32),
                   jax.ShapeDtypeStruct((2, N_LANE_BLOCKS, SEQ, LANES), BF16),
                   jax.ShapeDtypeStruct((2, N_LANE_BLOCKS, SEQ, LANES), BF16)],
        scratch_shapes=[pltpu.VMEM((nq, SEQ, LANES), F32), pltpu.VMEM((nq, SEQ, LANES), F32),
                        pltpu.VMEM((SEQ, LANES), F32), pltpu.VMEM((SEQ, LANES), F32)],
        compiler_params=_cparams(("parallel",)),
    )(u, are, aim, bbr, bbi, cre, cim, dskip)


def _ssm_bwd(dy, u, xr, xi, are, aim, bbr, bbi, cre, cim, dskip, after=None):
    nq = SSM_Q
    body_rows = SEQ - 8
    deps = [] if after is None else [after]

    def body(dy_ref, u_ref, xr_ref, xi_ref, ar_ref, ai_ref, bbr_ref, bbi_ref, cr_ref, ci_ref, d_ref, *rest):
        (du_ref, dd_ref, dcr_ref, dci_ref, dbr_ref, dbi_ref, dar_ref, dai_ref,
         sre, sim, up, dyp, dup) = rest[len(deps):]
        _interleave_rows(u_ref, up)
        _interleave_rows(dy_ref, dyp)
        dyf = dyp[...]
        uf = up[...]
        dyb = dyf.astype(BF16)
        ub = uf.astype(BF16)
        dd_ref[...] = jnp.sum(dyf * uf, axis=0, keepdims=True)
        dup[...] = d_ref[...] * dyf
        row8 = lax.broadcasted_iota(jnp.int32, (8, LANES), 0)
        for d in range(2):
            for qi in range(nq):
                sre[qi] = _dot_nt(dyb, cr_ref[d, qi])
                sim[qi] = -_dot_nt(dyb, ci_ref[d, qi])
                dcr_ref[d, qi] = _dot_tn(xr_ref[d, qi], dyb)
                dci_ref[d, qi] = -_dot_tn(xi_ref[d, qi], dyb)
            _scan_inplace(sre, sim, [ar_ref[d, qi] for qi in range(nq)], [-ai_ref[d, qi] for qi in range(nq)],
                          reverse=(d == 0))
            for qi in range(nq):
                gr = sre[qi]
                gi = sim[qi]
                xrf = xr_ref[d, qi].astype(F32)
                xif = xi_ref[d, qi].astype(F32)
                if d == 0:
                    g_main_r, g_main_i = gr[8:], gi[8:]
                    x_main_r, x_main_i = xrf[:body_rows], xif[:body_rows]
                    g_edge_r, g_edge_i = gr[:8], gi[:8]
                    x_edge_r = jnp.where(row8 == 0, 0.0, pltpu.roll(xrf[body_rows:], 1, axis=0))
                    x_edge_i = jnp.where(row8 == 0, 0.0, pltpu.roll(xif[body_rows:], 1, axis=0))
                else:
                    g_main_r, g_main_i = gr[:body_rows], gi[:body_rows]
                    x_main_r, x_main_i = xrf[8:], xif[8:]
                    g_edge_r, g_edge_i = gr[body_rows:], gi[body_rows:]
                    x_edge_r = jnp.where(row8 == 7, 0.0, pltpu.roll(xrf[:8], 7, axis=0))
                    x_edge_i = jnp.where(row8 == 7, 0.0, pltpu.roll(xif[:8], 7, axis=0))
                dar_ref[d, qi] = (jnp.sum(g_main_r * x_main_r + g_main_i * x_main_i, axis=0, keepdims=True)
                                  + jnp.sum(g_edge_r * x_edge_r + g_edge_i * x_edge_i, axis=0, keepdims=True))
                dai_ref[d, qi] = (jnp.sum(g_main_i * x_main_r - g_main_r * x_main_i, axis=0, keepdims=True)
                                  + jnp.sum(g_edge_i * x_edge_r - g_edge_r * x_edge_i, axis=0, keepdims=True))
                grb = gr.astype(BF16)
                gib = gi.astype(BF16)
                dup[...] += _dot_nt(grb, bbr_ref[d, qi]) + _dot_nt(gib, bbi_ref[d, qi])
                dbr_ref[d, qi] = _dot_tn(ub, grb)
                dbi_ref[d, qi] = _dot_tn(ub, gib)
        _deinterleave_rows(dup, du_ref)

    blk4 = lambda k: (0, k, 0, 0)
    col = lambda k: (0, k)
    w_spec = pl.BlockSpec((2, nq, LANES, LANES), blk4)
    a_spec = pl.BlockSpec((2, nq, 1, LANES), blk4)
    x_spec = pl.BlockSpec((2, nq, SEQ, LANES), blk4)
    w_shape = jax.ShapeDtypeStruct((2, N_LANE_BLOCKS, LANES, LANES), F32)
    a_shape = jax.ShapeDtypeStruct((2, N_LANE_BLOCKS, 1, LANES), F32)
    return pl.pallas_call(
        body, name="ssm_bwd", grid=(SSM_WIDTH // LANES,),
        in_specs=[pl.BlockSpec((SEQ, LANES), col), pl.BlockSpec((SEQ, LANES), col), x_spec, x_spec,
                  a_spec, a_spec, w_spec, w_spec, w_spec, w_spec, pl.BlockSpec((1, LANES), col)]
        + [pl.BlockSpec(memory_space=pl.ANY)] * len(deps),
        out_specs=[pl.BlockSpec((SEQ, LANES), col), pl.BlockSpec((1, LANES), col),
                   w_spec, w_spec, w_spec, w_spec, a_spec, a_spec],
        out_shape=[jax.ShapeDtypeStruct((SEQ, SSM_WIDTH), F32), jax.ShapeDtypeStruct((1, SSM_WIDTH), F32),
                   w_shape, w_shape, w_shape, w_shape, a_shape, a_shape],
        scratch_shapes=[pltpu.VMEM((nq, SEQ, LANES), F32), pltpu.VMEM((nq, SEQ, LANES), F32),
                        pltpu.VMEM((SEQ, LANES), F32), pltpu.VMEM((SEQ, LANES), F32), pltpu.VMEM((SEQ, LANES), F32)],
        compiler_params=_cparams(("parallel",)),
    )(dy, u, xr, xi, are, aim, bbr, bbi, cre, cim, dskip, *deps)


GELU_C = 0.7978845608028654
GELU_K = 0.044715


def _gelu(y):
    return 0.5 * y * (1.0 + jnp.tanh(GELU_C * (y + GELU_K * y * y * y)))


def _gelu_grad(y):
    t = jnp.tanh(GELU_C * (y + GELU_K * y * y * y))
    return 0.5 * (1.0 + t) + 0.5 * y * (1.0 - t * t) * GELU_C * (1.0 + 3.0 * GELU_K * y * y)


def _mixout_fwd(o, y, glu_w, glu_b, gan, gsn, wout, x1):
    tm = MIX_TM

    def body(o_ref, y_ref, gw_ref, gb_ref, gan_ref, gsn_ref, w_ref, x1_ref, x2_ref, mx_ref):
        yg = _gelu(y_ref[...])
        z = _dot(yg.astype(BF16), gw_ref[...]) + gb_ref[...]
        so = yg * _sigmoid(z)
        na = _rms_fwd(o_ref[...], gan_ref[...])
        ns = _rms_fwd(so, gsn_ref[...])
        mixed = jnp.concatenate([na, ns], axis=-1).astype(BF16)
        mx_ref[...] = mixed
        x2_ref[...] = x1_ref[...] + _dot(mixed, w_ref[...])

    row = lambda i: (i, 0)
    const = lambda i: (0, 0)
    return pl.pallas_call(
        body, name="mixout_fwd", grid=(SEQ // tm,),
        in_specs=[pl.BlockSpec((tm, ATTN_WIDTH), row), pl.BlockSpec((tm, SSM_WIDTH), row),
                  pl.BlockSpec((SSM_WIDTH, SSM_WIDTH), const), pl.BlockSpec((1, SSM_WIDTH), const),
                  pl.BlockSpec((1, ATTN_WIDTH), const), pl.BlockSpec((1, SSM_WIDTH), const),
                  pl.BlockSpec((D_MODEL, D_MODEL), const), pl.BlockSpec((tm, D_MODEL), row)],
        out_specs=[pl.BlockSpec((tm, D_MODEL), row), pl.BlockSpec((tm, D_MODEL), row)],
        out_shape=[jax.ShapeDtypeStruct((SEQ, D_MODEL), F32), jax.ShapeDtypeStruct((SEQ, D_MODEL), BF16)],
        compiler_params=_cparams(("parallel",)),
    )(o, y, glu_w, glu_b, gan, gsn, wout, x1)


def _mixout_bwd(dx2, o, y, glu_w, glu_b, gan, gsn, wout):
    tm = MIX_TM

    def body(dx2_ref, o_ref, y_ref, gw_ref, gb_ref, gan_ref, gsn_ref, w_ref,
             do_ref, dy_ref, dz_ref, yg_ref, dxb_ref, dgan_ref, dgsn_ref, dgb_ref):
        i = pl.program_id(0)
        dxb = dx2_ref[...].astype(BF16)
        dxb_ref[...] = dxb
        dmixed = _dot_nt(dxb, w_ref[...])
        do, dgan = _rms_bwd(dmixed[:, :ATTN_WIDTH], o_ref[...], gan_ref[...])
        do_ref[...] = do
        yv = y_ref[...]
        yg = _gelu(yv)
        ygb = yg.astype(BF16)
        yg_ref[...] = ygb
        sg = _sigmoid(_dot(ygb, gw_ref[...]) + gb_ref[...])
        dso, dgsn = _rms_bwd(dmixed[:, ATTN_WIDTH:], yg * sg, gsn_ref[...])
        dz = dso * yg * sg * (1.0 - sg)
        dzb = dz.astype(BF16)
        dz_ref[...] = dzb
        dyg = dso * sg + _dot_nt(dzb, gw_ref[...])
        dy_ref[...] = dyg * _gelu_grad(yv)
        dgb = jnp.sum(dz, axis=0, keepdims=True)

        @pl.when(i == 0)
        def _():
            dgan_ref[...] = dgan
            dgsn_ref[...] = dgsn
            dgb_ref[...] = dgb

        @pl.when(i != 0)
        def _():
            dgan_ref[...] += dgan
            dgsn_ref[...] += dgsn
            dgb_ref[...] += dgb

    row = lambda i: (i, 0)
    const = lambda i: (0, 0)
    return pl.pallas_call(
        body, name="mixout_bwd", grid=(SEQ // tm,),
        in_specs=[pl.BlockSpec((tm, D_MODEL), row), pl.BlockSpec((tm, ATTN_WIDTH), row),
                  pl.BlockSpec((tm, SSM_WIDTH), row),
                  pl.BlockSpec((SSM_WIDTH, SSM_WIDTH), const), pl.BlockSpec((1, SSM_WIDTH), const),
                  pl.BlockSpec((1, ATTN_WIDTH), const), pl.BlockSpec((1, SSM_WIDTH), const),
                  pl.BlockSpec((D_MODEL, D_MODEL), const)],
        out_specs=[pl.BlockSpec((tm, ATTN_WIDTH), row), pl.BlockSpec((tm, SSM_WIDTH), row),
                   pl.BlockSpec((tm, SSM_WIDTH), row), pl.BlockSpec((tm, SSM_WIDTH), row),
                   pl.BlockSpec((tm, D_MODEL), row),
                   pl.BlockSpec((1, ATTN_WIDTH), const), pl.BlockSpec((1, SSM_WIDTH), const),
                   pl.BlockSpec((1, SSM_WIDTH), const)],
        out_shape=[jax.ShapeDtypeStruct((SEQ, ATTN_WIDTH), F32), jax.ShapeDtypeStruct((SEQ, SSM_WIDTH), F32),
                   jax.ShapeDtypeStruct((SEQ, SSM_WIDTH), BF16), jax.ShapeDtypeStruct((SEQ, SSM_WIDTH), BF16),
                   jax.ShapeDtypeStruct((SEQ, D_MODEL), BF16),
                   jax.ShapeDtypeStruct((1, ATTN_WIDTH), F32), jax.ShapeDtypeStruct((1, SSM_WIDTH), F32),
                   jax.ShapeDtypeStruct((1, SSM_WIDTH), F32)],
        compiler_params=_cparams(("arbitrary",)),
    )(dx2, o, y, glu_w, glu_b, gan, gsn, wout)


def _loss_head(x, g, target):
    tm = MIX_TM

    def body(x_ref, g_ref, t_ref, loss_ref, dx_ref, dg_ref):
        i = pl.program_id(0)
        xv = x_ref[...]
        gv = g_ref[...]
        err = _rms_fwd(xv, gv) - t_ref[...]
        part = jnp.broadcast_to(0.5 * jnp.sum(err * err) / D_MODEL, (1, LANES))
        dx, dg = _rms_bwd(err * (1.0 / D_MODEL), xv, gv)
        dx_ref[...] = dx

        @pl.when(i == 0)
        def _():
            loss_ref[...] = part
            dg_ref[...] = dg

        @pl.when(i != 0)
        def _():
            loss_ref[...] += part
            dg_ref[...] += dg

    row = lambda i: (i, 0)
    const = lambda i: (0, 0)
    return pl.pallas_call(
        body, name="loss_head", grid=(SEQ // tm,),
        in_specs=[pl.BlockSpec((tm, D_MODEL), row), pl.BlockSpec((1, D_MODEL), const),
                  pl.BlockSpec((tm, D_MODEL), row)],
        out_specs=[pl.BlockSpec((1, LANES), const), pl.BlockSpec((tm, D_MODEL), row),
                   pl.BlockSpec((1, D_MODEL), const)],
        out_shape=[jax.ShapeDtypeStruct((1, LANES), F32), jax.ShapeDtypeStruct((SEQ, D_MODEL), F32),
                   jax.ShapeDtypeStruct((1, D_MODEL), F32)],
        compiler_params=_cparams(("arbitrary",)),
    )(x, g, target)


def _local_step(x, target, w, p, late_weights, early_grads, after=None):
    x1, h1, a1, b1 = _ffn_fwd(x, p["norm_ffn1"], w["wgt1"], w["wut1"], w["wd1"], "ffn1_fwd", after=after)
    h2, q, k, v, u = _mixin_fwd(x1, p["norm_mix"], w["wint"])
    kp = jnp.pad(k, ((WINDOW, WINDOW), (0, 0)))
    vp = jnp.pad(v, ((WINDOW, WINDOW), (0, 0)))
    o = _attn_fwd(q, kp, vp, p["attn_sinks"])

    lam_re = p["ssm_lambda_re"].reshape(2 * N_LANE_BLOCKS, LANES)
    lam_im = p["ssm_lambda_im"].reshape(2 * N_LANE_BLOCKS, LANES)
    log_dt = jnp.repeat(p["ssm_log_dt"].reshape(2, 32), 64, axis=-1).reshape(2 * N_LANE_BLOCKS, LANES)
    a_re, a_im, bbr, bbi, cre, cim = _ssm_prep(lam_re, lam_im, log_dt, p["ssm_b_re"], p["ssm_b_im"],
                                               p["ssm_c_re"], p["ssm_c_im"])
    shape_a = (2, N_LANE_BLOCKS, 1, LANES)
    shape_w = (2, N_LANE_BLOCKS, LANES, LANES)
    a_re4, a_im4 = a_re.reshape(shape_a), a_im.reshape(shape_a)
    bbr4, bbi4 = bbr.reshape(shape_w), bbi.reshape(shape_w)
    cre, cim = cre.reshape(shape_w), cim.reshape(shape_w)
    dskip = p["ssm_d"].reshape(1, SSM_WIDTH)
    y, xr, xi = _ssm_fwd(u, a_re4, a_im4, bbr4, bbi4, cre, cim, dskip)

    w2 = late_weights(y)
    x2, mixed = _mixout_fwd(o, y, w2["glu"], p["ssm_glu_b"], p["attn_out_norm"], p["ssm_out_norm"], w2["wout"], x1)
    x3, h3, a3, b3 = _ffn_fwd(x2, p["norm_ffn2"], w2["wgt2"], w2["wut2"], w2["wd2"], "ffn2_fwd")

    loss, dx3, d_final = _loss_head(x3, p["final_norm"], target)
    dx2, da3, db3, s3, df3, d_n2 = _ffn_bwd_act(dx3, x2, p["norm_ffn2"], a3, b3, w2["wgt2"], w2["wut2"], w2["wd2"],
                                                "ffn2_bwd_act")
    g_wgt2, g_wut2, g_wd2 = _mm_tn([(da3, h3), (db3, h3), (s3, df3)], "ffn2_bwd_w")

    do, dy, dz, ygb, dx2b, d_gan, d_gsn, d_glub = _mixout_bwd(
        dx2, o, y, w2["glu"], p["ssm_glu_b"], p["attn_out_norm"], p["ssm_out_norm"], w2["wout"])
    (g_wout,) = _mm_tn([(mixed, dx2b)], "wout_bwd_w")
    (g_glu,) = _mm_tn([(ygb, dz)], "glu_bwd_w")
    sent = early_grads(dict(glu=g_glu, wout=g_wout, wgt2=g_wgt2, wut2=g_wut2, wd2=g_wd2))

    du, d_dskip, dcre, dcim, dbbr, dbbi, dar, dai = _ssm_bwd(dy, u, xr, xi, a_re4, a_im4, bbr4, bbi4, cre, cim, dskip,
                                                             after=sent)
    nb = 2 * N_LANE_BLOCKS
    blocks3 = (nb, LANES, LANES)
    g_lre, g_lim, g_ldt, g_btr, g_bti, g_cre, g_cim = _ssm_prep_bwd(
        lam_re, lam_im, log_dt, p["ssm_b_re"], p["ssm_b_im"], dar.reshape(nb, LANES), dai.reshape(nb, LANES),
        dbbr.reshape(blocks3), dbbi.reshape(blocks3), dcre.reshape(blocks3), dcim.reshape(blocks3))

    dq, dkp, dvp, d_sinks = _attn_bwd(q, kp, vp, p["attn_sinks"], do)
    dk = dkp[WINDOW:WINDOW + SEQ]
    dv = dvp[WINDOW:WINDOW + SEQ]
    dx1, dproj, d_nmix = _mixin_bwd(dq, dk, dv, du, w["wint"], x1, p["norm_mix"], dx2)
    (g_wint,) = _mm_tn([(dproj, h2)], "win_bwd_w")

    dx0, da1, db1, s1, df1, d_n1 = _ffn_bwd_act(dx1, x, p["norm_ffn1"], a1, b1, w["wgt1"], w["wut1"], w["wd1"],
                                                "ffn1_bwd_act")
    g_wgt1, g_wut1, g_wd1 = _mm_tn([(da1, h1), (db1, h1), (s1, df1)], "ffn1_bwd_w")

    big = dict(wgt1=g_wgt1, wut1=g_wut1, wd1=g_wd1, wint=g_wint)
    small = dict(
        norm_ffn1=d_n1, norm_mix=d_nmix, attn_sinks=d_sinks,
        ssm_lambda_re=g_lre.reshape(64, 64), ssm_lambda_im=g_lim.reshape(64, 64),
        ssm_log_dt=g_ldt.reshape(2, 32), ssm_b_re=g_btr, ssm_b_im=g_bti, ssm_c_re=g_cre, ssm_c_im=g_cim,
        ssm_d=d_dskip.reshape(32, 16), ssm_glu_b=d_glub, attn_out_norm=d_gan, ssm_out_norm=d_gsn,
        norm_ffn2=d_n2, final_norm=d_final)
    return loss, dx0, big, small


BIG = dict(
    wgt1=("ffn1_w_gate", 352, 1024, True), wut1=("ffn1_w_up", 352, 1024, True), wd1=("ffn1_w_down", 352, 1024, False),
    wint=("w_in", 160, 1024, True), glu=("ssm_glu_w", 64, 512, False), wout=("w_out", 128, 1024, False),
    wgt2=("ffn2_w_gate", 352, 1024, True), wut2=("ffn2_w_up", 352, 1024, True), wd2=("ffn2_w_down", 352, 1024, False))

SMALL = dict(
    norm_ffn1=(1, 1024), norm_mix=(1, 1024), attn_sinks=(1, 8), ssm_lambda_re=(64, 64), ssm_lambda_im=(64, 64),
    ssm_log_dt=(2, 32), ssm_b_re=(1024, 64), ssm_b_im=(1024, 64), ssm_c_re=(1024, 64), ssm_c_im=(1024, 64),
    ssm_d=(32, 16), ssm_glu_b=(1, 512), attn_out_norm=(1, 512), ssm_out_norm=(1, 512), norm_ffn2=(1, 1024),
    final_norm=(1, 1024))
SMALL_TRANSPOSED = ("ssm_b_re", "ssm_b_im")

SMALL_PAIRS = (("ssm_lambda_re", "ssm_lambda_im"), ("ssm_c_re", "ssm_c_im"), ("ssm_b_re", "ssm_b_im"))
SMALL_VECS = ("norm_ffn1", "norm_mix", "norm_ffn2", "final_norm", "ssm_glu_b", "attn_out_norm", "ssm_out_norm")
SMALL_TILES = ("ssm_log_dt", "attn_sinks", "ssm_d")


def _small_offsets():
    off, table = 0, {}
    for re, im in SMALL_PAIRS:
        table[re] = table[im] = off
        off += SMALL[re][0]
    for n in SMALL_VECS:
        table[n] = off
        off += SMALL[n][1] // LANES
    for n in SMALL_TILES:
        off = -(-off // 8) * 8
        table[n] = off
        off += SMALL[n][0]
    return table, off


SMALL_OFFSET, SMALL_USED_ROWS = _small_offsets()
SMALL_ROWS = -(-SMALL_USED_ROWS // (8 * N_DEV)) * 8 * N_DEV


def _pad_to(n, mult):
    return -(-n // mult) * mult


def _transpose_2d(x, rows_out, cols_out):
    r_in, c_in = x.shape
    rp, cp = _pad_to(r_in, LANES), _pad_to(c_in, LANES)
    if cp != c_in:
        x = jnp.concatenate([x, jnp.zeros((r_in, cp - c_in), x.dtype)], axis=1)
    if rp != r_in:
        x = jnp.concatenate([x, jnp.zeros((rp - r_in, cp), x.dtype)], axis=0)
    return x.T[:rows_out, :cols_out]


def _cast_shards(shards):
    names = list(BIG)

    def body(*refs):
        ins, outs = refs[:len(names)], refs[len(names):]
        for idx, n in enumerate(names):
            _, rows, cols, transposed = BIG[n]
            v = ins[idx][...]
            if transposed:
                v = _transpose_2d(v, rows, cols)
            outs[idx][...] = v.astype(BF16)

    return pl.pallas_call(
        body, name="cast_shards",
        out_shape=[jax.ShapeDtypeStruct((BIG[n][1], BIG[n][2]), BF16) for n in names],
        compiler_params=_cparams(),
    )(*[shards[n] for n in names])


def _peer(x, y, c, r):
    px = 1 - x if r & 4 else x
    py = 1 - y if r & 2 else y
    pc = 1 - c if r & 1 else c
    return px, py, pc


FIRST_GROUP = ("wgt1", "wut1", "wd1", "wint")
LATE_GROUP = ("glu", "wout", "wgt2", "wut2", "wd2")
N_PEERS = N_DEV - 1
ANY_SPEC = pl.BlockSpec(memory_space=pl.ANY)
HBM_SPEC = pl.BlockSpec(memory_space=pltpu.HBM)
SEM_SPEC = pl.BlockSpec(memory_space=pltpu.SEMAPHORE)
DATAFLOW_EFFECT = pltpu.SideEffectType.DATAFLOW_SIDE_EFFECTING


def _mesh_pos():
    x, y, c = lax.axis_index("x"), lax.axis_index("y"), lax.axis_index("c")
    return x, y, c, 4 * x + 2 * y + c


def _gather_first(first, late):
    nf, nl = len(first), len(late)

    def body(*refs):
        f_in, l_in = refs[:nf], refs[nf:nf + nl]
        f_out, l_out = refs[nf + nl:2 * nf + nl], refs[2 * nf + nl:2 * (nf + nl)]
        send_sems, recv_sems, local_sems = refs[2 * (nf + nl):]
        x, y, c, me = _mesh_pos()
        sibling = (x, y, 1 - c)
        chips = [(x, 1 - y), (1 - x, y), (1 - x, 1 - y)]

        def idx(px, py, pc):
            return 4 * px + 2 * py + pc

        def copy(k, s, block, to, src=None):
            slot = f_out[k].at[block]
            return pltpu.make_async_remote_copy(
                src_ref=slot if src is None else src, dst_ref=slot, send_sem=send_sems.at[k, s],
                recv_sem=recv_sems.at[k, s], device_id=to, device_id_type=MESH_ID)

        local = []
        for k in range(nf + nl):
            src, dst = (f_in[k], f_out[k]) if k < nf else (l_in[k - nf], l_out[k - nf])
            mine = pltpu.make_async_copy(src, dst.at[me], local_sems.at[k])
            mine.start()
            local.append(mine)
        sends = []
        for j, chip in enumerate(chips):
            for k in range(nf):
                sends.append(copy(k, 1 + j, me, (*chip, c), src=f_in[k]))
                sends[-1].start()
        for k in range(nf):
            sends.append(copy(k, 0, me, sibling, src=f_in[k]))
            sends[-1].start()
        for j, chip in enumerate(chips):
            for k in range(nf):
                copy(k, 1 + j, idx(*chip, c), (*chip, c)).wait_recv()
                sends.append(copy(k, 4 + j, idx(*chip, c), sibling))
                sends[-1].start()
        for k in range(nf):
            copy(k, 0, idx(*sibling), sibling).wait_recv()
        for j, chip in enumerate(chips):
            for k in range(nf):
                copy(k, 4 + j, idx(*chip, 1 - c), sibling).wait_recv()
        for cp in sends:
            cp.wait_send()
        for cp in local:
            cp.wait()

    return pl.pallas_call(
        body, name="gather_first",
        in_specs=[ANY_SPEC] * (nf + nl), out_specs=[ANY_SPEC] * (nf + nl),
        out_shape=[jax.ShapeDtypeStruct((N_DEV,) + s.shape, s.dtype) for s in list(first) + list(late)],
        scratch_shapes=[pltpu.SemaphoreType.DMA((nf, N_PEERS)), pltpu.SemaphoreType.DMA((nf, N_PEERS)),
                        pltpu.SemaphoreType.DMA((nf + nl,))],
        compiler_params=pltpu.CompilerParams(has_side_effects=True),
    )(*first, *late)


def _split_copy(src_refs, land_refs, send_sems, recv_sems, k, r, pos, scatter, receiving):
    x, y, c, me = pos
    px, py, pc = _peer(x, y, c, r)
    peer_idx = 4 * px + 2 * py + pc
    if scatter:
        src, dst = src_refs[k].at[peer_idx], land_refs[k].at[r - 1]
    else:
        src, dst = src_refs[k], land_refs[k].at[peer_idx if receiving else me]
    return pltpu.make_async_remote_copy(
        src_ref=src, dst_ref=dst, send_sem=send_sems.at[k * N_PEERS + r - 1],
        recv_sem=recv_sems.at[k * N_PEERS + r - 1], device_id=(px, py, pc), device_id_type=MESH_ID)


def _split_start(name, srcs, lands, scatter):
    n = len(srcs)

    def body(*refs):
        src_refs, land_refs = refs[:n], refs[n:2 * n]
        send_sems, recv_sems = refs[2 * n], refs[2 * n + 1]
        token = refs[-1]
        pos = _mesh_pos()
        for k in range(n):
            for r in range(1, N_DEV):
                _split_copy(src_refs, land_refs, send_sems, recv_sems, k, r, pos, scatter, False).start()
        token[...] = jnp.zeros_like(token)

    thru = [pltpu.HBM(a.shape, a.dtype) for a in list(srcs) + list(lands)]
    outs = pl.pallas_call(
        body, name=name,
        in_specs=[HBM_SPEC] * (2 * n),
        out_specs=[SEM_SPEC, SEM_SPEC] + [HBM_SPEC] * (2 * n) + [pl.BlockSpec(memory_space=pltpu.VMEM)],
        out_shape=[pltpu.SemaphoreType.DMA((n * N_PEERS,)), pltpu.SemaphoreType.DMA((n * N_PEERS,))] + thru
        + [jax.ShapeDtypeStruct((8, LANES), F32)],
        input_output_aliases={i: 2 + i for i in range(2 * n)},
        compiler_params=pltpu.CompilerParams(has_side_effects=DATAFLOW_EFFECT),
    )(*[pltpu.with_memory_space_constraint(a, pltpu.HBM) for a in list(srcs) + list(lands)])
    return outs[0], outs[1], outs[2:2 + n], outs[2 + n:2 + 2 * n], outs[-1]


def _split_wait(name, send_sems, recv_sems, srcs, lands, scatter, after):
    n = len(srcs)

    def body(*refs):
        src_refs, land_refs = refs[:n], refs[n:2 * n]
        send, recv = refs[2 * n], refs[2 * n + 1]
        pos = _mesh_pos()
        for k in range(n):
            for r in range(1, N_DEV):
                cp = _split_copy(src_refs, land_refs, send, recv, k, r, pos, scatter, True)
                cp.wait_send()
                cp.wait_recv()

    thru = [pltpu.HBM(a.shape, a.dtype) for a in list(srcs) + list(lands)]
    outs = pl.pallas_call(
        body, name=name,
        in_specs=[HBM_SPEC] * (2 * n) + [SEM_SPEC, SEM_SPEC, ANY_SPEC],
        out_specs=[HBM_SPEC] * (2 * n), out_shape=thru,
        input_output_aliases={i: i for i in range(2 * n)},
        compiler_params=pltpu.CompilerParams(has_side_effects=DATAFLOW_EFFECT),
    )(*srcs, *lands, send_sems, recv_sems, after)
    return outs[:n], outs[n:]


def _exchange_last(grads, small_packed):
    ng = len(grads)
    ch = SMALL_ROWS // N_DEV
    max_rows = max(g.shape[1] for g in grads)
    cols = grads[0].shape[2]

    def body(*refs):
        g_in, s_in = refs[:ng], refs[ng]
        outs = refs[ng + 1:]
        own_out, land, stage = outs[:ng], outs[ng:2 * ng], outs[2 * ng:3 * ng]
        s_red, s_stage = outs[3 * ng], outs[3 * ng + 1]
        (va, vb, vo, vs, sm_in, sm_out, d2d_send, d2d_recv, ici_send, ici_recv, s1_send, s1_recv, s2_send, s2_recv,
         local_sems) = outs[3 * ng + 2:]
        x, y, c, me = _mesh_pos()
        sibling = (x, y, 1 - c)
        chips = [(x, y), (x, 1 - y), (1 - x, y), (1 - x, 1 - y)]

        def idx(chip, core):
            return 4 * chip[0] + 2 * chip[1] + core

        def d2d(k, j):
            return pltpu.make_async_remote_copy(
                src_ref=g_in[k].at[idx(chips[j], 1 - c)], dst_ref=stage[k].at[j], send_sem=d2d_send.at[k, j],
                recv_sem=d2d_recv.at[k, j], device_id=sibling, device_id_type=MESH_ID)

        def ici(k, j, slot):
            rows = g_in[k].shape[1]
            return pltpu.make_async_remote_copy(
                src_ref=vo.at[slot, pl.ds(0, rows)], dst_ref=land[k].at[j - 1], send_sem=ici_send.at[k, j - 1],
                recv_sem=ici_recv.at[k, j - 1], device_id=(*chips[j], c), device_id_type=MESH_ID)

        def small_scatter(r):
            px, py, pc = _peer(x, y, c, r)
            return pltpu.make_async_remote_copy(
                src_ref=s_in.at[pl.ds(pl.multiple_of((4 * px + 2 * py + pc) * ch, 8), ch)], dst_ref=s_stage.at[me],
                send_sem=s1_send.at[r - 1], recv_sem=s1_recv.at[r - 1], device_id=(px, py, pc), device_id_type=MESH_ID)

        def small_gather(r):
            return pltpu.make_async_remote_copy(
                src_ref=sm_out, dst_ref=s_red.at[me], send_sem=s2_send.at[r - 1], recv_sem=s2_recv.at[r - 1],
                device_id=_peer(x, y, c, r), device_id_type=MESH_ID)

        for r in range(1, N_DEV):
            small_scatter(r).start()
        mine = pltpu.make_async_copy(s_in.at[pl.ds(pl.multiple_of(me * ch, 8), ch)], s_stage.at[me], local_sems.at[0])
        mine.start()
        for j in (1, 2, 3, 0):
            for k in range(ng):
                d2d(k, j).start()

        for r in range(1, N_DEV):
            small_scatter(r).wait_recv()
        mine.wait()
        load = pltpu.make_async_copy(s_stage, sm_in, local_sems.at[1])
        load.start()
        load.wait()
        total = sm_in[0]
        for i in range(1, N_DEV):
            total = total + sm_in[i]
        sm_out[...] = total
        for r in range(1, N_DEV):
            small_gather(r).start()
        keep = pltpu.make_async_copy(sm_out, s_red.at[me], local_sems.at[2])
        keep.start()

        pairs = [(k, j) for j in (1, 2, 3, 0) for k in range(ng)]
        in_flight = {}
        for i, (k, j) in enumerate(pairs):
            slot = i % 2
            rows = g_in[k].shape[1]
            if slot in in_flight:
                in_flight.pop(slot).wait_send()
            d2d(k, j).wait_recv()
            la = pltpu.make_async_copy(g_in[k].at[idx(chips[j], c)], va.at[slot, pl.ds(0, rows)], local_sems.at[3])
            lb = pltpu.make_async_copy(stage[k].at[j], vb.at[slot, pl.ds(0, rows)], local_sems.at[4])
            la.start()
            lb.start()
            la.wait()
            lb.wait()
            total = va[slot, pl.ds(0, rows)].astype(F32) + vb[slot, pl.ds(0, rows)].astype(F32)
            if j == 0:
                vs[pl.ds(0, rows)] = total
                st = pltpu.make_async_copy(vs.at[pl.ds(0, rows)], own_out[k], local_sems.at[5])
                st.start()
                st.wait()
            else:
                vo[slot, pl.ds(0, rows)] = total.astype(BF16)
                cp = ici(k, j, slot)
                cp.start()
                in_flight[slot] = cp
        for cp in in_flight.values():
            cp.wait_send()

        for j in (1, 2, 3, 0):
            for k in range(ng):
                d2d(k, j).wait_send()
        for j in (1, 2, 3):
            for k in range(ng):
                ici(k, j, 0).wait_recv()
        for r in range(1, N_DEV):
            small_scatter(r).wait_send()
            small_gather(r).wait_send()
            small_gather(r).wait_recv()
        keep.wait()

    out_shape = [jax.ShapeDtypeStruct(g.shape[1:], F32) for g in grads]
    out_shape += [jax.ShapeDtypeStruct((3,) + g.shape[1:], BF16) for g in grads]
    out_shape += [jax.ShapeDtypeStruct((4,) + g.shape[1:], BF16) for g in grads]
    out_shape += [jax.ShapeDtypeStruct((N_DEV, ch, LANES), F32), jax.ShapeDtypeStruct((N_DEV, ch, LANES), F32)]
    outs = pl.pallas_call(
        body, name="exchange_last",
        in_specs=[ANY_SPEC] * (ng + 1), out_specs=[ANY_SPEC] * len(out_shape), out_shape=out_shape,
        scratch_shapes=[pltpu.VMEM((2, max_rows, cols), BF16), pltpu.VMEM((2, max_rows, cols), BF16),
                        pltpu.VMEM((2, max_rows, cols), BF16), pltpu.VMEM((max_rows, cols), F32),
                        pltpu.VMEM((N_DEV, ch, LANES), F32), pltpu.VMEM((ch, LANES), F32),
                        pltpu.SemaphoreType.DMA((ng, 4)), pltpu.SemaphoreType.DMA((ng, 4)),
                        pltpu.SemaphoreType.DMA((ng, 3)), pltpu.SemaphoreType.DMA((ng, 3)),
                        pltpu.SemaphoreType.DMA((N_PEERS,)), pltpu.SemaphoreType.DMA((N_PEERS,)),
                        pltpu.SemaphoreType.DMA((N_PEERS,)), pltpu.SemaphoreType.DMA((N_PEERS,)),
                        pltpu.SemaphoreType.DMA((6,))],
        compiler_params=pltpu.CompilerParams(has_side_effects=True, vmem_limit_bytes=VMEM_LIMIT),
    )(*grads, small_packed)
    return outs[:ng], outs[ng:2 * ng], outs[3 * ng].reshape(SMALL_ROWS, LANES)


def _adamw_math(w, g, m, v):
    m2 = ADAM_B1 * m + (1.0 - ADAM_B1) * g
    v2 = ADAM_B2 * v + (1.0 - ADAM_B2) * (g * g)
    m_hat = m2 / (1.0 - ADAM_B1 ** ADAM_STEP)
    v_hat = v2 / (1.0 - ADAM_B2 ** ADAM_STEP)
    delta = -ADAM_LR * (m_hat / (jnp.sqrt(v_hat) + ADAM_EPS) + ADAM_WD * w)
    return delta, m2, v2


def _adamw_big(own, parts, w, m, v, name, transposed):
    shape = w.shape
    own_is_blocks = own.ndim == 3

    def body(own_ref, p_ref, w_ref, m_ref, v_ref, g_ref, d_ref, m2_ref, v2_ref, own_s, sem):
        if own_is_blocks:
            cp = pltpu.make_async_copy(own_ref.at[_mesh_pos()[3]], own_s, sem)
        else:
            cp = pltpu.make_async_copy(own_ref, own_s, sem)
        cp.start()
        cp.wait()
        g = own_s[...].astype(F32)
        for i in range(parts.shape[0]):
            g = g + p_ref[i].astype(F32)
        if transposed:
            g = _transpose_2d(g, shape[0], shape[1])
        delta, m2, v2 = _adamw_math(w_ref[...], g, m_ref[...], v_ref[...])
        g_ref[...] = g
        d_ref[...] = delta
        m2_ref[...] = m2
        v2_ref[...] = v2

    vmem = pl.BlockSpec(memory_space=pltpu.VMEM)
    return pl.pallas_call(
        body, name=name, in_specs=[ANY_SPEC, vmem, vmem, vmem, vmem], out_specs=[vmem] * 4,
        out_shape=[jax.ShapeDtypeStruct(shape, F32)] * 4,
        scratch_shapes=[pltpu.VMEM(own.shape[-2:], own.dtype), pltpu.SemaphoreType.DMA(())],
        compiler_params=_cparams(),
    )(own, parts, w, m, v)


def _pack_small(grads):
    names = list(SMALL)

    def body(*refs):
        ins, out = dict(zip(names, refs[:-1])), refs[-1]
        out[...] = jnp.zeros_like(out)
        for re, im in SMALL_PAIRS:
            off, rows = SMALL_OFFSET[re], SMALL[re][0]
            out[off:off + rows, :] = jnp.concatenate([ins[re][...], ins[im][...]], axis=1)
        for n in SMALL_VECS:
            off, vec = SMALL_OFFSET[n], ins[n][...]
            for i in range(SMALL[n][1] // LANES):
                out[off + i:off + i + 1, :] = vec[:, i * LANES:(i + 1) * LANES]
        for n in SMALL_TILES:
            off, (rows, cols) = SMALL_OFFSET[n], SMALL[n]
            out[off:off + rows, 0:cols] = ins[n][...]

    return pl.pallas_call(
        body, name="pack_small", out_shape=jax.ShapeDtypeStruct((SMALL_ROWS, LANES), F32),
        compiler_params=_cparams(),
    )(*[grads[n] for n in names])


def _unpack_small_ref(g_ref, n):
    off, (rows, cols) = SMALL_OFFSET[n], SMALL[n]
    for re, im in SMALL_PAIRS:
        if n == re:
            return g_ref[off:off + rows, 0:HALF_LANES]
        if n == im:
            return g_ref[off:off + rows, HALF_LANES:LANES]
    if n in SMALL_VECS:
        return jnp.concatenate([g_ref[off + i:off + i + 1, :] for i in range(cols // LANES)], axis=1)
    return g_ref[off:off + rows, 0:cols]


def _adamw_small(g_packed, w, m, v):
    names = list(SMALL)
    n = len(names)

    def body(g_ref, *refs):
        w_refs, m_refs, v_refs, outs = refs[:n], refs[n:2 * n], refs[2 * n:3 * n], refs[3 * n:]
        for idx, name in enumerate(names):
            g = _unpack_small_ref(g_ref, name)
            delta, m2, v2 = _adamw_math(w_refs[idx][...], g, m_refs[idx][...], v_refs[idx][...])
            outs[4 * idx][...] = g
            outs[4 * idx + 1][...] = delta
            outs[4 * idx + 2][...] = m2
            outs[4 * idx + 3][...] = v2

    outs = pl.pallas_call(
        body, name="adamw_small",
        out_shape=[jax.ShapeDtypeStruct(SMALL[name], F32) for name in names for _ in range(4)],
        compiler_params=_cparams(),
    )(g_packed, *[w[k] for k in names], *[m[k] for k in names], *[v[k] for k in names])
    return {name: outs[4 * idx:4 * idx + 4] for idx, name in enumerate(names)}


WEIGHT_NAMES = ['norm_ffn1', 'ffn1_w_gate', 'ffn1_w_up', 'ffn1_w_down', 'norm_mix', 'w_in', 'attn_sinks',
                'ssm_lambda_re', 'ssm_lambda_im', 'ssm_log_dt', 'ssm_b_re', 'ssm_b_im', 'ssm_c_re', 'ssm_c_im',
                'ssm_d', 'ssm_glu_w', 'ssm_glu_b', 'attn_out_norm', 'ssm_out_norm', 'w_out', 'norm_ffn2',
                'ffn2_w_gate', 'ffn2_w_up', 'ffn2_w_down', 'final_norm']


def kernel(x, norm_ffn1, ffn1_w_gate, ffn1_w_up, ffn1_w_down, norm_mix, w_in, attn_sinks, ssm_lambda_re, ssm_lambda_im, ssm_log_dt, ssm_b_re, ssm_b_im, ssm_c_re, ssm_c_im, ssm_d, ssm_glu_w, ssm_glu_b, attn_out_norm, ssm_out_norm, w_out, norm_ffn2, ffn2_w_gate, ffn2_w_up, ffn2_w_down, final_norm, loss_target, m_norm_ffn1, m_ffn1_w_gate, m_ffn1_w_up, m_ffn1_w_down, m_norm_mix, m_w_in, m_attn_sinks, m_ssm_lambda_re, m_ssm_lambda_im, m_ssm_log_dt, m_ssm_b_re, m_ssm_b_im, m_ssm_c_re, m_ssm_c_im, m_ssm_d, m_ssm_glu_w, m_ssm_glu_b, m_attn_out_norm, m_ssm_out_norm, m_w_out, m_norm_ffn2, m_ffn2_w_gate, m_ffn2_w_up, m_ffn2_w_down, m_final_norm, v_norm_ffn1, v_ffn1_w_gate, v_ffn1_w_up, v_ffn1_w_down, v_norm_mix, v_w_in, v_attn_sinks, v_ssm_lambda_re, v_ssm_lambda_im, v_ssm_log_dt, v_ssm_b_re, v_ssm_b_im, v_ssm_c_re, v_ssm_c_im, v_ssm_d, v_ssm_glu_w, v_ssm_glu_b, v_attn_out_norm, v_ssm_out_norm, v_w_out, v_norm_ffn2, v_ffn2_w_gate, v_ffn2_w_up, v_ffn2_w_down, v_final_norm):
    args = dict(locals())
    weights = {n: args[n] for n in WEIGHT_NAMES}
    moms = {n: args["m_" + n] for n in WEIGHT_NAMES}
    vars_ = {n: args["v_" + n] for n in WEIGHT_NAMES}

    def shard2d(a):
        return a.reshape(a.shape[-2], a.shape[-1])

    def blocks(g, k):
        return g.reshape(N_DEV, BIG[k][1], BIG[k][2])

    def full(g, k):
        return g.reshape(N_DEV * BIG[k][1], BIG[k][2])

    shards = dict(zip(BIG, _cast_shards({k: shard2d(weights[BIG[k][0]]) for k in BIG})))
    nf = len(FIRST_GROUP)
    got = _gather_first([shards[k] for k in FIRST_GROUP], [shards[k] for k in LATE_GROUP])
    w_first = {k: full(g, k) for k, g in zip(FIRST_GROUP, got[:nf])}
    w_send, w_recv, w_srcs, w_lands, w_token = _split_start(
        "gather_late_start", [shards[k] for k in LATE_GROUP], got[nf:], scatter=False)

    def late_weights(dep):
        _, lands = _split_wait("gather_late_wait", w_send, w_recv, w_srcs, w_lands, False, dep)
        return {k: full(g, k) for k, g in zip(LATE_GROUP, lands)}

    early = {}

    def early_grads(g):
        srcs = [blocks(g[k], k) for k in LATE_GROUP]
        lands = [lax.empty((N_PEERS, BIG[k][1], BIG[k][2]), BF16) for k in LATE_GROUP]
        early["send"], early["recv"], early["srcs"], early["lands"], token = _split_start(
            "grads_late_start", srcs, lands, scatter=True)
        return token

    def small2d(a, n):
        if n in SMALL_TRANSPOSED:
            a = jnp.swapaxes(a, -1, -2)
        return a.reshape(SMALL[n])

    def small_master(a, n):
        if n in SMALL_TRANSPOSED:
            shape = weights[n].shape
            return jnp.swapaxes(a.reshape(shape[:-2] + (shape[-1], shape[-2])), -1, -2)
        return a.reshape(weights[n].shape)

    small_p = {n: small2d(weights[n], n) for n in SMALL}
    loss, grad_x, g_first, g_small = _local_step(
        x.reshape(SEQ, D_MODEL), loss_target.reshape(SEQ, D_MODEL), w_first, small_p, late_weights, early_grads,
        after=w_token)

    own_sums, first_parts, small_grad = _exchange_last([blocks(g_first[k], k) for k in FIRST_GROUP],
                                                       _pack_small(g_small))
    own_late, late_parts = _split_wait("grads_late_wait", early["send"], early["recv"], early["srcs"],
                                       early["lands"], True, small_grad)
    own = dict(zip(FIRST_GROUP + LATE_GROUP, list(own_sums) + list(own_late)))
    parts = dict(zip(FIRST_GROUP + LATE_GROUP, list(first_parts) + list(late_parts)))
    outs = {}
    for k in BIG:
        n = BIG[k][0]
        outs[n] = [o.reshape(weights[n].shape) for o in
                   _adamw_big(own[k], parts[k], shard2d(weights[n]), shard2d(moms[n]), shard2d(vars_[n]),
                              "adamw_" + n, BIG[k][3])]
    small_out = _adamw_small(small_grad, small_p, {n: small2d(moms[n], n) for n in SMALL},
                             {n: small2d(vars_[n], n) for n in SMALL})
    for n in SMALL:
        outs[n] = [small_master(o, n) for o in small_out[n]]

    total_loss = lax.psum(loss[0, 0], ("x", "y", "c"))
    result = [total_loss, grad_x.reshape(x.shape)]
    for i in range(4):
        result += [outs[n][i] for n in WEIGHT_NAMES]
    return tuple(result)
```

```python
import functools

import jax
import jax.numpy as jnp
from jax import lax
from jax.experimental import pallas as pl
from jax.experimental.pallas import tpu as pltpu

F32 = jnp.float32
BF16 = jnp.bfloat16

N_DEV = 8
SEQ = 2048
D_MODEL = 1024
D_FF = 2816
ATTN_HEADS = 8
KV_HEADS = 2
HEAD_DIM = 64
ATTN_WIDTH = 512
KV_WIDTH = 128
WINDOW = 128
SSM_WIDTH = 512
IN_WIDTH = 1280
EPS = 1e-6
NEG_INF = -1e30
LAMBDA_RE_MAX = -1e-4
LANES = 128
N_LANE_BLOCKS = 16
SCAN_CHUNK = SEQ // 8

ADAM_LR = 0.001
ADAM_B1 = 0.9
ADAM_B2 = 0.999
ADAM_EPS = 1e-08
ADAM_WD = 0.01
ADAM_STEP = 10

VMEM_LIMIT = 56 * 1024 * 1024
MESH_ID = pl.DeviceIdType.MESH


def _cparams(sem=None):
    return pltpu.CompilerParams(dimension_semantics=sem, vmem_limit_bytes=VMEM_LIMIT)


def _dot(a, b):
    return jnp.dot(a, b, preferred_element_type=F32)


def _dot_nt(a, b):
    return lax.dot_general(a, b, (((1,), (1,)), ((), ())), preferred_element_type=F32)


def _dot_tn(a, b):
    return lax.dot_general(a, b, (((0,), (0,)), ((), ())), preferred_element_type=F32)


def _rms_fwd(x, g):
    r = lax.rsqrt(jnp.mean(x * x, axis=-1, keepdims=True) + EPS)
    return x * r * g


def _rms_bwd(dh, x, g):
    r = lax.rsqrt(jnp.mean(x * x, axis=-1, keepdims=True) + EPS)
    xh = x * r
    dg = jnp.sum(dh * xh, axis=0, keepdims=True)
    dxh = dh * g
    dx = r * (dxh - xh * jnp.mean(dxh * xh, axis=-1, keepdims=True))
    return dx, dg


def _sigmoid(x):
    return 1.0 / (1.0 + jnp.exp(-x))


FFN_TM = 512
FFN_TF = 1408


def _ffn_fwd(x, g, wgt, wut, wd, name, after=None):
    tm, tf = FFN_TM, FFN_TF
    nj = D_FF // tf
    deps = [] if after is None else [after]

    def body(x_ref, g_ref, wg_ref, wu_ref, wd_ref, *rest):
        xo_ref, h_ref, a_ref, b_ref, h_s, acc = rest[len(deps):]
        j = pl.program_id(1)

        @pl.when(j == 0)
        def _():
            h = _rms_fwd(x_ref[...], g_ref[...]).astype(BF16)
            h_s[...] = h
            h_ref[...] = h
            acc[...] = jnp.zeros_like(acc)

        h = h_s[...]
        a = _dot_nt(h, wg_ref[...])
        b = _dot_nt(h, wu_ref[...])
        a_ref[...] = a.astype(BF16)
        b_ref[...] = b.astype(BF16)
        s = (a * _sigmoid(a) * b).astype(BF16)
        acc[...] += _dot(s, wd_ref[...])

        @pl.when(j == nj - 1)
        def _():
            xo_ref[...] = x_ref[...] + 0.5 * acc[...]

    return pl.pallas_call(
        body, name=name, grid=(SEQ // tm, nj),
        in_specs=[pl.BlockSpec((tm, D_MODEL), lambda i, j: (i, 0)),
                  pl.BlockSpec((1, D_MODEL), lambda i, j: (0, 0)),
                  pl.BlockSpec((tf, D_MODEL), lambda i, j: (j, 0)),
                  pl.BlockSpec((tf, D_MODEL), lambda i, j: (j, 0)),
                  pl.BlockSpec((tf, D_MODEL), lambda i, j: (j, 0))] + [pl.BlockSpec(memory_space=pl.ANY)] * len(deps),
        out_specs=[pl.BlockSpec((tm, D_MODEL), lambda i, j: (i, 0)),
                   pl.BlockSpec((tm, D_MODEL), lambda i, j: (i, 0)),
                   pl.BlockSpec((tm, tf), lambda i, j: (i, j)),
                   pl.BlockSpec((tm, tf), lambda i, j: (i, j))],
        out_shape=[jax.ShapeDtypeStruct((SEQ, D_MODEL), F32), jax.ShapeDtypeStruct((SEQ, D_MODEL), BF16),
                   jax.ShapeDtypeStruct((SEQ, D_FF), BF16), jax.ShapeDtypeStruct((SEQ, D_FF), BF16)],
        scratch_shapes=[pltpu.VMEM((tm, D_MODEL), BF16), pltpu.VMEM((tm, D_MODEL), F32)],
        compiler_params=_cparams(("parallel", "arbitrary")),
    )(x, g, wgt, wut, wd, *deps)


def _ffn_bwd_act(dxo, x, g, a, b, wgt, wut, wd, name):
    tm, tf = FFN_TM // 2, FFN_TF
    nj = D_FF // tf

    def body(dxo_ref, x_ref, g_ref, a_ref, b_ref, wg_ref, wu_ref, wd_ref,
             dx_ref, da_ref, db_ref, s_ref, df_ref, dg_ref, df_s, acc):
        i = pl.program_id(0)
        j = pl.program_id(1)

        @pl.when(j == 0)
        def _():
            df = (0.5 * dxo_ref[...]).astype(BF16)
            df_s[...] = df
            df_ref[...] = df
            acc[...] = jnp.zeros_like(acc)

        ds = _dot_nt(df_s[...], wd_ref[...])
        av = a_ref[...].astype(F32)
        bv = b_ref[...].astype(F32)
        sig = _sigmoid(av)
        sl = av * sig
        s_ref[...] = (sl * bv).astype(BF16)
        db = (ds * sl).astype(BF16)
        da = (ds * bv * (sig * (1.0 + av * (1.0 - sig)))).astype(BF16)
        da_ref[...] = da
        db_ref[...] = db
        acc[...] += _dot(da, wg_ref[...]) + _dot(db, wu_ref[...])

        @pl.when(j == nj - 1)
        def _():
            dx, dg = _rms_bwd(acc[...], x_ref[...], g_ref[...])
            dx_ref[...] = dxo_ref[...] + dx

            @pl.when(i == 0)
            def _():
                dg_ref[...] = dg

            @pl.when(i != 0)
            def _():
                dg_ref[...] += dg

    row = lambda i, j: (i, 0)
    col = lambda i, j: (j, 0)
    tile = lambda i, j: (i, j)
    return pl.pallas_call(
        body, name=name, grid=(SEQ // tm, nj),
        in_specs=[pl.BlockSpec((tm, D_MODEL), row), pl.BlockSpec((tm, D_MODEL), row),
                  pl.BlockSpec((1, D_MODEL), lambda i, j: (0, 0)),
                  pl.BlockSpec((tm, tf), tile), pl.BlockSpec((tm, tf), tile),
                  pl.BlockSpec((tf, D_MODEL), col), pl.BlockSpec((tf, D_MODEL), col), pl.BlockSpec((tf, D_MODEL), col)],
        out_specs=[pl.BlockSpec((tm, D_MODEL), row),
                   pl.BlockSpec((tm, tf), tile), pl.BlockSpec((tm, tf), tile), pl.BlockSpec((tm, tf), tile),
                   pl.BlockSpec((tm, D_MODEL), row),
                   pl.BlockSpec((1, D_MODEL), lambda i, j: (0, 0))],
        out_shape=[jax.ShapeDtypeStruct((SEQ, D_MODEL), F32),
                   jax.ShapeDtypeStruct((SEQ, D_FF), BF16), jax.ShapeDtypeStruct((SEQ, D_FF), BF16),
                   jax.ShapeDtypeStruct((SEQ, D_FF), BF16),
                   jax.ShapeDtypeStruct((SEQ, D_MODEL), BF16),
                   jax.ShapeDtypeStruct((1, D_MODEL), F32)],
        scratch_shapes=[pltpu.VMEM((tm, D_MODEL), BF16), pltpu.VMEM((tm, D_MODEL), F32)],
        compiler_params=_cparams(("arbitrary", "arbitrary")),
    )(dxo, x, g, a, b, wgt, wut, wd)


def _mm_tn(pairs, name, tmm=256):
    m = pairs[0][0].shape[1]
    n_pairs = len(pairs)

    def body(*refs):
        ins, outs = refs[:2 * n_pairs], refs[2 * n_pairs:]
        for p in range(n_pairs):
            outs[p][...] = _dot_tn(ins[2 * p][...], ins[2 * p + 1][...]).astype(BF16)

    in_specs, out_specs, out_shape, args = [], [], [], []
    for a, b in pairs:
        n = b.shape[1]
        in_specs += [pl.BlockSpec((SEQ, tmm), lambda i: (0, i)), pl.BlockSpec((SEQ, n), lambda i: (0, 0))]
        out_specs.append(pl.BlockSpec((tmm, n), lambda i: (i, 0)))
        out_shape.append(jax.ShapeDtypeStruct((m, n), BF16))
        args += [a, b]
    return pl.pallas_call(body, name=name, grid=(m // tmm,), in_specs=in_specs, out_specs=out_specs,
                          out_shape=out_shape, compiler_params=_cparams(("parallel",)))(*args)


MIX_TM = 256


def _mixin_fwd(x, g, wint):
    tm = MIX_TM

    def body(x_ref, g_ref, w_ref, h_ref, q_ref, k_ref, v_ref, u_ref):
        h = _rms_fwd(x_ref[...], g_ref[...]).astype(BF16)
        h_ref[...] = h
        proj = _dot_nt(h, w_ref[...])
        q_ref[...] = proj[:, :ATTN_WIDTH]
        k_ref[...] = proj[:, ATTN_WIDTH:ATTN_WIDTH + KV_WIDTH]
        v_ref[...] = proj[:, ATTN_WIDTH + KV_WIDTH:ATTN_WIDTH + 2 * KV_WIDTH]
        u_ref[...] = proj[:, ATTN_WIDTH + 2 * KV_WIDTH:]

    row = lambda i: (i, 0)
    return pl.pallas_call(
        body, name="mixin_fwd", grid=(SEQ // tm,),
        in_specs=[pl.BlockSpec((tm, D_MODEL), row), pl.BlockSpec((1, D_MODEL), lambda i: (0, 0)),
                  pl.BlockSpec((IN_WIDTH, D_MODEL), lambda i: (0, 0))],
        out_specs=[pl.BlockSpec((tm, D_MODEL), row), pl.BlockSpec((tm, ATTN_WIDTH), row),
                   pl.BlockSpec((tm, KV_WIDTH), row), pl.BlockSpec((tm, KV_WIDTH), row),
                   pl.BlockSpec((tm, SSM_WIDTH), row)],
        out_shape=[jax.ShapeDtypeStruct((SEQ, D_MODEL), BF16), jax.ShapeDtypeStruct((SEQ, ATTN_WIDTH), F32),
                   jax.ShapeDtypeStruct((SEQ, KV_WIDTH), F32), jax.ShapeDtypeStruct((SEQ, KV_WIDTH), F32),
                   jax.ShapeDtypeStruct((SEQ, SSM_WIDTH), F32)],
        compiler_params=_cparams(("parallel",)),
    )(x, g, wint)


def _mixin_bwd(dq, dk, dv, du, wint, x, g, dres):
    tm = MIX_TM

    def body(dq_ref, dk_ref, dv_ref, du_ref, w_ref, x_ref, g_ref, dres_ref, dx_ref, dp_ref, dg_ref):
        i = pl.program_id(0)
        dp = jnp.concatenate([dq_ref[...], dk_ref[...], dv_ref[...], du_ref[...]], axis=-1).astype(BF16)
        dp_ref[...] = dp
        dh = _dot(dp, w_ref[...])
        dx, dg = _rms_bwd(dh, x_ref[...], g_ref[...])
        dx_ref[...] = dres_ref[...] + dx

        @pl.when(i == 0)
        def _():
            dg_ref[...] = dg

        @pl.when(i != 0)
        def _():
            dg_ref[...] += dg

    row = lambda i: (i, 0)
    const = lambda i: (0, 0)
    return pl.pallas_call(
        body, name="mixin_bwd", grid=(SEQ // tm,),
        in_specs=[pl.BlockSpec((tm, ATTN_WIDTH), row), pl.BlockSpec((tm, KV_WIDTH), row),
                  pl.BlockSpec((tm, KV_WIDTH), row), pl.BlockSpec((tm, SSM_WIDTH), row),
                  pl.BlockSpec((IN_WIDTH, D_MODEL), const), pl.BlockSpec((tm, D_MODEL), row),
                  pl.BlockSpec((1, D_MODEL), const), pl.BlockSpec((tm, D_MODEL), row)],
        out_specs=[pl.BlockSpec((tm, D_MODEL), row), pl.BlockSpec((tm, IN_WIDTH), row),
                   pl.BlockSpec((1, D_MODEL), const)],
        out_shape=[jax.ShapeDtypeStruct((SEQ, D_MODEL), F32), jax.ShapeDtypeStruct((SEQ, IN_WIDTH), BF16),
                   jax.ShapeDtypeStruct((1, D_MODEL), F32)],
        compiler_params=_cparams(("arbitrary",)),
    )(dq, dk, dv, du, wint, x, g, dres)


N_QBLOCKS = SEQ // WINDOW
GROUP = ATTN_HEADS // KV_HEADS
SCALE = HEAD_DIM ** -0.5


def _alibi_slope(h):
    return 2.0 ** (-8.0 * (h + 1) / ATTN_HEADS)


def _window_masks(n):
    t_idx = lax.broadcasted_iota(jnp.int32, (WINDOW, 3 * WINDOW), 0)
    s_idx = lax.broadcasted_iota(jnp.int32, (WINDOW, 3 * WINDOW), 1)
    rel = s_idx - WINDOW - t_idx
    absrel = jnp.abs(rel)
    key_pos = n * WINDOW - WINDOW + s_idx
    valid = (absrel <= WINDOW) & (key_pos >= 0) & (key_pos < SEQ)
    return absrel.astype(F32), valid


def _head_probs(qh, kw, absrel, valid, slope, sink):
    s = _dot_nt(qh, kw) * SCALE
    s = jnp.where(valid, s - slope * absrel, NEG_INF)
    m = jnp.maximum(jnp.max(s, axis=-1, keepdims=True), sink)
    p = jnp.exp(s - m)
    ps = jnp.exp(sink - m)
    inv = 1.0 / (jnp.sum(p, axis=-1, keepdims=True) + ps)
    return p * inv, ps * inv


def _attn_fwd(q, kp, vp, sinks):
    def body(sk_ref, q_ref, kp_ref, vp_ref, o_ref):
        def blk(n, carry):
            r0 = pl.multiple_of(n * WINDOW, WINDOW)
            absrel, valid = _window_masks(n)
            for gi in range(KV_HEADS):
                kw = kp_ref[pl.ds(r0, 3 * WINDOW), gi * HEAD_DIM:(gi + 1) * HEAD_DIM].astype(BF16)
                vw = vp_ref[pl.ds(r0, 3 * WINDOW), gi * HEAD_DIM:(gi + 1) * HEAD_DIM].astype(BF16)
                for hh in range(GROUP):
                    h = gi * GROUP + hh
                    cols = slice(h * HEAD_DIM, (h + 1) * HEAD_DIM)
                    qh = q_ref[pl.ds(r0, WINDOW), cols].astype(BF16)
                    pr, _ = _head_probs(qh, kw, absrel, valid, _alibi_slope(h), sk_ref[0, h])
                    o_ref[pl.ds(r0, WINDOW), cols] = _dot(pr.astype(BF16), vw)
            return carry

        lax.fori_loop(0, N_QBLOCKS, blk, 0)

    vmem = pl.BlockSpec(memory_space=pltpu.VMEM)
    return pl.pallas_call(
        body, name="attn_fwd",
        in_specs=[pl.BlockSpec(memory_space=pltpu.SMEM), vmem, vmem, vmem], out_specs=vmem,
        out_shape=jax.ShapeDtypeStruct((SEQ, ATTN_WIDTH), F32),
        compiler_params=_cparams(),
    )(sinks, q, kp, vp)


def _attn_bwd(q, kp, vp, sinks, do):
    def body(sk_ref, q_ref, kp_ref, vp_ref, do_ref, dq_ref, dkp_ref, dvp_ref, dsk_ref, dsk_acc):
        dkp_ref[...] = jnp.zeros_like(dkp_ref)
        dvp_ref[...] = jnp.zeros_like(dvp_ref)
        dsk_acc[...] = jnp.zeros_like(dsk_acc)

        def blk(n, carry):
            r0 = pl.multiple_of(n * WINDOW, WINDOW)
            absrel, valid = _window_masks(n)
            for gi in range(KV_HEADS):
                gcols = slice(gi * HEAD_DIM, (gi + 1) * HEAD_DIM)
                kw = kp_ref[pl.ds(r0, 3 * WINDOW), gcols].astype(BF16)
                vw = vp_ref[pl.ds(r0, 3 * WINDOW), gcols].astype(BF16)
                dkw = jnp.zeros((3 * WINDOW, HEAD_DIM), F32)
                dvw = jnp.zeros((3 * WINDOW, HEAD_DIM), F32)
                for hh in range(GROUP):
                    h = gi * GROUP + hh
                    cols = slice(h * HEAD_DIM, (h + 1) * HEAD_DIM)
                    qh = q_ref[pl.ds(r0, WINDOW), cols].astype(BF16)
                    doh = do_ref[pl.ds(r0, WINDOW), cols].astype(BF16)
                    pr, psink = _head_probs(qh, kw, absrel, valid, _alibi_slope(h), sk_ref[0, h])
                    dp = _dot_nt(doh, vw)
                    delta = jnp.sum(pr * dp, axis=-1, keepdims=True)
                    ds = (pr * (dp - delta)).astype(BF16)
                    dsk_acc[:, h:h + 1] += -(psink * delta)
                    dq_ref[pl.ds(r0, WINDOW), cols] = _dot(ds, kw) * SCALE
                    dkw = dkw + _dot_tn(ds, qh) * SCALE
                    dvw = dvw + _dot_tn(pr.astype(BF16), doh)
                dkp_ref[pl.ds(r0, 3 * WINDOW), gcols] += dkw
                dvp_ref[pl.ds(r0, 3 * WINDOW), gcols] += dvw
            return carry

        lax.fori_loop(0, N_QBLOCKS, blk, 0)
        dsk_ref[...] = jnp.sum(dsk_acc[...], axis=0, keepdims=True)

    vmem = pl.BlockSpec(memory_space=pltpu.VMEM)
    return pl.pallas_call(
        body, name="attn_bwd",
        in_specs=[pl.BlockSpec(memory_space=pltpu.SMEM), vmem, vmem, vmem, vmem],
        out_specs=[vmem, vmem, vmem, vmem],
        out_shape=[jax.ShapeDtypeStruct((SEQ, ATTN_WIDTH), F32),
                   jax.ShapeDtypeStruct((SEQ + 2 * WINDOW, KV_WIDTH), F32),
                   jax.ShapeDtypeStruct((SEQ + 2 * WINDOW, KV_WIDTH), F32),
                   jax.ShapeDtypeStruct((1, ATTN_HEADS), F32)],
        scratch_shapes=[pltpu.VMEM((WINDOW, ATTN_HEADS), F32)],
        compiler_params=_cparams(),
    )(sinks, q, kp, vp, do)


HALF_LANES = LANES // 2
BLOCK_ROWS = 32


def _embed_block(bt, q):
    z = jnp.zeros((16, HALF_LANES), bt.dtype)
    blk = jnp.concatenate([jnp.concatenate([bt[:16], z], axis=1), jnp.concatenate([z, bt[16:]], axis=1)], axis=0)
    parts = [jnp.zeros((BLOCK_ROWS * q, LANES), bt.dtype)] if q else []
    parts.append(blk)
    if q < 3:
        parts.append(jnp.zeros((BLOCK_ROWS * (3 - q), LANES), bt.dtype))
    return jnp.concatenate(parts, axis=0)


def _extract_block(m, q):
    blk = m[BLOCK_ROWS * q:BLOCK_ROWS * (q + 1)]
    return jnp.concatenate([blk[:16, :HALF_LANES], blk[16:, HALF_LANES:]], axis=0)


def _ssm_prep(lam_re, lam_im, log_dt, bt_re, bt_im, c_re, c_im):
    nb = 2 * N_LANE_BLOCKS

    def body(lr_ref, li_ref, ldt_ref, btr_ref, bti_ref, ctr_ref, cti_ref,
             ar_ref, ai_ref, bbr_ref, bbi_ref, cpr_ref, cpi_ref):
        lr = jnp.minimum(lr_ref[...], LAMBDA_RE_MAX)
        li = li_ref[...]
        dt = jnp.exp(ldt_ref[...])
        mag = jnp.exp(lr * dt)
        ar = mag * jnp.cos(li * dt)
        ai = mag * jnp.sin(li * dt)
        den = lr * lr + li * li
        cr = ((ar - 1.0) * lr + ai * li) / den
        ci = (ai * lr - (ar - 1.0) * li) / den
        ar_ref[...] = ar
        ai_ref[...] = ai
        for i in range(nb):
            q = i % 4
            rows = slice(BLOCK_ROWS * i, BLOCK_ROWS * (i + 1))
            br = _embed_block(btr_ref[rows, :], q)
            bi = _embed_block(bti_ref[rows, :], q)
            cri, cii = cr[i:i + 1, :], ci[i:i + 1, :]
            bbr_ref[i] = (cri * br - cii * bi).astype(BF16)
            bbi_ref[i] = (cri * bi + cii * br).astype(BF16)
            cpr_ref[i] = _embed_block(ctr_ref[rows, :], q).T.astype(BF16)
            cpi_ref[i] = _embed_block(cti_ref[rows, :], q).T.astype(BF16)

    w_shape = jax.ShapeDtypeStruct((nb, LANES, LANES), BF16)
    return pl.pallas_call(
        body, name="ssm_prep",
        out_shape=[jax.ShapeDtypeStruct((nb, LANES), F32), jax.ShapeDtypeStruct((nb, LANES), F32),
                   w_shape, w_shape, w_shape, w_shape],
        compiler_params=_cparams(),
    )(lam_re, lam_im, log_dt, bt_re, bt_im, c_re, c_im)


def _ssm_prep_bwd(lam_re, lam_im, log_dt, bt_re, bt_im, dar, dai, dbbr, dbbi, dcr, dci):
    nb = 2 * N_LANE_BLOCKS

    def body(lr_ref, li_ref, ldt_ref, btr_ref, bti_ref, dar_ref, dai_ref, dbbr_ref, dbbi_ref, dcr_ref, dci_ref,
             glr_ref, gli_ref, gdt_ref, gbr_ref, gbi_ref, gcre_ref, gcim_ref, gcr_s, gci_s):
        lam = lr_ref[...]
        lr = jnp.minimum(lam, LAMBDA_RE_MAX)
        li = li_ref[...]
        dt = jnp.exp(ldt_ref[...])
        mag = jnp.exp(lr * dt)
        cs = jnp.cos(li * dt)
        sn = jnp.sin(li * dt)
        ar = mag * cs
        ai = mag * sn
        den = lr * lr + li * li
        nr = (ar - 1.0) * lr + ai * li
        ni = ai * lr - (ar - 1.0) * li
        cr = nr / den
        ci = ni / den
        for i in range(nb):
            q = i % 4
            rows = slice(BLOCK_ROWS * i, BLOCK_ROWS * (i + 1))
            br = _embed_block(btr_ref[rows, :], q)
            bi = _embed_block(bti_ref[rows, :], q)
            gbbr = dbbr_ref[i]
            gbbi = dbbi_ref[i]
            cri, cii = cr[i:i + 1, :], ci[i:i + 1, :]
            gcr_s[i:i + 1, :] = jnp.sum(gbbr * br + gbbi * bi, axis=0, keepdims=True)
            gci_s[i:i + 1, :] = jnp.sum(gbbi * br - gbbr * bi, axis=0, keepdims=True)
            gbr_ref[rows, :] = _extract_block(cri * gbbr + cii * gbbi, q)
            gbi_ref[rows, :] = _extract_block(cri * gbbi - cii * gbbr, q)
            gcre_ref[rows, :] = _extract_block(dcr_ref[i].T, q)
            gcim_ref[rows, :] = _extract_block(dci_ref[i].T, q)
        g_cr = gcr_s[...]
        g_ci = gci_s[...]
        g_nr = g_cr / den
        g_ni = g_ci / den
        g_den = -(g_cr * nr + g_ci * ni) / (den * den)
        g_ar = dar_ref[...] + g_nr * lr - g_ni * li
        g_ai = dai_ref[...] + g_nr * li + g_ni * lr
        g_lr = g_nr * (ar - 1.0) + g_ni * ai + g_den * 2.0 * lr
        g_li = g_nr * ai - g_ni * (ar - 1.0) + g_den * 2.0 * li
        g_mag = g_ar * cs + g_ai * sn
        g_th = (g_ai * cs - g_ar * sn) * mag
        g_lr = g_lr + g_mag * mag * dt
        g_li = g_li + g_th * dt
        g_dt = g_mag * mag * lr + g_th * li
        glr_ref[...] = jnp.where(lam < LAMBDA_RE_MAX, g_lr, 0.0)
        gli_ref[...] = g_li
        gl = g_dt * dt
        half = LANES // 2
        gdt_ref[:, 0:1] = jnp.sum(gl[:, :half], axis=1, keepdims=True)
        gdt_ref[:, 1:2] = jnp.sum(gl[:, half:], axis=1, keepdims=True)

    rows_shape = jax.ShapeDtypeStruct((nb * BLOCK_ROWS, HALF_LANES), F32)
    return pl.pallas_call(
        body, name="ssm_prep_bwd",
        out_shape=[jax.ShapeDtypeStruct((nb, LANES), F32), jax.ShapeDtypeStruct((nb, LANES), F32),
                   jax.ShapeDtypeStruct((nb, 2), F32), rows_shape, rows_shape, rows_shape, rows_shape],
        scratch_shapes=[pltpu.VMEM((nb, LANES), F32), pltpu.VMEM((nb, LANES), F32)],
        compiler_params=_cparams(),
    )(lam_re, lam_im, log_dt, bt_re, bt_im, dar, dai, dbbr, dbbi, dcr, dci)


def _cmul(ar, ai, br, bi):
    return ar * br - ai * bi, ar * bi + ai * br


def _interleave_rows(src_ref, dst_ref):
    def step(j, carry):
        dst_ref[pl.ds(pl.multiple_of(j * 8, 8), 8), :] = src_ref[pl.ds(j, 8, stride=SCAN_CHUNK), :]
        return carry
    lax.fori_loop(0, SCAN_CHUNK, step, 0, unroll=4)


def _deinterleave_rows(src_ref, dst_ref):
    def step(j, carry):
        dst_ref[pl.ds(j, 8, stride=SCAN_CHUNK), :] = src_ref[pl.ds(pl.multiple_of(j * 8, 8), 8), :]
        return carry
    lax.fori_loop(0, SCAN_CHUNK, step, 0, unroll=4)


def _scan_inplace(re_ref, im_ref, a_re, a_im, reverse):
    nq = len(a_re)
    ch = SCAN_CHUNK
    ab_re = [jnp.broadcast_to(a, (8, LANES)) for a in a_re]
    ab_im = [jnp.broadcast_to(a, (8, LANES)) for a in a_im]

    def rows(j):
        jj = (ch - 1 - j) if reverse else j
        return pl.ds(pl.multiple_of(jj * 8, 8), 8)

    def sweep(init, store):
        def step(j, st):
            out = []
            r = rows(j)
            for qi in range(nq):
                xr, xi = st[2 * qi], st[2 * qi + 1]
                pr, pi = _cmul(ab_re[qi], ab_im[qi], xr, xi)
                xr = pr + re_ref[qi, r, :]
                xi = pi + im_ref[qi, r, :]
                if store:
                    re_ref[qi, r, :] = xr
                    im_ref[qi, r, :] = xi
                out += [xr, xi]
            return tuple(out)
        return lax.fori_loop(0, ch, step, tuple(init), unroll=2)

    zeros = [jnp.zeros((8, LANES), F32)] * (2 * nq)
    finals = sweep(zeros, store=False)

    row_id = lax.broadcasted_iota(jnp.int32, (8, LANES), 0)
    carries = []
    for qi in range(nq):
        pr, pi = ab_re[qi], ab_im[qi]
        for _ in range(8):
            pr, pi = _cmul(pr, pi, pr, pi)
        fr, fi = finals[2 * qi], finals[2 * qi + 1]
        sr = jnp.zeros((8, LANES), F32)
        si = jnp.zeros((8, LANES), F32)
        for _ in range(7):
            tr, ti = _cmul(pr, pi, sr, si)
            tr, ti = tr + fr, ti + fi
            if reverse:
                sr = jnp.where(row_id == 7, 0.0, pltpu.roll(tr, 7, axis=0))
                si = jnp.where(row_id == 7, 0.0, pltpu.roll(ti, 7, axis=0))
            else:
                sr = jnp.where(row_id == 0, 0.0, pltpu.roll(tr, 1, axis=0))
                si = jnp.where(row_id == 0, 0.0, pltpu.roll(ti, 1, axis=0))
        carries += [sr, si]
    sweep(carries, store=True)


SSM_Q = 4


def _ssm_fwd(u, are, aim, bbr, bbi, cre, cim, dskip):
    nq = SSM_Q

    def body(u_ref, ar_ref, ai_ref, bbr_ref, bbi_ref, cr_ref, ci_ref, d_ref, y_ref, xr_ref, xi_ref,
             sre, sim, up, yp):
        _interleave_rows(u_ref, up)
        uf = up[...]
        ub = uf.astype(BF16)
        yp[...] = d_ref[...] * uf
        for d in range(2):
            for qi in range(nq):
                sre[qi] = _dot(ub, bbr_ref[d, qi])
                sim[qi] = _dot(ub, bbi_ref[d, qi])
            _scan_inplace(sre, sim, [ar_ref[d, qi] for qi in range(nq)], [ai_ref[d, qi] for qi in range(nq)],
                          reverse=(d == 1))
            for qi in range(nq):
                xrb = sre[qi].astype(BF16)
                xib = sim[qi].astype(BF16)
                xr_ref[d, qi] = xrb
                xi_ref[d, qi] = xib
                yp[...] += _dot(xrb, cr_ref[d, qi]) - _dot(xib, ci_ref[d, qi])
        _deinterleave_rows(yp, y_ref)

    blk4 = lambda k: (0, k, 0, 0)
    return pl.pallas_call(
        body, name="ssm_fwd", grid=(SSM_WIDTH // LANES,),
        in_specs=[pl.BlockSpec((SEQ, LANES), lambda k: (0, k)),
                  pl.BlockSpec((2, nq, 1, LANES), blk4), pl.BlockSpec((2, nq, 1, LANES), blk4),
                  pl.BlockSpec((2, nq, LANES, LANES), blk4), pl.BlockSpec((2, nq, LANES, LANES), blk4),
                  pl.BlockSpec((2, nq, LANES, LANES), blk4), pl.BlockSpec((2, nq, LANES, LANES), blk4),
                  pl.BlockSpec((1, LANES), lambda k: (0, k))],
        out_specs=[pl.BlockSpec((SEQ, LANES), lambda k: (0, k)),
                   pl.BlockSpec((2, nq, SEQ, LANES), blk4), pl.BlockSpec((2, nq, SEQ, LANES), blk4)],
        out_shape=[jax.ShapeDtypeStruct((SEQ, SSM_WIDTH), F32),
                   jax.ShapeDtypeStruct((2, N_LANE_BLOCKS, SEQ, LANES), BF16),
                   jax.ShapeDtypeStruct((2, N_LANE_BLOCKS, SEQ, LANES), BF16)],
        scratch_shapes=[pltpu.VMEM((nq, SEQ, LANES), F32), pltpu.VMEM((nq, SEQ, LANES), F32),
                        pltpu.VMEM((SEQ, LANES), F32), pltpu.VMEM((SEQ, LANES), F32)],
        compiler_params=_cparams(("parallel",)),
    )(u, are, aim, bbr, bbi, cre, cim, dskip)


def _ssm_bwd(dy, u, xr, xi, are, aim, bbr, bbi, cre, cim, dskip, after=None):
    nq = SSM_Q
    body_rows = SEQ - 8
    deps = [] if after is None else [after]

    def body(dy_ref, u_ref, xr_ref, xi_ref, ar_ref, ai_ref, bbr_ref, bbi_ref, cr_ref, ci_ref, d_ref, *rest):
        (du_ref, dd_ref, dcr_ref, dci_ref, dbr_ref, dbi_ref, dar_ref, dai_ref,
         sre, sim, up, dyp, dup) = rest[len(deps):]
        _interleave_rows(u_ref, up)
        _interleave_rows(dy_ref, dyp)
        dyf = dyp[...]
        uf = up[...]
        dyb = dyf.astype(BF16)
        ub = uf.astype(BF16)
        dd_ref[...] = jnp.sum(dyf * uf, axis=0, keepdims=True)
        dup[...] = d_ref[...] * dyf
        row8 = lax.broadcasted_iota(jnp.int32, (8, LANES), 0)
        for d in range(2):
            for qi in range(nq):
                sre[qi] = _dot_nt(dyb, cr_ref[d, qi])
                sim[qi] = -_dot_nt(dyb, ci_ref[d, qi])
                dcr_ref[d, qi] = _dot_tn(xr_ref[d, qi], dyb)
                dci_ref[d, qi] = -_dot_tn(xi_ref[d, qi], dyb)
            _scan_inplace(sre, sim, [ar_ref[d, qi] for qi in range(nq)], [-ai_ref[d, qi] for qi in range(nq)],
                          reverse=(d == 0))
            for qi in range(nq):
                gr = sre[qi]
                gi = sim[qi]
                xrf = xr_ref[d, qi].astype(F32)
                xif = xi_ref[d, qi].astype(F32)
                if d == 0:
                    g_main_r, g_main_i = gr[8:], gi[8:]
                    x_main_r, x_main_i = xrf[:body_rows], xif[:body_rows]
                    g_edge_r, g_edge_i = gr[:8], gi[:8]
                    x_edge_r = jnp.where(row8 == 0, 0.0, pltpu.roll(xrf[body_rows:], 1, axis=0))
                    x_edge_i = jnp.where(row8 == 0, 0.0, pltpu.roll(xif[body_rows:], 1, axis=0))
                else:
                    g_main_r, g_main_i = gr[:body_rows], gi[:body_rows]
                    x_main_r, x_main_i = xrf[8:], xif[8:]
                    g_edge_r, g_edge_i = gr[body_rows:], gi[body_rows:]
                    x_edge_r = jnp.where(row8 == 7, 0.0, pltpu.roll(xrf[:8], 7, axis=0))
                    x_edge_i = jnp.where(row8 == 7, 0.0, pltpu.roll(xif[:8], 7, axis=0))
                dar_ref[d, qi] = (jnp.sum(g_main_r * x_main_r + g_main_i * x_main_i, axis=0, keepdims=True)
                                  + jnp.sum(g_edge_r * x_edge_r + g_edge_i * x_edge_i, axis=0, keepdims=True))
                dai_ref[d, qi] = (jnp.sum(g_main_i * x_main_r - g_main_r * x_main_i, axis=0, keepdims=True)
                                  + jnp.sum(g_edge_i * x_edge_r - g_edge_r * x_edge_i, axis=0, keepdims=True))
                grb = gr.astype(BF16)
                gib = gi.astype(BF16)
                dup[...] += _dot_nt(grb, bbr_ref[d, qi]) + _dot_nt(gib, bbi_ref[d, qi])
                dbr_ref[d, qi] = _dot_tn(ub, grb)
                dbi_ref[d, qi] = _dot_tn(ub, gib)
        _deinterleave_rows(dup, du_ref)

    blk4 = lambda k: (0, k, 0, 0)
    col = lambda k: (0, k)
    w_spec = pl.BlockSpec((2, nq, LANES, LANES), blk4)
    a_spec = pl.BlockSpec((2, nq, 1, LANES), blk4)
    x_spec = pl.BlockSpec((2, nq, SEQ, LANES), blk4)
    w_shape = jax.ShapeDtypeStruct((2, N_LANE_BLOCKS, LANES, LANES), F32)
    a_shape = jax.ShapeDtypeStruct((2, N_LANE_BLOCKS, 1, LANES), F32)
    return pl.pallas_call(
        body, name="ssm_bwd", grid=(SSM_WIDTH // LANES,),
        in_specs=[pl.BlockSpec((SEQ, LANES), col), pl.BlockSpec((SEQ, LANES), col), x_spec, x_spec,
                  a_spec, a_spec, w_spec, w_spec, w_spec, w_spec, pl.BlockSpec((1, LANES), col)]
        + [pl.BlockSpec(memory_space=pl.ANY)] * len(deps),
        out_specs=[pl.BlockSpec((SEQ, LANES), col), pl.BlockSpec((1, LANES), col),
                   w_spec, w_spec, w_spec, w_spec, a_spec, a_spec],
        out_shape=[jax.ShapeDtypeStruct((SEQ, SSM_WIDTH), F32), jax.ShapeDtypeStruct((1, SSM_WIDTH), F32),
                   w_shape, w_shape, w_shape, w_shape, a_shape, a_shape],
        scratch_shapes=[pltpu.VMEM((nq, SEQ, LANES), F32), pltpu.VMEM((nq, SEQ, LANES), F32),
                        pltpu.VMEM((SEQ, LANES), F32), pltpu.VMEM((SEQ, LANES), F32), pltpu.VMEM((SEQ, LANES), F32)],
        compiler_params=_cparams(("parallel",)),
    )(dy, u, xr, xi, are, aim, bbr, bbi, cre, cim, dskip, *deps)


GELU_C = 0.7978845608028654
GELU_K = 0.044715


def _gelu(y):
    return 0.5 * y * (1.0 + jnp.tanh(GELU_C * (y + GELU_K * y * y * y)))


def _gelu_grad(y):
    t = jnp.tanh(GELU_C * (y + GELU_K * y * y * y))
    return 0.5 * (1.0 + t) + 0.5 * y * (1.0 - t * t) * GELU_C * (1.0 + 3.0 * GELU_K * y * y)


def _mixout_fwd(o, y, glu_w, glu_b, gan, gsn, wout, x1):
    tm = MIX_TM

    def body(o_ref, y_ref, gw_ref, gb_ref, gan_ref, gsn_ref, w_ref, x1_ref, x2_ref, mx_ref):
        yg = _gelu(y_ref[...])
        z = _dot(yg.astype(BF16), gw_ref[...]) + gb_ref[...]
        so = yg * _sigmoid(z)
        na = _rms_fwd(o_ref[...], gan_ref[...])
        ns = _rms_fwd(so, gsn_ref[...])
        mixed = jnp.concatenate([na, ns], axis=-1).astype(BF16)
        mx_ref[...] = mixed
        x2_ref[...] = x1_ref[...] + _dot(mixed, w_ref[...])

    row = lambda i: (i, 0)
    const = lambda i: (0, 0)
    return pl.pallas_call(
        body, name="mixout_fwd", grid=(SEQ // tm,),
        in_specs=[pl.BlockSpec((tm, ATTN_WIDTH), row), pl.BlockSpec((tm, SSM_WIDTH), row),
                  pl.BlockSpec((SSM_WIDTH, SSM_WIDTH), const), pl.BlockSpec((1, SSM_WIDTH), const),
                  pl.BlockSpec((1, ATTN_WIDTH), const), pl.BlockSpec((1, SSM_WIDTH), const),
                  pl.BlockSpec((D_MODEL, D_MODEL), const), pl.BlockSpec((tm, D_MODEL), row)],
        out_specs=[pl.BlockSpec((tm, D_MODEL), row), pl.BlockSpec((tm, D_MODEL), row)],
        out_shape=[jax.ShapeDtypeStruct((SEQ, D_MODEL), F32), jax.ShapeDtypeStruct((SEQ, D_MODEL), BF16)],
        compiler_params=_cparams(("parallel",)),
    )(o, y, glu_w, glu_b, gan, gsn, wout, x1)


def _mixout_bwd(dx2, o, y, glu_w, glu_b, gan, gsn, wout):
    tm = MIX_TM

    def body(dx2_ref, o_ref, y_ref, gw_ref, gb_ref, gan_ref, gsn_ref, w_ref,
             do_ref, dy_ref, dz_ref, yg_ref, dxb_ref, dgan_ref, dgsn_ref, dgb_ref):
        i = pl.program_id(0)
        dxb = dx2_ref[...].astype(BF16)
        dxb_ref[...] = dxb
        dmixed = _dot_nt(dxb, w_ref[...])
        do, dgan = _rms_bwd(dmixed[:, :ATTN_WIDTH], o_ref[...], gan_ref[...])
        do_ref[...] = do
        yv = y_ref[...]
        yg = _gelu(yv)
        ygb = yg.astype(BF16)
        yg_ref[...] = ygb
        sg = _sigmoid(_dot(ygb, gw_ref[...]) + gb_ref[...])
        dso, dgsn = _rms_bwd(dmixed[:, ATTN_WIDTH:], yg * sg, gsn_ref[...])
        dz = dso * yg * sg * (1.0 - sg)
        dzb = dz.astype(BF16)
        dz_ref[...] = dzb
        dyg = dso * sg + _dot_nt(dzb, gw_ref[...])
        dy_ref[...] = dyg * _gelu_grad(yv)
        dgb = jnp.sum(dz, axis=0, keepdims=True)

        @pl.when(i == 0)
        def _():
            dgan_ref[...] = dgan
            dgsn_ref[...] = dgsn
            dgb_ref[...] = dgb

        @pl.when(i != 0)
        def _():
            dgan_ref[...] += dgan
            dgsn_ref[...] += dgsn
            dgb_ref[...] += dgb

    row = lambda i: (i, 0)
    const = lambda i: (0, 0)
    return pl.pallas_call(
        body, name="mixout_bwd", grid=(SEQ // tm,),
        in_specs=[pl.BlockSpec((tm, D_MODEL), row), pl.BlockSpec((tm, ATTN_WIDTH), row),
                  pl.BlockSpec((tm, SSM_WIDTH), row),
                  pl.BlockSpec((SSM_WIDTH, SSM_WIDTH), const), pl.BlockSpec((1, SSM_WIDTH), const),
                  pl.BlockSpec((1, ATTN_WIDTH), const), pl.BlockSpec((1, SSM_WIDTH), const),
                  pl.BlockSpec((D_MODEL, D_MODEL), const)],
        out_specs=[pl.BlockSpec((tm, ATTN_WIDTH), row), pl.BlockSpec((tm, SSM_WIDTH), row),
                   pl.BlockSpec((tm, SSM_WIDTH), row), pl.BlockSpec((tm, SSM_WIDTH), row),
                   pl.BlockSpec((tm, D_MODEL), row),
                   pl.BlockSpec((1, ATTN_WIDTH), const), pl.BlockSpec((1, SSM_WIDTH), const),
                   pl.BlockSpec((1, SSM_WIDTH), const)],
        out_shape=[jax.ShapeDtypeStruct((SEQ, ATTN_WIDTH), F32), jax.ShapeDtypeStruct((SEQ, SSM_WIDTH), F32),
                   jax.ShapeDtypeStruct((SEQ, SSM_WIDTH), BF16), jax.ShapeDtypeStruct((SEQ, SSM_WIDTH), BF16),
                   jax.ShapeDtypeStruct((SEQ, D_MODEL), BF16),
                   jax.ShapeDtypeStruct((1, ATTN_WIDTH), F32), jax.ShapeDtypeStruct((1, SSM_WIDTH), F32),
                   jax.ShapeDtypeStruct((1, SSM_WIDTH), F32)],
        compiler_params=_cparams(("arbitrary",)),
    )(dx2, o, y, glu_w, glu_b, gan, gsn, wout)


def _loss_head(x, g, target):
    tm = MIX_TM

    def body(x_ref, g_ref, t_ref, loss_ref, dx_ref, dg_ref):
        i = pl.program_id(0)
        xv = x_ref[...]
        gv = g_ref[...]
        err = _rms_fwd(xv, gv) - t_ref[...]
        part = jnp.broadcast_to(0.5 * jnp.sum(err * err) / D_MODEL, (1, LANES))
        dx, dg = _rms_bwd(err * (1.0 / D_MODEL), xv, gv)
        dx_ref[...] = dx

        @pl.when(i == 0)
        def _():
            loss_ref[...] = part
            dg_ref[...] = dg

        @pl.when(i != 0)
        def _():
            loss_ref[...] += part
            dg_ref[...] += dg

    row = lambda i: (i, 0)
    const = lambda i: (0, 0)
    return pl.pallas_call(
        body, name="loss_head", grid=(SEQ // tm,),
        in_specs=[pl.BlockSpec((tm, D_MODEL), row), pl.BlockSpec((1, D_MODEL), const),
                  pl.BlockSpec((tm, D_MODEL), row)],
        out_specs=[pl.BlockSpec((1, LANES), const), pl.BlockSpec((tm, D_MODEL), row),
                   pl.BlockSpec((1, D_MODEL), const)],
        out_shape=[jax.ShapeDtypeStruct((1, LANES), F32), jax.ShapeDtypeStruct((SEQ, D_MODEL), F32),
                   jax.ShapeDtypeStruct((1, D_MODEL), F32)],
        compiler_params=_cparams(("arbitrary",)),
    )(x, g, target)


def _local_step(x, target, w, p, late_weights, early_grads, after=None):
    x1, h1, a1, b1 = _ffn_fwd(x, p["norm_ffn1"], w["wgt1"], w["wut1"], w["wd1"], "ffn1_fwd", after=after)
    h2, q, k, v, u = _mixin_fwd(x1, p["norm_mix"], w["wint"])
    kp = jnp.pad(k, ((WINDOW, WINDOW), (0, 0)))
    vp = jnp.pad(v, ((WINDOW, WINDOW), (0, 0)))
    o = _attn_fwd(q, kp, vp, p["attn_sinks"])

    lam_re = p["ssm_lambda_re"].reshape(2 * N_LANE_BLOCKS, LANES)
    lam_im = p["ssm_lambda_im"].reshape(2 * N_LANE_BLOCKS, LANES)
    log_dt = jnp.repeat(p["ssm_log_dt"].reshape(2, 32), 64, axis=-1).reshape(2 * N_LANE_BLOCKS, LANES)
    a_re, a_im, bbr, bbi, cre, cim = _ssm_prep(lam_re, lam_im, log_dt, p["ssm_b_re"], p["ssm_b_im"],
                                               p["ssm_c_re"], p["ssm_c_im"])
    shape_a = (2, N_LANE_BLOCKS, 1, LANES)
    shape_w = (2, N_LANE_BLOCKS, LANES, LANES)
    a_re4, a_im4 = a_re.reshape(shape_a), a_im.reshape(shape_a)
    bbr4, bbi4 = bbr.reshape(shape_w), bbi.reshape(shape_w)
    cre, cim = cre.reshape(shape_w), cim.reshape(shape_w)
    dskip = p["ssm_d"].T.reshape(1, SSM_WIDTH)
    y, xr, xi = _ssm_fwd(u, a_re4, a_im4, bbr4, bbi4, cre, cim, dskip)

    w2 = late_weights(y)
    x2, mixed = _mixout_fwd(o, y, w2["glu"], p["ssm_glu_b"], p["attn_out_norm"], p["ssm_out_norm"], w2["wout"], x1)
    x3, h3, a3, b3 = _ffn_fwd(x2, p["norm_ffn2"], w2["wgt2"], w2["wut2"], w2["wd2"], "ffn2_fwd")

    loss, dx3, d_final = _loss_head(x3, p["final_norm"], target)
    dx2, da3, db3, s3, df3, d_n2 = _ffn_bwd_act(dx3, x2, p["norm_ffn2"], a3, b3, w2["wgt2"], w2["wut2"], w2["wd2"],
                                                "ffn2_bwd_act")
    g_wgt2, g_wut2, g_wd2 = _mm_tn([(da3, h3), (db3, h3), (s3, df3)], "ffn2_bwd_w")

    do, dy, dz, ygb, dx2b, d_gan, d_gsn, d_glub = _mixout_bwd(
        dx2, o, y, w2["glu"], p["ssm_glu_b"], p["attn_out_norm"], p["ssm_out_norm"], w2["wout"])
    (g_wout,) = _mm_tn([(mixed, dx2b)], "wout_bwd_w")
    (g_glu,) = _mm_tn([(ygb, dz)], "glu_bwd_w")
    sent = early_grads(dict(glu=g_glu, wout=g_wout, wgt2=g_wgt2, wut2=g_wut2, wd2=g_wd2))

    du, d_dskip, dcre, dcim, dbbr, dbbi, dar, dai = _ssm_bwd(dy, u, xr, xi, a_re4, a_im4, bbr4, bbi4, cre, cim, dskip,
                                                             after=sent)
    nb = 2 * N_LANE_BLOCKS
    blocks3 = (nb, LANES, LANES)
    g_lre, g_lim, g_ldt, g_btr, g_bti, g_cre, g_cim = _ssm_prep_bwd(
        lam_re, lam_im, log_dt, p["ssm_b_re"], p["ssm_b_im"], dar.reshape(nb, LANES), dai.reshape(nb, LANES),
        dbbr.reshape(blocks3), dbbi.reshape(blocks3), dcre.reshape(blocks3), dcim.reshape(blocks3))

    dq, dkp, dvp, d_sinks = _attn_bwd(q, kp, vp, p["attn_sinks"], do)
    dk = dkp[WINDOW:WINDOW + SEQ]
    dv = dvp[WINDOW:WINDOW + SEQ]
    dx1, dproj, d_nmix = _mixin_bwd(dq, dk, dv, du, w["wint"], x1, p["norm_mix"], dx2)
    (g_wint,) = _mm_tn([(dproj, h2)], "win_bwd_w")

    dx0, da1, db1, s1, df1, d_n1 = _ffn_bwd_act(dx1, x, p["norm_ffn1"], a1, b1, w["wgt1"], w["wut1"], w["wd1"],
                                                "ffn1_bwd_act")
    g_wgt1, g_wut1, g_wd1 = _mm_tn([(da1, h1), (db1, h1), (s1, df1)], "ffn1_bwd_w")

    big = dict(wgt1=g_wgt1, wut1=g_wut1, wd1=g_wd1, wint=g_wint)
    small = dict(
        norm_ffn1=d_n1, norm_mix=d_nmix, attn_sinks=d_sinks,
        ssm_lambda_re=g_lre.reshape(64, 64), ssm_lambda_im=g_lim.reshape(64, 64),
        ssm_log_dt=g_ldt.reshape(2, 32), ssm_b_re=g_btr, ssm_b_im=g_bti, ssm_c_re=g_cre, ssm_c_im=g_cim,
        ssm_d=d_dskip.reshape(32, 16).T, ssm_glu_b=d_glub, attn_out_norm=d_gan, ssm_out_norm=d_gsn,
        norm_ffn2=d_n2, final_norm=d_final, loss=loss)
    return loss, dx0, big, small


BIG = dict(
    wgt1=("ffn1_w_gate", 352, 1024, True), wut1=("ffn1_w_up", 352, 1024, True), wd1=("ffn1_w_down", 352, 1024, False),
    wint=("w_in", 160, 1024, True), glu=("ssm_glu_w", 64, 512, False), wout=("w_out", 128, 1024, False),
    wgt2=("ffn2_w_gate", 352, 1024, True), wut2=("ffn2_w_up", 352, 1024, True), wd2=("ffn2_w_down", 352, 1024, False))

SMALL = dict(
    norm_ffn1=(1, 1024), norm_mix=(1, 1024), attn_sinks=(1, 8), ssm_lambda_re=(64, 64), ssm_lambda_im=(64, 64),
    ssm_log_dt=(2, 32), ssm_b_re=(1024, 64), ssm_b_im=(1024, 64), ssm_c_re=(1024, 64), ssm_c_im=(1024, 64),
    ssm_d=(16, 32), ssm_glu_b=(1, 512), attn_out_norm=(1, 512), ssm_out_norm=(1, 512), norm_ffn2=(1, 1024),
    final_norm=(1, 1024), loss=(1, 128))
SMALL_TRANSPOSED = ("ssm_b_re", "ssm_b_im", "ssm_d")
SMALL_PARAMS = tuple(n for n in SMALL if n != "loss")

SMALL_PAIRS = (("ssm_lambda_re", "ssm_lambda_im"), ("ssm_c_re", "ssm_c_im"), ("ssm_b_re", "ssm_b_im"))
SMALL_VECS = ("norm_ffn1", "norm_mix", "norm_ffn2", "final_norm", "ssm_glu_b", "attn_out_norm", "ssm_out_norm")
SMALL_TILES = ("ssm_log_dt", "attn_sinks", "ssm_d", "loss")


def _small_offsets():
    off, table = 0, {}
    for re, im in SMALL_PAIRS:
        table[re] = table[im] = off
        off += SMALL[re][0]
    for n in SMALL_VECS:
        table[n] = off
        off += SMALL[n][1] // LANES
    for n in SMALL_TILES:
        off = -(-off // 8) * 8
        table[n] = off
        off += SMALL[n][0]
    return table, off


SMALL_OFFSET, SMALL_USED_ROWS = _small_offsets()
SMALL_ROWS = -(-SMALL_USED_ROWS // (8 * N_DEV)) * 8 * N_DEV


def _cast_shards(shards):
    names = list(BIG)

    def body(*refs):
        ins, outs = refs[:len(names)], refs[len(names):]
        for idx in range(len(names)):
            outs[idx][...] = ins[idx][...].astype(BF16)

    return pl.pallas_call(
        body, name="cast_shards",
        out_shape=[jax.ShapeDtypeStruct((BIG[n][1], BIG[n][2]), BF16) for n in names],
        compiler_params=_cparams(),
    )(*[shards[n] for n in names])


def _peer(x, y, c, r):
    px = 1 - x if r & 4 else x
    py = 1 - y if r & 2 else y
    pc = 1 - c if r & 1 else c
    return px, py, pc


FIRST_GROUP = ("wgt1", "wut1", "wd1", "wint")
LATE_GROUP = ("glu", "wout", "wgt2", "wut2", "wd2")
N_PEERS = N_DEV - 1
ANY_SPEC = pl.BlockSpec(memory_space=pl.ANY)
HBM_SPEC = pl.BlockSpec(memory_space=pltpu.HBM)
SEM_SPEC = pl.BlockSpec(memory_space=pltpu.SEMAPHORE)
DATAFLOW_EFFECT = pltpu.SideEffectType.DATAFLOW_SIDE_EFFECTING


def _mesh_pos():
    x, y, c = lax.axis_index("x"), lax.axis_index("y"), lax.axis_index("c")
    return x, y, c, 4 * x + 2 * y + c


def _gather_first(first, late):
    nf, nl = len(first), len(late)

    def body(*refs):
        f_in, l_in = refs[:nf], refs[nf:nf + nl]
        f_out, l_out = refs[nf + nl:2 * nf + nl], refs[2 * nf + nl:2 * (nf + nl)]
        send_sems, recv_sems, local_sems = refs[2 * (nf + nl):]
        x, y, c, me = _mesh_pos()
        sibling = (x, y, 1 - c)
        chips = [(x, 1 - y), (1 - x, y), (1 - x, 1 - y)]

        def idx(px, py, pc):
            return 4 * px + 2 * py + pc

        def copy(k, s, block, to, src=None):
            slot = f_out[k].at[block]
            return pltpu.make_async_remote_copy(
                src_ref=slot if src is None else src, dst_ref=slot, send_sem=send_sems.at[k, s],
                recv_sem=recv_sems.at[k, s], device_id=to, device_id_type=MESH_ID)

        local = []
        for k in range(nf + nl):
            src, dst = (f_in[k], f_out[k]) if k < nf else (l_in[k - nf], l_out[k - nf])
            mine = pltpu.make_async_copy(src, dst.at[me], local_sems.at[k])
            mine.start()
            local.append(mine)
        sends = []
        for j, chip in enumerate(chips):
            for k in range(nf):
                sends.append(copy(k, 1 + j, me, (*chip, c), src=f_in[k]))
                sends[-1].start()
        for k in range(nf):
            sends.append(copy(k, 0, me, sibling, src=f_in[k]))
            sends[-1].start()
        for j, chip in enumerate(chips):
            for k in range(nf):
                copy(k, 1 + j, idx(*chip, c), (*chip, c)).wait_recv()
                sends.append(copy(k, 4 + j, idx(*chip, c), sibling))
                sends[-1].start()
        for k in range(nf):
            copy(k, 0, idx(*sibling), sibling).wait_recv()
        for j, chip in enumerate(chips):
            for k in range(nf):
                copy(k, 4 + j, idx(*chip, 1 - c), sibling).wait_recv()
        for cp in sends:
            cp.wait_send()
        for cp in local:
            cp.wait()

    return pl.pallas_call(
        body, name="gather_first",
        in_specs=[ANY_SPEC] * (nf + nl), out_specs=[ANY_SPEC] * (nf + nl),
        out_shape=[jax.ShapeDtypeStruct((N_DEV,) + s.shape, s.dtype) for s in list(first) + list(late)],
        scratch_shapes=[pltpu.SemaphoreType.DMA((nf, N_PEERS)), pltpu.SemaphoreType.DMA((nf, N_PEERS)),
                        pltpu.SemaphoreType.DMA((nf + nl,))],
        compiler_params=pltpu.CompilerParams(has_side_effects=True),
    )(*first, *late)


def _split_copy(src_refs, land_refs, send_sems, recv_sems, k, r, pos, scatter, receiving):
    x, y, c, me = pos
    px, py, pc = _peer(x, y, c, r)
    peer_idx = 4 * px + 2 * py + pc
    if scatter:
        src, dst = src_refs[k].at[peer_idx], land_refs[k].at[r - 1]
    else:
        src, dst = src_refs[k], land_refs[k].at[peer_idx if receiving else me]
    return pltpu.make_async_remote_copy(
        src_ref=src, dst_ref=dst, send_sem=send_sems.at[k * N_PEERS + r - 1],
        recv_sem=recv_sems.at[k * N_PEERS + r - 1], device_id=(px, py, pc), device_id_type=MESH_ID)


def _split_start(name, srcs, lands, scatter):
    n = len(srcs)

    def body(*refs):
        src_refs, land_refs = refs[:n], refs[n:2 * n]
        send_sems, recv_sems = refs[2 * n], refs[2 * n + 1]
        token = refs[-1]
        pos = _mesh_pos()
        for k in range(n):
            for r in range(1, N_DEV):
                _split_copy(src_refs, land_refs, send_sems, recv_sems, k, r, pos, scatter, False).start()
        token[...] = jnp.zeros_like(token)

    thru = [pltpu.HBM(a.shape, a.dtype) for a in list(srcs) + list(lands)]
    outs = pl.pallas_call(
        body, name=name,
        in_specs=[HBM_SPEC] * (2 * n),
        out_specs=[SEM_SPEC, SEM_SPEC] + [HBM_SPEC] * (2 * n) + [pl.BlockSpec(memory_space=pltpu.VMEM)],
        out_shape=[pltpu.SemaphoreType.DMA((n * N_PEERS,)), pltpu.SemaphoreType.DMA((n * N_PEERS,))] + thru
        + [jax.ShapeDtypeStruct((8, LANES), F32)],
        input_output_aliases={i: 2 + i for i in range(2 * n)},
        compiler_params=pltpu.CompilerParams(has_side_effects=DATAFLOW_EFFECT),
    )(*[pltpu.with_memory_space_constraint(a, pltpu.HBM) for a in list(srcs) + list(lands)])
    return outs[0], outs[1], outs[2:2 + n], outs[2 + n:2 + 2 * n], outs[-1]


def _split_wait(name, send_sems, recv_sems, srcs, lands, scatter, after):
    n = len(srcs)

    def body(*refs):
        src_refs, land_refs = refs[:n], refs[n:2 * n]
        send, recv = refs[2 * n], refs[2 * n + 1]
        pos = _mesh_pos()
        for k in range(n):
            for r in range(1, N_DEV):
                cp = _split_copy(src_refs, land_refs, send, recv, k, r, pos, scatter, True)
                cp.wait_send()
                cp.wait_recv()

    thru = [pltpu.HBM(a.shape, a.dtype) for a in list(srcs) + list(lands)]
    outs = pl.pallas_call(
        body, name=name,
        in_specs=[HBM_SPEC] * (2 * n) + [SEM_SPEC, SEM_SPEC, ANY_SPEC],
        out_specs=[HBM_SPEC] * (2 * n), out_shape=thru,
        input_output_aliases={i: i for i in range(2 * n)},
        compiler_params=pltpu.CompilerParams(has_side_effects=DATAFLOW_EFFECT),
    )(*srcs, *lands, send_sems, recv_sems, after)
    return outs[:n], outs[n:]


def _exchange_last(grads, small_packed):
    ng = len(grads)
    ch = SMALL_ROWS // N_DEV
    max_rows = max(g.shape[1] for g in grads)
    cols = grads[0].shape[2]

    def body(*refs):
        g_in, s_in = refs[:ng], refs[ng]
        outs = refs[ng + 1:]
        own_out, land, stage = outs[:ng], outs[ng:2 * ng], outs[2 * ng:3 * ng]
        s_red, s_stage = outs[3 * ng], outs[3 * ng + 1]
        (va, vb, vo, vs, sm_in, sm_out, d2d_send, d2d_recv, ici_send, ici_recv, s1_send, s1_recv, s2_send, s2_recv,
         local_sems) = outs[3 * ng + 2:]
        x, y, c, me = _mesh_pos()
        sibling = (x, y, 1 - c)
        chips = [(x, y), (x, 1 - y), (1 - x, y), (1 - x, 1 - y)]

        def idx(chip, core):
            return 4 * chip[0] + 2 * chip[1] + core

        def d2d(k, j):
            return pltpu.make_async_remote_copy(
                src_ref=g_in[k].at[idx(chips[j], 1 - c)], dst_ref=stage[k].at[j], send_sem=d2d_send.at[k, j],
                recv_sem=d2d_recv.at[k, j], device_id=sibling, device_id_type=MESH_ID)

        def ici(k, j, slot):
            rows = g_in[k].shape[1]
            return pltpu.make_async_remote_copy(
                src_ref=vo.at[slot, pl.ds(0, rows)], dst_ref=land[k].at[j - 1], send_sem=ici_send.at[k, j - 1],
                recv_sem=ici_recv.at[k, j - 1], device_id=(*chips[j], c), device_id_type=MESH_ID)

        def small_scatter(r):
            px, py, pc = _peer(x, y, c, r)
            return pltpu.make_async_remote_copy(
                src_ref=s_in.at[pl.ds(pl.multiple_of((4 * px + 2 * py + pc) * ch, 8), ch)], dst_ref=s_stage.at[me],
                send_sem=s1_send.at[r - 1], recv_sem=s1_recv.at[r - 1], device_id=(px, py, pc), device_id_type=MESH_ID)

        def small_gather(r):
            return pltpu.make_async_remote_copy(
                src_ref=sm_out, dst_ref=s_red.at[me], send_sem=s2_send.at[r - 1], recv_sem=s2_recv.at[r - 1],
                device_id=_peer(x, y, c, r), device_id_type=MESH_ID)

        for r in range(1, N_DEV):
            small_scatter(r).start()
        mine = pltpu.make_async_copy(s_in.at[pl.ds(pl.multiple_of(me * ch, 8), ch)], s_stage.at[me], local_sems.at[0])
        mine.start()
        for j in (1, 2, 3, 0):
            for k in range(ng):
                d2d(k, j).start()

        for r in range(1, N_DEV):
            small_scatter(r).wait_recv()
        mine.wait()
        load = pltpu.make_async_copy(s_stage, sm_in, local_sems.at[1])
        load.start()
        load.wait()
        total = sm_in[0]
        for i in range(1, N_DEV):
            total = total + sm_in[i]
        sm_out[...] = total
        for r in range(1, N_DEV):
            small_gather(r).start()
        keep = pltpu.make_async_copy(sm_out, s_red.at[me], local_sems.at[2])
        keep.start()

        pairs = [(k, j) for j in (1, 2, 3, 0) for k in range(ng)]
        in_flight = {}
        for i, (k, j) in enumerate(pairs):
            slot = i % 2
            rows = g_in[k].shape[1]
            if slot in in_flight:
                in_flight.pop(slot).wait_send()
            d2d(k, j).wait_recv()
            la = pltpu.make_async_copy(g_in[k].at[idx(chips[j], c)], va.at[slot, pl.ds(0, rows)], local_sems.at[3])
            lb = pltpu.make_async_copy(stage[k].at[j], vb.at[slot, pl.ds(0, rows)], local_sems.at[4])
            la.start()
            lb.start()
            la.wait()
            lb.wait()
            total = va[slot, pl.ds(0, rows)].astype(F32) + vb[slot, pl.ds(0, rows)].astype(F32)
            if j == 0:
                vs[pl.ds(0, rows)] = total
                st = pltpu.make_async_copy(vs.at[pl.ds(0, rows)], own_out[k], local_sems.at[5])
                st.start()
                st.wait()
            else:
                vo[slot, pl.ds(0, rows)] = total.astype(BF16)
                cp = ici(k, j, slot)
                cp.start()
                in_flight[slot] = cp
        for cp in in_flight.values():
            cp.wait_send()

        for j in (1, 2, 3, 0):
            for k in range(ng):
                d2d(k, j).wait_send()
        for j in (1, 2, 3):
            for k in range(ng):
                ici(k, j, 0).wait_recv()
        for r in range(1, N_DEV):
            small_scatter(r).wait_send()
            small_gather(r).wait_send()
            small_gather(r).wait_recv()
        keep.wait()

    out_shape = [jax.ShapeDtypeStruct(g.shape[1:], F32) for g in grads]
    out_shape += [jax.ShapeDtypeStruct((3,) + g.shape[1:], BF16) for g in grads]
    out_shape += [jax.ShapeDtypeStruct((4,) + g.shape[1:], BF16) for g in grads]
    out_shape += [jax.ShapeDtypeStruct((N_DEV, ch, LANES), F32), jax.ShapeDtypeStruct((N_DEV, ch, LANES), F32)]
    outs = pl.pallas_call(
        body, name="exchange_last",
        in_specs=[ANY_SPEC] * (ng + 1), out_specs=[ANY_SPEC] * len(out_shape), out_shape=out_shape,
        scratch_shapes=[pltpu.VMEM((2, max_rows, cols), BF16), pltpu.VMEM((2, max_rows, cols), BF16),
                        pltpu.VMEM((2, max_rows, cols), BF16), pltpu.VMEM((max_rows, cols), F32),
                        pltpu.VMEM((N_DEV, ch, LANES), F32), pltpu.VMEM((ch, LANES), F32),
                        pltpu.SemaphoreType.DMA((ng, 4)), pltpu.SemaphoreType.DMA((ng, 4)),
                        pltpu.SemaphoreType.DMA((ng, 3)), pltpu.SemaphoreType.DMA((ng, 3)),
                        pltpu.SemaphoreType.DMA((N_PEERS,)), pltpu.SemaphoreType.DMA((N_PEERS,)),
                        pltpu.SemaphoreType.DMA((N_PEERS,)), pltpu.SemaphoreType.DMA((N_PEERS,)),
                        pltpu.SemaphoreType.DMA((6,))],
        compiler_params=pltpu.CompilerParams(has_side_effects=True, vmem_limit_bytes=VMEM_LIMIT),
    )(*grads, small_packed)
    return outs[:ng], outs[ng:2 * ng], outs[3 * ng].reshape(SMALL_ROWS, LANES)


def _adamw_math(w, g, m, v):
    m2 = ADAM_B1 * m + (1.0 - ADAM_B1) * g
    v2 = ADAM_B2 * v + (1.0 - ADAM_B2) * (g * g)
    m_hat = m2 / (1.0 - ADAM_B1 ** ADAM_STEP)
    v_hat = v2 / (1.0 - ADAM_B2 ** ADAM_STEP)
    delta = -ADAM_LR * (m_hat / (jnp.sqrt(v_hat) + ADAM_EPS) + ADAM_WD * w)
    return delta, m2, v2


def _adamw_big(own, parts, w, m, v, name):
    shape = w.shape
    own_is_blocks = own.ndim == 3

    def body(own_ref, p_ref, w_ref, m_ref, v_ref, g_ref, d_ref, m2_ref, v2_ref, own_s, sem):
        if own_is_blocks:
            cp = pltpu.make_async_copy(own_ref.at[_mesh_pos()[3]], own_s, sem)
        else:
            cp = pltpu.make_async_copy(own_ref, own_s, sem)
        cp.start()
        cp.wait()
        g = own_s[...].astype(F32)
        for i in range(parts.shape[0]):
            g = g + p_ref[i].astype(F32)
        delta, m2, v2 = _adamw_math(w_ref[...], g, m_ref[...], v_ref[...])
        g_ref[...] = g
        d_ref[...] = delta
        m2_ref[...] = m2
        v2_ref[...] = v2

    vmem = pl.BlockSpec(memory_space=pltpu.VMEM)
    return pl.pallas_call(
        body, name=name, in_specs=[ANY_SPEC, vmem, vmem, vmem, vmem], out_specs=[vmem] * 4,
        out_shape=[jax.ShapeDtypeStruct(shape, F32)] * 4,
        scratch_shapes=[pltpu.VMEM(own.shape[-2:], own.dtype), pltpu.SemaphoreType.DMA(())],
        compiler_params=_cparams(),
    )(own, parts, w, m, v)


def _pack_small(grads):
    names = list(SMALL)

    def body(*refs):
        ins, out = dict(zip(names, refs[:-1])), refs[-1]
        out[...] = jnp.zeros_like(out)
        for re, im in SMALL_PAIRS:
            off, rows = SMALL_OFFSET[re], SMALL[re][0]
            out[off:off + rows, :] = jnp.concatenate([ins[re][...], ins[im][...]], axis=1)
        for n in SMALL_VECS:
            off, vec = SMALL_OFFSET[n], ins[n][...]
            for i in range(SMALL[n][1] // LANES):
                out[off + i:off + i + 1, :] = vec[:, i * LANES:(i + 1) * LANES]
        for n in SMALL_TILES:
            off, (rows, cols) = SMALL_OFFSET[n], SMALL[n]
            out[off:off + rows, 0:cols] = ins[n][...]

    return pl.pallas_call(
        body, name="pack_small", out_shape=jax.ShapeDtypeStruct((SMALL_ROWS, LANES), F32),
        compiler_params=_cparams(),
    )(*[grads[n] for n in names])


def _unpack_small_ref(g_ref, n):
    off, (rows, cols) = SMALL_OFFSET[n], SMALL[n]
    for re, im in SMALL_PAIRS:
        if n == re:
            return g_ref[off:off + rows, 0:HALF_LANES]
        if n == im:
            return g_ref[off:off + rows, HALF_LANES:LANES]
    if n in SMALL_VECS:
        return jnp.concatenate([g_ref[off + i:off + i + 1, :] for i in range(cols // LANES)], axis=1)
    return g_ref[off:off + rows, 0:cols]


def _adamw_small(g_packed, w, m, v):
    names = list(SMALL_PARAMS)
    n = len(names)

    def body(g_ref, *refs):
        w_refs, m_refs, v_refs, outs = refs[:n], refs[n:2 * n], refs[2 * n:3 * n], refs[3 * n:]
        for idx, name in enumerate(names):
            g = _unpack_small_ref(g_ref, name)
            delta, m2, v2 = _adamw_math(w_refs[idx][...], g, m_refs[idx][...], v_refs[idx][...])
            outs[4 * idx][...] = g
            outs[4 * idx + 1][...] = delta
            outs[4 * idx + 2][...] = m2
            outs[4 * idx + 3][...] = v2
        outs[4 * n][...] = _unpack_small_ref(g_ref, "loss")

    outs = pl.pallas_call(
        body, name="adamw_small",
        out_shape=[jax.ShapeDtypeStruct(SMALL[name], F32) for name in names for _ in range(4)]
        + [jax.ShapeDtypeStruct(SMALL["loss"], F32)],
        compiler_params=_cparams(),
    )(g_packed, *[w[k] for k in names], *[m[k] for k in names], *[v[k] for k in names])
    return {name: outs[4 * idx:4 * idx + 4] for idx, name in enumerate(names)}, outs[4 * n]


WEIGHT_NAMES = ['norm_ffn1', 'ffn1_w_gate', 'ffn1_w_up', 'ffn1_w_down', 'norm_mix', 'w_in', 'attn_sinks',
                'ssm_lambda_re', 'ssm_lambda_im', 'ssm_log_dt', 'ssm_b_re', 'ssm_b_im', 'ssm_c_re', 'ssm_c_im',
                'ssm_d', 'ssm_glu_w', 'ssm_glu_b', 'attn_out_norm', 'ssm_out_norm', 'w_out', 'norm_ffn2',
                'ffn2_w_gate', 'ffn2_w_up', 'ffn2_w_down', 'final_norm']


def kernel(x, norm_ffn1, ffn1_w_gate, ffn1_w_up, ffn1_w_down, norm_mix, w_in, attn_sinks, ssm_lambda_re, ssm_lambda_im, ssm_log_dt, ssm_b_re, ssm_b_im, ssm_c_re, ssm_c_im, ssm_d, ssm_glu_w, ssm_glu_b, attn_out_norm, ssm_out_norm, w_out, norm_ffn2, ffn2_w_gate, ffn2_w_up, ffn2_w_down, final_norm, loss_target, m_norm_ffn1, m_ffn1_w_gate, m_ffn1_w_up, m_ffn1_w_down, m_norm_mix, m_w_in, m_attn_sinks, m_ssm_lambda_re, m_ssm_lambda_im, m_ssm_log_dt, m_ssm_b_re, m_ssm_b_im, m_ssm_c_re, m_ssm_c_im, m_ssm_d, m_ssm_glu_w, m_ssm_glu_b, m_attn_out_norm, m_ssm_out_norm, m_w_out, m_norm_ffn2, m_ffn2_w_gate, m_ffn2_w_up, m_ffn2_w_down, m_final_norm, v_norm_ffn1, v_ffn1_w_gate, v_ffn1_w_up, v_ffn1_w_down, v_norm_mix, v_w_in, v_attn_sinks, v_ssm_lambda_re, v_ssm_lambda_im, v_ssm_log_dt, v_ssm_b_re, v_ssm_b_im, v_ssm_c_re, v_ssm_c_im, v_ssm_d, v_ssm_glu_w, v_ssm_glu_b, v_attn_out_norm, v_ssm_out_norm, v_w_out, v_norm_ffn2, v_ffn2_w_gate, v_ffn2_w_up, v_ffn2_w_down, v_final_norm):
    args = dict(locals())
    weights = {n: args[n] for n in WEIGHT_NAMES}
    moms = {n: args["m_" + n] for n in WEIGHT_NAMES}
    vars_ = {n: args["v_" + n] for n in WEIGHT_NAMES}

    def shard2d(a, k):
        a = a.reshape(a.shape[-2], a.shape[-1])
        return a.T if BIG[k][3] else a

    def shard_master(a, k):
        return (a.T if BIG[k][3] else a).reshape(weights[BIG[k][0]].shape)

    def blocks(g, k):
        return g.reshape(N_DEV, BIG[k][1], BIG[k][2])

    def full(g, k):
        return g.reshape(N_DEV * BIG[k][1], BIG[k][2])

    shards = dict(zip(BIG, _cast_shards({k: shard2d(weights[BIG[k][0]], k) for k in BIG})))
    nf = len(FIRST_GROUP)
    got = _gather_first([shards[k] for k in FIRST_GROUP], [shards[k] for k in LATE_GROUP])
    w_first = {k: full(g, k) for k, g in zip(FIRST_GROUP, got[:nf])}
    w_send, w_recv, w_srcs, w_lands, w_token = _split_start(
        "gather_late_start", [shards[k] for k in LATE_GROUP], got[nf:], scatter=False)

    def late_weights(dep):
        _, lands = _split_wait("gather_late_wait", w_send, w_recv, w_srcs, w_lands, False, dep)
        return {k: full(g, k) for k, g in zip(LATE_GROUP, lands)}

    early = {}

    def early_grads(g):
        srcs = [blocks(g[k], k) for k in LATE_GROUP]
        lands = [lax.empty((N_PEERS, BIG[k][1], BIG[k][2]), BF16) for k in LATE_GROUP]
        early["send"], early["recv"], early["srcs"], early["lands"], token = _split_start(
            "grads_late_start", srcs, lands, scatter=True)
        return token

    def small2d(a, n):
        if n in SMALL_TRANSPOSED:
            a = jnp.swapaxes(a, -1, -2)
        return a.reshape(SMALL[n])

    def small_master(a, n):
        if n in SMALL_TRANSPOSED:
            shape = weights[n].shape
            return jnp.swapaxes(a.reshape(shape[:-2] + (shape[-1], shape[-2])), -1, -2)
        return a.reshape(weights[n].shape)

    small_p = {n: small2d(weights[n], n) for n in SMALL_PARAMS}
    _, grad_x, g_first, g_small = _local_step(
        x.reshape(SEQ, D_MODEL), loss_target.reshape(SEQ, D_MODEL), w_first, small_p, late_weights, early_grads,
        after=w_token)

    own_sums, first_parts, small_grad = _exchange_last([blocks(g_first[k], k) for k in FIRST_GROUP],
                                                       _pack_small(g_small))
    own_late, late_parts = _split_wait("grads_late_wait", early["send"], early["recv"], early["srcs"],
                                       early["lands"], True, small_grad)
    own = dict(zip(FIRST_GROUP + LATE_GROUP, list(own_sums) + list(own_late)))
    parts = dict(zip(FIRST_GROUP + LATE_GROUP, list(first_parts) + list(late_parts)))
    outs = {}
    for k in BIG:
        n = BIG[k][0]
        outs[n] = [shard_master(o, k) for o in
                   _adamw_big(own[k], parts[k], shard2d(weights[n], k), shard2d(moms[n], k), shard2d(vars_[n], k),
                              "adamw_" + n)]
    small_out, loss_row = _adamw_small(small_grad, small_p, {n: small2d(moms[n], n) for n in SMALL_PARAMS},
                                       {n: small2d(vars_[n], n) for n in SMALL_PARAMS})
    for n in SMALL_PARAMS:
        outs[n] = [small_master(o, n) for o in small_out[n]]

    result = [loss_row[0, 0], grad_x.reshape(x.shape)]
    for i in range(4):
        result += [outs[n][i] for n in WEIGHT_NAMES]
    return tuple(result)
```

```python
import functools

import jax
import jax.numpy as jnp
from jax import lax
from jax.experimental import pallas as pl
from jax.experimental.pallas import tpu as pltpu

F32 = jnp.float32
BF16 = jnp.bfloat16

N_DEV = 8
SEQ = 2048
D_MODEL = 1024
D_FF = 2816
ATTN_HEADS = 8
KV_HEADS = 2
HEAD_DIM = 64
ATTN_WIDTH = 512
KV_WIDTH = 128
WINDOW = 128
SSM_WIDTH = 512
IN_WIDTH = 1280
EPS = 1e-6
NEG_INF = -1e30
LAMBDA_RE_MAX = -1e-4
LANES = 128
N_LANE_BLOCKS = 16
SCAN_CHUNK = SEQ // 8

ADAM_LR = 0.001
ADAM_B1 = 0.9
ADAM_B2 = 0.999
ADAM_EPS = 1e-08
ADAM_WD = 0.01
ADAM_STEP = 10

VMEM_LIMIT = 56 * 1024 * 1024
MESH_ID = pl.DeviceIdType.MESH


def _cparams(sem=None):
    return pltpu.CompilerParams(dimension_semantics=sem, vmem_limit_bytes=VMEM_LIMIT)


def _dot(a, b):
    return jnp.dot(a, b, preferred_element_type=F32)


def _dot_nt(a, b):
    return lax.dot_general(a, b, (((1,), (1,)), ((), ())), preferred_element_type=F32)


def _dot_tn(a, b):
    return lax.dot_general(a, b, (((0,), (0,)), ((), ())), preferred_element_type=F32)


def _rms_fwd(x, g):
    r = lax.rsqrt(jnp.mean(x * x, axis=-1, keepdims=True) + EPS)
    return x * r * g


def _rms_bwd(dh, x, g):
    r = lax.rsqrt(jnp.mean(x * x, axis=-1, keepdims=True) + EPS)
    xh = x * r
    dg = jnp.sum(dh * xh, axis=0, keepdims=True)
    dxh = dh * g
    dx = r * (dxh - xh * jnp.mean(dxh * xh, axis=-1, keepdims=True))
    return dx, dg


def _sigmoid(x):
    return 1.0 / (1.0 + jnp.exp(-x))


FFN_TM = 512
FFN_TF = 1408


def _ffn_fwd(x, g, wgt, wut, wd, name, after=None):
    tm, tf = FFN_TM, FFN_TF
    nj = D_FF // tf
    deps = [] if after is None else [after]

    def body(x_ref, g_ref, wg_ref, wu_ref, wd_ref, *rest):
        xo_ref, h_ref, a_ref, b_ref, h_s, acc = rest[len(deps):]
        j = pl.program_id(1)

        @pl.when(j == 0)
        def _():
            h = _rms_fwd(x_ref[...], g_ref[...]).astype(BF16)
            h_s[...] = h
            h_ref[...] = h
            acc[...] = jnp.zeros_like(acc)

        h = h_s[...]
        a = _dot_nt(h, wg_ref[...])
        b = _dot_nt(h, wu_ref[...])
        a_ref[...] = a.astype(BF16)
        b_ref[...] = b.astype(BF16)
        s = (a * _sigmoid(a) * b).astype(BF16)
        acc[...] += _dot(s, wd_ref[...])

        @pl.when(j == nj - 1)
        def _():
            xo_ref[...] = x_ref[...] + 0.5 * acc[...]

    return pl.pallas_call(
        body, name=name, grid=(SEQ // tm, nj),
        in_specs=[pl.BlockSpec((tm, D_MODEL), lambda i, j: (i, 0)),
                  pl.BlockSpec((1, D_MODEL), lambda i, j: (0, 0)),
                  pl.BlockSpec((tf, D_MODEL), lambda i, j: (j, 0)),
                  pl.BlockSpec((tf, D_MODEL), lambda i, j: (j, 0)),
                  pl.BlockSpec((tf, D_MODEL), lambda i, j: (j, 0))] + [pl.BlockSpec(memory_space=pl.ANY)] * len(deps),
        out_specs=[pl.BlockSpec((tm, D_MODEL), lambda i, j: (i, 0)),
                   pl.BlockSpec((tm, D_MODEL), lambda i, j: (i, 0)),
                   pl.BlockSpec((tm, tf), lambda i, j: (i, j)),
                   pl.BlockSpec((tm, tf), lambda i, j: (i, j))],
        out_shape=[jax.ShapeDtypeStruct((SEQ, D_MODEL), F32), jax.ShapeDtypeStruct((SEQ, D_MODEL), BF16),
                   jax.ShapeDtypeStruct((SEQ, D_FF), BF16), jax.ShapeDtypeStruct((SEQ, D_FF), BF16)],
        scratch_shapes=[pltpu.VMEM((tm, D_MODEL), BF16), pltpu.VMEM((tm, D_MODEL), F32)],
        compiler_params=_cparams(("parallel", "arbitrary")),
    )(x, g, wgt, wut, wd, *deps)


def _ffn_bwd_act(dxo, x, g, a, b, wgt, wut, wd, name):
    tm, tf = FFN_TM // 2, FFN_TF
    nj = D_FF // tf

    def body(dxo_ref, x_ref, g_ref, a_ref, b_ref, wg_ref, wu_ref, wd_ref,
             dx_ref, da_ref, db_ref, s_ref, df_ref, dg_ref, df_s, acc):
        i = pl.program_id(0)
        j = pl.program_id(1)

        @pl.when(j == 0)
        def _():
            df = (0.5 * dxo_ref[...]).astype(BF16)
            df_s[...] = df
            df_ref[...] = df
            acc[...] = jnp.zeros_like(acc)

        ds = _dot_nt(df_s[...], wd_ref[...])
        av = a_ref[...].astype(F32)
        bv = b_ref[...].astype(F32)
        sig = _sigmoid(av)
        sl = av * sig
        s_ref[...] = (sl * bv).astype(BF16)
        db = (ds * sl).astype(BF16)
        da = (ds * bv * (sig * (1.0 + av * (1.0 - sig)))).astype(BF16)
        da_ref[...] = da
        db_ref[...] = db
        acc[...] += _dot(da, wg_ref[...]) + _dot(db, wu_ref[...])

        @pl.when(j == nj - 1)
        def _():
            dx, dg = _rms_bwd(acc[...], x_ref[...], g_ref[...])
            dx_ref[...] = dxo_ref[...] + dx

            @pl.when(i == 0)
            def _():
                dg_ref[...] = dg

            @pl.when(i != 0)
            def _():
                dg_ref[...] += dg

    row = lambda i, j: (i, 0)
    col = lambda i, j: (j, 0)
    tile = lambda i, j: (i, j)
    return pl.pallas_call(
        body, name=name, grid=(SEQ // tm, nj),
        in_specs=[pl.BlockSpec((tm, D_MODEL), row), pl.BlockSpec((tm, D_MODEL), row),
                  pl.BlockSpec((1, D_MODEL), lambda i, j: (0, 0)),
                  pl.BlockSpec((tm, tf), tile), pl.BlockSpec((tm, tf), tile),
                  pl.BlockSpec((tf, D_MODEL), col), pl.BlockSpec((tf, D_MODEL), col), pl.BlockSpec((tf, D_MODEL), col)],
        out_specs=[pl.BlockSpec((tm, D_MODEL), row),
                   pl.BlockSpec((tm, tf), tile), pl.BlockSpec((tm, tf), tile), pl.BlockSpec((tm, tf), tile),
                   pl.BlockSpec((tm, D_MODEL), row),
                   pl.BlockSpec((1, D_MODEL), lambda i, j: (0, 0))],
        out_shape=[jax.ShapeDtypeStruct((SEQ, D_MODEL), F32),
                   jax.ShapeDtypeStruct((SEQ, D_FF), BF16), jax.ShapeDtypeStruct((SEQ, D_FF), BF16),
                   jax.ShapeDtypeStruct((SEQ, D_FF), BF16),
                   jax.ShapeDtypeStruct((SEQ, D_MODEL), BF16),
                   jax.ShapeDtypeStruct((1, D_MODEL), F32)],
        scratch_shapes=[pltpu.VMEM((tm, D_MODEL), BF16), pltpu.VMEM((tm, D_MODEL), F32)],
        compiler_params=_cparams(("arbitrary", "arbitrary")),
    )(dxo, x, g, a, b, wgt, wut, wd)


def _mm_tn(pairs, name, tmm=256):
    m = pairs[0][0].shape[1]
    n_pairs = len(pairs)

    def body(*refs):
        ins, outs = refs[:2 * n_pairs], refs[2 * n_pairs:]
        for p in range(n_pairs):
            outs[p][...] = _dot_tn(ins[2 * p][...], ins[2 * p + 1][...]).astype(BF16)

    in_specs, out_specs, out_shape, args = [], [], [], []
    for a, b in pairs:
        n = b.shape[1]
        in_specs += [pl.BlockSpec((SEQ, tmm), lambda i: (0, i)), pl.BlockSpec((SEQ, n), lambda i: (0, 0))]
        out_specs.append(pl.BlockSpec((tmm, n), lambda i: (i, 0)))
        out_shape.append(jax.ShapeDtypeStruct((m, n), BF16))
        args += [a, b]
    return pl.pallas_call(body, name=name, grid=(m // tmm,), in_specs=in_specs, out_specs=out_specs,
                          out_shape=out_shape, compiler_params=_cparams(("parallel",)))(*args)


MIX_TM = 256


def _mixin_fwd(x, g, wint):
    tm = MIX_TM

    def body(x_ref, g_ref, w_ref, h_ref, q_ref, k_ref, v_ref, u_ref):
        h = _rms_fwd(x_ref[...], g_ref[...]).astype(BF16)
        h_ref[...] = h
        proj = _dot_nt(h, w_ref[...])
        q_ref[...] = proj[:, :ATTN_WIDTH]
        k_ref[...] = proj[:, ATTN_WIDTH:ATTN_WIDTH + KV_WIDTH]
        v_ref[...] = proj[:, ATTN_WIDTH + KV_WIDTH:ATTN_WIDTH + 2 * KV_WIDTH]
        u_ref[...] = proj[:, ATTN_WIDTH + 2 * KV_WIDTH:]

    row = lambda i: (i, 0)
    return pl.pallas_call(
        body, name="mixin_fwd", grid=(SEQ // tm,),
        in_specs=[pl.BlockSpec((tm, D_MODEL), row), pl.BlockSpec((1, D_MODEL), lambda i: (0, 0)),
                  pl.BlockSpec((IN_WIDTH, D_MODEL), lambda i: (0, 0))],
        out_specs=[pl.BlockSpec((tm, D_MODEL), row), pl.BlockSpec((tm, ATTN_WIDTH), row),
                   pl.BlockSpec((tm, KV_WIDTH), row), pl.BlockSpec((tm, KV_WIDTH), row),
                   pl.BlockSpec((tm, SSM_WIDTH), row)],
        out_shape=[jax.ShapeDtypeStruct((SEQ, D_MODEL), BF16), jax.ShapeDtypeStruct((SEQ, ATTN_WIDTH), F32),
                   jax.ShapeDtypeStruct((SEQ, KV_WIDTH), F32), jax.ShapeDtypeStruct((SEQ, KV_WIDTH), F32),
                   jax.ShapeDtypeStruct((SEQ, SSM_WIDTH), F32)],
        compiler_params=_cparams(("parallel",)),
    )(x, g, wint)


def _mixin_bwd(dq, dk, dv, du, wint, x, g, dres):
    tm = MIX_TM

    def body(dq_ref, dk_ref, dv_ref, du_ref, w_ref, x_ref, g_ref, dres_ref, dx_ref, dp_ref, dg_ref):
        i = pl.program_id(0)
        dp = jnp.concatenate([dq_ref[...], dk_ref[...], dv_ref[...], du_ref[...]], axis=-1).astype(BF16)
        dp_ref[...] = dp
        dh = _dot(dp, w_ref[...])
        dx, dg = _rms_bwd(dh, x_ref[...], g_ref[...])
        dx_ref[...] = dres_ref[...] + dx

        @pl.when(i == 0)
        def _():
            dg_ref[...] = dg

        @pl.when(i != 0)
        def _():
            dg_ref[...] += dg

    row = lambda i: (i, 0)
    const = lambda i: (0, 0)
    return pl.pallas_call(
        body, name="mixin_bwd", grid=(SEQ // tm,),
        in_specs=[pl.BlockSpec((tm, ATTN_WIDTH), row), pl.BlockSpec((tm, KV_WIDTH), row),
                  pl.BlockSpec((tm, KV_WIDTH), row), pl.BlockSpec((tm, SSM_WIDTH), row),
                  pl.BlockSpec((IN_WIDTH, D_MODEL), const), pl.BlockSpec((tm, D_MODEL), row),
                  pl.BlockSpec((1, D_MODEL), const), pl.BlockSpec((tm, D_MODEL), row)],
        out_specs=[pl.BlockSpec((tm, D_MODEL), row), pl.BlockSpec((tm, IN_WIDTH), row),
                   pl.BlockSpec((1, D_MODEL), const)],
        out_shape=[jax.ShapeDtypeStruct((SEQ, D_MODEL), F32), jax.ShapeDtypeStruct((SEQ, IN_WIDTH), BF16),
                   jax.ShapeDtypeStruct((1, D_MODEL), F32)],
        compiler_params=_cparams(("arbitrary",)),
    )(dq, dk, dv, du, wint, x, g, dres)


N_QBLOCKS = SEQ // WINDOW
GROUP = ATTN_HEADS // KV_HEADS
SCALE = HEAD_DIM ** -0.5


def _alibi_slope(h):
    return 2.0 ** (-8.0 * (h + 1) / ATTN_HEADS)


def _window_masks(n):
    t_idx = lax.broadcasted_iota(jnp.int32, (WINDOW, 3 * WINDOW), 0)
    s_idx = lax.broadcasted_iota(jnp.int32, (WINDOW, 3 * WINDOW), 1)
    rel = s_idx - WINDOW - t_idx
    absrel = jnp.abs(rel)
    key_pos = n * WINDOW - WINDOW + s_idx
    valid = (absrel <= WINDOW) & (key_pos >= 0) & (key_pos < SEQ)
    return absrel.astype(F32), valid


def _head_probs(qh, kw, absrel, valid, slope, sink):
    s = _dot_nt(qh, kw) * SCALE
    s = jnp.where(valid, s - slope * absrel, NEG_INF)
    m = jnp.maximum(jnp.max(s, axis=-1, keepdims=True), sink)
    p = jnp.exp(s - m)
    ps = jnp.exp(sink - m)
    inv = 1.0 / (jnp.sum(p, axis=-1, keepdims=True) + ps)
    return p * inv, ps * inv


def _attn_fwd(q, kp, vp, sinks):
    def body(sk_ref, q_ref, kp_ref, vp_ref, o_ref):
        def blk(n, carry):
            r0 = pl.multiple_of(n * WINDOW, WINDOW)
            absrel, valid = _window_masks(n)
            for gi in range(KV_HEADS):
                kw = kp_ref[pl.ds(r0, 3 * WINDOW), gi * HEAD_DIM:(gi + 1) * HEAD_DIM].astype(BF16)
                vw = vp_ref[pl.ds(r0, 3 * WINDOW), gi * HEAD_DIM:(gi + 1) * HEAD_DIM].astype(BF16)
                for hh in range(GROUP):
                    h = gi * GROUP + hh
                    cols = slice(h * HEAD_DIM, (h + 1) * HEAD_DIM)
                    qh = q_ref[pl.ds(r0, WINDOW), cols].astype(BF16)
                    pr, _ = _head_probs(qh, kw, absrel, valid, _alibi_slope(h), sk_ref[0, h])
                    o_ref[pl.ds(r0, WINDOW), cols] = _dot(pr.astype(BF16), vw)
            return carry

        lax.fori_loop(0, N_QBLOCKS, blk, 0)

    vmem = pl.BlockSpec(memory_space=pltpu.VMEM)
    return pl.pallas_call(
        body, name="attn_fwd",
        in_specs=[pl.BlockSpec(memory_space=pltpu.SMEM), vmem, vmem, vmem], out_specs=vmem,
        out_shape=jax.ShapeDtypeStruct((SEQ, ATTN_WIDTH), F32),
        compiler_params=_cparams(),
    )(sinks, q, kp, vp)


def _attn_bwd(q, kp, vp, sinks, do):
    def body(sk_ref, q_ref, kp_ref, vp_ref, do_ref, dq_ref, dkp_ref, dvp_ref, dsk_ref, dsk_acc):
        dkp_ref[...] = jnp.zeros_like(dkp_ref)
        dvp_ref[...] = jnp.zeros_like(dvp_ref)
        dsk_acc[...] = jnp.zeros_like(dsk_acc)

        def blk(n, carry):
            r0 = pl.multiple_of(n * WINDOW, WINDOW)
            absrel, valid = _window_masks(n)
            for gi in range(KV_HEADS):
                gcols = slice(gi * HEAD_DIM, (gi + 1) * HEAD_DIM)
                kw = kp_ref[pl.ds(r0, 3 * WINDOW), gcols].astype(BF16)
                vw = vp_ref[pl.ds(r0, 3 * WINDOW), gcols].astype(BF16)
                dkw = jnp.zeros((3 * WINDOW, HEAD_DIM), F32)
                dvw = jnp.zeros((3 * WINDOW, HEAD_DIM), F32)
                for hh in range(GROUP):
                    h = gi * GROUP + hh
                    cols = slice(h * HEAD_DIM, (h + 1) * HEAD_DIM)
                    qh = q_ref[pl.ds(r0, WINDOW), cols].astype(BF16)
                    doh = do_ref[pl.ds(r0, WINDOW), cols].astype(BF16)
                    pr, psink = _head_probs(qh, kw, absrel, valid, _alibi_slope(h), sk_ref[0, h])
                    dp = _dot_nt(doh, vw)
                    delta = jnp.sum(pr * dp, axis=-1, keepdims=True)
                    ds = (pr * (dp - delta)).astype(BF16)
                    dsk_acc[:, h:h + 1] += -(psink * delta)
                    dq_ref[pl.ds(r0, WINDOW), cols] = _dot(ds, kw) * SCALE
                    dkw = dkw + _dot_tn(ds, qh) * SCALE
                    dvw = dvw + _dot_tn(pr.astype(BF16), doh)
                dkp_ref[pl.ds(r0, 3 * WINDOW), gcols] += dkw
                dvp_ref[pl.ds(r0, 3 * WINDOW), gcols] += dvw
            return carry

        lax.fori_loop(0, N_QBLOCKS, blk, 0)
        dsk_ref[...] = jnp.sum(dsk_acc[...], axis=0, keepdims=True)

    vmem = pl.BlockSpec(memory_space=pltpu.VMEM)
    return pl.pallas_call(
        body, name="attn_bwd",
        in_specs=[pl.BlockSpec(memory_space=pltpu.SMEM), vmem, vmem, vmem, vmem],
        out_specs=[vmem, vmem, vmem, vmem],
        out_shape=[jax.ShapeDtypeStruct((SEQ, ATTN_WIDTH), F32),
                   jax.ShapeDtypeStruct((SEQ + 2 * WINDOW, KV_WIDTH), F32),
                   jax.ShapeDtypeStruct((SEQ + 2 * WINDOW, KV_WIDTH), F32),
                   jax.ShapeDtypeStruct((1, ATTN_HEADS), F32)],
        scratch_shapes=[pltpu.VMEM((WINDOW, ATTN_HEADS), F32)],
        compiler_params=_cparams(),
    )(sinks, q, kp, vp, do)


HALF_LANES = LANES // 2
BLOCK_ROWS = 32


def _embed_block(bt, q):
    z = jnp.zeros((16, HALF_LANES), bt.dtype)
    blk = jnp.concatenate([jnp.concatenate([bt[:16], z], axis=1), jnp.concatenate([z, bt[16:]], axis=1)], axis=0)
    parts = [jnp.zeros((BLOCK_ROWS * q, LANES), bt.dtype)] if q else []
    parts.append(blk)
    if q < 3:
        parts.append(jnp.zeros((BLOCK_ROWS * (3 - q), LANES), bt.dtype))
    return jnp.concatenate(parts, axis=0)


def _extract_block(m, q):
    blk = m[BLOCK_ROWS * q:BLOCK_ROWS * (q + 1)]
    return jnp.concatenate([blk[:16, :HALF_LANES], blk[16:, HALF_LANES:]], axis=0)


def _ssm_prep(lam_re, lam_im, log_dt, bt_re, bt_im, c_re, c_im):
    nb = 2 * N_LANE_BLOCKS

    def body(lr_ref, li_ref, ldt_ref, btr_ref, bti_ref, ctr_ref, cti_ref,
             ar_ref, ai_ref, bbr_ref, bbi_ref, cpr_ref, cpi_ref):
        lr = jnp.minimum(lr_ref[...], LAMBDA_RE_MAX)
        li = li_ref[...]
        dt = jnp.exp(ldt_ref[...])
        mag = jnp.exp(lr * dt)
        ar = mag * jnp.cos(li * dt)
        ai = mag * jnp.sin(li * dt)
        den = lr * lr + li * li
        cr = ((ar - 1.0) * lr + ai * li) / den
        ci = (ai * lr - (ar - 1.0) * li) / den
        ar_ref[...] = ar
        ai_ref[...] = ai
        for i in range(nb):
            q = i % 4
            rows = slice(BLOCK_ROWS * i, BLOCK_ROWS * (i + 1))
            br = _embed_block(btr_ref[rows, :], q)
            bi = _embed_block(bti_ref[rows, :], q)
            cri, cii = cr[i:i + 1, :], ci[i:i + 1, :]
            bbr_ref[i] = (cri * br - cii * bi).astype(BF16)
            bbi_ref[i] = (cri * bi + cii * br).astype(BF16)
            cpr_ref[i] = _embed_block(ctr_ref[rows, :], q).T.astype(BF16)
            cpi_ref[i] = _embed_block(cti_ref[rows, :], q).T.astype(BF16)

    w_shape = jax.ShapeDtypeStruct((nb, LANES, LANES), BF16)
    return pl.pallas_call(
        body, name="ssm_prep",
        out_shape=[jax.ShapeDtypeStruct((nb, LANES), F32), jax.ShapeDtypeStruct((nb, LANES), F32),
                   w_shape, w_shape, w_shape, w_shape],
        compiler_params=_cparams(),
    )(lam_re, lam_im, log_dt, bt_re, bt_im, c_re, c_im)


def _ssm_prep_bwd(lam_re, lam_im, log_dt, bt_re, bt_im, dar, dai, dbbr, dbbi, dcr, dci):
    nb = 2 * N_LANE_BLOCKS

    def body(lr_ref, li_ref, ldt_ref, btr_ref, bti_ref, dar_ref, dai_ref, dbbr_ref, dbbi_ref, dcr_ref, dci_ref,
             glr_ref, gli_ref, gdt_ref, gbr_ref, gbi_ref, gcre_ref, gcim_ref, gcr_s, gci_s):
        lam = lr_ref[...]
        lr = jnp.minimum(lam, LAMBDA_RE_MAX)
        li = li_ref[...]
        dt = jnp.exp(ldt_ref[...])
        mag = jnp.exp(lr * dt)
        cs = jnp.cos(li * dt)
        sn = jnp.sin(li * dt)
        ar = mag * cs
        ai = mag * sn
        den = lr * lr + li * li
        nr = (ar - 1.0) * lr + ai * li
        ni = ai * lr - (ar - 1.0) * li
        cr = nr / den
        ci = ni / den
        for i in range(nb):
            q = i % 4
            rows = slice(BLOCK_ROWS * i, BLOCK_ROWS * (i + 1))
            br = _embed_block(btr_ref[rows, :], q)
            bi = _embed_block(bti_ref[rows, :], q)
            gbbr = dbbr_ref[i]
            gbbi = dbbi_ref[i]
            cri, cii = cr[i:i + 1, :], ci[i:i + 1, :]
            gcr_s[i:i + 1, :] = jnp.sum(gbbr * br + gbbi * bi, axis=0, keepdims=True)
            gci_s[i:i + 1, :] = jnp.sum(gbbi * br - gbbr * bi, axis=0, keepdims=True)
            gbr_ref[rows, :] = _extract_block(cri * gbbr + cii * gbbi, q)
            gbi_ref[rows, :] = _extract_block(cri * gbbi - cii * gbbr, q)
            gcre_ref[rows, :] = _extract_block(dcr_ref[i].T, q)
            gcim_ref[rows, :] = _extract_block(dci_ref[i].T, q)
        g_cr = gcr_s[...]
        g_ci = gci_s[...]
        g_nr = g_cr / den
        g_ni = g_ci / den
        g_den = -(g_cr * nr + g_ci * ni) / (den * den)
        g_ar = dar_ref[...] + g_nr * lr - g_ni * li
        g_ai = dai_ref[...] + g_nr * li + g_ni * lr
        g_lr = g_nr * (ar - 1.0) + g_ni * ai + g_den * 2.0 * lr
        g_li = g_nr * ai - g_ni * (ar - 1.0) + g_den * 2.0 * li
        g_mag = g_ar * cs + g_ai * sn
        g_th = (g_ai * cs - g_ar * sn) * mag
        g_lr = g_lr + g_mag * mag * dt
        g_li = g_li + g_th * dt
        g_dt = g_mag * mag * lr + g_th * li
        glr_ref[...] = jnp.where(lam < LAMBDA_RE_MAX, g_lr, 0.0)
        gli_ref[...] = g_li
        gl = g_dt * dt
        half = LANES // 2
        gdt_ref[:, 0:1] = jnp.sum(gl[:, :half], axis=1, keepdims=True)
        gdt_ref[:, 1:2] = jnp.sum(gl[:, half:], axis=1, keepdims=True)

    rows_shape = jax.ShapeDtypeStruct((nb * BLOCK_ROWS, HALF_LANES), F32)
    return pl.pallas_call(
        body, name="ssm_prep_bwd",
        out_shape=[jax.ShapeDtypeStruct((nb, LANES), F32), jax.ShapeDtypeStruct((nb, LANES), F32),
                   jax.ShapeDtypeStruct((nb, 2), F32), rows_shape, rows_shape, rows_shape, rows_shape],
        scratch_shapes=[pltpu.VMEM((nb, LANES), F32), pltpu.VMEM((nb, LANES), F32)],
        compiler_params=_cparams(),
    )(lam_re, lam_im, log_dt, bt_re, bt_im, dar, dai, dbbr, dbbi, dcr, dci)


def _cmul(ar, ai, br, bi):
    return ar * br - ai * bi, ar * bi + ai * br


def _interleave_rows(src_ref, dst_ref):
    def step(j, carry):
        dst_ref[pl.ds(pl.multiple_of(j * 8, 8), 8), :] = src_ref[pl.ds(j, 8, stride=SCAN_CHUNK), :]
        return carry
    lax.fori_loop(0, SCAN_CHUNK, step, 0, unroll=4)


def _deinterleave_rows(src_ref, dst_ref):
    def step(j, carry):
        dst_ref[pl.ds(j, 8, stride=SCAN_CHUNK), :] = src_ref[pl.ds(pl.multiple_of(j * 8, 8), 8), :]
        return carry
    lax.fori_loop(0, SCAN_CHUNK, step, 0, unroll=4)


def _scan_inplace(re_ref, im_ref, a_re, a_im, reverse):
    nq = len(a_re)
    ch = SCAN_CHUNK
    ab_re = [jnp.broadcast_to(a, (8, LANES)) for a in a_re]
    ab_im = [jnp.broadcast_to(a, (8, LANES)) for a in a_im]

    def rows(j):
        jj = (ch - 1 - j) if reverse else j
        return pl.ds(pl.multiple_of(jj * 8, 8), 8)

    def sweep(init, store):
        def step(j, st):
            out = []
            r = rows(j)
            for qi in range(nq):
                xr, xi = st[2 * qi], st[2 * qi + 1]
                pr, pi = _cmul(ab_re[qi], ab_im[qi], xr, xi)
                xr = pr + re_ref[qi, r, :]
                xi = pi + im_ref[qi, r, :]
                if store:
                    re_ref[qi, r, :] = xr
                    im_ref[qi, r, :] = xi
                out += [xr, xi]
            return tuple(out)
        return lax.fori_loop(0, ch, step, tuple(init), unroll=2)

    zeros = [jnp.zeros((8, LANES), F32)] * (2 * nq)
    finals = sweep(zeros, store=False)

    row_id = lax.broadcasted_iota(jnp.int32, (8, LANES), 0)
    carries = []
    for qi in range(nq):
        pr, pi = ab_re[qi], ab_im[qi]
        for _ in range(8):
            pr, pi = _cmul(pr, pi, pr, pi)
        fr, fi = finals[2 * qi], finals[2 * qi + 1]
        sr = jnp.zeros((8, LANES), F32)
        si = jnp.zeros((8, LANES), F32)
        for _ in range(7):
            tr, ti = _cmul(pr, pi, sr, si)
            tr, ti = tr + fr, ti + fi
            if reverse:
                sr = jnp.where(row_id == 7, 0.0, pltpu.roll(tr, 7, axis=0))
                si = jnp.where(row_id == 7, 0.0, pltpu.roll(ti, 7, axis=0))
            else:
                sr = jnp.where(row_id == 0, 0.0, pltpu.roll(tr, 1, axis=0))
                si = jnp.where(row_id == 0, 0.0, pltpu.roll(ti, 1, axis=0))
        carries += [sr, si]
    sweep(carries, store=True)


SSM_Q = 4


def _ssm_fwd(u, are, aim, bbr, bbi, cre, cim, dskip, after=None):
    nq = SSM_Q
    deps = [] if after is None else [after]

    def body(u_ref, ar_ref, ai_ref, bbr_ref, bbi_ref, cr_ref, ci_ref, d_ref, *rest):
        y_ref, xr_ref, xi_ref, sre, sim, up, yp = rest[len(deps):]
        _interleave_rows(u_ref, up)
        uf = up[...]
        ub = uf.astype(BF16)
        yp[...] = d_ref[...] * uf
        for d in range(2):
            for qi in range(nq):
                sre[qi] = _dot(ub, bbr_ref[d, qi])
                sim[qi] = _dot(ub, bbi_ref[d, qi])
            _scan_inplace(sre, sim, [ar_ref[d, qi] for qi in range(nq)], [ai_ref[d, qi] for qi in range(nq)],
                          reverse=(d == 1))
            for qi in range(nq):
                xrb = sre[qi].astype(BF16)
                xib = sim[qi].astype(BF16)
                xr_ref[d, qi] = xrb
                xi_ref[d, qi] = xib
                yp[...] += _dot(xrb, cr_ref[d, qi]) - _dot(xib, ci_ref[d, qi])
        _deinterleave_rows(yp, y_ref)

    blk4 = lambda k: (0, k, 0, 0)
    return pl.pallas_call(
        body, name="ssm_fwd", grid=(SSM_WIDTH // LANES,),
        in_specs=[pl.BlockSpec((SEQ, LANES), lambda k: (0, k)),
                  pl.BlockSpec((2, nq, 1, LANES), blk4), pl.BlockSpec((2, nq, 1, LANES), blk4),
                  pl.BlockSpec((2, nq, LANES, LANES), blk4), pl.BlockSpec((2, nq, LANES, LANES), blk4),
                  pl.BlockSpec((2, nq, LANES, LANES), blk4), pl.BlockSpec((2, nq, LANES, LANES), blk4),
                  pl.BlockSpec((1, LANES), lambda k: (0, k))] + [pl.BlockSpec(memory_space=pl.ANY)] * len(deps),
        out_specs=[pl.BlockSpec((SEQ, LANES), lambda k: (0, k)),
                   pl.BlockSpec((2, nq, SEQ, LANES), blk4), pl.BlockSpec((2, nq, SEQ, LANES), blk4)],
        out_shape=[jax.ShapeDtypeStruct((SEQ, SSM_WIDTH), F32),
                   jax.ShapeDtypeStruct((2, N_LANE_BLOCKS, SEQ, LANES), BF16),
                   jax.ShapeDtypeStruct((2, N_LANE_BLOCKS, SEQ, LANES), BF16)],
        scratch_shapes=[pltpu.VMEM((nq, SEQ, LANES), F32), pltpu.VMEM((nq, SEQ, LANES), F32),
                        pltpu.VMEM((SEQ, LANES), F32), pltpu.VMEM((SEQ, LANES), F32)],
        compiler_params=_cparams(("parallel",)),
    )(u, are, aim, bbr, bbi, cre, cim, dskip, *deps)


def _ssm_bwd(dy, u, xr, xi, are, aim, bbr, bbi, cre, cim, dskip, after=None):
    nq = SSM_Q
    body_rows = SEQ - 8
    deps = [] if after is None else [after]

    def body(dy_ref, u_ref, xr_ref, xi_ref, ar_ref, ai_ref, bbr_ref, bbi_ref, cr_ref, ci_ref, d_ref, *rest):
        (du_ref, dd_ref, dcr_ref, dci_ref, dbr_ref, dbi_ref, dar_ref, dai_ref,
         sre, sim, up, dyp, dup) = rest[len(deps):]
        _interleave_rows(u_ref, up)
        _interleave_rows(dy_ref, dyp)
        dyf = dyp[...]
        uf = up[...]
        dyb = dyf.astype(BF16)
        ub = uf.astype(BF16)
        dd_ref[...] = jnp.sum(dyf * uf, axis=0, keepdims=True)
        dup[...] = d_ref[...] * dyf
        row8 = lax.broadcasted_iota(jnp.int32, (8, LANES), 0)
        for d in range(2):
            for qi in range(nq):
                sre[qi] = _dot_nt(dyb, cr_ref[d, qi])
                sim[qi] = -_dot_nt(dyb, ci_ref[d, qi])
                dcr_ref[d, qi] = _dot_tn(xr_ref[d, qi], dyb)
                dci_ref[d, qi] = -_dot_tn(xi_ref[d, qi], dyb)
            _scan_inplace(sre, sim, [ar_ref[d, qi] for qi in range(nq)], [-ai_ref[d, qi] for qi in range(nq)],
                          reverse=(d == 0))
            for qi in range(nq):
                gr = sre[qi]
                gi = sim[qi]
                xrf = xr_ref[d, qi].astype(F32)
                xif = xi_ref[d, qi].astype(F32)
                if d == 0:
                    g_main_r, g_main_i = gr[8:], gi[8:]
                    x_main_r, x_main_i = xrf[:body_rows], xif[:body_rows]
                    g_edge_r, g_edge_i = gr[:8], gi[:8]
                    x_edge_r = jnp.where(row8 == 0, 0.0, pltpu.roll(xrf[body_rows:], 1, axis=0))
                    x_edge_i = jnp.where(row8 == 0, 0.0, pltpu.roll(xif[body_rows:], 1, axis=0))
                else:
                    g_main_r, g_main_i = gr[:body_rows], gi[:body_rows]
                    x_main_r, x_main_i = xrf[8:], xif[8:]
                    g_edge_r, g_edge_i = gr[body_rows:], gi[body_rows:]
                    x_edge_r = jnp.where(row8 == 7, 0.0, pltpu.roll(xrf[:8], 7, axis=0))
                    x_edge_i = jnp.where(row8 == 7, 0.0, pltpu.roll(xif[:8], 7, axis=0))
                dar_ref[d, qi] = (jnp.sum(g_main_r * x_main_r + g_main_i * x_main_i, axis=0, keepdims=True)
                                  + jnp.sum(g_edge_r * x_edge_r + g_edge_i * x_edge_i, axis=0, keepdims=True))
                dai_ref[d, qi] = (jnp.sum(g_main_i * x_main_r - g_main_r * x_main_i, axis=0, keepdims=True)
                                  + jnp.sum(g_edge_i * x_edge_r - g_edge_r * x_edge_i, axis=0, keepdims=True))
                grb = gr.astype(BF16)
                gib = gi.astype(BF16)
                dup[...] += _dot_nt(grb, bbr_ref[d, qi]) + _dot_nt(gib, bbi_ref[d, qi])
                dbr_ref[d, qi] = _dot_tn(ub, grb)
                dbi_ref[d, qi] = _dot_tn(ub, gib)
        _deinterleave_rows(dup, du_ref)

    blk4 = lambda k: (0, k, 0, 0)
    col = lambda k: (0, k)
    w_spec = pl.BlockSpec((2, nq, LANES, LANES), blk4)
    a_spec = pl.BlockSpec((2, nq, 1, LANES), blk4)
    x_spec = pl.BlockSpec((2, nq, SEQ, LANES), blk4)
    w_shape = jax.ShapeDtypeStruct((2, N_LANE_BLOCKS, LANES, LANES), F32)
    a_shape = jax.ShapeDtypeStruct((2, N_LANE_BLOCKS, 1, LANES), F32)
    return pl.pallas_call(
        body, name="ssm_bwd", grid=(SSM_WIDTH // LANES,),
        in_specs=[pl.BlockSpec((SEQ, LANES), col), pl.BlockSpec((SEQ, LANES), col), x_spec, x_spec,
                  a_spec, a_spec, w_spec, w_spec, w_spec, w_spec, pl.BlockSpec((1, LANES), col)]
        + [pl.BlockSpec(memory_space=pl.ANY)] * len(deps),
        out_specs=[pl.BlockSpec((SEQ, LANES), col), pl.BlockSpec((1, LANES), col),
                   w_spec, w_spec, w_spec, w_spec, a_spec, a_spec],
        out_shape=[jax.ShapeDtypeStruct((SEQ, SSM_WIDTH), F32), jax.ShapeDtypeStruct((1, SSM_WIDTH), F32),
                   w_shape, w_shape, w_shape, w_shape, a_shape, a_shape],
        scratch_shapes=[pltpu.VMEM((nq, SEQ, LANES), F32), pltpu.VMEM((nq, SEQ, LANES), F32),
                        pltpu.VMEM((SEQ, LANES), F32), pltpu.VMEM((SEQ, LANES), F32), pltpu.VMEM((SEQ, LANES), F32)],
        compiler_params=_cparams(("parallel",)),
    )(dy, u, xr, xi, are, aim, bbr, bbi, cre, cim, dskip, *deps)


GELU_C = 0.7978845608028654
GELU_K = 0.044715


def _gelu(y):
    return 0.5 * y * (1.0 + jnp.tanh(GELU_C * (y + GELU_K * y * y * y)))


def _gelu_grad(y):
    t = jnp.tanh(GELU_C * (y + GELU_K * y * y * y))
    return 0.5 * (1.0 + t) + 0.5 * y * (1.0 - t * t) * GELU_C * (1.0 + 3.0 * GELU_K * y * y)


def _mixout_fwd(o, y, glu_w, glu_b, gan, gsn, wout, x1):
    tm = MIX_TM

    def body(o_ref, y_ref, gw_ref, gb_ref, gan_ref, gsn_ref, w_ref, x1_ref, x2_ref, mx_ref):
        yg = _gelu(y_ref[...])
        z = _dot(yg.astype(BF16), gw_ref[...]) + gb_ref[...]
        so = yg * _sigmoid(z)
        na = _rms_fwd(o_ref[...], gan_ref[...])
        ns = _rms_fwd(so, gsn_ref[...])
        mixed = jnp.concatenate([na, ns], axis=-1).astype(BF16)
        mx_ref[...] = mixed
        x2_ref[...] = x1_ref[...] + _dot(mixed, w_ref[...])

    row = lambda i: (i, 0)
    const = lambda i: (0, 0)
    return pl.pallas_call(
        body, name="mixout_fwd", grid=(SEQ // tm,),
        in_specs=[pl.BlockSpec((tm, ATTN_WIDTH), row), pl.BlockSpec((tm, SSM_WIDTH), row),
                  pl.BlockSpec((SSM_WIDTH, SSM_WIDTH), const), pl.BlockSpec((1, SSM_WIDTH), const),
                  pl.BlockSpec((1, ATTN_WIDTH), const), pl.BlockSpec((1, SSM_WIDTH), const),
                  pl.BlockSpec((D_MODEL, D_MODEL), const), pl.BlockSpec((tm, D_MODEL), row)],
        out_specs=[pl.BlockSpec((tm, D_MODEL), row), pl.BlockSpec((tm, D_MODEL), row)],
        out_shape=[jax.ShapeDtypeStruct((SEQ, D_MODEL), F32), jax.ShapeDtypeStruct((SEQ, D_MODEL), BF16)],
        compiler_params=_cparams(("parallel",)),
    )(o, y, glu_w, glu_b, gan, gsn, wout, x1)


def _mixout_bwd(dx2, o, y, glu_w, glu_b, gan, gsn, wout):
    tm = MIX_TM

    def body(dx2_ref, o_ref, y_ref, gw_ref, gb_ref, gan_ref, gsn_ref, w_ref,
             do_ref, dy_ref, dz_ref, yg_ref, dxb_ref, dgan_ref, dgsn_ref, dgb_ref):
        i = pl.program_id(0)
        dxb = dx2_ref[...].astype(BF16)
        dxb_ref[...] = dxb
        dmixed = _dot_nt(dxb, w_ref[...])
        do, dgan = _rms_bwd(dmixed[:, :ATTN_WIDTH], o_ref[...], gan_ref[...])
        do_ref[...] = do
        yv = y_ref[...]
        yg = _gelu(yv)
        ygb = yg.astype(BF16)
        yg_ref[...] = ygb
        sg = _sigmoid(_dot(ygb, gw_ref[...]) + gb_ref[...])
        dso, dgsn = _rms_bwd(dmixed[:, ATTN_WIDTH:], yg * sg, gsn_ref[...])
        dz = dso * yg * sg * (1.0 - sg)
        dzb = dz.astype(BF16)
        dz_ref[...] = dzb
        dyg = dso * sg + _dot_nt(dzb, gw_ref[...])
        dy_ref[...] = dyg * _gelu_grad(yv)
        dgb = jnp.sum(dz, axis=0, keepdims=True)

        @pl.when(i == 0)
        def _():
            dgan_ref[...] = dgan
            dgsn_ref[...] = dgsn
            dgb_ref[...] = dgb

        @pl.when(i != 0)
        def _():
            dgan_ref[...] += dgan
            dgsn_ref[...] += dgsn
            dgb_ref[...] += dgb

    row = lambda i: (i, 0)
    const = lambda i: (0, 0)
    return pl.pallas_call(
        body, name="mixout_bwd", grid=(SEQ // tm,),
        in_specs=[pl.BlockSpec((tm, D_MODEL), row), pl.BlockSpec((tm, ATTN_WIDTH), row),
                  pl.BlockSpec((tm, SSM_WIDTH), row),
                  pl.BlockSpec((SSM_WIDTH, SSM_WIDTH), const), pl.BlockSpec((1, SSM_WIDTH), const),
                  pl.BlockSpec((1, ATTN_WIDTH), const), pl.BlockSpec((1, SSM_WIDTH), const),
                  pl.BlockSpec((D_MODEL, D_MODEL), const)],
        out_specs=[pl.BlockSpec((tm, ATTN_WIDTH), row), pl.BlockSpec((tm, SSM_WIDTH), row),
                   pl.BlockSpec((tm, SSM_WIDTH), row), pl.BlockSpec((tm, SSM_WIDTH), row),
                   pl.BlockSpec((tm, D_MODEL), row),
                   pl.BlockSpec((1, ATTN_WIDTH), const), pl.BlockSpec((1, SSM_WIDTH), const),
                   pl.BlockSpec((1, SSM_WIDTH), const)],
        out_shape=[jax.ShapeDtypeStruct((SEQ, ATTN_WIDTH), F32), jax.ShapeDtypeStruct((SEQ, SSM_WIDTH), F32),
                   jax.ShapeDtypeStruct((SEQ, SSM_WIDTH), BF16), jax.ShapeDtypeStruct((SEQ, SSM_WIDTH), BF16),
                   jax.ShapeDtypeStruct((SEQ, D_MODEL), BF16),
                   jax.ShapeDtypeStruct((1, ATTN_WIDTH), F32), jax.ShapeDtypeStruct((1, SSM_WIDTH), F32),
                   jax.ShapeDtypeStruct((1, SSM_WIDTH), F32)],
        compiler_params=_cparams(("arbitrary",)),
    )(dx2, o, y, glu_w, glu_b, gan, gsn, wout)


def _loss_head(x, g, target):
    tm = MIX_TM

    def body(x_ref, g_ref, t_ref, loss_ref, dx_ref, dg_ref):
        i = pl.program_id(0)
        xv = x_ref[...]
        gv = g_ref[...]
        err = _rms_fwd(xv, gv) - t_ref[...]
        part = jnp.broadcast_to(0.5 * jnp.sum(err * err) / D_MODEL, (1, LANES))
        dx, dg = _rms_bwd(err * (1.0 / D_MODEL), xv, gv)
        dx_ref[...] = dx

        @pl.when(i == 0)
        def _():
            loss_ref[...] = part
            dg_ref[...] = dg

        @pl.when(i != 0)
        def _():
            loss_ref[...] += part
            dg_ref[...] += dg

    row = lambda i: (i, 0)
    const = lambda i: (0, 0)
    return pl.pallas_call(
        body, name="loss_head", grid=(SEQ // tm,),
        in_specs=[pl.BlockSpec((tm, D_MODEL), row), pl.BlockSpec((1, D_MODEL), const),
                  pl.BlockSpec((tm, D_MODEL), row)],
        out_specs=[pl.BlockSpec((1, LANES), const), pl.BlockSpec((tm, D_MODEL), row),
                   pl.BlockSpec((1, D_MODEL), const)],
        out_shape=[jax.ShapeDtypeStruct((1, LANES), F32), jax.ShapeDtypeStruct((SEQ, D_MODEL), F32),
                   jax.ShapeDtypeStruct((1, D_MODEL), F32)],
        compiler_params=_cparams(("arbitrary",)),
    )(x, g, target)


def _local_step(x, target, w, p, late_weights, early_grads, after=None, midway=None):
    x1, h1, a1, b1 = _ffn_fwd(x, p["norm_ffn1"], w["wgt1"], w["wut1"], w["wd1"], "ffn1_fwd", after=after)
    h2, q, k, v, u = _mixin_fwd(x1, p["norm_mix"], w["wint"])
    kp = jnp.pad(k, ((WINDOW, WINDOW), (0, 0)))
    vp = jnp.pad(v, ((WINDOW, WINDOW), (0, 0)))
    o = _attn_fwd(q, kp, vp, p["attn_sinks"])

    lam_re = p["ssm_lambda_re"].reshape(2 * N_LANE_BLOCKS, LANES)
    lam_im = p["ssm_lambda_im"].reshape(2 * N_LANE_BLOCKS, LANES)
    log_dt = jnp.repeat(p["ssm_log_dt"].reshape(2, 32), 64, axis=-1).reshape(2 * N_LANE_BLOCKS, LANES)
    a_re, a_im, bbr, bbi, cre, cim = _ssm_prep(lam_re, lam_im, log_dt, p["ssm_b_re"], p["ssm_b_im"],
                                               p["ssm_c_re"], p["ssm_c_im"])
    shape_a = (2, N_LANE_BLOCKS, 1, LANES)
    shape_w = (2, N_LANE_BLOCKS, LANES, LANES)
    a_re4, a_im4 = a_re.reshape(shape_a), a_im.reshape(shape_a)
    bbr4, bbi4 = bbr.reshape(shape_w), bbi.reshape(shape_w)
    cre, cim = cre.reshape(shape_w), cim.reshape(shape_w)
    dskip = p["ssm_d"].T.reshape(1, SSM_WIDTH)
    y, xr, xi = _ssm_fwd(u, a_re4, a_im4, bbr4, bbi4, cre, cim, dskip,
                         after=None if midway is None else midway(o))

    w2 = late_weights(y)
    x2, mixed = _mixout_fwd(o, y, w2["glu"], p["ssm_glu_b"], p["attn_out_norm"], p["ssm_out_norm"], w2["wout"], x1)
    x3, h3, a3, b3 = _ffn_fwd(x2, p["norm_ffn2"], w2["wgt2"], w2["wut2"], w2["wd2"], "ffn2_fwd")

    loss, dx3, d_final = _loss_head(x3, p["final_norm"], target)
    dx2, da3, db3, s3, df3, d_n2 = _ffn_bwd_act(dx3, x2, p["norm_ffn2"], a3, b3, w2["wgt2"], w2["wut2"], w2["wd2"],
                                                "ffn2_bwd_act")
    g_wgt2, g_wut2, g_wd2 = _mm_tn([(da3, h3), (db3, h3), (s3, df3)], "ffn2_bwd_w")

    do, dy, dz, ygb, dx2b, d_gan, d_gsn, d_glub = _mixout_bwd(
        dx2, o, y, w2["glu"], p["ssm_glu_b"], p["attn_out_norm"], p["ssm_out_norm"], w2["wout"])
    (g_wout,) = _mm_tn([(mixed, dx2b)], "wout_bwd_w")
    (g_glu,) = _mm_tn([(ygb, dz)], "glu_bwd_w")
    sent = early_grads(dict(glu=g_glu, wout=g_wout, wgt2=g_wgt2, wut2=g_wut2, wd2=g_wd2))

    du, d_dskip, dcre, dcim, dbbr, dbbi, dar, dai = _ssm_bwd(dy, u, xr, xi, a_re4, a_im4, bbr4, bbi4, cre, cim, dskip,
                                                             after=sent)
    nb = 2 * N_LANE_BLOCKS
    blocks3 = (nb, LANES, LANES)
    g_lre, g_lim, g_ldt, g_btr, g_bti, g_cre, g_cim = _ssm_prep_bwd(
        lam_re, lam_im, log_dt, p["ssm_b_re"], p["ssm_b_im"], dar.reshape(nb, LANES), dai.reshape(nb, LANES),
        dbbr.reshape(blocks3), dbbi.reshape(blocks3), dcre.reshape(blocks3), dcim.reshape(blocks3))

    dq, dkp, dvp, d_sinks = _attn_bwd(q, kp, vp, p["attn_sinks"], do)
    dk = dkp[WINDOW:WINDOW + SEQ]
    dv = dvp[WINDOW:WINDOW + SEQ]
    dx1, dproj, d_nmix = _mixin_bwd(dq, dk, dv, du, w["wint"], x1, p["norm_mix"], dx2)
    (g_wint,) = _mm_tn([(dproj, h2)], "win_bwd_w")

    dx0, da1, db1, s1, df1, d_n1 = _ffn_bwd_act(dx1, x, p["norm_ffn1"], a1, b1, w["wgt1"], w["wut1"], w["wd1"],
                                                "ffn1_bwd_act")
    g_wgt1, g_wut1, g_wd1 = _mm_tn([(da1, h1), (db1, h1), (s1, df1)], "ffn1_bwd_w")

    big = dict(wgt1=g_wgt1, wut1=g_wut1, wd1=g_wd1, wint=g_wint)
    small = dict(
        norm_ffn1=d_n1, norm_mix=d_nmix, attn_sinks=d_sinks,
        ssm_lambda_re=g_lre.reshape(64, 64), ssm_lambda_im=g_lim.reshape(64, 64),
        ssm_log_dt=g_ldt.reshape(2, 32), ssm_b_re=g_btr, ssm_b_im=g_bti, ssm_c_re=g_cre, ssm_c_im=g_cim,
        ssm_d=d_dskip.reshape(32, 16).T, ssm_glu_b=d_glub, attn_out_norm=d_gan, ssm_out_norm=d_gsn,
        norm_ffn2=d_n2, final_norm=d_final, loss=loss)
    return loss, dx0, big, small


BIG = dict(
    wgt1=("ffn1_w_gate", 352, 1024, True), wut1=("ffn1_w_up", 352, 1024, True), wd1=("ffn1_w_down", 352, 1024, False),
    wint=("w_in", 160, 1024, True), glu=("ssm_glu_w", 64, 512, False), wout=("w_out", 128, 1024, False),
    wgt2=("ffn2_w_gate", 352, 1024, True), wut2=("ffn2_w_up", 352, 1024, True), wd2=("ffn2_w_down", 352, 1024, False))

SMALL = dict(
    norm_ffn1=(1, 1024), norm_mix=(1, 1024), attn_sinks=(1, 8), ssm_lambda_re=(64, 64), ssm_lambda_im=(64, 64),
    ssm_log_dt=(2, 32), ssm_b_re=(1024, 64), ssm_b_im=(1024, 64), ssm_c_re=(1024, 64), ssm_c_im=(1024, 64),
    ssm_d=(16, 32), ssm_glu_b=(1, 512), attn_out_norm=(1, 512), ssm_out_norm=(1, 512), norm_ffn2=(1, 1024),
    final_norm=(1, 1024), loss=(1, 128))
SMALL_TRANSPOSED = ("ssm_b_re", "ssm_b_im", "ssm_d")
SMALL_PARAMS = tuple(n for n in SMALL if n != "loss")

SMALL_PAIRS = (("ssm_lambda_re", "ssm_lambda_im"), ("ssm_c_re", "ssm_c_im"), ("ssm_b_re", "ssm_b_im"))
SMALL_VECS = ("norm_ffn1", "norm_mix", "norm_ffn2", "final_norm", "ssm_glu_b", "attn_out_norm", "ssm_out_norm")
SMALL_TILES = ("ssm_log_dt", "attn_sinks", "ssm_d", "loss")


def _small_offsets():
    off, table = 0, {}
    for re, im in SMALL_PAIRS:
        table[re] = table[im] = off
        off += SMALL[re][0]
    for n in SMALL_VECS:
        table[n] = off
        off += SMALL[n][1] // LANES
    for n in SMALL_TILES:
        off = -(-off // 8) * 8
        table[n] = off
        off += SMALL[n][0]
    return table, off


SMALL_OFFSET, SMALL_USED_ROWS = _small_offsets()
SMALL_ROWS = -(-SMALL_USED_ROWS // (8 * N_DEV)) * 8 * N_DEV


def _cast_shards(shards):
    names = list(BIG)

    def body(*refs):
        ins, outs = refs[:len(names)], refs[len(names):]
        for idx in range(len(names)):
            outs[idx][...] = ins[idx][...].astype(BF16)

    return pl.pallas_call(
        body, name="cast_shards",
        out_shape=[jax.ShapeDtypeStruct((BIG[n][1], BIG[n][2]), BF16) for n in names],
        compiler_params=_cparams(),
    )(*[shards[n] for n in names])


def _peer(x, y, c, r):
    px = 1 - x if r & 4 else x
    py = 1 - y if r & 2 else y
    pc = 1 - c if r & 1 else c
    return px, py, pc


FIRST_GROUP = ("wgt1", "wut1", "wd1", "wint")
LATE_GROUP = ("glu", "wout", "wgt2", "wut2", "wd2")
N_PEERS = N_DEV - 1
ANY_SPEC = pl.BlockSpec(memory_space=pl.ANY)
HBM_SPEC = pl.BlockSpec(memory_space=pltpu.HBM)
SEM_SPEC = pl.BlockSpec(memory_space=pltpu.SEMAPHORE)
DATAFLOW_EFFECT = pltpu.SideEffectType.DATAFLOW_SIDE_EFFECTING


def _mesh_pos():
    x, y, c = lax.axis_index("x"), lax.axis_index("y"), lax.axis_index("c")
    return x, y, c, 4 * x + 2 * y + c


def _gather_first(first, late):
    nf, nl = len(first), len(late)

    def body(*refs):
        f_in, l_in = refs[:nf], refs[nf:nf + nl]
        f_out, l_out = refs[nf + nl:2 * nf + nl], refs[2 * nf + nl:2 * (nf + nl)]
        send_sems, recv_sems, local_sems = refs[2 * (nf + nl):]
        x, y, c, me = _mesh_pos()
        sibling = (x, y, 1 - c)
        chips = [(x, 1 - y), (1 - x, y), (1 - x, 1 - y)]

        def idx(px, py, pc):
            return 4 * px + 2 * py + pc

        def copy(k, s, block, to, src=None):
            slot = f_out[k].at[block]
            return pltpu.make_async_remote_copy(
                src_ref=slot if src is None else src, dst_ref=slot, send_sem=send_sems.at[k, s],
                recv_sem=recv_sems.at[k, s], device_id=to, device_id_type=MESH_ID)

        local = []
        for k in range(nf + nl):
            src, dst = (f_in[k], f_out[k]) if k < nf else (l_in[k - nf], l_out[k - nf])
            mine = pltpu.make_async_copy(src, dst.at[me], local_sems.at[k])
            mine.start()
            local.append(mine)
        sends = []
        for j, chip in enumerate(chips):
            for k in range(nf):
                sends.append(copy(k, 1 + j, me, (*chip, c), src=f_in[k]))
                sends[-1].start()
        for k in range(nf):
            sends.append(copy(k, 0, me, sibling, src=f_in[k]))
            sends[-1].start()
        for j, chip in enumerate(chips):
            for k in range(nf):
                copy(k, 1 + j, idx(*chip, c), (*chip, c)).wait_recv()
                sends.append(copy(k, 4 + j, idx(*chip, c), sibling))
                sends[-1].start()
        for k in range(nf):
            copy(k, 0, idx(*sibling), sibling).wait_recv()
        for j, chip in enumerate(chips):
            for k in range(nf):
                copy(k, 4 + j, idx(*chip, 1 - c), sibling).wait_recv()
        for cp in sends:
            cp.wait_send()
        for cp in local:
            cp.wait()

    return pl.pallas_call(
        body, name="gather_first",
        in_specs=[ANY_SPEC] * (nf + nl), out_specs=[ANY_SPEC] * (nf + nl),
        out_shape=[jax.ShapeDtypeStruct((N_DEV,) + s.shape, s.dtype) for s in list(first) + list(late)],
        scratch_shapes=[pltpu.SemaphoreType.DMA((nf, N_PEERS)), pltpu.SemaphoreType.DMA((nf, N_PEERS)),
                        pltpu.SemaphoreType.DMA((nf + nl,))],
        compiler_params=pltpu.CompilerParams(has_side_effects=True),
    )(*first, *late)


def _split_copy(src_refs, land_refs, send_sems, recv_sems, k, r, pos, scatter, receiving):
    x, y, c, me = pos
    px, py, pc = _peer(x, y, c, r)
    peer_idx = 4 * px + 2 * py + pc
    if scatter:
        src, dst = src_refs[k].at[peer_idx], land_refs[k].at[r - 1]
    else:
        src, dst = src_refs[k], land_refs[k].at[peer_idx if receiving else me]
    return pltpu.make_async_remote_copy(
        src_ref=src, dst_ref=dst, send_sem=send_sems.at[k * N_PEERS + r - 1],
        recv_sem=recv_sems.at[k * N_PEERS + r - 1], device_id=(px, py, pc), device_id_type=MESH_ID)


def _split_start(name, srcs, lands, scatter):
    n = len(srcs)

    def body(*refs):
        src_refs, land_refs = refs[:n], refs[n:2 * n]
        send_sems, recv_sems = refs[2 * n], refs[2 * n + 1]
        token = refs[-1]
        pos = _mesh_pos()
        for k in range(n):
            for r in range(1, N_DEV):
                _split_copy(src_refs, land_refs, send_sems, recv_sems, k, r, pos, scatter, False).start()
        token[...] = jnp.zeros_like(token)

    thru = [pltpu.HBM(a.shape, a.dtype) for a in list(srcs) + list(lands)]
    outs = pl.pallas_call(
        body, name=name,
        in_specs=[HBM_SPEC] * (2 * n),
        out_specs=[SEM_SPEC, SEM_SPEC] + [HBM_SPEC] * (2 * n) + [pl.BlockSpec(memory_space=pltpu.VMEM)],
        out_shape=[pltpu.SemaphoreType.DMA((n * N_PEERS,)), pltpu.SemaphoreType.DMA((n * N_PEERS,))] + thru
        + [jax.ShapeDtypeStruct((8, LANES), F32)],
        input_output_aliases={i: 2 + i for i in range(2 * n)},
        compiler_params=pltpu.CompilerParams(has_side_effects=DATAFLOW_EFFECT),
    )(*[pltpu.with_memory_space_constraint(a, pltpu.HBM) for a in list(srcs) + list(lands)])
    return outs[0], outs[1], outs[2:2 + n], outs[2 + n:2 + 2 * n], outs[-1]


def _split_wait(name, send_sems, recv_sems, srcs, lands, scatter, after):
    n = len(srcs)

    def body(*refs):
        src_refs, land_refs = refs[:n], refs[n:2 * n]
        send, recv = refs[2 * n], refs[2 * n + 1]
        pos = _mesh_pos()
        for k in range(n):
            for r in range(1, N_DEV):
                cp = _split_copy(src_refs, land_refs, send, recv, k, r, pos, scatter, True)
                cp.wait_send()
                cp.wait_recv()

    thru = [pltpu.HBM(a.shape, a.dtype) for a in list(srcs) + list(lands)]
    outs = pl.pallas_call(
        body, name=name,
        in_specs=[HBM_SPEC] * (2 * n) + [SEM_SPEC, SEM_SPEC, ANY_SPEC],
        out_specs=[HBM_SPEC] * (2 * n), out_shape=thru,
        input_output_aliases={i: i for i in range(2 * n)},
        compiler_params=pltpu.CompilerParams(has_side_effects=DATAFLOW_EFFECT),
    )(*srcs, *lands, send_sems, recv_sems, after)
    return outs[:n], outs[n:]


def _late_copy(passing, src_refs, land_refs, send_sems, recv_sems, k, s, pos, receiving):
    x, y, c, me = pos
    chips = [(x, 1 - y), (1 - x, y), (1 - x, 1 - y)]
    sibling = (x, y, 1 - c)

    def idx(dev):
        return 4 * dev[0] + 2 * dev[1] + dev[2]

    if passing:
        to = sibling
        block = idx((*chips[s], 1 - c)) if receiving else idx((*chips[s], c))
        src = dst = land_refs[k].at[block]
        sem = k * 3 + s
    else:
        to = sibling if s == 0 else (*chips[s - 1], c)
        src, dst = src_refs[k], land_refs[k].at[idx(to) if receiving else me]
        sem = k * 4 + s
    return pltpu.make_async_remote_copy(src_ref=src, dst_ref=dst, send_sem=send_sems.at[sem],
                                        recv_sem=recv_sems.at[sem], device_id=to, device_id_type=MESH_ID)


def _late_gather_call(name, stage, srcs, lands, sems, after=None):
    n = len(srcs)
    n_sem_in = len(sems)
    has_after = after is not None

    def body(*refs):
        src_refs, land_refs = refs[:n], refs[n:2 * n]
        sem_in = refs[2 * n:2 * n + n_sem_in]
        outs = refs[2 * n + n_sem_in + (1 if has_after else 0):]
        pos = _mesh_pos()
        if stage == 0:
            own_send, own_recv = outs[0], outs[1]
            for s in (1, 2, 3, 0):
                for k in range(n):
                    _late_copy(False, src_refs, land_refs, own_send, own_recv, k, s, pos, False).start()
            outs[-1][...] = jnp.zeros_like(outs[-1])
        elif stage == 1:
            own_recv = sem_in[1]
            pass_send, pass_recv = outs[0], outs[1]
            for s in range(3):
                for k in range(n):
                    _late_copy(False, src_refs, land_refs, sem_in[0], own_recv, k, s + 1, pos, True).wait_recv()
                    _late_copy(True, src_refs, land_refs, pass_send, pass_recv, k, s, pos, False).start()
            outs[-1][...] = jnp.zeros_like(outs[-1])
        else:
            own_send, own_recv, pass_send, pass_recv = sem_in
            for k in range(n):
                _late_copy(False, src_refs, land_refs, own_send, own_recv, k, 0, pos, True).wait_recv()
                for s in range(4):
                    _late_copy(False, src_refs, land_refs, own_send, own_recv, k, s, pos, False).wait_send()
                for s in range(3):
                    cp = _late_copy(True, src_refs, land_refs, pass_send, pass_recv, k, s, pos, True)
                    cp.wait_recv()
                    cp.wait_send()

    thru = [pltpu.HBM(a.shape, a.dtype) for a in list(srcs) + list(lands)]
    new_sems = [[pltpu.SemaphoreType.DMA((n * 4,))] * 2, [pltpu.SemaphoreType.DMA((n * 3,))] * 2, []][stage]
    extra = [] if stage == 2 else [jax.ShapeDtypeStruct((8, LANES), F32)]
    outs = pl.pallas_call(
        body, name=name,
        in_specs=[HBM_SPEC] * (2 * n) + [SEM_SPEC] * n_sem_in + [ANY_SPEC] * has_after,
        out_specs=[SEM_SPEC] * len(new_sems) + [HBM_SPEC] * (2 * n) + [pl.BlockSpec(memory_space=pltpu.VMEM)] * len(extra),
        out_shape=new_sems + thru + extra,
        input_output_aliases={i: len(new_sems) + i for i in range(2 * n)},
        compiler_params=pltpu.CompilerParams(has_side_effects=DATAFLOW_EFFECT),
    )(*[pltpu.with_memory_space_constraint(a, pltpu.HBM) for a in list(srcs) + list(lands)], *sems,
      *([after] if has_after else []))
    ns = len(new_sems)
    return list(outs[:ns]), outs[ns:ns + n], outs[ns + n:ns + 2 * n], (outs[-1] if extra else None)


def _exchange_last(grads, small_packed):
    ng = len(grads)
    ch = SMALL_ROWS // N_DEV
    max_rows = max(g.shape[1] for g in grads)
    cols = grads[0].shape[2]

    def body(*refs):
        g_in, s_in = refs[:ng], refs[ng]
        outs = refs[ng + 1:]
        own_out, land, stage = outs[:ng], outs[ng:2 * ng], outs[2 * ng:3 * ng]
        s_red, s_stage = outs[3 * ng], outs[3 * ng + 1]
        (va, vb, vo, vs, sm_in, sm_out, d2d_send, d2d_recv, ici_send, ici_recv, s1_send, s1_recv, s2_send, s2_recv,
         local_sems) = outs[3 * ng + 2:]
        x, y, c, me = _mesh_pos()
        sibling = (x, y, 1 - c)
        chips = [(x, y), (x, 1 - y), (1 - x, y), (1 - x, 1 - y)]

        def idx(chip, core):
            return 4 * chip[0] + 2 * chip[1] + core

        def d2d(k, j):
            return pltpu.make_async_remote_copy(
                src_ref=g_in[k].at[idx(chips[j], 1 - c)], dst_ref=stage[k].at[j], send_sem=d2d_send.at[k, j],
                recv_sem=d2d_recv.at[k, j], device_id=sibling, device_id_type=MESH_ID)

        def ici(k, j, slot):
            rows = g_in[k].shape[1]
            return pltpu.make_async_remote_copy(
                src_ref=vo.at[slot, pl.ds(0, rows)], dst_ref=land[k].at[j - 1], send_sem=ici_send.at[k, j - 1],
                recv_sem=ici_recv.at[k, j - 1], device_id=(*chips[j], c), device_id_type=MESH_ID)

        def small_scatter(r):
            px, py, pc = _peer(x, y, c, r)
            return pltpu.make_async_remote_copy(
                src_ref=s_in.at[pl.ds(pl.multiple_of((4 * px + 2 * py + pc) * ch, 8), ch)], dst_ref=s_stage.at[me],
                send_sem=s1_send.at[r - 1], recv_sem=s1_recv.at[r - 1], device_id=(px, py, pc), device_id_type=MESH_ID)

        def small_gather(r):
            return pltpu.make_async_remote_copy(
                src_ref=sm_out, dst_ref=s_red.at[me], send_sem=s2_send.at[r - 1], recv_sem=s2_recv.at[r - 1],
                device_id=_peer(x, y, c, r), device_id_type=MESH_ID)

        for r in range(1, N_DEV):
            small_scatter(r).start()
        mine = pltpu.make_async_copy(s_in.at[pl.ds(pl.multiple_of(me * ch, 8), ch)], s_stage.at[me], local_sems.at[0])
        mine.start()
        for j in (1, 2, 3, 0):
            for k in range(ng):
                d2d(k, j).start()

        for r in range(1, N_DEV):
            small_scatter(r).wait_recv()
        mine.wait()
        load = pltpu.make_async_copy(s_stage, sm_in, local_sems.at[1])
        load.start()
        load.wait()
        total = sm_in[0]
        for i in range(1, N_DEV):
            total = total + sm_in[i]
        sm_out[...] = total
        for r in range(1, N_DEV):
            small_gather(r).start()
        keep = pltpu.make_async_copy(sm_out, s_red.at[me], local_sems.at[2])
        keep.start()

        pairs = [(k, j) for j in (1, 2, 3, 0) for k in range(ng)]
        in_flight = {}
        for i, (k, j) in enumerate(pairs):
            slot = i % 2
            rows = g_in[k].shape[1]
            if slot in in_flight:
                in_flight.pop(slot).wait_send()
            d2d(k, j).wait_recv()
            la = pltpu.make_async_copy(g_in[k].at[idx(chips[j], c)], va.at[slot, pl.ds(0, rows)], local_sems.at[3])
            lb = pltpu.make_async_copy(stage[k].at[j], vb.at[slot, pl.ds(0, rows)], local_sems.at[4])
            la.start()
            lb.start()
            la.wait()
            lb.wait()
            total = va[slot, pl.ds(0, rows)].astype(F32) + vb[slot, pl.ds(0, rows)].astype(F32)
            if j == 0:
                vs[pl.ds(0, rows)] = total
                st = pltpu.make_async_copy(vs.at[pl.ds(0, rows)], own_out[k], local_sems.at[5])
                st.start()
                st.wait()
            else:
                vo[slot, pl.ds(0, rows)] = total.astype(BF16)
                cp = ici(k, j, slot)
                cp.start()
                in_flight[slot] = cp
        for cp in in_flight.values():
            cp.wait_send()

        for j in (1, 2, 3, 0):
            for k in range(ng):
                d2d(k, j).wait_send()
        for j in (1, 2, 3):
            for k in range(ng):
                ici(k, j, 0).wait_recv()
        for r in range(1, N_DEV):
            small_scatter(r).wait_send()
            small_gather(r).wait_send()
            small_gather(r).wait_recv()
        keep.wait()

    out_shape = [jax.ShapeDtypeStruct(g.shape[1:], F32) for g in grads]
    out_shape += [jax.ShapeDtypeStruct((3,) + g.shape[1:], BF16) for g in grads]
    out_shape += [jax.ShapeDtypeStruct((4,) + g.shape[1:], BF16) for g in grads]
    out_shape += [jax.ShapeDtypeStruct((N_DEV, ch, LANES), F32), jax.ShapeDtypeStruct((N_DEV, ch, LANES), F32)]
    outs = pl.pallas_call(
        body, name="exchange_last",
        in_specs=[ANY_SPEC] * (ng + 1), out_specs=[ANY_SPEC] * len(out_shape), out_shape=out_shape,
        scratch_shapes=[pltpu.VMEM((2, max_rows, cols), BF16), pltpu.VMEM((2, max_rows, cols), BF16),
                        pltpu.VMEM((2, max_rows, cols), BF16), pltpu.VMEM((max_rows, cols), F32),
                        pltpu.VMEM((N_DEV, ch, LANES), F32), pltpu.VMEM((ch, LANES), F32),
                        pltpu.SemaphoreType.DMA((ng, 4)), pltpu.SemaphoreType.DMA((ng, 4)),
                        pltpu.SemaphoreType.DMA((ng, 3)), pltpu.SemaphoreType.DMA((ng, 3)),
                        pltpu.SemaphoreType.DMA((N_PEERS,)), pltpu.SemaphoreType.DMA((N_PEERS,)),
                        pltpu.SemaphoreType.DMA((N_PEERS,)), pltpu.SemaphoreType.DMA((N_PEERS,)),
                        pltpu.SemaphoreType.DMA((6,))],
        compiler_params=pltpu.CompilerParams(has_side_effects=True, vmem_limit_bytes=VMEM_LIMIT),
    )(*grads, small_packed)
    return outs[:ng], outs[ng:2 * ng], outs[3 * ng].reshape(SMALL_ROWS, LANES)


def _adamw_math(w, g, m, v):
    m2 = ADAM_B1 * m + (1.0 - ADAM_B1) * g
    v2 = ADAM_B2 * v + (1.0 - ADAM_B2) * (g * g)
    m_hat = m2 / (1.0 - ADAM_B1 ** ADAM_STEP)
    v_hat = v2 / (1.0 - ADAM_B2 ** ADAM_STEP)
    delta = -ADAM_LR * (m_hat / (jnp.sqrt(v_hat) + ADAM_EPS) + ADAM_WD * w)
    return delta, m2, v2


ADAM_ROW_TILES = 2


def _adamw_big(own, parts, w, m, v, name):
    shape = w.shape
    own_is_blocks = own.ndim == 3
    tr = shape[0] // ADAM_ROW_TILES
    n_parts = parts.shape[0]

    def body(own_ref, p_ref, w_ref, m_ref, v_ref, g_ref, d_ref, m2_ref, v2_ref, own_s, sem):
        rows = pl.ds(pl.multiple_of(pl.program_id(0) * tr, 16), tr)
        if own_is_blocks:
            cp = pltpu.make_async_copy(own_ref.at[_mesh_pos()[3], rows], own_s, sem)
        else:
            cp = pltpu.make_async_copy(own_ref.at[rows], own_s, sem)
        cp.start()
        cp.wait()
        g = own_s[...].astype(F32)
        for i in range(parts.shape[0]):
            g = g + p_ref[i].astype(F32)
        delta, m2, v2 = _adamw_math(w_ref[...], g, m_ref[...], v_ref[...])
        g_ref[...] = g
        d_ref[...] = delta
        m2_ref[...] = m2
        v2_ref[...] = v2

    tile = pl.BlockSpec((tr, shape[1]), lambda i: (i, 0))
    return pl.pallas_call(
        body, name=name, grid=(ADAM_ROW_TILES,),
        in_specs=[ANY_SPEC, pl.BlockSpec((n_parts, tr, shape[1]), lambda i: (0, i, 0)), tile, tile, tile],
        out_specs=[tile] * 4, out_shape=[jax.ShapeDtypeStruct(shape, F32)] * 4,
        scratch_shapes=[pltpu.VMEM((tr, shape[1]), own.dtype), pltpu.SemaphoreType.DMA(())],
        compiler_params=_cparams(("arbitrary",)),
    )(own, parts, w, m, v)


def _pack_small(grads):
    names = list(SMALL)

    def body(*refs):
        ins, out = dict(zip(names, refs[:-1])), refs[-1]
        out[...] = jnp.zeros_like(out)
        for re, im in SMALL_PAIRS:
            off, rows = SMALL_OFFSET[re], SMALL[re][0]
            out[off:off + rows, :] = jnp.concatenate([ins[re][...], ins[im][...]], axis=1)
        for n in SMALL_VECS:
            off, vec = SMALL_OFFSET[n], ins[n][...]
            for i in range(SMALL[n][1] // LANES):
                out[off + i:off + i + 1, :] = vec[:, i * LANES:(i + 1) * LANES]
        for n in SMALL_TILES:
            off, (rows, cols) = SMALL_OFFSET[n], SMALL[n]
            out[off:off + rows, 0:cols] = ins[n][...]

    return pl.pallas_call(
        body, name="pack_small", out_shape=jax.ShapeDtypeStruct((SMALL_ROWS, LANES), F32),
        compiler_params=_cparams(),
    )(*[grads[n] for n in names])


def _unpack_small_ref(g_ref, n):
    off, (rows, cols) = SMALL_OFFSET[n], SMALL[n]
    for re, im in SMALL_PAIRS:
        if n == re:
            return g_ref[off:off + rows, 0:HALF_LANES]
        if n == im:
            return g_ref[off:off + rows, HALF_LANES:LANES]
    if n in SMALL_VECS:
        return jnp.concatenate([g_ref[off + i:off + i + 1, :] for i in range(cols // LANES)], axis=1)
    return g_ref[off:off + rows, 0:cols]


def _adamw_small(g_packed, w, m, v):
    names = list(SMALL_PARAMS)
    n = len(names)

    def body(g_ref, *refs):
        w_refs, m_refs, v_refs, outs = refs[:n], refs[n:2 * n], refs[2 * n:3 * n], refs[3 * n:]
        for idx, name in enumerate(names):
            g = _unpack_small_ref(g_ref, name)
            delta, m2, v2 = _adamw_math(w_refs[idx][...], g, m_refs[idx][...], v_refs[idx][...])
            outs[4 * idx][...] = g
            outs[4 * idx + 1][...] = delta
            outs[4 * idx + 2][...] = m2
            outs[4 * idx + 3][...] = v2
        outs[4 * n][...] = _unpack_small_ref(g_ref, "loss")

    outs = pl.pallas_call(
        body, name="adamw_small",
        out_shape=[jax.ShapeDtypeStruct(SMALL[name], F32) for name in names for _ in range(4)]
        + [jax.ShapeDtypeStruct(SMALL["loss"], F32)],
        compiler_params=_cparams(),
    )(g_packed, *[w[k] for k in names], *[m[k] for k in names], *[v[k] for k in names])
    return {name: outs[4 * idx:4 * idx + 4] for idx, name in enumerate(names)}, outs[4 * n]


WEIGHT_NAMES = ['norm_ffn1', 'ffn1_w_gate', 'ffn1_w_up', 'ffn1_w_down', 'norm_mix', 'w_in', 'attn_sinks',
                'ssm_lambda_re', 'ssm_lambda_im', 'ssm_log_dt', 'ssm_b_re', 'ssm_b_im', 'ssm_c_re', 'ssm_c_im',
                'ssm_d', 'ssm_glu_w', 'ssm_glu_b', 'attn_out_norm', 'ssm_out_norm', 'w_out', 'norm_ffn2',
                'ffn2_w_gate', 'ffn2_w_up', 'ffn2_w_down', 'final_norm']


def kernel(x, norm_ffn1, ffn1_w_gate, ffn1_w_up, ffn1_w_down, norm_mix, w_in, attn_sinks, ssm_lambda_re, ssm_lambda_im, ssm_log_dt, ssm_b_re, ssm_b_im, ssm_c_re, ssm_c_im, ssm_d, ssm_glu_w, ssm_glu_b, attn_out_norm, ssm_out_norm, w_out, norm_ffn2, ffn2_w_gate, ffn2_w_up, ffn2_w_down, final_norm, loss_target, m_norm_ffn1, m_ffn1_w_gate, m_ffn1_w_up, m_ffn1_w_down, m_norm_mix, m_w_in, m_attn_sinks, m_ssm_lambda_re, m_ssm_lambda_im, m_ssm_log_dt, m_ssm_b_re, m_ssm_b_im, m_ssm_c_re, m_ssm_c_im, m_ssm_d, m_ssm_glu_w, m_ssm_glu_b, m_attn_out_norm, m_ssm_out_norm, m_w_out, m_norm_ffn2, m_ffn2_w_gate, m_ffn2_w_up, m_ffn2_w_down, m_final_norm, v_norm_ffn1, v_ffn1_w_gate, v_ffn1_w_up, v_ffn1_w_down, v_norm_mix, v_w_in, v_attn_sinks, v_ssm_lambda_re, v_ssm_lambda_im, v_ssm_log_dt, v_ssm_b_re, v_ssm_b_im, v_ssm_c_re, v_ssm_c_im, v_ssm_d, v_ssm_glu_w, v_ssm_glu_b, v_attn_out_norm, v_ssm_out_norm, v_w_out, v_norm_ffn2, v_ffn2_w_gate, v_ffn2_w_up, v_ffn2_w_down, v_final_norm):
    args = dict(locals())
    weights = {n: args[n] for n in WEIGHT_NAMES}
    moms = {n: args["m_" + n] for n in WEIGHT_NAMES}
    vars_ = {n: args["v_" + n] for n in WEIGHT_NAMES}

    def shard2d(a, k):
        a = a.reshape(a.shape[-2], a.shape[-1])
        return a.T if BIG[k][3] else a

    def shard_master(a, k):
        return (a.T if BIG[k][3] else a).reshape(weights[BIG[k][0]].shape)

    def blocks(g, k):
        return g.reshape(N_DEV, BIG[k][1], BIG[k][2])

    def full(g, k):
        return g.reshape(N_DEV * BIG[k][1], BIG[k][2])

    shards = dict(zip(BIG, _cast_shards({k: shard2d(weights[BIG[k][0]], k) for k in BIG})))
    nf = len(FIRST_GROUP)
    got = _gather_first([shards[k] for k in FIRST_GROUP], [shards[k] for k in LATE_GROUP])
    w_first = {k: full(g, k) for k, g in zip(FIRST_GROUP, got[:nf])}
    late = {}
    late["own_sems"], late["srcs"], late["lands"], w_token = _late_gather_call(
        "gather_late_start", 0, [shards[k] for k in LATE_GROUP], got[nf:], [])

    def late_pass(dep):
        late["pass_sems"], late["srcs"], late["lands"], token = _late_gather_call(
            "gather_late_pass", 1, late["srcs"], late["lands"], late["own_sems"], after=dep)
        return token

    def late_weights(dep):
        _, _, lands, _ = _late_gather_call("gather_late_wait", 2, late["srcs"], late["lands"],
                                           late["own_sems"] + late["pass_sems"], after=dep)
        return {k: full(g, k) for k, g in zip(LATE_GROUP, lands)}

    early = {}

    def early_grads(g):
        srcs = [blocks(g[k], k) for k in LATE_GROUP]
        lands = [lax.empty((N_PEERS, BIG[k][1], BIG[k][2]), BF16) for k in LATE_GROUP]
        early["send"], early["recv"], early["srcs"], early["lands"], token = _split_start(
            "grads_late_start", srcs, lands, scatter=True)
        return token

    def small2d(a, n):
        if n in SMALL_TRANSPOSED:
            a = jnp.swapaxes(a, -1, -2)
        return a.reshape(SMALL[n])

    def small_master(a, n):
        if n in SMALL_TRANSPOSED:
            shape = weights[n].shape
            return jnp.swapaxes(a.reshape(shape[:-2] + (shape[-1], shape[-2])), -1, -2)
        return a.reshape(weights[n].shape)

    small_p = {n: small2d(weights[n], n) for n in SMALL_PARAMS}
    _, grad_x, g_first, g_small = _local_step(
        x.reshape(SEQ, D_MODEL), loss_target.reshape(SEQ, D_MODEL), w_first, small_p, late_weights, early_grads,
        after=w_token, midway=late_pass)

    own_sums, first_parts, small_grad = _exchange_last([blocks(g_first[k], k) for k in FIRST_GROUP],
                                                       _pack_small(g_small))
    own_late, late_parts = _split_wait("grads_late_wait", early["send"], early["recv"], early["srcs"],
                                       early["lands"], True, small_grad)
    own = dict(zip(FIRST_GROUP + LATE_GROUP, list(own_sums) + list(own_late)))
    parts = dict(zip(FIRST_GROUP + LATE_GROUP, list(first_parts) + list(late_parts)))
    outs = {}
    for k in BIG:
        n = BIG[k][0]
        outs[n] = [shard_master(o, k) for o in
                   _adamw_big(own[k], parts[k], shard2d(weights[n], k), shard2d(moms[n], k), shard2d(vars_[n], k),
                              "adamw_" + n)]
    small_out, loss_row = _adamw_small(small_grad, small_p, {n: small2d(moms[n], n) for n in SMALL_PARAMS},
                                       {n: small2d(vars_[n], n) for n in SMALL_PARAMS})
    for n in SMALL_PARAMS:
        outs[n] = [small_master(o, n) for o in small_out[n]]

    result = [loss_row[0, 0], grad_x.reshape(x.shape)]
    for i in range(4):
        result += [outs[n][i] for n in WEIGHT_NAMES]
    return tuple(result)
```

```python
import functools

import jax
import jax.numpy as jnp
from jax import lax
from jax.experimental import pallas as pl
from jax.experimental.pallas import tpu as pltpu

F32 = jnp.float32
BF16 = jnp.bfloat16

N_DEV = 8
SEQ = 2048
D_MODEL = 1024
D_FF = 2816
ATTN_HEADS = 8
KV_HEADS = 2
HEAD_DIM = 64
ATTN_WIDTH = 512
KV_WIDTH = 128
WINDOW = 128
SSM_WIDTH = 512
IN_WIDTH = 1280
EPS = 1e-6
NEG_INF = -1e30
LAMBDA_RE_MAX = -1e-4
LANES = 128
N_LANE_BLOCKS = 16
SCAN_CHUNK = SEQ // 8

ADAM_LR = 0.001
ADAM_B1 = 0.9
ADAM_B2 = 0.999
ADAM_EPS = 1e-08
ADAM_WD = 0.01
ADAM_STEP = 10

VMEM_LIMIT = 56 * 1024 * 1024
MESH_ID = pl.DeviceIdType.MESH


def _cparams(sem=None):
    return pltpu.CompilerParams(dimension_semantics=sem, vmem_limit_bytes=VMEM_LIMIT)


def _dot(a, b):
    return jnp.dot(a, b, preferred_element_type=F32)


def _dot_nt(a, b):
    return lax.dot_general(a, b, (((1,), (1,)), ((), ())), preferred_element_type=F32)


def _dot_tn(a, b):
    return lax.dot_general(a, b, (((0,), (0,)), ((), ())), preferred_element_type=F32)


def _rms_fwd(x, g):
    r = lax.rsqrt(jnp.mean(x * x, axis=-1, keepdims=True) + EPS)
    return x * r * g


def _rms_bwd(dh, x, g):
    r = lax.rsqrt(jnp.mean(x * x, axis=-1, keepdims=True) + EPS)
    xh = x * r
    dg = jnp.sum(dh * xh, axis=0, keepdims=True)
    dxh = dh * g
    dx = r * (dxh - xh * jnp.mean(dxh * xh, axis=-1, keepdims=True))
    return dx, dg


def _sigmoid(x):
    return 1.0 / (1.0 + jnp.exp(-x))


FFN_TM = 512
FFN_TF = 1408


def _ffn_fwd(x, g, wgt, wut, wd, name, after=None):
    tm, tf = FFN_TM, FFN_TF
    nj = D_FF // tf
    deps = [] if after is None else [after]

    def body(x_ref, g_ref, wg_ref, wu_ref, wd_ref, *rest):
        xo_ref, h_ref, a_ref, b_ref, h_s, acc = rest[len(deps):]
        j = pl.program_id(1)

        @pl.when(j == 0)
        def _():
            h = _rms_fwd(x_ref[...], g_ref[...]).astype(BF16)
            h_s[...] = h
            h_ref[...] = h
            acc[...] = jnp.zeros_like(acc)

        h = h_s[...]
        a = _dot_nt(h, wg_ref[...])
        b = _dot_nt(h, wu_ref[...])
        a_ref[...] = a.astype(BF16)
        b_ref[...] = b.astype(BF16)
        s = (a * _sigmoid(a) * b).astype(BF16)
        acc[...] += _dot(s, wd_ref[...])

        @pl.when(j == nj - 1)
        def _():
            xo_ref[...] = x_ref[...] + 0.5 * acc[...]

    return pl.pallas_call(
        body, name=name, grid=(SEQ // tm, nj),
        in_specs=[pl.BlockSpec((tm, D_MODEL), lambda i, j: (i, 0)),
                  pl.BlockSpec((1, D_MODEL), lambda i, j: (0, 0)),
                  pl.BlockSpec((tf, D_MODEL), lambda i, j: (j, 0)),
                  pl.BlockSpec((tf, D_MODEL), lambda i, j: (j, 0)),
                  pl.BlockSpec((tf, D_MODEL), lambda i, j: (j, 0))] + [pl.BlockSpec(memory_space=pl.ANY)] * len(deps),
        out_specs=[pl.BlockSpec((tm, D_MODEL), lambda i, j: (i, 0)),
                   pl.BlockSpec((tm, D_MODEL), lambda i, j: (i, 0)),
                   pl.BlockSpec((tm, tf), lambda i, j: (i, j)),
                   pl.BlockSpec((tm, tf), lambda i, j: (i, j))],
        out_shape=[jax.ShapeDtypeStruct((SEQ, D_MODEL), F32), jax.ShapeDtypeStruct((SEQ, D_MODEL), BF16),
                   jax.ShapeDtypeStruct((SEQ, D_FF), BF16), jax.ShapeDtypeStruct((SEQ, D_FF), BF16)],
        scratch_shapes=[pltpu.VMEM((tm, D_MODEL), BF16), pltpu.VMEM((tm, D_MODEL), F32)],
        compiler_params=_cparams(("parallel", "arbitrary")),
    )(x, g, wgt, wut, wd, *deps)


def _ffn_bwd_act(dxo, x, g, a, b, wgt, wut, wd, name):
    tm, tf = FFN_TM // 2, FFN_TF
    nj = D_FF // tf

    def body(dxo_ref, x_ref, g_ref, a_ref, b_ref, wg_ref, wu_ref, wd_ref,
             dx_ref, da_ref, db_ref, s_ref, df_ref, dg_ref, df_s, acc):
        i = pl.program_id(0)
        j = pl.program_id(1)

        @pl.when(j == 0)
        def _():
            df = (0.5 * dxo_ref[...]).astype(BF16)
            df_s[...] = df
            df_ref[...] = df
            acc[...] = jnp.zeros_like(acc)

        ds = _dot_nt(df_s[...], wd_ref[...])
        av = a_ref[...].astype(F32)
        bv = b_ref[...].astype(F32)
        sig = _sigmoid(av)
        sl = av * sig
        s_ref[...] = (sl * bv).astype(BF16)
        db = (ds * sl).astype(BF16)
        da = (ds * bv * (sig * (1.0 + av * (1.0 - sig)))).astype(BF16)
        da_ref[...] = da
        db_ref[...] = db
        acc[...] += _dot(da, wg_ref[...]) + _dot(db, wu_ref[...])

        @pl.when(j == nj - 1)
        def _():
            dx, dg = _rms_bwd(acc[...], x_ref[...], g_ref[...])
            dx_ref[...] = dxo_ref[...] + dx

            @pl.when(i == 0)
            def _():
                dg_ref[...] = dg

            @pl.when(i != 0)
            def _():
                dg_ref[...] += dg

    row = lambda i, j: (i, 0)
    col = lambda i, j: (j, 0)
    tile = lambda i, j: (i, j)
    return pl.pallas_call(
        body, name=name, grid=(SEQ // tm, nj),
        in_specs=[pl.BlockSpec((tm, D_MODEL), row), pl.BlockSpec((tm, D_MODEL), row),
                  pl.BlockSpec((1, D_MODEL), lambda i, j: (0, 0)),
                  pl.BlockSpec((tm, tf), tile), pl.BlockSpec((tm, tf), tile),
                  pl.BlockSpec((tf, D_MODEL), col), pl.BlockSpec((tf, D_MODEL), col), pl.BlockSpec((tf, D_MODEL), col)],
        out_specs=[pl.BlockSpec((tm, D_MODEL), row),
                   pl.BlockSpec((tm, tf), tile), pl.BlockSpec((tm, tf), tile), pl.BlockSpec((tm, tf), tile),
                   pl.BlockSpec((tm, D_MODEL), row),
                   pl.BlockSpec((1, D_MODEL), lambda i, j: (0, 0))],
        out_shape=[jax.ShapeDtypeStruct((SEQ, D_MODEL), F32),
                   jax.ShapeDtypeStruct((SEQ, D_FF), BF16), jax.ShapeDtypeStruct((SEQ, D_FF), BF16),
                   jax.ShapeDtypeStruct((SEQ, D_FF), BF16),
                   jax.ShapeDtypeStruct((SEQ, D_MODEL), BF16),
                   jax.ShapeDtypeStruct((1, D_MODEL), F32)],
        scratch_shapes=[pltpu.VMEM((tm, D_MODEL), BF16), pltpu.VMEM((tm, D_MODEL), F32)],
        compiler_params=_cparams(("arbitrary", "arbitrary")),
    )(dxo, x, g, a, b, wgt, wut, wd)


def _mm_tn(pairs, name, tmm=256):
    m = pairs[0][0].shape[1]
    n_pairs = len(pairs)

    def body(*refs):
        ins, outs = refs[:2 * n_pairs], refs[2 * n_pairs:]
        for p in range(n_pairs):
            outs[p][...] = _dot_tn(ins[2 * p][...], ins[2 * p + 1][...]).astype(BF16)

    in_specs, out_specs, out_shape, args = [], [], [], []
    for a, b in pairs:
        n = b.shape[1]
        in_specs += [pl.BlockSpec((SEQ, tmm), lambda i: (0, i)), pl.BlockSpec((SEQ, n), lambda i: (0, 0))]
        out_specs.append(pl.BlockSpec((tmm, n), lambda i: (i, 0)))
        out_shape.append(jax.ShapeDtypeStruct((m, n), BF16))
        args += [a, b]
    return pl.pallas_call(body, name=name, grid=(m // tmm,), in_specs=in_specs, out_specs=out_specs,
                          out_shape=out_shape, compiler_params=_cparams(("parallel",)))(*args)


MIX_TM = 256


def _mixin_fwd(x, g, wint):
    tm = MIX_TM

    def body(x_ref, g_ref, w_ref, h_ref, q_ref, k_ref, v_ref, u_ref):
        h = _rms_fwd(x_ref[...], g_ref[...]).astype(BF16)
        h_ref[...] = h
        proj = _dot_nt(h, w_ref[...])
        q_ref[...] = proj[:, :ATTN_WIDTH]
        k_ref[...] = proj[:, ATTN_WIDTH:ATTN_WIDTH + KV_WIDTH]
        v_ref[...] = proj[:, ATTN_WIDTH + KV_WIDTH:ATTN_WIDTH + 2 * KV_WIDTH]
        u_ref[...] = proj[:, ATTN_WIDTH + 2 * KV_WIDTH:]

    row = lambda i: (i, 0)
    return pl.pallas_call(
        body, name="mixin_fwd", grid=(SEQ // tm,),
        in_specs=[pl.BlockSpec((tm, D_MODEL), row), pl.BlockSpec((1, D_MODEL), lambda i: (0, 0)),
                  pl.BlockSpec((IN_WIDTH, D_MODEL), lambda i: (0, 0))],
        out_specs=[pl.BlockSpec((tm, D_MODEL), row), pl.BlockSpec((tm, ATTN_WIDTH), row),
                   pl.BlockSpec((tm, KV_WIDTH), row), pl.BlockSpec((tm, KV_WIDTH), row),
                   pl.BlockSpec((tm, SSM_WIDTH), row)],
        out_shape=[jax.ShapeDtypeStruct((SEQ, D_MODEL), BF16), jax.ShapeDtypeStruct((SEQ, ATTN_WIDTH), F32),
                   jax.ShapeDtypeStruct((SEQ, KV_WIDTH), F32), jax.ShapeDtypeStruct((SEQ, KV_WIDTH), F32),
                   jax.ShapeDtypeStruct((SEQ, SSM_WIDTH), F32)],
        compiler_params=_cparams(("parallel",)),
    )(x, g, wint)


def _mixin_bwd(dq, dk, dv, du, wint, x, g, dres):
    tm = MIX_TM

    def body(dq_ref, dk_ref, dv_ref, du_ref, w_ref, x_ref, g_ref, dres_ref, dx_ref, dp_ref, dg_ref):
        i = pl.program_id(0)
        dp = jnp.concatenate([dq_ref[...], dk_ref[...], dv_ref[...], du_ref[...]], axis=-1).astype(BF16)
        dp_ref[...] = dp
        dh = _dot(dp, w_ref[...])
        dx, dg = _rms_bwd(dh, x_ref[...], g_ref[...])
        dx_ref[...] = dres_ref[...] + dx

        @pl.when(i == 0)
        def _():
            dg_ref[...] = dg

        @pl.when(i != 0)
        def _():
            dg_ref[...] += dg

    row = lambda i: (i, 0)
    const = lambda i: (0, 0)
    return pl.pallas_call(
        body, name="mixin_bwd", grid=(SEQ // tm,),
        in_specs=[pl.BlockSpec((tm, ATTN_WIDTH), row), pl.BlockSpec((tm, KV_WIDTH), row),
                  pl.BlockSpec((tm, KV_WIDTH), row), pl.BlockSpec((tm, SSM_WIDTH), row),
                  pl.BlockSpec((IN_WIDTH, D_MODEL), const), pl.BlockSpec((tm, D_MODEL), row),
                  pl.BlockSpec((1, D_MODEL), const), pl.BlockSpec((tm, D_MODEL), row)],
        out_specs=[pl.BlockSpec((tm, D_MODEL), row), pl.BlockSpec((tm, IN_WIDTH), row),
                   pl.BlockSpec((1, D_MODEL), const)],
        out_shape=[jax.ShapeDtypeStruct((SEQ, D_MODEL), F32), jax.ShapeDtypeStruct((SEQ, IN_WIDTH), BF16),
                   jax.ShapeDtypeStruct((1, D_MODEL), F32)],
        compiler_params=_cparams(("arbitrary",)),
    )(dq, dk, dv, du, wint, x, g, dres)


N_QBLOCKS = SEQ // WINDOW
GROUP = ATTN_HEADS // KV_HEADS
SCALE = HEAD_DIM ** -0.5


def _alibi_slope(h):
    return 2.0 ** (-8.0 * (h + 1) / ATTN_HEADS)


def _window_masks(n):
    t_idx = lax.broadcasted_iota(jnp.int32, (WINDOW, 3 * WINDOW), 0)
    s_idx = lax.broadcasted_iota(jnp.int32, (WINDOW, 3 * WINDOW), 1)
    rel = s_idx - WINDOW - t_idx
    absrel = jnp.abs(rel)
    key_pos = n * WINDOW - WINDOW + s_idx
    valid = (absrel <= WINDOW) & (key_pos >= 0) & (key_pos < SEQ)
    return absrel.astype(F32), valid


def _group_rows(ref, r0, gi):
    return jnp.concatenate(
        [ref[pl.ds(r0, WINDOW), (gi * GROUP + hh) * HEAD_DIM:(gi * GROUP + hh + 1) * HEAD_DIM].astype(BF16)
         for hh in range(GROUP)], axis=0)


def _group_probs(qg, kw, absrel, valid, gi, sk_ref):
    bias = jnp.concatenate([jnp.where(valid, -_alibi_slope(gi * GROUP + hh) * absrel, NEG_INF)
                            for hh in range(GROUP)], axis=0)
    sink = jnp.concatenate([jnp.full((WINDOW, 1), sk_ref[0, gi * GROUP + hh], F32) for hh in range(GROUP)], axis=0)
    s = _dot_nt(qg, kw) * SCALE + bias
    m = jnp.maximum(jnp.max(s, axis=-1, keepdims=True), sink)
    p = jnp.exp(s - m)
    ps = jnp.exp(sink - m)
    inv = 1.0 / (jnp.sum(p, axis=-1, keepdims=True) + ps)
    return p * inv, ps * inv


def _attn_fwd(q, kp, vp, sinks):
    def body(sk_ref, q_ref, kp_ref, vp_ref, o_ref):
        def blk(n, carry):
            r0 = pl.multiple_of(n * WINDOW, WINDOW)
            absrel, valid = _window_masks(n)
            for gi in range(KV_HEADS):
                kw = kp_ref[pl.ds(r0, 3 * WINDOW), gi * HEAD_DIM:(gi + 1) * HEAD_DIM].astype(BF16)
                vw = vp_ref[pl.ds(r0, 3 * WINDOW), gi * HEAD_DIM:(gi + 1) * HEAD_DIM].astype(BF16)
                pr, _ = _group_probs(_group_rows(q_ref, r0, gi), kw, absrel, valid, gi, sk_ref)
                og = _dot(pr.astype(BF16), vw)
                for hh in range(GROUP):
                    h = gi * GROUP + hh
                    o_ref[pl.ds(r0, WINDOW), h * HEAD_DIM:(h + 1) * HEAD_DIM] = og[hh * WINDOW:(hh + 1) * WINDOW]
            return carry

        lax.fori_loop(0, N_QBLOCKS, blk, 0)

    vmem = pl.BlockSpec(memory_space=pltpu.VMEM)
    return pl.pallas_call(
        body, name="attn_fwd",
        in_specs=[pl.BlockSpec(memory_space=pltpu.SMEM), vmem, vmem, vmem], out_specs=vmem,
        out_shape=jax.ShapeDtypeStruct((SEQ, ATTN_WIDTH), F32),
        compiler_params=_cparams(),
    )(sinks, q, kp, vp)


def _attn_bwd(q, kp, vp, sinks, do):
    def body(sk_ref, q_ref, kp_ref, vp_ref, do_ref, dq_ref, dkp_ref, dvp_ref, dsk_ref, dsk_acc):
        dkp_ref[...] = jnp.zeros_like(dkp_ref)
        dvp_ref[...] = jnp.zeros_like(dvp_ref)
        dsk_acc[...] = jnp.zeros_like(dsk_acc)

        def blk(n, carry):
            r0 = pl.multiple_of(n * WINDOW, WINDOW)
            absrel, valid = _window_masks(n)
            for gi in range(KV_HEADS):
                gcols = slice(gi * HEAD_DIM, (gi + 1) * HEAD_DIM)
                kw = kp_ref[pl.ds(r0, 3 * WINDOW), gcols].astype(BF16)
                vw = vp_ref[pl.ds(r0, 3 * WINDOW), gcols].astype(BF16)
                qg = _group_rows(q_ref, r0, gi)
                dog = _group_rows(do_ref, r0, gi)
                pr, psink = _group_probs(qg, kw, absrel, valid, gi, sk_ref)
                dp = _dot_nt(dog, vw)
                delta = jnp.sum(pr * dp, axis=-1, keepdims=True)
                ds = (pr * (dp - delta)).astype(BF16)
                dsink = -(psink * delta)
                dqg = _dot(ds, kw) * SCALE
                for hh in range(GROUP):
                    h = gi * GROUP + hh
                    rows = slice(hh * WINDOW, (hh + 1) * WINDOW)
                    dsk_acc[:, h:h + 1] += dsink[rows]
                    dq_ref[pl.ds(r0, WINDOW), h * HEAD_DIM:(h + 1) * HEAD_DIM] = dqg[rows]
                dkp_ref[pl.ds(r0, 3 * WINDOW), gcols] += _dot_tn(ds, qg) * SCALE
                dvp_ref[pl.ds(r0, 3 * WINDOW), gcols] += _dot_tn(pr.astype(BF16), dog)
            return carry

        lax.fori_loop(0, N_QBLOCKS, blk, 0)
        dsk_ref[...] = jnp.sum(dsk_acc[...], axis=0, keepdims=True)

    vmem = pl.BlockSpec(memory_space=pltpu.VMEM)
    return pl.pallas_call(
        body, name="attn_bwd",
        in_specs=[pl.BlockSpec(memory_space=pltpu.SMEM), vmem, vmem, vmem, vmem],
        out_specs=[vmem, vmem, vmem, vmem],
        out_shape=[jax.ShapeDtypeStruct((SEQ, ATTN_WIDTH), F32),
                   jax.ShapeDtypeStruct((SEQ + 2 * WINDOW, KV_WIDTH), F32),
                   jax.ShapeDtypeStruct((SEQ + 2 * WINDOW, KV_WIDTH), F32),
                   jax.ShapeDtypeStruct((1, ATTN_HEADS), F32)],
        scratch_shapes=[pltpu.VMEM((WINDOW, ATTN_HEADS), F32)],
        compiler_params=_cparams(),
    )(sinks, q, kp, vp, do)


HALF_LANES = LANES // 2
BLOCK_ROWS = 32


def _embed_block(bt, q):
    z = jnp.zeros((16, HALF_LANES), bt.dtype)
    blk = jnp.concatenate([jnp.concatenate([bt[:16], z], axis=1), jnp.concatenate([z, bt[16:]], axis=1)], axis=0)
    parts = [jnp.zeros((BLOCK_ROWS * q, LANES), bt.dtype)] if q else []
    parts.append(blk)
    if q < 3:
        parts.append(jnp.zeros((BLOCK_ROWS * (3 - q), LANES), bt.dtype))
    return jnp.concatenate(parts, axis=0)


def _extract_block(m, q):
    blk = m[BLOCK_ROWS * q:BLOCK_ROWS * (q + 1)]
    return jnp.concatenate([blk[:16, :HALF_LANES], blk[16:, HALF_LANES:]], axis=0)


def _ssm_prep(lam_re, lam_im, log_dt, bt_re, bt_im, c_re, c_im):
    nb = 2 * N_LANE_BLOCKS

    def body(lr_ref, li_ref, ldt_ref, btr_ref, bti_ref, ctr_ref, cti_ref,
             ar_ref, ai_ref, bbr_ref, bbi_ref, cpr_ref, cpi_ref):
        lr = jnp.minimum(lr_ref[...], LAMBDA_RE_MAX)
        li = li_ref[...]
        dt = jnp.exp(ldt_ref[...])
        mag = jnp.exp(lr * dt)
        ar = mag * jnp.cos(li * dt)
        ai = mag * jnp.sin(li * dt)
        den = lr * lr + li * li
        cr = ((ar - 1.0) * lr + ai * li) / den
        ci = (ai * lr - (ar - 1.0) * li) / den
        ar_ref[...] = ar
        ai_ref[...] = ai
        for i in range(nb):
            q = i % 4
            rows = slice(BLOCK_ROWS * i, BLOCK_ROWS * (i + 1))
            br = _embed_block(btr_ref[rows, :], q)
            bi = _embed_block(bti_ref[rows, :], q)
            cri, cii = cr[i:i + 1, :], ci[i:i + 1, :]
            bbr_ref[i] = (cri * br - cii * bi).astype(BF16)
            bbi_ref[i] = (cri * bi + cii * br).astype(BF16)
            cpr_ref[i] = _embed_block(ctr_ref[rows, :], q).T.astype(BF16)
            cpi_ref[i] = _embed_block(cti_ref[rows, :], q).T.astype(BF16)

    w_shape = jax.ShapeDtypeStruct((nb, LANES, LANES), BF16)
    return pl.pallas_call(
        body, name="ssm_prep",
        out_shape=[jax.ShapeDtypeStruct((nb, LANES), F32), jax.ShapeDtypeStruct((nb, LANES), F32),
                   w_shape, w_shape, w_shape, w_shape],
        compiler_params=_cparams(),
    )(lam_re, lam_im, log_dt, bt_re, bt_im, c_re, c_im)


def _ssm_prep_bwd(lam_re, lam_im, log_dt, bt_re, bt_im, dar, dai, dbbr, dbbi, dcr, dci):
    nb = 2 * N_LANE_BLOCKS

    def body(lr_ref, li_ref, ldt_ref, btr_ref, bti_ref, dar_ref, dai_ref, dbbr_ref, dbbi_ref, dcr_ref, dci_ref,
             glr_ref, gli_ref, gdt_ref, gbr_ref, gbi_ref, gcre_ref, gcim_ref, gcr_s, gci_s):
        lam = lr_ref[...]
        lr = jnp.minimum(lam, LAMBDA_RE_MAX)
        li = li_ref[...]
        dt = jnp.exp(ldt_ref[...])
        mag = jnp.exp(lr * dt)
        cs = jnp.cos(li * dt)
        sn = jnp.sin(li * dt)
        ar = mag * cs
        ai = mag * sn
        den = lr * lr + li * li
        nr = (ar - 1.0) * lr + ai * li
        ni = ai * lr - (ar - 1.0) * li
        cr = nr / den
        ci = ni / den
        for i in range(nb):
            q = i % 4
            rows = slice(BLOCK_ROWS * i, BLOCK_ROWS * (i + 1))
            br = _embed_block(btr_ref[rows, :], q)
            bi = _embed_block(bti_ref[rows, :], q)
            gbbr = dbbr_ref[i]
            gbbi = dbbi_ref[i]
            cri, cii = cr[i:i + 1, :], ci[i:i + 1, :]
            gcr_s[i:i + 1, :] = jnp.sum(gbbr * br + gbbi * bi, axis=0, keepdims=True)
            gci_s[i:i + 1, :] = jnp.sum(gbbi * br - gbbr * bi, axis=0, keepdims=True)
            gbr_ref[rows, :] = _extract_block(cri * gbbr + cii * gbbi, q)
            gbi_ref[rows, :] = _extract_block(cri * gbbi - cii * gbbr, q)
            gcre_ref[rows, :] = _extract_block(dcr_ref[i].T, q)
            gcim_ref[rows, :] = _extract_block(dci_ref[i].T, q)
        g_cr = gcr_s[...]
        g_ci = gci_s[...]
        g_nr = g_cr / den
        g_ni = g_ci / den
        g_den = -(g_cr * nr + g_ci * ni) / (den * den)
        g_ar = dar_ref[...] + g_nr * lr - g_ni * li
        g_ai = dai_ref[...] + g_nr * li + g_ni * lr
        g_lr = g_nr * (ar - 1.0) + g_ni * ai + g_den * 2.0 * lr
        g_li = g_nr * ai - g_ni * (ar - 1.0) + g_den * 2.0 * li
        g_mag = g_ar * cs + g_ai * sn
        g_th = (g_ai * cs - g_ar * sn) * mag
        g_lr = g_lr + g_mag * mag * dt
        g_li = g_li + g_th * dt
        g_dt = g_mag * mag * lr + g_th * li
        glr_ref[...] = jnp.where(lam < LAMBDA_RE_MAX, g_lr, 0.0)
        gli_ref[...] = g_li
        gl = g_dt * dt
        half = LANES // 2
        gdt_ref[:, 0:1] = jnp.sum(gl[:, :half], axis=1, keepdims=True)
        gdt_ref[:, 1:2] = jnp.sum(gl[:, half:], axis=1, keepdims=True)

    rows_shape = jax.ShapeDtypeStruct((nb * BLOCK_ROWS, HALF_LANES), F32)
    return pl.pallas_call(
        body, name="ssm_prep_bwd",
        out_shape=[jax.ShapeDtypeStruct((nb, LANES), F32), jax.ShapeDtypeStruct((nb, LANES), F32),
                   jax.ShapeDtypeStruct((nb, 2), F32), rows_shape, rows_shape, rows_shape, rows_shape],
        scratch_shapes=[pltpu.VMEM((nb, LANES), F32), pltpu.VMEM((nb, LANES), F32)],
        compiler_params=_cparams(),
    )(lam_re, lam_im, log_dt, bt_re, bt_im, dar, dai, dbbr, dbbi, dcr, dci)


def _cmul(ar, ai, br, bi):
    return ar * br - ai * bi, ar * bi + ai * br


def _interleave_rows(src_ref, dst_ref):
    def step(j, carry):
        dst_ref[pl.ds(pl.multiple_of(j * 8, 8), 8), :] = src_ref[pl.ds(j, 8, stride=SCAN_CHUNK), :]
        return carry
    lax.fori_loop(0, SCAN_CHUNK, step, 0, unroll=4)


def _deinterleave_rows(src_ref, dst_ref):
    def step(j, carry):
        dst_ref[pl.ds(j, 8, stride=SCAN_CHUNK), :] = src_ref[pl.ds(pl.multiple_of(j * 8, 8), 8), :]
        return carry
    lax.fori_loop(0, SCAN_CHUNK, step, 0, unroll=4)


def _scan_inplace(re_ref, im_ref, a_re, a_im, reverse):
    nq = len(a_re)
    ch = SCAN_CHUNK
    ab_re = [jnp.broadcast_to(a, (8, LANES)) for a in a_re]
    ab_im = [jnp.broadcast_to(a, (8, LANES)) for a in a_im]

    def rows(j):
        jj = (ch - 1 - j) if reverse else j
        return pl.ds(pl.multiple_of(jj * 8, 8), 8)

    def sweep(init, store):
        def step(j, st):
            out = []
            r = rows(j)
            for qi in range(nq):
                xr, xi = st[2 * qi], st[2 * qi + 1]
                pr, pi = _cmul(ab_re[qi], ab_im[qi], xr, xi)
                xr = pr + re_ref[qi, r, :]
                xi = pi + im_ref[qi, r, :]
                if store:
                    re_ref[qi, r, :] = xr
                    im_ref[qi, r, :] = xi
                out += [xr, xi]
            return tuple(out)
        return lax.fori_loop(0, ch, step, tuple(init), unroll=2)

    zeros = [jnp.zeros((8, LANES), F32)] * (2 * nq)
    finals = sweep(zeros, store=False)

    row_id = lax.broadcasted_iota(jnp.int32, (8, LANES), 0)
    carries = []
    for qi in range(nq):
        pr, pi = ab_re[qi], ab_im[qi]
        for _ in range(8):
            pr, pi = _cmul(pr, pi, pr, pi)
        fr, fi = finals[2 * qi], finals[2 * qi + 1]
        sr = jnp.zeros((8, LANES), F32)
        si = jnp.zeros((8, LANES), F32)
        for _ in range(7):
            tr, ti = _cmul(pr, pi, sr, si)
            tr, ti = tr + fr, ti + fi
            if reverse:
                sr = jnp.where(row_id == 7, 0.0, pltpu.roll(tr, 7, axis=0))
                si = jnp.where(row_id == 7, 0.0, pltpu.roll(ti, 7, axis=0))
            else:
                sr = jnp.where(row_id == 0, 0.0, pltpu.roll(tr, 1, axis=0))
                si = jnp.where(row_id == 0, 0.0, pltpu.roll(ti, 1, axis=0))
        carries += [sr, si]
    sweep(carries, store=True)


SSM_Q = 4


def _ssm_fwd(u, are, aim, bbr, bbi, cre, cim, dskip, after=None):
    nq = SSM_Q
    deps = [] if after is None else [after]

    def body(u_ref, ar_ref, ai_ref, bbr_ref, bbi_ref, cr_ref, ci_ref, d_ref, *rest):
        y_ref, xr_ref, xi_ref, sre, sim, up, yp = rest[len(deps):]
        _interleave_rows(u_ref, up)
        uf = up[...]
        ub = uf.astype(BF16)
        yp[...] = d_ref[...] * uf
        for d in range(2):
            for qi in range(nq):
                sre[qi] = _dot(ub, bbr_ref[d, qi])
                sim[qi] = _dot(ub, bbi_ref[d, qi])
            _scan_inplace(sre, sim, [ar_ref[d, qi] for qi in range(nq)], [ai_ref[d, qi] for qi in range(nq)],
                          reverse=(d == 1))
            for qi in range(nq):
                xrb = sre[qi].astype(BF16)
                xib = sim[qi].astype(BF16)
                xr_ref[d, qi] = xrb
                xi_ref[d, qi] = xib
                yp[...] += _dot(xrb, cr_ref[d, qi]) - _dot(xib, ci_ref[d, qi])
        _deinterleave_rows(yp, y_ref)

    blk4 = lambda k: (0, k, 0, 0)
    return pl.pallas_call(
        body, name="ssm_fwd", grid=(SSM_WIDTH // LANES,),
        in_specs=[pl.BlockSpec((SEQ, LANES), lambda k: (0, k)),
                  pl.BlockSpec((2, nq, 1, LANES), blk4), pl.BlockSpec((2, nq, 1, LANES), blk4),
                  pl.BlockSpec((2, nq, LANES, LANES), blk4), pl.BlockSpec((2, nq, LANES, LANES), blk4),
                  pl.BlockSpec((2, nq, LANES, LANES), blk4), pl.BlockSpec((2, nq, LANES, LANES), blk4),
                  pl.BlockSpec((1, LANES), lambda k: (0, k))] + [pl.BlockSpec(memory_space=pl.ANY)] * len(deps),
        out_specs=[pl.BlockSpec((SEQ, LANES), lambda k: (0, k)),
                   pl.BlockSpec((2, nq, SEQ, LANES), blk4), pl.BlockSpec((2, nq, SEQ, LANES), blk4)],
        out_shape=[jax.ShapeDtypeStruct((SEQ, SSM_WIDTH), F32),
                   jax.ShapeDtypeStruct((2, N_LANE_BLOCKS, SEQ, LANES), BF16),
                   jax.ShapeDtypeStruct((2, N_LANE_BLOCKS, SEQ, LANES), BF16)],
        scratch_shapes=[pltpu.VMEM((nq, SEQ, LANES), F32), pltpu.VMEM((nq, SEQ, LANES), F32),
                        pltpu.VMEM((SEQ, LANES), F32), pltpu.VMEM((SEQ, LANES), F32)],
        compiler_params=_cparams(("parallel",)),
    )(u, are, aim, bbr, bbi, cre, cim, dskip, *deps)


def _ssm_bwd(dy, u, xr, xi, are, aim, bbr, bbi, cre, cim, dskip, after=None):
    nq = SSM_Q
    body_rows = SEQ - 8
    deps = [] if after is None else [after]

    def body(dy_ref, u_ref, xr_ref, xi_ref, ar_ref, ai_ref, bbr_ref, bbi_ref, cr_ref, ci_ref, d_ref, *rest):
        (du_ref, dd_ref, dcr_ref, dci_ref, dbr_ref, dbi_ref, dar_ref, dai_ref,
         sre, sim, up, dyp, dup) = rest[len(deps):]
        _interleave_rows(u_ref, up)
        _interleave_rows(dy_ref, dyp)
        dyf = dyp[...]
        uf = up[...]
        dyb = dyf.astype(BF16)
        ub = uf.astype(BF16)
        dd_ref[...] = jnp.sum(dyf * uf, axis=0, keepdims=True)
        dup[...] = d_ref[...] * dyf
        row8 = lax.broadcasted_iota(jnp.int32, (8, LANES), 0)
        for d in range(2):
            for qi in range(nq):
                sre[qi] = _dot_nt(dyb, cr_ref[d, qi])
                sim[qi] = -_dot_nt(dyb, ci_ref[d, qi])
                dcr_ref[d, qi] = _dot_tn(xr_ref[d, qi], dyb)
                dci_ref[d, qi] = -_dot_tn(xi_ref[d, qi], dyb)
            _scan_inplace(sre, sim, [ar_ref[d, qi] for qi in range(nq)], [-ai_ref[d, qi] for qi in range(nq)],
                          reverse=(d == 0))
            for qi in range(nq):
                gr = sre[qi]
                gi = sim[qi]
                xrf = xr_ref[d, qi].astype(F32)
                xif = xi_ref[d, qi].astype(F32)
                if d == 0:
                    g_main_r, g_main_i = gr[8:], gi[8:]
                    x_main_r, x_main_i = xrf[:body_rows], xif[:body_rows]
                    g_edge_r, g_edge_i = gr[:8], gi[:8]
                    x_edge_r = jnp.where(row8 == 0, 0.0, pltpu.roll(xrf[body_rows:], 1, axis=0))
                    x_edge_i = jnp.where(row8 == 0, 0.0, pltpu.roll(xif[body_rows:], 1, axis=0))
                else:
                    g_main_r, g_main_i = gr[:body_rows], gi[:body_rows]
                    x_main_r, x_main_i = xrf[8:], xif[8:]
                    g_edge_r, g_edge_i = gr[body_rows:], gi[body_rows:]
                    x_edge_r = jnp.where(row8 == 7, 0.0, pltpu.roll(xrf[:8], 7, axis=0))
                    x_edge_i = jnp.where(row8 == 7, 0.0, pltpu.roll(xif[:8], 7, axis=0))
                dar_ref[d, qi] = (jnp.sum(g_main_r * x_main_r + g_main_i * x_main_i, axis=0, keepdims=True)
                                  + jnp.sum(g_edge_r * x_edge_r + g_edge_i * x_edge_i, axis=0, keepdims=True))
                dai_ref[d, qi] = (jnp.sum(g_main_i * x_main_r - g_main_r * x_main_i, axis=0, keepdims=True)
                                  + jnp.sum(g_edge_i * x_edge_r - g_edge_r * x_edge_i, axis=0, keepdims=True))
                grb = gr.astype(BF16)
                gib = gi.astype(BF16)
                dup[...] += _dot_nt(grb, bbr_ref[d, qi]) + _dot_nt(gib, bbi_ref[d, qi])
                dbr_ref[d, qi] = _dot_tn(ub, grb)
                dbi_ref[d, qi] = _dot_tn(ub, gib)
        _deinterleave_rows(dup, du_ref)

    blk4 = lambda k: (0, k, 0, 0)
    col = lambda k: (0, k)
    w_spec = pl.BlockSpec((2, nq, LANES, LANES), blk4)
    a_spec = pl.BlockSpec((2, nq, 1, LANES), blk4)
    x_spec = pl.BlockSpec((2, nq, SEQ, LANES), blk4)
    w_shape = jax.ShapeDtypeStruct((2, N_LANE_BLOCKS, LANES, LANES), F32)
    a_shape = jax.ShapeDtypeStruct((2, N_LANE_BLOCKS, 1, LANES), F32)
    return pl.pallas_call(
        body, name="ssm_bwd", grid=(SSM_WIDTH // LANES,),
        in_specs=[pl.BlockSpec((SEQ, LANES), col), pl.BlockSpec((SEQ, LANES), col), x_spec, x_spec,
                  a_spec, a_spec, w_spec, w_spec, w_spec, w_spec, pl.BlockSpec((1, LANES), col)]
        + [pl.BlockSpec(memory_space=pl.ANY)] * len(deps),
        out_specs=[pl.BlockSpec((SEQ, LANES), col), pl.BlockSpec((1, LANES), col),
                   w_spec, w_spec, w_spec, w_spec, a_spec, a_spec],
        out_shape=[jax.ShapeDtypeStruct((SEQ, SSM_WIDTH), F32), jax.ShapeDtypeStruct((1, SSM_WIDTH), F32),
                   w_shape, w_shape, w_shape, w_shape, a_shape, a_shape],
        scratch_shapes=[pltpu.VMEM((nq, SEQ, LANES), F32), pltpu.VMEM((nq, SEQ, LANES), F32),
                        pltpu.VMEM((SEQ, LANES), F32), pltpu.VMEM((SEQ, LANES), F32), pltpu.VMEM((SEQ, LANES), F32)],
        compiler_params=_cparams(("parallel",)),
    )(dy, u, xr, xi, are, aim, bbr, bbi, cre, cim, dskip, *deps)


GELU_C = 0.7978845608028654
GELU_K = 0.044715


def _gelu(y):
    return 0.5 * y * (1.0 + jnp.tanh(GELU_C * (y + GELU_K * y * y * y)))


def _gelu_grad(y):
    t = jnp.tanh(GELU_C * (y + GELU_K * y * y * y))
    return 0.5 * (1.0 + t) + 0.5 * y * (1.0 - t * t) * GELU_C * (1.0 + 3.0 * GELU_K * y * y)


def _mixout_fwd(o, y, glu_w, glu_b, gan, gsn, wout, x1):
    tm = MIX_TM

    def body(o_ref, y_ref, gw_ref, gb_ref, gan_ref, gsn_ref, w_ref, x1_ref, x2_ref, mx_ref):
        yg = _gelu(y_ref[...])
        z = _dot(yg.astype(BF16), gw_ref[...]) + gb_ref[...]
        so = yg * _sigmoid(z)
        na = _rms_fwd(o_ref[...], gan_ref[...])
        ns = _rms_fwd(so, gsn_ref[...])
        mixed = jnp.concatenate([na, ns], axis=-1).astype(BF16)
        mx_ref[...] = mixed
        x2_ref[...] = x1_ref[...] + _dot(mixed, w_ref[...])

    row = lambda i: (i, 0)
    const = lambda i: (0, 0)
    return pl.pallas_call(
        body, name="mixout_fwd", grid=(SEQ // tm,),
        in_specs=[pl.BlockSpec((tm, ATTN_WIDTH), row), pl.BlockSpec((tm, SSM_WIDTH), row),
                  pl.BlockSpec((SSM_WIDTH, SSM_WIDTH), const), pl.BlockSpec((1, SSM_WIDTH), const),
                  pl.BlockSpec((1, ATTN_WIDTH), const), pl.BlockSpec((1, SSM_WIDTH), const),
                  pl.BlockSpec((D_MODEL, D_MODEL), const), pl.BlockSpec((tm, D_MODEL), row)],
        out_specs=[pl.BlockSpec((tm, D_MODEL), row), pl.BlockSpec((tm, D_MODEL), row)],
        out_shape=[jax.ShapeDtypeStruct((SEQ, D_MODEL), F32), jax.ShapeDtypeStruct((SEQ, D_MODEL), BF16)],
        compiler_params=_cparams(("parallel",)),
    )(o, y, glu_w, glu_b, gan, gsn, wout, x1)


def _mixout_bwd(dx2, o, y, glu_w, glu_b, gan, gsn, wout):
    tm = MIX_TM

    def body(dx2_ref, o_ref, y_ref, gw_ref, gb_ref, gan_ref, gsn_ref, w_ref,
             do_ref, dy_ref, dz_ref, yg_ref, dxb_ref, dgan_ref, dgsn_ref, dgb_ref):
        i = pl.program_id(0)
        dxb = dx2_ref[...].astype(BF16)
        dxb_ref[...] = dxb
        dmixed = _dot_nt(dxb, w_ref[...])
        do, dgan = _rms_bwd(dmixed[:, :ATTN_WIDTH], o_ref[...], gan_ref[...])
        do_ref[...] = do
        yv = y_ref[...]
        yg = _gelu(yv)
        ygb = yg.astype(BF16)
        yg_ref[...] = ygb
        sg = _sigmoid(_dot(ygb, gw_ref[...]) + gb_ref[...])
        dso, dgsn = _rms_bwd(dmixed[:, ATTN_WIDTH:], yg * sg, gsn_ref[...])
        dz = dso * yg * sg * (1.0 - sg)
        dzb = dz.astype(BF16)
        dz_ref[...] = dzb
        dyg = dso * sg + _dot_nt(dzb, gw_ref[...])
        dy_ref[...] = dyg * _gelu_grad(yv)
        dgb = jnp.sum(dz, axis=0, keepdims=True)

        @pl.when(i == 0)
        def _():
            dgan_ref[...] = dgan
            dgsn_ref[...] = dgsn
            dgb_ref[...] = dgb

        @pl.when(i != 0)
        def _():
            dgan_ref[...] += dgan
            dgsn_ref[...] += dgsn
            dgb_ref[...] += dgb

    row = lambda i: (i, 0)
    const = lambda i: (0, 0)
    return pl.pallas_call(
        body, name="mixout_bwd", grid=(SEQ // tm,),
        in_specs=[pl.BlockSpec((tm, D_MODEL), row), pl.BlockSpec((tm, ATTN_WIDTH), row),
                  pl.BlockSpec((tm, SSM_WIDTH), row),
                  pl.BlockSpec((SSM_WIDTH, SSM_WIDTH), const), pl.BlockSpec((1, SSM_WIDTH), const),
                  pl.BlockSpec((1, ATTN_WIDTH), const), pl.BlockSpec((1, SSM_WIDTH), const),
                  pl.BlockSpec((D_MODEL, D_MODEL), const)],
        out_specs=[pl.BlockSpec((tm, ATTN_WIDTH), row), pl.BlockSpec((tm, SSM_WIDTH), row),
                   pl.BlockSpec((tm, SSM_WIDTH), row), pl.BlockSpec((tm, SSM_WIDTH), row),
                   pl.BlockSpec((tm, D_MODEL), row),
                   pl.BlockSpec((1, ATTN_WIDTH), const), pl.BlockSpec((1, SSM_WIDTH), const),
                   pl.BlockSpec((1, SSM_WIDTH), const)],
        out_shape=[jax.ShapeDtypeStruct((SEQ, ATTN_WIDTH), F32), jax.ShapeDtypeStruct((SEQ, SSM_WIDTH), F32),
                   jax.ShapeDtypeStruct((SEQ, SSM_WIDTH), BF16), jax.ShapeDtypeStruct((SEQ, SSM_WIDTH), BF16),
                   jax.ShapeDtypeStruct((SEQ, D_MODEL), BF16),
                   jax.ShapeDtypeStruct((1, ATTN_WIDTH), F32), jax.ShapeDtypeStruct((1, SSM_WIDTH), F32),
                   jax.ShapeDtypeStruct((1, SSM_WIDTH), F32)],
        compiler_params=_cparams(("arbitrary",)),
    )(dx2, o, y, glu_w, glu_b, gan, gsn, wout)


def _loss_head(x, g, target):
    tm = MIX_TM

    def body(x_ref, g_ref, t_ref, loss_ref, dx_ref, dg_ref):
        i = pl.program_id(0)
        xv = x_ref[...]
        gv = g_ref[...]
        err = _rms_fwd(xv, gv) - t_ref[...]
        part = jnp.broadcast_to(0.5 * jnp.sum(err * err) / D_MODEL, (1, LANES))
        dx, dg = _rms_bwd(err * (1.0 / D_MODEL), xv, gv)
        dx_ref[...] = dx

        @pl.when(i == 0)
        def _():
            loss_ref[...] = part
            dg_ref[...] = dg

        @pl.when(i != 0)
        def _():
            loss_ref[...] += part
            dg_ref[...] += dg

    row = lambda i: (i, 0)
    const = lambda i: (0, 0)
    return pl.pallas_call(
        body, name="loss_head", grid=(SEQ // tm,),
        in_specs=[pl.BlockSpec((tm, D_MODEL), row), pl.BlockSpec((1, D_MODEL), const),
                  pl.BlockSpec((tm, D_MODEL), row)],
        out_specs=[pl.BlockSpec((1, LANES), const), pl.BlockSpec((tm, D_MODEL), row),
                   pl.BlockSpec((1, D_MODEL), const)],
        out_shape=[jax.ShapeDtypeStruct((1, LANES), F32), jax.ShapeDtypeStruct((SEQ, D_MODEL), F32),
                   jax.ShapeDtypeStruct((1, D_MODEL), F32)],
        compiler_params=_cparams(("arbitrary",)),
    )(x, g, target)


def _local_step(x, target, w, p, late_weights, early_grads, after=None, midway=None):
    x1, h1, a1, b1 = _ffn_fwd(x, p["norm_ffn1"], w["wgt1"], w["wut1"], w["wd1"], "ffn1_fwd", after=after)
    h2, q, k, v, u = _mixin_fwd(x1, p["norm_mix"], w["wint"])
    kp = jnp.pad(k, ((WINDOW, WINDOW), (0, 0)))
    vp = jnp.pad(v, ((WINDOW, WINDOW), (0, 0)))
    o = _attn_fwd(q, kp, vp, p["attn_sinks"])

    lam_re = p["ssm_lambda_re"].reshape(2 * N_LANE_BLOCKS, LANES)
    lam_im = p["ssm_lambda_im"].reshape(2 * N_LANE_BLOCKS, LANES)
    log_dt = jnp.repeat(p["ssm_log_dt"].reshape(2, 32), 64, axis=-1).reshape(2 * N_LANE_BLOCKS, LANES)
    a_re, a_im, bbr, bbi, cre, cim = _ssm_prep(lam_re, lam_im, log_dt, p["ssm_b_re"], p["ssm_b_im"],
                                               p["ssm_c_re"], p["ssm_c_im"])
    shape_a = (2, N_LANE_BLOCKS, 1, LANES)
    shape_w = (2, N_LANE_BLOCKS, LANES, LANES)
    a_re4, a_im4 = a_re.reshape(shape_a), a_im.reshape(shape_a)
    bbr4, bbi4 = bbr.reshape(shape_w), bbi.reshape(shape_w)
    cre, cim = cre.reshape(shape_w), cim.reshape(shape_w)
    dskip = p["ssm_d"].T.reshape(1, SSM_WIDTH)
    y, xr, xi = _ssm_fwd(u, a_re4, a_im4, bbr4, bbi4, cre, cim, dskip,
                         after=None if midway is None else midway(o))

    w2 = late_weights(y)
    x2, mixed = _mixout_fwd(o, y, w2["glu"], p["ssm_glu_b"], p["attn_out_norm"], p["ssm_out_norm"], w2["wout"], x1)
    x3, h3, a3, b3 = _ffn_fwd(x2, p["norm_ffn2"], w2["wgt2"], w2["wut2"], w2["wd2"], "ffn2_fwd")

    loss, dx3, d_final = _loss_head(x3, p["final_norm"], target)
    dx2, da3, db3, s3, df3, d_n2 = _ffn_bwd_act(dx3, x2, p["norm_ffn2"], a3, b3, w2["wgt2"], w2["wut2"], w2["wd2"],
                                                "ffn2_bwd_act")
    g_wgt2, g_wut2, g_wd2 = _mm_tn([(da3, h3), (db3, h3), (s3, df3)], "ffn2_bwd_w")

    do, dy, dz, ygb, dx2b, d_gan, d_gsn, d_glub = _mixout_bwd(
        dx2, o, y, w2["glu"], p["ssm_glu_b"], p["attn_out_norm"], p["ssm_out_norm"], w2["wout"])
    (g_wout,) = _mm_tn([(mixed, dx2b)], "wout_bwd_w")
    (g_glu,) = _mm_tn([(ygb, dz)], "glu_bwd_w")
    sent = early_grads(dict(glu=g_glu, wout=g_wout, wgt2=g_wgt2, wut2=g_wut2, wd2=g_wd2))

    du, d_dskip, dcre, dcim, dbbr, dbbi, dar, dai = _ssm_bwd(dy, u, xr, xi, a_re4, a_im4, bbr4, bbi4, cre, cim, dskip,
                                                             after=sent)
    nb = 2 * N_LANE_BLOCKS
    blocks3 = (nb, LANES, LANES)
    g_lre, g_lim, g_ldt, g_btr, g_bti, g_cre, g_cim = _ssm_prep_bwd(
        lam_re, lam_im, log_dt, p["ssm_b_re"], p["ssm_b_im"], dar.reshape(nb, LANES), dai.reshape(nb, LANES),
        dbbr.reshape(blocks3), dbbi.reshape(blocks3), dcre.reshape(blocks3), dcim.reshape(blocks3))

    dq, dkp, dvp, d_sinks = _attn_bwd(q, kp, vp, p["attn_sinks"], do)
    dk = dkp[WINDOW:WINDOW + SEQ]
    dv = dvp[WINDOW:WINDOW + SEQ]
    dx1, dproj, d_nmix = _mixin_bwd(dq, dk, dv, du, w["wint"], x1, p["norm_mix"], dx2)
    (g_wint,) = _mm_tn([(dproj, h2)], "win_bwd_w")

    dx0, da1, db1, s1, df1, d_n1 = _ffn_bwd_act(dx1, x, p["norm_ffn1"], a1, b1, w["wgt1"], w["wut1"], w["wd1"],
                                                "ffn1_bwd_act")
    g_wgt1, g_wut1, g_wd1 = _mm_tn([(da1, h1), (db1, h1), (s1, df1)], "ffn1_bwd_w")

    big = dict(wgt1=g_wgt1, wut1=g_wut1, wd1=g_wd1, wint=g_wint)
    small = dict(
        norm_ffn1=d_n1, norm_mix=d_nmix, attn_sinks=d_sinks,
        ssm_lambda_re=g_lre.reshape(64, 64), ssm_lambda_im=g_lim.reshape(64, 64),
        ssm_log_dt=g_ldt.reshape(2, 32), ssm_b_re=g_btr, ssm_b_im=g_bti, ssm_c_re=g_cre, ssm_c_im=g_cim,
        ssm_d=d_dskip.reshape(32, 16).T, ssm_glu_b=d_glub, attn_out_norm=d_gan, ssm_out_norm=d_gsn,
        norm_ffn2=d_n2, final_norm=d_final, loss=loss)
    return loss, dx0, big, small


BIG = dict(
    wgt1=("ffn1_w_gate", 352, 1024, True), wut1=("ffn1_w_up", 352, 1024, True), wd1=("ffn1_w_down", 352, 1024, False),
    wint=("w_in", 160, 1024, True), glu=("ssm_glu_w", 64, 512, False), wout=("w_out", 128, 1024, False),
    wgt2=("ffn2_w_gate", 352, 1024, True), wut2=("ffn2_w_up", 352, 1024, True), wd2=("ffn2_w_down", 352, 1024, False))

SMALL = dict(
    norm_ffn1=(1, 1024), norm_mix=(1, 1024), attn_sinks=(1, 8), ssm_lambda_re=(64, 64), ssm_lambda_im=(64, 64),
    ssm_log_dt=(2, 32), ssm_b_re=(1024, 64), ssm_b_im=(1024, 64), ssm_c_re=(1024, 64), ssm_c_im=(1024, 64),
    ssm_d=(16, 32), ssm_glu_b=(1, 512), attn_out_norm=(1, 512), ssm_out_norm=(1, 512), norm_ffn2=(1, 1024),
    final_norm=(1, 1024), loss=(1, 128))
SMALL_TRANSPOSED = ("ssm_b_re", "ssm_b_im", "ssm_d")
SMALL_PARAMS = tuple(n for n in SMALL if n != "loss")

SMALL_PAIRS = (("ssm_lambda_re", "ssm_lambda_im"), ("ssm_c_re", "ssm_c_im"), ("ssm_b_re", "ssm_b_im"))
SMALL_VECS = ("norm_ffn1", "norm_mix", "norm_ffn2", "final_norm", "ssm_glu_b", "attn_out_norm", "ssm_out_norm")
SMALL_TILES = ("ssm_log_dt", "attn_sinks", "ssm_d", "loss")


def _small_offsets():
    off, table = 0, {}
    for re, im in SMALL_PAIRS:
        table[re] = table[im] = off
        off += SMALL[re][0]
    for n in SMALL_VECS:
        table[n] = off
        off += SMALL[n][1] // LANES
    for n in SMALL_TILES:
        off = -(-off // 8) * 8
        table[n] = off
        off += SMALL[n][0]
    return table, off


SMALL_OFFSET, SMALL_USED_ROWS = _small_offsets()
SMALL_ROWS = -(-SMALL_USED_ROWS // (8 * N_DEV)) * 8 * N_DEV


def _cast_shards(shards):
    names = list(BIG)

    def body(*refs):
        ins, outs = refs[:len(names)], refs[len(names):]
        for idx in range(len(names)):
            outs[idx][...] = ins[idx][...].astype(BF16)

    return pl.pallas_call(
        body, name="cast_shards",
        out_shape=[jax.ShapeDtypeStruct((BIG[n][1], BIG[n][2]), BF16) for n in names],
        compiler_params=_cparams(),
    )(*[shards[n] for n in names])


def _peer(x, y, c, r):
    px = 1 - x if r & 4 else x
    py = 1 - y if r & 2 else y
    pc = 1 - c if r & 1 else c
    return px, py, pc


FIRST_GROUP = ("wgt1", "wut1", "wd1", "wint")
LATE_GROUP = ("glu", "wout", "wgt2", "wut2", "wd2")
N_PEERS = N_DEV - 1
ANY_SPEC = pl.BlockSpec(memory_space=pl.ANY)
HBM_SPEC = pl.BlockSpec(memory_space=pltpu.HBM)
SEM_SPEC = pl.BlockSpec(memory_space=pltpu.SEMAPHORE)
DATAFLOW_EFFECT = pltpu.SideEffectType.DATAFLOW_SIDE_EFFECTING


def _mesh_pos():
    x, y, c = lax.axis_index("x"), lax.axis_index("y"), lax.axis_index("c")
    return x, y, c, 4 * x + 2 * y + c


def _gather_first(first, late):
    nf, nl = len(first), len(late)

    def body(*refs):
        f_in, l_in = refs[:nf], refs[nf:nf + nl]
        f_out, l_out = refs[nf + nl:2 * nf + nl], refs[2 * nf + nl:2 * (nf + nl)]
        send_sems, recv_sems, local_sems = refs[2 * (nf + nl):]
        x, y, c, me = _mesh_pos()
        sibling = (x, y, 1 - c)
        chips = [(x, 1 - y), (1 - x, y), (1 - x, 1 - y)]

        def idx(px, py, pc):
            return 4 * px + 2 * py + pc

        def copy(k, s, block, to, src=None):
            slot = f_out[k].at[block]
            return pltpu.make_async_remote_copy(
                src_ref=slot if src is None else src, dst_ref=slot, send_sem=send_sems.at[k, s],
                recv_sem=recv_sems.at[k, s], device_id=to, device_id_type=MESH_ID)

        local = []
        for k in range(nf + nl):
            src, dst = (f_in[k], f_out[k]) if k < nf else (l_in[k - nf], l_out[k - nf])
            mine = pltpu.make_async_copy(src, dst.at[me], local_sems.at[k])
            mine.start()
            local.append(mine)
        sends = []
        for j, chip in enumerate(chips):
            for k in range(nf):
                sends.append(copy(k, 1 + j, me, (*chip, c), src=f_in[k]))
                sends[-1].start()
        for k in range(nf):
            sends.append(copy(k, 0, me, sibling, src=f_in[k]))
            sends[-1].start()
        for j, chip in enumerate(chips):
            for k in range(nf):
                copy(k, 1 + j, idx(*chip, c), (*chip, c)).wait_recv()
                sends.append(copy(k, 4 + j, idx(*chip, c), sibling))
                sends[-1].start()
        for k in range(nf):
            copy(k, 0, idx(*sibling), sibling).wait_recv()
        for j, chip in enumerate(chips):
            for k in range(nf):
                copy(k, 4 + j, idx(*chip, 1 - c), sibling).wait_recv()
        for cp in sends:
            cp.wait_send()
        for cp in local:
            cp.wait()

    return pl.pallas_call(
        body, name="gather_first",
        in_specs=[ANY_SPEC] * (nf + nl), out_specs=[ANY_SPEC] * (nf + nl),
        out_shape=[jax.ShapeDtypeStruct((N_DEV,) + s.shape, s.dtype) for s in list(first) + list(late)],
        scratch_shapes=[pltpu.SemaphoreType.DMA((nf, N_PEERS)), pltpu.SemaphoreType.DMA((nf, N_PEERS)),
                        pltpu.SemaphoreType.DMA((nf + nl,))],
        compiler_params=pltpu.CompilerParams(has_side_effects=True),
    )(*first, *late)


def _split_copy(src_refs, land_refs, send_sems, recv_sems, k, r, pos, scatter, receiving):
    x, y, c, me = pos
    px, py, pc = _peer(x, y, c, r)
    peer_idx = 4 * px + 2 * py + pc
    if scatter:
        src, dst = src_refs[k].at[peer_idx], land_refs[k].at[r - 1]
    else:
        src, dst = src_refs[k], land_refs[k].at[peer_idx if receiving else me]
    return pltpu.make_async_remote_copy(
        src_ref=src, dst_ref=dst, send_sem=send_sems.at[k * N_PEERS + r - 1],
        recv_sem=recv_sems.at[k * N_PEERS + r - 1], device_id=(px, py, pc), device_id_type=MESH_ID)


def _split_start(name, srcs, lands, scatter):
    n = len(srcs)

    def body(*refs):
        src_refs, land_refs = refs[:n], refs[n:2 * n]
        send_sems, recv_sems = refs[2 * n], refs[2 * n + 1]
        token = refs[-1]
        pos = _mesh_pos()
        for k in range(n):
            for r in range(1, N_DEV):
                _split_copy(src_refs, land_refs, send_sems, recv_sems, k, r, pos, scatter, False).start()
        token[...] = jnp.zeros_like(token)

    thru = [pltpu.HBM(a.shape, a.dtype) for a in list(srcs) + list(lands)]
    outs = pl.pallas_call(
        body, name=name,
        in_specs=[HBM_SPEC] * (2 * n),
        out_specs=[SEM_SPEC, SEM_SPEC] + [HBM_SPEC] * (2 * n) + [pl.BlockSpec(memory_space=pltpu.VMEM)],
        out_shape=[pltpu.SemaphoreType.DMA((n * N_PEERS,)), pltpu.SemaphoreType.DMA((n * N_PEERS,))] + thru
        + [jax.ShapeDtypeStruct((8, LANES), F32)],
        input_output_aliases={i: 2 + i for i in range(2 * n)},
        compiler_params=pltpu.CompilerParams(has_side_effects=DATAFLOW_EFFECT),
    )(*[pltpu.with_memory_space_constraint(a, pltpu.HBM) for a in list(srcs) + list(lands)])
    return outs[0], outs[1], outs[2:2 + n], outs[2 + n:2 + 2 * n], outs[-1]


def _split_wait(name, send_sems, recv_sems, srcs, lands, scatter, after):
    n = len(srcs)

    def body(*refs):
        src_refs, land_refs = refs[:n], refs[n:2 * n]
        send, recv = refs[2 * n], refs[2 * n + 1]
        pos = _mesh_pos()
        for k in range(n):
            for r in range(1, N_DEV):
                cp = _split_copy(src_refs, land_refs, send, recv, k, r, pos, scatter, True)
                cp.wait_send()
                cp.wait_recv()

    thru = [pltpu.HBM(a.shape, a.dtype) for a in list(srcs) + list(lands)]
    outs = pl.pallas_call(
        body, name=name,
        in_specs=[HBM_SPEC] * (2 * n) + [SEM_SPEC, SEM_SPEC, ANY_SPEC],
        out_specs=[HBM_SPEC] * (2 * n), out_shape=thru,
        input_output_aliases={i: i for i in range(2 * n)},
        compiler_params=pltpu.CompilerParams(has_side_effects=DATAFLOW_EFFECT),
    )(*srcs, *lands, send_sems, recv_sems, after)
    return outs[:n], outs[n:]


def _late_copy(passing, src_refs, land_refs, send_sems, recv_sems, k, s, pos, receiving):
    x, y, c, me = pos
    chips = [(x, 1 - y), (1 - x, y), (1 - x, 1 - y)]
    sibling = (x, y, 1 - c)

    def idx(dev):
        return 4 * dev[0] + 2 * dev[1] + dev[2]

    if passing:
        to = sibling
        block = idx((*chips[s], 1 - c)) if receiving else idx((*chips[s], c))
        src = dst = land_refs[k].at[block]
        sem = k * 3 + s
    else:
        to = sibling if s == 0 else (*chips[s - 1], c)
        src, dst = src_refs[k], land_refs[k].at[idx(to) if receiving else me]
        sem = k * 4 + s
    return pltpu.make_async_remote_copy(src_ref=src, dst_ref=dst, send_sem=send_sems.at[sem],
                                        recv_sem=recv_sems.at[sem], device_id=to, device_id_type=MESH_ID)


def _late_gather_call(name, stage, srcs, lands, sems, after=None):
    n = len(srcs)
    n_sem_in = len(sems)
    has_after = after is not None

    def body(*refs):
        src_refs, land_refs = refs[:n], refs[n:2 * n]
        sem_in = refs[2 * n:2 * n + n_sem_in]
        outs = refs[2 * n + n_sem_in + (1 if has_after else 0):]
        pos = _mesh_pos()
        if stage == 0:
            own_send, own_recv = outs[0], outs[1]
            for s in (1, 2, 3, 0):
                for k in range(n):
                    _late_copy(False, src_refs, land_refs, own_send, own_recv, k, s, pos, False).start()
            outs[-1][...] = jnp.zeros_like(outs[-1])
        elif stage == 1:
            own_recv = sem_in[1]
            pass_send, pass_recv = outs[0], outs[1]
            for s in range(3):
                for k in range(n):
                    _late_copy(False, src_refs, land_refs, sem_in[0], own_recv, k, s + 1, pos, True).wait_recv()
                    _late_copy(True, src_refs, land_refs, pass_send, pass_recv, k, s, pos, False).start()
            outs[-1][...] = jnp.zeros_like(outs[-1])
        else:
            own_send, own_recv, pass_send, pass_recv = sem_in
            for k in range(n):
                _late_copy(False, src_refs, land_refs, own_send, own_recv, k, 0, pos, True).wait_recv()
                for s in range(4):
                    _late_copy(False, src_refs, land_refs, own_send, own_recv, k, s, pos, False).wait_send()
                for s in range(3):
                    cp = _late_copy(True, src_refs, land_refs, pass_send, pass_recv, k, s, pos, True)
                    cp.wait_recv()
                    cp.wait_send()

    thru = [pltpu.HBM(a.shape, a.dtype) for a in list(srcs) + list(lands)]
    new_sems = [[pltpu.SemaphoreType.DMA((n * 4,))] * 2, [pltpu.SemaphoreType.DMA((n * 3,))] * 2, []][stage]
    extra = [] if stage == 2 else [jax.ShapeDtypeStruct((8, LANES), F32)]
    outs = pl.pallas_call(
        body, name=name,
        in_specs=[HBM_SPEC] * (2 * n) + [SEM_SPEC] * n_sem_in + [ANY_SPEC] * has_after,
        out_specs=[SEM_SPEC] * len(new_sems) + [HBM_SPEC] * (2 * n) + [pl.BlockSpec(memory_space=pltpu.VMEM)] * len(extra),
        out_shape=new_sems + thru + extra,
        input_output_aliases={i: len(new_sems) + i for i in range(2 * n)},
        compiler_params=pltpu.CompilerParams(has_side_effects=DATAFLOW_EFFECT),
    )(*[pltpu.with_memory_space_constraint(a, pltpu.HBM) for a in list(srcs) + list(lands)], *sems,
      *([after] if has_after else []))
    ns = len(new_sems)
    return list(outs[:ns]), outs[ns:ns + n], outs[ns + n:ns + 2 * n], (outs[-1] if extra else None)


N_SEND_SLOTS = 3


def _exchange_last(grads, small_packed):
    ng = len(grads)
    ch = SMALL_ROWS // N_DEV
    max_rows = max(g.shape[1] for g in grads)
    cols = grads[0].shape[2]

    def body(*refs):
        g_in, s_in = refs[:ng], refs[ng]
        outs = refs[ng + 1:]
        own_out, land, stage = outs[:ng], outs[ng:2 * ng], outs[2 * ng:3 * ng]
        s_red, s_stage = outs[3 * ng], outs[3 * ng + 1]
        (va, vb, vo, vs, sm_in, sm_out, d2d_send, d2d_recv, ici_send, ici_recv, s1_send, s1_recv, s2_send, s2_recv,
         local_sems) = outs[3 * ng + 2:]
        x, y, c, me = _mesh_pos()
        sibling = (x, y, 1 - c)
        chips = [(x, y), (x, 1 - y), (1 - x, y), (1 - x, 1 - y)]

        def idx(chip, core):
            return 4 * chip[0] + 2 * chip[1] + core

        def d2d(k, j):
            return pltpu.make_async_remote_copy(
                src_ref=g_in[k].at[idx(chips[j], 1 - c)], dst_ref=stage[k].at[j], send_sem=d2d_send.at[k, j],
                recv_sem=d2d_recv.at[k, j], device_id=sibling, device_id_type=MESH_ID)

        def ici(k, j, slot):
            rows = g_in[k].shape[1]
            return pltpu.make_async_remote_copy(
                src_ref=vo.at[slot, pl.ds(0, rows)], dst_ref=land[k].at[j - 1], send_sem=ici_send.at[k, j - 1],
                recv_sem=ici_recv.at[k, j - 1], device_id=(*chips[j], c), device_id_type=MESH_ID)

        def small_scatter(r):
            px, py, pc = _peer(x, y, c, r)
            return pltpu.make_async_remote_copy(
                src_ref=s_in.at[pl.ds(pl.multiple_of((4 * px + 2 * py + pc) * ch, 8), ch)], dst_ref=s_stage.at[me],
                send_sem=s1_send.at[r - 1], recv_sem=s1_recv.at[r - 1], device_id=(px, py, pc), device_id_type=MESH_ID)

        def small_gather(r):
            return pltpu.make_async_remote_copy(
                src_ref=sm_out, dst_ref=s_red.at[me], send_sem=s2_send.at[r - 1], recv_sem=s2_recv.at[r - 1],
                device_id=_peer(x, y, c, r), device_id_type=MESH_ID)

        for r in range(1, N_DEV):
            small_scatter(r).start()
        mine = pltpu.make_async_copy(s_in.at[pl.ds(pl.multiple_of(me * ch, 8), ch)], s_stage.at[me], local_sems.at[0])
        mine.start()
        pairs = [(k, j) for k in range(ng) for j in (1, 2, 3)] + [(k, 0) for k in range(ng)]
        for k, j in pairs:
            d2d(k, j).start()

        for r in range(1, N_DEV):
            small_scatter(r).wait_recv()
        mine.wait()
        load = pltpu.make_async_copy(s_stage, sm_in, local_sems.at[1])
        load.start()
        load.wait()
        total = sm_in[0]
        for i in range(1, N_DEV):
            total = total + sm_in[i]
        sm_out[...] = total
        for r in range(1, N_DEV):
            small_gather(r).start()
        keep = pltpu.make_async_copy(sm_out, s_red.at[me], local_sems.at[2])
        keep.start()

        in_flight = {}
        for i, (k, j) in enumerate(pairs):
            slot = i % N_SEND_SLOTS
            rows = g_in[k].shape[1]
            if slot in in_flight:
                in_flight.pop(slot).wait_send()
            d2d(k, j).wait_recv()
            la = pltpu.make_async_copy(g_in[k].at[idx(chips[j], c)], va.at[pl.ds(0, rows)], local_sems.at[3])
            lb = pltpu.make_async_copy(stage[k].at[j], vb.at[pl.ds(0, rows)], local_sems.at[4])
            la.start()
            lb.start()
            la.wait()
            lb.wait()
            total = va[pl.ds(0, rows)].astype(F32) + vb[pl.ds(0, rows)].astype(F32)
            if j == 0:
                vs[pl.ds(0, rows)] = total
                st = pltpu.make_async_copy(vs.at[pl.ds(0, rows)], own_out[k], local_sems.at[5])
                st.start()
                st.wait()
            else:
                vo[slot, pl.ds(0, rows)] = total.astype(BF16)
                cp = ici(k, j, slot)
                cp.start()
                in_flight[slot] = cp
        for cp in in_flight.values():
            cp.wait_send()

        for j in (1, 2, 3, 0):
            for k in range(ng):
                d2d(k, j).wait_send()
        for j in (1, 2, 3):
            for k in range(ng):
                ici(k, j, 0).wait_recv()
        for r in range(1, N_DEV):
            small_scatter(r).wait_send()
            small_gather(r).wait_send()
            small_gather(r).wait_recv()
        keep.wait()

    out_shape = [jax.ShapeDtypeStruct(g.shape[1:], F32) for g in grads]
    out_shape += [jax.ShapeDtypeStruct((3,) + g.shape[1:], BF16) for g in grads]
    out_shape += [jax.ShapeDtypeStruct((4,) + g.shape[1:], BF16) for g in grads]
    out_shape += [jax.ShapeDtypeStruct((N_DEV, ch, LANES), F32), jax.ShapeDtypeStruct((N_DEV, ch, LANES), F32)]
    outs = pl.pallas_call(
        body, name="exchange_last",
        in_specs=[ANY_SPEC] * (ng + 1), out_specs=[ANY_SPEC] * len(out_shape), out_shape=out_shape,
        scratch_shapes=[pltpu.VMEM((max_rows, cols), BF16), pltpu.VMEM((max_rows, cols), BF16),
                        pltpu.VMEM((N_SEND_SLOTS, max_rows, cols), BF16), pltpu.VMEM((max_rows, cols), F32),
                        pltpu.VMEM((N_DEV, ch, LANES), F32), pltpu.VMEM((ch, LANES), F32),
                        pltpu.SemaphoreType.DMA((ng, 4)), pltpu.SemaphoreType.DMA((ng, 4)),
                        pltpu.SemaphoreType.DMA((ng, 3)), pltpu.SemaphoreType.DMA((ng, 3)),
                        pltpu.SemaphoreType.DMA((N_PEERS,)), pltpu.SemaphoreType.DMA((N_PEERS,)),
                        pltpu.SemaphoreType.DMA((N_PEERS,)), pltpu.SemaphoreType.DMA((N_PEERS,)),
                        pltpu.SemaphoreType.DMA((6,))],
        compiler_params=pltpu.CompilerParams(has_side_effects=True, vmem_limit_bytes=VMEM_LIMIT),
    )(*grads, small_packed)
    return outs[:ng], outs[ng:2 * ng], outs[3 * ng].reshape(SMALL_ROWS, LANES)


def _adamw_math(w, g, m, v):
    m2 = ADAM_B1 * m + (1.0 - ADAM_B1) * g
    v2 = ADAM_B2 * v + (1.0 - ADAM_B2) * (g * g)
    m_hat = m2 / (1.0 - ADAM_B1 ** ADAM_STEP)
    v_hat = v2 / (1.0 - ADAM_B2 ** ADAM_STEP)
    delta = -ADAM_LR * (m_hat / (jnp.sqrt(v_hat) + ADAM_EPS) + ADAM_WD * w)
    return delta, m2, v2


ADAM_ROW_TILES = 2


def _adamw_big(own, parts, w, m, v, name):
    shape = w.shape
    own_is_blocks = own.ndim == 3
    tr = shape[0] // ADAM_ROW_TILES
    n_parts = parts.shape[0]

    def body(own_ref, p_ref, w_ref, m_ref, v_ref, g_ref, d_ref, m2_ref, v2_ref, own_s, sem):
        rows = pl.ds(pl.multiple_of(pl.program_id(0) * tr, 16), tr)
        if own_is_blocks:
            cp = pltpu.make_async_copy(own_ref.at[_mesh_pos()[3], rows], own_s, sem)
        else:
            cp = pltpu.make_async_copy(own_ref.at[rows], own_s, sem)
        cp.start()
        cp.wait()
        g = own_s[...].astype(F32)
        for i in range(parts.shape[0]):
            g = g + p_ref[i].astype(F32)
        delta, m2, v2 = _adamw_math(w_ref[...], g, m_ref[...], v_ref[...])
        g_ref[...] = g
        d_ref[...] = delta
        m2_ref[...] = m2
        v2_ref[...] = v2

    tile = pl.BlockSpec((tr, shape[1]), lambda i: (i, 0))
    return pl.pallas_call(
        body, name=name, grid=(ADAM_ROW_TILES,),
        in_specs=[ANY_SPEC, pl.BlockSpec((n_parts, tr, shape[1]), lambda i: (0, i, 0)), tile, tile, tile],
        out_specs=[tile] * 4, out_shape=[jax.ShapeDtypeStruct(shape, F32)] * 4,
        scratch_shapes=[pltpu.VMEM((tr, shape[1]), own.dtype), pltpu.SemaphoreType.DMA(())],
        compiler_params=_cparams(("arbitrary",)),
    )(own, parts, w, m, v)


def _pack_small(grads):
    names = list(SMALL)

    def body(*refs):
        ins, out = dict(zip(names, refs[:-1])), refs[-1]
        out[...] = jnp.zeros_like(out)
        for re, im in SMALL_PAIRS:
            off, rows = SMALL_OFFSET[re], SMALL[re][0]
            out[off:off + rows, :] = jnp.concatenate([ins[re][...], ins[im][...]], axis=1)
        for n in SMALL_VECS:
            off, vec = SMALL_OFFSET[n], ins[n][...]
            for i in range(SMALL[n][1] // LANES):
                out[off + i:off + i + 1, :] = vec[:, i * LANES:(i + 1) * LANES]
        for n in SMALL_TILES:
            off, (rows, cols) = SMALL_OFFSET[n], SMALL[n]
            out[off:off + rows, 0:cols] = ins[n][...]

    return pl.pallas_call(
        body, name="pack_small", out_shape=jax.ShapeDtypeStruct((SMALL_ROWS, LANES), F32),
        compiler_params=_cparams(),
    )(*[grads[n] for n in names])


def _unpack_small_ref(g_ref, n):
    off, (rows, cols) = SMALL_OFFSET[n], SMALL[n]
    for re, im in SMALL_PAIRS:
        if n == re:
            return g_ref[off:off + rows, 0:HALF_LANES]
        if n == im:
            return g_ref[off:off + rows, HALF_LANES:LANES]
    if n in SMALL_VECS:
        return jnp.concatenate([g_ref[off + i:off + i + 1, :] for i in range(cols // LANES)], axis=1)
    return g_ref[off:off + rows, 0:cols]


def _adamw_small(g_packed, w, m, v):
    names = list(SMALL_PARAMS)
    n = len(names)

    def body(g_ref, *refs):
        w_refs, m_refs, v_refs, outs = refs[:n], refs[n:2 * n], refs[2 * n:3 * n], refs[3 * n:]
        for idx, name in enumerate(names):
            g = _unpack_small_ref(g_ref, name)
            delta, m2, v2 = _adamw_math(w_refs[idx][...], g, m_refs[idx][...], v_refs[idx][...])
            outs[4 * idx][...] = g
            outs[4 * idx + 1][...] = delta
            outs[4 * idx + 2][...] = m2
            outs[4 * idx + 3][...] = v2
        outs[4 * n][...] = _unpack_small_ref(g_ref, "loss")

    outs = pl.pallas_call(
        body, name="adamw_small",
        out_shape=[jax.ShapeDtypeStruct(SMALL[name], F32) for name in names for _ in range(4)]
        + [jax.ShapeDtypeStruct(SMALL["loss"], F32)],
        compiler_params=_cparams(),
    )(g_packed, *[w[k] for k in names], *[m[k] for k in names], *[v[k] for k in names])
    return {name: outs[4 * idx:4 * idx + 4] for idx, name in enumerate(names)}, outs[4 * n]


WEIGHT_NAMES = ['norm_ffn1', 'ffn1_w_gate', 'ffn1_w_up', 'ffn1_w_down', 'norm_mix', 'w_in', 'attn_sinks',
                'ssm_lambda_re', 'ssm_lambda_im', 'ssm_log_dt', 'ssm_b_re', 'ssm_b_im', 'ssm_c_re', 'ssm_c_im',
                'ssm_d', 'ssm_glu_w', 'ssm_glu_b', 'attn_out_norm', 'ssm_out_norm', 'w_out', 'norm_ffn2',
                'ffn2_w_gate', 'ffn2_w_up', 'ffn2_w_down', 'final_norm']


def kernel(x, norm_ffn1, ffn1_w_gate, ffn1_w_up, ffn1_w_down, norm_mix, w_in, attn_sinks, ssm_lambda_re, ssm_lambda_im, ssm_log_dt, ssm_b_re, ssm_b_im, ssm_c_re, ssm_c_im, ssm_d, ssm_glu_w, ssm_glu_b, attn_out_norm, ssm_out_norm, w_out, norm_ffn2, ffn2_w_gate, ffn2_w_up, ffn2_w_down, final_norm, loss_target, m_norm_ffn1, m_ffn1_w_gate, m_ffn1_w_up, m_ffn1_w_down, m_norm_mix, m_w_in, m_attn_sinks, m_ssm_lambda_re, m_ssm_lambda_im, m_ssm_log_dt, m_ssm_b_re, m_ssm_b_im, m_ssm_c_re, m_ssm_c_im, m_ssm_d, m_ssm_glu_w, m_ssm_glu_b, m_attn_out_norm, m_ssm_out_norm, m_w_out, m_norm_ffn2, m_ffn2_w_gate, m_ffn2_w_up, m_ffn2_w_down, m_final_norm, v_norm_ffn1, v_ffn1_w_gate, v_ffn1_w_up, v_ffn1_w_down, v_norm_mix, v_w_in, v_attn_sinks, v_ssm_lambda_re, v_ssm_lambda_im, v_ssm_log_dt, v_ssm_b_re, v_ssm_b_im, v_ssm_c_re, v_ssm_c_im, v_ssm_d, v_ssm_glu_w, v_ssm_glu_b, v_attn_out_norm, v_ssm_out_norm, v_w_out, v_norm_ffn2, v_ffn2_w_gate, v_ffn2_w_up, v_ffn2_w_down, v_final_norm):
    args = dict(locals())
    weights = {n: args[n] for n in WEIGHT_NAMES}
    moms = {n: args["m_" + n] for n in WEIGHT_NAMES}
    vars_ = {n: args["v_" + n] for n in WEIGHT_NAMES}

    def shard2d(a, k):
        a = a.reshape(a.shape[-2], a.shape[-1])
        return a.T if BIG[k][3] else a

    def shard_master(a, k):
        return (a.T if BIG[k][3] else a).reshape(weights[BIG[k][0]].shape)

    def blocks(g, k):
        return g.reshape(N_DEV, BIG[k][1], BIG[k][2])

    def full(g, k):
        return g.reshape(N_DEV * BIG[k][1], BIG[k][2])

    shards = dict(zip(BIG, _cast_shards({k: shard2d(weights[BIG[k][0]], k) for k in BIG})))
    nf = len(FIRST_GROUP)
    got = _gather_first([shards[k] for k in FIRST_GROUP], [shards[k] for k in LATE_GROUP])
    w_first = {k: full(g, k) for k, g in zip(FIRST_GROUP, got[:nf])}
    late = {}
    late["own_sems"], late["srcs"], late["lands"], w_token = _late_gather_call(
        "gather_late_start", 0, [shards[k] for k in LATE_GROUP], got[nf:], [])

    def late_pass(dep):
        late["pass_sems"], late["srcs"], late["lands"], token = _late_gather_call(
            "gather_late_pass", 1, late["srcs"], late["lands"], late["own_sems"], after=dep)
        return token

    def late_weights(dep):
        _, _, lands, _ = _late_gather_call("gather_late_wait", 2, late["srcs"], late["lands"],
                                           late["own_sems"] + late["pass_sems"], after=dep)
        return {k: full(g, k) for k, g in zip(LATE_GROUP, lands)}

    early = {}

    def early_grads(g):
        srcs = [blocks(g[k], k) for k in LATE_GROUP]
        lands = [lax.empty((N_PEERS, BIG[k][1], BIG[k][2]), BF16) for k in LATE_GROUP]
        early["send"], early["recv"], early["srcs"], early["lands"], token = _split_start(
            "grads_late_start", srcs, lands, scatter=True)
        return token

    def small2d(a, n):
        if n in SMALL_TRANSPOSED:
            a = jnp.swapaxes(a, -1, -2)
        return a.reshape(SMALL[n])

    def small_master(a, n):
        if n in SMALL_TRANSPOSED:
            shape = weights[n].shape
            return jnp.swapaxes(a.reshape(shape[:-2] + (shape[-1], shape[-2])), -1, -2)
        return a.reshape(weights[n].shape)

    small_p = {n: small2d(weights[n], n) for n in SMALL_PARAMS}
    _, grad_x, g_first, g_small = _local_step(
        x.reshape(SEQ, D_MODEL), loss_target.reshape(SEQ, D_MODEL), w_first, small_p, late_weights, early_grads,
        after=w_token, midway=late_pass)

    own_sums, first_parts, small_grad = _exchange_last([blocks(g_first[k], k) for k in FIRST_GROUP],
                                                       _pack_small(g_small))
    own_late, late_parts = _split_wait("grads_late_wait", early["send"], early["recv"], early["srcs"],
                                       early["lands"], True, small_grad)
    own = dict(zip(FIRST_GROUP + LATE_GROUP, list(own_sums) + list(own_late)))
    parts = dict(zip(FIRST_GROUP + LATE_GROUP, list(first_parts) + list(late_parts)))
    outs = {}
    for k in BIG:
        n = BIG[k][0]
        outs[n] = [shard_master(o, k) for o in
                   _adamw_big(own[k], parts[k], shard2d(weights[n], k), shard2d(moms[n], k), shard2d(vars_[n], k),
                              "adamw_" + n)]
    small_out, loss_row = _adamw_small(small_grad, small_p, {n: small2d(moms[n], n) for n in SMALL_PARAMS},
                                       {n: small2d(vars_[n], n) for n in SMALL_PARAMS})
    for n in SMALL_PARAMS:
        outs[n] = [small_master(o, n) for o in small_out[n]]

    result = [loss_row[0, 0], grad_x.reshape(x.shape)]
    for i in range(4):
        result += [outs[n][i] for n in WEIGHT_NAMES]
    return tuple(result)
```

```python
import functools

import jax
import jax.numpy as jnp
from jax import lax
from jax.experimental import pallas as pl
from jax.experimental.pallas import tpu as pltpu

F32 = jnp.float32
BF16 = jnp.bfloat16

N_DEV = 8
SEQ = 2048
D_MODEL = 1024
D_FF = 2816
ATTN_HEADS = 8
KV_HEADS = 2
HEAD_DIM = 64
ATTN_WIDTH = 512
KV_WIDTH = 128
WINDOW = 128
SSM_WIDTH = 512
IN_WIDTH = 1280
EPS = 1e-6
NEG_INF = -1e30
LAMBDA_RE_MAX = -1e-4
LANES = 128
N_LANE_BLOCKS = 16
SCAN_CHUNK = SEQ // 8

ADAM_LR = 0.001
ADAM_B1 = 0.9
ADAM_B2 = 0.999
ADAM_EPS = 1e-08
ADAM_WD = 0.01
ADAM_STEP = 10

VMEM_LIMIT = 56 * 1024 * 1024
MESH_ID = pl.DeviceIdType.MESH


def _cparams(sem=None):
    return pltpu.CompilerParams(dimension_semantics=sem, vmem_limit_bytes=VMEM_LIMIT)


def _dot(a, b):
    return jnp.dot(a, b, preferred_element_type=F32)


def _dot_nt(a, b):
    return lax.dot_general(a, b, (((1,), (1,)), ((), ())), preferred_element_type=F32)


def _dot_tn(a, b):
    return lax.dot_general(a, b, (((0,), (0,)), ((), ())), preferred_element_type=F32)


def _rms_fwd(x, g):
    r = lax.rsqrt(jnp.mean(x * x, axis=-1, keepdims=True) + EPS)
    return x * r * g


def _rms_bwd(dh, x, g):
    r = lax.rsqrt(jnp.mean(x * x, axis=-1, keepdims=True) + EPS)
    xh = x * r
    dg = jnp.sum(dh * xh, axis=0, keepdims=True)
    dxh = dh * g
    dx = r * (dxh - xh * jnp.mean(dxh * xh, axis=-1, keepdims=True))
    return dx, dg


def _sigmoid(x):
    return 1.0 / (1.0 + jnp.exp(-x))


FFN_TM = 512
FFN_TF = 1408


def _ffn_fwd(x, g, wgt, wut, wd, name, after=None):
    tm, tf = FFN_TM, FFN_TF
    nj = D_FF // tf
    deps = [] if after is None else [after]

    def body(x_ref, g_ref, wg_ref, wu_ref, wd_ref, *rest):
        xo_ref, h_ref, a_ref, b_ref, h_s, acc = rest[len(deps):]
        j = pl.program_id(1)

        @pl.when(j == 0)
        def _():
            h = _rms_fwd(x_ref[...], g_ref[...]).astype(BF16)
            h_s[...] = h
            h_ref[...] = h
            acc[...] = jnp.zeros_like(acc)

        h = h_s[...]
        a = _dot_nt(h, wg_ref[...])
        b = _dot_nt(h, wu_ref[...])
        a_ref[...] = a.astype(BF16)
        b_ref[...] = b.astype(BF16)
        s = (a * _sigmoid(a) * b).astype(BF16)
        acc[...] += _dot(s, wd_ref[...])

        @pl.when(j == nj - 1)
        def _():
            xo_ref[...] = x_ref[...] + 0.5 * acc[...]

    return pl.pallas_call(
        body, name=name, grid=(SEQ // tm, nj),
        in_specs=[pl.BlockSpec((tm, D_MODEL), lambda i, j: (i, 0)),
                  pl.BlockSpec((1, D_MODEL), lambda i, j: (0, 0)),
                  pl.BlockSpec((tf, D_MODEL), lambda i, j: (j, 0)),
                  pl.BlockSpec((tf, D_MODEL), lambda i, j: (j, 0)),
                  pl.BlockSpec((tf, D_MODEL), lambda i, j: (j, 0))] + [pl.BlockSpec(memory_space=pl.ANY)] * len(deps),
        out_specs=[pl.BlockSpec((tm, D_MODEL), lambda i, j: (i, 0)),
                   pl.BlockSpec((tm, D_MODEL), lambda i, j: (i, 0)),
                   pl.BlockSpec((tm, tf), lambda i, j: (i, j)),
                   pl.BlockSpec((tm, tf), lambda i, j: (i, j))],
        out_shape=[jax.ShapeDtypeStruct((SEQ, D_MODEL), F32), jax.ShapeDtypeStruct((SEQ, D_MODEL), BF16),
                   jax.ShapeDtypeStruct((SEQ, D_FF), BF16), jax.ShapeDtypeStruct((SEQ, D_FF), BF16)],
        scratch_shapes=[pltpu.VMEM((tm, D_MODEL), BF16), pltpu.VMEM((tm, D_MODEL), F32)],
        compiler_params=_cparams(("parallel", "arbitrary")),
    )(x, g, wgt, wut, wd, *deps)


def _ffn_bwd_act(dxo, x, g, a, b, wgt, wut, wd, name):
    tm, tf = FFN_TM // 2, FFN_TF
    nj = D_FF // tf

    def body(dxo_ref, x_ref, g_ref, a_ref, b_ref, wg_ref, wu_ref, wd_ref,
             dx_ref, da_ref, db_ref, s_ref, df_ref, dg_ref, df_s, acc):
        i = pl.program_id(0)
        j = pl.program_id(1)

        @pl.when(j == 0)
        def _():
            df = (0.5 * dxo_ref[...]).astype(BF16)
            df_s[...] = df
            df_ref[...] = df
            acc[...] = jnp.zeros_like(acc)

        ds = _dot_nt(df_s[...], wd_ref[...])
        av = a_ref[...].astype(F32)
        bv = b_ref[...].astype(F32)
        sig = _sigmoid(av)
        sl = av * sig
        s_ref[...] = (sl * bv).astype(BF16)
        db = (ds * sl).astype(BF16)
        da = (ds * bv * (sig * (1.0 + av * (1.0 - sig)))).astype(BF16)
        da_ref[...] = da
        db_ref[...] = db
        acc[...] += _dot(da, wg_ref[...]) + _dot(db, wu_ref[...])

        @pl.when(j == nj - 1)
        def _():
            dx, dg = _rms_bwd(acc[...], x_ref[...], g_ref[...])
            dx_ref[...] = dxo_ref[...] + dx

            @pl.when(i == 0)
            def _():
                dg_ref[...] = dg

            @pl.when(i != 0)
            def _():
                dg_ref[...] += dg

    row = lambda i, j: (i, 0)
    col = lambda i, j: (j, 0)
    tile = lambda i, j: (i, j)
    return pl.pallas_call(
        body, name=name, grid=(SEQ // tm, nj),
        in_specs=[pl.BlockSpec((tm, D_MODEL), row), pl.BlockSpec((tm, D_MODEL), row),
                  pl.BlockSpec((1, D_MODEL), lambda i, j: (0, 0)),
                  pl.BlockSpec((tm, tf), tile), pl.BlockSpec((tm, tf), tile),
                  pl.BlockSpec((tf, D_MODEL), col), pl.BlockSpec((tf, D_MODEL), col), pl.BlockSpec((tf, D_MODEL), col)],
        out_specs=[pl.BlockSpec((tm, D_MODEL), row),
                   pl.BlockSpec((tm, tf), tile), pl.BlockSpec((tm, tf), tile), pl.BlockSpec((tm, tf), tile),
                   pl.BlockSpec((tm, D_MODEL), row),
                   pl.BlockSpec((1, D_MODEL), lambda i, j: (0, 0))],
        out_shape=[jax.ShapeDtypeStruct((SEQ, D_MODEL), F32),
                   jax.ShapeDtypeStruct((SEQ, D_FF), BF16), jax.ShapeDtypeStruct((SEQ, D_FF), BF16),
                   jax.ShapeDtypeStruct((SEQ, D_FF), BF16),
                   jax.ShapeDtypeStruct((SEQ, D_MODEL), BF16),
                   jax.ShapeDtypeStruct((1, D_MODEL), F32)],
        scratch_shapes=[pltpu.VMEM((tm, D_MODEL), BF16), pltpu.VMEM((tm, D_MODEL), F32)],
        compiler_params=_cparams(("arbitrary", "arbitrary")),
    )(dxo, x, g, a, b, wgt, wut, wd)


def _mm_tn(pairs, name, tmm=256):
    m = pairs[0][0].shape[1]
    n_pairs = len(pairs)

    def body(*refs):
        ins, outs = refs[:2 * n_pairs], refs[2 * n_pairs:]
        for p in range(n_pairs):
            outs[p][...] = _dot_tn(ins[2 * p][...], ins[2 * p + 1][...]).astype(BF16)

    in_specs, out_specs, out_shape, args = [], [], [], []
    for a, b in pairs:
        n = b.shape[1]
        in_specs += [pl.BlockSpec((SEQ, tmm), lambda i: (0, i)), pl.BlockSpec((SEQ, n), lambda i: (0, 0))]
        out_specs.append(pl.BlockSpec((tmm, n), lambda i: (i, 0)))
        out_shape.append(jax.ShapeDtypeStruct((m, n), BF16))
        args += [a, b]
    return pl.pallas_call(body, name=name, grid=(m // tmm,), in_specs=in_specs, out_specs=out_specs,
                          out_shape=out_shape, compiler_params=_cparams(("parallel",)))(*args)


MIX_TM = 256


def _mixin_fwd(x, g, wint):
    tm = MIX_TM

    def body(x_ref, g_ref, w_ref, h_ref, q_ref, k_ref, v_ref, u_ref):
        h = _rms_fwd(x_ref[...], g_ref[...]).astype(BF16)
        h_ref[...] = h
        proj = _dot_nt(h, w_ref[...])
        q_ref[...] = proj[:, :ATTN_WIDTH].T
        k_ref[...] = proj[:, ATTN_WIDTH:ATTN_WIDTH + KV_WIDTH]
        v_ref[...] = proj[:, ATTN_WIDTH + KV_WIDTH:ATTN_WIDTH + 2 * KV_WIDTH]
        u_ref[...] = proj[:, ATTN_WIDTH + 2 * KV_WIDTH:]

    row = lambda i: (i, 0)
    return pl.pallas_call(
        body, name="mixin_fwd", grid=(SEQ // tm,),
        in_specs=[pl.BlockSpec((tm, D_MODEL), row), pl.BlockSpec((1, D_MODEL), lambda i: (0, 0)),
                  pl.BlockSpec((IN_WIDTH, D_MODEL), lambda i: (0, 0))],
        out_specs=[pl.BlockSpec((tm, D_MODEL), row), pl.BlockSpec((ATTN_WIDTH, tm), lambda i: (0, i)),
                   pl.BlockSpec((tm, KV_WIDTH), row), pl.BlockSpec((tm, KV_WIDTH), row),
                   pl.BlockSpec((tm, SSM_WIDTH), row)],
        out_shape=[jax.ShapeDtypeStruct((SEQ, D_MODEL), BF16), jax.ShapeDtypeStruct((ATTN_WIDTH, SEQ), F32),
                   jax.ShapeDtypeStruct((SEQ, KV_WIDTH), F32), jax.ShapeDtypeStruct((SEQ, KV_WIDTH), F32),
                   jax.ShapeDtypeStruct((SEQ, SSM_WIDTH), F32)],
        compiler_params=_cparams(("parallel",)),
    )(x, g, wint)


def _mixin_bwd(dqt, dk, dv, du, wint, x, g, dres):
    tm = MIX_TM

    def body(dq_ref, dk_ref, dv_ref, du_ref, w_ref, x_ref, g_ref, dres_ref, dx_ref, dp_ref, dg_ref):
        i = pl.program_id(0)
        dp = jnp.concatenate([dq_ref[...].T, dk_ref[...], dv_ref[...], du_ref[...]], axis=-1).astype(BF16)
        dp_ref[...] = dp
        dh = _dot(dp, w_ref[...])
        dx, dg = _rms_bwd(dh, x_ref[...], g_ref[...])
        dx_ref[...] = dres_ref[...] + dx

        @pl.when(i == 0)
        def _():
            dg_ref[...] = dg

        @pl.when(i != 0)
        def _():
            dg_ref[...] += dg

    row = lambda i: (i, 0)
    const = lambda i: (0, 0)
    return pl.pallas_call(
        body, name="mixin_bwd", grid=(SEQ // tm,),
        in_specs=[pl.BlockSpec((ATTN_WIDTH, tm), lambda i: (0, i)), pl.BlockSpec((tm, KV_WIDTH), row),
                  pl.BlockSpec((tm, KV_WIDTH), row), pl.BlockSpec((tm, SSM_WIDTH), row),
                  pl.BlockSpec((IN_WIDTH, D_MODEL), const), pl.BlockSpec((tm, D_MODEL), row),
                  pl.BlockSpec((1, D_MODEL), const), pl.BlockSpec((tm, D_MODEL), row)],
        out_specs=[pl.BlockSpec((tm, D_MODEL), row), pl.BlockSpec((tm, IN_WIDTH), row),
                   pl.BlockSpec((1, D_MODEL), const)],
        out_shape=[jax.ShapeDtypeStruct((SEQ, D_MODEL), F32), jax.ShapeDtypeStruct((SEQ, IN_WIDTH), BF16),
                   jax.ShapeDtypeStruct((1, D_MODEL), F32)],
        compiler_params=_cparams(("arbitrary",)),
    )(dqt, dk, dv, du, wint, x, g, dres)


N_QBLOCKS = SEQ // WINDOW
GROUP = ATTN_HEADS // KV_HEADS
SCALE = HEAD_DIM ** -0.5


def _alibi_slope(h):
    return 2.0 ** (-8.0 * (h + 1) / ATTN_HEADS)


def _window_masks(n):
    s_idx = lax.broadcasted_iota(jnp.int32, (3 * WINDOW, WINDOW), 0)
    t_idx = lax.broadcasted_iota(jnp.int32, (3 * WINDOW, WINDOW), 1)
    absrel = jnp.abs(s_idx - WINDOW - t_idx)
    key_pos = n * WINDOW - WINDOW + s_idx
    valid = (absrel <= WINDOW) & (key_pos >= 0) & (key_pos < SEQ)
    return absrel.astype(F32), valid


def _group_cols(ref, r0, gi):
    return jnp.concatenate(
        [ref[(gi * GROUP + hh) * HEAD_DIM:(gi * GROUP + hh + 1) * HEAD_DIM, pl.ds(r0, WINDOW)].astype(BF16)
         for hh in range(GROUP)], axis=1)


def _group_probs(qgt, kw, absrel, valid, gi, sk_ref):
    bias = jnp.concatenate([jnp.where(valid, -_alibi_slope(gi * GROUP + hh) * absrel, NEG_INF)
                            for hh in range(GROUP)], axis=1)
    sink = jnp.concatenate([jnp.full((1, WINDOW), sk_ref[0, gi * GROUP + hh], F32) for hh in range(GROUP)], axis=1)
    s = _dot(kw, qgt) * SCALE + bias
    m = jnp.maximum(jnp.max(s, axis=0, keepdims=True), sink)
    p = jnp.exp(s - m)
    ps = jnp.exp(sink - m)
    inv = 1.0 / (jnp.sum(p, axis=0, keepdims=True) + ps)
    return p * inv, ps * inv


def _attn_fwd(qt, kp, vp, sinks):
    def body(sk_ref, qt_ref, kp_ref, vp_ref, o_ref):
        def blk(n, carry):
            r0 = pl.multiple_of(n * WINDOW, WINDOW)
            absrel, valid = _window_masks(n)
            for gi in range(KV_HEADS):
                kw = kp_ref[pl.ds(r0, 3 * WINDOW), gi * HEAD_DIM:(gi + 1) * HEAD_DIM].astype(BF16)
                vw = vp_ref[pl.ds(r0, 3 * WINDOW), gi * HEAD_DIM:(gi + 1) * HEAD_DIM].astype(BF16)
                pr, _ = _group_probs(_group_cols(qt_ref, r0, gi), kw, absrel, valid, gi, sk_ref)
                og = _dot_tn(pr.astype(BF16), vw)
                for hh in range(GROUP):
                    h = gi * GROUP + hh
                    o_ref[pl.ds(r0, WINDOW), h * HEAD_DIM:(h + 1) * HEAD_DIM] = og[hh * WINDOW:(hh + 1) * WINDOW]
            return carry

        lax.fori_loop(0, N_QBLOCKS, blk, 0)

    vmem = pl.BlockSpec(memory_space=pltpu.VMEM)
    return pl.pallas_call(
        body, name="attn_fwd",
        in_specs=[pl.BlockSpec(memory_space=pltpu.SMEM), vmem, vmem, vmem], out_specs=vmem,
        out_shape=jax.ShapeDtypeStruct((SEQ, ATTN_WIDTH), F32),
        compiler_params=_cparams(),
    )(sinks, qt, kp, vp)


def _attn_bwd(qt, kp, vp, sinks, dot_):
    def body(sk_ref, qt_ref, kp_ref, vp_ref, dot_ref, dqt_ref, dkp_ref, dvp_ref, dsk_ref, dsk_acc):
        dkp_ref[...] = jnp.zeros_like(dkp_ref)
        dvp_ref[...] = jnp.zeros_like(dvp_ref)
        dsk_acc[...] = jnp.zeros_like(dsk_acc)

        def blk(n, carry):
            r0 = pl.multiple_of(n * WINDOW, WINDOW)
            absrel, valid = _window_masks(n)
            for gi in range(KV_HEADS):
                gcols = slice(gi * HEAD_DIM, (gi + 1) * HEAD_DIM)
                kw = kp_ref[pl.ds(r0, 3 * WINDOW), gcols].astype(BF16)
                vw = vp_ref[pl.ds(r0, 3 * WINDOW), gcols].astype(BF16)
                qgt = _group_cols(qt_ref, r0, gi)
                dogt = _group_cols(dot_ref, r0, gi)
                pr, psink = _group_probs(qgt, kw, absrel, valid, gi, sk_ref)
                dp = _dot(vw, dogt)
                delta = jnp.sum(pr * dp, axis=0, keepdims=True)
                ds = (pr * (dp - delta)).astype(BF16)
                dsk_acc[gi:gi + 1, :] += -(psink * delta)
                dqgt = _dot_tn(kw, ds) * SCALE
                for hh in range(GROUP):
                    h = gi * GROUP + hh
                    dqt_ref[h * HEAD_DIM:(h + 1) * HEAD_DIM, pl.ds(r0, WINDOW)] = dqgt[:, hh * WINDOW:(hh + 1) * WINDOW]
                dkp_ref[pl.ds(r0, 3 * WINDOW), gcols] += _dot_nt(ds, qgt) * SCALE
                dvp_ref[pl.ds(r0, 3 * WINDOW), gcols] += _dot_nt(pr.astype(BF16), dogt)
            return carry

        lax.fori_loop(0, N_QBLOCKS, blk, 0)
        for h in range(ATTN_HEADS):
            gi, hh = divmod(h, GROUP)
            dsk_ref[:, h:h + 1] = jnp.sum(dsk_acc[gi:gi + 1, hh * WINDOW:(hh + 1) * WINDOW], axis=1, keepdims=True)

    vmem = pl.BlockSpec(memory_space=pltpu.VMEM)
    return pl.pallas_call(
        body, name="attn_bwd",
        in_specs=[pl.BlockSpec(memory_space=pltpu.SMEM), vmem, vmem, vmem, vmem],
        out_specs=[vmem, vmem, vmem, vmem],
        out_shape=[jax.ShapeDtypeStruct((ATTN_WIDTH, SEQ), F32),
                   jax.ShapeDtypeStruct((SEQ + 2 * WINDOW, KV_WIDTH), F32),
                   jax.ShapeDtypeStruct((SEQ + 2 * WINDOW, KV_WIDTH), F32),
                   jax.ShapeDtypeStruct((1, ATTN_HEADS), F32)],
        scratch_shapes=[pltpu.VMEM((KV_HEADS, GROUP * WINDOW), F32)],
        compiler_params=_cparams(),
    )(sinks, qt, kp, vp, dot_)


HALF_LANES = LANES // 2
BLOCK_ROWS = 32


def _embed_block(bt, q):
    z = jnp.zeros((16, HALF_LANES), bt.dtype)
    blk = jnp.concatenate([jnp.concatenate([bt[:16], z], axis=1), jnp.concatenate([z, bt[16:]], axis=1)], axis=0)
    parts = [jnp.zeros((BLOCK_ROWS * q, LANES), bt.dtype)] if q else []
    parts.append(blk)
    if q < 3:
        parts.append(jnp.zeros((BLOCK_ROWS * (3 - q), LANES), bt.dtype))
    return jnp.concatenate(parts, axis=0)


def _extract_block(m, q):
    blk = m[BLOCK_ROWS * q:BLOCK_ROWS * (q + 1)]
    return jnp.concatenate([blk[:16, :HALF_LANES], blk[16:, HALF_LANES:]], axis=0)


def _ssm_prep(lam_re, lam_im, log_dt, bt_re, bt_im, c_re, c_im):
    nb = 2 * N_LANE_BLOCKS

    def body(lr_ref, li_ref, ldt_ref, btr_ref, bti_ref, ctr_ref, cti_ref,
             ar_ref, ai_ref, bbr_ref, bbi_ref, cpr_ref, cpi_ref):
        lr = jnp.minimum(lr_ref[...], LAMBDA_RE_MAX)
        li = li_ref[...]
        dt = jnp.exp(ldt_ref[...])
        mag = jnp.exp(lr * dt)
        ar = mag * jnp.cos(li * dt)
        ai = mag * jnp.sin(li * dt)
        den = lr * lr + li * li
        cr = ((ar - 1.0) * lr + ai * li) / den
        ci = (ai * lr - (ar - 1.0) * li) / den
        ar_ref[...] = ar
        ai_ref[...] = ai
        for i in range(nb):
            q = i % 4
            rows = slice(BLOCK_ROWS * i, BLOCK_ROWS * (i + 1))
            br = _embed_block(btr_ref[rows, :], q)
            bi = _embed_block(bti_ref[rows, :], q)
            cri, cii = cr[i:i + 1, :], ci[i:i + 1, :]
            bbr_ref[i] = (cri * br - cii * bi).astype(BF16)
            bbi_ref[i] = (cri * bi + cii * br).astype(BF16)
            cpr_ref[i] = _embed_block(ctr_ref[rows, :], q).T.astype(BF16)
            cpi_ref[i] = _embed_block(cti_ref[rows, :], q).T.astype(BF16)

    w_shape = jax.ShapeDtypeStruct((nb, LANES, LANES), BF16)
    return pl.pallas_call(
        body, name="ssm_prep",
        out_shape=[jax.ShapeDtypeStruct((nb, LANES), F32), jax.ShapeDtypeStruct((nb, LANES), F32),
                   w_shape, w_shape, w_shape, w_shape],
        compiler_params=_cparams(),
    )(lam_re, lam_im, log_dt, bt_re, bt_im, c_re, c_im)


def _ssm_prep_bwd(lam_re, lam_im, log_dt, bt_re, bt_im, dar, dai, dbbr, dbbi, dcr, dci):
    nb = 2 * N_LANE_BLOCKS

    def body(lr_ref, li_ref, ldt_ref, btr_ref, bti_ref, dar_ref, dai_ref, dbbr_ref, dbbi_ref, dcr_ref, dci_ref,
             glr_ref, gli_ref, gdt_ref, gbr_ref, gbi_ref, gcre_ref, gcim_ref, gcr_s, gci_s):
        lam = lr_ref[...]
        lr = jnp.minimum(lam, LAMBDA_RE_MAX)
        li = li_ref[...]
        dt = jnp.exp(ldt_ref[...])
        mag = jnp.exp(lr * dt)
        cs = jnp.cos(li * dt)
        sn = jnp.sin(li * dt)
        ar = mag * cs
        ai = mag * sn
        den = lr * lr + li * li
        nr = (ar - 1.0) * lr + ai * li
        ni = ai * lr - (ar - 1.0) * li
        cr = nr / den
        ci = ni / den
        for i in range(nb):
            q = i % 4
            rows = slice(BLOCK_ROWS * i, BLOCK_ROWS * (i + 1))
            br = _embed_block(btr_ref[rows, :], q)
            bi = _embed_block(bti_ref[rows, :], q)
            gbbr = dbbr_ref[i]
            gbbi = dbbi_ref[i]
            cri, cii = cr[i:i + 1, :], ci[i:i + 1, :]
            gcr_s[i:i + 1, :] = jnp.sum(gbbr * br + gbbi * bi, axis=0, keepdims=True)
            gci_s[i:i + 1, :] = jnp.sum(gbbi * br - gbbr * bi, axis=0, keepdims=True)
            gbr_ref[rows, :] = _extract_block(cri * gbbr + cii * gbbi, q)
            gbi_ref[rows, :] = _extract_block(cri * gbbi - cii * gbbr, q)
            gcre_ref[rows, :] = _extract_block(dcr_ref[i].T, q)
            gcim_ref[rows, :] = _extract_block(dci_ref[i].T, q)
        g_cr = gcr_s[...]
        g_ci = gci_s[...]
        g_nr = g_cr / den
        g_ni = g_ci / den
        g_den = -(g_cr * nr + g_ci * ni) / (den * den)
        g_ar = dar_ref[...] + g_nr * lr - g_ni * li
        g_ai = dai_ref[...] + g_nr * li + g_ni * lr
        g_lr = g_nr * (ar - 1.0) + g_ni * ai + g_den * 2.0 * lr
        g_li = g_nr * ai - g_ni * (ar - 1.0) + g_den * 2.0 * li
        g_mag = g_ar * cs + g_ai * sn
        g_th = (g_ai * cs - g_ar * sn) * mag
        g_lr = g_lr + g_mag * mag * dt
        g_li = g_li + g_th * dt
        g_dt = g_mag * mag * lr + g_th * li
        glr_ref[...] = jnp.where(lam < LAMBDA_RE_MAX, g_lr, 0.0)
        gli_ref[...] = g_li
        gl = g_dt * dt
        half = LANES // 2
        gdt_ref[:, 0:1] = jnp.sum(gl[:, :half], axis=1, keepdims=True)
        gdt_ref[:, 1:2] = jnp.sum(gl[:, half:], axis=1, keepdims=True)

    rows_shape = jax.ShapeDtypeStruct((nb * BLOCK_ROWS, HALF_LANES), F32)
    return pl.pallas_call(
        body, name="ssm_prep_bwd",
        out_shape=[jax.ShapeDtypeStruct((nb, LANES), F32), jax.ShapeDtypeStruct((nb, LANES), F32),
                   jax.ShapeDtypeStruct((nb, 2), F32), rows_shape, rows_shape, rows_shape, rows_shape],
        scratch_shapes=[pltpu.VMEM((nb, LANES), F32), pltpu.VMEM((nb, LANES), F32)],
        compiler_params=_cparams(),
    )(lam_re, lam_im, log_dt, bt_re, bt_im, dar, dai, dbbr, dbbi, dcr, dci)


def _cmul(ar, ai, br, bi):
    return ar * br - ai * bi, ar * bi + ai * br


def _interleave_rows(src_ref, dst_ref):
    def step(j, carry):
        dst_ref[pl.ds(pl.multiple_of(j * 8, 8), 8), :] = src_ref[pl.ds(j, 8, stride=SCAN_CHUNK), :]
        return carry
    lax.fori_loop(0, SCAN_CHUNK, step, 0, unroll=4)


def _deinterleave_rows(src_ref, dst_ref):
    def step(j, carry):
        dst_ref[pl.ds(j, 8, stride=SCAN_CHUNK), :] = src_ref[pl.ds(pl.multiple_of(j * 8, 8), 8), :]
        return carry
    lax.fori_loop(0, SCAN_CHUNK, step, 0, unroll=4)


def _scan_inplace(re_ref, im_ref, a_re, a_im, reverse):
    nq = len(a_re)
    ch = SCAN_CHUNK
    ab_re = [jnp.broadcast_to(a, (8, LANES)) for a in a_re]
    ab_im = [jnp.broadcast_to(a, (8, LANES)) for a in a_im]

    def rows(j):
        jj = (ch - 1 - j) if reverse else j
        return pl.ds(pl.multiple_of(jj * 8, 8), 8)

    def sweep(init, store):
        def step(j, st):
            out = []
            r = rows(j)
            for qi in range(nq):
                xr, xi = st[2 * qi], st[2 * qi + 1]
                pr, pi = _cmul(ab_re[qi], ab_im[qi], xr, xi)
                xr = pr + re_ref[qi, r, :]
                xi = pi + im_ref[qi, r, :]
                if store:
                    re_ref[qi, r, :] = xr
                    im_ref[qi, r, :] = xi
                out += [xr, xi]
            return tuple(out)
        return lax.fori_loop(0, ch, step, tuple(init), unroll=2)

    zeros = [jnp.zeros((8, LANES), F32)] * (2 * nq)
    finals = sweep(zeros, store=False)

    row_id = lax.broadcasted_iota(jnp.int32, (8, LANES), 0)
    carries = []
    for qi in range(nq):
        pr, pi = ab_re[qi], ab_im[qi]
        for _ in range(8):
            pr, pi = _cmul(pr, pi, pr, pi)
        fr, fi = finals[2 * qi], finals[2 * qi + 1]
        sr = jnp.zeros((8, LANES), F32)
        si = jnp.zeros((8, LANES), F32)
        for _ in range(7):
            tr, ti = _cmul(pr, pi, sr, si)
            tr, ti = tr + fr, ti + fi
            if reverse:
                sr = jnp.where(row_id == 7, 0.0, pltpu.roll(tr, 7, axis=0))
                si = jnp.where(row_id == 7, 0.0, pltpu.roll(ti, 7, axis=0))
            else:
                sr = jnp.where(row_id == 0, 0.0, pltpu.roll(tr, 1, axis=0))
                si = jnp.where(row_id == 0, 0.0, pltpu.roll(ti, 1, axis=0))
        carries += [sr, si]
    sweep(carries, store=True)


SSM_Q = 4


def _ssm_fwd(u, are, aim, bbr, bbi, cre, cim, dskip, after=None):
    nq = SSM_Q
    deps = [] if after is None else [after]

    def body(u_ref, ar_ref, ai_ref, bbr_ref, bbi_ref, cr_ref, ci_ref, d_ref, *rest):
        y_ref, xr_ref, xi_ref, sre, sim, up, yp = rest[len(deps):]
        _interleave_rows(u_ref, up)
        uf = up[...]
        ub = uf.astype(BF16)
        yp[...] = d_ref[...] * uf
        for d in range(2):
            for qi in range(nq):
                sre[qi] = _dot(ub, bbr_ref[d, qi])
                sim[qi] = _dot(ub, bbi_ref[d, qi])
            _scan_inplace(sre, sim, [ar_ref[d, qi] for qi in range(nq)], [ai_ref[d, qi] for qi in range(nq)],
                          reverse=(d == 1))
            for qi in range(nq):
                xrb = sre[qi].astype(BF16)
                xib = sim[qi].astype(BF16)
                xr_ref[d, qi] = xrb
                xi_ref[d, qi] = xib
                yp[...] += _dot(xrb, cr_ref[d, qi]) - _dot(xib, ci_ref[d, qi])
        _deinterleave_rows(yp, y_ref)

    blk4 = lambda k: (0, k, 0, 0)
    return pl.pallas_call(
        body, name="ssm_fwd", grid=(SSM_WIDTH // LANES,),
        in_specs=[pl.BlockSpec((SEQ, LANES), lambda k: (0, k)),
                  pl.BlockSpec((2, nq, 1, LANES), blk4), pl.BlockSpec((2, nq, 1, LANES), blk4),
                  pl.BlockSpec((2, nq, LANES, LANES), blk4), pl.BlockSpec((2, nq, LANES, LANES), blk4),
                  pl.BlockSpec((2, nq, LANES, LANES), blk4), pl.BlockSpec((2, nq, LANES, LANES), blk4),
                  pl.BlockSpec((1, LANES), lambda k: (0, k))] + [pl.BlockSpec(memory_space=pl.ANY)] * len(deps),
        out_specs=[pl.BlockSpec((SEQ, LANES), lambda k: (0, k)),
                   pl.BlockSpec((2, nq, SEQ, LANES), blk4), pl.BlockSpec((2, nq, SEQ, LANES), blk4)],
        out_shape=[jax.ShapeDtypeStruct((SEQ, SSM_WIDTH), F32),
                   jax.ShapeDtypeStruct((2, N_LANE_BLOCKS, SEQ, LANES), BF16),
                   jax.ShapeDtypeStruct((2, N_LANE_BLOCKS, SEQ, LANES), BF16)],
        scratch_shapes=[pltpu.VMEM((nq, SEQ, LANES), F32), pltpu.VMEM((nq, SEQ, LANES), F32),
                        pltpu.VMEM((SEQ, LANES), F32), pltpu.VMEM((SEQ, LANES), F32)],
        compiler_params=_cparams(("parallel",)),
    )(u, are, aim, bbr, bbi, cre, cim, dskip, *deps)


def _ssm_bwd(dy, u, xr, xi, are, aim, bbr, bbi, cre, cim, dskip, after=None):
    nq = SSM_Q
    body_rows = SEQ - 8
    deps = [] if after is None else [after]

    def body(dy_ref, u_ref, xr_ref, xi_ref, ar_ref, ai_ref, bbr_ref, bbi_ref, cr_ref, ci_ref, d_ref, *rest):
        (du_ref, dd_ref, dcr_ref, dci_ref, dbr_ref, dbi_ref, dar_ref, dai_ref,
         sre, sim, up, dyp, dup) = rest[len(deps):]
        _interleave_rows(u_ref, up)
        _interleave_rows(dy_ref, dyp)
        dyf = dyp[...]
        uf = up[...]
        dyb = dyf.astype(BF16)
        ub = uf.astype(BF16)
        dd_ref[...] = jnp.sum(dyf * uf, axis=0, keepdims=True)
        dup[...] = d_ref[...] * dyf
        row8 = lax.broadcasted_iota(jnp.int32, (8, LANES), 0)
        for d in range(2):
            for qi in range(nq):
                sre[qi] = _dot_nt(dyb, cr_ref[d, qi])
                sim[qi] = -_dot_nt(dyb, ci_ref[d, qi])
                dcr_ref[d, qi] = _dot_tn(xr_ref[d, qi], dyb)
                dci_ref[d, qi] = -_dot_tn(xi_ref[d, qi], dyb)
            _scan_inplace(sre, sim, [ar_ref[d, qi] for qi in range(nq)], [-ai_ref[d, qi] for qi in range(nq)],
                          reverse=(d == 0))
            for qi in range(nq):
                gr = sre[qi]
                gi = sim[qi]
                xrf = xr_ref[d, qi].astype(F32)
                xif = xi_ref[d, qi].astype(F32)
                if d == 0:
                    g_main_r, g_main_i = gr[8:], gi[8:]
                    x_main_r, x_main_i = xrf[:body_rows], xif[:body_rows]
                    g_edge_r, g_edge_i = gr[:8], gi[:8]
                    x_edge_r = jnp.where(row8 == 0, 0.0, pltpu.roll(xrf[body_rows:], 1, axis=0))
                    x_edge_i = jnp.where(row8 == 0, 0.0, pltpu.roll(xif[body_rows:], 1, axis=0))
                else:
                    g_main_r, g_main_i = gr[:body_rows], gi[:body_rows]
                    x_main_r, x_main_i = xrf[8:], xif[8:]
                    g_edge_r, g_edge_i = gr[body_rows:], gi[body_rows:]
                    x_edge_r = jnp.where(row8 == 7, 0.0, pltpu.roll(xrf[:8], 7, axis=0))
                    x_edge_i = jnp.where(row8 == 7, 0.0, pltpu.roll(xif[:8], 7, axis=0))
                dar_ref[d, qi] = (jnp.sum(g_main_r * x_main_r + g_main_i * x_main_i, axis=0, keepdims=True)
                                  + jnp.sum(g_edge_r * x_edge_r + g_edge_i * x_edge_i, axis=0, keepdims=True))
                dai_ref[d, qi] = (jnp.sum(g_main_i * x_main_r - g_main_r * x_main_i, axis=0, keepdims=True)
                                  + jnp.sum(g_edge_i * x_edge_r - g_edge_r * x_edge_i, axis=0, keepdims=True))
                grb = gr.astype(BF16)
                gib = gi.astype(BF16)
                dup[...] += _dot_nt(grb, bbr_ref[d, qi]) + _dot_nt(gib, bbi_ref[d, qi])
                dbr_ref[d, qi] = _dot_tn(ub, grb)
                dbi_ref[d, qi] = _dot_tn(ub, gib)
        _deinterleave_rows(dup, du_ref)

    blk4 = lambda k: (0, k, 0, 0)
    col = lambda k: (0, k)
    w_spec = pl.BlockSpec((2, nq, LANES, LANES), blk4)
    a_spec = pl.BlockSpec((2, nq, 1, LANES), blk4)
    x_spec = pl.BlockSpec((2, nq, SEQ, LANES), blk4)
    w_shape = jax.ShapeDtypeStruct((2, N_LANE_BLOCKS, LANES, LANES), F32)
    a_shape = jax.ShapeDtypeStruct((2, N_LANE_BLOCKS, 1, LANES), F32)
    return pl.pallas_call(
        body, name="ssm_bwd", grid=(SSM_WIDTH // LANES,),
        in_specs=[pl.BlockSpec((SEQ, LANES), col), pl.BlockSpec((SEQ, LANES), col), x_spec, x_spec,
                  a_spec, a_spec, w_spec, w_spec, w_spec, w_spec, pl.BlockSpec((1, LANES), col)]
        + [pl.BlockSpec(memory_space=pl.ANY)] * len(deps),
        out_specs=[pl.BlockSpec((SEQ, LANES), col), pl.BlockSpec((1, LANES), col),
                   w_spec, w_spec, w_spec, w_spec, a_spec, a_spec],
        out_shape=[jax.ShapeDtypeStruct((SEQ, SSM_WIDTH), F32), jax.ShapeDtypeStruct((1, SSM_WIDTH), F32),
                   w_shape, w_shape, w_shape, w_shape, a_shape, a_shape],
        scratch_shapes=[pltpu.VMEM((nq, SEQ, LANES), F32), pltpu.VMEM((nq, SEQ, LANES), F32),
                        pltpu.VMEM((SEQ, LANES), F32), pltpu.VMEM((SEQ, LANES), F32), pltpu.VMEM((SEQ, LANES), F32)],
        compiler_params=_cparams(("parallel",)),
    )(dy, u, xr, xi, are, aim, bbr, bbi, cre, cim, dskip, *deps)


GELU_C = 0.7978845608028654
GELU_K = 0.044715


def _gelu(y):
    return 0.5 * y * (1.0 + jnp.tanh(GELU_C * (y + GELU_K * y * y * y)))


def _gelu_grad(y):
    t = jnp.tanh(GELU_C * (y + GELU_K * y * y * y))
    return 0.5 * (1.0 + t) + 0.5 * y * (1.0 - t * t) * GELU_C * (1.0 + 3.0 * GELU_K * y * y)


def _mixout_fwd(o, y, glu_w, glu_b, gan, gsn, wout, x1):
    tm = MIX_TM

    def body(o_ref, y_ref, gw_ref, gb_ref, gan_ref, gsn_ref, w_ref, x1_ref, x2_ref, mx_ref):
        yg = _gelu(y_ref[...])
        z = _dot(yg.astype(BF16), gw_ref[...]) + gb_ref[...]
        so = yg * _sigmoid(z)
        na = _rms_fwd(o_ref[...], gan_ref[...])
        ns = _rms_fwd(so, gsn_ref[...])
        mixed = jnp.concatenate([na, ns], axis=-1).astype(BF16)
        mx_ref[...] = mixed
        x2_ref[...] = x1_ref[...] + _dot(mixed, w_ref[...])

    row = lambda i: (i, 0)
    const = lambda i: (0, 0)
    return pl.pallas_call(
        body, name="mixout_fwd", grid=(SEQ // tm,),
        in_specs=[pl.BlockSpec((tm, ATTN_WIDTH), row), pl.BlockSpec((tm, SSM_WIDTH), row),
                  pl.BlockSpec((SSM_WIDTH, SSM_WIDTH), const), pl.BlockSpec((1, SSM_WIDTH), const),
                  pl.BlockSpec((1, ATTN_WIDTH), const), pl.BlockSpec((1, SSM_WIDTH), const),
                  pl.BlockSpec((D_MODEL, D_MODEL), const), pl.BlockSpec((tm, D_MODEL), row)],
        out_specs=[pl.BlockSpec((tm, D_MODEL), row), pl.BlockSpec((tm, D_MODEL), row)],
        out_shape=[jax.ShapeDtypeStruct((SEQ, D_MODEL), F32), jax.ShapeDtypeStruct((SEQ, D_MODEL), BF16)],
        compiler_params=_cparams(("parallel",)),
    )(o, y, glu_w, glu_b, gan, gsn, wout, x1)


def _mixout_bwd(dx2, o, y, glu_w, glu_b, gan, gsn, wout):
    tm = MIX_TM

    def body(dx2_ref, o_ref, y_ref, gw_ref, gb_ref, gan_ref, gsn_ref, w_ref,
             do_ref, dy_ref, dz_ref, yg_ref, dxb_ref, dgan_ref, dgsn_ref, dgb_ref):
        i = pl.program_id(0)
        dxb = dx2_ref[...].astype(BF16)
        dxb_ref[...] = dxb
        dmixed = _dot_nt(dxb, w_ref[...])
        do, dgan = _rms_bwd(dmixed[:, :ATTN_WIDTH], o_ref[...], gan_ref[...])
        do_ref[...] = do.T
        yv = y_ref[...]
        yg = _gelu(yv)
        ygb = yg.astype(BF16)
        yg_ref[...] = ygb
        sg = _sigmoid(_dot(ygb, gw_ref[...]) + gb_ref[...])
        dso, dgsn = _rms_bwd(dmixed[:, ATTN_WIDTH:], yg * sg, gsn_ref[...])
        dz = dso * yg * sg * (1.0 - sg)
        dzb = dz.astype(BF16)
        dz_ref[...] = dzb
        dyg = dso * sg + _dot_nt(dzb, gw_ref[...])
        dy_ref[...] = dyg * _gelu_grad(yv)
        dgb = jnp.sum(dz, axis=0, keepdims=True)

        @pl.when(i == 0)
        def _():
            dgan_ref[...] = dgan
            dgsn_ref[...] = dgsn
            dgb_ref[...] = dgb

        @pl.when(i != 0)
        def _():
            dgan_ref[...] += dgan
            dgsn_ref[...] += dgsn
            dgb_ref[...] += dgb

    row = lambda i: (i, 0)
    const = lambda i: (0, 0)
    return pl.pallas_call(
        body, name="mixout_bwd", grid=(SEQ // tm,),
        in_specs=[pl.BlockSpec((tm, D_MODEL), row), pl.BlockSpec((tm, ATTN_WIDTH), row),
                  pl.BlockSpec((tm, SSM_WIDTH), row),
                  pl.BlockSpec((SSM_WIDTH, SSM_WIDTH), const), pl.BlockSpec((1, SSM_WIDTH), const),
                  pl.BlockSpec((1, ATTN_WIDTH), const), pl.BlockSpec((1, SSM_WIDTH), const),
                  pl.BlockSpec((D_MODEL, D_MODEL), const)],
        out_specs=[pl.BlockSpec((ATTN_WIDTH, tm), lambda i: (0, i)), pl.BlockSpec((tm, SSM_WIDTH), row),
                   pl.BlockSpec((tm, SSM_WIDTH), row), pl.BlockSpec((tm, SSM_WIDTH), row),
                   pl.BlockSpec((tm, D_MODEL), row),
                   pl.BlockSpec((1, ATTN_WIDTH), const), pl.BlockSpec((1, SSM_WIDTH), const),
                   pl.BlockSpec((1, SSM_WIDTH), const)],
        out_shape=[jax.ShapeDtypeStruct((ATTN_WIDTH, SEQ), F32), jax.ShapeDtypeStruct((SEQ, SSM_WIDTH), F32),
                   jax.ShapeDtypeStruct((SEQ, SSM_WIDTH), BF16), jax.ShapeDtypeStruct((SEQ, SSM_WIDTH), BF16),
                   jax.ShapeDtypeStruct((SEQ, D_MODEL), BF16),
                   jax.ShapeDtypeStruct((1, ATTN_WIDTH), F32), jax.ShapeDtypeStruct((1, SSM_WIDTH), F32),
                   jax.ShapeDtypeStruct((1, SSM_WIDTH), F32)],
        compiler_params=_cparams(("arbitrary",)),
    )(dx2, o, y, glu_w, glu_b, gan, gsn, wout)


def _loss_head(x, g, target):
    tm = MIX_TM

    def body(x_ref, g_ref, t_ref, loss_ref, dx_ref, dg_ref):
        i = pl.program_id(0)
        xv = x_ref[...]
        gv = g_ref[...]
        err = _rms_fwd(xv, gv) - t_ref[...]
        part = jnp.broadcast_to(0.5 * jnp.sum(err * err) / D_MODEL, (1, LANES))
        dx, dg = _rms_bwd(err * (1.0 / D_MODEL), xv, gv)
        dx_ref[...] = dx

        @pl.when(i == 0)
        def _():
            loss_ref[...] = part
            dg_ref[...] = dg

        @pl.when(i != 0)
        def _():
            loss_ref[...] += part
            dg_ref[...] += dg

    row = lambda i: (i, 0)
    const = lambda i: (0, 0)
    return pl.pallas_call(
        body, name="loss_head", grid=(SEQ // tm,),
        in_specs=[pl.BlockSpec((tm, D_MODEL), row), pl.BlockSpec((1, D_MODEL), const),
                  pl.BlockSpec((tm, D_MODEL), row)],
        out_specs=[pl.BlockSpec((1, LANES), const), pl.BlockSpec((tm, D_MODEL), row),
                   pl.BlockSpec((1, D_MODEL), const)],
        out_shape=[jax.ShapeDtypeStruct((1, LANES), F32), jax.ShapeDtypeStruct((SEQ, D_MODEL), F32),
                   jax.ShapeDtypeStruct((1, D_MODEL), F32)],
        compiler_params=_cparams(("arbitrary",)),
    )(x, g, target)


def _local_step(x, target, w, p, late_weights, early_grads, after=None, midway=None):
    x1, h1, a1, b1 = _ffn_fwd(x, p["norm_ffn1"], w["wgt1"], w["wut1"], w["wd1"], "ffn1_fwd", after=after)
    h2, q, k, v, u = _mixin_fwd(x1, p["norm_mix"], w["wint"])
    kp = jnp.pad(k, ((WINDOW, WINDOW), (0, 0)))
    vp = jnp.pad(v, ((WINDOW, WINDOW), (0, 0)))
    o = _attn_fwd(q, kp, vp, p["attn_sinks"])

    lam_re = p["ssm_lambda_re"].reshape(2 * N_LANE_BLOCKS, LANES)
    lam_im = p["ssm_lambda_im"].reshape(2 * N_LANE_BLOCKS, LANES)
    log_dt = jnp.repeat(p["ssm_log_dt"].reshape(2, 32), 64, axis=-1).reshape(2 * N_LANE_BLOCKS, LANES)
    a_re, a_im, bbr, bbi, cre, cim = _ssm_prep(lam_re, lam_im, log_dt, p["ssm_b_re"], p["ssm_b_im"],
                                               p["ssm_c_re"], p["ssm_c_im"])
    shape_a = (2, N_LANE_BLOCKS, 1, LANES)
    shape_w = (2, N_LANE_BLOCKS, LANES, LANES)
    a_re4, a_im4 = a_re.reshape(shape_a), a_im.reshape(shape_a)
    bbr4, bbi4 = bbr.reshape(shape_w), bbi.reshape(shape_w)
    cre, cim = cre.reshape(shape_w), cim.reshape(shape_w)
    dskip = p["ssm_d"].T.reshape(1, SSM_WIDTH)
    y, xr, xi = _ssm_fwd(u, a_re4, a_im4, bbr4, bbi4, cre, cim, dskip,
                         after=None if midway is None else midway(o))

    w2 = late_weights(y)
    x2, mixed = _mixout_fwd(o, y, w2["glu"], p["ssm_glu_b"], p["attn_out_norm"], p["ssm_out_norm"], w2["wout"], x1)
    x3, h3, a3, b3 = _ffn_fwd(x2, p["norm_ffn2"], w2["wgt2"], w2["wut2"], w2["wd2"], "ffn2_fwd")

    loss, dx3, d_final = _loss_head(x3, p["final_norm"], target)
    dx2, da3, db3, s3, df3, d_n2 = _ffn_bwd_act(dx3, x2, p["norm_ffn2"], a3, b3, w2["wgt2"], w2["wut2"], w2["wd2"],
                                                "ffn2_bwd_act")
    g_wgt2, g_wut2, g_wd2 = _mm_tn([(da3, h3), (db3, h3), (s3, df3)], "ffn2_bwd_w")

    do, dy, dz, ygb, dx2b, d_gan, d_gsn, d_glub = _mixout_bwd(
        dx2, o, y, w2["glu"], p["ssm_glu_b"], p["attn_out_norm"], p["ssm_out_norm"], w2["wout"])
    (g_wout,) = _mm_tn([(mixed, dx2b)], "wout_bwd_w")
    (g_glu,) = _mm_tn([(ygb, dz)], "glu_bwd_w")
    sent = early_grads(dict(glu=g_glu, wout=g_wout, wgt2=g_wgt2, wut2=g_wut2, wd2=g_wd2))

    du, d_dskip, dcre, dcim, dbbr, dbbi, dar, dai = _ssm_bwd(dy, u, xr, xi, a_re4, a_im4, bbr4, bbi4, cre, cim, dskip,
                                                             after=sent)
    nb = 2 * N_LANE_BLOCKS
    blocks3 = (nb, LANES, LANES)
    g_lre, g_lim, g_ldt, g_btr, g_bti, g_cre, g_cim = _ssm_prep_bwd(
        lam_re, lam_im, log_dt, p["ssm_b_re"], p["ssm_b_im"], dar.reshape(nb, LANES), dai.reshape(nb, LANES),
        dbbr.reshape(blocks3), dbbi.reshape(blocks3), dcre.reshape(blocks3), dcim.reshape(blocks3))

    dq, dkp, dvp, d_sinks = _attn_bwd(q, kp, vp, p["attn_sinks"], do)
    dk = dkp[WINDOW:WINDOW + SEQ]
    dv = dvp[WINDOW:WINDOW + SEQ]
    dx1, dproj, d_nmix = _mixin_bwd(dq, dk, dv, du, w["wint"], x1, p["norm_mix"], dx2)
    (g_wint,) = _mm_tn([(dproj, h2)], "win_bwd_w")

    dx0, da1, db1, s1, df1, d_n1 = _ffn_bwd_act(dx1, x, p["norm_ffn1"], a1, b1, w["wgt1"], w["wut1"], w["wd1"],
                                                "ffn1_bwd_act")
    g_wgt1, g_wut1, g_wd1 = _mm_tn([(da1, h1), (db1, h1), (s1, df1)], "ffn1_bwd_w")

    big = dict(wgt1=g_wgt1, wut1=g_wut1, wd1=g_wd1, wint=g_wint)
    small = dict(
        norm_ffn1=d_n1, norm_mix=d_nmix, attn_sinks=d_sinks,
        ssm_lambda_re=g_lre.reshape(64, 64), ssm_lambda_im=g_lim.reshape(64, 64),
        ssm_log_dt=g_ldt.reshape(2, 32), ssm_b_re=g_btr, ssm_b_im=g_bti, ssm_c_re=g_cre, ssm_c_im=g_cim,
        ssm_d=d_dskip.reshape(32, 16).T, ssm_glu_b=d_glub, attn_out_norm=d_gan, ssm_out_norm=d_gsn,
        norm_ffn2=d_n2, final_norm=d_final, loss=loss)
    return loss, dx0, big, small


BIG = dict(
    wgt1=("ffn1_w_gate", 352, 1024, True), wut1=("ffn1_w_up", 352, 1024, True), wd1=("ffn1_w_down", 352, 1024, False),
    wint=("w_in", 160, 1024, True), glu=("ssm_glu_w", 64, 512, False), wout=("w_out", 128, 1024, False),
    wgt2=("ffn2_w_gate", 352, 1024, True), wut2=("ffn2_w_up", 352, 1024, True), wd2=("ffn2_w_down", 352, 1024, False))

SMALL = dict(
    norm_ffn1=(1, 1024), norm_mix=(1, 1024), attn_sinks=(1, 8), ssm_lambda_re=(64, 64), ssm_lambda_im=(64, 64),
    ssm_log_dt=(2, 32), ssm_b_re=(1024, 64), ssm_b_im=(1024, 64), ssm_c_re=(1024, 64), ssm_c_im=(1024, 64),
    ssm_d=(16, 32), ssm_glu_b=(1, 512), attn_out_norm=(1, 512), ssm_out_norm=(1, 512), norm_ffn2=(1, 1024),
    final_norm=(1, 1024), loss=(1, 128))
SMALL_TRANSPOSED = ("ssm_b_re", "ssm_b_im", "ssm_d")
SMALL_PARAMS = tuple(n for n in SMALL if n != "loss")

SMALL_PAIRS = (("ssm_lambda_re", "ssm_lambda_im"), ("ssm_c_re", "ssm_c_im"), ("ssm_b_re", "ssm_b_im"))
SMALL_VECS = ("norm_ffn1", "norm_mix", "norm_ffn2", "final_norm", "ssm_glu_b", "attn_out_norm", "ssm_out_norm")
SMALL_TILES = ("ssm_log_dt", "attn_sinks", "ssm_d", "loss")


def _small_offsets():
    off, table = 0, {}
    for re, im in SMALL_PAIRS:
        table[re] = table[im] = off
        off += SMALL[re][0]
    for n in SMALL_VECS:
        table[n] = off
        off += SMALL[n][1] // LANES
    for n in SMALL_TILES:
        off = -(-off // 8) * 8
        table[n] = off
        off += SMALL[n][0]
    return table, off


SMALL_OFFSET, SMALL_USED_ROWS = _small_offsets()
SMALL_ROWS = -(-SMALL_USED_ROWS // (8 * N_DEV)) * 8 * N_DEV


def _cast_shards(shards):
    names = list(BIG)

    def body(*refs):
        ins, outs = refs[:len(names)], refs[len(names):]
        for idx in range(len(names)):
            outs[idx][...] = ins[idx][...].astype(BF16)

    return pl.pallas_call(
        body, name="cast_shards",
        out_shape=[jax.ShapeDtypeStruct((BIG[n][1], BIG[n][2]), BF16) for n in names],
        compiler_params=_cparams(),
    )(*[shards[n] for n in names])


def _peer(x, y, c, r):
    px = 1 - x if r & 4 else x
    py = 1 - y if r & 2 else y
    pc = 1 - c if r & 1 else c
    return px, py, pc


FIRST_GROUP = ("wgt1", "wut1", "wd1", "wint")
LATE_GROUP = ("glu", "wout", "wgt2", "wut2", "wd2")
N_PEERS = N_DEV - 1
ANY_SPEC = pl.BlockSpec(memory_space=pl.ANY)
HBM_SPEC = pl.BlockSpec(memory_space=pltpu.HBM)
SEM_SPEC = pl.BlockSpec(memory_space=pltpu.SEMAPHORE)
DATAFLOW_EFFECT = pltpu.SideEffectType.DATAFLOW_SIDE_EFFECTING


def _mesh_pos():
    x, y, c = lax.axis_index("x"), lax.axis_index("y"), lax.axis_index("c")
    return x, y, c, 4 * x + 2 * y + c


def _gather_first(first, late):
    nf, nl = len(first), len(late)

    def body(*refs):
        f_in, l_in = refs[:nf], refs[nf:nf + nl]
        f_out, l_out = refs[nf + nl:2 * nf + nl], refs[2 * nf + nl:2 * (nf + nl)]
        send_sems, recv_sems, local_sems = refs[2 * (nf + nl):]
        x, y, c, me = _mesh_pos()
        sibling = (x, y, 1 - c)
        chips = [(x, 1 - y), (1 - x, y), (1 - x, 1 - y)]

        def idx(px, py, pc):
            return 4 * px + 2 * py + pc

        def copy(k, s, block, to, src=None):
            slot = f_out[k].at[block]
            return pltpu.make_async_remote_copy(
                src_ref=slot if src is None else src, dst_ref=slot, send_sem=send_sems.at[k, s],
                recv_sem=recv_sems.at[k, s], device_id=to, device_id_type=MESH_ID)

        local = []
        for k in range(nf + nl):
            src, dst = (f_in[k], f_out[k]) if k < nf else (l_in[k - nf], l_out[k - nf])
            mine = pltpu.make_async_copy(src, dst.at[me], local_sems.at[k])
            mine.start()
            local.append(mine)
        sends = []
        for j, chip in enumerate(chips):
            for k in range(nf):
                sends.append(copy(k, 1 + j, me, (*chip, c), src=f_in[k]))
                sends[-1].start()
        for k in range(nf):
            sends.append(copy(k, 0, me, sibling, src=f_in[k]))
            sends[-1].start()
        for j, chip in enumerate(chips):
            for k in range(nf):
                copy(k, 1 + j, idx(*chip, c), (*chip, c)).wait_recv()
                sends.append(copy(k, 4 + j, idx(*chip, c), sibling))
                sends[-1].start()
        for k in range(nf):
            copy(k, 0, idx(*sibling), sibling).wait_recv()
        for j, chip in enumerate(chips):
            for k in range(nf):
                copy(k, 4 + j, idx(*chip, 1 - c), sibling).wait_recv()
        for cp in sends:
            cp.wait_send()
        for cp in local:
            cp.wait()

    return pl.pallas_call(
        body, name="gather_first",
        in_specs=[ANY_SPEC] * (nf + nl), out_specs=[ANY_SPEC] * (nf + nl),
        out_shape=[jax.ShapeDtypeStruct((N_DEV,) + s.shape, s.dtype) for s in list(first) + list(late)],
        scratch_shapes=[pltpu.SemaphoreType.DMA((nf, N_PEERS)), pltpu.SemaphoreType.DMA((nf, N_PEERS)),
                        pltpu.SemaphoreType.DMA((nf + nl,))],
        compiler_params=pltpu.CompilerParams(has_side_effects=True),
    )(*first, *late)


def _split_copy(src_refs, land_refs, send_sems, recv_sems, k, r, pos, scatter, receiving):
    x, y, c, me = pos
    px, py, pc = _peer(x, y, c, r)
    peer_idx = 4 * px + 2 * py + pc
    if scatter:
        src, dst = src_refs[k].at[peer_idx], land_refs[k].at[r - 1]
    else:
        src, dst = src_refs[k], land_refs[k].at[peer_idx if receiving else me]
    return pltpu.make_async_remote_copy(
        src_ref=src, dst_ref=dst, send_sem=send_sems.at[k * N_PEERS + r - 1],
        recv_sem=recv_sems.at[k * N_PEERS + r - 1], device_id=(px, py, pc), device_id_type=MESH_ID)


def _split_start(name, srcs, lands, scatter):
    n = len(srcs)

    def body(*refs):
        src_refs, land_refs = refs[:n], refs[n:2 * n]
        send_sems, recv_sems = refs[2 * n], refs[2 * n + 1]
        token = refs[-1]
        pos = _mesh_pos()
        for k in range(n):
            for r in range(1, N_DEV):
                _split_copy(src_refs, land_refs, send_sems, recv_sems, k, r, pos, scatter, False).start()
        token[...] = jnp.zeros_like(token)

    thru = [pltpu.HBM(a.shape, a.dtype) for a in list(srcs) + list(lands)]
    outs = pl.pallas_call(
        body, name=name,
        in_specs=[HBM_SPEC] * (2 * n),
        out_specs=[SEM_SPEC, SEM_SPEC] + [HBM_SPEC] * (2 * n) + [pl.BlockSpec(memory_space=pltpu.VMEM)],
        out_shape=[pltpu.SemaphoreType.DMA((n * N_PEERS,)), pltpu.SemaphoreType.DMA((n * N_PEERS,))] + thru
        + [jax.ShapeDtypeStruct((8, LANES), F32)],
        input_output_aliases={i: 2 + i for i in range(2 * n)},
        compiler_params=pltpu.CompilerParams(has_side_effects=DATAFLOW_EFFECT),
    )(*[pltpu.with_memory_space_constraint(a, pltpu.HBM) for a in list(srcs) + list(lands)])
    return outs[0], outs[1], outs[2:2 + n], outs[2 + n:2 + 2 * n], outs[-1]


def _split_wait(name, send_sems, recv_sems, srcs, lands, scatter, after):
    n = len(srcs)

    def body(*refs):
        src_refs, land_refs = refs[:n], refs[n:2 * n]
        send, recv = refs[2 * n], refs[2 * n + 1]
        pos = _mesh_pos()
        for k in range(n):
            for r in range(1, N_DEV):
                cp = _split_copy(src_refs, land_refs, send, recv, k, r, pos, scatter, True)
                cp.wait_send()
                cp.wait_recv()

    thru = [pltpu.HBM(a.shape, a.dtype) for a in list(srcs) + list(lands)]
    outs = pl.pallas_call(
        body, name=name,
        in_specs=[HBM_SPEC] * (2 * n) + [SEM_SPEC, SEM_SPEC, ANY_SPEC],
        out_specs=[HBM_SPEC] * (2 * n), out_shape=thru,
        input_output_aliases={i: i for i in range(2 * n)},
        compiler_params=pltpu.CompilerParams(has_side_effects=DATAFLOW_EFFECT),
    )(*srcs, *lands, send_sems, recv_sems, after)
    return outs[:n], outs[n:]


def _late_copy(passing, src_refs, land_refs, send_sems, recv_sems, k, s, pos, receiving):
    x, y, c, me = pos
    chips = [(x, 1 - y), (1 - x, y), (1 - x, 1 - y)]
    sibling = (x, y, 1 - c)

    def idx(dev):
        return 4 * dev[0] + 2 * dev[1] + dev[2]

    if passing:
        to = sibling
        block = idx((*chips[s], 1 - c)) if receiving else idx((*chips[s], c))
        src = dst = land_refs[k].at[block]
        sem = k * 3 + s
    else:
        to = sibling if s == 0 else (*chips[s - 1], c)
        src, dst = src_refs[k], land_refs[k].at[idx(to) if receiving else me]
        sem = k * 4 + s
    return pltpu.make_async_remote_copy(src_ref=src, dst_ref=dst, send_sem=send_sems.at[sem],
                                        recv_sem=recv_sems.at[sem], device_id=to, device_id_type=MESH_ID)


def _late_gather_call(name, stage, srcs, lands, sems, after=None):
    n = len(srcs)
    n_sem_in = len(sems)
    has_after = after is not None

    def body(*refs):
        src_refs, land_refs = refs[:n], refs[n:2 * n]
        sem_in = refs[2 * n:2 * n + n_sem_in]
        outs = refs[2 * n + n_sem_in + (1 if has_after else 0):]
        pos = _mesh_pos()
        if stage == 0:
            own_send, own_recv = outs[0], outs[1]
            for s in (1, 2, 3, 0):
                for k in range(n):
                    _late_copy(False, src_refs, land_refs, own_send, own_recv, k, s, pos, False).start()
            outs[-1][...] = jnp.zeros_like(outs[-1])
        elif stage == 1:
            own_recv = sem_in[1]
            pass_send, pass_recv = outs[0], outs[1]
            for s in range(3):
                for k in range(n):
                    _late_copy(False, src_refs, land_refs, sem_in[0], own_recv, k, s + 1, pos, True).wait_recv()
                    _late_copy(True, src_refs, land_refs, pass_send, pass_recv, k, s, pos, False).start()
            outs[-1][...] = jnp.zeros_like(outs[-1])
        else:
            own_send, own_recv, pass_send, pass_recv = sem_in
            for k in range(n):
                _late_copy(False, src_refs, land_refs, own_send, own_recv, k, 0, pos, True).wait_recv()
                for s in range(4):
                    _late_copy(False, src_refs, land_refs, own_send, own_recv, k, s, pos, False).wait_send()
                for s in range(3):
                    cp = _late_copy(True, src_refs, land_refs, pass_send, pass_recv, k, s, pos, True)
                    cp.wait_recv()
                    cp.wait_send()

    thru = [pltpu.HBM(a.shape, a.dtype) for a in list(srcs) + list(lands)]
    new_sems = [[pltpu.SemaphoreType.DMA((n * 4,))] * 2, [pltpu.SemaphoreType.DMA((n * 3,))] * 2, []][stage]
    extra = [] if stage == 2 else [jax.ShapeDtypeStruct((8, LANES), F32)]
    outs = pl.pallas_call(
        body, name=name,
        in_specs=[HBM_SPEC] * (2 * n) + [SEM_SPEC] * n_sem_in + [ANY_SPEC] * has_after,
        out_specs=[SEM_SPEC] * len(new_sems) + [HBM_SPEC] * (2 * n) + [pl.BlockSpec(memory_space=pltpu.VMEM)] * len(extra),
        out_shape=new_sems + thru + extra,
        input_output_aliases={i: len(new_sems) + i for i in range(2 * n)},
        compiler_params=pltpu.CompilerParams(has_side_effects=DATAFLOW_EFFECT),
    )(*[pltpu.with_memory_space_constraint(a, pltpu.HBM) for a in list(srcs) + list(lands)], *sems,
      *([after] if has_after else []))
    ns = len(new_sems)
    return list(outs[:ns]), outs[ns:ns + n], outs[ns + n:ns + 2 * n], (outs[-1] if extra else None)


N_SEND_SLOTS = 3


def _exchange_last(grads, small_packed):
    ng = len(grads)
    ch = SMALL_ROWS // N_DEV
    max_rows = max(g.shape[1] for g in grads)
    cols = grads[0].shape[2]

    def body(*refs):
        g_in, s_in = refs[:ng], refs[ng]
        outs = refs[ng + 1:]
        own_out, land, stage = outs[:ng], outs[ng:2 * ng], outs[2 * ng:3 * ng]
        s_red, s_stage = outs[3 * ng], outs[3 * ng + 1]
        (va, vb, vo, vs, sm_in, sm_out, d2d_send, d2d_recv, ici_send, ici_recv, s1_send, s1_recv, s2_send, s2_recv,
         local_sems) = outs[3 * ng + 2:]
        x, y, c, me = _mesh_pos()
        sibling = (x, y, 1 - c)
        chips = [(x, y), (x, 1 - y), (1 - x, y), (1 - x, 1 - y)]

        def idx(chip, core):
            return 4 * chip[0] + 2 * chip[1] + core

        def d2d(k, j):
            return pltpu.make_async_remote_copy(
                src_ref=g_in[k].at[idx(chips[j], 1 - c)], dst_ref=stage[k].at[j], send_sem=d2d_send.at[k, j],
                recv_sem=d2d_recv.at[k, j], device_id=sibling, device_id_type=MESH_ID)

        def ici(k, j, slot):
            rows = g_in[k].shape[1]
            return pltpu.make_async_remote_copy(
                src_ref=vo.at[slot, pl.ds(0, rows)], dst_ref=land[k].at[j - 1], send_sem=ici_send.at[k, j - 1],
                recv_sem=ici_recv.at[k, j - 1], device_id=(*chips[j], c), device_id_type=MESH_ID)

        def small_scatter(r):
            px, py, pc = _peer(x, y, c, r)
            return pltpu.make_async_remote_copy(
                src_ref=s_in.at[pl.ds(pl.multiple_of((4 * px + 2 * py + pc) * ch, 8), ch)], dst_ref=s_stage.at[me],
                send_sem=s1_send.at[r - 1], recv_sem=s1_recv.at[r - 1], device_id=(px, py, pc), device_id_type=MESH_ID)

        def small_gather(r):
            return pltpu.make_async_remote_copy(
                src_ref=sm_out, dst_ref=s_red.at[me], send_sem=s2_send.at[r - 1], recv_sem=s2_recv.at[r - 1],
                device_id=_peer(x, y, c, r), device_id_type=MESH_ID)

        for r in range(1, N_DEV):
            small_scatter(r).start()
        mine = pltpu.make_async_copy(s_in.at[pl.ds(pl.multiple_of(me * ch, 8), ch)], s_stage.at[me], local_sems.at[0])
        mine.start()
        pairs = [(k, j) for k in range(ng) for j in (1, 2, 3)] + [(k, 0) for k in range(ng)]
        for k, j in pairs:
            d2d(k, j).start()

        for r in range(1, N_DEV):
            small_scatter(r).wait_recv()
        mine.wait()
        load = pltpu.make_async_copy(s_stage, sm_in, local_sems.at[1])
        load.start()
        load.wait()
        total = sm_in[0]
        for i in range(1, N_DEV):
            total = total + sm_in[i]
        sm_out[...] = total
        for r in range(1, N_DEV):
            small_gather(r).start()
        keep = pltpu.make_async_copy(sm_out, s_red.at[me], local_sems.at[2])
        keep.start()

        in_flight = {}
        for i, (k, j) in enumerate(pairs):
            slot = i % N_SEND_SLOTS
            rows = g_in[k].shape[1]
            if slot in in_flight:
                in_flight.pop(slot).wait_send()
            d2d(k, j).wait_recv()
            la = pltpu.make_async_copy(g_in[k].at[idx(chips[j], c)], va.at[pl.ds(0, rows)], local_sems.at[3])
            lb = pltpu.make_async_copy(stage[k].at[j], vb.at[pl.ds(0, rows)], local_sems.at[4])
            la.start()
            lb.start()
            la.wait()
            lb.wait()
            total = va[pl.ds(0, rows)].astype(F32) + vb[pl.ds(0, rows)].astype(F32)
            if j == 0:
                vs[pl.ds(0, rows)] = total
                st = pltpu.make_async_copy(vs.at[pl.ds(0, rows)], own_out[k], local_sems.at[5])
                st.start()
                st.wait()
            else:
                vo[slot, pl.ds(0, rows)] = total.astype(BF16)
                cp = ici(k, j, slot)
                cp.start()
                in_flight[slot] = cp
        for cp in in_flight.values():
            cp.wait_send()

        for j in (1, 2, 3, 0):
            for k in range(ng):
                d2d(k, j).wait_send()
        for j in (1, 2, 3):
            for k in range(ng):
                ici(k, j, 0).wait_recv()
        for r in range(1, N_DEV):
            small_scatter(r).wait_send()
            small_gather(r).wait_send()
            small_gather(r).wait_recv()
        keep.wait()

    out_shape = [jax.ShapeDtypeStruct(g.shape[1:], F32) for g in grads]
    out_shape += [jax.ShapeDtypeStruct((3,) + g.shape[1:], BF16) for g in grads]
    out_shape += [jax.ShapeDtypeStruct((4,) + g.shape[1:], BF16) for g in grads]
    out_shape += [jax.ShapeDtypeStruct((N_DEV, ch, LANES), F32), jax.ShapeDtypeStruct((N_DEV, ch, LANES), F32)]
    outs = pl.pallas_call(
        body, name="exchange_last",
        in_specs=[ANY_SPEC] * (ng + 1), out_specs=[ANY_SPEC] * len(out_shape), out_shape=out_shape,
        scratch_shapes=[pltpu.VMEM((max_rows, cols), BF16), pltpu.VMEM((max_rows, cols), BF16),
                        pltpu.VMEM((N_SEND_SLOTS, max_rows, cols), BF16), pltpu.VMEM((max_rows, cols), F32),
                        pltpu.VMEM((N_DEV, ch, LANES), F32), pltpu.VMEM((ch, LANES), F32),
                        pltpu.SemaphoreType.DMA((ng, 4)), pltpu.SemaphoreType.DMA((ng, 4)),
                        pltpu.SemaphoreType.DMA((ng, 3)), pltpu.SemaphoreType.DMA((ng, 3)),
                        pltpu.SemaphoreType.DMA((N_PEERS,)), pltpu.SemaphoreType.DMA((N_PEERS,)),
                        pltpu.SemaphoreType.DMA((N_PEERS,)), pltpu.SemaphoreType.DMA((N_PEERS,)),
                        pltpu.SemaphoreType.DMA((6,))],
        compiler_params=pltpu.CompilerParams(has_side_effects=True, vmem_limit_bytes=VMEM_LIMIT),
    )(*grads, small_packed)
    return outs[:ng], outs[ng:2 * ng], outs[3 * ng].reshape(SMALL_ROWS, LANES)


def _adamw_math(w, g, m, v):
    m2 = ADAM_B1 * m + (1.0 - ADAM_B1) * g
    v2 = ADAM_B2 * v + (1.0 - ADAM_B2) * (g * g)
    m_hat = m2 / (1.0 - ADAM_B1 ** ADAM_STEP)
    v_hat = v2 / (1.0 - ADAM_B2 ** ADAM_STEP)
    delta = -ADAM_LR * (m_hat / (jnp.sqrt(v_hat) + ADAM_EPS) + ADAM_WD * w)
    return delta, m2, v2


ADAM_ROW_TILES = 2


def _adamw_big(own, parts, w, m, v, name):
    shape = w.shape
    own_is_blocks = own.ndim == 3
    tr = shape[0] // ADAM_ROW_TILES
    n_parts = parts.shape[0]

    def body(own_ref, p_ref, w_ref, m_ref, v_ref, g_ref, d_ref, m2_ref, v2_ref, own_s, sem):
        rows = pl.ds(pl.multiple_of(pl.program_id(0) * tr, 16), tr)
        if own_is_blocks:
            cp = pltpu.make_async_copy(own_ref.at[_mesh_pos()[3], rows], own_s, sem)
        else:
            cp = pltpu.make_async_copy(own_ref.at[rows], own_s, sem)
        cp.start()
        cp.wait()
        g = own_s[...].astype(F32)
        for i in range(parts.shape[0]):
            g = g + p_ref[i].astype(F32)
        delta, m2, v2 = _adamw_math(w_ref[...], g, m_ref[...], v_ref[...])
        g_ref[...] = g
        d_ref[...] = delta
        m2_ref[...] = m2
        v2_ref[...] = v2

    tile = pl.BlockSpec((tr, shape[1]), lambda i: (i, 0))
    return pl.pallas_call(
        body, name=name, grid=(ADAM_ROW_TILES,),
        in_specs=[ANY_SPEC, pl.BlockSpec((n_parts, tr, shape[1]), lambda i: (0, i, 0)), tile, tile, tile],
        out_specs=[tile] * 4, out_shape=[jax.ShapeDtypeStruct(shape, F32)] * 4,
        scratch_shapes=[pltpu.VMEM((tr, shape[1]), own.dtype), pltpu.SemaphoreType.DMA(())],
        compiler_params=_cparams(("arbitrary",)),
    )(own, parts, w, m, v)


def _pack_small(grads):
    names = list(SMALL)

    def body(*refs):
        ins, out = dict(zip(names, refs[:-1])), refs[-1]
        out[...] = jnp.zeros_like(out)
        for re, im in SMALL_PAIRS:
            off, rows = SMALL_OFFSET[re], SMALL[re][0]
            out[off:off + rows, :] = jnp.concatenate([ins[re][...], ins[im][...]], axis=1)
        for n in SMALL_VECS:
            off, vec = SMALL_OFFSET[n], ins[n][...]
            for i in range(SMALL[n][1] // LANES):
                out[off + i:off + i + 1, :] = vec[:, i * LANES:(i + 1) * LANES]
        for n in SMALL_TILES:
            off, (rows, cols) = SMALL_OFFSET[n], SMALL[n]
            out[off:off + rows, 0:cols] = ins[n][...]

    return pl.pallas_call(
        body, name="pack_small", out_shape=jax.ShapeDtypeStruct((SMALL_ROWS, LANES), F32),
        compiler_params=_cparams(),
    )(*[grads[n] for n in names])


def _unpack_small_ref(g_ref, n):
    off, (rows, cols) = SMALL_OFFSET[n], SMALL[n]
    for re, im in SMALL_PAIRS:
        if n == re:
            return g_ref[off:off + rows, 0:HALF_LANES]
        if n == im:
            return g_ref[off:off + rows, HALF_LANES:LANES]
    if n in SMALL_VECS:
        return jnp.concatenate([g_ref[off + i:off + i + 1, :] for i in range(cols // LANES)], axis=1)
    return g_ref[off:off + rows, 0:cols]


def _adamw_small(g_packed, w, m, v):
    names = list(SMALL_PARAMS)
    n = len(names)

    def body(g_ref, *refs):
        w_refs, m_refs, v_refs, outs = refs[:n], refs[n:2 * n], refs[2 * n:3 * n], refs[3 * n:]
        for idx, name in enumerate(names):
            g = _unpack_small_ref(g_ref, name)
            delta, m2, v2 = _adamw_math(w_refs[idx][...], g, m_refs[idx][...], v_refs[idx][...])
            outs[4 * idx][...] = g
            outs[4 * idx + 1][...] = delta
            outs[4 * idx + 2][...] = m2
            outs[4 * idx + 3][...] = v2
        outs[4 * n][...] = _unpack_small_ref(g_ref, "loss")

    outs = pl.pallas_call(
        body, name="adamw_small",
        out_shape=[jax.ShapeDtypeStruct(SMALL[name], F32) for name in names for _ in range(4)]
        + [jax.ShapeDtypeStruct(SMALL["loss"], F32)],
        compiler_params=_cparams(),
    )(g_packed, *[w[k] for k in names], *[m[k] for k in names], *[v[k] for k in names])
    return {name: outs[4 * idx:4 * idx + 4] for idx, name in enumerate(names)}, outs[4 * n]


WEIGHT_NAMES = ['norm_ffn1', 'ffn1_w_gate', 'ffn1_w_up', 'ffn1_w_down', 'norm_mix', 'w_in', 'attn_sinks',
                'ssm_lambda_re', 'ssm_lambda_im', 'ssm_log_dt', 'ssm_b_re', 'ssm_b_im', 'ssm_c_re', 'ssm_c_im',
                'ssm_d', 'ssm_glu_w', 'ssm_glu_b', 'attn_out_norm', 'ssm_out_norm', 'w_out', 'norm_ffn2',
                'ffn2_w_gate', 'ffn2_w_up', 'ffn2_w_down', 'final_norm']


def kernel(x, norm_ffn1, ffn1_w_gate, ffn1_w_up, ffn1_w_down, norm_mix, w_in, attn_sinks, ssm_lambda_re, ssm_lambda_im, ssm_log_dt, ssm_b_re, ssm_b_im, ssm_c_re, ssm_c_im, ssm_d, ssm_glu_w, ssm_glu_b, attn_out_norm, ssm_out_norm, w_out, norm_ffn2, ffn2_w_gate, ffn2_w_up, ffn2_w_down, final_norm, loss_target, m_norm_ffn1, m_ffn1_w_gate, m_ffn1_w_up, m_ffn1_w_down, m_norm_mix, m_w_in, m_attn_sinks, m_ssm_lambda_re, m_ssm_lambda_im, m_ssm_log_dt, m_ssm_b_re, m_ssm_b_im, m_ssm_c_re, m_ssm_c_im, m_ssm_d, m_ssm_glu_w, m_ssm_glu_b, m_attn_out_norm, m_ssm_out_norm, m_w_out, m_norm_ffn2, m_ffn2_w_gate, m_ffn2_w_up, m_ffn2_w_down, m_final_norm, v_norm_ffn1, v_ffn1_w_gate, v_ffn1_w_up, v_ffn1_w_down, v_norm_mix, v_w_in, v_attn_sinks, v_ssm_lambda_re, v_ssm_lambda_im, v_ssm_log_dt, v_ssm_b_re, v_ssm_b_im, v_ssm_c_re, v_ssm_c_im, v_ssm_d, v_ssm_glu_w, v_ssm_glu_b, v_attn_out_norm, v_ssm_out_norm, v_w_out, v_norm_ffn2, v_ffn2_w_gate, v_ffn2_w_up, v_ffn2_w_down, v_final_norm):
    args = dict(locals())
    weights = {n: args[n] for n in WEIGHT_NAMES}
    moms = {n: args["m_" + n] for n in WEIGHT_NAMES}
    vars_ = {n: args["v_" + n] for n in WEIGHT_NAMES}

    def shard2d(a, k):
        a = a.reshape(a.shape[-2], a.shape[-1])
        return a.T if BIG[k][3] else a

    def shard_master(a, k):
        return (a.T if BIG[k][3] else a).reshape(weights[BIG[k][0]].shape)

    def blocks(g, k):
        return g.reshape(N_DEV, BIG[k][1], BIG[k][2])

    def full(g, k):
        return g.reshape(N_DEV * BIG[k][1], BIG[k][2])

    shards = dict(zip(BIG, _cast_shards({k: shard2d(weights[BIG[k][0]], k) for k in BIG})))
    nf = len(FIRST_GROUP)
    got = _gather_first([shards[k] for k in FIRST_GROUP], [shards[k] for k in LATE_GROUP])
    w_first = {k: full(g, k) for k, g in zip(FIRST_GROUP, got[:nf])}
    late = {}
    late["own_sems"], late["srcs"], late["lands"], w_token = _late_gather_call(
        "gather_late_start", 0, [shards[k] for k in LATE_GROUP], got[nf:], [])

    def late_pass(dep):
        late["pass_sems"], late["srcs"], late["lands"], token = _late_gather_call(
            "gather_late_pass", 1, late["srcs"], late["lands"], late["own_sems"], after=dep)
        return token

    def late_weights(dep):
        _, _, lands, _ = _late_gather_call("gather_late_wait", 2, late["srcs"], late["lands"],
                                           late["own_sems"] + late["pass_sems"], after=dep)
        return {k: full(g, k) for k, g in zip(LATE_GROUP, lands)}

    early = {}

    def early_grads(g):
        srcs = [blocks(g[k], k) for k in LATE_GROUP]
        lands = [lax.empty((N_PEERS, BIG[k][1], BIG[k][2]), BF16) for k in LATE_GROUP]
        early["send"], early["recv"], early["srcs"], early["lands"], token = _split_start(
            "grads_late_start", srcs, lands, scatter=True)
        return token

    def small2d(a, n):
        if n in SMALL_TRANSPOSED:
            a = jnp.swapaxes(a, -1, -2)
        return a.reshape(SMALL[n])

    def small_master(a, n):
        if n in SMALL_TRANSPOSED:
            shape = weights[n].shape
            return jnp.swapaxes(a.reshape(shape[:-2] + (shape[-1], shape[-2])), -1, -2)
        return a.reshape(weights[n].shape)

    small_p = {n: small2d(weights[n], n) for n in SMALL_PARAMS}
    _, grad_x, g_first, g_small = _local_step(
        x.reshape(SEQ, D_MODEL), loss_target.reshape(SEQ, D_MODEL), w_first, small_p, late_weights, early_grads,
        after=w_token, midway=late_pass)

    own_sums, first_parts, small_grad = _exchange_last([blocks(g_first[k], k) for k in FIRST_GROUP],
                                                       _pack_small(g_small))
    own_late, late_parts = _split_wait("grads_late_wait", early["send"], early["recv"], early["srcs"],
                                       early["lands"], True, small_grad)
    own = dict(zip(FIRST_GROUP + LATE_GROUP, list(own_sums) + list(own_late)))
    parts = dict(zip(FIRST_GROUP + LATE_GROUP, list(first_parts) + list(late_parts)))
    outs = {}
    for k in BIG:
        n = BIG[k][0]
        outs[n] = [shard_master(o, k) for o in
                   _adamw_big(own[k], parts[k], shard2d(weights[n], k), shard2d(moms[n], k), shard2d(vars_[n], k),
                              "adamw_" + n)]
    small_out, loss_row = _adamw_small(small_grad, small_p, {n: small2d(moms[n], n) for n in SMALL_PARAMS},
                                       {n: small2d(vars_[n], n) for n in SMALL_PARAMS})
    for n in SMALL_PARAMS:
        outs[n] = [small_master(o, n) for o in small_out[n]]

    result = [loss_row[0, 0], grad_x.reshape(x.shape)]
    for i in range(4):
        result += [outs[n][i] for n in WEIGHT_NAMES]
    return tuple(result)
```

```python
import functools

import jax
import jax.numpy as jnp
from jax import lax
from jax.experimental import pallas as pl
from jax.experimental.pallas import tpu as pltpu

F32 = jnp.float32
BF16 = jnp.bfloat16

N_DEV = 8
SEQ = 2048
D_MODEL = 1024
D_FF = 2816
ATTN_HEADS = 8
KV_HEADS = 2
HEAD_DIM = 64
ATTN_WIDTH = 512
KV_WIDTH = 128
WINDOW = 128
SSM_WIDTH = 512
IN_WIDTH = 1280
EPS = 1e-6
NEG_INF = -1e30
LAMBDA_RE_MAX = -1e-4
LANES = 128
N_LANE_BLOCKS = 16
SCAN_CHUNK = SEQ // 8

ADAM_LR = 0.001
ADAM_B1 = 0.9
ADAM_B2 = 0.999
ADAM_EPS = 1e-08
ADAM_WD = 0.01
ADAM_STEP = 10

VMEM_LIMIT = 56 * 1024 * 1024
MESH_ID = pl.DeviceIdType.MESH


def _cparams(sem=None):
    return pltpu.CompilerParams(dimension_semantics=sem, vmem_limit_bytes=VMEM_LIMIT)


def _dot(a, b):
    return jnp.dot(a, b, preferred_element_type=F32)


def _dot_nt(a, b):
    return lax.dot_general(a, b, (((1,), (1,)), ((), ())), preferred_element_type=F32)


def _dot_tn(a, b):
    return lax.dot_general(a, b, (((0,), (0,)), ((), ())), preferred_element_type=F32)


def _rms_fwd(x, g):
    r = lax.rsqrt(jnp.mean(x * x, axis=-1, keepdims=True) + EPS)
    return x * r * g


def _rms_bwd(dh, x, g):
    r = lax.rsqrt(jnp.mean(x * x, axis=-1, keepdims=True) + EPS)
    xh = x * r
    dg = jnp.sum(dh * xh, axis=0, keepdims=True)
    dxh = dh * g
    dx = r * (dxh - xh * jnp.mean(dxh * xh, axis=-1, keepdims=True))
    return dx, dg


def _sigmoid(x):
    return 1.0 / (1.0 + jnp.exp(-x))


FFN_TM = 512
FFN_TF = 1408


def _ffn_fwd(x, g, wgt, wut, wd, name, after=None):
    tm, tf = FFN_TM, FFN_TF
    nj = D_FF // tf
    deps = [] if after is None else [after]

    def body(x_ref, g_ref, wg_ref, wu_ref, wd_ref, *rest):
        xo_ref, h_ref, a_ref, b_ref, h_s, acc = rest[len(deps):]
        j = pl.program_id(1)

        @pl.when(j == 0)
        def _():
            h = _rms_fwd(x_ref[...], g_ref[...]).astype(BF16)
            h_s[...] = h
            h_ref[...] = h
            acc[...] = jnp.zeros_like(acc)

        h = h_s[...]
        a = _dot_nt(h, wg_ref[...])
        b = _dot_nt(h, wu_ref[...])
        a_ref[...] = a.astype(BF16)
        b_ref[...] = b.astype(BF16)
        s = (a * _sigmoid(a) * b).astype(BF16)
        acc[...] += _dot(s, wd_ref[...])

        @pl.when(j == nj - 1)
        def _():
            xo_ref[...] = x_ref[...] + 0.5 * acc[...]

    return pl.pallas_call(
        body, name=name, grid=(SEQ // tm, nj),
        in_specs=[pl.BlockSpec((tm, D_MODEL), lambda i, j: (i, 0)),
                  pl.BlockSpec((1, D_MODEL), lambda i, j: (0, 0)),
                  pl.BlockSpec((tf, D_MODEL), lambda i, j: (j, 0)),
                  pl.BlockSpec((tf, D_MODEL), lambda i, j: (j, 0)),
                  pl.BlockSpec((tf, D_MODEL), lambda i, j: (j, 0))] + [pl.BlockSpec(memory_space=pl.ANY)] * len(deps),
        out_specs=[pl.BlockSpec((tm, D_MODEL), lambda i, j: (i, 0)),
                   pl.BlockSpec((tm, D_MODEL), lambda i, j: (i, 0)),
                   pl.BlockSpec((tm, tf), lambda i, j: (i, j)),
                   pl.BlockSpec((tm, tf), lambda i, j: (i, j))],
        out_shape=[jax.ShapeDtypeStruct((SEQ, D_MODEL), F32), jax.ShapeDtypeStruct((SEQ, D_MODEL), BF16),
                   jax.ShapeDtypeStruct((SEQ, D_FF), BF16), jax.ShapeDtypeStruct((SEQ, D_FF), BF16)],
        scratch_shapes=[pltpu.VMEM((tm, D_MODEL), BF16), pltpu.VMEM((tm, D_MODEL), F32)],
        compiler_params=_cparams(("parallel", "arbitrary")),
    )(x, g, wgt, wut, wd, *deps)


def _ffn_bwd_act(dxo, x, g, a, b, wgt, wut, wd, name):
    tm, tf = FFN_TM // 2, FFN_TF
    nj = D_FF // tf

    def body(dxo_ref, x_ref, g_ref, a_ref, b_ref, wg_ref, wu_ref, wd_ref,
             dx_ref, da_ref, db_ref, s_ref, df_ref, dg_ref, df_s, acc):
        i = pl.program_id(0)
        j = pl.program_id(1)

        @pl.when(j == 0)
        def _():
            df = (0.5 * dxo_ref[...]).astype(BF16)
            df_s[...] = df
            df_ref[...] = df
            acc[...] = jnp.zeros_like(acc)

        ds = _dot_nt(df_s[...], wd_ref[...])
        av = a_ref[...].astype(F32)
        bv = b_ref[...].astype(F32)
        sig = _sigmoid(av)
        sl = av * sig
        s_ref[...] = (sl * bv).astype(BF16)
        db = (ds * sl).astype(BF16)
        da = (ds * bv * (sig * (1.0 + av * (1.0 - sig)))).astype(BF16)
        da_ref[...] = da
        db_ref[...] = db
        acc[...] += _dot(da, wg_ref[...]) + _dot(db, wu_ref[...])

        @pl.when(j == nj - 1)
        def _():
            dx, dg = _rms_bwd(acc[...], x_ref[...], g_ref[...])
            dx_ref[...] = dxo_ref[...] + dx

            @pl.when(i == 0)
            def _():
                dg_ref[...] = dg

            @pl.when(i != 0)
            def _():
                dg_ref[...] += dg

    row = lambda i, j: (i, 0)
    col = lambda i, j: (j, 0)
    tile = lambda i, j: (i, j)
    return pl.pallas_call(
        body, name=name, grid=(SEQ // tm, nj),
        in_specs=[pl.BlockSpec((tm, D_MODEL), row), pl.BlockSpec((tm, D_MODEL), row),
                  pl.BlockSpec((1, D_MODEL), lambda i, j: (0, 0)),
                  pl.BlockSpec((tm, tf), tile), pl.BlockSpec((tm, tf), tile),
                  pl.BlockSpec((tf, D_MODEL), col), pl.BlockSpec((tf, D_MODEL), col), pl.BlockSpec((tf, D_MODEL), col)],
        out_specs=[pl.BlockSpec((tm, D_MODEL), row),
                   pl.BlockSpec((tm, tf), tile), pl.BlockSpec((tm, tf), tile), pl.BlockSpec((tm, tf), tile),
                   pl.BlockSpec((tm, D_MODEL), row),
                   pl.BlockSpec((1, D_MODEL), lambda i, j: (0, 0))],
        out_shape=[jax.ShapeDtypeStruct((SEQ, D_MODEL), F32),
                   jax.ShapeDtypeStruct((SEQ, D_FF), BF16), jax.ShapeDtypeStruct((SEQ, D_FF), BF16),
                   jax.ShapeDtypeStruct((SEQ, D_FF), BF16),
                   jax.ShapeDtypeStruct((SEQ, D_MODEL), BF16),
                   jax.ShapeDtypeStruct((1, D_MODEL), F32)],
        scratch_shapes=[pltpu.VMEM((tm, D_MODEL), BF16), pltpu.VMEM((tm, D_MODEL), F32)],
        compiler_params=_cparams(("arbitrary", "arbitrary")),
    )(dxo, x, g, a, b, wgt, wut, wd)


def _mm_tn(pairs, name, tmm=256):
    m = pairs[0][0].shape[1]
    n_pairs = len(pairs)

    def body(*refs):
        ins, outs = refs[:2 * n_pairs], refs[2 * n_pairs:]
        for p in range(n_pairs):
            outs[p][...] = _dot_tn(ins[2 * p][...], ins[2 * p + 1][...]).astype(BF16)

    in_specs, out_specs, out_shape, args = [], [], [], []
    for a, b in pairs:
        n = b.shape[1]
        in_specs += [pl.BlockSpec((SEQ, tmm), lambda i: (0, i)), pl.BlockSpec((SEQ, n), lambda i: (0, 0))]
        out_specs.append(pl.BlockSpec((tmm, n), lambda i: (i, 0)))
        out_shape.append(jax.ShapeDtypeStruct((m, n), BF16))
        args += [a, b]
    return pl.pallas_call(body, name=name, grid=(m // tmm,), in_specs=in_specs, out_specs=out_specs,
                          out_shape=out_shape, compiler_params=_cparams(("parallel",)))(*args)


MIX_TM = 256


def _mixin_fwd(x, g, wint):
    tm = MIX_TM

    def body(x_ref, g_ref, w_ref, h_ref, q_ref, k_ref, v_ref, u_ref):
        h = _rms_fwd(x_ref[...], g_ref[...]).astype(BF16)
        h_ref[...] = h
        proj = _dot_nt(h, w_ref[...])
        q_ref[...] = proj[:, :ATTN_WIDTH].T
        k_ref[...] = proj[:, ATTN_WIDTH:ATTN_WIDTH + KV_WIDTH]
        v_ref[...] = proj[:, ATTN_WIDTH + KV_WIDTH:ATTN_WIDTH + 2 * KV_WIDTH]
        u_ref[...] = proj[:, ATTN_WIDTH + 2 * KV_WIDTH:]

    row = lambda i: (i, 0)
    return pl.pallas_call(
        body, name="mixin_fwd", grid=(SEQ // tm,),
        in_specs=[pl.BlockSpec((tm, D_MODEL), row), pl.BlockSpec((1, D_MODEL), lambda i: (0, 0)),
                  pl.BlockSpec((IN_WIDTH, D_MODEL), lambda i: (0, 0))],
        out_specs=[pl.BlockSpec((tm, D_MODEL), row), pl.BlockSpec((ATTN_WIDTH, tm), lambda i: (0, i)),
                   pl.BlockSpec((tm, KV_WIDTH), row), pl.BlockSpec((tm, KV_WIDTH), row),
                   pl.BlockSpec((tm, SSM_WIDTH), row)],
        out_shape=[jax.ShapeDtypeStruct((SEQ, D_MODEL), BF16), jax.ShapeDtypeStruct((ATTN_WIDTH, SEQ), F32),
                   jax.ShapeDtypeStruct((SEQ, KV_WIDTH), F32), jax.ShapeDtypeStruct((SEQ, KV_WIDTH), F32),
                   jax.ShapeDtypeStruct((SEQ, SSM_WIDTH), F32)],
        compiler_params=_cparams(("parallel",)),
    )(x, g, wint)


def _mixin_bwd(dqt, dk, dv, du, wint, x, g, dres):
    tm = MIX_TM

    def body(dq_ref, dk_ref, dv_ref, du_ref, w_ref, x_ref, g_ref, dres_ref, dx_ref, dp_ref, dg_ref):
        i = pl.program_id(0)
        dp = jnp.concatenate([dq_ref[...].T, dk_ref[...], dv_ref[...], du_ref[...]], axis=-1).astype(BF16)
        dp_ref[...] = dp
        dh = _dot(dp, w_ref[...])
        dx, dg = _rms_bwd(dh, x_ref[...], g_ref[...])
        dx_ref[...] = dres_ref[...] + dx

        @pl.when(i == 0)
        def _():
            dg_ref[...] = dg

        @pl.when(i != 0)
        def _():
            dg_ref[...] += dg

    row = lambda i: (i, 0)
    const = lambda i: (0, 0)
    return pl.pallas_call(
        body, name="mixin_bwd", grid=(SEQ // tm,),
        in_specs=[pl.BlockSpec((ATTN_WIDTH, tm), lambda i: (0, i)), pl.BlockSpec((tm, KV_WIDTH), row),
                  pl.BlockSpec((tm, KV_WIDTH), row), pl.BlockSpec((tm, SSM_WIDTH), row),
                  pl.BlockSpec((IN_WIDTH, D_MODEL), const), pl.BlockSpec((tm, D_MODEL), row),
                  pl.BlockSpec((1, D_MODEL), const), pl.BlockSpec((tm, D_MODEL), row)],
        out_specs=[pl.BlockSpec((tm, D_MODEL), row), pl.BlockSpec((tm, IN_WIDTH), row),
                   pl.BlockSpec((1, D_MODEL), const)],
        out_shape=[jax.ShapeDtypeStruct((SEQ, D_MODEL), F32), jax.ShapeDtypeStruct((SEQ, IN_WIDTH), BF16),
                   jax.ShapeDtypeStruct((1, D_MODEL), F32)],
        compiler_params=_cparams(("arbitrary",)),
    )(dqt, dk, dv, du, wint, x, g, dres)


N_QBLOCKS = SEQ // WINDOW
GROUP = ATTN_HEADS // KV_HEADS
SCALE = HEAD_DIM ** -0.5


def _alibi_slope(h):
    return 2.0 ** (-8.0 * (h + 1) / ATTN_HEADS)


def _window_masks(n):
    s_idx = lax.broadcasted_iota(jnp.int32, (3 * WINDOW, WINDOW), 0)
    t_idx = lax.broadcasted_iota(jnp.int32, (3 * WINDOW, WINDOW), 1)
    absrel = jnp.abs(s_idx - WINDOW - t_idx)
    key_pos = n * WINDOW - WINDOW + s_idx
    valid = (absrel <= WINDOW) & (key_pos >= 0) & (key_pos < SEQ)
    return absrel.astype(F32), valid


def _group_cols(ref, r0, gi):
    return jnp.concatenate(
        [ref[(gi * GROUP + hh) * HEAD_DIM:(gi * GROUP + hh + 1) * HEAD_DIM, pl.ds(r0, WINDOW)].astype(BF16)
         for hh in range(GROUP)], axis=1)


def _group_probs(qgt, kw, absrel, valid, gi, sk_ref):
    bias = jnp.concatenate([jnp.where(valid, -_alibi_slope(gi * GROUP + hh) * absrel, NEG_INF)
                            for hh in range(GROUP)], axis=1)
    sink = jnp.concatenate([jnp.full((1, WINDOW), sk_ref[0, gi * GROUP + hh], F32) for hh in range(GROUP)], axis=1)
    s = _dot(kw, qgt) * SCALE + bias
    m = jnp.maximum(jnp.max(s, axis=0, keepdims=True), sink)
    p = jnp.exp(s - m)
    ps = jnp.exp(sink - m)
    inv = 1.0 / (jnp.sum(p, axis=0, keepdims=True) + ps)
    return p * inv, ps * inv


def _attn_fwd(qt, kp, vp, sinks):
    def body(sk_ref, qt_ref, kp_ref, vp_ref, o_ref):
        def blk(n, carry):
            r0 = pl.multiple_of(n * WINDOW, WINDOW)
            absrel, valid = _window_masks(n)
            for gi in range(KV_HEADS):
                kw = kp_ref[pl.ds(r0, 3 * WINDOW), gi * HEAD_DIM:(gi + 1) * HEAD_DIM].astype(BF16)
                vw = vp_ref[pl.ds(r0, 3 * WINDOW), gi * HEAD_DIM:(gi + 1) * HEAD_DIM].astype(BF16)
                pr, _ = _group_probs(_group_cols(qt_ref, r0, gi), kw, absrel, valid, gi, sk_ref)
                og = _dot_tn(pr.astype(BF16), vw)
                for hh in range(GROUP):
                    h = gi * GROUP + hh
                    o_ref[pl.ds(r0, WINDOW), h * HEAD_DIM:(h + 1) * HEAD_DIM] = og[hh * WINDOW:(hh + 1) * WINDOW]
            return carry

        lax.fori_loop(0, N_QBLOCKS, blk, 0)

    vmem = pl.BlockSpec(memory_space=pltpu.VMEM)
    return pl.pallas_call(
        body, name="attn_fwd",
        in_specs=[pl.BlockSpec(memory_space=pltpu.SMEM), vmem, vmem, vmem], out_specs=vmem,
        out_shape=jax.ShapeDtypeStruct((SEQ, ATTN_WIDTH), F32),
        compiler_params=_cparams(),
    )(sinks, qt, kp, vp)


def _attn_bwd(qt, kp, vp, sinks, dot_):
    def body(sk_ref, qt_ref, kp_ref, vp_ref, dot_ref, dqt_ref, dkp_ref, dvp_ref, dsk_ref, dsk_acc):
        dkp_ref[...] = jnp.zeros_like(dkp_ref)
        dvp_ref[...] = jnp.zeros_like(dvp_ref)
        dsk_acc[...] = jnp.zeros_like(dsk_acc)

        def blk(n, carry):
            r0 = pl.multiple_of(n * WINDOW, WINDOW)
            absrel, valid = _window_masks(n)
            for gi in range(KV_HEADS):
                gcols = slice(gi * HEAD_DIM, (gi + 1) * HEAD_DIM)
                kw = kp_ref[pl.ds(r0, 3 * WINDOW), gcols].astype(BF16)
                vw = vp_ref[pl.ds(r0, 3 * WINDOW), gcols].astype(BF16)
                qgt = _group_cols(qt_ref, r0, gi)
                dogt = _group_cols(dot_ref, r0, gi)
                pr, psink = _group_probs(qgt, kw, absrel, valid, gi, sk_ref)
                dp = _dot(vw, dogt)
                delta = jnp.sum(pr * dp, axis=0, keepdims=True)
                ds = (pr * (dp - delta)).astype(BF16)
                dsk_acc[gi:gi + 1, :] += -(psink * delta)
                dqgt = _dot_tn(kw, ds) * SCALE
                for hh in range(GROUP):
                    h = gi * GROUP + hh
                    dqt_ref[h * HEAD_DIM:(h + 1) * HEAD_DIM, pl.ds(r0, WINDOW)] = dqgt[:, hh * WINDOW:(hh + 1) * WINDOW]
                dkp_ref[pl.ds(r0, 3 * WINDOW), gcols] += _dot_nt(ds, qgt) * SCALE
                dvp_ref[pl.ds(r0, 3 * WINDOW), gcols] += _dot_nt(pr.astype(BF16), dogt)
            return carry

        lax.fori_loop(0, N_QBLOCKS, blk, 0)
        for h in range(ATTN_HEADS):
            gi, hh = divmod(h, GROUP)
            dsk_ref[:, h:h + 1] = jnp.sum(dsk_acc[gi:gi + 1, hh * WINDOW:(hh + 1) * WINDOW], axis=1, keepdims=True)

    vmem = pl.BlockSpec(memory_space=pltpu.VMEM)
    return pl.pallas_call(
        body, name="attn_bwd",
        in_specs=[pl.BlockSpec(memory_space=pltpu.SMEM), vmem, vmem, vmem, vmem],
        out_specs=[vmem, vmem, vmem, vmem],
        out_shape=[jax.ShapeDtypeStruct((ATTN_WIDTH, SEQ), F32),
                   jax.ShapeDtypeStruct((SEQ + 2 * WINDOW, KV_WIDTH), F32),
                   jax.ShapeDtypeStruct((SEQ + 2 * WINDOW, KV_WIDTH), F32),
                   jax.ShapeDtypeStruct((1, ATTN_HEADS), F32)],
        scratch_shapes=[pltpu.VMEM((KV_HEADS, GROUP * WINDOW), F32)],
        compiler_params=_cparams(),
    )(sinks, qt, kp, vp, dot_)


HALF_LANES = LANES // 2
BLOCK_ROWS = 32


def _embed_block(bt, q):
    z = jnp.zeros((16, HALF_LANES), bt.dtype)
    blk = jnp.concatenate([jnp.concatenate([bt[:16], z], axis=1), jnp.concatenate([z, bt[16:]], axis=1)], axis=0)
    parts = [jnp.zeros((BLOCK_ROWS * q, LANES), bt.dtype)] if q else []
    parts.append(blk)
    if q < 3:
        parts.append(jnp.zeros((BLOCK_ROWS * (3 - q), LANES), bt.dtype))
    return jnp.concatenate(parts, axis=0)


def _extract_block(m, q):
    blk = m[BLOCK_ROWS * q:BLOCK_ROWS * (q + 1)]
    return jnp.concatenate([blk[:16, :HALF_LANES], blk[16:, HALF_LANES:]], axis=0)


def _ssm_prep(lam_re, lam_im, log_dt, bt_re, bt_im, c_re, c_im):
    nb = 2 * N_LANE_BLOCKS

    def body(lr_ref, li_ref, ldt_ref, btr_ref, bti_ref, ctr_ref, cti_ref, ar_ref, ai_ref, bb_ref, cc_ref):
        lr = jnp.minimum(lr_ref[...], LAMBDA_RE_MAX)
        li = li_ref[...]
        dt = jnp.exp(ldt_ref[...])
        mag = jnp.exp(lr * dt)
        ar = mag * jnp.cos(li * dt)
        ai = mag * jnp.sin(li * dt)
        den = lr * lr + li * li
        cr = ((ar - 1.0) * lr + ai * li) / den
        ci = (ai * lr - (ar - 1.0) * li) / den
        ar_ref[...] = ar
        ai_ref[...] = ai
        for i in range(nb):
            q = i % 4
            rows = slice(BLOCK_ROWS * i, BLOCK_ROWS * (i + 1))
            br = _embed_block(btr_ref[rows, :], q)
            bi = _embed_block(bti_ref[rows, :], q)
            cri, cii = cr[i:i + 1, :], ci[i:i + 1, :]
            bb_ref[i] = jnp.concatenate([cri * br - cii * bi, cri * bi + cii * br], axis=1).astype(BF16)
            cc_ref[i] = jnp.concatenate([_embed_block(ctr_ref[rows, :], q).T,
                                         -_embed_block(cti_ref[rows, :], q).T], axis=0).astype(BF16)

    return pl.pallas_call(
        body, name="ssm_prep",
        out_shape=[jax.ShapeDtypeStruct((nb, LANES), F32), jax.ShapeDtypeStruct((nb, LANES), F32),
                   jax.ShapeDtypeStruct((nb, LANES, 2 * LANES), BF16),
                   jax.ShapeDtypeStruct((nb, 2 * LANES, LANES), BF16)],
        compiler_params=_cparams(),
    )(lam_re, lam_im, log_dt, bt_re, bt_im, c_re, c_im)


def _ssm_prep_bwd(lam_re, lam_im, log_dt, bt_re, bt_im, dar, dai, dbb, dcc):
    nb = 2 * N_LANE_BLOCKS

    def body(lr_ref, li_ref, ldt_ref, btr_ref, bti_ref, dar_ref, dai_ref, dbb_ref, dcc_ref,
             glr_ref, gli_ref, gdt_ref, gbr_ref, gbi_ref, gcre_ref, gcim_ref, gcr_s, gci_s):
        lam = lr_ref[...]
        lr = jnp.minimum(lam, LAMBDA_RE_MAX)
        li = li_ref[...]
        dt = jnp.exp(ldt_ref[...])
        mag = jnp.exp(lr * dt)
        cs = jnp.cos(li * dt)
        sn = jnp.sin(li * dt)
        ar = mag * cs
        ai = mag * sn
        den = lr * lr + li * li
        nr = (ar - 1.0) * lr + ai * li
        ni = ai * lr - (ar - 1.0) * li
        cr = nr / den
        ci = ni / den
        for i in range(nb):
            q = i % 4
            rows = slice(BLOCK_ROWS * i, BLOCK_ROWS * (i + 1))
            br = _embed_block(btr_ref[rows, :], q)
            bi = _embed_block(bti_ref[rows, :], q)
            gbbr = dbb_ref[i, :, :LANES]
            gbbi = dbb_ref[i, :, LANES:]
            cri, cii = cr[i:i + 1, :], ci[i:i + 1, :]
            gcr_s[i:i + 1, :] = jnp.sum(gbbr * br + gbbi * bi, axis=0, keepdims=True)
            gci_s[i:i + 1, :] = jnp.sum(gbbi * br - gbbr * bi, axis=0, keepdims=True)
            gbr_ref[rows, :] = _extract_block(cri * gbbr + cii * gbbi, q)
            gbi_ref[rows, :] = _extract_block(cri * gbbi - cii * gbbr, q)
            gcre_ref[rows, :] = _extract_block(dcc_ref[i, :LANES, :].T, q)
            gcim_ref[rows, :] = -_extract_block(dcc_ref[i, LANES:, :].T, q)
        g_cr = gcr_s[...]
        g_ci = gci_s[...]
        g_nr = g_cr / den
        g_ni = g_ci / den
        g_den = -(g_cr * nr + g_ci * ni) / (den * den)
        g_ar = dar_ref[...] + g_nr * lr - g_ni * li
        g_ai = dai_ref[...] + g_nr * li + g_ni * lr
        g_lr = g_nr * (ar - 1.0) + g_ni * ai + g_den * 2.0 * lr
        g_li = g_nr * ai - g_ni * (ar - 1.0) + g_den * 2.0 * li
        g_mag = g_ar * cs + g_ai * sn
        g_th = (g_ai * cs - g_ar * sn) * mag
        g_lr = g_lr + g_mag * mag * dt
        g_li = g_li + g_th * dt
        g_dt = g_mag * mag * lr + g_th * li
        glr_ref[...] = jnp.where(lam < LAMBDA_RE_MAX, g_lr, 0.0)
        gli_ref[...] = g_li
        gl = g_dt * dt
        half = LANES // 2
        gdt_ref[:, 0:1] = jnp.sum(gl[:, :half], axis=1, keepdims=True)
        gdt_ref[:, 1:2] = jnp.sum(gl[:, half:], axis=1, keepdims=True)

    rows_shape = jax.ShapeDtypeStruct((nb * BLOCK_ROWS, HALF_LANES), F32)
    return pl.pallas_call(
        body, name="ssm_prep_bwd",
        out_shape=[jax.ShapeDtypeStruct((nb, LANES), F32), jax.ShapeDtypeStruct((nb, LANES), F32),
                   jax.ShapeDtypeStruct((nb, 2), F32), rows_shape, rows_shape, rows_shape, rows_shape],
        scratch_shapes=[pltpu.VMEM((nb, LANES), F32), pltpu.VMEM((nb, LANES), F32)],
        compiler_params=_cparams(),
    )(lam_re, lam_im, log_dt, bt_re, bt_im, dar, dai, dbb, dcc)


def _cmul(ar, ai, br, bi):
    return ar * br - ai * bi, ar * bi + ai * br


def _interleave_rows(src_ref, dst_ref):
    def step(j, carry):
        dst_ref[pl.ds(pl.multiple_of(j * 8, 8), 8), :] = src_ref[pl.ds(j, 8, stride=SCAN_CHUNK), :]
        return carry
    lax.fori_loop(0, SCAN_CHUNK, step, 0, unroll=4)


def _deinterleave_rows(src_ref, dst_ref):
    def step(j, carry):
        dst_ref[pl.ds(j, 8, stride=SCAN_CHUNK), :] = src_ref[pl.ds(pl.multiple_of(j * 8, 8), 8), :]
        return carry
    lax.fori_loop(0, SCAN_CHUNK, step, 0, unroll=4)


def _scan_inplace(re_ref, im_ref, a_re, a_im, reverse):
    nq = len(a_re)
    ch = SCAN_CHUNK
    ab_re = [jnp.broadcast_to(a, (8, LANES)) for a in a_re]
    ab_im = [jnp.broadcast_to(a, (8, LANES)) for a in a_im]

    def rows(j):
        jj = (ch - 1 - j) if reverse else j
        return pl.ds(pl.multiple_of(jj * 8, 8), 8)

    def sweep(init, store):
        def step(j, st):
            out = []
            r = rows(j)
            for qi in range(nq):
                xr, xi = st[2 * qi], st[2 * qi + 1]
                pr, pi = _cmul(ab_re[qi], ab_im[qi], xr, xi)
                xr = pr + re_ref[qi, r, :]
                xi = pi + im_ref[qi, r, :]
                if store:
                    re_ref[qi, r, :] = xr
                    im_ref[qi, r, :] = xi
                out += [xr, xi]
            return tuple(out)
        return lax.fori_loop(0, ch, step, tuple(init), unroll=2)

    zeros = [jnp.zeros((8, LANES), F32)] * (2 * nq)
    finals = sweep(zeros, store=False)

    row_id = lax.broadcasted_iota(jnp.int32, (8, LANES), 0)
    carries = []
    for qi in range(nq):
        pr, pi = ab_re[qi], ab_im[qi]
        for _ in range(8):
            pr, pi = _cmul(pr, pi, pr, pi)
        fr, fi = finals[2 * qi], finals[2 * qi + 1]
        sr = jnp.zeros((8, LANES), F32)
        si = jnp.zeros((8, LANES), F32)
        for _ in range(7):
            tr, ti = _cmul(pr, pi, sr, si)
            tr, ti = tr + fr, ti + fi
            if reverse:
                sr = jnp.where(row_id == 7, 0.0, pltpu.roll(tr, 7, axis=0))
                si = jnp.where(row_id == 7, 0.0, pltpu.roll(ti, 7, axis=0))
            else:
                sr = jnp.where(row_id == 0, 0.0, pltpu.roll(tr, 1, axis=0))
                si = jnp.where(row_id == 0, 0.0, pltpu.roll(ti, 1, axis=0))
        carries += [sr, si]
    sweep(carries, store=True)


SSM_Q = 4


def _ssm_fwd(u, are, aim, bb, cc, dskip, after=None):
    nq = SSM_Q
    deps = [] if after is None else [after]

    def body(u_ref, ar_ref, ai_ref, bb_ref, cc_ref, d_ref, *rest):
        y_ref, x_ref, sre, sim, up, yp = rest[len(deps):]
        _interleave_rows(u_ref, up)
        uf = up[...]
        ub = uf.astype(BF16)
        yp[...] = d_ref[...] * uf
        for d in range(2):
            for qi in range(nq):
                bu = _dot(ub, bb_ref[d, qi])
                sre[qi] = bu[:, :LANES]
                sim[qi] = bu[:, LANES:]
            _scan_inplace(sre, sim, [ar_ref[d, qi] for qi in range(nq)], [ai_ref[d, qi] for qi in range(nq)],
                          reverse=(d == 1))
            for qi in range(nq):
                xb = jnp.concatenate([sre[qi], sim[qi]], axis=1).astype(BF16)
                x_ref[d, qi] = xb
                yp[...] += _dot(xb, cc_ref[d, qi])
        _deinterleave_rows(yp, y_ref)

    blk4 = lambda k: (0, k, 0, 0)
    return pl.pallas_call(
        body, name="ssm_fwd", grid=(SSM_WIDTH // LANES,),
        in_specs=[pl.BlockSpec((SEQ, LANES), lambda k: (0, k)),
                  pl.BlockSpec((2, nq, 1, LANES), blk4), pl.BlockSpec((2, nq, 1, LANES), blk4),
                  pl.BlockSpec((2, nq, LANES, 2 * LANES), blk4), pl.BlockSpec((2, nq, 2 * LANES, LANES), blk4),
                  pl.BlockSpec((1, LANES), lambda k: (0, k))] + [pl.BlockSpec(memory_space=pl.ANY)] * len(deps),
        out_specs=[pl.BlockSpec((SEQ, LANES), lambda k: (0, k)), pl.BlockSpec((2, nq, SEQ, 2 * LANES), blk4)],
        out_shape=[jax.ShapeDtypeStruct((SEQ, SSM_WIDTH), F32),
                   jax.ShapeDtypeStruct((2, N_LANE_BLOCKS, SEQ, 2 * LANES), BF16)],
        scratch_shapes=[pltpu.VMEM((nq, SEQ, LANES), F32), pltpu.VMEM((nq, SEQ, LANES), F32),
                        pltpu.VMEM((SEQ, LANES), F32), pltpu.VMEM((SEQ, LANES), F32)],
        compiler_params=_cparams(("parallel",)),
    )(u, are, aim, bb, cc, dskip, *deps)


def _ssm_bwd(dy, u, x, are, aim, bb, cc, dskip, after=None):
    nq = SSM_Q
    body_rows = SEQ - 8
    deps = [] if after is None else [after]

    def body(dy_ref, u_ref, x_ref, ar_ref, ai_ref, bb_ref, cc_ref, d_ref, *rest):
        du_ref, dd_ref, dcc_ref, dbb_ref, dar_ref, dai_ref, sre, sim, up, dyp, dup = rest[len(deps):]
        _interleave_rows(u_ref, up)
        _interleave_rows(dy_ref, dyp)
        dyf = dyp[...]
        uf = up[...]
        dyb = dyf.astype(BF16)
        ub = uf.astype(BF16)
        dd_ref[...] = jnp.sum(dyf * uf, axis=0, keepdims=True)
        dup[...] = d_ref[...] * dyf
        row8 = lax.broadcasted_iota(jnp.int32, (8, LANES), 0)
        for d in range(2):
            for qi in range(nq):
                dx = _dot_nt(dyb, cc_ref[d, qi])
                sre[qi] = dx[:, :LANES]
                sim[qi] = dx[:, LANES:]
                dcc_ref[d, qi] = _dot_tn(x_ref[d, qi], dyb)
            _scan_inplace(sre, sim, [ar_ref[d, qi] for qi in range(nq)], [-ai_ref[d, qi] for qi in range(nq)],
                          reverse=(d == 0))
            for qi in range(nq):
                gr = sre[qi]
                gi = sim[qi]
                xrf = x_ref[d, qi, :, :LANES].astype(F32)
                xif = x_ref[d, qi, :, LANES:].astype(F32)
                if d == 0:
                    g_main_r, g_main_i = gr[8:], gi[8:]
                    x_main_r, x_main_i = xrf[:body_rows], xif[:body_rows]
                    g_edge_r, g_edge_i = gr[:8], gi[:8]
                    x_edge_r = jnp.where(row8 == 0, 0.0, pltpu.roll(xrf[body_rows:], 1, axis=0))
                    x_edge_i = jnp.where(row8 == 0, 0.0, pltpu.roll(xif[body_rows:], 1, axis=0))
                else:
                    g_main_r, g_main_i = gr[:body_rows], gi[:body_rows]
                    x_main_r, x_main_i = xrf[8:], xif[8:]
                    g_edge_r, g_edge_i = gr[body_rows:], gi[body_rows:]
                    x_edge_r = jnp.where(row8 == 7, 0.0, pltpu.roll(xrf[:8], 7, axis=0))
                    x_edge_i = jnp.where(row8 == 7, 0.0, pltpu.roll(xif[:8], 7, axis=0))
                dar_ref[d, qi] = (jnp.sum(g_main_r * x_main_r + g_main_i * x_main_i, axis=0, keepdims=True)
                                  + jnp.sum(g_edge_r * x_edge_r + g_edge_i * x_edge_i, axis=0, keepdims=True))
                dai_ref[d, qi] = (jnp.sum(g_main_i * x_main_r - g_main_r * x_main_i, axis=0, keepdims=True)
                                  + jnp.sum(g_edge_i * x_edge_r - g_edge_r * x_edge_i, axis=0, keepdims=True))
                gb = jnp.concatenate([gr, gi], axis=1).astype(BF16)
                dup[...] += _dot_nt(gb, bb_ref[d, qi])
                dbb_ref[d, qi] = _dot_tn(ub, gb)
        _deinterleave_rows(dup, du_ref)

    blk4 = lambda k: (0, k, 0, 0)
    col = lambda k: (0, k)
    bb_spec = pl.BlockSpec((2, nq, LANES, 2 * LANES), blk4)
    cc_spec = pl.BlockSpec((2, nq, 2 * LANES, LANES), blk4)
    a_spec = pl.BlockSpec((2, nq, 1, LANES), blk4)
    x_spec = pl.BlockSpec((2, nq, SEQ, 2 * LANES), blk4)
    a_shape = jax.ShapeDtypeStruct((2, N_LANE_BLOCKS, 1, LANES), F32)
    return pl.pallas_call(
        body, name="ssm_bwd", grid=(SSM_WIDTH // LANES,),
        in_specs=[pl.BlockSpec((SEQ, LANES), col), pl.BlockSpec((SEQ, LANES), col), x_spec,
                  a_spec, a_spec, bb_spec, cc_spec, pl.BlockSpec((1, LANES), col)]
        + [pl.BlockSpec(memory_space=pl.ANY)] * len(deps),
        out_specs=[pl.BlockSpec((SEQ, LANES), col), pl.BlockSpec((1, LANES), col),
                   cc_spec, bb_spec, a_spec, a_spec],
        out_shape=[jax.ShapeDtypeStruct((SEQ, SSM_WIDTH), F32), jax.ShapeDtypeStruct((1, SSM_WIDTH), F32),
                   jax.ShapeDtypeStruct((2, N_LANE_BLOCKS, 2 * LANES, LANES), F32),
                   jax.ShapeDtypeStruct((2, N_LANE_BLOCKS, LANES, 2 * LANES), F32), a_shape, a_shape],
        scratch_shapes=[pltpu.VMEM((nq, SEQ, LANES), F32), pltpu.VMEM((nq, SEQ, LANES), F32),
                        pltpu.VMEM((SEQ, LANES), F32), pltpu.VMEM((SEQ, LANES), F32), pltpu.VMEM((SEQ, LANES), F32)],
        compiler_params=_cparams(("parallel",)),
    )(dy, u, x, are, aim, bb, cc, dskip, *deps)


GELU_C = 0.7978845608028654
GELU_K = 0.044715


def _gelu(y):
    return 0.5 * y * (1.0 + jnp.tanh(GELU_C * (y + GELU_K * y * y * y)))


def _gelu_grad(y):
    t = jnp.tanh(GELU_C * (y + GELU_K * y * y * y))
    return 0.5 * (1.0 + t) + 0.5 * y * (1.0 - t * t) * GELU_C * (1.0 + 3.0 * GELU_K * y * y)


def _mixout_fwd(o, y, glu_w, glu_b, gan, gsn, wout, x1):
    tm = MIX_TM

    def body(o_ref, y_ref, gw_ref, gb_ref, gan_ref, gsn_ref, w_ref, x1_ref, x2_ref, mx_ref):
        yg = _gelu(y_ref[...])
        z = _dot(yg.astype(BF16), gw_ref[...]) + gb_ref[...]
        so = yg * _sigmoid(z)
        na = _rms_fwd(o_ref[...], gan_ref[...])
        ns = _rms_fwd(so, gsn_ref[...])
        mixed = jnp.concatenate([na, ns], axis=-1).astype(BF16)
        mx_ref[...] = mixed
        x2_ref[...] = x1_ref[...] + _dot(mixed, w_ref[...])

    row = lambda i: (i, 0)
    const = lambda i: (0, 0)
    return pl.pallas_call(
        body, name="mixout_fwd", grid=(SEQ // tm,),
        in_specs=[pl.BlockSpec((tm, ATTN_WIDTH), row), pl.BlockSpec((tm, SSM_WIDTH), row),
                  pl.BlockSpec((SSM_WIDTH, SSM_WIDTH), const), pl.BlockSpec((1, SSM_WIDTH), const),
                  pl.BlockSpec((1, ATTN_WIDTH), const), pl.BlockSpec((1, SSM_WIDTH), const),
                  pl.BlockSpec((D_MODEL, D_MODEL), const), pl.BlockSpec((tm, D_MODEL), row)],
        out_specs=[pl.BlockSpec((tm, D_MODEL), row), pl.BlockSpec((tm, D_MODEL), row)],
        out_shape=[jax.ShapeDtypeStruct((SEQ, D_MODEL), F32), jax.ShapeDtypeStruct((SEQ, D_MODEL), BF16)],
        compiler_params=_cparams(("parallel",)),
    )(o, y, glu_w, glu_b, gan, gsn, wout, x1)


def _mixout_bwd(dx2, o, y, glu_w, glu_b, gan, gsn, wout):
    tm = MIX_TM

    def body(dx2_ref, o_ref, y_ref, gw_ref, gb_ref, gan_ref, gsn_ref, w_ref,
             do_ref, dy_ref, dz_ref, yg_ref, dxb_ref, dgan_ref, dgsn_ref, dgb_ref):
        i = pl.program_id(0)
        dxb = dx2_ref[...].astype(BF16)
        dxb_ref[...] = dxb
        dmixed = _dot_nt(dxb, w_ref[...])
        do, dgan = _rms_bwd(dmixed[:, :ATTN_WIDTH], o_ref[...], gan_ref[...])
        do_ref[...] = do.T
        yv = y_ref[...]
        yg = _gelu(yv)
        ygb = yg.astype(BF16)
        yg_ref[...] = ygb
        sg = _sigmoid(_dot(ygb, gw_ref[...]) + gb_ref[...])
        dso, dgsn = _rms_bwd(dmixed[:, ATTN_WIDTH:], yg * sg, gsn_ref[...])
        dz = dso * yg * sg * (1.0 - sg)
        dzb = dz.astype(BF16)
        dz_ref[...] = dzb
        dyg = dso * sg + _dot_nt(dzb, gw_ref[...])
        dy_ref[...] = dyg * _gelu_grad(yv)
        dgb = jnp.sum(dz, axis=0, keepdims=True)

        @pl.when(i == 0)
        def _():
            dgan_ref[...] = dgan
            dgsn_ref[...] = dgsn
            dgb_ref[...] = dgb

        @pl.when(i != 0)
        def _():
            dgan_ref[...] += dgan
            dgsn_ref[...] += dgsn
            dgb_ref[...] += dgb

    row = lambda i: (i, 0)
    const = lambda i: (0, 0)
    return pl.pallas_call(
        body, name="mixout_bwd", grid=(SEQ // tm,),
        in_specs=[pl.BlockSpec((tm, D_MODEL), row), pl.BlockSpec((tm, ATTN_WIDTH), row),
                  pl.BlockSpec((tm, SSM_WIDTH), row),
                  pl.BlockSpec((SSM_WIDTH, SSM_WIDTH), const), pl.BlockSpec((1, SSM_WIDTH), const),
                  pl.BlockSpec((1, ATTN_WIDTH), const), pl.BlockSpec((1, SSM_WIDTH), const),
                  pl.BlockSpec((D_MODEL, D_MODEL), const)],
        out_specs=[pl.BlockSpec((ATTN_WIDTH, tm), lambda i: (0, i)), pl.BlockSpec((tm, SSM_WIDTH), row),
                   pl.BlockSpec((tm, SSM_WIDTH), row), pl.BlockSpec((tm, SSM_WIDTH), row),
                   pl.BlockSpec((tm, D_MODEL), row),
                   pl.BlockSpec((1, ATTN_WIDTH), const), pl.BlockSpec((1, SSM_WIDTH), const),
                   pl.BlockSpec((1, SSM_WIDTH), const)],
        out_shape=[jax.ShapeDtypeStruct((ATTN_WIDTH, SEQ), F32), jax.ShapeDtypeStruct((SEQ, SSM_WIDTH), F32),
                   jax.ShapeDtypeStruct((SEQ, SSM_WIDTH), BF16), jax.ShapeDtypeStruct((SEQ, SSM_WIDTH), BF16),
                   jax.ShapeDtypeStruct((SEQ, D_MODEL), BF16),
                   jax.ShapeDtypeStruct((1, ATTN_WIDTH), F32), jax.ShapeDtypeStruct((1, SSM_WIDTH), F32),
                   jax.ShapeDtypeStruct((1, SSM_WIDTH), F32)],
        compiler_params=_cparams(("arbitrary",)),
    )(dx2, o, y, glu_w, glu_b, gan, gsn, wout)


def _loss_head(x, g, target):
    tm = MIX_TM

    def body(x_ref, g_ref, t_ref, loss_ref, dx_ref, dg_ref):
        i = pl.program_id(0)
        xv = x_ref[...]
        gv = g_ref[...]
        err = _rms_fwd(xv, gv) - t_ref[...]
        part = jnp.broadcast_to(0.5 * jnp.sum(err * err) / D_MODEL, (1, LANES))
        dx, dg = _rms_bwd(err * (1.0 / D_MODEL), xv, gv)
        dx_ref[...] = dx

        @pl.when(i == 0)
        def _():
            loss_ref[...] = part
            dg_ref[...] = dg

        @pl.when(i != 0)
        def _():
            loss_ref[...] += part
            dg_ref[...] += dg

    row = lambda i: (i, 0)
    const = lambda i: (0, 0)
    return pl.pallas_call(
        body, name="loss_head", grid=(SEQ // tm,),
        in_specs=[pl.BlockSpec((tm, D_MODEL), row), pl.BlockSpec((1, D_MODEL), const),
                  pl.BlockSpec((tm, D_MODEL), row)],
        out_specs=[pl.BlockSpec((1, LANES), const), pl.BlockSpec((tm, D_MODEL), row),
                   pl.BlockSpec((1, D_MODEL), const)],
        out_shape=[jax.ShapeDtypeStruct((1, LANES), F32), jax.ShapeDtypeStruct((SEQ, D_MODEL), F32),
                   jax.ShapeDtypeStruct((1, D_MODEL), F32)],
        compiler_params=_cparams(("arbitrary",)),
    )(x, g, target)


def _local_step(x, target, w, p, late_weights, early_grads, after=None, midway=None):
    x1, h1, a1, b1 = _ffn_fwd(x, p["norm_ffn1"], w["wgt1"], w["wut1"], w["wd1"], "ffn1_fwd", after=after)
    h2, q, k, v, u = _mixin_fwd(x1, p["norm_mix"], w["wint"])
    kp = jnp.pad(k, ((WINDOW, WINDOW), (0, 0)))
    vp = jnp.pad(v, ((WINDOW, WINDOW), (0, 0)))
    o = _attn_fwd(q, kp, vp, p["attn_sinks"])

    lam_re = p["ssm_lambda_re"].reshape(2 * N_LANE_BLOCKS, LANES)
    lam_im = p["ssm_lambda_im"].reshape(2 * N_LANE_BLOCKS, LANES)
    log_dt = jnp.repeat(p["ssm_log_dt"].reshape(2, 32), 64, axis=-1).reshape(2 * N_LANE_BLOCKS, LANES)
    a_re, a_im, bb, cc = _ssm_prep(lam_re, lam_im, log_dt, p["ssm_b_re"], p["ssm_b_im"],
                                   p["ssm_c_re"], p["ssm_c_im"])
    shape_a = (2, N_LANE_BLOCKS, 1, LANES)
    a_re4, a_im4 = a_re.reshape(shape_a), a_im.reshape(shape_a)
    bb4 = bb.reshape(2, N_LANE_BLOCKS, LANES, 2 * LANES)
    cc4 = cc.reshape(2, N_LANE_BLOCKS, 2 * LANES, LANES)
    dskip = p["ssm_d"].T.reshape(1, SSM_WIDTH)
    y, xs = _ssm_fwd(u, a_re4, a_im4, bb4, cc4, dskip, after=None if midway is None else midway(o))

    w2 = late_weights(y)
    x2, mixed = _mixout_fwd(o, y, w2["glu"], p["ssm_glu_b"], p["attn_out_norm"], p["ssm_out_norm"], w2["wout"], x1)
    x3, h3, a3, b3 = _ffn_fwd(x2, p["norm_ffn2"], w2["wgt2"], w2["wut2"], w2["wd2"], "ffn2_fwd")

    loss, dx3, d_final = _loss_head(x3, p["final_norm"], target)
    dx2, da3, db3, s3, df3, d_n2 = _ffn_bwd_act(dx3, x2, p["norm_ffn2"], a3, b3, w2["wgt2"], w2["wut2"], w2["wd2"],
                                                "ffn2_bwd_act")
    g_wgt2, g_wut2, g_wd2 = _mm_tn([(da3, h3), (db3, h3), (s3, df3)], "ffn2_bwd_w")

    do, dy, dz, ygb, dx2b, d_gan, d_gsn, d_glub = _mixout_bwd(
        dx2, o, y, w2["glu"], p["ssm_glu_b"], p["attn_out_norm"], p["ssm_out_norm"], w2["wout"])
    (g_wout,) = _mm_tn([(mixed, dx2b)], "wout_bwd_w")
    (g_glu,) = _mm_tn([(ygb, dz)], "glu_bwd_w")
    sent = early_grads(dict(glu=g_glu, wout=g_wout, wgt2=g_wgt2, wut2=g_wut2, wd2=g_wd2))

    du, d_dskip, dcc, dbb, dar, dai = _ssm_bwd(dy, u, xs, a_re4, a_im4, bb4, cc4, dskip, after=sent)
    nb = 2 * N_LANE_BLOCKS
    g_lre, g_lim, g_ldt, g_btr, g_bti, g_cre, g_cim = _ssm_prep_bwd(
        lam_re, lam_im, log_dt, p["ssm_b_re"], p["ssm_b_im"], dar.reshape(nb, LANES), dai.reshape(nb, LANES),
        dbb.reshape(nb, LANES, 2 * LANES), dcc.reshape(nb, 2 * LANES, LANES))

    dq, dkp, dvp, d_sinks = _attn_bwd(q, kp, vp, p["attn_sinks"], do)
    dk = dkp[WINDOW:WINDOW + SEQ]
    dv = dvp[WINDOW:WINDOW + SEQ]
    dx1, dproj, d_nmix = _mixin_bwd(dq, dk, dv, du, w["wint"], x1, p["norm_mix"], dx2)
    (g_wint,) = _mm_tn([(dproj, h2)], "win_bwd_w")

    dx0, da1, db1, s1, df1, d_n1 = _ffn_bwd_act(dx1, x, p["norm_ffn1"], a1, b1, w["wgt1"], w["wut1"], w["wd1"],
                                                "ffn1_bwd_act")
    g_wgt1, g_wut1, g_wd1 = _mm_tn([(da1, h1), (db1, h1), (s1, df1)], "ffn1_bwd_w")

    big = dict(wgt1=g_wgt1, wut1=g_wut1, wd1=g_wd1, wint=g_wint)
    small = dict(
        norm_ffn1=d_n1, norm_mix=d_nmix, attn_sinks=d_sinks,
        ssm_lambda_re=g_lre.reshape(64, 64), ssm_lambda_im=g_lim.reshape(64, 64),
        ssm_log_dt=g_ldt.reshape(2, 32), ssm_b_re=g_btr, ssm_b_im=g_bti, ssm_c_re=g_cre, ssm_c_im=g_cim,
        ssm_d=d_dskip.reshape(32, 16).T, ssm_glu_b=d_glub, attn_out_norm=d_gan, ssm_out_norm=d_gsn,
        norm_ffn2=d_n2, final_norm=d_final, loss=loss)
    return loss, dx0, big, small


BIG = dict(
    wgt1=("ffn1_w_gate", 352, 1024, True), wut1=("ffn1_w_up", 352, 1024, True), wd1=("ffn1_w_down", 352, 1024, False),
    wint=("w_in", 160, 1024, True), glu=("ssm_glu_w", 64, 512, False), wout=("w_out", 128, 1024, False),
    wgt2=("ffn2_w_gate", 352, 1024, True), wut2=("ffn2_w_up", 352, 1024, True), wd2=("ffn2_w_down", 352, 1024, False))

SMALL = dict(
    norm_ffn1=(1, 1024), norm_mix=(1, 1024), attn_sinks=(1, 8), ssm_lambda_re=(64, 64), ssm_lambda_im=(64, 64),
    ssm_log_dt=(2, 32), ssm_b_re=(1024, 64), ssm_b_im=(1024, 64), ssm_c_re=(1024, 64), ssm_c_im=(1024, 64),
    ssm_d=(16, 32), ssm_glu_b=(1, 512), attn_out_norm=(1, 512), ssm_out_norm=(1, 512), norm_ffn2=(1, 1024),
    final_norm=(1, 1024), loss=(1, 128))
SMALL_TRANSPOSED = ("ssm_b_re", "ssm_b_im", "ssm_d")
SMALL_PARAMS = tuple(n for n in SMALL if n != "loss")

SMALL_PAIRS = (("ssm_lambda_re", "ssm_lambda_im"), ("ssm_c_re", "ssm_c_im"), ("ssm_b_re", "ssm_b_im"))
SMALL_VECS = ("norm_ffn1", "norm_mix", "norm_ffn2", "final_norm", "ssm_glu_b", "attn_out_norm", "ssm_out_norm")
SMALL_TILES = ("ssm_log_dt", "attn_sinks", "ssm_d", "loss")


def _small_offsets():
    off, table = 0, {}
    for re, im in SMALL_PAIRS:
        table[re] = table[im] = off
        off += SMALL[re][0]
    for n in SMALL_VECS:
        table[n] = off
        off += SMALL[n][1] // LANES
    for n in SMALL_TILES:
        off = -(-off // 8) * 8
        table[n] = off
        off += SMALL[n][0]
    return table, off


SMALL_OFFSET, SMALL_USED_ROWS = _small_offsets()
SMALL_ROWS = -(-SMALL_USED_ROWS // (8 * N_DEV)) * 8 * N_DEV


def _cast_shards(shards):
    names = list(BIG)

    def body(*refs):
        ins, outs = refs[:len(names)], refs[len(names):]
        for idx in range(len(names)):
            outs[idx][...] = ins[idx][...].astype(BF16)

    return pl.pallas_call(
        body, name="cast_shards",
        out_shape=[jax.ShapeDtypeStruct((BIG[n][1], BIG[n][2]), BF16) for n in names],
        compiler_params=_cparams(),
    )(*[shards[n] for n in names])


def _peer(x, y, c, r):
    px = 1 - x if r & 4 else x
    py = 1 - y if r & 2 else y
    pc = 1 - c if r & 1 else c
    return px, py, pc


FIRST_GROUP = ("wgt1", "wut1", "wd1", "wint")
LATE_GROUP = ("glu", "wout", "wgt2", "wut2", "wd2")
N_PEERS = N_DEV - 1
ANY_SPEC = pl.BlockSpec(memory_space=pl.ANY)
HBM_SPEC = pl.BlockSpec(memory_space=pltpu.HBM)
SEM_SPEC = pl.BlockSpec(memory_space=pltpu.SEMAPHORE)
DATAFLOW_EFFECT = pltpu.SideEffectType.DATAFLOW_SIDE_EFFECTING


def _mesh_pos():
    x, y, c = lax.axis_index("x"), lax.axis_index("y"), lax.axis_index("c")
    return x, y, c, 4 * x + 2 * y + c


def _gather_first(first, late):
    nf, nl = len(first), len(late)

    def body(*refs):
        f_in, l_in = refs[:nf], refs[nf:nf + nl]
        f_out, l_out = refs[nf + nl:2 * nf + nl], refs[2 * nf + nl:2 * (nf + nl)]
        send_sems, recv_sems, local_sems = refs[2 * (nf + nl):]
        x, y, c, me = _mesh_pos()
        sibling = (x, y, 1 - c)
        chips = [(x, 1 - y), (1 - x, y), (1 - x, 1 - y)]

        def idx(px, py, pc):
            return 4 * px + 2 * py + pc

        def copy(k, s, block, to, src=None):
            slot = f_out[k].at[block]
            return pltpu.make_async_remote_copy(
                src_ref=slot if src is None else src, dst_ref=slot, send_sem=send_sems.at[k, s],
                recv_sem=recv_sems.at[k, s], device_id=to, device_id_type=MESH_ID)

        local = []
        for k in range(nf + nl):
            src, dst = (f_in[k], f_out[k]) if k < nf else (l_in[k - nf], l_out[k - nf])
            mine = pltpu.make_async_copy(src, dst.at[me], local_sems.at[k])
            mine.start()
            local.append(mine)
        sends = []
        for j, chip in enumerate(chips):
            for k in range(nf):
                sends.append(copy(k, 1 + j, me, (*chip, c), src=f_in[k]))
                sends[-1].start()
        for k in range(nf):
            sends.append(copy(k, 0, me, sibling, src=f_in[k]))
            sends[-1].start()
        for j, chip in enumerate(chips):
            for k in range(nf):
                copy(k, 1 + j, idx(*chip, c), (*chip, c)).wait_recv()
                sends.append(copy(k, 4 + j, idx(*chip, c), sibling))
                sends[-1].start()
        for k in range(nf):
            copy(k, 0, idx(*sibling), sibling).wait_recv()
        for j, chip in enumerate(chips):
            for k in range(nf):
                copy(k, 4 + j, idx(*chip, 1 - c), sibling).wait_recv()
        for cp in sends:
            cp.wait_send()
        for cp in local:
            cp.wait()

    return pl.pallas_call(
        body, name="gather_first",
        in_specs=[ANY_SPEC] * (nf + nl), out_specs=[ANY_SPEC] * (nf + nl),
        out_shape=[jax.ShapeDtypeStruct((N_DEV,) + s.shape, s.dtype) for s in list(first) + list(late)],
        scratch_shapes=[pltpu.SemaphoreType.DMA((nf, N_PEERS)), pltpu.SemaphoreType.DMA((nf, N_PEERS)),
                        pltpu.SemaphoreType.DMA((nf + nl,))],
        compiler_params=pltpu.CompilerParams(has_side_effects=True),
    )(*first, *late)


def _split_copy(src_refs, land_refs, send_sems, recv_sems, k, r, pos, scatter, receiving):
    x, y, c, me = pos
    px, py, pc = _peer(x, y, c, r)
    peer_idx = 4 * px + 2 * py + pc
    if scatter:
        src, dst = src_refs[k].at[peer_idx], land_refs[k].at[r - 1]
    else:
        src, dst = src_refs[k], land_refs[k].at[peer_idx if receiving else me]
    return pltpu.make_async_remote_copy(
        src_ref=src, dst_ref=dst, send_sem=send_sems.at[k * N_PEERS + r - 1],
        recv_sem=recv_sems.at[k * N_PEERS + r - 1], device_id=(px, py, pc), device_id_type=MESH_ID)


def _split_start(name, srcs, lands, scatter):
    n = len(srcs)

    def body(*refs):
        src_refs, land_refs = refs[:n], refs[n:2 * n]
        send_sems, recv_sems = refs[2 * n], refs[2 * n + 1]
        token = refs[-1]
        pos = _mesh_pos()
        for k in range(n):
            for r in range(1, N_DEV):
                _split_copy(src_refs, land_refs, send_sems, recv_sems, k, r, pos, scatter, False).start()
        token[...] = jnp.zeros_like(token)

    thru = [pltpu.HBM(a.shape, a.dtype) for a in list(srcs) + list(lands)]
    outs = pl.pallas_call(
        body, name=name,
        in_specs=[HBM_SPEC] * (2 * n),
        out_specs=[SEM_SPEC, SEM_SPEC] + [HBM_SPEC] * (2 * n) + [pl.BlockSpec(memory_space=pltpu.VMEM)],
        out_shape=[pltpu.SemaphoreType.DMA((n * N_PEERS,)), pltpu.SemaphoreType.DMA((n * N_PEERS,))] + thru
        + [jax.ShapeDtypeStruct((8, LANES), F32)],
        input_output_aliases={i: 2 + i for i in range(2 * n)},
        compiler_params=pltpu.CompilerParams(has_side_effects=DATAFLOW_EFFECT),
    )(*[pltpu.with_memory_space_constraint(a, pltpu.HBM) for a in list(srcs) + list(lands)])
    return outs[0], outs[1], outs[2:2 + n], outs[2 + n:2 + 2 * n], outs[-1]


def _split_wait(name, send_sems, recv_sems, srcs, lands, scatter, after):
    n = len(srcs)

    def body(*refs):
        src_refs, land_refs = refs[:n], refs[n:2 * n]
        send, recv = refs[2 * n], refs[2 * n + 1]
        pos = _mesh_pos()
        for k in range(n):
            for r in range(1, N_DEV):
                cp = _split_copy(src_refs, land_refs, send, recv, k, r, pos, scatter, True)
                cp.wait_send()
                cp.wait_recv()

    thru = [pltpu.HBM(a.shape, a.dtype) for a in list(srcs) + list(lands)]
    outs = pl.pallas_call(
        body, name=name,
        in_specs=[HBM_SPEC] * (2 * n) + [SEM_SPEC, SEM_SPEC, ANY_SPEC],
        out_specs=[HBM_SPEC] * (2 * n), out_shape=thru,
        input_output_aliases={i: i for i in range(2 * n)},
        compiler_params=pltpu.CompilerParams(has_side_effects=DATAFLOW_EFFECT),
    )(*srcs, *lands, send_sems, recv_sems, after)
    return outs[:n], outs[n:]


def _late_copy(passing, src_refs, land_refs, send_sems, recv_sems, k, s, pos, receiving):
    x, y, c, me = pos
    chips = [(x, 1 - y), (1 - x, y), (1 - x, 1 - y)]
    sibling = (x, y, 1 - c)

    def idx(dev):
        return 4 * dev[0] + 2 * dev[1] + dev[2]

    if passing:
        to = sibling
        block = idx((*chips[s], 1 - c)) if receiving else idx((*chips[s], c))
        src = dst = land_refs[k].at[block]
        sem = k * 3 + s
    else:
        to = sibling if s == 0 else (*chips[s - 1], c)
        src, dst = src_refs[k], land_refs[k].at[idx(to) if receiving else me]
        sem = k * 4 + s
    return pltpu.make_async_remote_copy(src_ref=src, dst_ref=dst, send_sem=send_sems.at[sem],
                                        recv_sem=recv_sems.at[sem], device_id=to, device_id_type=MESH_ID)


def _late_gather_call(name, stage, srcs, lands, sems, after=None):
    n = len(srcs)
    n_sem_in = len(sems)
    has_after = after is not None

    def body(*refs):
        src_refs, land_refs = refs[:n], refs[n:2 * n]
        sem_in = refs[2 * n:2 * n + n_sem_in]
        outs = refs[2 * n + n_sem_in + (1 if has_after else 0):]
        pos = _mesh_pos()
        if stage == 0:
            own_send, own_recv = outs[0], outs[1]
            for s in (1, 2, 3, 0):
                for k in range(n):
                    _late_copy(False, src_refs, land_refs, own_send, own_recv, k, s, pos, False).start()
            outs[-1][...] = jnp.zeros_like(outs[-1])
        elif stage == 1:
            own_recv = sem_in[1]
            pass_send, pass_recv = outs[0], outs[1]
            for s in range(3):
                for k in range(n):
                    _late_copy(False, src_refs, land_refs, sem_in[0], own_recv, k, s + 1, pos, True).wait_recv()
                    _late_copy(True, src_refs, land_refs, pass_send, pass_recv, k, s, pos, False).start()
            outs[-1][...] = jnp.zeros_like(outs[-1])
        else:
            own_send, own_recv, pass_send, pass_recv = sem_in
            for k in range(n):
                _late_copy(False, src_refs, land_refs, own_send, own_recv, k, 0, pos, True).wait_recv()
                for s in range(4):
                    _late_copy(False, src_refs, land_refs, own_send, own_recv, k, s, pos, False).wait_send()
                for s in range(3):
                    cp = _late_copy(True, src_refs, land_refs, pass_send, pass_recv, k, s, pos, True)
                    cp.wait_recv()
                    cp.wait_send()

    thru = [pltpu.HBM(a.shape, a.dtype) for a in list(srcs) + list(lands)]
    new_sems = [[pltpu.SemaphoreType.DMA((n * 4,))] * 2, [pltpu.SemaphoreType.DMA((n * 3,))] * 2, []][stage]
    extra = [] if stage == 2 else [jax.ShapeDtypeStruct((8, LANES), F32)]
    outs = pl.pallas_call(
        body, name=name,
        in_specs=[HBM_SPEC] * (2 * n) + [SEM_SPEC] * n_sem_in + [ANY_SPEC] * has_after,
        out_specs=[SEM_SPEC] * len(new_sems) + [HBM_SPEC] * (2 * n) + [pl.BlockSpec(memory_space=pltpu.VMEM)] * len(extra),
        out_shape=new_sems + thru + extra,
        input_output_aliases={i: len(new_sems) + i for i in range(2 * n)},
        compiler_params=pltpu.CompilerParams(has_side_effects=DATAFLOW_EFFECT),
    )(*[pltpu.with_memory_space_constraint(a, pltpu.HBM) for a in list(srcs) + list(lands)], *sems,
      *([after] if has_after else []))
    ns = len(new_sems)
    return list(outs[:ns]), outs[ns:ns + n], outs[ns + n:ns + 2 * n], (outs[-1] if extra else None)


N_SEND_SLOTS = 3


def _exchange_last(grads, small_packed):
    ng = len(grads)
    ch = SMALL_ROWS // N_DEV
    max_rows = max(g.shape[1] for g in grads)
    cols = grads[0].shape[2]

    def body(*refs):
        g_in, s_in = refs[:ng], refs[ng]
        outs = refs[ng + 1:]
        own_out, land, stage = outs[:ng], outs[ng:2 * ng], outs[2 * ng:3 * ng]
        s_red, s_stage = outs[3 * ng], outs[3 * ng + 1]
        (va, vb, vo, vs, sm_in, sm_out, d2d_send, d2d_recv, ici_send, ici_recv, s1_send, s1_recv, s2_send, s2_recv,
         local_sems) = outs[3 * ng + 2:]
        x, y, c, me = _mesh_pos()
        sibling = (x, y, 1 - c)
        chips = [(x, y), (x, 1 - y), (1 - x, y), (1 - x, 1 - y)]

        def idx(chip, core):
            return 4 * chip[0] + 2 * chip[1] + core

        def d2d(k, j):
            return pltpu.make_async_remote_copy(
                src_ref=g_in[k].at[idx(chips[j], 1 - c)], dst_ref=stage[k].at[j], send_sem=d2d_send.at[k, j],
                recv_sem=d2d_recv.at[k, j], device_id=sibling, device_id_type=MESH_ID)

        def ici(k, j, slot):
            rows = g_in[k].shape[1]
            return pltpu.make_async_remote_copy(
                src_ref=vo.at[slot, pl.ds(0, rows)], dst_ref=land[k].at[j - 1], send_sem=ici_send.at[k, j - 1],
                recv_sem=ici_recv.at[k, j - 1], device_id=(*chips[j], c), device_id_type=MESH_ID)

        def small_scatter(r):
            px, py, pc = _peer(x, y, c, r)
            return pltpu.make_async_remote_copy(
                src_ref=s_in.at[pl.ds(pl.multiple_of((4 * px + 2 * py + pc) * ch, 8), ch)], dst_ref=s_stage.at[me],
                send_sem=s1_send.at[r - 1], recv_sem=s1_recv.at[r - 1], device_id=(px, py, pc), device_id_type=MESH_ID)

        def small_gather(r):
            return pltpu.make_async_remote_copy(
                src_ref=sm_out, dst_ref=s_red.at[me], send_sem=s2_send.at[r - 1], recv_sem=s2_recv.at[r - 1],
                device_id=_peer(x, y, c, r), device_id_type=MESH_ID)

        for r in range(1, N_DEV):
            small_scatter(r).start()
        mine = pltpu.make_async_copy(s_in.at[pl.ds(pl.multiple_of(me * ch, 8), ch)], s_stage.at[me], local_sems.at[0])
        mine.start()
        pairs = [(k, j) for k in range(ng) for j in (1, 2, 3)] + [(k, 0) for k in range(ng)]
        for k, j in pairs:
            d2d(k, j).start()

        for r in range(1, N_DEV):
            small_scatter(r).wait_recv()
        mine.wait()
        load = pltpu.make_async_copy(s_stage, sm_in, local_sems.at[1])
        load.start()
        load.wait()
        total = sm_in[0]
        for i in range(1, N_DEV):
            total = total + sm_in[i]
        sm_out[...] = total
        for r in range(1, N_DEV):
            small_gather(r).start()
        keep = pltpu.make_async_copy(sm_out, s_red.at[me], local_sems.at[2])
        keep.start()

        in_flight = {}
        for i, (k, j) in enumerate(pairs):
            slot = i % N_SEND_SLOTS
            rows = g_in[k].shape[1]
            if slot in in_flight:
                in_flight.pop(slot).wait_send()
            d2d(k, j).wait_recv()
            la = pltpu.make_async_copy(g_in[k].at[idx(chips[j], c)], va.at[pl.ds(0, rows)], local_sems.at[3])
            lb = pltpu.make_async_copy(stage[k].at[j], vb.at[pl.ds(0, rows)], local_sems.at[4])
            la.start()
            lb.start()
            la.wait()
            lb.wait()
            total = va[pl.ds(0, rows)].astype(F32) + vb[pl.ds(0, rows)].astype(F32)
            if j == 0:
                vs[pl.ds(0, rows)] = total
                st = pltpu.make_async_copy(vs.at[pl.ds(0, rows)], own_out[k], local_sems.at[5])
                st.start()
                st.wait()
            else:
                vo[slot, pl.ds(0, rows)] = total.astype(BF16)
                cp = ici(k, j, slot)
                cp.start()
                in_flight[slot] = cp
        for cp in in_flight.values():
            cp.wait_send()

        for j in (1, 2, 3, 0):
            for k in range(ng):
                d2d(k, j).wait_send()
        for j in (1, 2, 3):
            for k in range(ng):
                ici(k, j, 0).wait_recv()
        for r in range(1, N_DEV):
            small_scatter(r).wait_send()
            small_gather(r).wait_send()
            small_gather(r).wait_recv()
        keep.wait()

    out_shape = [jax.ShapeDtypeStruct(g.shape[1:], F32) for g in grads]
    out_shape += [jax.ShapeDtypeStruct((3,) + g.shape[1:], BF16) for g in grads]
    out_shape += [jax.ShapeDtypeStruct((4,) + g.shape[1:], BF16) for g in grads]
    out_shape += [jax.ShapeDtypeStruct((N_DEV, ch, LANES), F32), jax.ShapeDtypeStruct((N_DEV, ch, LANES), F32)]
    outs = pl.pallas_call(
        body, name="exchange_last",
        in_specs=[ANY_SPEC] * (ng + 1), out_specs=[ANY_SPEC] * len(out_shape), out_shape=out_shape,
        scratch_shapes=[pltpu.VMEM((max_rows, cols), BF16), pltpu.VMEM((max_rows, cols), BF16),
                        pltpu.VMEM((N_SEND_SLOTS, max_rows, cols), BF16), pltpu.VMEM((max_rows, cols), F32),
                        pltpu.VMEM((N_DEV, ch, LANES), F32), pltpu.VMEM((ch, LANES), F32),
                        pltpu.SemaphoreType.DMA((ng, 4)), pltpu.SemaphoreType.DMA((ng, 4)),
                        pltpu.SemaphoreType.DMA((ng, 3)), pltpu.SemaphoreType.DMA((ng, 3)),
                        pltpu.SemaphoreType.DMA((N_PEERS,)), pltpu.SemaphoreType.DMA((N_PEERS,)),
                        pltpu.SemaphoreType.DMA((N_PEERS,)), pltpu.SemaphoreType.DMA((N_PEERS,)),
                        pltpu.SemaphoreType.DMA((6,))],
        compiler_params=pltpu.CompilerParams(has_side_effects=True, vmem_limit_bytes=VMEM_LIMIT),
    )(*grads, small_packed)
    return outs[:ng], outs[ng:2 * ng], outs[3 * ng].reshape(SMALL_ROWS, LANES)


def _adamw_math(w, g, m, v):
    m2 = ADAM_B1 * m + (1.0 - ADAM_B1) * g
    v2 = ADAM_B2 * v + (1.0 - ADAM_B2) * (g * g)
    m_hat = m2 / (1.0 - ADAM_B1 ** ADAM_STEP)
    v_hat = v2 / (1.0 - ADAM_B2 ** ADAM_STEP)
    delta = -ADAM_LR * (m_hat / (jnp.sqrt(v_hat) + ADAM_EPS) + ADAM_WD * w)
    return delta, m2, v2


ADAM_ROW_TILES = 2


def _adamw_big(own, parts, w, m, v, name):
    shape = w.shape
    own_is_blocks = own.ndim == 3
    tr = shape[0] // ADAM_ROW_TILES
    n_parts = parts.shape[0]

    def body(own_ref, p_ref, w_ref, m_ref, v_ref, g_ref, d_ref, m2_ref, v2_ref, own_s, sem):
        rows = pl.ds(pl.multiple_of(pl.program_id(0) * tr, 16), tr)
        if own_is_blocks:
            cp = pltpu.make_async_copy(own_ref.at[_mesh_pos()[3], rows], own_s, sem)
        else:
            cp = pltpu.make_async_copy(own_ref.at[rows], own_s, sem)
        cp.start()
        cp.wait()
        g = own_s[...].astype(F32)
        for i in range(parts.shape[0]):
            g = g + p_ref[i].astype(F32)
        delta, m2, v2 = _adamw_math(w_ref[...], g, m_ref[...], v_ref[...])
        g_ref[...] = g
        d_ref[...] = delta
        m2_ref[...] = m2
        v2_ref[...] = v2

    tile = pl.BlockSpec((tr, shape[1]), lambda i: (i, 0))
    return pl.pallas_call(
        body, name=name, grid=(ADAM_ROW_TILES,),
        in_specs=[ANY_SPEC, pl.BlockSpec((n_parts, tr, shape[1]), lambda i: (0, i, 0)), tile, tile, tile],
        out_specs=[tile] * 4, out_shape=[jax.ShapeDtypeStruct(shape, F32)] * 4,
        scratch_shapes=[pltpu.VMEM((tr, shape[1]), own.dtype), pltpu.SemaphoreType.DMA(())],
        compiler_params=_cparams(("arbitrary",)),
    )(own, parts, w, m, v)


def _pack_small(grads):
    names = list(SMALL)

    def body(*refs):
        ins, out = dict(zip(names, refs[:-1])), refs[-1]
        out[...] = jnp.zeros_like(out)
        for re, im in SMALL_PAIRS:
            off, rows = SMALL_OFFSET[re], SMALL[re][0]
            out[off:off + rows, :] = jnp.concatenate([ins[re][...], ins[im][...]], axis=1)
        for n in SMALL_VECS:
            off, vec = SMALL_OFFSET[n], ins[n][...]
            for i in range(SMALL[n][1] // LANES):
                out[off + i:off + i + 1, :] = vec[:, i * LANES:(i + 1) * LANES]
        for n in SMALL_TILES:
            off, (rows, cols) = SMALL_OFFSET[n], SMALL[n]
            out[off:off + rows, 0:cols] = ins[n][...]

    return pl.pallas_call(
        body, name="pack_small", out_shape=jax.ShapeDtypeStruct((SMALL_ROWS, LANES), F32),
        compiler_params=_cparams(),
    )(*[grads[n] for n in names])


def _unpack_small_ref(g_ref, n):
    off, (rows, cols) = SMALL_OFFSET[n], SMALL[n]
    for re, im in SMALL_PAIRS:
        if n == re:
            return g_ref[off:off + rows, 0:HALF_LANES]
        if n == im:
            return g_ref[off:off + rows, HALF_LANES:LANES]
    if n in SMALL_VECS:
        return jnp.concatenate([g_ref[off + i:off + i + 1, :] for i in range(cols // LANES)], axis=1)
    return g_ref[off:off + rows, 0:cols]


def _adamw_small(g_packed, w, m, v):
    names = list(SMALL_PARAMS)
    n = len(names)

    def body(g_ref, *refs):
        w_refs, m_refs, v_refs, outs = refs[:n], refs[n:2 * n], refs[2 * n:3 * n], refs[3 * n:]
        for idx, name in enumerate(names):
            g = _unpack_small_ref(g_ref, name)
            delta, m2, v2 = _adamw_math(w_refs[idx][...], g, m_refs[idx][...], v_refs[idx][...])
            outs[4 * idx][...] = g
            outs[4 * idx + 1][...] = delta
            outs[4 * idx + 2][...] = m2
            outs[4 * idx + 3][...] = v2
        outs[4 * n][...] = _unpack_small_ref(g_ref, "loss")

    outs = pl.pallas_call(
        body, name="adamw_small",
        out_shape=[jax.ShapeDtypeStruct(SMALL[name], F32) for name in names for _ in range(4)]
        + [jax.ShapeDtypeStruct(SMALL["loss"], F32)],
        compiler_params=_cparams(),
    )(g_packed, *[w[k] for k in names], *[m[k] for k in names], *[v[k] for k in names])
    return {name: outs[4 * idx:4 * idx + 4] for idx, name in enumerate(names)}, outs[4 * n]


WEIGHT_NAMES = ['norm_ffn1', 'ffn1_w_gate', 'ffn1_w_up', 'ffn1_w_down', 'norm_mix', 'w_in', 'attn_sinks',
                'ssm_lambda_re', 'ssm_lambda_im', 'ssm_log_dt', 'ssm_b_re', 'ssm_b_im', 'ssm_c_re', 'ssm_c_im',
                'ssm_d', 'ssm_glu_w', 'ssm_glu_b', 'attn_out_norm', 'ssm_out_norm', 'w_out', 'norm_ffn2',
                'ffn2_w_gate', 'ffn2_w_up', 'ffn2_w_down', 'final_norm']


def kernel(x, norm_ffn1, ffn1_w_gate, ffn1_w_up, ffn1_w_down, norm_mix, w_in, attn_sinks, ssm_lambda_re, ssm_lambda_im, ssm_log_dt, ssm_b_re, ssm_b_im, ssm_c_re, ssm_c_im, ssm_d, ssm_glu_w, ssm_glu_b, attn_out_norm, ssm_out_norm, w_out, norm_ffn2, ffn2_w_gate, ffn2_w_up, ffn2_w_down, final_norm, loss_target, m_norm_ffn1, m_ffn1_w_gate, m_ffn1_w_up, m_ffn1_w_down, m_norm_mix, m_w_in, m_attn_sinks, m_ssm_lambda_re, m_ssm_lambda_im, m_ssm_log_dt, m_ssm_b_re, m_ssm_b_im, m_ssm_c_re, m_ssm_c_im, m_ssm_d, m_ssm_glu_w, m_ssm_glu_b, m_attn_out_norm, m_ssm_out_norm, m_w_out, m_norm_ffn2, m_ffn2_w_gate, m_ffn2_w_up, m_ffn2_w_down, m_final_norm, v_norm_ffn1, v_ffn1_w_gate, v_ffn1_w_up, v_ffn1_w_down, v_norm_mix, v_w_in, v_attn_sinks, v_ssm_lambda_re, v_ssm_lambda_im, v_ssm_log_dt, v_ssm_b_re, v_ssm_b_im, v_ssm_c_re, v_ssm_c_im, v_ssm_d, v_ssm_glu_w, v_ssm_glu_b, v_attn_out_norm, v_ssm_out_norm, v_w_out, v_norm_ffn2, v_ffn2_w_gate, v_ffn2_w_up, v_ffn2_w_down, v_final_norm):
    args = dict(locals())
    weights = {n: args[n] for n in WEIGHT_NAMES}
    moms = {n: args["m_" + n] for n in WEIGHT_NAMES}
    vars_ = {n: args["v_" + n] for n in WEIGHT_NAMES}

    def shard2d(a, k):
        a = a.reshape(a.shape[-2], a.shape[-1])
        return a.T if BIG[k][3] else a

    def shard_master(a, k):
        return (a.T if BIG[k][3] else a).reshape(weights[BIG[k][0]].shape)

    def blocks(g, k):
        return g.reshape(N_DEV, BIG[k][1], BIG[k][2])

    def full(g, k):
        return g.reshape(N_DEV * BIG[k][1], BIG[k][2])

    shards = dict(zip(BIG, _cast_shards({k: shard2d(weights[BIG[k][0]], k) for k in BIG})))
    nf = len(FIRST_GROUP)
    got = _gather_first([shards[k] for k in FIRST_GROUP], [shards[k] for k in LATE_GROUP])
    w_first = {k: full(g, k) for k, g in zip(FIRST_GROUP, got[:nf])}
    late = {}
    late["own_sems"], late["srcs"], late["lands"], w_token = _late_gather_call(
        "gather_late_start", 0, [shards[k] for k in LATE_GROUP], got[nf:], [])

    def late_pass(dep):
        late["pass_sems"], late["srcs"], late["lands"], token = _late_gather_call(
            "gather_late_pass", 1, late["srcs"], late["lands"], late["own_sems"], after=dep)
        return token

    def late_weights(dep):
        _, _, lands, _ = _late_gather_call("gather_late_wait", 2, late["srcs"], late["lands"],
                                           late["own_sems"] + late["pass_sems"], after=dep)
        return {k: full(g, k) for k, g in zip(LATE_GROUP, lands)}

    early = {}

    def early_grads(g):
        srcs = [blocks(g[k], k) for k in LATE_GROUP]
        lands = [lax.empty((N_PEERS, BIG[k][1], BIG[k][2]), BF16) for k in LATE_GROUP]
        early["send"], early["recv"], early["srcs"], early["lands"], token = _split_start(
            "grads_late_start", srcs, lands, scatter=True)
        return token

    def small2d(a, n):
        if n in SMALL_TRANSPOSED:
            a = jnp.swapaxes(a, -1, -2)
        return a.reshape(SMALL[n])

    def small_master(a, n):
        if n in SMALL_TRANSPOSED:
            shape = weights[n].shape
            return jnp.swapaxes(a.reshape(shape[:-2] + (shape[-1], shape[-2])), -1, -2)
        return a.reshape(weights[n].shape)

    small_p = {n: small2d(weights[n], n) for n in SMALL_PARAMS}
    _, grad_x, g_first, g_small = _local_step(
        x.reshape(SEQ, D_MODEL), loss_target.reshape(SEQ, D_MODEL), w_first, small_p, late_weights, early_grads,
        after=w_token, midway=late_pass)

    own_sums, first_parts, small_grad = _exchange_last([blocks(g_first[k], k) for k in FIRST_GROUP],
                                                       _pack_small(g_small))
    own_late, late_parts = _split_wait("grads_late_wait", early["send"], early["recv"], early["srcs"],
                                       early["lands"], True, small_grad)
    own = dict(zip(FIRST_GROUP + LATE_GROUP, list(own_sums) + list(own_late)))
    parts = dict(zip(FIRST_GROUP + LATE_GROUP, list(first_parts) + list(late_parts)))
    outs = {}
    for k in BIG:
        n = BIG[k][0]
        outs[n] = [shard_master(o, k) for o in
                   _adamw_big(own[k], parts[k], shard2d(weights[n], k), shard2d(moms[n], k), shard2d(vars_[n], k),
                              "adamw_" + n)]
    small_out, loss_row = _adamw_small(small_grad, small_p, {n: small2d(moms[n], n) for n in SMALL_PARAMS},
                                       {n: small2d(vars_[n], n) for n in SMALL_PARAMS})
    for n in SMALL_PARAMS:
        outs[n] = [small_master(o, n) for o in small_out[n]]

    result = [loss_row[0, 0], grad_x.reshape(x.shape)]
    for i in range(4):
        result += [outs[n][i] for n in WEIGHT_NAMES]
    return tuple(result)
```

```python
import functools

import jax
import jax.numpy as jnp
from jax import lax
from jax.experimental import pallas as pl
from jax.experimental.pallas import tpu as pltpu

F32 = jnp.float32
BF16 = jnp.bfloat16

N_DEV = 8
SEQ = 2048
D_MODEL = 1024
D_FF = 2816
ATTN_HEADS = 8
KV_HEADS = 2
HEAD_DIM = 64
ATTN_WIDTH = 512
KV_WIDTH = 128
WINDOW = 128
SSM_WIDTH = 512
IN_WIDTH = 1280
EPS = 1e-6
NEG_INF = -1e30
LAMBDA_RE_MAX = -1e-4
LANES = 128
N_LANE_BLOCKS = 16
SCAN_CHUNK = SEQ // 8

ADAM_LR = 0.001
ADAM_B1 = 0.9
ADAM_B2 = 0.999
ADAM_EPS = 1e-08
ADAM_WD = 0.01
ADAM_STEP = 10

VMEM_LIMIT = 56 * 1024 * 1024
MESH_ID = pl.DeviceIdType.MESH


def _cparams(sem=None):
    return pltpu.CompilerParams(dimension_semantics=sem, vmem_limit_bytes=VMEM_LIMIT)


def _dot(a, b):
    return jnp.dot(a, b, preferred_element_type=F32)


def _dot_nt(a, b):
    return lax.dot_general(a, b, (((1,), (1,)), ((), ())), preferred_element_type=F32)


def _dot_tn(a, b):
    return lax.dot_general(a, b, (((0,), (0,)), ((), ())), preferred_element_type=F32)


def _rms_fwd(x, g):
    r = lax.rsqrt(jnp.mean(x * x, axis=-1, keepdims=True) + EPS)
    return x * r * g


def _rms_bwd(dh, x, g):
    r = lax.rsqrt(jnp.mean(x * x, axis=-1, keepdims=True) + EPS)
    xh = x * r
    dg = jnp.sum(dh * xh, axis=0, keepdims=True)
    dxh = dh * g
    dx = r * (dxh - xh * jnp.mean(dxh * xh, axis=-1, keepdims=True))
    return dx, dg


def _sigmoid(x):
    return 1.0 / (1.0 + jnp.exp(-x))


FFN_TM = 512
FFN_TF = 1408


def _ffn_fwd(x, g, wgt, wut, wd, name, after=None, head=None):
    tm, tf = FFN_TM, FFN_TF
    nj = D_FF // tf
    deps = [] if after is None else [after]
    n_in = len(deps) + (2 if head else 0)

    def body(x_ref, g_ref, wg_ref, wu_ref, wd_ref, *rest):
        i = pl.program_id(0)
        j = pl.program_id(1)
        if head:
            gf_ref, t_ref = rest[len(deps):n_in]
            xo_ref, h_ref, a_ref, b_ref, loss_ref, dgf_ref, h_s, acc = rest[n_in:]
        else:
            xo_ref, h_ref, a_ref, b_ref, h_s, acc = rest[n_in:]

        @pl.when(j == 0)
        def _():
            h = _rms_fwd(x_ref[...], g_ref[...]).astype(BF16)
            h_s[...] = h
            h_ref[...] = h
            acc[...] = jnp.zeros_like(acc)

        h = h_s[...]
        a = _dot_nt(h, wg_ref[...])
        b = _dot_nt(h, wu_ref[...])
        a_ref[...] = a.astype(BF16)
        b_ref[...] = b.astype(BF16)
        s = (a * _sigmoid(a) * b).astype(BF16)
        acc[...] += _dot(s, wd_ref[...])

        @pl.when(j == nj - 1)
        def _():
            xo = x_ref[...] + 0.5 * acc[...]
            if not head:
                xo_ref[...] = xo
                return
            gf = gf_ref[...]
            err = _rms_fwd(xo, gf) - t_ref[...]
            part = jnp.broadcast_to(0.5 * jnp.sum(err * err) / D_MODEL, (1, LANES))
            dx, dgf = _rms_bwd(err * (1.0 / D_MODEL), xo, gf)
            xo_ref[...] = dx

            @pl.when(i == 0)
            def _():
                loss_ref[...] = part
                dgf_ref[...] = dgf

            @pl.when(i != 0)
            def _():
                loss_ref[...] += part
                dgf_ref[...] += dgf

    row = lambda i, j: (i, 0)
    const = lambda i, j: (0, 0)
    head_in = [pl.BlockSpec((1, D_MODEL), const), pl.BlockSpec((tm, D_MODEL), row)] if head else []
    head_out = [pl.BlockSpec((1, LANES), const), pl.BlockSpec((1, D_MODEL), const)] if head else []
    head_shape = [jax.ShapeDtypeStruct((1, LANES), F32), jax.ShapeDtypeStruct((1, D_MODEL), F32)] if head else []
    return pl.pallas_call(
        body, name=name, grid=(SEQ // tm, nj),
        in_specs=[pl.BlockSpec((tm, D_MODEL), row), pl.BlockSpec((1, D_MODEL), const),
                  pl.BlockSpec((tf, D_MODEL), lambda i, j: (j, 0)),
                  pl.BlockSpec((tf, D_MODEL), lambda i, j: (j, 0)),
                  pl.BlockSpec((tf, D_MODEL), lambda i, j: (j, 0))] + [pl.BlockSpec(memory_space=pl.ANY)] * len(deps)
        + head_in,
        out_specs=[pl.BlockSpec((tm, D_MODEL), row), pl.BlockSpec((tm, D_MODEL), row),
                   pl.BlockSpec((tm, tf), lambda i, j: (i, j)),
                   pl.BlockSpec((tm, tf), lambda i, j: (i, j))] + head_out,
        out_shape=[jax.ShapeDtypeStruct((SEQ, D_MODEL), F32), jax.ShapeDtypeStruct((SEQ, D_MODEL), BF16),
                   jax.ShapeDtypeStruct((SEQ, D_FF), BF16), jax.ShapeDtypeStruct((SEQ, D_FF), BF16)] + head_shape,
        scratch_shapes=[pltpu.VMEM((tm, D_MODEL), BF16), pltpu.VMEM((tm, D_MODEL), F32)],
        compiler_params=_cparams(("arbitrary" if head else "parallel", "arbitrary")),
    )(x, g, wgt, wut, wd, *deps, *(head or ()))


def _ffn_bwd_act(dxo, x, g, a, b, wgt, wut, wd, name):
    tm, tf = FFN_TM // 2, FFN_TF
    nj = D_FF // tf

    def body(dxo_ref, x_ref, g_ref, a_ref, b_ref, wg_ref, wu_ref, wd_ref,
             dx_ref, da_ref, db_ref, s_ref, df_ref, dg_ref, df_s, acc):
        i = pl.program_id(0)
        j = pl.program_id(1)

        @pl.when(j == 0)
        def _():
            df = (0.5 * dxo_ref[...]).astype(BF16)
            df_s[...] = df
            df_ref[...] = df
            acc[...] = jnp.zeros_like(acc)

        ds = _dot_nt(df_s[...], wd_ref[...])
        av = a_ref[...].astype(F32)
        bv = b_ref[...].astype(F32)
        sig = _sigmoid(av)
        sl = av * sig
        s_ref[...] = (sl * bv).astype(BF16)
        db = (ds * sl).astype(BF16)
        da = (ds * bv * (sig * (1.0 + av * (1.0 - sig)))).astype(BF16)
        da_ref[...] = da
        db_ref[...] = db
        acc[...] += _dot(da, wg_ref[...]) + _dot(db, wu_ref[...])

        @pl.when(j == nj - 1)
        def _():
            dx, dg = _rms_bwd(acc[...], x_ref[...], g_ref[...])
            dx_ref[...] = dxo_ref[...] + dx

            @pl.when(i == 0)
            def _():
                dg_ref[...] = dg

            @pl.when(i != 0)
            def _():
                dg_ref[...] += dg

    row = lambda i, j: (i, 0)
    col = lambda i, j: (j, 0)
    tile = lambda i, j: (i, j)
    return pl.pallas_call(
        body, name=name, grid=(SEQ // tm, nj),
        in_specs=[pl.BlockSpec((tm, D_MODEL), row), pl.BlockSpec((tm, D_MODEL), row),
                  pl.BlockSpec((1, D_MODEL), lambda i, j: (0, 0)),
                  pl.BlockSpec((tm, tf), tile), pl.BlockSpec((tm, tf), tile),
                  pl.BlockSpec((tf, D_MODEL), col), pl.BlockSpec((tf, D_MODEL), col), pl.BlockSpec((tf, D_MODEL), col)],
        out_specs=[pl.BlockSpec((tm, D_MODEL), row),
                   pl.BlockSpec((tm, tf), tile), pl.BlockSpec((tm, tf), tile), pl.BlockSpec((tm, tf), tile),
                   pl.BlockSpec((tm, D_MODEL), row),
                   pl.BlockSpec((1, D_MODEL), lambda i, j: (0, 0))],
        out_shape=[jax.ShapeDtypeStruct((SEQ, D_MODEL), F32),
                   jax.ShapeDtypeStruct((SEQ, D_FF), BF16), jax.ShapeDtypeStruct((SEQ, D_FF), BF16),
                   jax.ShapeDtypeStruct((SEQ, D_FF), BF16),
                   jax.ShapeDtypeStruct((SEQ, D_MODEL), BF16),
                   jax.ShapeDtypeStruct((1, D_MODEL), F32)],
        scratch_shapes=[pltpu.VMEM((tm, D_MODEL), BF16), pltpu.VMEM((tm, D_MODEL), F32)],
        compiler_params=_cparams(("arbitrary", "arbitrary")),
    )(dxo, x, g, a, b, wgt, wut, wd)


def _mm_tn(pairs, name, tmm=256):
    m = pairs[0][0].shape[1]
    n_pairs = len(pairs)

    def body(*refs):
        ins, outs = refs[:2 * n_pairs], refs[2 * n_pairs:]
        for p in range(n_pairs):
            outs[p][...] = _dot_tn(ins[2 * p][...], ins[2 * p + 1][...]).astype(BF16)

    in_specs, out_specs, out_shape, args = [], [], [], []
    for a, b in pairs:
        n = b.shape[1]
        in_specs += [pl.BlockSpec((SEQ, tmm), lambda i: (0, i)), pl.BlockSpec((SEQ, n), lambda i: (0, 0))]
        out_specs.append(pl.BlockSpec((tmm, n), lambda i: (i, 0)))
        out_shape.append(jax.ShapeDtypeStruct((m, n), BF16))
        args += [a, b]
    return pl.pallas_call(body, name=name, grid=(m // tmm,), in_specs=in_specs, out_specs=out_specs,
                          out_shape=out_shape, compiler_params=_cparams(("parallel",)))(*args)


MIX_TM = 256


def _mixin_fwd(x, g, wint):
    tm = MIX_TM

    def body(x_ref, g_ref, w_ref, h_ref, q_ref, k_ref, v_ref, u_ref):
        h = _rms_fwd(x_ref[...], g_ref[...]).astype(BF16)
        h_ref[...] = h
        proj = _dot_nt(h, w_ref[...])
        q_ref[...] = proj[:, :ATTN_WIDTH].T
        k_ref[...] = proj[:, ATTN_WIDTH:ATTN_WIDTH + KV_WIDTH]
        v_ref[...] = proj[:, ATTN_WIDTH + KV_WIDTH:ATTN_WIDTH + 2 * KV_WIDTH]
        u_ref[...] = proj[:, ATTN_WIDTH + 2 * KV_WIDTH:]

    row = lambda i: (i, 0)
    return pl.pallas_call(
        body, name="mixin_fwd", grid=(SEQ // tm,),
        in_specs=[pl.BlockSpec((tm, D_MODEL), row), pl.BlockSpec((1, D_MODEL), lambda i: (0, 0)),
                  pl.BlockSpec((IN_WIDTH, D_MODEL), lambda i: (0, 0))],
        out_specs=[pl.BlockSpec((tm, D_MODEL), row), pl.BlockSpec((ATTN_WIDTH, tm), lambda i: (0, i)),
                   pl.BlockSpec((tm, KV_WIDTH), row), pl.BlockSpec((tm, KV_WIDTH), row),
                   pl.BlockSpec((tm, SSM_WIDTH), row)],
        out_shape=[jax.ShapeDtypeStruct((SEQ, D_MODEL), BF16), jax.ShapeDtypeStruct((ATTN_WIDTH, SEQ), F32),
                   jax.ShapeDtypeStruct((SEQ, KV_WIDTH), F32), jax.ShapeDtypeStruct((SEQ, KV_WIDTH), F32),
                   jax.ShapeDtypeStruct((SEQ, SSM_WIDTH), F32)],
        compiler_params=_cparams(("parallel",)),
    )(x, g, wint)


def _mixin_bwd(dqt, dk, dv, du, wint, x, g, dres):
    tm = MIX_TM

    def body(dq_ref, dk_ref, dv_ref, du_ref, w_ref, x_ref, g_ref, dres_ref, dx_ref, dp_ref, dg_ref):
        i = pl.program_id(0)
        dp = jnp.concatenate([dq_ref[...].T, dk_ref[...], dv_ref[...], du_ref[...]], axis=-1).astype(BF16)
        dp_ref[...] = dp
        dh = _dot(dp, w_ref[...])
        dx, dg = _rms_bwd(dh, x_ref[...], g_ref[...])
        dx_ref[...] = dres_ref[...] + dx

        @pl.when(i == 0)
        def _():
            dg_ref[...] = dg

        @pl.when(i != 0)
        def _():
            dg_ref[...] += dg

    row = lambda i: (i, 0)
    const = lambda i: (0, 0)
    return pl.pallas_call(
        body, name="mixin_bwd", grid=(SEQ // tm,),
        in_specs=[pl.BlockSpec((ATTN_WIDTH, tm), lambda i: (0, i)), pl.BlockSpec((tm, KV_WIDTH), row),
                  pl.BlockSpec((tm, KV_WIDTH), row), pl.BlockSpec((tm, SSM_WIDTH), row),
                  pl.BlockSpec((IN_WIDTH, D_MODEL), const), pl.BlockSpec((tm, D_MODEL), row),
                  pl.BlockSpec((1, D_MODEL), const), pl.BlockSpec((tm, D_MODEL), row)],
        out_specs=[pl.BlockSpec((tm, D_MODEL), row), pl.BlockSpec((tm, IN_WIDTH), row),
                   pl.BlockSpec((1, D_MODEL), const)],
        out_shape=[jax.ShapeDtypeStruct((SEQ, D_MODEL), F32), jax.ShapeDtypeStruct((SEQ, IN_WIDTH), BF16),
                   jax.ShapeDtypeStruct((1, D_MODEL), F32)],
        compiler_params=_cparams(("arbitrary",)),
    )(dqt, dk, dv, du, wint, x, g, dres)


N_QBLOCKS = SEQ // WINDOW
GROUP = ATTN_HEADS // KV_HEADS
SCALE = HEAD_DIM ** -0.5


def _alibi_slope(h):
    return 2.0 ** (-8.0 * (h + 1) / ATTN_HEADS)


def _window_masks(n):
    s_idx = lax.broadcasted_iota(jnp.int32, (3 * WINDOW, WINDOW), 0)
    t_idx = lax.broadcasted_iota(jnp.int32, (3 * WINDOW, WINDOW), 1)
    absrel = jnp.abs(s_idx - WINDOW - t_idx)
    key_pos = n * WINDOW - WINDOW + s_idx
    valid = (absrel <= WINDOW) & (key_pos >= 0) & (key_pos < SEQ)
    return absrel.astype(F32), valid


def _group_cols(ref, r0, gi):
    return jnp.concatenate(
        [ref[(gi * GROUP + hh) * HEAD_DIM:(gi * GROUP + hh + 1) * HEAD_DIM, pl.ds(r0, WINDOW)].astype(BF16)
         for hh in range(GROUP)], axis=1)


def _group_probs(qgt, kw, absrel, valid, gi, sk_ref):
    bias = jnp.concatenate([jnp.where(valid, -_alibi_slope(gi * GROUP + hh) * absrel, NEG_INF)
                            for hh in range(GROUP)], axis=1)
    sink = jnp.concatenate([jnp.full((1, WINDOW), sk_ref[0, gi * GROUP + hh], F32) for hh in range(GROUP)], axis=1)
    s = _dot(kw, qgt) * SCALE + bias
    m = jnp.maximum(jnp.max(s, axis=0, keepdims=True), sink)
    p = jnp.exp(s - m)
    ps = jnp.exp(sink - m)
    inv = 1.0 / (jnp.sum(p, axis=0, keepdims=True) + ps)
    return p * inv, ps * inv


def _attn_fwd(qt, kp, vp, sinks):
    def body(sk_ref, qt_ref, kp_ref, vp_ref, o_ref):
        def blk(n, carry):
            r0 = pl.multiple_of(n * WINDOW, WINDOW)
            absrel, valid = _window_masks(n)
            for gi in range(KV_HEADS):
                kw = kp_ref[pl.ds(r0, 3 * WINDOW), gi * HEAD_DIM:(gi + 1) * HEAD_DIM].astype(BF16)
                vw = vp_ref[pl.ds(r0, 3 * WINDOW), gi * HEAD_DIM:(gi + 1) * HEAD_DIM].astype(BF16)
                pr, _ = _group_probs(_group_cols(qt_ref, r0, gi), kw, absrel, valid, gi, sk_ref)
                og = _dot_tn(pr.astype(BF16), vw)
                for hh in range(GROUP):
                    h = gi * GROUP + hh
                    o_ref[pl.ds(r0, WINDOW), h * HEAD_DIM:(h + 1) * HEAD_DIM] = og[hh * WINDOW:(hh + 1) * WINDOW]
            return carry

        lax.fori_loop(0, N_QBLOCKS, blk, 0)

    vmem = pl.BlockSpec(memory_space=pltpu.VMEM)
    return pl.pallas_call(
        body, name="attn_fwd",
        in_specs=[pl.BlockSpec(memory_space=pltpu.SMEM), vmem, vmem, vmem], out_specs=vmem,
        out_shape=jax.ShapeDtypeStruct((SEQ, ATTN_WIDTH), F32),
        compiler_params=_cparams(),
    )(sinks, qt, kp, vp)


def _attn_bwd(qt, kp, vp, sinks, dot_):
    def body(sk_ref, qt_ref, kp_ref, vp_ref, dot_ref, dqt_ref, dkp_ref, dvp_ref, dsk_ref, dsk_acc):
        dkp_ref[...] = jnp.zeros_like(dkp_ref)
        dvp_ref[...] = jnp.zeros_like(dvp_ref)
        dsk_acc[...] = jnp.zeros_like(dsk_acc)

        def blk(n, carry):
            r0 = pl.multiple_of(n * WINDOW, WINDOW)
            absrel, valid = _window_masks(n)
            for gi in range(KV_HEADS):
                gcols = slice(gi * HEAD_DIM, (gi + 1) * HEAD_DIM)
                kw = kp_ref[pl.ds(r0, 3 * WINDOW), gcols].astype(BF16)
                vw = vp_ref[pl.ds(r0, 3 * WINDOW), gcols].astype(BF16)
                qgt = _group_cols(qt_ref, r0, gi)
                dogt = _group_cols(dot_ref, r0, gi)
                pr, psink = _group_probs(qgt, kw, absrel, valid, gi, sk_ref)
                dp = _dot(vw, dogt)
                delta = jnp.sum(pr * dp, axis=0, keepdims=True)
                ds = (pr * (dp - delta)).astype(BF16)
                dsk_acc[gi:gi + 1, :] += -(psink * delta)
                dqgt = _dot_tn(kw, ds) * SCALE
                for hh in range(GROUP):
                    h = gi * GROUP + hh
                    dqt_ref[h * HEAD_DIM:(h + 1) * HEAD_DIM, pl.ds(r0, WINDOW)] = dqgt[:, hh * WINDOW:(hh + 1) * WINDOW]
                dkp_ref[pl.ds(r0, 3 * WINDOW), gcols] += _dot_nt(ds, qgt) * SCALE
                dvp_ref[pl.ds(r0, 3 * WINDOW), gcols] += _dot_nt(pr.astype(BF16), dogt)
            return carry

        lax.fori_loop(0, N_QBLOCKS, blk, 0)
        for h in range(ATTN_HEADS):
            gi, hh = divmod(h, GROUP)
            dsk_ref[:, h:h + 1] = jnp.sum(dsk_acc[gi:gi + 1, hh * WINDOW:(hh + 1) * WINDOW], axis=1, keepdims=True)

    vmem = pl.BlockSpec(memory_space=pltpu.VMEM)
    return pl.pallas_call(
        body, name="attn_bwd",
        in_specs=[pl.BlockSpec(memory_space=pltpu.SMEM), vmem, vmem, vmem, vmem],
        out_specs=[vmem, vmem, vmem, vmem],
        out_shape=[jax.ShapeDtypeStruct((ATTN_WIDTH, SEQ), F32),
                   jax.ShapeDtypeStruct((SEQ + 2 * WINDOW, KV_WIDTH), F32),
                   jax.ShapeDtypeStruct((SEQ + 2 * WINDOW, KV_WIDTH), F32),
                   jax.ShapeDtypeStruct((1, ATTN_HEADS), F32)],
        scratch_shapes=[pltpu.VMEM((KV_HEADS, GROUP * WINDOW), F32)],
        compiler_params=_cparams(),
    )(sinks, qt, kp, vp, dot_)


HALF_LANES = LANES // 2
BLOCK_ROWS = 32


def _embed_block(bt, q):
    z = jnp.zeros((16, HALF_LANES), bt.dtype)
    blk = jnp.concatenate([jnp.concatenate([bt[:16], z], axis=1), jnp.concatenate([z, bt[16:]], axis=1)], axis=0)
    parts = [jnp.zeros((BLOCK_ROWS * q, LANES), bt.dtype)] if q else []
    parts.append(blk)
    if q < 3:
        parts.append(jnp.zeros((BLOCK_ROWS * (3 - q), LANES), bt.dtype))
    return jnp.concatenate(parts, axis=0)


def _extract_block(m, q):
    blk = m[BLOCK_ROWS * q:BLOCK_ROWS * (q + 1)]
    return jnp.concatenate([blk[:16, :HALF_LANES], blk[16:, HALF_LANES:]], axis=0)


def _ssm_prep(lam_re, lam_im, log_dt, bt_re, bt_im, c_re, c_im):
    nb = 2 * N_LANE_BLOCKS

    def body(lr_ref, li_ref, ldt_ref, btr_ref, bti_ref, ctr_ref, cti_ref, ar_ref, ai_ref, bb_ref, cc_ref):
        lr = jnp.minimum(lr_ref[...], LAMBDA_RE_MAX)
        li = li_ref[...]
        dt = jnp.exp(ldt_ref[...])
        mag = jnp.exp(lr * dt)
        ar = mag * jnp.cos(li * dt)
        ai = mag * jnp.sin(li * dt)
        den = lr * lr + li * li
        cr = ((ar - 1.0) * lr + ai * li) / den
        ci = (ai * lr - (ar - 1.0) * li) / den
        ar_ref[...] = ar
        ai_ref[...] = ai
        for i in range(nb):
            q = i % 4
            rows = slice(BLOCK_ROWS * i, BLOCK_ROWS * (i + 1))
            br = _embed_block(btr_ref[rows, :], q)
            bi = _embed_block(bti_ref[rows, :], q)
            cri, cii = cr[i:i + 1, :], ci[i:i + 1, :]
            bb_ref[i] = jnp.concatenate([cri * br - cii * bi, cri * bi + cii * br], axis=1).astype(BF16)
            cc_ref[i] = jnp.concatenate([_embed_block(ctr_ref[rows, :], q).T,
                                         -_embed_block(cti_ref[rows, :], q).T], axis=0).astype(BF16)

    return pl.pallas_call(
        body, name="ssm_prep",
        out_shape=[jax.ShapeDtypeStruct((nb, LANES), F32), jax.ShapeDtypeStruct((nb, LANES), F32),
                   jax.ShapeDtypeStruct((nb, LANES, 2 * LANES), BF16),
                   jax.ShapeDtypeStruct((nb, 2 * LANES, LANES), BF16)],
        compiler_params=_cparams(),
    )(lam_re, lam_im, log_dt, bt_re, bt_im, c_re, c_im)


def _ssm_prep_bwd(lam_re, lam_im, log_dt, bt_re, bt_im, dar, dai, dbb, dcc):
    nb = 2 * N_LANE_BLOCKS

    def body(lr_ref, li_ref, ldt_ref, btr_ref, bti_ref, dar_ref, dai_ref, dbb_ref, dcc_ref,
             glr_ref, gli_ref, gdt_ref, gbr_ref, gbi_ref, gcre_ref, gcim_ref, gcr_s, gci_s):
        lam = lr_ref[...]
        lr = jnp.minimum(lam, LAMBDA_RE_MAX)
        li = li_ref[...]
        dt = jnp.exp(ldt_ref[...])
        mag = jnp.exp(lr * dt)
        cs = jnp.cos(li * dt)
        sn = jnp.sin(li * dt)
        ar = mag * cs
        ai = mag * sn
        den = lr * lr + li * li
        nr = (ar - 1.0) * lr + ai * li
        ni = ai * lr - (ar - 1.0) * li
        cr = nr / den
        ci = ni / den
        for i in range(nb):
            q = i % 4
            rows = slice(BLOCK_ROWS * i, BLOCK_ROWS * (i + 1))
            br = _embed_block(btr_ref[rows, :], q)
            bi = _embed_block(bti_ref[rows, :], q)
            gbbr = dbb_ref[i, :, :LANES]
            gbbi = dbb_ref[i, :, LANES:]
            cri, cii = cr[i:i + 1, :], ci[i:i + 1, :]
            gcr_s[i:i + 1, :] = jnp.sum(gbbr * br + gbbi * bi, axis=0, keepdims=True)
            gci_s[i:i + 1, :] = jnp.sum(gbbi * br - gbbr * bi, axis=0, keepdims=True)
            gbr_ref[rows, :] = _extract_block(cri * gbbr + cii * gbbi, q)
            gbi_ref[rows, :] = _extract_block(cri * gbbi - cii * gbbr, q)
            gcre_ref[rows, :] = _extract_block(dcc_ref[i, :LANES, :].T, q)
            gcim_ref[rows, :] = -_extract_block(dcc_ref[i, LANES:, :].T, q)
        g_cr = gcr_s[...]
        g_ci = gci_s[...]
        g_nr = g_cr / den
        g_ni = g_ci / den
        g_den = -(g_cr * nr + g_ci * ni) / (den * den)
        g_ar = dar_ref[...] + g_nr * lr - g_ni * li
        g_ai = dai_ref[...] + g_nr * li + g_ni * lr
        g_lr = g_nr * (ar - 1.0) + g_ni * ai + g_den * 2.0 * lr
        g_li = g_nr * ai - g_ni * (ar - 1.0) + g_den * 2.0 * li
        g_mag = g_ar * cs + g_ai * sn
        g_th = (g_ai * cs - g_ar * sn) * mag
        g_lr = g_lr + g_mag * mag * dt
        g_li = g_li + g_th * dt
        g_dt = g_mag * mag * lr + g_th * li
        glr_ref[...] = jnp.where(lam < LAMBDA_RE_MAX, g_lr, 0.0)
        gli_ref[...] = g_li
        gl = g_dt * dt
        half = LANES // 2
        gdt_ref[:, 0:1] = jnp.sum(gl[:, :half], axis=1, keepdims=True)
        gdt_ref[:, 1:2] = jnp.sum(gl[:, half:], axis=1, keepdims=True)

    rows_shape = jax.ShapeDtypeStruct((nb * BLOCK_ROWS, HALF_LANES), F32)
    return pl.pallas_call(
        body, name="ssm_prep_bwd",
        out_shape=[jax.ShapeDtypeStruct((nb, LANES), F32), jax.ShapeDtypeStruct((nb, LANES), F32),
                   jax.ShapeDtypeStruct((nb, 2), F32), rows_shape, rows_shape, rows_shape, rows_shape],
        scratch_shapes=[pltpu.VMEM((nb, LANES), F32), pltpu.VMEM((nb, LANES), F32)],
        compiler_params=_cparams(),
    )(lam_re, lam_im, log_dt, bt_re, bt_im, dar, dai, dbb, dcc)


def _cmul(ar, ai, br, bi):
    return ar * br - ai * bi, ar * bi + ai * br


def _interleave_rows(src_ref, dst_ref):
    def step(j, carry):
        dst_ref[pl.ds(pl.multiple_of(j * 8, 8), 8), :] = src_ref[pl.ds(j, 8, stride=SCAN_CHUNK), :]
        return carry
    lax.fori_loop(0, SCAN_CHUNK, step, 0, unroll=4)


def _deinterleave_rows(src_ref, dst_ref):
    def step(j, carry):
        dst_ref[pl.ds(j, 8, stride=SCAN_CHUNK), :] = src_ref[pl.ds(pl.multiple_of(j * 8, 8), 8), :]
        return carry
    lax.fori_loop(0, SCAN_CHUNK, step, 0, unroll=4)


def _scan_inplace(re_ref, im_ref, a_re, a_im, reverse):
    nq = len(a_re)
    ch = SCAN_CHUNK
    ab_re = [jnp.broadcast_to(a, (8, LANES)) for a in a_re]
    ab_im = [jnp.broadcast_to(a, (8, LANES)) for a in a_im]

    def rows(j):
        jj = (ch - 1 - j) if reverse else j
        return pl.ds(pl.multiple_of(jj * 8, 8), 8)

    def sweep(init, store):
        def step(j, st):
            out = []
            r = rows(j)
            for qi in range(nq):
                xr, xi = st[2 * qi], st[2 * qi + 1]
                pr, pi = _cmul(ab_re[qi], ab_im[qi], xr, xi)
                xr = pr + re_ref[qi, r, :]
                xi = pi + im_ref[qi, r, :]
                if store:
                    re_ref[qi, r, :] = xr
                    im_ref[qi, r, :] = xi
                out += [xr, xi]
            return tuple(out)
        return lax.fori_loop(0, ch, step, tuple(init), unroll=2)

    zeros = [jnp.zeros((8, LANES), F32)] * (2 * nq)
    finals = sweep(zeros, store=False)

    row_id = lax.broadcasted_iota(jnp.int32, (8, LANES), 0)
    carries = []
    for qi in range(nq):
        pr, pi = ab_re[qi], ab_im[qi]
        for _ in range(8):
            pr, pi = _cmul(pr, pi, pr, pi)
        fr, fi = finals[2 * qi], finals[2 * qi + 1]
        sr = jnp.zeros((8, LANES), F32)
        si = jnp.zeros((8, LANES), F32)
        for _ in range(7):
            tr, ti = _cmul(pr, pi, sr, si)
            tr, ti = tr + fr, ti + fi
            if reverse:
                sr = jnp.where(row_id == 7, 0.0, pltpu.roll(tr, 7, axis=0))
                si = jnp.where(row_id == 7, 0.0, pltpu.roll(ti, 7, axis=0))
            else:
                sr = jnp.where(row_id == 0, 0.0, pltpu.roll(tr, 1, axis=0))
                si = jnp.where(row_id == 0, 0.0, pltpu.roll(ti, 1, axis=0))
        carries += [sr, si]
    sweep(carries, store=True)


SSM_Q = 4


def _ssm_fwd(u, are, aim, bb, cc, dskip, after=None):
    nq = SSM_Q
    deps = [] if after is None else [after]

    def body(u_ref, ar_ref, ai_ref, bb_ref, cc_ref, d_ref, *rest):
        y_ref, x_ref, sre, sim, up, yp = rest[len(deps):]
        _interleave_rows(u_ref, up)
        uf = up[...]
        ub = uf.astype(BF16)
        yp[...] = d_ref[...] * uf
        for d in range(2):
            for qi in range(nq):
                sre[qi] = _dot(ub, bb_ref[d, qi, :, :LANES])
                sim[qi] = _dot(ub, bb_ref[d, qi, :, LANES:])
            _scan_inplace(sre, sim, [ar_ref[d, qi] for qi in range(nq)], [ai_ref[d, qi] for qi in range(nq)],
                          reverse=(d == 1))
            for qi in range(nq):
                xrb = sre[qi].astype(BF16)
                xib = sim[qi].astype(BF16)
                x_ref[d, qi, :, :LANES] = xrb
                x_ref[d, qi, :, LANES:] = xib
                yp[...] += _dot(xrb, cc_ref[d, qi, :LANES, :]) + _dot(xib, cc_ref[d, qi, LANES:, :])
        _deinterleave_rows(yp, y_ref)

    blk4 = lambda k: (0, k, 0, 0)
    return pl.pallas_call(
        body, name="ssm_fwd", grid=(SSM_WIDTH // LANES,),
        in_specs=[pl.BlockSpec((SEQ, LANES), lambda k: (0, k)),
                  pl.BlockSpec((2, nq, 1, LANES), blk4), pl.BlockSpec((2, nq, 1, LANES), blk4),
                  pl.BlockSpec((2, nq, LANES, 2 * LANES), blk4), pl.BlockSpec((2, nq, 2 * LANES, LANES), blk4),
                  pl.BlockSpec((1, LANES), lambda k: (0, k))] + [pl.BlockSpec(memory_space=pl.ANY)] * len(deps),
        out_specs=[pl.BlockSpec((SEQ, LANES), lambda k: (0, k)), pl.BlockSpec((2, nq, SEQ, 2 * LANES), blk4)],
        out_shape=[jax.ShapeDtypeStruct((SEQ, SSM_WIDTH), F32),
                   jax.ShapeDtypeStruct((2, N_LANE_BLOCKS, SEQ, 2 * LANES), BF16)],
        scratch_shapes=[pltpu.VMEM((nq, SEQ, LANES), F32), pltpu.VMEM((nq, SEQ, LANES), F32),
                        pltpu.VMEM((SEQ, LANES), F32), pltpu.VMEM((SEQ, LANES), F32)],
        compiler_params=_cparams(("parallel",)),
    )(u, are, aim, bb, cc, dskip, *deps)


def _ssm_bwd(dy, u, x, are, aim, bb, cc, dskip, after=None):
    nq = SSM_Q
    body_rows = SEQ - 8
    deps = [] if after is None else [after]

    def body(dy_ref, u_ref, x_ref, ar_ref, ai_ref, bb_ref, cc_ref, d_ref, *rest):
        du_ref, dd_ref, dcc_ref, dbb_ref, dar_ref, dai_ref, sre, sim, up, dyp, dup = rest[len(deps):]
        _interleave_rows(u_ref, up)
        _interleave_rows(dy_ref, dyp)
        dyf = dyp[...]
        uf = up[...]
        dyb = dyf.astype(BF16)
        ub = uf.astype(BF16)
        dd_ref[...] = jnp.sum(dyf * uf, axis=0, keepdims=True)
        dup[...] = d_ref[...] * dyf
        row8 = lax.broadcasted_iota(jnp.int32, (8, LANES), 0)
        for d in range(2):
            for qi in range(nq):
                dx = _dot_nt(dyb, cc_ref[d, qi])
                sre[qi] = dx[:, :LANES]
                sim[qi] = dx[:, LANES:]
                dcc_ref[d, qi] = _dot_tn(x_ref[d, qi], dyb)
            _scan_inplace(sre, sim, [ar_ref[d, qi] for qi in range(nq)], [-ai_ref[d, qi] for qi in range(nq)],
                          reverse=(d == 0))
            for qi in range(nq):
                gr = sre[qi]
                gi = sim[qi]
                xrf = x_ref[d, qi, :, :LANES].astype(F32)
                xif = x_ref[d, qi, :, LANES:].astype(F32)
                if d == 0:
                    g_main_r, g_main_i = gr[8:], gi[8:]
                    x_main_r, x_main_i = xrf[:body_rows], xif[:body_rows]
                    g_edge_r, g_edge_i = gr[:8], gi[:8]
                    x_edge_r = jnp.where(row8 == 0, 0.0, pltpu.roll(xrf[body_rows:], 1, axis=0))
                    x_edge_i = jnp.where(row8 == 0, 0.0, pltpu.roll(xif[body_rows:], 1, axis=0))
                else:
                    g_main_r, g_main_i = gr[:body_rows], gi[:body_rows]
                    x_main_r, x_main_i = xrf[8:], xif[8:]
                    g_edge_r, g_edge_i = gr[body_rows:], gi[body_rows:]
                    x_edge_r = jnp.where(row8 == 7, 0.0, pltpu.roll(xrf[:8], 7, axis=0))
                    x_edge_i = jnp.where(row8 == 7, 0.0, pltpu.roll(xif[:8], 7, axis=0))
                dar_ref[d, qi] = (jnp.sum(g_main_r * x_main_r + g_main_i * x_main_i, axis=0, keepdims=True)
                                  + jnp.sum(g_edge_r * x_edge_r + g_edge_i * x_edge_i, axis=0, keepdims=True))
                dai_ref[d, qi] = (jnp.sum(g_main_i * x_main_r - g_main_r * x_main_i, axis=0, keepdims=True)
                                  + jnp.sum(g_edge_i * x_edge_r - g_edge_r * x_edge_i, axis=0, keepdims=True))
                gb = jnp.concatenate([gr, gi], axis=1).astype(BF16)
                dup[...] += _dot_nt(gb, bb_ref[d, qi])
                dbb_ref[d, qi] = _dot_tn(ub, gb)
        _deinterleave_rows(dup, du_ref)

    blk4 = lambda k: (0, k, 0, 0)
    col = lambda k: (0, k)
    bb_spec = pl.BlockSpec((2, nq, LANES, 2 * LANES), blk4)
    cc_spec = pl.BlockSpec((2, nq, 2 * LANES, LANES), blk4)
    a_spec = pl.BlockSpec((2, nq, 1, LANES), blk4)
    x_spec = pl.BlockSpec((2, nq, SEQ, 2 * LANES), blk4)
    a_shape = jax.ShapeDtypeStruct((2, N_LANE_BLOCKS, 1, LANES), F32)
    return pl.pallas_call(
        body, name="ssm_bwd", grid=(SSM_WIDTH // LANES,),
        in_specs=[pl.BlockSpec((SEQ, LANES), col), pl.BlockSpec((SEQ, LANES), col), x_spec,
                  a_spec, a_spec, bb_spec, cc_spec, pl.BlockSpec((1, LANES), col)]
        + [pl.BlockSpec(memory_space=pl.ANY)] * len(deps),
        out_specs=[pl.BlockSpec((SEQ, LANES), col), pl.BlockSpec((1, LANES), col),
                   cc_spec, bb_spec, a_spec, a_spec],
        out_shape=[jax.ShapeDtypeStruct((SEQ, SSM_WIDTH), F32), jax.ShapeDtypeStruct((1, SSM_WIDTH), F32),
                   jax.ShapeDtypeStruct((2, N_LANE_BLOCKS, 2 * LANES, LANES), F32),
                   jax.ShapeDtypeStruct((2, N_LANE_BLOCKS, LANES, 2 * LANES), F32), a_shape, a_shape],
        scratch_shapes=[pltpu.VMEM((nq, SEQ, LANES), F32), pltpu.VMEM((nq, SEQ, LANES), F32),
                        pltpu.VMEM((SEQ, LANES), F32), pltpu.VMEM((SEQ, LANES), F32), pltpu.VMEM((SEQ, LANES), F32)],
        compiler_params=_cparams(("parallel",)),
    )(dy, u, x, are, aim, bb, cc, dskip, *deps)


GELU_C = 0.7978845608028654
GELU_K = 0.044715


def _gelu(y):
    return 0.5 * y * (1.0 + jnp.tanh(GELU_C * (y + GELU_K * y * y * y)))


def _gelu_grad(y):
    t = jnp.tanh(GELU_C * (y + GELU_K * y * y * y))
    return 0.5 * (1.0 + t) + 0.5 * y * (1.0 - t * t) * GELU_C * (1.0 + 3.0 * GELU_K * y * y)


def _mixout_fwd(o, y, glu_w, glu_b, gan, gsn, wout, x1):
    tm = MIX_TM

    def body(o_ref, y_ref, gw_ref, gb_ref, gan_ref, gsn_ref, w_ref, x1_ref, x2_ref, mx_ref):
        yg = _gelu(y_ref[...])
        z = _dot(yg.astype(BF16), gw_ref[...]) + gb_ref[...]
        so = yg * _sigmoid(z)
        na = _rms_fwd(o_ref[...], gan_ref[...])
        ns = _rms_fwd(so, gsn_ref[...])
        mixed = jnp.concatenate([na, ns], axis=-1).astype(BF16)
        mx_ref[...] = mixed
        x2_ref[...] = x1_ref[...] + _dot(mixed, w_ref[...])

    row = lambda i: (i, 0)
    const = lambda i: (0, 0)
    return pl.pallas_call(
        body, name="mixout_fwd", grid=(SEQ // tm,),
        in_specs=[pl.BlockSpec((tm, ATTN_WIDTH), row), pl.BlockSpec((tm, SSM_WIDTH), row),
                  pl.BlockSpec((SSM_WIDTH, SSM_WIDTH), const), pl.BlockSpec((1, SSM_WIDTH), const),
                  pl.BlockSpec((1, ATTN_WIDTH), const), pl.BlockSpec((1, SSM_WIDTH), const),
                  pl.BlockSpec((D_MODEL, D_MODEL), const), pl.BlockSpec((tm, D_MODEL), row)],
        out_specs=[pl.BlockSpec((tm, D_MODEL), row), pl.BlockSpec((tm, D_MODEL), row)],
        out_shape=[jax.ShapeDtypeStruct((SEQ, D_MODEL), F32), jax.ShapeDtypeStruct((SEQ, D_MODEL), BF16)],
        compiler_params=_cparams(("parallel",)),
    )(o, y, glu_w, glu_b, gan, gsn, wout, x1)


def _mixout_bwd(dx2, o, y, glu_w, glu_b, gan, gsn, wout):
    tm = MIX_TM

    def body(dx2_ref, o_ref, y_ref, gw_ref, gb_ref, gan_ref, gsn_ref, w_ref,
             do_ref, dy_ref, dz_ref, yg_ref, dxb_ref, dgan_ref, dgsn_ref, dgb_ref):
        i = pl.program_id(0)
        dxb = dx2_ref[...].astype(BF16)
        dxb_ref[...] = dxb
        dmixed = _dot_nt(dxb, w_ref[...])
        do, dgan = _rms_bwd(dmixed[:, :ATTN_WIDTH], o_ref[...], gan_ref[...])
        do_ref[...] = do.T
        yv = y_ref[...]
        yg = _gelu(yv)
        ygb = yg.astype(BF16)
        yg_ref[...] = ygb
        sg = _sigmoid(_dot(ygb, gw_ref[...]) + gb_ref[...])
        dso, dgsn = _rms_bwd(dmixed[:, ATTN_WIDTH:], yg * sg, gsn_ref[...])
        dz = dso * yg * sg * (1.0 - sg)
        dzb = dz.astype(BF16)
        dz_ref[...] = dzb
        dyg = dso * sg + _dot_nt(dzb, gw_ref[...])
        dy_ref[...] = dyg * _gelu_grad(yv)
        dgb = jnp.sum(dz, axis=0, keepdims=True)

        @pl.when(i == 0)
        def _():
            dgan_ref[...] = dgan
            dgsn_ref[...] = dgsn
            dgb_ref[...] = dgb

        @pl.when(i != 0)
        def _():
            dgan_ref[...] += dgan
            dgsn_ref[...] += dgsn
            dgb_ref[...] += dgb

    row = lambda i: (i, 0)
    const = lambda i: (0, 0)
    return pl.pallas_call(
        body, name="mixout_bwd", grid=(SEQ // tm,),
        in_specs=[pl.BlockSpec((tm, D_MODEL), row), pl.BlockSpec((tm, ATTN_WIDTH), row),
                  pl.BlockSpec((tm, SSM_WIDTH), row),
                  pl.BlockSpec((SSM_WIDTH, SSM_WIDTH), const), pl.BlockSpec((1, SSM_WIDTH), const),
                  pl.BlockSpec((1, ATTN_WIDTH), const), pl.BlockSpec((1, SSM_WIDTH), const),
                  pl.BlockSpec((D_MODEL, D_MODEL), const)],
        out_specs=[pl.BlockSpec((ATTN_WIDTH, tm), lambda i: (0, i)), pl.BlockSpec((tm, SSM_WIDTH), row),
                   pl.BlockSpec((tm, SSM_WIDTH), row), pl.BlockSpec((tm, SSM_WIDTH), row),
                   pl.BlockSpec((tm, D_MODEL), row),
                   pl.BlockSpec((1, ATTN_WIDTH), const), pl.BlockSpec((1, SSM_WIDTH), const),
                   pl.BlockSpec((1, SSM_WIDTH), const)],
        out_shape=[jax.ShapeDtypeStruct((ATTN_WIDTH, SEQ), F32), jax.ShapeDtypeStruct((SEQ, SSM_WIDTH), F32),
                   jax.ShapeDtypeStruct((SEQ, SSM_WIDTH), BF16), jax.ShapeDtypeStruct((SEQ, SSM_WIDTH), BF16),
                   jax.ShapeDtypeStruct((SEQ, D_MODEL), BF16),
                   jax.ShapeDtypeStruct((1, ATTN_WIDTH), F32), jax.ShapeDtypeStruct((1, SSM_WIDTH), F32),
                   jax.ShapeDtypeStruct((1, SSM_WIDTH), F32)],
        compiler_params=_cparams(("arbitrary",)),
    )(dx2, o, y, glu_w, glu_b, gan, gsn, wout)


def _local_step(x, target, w, p, late_weights, early_grads, after=None, midway=None):
    x1, h1, a1, b1 = _ffn_fwd(x, p["norm_ffn1"], w["wgt1"], w["wut1"], w["wd1"], "ffn1_fwd", after=after)
    h2, q, k, v, u = _mixin_fwd(x1, p["norm_mix"], w["wint"])
    kp = jnp.pad(k, ((WINDOW, WINDOW), (0, 0)))
    vp = jnp.pad(v, ((WINDOW, WINDOW), (0, 0)))
    o = _attn_fwd(q, kp, vp, p["attn_sinks"])

    lam_re = p["ssm_lambda_re"].reshape(2 * N_LANE_BLOCKS, LANES)
    lam_im = p["ssm_lambda_im"].reshape(2 * N_LANE_BLOCKS, LANES)
    log_dt = jnp.repeat(p["ssm_log_dt"].reshape(2, 32), 64, axis=-1).reshape(2 * N_LANE_BLOCKS, LANES)
    a_re, a_im, bb, cc = _ssm_prep(lam_re, lam_im, log_dt, p["ssm_b_re"], p["ssm_b_im"],
                                   p["ssm_c_re"], p["ssm_c_im"])
    shape_a = (2, N_LANE_BLOCKS, 1, LANES)
    a_re4, a_im4 = a_re.reshape(shape_a), a_im.reshape(shape_a)
    bb4 = bb.reshape(2, N_LANE_BLOCKS, LANES, 2 * LANES)
    cc4 = cc.reshape(2, N_LANE_BLOCKS, 2 * LANES, LANES)
    dskip = p["ssm_d"].T.reshape(1, SSM_WIDTH)
    y, xs = _ssm_fwd(u, a_re4, a_im4, bb4, cc4, dskip, after=None if midway is None else midway(o))

    w2 = late_weights(y)
    x2, mixed = _mixout_fwd(o, y, w2["glu"], p["ssm_glu_b"], p["attn_out_norm"], p["ssm_out_norm"], w2["wout"], x1)
    dx3, h3, a3, b3, loss, d_final = _ffn_fwd(x2, p["norm_ffn2"], w2["wgt2"], w2["wut2"], w2["wd2"], "ffn2_fwd",
                                              head=(p["final_norm"], target))
    dx2, da3, db3, s3, df3, d_n2 = _ffn_bwd_act(dx3, x2, p["norm_ffn2"], a3, b3, w2["wgt2"], w2["wut2"], w2["wd2"],
                                                "ffn2_bwd_act")
    g_wgt2, g_wut2, g_wd2 = _mm_tn([(da3, h3), (db3, h3), (s3, df3)], "ffn2_bwd_w")

    do, dy, dz, ygb, dx2b, d_gan, d_gsn, d_glub = _mixout_bwd(
        dx2, o, y, w2["glu"], p["ssm_glu_b"], p["attn_out_norm"], p["ssm_out_norm"], w2["wout"])
    (g_wout,) = _mm_tn([(mixed, dx2b)], "wout_bwd_w")
    (g_glu,) = _mm_tn([(ygb, dz)], "glu_bwd_w")
    sent = early_grads(dict(glu=g_glu, wout=g_wout, wgt2=g_wgt2, wut2=g_wut2, wd2=g_wd2))

    du, d_dskip, dcc, dbb, dar, dai = _ssm_bwd(dy, u, xs, a_re4, a_im4, bb4, cc4, dskip, after=sent)
    nb = 2 * N_LANE_BLOCKS
    g_lre, g_lim, g_ldt, g_btr, g_bti, g_cre, g_cim = _ssm_prep_bwd(
        lam_re, lam_im, log_dt, p["ssm_b_re"], p["ssm_b_im"], dar.reshape(nb, LANES), dai.reshape(nb, LANES),
        dbb.reshape(nb, LANES, 2 * LANES), dcc.reshape(nb, 2 * LANES, LANES))

    dq, dkp, dvp, d_sinks = _attn_bwd(q, kp, vp, p["attn_sinks"], do)
    dk = dkp[WINDOW:WINDOW + SEQ]
    dv = dvp[WINDOW:WINDOW + SEQ]
    dx1, dproj, d_nmix = _mixin_bwd(dq, dk, dv, du, w["wint"], x1, p["norm_mix"], dx2)
    (g_wint,) = _mm_tn([(dproj, h2)], "win_bwd_w")

    dx0, da1, db1, s1, df1, d_n1 = _ffn_bwd_act(dx1, x, p["norm_ffn1"], a1, b1, w["wgt1"], w["wut1"], w["wd1"],
                                                "ffn1_bwd_act")
    g_wgt1, g_wut1, g_wd1 = _mm_tn([(da1, h1), (db1, h1), (s1, df1)], "ffn1_bwd_w")

    big = dict(wgt1=g_wgt1, wut1=g_wut1, wd1=g_wd1, wint=g_wint)
    small = dict(
        norm_ffn1=d_n1, norm_mix=d_nmix, attn_sinks=d_sinks,
        ssm_lambda_re=g_lre.reshape(64, 64), ssm_lambda_im=g_lim.reshape(64, 64),
        ssm_log_dt=g_ldt.reshape(2, 32), ssm_b_re=g_btr, ssm_b_im=g_bti, ssm_c_re=g_cre, ssm_c_im=g_cim,
        ssm_d=d_dskip.reshape(32, 16).T, ssm_glu_b=d_glub, attn_out_norm=d_gan, ssm_out_norm=d_gsn,
        norm_ffn2=d_n2, final_norm=d_final, loss=loss)
    return loss, dx0, big, small


BIG = dict(
    wgt1=("ffn1_w_gate", 352, 1024, True), wut1=("ffn1_w_up", 352, 1024, True), wd1=("ffn1_w_down", 352, 1024, False),
    wint=("w_in", 160, 1024, True), glu=("ssm_glu_w", 64, 512, False), wout=("w_out", 128, 1024, False),
    wgt2=("ffn2_w_gate", 352, 1024, True), wut2=("ffn2_w_up", 352, 1024, True), wd2=("ffn2_w_down", 352, 1024, False))

SMALL = dict(
    norm_ffn1=(1, 1024), norm_mix=(1, 1024), attn_sinks=(1, 8), ssm_lambda_re=(64, 64), ssm_lambda_im=(64, 64),
    ssm_log_dt=(2, 32), ssm_b_re=(1024, 64), ssm_b_im=(1024, 64), ssm_c_re=(1024, 64), ssm_c_im=(1024, 64),
    ssm_d=(16, 32), ssm_glu_b=(1, 512), attn_out_norm=(1, 512), ssm_out_norm=(1, 512), norm_ffn2=(1, 1024),
    final_norm=(1, 1024), loss=(1, 128))
SMALL_TRANSPOSED = ("ssm_b_re", "ssm_b_im", "ssm_d")
SMALL_PARAMS = tuple(n for n in SMALL if n != "loss")

SMALL_PAIRS = (("ssm_lambda_re", "ssm_lambda_im"), ("ssm_c_re", "ssm_c_im"), ("ssm_b_re", "ssm_b_im"))
SMALL_VECS = ("norm_ffn1", "norm_mix", "norm_ffn2", "final_norm", "ssm_glu_b", "attn_out_norm", "ssm_out_norm")
SMALL_TILES = ("ssm_log_dt", "attn_sinks", "ssm_d", "loss")


def _small_offsets():
    off, table = 0, {}
    for re, im in SMALL_PAIRS:
        table[re] = table[im] = off
        off += SMALL[re][0]
    for n in SMALL_VECS:
        table[n] = off
        off += SMALL[n][1] // LANES
    for n in SMALL_TILES:
        off = -(-off // 8) * 8
        table[n] = off
        off += SMALL[n][0]
    return table, off


SMALL_OFFSET, SMALL_USED_ROWS = _small_offsets()
SMALL_ROWS = -(-SMALL_USED_ROWS // (8 * N_DEV)) * 8 * N_DEV


def _cast_shards(shards):
    names = list(BIG)

    def body(*refs):
        ins, outs = refs[:len(names)], refs[len(names):]
        for idx in range(len(names)):
            outs[idx][...] = ins[idx][...].astype(BF16)

    return pl.pallas_call(
        body, name="cast_shards",
        out_shape=[jax.ShapeDtypeStruct((BIG[n][1], BIG[n][2]), BF16) for n in names],
        compiler_params=_cparams(),
    )(*[shards[n] for n in names])


def _peer(x, y, c, r):
    px = 1 - x if r & 4 else x
    py = 1 - y if r & 2 else y
    pc = 1 - c if r & 1 else c
    return px, py, pc


FIRST_GROUP = ("wgt1", "wut1", "wd1", "wint")
LATE_GROUP = ("glu", "wout", "wgt2", "wut2", "wd2")
ADAM_GROUPS = dict(ffn1=("wgt1", "wut1", "wd1"), w_in=("wint",), ssm_glu_w=("glu",), w_out=("wout",),
                   ffn2=("wgt2", "wut2", "wd2"))
N_PEERS = N_DEV - 1
ANY_SPEC = pl.BlockSpec(memory_space=pl.ANY)
HBM_SPEC = pl.BlockSpec(memory_space=pltpu.HBM)
SEM_SPEC = pl.BlockSpec(memory_space=pltpu.SEMAPHORE)
DATAFLOW_EFFECT = pltpu.SideEffectType.DATAFLOW_SIDE_EFFECTING


def _mesh_pos():
    x, y, c = lax.axis_index("x"), lax.axis_index("y"), lax.axis_index("c")
    return x, y, c, 4 * x + 2 * y + c


def _gather_first(first, late):
    nf, nl = len(first), len(late)

    def body(*refs):
        f_in, l_in = refs[:nf], refs[nf:nf + nl]
        f_out, l_out = refs[nf + nl:2 * nf + nl], refs[2 * nf + nl:2 * (nf + nl)]
        send_sems, recv_sems, local_sems = refs[2 * (nf + nl):]
        x, y, c, me = _mesh_pos()
        sibling = (x, y, 1 - c)
        chips = [(x, 1 - y), (1 - x, y), (1 - x, 1 - y)]

        def idx(px, py, pc):
            return 4 * px + 2 * py + pc

        def copy(k, s, block, to, src=None):
            slot = f_out[k].at[block]
            return pltpu.make_async_remote_copy(
                src_ref=slot if src is None else src, dst_ref=slot, send_sem=send_sems.at[k, s],
                recv_sem=recv_sems.at[k, s], device_id=to, device_id_type=MESH_ID)

        local = []
        for k in range(nf + nl):
            src, dst = (f_in[k], f_out[k]) if k < nf else (l_in[k - nf], l_out[k - nf])
            mine = pltpu.make_async_copy(src, dst.at[me], local_sems.at[k])
            mine.start()
            local.append(mine)
        sends = []
        for j, chip in enumerate(chips):
            for k in range(nf):
                sends.append(copy(k, 1 + j, me, (*chip, c), src=f_in[k]))
                sends[-1].start()
        for k in range(nf):
            sends.append(copy(k, 0, me, sibling, src=f_in[k]))
            sends[-1].start()
        for j, chip in enumerate(chips):
            for k in range(nf):
                copy(k, 1 + j, idx(*chip, c), (*chip, c)).wait_recv()
                sends.append(copy(k, 4 + j, idx(*chip, c), sibling))
                sends[-1].start()
        for k in range(nf):
            copy(k, 0, idx(*sibling), sibling).wait_recv()
        for j, chip in enumerate(chips):
            for k in range(nf):
                copy(k, 4 + j, idx(*chip, 1 - c), sibling).wait_recv()
        for cp in sends:
            cp.wait_send()
        for cp in local:
            cp.wait()

    return pl.pallas_call(
        body, name="gather_first",
        in_specs=[ANY_SPEC] * (nf + nl), out_specs=[ANY_SPEC] * (nf + nl),
        out_shape=[jax.ShapeDtypeStruct((N_DEV,) + s.shape, s.dtype) for s in list(first) + list(late)],
        scratch_shapes=[pltpu.SemaphoreType.DMA((nf, N_PEERS)), pltpu.SemaphoreType.DMA((nf, N_PEERS)),
                        pltpu.SemaphoreType.DMA((nf + nl,))],
        compiler_params=pltpu.CompilerParams(has_side_effects=True),
    )(*first, *late)


def _split_copy(src_refs, land_refs, send_sems, recv_sems, k, r, pos, scatter, receiving):
    x, y, c, me = pos
    px, py, pc = _peer(x, y, c, r)
    peer_idx = 4 * px + 2 * py + pc
    if scatter:
        src, dst = src_refs[k].at[peer_idx], land_refs[k].at[r - 1]
    else:
        src, dst = src_refs[k], land_refs[k].at[peer_idx if receiving else me]
    return pltpu.make_async_remote_copy(
        src_ref=src, dst_ref=dst, send_sem=send_sems.at[k * N_PEERS + r - 1],
        recv_sem=recv_sems.at[k * N_PEERS + r - 1], device_id=(px, py, pc), device_id_type=MESH_ID)


def _split_start(name, srcs, lands, scatter):
    n = len(srcs)

    def body(*refs):
        src_refs, land_refs = refs[:n], refs[n:2 * n]
        send_sems, recv_sems = refs[2 * n], refs[2 * n + 1]
        token = refs[-1]
        pos = _mesh_pos()
        for k in range(n):
            for r in range(1, N_DEV):
                _split_copy(src_refs, land_refs, send_sems, recv_sems, k, r, pos, scatter, False).start()
        token[...] = jnp.zeros_like(token)

    thru = [pltpu.HBM(a.shape, a.dtype) for a in list(srcs) + list(lands)]
    outs = pl.pallas_call(
        body, name=name,
        in_specs=[HBM_SPEC] * (2 * n),
        out_specs=[SEM_SPEC, SEM_SPEC] + [HBM_SPEC] * (2 * n) + [pl.BlockSpec(memory_space=pltpu.VMEM)],
        out_shape=[pltpu.SemaphoreType.DMA((n * N_PEERS,)), pltpu.SemaphoreType.DMA((n * N_PEERS,))] + thru
        + [jax.ShapeDtypeStruct((8, LANES), F32)],
        input_output_aliases={i: 2 + i for i in range(2 * n)},
        compiler_params=pltpu.CompilerParams(has_side_effects=DATAFLOW_EFFECT),
    )(*[pltpu.with_memory_space_constraint(a, pltpu.HBM) for a in list(srcs) + list(lands)])
    return outs[0], outs[1], outs[2:2 + n], outs[2 + n:2 + 2 * n], outs[-1]


def _split_wait(name, send_sems, recv_sems, srcs, lands, scatter, after):
    n = len(srcs)

    def body(*refs):
        src_refs, land_refs = refs[:n], refs[n:2 * n]
        send, recv = refs[2 * n], refs[2 * n + 1]
        pos = _mesh_pos()
        for k in range(n):
            for r in range(1, N_DEV):
                cp = _split_copy(src_refs, land_refs, send, recv, k, r, pos, scatter, True)
                cp.wait_send()
                cp.wait_recv()

    thru = [pltpu.HBM(a.shape, a.dtype) for a in list(srcs) + list(lands)]
    outs = pl.pallas_call(
        body, name=name,
        in_specs=[HBM_SPEC] * (2 * n) + [SEM_SPEC, SEM_SPEC, ANY_SPEC],
        out_specs=[HBM_SPEC] * (2 * n), out_shape=thru,
        input_output_aliases={i: i for i in range(2 * n)},
        compiler_params=pltpu.CompilerParams(has_side_effects=DATAFLOW_EFFECT),
    )(*srcs, *lands, send_sems, recv_sems, after)
    return outs[:n], outs[n:]


def _late_copy(passing, src_refs, land_refs, send_sems, recv_sems, k, s, pos, receiving):
    x, y, c, me = pos
    chips = [(x, 1 - y), (1 - x, y), (1 - x, 1 - y)]
    sibling = (x, y, 1 - c)

    def idx(dev):
        return 4 * dev[0] + 2 * dev[1] + dev[2]

    if passing:
        to = sibling
        block = idx((*chips[s], 1 - c)) if receiving else idx((*chips[s], c))
        src = dst = land_refs[k].at[block]
        sem = k * 3 + s
    else:
        to = sibling if s == 0 else (*chips[s - 1], c)
        src, dst = src_refs[k], land_refs[k].at[idx(to) if receiving else me]
        sem = k * 4 + s
    return pltpu.make_async_remote_copy(src_ref=src, dst_ref=dst, send_sem=send_sems.at[sem],
                                        recv_sem=recv_sems.at[sem], device_id=to, device_id_type=MESH_ID)


def _late_gather_call(name, stage, srcs, lands, sems, after=None):
    n = len(srcs)
    n_sem_in = len(sems)
    has_after = after is not None

    def body(*refs):
        src_refs, land_refs = refs[:n], refs[n:2 * n]
        sem_in = refs[2 * n:2 * n + n_sem_in]
        outs = refs[2 * n + n_sem_in + (1 if has_after else 0):]
        pos = _mesh_pos()
        if stage == 0:
            own_send, own_recv = outs[0], outs[1]
            for s in (1, 2, 3, 0):
                for k in range(n):
                    _late_copy(False, src_refs, land_refs, own_send, own_recv, k, s, pos, False).start()
            outs[-1][...] = jnp.zeros_like(outs[-1])
        elif stage == 1:
            own_recv = sem_in[1]
            pass_send, pass_recv = outs[0], outs[1]
            for s in range(3):
                for k in range(n):
                    _late_copy(False, src_refs, land_refs, sem_in[0], own_recv, k, s + 1, pos, True).wait_recv()
                    _late_copy(True, src_refs, land_refs, pass_send, pass_recv, k, s, pos, False).start()
            outs[-1][...] = jnp.zeros_like(outs[-1])
        else:
            own_send, own_recv, pass_send, pass_recv = sem_in
            for k in range(n):
                _late_copy(False, src_refs, land_refs, own_send, own_recv, k, 0, pos, True).wait_recv()
                for s in range(4):
                    _late_copy(False, src_refs, land_refs, own_send, own_recv, k, s, pos, False).wait_send()
                for s in range(3):
                    cp = _late_copy(True, src_refs, land_refs, pass_send, pass_recv, k, s, pos, True)
                    cp.wait_recv()
                    cp.wait_send()

    thru = [pltpu.HBM(a.shape, a.dtype) for a in list(srcs) + list(lands)]
    new_sems = [[pltpu.SemaphoreType.DMA((n * 4,))] * 2, [pltpu.SemaphoreType.DMA((n * 3,))] * 2, []][stage]
    extra = [] if stage == 2 else [jax.ShapeDtypeStruct((8, LANES), F32)]
    outs = pl.pallas_call(
        body, name=name,
        in_specs=[HBM_SPEC] * (2 * n) + [SEM_SPEC] * n_sem_in + [ANY_SPEC] * has_after,
        out_specs=[SEM_SPEC] * len(new_sems) + [HBM_SPEC] * (2 * n) + [pl.BlockSpec(memory_space=pltpu.VMEM)] * len(extra),
        out_shape=new_sems + thru + extra,
        input_output_aliases={i: len(new_sems) + i for i in range(2 * n)},
        compiler_params=pltpu.CompilerParams(has_side_effects=DATAFLOW_EFFECT),
    )(*[pltpu.with_memory_space_constraint(a, pltpu.HBM) for a in list(srcs) + list(lands)], *sems,
      *([after] if has_after else []))
    ns = len(new_sems)
    return list(outs[:ns]), outs[ns:ns + n], outs[ns + n:ns + 2 * n], (outs[-1] if extra else None)


N_SEND_SLOTS = 3


def _exchange_last(grads, small_packed):
    ng = len(grads)
    ch = SMALL_ROWS // N_DEV
    max_rows = max(g.shape[1] for g in grads)
    cols = grads[0].shape[2]

    def body(*refs):
        g_in, s_in = refs[:ng], refs[ng]
        outs = refs[ng + 1:]
        own_out, land, stage = outs[:ng], outs[ng:2 * ng], outs[2 * ng:3 * ng]
        s_red, s_stage = outs[3 * ng], outs[3 * ng + 1]
        (va, vb, vo, vs, sm_in, sm_out, d2d_send, d2d_recv, ici_send, ici_recv, s1_send, s1_recv, s2_send, s2_recv,
         local_sems) = outs[3 * ng + 2:]
        x, y, c, me = _mesh_pos()
        sibling = (x, y, 1 - c)
        chips = [(x, y), (x, 1 - y), (1 - x, y), (1 - x, 1 - y)]

        def idx(chip, core):
            return 4 * chip[0] + 2 * chip[1] + core

        def d2d(k, j):
            return pltpu.make_async_remote_copy(
                src_ref=g_in[k].at[idx(chips[j], 1 - c)], dst_ref=stage[k].at[j], send_sem=d2d_send.at[k, j],
                recv_sem=d2d_recv.at[k, j], device_id=sibling, device_id_type=MESH_ID)

        def ici(k, j, slot):
            rows = g_in[k].shape[1]
            return pltpu.make_async_remote_copy(
                src_ref=vo.at[slot, pl.ds(0, rows)], dst_ref=land[k].at[j - 1], send_sem=ici_send.at[k, j - 1],
                recv_sem=ici_recv.at[k, j - 1], device_id=(*chips[j], c), device_id_type=MESH_ID)

        def small_scatter(r):
            px, py, pc = _peer(x, y, c, r)
            return pltpu.make_async_remote_copy(
                src_ref=s_in.at[pl.ds(pl.multiple_of((4 * px + 2 * py + pc) * ch, 8), ch)], dst_ref=s_stage.at[me],
                send_sem=s1_send.at[r - 1], recv_sem=s1_recv.at[r - 1], device_id=(px, py, pc), device_id_type=MESH_ID)

        def small_gather(r):
            return pltpu.make_async_remote_copy(
                src_ref=sm_out, dst_ref=s_red.at[me], send_sem=s2_send.at[r - 1], recv_sem=s2_recv.at[r - 1],
                device_id=_peer(x, y, c, r), device_id_type=MESH_ID)

        for r in range(1, N_DEV):
            small_scatter(r).start()
        mine = pltpu.make_async_copy(s_in.at[pl.ds(pl.multiple_of(me * ch, 8), ch)], s_stage.at[me], local_sems.at[0])
        mine.start()
        pairs = [(k, j) for k in range(ng) for j in (1, 2, 3)] + [(k, 0) for k in range(ng)]
        for k, j in pairs:
            d2d(k, j).start()

        for r in range(1, N_DEV):
            small_scatter(r).wait_recv()
        mine.wait()
        load = pltpu.make_async_copy(s_stage, sm_in, local_sems.at[1])
        load.start()
        load.wait()
        total = sm_in[0]
        for i in range(1, N_DEV):
            total = total + sm_in[i]
        sm_out[...] = total
        for r in range(1, N_DEV):
            small_gather(r).start()
        keep = pltpu.make_async_copy(sm_out, s_red.at[me], local_sems.at[2])
        keep.start()

        in_flight = {}
        for i, (k, j) in enumerate(pairs):
            slot = i % N_SEND_SLOTS
            rows = g_in[k].shape[1]
            if slot in in_flight:
                in_flight.pop(slot).wait_send()
            d2d(k, j).wait_recv()
            la = pltpu.make_async_copy(g_in[k].at[idx(chips[j], c)], va.at[pl.ds(0, rows)], local_sems.at[3])
            lb = pltpu.make_async_copy(stage[k].at[j], vb.at[pl.ds(0, rows)], local_sems.at[4])
            la.start()
            lb.start()
            la.wait()
            lb.wait()
            total = va[pl.ds(0, rows)].astype(F32) + vb[pl.ds(0, rows)].astype(F32)
            if j == 0:
                vs[pl.ds(0, rows)] = total
                st = pltpu.make_async_copy(vs.at[pl.ds(0, rows)], own_out[k], local_sems.at[5])
                st.start()
                st.wait()
            else:
                vo[slot, pl.ds(0, rows)] = total.astype(BF16)
                cp = ici(k, j, slot)
                cp.start()
                in_flight[slot] = cp
        for cp in in_flight.values():
            cp.wait_send()

        for j in (1, 2, 3, 0):
            for k in range(ng):
                d2d(k, j).wait_send()
        for j in (1, 2, 3):
            for k in range(ng):
                ici(k, j, 0).wait_recv()
        for r in range(1, N_DEV):
            small_scatter(r).wait_send()
            small_gather(r).wait_send()
            small_gather(r).wait_recv()
        keep.wait()

    out_shape = [jax.ShapeDtypeStruct(g.shape[1:], F32) for g in grads]
    out_shape += [jax.ShapeDtypeStruct((3,) + g.shape[1:], BF16) for g in grads]
    out_shape += [jax.ShapeDtypeStruct((4,) + g.shape[1:], BF16) for g in grads]
    out_shape += [jax.ShapeDtypeStruct((N_DEV, ch, LANES), F32), jax.ShapeDtypeStruct((N_DEV, ch, LANES), F32)]
    outs = pl.pallas_call(
        body, name="exchange_last",
        in_specs=[ANY_SPEC] * (ng + 1), out_specs=[ANY_SPEC] * len(out_shape), out_shape=out_shape,
        scratch_shapes=[pltpu.VMEM((max_rows, cols), BF16), pltpu.VMEM((max_rows, cols), BF16),
                        pltpu.VMEM((N_SEND_SLOTS, max_rows, cols), BF16), pltpu.VMEM((max_rows, cols), F32),
                        pltpu.VMEM((N_DEV, ch, LANES), F32), pltpu.VMEM((ch, LANES), F32),
                        pltpu.SemaphoreType.DMA((ng, 4)), pltpu.SemaphoreType.DMA((ng, 4)),
                        pltpu.SemaphoreType.DMA((ng, 3)), pltpu.SemaphoreType.DMA((ng, 3)),
                        pltpu.SemaphoreType.DMA((N_PEERS,)), pltpu.SemaphoreType.DMA((N_PEERS,)),
                        pltpu.SemaphoreType.DMA((N_PEERS,)), pltpu.SemaphoreType.DMA((N_PEERS,)),
                        pltpu.SemaphoreType.DMA((6,))],
        compiler_params=pltpu.CompilerParams(has_side_effects=True, vmem_limit_bytes=VMEM_LIMIT),
    )(*grads, small_packed)
    return outs[:ng], outs[ng:2 * ng], outs[3 * ng].reshape(SMALL_ROWS, LANES)


def _adamw_math(w, g, m, v):
    m2 = ADAM_B1 * m + (1.0 - ADAM_B1) * g
    v2 = ADAM_B2 * v + (1.0 - ADAM_B2) * (g * g)
    m_hat = m2 / (1.0 - ADAM_B1 ** ADAM_STEP)
    v_hat = v2 / (1.0 - ADAM_B2 ** ADAM_STEP)
    delta = -ADAM_LR * (m_hat / (jnp.sqrt(v_hat) + ADAM_EPS) + ADAM_WD * w)
    return delta, m2, v2


ADAM_COL_TILE = 256


def _adamw_big(owns, parts, ws, ms, vs, name):
    n_ops = len(owns)
    shape = ws[0].shape
    own_is_blocks = owns[0].ndim == 3
    tc = ADAM_COL_TILE
    n_tiles = shape[1] // tc
    n_parts = parts[0].shape[0]

    def body(*refs):
        ins, outs = refs[:5 * n_ops], refs[5 * n_ops:9 * n_ops]
        own_s, sem = refs[9 * n_ops:]
        op = pl.program_id(0)
        cols = pl.ds(pl.multiple_of(pl.program_id(1) * tc, LANES), tc)
        for q in range(n_ops):
            own_ref, p_ref, w_ref, m_ref, v_ref = ins[5 * q:5 * q + 5]
            g_ref, d_ref, m2_ref, v2_ref = outs[4 * q:4 * q + 4]

            @pl.when(op == q)
            def _(own_ref=own_ref, p_ref=p_ref, w_ref=w_ref, m_ref=m_ref, v_ref=v_ref,
                  g_ref=g_ref, d_ref=d_ref, m2_ref=m2_ref, v2_ref=v2_ref):
                if own_is_blocks:
                    cp = pltpu.make_async_copy(own_ref.at[_mesh_pos()[3], :, cols], own_s, sem)
                else:
                    cp = pltpu.make_async_copy(own_ref.at[:, cols], own_s, sem)
                cp.start()
                cp.wait()
                g = own_s[...].astype(F32)
                for i in range(n_parts):
                    g = g + p_ref[i].astype(F32)
                delta, m2, v2 = _adamw_math(w_ref[...], g, m_ref[...], v_ref[...])
                g_ref[...] = g
                d_ref[...] = delta
                m2_ref[...] = m2
                v2_ref[...] = v2

    def tile_of(q):
        return lambda o, i: jnp.clip(i + (o - q) * n_tiles, 0, n_tiles - 1)

    in_specs, out_specs, args = [], [], []
    for q in range(n_ops):
        t = tile_of(q)
        tile = pl.BlockSpec((shape[0], tc), lambda o, i, t=t: (0, t(o, i)))
        in_specs += [ANY_SPEC, pl.BlockSpec((n_parts, shape[0], tc), lambda o, i, t=t: (0, 0, t(o, i))),
                     tile, tile, tile]
        out_specs += [tile] * 4
        args += [owns[q], parts[q], ws[q], ms[q], vs[q]]
    outs = pl.pallas_call(
        body, name=name, grid=(n_ops, n_tiles), in_specs=in_specs, out_specs=out_specs,
        out_shape=[jax.ShapeDtypeStruct(shape, F32)] * (4 * n_ops),
        scratch_shapes=[pltpu.VMEM((shape[0], tc), owns[0].dtype), pltpu.SemaphoreType.DMA(())],
        compiler_params=_cparams(("arbitrary", "arbitrary")),
    )(*args)
    return [outs[4 * q:4 * q + 4] for q in range(n_ops)]


def _pack_small(grads):
    names = list(SMALL)

    def body(*refs):
        ins, out = dict(zip(names, refs[:-1])), refs[-1]
        out[...] = jnp.zeros_like(out)
        for re, im in SMALL_PAIRS:
            off, rows = SMALL_OFFSET[re], SMALL[re][0]
            out[off:off + rows, :] = jnp.concatenate([ins[re][...], ins[im][...]], axis=1)
        for n in SMALL_VECS:
            off, vec = SMALL_OFFSET[n], ins[n][...]
            for i in range(SMALL[n][1] // LANES):
                out[off + i:off + i + 1, :] = vec[:, i * LANES:(i + 1) * LANES]
        for n in SMALL_TILES:
            off, (rows, cols) = SMALL_OFFSET[n], SMALL[n]
            out[off:off + rows, 0:cols] = ins[n][...]

    return pl.pallas_call(
        body, name="pack_small", out_shape=jax.ShapeDtypeStruct((SMALL_ROWS, LANES), F32),
        compiler_params=_cparams(),
    )(*[grads[n] for n in names])


def _unpack_small_ref(g_ref, n):
    off, (rows, cols) = SMALL_OFFSET[n], SMALL[n]
    for re, im in SMALL_PAIRS:
        if n == re:
            return g_ref[off:off + rows, 0:HALF_LANES]
        if n == im:
            return g_ref[off:off + rows, HALF_LANES:LANES]
    if n in SMALL_VECS:
        return jnp.concatenate([g_ref[off + i:off + i + 1, :] for i in range(cols // LANES)], axis=1)
    return g_ref[off:off + rows, 0:cols]


def _adamw_small(g_packed, w, m, v):
    names = list(SMALL_PARAMS)
    n = len(names)

    def body(g_ref, *refs):
        w_refs, m_refs, v_refs, outs = refs[:n], refs[n:2 * n], refs[2 * n:3 * n], refs[3 * n:]
        for idx, name in enumerate(names):
            g = _unpack_small_ref(g_ref, name)
            delta, m2, v2 = _adamw_math(w_refs[idx][...], g, m_refs[idx][...], v_refs[idx][...])
            outs[4 * idx][...] = g
            outs[4 * idx + 1][...] = delta
            outs[4 * idx + 2][...] = m2
            outs[4 * idx + 3][...] = v2
        outs[4 * n][...] = _unpack_small_ref(g_ref, "loss")

    outs = pl.pallas_call(
        body, name="adamw_small",
        out_shape=[jax.ShapeDtypeStruct(SMALL[name], F32) for name in names for _ in range(4)]
        + [jax.ShapeDtypeStruct(SMALL["loss"], F32)],
        compiler_params=_cparams(),
    )(g_packed, *[w[k] for k in names], *[m[k] for k in names], *[v[k] for k in names])
    return {name: outs[4 * idx:4 * idx + 4] for idx, name in enumerate(names)}, outs[4 * n]


WEIGHT_NAMES = ['norm_ffn1', 'ffn1_w_gate', 'ffn1_w_up', 'ffn1_w_down', 'norm_mix', 'w_in', 'attn_sinks',
                'ssm_lambda_re', 'ssm_lambda_im', 'ssm_log_dt', 'ssm_b_re', 'ssm_b_im', 'ssm_c_re', 'ssm_c_im',
                'ssm_d', 'ssm_glu_w', 'ssm_glu_b', 'attn_out_norm', 'ssm_out_norm', 'w_out', 'norm_ffn2',
                'ffn2_w_gate', 'ffn2_w_up', 'ffn2_w_down', 'final_norm']


def kernel(x, norm_ffn1, ffn1_w_gate, ffn1_w_up, ffn1_w_down, norm_mix, w_in, attn_sinks, ssm_lambda_re, ssm_lambda_im, ssm_log_dt, ssm_b_re, ssm_b_im, ssm_c_re, ssm_c_im, ssm_d, ssm_glu_w, ssm_glu_b, attn_out_norm, ssm_out_norm, w_out, norm_ffn2, ffn2_w_gate, ffn2_w_up, ffn2_w_down, final_norm, loss_target, m_norm_ffn1, m_ffn1_w_gate, m_ffn1_w_up, m_ffn1_w_down, m_norm_mix, m_w_in, m_attn_sinks, m_ssm_lambda_re, m_ssm_lambda_im, m_ssm_log_dt, m_ssm_b_re, m_ssm_b_im, m_ssm_c_re, m_ssm_c_im, m_ssm_d, m_ssm_glu_w, m_ssm_glu_b, m_attn_out_norm, m_ssm_out_norm, m_w_out, m_norm_ffn2, m_ffn2_w_gate, m_ffn2_w_up, m_ffn2_w_down, m_final_norm, v_norm_ffn1, v_ffn1_w_gate, v_ffn1_w_up, v_ffn1_w_down, v_norm_mix, v_w_in, v_attn_sinks, v_ssm_lambda_re, v_ssm_lambda_im, v_ssm_log_dt, v_ssm_b_re, v_ssm_b_im, v_ssm_c_re, v_ssm_c_im, v_ssm_d, v_ssm_glu_w, v_ssm_glu_b, v_attn_out_norm, v_ssm_out_norm, v_w_out, v_norm_ffn2, v_ffn2_w_gate, v_ffn2_w_up, v_ffn2_w_down, v_final_norm):
    args = dict(locals())
    weights = {n: args[n] for n in WEIGHT_NAMES}
    moms = {n: args["m_" + n] for n in WEIGHT_NAMES}
    vars_ = {n: args["v_" + n] for n in WEIGHT_NAMES}

    def shard2d(a, k):
        a = a.reshape(a.shape[-2], a.shape[-1])
        return a.T if BIG[k][3] else a

    def shard_master(a, k):
        return (a.T if BIG[k][3] else a).reshape(weights[BIG[k][0]].shape)

    def blocks(g, k):
        return g.reshape(N_DEV, BIG[k][1], BIG[k][2])

    def full(g, k):
        return g.reshape(N_DEV * BIG[k][1], BIG[k][2])

    shards = dict(zip(BIG, _cast_shards({k: shard2d(weights[BIG[k][0]], k) for k in BIG})))
    nf = len(FIRST_GROUP)
    got = _gather_first([shards[k] for k in FIRST_GROUP], [shards[k] for k in LATE_GROUP])
    w_first = {k: full(g, k) for k, g in zip(FIRST_GROUP, got[:nf])}
    late = {}
    late["own_sems"], late["srcs"], late["lands"], w_token = _late_gather_call(
        "gather_late_start", 0, [shards[k] for k in LATE_GROUP], got[nf:], [])

    def late_pass(dep):
        late["pass_sems"], late["srcs"], late["lands"], token = _late_gather_call(
            "gather_late_pass", 1, late["srcs"], late["lands"], late["own_sems"], after=dep)
        return token

    def late_weights(dep):
        _, _, lands, _ = _late_gather_call("gather_late_wait", 2, late["srcs"], late["lands"],
                                           late["own_sems"] + late["pass_sems"], after=dep)
        return {k: full(g, k) for k, g in zip(LATE_GROUP, lands)}

    early = {}

    def early_grads(g):
        srcs = [blocks(g[k], k) for k in LATE_GROUP]
        lands = [lax.empty((N_PEERS, BIG[k][1], BIG[k][2]), BF16) for k in LATE_GROUP]
        early["send"], early["recv"], early["srcs"], early["lands"], token = _split_start(
            "grads_late_start", srcs, lands, scatter=True)
        return token

    def small2d(a, n):
        if n in SMALL_TRANSPOSED:
            a = jnp.swapaxes(a, -1, -2)
        return a.reshape(SMALL[n])

    def small_master(a, n):
        if n in SMALL_TRANSPOSED:
            shape = weights[n].shape
            return jnp.swapaxes(a.reshape(shape[:-2] + (shape[-1], shape[-2])), -1, -2)
        return a.reshape(weights[n].shape)

    small_p = {n: small2d(weights[n], n) for n in SMALL_PARAMS}
    _, grad_x, g_first, g_small = _local_step(
        x.reshape(SEQ, D_MODEL), loss_target.reshape(SEQ, D_MODEL), w_first, small_p, late_weights, early_grads,
        after=w_token, midway=late_pass)

    own_sums, first_parts, small_grad = _exchange_last([blocks(g_first[k], k) for k in FIRST_GROUP],
                                                       _pack_small(g_small))
    own_late, late_parts = _split_wait("grads_late_wait", early["send"], early["recv"], early["srcs"],
                                       early["lands"], True, small_grad)
    own = dict(zip(FIRST_GROUP + LATE_GROUP, list(own_sums) + list(own_late)))
    parts = dict(zip(FIRST_GROUP + LATE_GROUP, list(first_parts) + list(late_parts)))
    outs = {}
    for name, group in ADAM_GROUPS.items():
        res = _adamw_big([own[k] for k in group], [parts[k] for k in group],
                         [shard2d(weights[BIG[k][0]], k) for k in group], [shard2d(moms[BIG[k][0]], k) for k in group],
                         [shard2d(vars_[BIG[k][0]], k) for k in group], "adamw_" + name)
        for k, quad in zip(group, res):
            outs[BIG[k][0]] = [shard_master(o, k) for o in quad]
    small_out, loss_row = _adamw_small(small_grad, small_p, {n: small2d(moms[n], n) for n in SMALL_PARAMS},
                                       {n: small2d(vars_[n], n) for n in SMALL_PARAMS})
    for n in SMALL_PARAMS:
        outs[n] = [small_master(o, n) for o in small_out[n]]

    result = [loss_row[0, 0], grad_x.reshape(x.shape)]
    for i in range(4):
        result += [outs[n][i] for n in WEIGHT_NAMES]
    return tuple(result)
```

```python
import functools

import jax
import jax.numpy as jnp
from jax import lax
from jax.experimental import pallas as pl
from jax.experimental.pallas import tpu as pltpu

F32 = jnp.float32
BF16 = jnp.bfloat16

N_DEV = 8
SEQ = 2048
D_MODEL = 1024
D_FF = 2816
ATTN_HEADS = 8
KV_HEADS = 2
HEAD_DIM = 64
ATTN_WIDTH = 512
KV_WIDTH = 128
WINDOW = 128
SSM_WIDTH = 512
IN_WIDTH = 1280
EPS = 1e-6
NEG_INF = -1e30
LAMBDA_RE_MAX = -1e-4
LANES = 128
N_LANE_BLOCKS = 16
SCAN_CHUNK = SEQ // 8

ADAM_LR = 0.001
ADAM_B1 = 0.9
ADAM_B2 = 0.999
ADAM_EPS = 1e-08
ADAM_WD = 0.01
ADAM_STEP = 10

VMEM_LIMIT = 56 * 1024 * 1024
MESH_ID = pl.DeviceIdType.MESH


def _cparams(sem=None):
    return pltpu.CompilerParams(dimension_semantics=sem, vmem_limit_bytes=VMEM_LIMIT)


def _dot(a, b):
    return jnp.dot(a, b, preferred_element_type=F32)


def _dot_nt(a, b):
    return lax.dot_general(a, b, (((1,), (1,)), ((), ())), preferred_element_type=F32)


def _dot_tn(a, b):
    return lax.dot_general(a, b, (((0,), (0,)), ((), ())), preferred_element_type=F32)


def _rms_fwd(x, g):
    r = lax.rsqrt(jnp.mean(x * x, axis=-1, keepdims=True) + EPS)
    return x * r * g


def _rms_bwd(dh, x, g):
    r = lax.rsqrt(jnp.mean(x * x, axis=-1, keepdims=True) + EPS)
    xh = x * r
    dg = jnp.sum(dh * xh, axis=0, keepdims=True)
    dxh = dh * g
    dx = r * (dxh - xh * jnp.mean(dxh * xh, axis=-1, keepdims=True))
    return dx, dg


def _sigmoid(x):
    return 1.0 / (1.0 + jnp.exp(-x))


FFN_TM = 512
FFN_TF = 1408


def _ffn_fwd(x, g, wgt, wut, wd, name, after=None, head=None):
    tm, tf = FFN_TM, FFN_TF
    nj = D_FF // tf
    deps = [] if after is None else [after]
    n_in = len(deps) + (2 if head else 0)

    def body(x_ref, g_ref, wg_ref, wu_ref, wd_ref, *rest):
        i = pl.program_id(0)
        j = pl.program_id(1)
        if head:
            gf_ref, t_ref = rest[len(deps):n_in]
            xo_ref, h_ref, a_ref, b_ref, loss_ref, dgf_ref, h_s, acc = rest[n_in:]
        else:
            xo_ref, h_ref, a_ref, b_ref, h_s, acc = rest[n_in:]

        @pl.when(j == 0)
        def _():
            h = _rms_fwd(x_ref[...], g_ref[...]).astype(BF16)
            h_s[...] = h
            h_ref[...] = h
            acc[...] = jnp.zeros_like(acc)

        h = h_s[...]
        a = _dot_nt(h, wg_ref[...])
        b = _dot_nt(h, wu_ref[...])
        a_ref[...] = a.astype(BF16)
        b_ref[...] = b.astype(BF16)
        s = (a * _sigmoid(a) * b).astype(BF16)
        acc[...] += _dot(s, wd_ref[...])

        @pl.when(j == nj - 1)
        def _():
            xo = x_ref[...] + 0.5 * acc[...]
            if not head:
                xo_ref[...] = xo
                return
            gf = gf_ref[...]
            err = _rms_fwd(xo, gf) - t_ref[...]
            part = jnp.broadcast_to(0.5 * jnp.sum(err * err) / D_MODEL, (1, LANES))
            dx, dgf = _rms_bwd(err * (1.0 / D_MODEL), xo, gf)
            xo_ref[...] = dx

            @pl.when(i == 0)
            def _():
                loss_ref[...] = part
                dgf_ref[...] = dgf

            @pl.when(i != 0)
            def _():
                loss_ref[...] += part
                dgf_ref[...] += dgf

    row = lambda i, j: (i, 0)
    const = lambda i, j: (0, 0)
    head_in = [pl.BlockSpec((1, D_MODEL), const), pl.BlockSpec((tm, D_MODEL), row)] if head else []
    head_out = [pl.BlockSpec((1, LANES), const), pl.BlockSpec((1, D_MODEL), const)] if head else []
    head_shape = [jax.ShapeDtypeStruct((1, LANES), F32), jax.ShapeDtypeStruct((1, D_MODEL), F32)] if head else []
    return pl.pallas_call(
        body, name=name, grid=(SEQ // tm, nj),
        in_specs=[pl.BlockSpec((tm, D_MODEL), row), pl.BlockSpec((1, D_MODEL), const),
                  pl.BlockSpec((tf, D_MODEL), lambda i, j: (j, 0)),
                  pl.BlockSpec((tf, D_MODEL), lambda i, j: (j, 0)),
                  pl.BlockSpec((tf, D_MODEL), lambda i, j: (j, 0))] + [pl.BlockSpec(memory_space=pl.ANY)] * len(deps)
        + head_in,
        out_specs=[pl.BlockSpec((tm, D_MODEL), row), pl.BlockSpec((tm, D_MODEL), row),
                   pl.BlockSpec((tm, tf), lambda i, j: (i, j)),
                   pl.BlockSpec((tm, tf), lambda i, j: (i, j))] + head_out,
        out_shape=[jax.ShapeDtypeStruct((SEQ, D_MODEL), F32), jax.ShapeDtypeStruct((SEQ, D_MODEL), BF16),
                   jax.ShapeDtypeStruct((SEQ, D_FF), BF16), jax.ShapeDtypeStruct((SEQ, D_FF), BF16)] + head_shape,
        scratch_shapes=[pltpu.VMEM((tm, D_MODEL), BF16), pltpu.VMEM((tm, D_MODEL), F32)],
        compiler_params=_cparams(("arbitrary" if head else "parallel", "arbitrary")),
    )(x, g, wgt, wut, wd, *deps, *(head or ()))


def _ffn_bwd_act(dxo, x, g, a, b, wgt, wut, wd, name):
    tm, tf = FFN_TM // 2, FFN_TF
    nj = D_FF // tf

    def body(dxo_ref, x_ref, g_ref, a_ref, b_ref, wg_ref, wu_ref, wd_ref,
             dx_ref, da_ref, db_ref, s_ref, df_ref, dg_ref, df_s, acc):
        i = pl.program_id(0)
        j = pl.program_id(1)

        @pl.when(j == 0)
        def _():
            df = (0.5 * dxo_ref[...]).astype(BF16)
            df_s[...] = df
            df_ref[...] = df
            acc[...] = jnp.zeros_like(acc)

        ds = _dot_nt(df_s[...], wd_ref[...])
        av = a_ref[...].astype(F32)
        bv = b_ref[...].astype(F32)
        sig = _sigmoid(av)
        sl = av * sig
        s_ref[...] = (sl * bv).astype(BF16)
        db = (ds * sl).astype(BF16)
        da = (ds * bv * (sig * (1.0 + av * (1.0 - sig)))).astype(BF16)
        da_ref[...] = da
        db_ref[...] = db
        acc[...] += _dot(da, wg_ref[...]) + _dot(db, wu_ref[...])

        @pl.when(j == nj - 1)
        def _():
            dx, dg = _rms_bwd(acc[...], x_ref[...], g_ref[...])
            dx_ref[...] = dxo_ref[...] + dx

            @pl.when(i == 0)
            def _():
                dg_ref[...] = dg

            @pl.when(i != 0)
            def _():
                dg_ref[...] += dg

    row = lambda i, j: (i, 0)
    col = lambda i, j: (j, 0)
    tile = lambda i, j: (i, j)
    return pl.pallas_call(
        body, name=name, grid=(SEQ // tm, nj),
        in_specs=[pl.BlockSpec((tm, D_MODEL), row), pl.BlockSpec((tm, D_MODEL), row),
                  pl.BlockSpec((1, D_MODEL), lambda i, j: (0, 0)),
                  pl.BlockSpec((tm, tf), tile), pl.BlockSpec((tm, tf), tile),
                  pl.BlockSpec((tf, D_MODEL), col), pl.BlockSpec((tf, D_MODEL), col), pl.BlockSpec((tf, D_MODEL), col)],
        out_specs=[pl.BlockSpec((tm, D_MODEL), row),
                   pl.BlockSpec((tm, tf), tile), pl.BlockSpec((tm, tf), tile), pl.BlockSpec((tm, tf), tile),
                   pl.BlockSpec((tm, D_MODEL), row),
                   pl.BlockSpec((1, D_MODEL), lambda i, j: (0, 0))],
        out_shape=[jax.ShapeDtypeStruct((SEQ, D_MODEL), F32),
                   jax.ShapeDtypeStruct((SEQ, D_FF), BF16), jax.ShapeDtypeStruct((SEQ, D_FF), BF16),
                   jax.ShapeDtypeStruct((SEQ, D_FF), BF16),
                   jax.ShapeDtypeStruct((SEQ, D_MODEL), BF16),
                   jax.ShapeDtypeStruct((1, D_MODEL), F32)],
        scratch_shapes=[pltpu.VMEM((tm, D_MODEL), BF16), pltpu.VMEM((tm, D_MODEL), F32)],
        compiler_params=_cparams(("arbitrary", "arbitrary")),
    )(dxo, x, g, a, b, wgt, wut, wd)


def _mm_tn(pairs, name, tmm=256):
    m = pairs[0][0].shape[1]
    n_pairs = len(pairs)

    def body(*refs):
        ins, outs = refs[:2 * n_pairs], refs[2 * n_pairs:]
        for p in range(n_pairs):
            outs[p][...] = _dot_tn(ins[2 * p][...], ins[2 * p + 1][...]).astype(BF16)

    in_specs, out_specs, out_shape, args = [], [], [], []
    for a, b in pairs:
        n = b.shape[1]
        in_specs += [pl.BlockSpec((SEQ, tmm), lambda i: (0, i)), pl.BlockSpec((SEQ, n), lambda i: (0, 0))]
        out_specs.append(pl.BlockSpec((tmm, n), lambda i: (i, 0)))
        out_shape.append(jax.ShapeDtypeStruct((m, n), BF16))
        args += [a, b]
    return pl.pallas_call(body, name=name, grid=(m // tmm,), in_specs=in_specs, out_specs=out_specs,
                          out_shape=out_shape, compiler_params=_cparams(("parallel",)))(*args)


MIX_TM = 256


def _mixin_fwd(x, g, wint):
    tm = MIX_TM

    def body(x_ref, g_ref, w_ref, h_ref, q_ref, k_ref, v_ref, u_ref):
        h = _rms_fwd(x_ref[...], g_ref[...]).astype(BF16)
        h_ref[...] = h
        proj = _dot_nt(h, w_ref[...])
        q_ref[...] = proj[:, :ATTN_WIDTH].T
        k_ref[...] = proj[:, ATTN_WIDTH:ATTN_WIDTH + KV_WIDTH]
        v_ref[...] = proj[:, ATTN_WIDTH + KV_WIDTH:ATTN_WIDTH + 2 * KV_WIDTH]
        u_ref[...] = proj[:, ATTN_WIDTH + 2 * KV_WIDTH:]

    row = lambda i: (i, 0)
    return pl.pallas_call(
        body, name="mixin_fwd", grid=(SEQ // tm,),
        in_specs=[pl.BlockSpec((tm, D_MODEL), row), pl.BlockSpec((1, D_MODEL), lambda i: (0, 0)),
                  pl.BlockSpec((IN_WIDTH, D_MODEL), lambda i: (0, 0))],
        out_specs=[pl.BlockSpec((tm, D_MODEL), row), pl.BlockSpec((ATTN_WIDTH, tm), lambda i: (0, i)),
                   pl.BlockSpec((tm, KV_WIDTH), row), pl.BlockSpec((tm, KV_WIDTH), row),
                   pl.BlockSpec((tm, SSM_WIDTH), row)],
        out_shape=[jax.ShapeDtypeStruct((SEQ, D_MODEL), BF16), jax.ShapeDtypeStruct((ATTN_WIDTH, SEQ), F32),
                   jax.ShapeDtypeStruct((SEQ, KV_WIDTH), F32), jax.ShapeDtypeStruct((SEQ, KV_WIDTH), F32),
                   jax.ShapeDtypeStruct((SEQ, SSM_WIDTH), F32)],
        compiler_params=_cparams(("parallel",)),
    )(x, g, wint)


def _mixin_bwd(dqt, dk, dv, du, wint, x, g, dres):
    tm = MIX_TM

    def body(dq_ref, dk_ref, dv_ref, du_ref, w_ref, x_ref, g_ref, dres_ref, dx_ref, dp_ref, dg_ref):
        i = pl.program_id(0)
        dp = jnp.concatenate([dq_ref[...].T, dk_ref[...], dv_ref[...], du_ref[...]], axis=-1).astype(BF16)
        dp_ref[...] = dp
        dh = _dot(dp, w_ref[...])
        dx, dg = _rms_bwd(dh, x_ref[...], g_ref[...])
        dx_ref[...] = dres_ref[...] + dx

        @pl.when(i == 0)
        def _():
            dg_ref[...] = dg

        @pl.when(i != 0)
        def _():
            dg_ref[...] += dg

    row = lambda i: (i, 0)
    const = lambda i: (0, 0)
    return pl.pallas_call(
        body, name="mixin_bwd", grid=(SEQ // tm,),
        in_specs=[pl.BlockSpec((ATTN_WIDTH, tm), lambda i: (0, i)), pl.BlockSpec((tm, KV_WIDTH), row),
                  pl.BlockSpec((tm, KV_WIDTH), row), pl.BlockSpec((tm, SSM_WIDTH), row),
                  pl.BlockSpec((IN_WIDTH, D_MODEL), const), pl.BlockSpec((tm, D_MODEL), row),
                  pl.BlockSpec((1, D_MODEL), const), pl.BlockSpec((tm, D_MODEL), row)],
        out_specs=[pl.BlockSpec((tm, D_MODEL), row), pl.BlockSpec((tm, IN_WIDTH), row),
                   pl.BlockSpec((1, D_MODEL), const)],
        out_shape=[jax.ShapeDtypeStruct((SEQ, D_MODEL), F32), jax.ShapeDtypeStruct((SEQ, IN_WIDTH), BF16),
                   jax.ShapeDtypeStruct((1, D_MODEL), F32)],
        compiler_params=_cparams(("arbitrary",)),
    )(dqt, dk, dv, du, wint, x, g, dres)


N_QBLOCKS = SEQ // WINDOW
GROUP = ATTN_HEADS // KV_HEADS
SCALE = HEAD_DIM ** -0.5


def _alibi_slope(h):
    return 2.0 ** (-8.0 * (h + 1) / ATTN_HEADS)


def _window_masks(n):
    s_idx = lax.broadcasted_iota(jnp.int32, (3 * WINDOW, WINDOW), 0)
    t_idx = lax.broadcasted_iota(jnp.int32, (3 * WINDOW, WINDOW), 1)
    absrel = jnp.abs(s_idx - WINDOW - t_idx)
    key_pos = n * WINDOW - WINDOW + s_idx
    valid = (absrel <= WINDOW) & (key_pos >= 0) & (key_pos < SEQ)
    return absrel.astype(F32), valid


def _group_cols(ref, r0, gi):
    return jnp.concatenate(
        [ref[(gi * GROUP + hh) * HEAD_DIM:(gi * GROUP + hh + 1) * HEAD_DIM, pl.ds(r0, WINDOW)].astype(BF16)
         for hh in range(GROUP)], axis=1)


def _group_probs(qgt, kw, absrel, valid, gi, sk_ref):
    bias = jnp.concatenate([jnp.where(valid, -_alibi_slope(gi * GROUP + hh) * absrel, NEG_INF)
                            for hh in range(GROUP)], axis=1)
    sink = jnp.concatenate([jnp.full((1, WINDOW), sk_ref[0, gi * GROUP + hh], F32) for hh in range(GROUP)], axis=1)
    s = _dot(kw, qgt) * SCALE + bias
    m = jnp.maximum(jnp.max(s, axis=0, keepdims=True), sink)
    p = jnp.exp(s - m)
    ps = jnp.exp(sink - m)
    inv = 1.0 / (jnp.sum(p, axis=0, keepdims=True) + ps)
    return p * inv, ps * inv


def _attn_fwd(qt, kp, vp, sinks):
    def body(sk_ref, qt_ref, kp_ref, vp_ref, o_ref):
        def blk(n, carry):
            r0 = pl.multiple_of(n * WINDOW, WINDOW)
            absrel, valid = _window_masks(n)
            for gi in range(KV_HEADS):
                kw = kp_ref[pl.ds(r0, 3 * WINDOW), gi * HEAD_DIM:(gi + 1) * HEAD_DIM].astype(BF16)
                vw = vp_ref[pl.ds(r0, 3 * WINDOW), gi * HEAD_DIM:(gi + 1) * HEAD_DIM].astype(BF16)
                pr, _ = _group_probs(_group_cols(qt_ref, r0, gi), kw, absrel, valid, gi, sk_ref)
                og = _dot_tn(pr.astype(BF16), vw)
                for hh in range(GROUP):
                    h = gi * GROUP + hh
                    o_ref[pl.ds(r0, WINDOW), h * HEAD_DIM:(h + 1) * HEAD_DIM] = og[hh * WINDOW:(hh + 1) * WINDOW]
            return carry

        lax.fori_loop(0, N_QBLOCKS, blk, 0)

    vmem = pl.BlockSpec(memory_space=pltpu.VMEM)
    return pl.pallas_call(
        body, name="attn_fwd",
        in_specs=[pl.BlockSpec(memory_space=pltpu.SMEM), vmem, vmem, vmem], out_specs=vmem,
        out_shape=jax.ShapeDtypeStruct((SEQ, ATTN_WIDTH), F32),
        compiler_params=_cparams(),
    )(sinks, qt, kp, vp)


def _attn_bwd(qt, kp, vp, sinks, dot_):
    def body(sk_ref, qt_ref, kp_ref, vp_ref, dot_ref, dqt_ref, dkp_ref, dvp_ref, dsk_ref, dsk_acc):
        dkp_ref[...] = jnp.zeros_like(dkp_ref)
        dvp_ref[...] = jnp.zeros_like(dvp_ref)
        dsk_acc[...] = jnp.zeros_like(dsk_acc)

        def blk(n, carry):
            r0 = pl.multiple_of(n * WINDOW, WINDOW)
            absrel, valid = _window_masks(n)
            for gi in range(KV_HEADS):
                gcols = slice(gi * HEAD_DIM, (gi + 1) * HEAD_DIM)
                kw = kp_ref[pl.ds(r0, 3 * WINDOW), gcols].astype(BF16)
                vw = vp_ref[pl.ds(r0, 3 * WINDOW), gcols].astype(BF16)
                qgt = _group_cols(qt_ref, r0, gi)
                dogt = _group_cols(dot_ref, r0, gi)
                pr, psink = _group_probs(qgt, kw, absrel, valid, gi, sk_ref)
                dp = _dot(vw, dogt)
                delta = jnp.sum(pr * dp, axis=0, keepdims=True)
                ds = (pr * (dp - delta)).astype(BF16)
                dsk_acc[gi:gi + 1, :] += -(psink * delta)
                dqgt = _dot_tn(kw, ds) * SCALE
                for hh in range(GROUP):
                    h = gi * GROUP + hh
                    dqt_ref[h * HEAD_DIM:(h + 1) * HEAD_DIM, pl.ds(r0, WINDOW)] = dqgt[:, hh * WINDOW:(hh + 1) * WINDOW]
                dkp_ref[pl.ds(r0, 3 * WINDOW), gcols] += _dot_nt(ds, qgt) * SCALE
                dvp_ref[pl.ds(r0, 3 * WINDOW), gcols] += _dot_nt(pr.astype(BF16), dogt)
            return carry

        lax.fori_loop(0, N_QBLOCKS, blk, 0)
        for h in range(ATTN_HEADS):
            gi, hh = divmod(h, GROUP)
            dsk_ref[:, h:h + 1] = jnp.sum(dsk_acc[gi:gi + 1, hh * WINDOW:(hh + 1) * WINDOW], axis=1, keepdims=True)

    vmem = pl.BlockSpec(memory_space=pltpu.VMEM)
    return pl.pallas_call(
        body, name="attn_bwd",
        in_specs=[pl.BlockSpec(memory_space=pltpu.SMEM), vmem, vmem, vmem, vmem],
        out_specs=[vmem, vmem, vmem, vmem],
        out_shape=[jax.ShapeDtypeStruct((ATTN_WIDTH, SEQ), F32),
                   jax.ShapeDtypeStruct((SEQ + 2 * WINDOW, KV_WIDTH), F32),
                   jax.ShapeDtypeStruct((SEQ + 2 * WINDOW, KV_WIDTH), F32),
                   jax.ShapeDtypeStruct((1, ATTN_HEADS), F32)],
        scratch_shapes=[pltpu.VMEM((KV_HEADS, GROUP * WINDOW), F32)],
        compiler_params=_cparams(),
    )(sinks, qt, kp, vp, dot_)


HALF_LANES = LANES // 2
BLOCK_ROWS = 32


def _embed_block(bt, q):
    z = jnp.zeros((16, HALF_LANES), bt.dtype)
    blk = jnp.concatenate([jnp.concatenate([bt[:16], z], axis=1), jnp.concatenate([z, bt[16:]], axis=1)], axis=0)
    parts = [jnp.zeros((BLOCK_ROWS * q, LANES), bt.dtype)] if q else []
    parts.append(blk)
    if q < 3:
        parts.append(jnp.zeros((BLOCK_ROWS * (3 - q), LANES), bt.dtype))
    return jnp.concatenate(parts, axis=0)


def _extract_block(m, q):
    blk = m[BLOCK_ROWS * q:BLOCK_ROWS * (q + 1)]
    return jnp.concatenate([blk[:16, :HALF_LANES], blk[16:, HALF_LANES:]], axis=0)


def _ssm_prep(lam_re, lam_im, log_dt, bt_re, bt_im, c_re, c_im):
    nb = 2 * N_LANE_BLOCKS

    def body(lr_ref, li_ref, ldt_ref, btr_ref, bti_ref, ctr_ref, cti_ref, ar_ref, ai_ref, bb_ref, cc_ref):
        lr = jnp.minimum(lr_ref[...], LAMBDA_RE_MAX)
        li = li_ref[...]
        dt = jnp.exp(ldt_ref[...])
        mag = jnp.exp(lr * dt)
        ar = mag * jnp.cos(li * dt)
        ai = mag * jnp.sin(li * dt)
        den = lr * lr + li * li
        cr = ((ar - 1.0) * lr + ai * li) / den
        ci = (ai * lr - (ar - 1.0) * li) / den
        ar_ref[...] = ar
        ai_ref[...] = ai
        for i in range(nb):
            q = i % 4
            rows = slice(BLOCK_ROWS * i, BLOCK_ROWS * (i + 1))
            br = _embed_block(btr_ref[rows, :], q)
            bi = _embed_block(bti_ref[rows, :], q)
            cri, cii = cr[i:i + 1, :], ci[i:i + 1, :]
            bb_ref[i] = jnp.concatenate([cri * br - cii * bi, cri * bi + cii * br], axis=1).astype(BF16)
            cc_ref[i] = jnp.concatenate([_embed_block(ctr_ref[rows, :], q).T,
                                         -_embed_block(cti_ref[rows, :], q).T], axis=0).astype(BF16)

    return pl.pallas_call(
        body, name="ssm_prep",
        out_shape=[jax.ShapeDtypeStruct((nb, LANES), F32), jax.ShapeDtypeStruct((nb, LANES), F32),
                   jax.ShapeDtypeStruct((nb, LANES, 2 * LANES), BF16),
                   jax.ShapeDtypeStruct((nb, 2 * LANES, LANES), BF16)],
        compiler_params=_cparams(),
    )(lam_re, lam_im, log_dt, bt_re, bt_im, c_re, c_im)


def _ssm_prep_bwd(lam_re, lam_im, log_dt, bt_re, bt_im, dar, dai, dbb, dcc):
    nb = 2 * N_LANE_BLOCKS

    def body(lr_ref, li_ref, ldt_ref, btr_ref, bti_ref, dar_ref, dai_ref, dbb_ref, dcc_ref,
             glr_ref, gli_ref, gdt_ref, gbr_ref, gbi_ref, gcre_ref, gcim_ref, gcr_s, gci_s):
        lam = lr_ref[...]
        lr = jnp.minimum(lam, LAMBDA_RE_MAX)
        li = li_ref[...]
        dt = jnp.exp(ldt_ref[...])
        mag = jnp.exp(lr * dt)
        cs = jnp.cos(li * dt)
        sn = jnp.sin(li * dt)
        ar = mag * cs
        ai = mag * sn
        den = lr * lr + li * li
        nr = (ar - 1.0) * lr + ai * li
        ni = ai * lr - (ar - 1.0) * li
        cr = nr / den
        ci = ni / den
        for i in range(nb):
            q = i % 4
            rows = slice(BLOCK_ROWS * i, BLOCK_ROWS * (i + 1))
            br = _embed_block(btr_ref[rows, :], q)
            bi = _embed_block(bti_ref[rows, :], q)
            gbbr = dbb_ref[i, :, :LANES]
            gbbi = dbb_ref[i, :, LANES:]
            cri, cii = cr[i:i + 1, :], ci[i:i + 1, :]
            gcr_s[i:i + 1, :] = jnp.sum(gbbr * br + gbbi * bi, axis=0, keepdims=True)
            gci_s[i:i + 1, :] = jnp.sum(gbbi * br - gbbr * bi, axis=0, keepdims=True)
            gbr_ref[rows, :] = _extract_block(cri * gbbr + cii * gbbi, q)
            gbi_ref[rows, :] = _extract_block(cri * gbbi - cii * gbbr, q)
            gcre_ref[rows, :] = _extract_block(dcc_ref[i, :LANES, :].T, q)
            gcim_ref[rows, :] = -_extract_block(dcc_ref[i, LANES:, :].T, q)
        g_cr = gcr_s[...]
        g_ci = gci_s[...]
        g_nr = g_cr / den
        g_ni = g_ci / den
        g_den = -(g_cr * nr + g_ci * ni) / (den * den)
        g_ar = dar_ref[...] + g_nr * lr - g_ni * li
        g_ai = dai_ref[...] + g_nr * li + g_ni * lr
        g_lr = g_nr * (ar - 1.0) + g_ni * ai + g_den * 2.0 * lr
        g_li = g_nr * ai - g_ni * (ar - 1.0) + g_den * 2.0 * li
        g_mag = g_ar * cs + g_ai * sn
        g_th = (g_ai * cs - g_ar * sn) * mag
        g_lr = g_lr + g_mag * mag * dt
        g_li = g_li + g_th * dt
        g_dt = g_mag * mag * lr + g_th * li
        glr_ref[...] = jnp.where(lam < LAMBDA_RE_MAX, g_lr, 0.0)
        gli_ref[...] = g_li
        gl = g_dt * dt
        half = LANES // 2
        gdt_ref[:, 0:1] = jnp.sum(gl[:, :half], axis=1, keepdims=True)
        gdt_ref[:, 1:2] = jnp.sum(gl[:, half:], axis=1, keepdims=True)

    rows_shape = jax.ShapeDtypeStruct((nb * BLOCK_ROWS, HALF_LANES), F32)
    return pl.pallas_call(
        body, name="ssm_prep_bwd",
        out_shape=[jax.ShapeDtypeStruct((nb, LANES), F32), jax.ShapeDtypeStruct((nb, LANES), F32),
                   jax.ShapeDtypeStruct((nb, 2), F32), rows_shape, rows_shape, rows_shape, rows_shape],
        scratch_shapes=[pltpu.VMEM((nb, LANES), F32), pltpu.VMEM((nb, LANES), F32)],
        compiler_params=_cparams(),
    )(lam_re, lam_im, log_dt, bt_re, bt_im, dar, dai, dbb, dcc)


def _cmul(ar, ai, br, bi):
    return ar * br - ai * bi, ar * bi + ai * br


def _interleave_rows(src_ref, dst_ref):
    def step(j, carry):
        dst_ref[pl.ds(pl.multiple_of(j * 8, 8), 8), :] = src_ref[pl.ds(j, 8, stride=SCAN_CHUNK), :]
        return carry
    lax.fori_loop(0, SCAN_CHUNK, step, 0, unroll=4)


def _deinterleave_rows(src_ref, dst_ref):
    def step(j, carry):
        dst_ref[pl.ds(j, 8, stride=SCAN_CHUNK), :] = src_ref[pl.ds(pl.multiple_of(j * 8, 8), 8), :]
        return carry
    lax.fori_loop(0, SCAN_CHUNK, step, 0, unroll=4)


def _scan_inplace(re_ref, im_ref, a_re, a_im, reverse):
    nq = len(a_re)
    ch = SCAN_CHUNK
    ab_re = [jnp.broadcast_to(a, (8, LANES)) for a in a_re]
    ab_im = [jnp.broadcast_to(a, (8, LANES)) for a in a_im]

    def rows(j):
        jj = (ch - 1 - j) if reverse else j
        return pl.ds(pl.multiple_of(jj * 8, 8), 8)

    def sweep(init, store):
        def step(j, st):
            out = []
            r = rows(j)
            for qi in range(nq):
                xr, xi = st[2 * qi], st[2 * qi + 1]
                pr, pi = _cmul(ab_re[qi], ab_im[qi], xr, xi)
                xr = pr + re_ref[qi, r, :]
                xi = pi + im_ref[qi, r, :]
                if store:
                    re_ref[qi, r, :] = xr
                    im_ref[qi, r, :] = xi
                out += [xr, xi]
            return tuple(out)
        return lax.fori_loop(0, ch, step, tuple(init), unroll=2)

    zeros = [jnp.zeros((8, LANES), F32)] * (2 * nq)
    finals = sweep(zeros, store=False)

    row_id = lax.broadcasted_iota(jnp.int32, (8, LANES), 0)
    carries = []
    for qi in range(nq):
        pr, pi = ab_re[qi], ab_im[qi]
        for _ in range(8):
            pr, pi = _cmul(pr, pi, pr, pi)
        fr, fi = finals[2 * qi], finals[2 * qi + 1]
        sr = jnp.zeros((8, LANES), F32)
        si = jnp.zeros((8, LANES), F32)
        for _ in range(7):
            tr, ti = _cmul(pr, pi, sr, si)
            tr, ti = tr + fr, ti + fi
            if reverse:
                sr = jnp.where(row_id == 7, 0.0, pltpu.roll(tr, 7, axis=0))
                si = jnp.where(row_id == 7, 0.0, pltpu.roll(ti, 7, axis=0))
            else:
                sr = jnp.where(row_id == 0, 0.0, pltpu.roll(tr, 1, axis=0))
                si = jnp.where(row_id == 0, 0.0, pltpu.roll(ti, 1, axis=0))
        carries += [sr, si]
    sweep(carries, store=True)


SSM_Q = 4


def _ssm_fwd(u, are, aim, bb, cc, dskip, after=None):
    nq = SSM_Q
    deps = [] if after is None else [after]

    def body(u_ref, ar_ref, ai_ref, bb_ref, cc_ref, d_ref, *rest):
        y_ref, xr_ref, xi_ref, sre, sim, up, yp = rest[len(deps):]
        _interleave_rows(u_ref, up)
        uf = up[...]
        ub = uf.astype(BF16)
        yp[...] = d_ref[...] * uf
        for d in range(2):
            for qi in range(nq):
                sre[qi] = _dot(ub, bb_ref[d, qi, :, :LANES])
                sim[qi] = _dot(ub, bb_ref[d, qi, :, LANES:])
            _scan_inplace(sre, sim, [ar_ref[d, qi] for qi in range(nq)], [ai_ref[d, qi] for qi in range(nq)],
                          reverse=(d == 1))
            for qi in range(nq):
                xrb = sre[qi].astype(BF16)
                xib = sim[qi].astype(BF16)
                xr_ref[d, qi] = xrb
                xi_ref[d, qi] = xib
                yp[...] += _dot(xrb, cc_ref[d, qi, :LANES, :]) + _dot(xib, cc_ref[d, qi, LANES:, :])
        _deinterleave_rows(yp, y_ref)

    blk4 = lambda k: (0, k, 0, 0)
    return pl.pallas_call(
        body, name="ssm_fwd", grid=(SSM_WIDTH // LANES,),
        in_specs=[pl.BlockSpec((SEQ, LANES), lambda k: (0, k)),
                  pl.BlockSpec((2, nq, 1, LANES), blk4), pl.BlockSpec((2, nq, 1, LANES), blk4),
                  pl.BlockSpec((2, nq, LANES, 2 * LANES), blk4), pl.BlockSpec((2, nq, 2 * LANES, LANES), blk4),
                  pl.BlockSpec((1, LANES), lambda k: (0, k))] + [pl.BlockSpec(memory_space=pl.ANY)] * len(deps),
        out_specs=[pl.BlockSpec((SEQ, LANES), lambda k: (0, k)),
                   pl.BlockSpec((2, nq, SEQ, LANES), blk4), pl.BlockSpec((2, nq, SEQ, LANES), blk4)],
        out_shape=[jax.ShapeDtypeStruct((SEQ, SSM_WIDTH), F32),
                   jax.ShapeDtypeStruct((2, N_LANE_BLOCKS, SEQ, LANES), BF16),
                   jax.ShapeDtypeStruct((2, N_LANE_BLOCKS, SEQ, LANES), BF16)],
        scratch_shapes=[pltpu.VMEM((nq, SEQ, LANES), F32), pltpu.VMEM((nq, SEQ, LANES), F32),
                        pltpu.VMEM((SEQ, LANES), F32), pltpu.VMEM((SEQ, LANES), F32)],
        compiler_params=_cparams(("parallel",)),
    )(u, are, aim, bb, cc, dskip, *deps)


def _ssm_bwd(dy, u, xr, xi, are, aim, bb, cc, dskip, after=None):
    nq = SSM_Q
    body_rows = SEQ - 8
    deps = [] if after is None else [after]

    def body(dy_ref, u_ref, xr_ref, xi_ref, ar_ref, ai_ref, bb_ref, cc_ref, d_ref, *rest):
        du_ref, dd_ref, dcc_ref, dbb_ref, dar_ref, dai_ref, sre, sim, up, dyp, dup = rest[len(deps):]
        _interleave_rows(u_ref, up)
        _interleave_rows(dy_ref, dyp)
        dyf = dyp[...]
        uf = up[...]
        dyb = dyf.astype(BF16)
        ub = uf.astype(BF16)
        dd_ref[...] = jnp.sum(dyf * uf, axis=0, keepdims=True)
        dup[...] = d_ref[...] * dyf
        row8 = lax.broadcasted_iota(jnp.int32, (8, LANES), 0)
        for d in range(2):
            for qi in range(nq):
                dx = _dot_nt(dyb, cc_ref[d, qi])
                sre[qi] = dx[:, :LANES]
                sim[qi] = dx[:, LANES:]
                dcc_ref[d, qi] = _dot_tn(jnp.concatenate([xr_ref[d, qi], xi_ref[d, qi]], axis=1), dyb)
            _scan_inplace(sre, sim, [ar_ref[d, qi] for qi in range(nq)], [-ai_ref[d, qi] for qi in range(nq)],
                          reverse=(d == 0))
            for qi in range(nq):
                gr = sre[qi]
                gi = sim[qi]
                xrf = xr_ref[d, qi].astype(F32)
                xif = xi_ref[d, qi].astype(F32)
                if d == 0:
                    g_main_r, g_main_i = gr[8:], gi[8:]
                    x_main_r, x_main_i = xrf[:body_rows], xif[:body_rows]
                    g_edge_r, g_edge_i = gr[:8], gi[:8]
                    x_edge_r = jnp.where(row8 == 0, 0.0, pltpu.roll(xrf[body_rows:], 1, axis=0))
                    x_edge_i = jnp.where(row8 == 0, 0.0, pltpu.roll(xif[body_rows:], 1, axis=0))
                else:
                    g_main_r, g_main_i = gr[:body_rows], gi[:body_rows]
                    x_main_r, x_main_i = xrf[8:], xif[8:]
                    g_edge_r, g_edge_i = gr[body_rows:], gi[body_rows:]
                    x_edge_r = jnp.where(row8 == 7, 0.0, pltpu.roll(xrf[:8], 7, axis=0))
                    x_edge_i = jnp.where(row8 == 7, 0.0, pltpu.roll(xif[:8], 7, axis=0))
                dar_ref[d, qi] = (jnp.sum(g_main_r * x_main_r + g_main_i * x_main_i, axis=0, keepdims=True)
                                  + jnp.sum(g_edge_r * x_edge_r + g_edge_i * x_edge_i, axis=0, keepdims=True))
                dai_ref[d, qi] = (jnp.sum(g_main_i * x_main_r - g_main_r * x_main_i, axis=0, keepdims=True)
                                  + jnp.sum(g_edge_i * x_edge_r - g_edge_r * x_edge_i, axis=0, keepdims=True))
                gb = jnp.concatenate([gr, gi], axis=1).astype(BF16)
                dup[...] += _dot_nt(gb, bb_ref[d, qi])
                dbb_ref[d, qi] = _dot_tn(ub, gb)
        _deinterleave_rows(dup, du_ref)

    blk4 = lambda k: (0, k, 0, 0)
    col = lambda k: (0, k)
    bb_spec = pl.BlockSpec((2, nq, LANES, 2 * LANES), blk4)
    cc_spec = pl.BlockSpec((2, nq, 2 * LANES, LANES), blk4)
    a_spec = pl.BlockSpec((2, nq, 1, LANES), blk4)
    x_spec = pl.BlockSpec((2, nq, SEQ, LANES), blk4)
    a_shape = jax.ShapeDtypeStruct((2, N_LANE_BLOCKS, 1, LANES), F32)
    return pl.pallas_call(
        body, name="ssm_bwd", grid=(SSM_WIDTH // LANES,),
        in_specs=[pl.BlockSpec((SEQ, LANES), col), pl.BlockSpec((SEQ, LANES), col), x_spec, x_spec,
                  a_spec, a_spec, bb_spec, cc_spec, pl.BlockSpec((1, LANES), col)]
        + [pl.BlockSpec(memory_space=pl.ANY)] * len(deps),
        out_specs=[pl.BlockSpec((SEQ, LANES), col), pl.BlockSpec((1, LANES), col),
                   cc_spec, bb_spec, a_spec, a_spec],
        out_shape=[jax.ShapeDtypeStruct((SEQ, SSM_WIDTH), F32), jax.ShapeDtypeStruct((1, SSM_WIDTH), F32),
                   jax.ShapeDtypeStruct((2, N_LANE_BLOCKS, 2 * LANES, LANES), F32),
                   jax.ShapeDtypeStruct((2, N_LANE_BLOCKS, LANES, 2 * LANES), F32), a_shape, a_shape],
        scratch_shapes=[pltpu.VMEM((nq, SEQ, LANES), F32), pltpu.VMEM((nq, SEQ, LANES), F32),
                        pltpu.VMEM((SEQ, LANES), F32), pltpu.VMEM((SEQ, LANES), F32), pltpu.VMEM((SEQ, LANES), F32)],
        compiler_params=_cparams(("parallel",)),
    )(dy, u, xr, xi, are, aim, bb, cc, dskip, *deps)


GELU_C = 0.7978845608028654
GELU_K = 0.044715


def _gelu(y):
    return 0.5 * y * (1.0 + jnp.tanh(GELU_C * (y + GELU_K * y * y * y)))


def _gelu_grad(y):
    t = jnp.tanh(GELU_C * (y + GELU_K * y * y * y))
    return 0.5 * (1.0 + t) + 0.5 * y * (1.0 - t * t) * GELU_C * (1.0 + 3.0 * GELU_K * y * y)


def _mixout_fwd(o, y, glu_w, glu_b, gan, gsn, wout, x1):
    tm = MIX_TM

    def body(o_ref, y_ref, gw_ref, gb_ref, gan_ref, gsn_ref, w_ref, x1_ref, x2_ref, mx_ref):
        yg = _gelu(y_ref[...])
        z = _dot(yg.astype(BF16), gw_ref[...]) + gb_ref[...]
        so = yg * _sigmoid(z)
        na = _rms_fwd(o_ref[...], gan_ref[...])
        ns = _rms_fwd(so, gsn_ref[...])
        mixed = jnp.concatenate([na, ns], axis=-1).astype(BF16)
        mx_ref[...] = mixed
        x2_ref[...] = x1_ref[...] + _dot(mixed, w_ref[...])

    row = lambda i: (i, 0)
    const = lambda i: (0, 0)
    return pl.pallas_call(
        body, name="mixout_fwd", grid=(SEQ // tm,),
        in_specs=[pl.BlockSpec((tm, ATTN_WIDTH), row), pl.BlockSpec((tm, SSM_WIDTH), row),
                  pl.BlockSpec((SSM_WIDTH, SSM_WIDTH), const), pl.BlockSpec((1, SSM_WIDTH), const),
                  pl.BlockSpec((1, ATTN_WIDTH), const), pl.BlockSpec((1, SSM_WIDTH), const),
                  pl.BlockSpec((D_MODEL, D_MODEL), const), pl.BlockSpec((tm, D_MODEL), row)],
        out_specs=[pl.BlockSpec((tm, D_MODEL), row), pl.BlockSpec((tm, D_MODEL), row)],
        out_shape=[jax.ShapeDtypeStruct((SEQ, D_MODEL), F32), jax.ShapeDtypeStruct((SEQ, D_MODEL), BF16)],
        compiler_params=_cparams(("parallel",)),
    )(o, y, glu_w, glu_b, gan, gsn, wout, x1)


def _mixout_bwd(dx2, o, y, glu_w, glu_b, gan, gsn, wout):
    tm = MIX_TM

    def body(dx2_ref, o_ref, y_ref, gw_ref, gb_ref, gan_ref, gsn_ref, w_ref,
             do_ref, dy_ref, dz_ref, yg_ref, dxb_ref, dgan_ref, dgsn_ref, dgb_ref):
        i = pl.program_id(0)
        dxb = dx2_ref[...].astype(BF16)
        dxb_ref[...] = dxb
        dmixed = _dot_nt(dxb, w_ref[...])
        do, dgan = _rms_bwd(dmixed[:, :ATTN_WIDTH], o_ref[...], gan_ref[...])
        do_ref[...] = do.T
        yv = y_ref[...]
        yg = _gelu(yv)
        ygb = yg.astype(BF16)
        yg_ref[...] = ygb
        sg = _sigmoid(_dot(ygb, gw_ref[...]) + gb_ref[...])
        dso, dgsn = _rms_bwd(dmixed[:, ATTN_WIDTH:], yg * sg, gsn_ref[...])
        dz = dso * yg * sg * (1.0 - sg)
        dzb = dz.astype(BF16)
        dz_ref[...] = dzb
        dyg = dso * sg + _dot_nt(dzb, gw_ref[...])
        dy_ref[...] = dyg * _gelu_grad(yv)
        dgb = jnp.sum(dz, axis=0, keepdims=True)

        @pl.when(i == 0)
        def _():
            dgan_ref[...] = dgan
            dgsn_ref[...] = dgsn
            dgb_ref[...] = dgb

        @pl.when(i != 0)
        def _():
            dgan_ref[...] += dgan
            dgsn_ref[...] += dgsn
            dgb_ref[...] += dgb

    row = lambda i: (i, 0)
    const = lambda i: (0, 0)
    return pl.pallas_call(
        body, name="mixout_bwd", grid=(SEQ // tm,),
        in_specs=[pl.BlockSpec((tm, D_MODEL), row), pl.BlockSpec((tm, ATTN_WIDTH), row),
                  pl.BlockSpec((tm, SSM_WIDTH), row),
                  pl.BlockSpec((SSM_WIDTH, SSM_WIDTH), const), pl.BlockSpec((1, SSM_WIDTH), const),
                  pl.BlockSpec((1, ATTN_WIDTH), const), pl.BlockSpec((1, SSM_WIDTH), const),
                  pl.BlockSpec((D_MODEL, D_MODEL), const)],
        out_specs=[pl.BlockSpec((ATTN_WIDTH, tm), lambda i: (0, i)), pl.BlockSpec((tm, SSM_WIDTH), row),
                   pl.BlockSpec((tm, SSM_WIDTH), row), pl.BlockSpec((tm, SSM_WIDTH), row),
                   pl.BlockSpec((tm, D_MODEL), row),
                   pl.BlockSpec((1, ATTN_WIDTH), const), pl.BlockSpec((1, SSM_WIDTH), const),
                   pl.BlockSpec((1, SSM_WIDTH), const)],
        out_shape=[jax.ShapeDtypeStruct((ATTN_WIDTH, SEQ), F32), jax.ShapeDtypeStruct((SEQ, SSM_WIDTH), F32),
                   jax.ShapeDtypeStruct((SEQ, SSM_WIDTH), BF16), jax.ShapeDtypeStruct((SEQ, SSM_WIDTH), BF16),
                   jax.ShapeDtypeStruct((SEQ, D_MODEL), BF16),
                   jax.ShapeDtypeStruct((1, ATTN_WIDTH), F32), jax.ShapeDtypeStruct((1, SSM_WIDTH), F32),
                   jax.ShapeDtypeStruct((1, SSM_WIDTH), F32)],
        compiler_params=_cparams(("arbitrary",)),
    )(dx2, o, y, glu_w, glu_b, gan, gsn, wout)


def _local_step(x, target, w, p, late_weights, early_grads, after=None, midway=None):
    x1, h1, a1, b1 = _ffn_fwd(x, p["norm_ffn1"], w["wgt1"], w["wut1"], w["wd1"], "ffn1_fwd", after=after)
    h2, q, k, v, u = _mixin_fwd(x1, p["norm_mix"], w["wint"])
    kp = jnp.pad(k, ((WINDOW, WINDOW), (0, 0)))
    vp = jnp.pad(v, ((WINDOW, WINDOW), (0, 0)))
    o = _attn_fwd(q, kp, vp, p["attn_sinks"])

    lam_re = p["ssm_lambda_re"].reshape(2 * N_LANE_BLOCKS, LANES)
    lam_im = p["ssm_lambda_im"].reshape(2 * N_LANE_BLOCKS, LANES)
    log_dt = jnp.repeat(p["ssm_log_dt"].reshape(2, 32), 64, axis=-1).reshape(2 * N_LANE_BLOCKS, LANES)
    a_re, a_im, bb, cc = _ssm_prep(lam_re, lam_im, log_dt, p["ssm_b_re"], p["ssm_b_im"],
                                   p["ssm_c_re"], p["ssm_c_im"])
    shape_a = (2, N_LANE_BLOCKS, 1, LANES)
    a_re4, a_im4 = a_re.reshape(shape_a), a_im.reshape(shape_a)
    bb4 = bb.reshape(2, N_LANE_BLOCKS, LANES, 2 * LANES)
    cc4 = cc.reshape(2, N_LANE_BLOCKS, 2 * LANES, LANES)
    dskip = p["ssm_d"].T.reshape(1, SSM_WIDTH)
    y, xr, xi = _ssm_fwd(u, a_re4, a_im4, bb4, cc4, dskip, after=None if midway is None else midway(o))

    w2 = late_weights(y)
    x2, mixed = _mixout_fwd(o, y, w2["glu"], p["ssm_glu_b"], p["attn_out_norm"], p["ssm_out_norm"], w2["wout"], x1)
    dx3, h3, a3, b3, loss, d_final = _ffn_fwd(x2, p["norm_ffn2"], w2["wgt2"], w2["wut2"], w2["wd2"], "ffn2_fwd",
                                              head=(p["final_norm"], target))
    dx2, da3, db3, s3, df3, d_n2 = _ffn_bwd_act(dx3, x2, p["norm_ffn2"], a3, b3, w2["wgt2"], w2["wut2"], w2["wd2"],
                                                "ffn2_bwd_act")
    g_wgt2, g_wut2, g_wd2 = _mm_tn([(da3, h3), (db3, h3), (s3, df3)], "ffn2_bwd_w")

    do, dy, dz, ygb, dx2b, d_gan, d_gsn, d_glub = _mixout_bwd(
        dx2, o, y, w2["glu"], p["ssm_glu_b"], p["attn_out_norm"], p["ssm_out_norm"], w2["wout"])
    (g_wout,) = _mm_tn([(mixed, dx2b)], "wout_bwd_w")
    (g_glu,) = _mm_tn([(ygb, dz)], "glu_bwd_w")
    sent = early_grads(dict(glu=g_glu, wout=g_wout, wgt2=g_wgt2, wut2=g_wut2, wd2=g_wd2))

    du, d_dskip, dcc, dbb, dar, dai = _ssm_bwd(dy, u, xr, xi, a_re4, a_im4, bb4, cc4, dskip, after=sent)
    nb = 2 * N_LANE_BLOCKS
    g_lre, g_lim, g_ldt, g_btr, g_bti, g_cre, g_cim = _ssm_prep_bwd(
        lam_re, lam_im, log_dt, p["ssm_b_re"], p["ssm_b_im"], dar.reshape(nb, LANES), dai.reshape(nb, LANES),
        dbb.reshape(nb, LANES, 2 * LANES), dcc.reshape(nb, 2 * LANES, LANES))

    dq, dkp, dvp, d_sinks = _attn_bwd(q, kp, vp, p["attn_sinks"], do)
    dk = dkp[WINDOW:WINDOW + SEQ]
    dv = dvp[WINDOW:WINDOW + SEQ]
    dx1, dproj, d_nmix = _mixin_bwd(dq, dk, dv, du, w["wint"], x1, p["norm_mix"], dx2)
    (g_wint,) = _mm_tn([(dproj, h2)], "win_bwd_w")

    dx0, da1, db1, s1, df1, d_n1 = _ffn_bwd_act(dx1, x, p["norm_ffn1"], a1, b1, w["wgt1"], w["wut1"], w["wd1"],
                                                "ffn1_bwd_act")
    g_wgt1, g_wut1, g_wd1 = _mm_tn([(da1, h1), (db1, h1), (s1, df1)], "ffn1_bwd_w")

    big = dict(wgt1=g_wgt1, wut1=g_wut1, wd1=g_wd1, wint=g_wint)
    small = dict(
        norm_ffn1=d_n1, norm_mix=d_nmix, attn_sinks=d_sinks,
        ssm_lambda_re=g_lre.reshape(64, 64), ssm_lambda_im=g_lim.reshape(64, 64),
        ssm_log_dt=g_ldt.reshape(2, 32), ssm_b_re=g_btr, ssm_b_im=g_bti, ssm_c_re=g_cre, ssm_c_im=g_cim,
        ssm_d=d_dskip.reshape(32, 16).T, ssm_glu_b=d_glub, attn_out_norm=d_gan, ssm_out_norm=d_gsn,
        norm_ffn2=d_n2, final_norm=d_final, loss=loss)
    return loss, dx0, big, small


BIG = dict(
    wgt1=("ffn1_w_gate", 352, 1024, True), wut1=("ffn1_w_up", 352, 1024, True), wd1=("ffn1_w_down", 352, 1024, False),
    wint=("w_in", 160, 1024, True), glu=("ssm_glu_w", 64, 512, False), wout=("w_out", 128, 1024, False),
    wgt2=("ffn2_w_gate", 352, 1024, True), wut2=("ffn2_w_up", 352, 1024, True), wd2=("ffn2_w_down", 352, 1024, False))

SMALL = dict(
    norm_ffn1=(1, 1024), norm_mix=(1, 1024), attn_sinks=(1, 8), ssm_lambda_re=(64, 64), ssm_lambda_im=(64, 64),
    ssm_log_dt=(2, 32), ssm_b_re=(1024, 64), ssm_b_im=(1024, 64), ssm_c_re=(1024, 64), ssm_c_im=(1024, 64),
    ssm_d=(16, 32), ssm_glu_b=(1, 512), attn_out_norm=(1, 512), ssm_out_norm=(1, 512), norm_ffn2=(1, 1024),
    final_norm=(1, 1024), loss=(1, 128))
SMALL_TRANSPOSED = ("ssm_b_re", "ssm_b_im", "ssm_d")
SMALL_PARAMS = tuple(n for n in SMALL if n != "loss")

SMALL_PAIRS = (("ssm_lambda_re", "ssm_lambda_im"), ("ssm_c_re", "ssm_c_im"), ("ssm_b_re", "ssm_b_im"))
SMALL_VECS = ("norm_ffn1", "norm_mix", "norm_ffn2", "final_norm", "ssm_glu_b", "attn_out_norm", "ssm_out_norm")
SMALL_TILES = ("ssm_log_dt", "attn_sinks", "ssm_d", "loss")


def _small_offsets():
    off, table = 0, {}
    for re, im in SMALL_PAIRS:
        table[re] = table[im] = off
        off += SMALL[re][0]
    for n in SMALL_VECS:
        table[n] = off
        off += SMALL[n][1] // LANES
    for n in SMALL_TILES:
        off = -(-off // 8) * 8
        table[n] = off
        off += SMALL[n][0]
    return table, off


SMALL_OFFSET, SMALL_USED_ROWS = _small_offsets()
SMALL_ROWS = -(-SMALL_USED_ROWS // (8 * N_DEV)) * 8 * N_DEV


def _cast_shards(shards):
    names = list(BIG)

    def body(*refs):
        ins, outs = refs[:len(names)], refs[len(names):]
        for idx in range(len(names)):
            outs[idx][...] = ins[idx][...].astype(BF16)

    return pl.pallas_call(
        body, name="cast_shards",
        out_shape=[jax.ShapeDtypeStruct((BIG[n][1], BIG[n][2]), BF16) for n in names],
        compiler_params=_cparams(),
    )(*[shards[n] for n in names])


def _peer(x, y, c, r):
    px = 1 - x if r & 4 else x
    py = 1 - y if r & 2 else y
    pc = 1 - c if r & 1 else c
    return px, py, pc


FIRST_GROUP = ("wgt1", "wut1", "wd1", "wint")
LATE_GROUP = ("glu", "wout", "wgt2", "wut2", "wd2")
N_PEERS = N_DEV - 1
ANY_SPEC = pl.BlockSpec(memory_space=pl.ANY)
HBM_SPEC = pl.BlockSpec(memory_space=pltpu.HBM)
SEM_SPEC = pl.BlockSpec(memory_space=pltpu.SEMAPHORE)
DATAFLOW_EFFECT = pltpu.SideEffectType.DATAFLOW_SIDE_EFFECTING


def _mesh_pos():
    x, y, c = lax.axis_index("x"), lax.axis_index("y"), lax.axis_index("c")
    return x, y, c, 4 * x + 2 * y + c


def _gather_first(first, late):
    nf, nl = len(first), len(late)

    def body(*refs):
        f_in, l_in = refs[:nf], refs[nf:nf + nl]
        f_out, l_out = refs[nf + nl:2 * nf + nl], refs[2 * nf + nl:2 * (nf + nl)]
        send_sems, recv_sems, local_sems = refs[2 * (nf + nl):]
        x, y, c, me = _mesh_pos()
        sibling = (x, y, 1 - c)
        chips = [(x, 1 - y), (1 - x, y), (1 - x, 1 - y)]

        def idx(px, py, pc):
            return 4 * px + 2 * py + pc

        def copy(k, s, block, to, src=None):
            slot = f_out[k].at[block]
            return pltpu.make_async_remote_copy(
                src_ref=slot if src is None else src, dst_ref=slot, send_sem=send_sems.at[k, s],
                recv_sem=recv_sems.at[k, s], device_id=to, device_id_type=MESH_ID)

        local = []
        for k in range(nf + nl):
            src, dst = (f_in[k], f_out[k]) if k < nf else (l_in[k - nf], l_out[k - nf])
            mine = pltpu.make_async_copy(src, dst.at[me], local_sems.at[k])
            mine.start()
            local.append(mine)
        sends = []
        for j, chip in enumerate(chips):
            for k in range(nf):
                sends.append(copy(k, 1 + j, me, (*chip, c), src=f_in[k]))
                sends[-1].start()
        for k in range(nf):
            sends.append(copy(k, 0, me, sibling, src=f_in[k]))
            sends[-1].start()
        for j, chip in enumerate(chips):
            for k in range(nf):
                copy(k, 1 + j, idx(*chip, c), (*chip, c)).wait_recv()
                sends.append(copy(k, 4 + j, idx(*chip, c), sibling))
                sends[-1].start()
        for k in range(nf):
            copy(k, 0, idx(*sibling), sibling).wait_recv()
        for j, chip in enumerate(chips):
            for k in range(nf):
                copy(k, 4 + j, idx(*chip, 1 - c), sibling).wait_recv()
        for cp in sends:
            cp.wait_send()
        for cp in local:
            cp.wait()

    return pl.pallas_call(
        body, name="gather_first",
        in_specs=[ANY_SPEC] * (nf + nl), out_specs=[ANY_SPEC] * (nf + nl),
        out_shape=[jax.ShapeDtypeStruct((N_DEV,) + s.shape, s.dtype) for s in list(first) + list(late)],
        scratch_shapes=[pltpu.SemaphoreType.DMA((nf, N_PEERS)), pltpu.SemaphoreType.DMA((nf, N_PEERS)),
                        pltpu.SemaphoreType.DMA((nf + nl,))],
        compiler_params=pltpu.CompilerParams(has_side_effects=True),
    )(*first, *late)


def _split_copy(src_refs, land_refs, send_sems, recv_sems, k, r, pos, scatter, receiving):
    x, y, c, me = pos
    px, py, pc = _peer(x, y, c, r)
    peer_idx = 4 * px + 2 * py + pc
    if scatter:
        src, dst = src_refs[k].at[peer_idx], land_refs[k].at[r - 1]
    else:
        src, dst = src_refs[k], land_refs[k].at[peer_idx if receiving else me]
    return pltpu.make_async_remote_copy(
        src_ref=src, dst_ref=dst, send_sem=send_sems.at[k * N_PEERS + r - 1],
        recv_sem=recv_sems.at[k * N_PEERS + r - 1], device_id=(px, py, pc), device_id_type=MESH_ID)


def _split_start(name, srcs, lands, scatter):
    n = len(srcs)

    def body(*refs):
        src_refs, land_refs = refs[:n], refs[n:2 * n]
        send_sems, recv_sems = refs[2 * n], refs[2 * n + 1]
        token = refs[-1]
        pos = _mesh_pos()
        for k in range(n):
            for r in range(1, N_DEV):
                _split_copy(src_refs, land_refs, send_sems, recv_sems, k, r, pos, scatter, False).start()
        token[...] = jnp.zeros_like(token)

    thru = [pltpu.HBM(a.shape, a.dtype) for a in list(srcs) + list(lands)]
    outs = pl.pallas_call(
        body, name=name,
        in_specs=[HBM_SPEC] * (2 * n),
        out_specs=[SEM_SPEC, SEM_SPEC] + [HBM_SPEC] * (2 * n) + [pl.BlockSpec(memory_space=pltpu.VMEM)],
        out_shape=[pltpu.SemaphoreType.DMA((n * N_PEERS,)), pltpu.SemaphoreType.DMA((n * N_PEERS,))] + thru
        + [jax.ShapeDtypeStruct((8, LANES), F32)],
        input_output_aliases={i: 2 + i for i in range(2 * n)},
        compiler_params=pltpu.CompilerParams(has_side_effects=DATAFLOW_EFFECT),
    )(*[pltpu.with_memory_space_constraint(a, pltpu.HBM) for a in list(srcs) + list(lands)])
    return outs[0], outs[1], outs[2:2 + n], outs[2 + n:2 + 2 * n], outs[-1]


def _split_wait(name, send_sems, recv_sems, srcs, lands, scatter, after):
    n = len(srcs)

    def body(*refs):
        src_refs, land_refs = refs[:n], refs[n:2 * n]
        send, recv = refs[2 * n], refs[2 * n + 1]
        pos = _mesh_pos()
        for k in range(n):
            for r in range(1, N_DEV):
                cp = _split_copy(src_refs, land_refs, send, recv, k, r, pos, scatter, True)
                cp.wait_send()
                cp.wait_recv()

    thru = [pltpu.HBM(a.shape, a.dtype) for a in list(srcs) + list(lands)]
    outs = pl.pallas_call(
        body, name=name,
        in_specs=[HBM_SPEC] * (2 * n) + [SEM_SPEC, SEM_SPEC, ANY_SPEC],
        out_specs=[HBM_SPEC] * (2 * n), out_shape=thru,
        input_output_aliases={i: i for i in range(2 * n)},
        compiler_params=pltpu.CompilerParams(has_side_effects=DATAFLOW_EFFECT),
    )(*srcs, *lands, send_sems, recv_sems, after)
    return outs[:n], outs[n:]


def _late_copy(passing, src_refs, land_refs, send_sems, recv_sems, k, s, pos, receiving):
    x, y, c, me = pos
    chips = [(x, 1 - y), (1 - x, y), (1 - x, 1 - y)]
    sibling = (x, y, 1 - c)

    def idx(dev):
        return 4 * dev[0] + 2 * dev[1] + dev[2]

    if passing:
        to = sibling
        block = idx((*chips[s], 1 - c)) if receiving else idx((*chips[s], c))
        src = dst = land_refs[k].at[block]
        sem = k * 3 + s
    else:
        to = sibling if s == 0 else (*chips[s - 1], c)
        src, dst = src_refs[k], land_refs[k].at[idx(to) if receiving else me]
        sem = k * 4 + s
    return pltpu.make_async_remote_copy(src_ref=src, dst_ref=dst, send_sem=send_sems.at[sem],
                                        recv_sem=recv_sems.at[sem], device_id=to, device_id_type=MESH_ID)


def _late_gather_call(name, stage, srcs, lands, sems, after=None):
    n = len(srcs)
    n_sem_in = len(sems)
    has_after = after is not None

    def body(*refs):
        src_refs, land_refs = refs[:n], refs[n:2 * n]
        sem_in = refs[2 * n:2 * n + n_sem_in]
        outs = refs[2 * n + n_sem_in + (1 if has_after else 0):]
        pos = _mesh_pos()
        if stage == 0:
            own_send, own_recv = outs[0], outs[1]
            for s in (1, 2, 3, 0):
                for k in range(n):
                    _late_copy(False, src_refs, land_refs, own_send, own_recv, k, s, pos, False).start()
            outs[-1][...] = jnp.zeros_like(outs[-1])
        elif stage == 1:
            own_recv = sem_in[1]
            pass_send, pass_recv = outs[0], outs[1]
            for s in range(3):
                for k in range(n):
                    _late_copy(False, src_refs, land_refs, sem_in[0], own_recv, k, s + 1, pos, True).wait_recv()
                    _late_copy(True, src_refs, land_refs, pass_send, pass_recv, k, s, pos, False).start()
            outs[-1][...] = jnp.zeros_like(outs[-1])
        else:
            own_send, own_recv, pass_send, pass_recv = sem_in
            for k in range(n):
                _late_copy(False, src_refs, land_refs, own_send, own_recv, k, 0, pos, True).wait_recv()
                for s in range(4):
                    _late_copy(False, src_refs, land_refs, own_send, own_recv, k, s, pos, False).wait_send()
                for s in range(3):
                    cp = _late_copy(True, src_refs, land_refs, pass_send, pass_recv, k, s, pos, True)
                    cp.wait_recv()
                    cp.wait_send()

    thru = [pltpu.HBM(a.shape, a.dtype) for a in list(srcs) + list(lands)]
    new_sems = [[pltpu.SemaphoreType.DMA((n * 4,))] * 2, [pltpu.SemaphoreType.DMA((n * 3,))] * 2, []][stage]
    extra = [] if stage == 2 else [jax.ShapeDtypeStruct((8, LANES), F32)]
    outs = pl.pallas_call(
        body, name=name,
        in_specs=[HBM_SPEC] * (2 * n) + [SEM_SPEC] * n_sem_in + [ANY_SPEC] * has_after,
        out_specs=[SEM_SPEC] * len(new_sems) + [HBM_SPEC] * (2 * n) + [pl.BlockSpec(memory_space=pltpu.VMEM)] * len(extra),
        out_shape=new_sems + thru + extra,
        input_output_aliases={i: len(new_sems) + i for i in range(2 * n)},
        compiler_params=pltpu.CompilerParams(has_side_effects=DATAFLOW_EFFECT),
    )(*[pltpu.with_memory_space_constraint(a, pltpu.HBM) for a in list(srcs) + list(lands)], *sems,
      *([after] if has_after else []))
    ns = len(new_sems)
    return list(outs[:ns]), outs[ns:ns + n], outs[ns + n:ns + 2 * n], (outs[-1] if extra else None)


N_SEND_SLOTS = 3


def _exchange_last(grads, small_packed):
    ng = len(grads)
    ch = SMALL_ROWS // N_DEV
    max_rows = max(g.shape[1] for g in grads)
    cols = grads[0].shape[2]

    def body(*refs):
        g_in, s_in = refs[:ng], refs[ng]
        outs = refs[ng + 1:]
        own_out, land, stage = outs[:ng], outs[ng:2 * ng], outs[2 * ng:3 * ng]
        s_red, s_stage = outs[3 * ng], outs[3 * ng + 1]
        (va, vb, vo, vs, sm_in, sm_out, d2d_send, d2d_recv, ici_send, ici_recv, s1_send, s1_recv, s2_send, s2_recv,
         local_sems) = outs[3 * ng + 2:]
        x, y, c, me = _mesh_pos()
        sibling = (x, y, 1 - c)
        chips = [(x, y), (x, 1 - y), (1 - x, y), (1 - x, 1 - y)]

        def idx(chip, core):
            return 4 * chip[0] + 2 * chip[1] + core

        def d2d(k, j):
            return pltpu.make_async_remote_copy(
                src_ref=g_in[k].at[idx(chips[j], 1 - c)], dst_ref=stage[k].at[j], send_sem=d2d_send.at[k, j],
                recv_sem=d2d_recv.at[k, j], device_id=sibling, device_id_type=MESH_ID)

        def ici(k, j, slot):
            rows = g_in[k].shape[1]
            return pltpu.make_async_remote_copy(
                src_ref=vo.at[slot, pl.ds(0, rows)], dst_ref=land[k].at[j - 1], send_sem=ici_send.at[k, j - 1],
                recv_sem=ici_recv.at[k, j - 1], device_id=(*chips[j], c), device_id_type=MESH_ID)

        def small_scatter(r):
            px, py, pc = _peer(x, y, c, r)
            return pltpu.make_async_remote_copy(
                src_ref=s_in.at[pl.ds(pl.multiple_of((4 * px + 2 * py + pc) * ch, 8), ch)], dst_ref=s_stage.at[me],
                send_sem=s1_send.at[r - 1], recv_sem=s1_recv.at[r - 1], device_id=(px, py, pc), device_id_type=MESH_ID)

        def small_gather(r):
            return pltpu.make_async_remote_copy(
                src_ref=sm_out, dst_ref=s_red.at[me], send_sem=s2_send.at[r - 1], recv_sem=s2_recv.at[r - 1],
                device_id=_peer(x, y, c, r), device_id_type=MESH_ID)

        for r in range(1, N_DEV):
            small_scatter(r).start()
        mine = pltpu.make_async_copy(s_in.at[pl.ds(pl.multiple_of(me * ch, 8), ch)], s_stage.at[me], local_sems.at[0])
        mine.start()
        pairs = [(k, j) for k in range(ng) for j in (1, 2, 3)] + [(k, 0) for k in range(ng)]
        for k, j in pairs:
            d2d(k, j).start()

        for r in range(1, N_DEV):
            small_scatter(r).wait_recv()
        mine.wait()
        load = pltpu.make_async_copy(s_stage, sm_in, local_sems.at[1])
        load.start()
        load.wait()
        total = sm_in[0]
        for i in range(1, N_DEV):
            total = total + sm_in[i]
        sm_out[...] = total
        for r in range(1, N_DEV):
            small_gather(r).start()
        keep = pltpu.make_async_copy(sm_out, s_red.at[me], local_sems.at[2])
        keep.start()

        in_flight = {}
        for i, (k, j) in enumerate(pairs):
            slot = i % N_SEND_SLOTS
            rows = g_in[k].shape[1]
            if slot in in_flight:
                in_flight.pop(slot).wait_send()
            d2d(k, j).wait_recv()
            la = pltpu.make_async_copy(g_in[k].at[idx(chips[j], c)], va.at[pl.ds(0, rows)], local_sems.at[3])
            lb = pltpu.make_async_copy(stage[k].at[j], vb.at[pl.ds(0, rows)], local_sems.at[4])
            la.start()
            lb.start()
            la.wait()
            lb.wait()
            total = va[pl.ds(0, rows)].astype(F32) + vb[pl.ds(0, rows)].astype(F32)
            if j == 0:
                vs[pl.ds(0, rows)] = total
                st = pltpu.make_async_copy(vs.at[pl.ds(0, rows)], own_out[k], local_sems.at[5])
                st.start()
                st.wait()
            else:
                vo[slot, pl.ds(0, rows)] = total.astype(BF16)
                cp = ici(k, j, slot)
                cp.start()
                in_flight[slot] = cp
        for cp in in_flight.values():
            cp.wait_send()

        for j in (1, 2, 3, 0):
            for k in range(ng):
                d2d(k, j).wait_send()
        for j in (1, 2, 3):
            for k in range(ng):
                ici(k, j, 0).wait_recv()
        for r in range(1, N_DEV):
            small_scatter(r).wait_send()
            small_gather(r).wait_send()
            small_gather(r).wait_recv()
        keep.wait()

    out_shape = [jax.ShapeDtypeStruct(g.shape[1:], F32) for g in grads]
    out_shape += [jax.ShapeDtypeStruct((3,) + g.shape[1:], BF16) for g in grads]
    out_shape += [jax.ShapeDtypeStruct((4,) + g.shape[1:], BF16) for g in grads]
    out_shape += [jax.ShapeDtypeStruct((N_DEV, ch, LANES), F32), jax.ShapeDtypeStruct((N_DEV, ch, LANES), F32)]
    outs = pl.pallas_call(
        body, name="exchange_last",
        in_specs=[ANY_SPEC] * (ng + 1), out_specs=[ANY_SPEC] * len(out_shape), out_shape=out_shape,
        scratch_shapes=[pltpu.VMEM((max_rows, cols), BF16), pltpu.VMEM((max_rows, cols), BF16),
                        pltpu.VMEM((N_SEND_SLOTS, max_rows, cols), BF16), pltpu.VMEM((max_rows, cols), F32),
                        pltpu.VMEM((N_DEV, ch, LANES), F32), pltpu.VMEM((ch, LANES), F32),
                        pltpu.SemaphoreType.DMA((ng, 4)), pltpu.SemaphoreType.DMA((ng, 4)),
                        pltpu.SemaphoreType.DMA((ng, 3)), pltpu.SemaphoreType.DMA((ng, 3)),
                        pltpu.SemaphoreType.DMA((N_PEERS,)), pltpu.SemaphoreType.DMA((N_PEERS,)),
                        pltpu.SemaphoreType.DMA((N_PEERS,)), pltpu.SemaphoreType.DMA((N_PEERS,)),
                        pltpu.SemaphoreType.DMA((6,))],
        compiler_params=pltpu.CompilerParams(has_side_effects=True, vmem_limit_bytes=VMEM_LIMIT),
    )(*grads, small_packed)
    return outs[:ng], outs[ng:2 * ng], outs[3 * ng].reshape(SMALL_ROWS, LANES)


def _adamw_math(w, g, m, v):
    m2 = ADAM_B1 * m + (1.0 - ADAM_B1) * g
    v2 = ADAM_B2 * v + (1.0 - ADAM_B2) * (g * g)
    m_hat = m2 / (1.0 - ADAM_B1 ** ADAM_STEP)
    v_hat = v2 / (1.0 - ADAM_B2 ** ADAM_STEP)
    delta = -ADAM_LR * (m_hat / (jnp.sqrt(v_hat) + ADAM_EPS) + ADAM_WD * w)
    return delta, m2, v2


ADAM_ROW_TILES = 2


def _adamw_big(own, parts, w, m, v, name):
    shape = w.shape
    own_is_blocks = own.ndim == 3
    tr = shape[0] // ADAM_ROW_TILES
    n_parts = parts.shape[0]

    def body(own_ref, p_ref, w_ref, m_ref, v_ref, g_ref, d_ref, m2_ref, v2_ref, own_s, sem):
        rows = pl.ds(pl.multiple_of(pl.program_id(0) * tr, 16), tr)
        if own_is_blocks:
            cp = pltpu.make_async_copy(own_ref.at[_mesh_pos()[3], rows], own_s, sem)
        else:
            cp = pltpu.make_async_copy(own_ref.at[rows], own_s, sem)
        cp.start()
        cp.wait()
        g = own_s[...].astype(F32)
        for i in range(n_parts):
            g = g + p_ref[i].astype(F32)
        delta, m2, v2 = _adamw_math(w_ref[...], g, m_ref[...], v_ref[...])
        g_ref[...] = g
        d_ref[...] = delta
        m2_ref[...] = m2
        v2_ref[...] = v2

    tile = pl.BlockSpec((tr, shape[1]), lambda i: (i, 0))
    return pl.pallas_call(
        body, name=name, grid=(ADAM_ROW_TILES,),
        in_specs=[ANY_SPEC, pl.BlockSpec((n_parts, tr, shape[1]), lambda i: (0, i, 0)), tile, tile, tile],
        out_specs=[tile] * 4, out_shape=[jax.ShapeDtypeStruct(shape, F32)] * 4,
        scratch_shapes=[pltpu.VMEM((tr, shape[1]), own.dtype), pltpu.SemaphoreType.DMA(())],
        compiler_params=_cparams(("arbitrary",)),
    )(own, parts, w, m, v)


def _pack_small(grads):
    names = list(SMALL)

    def body(*refs):
        ins, out = dict(zip(names, refs[:-1])), refs[-1]
        out[...] = jnp.zeros_like(out)
        for re, im in SMALL_PAIRS:
            off, rows = SMALL_OFFSET[re], SMALL[re][0]
            out[off:off + rows, :] = jnp.concatenate([ins[re][...], ins[im][...]], axis=1)
        for n in SMALL_VECS:
            off, vec = SMALL_OFFSET[n], ins[n][...]
            for i in range(SMALL[n][1] // LANES):
                out[off + i:off + i + 1, :] = vec[:, i * LANES:(i + 1) * LANES]
        for n in SMALL_TILES:
            off, (rows, cols) = SMALL_OFFSET[n], SMALL[n]
            out[off:off + rows, 0:cols] = ins[n][...]

    return pl.pallas_call(
        body, name="pack_small", out_shape=jax.ShapeDtypeStruct((SMALL_ROWS, LANES), F32),
        compiler_params=_cparams(),
    )(*[grads[n] for n in names])


def _unpack_small_ref(g_ref, n):
    off, (rows, cols) = SMALL_OFFSET[n], SMALL[n]
    for re, im in SMALL_PAIRS:
        if n == re:
            return g_ref[off:off + rows, 0:HALF_LANES]
        if n == im:
            return g_ref[off:off + rows, HALF_LANES:LANES]
    if n in SMALL_VECS:
        return jnp.concatenate([g_ref[off + i:off + i + 1, :] for i in range(cols // LANES)], axis=1)
    return g_ref[off:off + rows, 0:cols]


def _adamw_small(g_packed, w, m, v):
    names = list(SMALL_PARAMS)
    n = len(names)

    def body(g_ref, *refs):
        w_refs, m_refs, v_refs, outs = refs[:n], refs[n:2 * n], refs[2 * n:3 * n], refs[3 * n:]
        for idx, name in enumerate(names):
            g = _unpack_small_ref(g_ref, name)
            delta, m2, v2 = _adamw_math(w_refs[idx][...], g, m_refs[idx][...], v_refs[idx][...])
            outs[4 * idx][...] = g
            outs[4 * idx + 1][...] = delta
            outs[4 * idx + 2][...] = m2
            outs[4 * idx + 3][...] = v2
        outs[4 * n][...] = _unpack_small_ref(g_ref, "loss")

    outs = pl.pallas_call(
        body, name="adamw_small",
        out_shape=[jax.ShapeDtypeStruct(SMALL[name], F32) for name in names for _ in range(4)]
        + [jax.ShapeDtypeStruct(SMALL["loss"], F32)],
        compiler_params=_cparams(),
    )(g_packed, *[w[k] for k in names], *[m[k] for k in names], *[v[k] for k in names])
    return {name: outs[4 * idx:4 * idx + 4] for idx, name in enumerate(names)}, outs[4 * n]


WEIGHT_NAMES = ['norm_ffn1', 'ffn1_w_gate', 'ffn1_w_up', 'ffn1_w_down', 'norm_mix', 'w_in', 'attn_sinks',
                'ssm_lambda_re', 'ssm_lambda_im', 'ssm_log_dt', 'ssm_b_re', 'ssm_b_im', 'ssm_c_re', 'ssm_c_im',
                'ssm_d', 'ssm_glu_w', 'ssm_glu_b', 'attn_out_norm', 'ssm_out_norm', 'w_out', 'norm_ffn2',
                'ffn2_w_gate', 'ffn2_w_up', 'ffn2_w_down', 'final_norm']


def kernel(x, norm_ffn1, ffn1_w_gate, ffn1_w_up, ffn1_w_down, norm_mix, w_in, attn_sinks, ssm_lambda_re, ssm_lambda_im, ssm_log_dt, ssm_b_re, ssm_b_im, ssm_c_re, ssm_c_im, ssm_d, ssm_glu_w, ssm_glu_b, attn_out_norm, ssm_out_norm, w_out, norm_ffn2, ffn2_w_gate, ffn2_w_up, ffn2_w_down, final_norm, loss_target, m_norm_ffn1, m_ffn1_w_gate, m_ffn1_w_up, m_ffn1_w_down, m_norm_mix, m_w_in, m_attn_sinks, m_ssm_lambda_re, m_ssm_lambda_im, m_ssm_log_dt, m_ssm_b_re, m_ssm_b_im, m_ssm_c_re, m_ssm_c_im, m_ssm_d, m_ssm_glu_w, m_ssm_glu_b, m_attn_out_norm, m_ssm_out_norm, m_w_out, m_norm_ffn2, m_ffn2_w_gate, m_ffn2_w_up, m_ffn2_w_down, m_final_norm, v_norm_ffn1, v_ffn1_w_gate, v_ffn1_w_up, v_ffn1_w_down, v_norm_mix, v_w_in, v_attn_sinks, v_ssm_lambda_re, v_ssm_lambda_im, v_ssm_log_dt, v_ssm_b_re, v_ssm_b_im, v_ssm_c_re, v_ssm_c_im, v_ssm_d, v_ssm_glu_w, v_ssm_glu_b, v_attn_out_norm, v_ssm_out_norm, v_w_out, v_norm_ffn2, v_ffn2_w_gate, v_ffn2_w_up, v_ffn2_w_down, v_final_norm):
    args = dict(locals())
    weights = {n: args[n] for n in WEIGHT_NAMES}
    moms = {n: args["m_" + n] for n in WEIGHT_NAMES}
    vars_ = {n: args["v_" + n] for n in WEIGHT_NAMES}

    def shard2d(a, k):
        a = a.reshape(a.shape[-2], a.shape[-1])
        return a.T if BIG[k][3] else a

    def shard_master(a, k):
        return (a.T if BIG[k][3] else a).reshape(weights[BIG[k][0]].shape)

    def blocks(g, k):
        return g.reshape(N_DEV, BIG[k][1], BIG[k][2])

    def full(g, k):
        return g.reshape(N_DEV * BIG[k][1], BIG[k][2])

    shards = dict(zip(BIG, _cast_shards({k: shard2d(weights[BIG[k][0]], k) for k in BIG})))
    nf = len(FIRST_GROUP)
    got = _gather_first([shards[k] for k in FIRST_GROUP], [shards[k] for k in LATE_GROUP])
    w_first = {k: full(g, k) for k, g in zip(FIRST_GROUP, got[:nf])}
    late = {}
    late["own_sems"], late["srcs"], late["lands"], w_token = _late_gather_call(
        "gather_late_start", 0, [shards[k] for k in LATE_GROUP], got[nf:], [])

    def late_pass(dep):
        late["pass_sems"], late["srcs"], late["lands"], token = _late_gather_call(
            "gather_late_pass", 1, late["srcs"], late["lands"], late["own_sems"], after=dep)
        return token

    def late_weights(dep):
        _, _, lands, _ = _late_gather_call("gather_late_wait", 2, late["srcs"], late["lands"],
                                           late["own_sems"] + late["pass_sems"], after=dep)
        return {k: full(g, k) for k, g in zip(LATE_GROUP, lands)}

    early = {}

    def early_grads(g):
        srcs = [blocks(g[k], k) for k in LATE_GROUP]
        lands = [lax.empty((N_PEERS, BIG[k][1], BIG[k][2]), BF16) for k in LATE_GROUP]
        early["send"], early["recv"], early["srcs"], early["lands"], token = _split_start(
            "grads_late_start", srcs, lands, scatter=True)
        return token

    def small2d(a, n):
        if n in SMALL_TRANSPOSED:
            a = jnp.swapaxes(a, -1, -2)
        return a.reshape(SMALL[n])

    def small_master(a, n):
        if n in SMALL_TRANSPOSED:
            shape = weights[n].shape
            return jnp.swapaxes(a.reshape(shape[:-2] + (shape[-1], shape[-2])), -1, -2)
        return a.reshape(weights[n].shape)

    small_p = {n: small2d(weights[n], n) for n in SMALL_PARAMS}
    _, grad_x, g_first, g_small = _local_step(
        x.reshape(SEQ, D_MODEL), loss_target.reshape(SEQ, D_MODEL), w_first, small_p, late_weights, early_grads,
        after=w_token, midway=late_pass)

    own_sums, first_parts, small_grad = _exchange_last([blocks(g_first[k], k) for k in FIRST_GROUP],
                                                       _pack_small(g_small))
    own_late, late_parts = _split_wait("grads_late_wait", early["send"], early["recv"], early["srcs"],
                                       early["lands"], True, small_grad)
    own = dict(zip(FIRST_GROUP + LATE_GROUP, list(own_sums) + list(own_late)))
    parts = dict(zip(FIRST_GROUP + LATE_GROUP, list(first_parts) + list(late_parts)))
    outs = {}
    for k in BIG:
        n = BIG[k][0]
        outs[n] = [shard_master(o, k) for o in
                   _adamw_big(own[k], parts[k], shard2d(weights[n], k), shard2d(moms[n], k), shard2d(vars_[n], k),
                              "adamw_" + n)]
    small_out, loss_row = _adamw_small(small_grad, small_p, {n: small2d(moms[n], n) for n in SMALL_PARAMS},
                                       {n: small2d(vars_[n], n) for n in SMALL_PARAMS})
    for n in SMALL_PARAMS:
        outs[n] = [small_master(o, n) for o in small_out[n]]

    result = [loss_row[0, 0], grad_x.reshape(x.shape)]
    for i in range(4):
        result += [outs[n][i] for n in WEIGHT_NAMES]
    return tuple(result)
```

```python
import functools

import jax
import jax.numpy as jnp
from jax import lax
from jax.experimental import pallas as pl
from jax.experimental.pallas import tpu as pltpu

F32 = jnp.float32
BF16 = jnp.bfloat16

N_DEV = 8
SEQ = 2048
D_MODEL = 1024
D_FF = 2816
ATTN_HEADS = 8
KV_HEADS = 2
HEAD_DIM = 64
ATTN_WIDTH = 512
KV_WIDTH = 128
WINDOW = 128
SSM_WIDTH = 512
IN_WIDTH = 1280
EPS = 1e-6
NEG_INF = -1e30
LAMBDA_RE_MAX = -1e-4
LANES = 128
N_LANE_BLOCKS = 16
SCAN_CHUNK = SEQ // 8

ADAM_LR = 0.001
ADAM_B1 = 0.9
ADAM_B2 = 0.999
ADAM_EPS = 1e-08
ADAM_WD = 0.01
ADAM_STEP = 10

VMEM_LIMIT = 60 * 1024 * 1024
MESH_ID = pl.DeviceIdType.MESH


def _cparams(sem=None):
    return pltpu.CompilerParams(dimension_semantics=sem, vmem_limit_bytes=VMEM_LIMIT)


def _dot(a, b):
    return jnp.dot(a, b, preferred_element_type=F32)


def _dot_nt(a, b):
    return lax.dot_general(a, b, (((1,), (1,)), ((), ())), preferred_element_type=F32)


def _dot_tn(a, b):
    return lax.dot_general(a, b, (((0,), (0,)), ((), ())), preferred_element_type=F32)


def _rms_fwd(x, g):
    r = lax.rsqrt(jnp.mean(x * x, axis=-1, keepdims=True) + EPS)
    return x * r * g


def _rms_bwd(dh, x, g):
    r = lax.rsqrt(jnp.mean(x * x, axis=-1, keepdims=True) + EPS)
    xh = x * r
    dg = jnp.sum(dh * xh, axis=0, keepdims=True)
    dxh = dh * g
    dx = r * (dxh - xh * jnp.mean(dxh * xh, axis=-1, keepdims=True))
    return dx, dg


def _sigmoid(x):
    return 1.0 / (1.0 + jnp.exp(-x))


FFN_TM = 512
FFN_TF = 1408


def _ffn_fwd(x, g, wgt, wut, wd, name, after=None, head=None):
    tm, tf = FFN_TM, FFN_TF
    nj = D_FF // tf
    deps = [] if after is None else [after]
    n_in = len(deps) + (2 if head else 0)

    def body(x_ref, g_ref, wg_ref, wu_ref, wd_ref, *rest):
        i = pl.program_id(0)
        j = pl.program_id(1)
        if head:
            gf_ref, t_ref = rest[len(deps):n_in]
            xo_ref, h_ref, a_ref, b_ref, loss_ref, dgf_ref, h_s, acc = rest[n_in:]
        else:
            xo_ref, h_ref, a_ref, b_ref, h_s, acc = rest[n_in:]

        @pl.when(j == 0)
        def _():
            h = _rms_fwd(x_ref[...], g_ref[...]).astype(BF16)
            h_s[...] = h
            h_ref[...] = h
            acc[...] = jnp.zeros_like(acc)

        h = h_s[...]
        a = _dot_nt(h, wg_ref[...])
        b = _dot_nt(h, wu_ref[...])
        a_ref[...] = a.astype(BF16)
        b_ref[...] = b.astype(BF16)
        s = (a * _sigmoid(a) * b).astype(BF16)
        acc[...] += _dot(s, wd_ref[...])

        @pl.when(j == nj - 1)
        def _():
            xo = x_ref[...] + 0.5 * acc[...]
            if not head:
                xo_ref[...] = xo
                return
            gf = gf_ref[...]
            err = _rms_fwd(xo, gf) - t_ref[...]
            part = jnp.broadcast_to(0.5 * jnp.sum(err * err) / D_MODEL, (1, LANES))
            dx, dgf = _rms_bwd(err * (1.0 / D_MODEL), xo, gf)
            xo_ref[...] = dx

            @pl.when(i == 0)
            def _():
                loss_ref[...] = part
                dgf_ref[...] = dgf

            @pl.when(i != 0)
            def _():
                loss_ref[...] += part
                dgf_ref[...] += dgf

    row = lambda i, j: (i, 0)
    const = lambda i, j: (0, 0)
    head_in = [pl.BlockSpec((1, D_MODEL), const), pl.BlockSpec((tm, D_MODEL), row)] if head else []
    head_out = [pl.BlockSpec((1, LANES), const), pl.BlockSpec((1, D_MODEL), const)] if head else []
    head_shape = [jax.ShapeDtypeStruct((1, LANES), F32), jax.ShapeDtypeStruct((1, D_MODEL), F32)] if head else []
    return pl.pallas_call(
        body, name=name, grid=(SEQ // tm, nj),
        in_specs=[pl.BlockSpec((tm, D_MODEL), row), pl.BlockSpec((1, D_MODEL), const),
                  pl.BlockSpec((tf, D_MODEL), lambda i, j: (j, 0)),
                  pl.BlockSpec((tf, D_MODEL), lambda i, j: (j, 0)),
                  pl.BlockSpec((tf, D_MODEL), lambda i, j: (j, 0))] + [pl.BlockSpec(memory_space=pl.ANY)] * len(deps)
        + head_in,
        out_specs=[pl.BlockSpec((tm, D_MODEL), row), pl.BlockSpec((tm, D_MODEL), row),
                   pl.BlockSpec((tm, tf), lambda i, j: (i, j)),
                   pl.BlockSpec((tm, tf), lambda i, j: (i, j))] + head_out,
        out_shape=[jax.ShapeDtypeStruct((SEQ, D_MODEL), F32), jax.ShapeDtypeStruct((SEQ, D_MODEL), BF16),
                   jax.ShapeDtypeStruct((SEQ, D_FF), BF16), jax.ShapeDtypeStruct((SEQ, D_FF), BF16)] + head_shape,
        scratch_shapes=[pltpu.VMEM((tm, D_MODEL), BF16), pltpu.VMEM((tm, D_MODEL), F32)],
        compiler_params=_cparams(("arbitrary" if head else "parallel", "arbitrary")),
    )(x, g, wgt, wut, wd, *deps, *(head or ()))


def _ffn_bwd_act(dxo, x, g, a, b, wgt, wut, wd, name):
    tm, tf = FFN_TM, FFN_TF
    nj = D_FF // tf

    def body(dxo_ref, x_ref, g_ref, a_ref, b_ref, wg_ref, wu_ref, wd_ref,
             dx_ref, da_ref, db_ref, s_ref, df_ref, dg_ref, df_s, acc):
        i = pl.program_id(0)
        j = pl.program_id(1)

        @pl.when(j == 0)
        def _():
            df = (0.5 * dxo_ref[...]).astype(BF16)
            df_s[...] = df
            df_ref[...] = df
            acc[...] = jnp.zeros_like(acc)

        ds = _dot_nt(df_s[...], wd_ref[...])
        av = a_ref[...].astype(F32)
        bv = b_ref[...].astype(F32)
        sig = _sigmoid(av)
        sl = av * sig
        s_ref[...] = (sl * bv).astype(BF16)
        db = (ds * sl).astype(BF16)
        da = (ds * bv * (sig * (1.0 + av * (1.0 - sig)))).astype(BF16)
        da_ref[...] = da
        db_ref[...] = db
        acc[...] += _dot(da, wg_ref[...]) + _dot(db, wu_ref[...])

        @pl.when(j == nj - 1)
        def _():
            dx, dg = _rms_bwd(acc[...], x_ref[...], g_ref[...])
            dx_ref[...] = dxo_ref[...] + dx

            @pl.when(i == 0)
            def _():
                dg_ref[...] = dg

            @pl.when(i != 0)
            def _():
                dg_ref[...] += dg

    row = lambda i, j: (i, 0)
    col = lambda i, j: (j, 0)
    tile = lambda i, j: (i, j)
    return pl.pallas_call(
        body, name=name, grid=(SEQ // tm, nj),
        in_specs=[pl.BlockSpec((tm, D_MODEL), row), pl.BlockSpec((tm, D_MODEL), row),
                  pl.BlockSpec((1, D_MODEL), lambda i, j: (0, 0)),
                  pl.BlockSpec((tm, tf), tile), pl.BlockSpec((tm, tf), tile),
                  pl.BlockSpec((tf, D_MODEL), col), pl.BlockSpec((tf, D_MODEL), col), pl.BlockSpec((tf, D_MODEL), col)],
        out_specs=[pl.BlockSpec((tm, D_MODEL), row),
                   pl.BlockSpec((tm, tf), tile), pl.BlockSpec((tm, tf), tile), pl.BlockSpec((tm, tf), tile),
                   pl.BlockSpec((tm, D_MODEL), row),
                   pl.BlockSpec((1, D_MODEL), lambda i, j: (0, 0))],
        out_shape=[jax.ShapeDtypeStruct((SEQ, D_MODEL), F32),
                   jax.ShapeDtypeStruct((SEQ, D_FF), BF16), jax.ShapeDtypeStruct((SEQ, D_FF), BF16),
                   jax.ShapeDtypeStruct((SEQ, D_FF), BF16),
                   jax.ShapeDtypeStruct((SEQ, D_MODEL), BF16),
                   jax.ShapeDtypeStruct((1, D_MODEL), F32)],
        scratch_shapes=[pltpu.VMEM((tm, D_MODEL), BF16), pltpu.VMEM((tm, D_MODEL), F32)],
        compiler_params=_cparams(("arbitrary", "arbitrary")),
    )(dxo, x, g, a, b, wgt, wut, wd)


def _mm_tn(pairs, name, tmm=256):
    m = pairs[0][0].shape[1]
    n_pairs = len(pairs)

    def body(*refs):
        ins, outs = refs[:2 * n_pairs], refs[2 * n_pairs:]
        for p in range(n_pairs):
            outs[p][...] = _dot_tn(ins[2 * p][...], ins[2 * p + 1][...]).astype(BF16)

    in_specs, out_specs, out_shape, args = [], [], [], []
    for a, b in pairs:
        n = b.shape[1]
        in_specs += [pl.BlockSpec((SEQ, tmm), lambda i: (0, i)), pl.BlockSpec((SEQ, n), lambda i: (0, 0))]
        out_specs.append(pl.BlockSpec((tmm, n), lambda i: (i, 0)))
        out_shape.append(jax.ShapeDtypeStruct((m, n), BF16))
        args += [a, b]
    return pl.pallas_call(body, name=name, grid=(m // tmm,), in_specs=in_specs, out_specs=out_specs,
                          out_shape=out_shape, compiler_params=_cparams(("parallel",)))(*args)


MIX_TM = 256


def _mixin_fwd(x, g, wint):
    tm = MIX_TM

    def body(x_ref, g_ref, w_ref, h_ref, q_ref, k_ref, v_ref, u_ref):
        h = _rms_fwd(x_ref[...], g_ref[...]).astype(BF16)
        h_ref[...] = h
        proj = _dot_nt(h, w_ref[...])
        q_ref[...] = proj[:, :ATTN_WIDTH].T
        k_ref[...] = proj[:, ATTN_WIDTH:ATTN_WIDTH + KV_WIDTH]
        v_ref[...] = proj[:, ATTN_WIDTH + KV_WIDTH:ATTN_WIDTH + 2 * KV_WIDTH]
        u_ref[...] = proj[:, ATTN_WIDTH + 2 * KV_WIDTH:]

    row = lambda i: (i, 0)
    return pl.pallas_call(
        body, name="mixin_fwd", grid=(SEQ // tm,),
        in_specs=[pl.BlockSpec((tm, D_MODEL), row), pl.BlockSpec((1, D_MODEL), lambda i: (0, 0)),
                  pl.BlockSpec((IN_WIDTH, D_MODEL), lambda i: (0, 0))],
        out_specs=[pl.BlockSpec((tm, D_MODEL), row), pl.BlockSpec((ATTN_WIDTH, tm), lambda i: (0, i)),
                   pl.BlockSpec((tm, KV_WIDTH), row), pl.BlockSpec((tm, KV_WIDTH), row),
                   pl.BlockSpec((tm, SSM_WIDTH), row)],
        out_shape=[jax.ShapeDtypeStruct((SEQ, D_MODEL), BF16), jax.ShapeDtypeStruct((ATTN_WIDTH, SEQ), F32),
                   jax.ShapeDtypeStruct((SEQ, KV_WIDTH), F32), jax.ShapeDtypeStruct((SEQ, KV_WIDTH), F32),
                   jax.ShapeDtypeStruct((SEQ, SSM_WIDTH), F32)],
        compiler_params=_cparams(("parallel",)),
    )(x, g, wint)


def _mixin_bwd(dqt, dk, dv, du, wint, x, g, dres):
    tm = MIX_TM

    def body(dq_ref, dk_ref, dv_ref, du_ref, w_ref, x_ref, g_ref, dres_ref, dx_ref, dp_ref, dg_ref):
        i = pl.program_id(0)
        dp = jnp.concatenate([dq_ref[...].T, dk_ref[...], dv_ref[...], du_ref[...]], axis=-1).astype(BF16)
        dp_ref[...] = dp
        dh = _dot(dp, w_ref[...])
        dx, dg = _rms_bwd(dh, x_ref[...], g_ref[...])
        dx_ref[...] = dres_ref[...] + dx

        @pl.when(i == 0)
        def _():
            dg_ref[...] = dg

        @pl.when(i != 0)
        def _():
            dg_ref[...] += dg

    row = lambda i: (i, 0)
    const = lambda i: (0, 0)
    return pl.pallas_call(
        body, name="mixin_bwd", grid=(SEQ // tm,),
        in_specs=[pl.BlockSpec((ATTN_WIDTH, tm), lambda i: (0, i)), pl.BlockSpec((tm, KV_WIDTH), row),
                  pl.BlockSpec((tm, KV_WIDTH), row), pl.BlockSpec((tm, SSM_WIDTH), row),
                  pl.BlockSpec((IN_WIDTH, D_MODEL), const), pl.BlockSpec((tm, D_MODEL), row),
                  pl.BlockSpec((1, D_MODEL), const), pl.BlockSpec((tm, D_MODEL), row)],
        out_specs=[pl.BlockSpec((tm, D_MODEL), row), pl.BlockSpec((tm, IN_WIDTH), row),
                   pl.BlockSpec((1, D_MODEL), const)],
        out_shape=[jax.ShapeDtypeStruct((SEQ, D_MODEL), F32), jax.ShapeDtypeStruct((SEQ, IN_WIDTH), BF16),
                   jax.ShapeDtypeStruct((1, D_MODEL), F32)],
        compiler_params=_cparams(("arbitrary",)),
    )(dqt, dk, dv, du, wint, x, g, dres)


N_QBLOCKS = SEQ // WINDOW
GROUP = ATTN_HEADS // KV_HEADS
SCALE = HEAD_DIM ** -0.5


def _alibi_slope(h):
    return 2.0 ** (-8.0 * (h + 1) / ATTN_HEADS)


def _window_masks(n):
    s_idx = lax.broadcasted_iota(jnp.int32, (3 * WINDOW, WINDOW), 0)
    t_idx = lax.broadcasted_iota(jnp.int32, (3 * WINDOW, WINDOW), 1)
    absrel = jnp.abs(s_idx - WINDOW - t_idx)
    key_pos = n * WINDOW - WINDOW + s_idx
    valid = (absrel <= WINDOW) & (key_pos >= 0) & (key_pos < SEQ)
    return absrel.astype(F32), valid


def _group_cols(ref, r0, gi):
    return jnp.concatenate(
        [ref[(gi * GROUP + hh) * HEAD_DIM:(gi * GROUP + hh + 1) * HEAD_DIM, pl.ds(r0, WINDOW)].astype(BF16)
         for hh in range(GROUP)], axis=1)


def _group_probs(qgt, kw, absrel, valid, gi, sk_ref):
    bias = jnp.concatenate([jnp.where(valid, -_alibi_slope(gi * GROUP + hh) * absrel, NEG_INF)
                            for hh in range(GROUP)], axis=1)
    sink = jnp.concatenate([jnp.full((1, WINDOW), sk_ref[0, gi * GROUP + hh], F32) for hh in range(GROUP)], axis=1)
    s = _dot(kw, qgt) * SCALE + bias
    m = jnp.maximum(jnp.max(s, axis=0, keepdims=True), sink)
    p = jnp.exp(s - m)
    ps = jnp.exp(sink - m)
    inv = 1.0 / (jnp.sum(p, axis=0, keepdims=True) + ps)
    return p * inv, ps * inv


def _attn_fwd(qt, kp, vp, sinks):
    def body(sk_ref, qt_ref, kp_ref, vp_ref, o_ref):
        def blk(n, carry):
            r0 = pl.multiple_of(n * WINDOW, WINDOW)
            absrel, valid = _window_masks(n)
            for gi in range(KV_HEADS):
                kw = kp_ref[pl.ds(r0, 3 * WINDOW), gi * HEAD_DIM:(gi + 1) * HEAD_DIM].astype(BF16)
                vw = vp_ref[pl.ds(r0, 3 * WINDOW), gi * HEAD_DIM:(gi + 1) * HEAD_DIM].astype(BF16)
                pr, _ = _group_probs(_group_cols(qt_ref, r0, gi), kw, absrel, valid, gi, sk_ref)
                og = _dot_tn(pr.astype(BF16), vw)
                for hh in range(GROUP):
                    h = gi * GROUP + hh
                    o_ref[pl.ds(r0, WINDOW), h * HEAD_DIM:(h + 1) * HEAD_DIM] = og[hh * WINDOW:(hh + 1) * WINDOW]
            return carry

        lax.fori_loop(0, N_QBLOCKS, blk, 0)

    vmem = pl.BlockSpec(memory_space=pltpu.VMEM)
    return pl.pallas_call(
        body, name="attn_fwd",
        in_specs=[pl.BlockSpec(memory_space=pltpu.SMEM), vmem, vmem, vmem], out_specs=vmem,
        out_shape=jax.ShapeDtypeStruct((SEQ, ATTN_WIDTH), F32),
        compiler_params=_cparams(),
    )(sinks, qt, kp, vp)


def _attn_bwd(qt, kp, vp, sinks, dot_):
    def body(sk_ref, qt_ref, kp_ref, vp_ref, dot_ref, dqt_ref, dkp_ref, dvp_ref, dsk_ref, dsk_acc):
        dkp_ref[...] = jnp.zeros_like(dkp_ref)
        dvp_ref[...] = jnp.zeros_like(dvp_ref)
        dsk_acc[...] = jnp.zeros_like(dsk_acc)

        def blk(n, carry):
            r0 = pl.multiple_of(n * WINDOW, WINDOW)
            absrel, valid = _window_masks(n)
            for gi in range(KV_HEADS):
                gcols = slice(gi * HEAD_DIM, (gi + 1) * HEAD_DIM)
                kw = kp_ref[pl.ds(r0, 3 * WINDOW), gcols].astype(BF16)
                vw = vp_ref[pl.ds(r0, 3 * WINDOW), gcols].astype(BF16)
                qgt = _group_cols(qt_ref, r0, gi)
                dogt = _group_cols(dot_ref, r0, gi)
                pr, psink = _group_probs(qgt, kw, absrel, valid, gi, sk_ref)
                dp = _dot(vw, dogt)
                delta = jnp.sum(pr * dp, axis=0, keepdims=True)
                ds = (pr * (dp - delta)).astype(BF16)
                dsk_acc[gi:gi + 1, :] += -(psink * delta)
                dqgt = _dot_tn(kw, ds) * SCALE
                for hh in range(GROUP):
                    h = gi * GROUP + hh
                    dqt_ref[h * HEAD_DIM:(h + 1) * HEAD_DIM, pl.ds(r0, WINDOW)] = dqgt[:, hh * WINDOW:(hh + 1) * WINDOW]
                dkp_ref[pl.ds(r0, 3 * WINDOW), gcols] += _dot_nt(ds, qgt) * SCALE
                dvp_ref[pl.ds(r0, 3 * WINDOW), gcols] += _dot_nt(pr.astype(BF16), dogt)
            return carry

        lax.fori_loop(0, N_QBLOCKS, blk, 0)
        for h in range(ATTN_HEADS):
            gi, hh = divmod(h, GROUP)
            dsk_ref[:, h:h + 1] = jnp.sum(dsk_acc[gi:gi + 1, hh * WINDOW:(hh + 1) * WINDOW], axis=1, keepdims=True)

    vmem = pl.BlockSpec(memory_space=pltpu.VMEM)
    return pl.pallas_call(
        body, name="attn_bwd",
        in_specs=[pl.BlockSpec(memory_space=pltpu.SMEM), vmem, vmem, vmem, vmem],
        out_specs=[vmem, vmem, vmem, vmem],
        out_shape=[jax.ShapeDtypeStruct((ATTN_WIDTH, SEQ), F32),
                   jax.ShapeDtypeStruct((SEQ + 2 * WINDOW, KV_WIDTH), F32),
                   jax.ShapeDtypeStruct((SEQ + 2 * WINDOW, KV_WIDTH), F32),
                   jax.ShapeDtypeStruct((1, ATTN_HEADS), F32)],
        scratch_shapes=[pltpu.VMEM((KV_HEADS, GROUP * WINDOW), F32)],
        compiler_params=_cparams(),
    )(sinks, qt, kp, vp, dot_)


HALF_LANES = LANES // 2
BLOCK_ROWS = 32


def _embed_block(bt, q):
    z = jnp.zeros((16, HALF_LANES), bt.dtype)
    blk = jnp.concatenate([jnp.concatenate([bt[:16], z], axis=1), jnp.concatenate([z, bt[16:]], axis=1)], axis=0)
    parts = [jnp.zeros((BLOCK_ROWS * q, LANES), bt.dtype)] if q else []
    parts.append(blk)
    if q < 3:
        parts.append(jnp.zeros((BLOCK_ROWS * (3 - q), LANES), bt.dtype))
    return jnp.concatenate(parts, axis=0)


def _extract_block(m, q):
    blk = m[BLOCK_ROWS * q:BLOCK_ROWS * (q + 1)]
    return jnp.concatenate([blk[:16, :HALF_LANES], blk[16:, HALF_LANES:]], axis=0)


def _ssm_prep(lam_re, lam_im, log_dt, bt_re, bt_im, c_re, c_im):
    nb = 2 * N_LANE_BLOCKS

    def body(lr_ref, li_ref, ldt_ref, btr_ref, bti_ref, ctr_ref, cti_ref, ar_ref, ai_ref, bb_ref, cc_ref):
        lr = jnp.minimum(lr_ref[...], LAMBDA_RE_MAX)
        li = li_ref[...]
        dt = jnp.exp(ldt_ref[...])
        mag = jnp.exp(lr * dt)
        ar = mag * jnp.cos(li * dt)
        ai = mag * jnp.sin(li * dt)
        den = lr * lr + li * li
        cr = ((ar - 1.0) * lr + ai * li) / den
        ci = (ai * lr - (ar - 1.0) * li) / den
        ar_ref[...] = ar
        ai_ref[...] = ai
        for i in range(nb):
            q = i % 4
            rows = slice(BLOCK_ROWS * i, BLOCK_ROWS * (i + 1))
            br = _embed_block(btr_ref[rows, :], q)
            bi = _embed_block(bti_ref[rows, :], q)
            cri, cii = cr[i:i + 1, :], ci[i:i + 1, :]
            bb_ref[i] = jnp.concatenate([cri * br - cii * bi, cri * bi + cii * br], axis=1).astype(BF16)
            cc_ref[i] = jnp.concatenate([_embed_block(ctr_ref[rows, :], q).T,
                                         -_embed_block(cti_ref[rows, :], q).T], axis=0).astype(BF16)

    return pl.pallas_call(
        body, name="ssm_prep",
        out_shape=[jax.ShapeDtypeStruct((nb, LANES), F32), jax.ShapeDtypeStruct((nb, LANES), F32),
                   jax.ShapeDtypeStruct((nb, LANES, 2 * LANES), BF16),
                   jax.ShapeDtypeStruct((nb, 2 * LANES, LANES), BF16)],
        compiler_params=_cparams(),
    )(lam_re, lam_im, log_dt, bt_re, bt_im, c_re, c_im)


def _ssm_prep_bwd(lam_re, lam_im, log_dt, bt_re, bt_im, dar, dai, dbb, dcc):
    nb = 2 * N_LANE_BLOCKS

    def body(lr_ref, li_ref, ldt_ref, btr_ref, bti_ref, dar_ref, dai_ref, dbb_ref, dcc_ref,
             glr_ref, gli_ref, gdt_ref, gbr_ref, gbi_ref, gcre_ref, gcim_ref, gcr_s, gci_s):
        lam = lr_ref[...]
        lr = jnp.minimum(lam, LAMBDA_RE_MAX)
        li = li_ref[...]
        dt = jnp.exp(ldt_ref[...])
        mag = jnp.exp(lr * dt)
        cs = jnp.cos(li * dt)
        sn = jnp.sin(li * dt)
        ar = mag * cs
        ai = mag * sn
        den = lr * lr + li * li
        nr = (ar - 1.0) * lr + ai * li
        ni = ai * lr - (ar - 1.0) * li
        cr = nr / den
        ci = ni / den
        for i in range(nb):
            q = i % 4
            rows = slice(BLOCK_ROWS * i, BLOCK_ROWS * (i + 1))
            br = _embed_block(btr_ref[rows, :], q)
            bi = _embed_block(bti_ref[rows, :], q)
            gbbr = dbb_ref[i, :, :LANES]
            gbbi = dbb_ref[i, :, LANES:]
            cri, cii = cr[i:i + 1, :], ci[i:i + 1, :]
            gcr_s[i:i + 1, :] = jnp.sum(gbbr * br + gbbi * bi, axis=0, keepdims=True)
            gci_s[i:i + 1, :] = jnp.sum(gbbi * br - gbbr * bi, axis=0, keepdims=True)
            gbr_ref[rows, :] = _extract_block(cri * gbbr + cii * gbbi, q)
            gbi_ref[rows, :] = _extract_block(cri * gbbi - cii * gbbr, q)
            gcre_ref[rows, :] = _extract_block(dcc_ref[i, :LANES, :].T, q)
            gcim_ref[rows, :] = -_extract_block(dcc_ref[i, LANES:, :].T, q)
        g_cr = gcr_s[...]
        g_ci = gci_s[...]
        g_nr = g_cr / den
        g_ni = g_ci / den
        g_den = -(g_cr * nr + g_ci * ni) / (den * den)
        g_ar = dar_ref[...] + g_nr * lr - g_ni * li
        g_ai = dai_ref[...] + g_nr * li + g_ni * lr
        g_lr = g_nr * (ar - 1.0) + g_ni * ai + g_den * 2.0 * lr
        g_li = g_nr * ai - g_ni * (ar - 1.0) + g_den * 2.0 * li
        g_mag = g_ar * cs + g_ai * sn
        g_th = (g_ai * cs - g_ar * sn) * mag
        g_lr = g_lr + g_mag * mag * dt
        g_li = g_li + g_th * dt
        g_dt = g_mag * mag * lr + g_th * li
        glr_ref[...] = jnp.where(lam < LAMBDA_RE_MAX, g_lr, 0.0)
        gli_ref[...] = g_li
        gl = g_dt * dt
        half = LANES // 2
        gdt_ref[:, 0:1] = jnp.sum(gl[:, :half], axis=1, keepdims=True)
        gdt_ref[:, 1:2] = jnp.sum(gl[:, half:], axis=1, keepdims=True)

    rows_shape = jax.ShapeDtypeStruct((nb * BLOCK_ROWS, HALF_LANES), F32)
    return pl.pallas_call(
        body, name="ssm_prep_bwd",
        out_shape=[jax.ShapeDtypeStruct((nb, LANES), F32), jax.ShapeDtypeStruct((nb, LANES), F32),
                   jax.ShapeDtypeStruct((nb, 2), F32), rows_shape, rows_shape, rows_shape, rows_shape],
        scratch_shapes=[pltpu.VMEM((nb, LANES), F32), pltpu.VMEM((nb, LANES), F32)],
        compiler_params=_cparams(),
    )(lam_re, lam_im, log_dt, bt_re, bt_im, dar, dai, dbb, dcc)


def _cmul(ar, ai, br, bi):
    return ar * br - ai * bi, ar * bi + ai * br


def _interleave_rows(src_ref, dst_ref):
    def step(j, carry):
        dst_ref[pl.ds(pl.multiple_of(j * 8, 8), 8), :] = src_ref[pl.ds(j, 8, stride=SCAN_CHUNK), :]
        return carry
    lax.fori_loop(0, SCAN_CHUNK, step, 0, unroll=4)


def _deinterleave_rows(src_ref, dst_ref):
    def step(j, carry):
        dst_ref[pl.ds(j, 8, stride=SCAN_CHUNK), :] = src_ref[pl.ds(pl.multiple_of(j * 8, 8), 8), :]
        return carry
    lax.fori_loop(0, SCAN_CHUNK, step, 0, unroll=4)


def _scan_inplace(re_ref, im_ref, a_re, a_im, reverse):
    nq = len(a_re)
    ch = SCAN_CHUNK
    ab_re = [jnp.broadcast_to(a, (8, LANES)) for a in a_re]
    ab_im = [jnp.broadcast_to(a, (8, LANES)) for a in a_im]

    def rows(j):
        jj = (ch - 1 - j) if reverse else j
        return pl.ds(pl.multiple_of(jj * 8, 8), 8)

    def sweep(init, store):
        def step(j, st):
            out = []
            r = rows(j)
            for qi in range(nq):
                xr, xi = st[2 * qi], st[2 * qi + 1]
                pr, pi = _cmul(ab_re[qi], ab_im[qi], xr, xi)
                xr = pr + re_ref[qi, r, :]
                xi = pi + im_ref[qi, r, :]
                if store:
                    re_ref[qi, r, :] = xr
                    im_ref[qi, r, :] = xi
                out += [xr, xi]
            return tuple(out)
        return lax.fori_loop(0, ch, step, tuple(init), unroll=2)

    zeros = [jnp.zeros((8, LANES), F32)] * (2 * nq)
    finals = sweep(zeros, store=False)

    row_id = lax.broadcasted_iota(jnp.int32, (8, LANES), 0)
    carries = []
    for qi in range(nq):
        pr, pi = ab_re[qi], ab_im[qi]
        for _ in range(8):
            pr, pi = _cmul(pr, pi, pr, pi)
        fr, fi = finals[2 * qi], finals[2 * qi + 1]
        sr = jnp.zeros((8, LANES), F32)
        si = jnp.zeros((8, LANES), F32)
        for _ in range(7):
            tr, ti = _cmul(pr, pi, sr, si)
            tr, ti = tr + fr, ti + fi
            if reverse:
                sr = jnp.where(row_id == 7, 0.0, pltpu.roll(tr, 7, axis=0))
                si = jnp.where(row_id == 7, 0.0, pltpu.roll(ti, 7, axis=0))
            else:
                sr = jnp.where(row_id == 0, 0.0, pltpu.roll(tr, 1, axis=0))
                si = jnp.where(row_id == 0, 0.0, pltpu.roll(ti, 1, axis=0))
        carries += [sr, si]
    sweep(carries, store=True)


SSM_Q = 4


def _ssm_fwd(u, are, aim, bb, cc, dskip, after=None):
    nq = SSM_Q
    deps = [] if after is None else [after]

    def body(u_ref, ar_ref, ai_ref, bb_ref, cc_ref, d_ref, *rest):
        y_ref, xr_ref, xi_ref, sre, sim, up, yp = rest[len(deps):]
        _interleave_rows(u_ref, up)
        uf = up[...]
        ub = uf.astype(BF16)
        yp[...] = d_ref[...] * uf
        for d in range(2):
            for qi in range(nq):
                sre[qi] = _dot(ub, bb_ref[d, qi, :, :LANES])
                sim[qi] = _dot(ub, bb_ref[d, qi, :, LANES:])
            _scan_inplace(sre, sim, [ar_ref[d, qi] for qi in range(nq)], [ai_ref[d, qi] for qi in range(nq)],
                          reverse=(d == 1))
            for qi in range(nq):
                xrb = sre[qi].astype(BF16)
                xib = sim[qi].astype(BF16)
                xr_ref[d, qi] = xrb
                xi_ref[d, qi] = xib
                yp[...] += _dot(xrb, cc_ref[d, qi, :LANES, :]) + _dot(xib, cc_ref[d, qi, LANES:, :])
        _deinterleave_rows(yp, y_ref)

    blk4 = lambda k: (0, k, 0, 0)
    return pl.pallas_call(
        body, name="ssm_fwd", grid=(SSM_WIDTH // LANES,),
        in_specs=[pl.BlockSpec((SEQ, LANES), lambda k: (0, k)),
                  pl.BlockSpec((2, nq, 1, LANES), blk4), pl.BlockSpec((2, nq, 1, LANES), blk4),
                  pl.BlockSpec((2, nq, LANES, 2 * LANES), blk4), pl.BlockSpec((2, nq, 2 * LANES, LANES), blk4),
                  pl.BlockSpec((1, LANES), lambda k: (0, k))] + [pl.BlockSpec(memory_space=pl.ANY)] * len(deps),
        out_specs=[pl.BlockSpec((SEQ, LANES), lambda k: (0, k)),
                   pl.BlockSpec((2, nq, SEQ, LANES), blk4), pl.BlockSpec((2, nq, SEQ, LANES), blk4)],
        out_shape=[jax.ShapeDtypeStruct((SEQ, SSM_WIDTH), F32),
                   jax.ShapeDtypeStruct((2, N_LANE_BLOCKS, SEQ, LANES), BF16),
                   jax.ShapeDtypeStruct((2, N_LANE_BLOCKS, SEQ, LANES), BF16)],
        scratch_shapes=[pltpu.VMEM((nq, SEQ, LANES), F32), pltpu.VMEM((nq, SEQ, LANES), F32),
                        pltpu.VMEM((SEQ, LANES), F32), pltpu.VMEM((SEQ, LANES), F32)],
        compiler_params=_cparams(("parallel",)),
    )(u, are, aim, bb, cc, dskip, *deps)


def _ssm_bwd(dy, u, xr, xi, are, aim, bb, cc, dskip, after=None):
    nq = SSM_Q
    body_rows = SEQ - 8
    deps = [] if after is None else [after]

    def body(dy_ref, u_ref, xr_ref, xi_ref, ar_ref, ai_ref, bb_ref, cc_ref, d_ref, *rest):
        du_ref, dd_ref, dcc_ref, dbb_ref, dar_ref, dai_ref, sre, sim, up, dyp, dup = rest[len(deps):]
        _interleave_rows(u_ref, up)
        _interleave_rows(dy_ref, dyp)
        dyf = dyp[...]
        uf = up[...]
        dyb = dyf.astype(BF16)
        ub = uf.astype(BF16)
        dd_ref[...] = jnp.sum(dyf * uf, axis=0, keepdims=True)
        dup[...] = d_ref[...] * dyf
        row8 = lax.broadcasted_iota(jnp.int32, (8, LANES), 0)
        for d in range(2):
            for qi in range(nq):
                dx = _dot_nt(dyb, cc_ref[d, qi])
                sre[qi] = dx[:, :LANES]
                sim[qi] = dx[:, LANES:]
                dcc_ref[d, qi] = _dot_tn(jnp.concatenate([xr_ref[d, qi], xi_ref[d, qi]], axis=1), dyb)
            _scan_inplace(sre, sim, [ar_ref[d, qi] for qi in range(nq)], [-ai_ref[d, qi] for qi in range(nq)],
                          reverse=(d == 0))
            for qi in range(nq):
                gr = sre[qi]
                gi = sim[qi]
                xrf = xr_ref[d, qi].astype(F32)
                xif = xi_ref[d, qi].astype(F32)
                if d == 0:
                    g_main_r, g_main_i = gr[8:], gi[8:]
                    x_main_r, x_main_i = xrf[:body_rows], xif[:body_rows]
                    g_edge_r, g_edge_i = gr[:8], gi[:8]
                    x_edge_r = jnp.where(row8 == 0, 0.0, pltpu.roll(xrf[body_rows:], 1, axis=0))
                    x_edge_i = jnp.where(row8 == 0, 0.0, pltpu.roll(xif[body_rows:], 1, axis=0))
                else:
                    g_main_r, g_main_i = gr[:body_rows], gi[:body_rows]
                    x_main_r, x_main_i = xrf[8:], xif[8:]
                    g_edge_r, g_edge_i = gr[body_rows:], gi[body_rows:]
                    x_edge_r = jnp.where(row8 == 7, 0.0, pltpu.roll(xrf[:8], 7, axis=0))
                    x_edge_i = jnp.where(row8 == 7, 0.0, pltpu.roll(xif[:8], 7, axis=0))
                dar_ref[d, qi] = (jnp.sum(g_main_r * x_main_r + g_main_i * x_main_i, axis=0, keepdims=True)
                                  + jnp.sum(g_edge_r * x_edge_r + g_edge_i * x_edge_i, axis=0, keepdims=True))
                dai_ref[d, qi] = (jnp.sum(g_main_i * x_main_r - g_main_r * x_main_i, axis=0, keepdims=True)
                                  + jnp.sum(g_edge_i * x_edge_r - g_edge_r * x_edge_i, axis=0, keepdims=True))
                gb = jnp.concatenate([gr, gi], axis=1).astype(BF16)
                dup[...] += _dot_nt(gb, bb_ref[d, qi])
                dbb_ref[d, qi] = _dot_tn(ub, gb)
        _deinterleave_rows(dup, du_ref)

    blk4 = lambda k: (0, k, 0, 0)
    col = lambda k: (0, k)
    bb_spec = pl.BlockSpec((2, nq, LANES, 2 * LANES), blk4)
    cc_spec = pl.BlockSpec((2, nq, 2 * LANES, LANES), blk4)
    a_spec = pl.BlockSpec((2, nq, 1, LANES), blk4)
    x_spec = pl.BlockSpec((2, nq, SEQ, LANES), blk4)
    a_shape = jax.ShapeDtypeStruct((2, N_LANE_BLOCKS, 1, LANES), F32)
    return pl.pallas_call(
        body, name="ssm_bwd", grid=(SSM_WIDTH // LANES,),
        in_specs=[pl.BlockSpec((SEQ, LANES), col), pl.BlockSpec((SEQ, LANES), col), x_spec, x_spec,
                  a_spec, a_spec, bb_spec, cc_spec, pl.BlockSpec((1, LANES), col)]
        + [pl.BlockSpec(memory_space=pl.ANY)] * len(deps),
        out_specs=[pl.BlockSpec((SEQ, LANES), col), pl.BlockSpec((1, LANES), col),
                   cc_spec, bb_spec, a_spec, a_spec],
        out_shape=[jax.ShapeDtypeStruct((SEQ, SSM_WIDTH), F32), jax.ShapeDtypeStruct((1, SSM_WIDTH), F32),
                   jax.ShapeDtypeStruct((2, N_LANE_BLOCKS, 2 * LANES, LANES), F32),
                   jax.ShapeDtypeStruct((2, N_LANE_BLOCKS, LANES, 2 * LANES), F32), a_shape, a_shape],
        scratch_shapes=[pltpu.VMEM((nq, SEQ, LANES), F32), pltpu.VMEM((nq, SEQ, LANES), F32),
                        pltpu.VMEM((SEQ, LANES), F32), pltpu.VMEM((SEQ, LANES), F32), pltpu.VMEM((SEQ, LANES), F32)],
        compiler_params=_cparams(("parallel",)),
    )(dy, u, xr, xi, are, aim, bb, cc, dskip, *deps)


GELU_C = 0.7978845608028654
GELU_K = 0.044715


def _gelu(y):
    return 0.5 * y * (1.0 + jnp.tanh(GELU_C * (y + GELU_K * y * y * y)))


def _gelu_grad(y):
    t = jnp.tanh(GELU_C * (y + GELU_K * y * y * y))
    return 0.5 * (1.0 + t) + 0.5 * y * (1.0 - t * t) * GELU_C * (1.0 + 3.0 * GELU_K * y * y)


def _mixout_fwd(o, y, glu_w, glu_b, gan, gsn, wout, x1):
    tm = MIX_TM

    def body(o_ref, y_ref, gw_ref, gb_ref, gan_ref, gsn_ref, w_ref, x1_ref, x2_ref, mx_ref):
        yg = _gelu(y_ref[...])
        z = _dot(yg.astype(BF16), gw_ref[...]) + gb_ref[...]
        so = yg * _sigmoid(z)
        na = _rms_fwd(o_ref[...], gan_ref[...])
        ns = _rms_fwd(so, gsn_ref[...])
        mixed = jnp.concatenate([na, ns], axis=-1).astype(BF16)
        mx_ref[...] = mixed
        x2_ref[...] = x1_ref[...] + _dot(mixed, w_ref[...])

    row = lambda i: (i, 0)
    const = lambda i: (0, 0)
    return pl.pallas_call(
        body, name="mixout_fwd", grid=(SEQ // tm,),
        in_specs=[pl.BlockSpec((tm, ATTN_WIDTH), row), pl.BlockSpec((tm, SSM_WIDTH), row),
                  pl.BlockSpec((SSM_WIDTH, SSM_WIDTH), const), pl.BlockSpec((1, SSM_WIDTH), const),
                  pl.BlockSpec((1, ATTN_WIDTH), const), pl.BlockSpec((1, SSM_WIDTH), const),
                  pl.BlockSpec((D_MODEL, D_MODEL), const), pl.BlockSpec((tm, D_MODEL), row)],
        out_specs=[pl.BlockSpec((tm, D_MODEL), row), pl.BlockSpec((tm, D_MODEL), row)],
        out_shape=[jax.ShapeDtypeStruct((SEQ, D_MODEL), F32), jax.ShapeDtypeStruct((SEQ, D_MODEL), BF16)],
        compiler_params=_cparams(("parallel",)),
    )(o, y, glu_w, glu_b, gan, gsn, wout, x1)


def _mixout_bwd(dx2, o, y, glu_w, glu_b, gan, gsn, wout):
    tm = MIX_TM

    def body(dx2_ref, o_ref, y_ref, gw_ref, gb_ref, gan_ref, gsn_ref, w_ref,
             do_ref, dy_ref, dz_ref, yg_ref, dxb_ref, dgan_ref, dgsn_ref, dgb_ref):
        i = pl.program_id(0)
        dxb = dx2_ref[...].astype(BF16)
        dxb_ref[...] = dxb
        dmixed = _dot_nt(dxb, w_ref[...])
        do, dgan = _rms_bwd(dmixed[:, :ATTN_WIDTH], o_ref[...], gan_ref[...])
        do_ref[...] = do.T
        yv = y_ref[...]
        yg = _gelu(yv)
        ygb = yg.astype(BF16)
        yg_ref[...] = ygb
        sg = _sigmoid(_dot(ygb, gw_ref[...]) + gb_ref[...])
        dso, dgsn = _rms_bwd(dmixed[:, ATTN_WIDTH:], yg * sg, gsn_ref[...])
        dz = dso * yg * sg * (1.0 - sg)
        dzb = dz.astype(BF16)
        dz_ref[...] = dzb
        dyg = dso * sg + _dot_nt(dzb, gw_ref[...])
        dy_ref[...] = dyg * _gelu_grad(yv)
        dgb = jnp.sum(dz, axis=0, keepdims=True)

        @pl.when(i == 0)
        def _():
            dgan_ref[...] = dgan
            dgsn_ref[...] = dgsn
            dgb_ref[...] = dgb

        @pl.when(i != 0)
        def _():
            dgan_ref[...] += dgan
            dgsn_ref[...] += dgsn
            dgb_ref[...] += dgb

    row = lambda i: (i, 0)
    const = lambda i: (0, 0)
    return pl.pallas_call(
        body, name="mixout_bwd", grid=(SEQ // tm,),
        in_specs=[pl.BlockSpec((tm, D_MODEL), row), pl.BlockSpec((tm, ATTN_WIDTH), row),
                  pl.BlockSpec((tm, SSM_WIDTH), row),
                  pl.BlockSpec((SSM_WIDTH, SSM_WIDTH), const), pl.BlockSpec((1, SSM_WIDTH), const),
                  pl.BlockSpec((1, ATTN_WIDTH), const), pl.BlockSpec((1, SSM_WIDTH), const),
                  pl.BlockSpec((D_MODEL, D_MODEL), const)],
        out_specs=[pl.BlockSpec((ATTN_WIDTH, tm), lambda i: (0, i)), pl.BlockSpec((tm, SSM_WIDTH), row),
                   pl.BlockSpec((tm, SSM_WIDTH), row), pl.BlockSpec((tm, SSM_WIDTH), row),
                   pl.BlockSpec((tm, D_MODEL), row),
                   pl.BlockSpec((1, ATTN_WIDTH), const), pl.BlockSpec((1, SSM_WIDTH), const),
                   pl.BlockSpec((1, SSM_WIDTH), const)],
        out_shape=[jax.ShapeDtypeStruct((ATTN_WIDTH, SEQ), F32), jax.ShapeDtypeStruct((SEQ, SSM_WIDTH), F32),
                   jax.ShapeDtypeStruct((SEQ, SSM_WIDTH), BF16), jax.ShapeDtypeStruct((SEQ, SSM_WIDTH), BF16),
                   jax.ShapeDtypeStruct((SEQ, D_MODEL), BF16),
                   jax.ShapeDtypeStruct((1, ATTN_WIDTH), F32), jax.ShapeDtypeStruct((1, SSM_WIDTH), F32),
                   jax.ShapeDtypeStruct((1, SSM_WIDTH), F32)],
        compiler_params=_cparams(("arbitrary",)),
    )(dx2, o, y, glu_w, glu_b, gan, gsn, wout)


def _local_step(x, target, w, p, late_weights, early_grads, after=None, midway=None):
    x1, h1, a1, b1 = _ffn_fwd(x, p["norm_ffn1"], w["wgt1"], w["wut1"], w["wd1"], "ffn1_fwd", after=after)
    h2, q, k, v, u = _mixin_fwd(x1, p["norm_mix"], w["wint"])
    kp = jnp.pad(k, ((WINDOW, WINDOW), (0, 0)))
    vp = jnp.pad(v, ((WINDOW, WINDOW), (0, 0)))
    o = _attn_fwd(q, kp, vp, p["attn_sinks"])

    lam_re = p["ssm_lambda_re"].reshape(2 * N_LANE_BLOCKS, LANES)
    lam_im = p["ssm_lambda_im"].reshape(2 * N_LANE_BLOCKS, LANES)
    log_dt = jnp.repeat(p["ssm_log_dt"].reshape(2, 32), 64, axis=-1).reshape(2 * N_LANE_BLOCKS, LANES)
    a_re, a_im, bb, cc = _ssm_prep(lam_re, lam_im, log_dt, p["ssm_b_re"], p["ssm_b_im"],
                                   p["ssm_c_re"], p["ssm_c_im"])
    shape_a = (2, N_LANE_BLOCKS, 1, LANES)
    a_re4, a_im4 = a_re.reshape(shape_a), a_im.reshape(shape_a)
    bb4 = bb.reshape(2, N_LANE_BLOCKS, LANES, 2 * LANES)
    cc4 = cc.reshape(2, N_LANE_BLOCKS, 2 * LANES, LANES)
    dskip = p["ssm_d"].T.reshape(1, SSM_WIDTH)
    y, xr, xi = _ssm_fwd(u, a_re4, a_im4, bb4, cc4, dskip, after=None if midway is None else midway(o))

    w2 = late_weights(y)
    x2, mixed = _mixout_fwd(o, y, w2["glu"], p["ssm_glu_b"], p["attn_out_norm"], p["ssm_out_norm"], w2["wout"], x1)
    dx3, h3, a3, b3, loss, d_final = _ffn_fwd(x2, p["norm_ffn2"], w2["wgt2"], w2["wut2"], w2["wd2"], "ffn2_fwd",
                                              head=(p["final_norm"], target))
    dx2, da3, db3, s3, df3, d_n2 = _ffn_bwd_act(dx3, x2, p["norm_ffn2"], a3, b3, w2["wgt2"], w2["wut2"], w2["wd2"],
                                                "ffn2_bwd_act")
    g_wgt2, g_wut2, g_wd2 = _mm_tn([(da3, h3), (db3, h3), (s3, df3)], "ffn2_bwd_w")

    do, dy, dz, ygb, dx2b, d_gan, d_gsn, d_glub = _mixout_bwd(
        dx2, o, y, w2["glu"], p["ssm_glu_b"], p["attn_out_norm"], p["ssm_out_norm"], w2["wout"])
    (g_wout,) = _mm_tn([(mixed, dx2b)], "wout_bwd_w")
    (g_glu,) = _mm_tn([(ygb, dz)], "glu_bwd_w")
    sent = early_grads(dict(glu=g_glu, wout=g_wout, wgt2=g_wgt2, wut2=g_wut2, wd2=g_wd2))

    du, d_dskip, dcc, dbb, dar, dai = _ssm_bwd(dy, u, xr, xi, a_re4, a_im4, bb4, cc4, dskip, after=sent)
    nb = 2 * N_LANE_BLOCKS
    g_lre, g_lim, g_ldt, g_btr, g_bti, g_cre, g_cim = _ssm_prep_bwd(
        lam_re, lam_im, log_dt, p["ssm_b_re"], p["ssm_b_im"], dar.reshape(nb, LANES), dai.reshape(nb, LANES),
        dbb.reshape(nb, LANES, 2 * LANES), dcc.reshape(nb, 2 * LANES, LANES))

    dq, dkp, dvp, d_sinks = _attn_bwd(q, kp, vp, p["attn_sinks"], do)
    dk = dkp[WINDOW:WINDOW + SEQ]
    dv = dvp[WINDOW:WINDOW + SEQ]
    dx1, dproj, d_nmix = _mixin_bwd(dq, dk, dv, du, w["wint"], x1, p["norm_mix"], dx2)
    (g_wint,) = _mm_tn([(dproj, h2)], "win_bwd_w")

    dx0, da1, db1, s1, df1, d_n1 = _ffn_bwd_act(dx1, x, p["norm_ffn1"], a1, b1, w["wgt1"], w["wut1"], w["wd1"],
                                                "ffn1_bwd_act")
    g_wgt1, g_wut1, g_wd1 = _mm_tn([(da1, h1), (db1, h1), (s1, df1)], "ffn1_bwd_w")

    big = dict(wgt1=g_wgt1, wut1=g_wut1, wd1=g_wd1, wint=g_wint)
    small = dict(
        norm_ffn1=d_n1, norm_mix=d_nmix, attn_sinks=d_sinks,
        ssm_lambda_re=g_lre.reshape(64, 64), ssm_lambda_im=g_lim.reshape(64, 64),
        ssm_log_dt=g_ldt.reshape(2, 32), ssm_b_re=g_btr, ssm_b_im=g_bti, ssm_c_re=g_cre, ssm_c_im=g_cim,
        ssm_d=d_dskip.reshape(32, 16).T, ssm_glu_b=d_glub, attn_out_norm=d_gan, ssm_out_norm=d_gsn,
        norm_ffn2=d_n2, final_norm=d_final, loss=loss)
    return loss, dx0, big, small


BIG = dict(
    wgt1=("ffn1_w_gate", 352, 1024, True), wut1=("ffn1_w_up", 352, 1024, True), wd1=("ffn1_w_down", 352, 1024, False),
    wint=("w_in", 160, 1024, True), glu=("ssm_glu_w", 64, 512, False), wout=("w_out", 128, 1024, False),
    wgt2=("ffn2_w_gate", 352, 1024, True), wut2=("ffn2_w_up", 352, 1024, True), wd2=("ffn2_w_down", 352, 1024, False))

SMALL = dict(
    norm_ffn1=(1, 1024), norm_mix=(1, 1024), attn_sinks=(1, 8), ssm_lambda_re=(64, 64), ssm_lambda_im=(64, 64),
    ssm_log_dt=(2, 32), ssm_b_re=(1024, 64), ssm_b_im=(1024, 64), ssm_c_re=(1024, 64), ssm_c_im=(1024, 64),
    ssm_d=(16, 32), ssm_glu_b=(1, 512), attn_out_norm=(1, 512), ssm_out_norm=(1, 512), norm_ffn2=(1, 1024),
    final_norm=(1, 1024), loss=(1, 128))
SMALL_TRANSPOSED = ("ssm_b_re", "ssm_b_im", "ssm_d")
SMALL_PARAMS = tuple(n for n in SMALL if n != "loss")

SMALL_PAIRS = (("ssm_lambda_re", "ssm_lambda_im"), ("ssm_c_re", "ssm_c_im"), ("ssm_b_re", "ssm_b_im"))
SMALL_VECS = ("norm_ffn1", "norm_mix", "norm_ffn2", "final_norm", "ssm_glu_b", "attn_out_norm", "ssm_out_norm")
SMALL_TILES = ("ssm_log_dt", "attn_sinks", "ssm_d", "loss")


def _small_offsets():
    off, table = 0, {}
    for re, im in SMALL_PAIRS:
        table[re] = table[im] = off
        off += SMALL[re][0]
    for n in SMALL_VECS:
        table[n] = off
        off += SMALL[n][1] // LANES
    for n in SMALL_TILES:
        off = -(-off // 8) * 8
        table[n] = off
        off += SMALL[n][0]
    return table, off


SMALL_OFFSET, SMALL_USED_ROWS = _small_offsets()
SMALL_ROWS = -(-SMALL_USED_ROWS // (8 * N_DEV)) * 8 * N_DEV


def _cast_shards(shards):
    names = list(BIG)

    def body(*refs):
        ins, outs = refs[:len(names)], refs[len(names):]
        for idx in range(len(names)):
            outs[idx][...] = ins[idx][...].astype(BF16)

    return pl.pallas_call(
        body, name="cast_shards",
        out_shape=[jax.ShapeDtypeStruct((BIG[n][1], BIG[n][2]), BF16) for n in names],
        compiler_params=_cparams(),
    )(*[shards[n] for n in names])


def _peer(x, y, c, r):
    px = 1 - x if r & 4 else x
    py = 1 - y if r & 2 else y
    pc = 1 - c if r & 1 else c
    return px, py, pc


FIRST_GROUP = ("wgt1", "wut1", "wd1", "wint")
LATE_GROUP = ("glu", "wout", "wgt2", "wut2", "wd2")
N_PEERS = N_DEV - 1
ANY_SPEC = pl.BlockSpec(memory_space=pl.ANY)
HBM_SPEC = pl.BlockSpec(memory_space=pltpu.HBM)
SEM_SPEC = pl.BlockSpec(memory_space=pltpu.SEMAPHORE)
DATAFLOW_EFFECT = pltpu.SideEffectType.DATAFLOW_SIDE_EFFECTING


def _mesh_pos():
    x, y, c = lax.axis_index("x"), lax.axis_index("y"), lax.axis_index("c")
    return x, y, c, 4 * x + 2 * y + c


def _gather_first(first, late):
    nf, nl = len(first), len(late)

    def body(*refs):
        f_in, l_in = refs[:nf], refs[nf:nf + nl]
        f_out, l_out = refs[nf + nl:2 * nf + nl], refs[2 * nf + nl:2 * (nf + nl)]
        send_sems, recv_sems, local_sems = refs[2 * (nf + nl):]
        x, y, c, me = _mesh_pos()
        sibling = (x, y, 1 - c)
        chips = [(x, 1 - y), (1 - x, y), (1 - x, 1 - y)]

        def idx(px, py, pc):
            return 4 * px + 2 * py + pc

        def copy(k, s, block, to, src=None):
            slot = f_out[k].at[block]
            return pltpu.make_async_remote_copy(
                src_ref=slot if src is None else src, dst_ref=slot, send_sem=send_sems.at[k, s],
                recv_sem=recv_sems.at[k, s], device_id=to, device_id_type=MESH_ID)

        local = []
        for k in range(nf + nl):
            src, dst = (f_in[k], f_out[k]) if k < nf else (l_in[k - nf], l_out[k - nf])
            mine = pltpu.make_async_copy(src, dst.at[me], local_sems.at[k])
            mine.start()
            local.append(mine)
        sends = []
        for j, chip in enumerate(chips):
            for k in range(nf):
                sends.append(copy(k, 1 + j, me, (*chip, c), src=f_in[k]))
                sends[-1].start()
        for k in range(nf):
            sends.append(copy(k, 0, me, sibling, src=f_in[k]))
            sends[-1].start()
        for j, chip in enumerate(chips):
            for k in range(nf):
                copy(k, 1 + j, idx(*chip, c), (*chip, c)).wait_recv()
                sends.append(copy(k, 4 + j, idx(*chip, c), sibling))
                sends[-1].start()
        for k in range(nf):
            copy(k, 0, idx(*sibling), sibling).wait_recv()
        for j, chip in enumerate(chips):
            for k in range(nf):
                copy(k, 4 + j, idx(*chip, 1 - c), sibling).wait_recv()
        for cp in sends:
            cp.wait_send()
        for cp in local:
            cp.wait()

    return pl.pallas_call(
        body, name="gather_first",
        in_specs=[ANY_SPEC] * (nf + nl), out_specs=[ANY_SPEC] * (nf + nl),
        out_shape=[jax.ShapeDtypeStruct((N_DEV,) + s.shape, s.dtype) for s in list(first) + list(late)],
        scratch_shapes=[pltpu.SemaphoreType.DMA((nf, N_PEERS)), pltpu.SemaphoreType.DMA((nf, N_PEERS)),
                        pltpu.SemaphoreType.DMA((nf + nl,))],
        compiler_params=pltpu.CompilerParams(has_side_effects=True),
    )(*first, *late)


def _split_copy(src_refs, land_refs, send_sems, recv_sems, k, r, pos, scatter, receiving):
    x, y, c, me = pos
    px, py, pc = _peer(x, y, c, r)
    peer_idx = 4 * px + 2 * py + pc
    if scatter:
        src, dst = src_refs[k].at[peer_idx], land_refs[k].at[r - 1]
    else:
        src, dst = src_refs[k], land_refs[k].at[peer_idx if receiving else me]
    return pltpu.make_async_remote_copy(
        src_ref=src, dst_ref=dst, send_sem=send_sems.at[k * N_PEERS + r - 1],
        recv_sem=recv_sems.at[k * N_PEERS + r - 1], device_id=(px, py, pc), device_id_type=MESH_ID)


def _split_start(name, srcs, lands, scatter):
    n = len(srcs)

    def body(*refs):
        src_refs, land_refs = refs[:n], refs[n:2 * n]
        send_sems, recv_sems = refs[2 * n], refs[2 * n + 1]
        token = refs[-1]
        pos = _mesh_pos()
        for k in range(n):
            for r in range(1, N_DEV):
                _split_copy(src_refs, land_refs, send_sems, recv_sems, k, r, pos, scatter, False).start()
        token[...] = jnp.zeros_like(token)

    thru = [pltpu.HBM(a.shape, a.dtype) for a in list(srcs) + list(lands)]
    outs = pl.pallas_call(
        body, name=name,
        in_specs=[HBM_SPEC] * (2 * n),
        out_specs=[SEM_SPEC, SEM_SPEC] + [HBM_SPEC] * (2 * n) + [pl.BlockSpec(memory_space=pltpu.VMEM)],
        out_shape=[pltpu.SemaphoreType.DMA((n * N_PEERS,)), pltpu.SemaphoreType.DMA((n * N_PEERS,))] + thru
        + [jax.ShapeDtypeStruct((8, LANES), F32)],
        input_output_aliases={i: 2 + i for i in range(2 * n)},
        compiler_params=pltpu.CompilerParams(has_side_effects=DATAFLOW_EFFECT),
    )(*[pltpu.with_memory_space_constraint(a, pltpu.HBM) for a in list(srcs) + list(lands)])
    return outs[0], outs[1], outs[2:2 + n], outs[2 + n:2 + 2 * n], outs[-1]


def _split_wait(name, send_sems, recv_sems, srcs, lands, scatter, after):
    n = len(srcs)

    def body(*refs):
        src_refs, land_refs = refs[:n], refs[n:2 * n]
        send, recv = refs[2 * n], refs[2 * n + 1]
        pos = _mesh_pos()
        for k in range(n):
            for r in range(1, N_DEV):
                cp = _split_copy(src_refs, land_refs, send, recv, k, r, pos, scatter, True)
                cp.wait_send()
                cp.wait_recv()

    thru = [pltpu.HBM(a.shape, a.dtype) for a in list(srcs) + list(lands)]
    outs = pl.pallas_call(
        body, name=name,
        in_specs=[HBM_SPEC] * (2 * n) + [SEM_SPEC, SEM_SPEC, ANY_SPEC],
        out_specs=[HBM_SPEC] * (2 * n), out_shape=thru,
        input_output_aliases={i: i for i in range(2 * n)},
        compiler_params=pltpu.CompilerParams(has_side_effects=DATAFLOW_EFFECT),
    )(*srcs, *lands, send_sems, recv_sems, after)
    return outs[:n], outs[n:]


def _late_copy(passing, src_refs, land_refs, send_sems, recv_sems, k, s, pos, receiving):
    x, y, c, me = pos
    chips = [(x, 1 - y), (1 - x, y), (1 - x, 1 - y)]
    sibling = (x, y, 1 - c)

    def idx(dev):
        return 4 * dev[0] + 2 * dev[1] + dev[2]

    if passing:
        to = sibling
        block = idx((*chips[s], 1 - c)) if receiving else idx((*chips[s], c))
        src = dst = land_refs[k].at[block]
        sem = k * 3 + s
    else:
        to = sibling if s == 0 else (*chips[s - 1], c)
        src, dst = src_refs[k], land_refs[k].at[idx(to) if receiving else me]
        sem = k * 4 + s
    return pltpu.make_async_remote_copy(src_ref=src, dst_ref=dst, send_sem=send_sems.at[sem],
                                        recv_sem=recv_sems.at[sem], device_id=to, device_id_type=MESH_ID)


def _late_gather_call(name, stage, srcs, lands, sems, after=None):
    n = len(srcs)
    n_sem_in = len(sems)
    has_after = after is not None

    def body(*refs):
        src_refs, land_refs = refs[:n], refs[n:2 * n]
        sem_in = refs[2 * n:2 * n + n_sem_in]
        outs = refs[2 * n + n_sem_in + (1 if has_after else 0):]
        pos = _mesh_pos()
        if stage == 0:
            own_send, own_recv = outs[0], outs[1]
            for s in (1, 2, 3, 0):
                for k in range(n):
                    _late_copy(False, src_refs, land_refs, own_send, own_recv, k, s, pos, False).start()
            outs[-1][...] = jnp.zeros_like(outs[-1])
        elif stage == 1:
            own_recv = sem_in[1]
            pass_send, pass_recv = outs[0], outs[1]
            for s in range(3):
                for k in range(n):
                    _late_copy(False, src_refs, land_refs, sem_in[0], own_recv, k, s + 1, pos, True).wait_recv()
                    _late_copy(True, src_refs, land_refs, pass_send, pass_recv, k, s, pos, False).start()
            outs[-1][...] = jnp.zeros_like(outs[-1])
        else:
            own_send, own_recv, pass_send, pass_recv = sem_in
            for k in range(n):
                _late_copy(False, src_refs, land_refs, own_send, own_recv, k, 0, pos, True).wait_recv()
                for s in range(4):
                    _late_copy(False, src_refs, land_refs, own_send, own_recv, k, s, pos, False).wait_send()
                for s in range(3):
                    cp = _late_copy(True, src_refs, land_refs, pass_send, pass_recv, k, s, pos, True)
                    cp.wait_recv()
                    cp.wait_send()

    thru = [pltpu.HBM(a.shape, a.dtype) for a in list(srcs) + list(lands)]
    new_sems = [[pltpu.SemaphoreType.DMA((n * 4,))] * 2, [pltpu.SemaphoreType.DMA((n * 3,))] * 2, []][stage]
    extra = [] if stage == 2 else [jax.ShapeDtypeStruct((8, LANES), F32)]
    outs = pl.pallas_call(
        body, name=name,
        in_specs=[HBM_SPEC] * (2 * n) + [SEM_SPEC] * n_sem_in + [ANY_SPEC] * has_after,
        out_specs=[SEM_SPEC] * len(new_sems) + [HBM_SPEC] * (2 * n) + [pl.BlockSpec(memory_space=pltpu.VMEM)] * len(extra),
        out_shape=new_sems + thru + extra,
        input_output_aliases={i: len(new_sems) + i for i in range(2 * n)},
        compiler_params=pltpu.CompilerParams(has_side_effects=DATAFLOW_EFFECT),
    )(*[pltpu.with_memory_space_constraint(a, pltpu.HBM) for a in list(srcs) + list(lands)], *sems,
      *([after] if has_after else []))
    ns = len(new_sems)
    return list(outs[:ns]), outs[ns:ns + n], outs[ns + n:ns + 2 * n], (outs[-1] if extra else None)


N_SEND_SLOTS = 3


def _exchange_last(grads, small_packed):
    ng = len(grads)
    ch = SMALL_ROWS // N_DEV
    max_rows = max(g.shape[1] for g in grads)
    cols = grads[0].shape[2]

    def body(*refs):
        g_in, s_in = refs[:ng], refs[ng]
        outs = refs[ng + 1:]
        own_out, land, stage = outs[:ng], outs[ng:2 * ng], outs[2 * ng:3 * ng]
        s_red, s_stage = outs[3 * ng], outs[3 * ng + 1]
        (va, vb, vo, vs, sm_in, sm_out, d2d_send, d2d_recv, ici_send, ici_recv, s1_send, s1_recv, s2_send, s2_recv,
         local_sems) = outs[3 * ng + 2:]
        x, y, c, me = _mesh_pos()
        sibling = (x, y, 1 - c)
        chips = [(x, y), (x, 1 - y), (1 - x, y), (1 - x, 1 - y)]

        def idx(chip, core):
            return 4 * chip[0] + 2 * chip[1] + core

        def d2d(k, j):
            return pltpu.make_async_remote_copy(
                src_ref=g_in[k].at[idx(chips[j], 1 - c)], dst_ref=stage[k].at[j], send_sem=d2d_send.at[k, j],
                recv_sem=d2d_recv.at[k, j], device_id=sibling, device_id_type=MESH_ID)

        def ici(k, j, slot):
            rows = g_in[k].shape[1]
            return pltpu.make_async_remote_copy(
                src_ref=vo.at[slot, pl.ds(0, rows)], dst_ref=land[k].at[j - 1], send_sem=ici_send.at[k, j - 1],
                recv_sem=ici_recv.at[k, j - 1], device_id=(*chips[j], c), device_id_type=MESH_ID)

        def small_scatter(r):
            px, py, pc = _peer(x, y, c, r)
            return pltpu.make_async_remote_copy(
                src_ref=s_in.at[pl.ds(pl.multiple_of((4 * px + 2 * py + pc) * ch, 8), ch)], dst_ref=s_stage.at[me],
                send_sem=s1_send.at[r - 1], recv_sem=s1_recv.at[r - 1], device_id=(px, py, pc), device_id_type=MESH_ID)

        def small_gather(r):
            return pltpu.make_async_remote_copy(
                src_ref=sm_out, dst_ref=s_red.at[me], send_sem=s2_send.at[r - 1], recv_sem=s2_recv.at[r - 1],
                device_id=_peer(x, y, c, r), device_id_type=MESH_ID)

        for r in range(1, N_DEV):
            small_scatter(r).start()
        mine = pltpu.make_async_copy(s_in.at[pl.ds(pl.multiple_of(me * ch, 8), ch)], s_stage.at[me], local_sems.at[0])
        mine.start()
        pairs = [(k, j) for k in range(ng) for j in (1, 2, 3)] + [(k, 0) for k in range(ng)]
        for k, j in pairs:
            d2d(k, j).start()

        for r in range(1, N_DEV):
            small_scatter(r).wait_recv()
        mine.wait()
        load = pltpu.make_async_copy(s_stage, sm_in, local_sems.at[1])
        load.start()
        load.wait()
        total = sm_in[0]
        for i in range(1, N_DEV):
            total = total + sm_in[i]
        sm_out[...] = total
        for r in range(1, N_DEV):
            small_gather(r).start()
        keep = pltpu.make_async_copy(sm_out, s_red.at[me], local_sems.at[2])
        keep.start()

        in_flight = {}
        for i, (k, j) in enumerate(pairs):
            slot = i % N_SEND_SLOTS
            rows = g_in[k].shape[1]
            if slot in in_flight:
                in_flight.pop(slot).wait_send()
            d2d(k, j).wait_recv()
            la = pltpu.make_async_copy(g_in[k].at[idx(chips[j], c)], va.at[pl.ds(0, rows)], local_sems.at[3])
            lb = pltpu.make_async_copy(stage[k].at[j], vb.at[pl.ds(0, rows)], local_sems.at[4])
            la.start()
            lb.start()
            la.wait()
            lb.wait()
            total = va[pl.ds(0, rows)].astype(F32) + vb[pl.ds(0, rows)].astype(F32)
            if j == 0:
                vs[pl.ds(0, rows)] = total
                st = pltpu.make_async_copy(vs.at[pl.ds(0, rows)], own_out[k], local_sems.at[5])
                st.start()
                st.wait()
            else:
                vo[slot, pl.ds(0, rows)] = total.astype(BF16)
                cp = ici(k, j, slot)
                cp.start()
                in_flight[slot] = cp
        for cp in in_flight.values():
            cp.wait_send()

        for j in (1, 2, 3, 0):
            for k in range(ng):
                d2d(k, j).wait_send()
        for j in (1, 2, 3):
            for k in range(ng):
                ici(k, j, 0).wait_recv()
        for r in range(1, N_DEV):
            small_scatter(r).wait_send()
            small_gather(r).wait_send()
            small_gather(r).wait_recv()
        keep.wait()

    out_shape = [jax.ShapeDtypeStruct(g.shape[1:], F32) for g in grads]
    out_shape += [jax.ShapeDtypeStruct((3,) + g.shape[1:], BF16) for g in grads]
    out_shape += [jax.ShapeDtypeStruct((4,) + g.shape[1:], BF16) for g in grads]
    out_shape += [jax.ShapeDtypeStruct((N_DEV, ch, LANES), F32), jax.ShapeDtypeStruct((N_DEV, ch, LANES), F32)]
    outs = pl.pallas_call(
        body, name="exchange_last",
        in_specs=[ANY_SPEC] * (ng + 1), out_specs=[ANY_SPEC] * len(out_shape), out_shape=out_shape,
        scratch_shapes=[pltpu.VMEM((max_rows, cols), BF16), pltpu.VMEM((max_rows, cols), BF16),
                        pltpu.VMEM((N_SEND_SLOTS, max_rows, cols), BF16), pltpu.VMEM((max_rows, cols), F32),
                        pltpu.VMEM((N_DEV, ch, LANES), F32), pltpu.VMEM((ch, LANES), F32),
                        pltpu.SemaphoreType.DMA((ng, 4)), pltpu.SemaphoreType.DMA((ng, 4)),
                        pltpu.SemaphoreType.DMA((ng, 3)), pltpu.SemaphoreType.DMA((ng, 3)),
                        pltpu.SemaphoreType.DMA((N_PEERS,)), pltpu.SemaphoreType.DMA((N_PEERS,)),
                        pltpu.SemaphoreType.DMA((N_PEERS,)), pltpu.SemaphoreType.DMA((N_PEERS,)),
                        pltpu.SemaphoreType.DMA((6,))],
        compiler_params=pltpu.CompilerParams(has_side_effects=True, vmem_limit_bytes=VMEM_LIMIT),
    )(*grads, small_packed)
    return outs[:ng], outs[ng:2 * ng], outs[3 * ng].reshape(SMALL_ROWS, LANES)


def _adamw_math(w, g, m, v):
    m2 = ADAM_B1 * m + (1.0 - ADAM_B1) * g
    v2 = ADAM_B2 * v + (1.0 - ADAM_B2) * (g * g)
    m_hat = m2 / (1.0 - ADAM_B1 ** ADAM_STEP)
    v_hat = v2 / (1.0 - ADAM_B2 ** ADAM_STEP)
    delta = -ADAM_LR * (m_hat / (jnp.sqrt(v_hat) + ADAM_EPS) + ADAM_WD * w)
    return delta, m2, v2


ADAM_ROW_TILES = 2


def _adamw_big(own, parts, w, m, v, name):
    shape = w.shape
    own_is_blocks = own.ndim == 3
    tr = shape[0] // ADAM_ROW_TILES
    n_parts = parts.shape[0]

    def body(own_ref, p_ref, w_ref, m_ref, v_ref, g_ref, d_ref, m2_ref, v2_ref, own_s, sem):
        rows = pl.ds(pl.multiple_of(pl.program_id(0) * tr, 16), tr)
        if own_is_blocks:
            cp = pltpu.make_async_copy(own_ref.at[_mesh_pos()[3], rows], own_s, sem)
        else:
            cp = pltpu.make_async_copy(own_ref.at[rows], own_s, sem)
        cp.start()
        cp.wait()
        g = own_s[...].astype(F32)
        for i in range(n_parts):
            g = g + p_ref[i].astype(F32)
        delta, m2, v2 = _adamw_math(w_ref[...], g, m_ref[...], v_ref[...])
        g_ref[...] = g
        d_ref[...] = delta
        m2_ref[...] = m2
        v2_ref[...] = v2

    tile = pl.BlockSpec((tr, shape[1]), lambda i: (i, 0))
    return pl.pallas_call(
        body, name=name, grid=(ADAM_ROW_TILES,),
        in_specs=[ANY_SPEC, pl.BlockSpec((n_parts, tr, shape[1]), lambda i: (0, i, 0)), tile, tile, tile],
        out_specs=[tile] * 4, out_shape=[jax.ShapeDtypeStruct(shape, F32)] * 4,
        scratch_shapes=[pltpu.VMEM((tr, shape[1]), own.dtype), pltpu.SemaphoreType.DMA(())],
        compiler_params=_cparams(("arbitrary",)),
    )(own, parts, w, m, v)


def _pack_small(grads):
    names = list(SMALL)

    def body(*refs):
        ins, out = dict(zip(names, refs[:-1])), refs[-1]
        out[...] = jnp.zeros_like(out)
        for re, im in SMALL_PAIRS:
            off, rows = SMALL_OFFSET[re], SMALL[re][0]
            out[off:off + rows, :] = jnp.concatenate([ins[re][...], ins[im][...]], axis=1)
        for n in SMALL_VECS:
            off, vec = SMALL_OFFSET[n], ins[n][...]
            for i in range(SMALL[n][1] // LANES):
                out[off + i:off + i + 1, :] = vec[:, i * LANES:(i + 1) * LANES]
        for n in SMALL_TILES:
            off, (rows, cols) = SMALL_OFFSET[n], SMALL[n]
            out[off:off + rows, 0:cols] = ins[n][...]

    return pl.pallas_call(
        body, name="pack_small", out_shape=jax.ShapeDtypeStruct((SMALL_ROWS, LANES), F32),
        compiler_params=_cparams(),
    )(*[grads[n] for n in names])


def _unpack_small_ref(g_ref, n):
    off, (rows, cols) = SMALL_OFFSET[n], SMALL[n]
    for re, im in SMALL_PAIRS:
        if n == re:
            return g_ref[off:off + rows, 0:HALF_LANES]
        if n == im:
            return g_ref[off:off + rows, HALF_LANES:LANES]
    if n in SMALL_VECS:
        return jnp.concatenate([g_ref[off + i:off + i + 1, :] for i in range(cols // LANES)], axis=1)
    return g_ref[off:off + rows, 0:cols]


def _adamw_small(g_packed, w, m, v):
    names = list(SMALL_PARAMS)
    n = len(names)

    def body(g_ref, *refs):
        w_refs, m_refs, v_refs, outs = refs[:n], refs[n:2 * n], refs[2 * n:3 * n], refs[3 * n:]
        for idx, name in enumerate(names):
            g = _unpack_small_ref(g_ref, name)
            delta, m2, v2 = _adamw_math(w_refs[idx][...], g, m_refs[idx][...], v_refs[idx][...])
            outs[4 * idx][...] = g
            outs[4 * idx + 1][...] = delta
            outs[4 * idx + 2][...] = m2
            outs[4 * idx + 3][...] = v2
        outs[4 * n][...] = _unpack_small_ref(g_ref, "loss")

    outs = pl.pallas_call(
        body, name="adamw_small",
        out_shape=[jax.ShapeDtypeStruct(SMALL[name], F32) for name in names for _ in range(4)]
        + [jax.ShapeDtypeStruct(SMALL["loss"], F32)],
        compiler_params=_cparams(),
    )(g_packed, *[w[k] for k in names], *[m[k] for k in names], *[v[k] for k in names])
    return {name: outs[4 * idx:4 * idx + 4] for idx, name in enumerate(names)}, outs[4 * n]


WEIGHT_NAMES = ['norm_ffn1', 'ffn1_w_gate', 'ffn1_w_up', 'ffn1_w_down', 'norm_mix', 'w_in', 'attn_sinks',
                'ssm_lambda_re', 'ssm_lambda_im', 'ssm_log_dt', 'ssm_b_re', 'ssm_b_im', 'ssm_c_re', 'ssm_c_im',
                'ssm_d', 'ssm_glu_w', 'ssm_glu_b', 'attn_out_norm', 'ssm_out_norm', 'w_out', 'norm_ffn2',
                'ffn2_w_gate', 'ffn2_w_up', 'ffn2_w_down', 'final_norm']


def kernel(x, norm_ffn1, ffn1_w_gate, ffn1_w_up, ffn1_w_down, norm_mix, w_in, attn_sinks, ssm_lambda_re, ssm_lambda_im, ssm_log_dt, ssm_b_re, ssm_b_im, ssm_c_re, ssm_c_im, ssm_d, ssm_glu_w, ssm_glu_b, attn_out_norm, ssm_out_norm, w_out, norm_ffn2, ffn2_w_gate, ffn2_w_up, ffn2_w_down, final_norm, loss_target, m_norm_ffn1, m_ffn1_w_gate, m_ffn1_w_up, m_ffn1_w_down, m_norm_mix, m_w_in, m_attn_sinks, m_ssm_lambda_re, m_ssm_lambda_im, m_ssm_log_dt, m_ssm_b_re, m_ssm_b_im, m_ssm_c_re, m_ssm_c_im, m_ssm_d, m_ssm_glu_w, m_ssm_glu_b, m_attn_out_norm, m_ssm_out_norm, m_w_out, m_norm_ffn2, m_ffn2_w_gate, m_ffn2_w_up, m_ffn2_w_down, m_final_norm, v_norm_ffn1, v_ffn1_w_gate, v_ffn1_w_up, v_ffn1_w_down, v_norm_mix, v_w_in, v_attn_sinks, v_ssm_lambda_re, v_ssm_lambda_im, v_ssm_log_dt, v_ssm_b_re, v_ssm_b_im, v_ssm_c_re, v_ssm_c_im, v_ssm_d, v_ssm_glu_w, v_ssm_glu_b, v_attn_out_norm, v_ssm_out_norm, v_w_out, v_norm_ffn2, v_ffn2_w_gate, v_ffn2_w_up, v_ffn2_w_down, v_final_norm):
    args = dict(locals())
    weights = {n: args[n] for n in WEIGHT_NAMES}
    moms = {n: args["m_" + n] for n in WEIGHT_NAMES}
    vars_ = {n: args["v_" + n] for n in WEIGHT_NAMES}

    def shard2d(a, k):
        a = a.reshape(a.shape[-2], a.shape[-1])
        return a.T if BIG[k][3] else a

    def shard_master(a, k):
        return (a.T if BIG[k][3] else a).reshape(weights[BIG[k][0]].shape)

    def blocks(g, k):
        return g.reshape(N_DEV, BIG[k][1], BIG[k][2])

    def full(g, k):
        return g.reshape(N_DEV * BIG[k][1], BIG[k][2])

    shards = dict(zip(BIG, _cast_shards({k: shard2d(weights[BIG[k][0]], k) for k in BIG})))
    nf = len(FIRST_GROUP)
    got = _gather_first([shards[k] for k in FIRST_GROUP], [shards[k] for k in LATE_GROUP])
    w_first = {k: full(g, k) for k, g in zip(FIRST_GROUP, got[:nf])}
    late = {}
    late["own_sems"], late["srcs"], late["lands"], w_token = _late_gather_call(
        "gather_late_start", 0, [shards[k] for k in LATE_GROUP], got[nf:], [])

    def late_pass(dep):
        late["pass_sems"], late["srcs"], late["lands"], token = _late_gather_call(
            "gather_late_pass", 1, late["srcs"], late["lands"], late["own_sems"], after=dep)
        return token

    def late_weights(dep):
        _, _, lands, _ = _late_gather_call("gather_late_wait", 2, late["srcs"], late["lands"],
                                           late["own_sems"] + late["pass_sems"], after=dep)
        return {k: full(g, k) for k, g in zip(LATE_GROUP, lands)}

    early = {}

    def early_grads(g):
        srcs = [blocks(g[k], k) for k in LATE_GROUP]
        lands = [lax.empty((N_PEERS, BIG[k][1], BIG[k][2]), BF16) for k in LATE_GROUP]
        early["send"], early["recv"], early["srcs"], early["lands"], token = _split_start(
            "grads_late_start", srcs, lands, scatter=True)
        return token

    def small2d(a, n):
        if n in SMALL_TRANSPOSED:
            a = jnp.swapaxes(a, -1, -2)
        return a.reshape(SMALL[n])

    def small_master(a, n):
        if n in SMALL_TRANSPOSED:
            shape = weights[n].shape
            return jnp.swapaxes(a.reshape(shape[:-2] + (shape[-1], shape[-2])), -1, -2)
        return a.reshape(weights[n].shape)

    small_p = {n: small2d(weights[n], n) for n in SMALL_PARAMS}
    _, grad_x, g_first, g_small = _local_step(
        x.reshape(SEQ, D_MODEL), loss_target.reshape(SEQ, D_MODEL), w_first, small_p, late_weights, early_grads,
        after=w_token, midway=late_pass)

    own_sums, first_parts, small_grad = _exchange_last([blocks(g_first[k], k) for k in FIRST_GROUP],
                                                       _pack_small(g_small))
    own_late, late_parts = _split_wait("grads_late_wait", early["send"], early["recv"], early["srcs"],
                                       early["lands"], True, small_grad)
    own = dict(zip(FIRST_GROUP + LATE_GROUP, list(own_sums) + list(own_late)))
    parts = dict(zip(FIRST_GROUP + LATE_GROUP, list(first_parts) + list(late_parts)))
    outs = {}
    for k in BIG:
        n = BIG[k][0]
        outs[n] = [shard_master(o, k) for o in
                   _adamw_big(own[k], parts[k], shard2d(weights[n], k), shard2d(moms[n], k), shard2d(vars_[n], k),
                              "adamw_" + n)]
    small_out, loss_row = _adamw_small(small_grad, small_p, {n: small2d(moms[n], n) for n in SMALL_PARAMS},
                                       {n: small2d(vars_[n], n) for n in SMALL_PARAMS})
    for n in SMALL_PARAMS:
        outs[n] = [small_master(o, n) for o in small_out[n]]

    result = [loss_row[0, 0], grad_x.reshape(x.shape)]
    for i in range(4):
        result += [outs[n][i] for n in WEIGHT_NAMES]
    return tuple(result)
```

```python
import functools

import jax
import jax.numpy as jnp
from jax import lax
from jax.experimental import pallas as pl
from jax.experimental.pallas import tpu as pltpu

F32 = jnp.float32
BF16 = jnp.bfloat16

N_DEV = 8
SEQ = 2048
D_MODEL = 1024
D_FF = 2816
ATTN_HEADS = 8
KV_HEADS = 2
HEAD_DIM = 64
ATTN_WIDTH = 512
KV_WIDTH = 128
WINDOW = 128
SSM_WIDTH = 512
IN_WIDTH = 1280
EPS = 1e-6
MASKED_DISTANCE = 1e33
LAMBDA_RE_MAX = -1e-4
LANES = 128
N_LANE_BLOCKS = 16
SCAN_CHUNK = SEQ // 8

ADAM_LR = 0.001
ADAM_B1 = 0.9
ADAM_B2 = 0.999
ADAM_EPS = 1e-08
ADAM_WD = 0.01
ADAM_STEP = 10

VMEM_LIMIT = 60 * 1024 * 1024
MESH_ID = pl.DeviceIdType.MESH


def _cparams(sem=None):
    return pltpu.CompilerParams(dimension_semantics=sem, vmem_limit_bytes=VMEM_LIMIT)


def _dot(a, b):
    return jnp.dot(a, b, preferred_element_type=F32)


def _dot_nt(a, b):
    return lax.dot_general(a, b, (((1,), (1,)), ((), ())), preferred_element_type=F32)


def _dot_tn(a, b):
    return lax.dot_general(a, b, (((0,), (0,)), ((), ())), preferred_element_type=F32)


def _rms_fwd(x, g):
    r = lax.rsqrt(jnp.mean(x * x, axis=-1, keepdims=True) + EPS)
    return x * r * g


def _rms_bwd(dh, x, g):
    r = lax.rsqrt(jnp.mean(x * x, axis=-1, keepdims=True) + EPS)
    xh = x * r
    dg = jnp.sum(dh * xh, axis=0, keepdims=True)
    dxh = dh * g
    dx = r * (dxh - xh * jnp.mean(dxh * xh, axis=-1, keepdims=True))
    return dx, dg


def _sigmoid(x):
    return 1.0 / (1.0 + jnp.exp(-x))


FFN_TM = 512
FFN_TF = 1408


def _ffn_fwd(x, g, wgt, wut, wd, name, after=None, head=None):
    tm, tf = FFN_TM, FFN_TF
    nj = D_FF // tf
    deps = [] if after is None else [after]
    n_in = len(deps) + (2 if head else 0)

    def body(x_ref, g_ref, wg_ref, wu_ref, wd_ref, *rest):
        i = pl.program_id(0)
        j = pl.program_id(1)
        if head:
            gf_ref, t_ref = rest[len(deps):n_in]
            xo_ref, h_ref, a_ref, b_ref, loss_ref, dgf_ref, h_s, acc = rest[n_in:]
        else:
            xo_ref, h_ref, a_ref, b_ref, h_s, acc = rest[n_in:]

        @pl.when(j == 0)
        def _():
            h = _rms_fwd(x_ref[...], g_ref[...]).astype(BF16)
            h_s[...] = h
            h_ref[...] = h
            acc[...] = jnp.zeros_like(acc)

        h = h_s[...]
        a = _dot_nt(h, wg_ref[...])
        b = _dot_nt(h, wu_ref[...])
        a_ref[...] = a.astype(BF16)
        b_ref[...] = b.astype(BF16)
        s = (a * _sigmoid(a) * b).astype(BF16)
        acc[...] += _dot(s, wd_ref[...])

        @pl.when(j == nj - 1)
        def _():
            xo = x_ref[...] + 0.5 * acc[...]
            if not head:
                xo_ref[...] = xo
                return
            gf = gf_ref[...]
            err = _rms_fwd(xo, gf) - t_ref[...]
            part = jnp.broadcast_to(0.5 * jnp.sum(err * err) / D_MODEL, (1, LANES))
            dx, dgf = _rms_bwd(err * (1.0 / D_MODEL), xo, gf)
            xo_ref[...] = dx

            @pl.when(i == 0)
            def _():
                loss_ref[...] = part
                dgf_ref[...] = dgf

            @pl.when(i != 0)
            def _():
                loss_ref[...] += part
                dgf_ref[...] += dgf

    row = lambda i, j: (i, 0)
    const = lambda i, j: (0, 0)
    head_in = [pl.BlockSpec((1, D_MODEL), const), pl.BlockSpec((tm, D_MODEL), row)] if head else []
    head_out = [pl.BlockSpec((1, LANES), const), pl.BlockSpec((1, D_MODEL), const)] if head else []
    head_shape = [jax.ShapeDtypeStruct((1, LANES), F32), jax.ShapeDtypeStruct((1, D_MODEL), F32)] if head else []
    return pl.pallas_call(
        body, name=name, grid=(SEQ // tm, nj),
        in_specs=[pl.BlockSpec((tm, D_MODEL), row), pl.BlockSpec((1, D_MODEL), const),
                  pl.BlockSpec((tf, D_MODEL), lambda i, j: (j, 0)),
                  pl.BlockSpec((tf, D_MODEL), lambda i, j: (j, 0)),
                  pl.BlockSpec((tf, D_MODEL), lambda i, j: (j, 0))] + [pl.BlockSpec(memory_space=pl.ANY)] * len(deps)
        + head_in,
        out_specs=[pl.BlockSpec((tm, D_MODEL), row), pl.BlockSpec((tm, D_MODEL), row),
                   pl.BlockSpec((tm, tf), lambda i, j: (i, j)),
                   pl.BlockSpec((tm, tf), lambda i, j: (i, j))] + head_out,
        out_shape=[jax.ShapeDtypeStruct((SEQ, D_MODEL), F32), jax.ShapeDtypeStruct((SEQ, D_MODEL), BF16),
                   jax.ShapeDtypeStruct((SEQ, D_FF), BF16), jax.ShapeDtypeStruct((SEQ, D_FF), BF16)] + head_shape,
        scratch_shapes=[pltpu.VMEM((tm, D_MODEL), BF16), pltpu.VMEM((tm, D_MODEL), F32)],
        compiler_params=_cparams(("arbitrary" if head else "parallel", "arbitrary")),
    )(x, g, wgt, wut, wd, *deps, *(head or ()))


def _ffn_bwd_act(dxo, x, g, a, b, wgt, wut, wd, name):
    tm, tf = FFN_TM, FFN_TF
    nj = D_FF // tf

    def body(dxo_ref, x_ref, g_ref, a_ref, b_ref, wg_ref, wu_ref, wd_ref,
             dx_ref, da_ref, db_ref, s_ref, df_ref, dg_ref, df_s, acc):
        i = pl.program_id(0)
        j = pl.program_id(1)

        @pl.when(j == 0)
        def _():
            df = (0.5 * dxo_ref[...]).astype(BF16)
            df_s[...] = df
            df_ref[...] = df
            acc[...] = jnp.zeros_like(acc)

        ds = _dot_nt(df_s[...], wd_ref[...])
        av = a_ref[...].astype(F32)
        bv = b_ref[...].astype(F32)
        sig = _sigmoid(av)
        sl = av * sig
        s_ref[...] = (sl * bv).astype(BF16)
        db = (ds * sl).astype(BF16)
        da = (ds * bv * (sig * (1.0 + av * (1.0 - sig)))).astype(BF16)
        da_ref[...] = da
        db_ref[...] = db
        acc[...] += _dot(da, wg_ref[...]) + _dot(db, wu_ref[...])

        @pl.when(j == nj - 1)
        def _():
            dx, dg = _rms_bwd(acc[...], x_ref[...], g_ref[...])
            dx_ref[...] = dxo_ref[...] + dx

            @pl.when(i == 0)
            def _():
                dg_ref[...] = dg

            @pl.when(i != 0)
            def _():
                dg_ref[...] += dg

    row = lambda i, j: (i, 0)
    col = lambda i, j: (j, 0)
    tile = lambda i, j: (i, j)
    return pl.pallas_call(
        body, name=name, grid=(SEQ // tm, nj),
        in_specs=[pl.BlockSpec((tm, D_MODEL), row), pl.BlockSpec((tm, D_MODEL), row),
                  pl.BlockSpec((1, D_MODEL), lambda i, j: (0, 0)),
                  pl.BlockSpec((tm, tf), tile), pl.BlockSpec((tm, tf), tile),
                  pl.BlockSpec((tf, D_MODEL), col), pl.BlockSpec((tf, D_MODEL), col), pl.BlockSpec((tf, D_MODEL), col)],
        out_specs=[pl.BlockSpec((tm, D_MODEL), row),
                   pl.BlockSpec((tm, tf), tile), pl.BlockSpec((tm, tf), tile), pl.BlockSpec((tm, tf), tile),
                   pl.BlockSpec((tm, D_MODEL), row),
                   pl.BlockSpec((1, D_MODEL), lambda i, j: (0, 0))],
        out_shape=[jax.ShapeDtypeStruct((SEQ, D_MODEL), F32),
                   jax.ShapeDtypeStruct((SEQ, D_FF), BF16), jax.ShapeDtypeStruct((SEQ, D_FF), BF16),
                   jax.ShapeDtypeStruct((SEQ, D_FF), BF16),
                   jax.ShapeDtypeStruct((SEQ, D_MODEL), BF16),
                   jax.ShapeDtypeStruct((1, D_MODEL), F32)],
        scratch_shapes=[pltpu.VMEM((tm, D_MODEL), BF16), pltpu.VMEM((tm, D_MODEL), F32)],
        compiler_params=_cparams(("arbitrary", "arbitrary")),
    )(dxo, x, g, a, b, wgt, wut, wd)


def _mm_tn(pairs, name, tmm=256):
    m = pairs[0][0].shape[1]
    n_pairs = len(pairs)

    def body(*refs):
        ins, outs = refs[:2 * n_pairs], refs[2 * n_pairs:]
        for p in range(n_pairs):
            outs[p][...] = _dot_tn(ins[2 * p][...], ins[2 * p + 1][...]).astype(BF16)

    in_specs, out_specs, out_shape, args = [], [], [], []
    for a, b in pairs:
        n = b.shape[1]
        in_specs += [pl.BlockSpec((SEQ, tmm), lambda i: (0, i)), pl.BlockSpec((SEQ, n), lambda i: (0, 0))]
        out_specs.append(pl.BlockSpec((tmm, n), lambda i: (i, 0)))
        out_shape.append(jax.ShapeDtypeStruct((m, n), BF16))
        args += [a, b]
    return pl.pallas_call(body, name=name, grid=(m // tmm,), in_specs=in_specs, out_specs=out_specs,
                          out_shape=out_shape, compiler_params=_cparams(("parallel",)))(*args)


MIX_TM = 256


def _mixin_fwd(x, g, wint):
    tm = MIX_TM

    def body(x_ref, g_ref, w_ref, h_ref, q_ref, k_ref, v_ref, u_ref):
        h = _rms_fwd(x_ref[...], g_ref[...]).astype(BF16)
        h_ref[...] = h
        proj = _dot_nt(h, w_ref[...])
        q_ref[...] = proj[:, :ATTN_WIDTH].T
        k_ref[...] = proj[:, ATTN_WIDTH:ATTN_WIDTH + KV_WIDTH]
        v_ref[...] = proj[:, ATTN_WIDTH + KV_WIDTH:ATTN_WIDTH + 2 * KV_WIDTH]
        u_ref[...] = proj[:, ATTN_WIDTH + 2 * KV_WIDTH:]

    row = lambda i: (i, 0)
    return pl.pallas_call(
        body, name="mixin_fwd", grid=(SEQ // tm,),
        in_specs=[pl.BlockSpec((tm, D_MODEL), row), pl.BlockSpec((1, D_MODEL), lambda i: (0, 0)),
                  pl.BlockSpec((IN_WIDTH, D_MODEL), lambda i: (0, 0))],
        out_specs=[pl.BlockSpec((tm, D_MODEL), row), pl.BlockSpec((ATTN_WIDTH, tm), lambda i: (0, i)),
                   pl.BlockSpec((tm, KV_WIDTH), row), pl.BlockSpec((tm, KV_WIDTH), row),
                   pl.BlockSpec((tm, SSM_WIDTH), row)],
        out_shape=[jax.ShapeDtypeStruct((SEQ, D_MODEL), BF16), jax.ShapeDtypeStruct((ATTN_WIDTH, SEQ), F32),
                   jax.ShapeDtypeStruct((SEQ, KV_WIDTH), F32), jax.ShapeDtypeStruct((SEQ, KV_WIDTH), F32),
                   jax.ShapeDtypeStruct((SEQ, SSM_WIDTH), F32)],
        compiler_params=_cparams(("parallel",)),
    )(x, g, wint)


def _mixin_bwd(dqt, dk, dv, du, wint, x, g, dres):
    tm = MIX_TM

    def body(dq_ref, dk_ref, dv_ref, du_ref, w_ref, x_ref, g_ref, dres_ref, dx_ref, dp_ref, dg_ref):
        i = pl.program_id(0)
        dp = jnp.concatenate([dq_ref[...].T, dk_ref[...], dv_ref[...], du_ref[...]], axis=-1).astype(BF16)
        dp_ref[...] = dp
        dh = _dot(dp, w_ref[...])
        dx, dg = _rms_bwd(dh, x_ref[...], g_ref[...])
        dx_ref[...] = dres_ref[...] + dx

        @pl.when(i == 0)
        def _():
            dg_ref[...] = dg

        @pl.when(i != 0)
        def _():
            dg_ref[...] += dg

    row = lambda i: (i, 0)
    const = lambda i: (0, 0)
    return pl.pallas_call(
        body, name="mixin_bwd", grid=(SEQ // tm,),
        in_specs=[pl.BlockSpec((ATTN_WIDTH, tm), lambda i: (0, i)), pl.BlockSpec((tm, KV_WIDTH), row),
                  pl.BlockSpec((tm, KV_WIDTH), row), pl.BlockSpec((tm, SSM_WIDTH), row),
                  pl.BlockSpec((IN_WIDTH, D_MODEL), const), pl.BlockSpec((tm, D_MODEL), row),
                  pl.BlockSpec((1, D_MODEL), const), pl.BlockSpec((tm, D_MODEL), row)],
        out_specs=[pl.BlockSpec((tm, D_MODEL), row), pl.BlockSpec((tm, IN_WIDTH), row),
                   pl.BlockSpec((1, D_MODEL), const)],
        out_shape=[jax.ShapeDtypeStruct((SEQ, D_MODEL), F32), jax.ShapeDtypeStruct((SEQ, IN_WIDTH), BF16),
                   jax.ShapeDtypeStruct((1, D_MODEL), F32)],
        compiler_params=_cparams(("arbitrary",)),
    )(dqt, dk, dv, du, wint, x, g, dres)


N_QBLOCKS = SEQ // WINDOW
GROUP = ATTN_HEADS // KV_HEADS
SCALE = HEAD_DIM ** -0.5


def _alibi_slope(h):
    return 2.0 ** (-8.0 * (h + 1) / ATTN_HEADS)


def _window_masks(n):
    s_idx = lax.broadcasted_iota(jnp.int32, (3 * WINDOW, WINDOW), 0)
    t_idx = lax.broadcasted_iota(jnp.int32, (3 * WINDOW, WINDOW), 1)
    absrel = jnp.abs(s_idx - WINDOW - t_idx)
    key_pos = n * WINDOW - WINDOW + s_idx
    valid = (absrel <= WINDOW) & (key_pos >= 0) & (key_pos < SEQ)
    return jnp.where(valid, absrel.astype(F32), MASKED_DISTANCE)


def _group_cols(ref, r0, gi):
    return jnp.concatenate(
        [ref[(gi * GROUP + hh) * HEAD_DIM:(gi * GROUP + hh + 1) * HEAD_DIM, pl.ds(r0, WINDOW)].astype(BF16)
         for hh in range(GROUP)], axis=1)


def _group_probs(qgt, kw, dist, gi, sk_ref):
    bias = jnp.concatenate([-_alibi_slope(gi * GROUP + hh) * dist for hh in range(GROUP)], axis=1)
    sink = jnp.concatenate([jnp.full((1, WINDOW), sk_ref[0, gi * GROUP + hh], F32) for hh in range(GROUP)], axis=1)
    s = _dot(kw, qgt) * SCALE + bias
    m = jnp.maximum(jnp.max(s, axis=0, keepdims=True), sink)
    p = jnp.exp(s - m)
    ps = jnp.exp(sink - m)
    inv = 1.0 / (jnp.sum(p, axis=0, keepdims=True) + ps)
    return p * inv, ps * inv


def _pad_window(src_ref, dst_ref):
    zeros = jnp.zeros((WINDOW, KV_WIDTH), BF16)
    dst_ref[0:WINDOW, :] = zeros
    dst_ref[WINDOW + SEQ:, :] = zeros
    dst_ref[WINDOW:WINDOW + SEQ, :] = src_ref[...].astype(BF16)


def _attn_fwd(qt, k, v, sinks):
    def body(sk_ref, qt_ref, k_ref, v_ref, o_ref, kp_ref, vp_ref):
        _pad_window(k_ref, kp_ref)
        _pad_window(v_ref, vp_ref)

        def blk(n, carry):
            r0 = pl.multiple_of(n * WINDOW, WINDOW)
            dist = _window_masks(n)
            for gi in range(KV_HEADS):
                kw = kp_ref[pl.ds(r0, 3 * WINDOW), gi * HEAD_DIM:(gi + 1) * HEAD_DIM]
                vw = vp_ref[pl.ds(r0, 3 * WINDOW), gi * HEAD_DIM:(gi + 1) * HEAD_DIM]
                pr, _ = _group_probs(_group_cols(qt_ref, r0, gi), kw, dist, gi, sk_ref)
                og = _dot_tn(pr.astype(BF16), vw)
                for hh in range(GROUP):
                    h = gi * GROUP + hh
                    o_ref[pl.ds(r0, WINDOW), h * HEAD_DIM:(h + 1) * HEAD_DIM] = og[hh * WINDOW:(hh + 1) * WINDOW]
            return carry

        lax.fori_loop(0, N_QBLOCKS, blk, 0)

    vmem = pl.BlockSpec(memory_space=pltpu.VMEM)
    return pl.pallas_call(
        body, name="attn_fwd",
        in_specs=[pl.BlockSpec(memory_space=pltpu.SMEM), vmem, vmem, vmem], out_specs=vmem,
        out_shape=jax.ShapeDtypeStruct((SEQ, ATTN_WIDTH), F32),
        scratch_shapes=[pltpu.VMEM((SEQ + 2 * WINDOW, KV_WIDTH), BF16)] * 2,
        compiler_params=_cparams(),
    )(sinks, qt, k, v)


def _attn_bwd(qt, k, v, sinks, dot_):
    def body(sk_ref, qt_ref, k_ref, v_ref, dot_ref, dqt_ref, dk_ref, dv_ref, dsk_ref,
             dsk_acc, kp_ref, vp_ref, dkp_ref, dvp_ref):
        _pad_window(k_ref, kp_ref)
        _pad_window(v_ref, vp_ref)
        dkp_ref[...] = jnp.zeros_like(dkp_ref)
        dvp_ref[...] = jnp.zeros_like(dvp_ref)
        dsk_acc[...] = jnp.zeros_like(dsk_acc)

        def blk(n, carry):
            r0 = pl.multiple_of(n * WINDOW, WINDOW)
            dist = _window_masks(n)
            for gi in range(KV_HEADS):
                gcols = slice(gi * HEAD_DIM, (gi + 1) * HEAD_DIM)
                kw = kp_ref[pl.ds(r0, 3 * WINDOW), gcols]
                vw = vp_ref[pl.ds(r0, 3 * WINDOW), gcols]
                qgt = _group_cols(qt_ref, r0, gi)
                dogt = _group_cols(dot_ref, r0, gi)
                pr, psink = _group_probs(qgt, kw, dist, gi, sk_ref)
                dp = _dot(vw, dogt)
                delta = jnp.sum(pr * dp, axis=0, keepdims=True)
                ds = (pr * (dp - delta)).astype(BF16)
                dsk_acc[gi:gi + 1, :] += -(psink * delta)
                dqgt = _dot_tn(kw, ds) * SCALE
                for hh in range(GROUP):
                    h = gi * GROUP + hh
                    dqt_ref[h * HEAD_DIM:(h + 1) * HEAD_DIM, pl.ds(r0, WINDOW)] = dqgt[:, hh * WINDOW:(hh + 1) * WINDOW]
                dkp_ref[pl.ds(r0, 3 * WINDOW), gcols] += _dot_nt(ds, qgt) * SCALE
                dvp_ref[pl.ds(r0, 3 * WINDOW), gcols] += _dot_nt(pr.astype(BF16), dogt)
            return carry

        lax.fori_loop(0, N_QBLOCKS, blk, 0)
        for h in range(ATTN_HEADS):
            gi, hh = divmod(h, GROUP)
            dsk_ref[:, h:h + 1] = jnp.sum(dsk_acc[gi:gi + 1, hh * WINDOW:(hh + 1) * WINDOW], axis=1, keepdims=True)
        dk_ref[...] = dkp_ref[WINDOW:WINDOW + SEQ, :]
        dv_ref[...] = dvp_ref[WINDOW:WINDOW + SEQ, :]

    vmem = pl.BlockSpec(memory_space=pltpu.VMEM)
    padded = (SEQ + 2 * WINDOW, KV_WIDTH)
    return pl.pallas_call(
        body, name="attn_bwd",
        in_specs=[pl.BlockSpec(memory_space=pltpu.SMEM), vmem, vmem, vmem, vmem],
        out_specs=[vmem, vmem, vmem, vmem],
        out_shape=[jax.ShapeDtypeStruct((ATTN_WIDTH, SEQ), F32),
                   jax.ShapeDtypeStruct((SEQ, KV_WIDTH), F32), jax.ShapeDtypeStruct((SEQ, KV_WIDTH), F32),
                   jax.ShapeDtypeStruct((1, ATTN_HEADS), F32)],
        scratch_shapes=[pltpu.VMEM((KV_HEADS, GROUP * WINDOW), F32), pltpu.VMEM(padded, BF16),
                        pltpu.VMEM(padded, BF16), pltpu.VMEM(padded, F32), pltpu.VMEM(padded, F32)],
        compiler_params=_cparams(),
    )(sinks, qt, k, v, dot_)


HALF_LANES = LANES // 2
BLOCK_ROWS = 32


def _embed_block(bt, q):
    z = jnp.zeros((16, HALF_LANES), bt.dtype)
    blk = jnp.concatenate([jnp.concatenate([bt[:16], z], axis=1), jnp.concatenate([z, bt[16:]], axis=1)], axis=0)
    parts = [jnp.zeros((BLOCK_ROWS * q, LANES), bt.dtype)] if q else []
    parts.append(blk)
    if q < 3:
        parts.append(jnp.zeros((BLOCK_ROWS * (3 - q), LANES), bt.dtype))
    return jnp.concatenate(parts, axis=0)


def _extract_block(m, q):
    blk = m[BLOCK_ROWS * q:BLOCK_ROWS * (q + 1)]
    return jnp.concatenate([blk[:16, :HALF_LANES], blk[16:, HALF_LANES:]], axis=0)


def _ssm_prep(lam_re, lam_im, log_dt, bt_re, bt_im, c_re, c_im):
    nb = 2 * N_LANE_BLOCKS

    def body(lr_ref, li_ref, ldt_ref, btr_ref, bti_ref, ctr_ref, cti_ref, ar_ref, ai_ref, bb_ref, cc_ref):
        lr = jnp.minimum(lr_ref[...], LAMBDA_RE_MAX)
        li = li_ref[...]
        dt = jnp.exp(ldt_ref[...])
        mag = jnp.exp(lr * dt)
        ar = mag * jnp.cos(li * dt)
        ai = mag * jnp.sin(li * dt)
        den = lr * lr + li * li
        cr = ((ar - 1.0) * lr + ai * li) / den
        ci = (ai * lr - (ar - 1.0) * li) / den
        ar_ref[...] = ar
        ai_ref[...] = ai
        for i in range(nb):
            q = i % 4
            rows = slice(BLOCK_ROWS * i, BLOCK_ROWS * (i + 1))
            br = _embed_block(btr_ref[rows, :], q)
            bi = _embed_block(bti_ref[rows, :], q)
            cri, cii = cr[i:i + 1, :], ci[i:i + 1, :]
            bb_ref[i] = jnp.concatenate([cri * br - cii * bi, cri * bi + cii * br], axis=1).astype(BF16)
            cc_ref[i] = jnp.concatenate([_embed_block(ctr_ref[rows, :], q).T,
                                         -_embed_block(cti_ref[rows, :], q).T], axis=0).astype(BF16)

    return pl.pallas_call(
        body, name="ssm_prep",
        out_shape=[jax.ShapeDtypeStruct((nb, LANES), F32), jax.ShapeDtypeStruct((nb, LANES), F32),
                   jax.ShapeDtypeStruct((nb, LANES, 2 * LANES), BF16),
                   jax.ShapeDtypeStruct((nb, 2 * LANES, LANES), BF16)],
        compiler_params=_cparams(),
    )(lam_re, lam_im, log_dt, bt_re, bt_im, c_re, c_im)


def _ssm_prep_bwd(lam_re, lam_im, log_dt, bt_re, bt_im, dar, dai, dbb, dcc):
    nb = 2 * N_LANE_BLOCKS

    def body(lr_ref, li_ref, ldt_ref, btr_ref, bti_ref, dar_ref, dai_ref, dbb_ref, dcc_ref,
             glr_ref, gli_ref, gdt_ref, gbr_ref, gbi_ref, gcre_ref, gcim_ref, gcr_s, gci_s):
        lam = lr_ref[...]
        lr = jnp.minimum(lam, LAMBDA_RE_MAX)
        li = li_ref[...]
        dt = jnp.exp(ldt_ref[...])
        mag = jnp.exp(lr * dt)
        cs = jnp.cos(li * dt)
        sn = jnp.sin(li * dt)
        ar = mag * cs
        ai = mag * sn
        den = lr * lr + li * li
        nr = (ar - 1.0) * lr + ai * li
        ni = ai * lr - (ar - 1.0) * li
        cr = nr / den
        ci = ni / den
        for i in range(nb):
            q = i % 4
            rows = slice(BLOCK_ROWS * i, BLOCK_ROWS * (i + 1))
            br = _embed_block(btr_ref[rows, :], q)
            bi = _embed_block(bti_ref[rows, :], q)
            gbbr = dbb_ref[i, :, :LANES]
            gbbi = dbb_ref[i, :, LANES:]
            cri, cii = cr[i:i + 1, :], ci[i:i + 1, :]
            gcr_s[i:i + 1, :] = jnp.sum(gbbr * br + gbbi * bi, axis=0, keepdims=True)
            gci_s[i:i + 1, :] = jnp.sum(gbbi * br - gbbr * bi, axis=0, keepdims=True)
            gbr_ref[rows, :] = _extract_block(cri * gbbr + cii * gbbi, q)
            gbi_ref[rows, :] = _extract_block(cri * gbbi - cii * gbbr, q)
            gcre_ref[rows, :] = _extract_block(dcc_ref[i, :LANES, :].T, q)
            gcim_ref[rows, :] = -_extract_block(dcc_ref[i, LANES:, :].T, q)
        g_cr = gcr_s[...]
        g_ci = gci_s[...]
        g_nr = g_cr / den
        g_ni = g_ci / den
        g_den = -(g_cr * nr + g_ci * ni) / (den * den)
        g_ar = dar_ref[...] + g_nr * lr - g_ni * li
        g_ai = dai_ref[...] + g_nr * li + g_ni * lr
        g_lr = g_nr * (ar - 1.0) + g_ni * ai + g_den * 2.0 * lr
        g_li = g_nr * ai - g_ni * (ar - 1.0) + g_den * 2.0 * li
        g_mag = g_ar * cs + g_ai * sn
        g_th = (g_ai * cs - g_ar * sn) * mag
        g_lr = g_lr + g_mag * mag * dt
        g_li = g_li + g_th * dt
        g_dt = g_mag * mag * lr + g_th * li
        glr_ref[...] = jnp.where(lam < LAMBDA_RE_MAX, g_lr, 0.0)
        gli_ref[...] = g_li
        gl = g_dt * dt
        half = LANES // 2
        gdt_ref[:, 0:1] = jnp.sum(gl[:, :half], axis=1, keepdims=True)
        gdt_ref[:, 1:2] = jnp.sum(gl[:, half:], axis=1, keepdims=True)

    rows_shape = jax.ShapeDtypeStruct((nb * BLOCK_ROWS, HALF_LANES), F32)
    return pl.pallas_call(
        body, name="ssm_prep_bwd",
        out_shape=[jax.ShapeDtypeStruct((nb, LANES), F32), jax.ShapeDtypeStruct((nb, LANES), F32),
                   jax.ShapeDtypeStruct((nb, 2), F32), rows_shape, rows_shape, rows_shape, rows_shape],
        scratch_shapes=[pltpu.VMEM((nb, LANES), F32), pltpu.VMEM((nb, LANES), F32)],
        compiler_params=_cparams(),
    )(lam_re, lam_im, log_dt, bt_re, bt_im, dar, dai, dbb, dcc)


def _cmul(ar, ai, br, bi):
    return ar * br - ai * bi, ar * bi + ai * br


def _interleave_rows(src_ref, dst_ref):
    def step(j, carry):
        dst_ref[pl.ds(pl.multiple_of(j * 8, 8), 8), :] = src_ref[pl.ds(j, 8, stride=SCAN_CHUNK), :]
        return carry
    lax.fori_loop(0, SCAN_CHUNK, step, 0, unroll=4)


def _deinterleave_rows(src_ref, dst_ref):
    def step(j, carry):
        dst_ref[pl.ds(j, 8, stride=SCAN_CHUNK), :] = src_ref[pl.ds(pl.multiple_of(j * 8, 8), 8), :]
        return carry
    lax.fori_loop(0, SCAN_CHUNK, step, 0, unroll=4)


def _scan_inplace(re_ref, im_ref, a_re, a_im, reverse):
    nq = len(a_re)
    ch = SCAN_CHUNK
    ab_re = [jnp.broadcast_to(a, (8, LANES)) for a in a_re]
    ab_im = [jnp.broadcast_to(a, (8, LANES)) for a in a_im]

    def rows(j):
        jj = (ch - 1 - j) if reverse else j
        return pl.ds(pl.multiple_of(jj * 8, 8), 8)

    def sweep(init, store):
        def step(j, st):
            out = []
            r = rows(j)
            for qi in range(nq):
                xr, xi = st[2 * qi], st[2 * qi + 1]
                pr, pi = _cmul(ab_re[qi], ab_im[qi], xr, xi)
                xr = pr + re_ref[qi, r, :]
                xi = pi + im_ref[qi, r, :]
                if store:
                    re_ref[qi, r, :] = xr
                    im_ref[qi, r, :] = xi
                out += [xr, xi]
            return tuple(out)
        return lax.fori_loop(0, ch, step, tuple(init), unroll=2)

    zeros = [jnp.zeros((8, LANES), F32)] * (2 * nq)
    finals = sweep(zeros, store=False)

    row_id = lax.broadcasted_iota(jnp.int32, (8, LANES), 0)
    carries = []
    for qi in range(nq):
        pr, pi = ab_re[qi], ab_im[qi]
        for _ in range(8):
            pr, pi = _cmul(pr, pi, pr, pi)
        fr, fi = finals[2 * qi], finals[2 * qi + 1]
        sr = jnp.zeros((8, LANES), F32)
        si = jnp.zeros((8, LANES), F32)
        for _ in range(7):
            tr, ti = _cmul(pr, pi, sr, si)
            tr, ti = tr + fr, ti + fi
            if reverse:
                sr = jnp.where(row_id == 7, 0.0, pltpu.roll(tr, 7, axis=0))
                si = jnp.where(row_id == 7, 0.0, pltpu.roll(ti, 7, axis=0))
            else:
                sr = jnp.where(row_id == 0, 0.0, pltpu.roll(tr, 1, axis=0))
                si = jnp.where(row_id == 0, 0.0, pltpu.roll(ti, 1, axis=0))
        carries += [sr, si]
    sweep(carries, store=True)


SSM_Q = 4


def _ssm_fwd(u, are, aim, bb, cc, dskip, after=None):
    nq = SSM_Q
    deps = [] if after is None else [after]

    def body(u_ref, ar_ref, ai_ref, bb_ref, cc_ref, d_ref, *rest):
        y_ref, xr_ref, xi_ref, sre, sim, up, yp = rest[len(deps):]
        _interleave_rows(u_ref, up)
        uf = up[...]
        ub = uf.astype(BF16)
        yp[...] = d_ref[...] * uf
        for d in range(2):
            for qi in range(nq):
                sre[qi] = _dot(ub, bb_ref[d, qi, :, :LANES])
                sim[qi] = _dot(ub, bb_ref[d, qi, :, LANES:])
            _scan_inplace(sre, sim, [ar_ref[d, qi] for qi in range(nq)], [ai_ref[d, qi] for qi in range(nq)],
                          reverse=(d == 1))
            for qi in range(nq):
                xrb = sre[qi].astype(BF16)
                xib = sim[qi].astype(BF16)
                xr_ref[d, qi] = xrb
                xi_ref[d, qi] = xib
                yp[...] += _dot(xrb, cc_ref[d, qi, :LANES, :]) + _dot(xib, cc_ref[d, qi, LANES:, :])
        _deinterleave_rows(yp, y_ref)

    blk4 = lambda k: (0, k, 0, 0)
    return pl.pallas_call(
        body, name="ssm_fwd", grid=(SSM_WIDTH // LANES,),
        in_specs=[pl.BlockSpec((SEQ, LANES), lambda k: (0, k)),
                  pl.BlockSpec((2, nq, 1, LANES), blk4), pl.BlockSpec((2, nq, 1, LANES), blk4),
                  pl.BlockSpec((2, nq, LANES, 2 * LANES), blk4), pl.BlockSpec((2, nq, 2 * LANES, LANES), blk4),
                  pl.BlockSpec((1, LANES), lambda k: (0, k))] + [pl.BlockSpec(memory_space=pl.ANY)] * len(deps),
        out_specs=[pl.BlockSpec((SEQ, LANES), lambda k: (0, k)),
                   pl.BlockSpec((2, nq, SEQ, LANES), blk4), pl.BlockSpec((2, nq, SEQ, LANES), blk4)],
        out_shape=[jax.ShapeDtypeStruct((SEQ, SSM_WIDTH), F32),
                   jax.ShapeDtypeStruct((2, N_LANE_BLOCKS, SEQ, LANES), BF16),
                   jax.ShapeDtypeStruct((2, N_LANE_BLOCKS, SEQ, LANES), BF16)],
        scratch_shapes=[pltpu.VMEM((nq, SEQ, LANES), F32), pltpu.VMEM((nq, SEQ, LANES), F32),
                        pltpu.VMEM((SEQ, LANES), F32), pltpu.VMEM((SEQ, LANES), F32)],
        compiler_params=_cparams(("parallel",)),
    )(u, are, aim, bb, cc, dskip, *deps)


def _ssm_bwd(dy, u, xr, xi, are, aim, bb, cc, dskip, after=None):
    nq = SSM_Q
    body_rows = SEQ - 8
    deps = [] if after is None else [after]

    def body(dy_ref, u_ref, xr_ref, xi_ref, ar_ref, ai_ref, bb_ref, cc_ref, d_ref, *rest):
        du_ref, dd_ref, dcc_ref, dbb_ref, dar_ref, dai_ref, sre, sim, up, dyp, dup = rest[len(deps):]
        _interleave_rows(u_ref, up)
        _interleave_rows(dy_ref, dyp)
        dyf = dyp[...]
        uf = up[...]
        dyb = dyf.astype(BF16)
        ub = uf.astype(BF16)
        dd_ref[...] = jnp.sum(dyf * uf, axis=0, keepdims=True)
        dup[...] = d_ref[...] * dyf
        row8 = lax.broadcasted_iota(jnp.int32, (8, LANES), 0)
        for d in range(2):
            for qi in range(nq):
                dx = _dot_nt(dyb, cc_ref[d, qi])
                sre[qi] = dx[:, :LANES]
                sim[qi] = dx[:, LANES:]
                dcc_ref[d, qi] = _dot_tn(jnp.concatenate([xr_ref[d, qi], xi_ref[d, qi]], axis=1), dyb)
            _scan_inplace(sre, sim, [ar_ref[d, qi] for qi in range(nq)], [-ai_ref[d, qi] for qi in range(nq)],
                          reverse=(d == 0))
            for qi in range(nq):
                gr = sre[qi]
                gi = sim[qi]
                xrf = xr_ref[d, qi].astype(F32)
                xif = xi_ref[d, qi].astype(F32)
                if d == 0:
                    g_main_r, g_main_i = gr[8:], gi[8:]
                    x_main_r, x_main_i = xrf[:body_rows], xif[:body_rows]
                    g_edge_r, g_edge_i = gr[:8], gi[:8]
                    x_edge_r = jnp.where(row8 == 0, 0.0, pltpu.roll(xrf[body_rows:], 1, axis=0))
                    x_edge_i = jnp.where(row8 == 0, 0.0, pltpu.roll(xif[body_rows:], 1, axis=0))
                else:
                    g_main_r, g_main_i = gr[:body_rows], gi[:body_rows]
                    x_main_r, x_main_i = xrf[8:], xif[8:]
                    g_edge_r, g_edge_i = gr[body_rows:], gi[body_rows:]
                    x_edge_r = jnp.where(row8 == 7, 0.0, pltpu.roll(xrf[:8], 7, axis=0))
                    x_edge_i = jnp.where(row8 == 7, 0.0, pltpu.roll(xif[:8], 7, axis=0))
                dar_ref[d, qi] = (jnp.sum(g_main_r * x_main_r + g_main_i * x_main_i, axis=0, keepdims=True)
                                  + jnp.sum(g_edge_r * x_edge_r + g_edge_i * x_edge_i, axis=0, keepdims=True))
                dai_ref[d, qi] = (jnp.sum(g_main_i * x_main_r - g_main_r * x_main_i, axis=0, keepdims=True)
                                  + jnp.sum(g_edge_i * x_edge_r - g_edge_r * x_edge_i, axis=0, keepdims=True))
                gb = jnp.concatenate([gr, gi], axis=1).astype(BF16)
                dup[...] += _dot_nt(gb, bb_ref[d, qi])
                dbb_ref[d, qi] = _dot_tn(ub, gb)
        _deinterleave_rows(dup, du_ref)

    blk4 = lambda k: (0, k, 0, 0)
    col = lambda k: (0, k)
    bb_spec = pl.BlockSpec((2, nq, LANES, 2 * LANES), blk4)
    cc_spec = pl.BlockSpec((2, nq, 2 * LANES, LANES), blk4)
    a_spec = pl.BlockSpec((2, nq, 1, LANES), blk4)
    x_spec = pl.BlockSpec((2, nq, SEQ, LANES), blk4)
    a_shape = jax.ShapeDtypeStruct((2, N_LANE_BLOCKS, 1, LANES), F32)
    return pl.pallas_call(
        body, name="ssm_bwd", grid=(SSM_WIDTH // LANES,),
        in_specs=[pl.BlockSpec((SEQ, LANES), col), pl.BlockSpec((SEQ, LANES), col), x_spec, x_spec,
                  a_spec, a_spec, bb_spec, cc_spec, pl.BlockSpec((1, LANES), col)]
        + [pl.BlockSpec(memory_space=pl.ANY)] * len(deps),
        out_specs=[pl.BlockSpec((SEQ, LANES), col), pl.BlockSpec((1, LANES), col),
                   cc_spec, bb_spec, a_spec, a_spec],
        out_shape=[jax.ShapeDtypeStruct((SEQ, SSM_WIDTH), F32), jax.ShapeDtypeStruct((1, SSM_WIDTH), F32),
                   jax.ShapeDtypeStruct((2, N_LANE_BLOCKS, 2 * LANES, LANES), F32),
                   jax.ShapeDtypeStruct((2, N_LANE_BLOCKS, LANES, 2 * LANES), F32), a_shape, a_shape],
        scratch_shapes=[pltpu.VMEM((nq, SEQ, LANES), F32), pltpu.VMEM((nq, SEQ, LANES), F32),
                        pltpu.VMEM((SEQ, LANES), F32), pltpu.VMEM((SEQ, LANES), F32), pltpu.VMEM((SEQ, LANES), F32)],
        compiler_params=_cparams(("parallel",)),
    )(dy, u, xr, xi, are, aim, bb, cc, dskip, *deps)


GELU_C = 0.7978845608028654
GELU_K = 0.044715


def _gelu(y):
    return 0.5 * y * (1.0 + jnp.tanh(GELU_C * (y + GELU_K * y * y * y)))


def _gelu_grad(y):
    t = jnp.tanh(GELU_C * (y + GELU_K * y * y * y))
    return 0.5 * (1.0 + t) + 0.5 * y * (1.0 - t * t) * GELU_C * (1.0 + 3.0 * GELU_K * y * y)


def _mixout_fwd(o, y, glu_w, glu_b, gan, gsn, wout, x1):
    tm = MIX_TM

    def body(o_ref, y_ref, gw_ref, gb_ref, gan_ref, gsn_ref, w_ref, x1_ref, x2_ref, mx_ref):
        yg = _gelu(y_ref[...])
        z = _dot(yg.astype(BF16), gw_ref[...]) + gb_ref[...]
        so = yg * _sigmoid(z)
        na = _rms_fwd(o_ref[...], gan_ref[...])
        ns = _rms_fwd(so, gsn_ref[...])
        mixed = jnp.concatenate([na, ns], axis=-1).astype(BF16)
        mx_ref[...] = mixed
        x2_ref[...] = x1_ref[...] + _dot(mixed, w_ref[...])

    row = lambda i: (i, 0)
    const = lambda i: (0, 0)
    return pl.pallas_call(
        body, name="mixout_fwd", grid=(SEQ // tm,),
        in_specs=[pl.BlockSpec((tm, ATTN_WIDTH), row), pl.BlockSpec((tm, SSM_WIDTH), row),
                  pl.BlockSpec((SSM_WIDTH, SSM_WIDTH), const), pl.BlockSpec((1, SSM_WIDTH), const),
                  pl.BlockSpec((1, ATTN_WIDTH), const), pl.BlockSpec((1, SSM_WIDTH), const),
                  pl.BlockSpec((D_MODEL, D_MODEL), const), pl.BlockSpec((tm, D_MODEL), row)],
        out_specs=[pl.BlockSpec((tm, D_MODEL), row), pl.BlockSpec((tm, D_MODEL), row)],
        out_shape=[jax.ShapeDtypeStruct((SEQ, D_MODEL), F32), jax.ShapeDtypeStruct((SEQ, D_MODEL), BF16)],
        compiler_params=_cparams(("parallel",)),
    )(o, y, glu_w, glu_b, gan, gsn, wout, x1)


def _mixout_bwd(dx2, o, y, glu_w, glu_b, gan, gsn, wout):
    tm = MIX_TM

    def body(dx2_ref, o_ref, y_ref, gw_ref, gb_ref, gan_ref, gsn_ref, w_ref,
             do_ref, dy_ref, dz_ref, yg_ref, dxb_ref, dgan_ref, dgsn_ref, dgb_ref):
        i = pl.program_id(0)
        dxb = dx2_ref[...].astype(BF16)
        dxb_ref[...] = dxb
        dmixed = _dot_nt(dxb, w_ref[...])
        do, dgan = _rms_bwd(dmixed[:, :ATTN_WIDTH], o_ref[...], gan_ref[...])
        do_ref[...] = do.T
        yv = y_ref[...]
        yg = _gelu(yv)
        ygb = yg.astype(BF16)
        yg_ref[...] = ygb
        sg = _sigmoid(_dot(ygb, gw_ref[...]) + gb_ref[...])
        dso, dgsn = _rms_bwd(dmixed[:, ATTN_WIDTH:], yg * sg, gsn_ref[...])
        dz = dso * yg * sg * (1.0 - sg)
        dzb = dz.astype(BF16)
        dz_ref[...] = dzb
        dyg = dso * sg + _dot_nt(dzb, gw_ref[...])
        dy_ref[...] = dyg * _gelu_grad(yv)
        dgb = jnp.sum(dz, axis=0, keepdims=True)

        @pl.when(i == 0)
        def _():
            dgan_ref[...] = dgan
            dgsn_ref[...] = dgsn
            dgb_ref[...] = dgb

        @pl.when(i != 0)
        def _():
            dgan_ref[...] += dgan
            dgsn_ref[...] += dgsn
            dgb_ref[...] += dgb

    row = lambda i: (i, 0)
    const = lambda i: (0, 0)
    return pl.pallas_call(
        body, name="mixout_bwd", grid=(SEQ // tm,),
        in_specs=[pl.BlockSpec((tm, D_MODEL), row), pl.BlockSpec((tm, ATTN_WIDTH), row),
                  pl.BlockSpec((tm, SSM_WIDTH), row),
                  pl.BlockSpec((SSM_WIDTH, SSM_WIDTH), const), pl.BlockSpec((1, SSM_WIDTH), const),
                  pl.BlockSpec((1, ATTN_WIDTH), const), pl.BlockSpec((1, SSM_WIDTH), const),
                  pl.BlockSpec((D_MODEL, D_MODEL), const)],
        out_specs=[pl.BlockSpec((ATTN_WIDTH, tm), lambda i: (0, i)), pl.BlockSpec((tm, SSM_WIDTH), row),
                   pl.BlockSpec((tm, SSM_WIDTH), row), pl.BlockSpec((tm, SSM_WIDTH), row),
                   pl.BlockSpec((tm, D_MODEL), row),
                   pl.BlockSpec((1, ATTN_WIDTH), const), pl.BlockSpec((1, SSM_WIDTH), const),
                   pl.BlockSpec((1, SSM_WIDTH), const)],
        out_shape=[jax.ShapeDtypeStruct((ATTN_WIDTH, SEQ), F32), jax.ShapeDtypeStruct((SEQ, SSM_WIDTH), F32),
                   jax.ShapeDtypeStruct((SEQ, SSM_WIDTH), BF16), jax.ShapeDtypeStruct((SEQ, SSM_WIDTH), BF16),
                   jax.ShapeDtypeStruct((SEQ, D_MODEL), BF16),
                   jax.ShapeDtypeStruct((1, ATTN_WIDTH), F32), jax.ShapeDtypeStruct((1, SSM_WIDTH), F32),
                   jax.ShapeDtypeStruct((1, SSM_WIDTH), F32)],
        compiler_params=_cparams(("arbitrary",)),
    )(dx2, o, y, glu_w, glu_b, gan, gsn, wout)


def _local_step(x, target, w, p, late_weights, early_grads, after=None, midway=None):
    x1, h1, a1, b1 = _ffn_fwd(x, p["norm_ffn1"], w["wgt1"], w["wut1"], w["wd1"], "ffn1_fwd", after=after)
    h2, q, k, v, u = _mixin_fwd(x1, p["norm_mix"], w["wint"])
    o = _attn_fwd(q, k, v, p["attn_sinks"])

    lam_re = p["ssm_lambda_re"].reshape(2 * N_LANE_BLOCKS, LANES)
    lam_im = p["ssm_lambda_im"].reshape(2 * N_LANE_BLOCKS, LANES)
    log_dt = jnp.repeat(p["ssm_log_dt"].reshape(2, 32), 64, axis=-1).reshape(2 * N_LANE_BLOCKS, LANES)
    a_re, a_im, bb, cc = _ssm_prep(lam_re, lam_im, log_dt, p["ssm_b_re"], p["ssm_b_im"],
                                   p["ssm_c_re"], p["ssm_c_im"])
    shape_a = (2, N_LANE_BLOCKS, 1, LANES)
    a_re4, a_im4 = a_re.reshape(shape_a), a_im.reshape(shape_a)
    bb4 = bb.reshape(2, N_LANE_BLOCKS, LANES, 2 * LANES)
    cc4 = cc.reshape(2, N_LANE_BLOCKS, 2 * LANES, LANES)
    dskip = p["ssm_d"].T.reshape(1, SSM_WIDTH)
    y, xr, xi = _ssm_fwd(u, a_re4, a_im4, bb4, cc4, dskip, after=None if midway is None else midway(o))

    w2 = late_weights(y)
    x2, mixed = _mixout_fwd(o, y, w2["glu"], p["ssm_glu_b"], p["attn_out_norm"], p["ssm_out_norm"], w2["wout"], x1)
    dx3, h3, a3, b3, loss, d_final = _ffn_fwd(x2, p["norm_ffn2"], w2["wgt2"], w2["wut2"], w2["wd2"], "ffn2_fwd",
                                              head=(p["final_norm"], target))
    dx2, da3, db3, s3, df3, d_n2 = _ffn_bwd_act(dx3, x2, p["norm_ffn2"], a3, b3, w2["wgt2"], w2["wut2"], w2["wd2"],
                                                "ffn2_bwd_act")
    g_wgt2, g_wut2, g_wd2 = _mm_tn([(da3, h3), (db3, h3), (s3, df3)], "ffn2_bwd_w")

    do, dy, dz, ygb, dx2b, d_gan, d_gsn, d_glub = _mixout_bwd(
        dx2, o, y, w2["glu"], p["ssm_glu_b"], p["attn_out_norm"], p["ssm_out_norm"], w2["wout"])
    (g_wout,) = _mm_tn([(mixed, dx2b)], "wout_bwd_w")
    (g_glu,) = _mm_tn([(ygb, dz)], "glu_bwd_w")
    sent = early_grads(dict(glu=g_glu, wout=g_wout, wgt2=g_wgt2, wut2=g_wut2, wd2=g_wd2))

    du, d_dskip, dcc, dbb, dar, dai = _ssm_bwd(dy, u, xr, xi, a_re4, a_im4, bb4, cc4, dskip, after=sent)
    nb = 2 * N_LANE_BLOCKS
    g_lre, g_lim, g_ldt, g_btr, g_bti, g_cre, g_cim = _ssm_prep_bwd(
        lam_re, lam_im, log_dt, p["ssm_b_re"], p["ssm_b_im"], dar.reshape(nb, LANES), dai.reshape(nb, LANES),
        dbb.reshape(nb, LANES, 2 * LANES), dcc.reshape(nb, 2 * LANES, LANES))

    dq, dk, dv, d_sinks = _attn_bwd(q, k, v, p["attn_sinks"], do)
    dx1, dproj, d_nmix = _mixin_bwd(dq, dk, dv, du, w["wint"], x1, p["norm_mix"], dx2)
    (g_wint,) = _mm_tn([(dproj, h2)], "win_bwd_w")

    dx0, da1, db1, s1, df1, d_n1 = _ffn_bwd_act(dx1, x, p["norm_ffn1"], a1, b1, w["wgt1"], w["wut1"], w["wd1"],
                                                "ffn1_bwd_act")
    g_wgt1, g_wut1, g_wd1 = _mm_tn([(da1, h1), (db1, h1), (s1, df1)], "ffn1_bwd_w")

    big = dict(wgt1=g_wgt1, wut1=g_wut1, wd1=g_wd1, wint=g_wint)
    small = dict(
        norm_ffn1=d_n1, norm_mix=d_nmix, attn_sinks=d_sinks,
        ssm_lambda_re=g_lre.reshape(64, 64), ssm_lambda_im=g_lim.reshape(64, 64),
        ssm_log_dt=g_ldt.reshape(2, 32), ssm_b_re=g_btr, ssm_b_im=g_bti, ssm_c_re=g_cre, ssm_c_im=g_cim,
        ssm_d=d_dskip.reshape(32, 16).T, ssm_glu_b=d_glub, attn_out_norm=d_gan, ssm_out_norm=d_gsn,
        norm_ffn2=d_n2, final_norm=d_final, loss=loss)
    return loss, dx0, big, small


BIG = dict(
    wgt1=("ffn1_w_gate", 352, 1024, True), wut1=("ffn1_w_up", 352, 1024, True), wd1=("ffn1_w_down", 352, 1024, False),
    wint=("w_in", 160, 1024, True), glu=("ssm_glu_w", 64, 512, False), wout=("w_out", 128, 1024, False),
    wgt2=("ffn2_w_gate", 352, 1024, True), wut2=("ffn2_w_up", 352, 1024, True), wd2=("ffn2_w_down", 352, 1024, False))

SMALL = dict(
    norm_ffn1=(1, 1024), norm_mix=(1, 1024), attn_sinks=(1, 8), ssm_lambda_re=(64, 64), ssm_lambda_im=(64, 64),
    ssm_log_dt=(2, 32), ssm_b_re=(1024, 64), ssm_b_im=(1024, 64), ssm_c_re=(1024, 64), ssm_c_im=(1024, 64),
    ssm_d=(16, 32), ssm_glu_b=(1, 512), attn_out_norm=(1, 512), ssm_out_norm=(1, 512), norm_ffn2=(1, 1024),
    final_norm=(1, 1024), loss=(1, 128))
SMALL_TRANSPOSED = ("ssm_b_re", "ssm_b_im", "ssm_d")
SMALL_PARAMS = tuple(n for n in SMALL if n != "loss")

SMALL_PAIRS = (("ssm_lambda_re", "ssm_lambda_im"), ("ssm_c_re", "ssm_c_im"), ("ssm_b_re", "ssm_b_im"))
SMALL_VECS = ("norm_ffn1", "norm_mix", "norm_ffn2", "final_norm", "ssm_glu_b", "attn_out_norm", "ssm_out_norm")
SMALL_TILES = ("ssm_log_dt", "attn_sinks", "ssm_d", "loss")


def _small_offsets():
    off, table = 0, {}
    for re, im in SMALL_PAIRS:
        table[re] = table[im] = off
        off += SMALL[re][0]
    for n in SMALL_VECS:
        table[n] = off
        off += SMALL[n][1] // LANES
    for n in SMALL_TILES:
        off = -(-off // 8) * 8
        table[n] = off
        off += SMALL[n][0]
    return table, off


SMALL_OFFSET, SMALL_USED_ROWS = _small_offsets()
SMALL_ROWS = -(-SMALL_USED_ROWS // (8 * N_DEV)) * 8 * N_DEV


def _cast_shards(shards):
    names = list(BIG)

    def body(*refs):
        ins, outs = refs[:len(names)], refs[len(names):]
        for idx in range(len(names)):
            outs[idx][...] = ins[idx][...].astype(BF16)

    return pl.pallas_call(
        body, name="cast_shards",
        out_shape=[jax.ShapeDtypeStruct((BIG[n][1], BIG[n][2]), BF16) for n in names],
        compiler_params=_cparams(),
    )(*[shards[n] for n in names])


def _peer(x, y, c, r):
    px = 1 - x if r & 4 else x
    py = 1 - y if r & 2 else y
    pc = 1 - c if r & 1 else c
    return px, py, pc


FIRST_GROUP = ("wgt1", "wut1", "wd1", "wint")
LATE_GROUP = ("glu", "wout", "wgt2", "wut2", "wd2")
N_PEERS = N_DEV - 1
ANY_SPEC = pl.BlockSpec(memory_space=pl.ANY)
HBM_SPEC = pl.BlockSpec(memory_space=pltpu.HBM)
SEM_SPEC = pl.BlockSpec(memory_space=pltpu.SEMAPHORE)
DATAFLOW_EFFECT = pltpu.SideEffectType.DATAFLOW_SIDE_EFFECTING


def _mesh_pos():
    x, y, c = lax.axis_index("x"), lax.axis_index("y"), lax.axis_index("c")
    return x, y, c, 4 * x + 2 * y + c


def _gather_first(first, late):
    nf, nl = len(first), len(late)

    def body(*refs):
        f_in, l_in = refs[:nf], refs[nf:nf + nl]
        f_out, l_out = refs[nf + nl:2 * nf + nl], refs[2 * nf + nl:2 * (nf + nl)]
        send_sems, recv_sems, local_sems = refs[2 * (nf + nl):]
        x, y, c, me = _mesh_pos()
        sibling = (x, y, 1 - c)
        chips = [(x, 1 - y), (1 - x, y), (1 - x, 1 - y)]

        def idx(px, py, pc):
            return 4 * px + 2 * py + pc

        def copy(k, s, block, to, src=None):
            slot = f_out[k].at[block]
            return pltpu.make_async_remote_copy(
                src_ref=slot if src is None else src, dst_ref=slot, send_sem=send_sems.at[k, s],
                recv_sem=recv_sems.at[k, s], device_id=to, device_id_type=MESH_ID)

        local = []
        for k in range(nf + nl):
            src, dst = (f_in[k], f_out[k]) if k < nf else (l_in[k - nf], l_out[k - nf])
            mine = pltpu.make_async_copy(src, dst.at[me], local_sems.at[k])
            mine.start()
            local.append(mine)
        sends = []
        for j, chip in enumerate(chips):
            for k in range(nf):
                sends.append(copy(k, 1 + j, me, (*chip, c), src=f_in[k]))
                sends[-1].start()
        for k in range(nf):
            sends.append(copy(k, 0, me, sibling, src=f_in[k]))
            sends[-1].start()
        for j, chip in enumerate(chips):
            for k in range(nf):
                copy(k, 1 + j, idx(*chip, c), (*chip, c)).wait_recv()
                sends.append(copy(k, 4 + j, idx(*chip, c), sibling))
                sends[-1].start()
        for k in range(nf):
            copy(k, 0, idx(*sibling), sibling).wait_recv()
        for j, chip in enumerate(chips):
            for k in range(nf):
                copy(k, 4 + j, idx(*chip, 1 - c), sibling).wait_recv()
        for cp in sends:
            cp.wait_send()
        for cp in local:
            cp.wait()

    return pl.pallas_call(
        body, name="gather_first",
        in_specs=[ANY_SPEC] * (nf + nl), out_specs=[ANY_SPEC] * (nf + nl),
        out_shape=[jax.ShapeDtypeStruct((N_DEV,) + s.shape, s.dtype) for s in list(first) + list(late)],
        scratch_shapes=[pltpu.SemaphoreType.DMA((nf, N_PEERS)), pltpu.SemaphoreType.DMA((nf, N_PEERS)),
                        pltpu.SemaphoreType.DMA((nf + nl,))],
        compiler_params=pltpu.CompilerParams(has_side_effects=True),
    )(*first, *late)


def _split_copy(src_refs, land_refs, send_sems, recv_sems, k, r, pos, scatter, receiving):
    x, y, c, me = pos
    px, py, pc = _peer(x, y, c, r)
    peer_idx = 4 * px + 2 * py + pc
    if scatter:
        src, dst = src_refs[k].at[peer_idx], land_refs[k].at[r - 1]
    else:
        src, dst = src_refs[k], land_refs[k].at[peer_idx if receiving else me]
    return pltpu.make_async_remote_copy(
        src_ref=src, dst_ref=dst, send_sem=send_sems.at[k * N_PEERS + r - 1],
        recv_sem=recv_sems.at[k * N_PEERS + r - 1], device_id=(px, py, pc), device_id_type=MESH_ID)


def _split_start(name, srcs, lands, scatter):
    n = len(srcs)

    def body(*refs):
        src_refs, land_refs = refs[:n], refs[n:2 * n]
        send_sems, recv_sems = refs[2 * n], refs[2 * n + 1]
        token = refs[-1]
        pos = _mesh_pos()
        for k in range(n):
            for r in range(1, N_DEV):
                _split_copy(src_refs, land_refs, send_sems, recv_sems, k, r, pos, scatter, False).start()
        token[...] = jnp.zeros_like(token)

    thru = [pltpu.HBM(a.shape, a.dtype) for a in list(srcs) + list(lands)]
    outs = pl.pallas_call(
        body, name=name,
        in_specs=[HBM_SPEC] * (2 * n),
        out_specs=[SEM_SPEC, SEM_SPEC] + [HBM_SPEC] * (2 * n) + [pl.BlockSpec(memory_space=pltpu.VMEM)],
        out_shape=[pltpu.SemaphoreType.DMA((n * N_PEERS,)), pltpu.SemaphoreType.DMA((n * N_PEERS,))] + thru
        + [jax.ShapeDtypeStruct((8, LANES), F32)],
        input_output_aliases={i: 2 + i for i in range(2 * n)},
        compiler_params=pltpu.CompilerParams(has_side_effects=DATAFLOW_EFFECT),
    )(*[pltpu.with_memory_space_constraint(a, pltpu.HBM) for a in list(srcs) + list(lands)])
    return outs[0], outs[1], outs[2:2 + n], outs[2 + n:2 + 2 * n], outs[-1]


def _split_wait(name, send_sems, recv_sems, srcs, lands, scatter, after):
    n = len(srcs)

    def body(*refs):
        src_refs, land_refs = refs[:n], refs[n:2 * n]
        send, recv = refs[2 * n], refs[2 * n + 1]
        pos = _mesh_pos()
        for k in range(n):
            for r in range(1, N_DEV):
                cp = _split_copy(src_refs, land_refs, send, recv, k, r, pos, scatter, True)
                cp.wait_send()
                cp.wait_recv()

    thru = [pltpu.HBM(a.shape, a.dtype) for a in list(srcs) + list(lands)]
    outs = pl.pallas_call(
        body, name=name,
        in_specs=[HBM_SPEC] * (2 * n) + [SEM_SPEC, SEM_SPEC, ANY_SPEC],
        out_specs=[HBM_SPEC] * (2 * n), out_shape=thru,
        input_output_aliases={i: i for i in range(2 * n)},
        compiler_params=pltpu.CompilerParams(has_side_effects=DATAFLOW_EFFECT),
    )(*srcs, *lands, send_sems, recv_sems, after)
    return outs[:n], outs[n:]


def _late_copy(passing, src_refs, land_refs, send_sems, recv_sems, k, s, pos, receiving):
    x, y, c, me = pos
    chips = [(x, 1 - y), (1 - x, y), (1 - x, 1 - y)]
    sibling = (x, y, 1 - c)

    def idx(dev):
        return 4 * dev[0] + 2 * dev[1] + dev[2]

    if passing:
        to = sibling
        block = idx((*chips[s], 1 - c)) if receiving else idx((*chips[s], c))
        src = dst = land_refs[k].at[block]
        sem = k * 3 + s
    else:
        to = sibling if s == 0 else (*chips[s - 1], c)
        src, dst = src_refs[k], land_refs[k].at[idx(to) if receiving else me]
        sem = k * 4 + s
    return pltpu.make_async_remote_copy(src_ref=src, dst_ref=dst, send_sem=send_sems.at[sem],
                                        recv_sem=recv_sems.at[sem], device_id=to, device_id_type=MESH_ID)


def _late_gather_call(name, stage, srcs, lands, sems, after=None):
    n = len(srcs)
    n_sem_in = len(sems)
    has_after = after is not None

    def body(*refs):
        src_refs, land_refs = refs[:n], refs[n:2 * n]
        sem_in = refs[2 * n:2 * n + n_sem_in]
        outs = refs[2 * n + n_sem_in + (1 if has_after else 0):]
        pos = _mesh_pos()
        if stage == 0:
            own_send, own_recv = outs[0], outs[1]
            for s in (1, 2, 3, 0):
                for k in range(n):
                    _late_copy(False, src_refs, land_refs, own_send, own_recv, k, s, pos, False).start()
            outs[-1][...] = jnp.zeros_like(outs[-1])
        elif stage == 1:
            own_recv = sem_in[1]
            pass_send, pass_recv = outs[0], outs[1]
            for s in range(3):
                for k in range(n):
                    _late_copy(False, src_refs, land_refs, sem_in[0], own_recv, k, s + 1, pos, True).wait_recv()
                    _late_copy(True, src_refs, land_refs, pass_send, pass_recv, k, s, pos, False).start()
            outs[-1][...] = jnp.zeros_like(outs[-1])
        else:
            own_send, own_recv, pass_send, pass_recv = sem_in
            for k in range(n):
                _late_copy(False, src_refs, land_refs, own_send, own_recv, k, 0, pos, True).wait_recv()
                for s in range(4):
                    _late_copy(False, src_refs, land_refs, own_send, own_recv, k, s, pos, False).wait_send()
                for s in range(3):
                    cp = _late_copy(True, src_refs, land_refs, pass_send, pass_recv, k, s, pos, True)
                    cp.wait_recv()
                    cp.wait_send()

    thru = [pltpu.HBM(a.shape, a.dtype) for a in list(srcs) + list(lands)]
    new_sems = [[pltpu.SemaphoreType.DMA((n * 4,))] * 2, [pltpu.SemaphoreType.DMA((n * 3,))] * 2, []][stage]
    extra = [] if stage == 2 else [jax.ShapeDtypeStruct((8, LANES), F32)]
    outs = pl.pallas_call(
        body, name=name,
        in_specs=[HBM_SPEC] * (2 * n) + [SEM_SPEC] * n_sem_in + [ANY_SPEC] * has_after,
        out_specs=[SEM_SPEC] * len(new_sems) + [HBM_SPEC] * (2 * n) + [pl.BlockSpec(memory_space=pltpu.VMEM)] * len(extra),
        out_shape=new_sems + thru + extra,
        input_output_aliases={i: len(new_sems) + i for i in range(2 * n)},
        compiler_params=pltpu.CompilerParams(has_side_effects=DATAFLOW_EFFECT),
    )(*[pltpu.with_memory_space_constraint(a, pltpu.HBM) for a in list(srcs) + list(lands)], *sems,
      *([after] if has_after else []))
    ns = len(new_sems)
    return list(outs[:ns]), outs[ns:ns + n], outs[ns + n:ns + 2 * n], (outs[-1] if extra else None)


N_SEND_SLOTS = 3


def _exchange_last(grads, small_packed):
    ng = len(grads)
    ch = SMALL_ROWS // N_DEV
    max_rows = max(g.shape[1] for g in grads)
    cols = grads[0].shape[2]

    def body(*refs):
        g_in, s_in = refs[:ng], refs[ng]
        outs = refs[ng + 1:]
        own_out, land, stage = outs[:ng], outs[ng:2 * ng], outs[2 * ng:3 * ng]
        s_red, s_stage = outs[3 * ng], outs[3 * ng + 1]
        (va, vb, vo, vs, sm_in, sm_out, d2d_send, d2d_recv, ici_send, ici_recv, s1_send, s1_recv, s2_send, s2_recv,
         local_sems) = outs[3 * ng + 2:]
        x, y, c, me = _mesh_pos()
        sibling = (x, y, 1 - c)
        chips = [(x, y), (x, 1 - y), (1 - x, y), (1 - x, 1 - y)]

        def idx(chip, core):
            return 4 * chip[0] + 2 * chip[1] + core

        def d2d(k, j):
            return pltpu.make_async_remote_copy(
                src_ref=g_in[k].at[idx(chips[j], 1 - c)], dst_ref=stage[k].at[j], send_sem=d2d_send.at[k, j],
                recv_sem=d2d_recv.at[k, j], device_id=sibling, device_id_type=MESH_ID)

        def ici(k, j, slot):
            rows = g_in[k].shape[1]
            return pltpu.make_async_remote_copy(
                src_ref=vo.at[slot, pl.ds(0, rows)], dst_ref=land[k].at[j - 1], send_sem=ici_send.at[k, j - 1],
                recv_sem=ici_recv.at[k, j - 1], device_id=(*chips[j], c), device_id_type=MESH_ID)

        def small_scatter(r):
            px, py, pc = _peer(x, y, c, r)
            return pltpu.make_async_remote_copy(
                src_ref=s_in.at[pl.ds(pl.multiple_of((4 * px + 2 * py + pc) * ch, 8), ch)], dst_ref=s_stage.at[me],
                send_sem=s1_send.at[r - 1], recv_sem=s1_recv.at[r - 1], device_id=(px, py, pc), device_id_type=MESH_ID)

        def small_gather(r):
            return pltpu.make_async_remote_copy(
                src_ref=sm_out, dst_ref=s_red.at[me], send_sem=s2_send.at[r - 1], recv_sem=s2_recv.at[r - 1],
                device_id=_peer(x, y, c, r), device_id_type=MESH_ID)

        for r in range(1, N_DEV):
            small_scatter(r).start()
        mine = pltpu.make_async_copy(s_in.at[pl.ds(pl.multiple_of(me * ch, 8), ch)], s_stage.at[me], local_sems.at[0])
        mine.start()
        pairs = [(k, j) for k in range(ng) for j in (1, 2, 3)] + [(k, 0) for k in range(ng)]
        for k, j in pairs:
            d2d(k, j).start()

        for r in range(1, N_DEV):
            small_scatter(r).wait_recv()
        mine.wait()
        load = pltpu.make_async_copy(s_stage, sm_in, local_sems.at[1])
        load.start()
        load.wait()
        total = sm_in[0]
        for i in range(1, N_DEV):
            total = total + sm_in[i]
        sm_out[...] = total
        for r in range(1, N_DEV):
            small_gather(r).start()
        keep = pltpu.make_async_copy(sm_out, s_red.at[me], local_sems.at[2])
        keep.start()

        in_flight = {}
        for i, (k, j) in enumerate(pairs):
            slot = i % N_SEND_SLOTS
            rows = g_in[k].shape[1]
            if slot in in_flight:
                in_flight.pop(slot).wait_send()
            d2d(k, j).wait_recv()
            la = pltpu.make_async_copy(g_in[k].at[idx(chips[j], c)], va.at[pl.ds(0, rows)], local_sems.at[3])
            lb = pltpu.make_async_copy(stage[k].at[j], vb.at[pl.ds(0, rows)], local_sems.at[4])
            la.start()
            lb.start()
            la.wait()
            lb.wait()
            total = va[pl.ds(0, rows)].astype(F32) + vb[pl.ds(0, rows)].astype(F32)
            if j == 0:
                vs[pl.ds(0, rows)] = total
                st = pltpu.make_async_copy(vs.at[pl.ds(0, rows)], own_out[k], local_sems.at[5])
                st.start()
                st.wait()
            else:
                vo[slot, pl.ds(0, rows)] = total.astype(BF16)
                cp = ici(k, j, slot)
                cp.start()
                in_flight[slot] = cp
        for cp in in_flight.values():
            cp.wait_send()

        for j in (1, 2, 3, 0):
            for k in range(ng):
                d2d(k, j).wait_send()
        for j in (1, 2, 3):
            for k in range(ng):
                ici(k, j, 0).wait_recv()
        for r in range(1, N_DEV):
            small_scatter(r).wait_send()
            small_gather(r).wait_send()
            small_gather(r).wait_recv()
        keep.wait()

    out_shape = [jax.ShapeDtypeStruct(g.shape[1:], F32) for g in grads]
    out_shape += [jax.ShapeDtypeStruct((3,) + g.shape[1:], BF16) for g in grads]
    out_shape += [jax.ShapeDtypeStruct((4,) + g.shape[1:], BF16) for g in grads]
    out_shape += [jax.ShapeDtypeStruct((N_DEV, ch, LANES), F32), jax.ShapeDtypeStruct((N_DEV, ch, LANES), F32)]
    outs = pl.pallas_call(
        body, name="exchange_last",
        in_specs=[ANY_SPEC] * (ng + 1), out_specs=[ANY_SPEC] * len(out_shape), out_shape=out_shape,
        scratch_shapes=[pltpu.VMEM((max_rows, cols), BF16), pltpu.VMEM((max_rows, cols), BF16),
                        pltpu.VMEM((N_SEND_SLOTS, max_rows, cols), BF16), pltpu.VMEM((max_rows, cols), F32),
                        pltpu.VMEM((N_DEV, ch, LANES), F32), pltpu.VMEM((ch, LANES), F32),
                        pltpu.SemaphoreType.DMA((ng, 4)), pltpu.SemaphoreType.DMA((ng, 4)),
                        pltpu.SemaphoreType.DMA((ng, 3)), pltpu.SemaphoreType.DMA((ng, 3)),
                        pltpu.SemaphoreType.DMA((N_PEERS,)), pltpu.SemaphoreType.DMA((N_PEERS,)),
                        pltpu.SemaphoreType.DMA((N_PEERS,)), pltpu.SemaphoreType.DMA((N_PEERS,)),
                        pltpu.SemaphoreType.DMA((6,))],
        compiler_params=pltpu.CompilerParams(has_side_effects=True, vmem_limit_bytes=VMEM_LIMIT),
    )(*grads, small_packed)
    return outs[:ng], outs[ng:2 * ng], outs[3 * ng].reshape(SMALL_ROWS, LANES)


def _adamw_math(w, g, m, v):
    m2 = ADAM_B1 * m + (1.0 - ADAM_B1) * g
    v2 = ADAM_B2 * v + (1.0 - ADAM_B2) * (g * g)
    m_hat = m2 / (1.0 - ADAM_B1 ** ADAM_STEP)
    v_hat = v2 / (1.0 - ADAM_B2 ** ADAM_STEP)
    delta = -ADAM_LR * (m_hat / (jnp.sqrt(v_hat) + ADAM_EPS) + ADAM_WD * w)
    return delta, m2, v2


ADAM_ROW_TILES = 2


def _adamw_big(own, parts, w, m, v, name):
    shape = w.shape
    own_is_blocks = own.ndim == 3
    tr = shape[0] // ADAM_ROW_TILES
    n_parts = parts.shape[0]

    def body(own_ref, p_ref, w_ref, m_ref, v_ref, g_ref, d_ref, m2_ref, v2_ref, own_s, sem):
        rows = pl.ds(pl.multiple_of(pl.program_id(0) * tr, 16), tr)
        if own_is_blocks:
            cp = pltpu.make_async_copy(own_ref.at[_mesh_pos()[3], rows], own_s, sem)
        else:
            cp = pltpu.make_async_copy(own_ref.at[rows], own_s, sem)
        cp.start()
        cp.wait()
        g = own_s[...].astype(F32)
        for i in range(n_parts):
            g = g + p_ref[i].astype(F32)
        delta, m2, v2 = _adamw_math(w_ref[...], g, m_ref[...], v_ref[...])
        g_ref[...] = g
        d_ref[...] = delta
        m2_ref[...] = m2
        v2_ref[...] = v2

    tile = pl.BlockSpec((tr, shape[1]), lambda i: (i, 0))
    return pl.pallas_call(
        body, name=name, grid=(ADAM_ROW_TILES,),
        in_specs=[ANY_SPEC, pl.BlockSpec((n_parts, tr, shape[1]), lambda i: (0, i, 0)), tile, tile, tile],
        out_specs=[tile] * 4, out_shape=[jax.ShapeDtypeStruct(shape, F32)] * 4,
        scratch_shapes=[pltpu.VMEM((tr, shape[1]), own.dtype), pltpu.SemaphoreType.DMA(())],
        compiler_params=_cparams(("arbitrary",)),
    )(own, parts, w, m, v)


def _pack_small(grads):
    names = list(SMALL)

    def body(*refs):
        ins, out = dict(zip(names, refs[:-1])), refs[-1]
        out[...] = jnp.zeros_like(out)
        for re, im in SMALL_PAIRS:
            off, rows = SMALL_OFFSET[re], SMALL[re][0]
            out[off:off + rows, :] = jnp.concatenate([ins[re][...], ins[im][...]], axis=1)
        for n in SMALL_VECS:
            off, vec = SMALL_OFFSET[n], ins[n][...]
            for i in range(SMALL[n][1] // LANES):
                out[off + i:off + i + 1, :] = vec[:, i * LANES:(i + 1) * LANES]
        for n in SMALL_TILES:
            off, (rows, cols) = SMALL_OFFSET[n], SMALL[n]
            out[off:off + rows, 0:cols] = ins[n][...]

    return pl.pallas_call(
        body, name="pack_small", out_shape=jax.ShapeDtypeStruct((SMALL_ROWS, LANES), F32),
        compiler_params=_cparams(),
    )(*[grads[n] for n in names])


def _unpack_small_ref(g_ref, n):
    off, (rows, cols) = SMALL_OFFSET[n], SMALL[n]
    for re, im in SMALL_PAIRS:
        if n == re:
            return g_ref[off:off + rows, 0:HALF_LANES]
        if n == im:
            return g_ref[off:off + rows, HALF_LANES:LANES]
    if n in SMALL_VECS:
        return jnp.concatenate([g_ref[off + i:off + i + 1, :] for i in range(cols // LANES)], axis=1)
    return g_ref[off:off + rows, 0:cols]


def _adamw_small(g_packed, w, m, v):
    names = list(SMALL_PARAMS)
    n = len(names)

    def body(g_ref, *refs):
        w_refs, m_refs, v_refs, outs = refs[:n], refs[n:2 * n], refs[2 * n:3 * n], refs[3 * n:]
        for idx, name in enumerate(names):
            g = _unpack_small_ref(g_ref, name)
            delta, m2, v2 = _adamw_math(w_refs[idx][...], g, m_refs[idx][...], v_refs[idx][...])
            outs[4 * idx][...] = g
            outs[4 * idx + 1][...] = delta
            outs[4 * idx + 2][...] = m2
            outs[4 * idx + 3][...] = v2
        outs[4 * n][...] = _unpack_small_ref(g_ref, "loss")

    outs = pl.pallas_call(
        body, name="adamw_small",
        out_shape=[jax.ShapeDtypeStruct(SMALL[name], F32) for name in names for _ in range(4)]
        + [jax.ShapeDtypeStruct(SMALL["loss"], F32)],
        compiler_params=_cparams(),
    )(g_packed, *[w[k] for k in names], *[m[k] for k in names], *[v[k] for k in names])
    return {name: outs[4 * idx:4 * idx + 4] for idx, name in enumerate(names)}, outs[4 * n]


WEIGHT_NAMES = ['norm_ffn1', 'ffn1_w_gate', 'ffn1_w_up', 'ffn1_w_down', 'norm_mix', 'w_in', 'attn_sinks',
                'ssm_lambda_re', 'ssm_lambda_im', 'ssm_log_dt', 'ssm_b_re', 'ssm_b_im', 'ssm_c_re', 'ssm_c_im',
                'ssm_d', 'ssm_glu_w', 'ssm_glu_b', 'attn_out_norm', 'ssm_out_norm', 'w_out', 'norm_ffn2',
                'ffn2_w_gate', 'ffn2_w_up', 'ffn2_w_down', 'final_norm']


def kernel(x, norm_ffn1, ffn1_w_gate, ffn1_w_up, ffn1_w_down, norm_mix, w_in, attn_sinks, ssm_lambda_re, ssm_lambda_im, ssm_log_dt, ssm_b_re, ssm_b_im, ssm_c_re, ssm_c_im, ssm_d, ssm_glu_w, ssm_glu_b, attn_out_norm, ssm_out_norm, w_out, norm_ffn2, ffn2_w_gate, ffn2_w_up, ffn2_w_down, final_norm, loss_target, m_norm_ffn1, m_ffn1_w_gate, m_ffn1_w_up, m_ffn1_w_down, m_norm_mix, m_w_in, m_attn_sinks, m_ssm_lambda_re, m_ssm_lambda_im, m_ssm_log_dt, m_ssm_b_re, m_ssm_b_im, m_ssm_c_re, m_ssm_c_im, m_ssm_d, m_ssm_glu_w, m_ssm_glu_b, m_attn_out_norm, m_ssm_out_norm, m_w_out, m_norm_ffn2, m_ffn2_w_gate, m_ffn2_w_up, m_ffn2_w_down, m_final_norm, v_norm_ffn1, v_ffn1_w_gate, v_ffn1_w_up, v_ffn1_w_down, v_norm_mix, v_w_in, v_attn_sinks, v_ssm_lambda_re, v_ssm_lambda_im, v_ssm_log_dt, v_ssm_b_re, v_ssm_b_im, v_ssm_c_re, v_ssm_c_im, v_ssm_d, v_ssm_glu_w, v_ssm_glu_b, v_attn_out_norm, v_ssm_out_norm, v_w_out, v_norm_ffn2, v_ffn2_w_gate, v_ffn2_w_up, v_ffn2_w_down, v_final_norm):
    args = dict(locals())
    weights = {n: args[n] for n in WEIGHT_NAMES}
    moms = {n: args["m_" + n] for n in WEIGHT_NAMES}
    vars_ = {n: args["v_" + n] for n in WEIGHT_NAMES}

    def shard2d(a, k):
        a = a.reshape(a.shape[-2], a.shape[-1])
        return a.T if BIG[k][3] else a

    def shard_master(a, k):
        return (a.T if BIG[k][3] else a).reshape(weights[BIG[k][0]].shape)

    def blocks(g, k):
        return g.reshape(N_DEV, BIG[k][1], BIG[k][2])

    def full(g, k):
        return g.reshape(N_DEV * BIG[k][1], BIG[k][2])

    shards = dict(zip(BIG, _cast_shards({k: shard2d(weights[BIG[k][0]], k) for k in BIG})))
    nf = len(FIRST_GROUP)
    got = _gather_first([shards[k] for k in FIRST_GROUP], [shards[k] for k in LATE_GROUP])
    w_first = {k: full(g, k) for k, g in zip(FIRST_GROUP, got[:nf])}
    late = {}
    late["own_sems"], late["srcs"], late["lands"], w_token = _late_gather_call(
        "gather_late_start", 0, [shards[k] for k in LATE_GROUP], got[nf:], [])

    def late_pass(dep):
        late["pass_sems"], late["srcs"], late["lands"], token = _late_gather_call(
            "gather_late_pass", 1, late["srcs"], late["lands"], late["own_sems"], after=dep)
        return token

    def late_weights(dep):
        _, _, lands, _ = _late_gather_call("gather_late_wait", 2, late["srcs"], late["lands"],
                                           late["own_sems"] + late["pass_sems"], after=dep)
        return {k: full(g, k) for k, g in zip(LATE_GROUP, lands)}

    early = {}

    def early_grads(g):
        srcs = [blocks(g[k], k) for k in LATE_GROUP]
        lands = [lax.empty((N_PEERS, BIG[k][1], BIG[k][2]), BF16) for k in LATE_GROUP]
        early["send"], early["recv"], early["srcs"], early["lands"], token = _split_start(
            "grads_late_start", srcs, lands, scatter=True)
        return token

    def small2d(a, n):
        if n in SMALL_TRANSPOSED:
            a = jnp.swapaxes(a, -1, -2)
        return a.reshape(SMALL[n])

    def small_master(a, n):
        if n in SMALL_TRANSPOSED:
            shape = weights[n].shape
            return jnp.swapaxes(a.reshape(shape[:-2] + (shape[-1], shape[-2])), -1, -2)
        return a.reshape(weights[n].shape)

    small_p = {n: small2d(weights[n], n) for n in SMALL_PARAMS}
    _, grad_x, g_first, g_small = _local_step(
        x.reshape(SEQ, D_MODEL), loss_target.reshape(SEQ, D_MODEL), w_first, small_p, late_weights, early_grads,
        after=w_token, midway=late_pass)

    own_sums, first_parts, small_grad = _exchange_last([blocks(g_first[k], k) for k in FIRST_GROUP],
                                                       _pack_small(g_small))
    own_late, late_parts = _split_wait("grads_late_wait", early["send"], early["recv"], early["srcs"],
                                       early["lands"], True, small_grad)
    own = dict(zip(FIRST_GROUP + LATE_GROUP, list(own_sums) + list(own_late)))
    parts = dict(zip(FIRST_GROUP + LATE_GROUP, list(first_parts) + list(late_parts)))
    outs = {}
    for k in BIG:
        n = BIG[k][0]
        outs[n] = [shard_master(o, k) for o in
                   _adamw_big(own[k], parts[k], shard2d(weights[n], k), shard2d(moms[n], k), shard2d(vars_[n], k),
                              "adamw_" + n)]
    small_out, loss_row = _adamw_small(small_grad, small_p, {n: small2d(moms[n], n) for n in SMALL_PARAMS},
                                       {n: small2d(vars_[n], n) for n in SMALL_PARAMS})
    for n in SMALL_PARAMS:
        outs[n] = [small_master(o, n) for o in small_out[n]]

    result = [loss_row[0, 0], grad_x.reshape(x.shape)]
    for i in range(4):
        result += [outs[n][i] for n in WEIGHT_NAMES]
    return tuple(result)
```

```python
import functools

import jax
import jax.numpy as jnp
from jax import lax
from jax.experimental import pallas as pl
from jax.experimental.pallas import tpu as pltpu

F32 = jnp.float32
BF16 = jnp.bfloat16

N_DEV = 8
SEQ = 2048
D_MODEL = 1024
D_FF = 2816
ATTN_HEADS = 8
KV_HEADS = 2
HEAD_DIM = 64
ATTN_WIDTH = 512
KV_WIDTH = 128
WINDOW = 128
SSM_WIDTH = 512
IN_WIDTH = 1280
EPS = 1e-6
MASKED_DISTANCE = 1e33
LAMBDA_RE_MAX = -1e-4
LANES = 128
N_LANE_BLOCKS = 16
SCAN_CHUNK = SEQ // 8

ADAM_LR = 0.001
ADAM_B1 = 0.9
ADAM_B2 = 0.999
ADAM_EPS = 1e-08
ADAM_WD = 0.01
ADAM_STEP = 10

VMEM_LIMIT = 60 * 1024 * 1024
MESH_ID = pl.DeviceIdType.MESH


def _cparams(sem=None):
    return pltpu.CompilerParams(dimension_semantics=sem, vmem_limit_bytes=VMEM_LIMIT)


def _dot(a, b):
    return jnp.dot(a, b, preferred_element_type=F32)


def _dot_nt(a, b):
    return lax.dot_general(a, b, (((1,), (1,)), ((), ())), preferred_element_type=F32)


def _dot_tn(a, b):
    return lax.dot_general(a, b, (((0,), (0,)), ((), ())), preferred_element_type=F32)


def _rms_fwd(x, g):
    r = lax.rsqrt(jnp.mean(x * x, axis=-1, keepdims=True) + EPS)
    return x * r * g


def _rms_bwd(dh, x, g):
    r = lax.rsqrt(jnp.mean(x * x, axis=-1, keepdims=True) + EPS)
    xh = x * r
    dg = jnp.sum(dh * xh, axis=0, keepdims=True)
    dxh = dh * g
    dx = r * (dxh - xh * jnp.mean(dxh * xh, axis=-1, keepdims=True))
    return dx, dg


def _sigmoid(x):
    return 1.0 / (1.0 + jnp.exp(-x))


FFN_TM = 512
FFN_TF = 1408


def _ffn_fwd(x, g, wgt, wut, wd, name, after=None, head=None):
    tm, tf = FFN_TM, FFN_TF
    nj = D_FF // tf
    deps = [] if after is None else [after]
    n_in = len(deps) + (2 if head else 0)

    def body(x_ref, g_ref, wg_ref, wu_ref, wd_ref, *rest):
        i = pl.program_id(0)
        j = pl.program_id(1)
        if head:
            gf_ref, t_ref = rest[len(deps):n_in]
            xo_ref, h_ref, a_ref, b_ref, loss_ref, dgf_ref, h_s, acc = rest[n_in:]
        else:
            xo_ref, h_ref, a_ref, b_ref, h_s, acc = rest[n_in:]

        @pl.when(j == 0)
        def _():
            h = _rms_fwd(x_ref[...], g_ref[...]).astype(BF16)
            h_s[...] = h
            h_ref[...] = h
            acc[...] = jnp.zeros_like(acc)

        h = h_s[...]
        a = _dot_nt(h, wg_ref[...])
        b = _dot_nt(h, wu_ref[...])
        a_ref[...] = a.astype(BF16)
        b_ref[...] = b.astype(BF16)
        s = (a * _sigmoid(a) * b).astype(BF16)
        acc[...] += _dot(s, wd_ref[...])

        @pl.when(j == nj - 1)
        def _():
            xo = x_ref[...] + 0.5 * acc[...]
            if not head:
                xo_ref[...] = xo
                return
            gf = gf_ref[...]
            err = _rms_fwd(xo, gf) - t_ref[...]
            part = jnp.broadcast_to(0.5 * jnp.sum(err * err) / D_MODEL, (1, LANES))
            dx, dgf = _rms_bwd(err * (1.0 / D_MODEL), xo, gf)
            xo_ref[...] = dx

            @pl.when(i == 0)
            def _():
                loss_ref[...] = part
                dgf_ref[...] = dgf

            @pl.when(i != 0)
            def _():
                loss_ref[...] += part
                dgf_ref[...] += dgf

    row = lambda i, j: (i, 0)
    const = lambda i, j: (0, 0)
    head_in = [pl.BlockSpec((1, D_MODEL), const), pl.BlockSpec((tm, D_MODEL), row)] if head else []
    head_out = [pl.BlockSpec((1, LANES), const), pl.BlockSpec((1, D_MODEL), const)] if head else []
    head_shape = [jax.ShapeDtypeStruct((1, LANES), F32), jax.ShapeDtypeStruct((1, D_MODEL), F32)] if head else []
    return pl.pallas_call(
        body, name=name, grid=(SEQ // tm, nj),
        in_specs=[pl.BlockSpec((tm, D_MODEL), row), pl.BlockSpec((1, D_MODEL), const),
                  pl.BlockSpec((tf, D_MODEL), lambda i, j: (j, 0)),
                  pl.BlockSpec((tf, D_MODEL), lambda i, j: (j, 0)),
                  pl.BlockSpec((tf, D_MODEL), lambda i, j: (j, 0))] + [pl.BlockSpec(memory_space=pl.ANY)] * len(deps)
        + head_in,
        out_specs=[pl.BlockSpec((tm, D_MODEL), row), pl.BlockSpec((tm, D_MODEL), row),
                   pl.BlockSpec((tm, tf), lambda i, j: (i, j)),
                   pl.BlockSpec((tm, tf), lambda i, j: (i, j))] + head_out,
        out_shape=[jax.ShapeDtypeStruct((SEQ, D_MODEL), F32), jax.ShapeDtypeStruct((SEQ, D_MODEL), BF16),
                   jax.ShapeDtypeStruct((SEQ, D_FF), BF16), jax.ShapeDtypeStruct((SEQ, D_FF), BF16)] + head_shape,
        scratch_shapes=[pltpu.VMEM((tm, D_MODEL), BF16), pltpu.VMEM((tm, D_MODEL), F32)],
        compiler_params=_cparams(("arbitrary" if head else "parallel", "arbitrary")),
    )(x, g, wgt, wut, wd, *deps, *(head or ()))


def _ffn_bwd_act(dxo, x, g, a, b, wgt, wut, wd, name):
    tm, tf = FFN_TM, FFN_TF
    nj = D_FF // tf

    def body(dxo_ref, x_ref, g_ref, a_ref, b_ref, wg_ref, wu_ref, wd_ref,
             dx_ref, da_ref, db_ref, s_ref, df_ref, dg_ref, df_s, acc):
        i = pl.program_id(0)
        j = pl.program_id(1)

        @pl.when(j == 0)
        def _():
            df = (0.5 * dxo_ref[...]).astype(BF16)
            df_s[...] = df
            df_ref[...] = df
            acc[...] = jnp.zeros_like(acc)

        ds = _dot_nt(df_s[...], wd_ref[...])
        av = a_ref[...].astype(F32)
        bv = b_ref[...].astype(F32)
        sig = _sigmoid(av)
        sl = av * sig
        s_ref[...] = (sl * bv).astype(BF16)
        db = (ds * sl).astype(BF16)
        da = (ds * bv * (sig * (1.0 + av * (1.0 - sig)))).astype(BF16)
        da_ref[...] = da
        db_ref[...] = db
        acc[...] += _dot(da, wg_ref[...]) + _dot(db, wu_ref[...])

        @pl.when(j == nj - 1)
        def _():
            dx, dg = _rms_bwd(acc[...], x_ref[...], g_ref[...])
            dx_ref[...] = dxo_ref[...] + dx

            @pl.when(i == 0)
            def _():
                dg_ref[...] = dg

            @pl.when(i != 0)
            def _():
                dg_ref[...] += dg

    row = lambda i, j: (i, 0)
    col = lambda i, j: (j, 0)
    tile = lambda i, j: (i, j)
    return pl.pallas_call(
        body, name=name, grid=(SEQ // tm, nj),
        in_specs=[pl.BlockSpec((tm, D_MODEL), row), pl.BlockSpec((tm, D_MODEL), row),
                  pl.BlockSpec((1, D_MODEL), lambda i, j: (0, 0)),
                  pl.BlockSpec((tm, tf), tile), pl.BlockSpec((tm, tf), tile),
                  pl.BlockSpec((tf, D_MODEL), col), pl.BlockSpec((tf, D_MODEL), col), pl.BlockSpec((tf, D_MODEL), col)],
        out_specs=[pl.BlockSpec((tm, D_MODEL), row),
                   pl.BlockSpec((tm, tf), tile), pl.BlockSpec((tm, tf), tile), pl.BlockSpec((tm, tf), tile),
                   pl.BlockSpec((tm, D_MODEL), row),
                   pl.BlockSpec((1, D_MODEL), lambda i, j: (0, 0))],
        out_shape=[jax.ShapeDtypeStruct((SEQ, D_MODEL), F32),
                   jax.ShapeDtypeStruct((SEQ, D_FF), BF16), jax.ShapeDtypeStruct((SEQ, D_FF), BF16),
                   jax.ShapeDtypeStruct((SEQ, D_FF), BF16),
                   jax.ShapeDtypeStruct((SEQ, D_MODEL), BF16),
                   jax.ShapeDtypeStruct((1, D_MODEL), F32)],
        scratch_shapes=[pltpu.VMEM((tm, D_MODEL), BF16), pltpu.VMEM((tm, D_MODEL), F32)],
        compiler_params=_cparams(("arbitrary", "arbitrary")),
    )(dxo, x, g, a, b, wgt, wut, wd)


def _mm_tn(pairs, name, tmm=256):
    m = pairs[0][0].shape[1]
    n_pairs = len(pairs)

    def body(*refs):
        ins, outs = refs[:2 * n_pairs], refs[2 * n_pairs:]
        for p in range(n_pairs):
            outs[p][...] = _dot_tn(ins[2 * p][...], ins[2 * p + 1][...]).astype(BF16)

    in_specs, out_specs, out_shape, args = [], [], [], []
    for a, b in pairs:
        n = b.shape[1]
        in_specs += [pl.BlockSpec((SEQ, tmm), lambda i: (0, i)), pl.BlockSpec((SEQ, n), lambda i: (0, 0))]
        out_specs.append(pl.BlockSpec((tmm, n), lambda i: (i, 0)))
        out_shape.append(jax.ShapeDtypeStruct((m, n), BF16))
        args += [a, b]
    return pl.pallas_call(body, name=name, grid=(m // tmm,), in_specs=in_specs, out_specs=out_specs,
                          out_shape=out_shape, compiler_params=_cparams(("parallel",)))(*args)


MIX_TM = 256


def _mixin_fwd(x, g, wint):
    tm = MIX_TM

    def body(x_ref, g_ref, w_ref, h_ref, q_ref, k_ref, v_ref, u_ref):
        h = _rms_fwd(x_ref[...], g_ref[...]).astype(BF16)
        h_ref[...] = h
        proj = _dot_nt(h, w_ref[...])
        q_ref[...] = proj[:, :ATTN_WIDTH].T
        k_ref[...] = proj[:, ATTN_WIDTH:ATTN_WIDTH + KV_WIDTH]
        v_ref[...] = proj[:, ATTN_WIDTH + KV_WIDTH:ATTN_WIDTH + 2 * KV_WIDTH]
        u_ref[...] = proj[:, ATTN_WIDTH + 2 * KV_WIDTH:]

    row = lambda i: (i, 0)
    return pl.pallas_call(
        body, name="mixin_fwd", grid=(SEQ // tm,),
        in_specs=[pl.BlockSpec((tm, D_MODEL), row), pl.BlockSpec((1, D_MODEL), lambda i: (0, 0)),
                  pl.BlockSpec((IN_WIDTH, D_MODEL), lambda i: (0, 0))],
        out_specs=[pl.BlockSpec((tm, D_MODEL), row), pl.BlockSpec((ATTN_WIDTH, tm), lambda i: (0, i)),
                   pl.BlockSpec((tm, KV_WIDTH), row), pl.BlockSpec((tm, KV_WIDTH), row),
                   pl.BlockSpec((tm, SSM_WIDTH), row)],
        out_shape=[jax.ShapeDtypeStruct((SEQ, D_MODEL), BF16), jax.ShapeDtypeStruct((ATTN_WIDTH, SEQ), F32),
                   jax.ShapeDtypeStruct((SEQ, KV_WIDTH), F32), jax.ShapeDtypeStruct((SEQ, KV_WIDTH), F32),
                   jax.ShapeDtypeStruct((SEQ, SSM_WIDTH), F32)],
        compiler_params=_cparams(("parallel",)),
    )(x, g, wint)


def _mixin_bwd(dqt, dk, dv, du, wint, x, g, dres):
    tm = MIX_TM

    def body(dq_ref, dk_ref, dv_ref, du_ref, w_ref, x_ref, g_ref, dres_ref, dx_ref, dp_ref, dg_ref):
        i = pl.program_id(0)
        dp = jnp.concatenate([dq_ref[...].T, dk_ref[...], dv_ref[...], du_ref[...]], axis=-1).astype(BF16)
        dp_ref[...] = dp
        dh = _dot(dp, w_ref[...])
        dx, dg = _rms_bwd(dh, x_ref[...], g_ref[...])
        dx_ref[...] = dres_ref[...] + dx

        @pl.when(i == 0)
        def _():
            dg_ref[...] = dg

        @pl.when(i != 0)
        def _():
            dg_ref[...] += dg

    row = lambda i: (i, 0)
    const = lambda i: (0, 0)
    return pl.pallas_call(
        body, name="mixin_bwd", grid=(SEQ // tm,),
        in_specs=[pl.BlockSpec((ATTN_WIDTH, tm), lambda i: (0, i)), pl.BlockSpec((tm, KV_WIDTH), row),
                  pl.BlockSpec((tm, KV_WIDTH), row), pl.BlockSpec((tm, SSM_WIDTH), row),
                  pl.BlockSpec((IN_WIDTH, D_MODEL), const), pl.BlockSpec((tm, D_MODEL), row),
                  pl.BlockSpec((1, D_MODEL), const), pl.BlockSpec((tm, D_MODEL), row)],
        out_specs=[pl.BlockSpec((tm, D_MODEL), row), pl.BlockSpec((tm, IN_WIDTH), row),
                   pl.BlockSpec((1, D_MODEL), const)],
        out_shape=[jax.ShapeDtypeStruct((SEQ, D_MODEL), F32), jax.ShapeDtypeStruct((SEQ, IN_WIDTH), BF16),
                   jax.ShapeDtypeStruct((1, D_MODEL), F32)],
        compiler_params=_cparams(("arbitrary",)),
    )(dqt, dk, dv, du, wint, x, g, dres)


N_QBLOCKS = SEQ // WINDOW
GROUP = ATTN_HEADS // KV_HEADS
SCALE = HEAD_DIM ** -0.5


def _alibi_slope(h):
    return 2.0 ** (-8.0 * (h + 1) / ATTN_HEADS)


def _window_masks(n):
    s_idx = lax.broadcasted_iota(jnp.int32, (3 * WINDOW, WINDOW), 0)
    t_idx = lax.broadcasted_iota(jnp.int32, (3 * WINDOW, WINDOW), 1)
    absrel = jnp.abs(s_idx - WINDOW - t_idx)
    key_pos = n * WINDOW - WINDOW + s_idx
    valid = (absrel <= WINDOW) & (key_pos >= 0) & (key_pos < SEQ)
    return jnp.where(valid, absrel.astype(F32), MASKED_DISTANCE)


def _group_cols(ref, r0, gi):
    return jnp.concatenate(
        [ref[(gi * GROUP + hh) * HEAD_DIM:(gi * GROUP + hh + 1) * HEAD_DIM, pl.ds(r0, WINDOW)].astype(BF16)
         for hh in range(GROUP)], axis=1)


def _group_probs(qgt, kw, dist, gi, sk_ref):
    bias = jnp.concatenate([-_alibi_slope(gi * GROUP + hh) * dist for hh in range(GROUP)], axis=1)
    sink = jnp.concatenate([jnp.full((1, WINDOW), sk_ref[0, gi * GROUP + hh], F32) for hh in range(GROUP)], axis=1)
    s = _dot(kw, qgt) * SCALE + bias
    m = jnp.maximum(jnp.max(s, axis=0, keepdims=True), sink)
    p = jnp.exp(s - m)
    ps = jnp.exp(sink - m)
    inv = 1.0 / (jnp.sum(p, axis=0, keepdims=True) + ps)
    return p * inv, ps * inv


def _pad_window(src_ref, dst_ref):
    zeros = jnp.zeros((WINDOW, KV_WIDTH), BF16)
    dst_ref[0:WINDOW, :] = zeros
    dst_ref[WINDOW + SEQ:, :] = zeros
    dst_ref[WINDOW:WINDOW + SEQ, :] = src_ref[...].astype(BF16)


def _attn_fwd(qt, k, v, sinks):
    def body(sk_ref, qt_ref, k_ref, v_ref, o_ref, kp_ref, vp_ref):
        _pad_window(k_ref, kp_ref)
        _pad_window(v_ref, vp_ref)

        def blk(n, carry):
            r0 = pl.multiple_of(n * WINDOW, WINDOW)
            dist = _window_masks(n)
            for gi in range(KV_HEADS):
                kw = kp_ref[pl.ds(r0, 3 * WINDOW), gi * HEAD_DIM:(gi + 1) * HEAD_DIM]
                vw = vp_ref[pl.ds(r0, 3 * WINDOW), gi * HEAD_DIM:(gi + 1) * HEAD_DIM]
                pr, _ = _group_probs(_group_cols(qt_ref, r0, gi), kw, dist, gi, sk_ref)
                og = _dot_tn(pr.astype(BF16), vw)
                for hh in range(GROUP):
                    h = gi * GROUP + hh
                    o_ref[pl.ds(r0, WINDOW), h * HEAD_DIM:(h + 1) * HEAD_DIM] = og[hh * WINDOW:(hh + 1) * WINDOW]
            return carry

        lax.fori_loop(0, N_QBLOCKS, blk, 0)

    vmem = pl.BlockSpec(memory_space=pltpu.VMEM)
    return pl.pallas_call(
        body, name="attn_fwd",
        in_specs=[pl.BlockSpec(memory_space=pltpu.SMEM), vmem, vmem, vmem], out_specs=vmem,
        out_shape=jax.ShapeDtypeStruct((SEQ, ATTN_WIDTH), F32),
        scratch_shapes=[pltpu.VMEM((SEQ + 2 * WINDOW, KV_WIDTH), BF16)] * 2,
        compiler_params=_cparams(),
    )(sinks, qt, k, v)


def _attn_bwd(qt, k, v, sinks, dot_):
    def body(sk_ref, qt_ref, k_ref, v_ref, dot_ref, dqt_ref, dk_ref, dv_ref, dsk_ref,
             dsk_acc, kp_ref, vp_ref, dkp_ref, dvp_ref):
        _pad_window(k_ref, kp_ref)
        _pad_window(v_ref, vp_ref)
        dkp_ref[...] = jnp.zeros_like(dkp_ref)
        dvp_ref[...] = jnp.zeros_like(dvp_ref)
        dsk_acc[...] = jnp.zeros_like(dsk_acc)

        def blk(n, carry):
            r0 = pl.multiple_of(n * WINDOW, WINDOW)
            dist = _window_masks(n)
            for gi in range(KV_HEADS):
                gcols = slice(gi * HEAD_DIM, (gi + 1) * HEAD_DIM)
                kw = kp_ref[pl.ds(r0, 3 * WINDOW), gcols]
                vw = vp_ref[pl.ds(r0, 3 * WINDOW), gcols]
                qgt = _group_cols(qt_ref, r0, gi)
                dogt = _group_cols(dot_ref, r0, gi)
                pr, psink = _group_probs(qgt, kw, dist, gi, sk_ref)
                dp = _dot(vw, dogt)
                delta = jnp.sum(pr * dp, axis=0, keepdims=True)
                ds = (pr * (dp - delta)).astype(BF16)
                dsk_acc[gi:gi + 1, :] += -(psink * delta)
                dqgt = _dot_tn(kw, ds) * SCALE
                for hh in range(GROUP):
                    h = gi * GROUP + hh
                    dqt_ref[h * HEAD_DIM:(h + 1) * HEAD_DIM, pl.ds(r0, WINDOW)] = dqgt[:, hh * WINDOW:(hh + 1) * WINDOW]
                dkp_ref[pl.ds(r0, 3 * WINDOW), gcols] += _dot_nt(ds, qgt) * SCALE
                dvp_ref[pl.ds(r0, 3 * WINDOW), gcols] += _dot_nt(pr.astype(BF16), dogt)
            return carry

        lax.fori_loop(0, N_QBLOCKS, blk, 0)
        for h in range(ATTN_HEADS):
            gi, hh = divmod(h, GROUP)
            dsk_ref[:, h:h + 1] = jnp.sum(dsk_acc[gi:gi + 1, hh * WINDOW:(hh + 1) * WINDOW], axis=1, keepdims=True)
        dk_ref[...] = dkp_ref[WINDOW:WINDOW + SEQ, :]
        dv_ref[...] = dvp_ref[WINDOW:WINDOW + SEQ, :]

    vmem = pl.BlockSpec(memory_space=pltpu.VMEM)
    padded = (SEQ + 2 * WINDOW, KV_WIDTH)
    return pl.pallas_call(
        body, name="attn_bwd",
        in_specs=[pl.BlockSpec(memory_space=pltpu.SMEM), vmem, vmem, vmem, vmem],
        out_specs=[vmem, vmem, vmem, vmem],
        out_shape=[jax.ShapeDtypeStruct((ATTN_WIDTH, SEQ), F32),
                   jax.ShapeDtypeStruct((SEQ, KV_WIDTH), F32), jax.ShapeDtypeStruct((SEQ, KV_WIDTH), F32),
                   jax.ShapeDtypeStruct((1, ATTN_HEADS), F32)],
        scratch_shapes=[pltpu.VMEM((KV_HEADS, GROUP * WINDOW), F32), pltpu.VMEM(padded, BF16),
                        pltpu.VMEM(padded, BF16), pltpu.VMEM(padded, F32), pltpu.VMEM(padded, F32)],
        compiler_params=_cparams(),
    )(sinks, qt, k, v, dot_)


HALF_LANES = LANES // 2
BLOCK_ROWS = 32


def _embed_block(bt, q):
    z = jnp.zeros((16, HALF_LANES), bt.dtype)
    blk = jnp.concatenate([jnp.concatenate([bt[:16], z], axis=1), jnp.concatenate([z, bt[16:]], axis=1)], axis=0)
    parts = [jnp.zeros((BLOCK_ROWS * q, LANES), bt.dtype)] if q else []
    parts.append(blk)
    if q < 3:
        parts.append(jnp.zeros((BLOCK_ROWS * (3 - q), LANES), bt.dtype))
    return jnp.concatenate(parts, axis=0)


def _extract_block(m, q):
    blk = m[BLOCK_ROWS * q:BLOCK_ROWS * (q + 1)]
    return jnp.concatenate([blk[:16, :HALF_LANES], blk[16:, HALF_LANES:]], axis=0)


def _ssm_prep(lam_re, lam_im, log_dt, bt_re, bt_im, c_re, c_im):
    nb = 2 * N_LANE_BLOCKS

    def body(lr_ref, li_ref, ldt_ref, btr_ref, bti_ref, ctr_ref, cti_ref, ar_ref, ai_ref, bb_ref, cc_ref):
        lr = jnp.minimum(lr_ref[...], LAMBDA_RE_MAX)
        li = li_ref[...]
        dt = jnp.exp(ldt_ref[...])
        mag = jnp.exp(lr * dt)
        ar = mag * jnp.cos(li * dt)
        ai = mag * jnp.sin(li * dt)
        den = lr * lr + li * li
        cr = ((ar - 1.0) * lr + ai * li) / den
        ci = (ai * lr - (ar - 1.0) * li) / den
        ar_ref[...] = ar
        ai_ref[...] = ai
        for i in range(nb):
            q = i % 4
            rows = slice(BLOCK_ROWS * i, BLOCK_ROWS * (i + 1))
            br = _embed_block(btr_ref[rows, :], q)
            bi = _embed_block(bti_ref[rows, :], q)
            cri, cii = cr[i:i + 1, :], ci[i:i + 1, :]
            bb_ref[i] = jnp.concatenate([cri * br - cii * bi, cri * bi + cii * br], axis=1).astype(BF16)
            cc_ref[i] = jnp.concatenate([_embed_block(ctr_ref[rows, :], q).T,
                                         -_embed_block(cti_ref[rows, :], q).T], axis=0).astype(BF16)

    return pl.pallas_call(
        body, name="ssm_prep",
        out_shape=[jax.ShapeDtypeStruct((nb, LANES), F32), jax.ShapeDtypeStruct((nb, LANES), F32),
                   jax.ShapeDtypeStruct((nb, LANES, 2 * LANES), BF16),
                   jax.ShapeDtypeStruct((nb, 2 * LANES, LANES), BF16)],
        compiler_params=_cparams(),
    )(lam_re, lam_im, log_dt, bt_re, bt_im, c_re, c_im)


def _ssm_prep_bwd(lam_re, lam_im, log_dt, bt_re, bt_im, dar, dai, dbb, dcc):
    nb = 2 * N_LANE_BLOCKS

    def body(lr_ref, li_ref, ldt_ref, btr_ref, bti_ref, dar_ref, dai_ref, dbb_ref, dcc_ref,
             glr_ref, gli_ref, gdt_ref, gbr_ref, gbi_ref, gcre_ref, gcim_ref, gcr_s, gci_s):
        lam = lr_ref[...]
        lr = jnp.minimum(lam, LAMBDA_RE_MAX)
        li = li_ref[...]
        dt = jnp.exp(ldt_ref[...])
        mag = jnp.exp(lr * dt)
        cs = jnp.cos(li * dt)
        sn = jnp.sin(li * dt)
        ar = mag * cs
        ai = mag * sn
        den = lr * lr + li * li
        nr = (ar - 1.0) * lr + ai * li
        ni = ai * lr - (ar - 1.0) * li
        cr = nr / den
        ci = ni / den
        for i in range(nb):
            q = i % 4
            rows = slice(BLOCK_ROWS * i, BLOCK_ROWS * (i + 1))
            br = _embed_block(btr_ref[rows, :], q)
            bi = _embed_block(bti_ref[rows, :], q)
            gbbr = dbb_ref[i, :, :LANES]
            gbbi = dbb_ref[i, :, LANES:]
            cri, cii = cr[i:i + 1, :], ci[i:i + 1, :]
            gcr_s[i:i + 1, :] = jnp.sum(gbbr * br + gbbi * bi, axis=0, keepdims=True)
            gci_s[i:i + 1, :] = jnp.sum(gbbi * br - gbbr * bi, axis=0, keepdims=True)
            gbr_ref[rows, :] = _extract_block(cri * gbbr + cii * gbbi, q)
            gbi_ref[rows, :] = _extract_block(cri * gbbi - cii * gbbr, q)
            gcre_ref[rows, :] = _extract_block(dcc_ref[i, :LANES, :].T, q)
            gcim_ref[rows, :] = -_extract_block(dcc_ref[i, LANES:, :].T, q)
        g_cr = gcr_s[...]
        g_ci = gci_s[...]
        g_nr = g_cr / den
        g_ni = g_ci / den
        g_den = -(g_cr * nr + g_ci * ni) / (den * den)
        g_ar = dar_ref[...] + g_nr * lr - g_ni * li
        g_ai = dai_ref[...] + g_nr * li + g_ni * lr
        g_lr = g_nr * (ar - 1.0) + g_ni * ai + g_den * 2.0 * lr
        g_li = g_nr * ai - g_ni * (ar - 1.0) + g_den * 2.0 * li
        g_mag = g_ar * cs + g_ai * sn
        g_th = (g_ai * cs - g_ar * sn) * mag
        g_lr = g_lr + g_mag * mag * dt
        g_li = g_li + g_th * dt
        g_dt = g_mag * mag * lr + g_th * li
        glr_ref[...] = jnp.where(lam < LAMBDA_RE_MAX, g_lr, 0.0)
        gli_ref[...] = g_li
        gl = g_dt * dt
        half = LANES // 2
        gdt_ref[:, 0:1] = jnp.sum(gl[:, :half], axis=1, keepdims=True)
        gdt_ref[:, 1:2] = jnp.sum(gl[:, half:], axis=1, keepdims=True)

    rows_shape = jax.ShapeDtypeStruct((nb * BLOCK_ROWS, HALF_LANES), F32)
    return pl.pallas_call(
        body, name="ssm_prep_bwd",
        out_shape=[jax.ShapeDtypeStruct((nb, LANES), F32), jax.ShapeDtypeStruct((nb, LANES), F32),
                   jax.ShapeDtypeStruct((nb, 2), F32), rows_shape, rows_shape, rows_shape, rows_shape],
        scratch_shapes=[pltpu.VMEM((nb, LANES), F32), pltpu.VMEM((nb, LANES), F32)],
        compiler_params=_cparams(),
    )(lam_re, lam_im, log_dt, bt_re, bt_im, dar, dai, dbb, dcc)


def _cmul(ar, ai, br, bi):
    return ar * br - ai * bi, ar * bi + ai * br


SCAN_ORDER = (SCAN_CHUNK, 8, LANES)


def _scan_order_copies(hbm_ref, lane_block, vmem_ref, sems, to_hbm):
    cols = pl.ds(pl.multiple_of(lane_block * LANES, LANES), LANES)
    copies = []
    for c in range(8):
        hbm = hbm_ref.at[pl.ds(c * SCAN_CHUNK, SCAN_CHUNK), cols]
        src, dst = (vmem_ref.at[:, c, :], hbm) if to_hbm else (hbm, vmem_ref.at[:, c, :])
        copies.append(pltpu.make_async_copy(src, dst, sems.at[c]))
    return copies


def _scan_inplace(re_ref, im_ref, a_re, a_im, reverse):
    nq = len(a_re)
    ch = SCAN_CHUNK
    ab_re = [jnp.broadcast_to(a, (8, LANES)) for a in a_re]
    ab_im = [jnp.broadcast_to(a, (8, LANES)) for a in a_im]

    def rows(j):
        jj = (ch - 1 - j) if reverse else j
        return pl.ds(pl.multiple_of(jj * 8, 8), 8)

    def sweep(init, store):
        def step(j, st):
            out = []
            r = rows(j)
            for qi in range(nq):
                xr, xi = st[2 * qi], st[2 * qi + 1]
                pr, pi = _cmul(ab_re[qi], ab_im[qi], xr, xi)
                xr = pr + re_ref[qi, r, :]
                xi = pi + im_ref[qi, r, :]
                if store:
                    re_ref[qi, r, :] = xr
                    im_ref[qi, r, :] = xi
                out += [xr, xi]
            return tuple(out)
        return lax.fori_loop(0, ch, step, tuple(init), unroll=2)

    zeros = [jnp.zeros((8, LANES), F32)] * (2 * nq)
    finals = sweep(zeros, store=False)

    row_id = lax.broadcasted_iota(jnp.int32, (8, LANES), 0)
    carries = []
    for qi in range(nq):
        pr, pi = ab_re[qi], ab_im[qi]
        for _ in range(8):
            pr, pi = _cmul(pr, pi, pr, pi)
        fr, fi = finals[2 * qi], finals[2 * qi + 1]
        sr = jnp.zeros((8, LANES), F32)
        si = jnp.zeros((8, LANES), F32)
        for _ in range(7):
            tr, ti = _cmul(pr, pi, sr, si)
            tr, ti = tr + fr, ti + fi
            if reverse:
                sr = jnp.where(row_id == 7, 0.0, pltpu.roll(tr, 7, axis=0))
                si = jnp.where(row_id == 7, 0.0, pltpu.roll(ti, 7, axis=0))
            else:
                sr = jnp.where(row_id == 0, 0.0, pltpu.roll(tr, 1, axis=0))
                si = jnp.where(row_id == 0, 0.0, pltpu.roll(ti, 1, axis=0))
        carries += [sr, si]
    sweep(carries, store=True)


SSM_Q = 4


def _ssm_fwd(u, are, aim, bb, cc, dskip, after=None):
    nq = SSM_Q
    deps = [] if after is None else [after]

    def body(u_hbm, ar_ref, ai_ref, bb_ref, cc_ref, d_ref, *rest):
        y_hbm, xr_ref, xi_ref, sre, sim, io3, yp, sems = rest[len(deps):]
        kb = pl.program_id(0)
        loads = _scan_order_copies(u_hbm, kb, io3, sems, to_hbm=False)
        for cp in loads:
            cp.start()
        for cp in loads:
            cp.wait()
        uf = io3[...].reshape(SEQ, LANES)
        ub = uf.astype(BF16)
        yp[...] = d_ref[...] * uf
        for d in range(2):
            for qi in range(nq):
                sre[qi] = _dot(ub, bb_ref[d, qi, :, :LANES])
                sim[qi] = _dot(ub, bb_ref[d, qi, :, LANES:])
            _scan_inplace(sre, sim, [ar_ref[d, qi] for qi in range(nq)], [ai_ref[d, qi] for qi in range(nq)],
                          reverse=(d == 1))
            for qi in range(nq):
                xrb = sre[qi].astype(BF16)
                xib = sim[qi].astype(BF16)
                xr_ref[d, qi] = xrb
                xi_ref[d, qi] = xib
                yp[...] += _dot(xrb, cc_ref[d, qi, :LANES, :]) + _dot(xib, cc_ref[d, qi, LANES:, :])
        io3[...] = yp[...].reshape(SCAN_ORDER)
        stores = _scan_order_copies(y_hbm, kb, io3, sems, to_hbm=True)
        for cp in stores:
            cp.start()
        for cp in stores:
            cp.wait()

    blk4 = lambda k: (0, k, 0, 0)
    return pl.pallas_call(
        body, name="ssm_fwd", grid=(SSM_WIDTH // LANES,),
        in_specs=[pl.BlockSpec(memory_space=pl.ANY),
                  pl.BlockSpec((2, nq, 1, LANES), blk4), pl.BlockSpec((2, nq, 1, LANES), blk4),
                  pl.BlockSpec((2, nq, LANES, 2 * LANES), blk4), pl.BlockSpec((2, nq, 2 * LANES, LANES), blk4),
                  pl.BlockSpec((1, LANES), lambda k: (0, k))] + [pl.BlockSpec(memory_space=pl.ANY)] * len(deps),
        out_specs=[pl.BlockSpec(memory_space=pl.ANY),
                   pl.BlockSpec((2, nq, SEQ, LANES), blk4), pl.BlockSpec((2, nq, SEQ, LANES), blk4)],
        out_shape=[jax.ShapeDtypeStruct((SEQ, SSM_WIDTH), F32),
                   jax.ShapeDtypeStruct((2, N_LANE_BLOCKS, SEQ, LANES), BF16),
                   jax.ShapeDtypeStruct((2, N_LANE_BLOCKS, SEQ, LANES), BF16)],
        scratch_shapes=[pltpu.VMEM((nq, SEQ, LANES), F32), pltpu.VMEM((nq, SEQ, LANES), F32),
                        pltpu.VMEM(SCAN_ORDER, F32), pltpu.VMEM((SEQ, LANES), F32), pltpu.SemaphoreType.DMA((8,))],
        compiler_params=_cparams(("arbitrary",)),
    )(u, are, aim, bb, cc, dskip, *deps)


def _ssm_bwd(dy, u, xr, xi, are, aim, bb, cc, dskip, after=None):
    nq = SSM_Q
    body_rows = SEQ - 8
    deps = [] if after is None else [after]

    def body(dy_hbm, u_hbm, xr_ref, xi_ref, ar_ref, ai_ref, bb_ref, cc_ref, d_ref, *rest):
        (du_hbm, dd_ref, dcc_ref, dbb_ref, dar_ref, dai_ref,
         sre, sim, u3, dy3, dup, u_sems, dy_sems) = rest[len(deps):]
        kb = pl.program_id(0)
        loads = (_scan_order_copies(u_hbm, kb, u3, u_sems, to_hbm=False)
                 + _scan_order_copies(dy_hbm, kb, dy3, dy_sems, to_hbm=False))
        for cp in loads:
            cp.start()
        for cp in loads:
            cp.wait()
        dyf = dy3[...].reshape(SEQ, LANES)
        uf = u3[...].reshape(SEQ, LANES)
        dyb = dyf.astype(BF16)
        ub = uf.astype(BF16)
        dd_ref[...] = jnp.sum(dyf * uf, axis=0, keepdims=True)
        dup[...] = d_ref[...] * dyf
        row8 = lax.broadcasted_iota(jnp.int32, (8, LANES), 0)
        for d in range(2):
            for qi in range(nq):
                dx = _dot_nt(dyb, cc_ref[d, qi])
                sre[qi] = dx[:, :LANES]
                sim[qi] = dx[:, LANES:]
                dcc_ref[d, qi] = _dot_tn(jnp.concatenate([xr_ref[d, qi], xi_ref[d, qi]], axis=1), dyb)
            _scan_inplace(sre, sim, [ar_ref[d, qi] for qi in range(nq)], [-ai_ref[d, qi] for qi in range(nq)],
                          reverse=(d == 0))
            for qi in range(nq):
                gr = sre[qi]
                gi = sim[qi]
                xrf = xr_ref[d, qi].astype(F32)
                xif = xi_ref[d, qi].astype(F32)
                if d == 0:
                    g_main_r, g_main_i = gr[8:], gi[8:]
                    x_main_r, x_main_i = xrf[:body_rows], xif[:body_rows]
                    g_edge_r, g_edge_i = gr[:8], gi[:8]
                    x_edge_r = jnp.where(row8 == 0, 0.0, pltpu.roll(xrf[body_rows:], 1, axis=0))
                    x_edge_i = jnp.where(row8 == 0, 0.0, pltpu.roll(xif[body_rows:], 1, axis=0))
                else:
                    g_main_r, g_main_i = gr[:body_rows], gi[:body_rows]
                    x_main_r, x_main_i = xrf[8:], xif[8:]
                    g_edge_r, g_edge_i = gr[body_rows:], gi[body_rows:]
                    x_edge_r = jnp.where(row8 == 7, 0.0, pltpu.roll(xrf[:8], 7, axis=0))
                    x_edge_i = jnp.where(row8 == 7, 0.0, pltpu.roll(xif[:8], 7, axis=0))
                dar_ref[d, qi] = (jnp.sum(g_main_r * x_main_r + g_main_i * x_main_i, axis=0, keepdims=True)
                                  + jnp.sum(g_edge_r * x_edge_r + g_edge_i * x_edge_i, axis=0, keepdims=True))
                dai_ref[d, qi] = (jnp.sum(g_main_i * x_main_r - g_main_r * x_main_i, axis=0, keepdims=True)
                                  + jnp.sum(g_edge_i * x_edge_r - g_edge_r * x_edge_i, axis=0, keepdims=True))
                gb = jnp.concatenate([gr, gi], axis=1).astype(BF16)
                dup[...] += _dot_nt(gb, bb_ref[d, qi])
                dbb_ref[d, qi] = _dot_tn(ub, gb)
        u3[...] = dup[...].reshape(SCAN_ORDER)
        stores = _scan_order_copies(du_hbm, kb, u3, u_sems, to_hbm=True)
        for cp in stores:
            cp.start()
        for cp in stores:
            cp.wait()

    blk4 = lambda k: (0, k, 0, 0)
    col = lambda k: (0, k)
    bb_spec = pl.BlockSpec((2, nq, LANES, 2 * LANES), blk4)
    cc_spec = pl.BlockSpec((2, nq, 2 * LANES, LANES), blk4)
    a_spec = pl.BlockSpec((2, nq, 1, LANES), blk4)
    x_spec = pl.BlockSpec((2, nq, SEQ, LANES), blk4)
    a_shape = jax.ShapeDtypeStruct((2, N_LANE_BLOCKS, 1, LANES), F32)
    return pl.pallas_call(
        body, name="ssm_bwd", grid=(SSM_WIDTH // LANES,),
        in_specs=[pl.BlockSpec(memory_space=pl.ANY), pl.BlockSpec(memory_space=pl.ANY), x_spec, x_spec,
                  a_spec, a_spec, bb_spec, cc_spec, pl.BlockSpec((1, LANES), col)]
        + [pl.BlockSpec(memory_space=pl.ANY)] * len(deps),
        out_specs=[pl.BlockSpec(memory_space=pl.ANY), pl.BlockSpec((1, LANES), col),
                   cc_spec, bb_spec, a_spec, a_spec],
        out_shape=[jax.ShapeDtypeStruct((SEQ, SSM_WIDTH), F32), jax.ShapeDtypeStruct((1, SSM_WIDTH), F32),
                   jax.ShapeDtypeStruct((2, N_LANE_BLOCKS, 2 * LANES, LANES), F32),
                   jax.ShapeDtypeStruct((2, N_LANE_BLOCKS, LANES, 2 * LANES), F32), a_shape, a_shape],
        scratch_shapes=[pltpu.VMEM((nq, SEQ, LANES), F32), pltpu.VMEM((nq, SEQ, LANES), F32),
                        pltpu.VMEM(SCAN_ORDER, F32), pltpu.VMEM(SCAN_ORDER, F32), pltpu.VMEM((SEQ, LANES), F32),
                        pltpu.SemaphoreType.DMA((8,)), pltpu.SemaphoreType.DMA((8,))],
        compiler_params=_cparams(("arbitrary",)),
    )(dy, u, xr, xi, are, aim, bb, cc, dskip, *deps)


GELU_C = 0.7978845608028654
GELU_K = 0.044715


def _gelu(y):
    return 0.5 * y * (1.0 + jnp.tanh(GELU_C * (y + GELU_K * y * y * y)))


def _gelu_grad(y):
    t = jnp.tanh(GELU_C * (y + GELU_K * y * y * y))
    return 0.5 * (1.0 + t) + 0.5 * y * (1.0 - t * t) * GELU_C * (1.0 + 3.0 * GELU_K * y * y)


def _mixout_fwd(o, y, glu_w, glu_b, gan, gsn, wout, x1):
    tm = MIX_TM

    def body(o_ref, y_ref, gw_ref, gb_ref, gan_ref, gsn_ref, w_ref, x1_ref, x2_ref, mx_ref):
        yg = _gelu(y_ref[...])
        z = _dot(yg.astype(BF16), gw_ref[...]) + gb_ref[...]
        so = yg * _sigmoid(z)
        na = _rms_fwd(o_ref[...], gan_ref[...])
        ns = _rms_fwd(so, gsn_ref[...])
        mixed = jnp.concatenate([na, ns], axis=-1).astype(BF16)
        mx_ref[...] = mixed
        x2_ref[...] = x1_ref[...] + _dot(mixed, w_ref[...])

    row = lambda i: (i, 0)
    const = lambda i: (0, 0)
    return pl.pallas_call(
        body, name="mixout_fwd", grid=(SEQ // tm,),
        in_specs=[pl.BlockSpec((tm, ATTN_WIDTH), row), pl.BlockSpec((tm, SSM_WIDTH), row),
                  pl.BlockSpec((SSM_WIDTH, SSM_WIDTH), const), pl.BlockSpec((1, SSM_WIDTH), const),
                  pl.BlockSpec((1, ATTN_WIDTH), const), pl.BlockSpec((1, SSM_WIDTH), const),
                  pl.BlockSpec((D_MODEL, D_MODEL), const), pl.BlockSpec((tm, D_MODEL), row)],
        out_specs=[pl.BlockSpec((tm, D_MODEL), row), pl.BlockSpec((tm, D_MODEL), row)],
        out_shape=[jax.ShapeDtypeStruct((SEQ, D_MODEL), F32), jax.ShapeDtypeStruct((SEQ, D_MODEL), BF16)],
        compiler_params=_cparams(("parallel",)),
    )(o, y, glu_w, glu_b, gan, gsn, wout, x1)


def _mixout_bwd(dx2, o, y, glu_w, glu_b, gan, gsn, wout):
    tm = MIX_TM

    def body(dx2_ref, o_ref, y_ref, gw_ref, gb_ref, gan_ref, gsn_ref, w_ref,
             do_ref, dy_ref, dz_ref, yg_ref, dxb_ref, dgan_ref, dgsn_ref, dgb_ref):
        i = pl.program_id(0)
        dxb = dx2_ref[...].astype(BF16)
        dxb_ref[...] = dxb
        dmixed = _dot_nt(dxb, w_ref[...])
        do, dgan = _rms_bwd(dmixed[:, :ATTN_WIDTH], o_ref[...], gan_ref[...])
        do_ref[...] = do.T
        yv = y_ref[...]
        yg = _gelu(yv)
        ygb = yg.astype(BF16)
        yg_ref[...] = ygb
        sg = _sigmoid(_dot(ygb, gw_ref[...]) + gb_ref[...])
        dso, dgsn = _rms_bwd(dmixed[:, ATTN_WIDTH:], yg * sg, gsn_ref[...])
        dz = dso * yg * sg * (1.0 - sg)
        dzb = dz.astype(BF16)
        dz_ref[...] = dzb
        dyg = dso * sg + _dot_nt(dzb, gw_ref[...])
        dy_ref[...] = dyg * _gelu_grad(yv)
        dgb = jnp.sum(dz, axis=0, keepdims=True)

        @pl.when(i == 0)
        def _():
            dgan_ref[...] = dgan
            dgsn_ref[...] = dgsn
            dgb_ref[...] = dgb

        @pl.when(i != 0)
        def _():
            dgan_ref[...] += dgan
            dgsn_ref[...] += dgsn
            dgb_ref[...] += dgb

    row = lambda i: (i, 0)
    const = lambda i: (0, 0)
    return pl.pallas_call(
        body, name="mixout_bwd", grid=(SEQ // tm,),
        in_specs=[pl.BlockSpec((tm, D_MODEL), row), pl.BlockSpec((tm, ATTN_WIDTH), row),
                  pl.BlockSpec((tm, SSM_WIDTH), row),
                  pl.BlockSpec((SSM_WIDTH, SSM_WIDTH), const), pl.BlockSpec((1, SSM_WIDTH), const),
                  pl.BlockSpec((1, ATTN_WIDTH), const), pl.BlockSpec((1, SSM_WIDTH), const),
                  pl.BlockSpec((D_MODEL, D_MODEL), const)],
        out_specs=[pl.BlockSpec((ATTN_WIDTH, tm), lambda i: (0, i)), pl.BlockSpec((tm, SSM_WIDTH), row),
                   pl.BlockSpec((tm, SSM_WIDTH), row), pl.BlockSpec((tm, SSM_WIDTH), row),
                   pl.BlockSpec((tm, D_MODEL), row),
                   pl.BlockSpec((1, ATTN_WIDTH), const), pl.BlockSpec((1, SSM_WIDTH), const),
                   pl.BlockSpec((1, SSM_WIDTH), const)],
        out_shape=[jax.ShapeDtypeStruct((ATTN_WIDTH, SEQ), F32), jax.ShapeDtypeStruct((SEQ, SSM_WIDTH), F32),
                   jax.ShapeDtypeStruct((SEQ, SSM_WIDTH), BF16), jax.ShapeDtypeStruct((SEQ, SSM_WIDTH), BF16),
                   jax.ShapeDtypeStruct((SEQ, D_MODEL), BF16),
                   jax.ShapeDtypeStruct((1, ATTN_WIDTH), F32), jax.ShapeDtypeStruct((1, SSM_WIDTH), F32),
                   jax.ShapeDtypeStruct((1, SSM_WIDTH), F32)],
        compiler_params=_cparams(("arbitrary",)),
    )(dx2, o, y, glu_w, glu_b, gan, gsn, wout)


def _local_step(x, target, w, p, late_weights, early_grads, after=None, midway=None):
    x1, h1, a1, b1 = _ffn_fwd(x, p["norm_ffn1"], w["wgt1"], w["wut1"], w["wd1"], "ffn1_fwd", after=after)
    h2, q, k, v, u = _mixin_fwd(x1, p["norm_mix"], w["wint"])
    o = _attn_fwd(q, k, v, p["attn_sinks"])

    lam_re = p["ssm_lambda_re"].reshape(2 * N_LANE_BLOCKS, LANES)
    lam_im = p["ssm_lambda_im"].reshape(2 * N_LANE_BLOCKS, LANES)
    log_dt = jnp.repeat(p["ssm_log_dt"].reshape(2, 32), 64, axis=-1).reshape(2 * N_LANE_BLOCKS, LANES)
    a_re, a_im, bb, cc = _ssm_prep(lam_re, lam_im, log_dt, p["ssm_b_re"], p["ssm_b_im"],
                                   p["ssm_c_re"], p["ssm_c_im"])
    shape_a = (2, N_LANE_BLOCKS, 1, LANES)
    a_re4, a_im4 = a_re.reshape(shape_a), a_im.reshape(shape_a)
    bb4 = bb.reshape(2, N_LANE_BLOCKS, LANES, 2 * LANES)
    cc4 = cc.reshape(2, N_LANE_BLOCKS, 2 * LANES, LANES)
    dskip = p["ssm_d"].T.reshape(1, SSM_WIDTH)
    y, xr, xi = _ssm_fwd(u, a_re4, a_im4, bb4, cc4, dskip, after=None if midway is None else midway(o))

    w2 = late_weights(y)
    x2, mixed = _mixout_fwd(o, y, w2["glu"], p["ssm_glu_b"], p["attn_out_norm"], p["ssm_out_norm"], w2["wout"], x1)
    dx3, h3, a3, b3, loss, d_final = _ffn_fwd(x2, p["norm_ffn2"], w2["wgt2"], w2["wut2"], w2["wd2"], "ffn2_fwd",
                                              head=(p["final_norm"], target))
    dx2, da3, db3, s3, df3, d_n2 = _ffn_bwd_act(dx3, x2, p["norm_ffn2"], a3, b3, w2["wgt2"], w2["wut2"], w2["wd2"],
                                                "ffn2_bwd_act")
    g_wgt2, g_wut2, g_wd2 = _mm_tn([(da3, h3), (db3, h3), (s3, df3)], "ffn2_bwd_w")

    do, dy, dz, ygb, dx2b, d_gan, d_gsn, d_glub = _mixout_bwd(
        dx2, o, y, w2["glu"], p["ssm_glu_b"], p["attn_out_norm"], p["ssm_out_norm"], w2["wout"])
    (g_wout,) = _mm_tn([(mixed, dx2b)], "wout_bwd_w")
    (g_glu,) = _mm_tn([(ygb, dz)], "glu_bwd_w")
    sent = early_grads(dict(glu=g_glu, wout=g_wout, wgt2=g_wgt2, wut2=g_wut2, wd2=g_wd2))

    du, d_dskip, dcc, dbb, dar, dai = _ssm_bwd(dy, u, xr, xi, a_re4, a_im4, bb4, cc4, dskip, after=sent)
    nb = 2 * N_LANE_BLOCKS
    g_lre, g_lim, g_ldt, g_btr, g_bti, g_cre, g_cim = _ssm_prep_bwd(
        lam_re, lam_im, log_dt, p["ssm_b_re"], p["ssm_b_im"], dar.reshape(nb, LANES), dai.reshape(nb, LANES),
        dbb.reshape(nb, LANES, 2 * LANES), dcc.reshape(nb, 2 * LANES, LANES))

    dq, dk, dv, d_sinks = _attn_bwd(q, k, v, p["attn_sinks"], do)
    dx1, dproj, d_nmix = _mixin_bwd(dq, dk, dv, du, w["wint"], x1, p["norm_mix"], dx2)
    (g_wint,) = _mm_tn([(dproj, h2)], "win_bwd_w")

    dx0, da1, db1, s1, df1, d_n1 = _ffn_bwd_act(dx1, x, p["norm_ffn1"], a1, b1, w["wgt1"], w["wut1"], w["wd1"],
                                                "ffn1_bwd_act")
    g_wgt1, g_wut1, g_wd1 = _mm_tn([(da1, h1), (db1, h1), (s1, df1)], "ffn1_bwd_w")

    big = dict(wgt1=g_wgt1, wut1=g_wut1, wd1=g_wd1, wint=g_wint)
    small = dict(
        norm_ffn1=d_n1, norm_mix=d_nmix, attn_sinks=d_sinks,
        ssm_lambda_re=g_lre.reshape(64, 64), ssm_lambda_im=g_lim.reshape(64, 64),
        ssm_log_dt=g_ldt.reshape(2, 32), ssm_b_re=g_btr, ssm_b_im=g_bti, ssm_c_re=g_cre, ssm_c_im=g_cim,
        ssm_d=d_dskip.reshape(32, 16).T, ssm_glu_b=d_glub, attn_out_norm=d_gan, ssm_out_norm=d_gsn,
        norm_ffn2=d_n2, final_norm=d_final, loss=loss)
    return loss, dx0, big, small


BIG = dict(
    wgt1=("ffn1_w_gate", 352, 1024, True), wut1=("ffn1_w_up", 352, 1024, True), wd1=("ffn1_w_down", 352, 1024, False),
    wint=("w_in", 160, 1024, True), glu=("ssm_glu_w", 64, 512, False), wout=("w_out", 128, 1024, False),
    wgt2=("ffn2_w_gate", 352, 1024, True), wut2=("ffn2_w_up", 352, 1024, True), wd2=("ffn2_w_down", 352, 1024, False))

SMALL = dict(
    norm_ffn1=(1, 1024), norm_mix=(1, 1024), attn_sinks=(1, 8), ssm_lambda_re=(64, 64), ssm_lambda_im=(64, 64),
    ssm_log_dt=(2, 32), ssm_b_re=(1024, 64), ssm_b_im=(1024, 64), ssm_c_re=(1024, 64), ssm_c_im=(1024, 64),
    ssm_d=(16, 32), ssm_glu_b=(1, 512), attn_out_norm=(1, 512), ssm_out_norm=(1, 512), norm_ffn2=(1, 1024),
    final_norm=(1, 1024), loss=(1, 128))
SMALL_TRANSPOSED = ("ssm_b_re", "ssm_b_im", "ssm_d")
SMALL_PARAMS = tuple(n for n in SMALL if n != "loss")

SMALL_PAIRS = (("ssm_lambda_re", "ssm_lambda_im"), ("ssm_c_re", "ssm_c_im"), ("ssm_b_re", "ssm_b_im"))
SMALL_VECS = ("norm_ffn1", "norm_mix", "norm_ffn2", "final_norm", "ssm_glu_b", "attn_out_norm", "ssm_out_norm")
SMALL_TILES = ("ssm_log_dt", "attn_sinks", "ssm_d", "loss")


def _small_offsets():
    off, table = 0, {}
    for re, im in SMALL_PAIRS:
        table[re] = table[im] = off
        off += SMALL[re][0]
    for n in SMALL_VECS:
        table[n] = off
        off += SMALL[n][1] // LANES
    for n in SMALL_TILES:
        off = -(-off // 8) * 8
        table[n] = off
        off += SMALL[n][0]
    return table, off


SMALL_OFFSET, SMALL_USED_ROWS = _small_offsets()
SMALL_ROWS = -(-SMALL_USED_ROWS // (8 * N_DEV)) * 8 * N_DEV


def _cast_shards(shards):
    names = list(BIG)

    def body(*refs):
        ins, outs = refs[:len(names)], refs[len(names):]
        for idx in range(len(names)):
            outs[idx][...] = ins[idx][...].astype(BF16)

    return pl.pallas_call(
        body, name="cast_shards",
        out_shape=[jax.ShapeDtypeStruct((BIG[n][1], BIG[n][2]), BF16) for n in names],
        compiler_params=_cparams(),
    )(*[shards[n] for n in names])


def _peer(x, y, c, r):
    px = 1 - x if r & 4 else x
    py = 1 - y if r & 2 else y
    pc = 1 - c if r & 1 else c
    return px, py, pc


FIRST_GROUP = ("wgt1", "wut1", "wd1", "wint")
LATE_GROUP = ("glu", "wout", "wgt2", "wut2", "wd2")
N_PEERS = N_DEV - 1
ANY_SPEC = pl.BlockSpec(memory_space=pl.ANY)
HBM_SPEC = pl.BlockSpec(memory_space=pltpu.HBM)
SEM_SPEC = pl.BlockSpec(memory_space=pltpu.SEMAPHORE)
DATAFLOW_EFFECT = pltpu.SideEffectType.DATAFLOW_SIDE_EFFECTING


def _mesh_pos():
    x, y, c = lax.axis_index("x"), lax.axis_index("y"), lax.axis_index("c")
    return x, y, c, 4 * x + 2 * y + c


def _gather_first(first, late):
    nf, nl = len(first), len(late)

    def body(*refs):
        f_in, l_in = refs[:nf], refs[nf:nf + nl]
        f_out, l_out = refs[nf + nl:2 * nf + nl], refs[2 * nf + nl:2 * (nf + nl)]
        send_sems, recv_sems, local_sems = refs[2 * (nf + nl):]
        x, y, c, me = _mesh_pos()
        sibling = (x, y, 1 - c)
        chips = [(x, 1 - y), (1 - x, y), (1 - x, 1 - y)]

        def idx(px, py, pc):
            return 4 * px + 2 * py + pc

        def copy(k, s, block, to, src=None):
            slot = f_out[k].at[block]
            return pltpu.make_async_remote_copy(
                src_ref=slot if src is None else src, dst_ref=slot, send_sem=send_sems.at[k, s],
                recv_sem=recv_sems.at[k, s], device_id=to, device_id_type=MESH_ID)

        local = []
        for k in range(nf + nl):
            src, dst = (f_in[k], f_out[k]) if k < nf else (l_in[k - nf], l_out[k - nf])
            mine = pltpu.make_async_copy(src, dst.at[me], local_sems.at[k])
            mine.start()
            local.append(mine)
        sends = []
        for j, chip in enumerate(chips):
            for k in range(nf):
                sends.append(copy(k, 1 + j, me, (*chip, c), src=f_in[k]))
                sends[-1].start()
        for k in range(nf):
            sends.append(copy(k, 0, me, sibling, src=f_in[k]))
            sends[-1].start()
        for j, chip in enumerate(chips):
            for k in range(nf):
                copy(k, 1 + j, idx(*chip, c), (*chip, c)).wait_recv()
                sends.append(copy(k, 4 + j, idx(*chip, c), sibling))
                sends[-1].start()
        for k in range(nf):
            copy(k, 0, idx(*sibling), sibling).wait_recv()
        for j, chip in enumerate(chips):
            for k in range(nf):
                copy(k, 4 + j, idx(*chip, 1 - c), sibling).wait_recv()
        for cp in sends:
            cp.wait_send()
        for cp in local:
            cp.wait()

    return pl.pallas_call(
        body, name="gather_first",
        in_specs=[ANY_SPEC] * (nf + nl), out_specs=[ANY_SPEC] * (nf + nl),
        out_shape=[jax.ShapeDtypeStruct((N_DEV,) + s.shape, s.dtype) for s in list(first) + list(late)],
        scratch_shapes=[pltpu.SemaphoreType.DMA((nf, N_PEERS)), pltpu.SemaphoreType.DMA((nf, N_PEERS)),
                        pltpu.SemaphoreType.DMA((nf + nl,))],
        compiler_params=pltpu.CompilerParams(has_side_effects=True),
    )(*first, *late)


def _split_copy(src_refs, land_refs, send_sems, recv_sems, k, r, pos, scatter, receiving):
    x, y, c, me = pos
    px, py, pc = _peer(x, y, c, r)
    peer_idx = 4 * px + 2 * py + pc
    if scatter:
        src, dst = src_refs[k].at[peer_idx], land_refs[k].at[r - 1]
    else:
        src, dst = src_refs[k], land_refs[k].at[peer_idx if receiving else me]
    return pltpu.make_async_remote_copy(
        src_ref=src, dst_ref=dst, send_sem=send_sems.at[k * N_PEERS + r - 1],
        recv_sem=recv_sems.at[k * N_PEERS + r - 1], device_id=(px, py, pc), device_id_type=MESH_ID)


def _split_start(name, srcs, lands, scatter):
    n = len(srcs)

    def body(*refs):
        src_refs, land_refs = refs[:n], refs[n:2 * n]
        send_sems, recv_sems = refs[2 * n], refs[2 * n + 1]
        token = refs[-1]
        pos = _mesh_pos()
        for k in range(n):
            for r in range(1, N_DEV):
                _split_copy(src_refs, land_refs, send_sems, recv_sems, k, r, pos, scatter, False).start()
        token[...] = jnp.zeros_like(token)

    thru = [pltpu.HBM(a.shape, a.dtype) for a in list(srcs) + list(lands)]
    outs = pl.pallas_call(
        body, name=name,
        in_specs=[HBM_SPEC] * (2 * n),
        out_specs=[SEM_SPEC, SEM_SPEC] + [HBM_SPEC] * (2 * n) + [pl.BlockSpec(memory_space=pltpu.VMEM)],
        out_shape=[pltpu.SemaphoreType.DMA((n * N_PEERS,)), pltpu.SemaphoreType.DMA((n * N_PEERS,))] + thru
        + [jax.ShapeDtypeStruct((8, LANES), F32)],
        input_output_aliases={i: 2 + i for i in range(2 * n)},
        compiler_params=pltpu.CompilerParams(has_side_effects=DATAFLOW_EFFECT),
    )(*[pltpu.with_memory_space_constraint(a, pltpu.HBM) for a in list(srcs) + list(lands)])
    return outs[0], outs[1], outs[2:2 + n], outs[2 + n:2 + 2 * n], outs[-1]


def _split_wait(name, send_sems, recv_sems, srcs, lands, scatter, after):
    n = len(srcs)

    def body(*refs):
        src_refs, land_refs = refs[:n], refs[n:2 * n]
        send, recv = refs[2 * n], refs[2 * n + 1]
        pos = _mesh_pos()
        for k in range(n):
            for r in range(1, N_DEV):
                cp = _split_copy(src_refs, land_refs, send, recv, k, r, pos, scatter, True)
                cp.wait_send()
                cp.wait_recv()

    thru = [pltpu.HBM(a.shape, a.dtype) for a in list(srcs) + list(lands)]
    outs = pl.pallas_call(
        body, name=name,
        in_specs=[HBM_SPEC] * (2 * n) + [SEM_SPEC, SEM_SPEC, ANY_SPEC],
        out_specs=[HBM_SPEC] * (2 * n), out_shape=thru,
        input_output_aliases={i: i for i in range(2 * n)},
        compiler_params=pltpu.CompilerParams(has_side_effects=DATAFLOW_EFFECT),
    )(*srcs, *lands, send_sems, recv_sems, after)
    return outs[:n], outs[n:]


def _late_copy(passing, src_refs, land_refs, send_sems, recv_sems, k, s, pos, receiving):
    x, y, c, me = pos
    chips = [(x, 1 - y), (1 - x, y), (1 - x, 1 - y)]
    sibling = (x, y, 1 - c)

    def idx(dev):
        return 4 * dev[0] + 2 * dev[1] + dev[2]

    if passing:
        to = sibling
        block = idx((*chips[s], 1 - c)) if receiving else idx((*chips[s], c))
        src = dst = land_refs[k].at[block]
        sem = k * 3 + s
    else:
        to = sibling if s == 0 else (*chips[s - 1], c)
        src, dst = src_refs[k], land_refs[k].at[idx(to) if receiving else me]
        sem = k * 4 + s
    return pltpu.make_async_remote_copy(src_ref=src, dst_ref=dst, send_sem=send_sems.at[sem],
                                        recv_sem=recv_sems.at[sem], device_id=to, device_id_type=MESH_ID)


def _late_gather_call(name, stage, srcs, lands, sems, after=None):
    n = len(srcs)
    n_sem_in = len(sems)
    has_after = after is not None

    def body(*refs):
        src_refs, land_refs = refs[:n], refs[n:2 * n]
        sem_in = refs[2 * n:2 * n + n_sem_in]
        outs = refs[2 * n + n_sem_in + (1 if has_after else 0):]
        pos = _mesh_pos()
        if stage == 0:
            own_send, own_recv = outs[0], outs[1]
            for s in (1, 2, 3, 0):
                for k in range(n):
                    _late_copy(False, src_refs, land_refs, own_send, own_recv, k, s, pos, False).start()
            outs[-1][...] = jnp.zeros_like(outs[-1])
        elif stage == 1:
            own_recv = sem_in[1]
            pass_send, pass_recv = outs[0], outs[1]
            for s in range(3):
                for k in range(n):
                    _late_copy(False, src_refs, land_refs, sem_in[0], own_recv, k, s + 1, pos, True).wait_recv()
                    _late_copy(True, src_refs, land_refs, pass_send, pass_recv, k, s, pos, False).start()
            outs[-1][...] = jnp.zeros_like(outs[-1])
        else:
            own_send, own_recv, pass_send, pass_recv = sem_in
            for k in range(n):
                _late_copy(False, src_refs, land_refs, own_send, own_recv, k, 0, pos, True).wait_recv()
                for s in range(4):
                    _late_copy(False, src_refs, land_refs, own_send, own_recv, k, s, pos, False).wait_send()
                for s in range(3):
                    cp = _late_copy(True, src_refs, land_refs, pass_send, pass_recv, k, s, pos, True)
                    cp.wait_recv()
                    cp.wait_send()

    thru = [pltpu.HBM(a.shape, a.dtype) for a in list(srcs) + list(lands)]
    new_sems = [[pltpu.SemaphoreType.DMA((n * 4,))] * 2, [pltpu.SemaphoreType.DMA((n * 3,))] * 2, []][stage]
    extra = [] if stage == 2 else [jax.ShapeDtypeStruct((8, LANES), F32)]
    outs = pl.pallas_call(
        body, name=name,
        in_specs=[HBM_SPEC] * (2 * n) + [SEM_SPEC] * n_sem_in + [ANY_SPEC] * has_after,
        out_specs=[SEM_SPEC] * len(new_sems) + [HBM_SPEC] * (2 * n) + [pl.BlockSpec(memory_space=pltpu.VMEM)] * len(extra),
        out_shape=new_sems + thru + extra,
        input_output_aliases={i: len(new_sems) + i for i in range(2 * n)},
        compiler_params=pltpu.CompilerParams(has_side_effects=DATAFLOW_EFFECT),
    )(*[pltpu.with_memory_space_constraint(a, pltpu.HBM) for a in list(srcs) + list(lands)], *sems,
      *([after] if has_after else []))
    ns = len(new_sems)
    return list(outs[:ns]), outs[ns:ns + n], outs[ns + n:ns + 2 * n], (outs[-1] if extra else None)


N_SEND_SLOTS = 3


def _exchange_last(grads, small_packed):
    ng = len(grads)
    ch = SMALL_ROWS // N_DEV
    max_rows = max(g.shape[1] for g in grads)
    cols = grads[0].shape[2]

    def body(*refs):
        g_in, s_in = refs[:ng], refs[ng]
        outs = refs[ng + 1:]
        own_out, land, stage = outs[:ng], outs[ng:2 * ng], outs[2 * ng:3 * ng]
        s_red, s_stage = outs[3 * ng], outs[3 * ng + 1]
        (va, vb, vo, vs, sm_in, sm_out, d2d_send, d2d_recv, ici_send, ici_recv, s1_send, s1_recv, s2_send, s2_recv,
         local_sems) = outs[3 * ng + 2:]
        x, y, c, me = _mesh_pos()
        sibling = (x, y, 1 - c)
        chips = [(x, y), (x, 1 - y), (1 - x, y), (1 - x, 1 - y)]

        def idx(chip, core):
            return 4 * chip[0] + 2 * chip[1] + core

        def d2d(k, j):
            return pltpu.make_async_remote_copy(
                src_ref=g_in[k].at[idx(chips[j], 1 - c)], dst_ref=stage[k].at[j], send_sem=d2d_send.at[k, j],
                recv_sem=d2d_recv.at[k, j], device_id=sibling, device_id_type=MESH_ID)

        def ici(k, j, slot):
            rows = g_in[k].shape[1]
            return pltpu.make_async_remote_copy(
                src_ref=vo.at[slot, pl.ds(0, rows)], dst_ref=land[k].at[j - 1], send_sem=ici_send.at[k, j - 1],
                recv_sem=ici_recv.at[k, j - 1], device_id=(*chips[j], c), device_id_type=MESH_ID)

        def small_scatter(r):
            px, py, pc = _peer(x, y, c, r)
            return pltpu.make_async_remote_copy(
                src_ref=s_in.at[pl.ds(pl.multiple_of((4 * px + 2 * py + pc) * ch, 8), ch)], dst_ref=s_stage.at[me],
                send_sem=s1_send.at[r - 1], recv_sem=s1_recv.at[r - 1], device_id=(px, py, pc), device_id_type=MESH_ID)

        def small_gather(r):
            return pltpu.make_async_remote_copy(
                src_ref=sm_out, dst_ref=s_red.at[me], send_sem=s2_send.at[r - 1], recv_sem=s2_recv.at[r - 1],
                device_id=_peer(x, y, c, r), device_id_type=MESH_ID)

        for r in range(1, N_DEV):
            small_scatter(r).start()
        mine = pltpu.make_async_copy(s_in.at[pl.ds(pl.multiple_of(me * ch, 8), ch)], s_stage.at[me], local_sems.at[0])
        mine.start()
        pairs = [(k, j) for k in range(ng) for j in (1, 2, 3)] + [(k, 0) for k in range(ng)]
        for k, j in pairs:
            d2d(k, j).start()

        for r in range(1, N_DEV):
            small_scatter(r).wait_recv()
        mine.wait()
        load = pltpu.make_async_copy(s_stage, sm_in, local_sems.at[1])
        load.start()
        load.wait()
        total = sm_in[0]
        for i in range(1, N_DEV):
            total = total + sm_in[i]
        sm_out[...] = total
        for r in range(1, N_DEV):
            small_gather(r).start()
        keep = pltpu.make_async_copy(sm_out, s_red.at[me], local_sems.at[2])
        keep.start()

        in_flight = {}
        for i, (k, j) in enumerate(pairs):
            slot = i % N_SEND_SLOTS
            rows = g_in[k].shape[1]
            if slot in in_flight:
                in_flight.pop(slot).wait_send()
            d2d(k, j).wait_recv()
            la = pltpu.make_async_copy(g_in[k].at[idx(chips[j], c)], va.at[pl.ds(0, rows)], local_sems.at[3])
            lb = pltpu.make_async_copy(stage[k].at[j], vb.at[pl.ds(0, rows)], local_sems.at[4])
            la.start()
            lb.start()
            la.wait()
            lb.wait()
            total = va[pl.ds(0, rows)].astype(F32) + vb[pl.ds(0, rows)].astype(F32)
            if j == 0:
                vs[pl.ds(0, rows)] = total
                st = pltpu.make_async_copy(vs.at[pl.ds(0, rows)], own_out[k], local_sems.at[5])
                st.start()
                st.wait()
            else:
                vo[slot, pl.ds(0, rows)] = total.astype(BF16)
                cp = ici(k, j, slot)
                cp.start()
                in_flight[slot] = cp
        for cp in in_flight.values():
            cp.wait_send()

        for j in (1, 2, 3, 0):
            for k in range(ng):
                d2d(k, j).wait_send()
        for j in (1, 2, 3):
            for k in range(ng):
                ici(k, j, 0).wait_recv()
        for r in range(1, N_DEV):
            small_scatter(r).wait_send()
            small_gather(r).wait_send()
            small_gather(r).wait_recv()
        keep.wait()

    out_shape = [jax.ShapeDtypeStruct(g.shape[1:], F32) for g in grads]
    out_shape += [jax.ShapeDtypeStruct((3,) + g.shape[1:], BF16) for g in grads]
    out_shape += [jax.ShapeDtypeStruct((4,) + g.shape[1:], BF16) for g in grads]
    out_shape += [jax.ShapeDtypeStruct((N_DEV, ch, LANES), F32), jax.ShapeDtypeStruct((N_DEV, ch, LANES), F32)]
    outs = pl.pallas_call(
        body, name="exchange_last",
        in_specs=[ANY_SPEC] * (ng + 1), out_specs=[ANY_SPEC] * len(out_shape), out_shape=out_shape,
        scratch_shapes=[pltpu.VMEM((max_rows, cols), BF16), pltpu.VMEM((max_rows, cols), BF16),
                        pltpu.VMEM((N_SEND_SLOTS, max_rows, cols), BF16), pltpu.VMEM((max_rows, cols), F32),
                        pltpu.VMEM((N_DEV, ch, LANES), F32), pltpu.VMEM((ch, LANES), F32),
                        pltpu.SemaphoreType.DMA((ng, 4)), pltpu.SemaphoreType.DMA((ng, 4)),
                        pltpu.SemaphoreType.DMA((ng, 3)), pltpu.SemaphoreType.DMA((ng, 3)),
                        pltpu.SemaphoreType.DMA((N_PEERS,)), pltpu.SemaphoreType.DMA((N_PEERS,)),
                        pltpu.SemaphoreType.DMA((N_PEERS,)), pltpu.SemaphoreType.DMA((N_PEERS,)),
                        pltpu.SemaphoreType.DMA((6,))],
        compiler_params=pltpu.CompilerParams(has_side_effects=True, vmem_limit_bytes=VMEM_LIMIT),
    )(*grads, small_packed)
    return outs[:ng], outs[ng:2 * ng], outs[3 * ng].reshape(SMALL_ROWS, LANES)


def _adamw_math(w, g, m, v):
    m2 = ADAM_B1 * m + (1.0 - ADAM_B1) * g
    v2 = ADAM_B2 * v + (1.0 - ADAM_B2) * (g * g)
    m_hat = m2 / (1.0 - ADAM_B1 ** ADAM_STEP)
    v_hat = v2 / (1.0 - ADAM_B2 ** ADAM_STEP)
    delta = -ADAM_LR * (m_hat / (jnp.sqrt(v_hat) + ADAM_EPS) + ADAM_WD * w)
    return delta, m2, v2


ADAM_ROW_TILES = 2


def _adamw_big(own, parts, w, m, v, name):
    shape = w.shape
    own_is_blocks = own.ndim == 3
    tr = shape[0] // ADAM_ROW_TILES
    n_parts = parts.shape[0]

    def body(own_ref, p_ref, w_ref, m_ref, v_ref, g_ref, d_ref, m2_ref, v2_ref, own_s, sem):
        rows = pl.ds(pl.multiple_of(pl.program_id(0) * tr, 16), tr)
        if own_is_blocks:
            cp = pltpu.make_async_copy(own_ref.at[_mesh_pos()[3], rows], own_s, sem)
        else:
            cp = pltpu.make_async_copy(own_ref.at[rows], own_s, sem)
        cp.start()
        cp.wait()
        g = own_s[...].astype(F32)
        for i in range(n_parts):
            g = g + p_ref[i].astype(F32)
        delta, m2, v2 = _adamw_math(w_ref[...], g, m_ref[...], v_ref[...])
        g_ref[...] = g
        d_ref[...] = delta
        m2_ref[...] = m2
        v2_ref[...] = v2

    tile = pl.BlockSpec((tr, shape[1]), lambda i: (i, 0))
    return pl.pallas_call(
        body, name=name, grid=(ADAM_ROW_TILES,),
        in_specs=[ANY_SPEC, pl.BlockSpec((n_parts, tr, shape[1]), lambda i: (0, i, 0)), tile, tile, tile],
        out_specs=[tile] * 4, out_shape=[jax.ShapeDtypeStruct(shape, F32)] * 4,
        scratch_shapes=[pltpu.VMEM((tr, shape[1]), own.dtype), pltpu.SemaphoreType.DMA(())],
        compiler_params=_cparams(("arbitrary",)),
    )(own, parts, w, m, v)


def _pack_small(grads):
    names = list(SMALL)

    def body(*refs):
        ins, out = dict(zip(names, refs[:-1])), refs[-1]
        out[...] = jnp.zeros_like(out)
        for re, im in SMALL_PAIRS:
            off, rows = SMALL_OFFSET[re], SMALL[re][0]
            out[off:off + rows, :] = jnp.concatenate([ins[re][...], ins[im][...]], axis=1)
        for n in SMALL_VECS:
            off, vec = SMALL_OFFSET[n], ins[n][...]
            for i in range(SMALL[n][1] // LANES):
                out[off + i:off + i + 1, :] = vec[:, i * LANES:(i + 1) * LANES]
        for n in SMALL_TILES:
            off, (rows, cols) = SMALL_OFFSET[n], SMALL[n]
            out[off:off + rows, 0:cols] = ins[n][...]

    return pl.pallas_call(
        body, name="pack_small", out_shape=jax.ShapeDtypeStruct((SMALL_ROWS, LANES), F32),
        compiler_params=_cparams(),
    )(*[grads[n] for n in names])


def _unpack_small_ref(g_ref, n):
    off, (rows, cols) = SMALL_OFFSET[n], SMALL[n]
    for re, im in SMALL_PAIRS:
        if n == re:
            return g_ref[off:off + rows, 0:HALF_LANES]
        if n == im:
            return g_ref[off:off + rows, HALF_LANES:LANES]
    if n in SMALL_VECS:
        return jnp.concatenate([g_ref[off + i:off + i + 1, :] for i in range(cols // LANES)], axis=1)
    return g_ref[off:off + rows, 0:cols]


def _adamw_small(g_packed, w, m, v):
    names = list(SMALL_PARAMS)
    n = len(names)

    def body(g_ref, *refs):
        w_refs, m_refs, v_refs, outs = refs[:n], refs[n:2 * n], refs[2 * n:3 * n], refs[3 * n:]
        for idx, name in enumerate(names):
            g = _unpack_small_ref(g_ref, name)
            delta, m2, v2 = _adamw_math(w_refs[idx][...], g, m_refs[idx][...], v_refs[idx][...])
            outs[4 * idx][...] = g
            outs[4 * idx + 1][...] = delta
            outs[4 * idx + 2][...] = m2
            outs[4 * idx + 3][...] = v2
        outs[4 * n][...] = _unpack_small_ref(g_ref, "loss")

    outs = pl.pallas_call(
        body, name="adamw_small",
        out_shape=[jax.ShapeDtypeStruct(SMALL[name], F32) for name in names for _ in range(4)]
        + [jax.ShapeDtypeStruct(SMALL["loss"], F32)],
        compiler_params=_cparams(),
    )(g_packed, *[w[k] for k in names], *[m[k] for k in names], *[v[k] for k in names])
    return {name: outs[4 * idx:4 * idx + 4] for idx, name in enumerate(names)}, outs[4 * n]


WEIGHT_NAMES = ['norm_ffn1', 'ffn1_w_gate', 'ffn1_w_up', 'ffn1_w_down', 'norm_mix', 'w_in', 'attn_sinks',
                'ssm_lambda_re', 'ssm_lambda_im', 'ssm_log_dt', 'ssm_b_re', 'ssm_b_im', 'ssm_c_re', 'ssm_c_im',
                'ssm_d', 'ssm_glu_w', 'ssm_glu_b', 'attn_out_norm', 'ssm_out_norm', 'w_out', 'norm_ffn2',
                'ffn2_w_gate', 'ffn2_w_up', 'ffn2_w_down', 'final_norm']


def kernel(x, norm_ffn1, ffn1_w_gate, ffn1_w_up, ffn1_w_down, norm_mix, w_in, attn_sinks, ssm_lambda_re, ssm_lambda_im, ssm_log_dt, ssm_b_re, ssm_b_im, ssm_c_re, ssm_c_im, ssm_d, ssm_glu_w, ssm_glu_b, attn_out_norm, ssm_out_norm, w_out, norm_ffn2, ffn2_w_gate, ffn2_w_up, ffn2_w_down, final_norm, loss_target, m_norm_ffn1, m_ffn1_w_gate, m_ffn1_w_up, m_ffn1_w_down, m_norm_mix, m_w_in, m_attn_sinks, m_ssm_lambda_re, m_ssm_lambda_im, m_ssm_log_dt, m_ssm_b_re, m_ssm_b_im, m_ssm_c_re, m_ssm_c_im, m_ssm_d, m_ssm_glu_w, m_ssm_glu_b, m_attn_out_norm, m_ssm_out_norm, m_w_out, m_norm_ffn2, m_ffn2_w_gate, m_ffn2_w_up, m_ffn2_w_down, m_final_norm, v_norm_ffn1, v_ffn1_w_gate, v_ffn1_w_up, v_ffn1_w_down, v_norm_mix, v_w_in, v_attn_sinks, v_ssm_lambda_re, v_ssm_lambda_im, v_ssm_log_dt, v_ssm_b_re, v_ssm_b_im, v_ssm_c_re, v_ssm_c_im, v_ssm_d, v_ssm_glu_w, v_ssm_glu_b, v_attn_out_norm, v_ssm_out_norm, v_w_out, v_norm_ffn2, v_ffn2_w_gate, v_ffn2_w_up, v_ffn2_w_down, v_final_norm):
    args = dict(locals())
    weights = {n: args[n] for n in WEIGHT_NAMES}
    moms = {n: args["m_" + n] for n in WEIGHT_NAMES}
    vars_ = {n: args["v_" + n] for n in WEIGHT_NAMES}

    def shard2d(a, k):
        a = a.reshape(a.shape[-2], a.shape[-1])
        return a.T if BIG[k][3] else a

    def shard_master(a, k):
        return (a.T if BIG[k][3] else a).reshape(weights[BIG[k][0]].shape)

    def blocks(g, k):
        return g.reshape(N_DEV, BIG[k][1], BIG[k][2])

    def full(g, k):
        return g.reshape(N_DEV * BIG[k][1], BIG[k][2])

    shards = dict(zip(BIG, _cast_shards({k: shard2d(weights[BIG[k][0]], k) for k in BIG})))
    nf = len(FIRST_GROUP)
    got = _gather_first([shards[k] for k in FIRST_GROUP], [shards[k] for k in LATE_GROUP])
    w_first = {k: full(g, k) for k, g in zip(FIRST_GROUP, got[:nf])}
    late = {}
    late["own_sems"], late["srcs"], late["lands"], w_token = _late_gather_call(
        "gather_late_start", 0, [shards[k] for k in LATE_GROUP], got[nf:], [])

    def late_pass(dep):
        late["pass_sems"], late["srcs"], late["lands"], token = _late_gather_call(
            "gather_late_pass", 1, late["srcs"], late["lands"], late["own_sems"], after=dep)
        return token

    def late_weights(dep):
        _, _, lands, _ = _late_gather_call("gather_late_wait", 2, late["srcs"], late["lands"],
                                           late["own_sems"] + late["pass_sems"], after=dep)
        return {k: full(g, k) for k, g in zip(LATE_GROUP, lands)}

    early = {}

    def early_grads(g):
        srcs = [blocks(g[k], k) for k in LATE_GROUP]
        lands = [lax.empty((N_PEERS, BIG[k][1], BIG[k][2]), BF16) for k in LATE_GROUP]
        early["send"], early["recv"], early["srcs"], early["lands"], token = _split_start(
            "grads_late_start", srcs, lands, scatter=True)
        return token

    def small2d(a, n):
        if n in SMALL_TRANSPOSED:
            a = jnp.swapaxes(a, -1, -2)
        return a.reshape(SMALL[n])

    def small_master(a, n):
        if n in SMALL_TRANSPOSED:
            shape = weights[n].shape
            return jnp.swapaxes(a.reshape(shape[:-2] + (shape[-1], shape[-2])), -1, -2)
        return a.reshape(weights[n].shape)

    small_p = {n: small2d(weights[n], n) for n in SMALL_PARAMS}
    _, grad_x, g_first, g_small = _local_step(
        x.reshape(SEQ, D_MODEL), loss_target.reshape(SEQ, D_MODEL), w_first, small_p, late_weights, early_grads,
        after=w_token, midway=late_pass)

    own_sums, first_parts, small_grad = _exchange_last([blocks(g_first[k], k) for k in FIRST_GROUP],
                                                       _pack_small(g_small))
    own_late, late_parts = _split_wait("grads_late_wait", early["send"], early["recv"], early["srcs"],
                                       early["lands"], True, small_grad)
    own = dict(zip(FIRST_GROUP + LATE_GROUP, list(own_sums) + list(own_late)))
    parts = dict(zip(FIRST_GROUP + LATE_GROUP, list(first_parts) + list(late_parts)))
    outs = {}
    for k in BIG:
        n = BIG[k][0]
        outs[n] = [shard_master(o, k) for o in
                   _adamw_big(own[k], parts[k], shard2d(weights[n], k), shard2d(moms[n], k), shard2d(vars_[n], k),
                              "adamw_" + n)]
    small_out, loss_row = _adamw_small(small_grad, small_p, {n: small2d(moms[n], n) for n in SMALL_PARAMS},
                                       {n: small2d(vars_[n], n) for n in SMALL_PARAMS})
    for n in SMALL_PARAMS:
        outs[n] = [small_master(o, n) for o in small_out[n]]

    result = [loss_row[0, 0], grad_x.reshape(x.shape)]
    for i in range(4):
        result += [outs[n][i] for n in WEIGHT_NAMES]
    return tuple(result)
```

```python
import functools

import jax
import jax.numpy as jnp
from jax import lax
from jax.experimental import pallas as pl
from jax.experimental.pallas import tpu as pltpu

F32 = jnp.float32
BF16 = jnp.bfloat16

N_DEV = 8
SEQ = 2048
D_MODEL = 1024
D_FF = 2816
ATTN_HEADS = 8
KV_HEADS = 2
HEAD_DIM = 64
ATTN_WIDTH = 512
KV_WIDTH = 128
WINDOW = 128
SSM_WIDTH = 512
IN_WIDTH = 1280
EPS = 1e-6
MASKED_DISTANCE = 1e33
LAMBDA_RE_MAX = -1e-4
LANES = 128
N_LANE_BLOCKS = 16
SCAN_CHUNK = SEQ // 8

ADAM_LR = 0.001
ADAM_B1 = 0.9
ADAM_B2 = 0.999
ADAM_EPS = 1e-08
ADAM_WD = 0.01
ADAM_STEP = 10

VMEM_LIMIT = 60 * 1024 * 1024
MESH_ID = pl.DeviceIdType.MESH


def _cparams(sem=None):
    return pltpu.CompilerParams(dimension_semantics=sem, vmem_limit_bytes=VMEM_LIMIT)


def _dot(a, b):
    return jnp.dot(a, b, preferred_element_type=F32)


def _dot_nt(a, b):
    return lax.dot_general(a, b, (((1,), (1,)), ((), ())), preferred_element_type=F32)


def _dot_tn(a, b):
    return lax.dot_general(a, b, (((0,), (0,)), ((), ())), preferred_element_type=F32)


def _rms_fwd(x, g):
    r = lax.rsqrt(jnp.mean(x * x, axis=-1, keepdims=True) + EPS)
    return x * r * g


def _rms_bwd(dh, x, g):
    r = lax.rsqrt(jnp.mean(x * x, axis=-1, keepdims=True) + EPS)
    xh = x * r
    dg = jnp.sum(dh * xh, axis=0, keepdims=True)
    dxh = dh * g
    dx = r * (dxh - xh * jnp.mean(dxh * xh, axis=-1, keepdims=True))
    return dx, dg


def _sigmoid(x):
    return 1.0 / (1.0 + jnp.exp(-x))


FFN_TM = 512
FFN_TF = 1408


def _ffn_fwd(x, g, wgt, wut, wd, name, after=None, head=None):
    tm, tf = FFN_TM, FFN_TF
    nj = D_FF // tf
    deps = [] if after is None else [after]
    n_in = len(deps) + (2 if head else 0)

    def body(x_ref, g_ref, wg_ref, wu_ref, wd_ref, *rest):
        i = pl.program_id(0)
        j = pl.program_id(1)
        if head:
            gf_ref, t_ref = rest[len(deps):n_in]
            xo_ref, h_ref, a_ref, b_ref, loss_ref, dgf_ref, h_s, acc = rest[n_in:]
        else:
            xo_ref, h_ref, a_ref, b_ref, h_s, acc = rest[n_in:]

        @pl.when(j == 0)
        def _():
            h = _rms_fwd(x_ref[...], g_ref[...]).astype(BF16)
            h_s[...] = h
            h_ref[...] = h
            acc[...] = jnp.zeros_like(acc)

        h = h_s[...]
        a = _dot_nt(h, wg_ref[...])
        b = _dot_nt(h, wu_ref[...])
        a_ref[...] = a.astype(BF16)
        b_ref[...] = b.astype(BF16)
        s = (a * _sigmoid(a) * b).astype(BF16)
        acc[...] += _dot(s, wd_ref[...])

        @pl.when(j == nj - 1)
        def _():
            xo = x_ref[...] + 0.5 * acc[...]
            if not head:
                xo_ref[...] = xo
                return
            gf = gf_ref[...]
            err = _rms_fwd(xo, gf) - t_ref[...]
            part = jnp.broadcast_to(0.5 * jnp.sum(err * err) / D_MODEL, (1, LANES))
            dx, dgf = _rms_bwd(err * (1.0 / D_MODEL), xo, gf)
            xo_ref[...] = dx

            @pl.when(i == 0)
            def _():
                loss_ref[...] = part
                dgf_ref[...] = dgf

            @pl.when(i != 0)
            def _():
                loss_ref[...] += part
                dgf_ref[...] += dgf

    row = lambda i, j: (i, 0)
    const = lambda i, j: (0, 0)
    head_in = [pl.BlockSpec((1, D_MODEL), const), pl.BlockSpec((tm, D_MODEL), row)] if head else []
    head_out = [pl.BlockSpec((1, LANES), const), pl.BlockSpec((1, D_MODEL), const)] if head else []
    head_shape = [jax.ShapeDtypeStruct((1, LANES), F32), jax.ShapeDtypeStruct((1, D_MODEL), F32)] if head else []
    return pl.pallas_call(
        body, name=name, grid=(SEQ // tm, nj),
        in_specs=[pl.BlockSpec((tm, D_MODEL), row), pl.BlockSpec((1, D_MODEL), const),
                  pl.BlockSpec((tf, D_MODEL), lambda i, j: (j, 0)),
                  pl.BlockSpec((tf, D_MODEL), lambda i, j: (j, 0)),
                  pl.BlockSpec((tf, D_MODEL), lambda i, j: (j, 0))] + [pl.BlockSpec(memory_space=pl.ANY)] * len(deps)
        + head_in,
        out_specs=[pl.BlockSpec((tm, D_MODEL), row), pl.BlockSpec((tm, D_MODEL), row),
                   pl.BlockSpec((tm, tf), lambda i, j: (i, j)),
                   pl.BlockSpec((tm, tf), lambda i, j: (i, j))] + head_out,
        out_shape=[jax.ShapeDtypeStruct((SEQ, D_MODEL), F32), jax.ShapeDtypeStruct((SEQ, D_MODEL), BF16),
                   jax.ShapeDtypeStruct((SEQ, D_FF), BF16), jax.ShapeDtypeStruct((SEQ, D_FF), BF16)] + head_shape,
        scratch_shapes=[pltpu.VMEM((tm, D_MODEL), BF16), pltpu.VMEM((tm, D_MODEL), F32)],
        compiler_params=_cparams(("arbitrary" if head else "parallel", "arbitrary")),
    )(x, g, wgt, wut, wd, *deps, *(head or ()))


def _ffn_bwd_act(dxo, x, g, a, b, wgt, wut, wd, name):
    tm, tf = FFN_TM, FFN_TF
    nj = D_FF // tf

    def body(dxo_ref, x_ref, g_ref, a_ref, b_ref, wg_ref, wu_ref, wd_ref,
             dx_ref, da_ref, db_ref, s_ref, df_ref, dg_ref, df_s, acc):
        i = pl.program_id(0)
        j = pl.program_id(1)

        @pl.when(j == 0)
        def _():
            df = (0.5 * dxo_ref[...]).astype(BF16)
            df_s[...] = df
            df_ref[...] = df
            acc[...] = jnp.zeros_like(acc)

        ds = _dot_nt(df_s[...], wd_ref[...])
        av = a_ref[...].astype(F32)
        bv = b_ref[...].astype(F32)
        sig = _sigmoid(av)
        sl = av * sig
        s_ref[...] = (sl * bv).astype(BF16)
        db = (ds * sl).astype(BF16)
        da = (ds * bv * (sig * (1.0 + av * (1.0 - sig)))).astype(BF16)
        da_ref[...] = da
        db_ref[...] = db
        acc[...] += _dot(da, wg_ref[...]) + _dot(db, wu_ref[...])

        @pl.when(j == nj - 1)
        def _():
            dx, dg = _rms_bwd(acc[...], x_ref[...], g_ref[...])
            dx_ref[...] = dxo_ref[...] + dx

            @pl.when(i == 0)
            def _():
                dg_ref[...] = dg

            @pl.when(i != 0)
            def _():
                dg_ref[...] += dg

    row = lambda i, j: (i, 0)
    col = lambda i, j: (j, 0)
    tile = lambda i, j: (i, j)
    return pl.pallas_call(
        body, name=name, grid=(SEQ // tm, nj),
        in_specs=[pl.BlockSpec((tm, D_MODEL), row), pl.BlockSpec((tm, D_MODEL), row),
                  pl.BlockSpec((1, D_MODEL), lambda i, j: (0, 0)),
                  pl.BlockSpec((tm, tf), tile), pl.BlockSpec((tm, tf), tile),
                  pl.BlockSpec((tf, D_MODEL), col), pl.BlockSpec((tf, D_MODEL), col), pl.BlockSpec((tf, D_MODEL), col)],
        out_specs=[pl.BlockSpec((tm, D_MODEL), row),
                   pl.BlockSpec((tm, tf), tile), pl.BlockSpec((tm, tf), tile), pl.BlockSpec((tm, tf), tile),
                   pl.BlockSpec((tm, D_MODEL), row),
                   pl.BlockSpec((1, D_MODEL), lambda i, j: (0, 0))],
        out_shape=[jax.ShapeDtypeStruct((SEQ, D_MODEL), F32),
                   jax.ShapeDtypeStruct((SEQ, D_FF), BF16), jax.ShapeDtypeStruct((SEQ, D_FF), BF16),
                   jax.ShapeDtypeStruct((SEQ, D_FF), BF16),
                   jax.ShapeDtypeStruct((SEQ, D_MODEL), BF16),
                   jax.ShapeDtypeStruct((1, D_MODEL), F32)],
        scratch_shapes=[pltpu.VMEM((tm, D_MODEL), BF16), pltpu.VMEM((tm, D_MODEL), F32)],
        compiler_params=_cparams(("arbitrary", "arbitrary")),
    )(dxo, x, g, a, b, wgt, wut, wd)


def _mm_tn(pairs, name, tmm=256):
    m = pairs[0][0].shape[1]
    n_pairs = len(pairs)

    def body(*refs):
        ins, outs = refs[:2 * n_pairs], refs[2 * n_pairs:]
        for p in range(n_pairs):
            outs[p][...] = _dot_tn(ins[2 * p][...], ins[2 * p + 1][...]).astype(BF16)

    in_specs, out_specs, out_shape, args = [], [], [], []
    for a, b in pairs:
        n = b.shape[1]
        in_specs += [pl.BlockSpec((SEQ, tmm), lambda i: (0, i)), pl.BlockSpec((SEQ, n), lambda i: (0, 0))]
        out_specs.append(pl.BlockSpec((tmm, n), lambda i: (i, 0)))
        out_shape.append(jax.ShapeDtypeStruct((m, n), BF16))
        args += [a, b]
    return pl.pallas_call(body, name=name, grid=(m // tmm,), in_specs=in_specs, out_specs=out_specs,
                          out_shape=out_shape, compiler_params=_cparams(("parallel",)))(*args)


MIX_TM = 256


def _mixin_fwd(x, g, wint):
    tm = MIX_TM

    def body(x_ref, g_ref, w_ref, h_ref, q_ref, k_ref, v_ref, u_ref):
        h = _rms_fwd(x_ref[...], g_ref[...]).astype(BF16)
        h_ref[...] = h
        proj = _dot_nt(h, w_ref[...])
        q_ref[...] = proj[:, :ATTN_WIDTH].T
        k_ref[...] = proj[:, ATTN_WIDTH:ATTN_WIDTH + KV_WIDTH]
        v_ref[...] = proj[:, ATTN_WIDTH + KV_WIDTH:ATTN_WIDTH + 2 * KV_WIDTH]
        u_ref[...] = proj[:, ATTN_WIDTH + 2 * KV_WIDTH:]

    row = lambda i: (i, 0)
    return pl.pallas_call(
        body, name="mixin_fwd", grid=(SEQ // tm,),
        in_specs=[pl.BlockSpec((tm, D_MODEL), row), pl.BlockSpec((1, D_MODEL), lambda i: (0, 0)),
                  pl.BlockSpec((IN_WIDTH, D_MODEL), lambda i: (0, 0))],
        out_specs=[pl.BlockSpec((tm, D_MODEL), row), pl.BlockSpec((ATTN_WIDTH, tm), lambda i: (0, i)),
                   pl.BlockSpec((tm, KV_WIDTH), row), pl.BlockSpec((tm, KV_WIDTH), row),
                   pl.BlockSpec((tm, SSM_WIDTH), row)],
        out_shape=[jax.ShapeDtypeStruct((SEQ, D_MODEL), BF16), jax.ShapeDtypeStruct((ATTN_WIDTH, SEQ), F32),
                   jax.ShapeDtypeStruct((SEQ, KV_WIDTH), F32), jax.ShapeDtypeStruct((SEQ, KV_WIDTH), F32),
                   jax.ShapeDtypeStruct((SEQ, SSM_WIDTH), F32)],
        compiler_params=_cparams(("parallel",)),
    )(x, g, wint)


def _mixin_bwd(dqt, dk, dv, du, wint, x, g, dres):
    tm = MIX_TM

    def body(dq_ref, dk_ref, dv_ref, du_ref, w_ref, x_ref, g_ref, dres_ref, dx_ref, dp_ref, dg_ref):
        i = pl.program_id(0)
        dp = jnp.concatenate([dq_ref[...].T, dk_ref[...], dv_ref[...], du_ref[...]], axis=-1).astype(BF16)
        dp_ref[...] = dp
        dh = _dot(dp, w_ref[...])
        dx, dg = _rms_bwd(dh, x_ref[...], g_ref[...])
        dx_ref[...] = dres_ref[...] + dx

        @pl.when(i == 0)
        def _():
            dg_ref[...] = dg

        @pl.when(i != 0)
        def _():
            dg_ref[...] += dg

    row = lambda i: (i, 0)
    const = lambda i: (0, 0)
    return pl.pallas_call(
        body, name="mixin_bwd", grid=(SEQ // tm,),
        in_specs=[pl.BlockSpec((ATTN_WIDTH, tm), lambda i: (0, i)), pl.BlockSpec((tm, KV_WIDTH), row),
                  pl.BlockSpec((tm, KV_WIDTH), row), pl.BlockSpec((tm, SSM_WIDTH), row),
                  pl.BlockSpec((IN_WIDTH, D_MODEL), const), pl.BlockSpec((tm, D_MODEL), row),
                  pl.BlockSpec((1, D_MODEL), const), pl.BlockSpec((tm, D_MODEL), row)],
        out_specs=[pl.BlockSpec((tm, D_MODEL), row), pl.BlockSpec((tm, IN_WIDTH), row),
                   pl.BlockSpec((1, D_MODEL), const)],
        out_shape=[jax.ShapeDtypeStruct((SEQ, D_MODEL), F32), jax.ShapeDtypeStruct((SEQ, IN_WIDTH), BF16),
                   jax.ShapeDtypeStruct((1, D_MODEL), F32)],
        compiler_params=_cparams(("arbitrary",)),
    )(dqt, dk, dv, du, wint, x, g, dres)


N_QBLOCKS = SEQ // WINDOW
GROUP = ATTN_HEADS // KV_HEADS
SCALE = HEAD_DIM ** -0.5


def _alibi_slope(h):
    return 2.0 ** (-8.0 * (h + 1) / ATTN_HEADS)


def _window_masks(n):
    s_idx = lax.broadcasted_iota(jnp.int32, (3 * WINDOW, WINDOW), 0)
    t_idx = lax.broadcasted_iota(jnp.int32, (3 * WINDOW, WINDOW), 1)
    absrel = jnp.abs(s_idx - WINDOW - t_idx)
    key_pos = n * WINDOW - WINDOW + s_idx
    valid = (absrel <= WINDOW) & (key_pos >= 0) & (key_pos < SEQ)
    return jnp.where(valid, absrel.astype(F32), MASKED_DISTANCE)


def _group_cols(ref, r0, gi):
    return jnp.concatenate(
        [ref[(gi * GROUP + hh) * HEAD_DIM:(gi * GROUP + hh + 1) * HEAD_DIM, pl.ds(r0, WINDOW)].astype(BF16)
         for hh in range(GROUP)], axis=1)


def _group_probs(qgt, kw, dist, gi, sk_ref):
    bias = jnp.concatenate([-_alibi_slope(gi * GROUP + hh) * dist for hh in range(GROUP)], axis=1)
    sink = jnp.concatenate([jnp.full((1, WINDOW), sk_ref[0, gi * GROUP + hh], F32) for hh in range(GROUP)], axis=1)
    s = _dot(kw, qgt) * SCALE + bias
    m = jnp.maximum(jnp.max(s, axis=0, keepdims=True), sink)
    p = jnp.exp(s - m)
    ps = jnp.exp(sink - m)
    inv = 1.0 / (jnp.sum(p, axis=0, keepdims=True) + ps)
    return p * inv, ps * inv


def _pad_window(src_ref, dst_ref):
    zeros = jnp.zeros((WINDOW, KV_WIDTH), BF16)
    dst_ref[0:WINDOW, :] = zeros
    dst_ref[WINDOW + SEQ:, :] = zeros
    dst_ref[WINDOW:WINDOW + SEQ, :] = src_ref[...].astype(BF16)


def _attn_fwd(qt, k, v, sinks, after=None):
    deps = [] if after is None else [after]

    def body(sk_ref, qt_ref, k_ref, v_ref, *rest):
        o_ref, kp_ref, vp_ref = rest[len(deps):]
        _pad_window(k_ref, kp_ref)
        _pad_window(v_ref, vp_ref)

        def blk(n, carry):
            r0 = pl.multiple_of(n * WINDOW, WINDOW)
            dist = _window_masks(n)
            for gi in range(KV_HEADS):
                kw = kp_ref[pl.ds(r0, 3 * WINDOW), gi * HEAD_DIM:(gi + 1) * HEAD_DIM]
                vw = vp_ref[pl.ds(r0, 3 * WINDOW), gi * HEAD_DIM:(gi + 1) * HEAD_DIM]
                pr, _ = _group_probs(_group_cols(qt_ref, r0, gi), kw, dist, gi, sk_ref)
                og = _dot_tn(pr.astype(BF16), vw)
                for hh in range(GROUP):
                    h = gi * GROUP + hh
                    o_ref[pl.ds(r0, WINDOW), h * HEAD_DIM:(h + 1) * HEAD_DIM] = og[hh * WINDOW:(hh + 1) * WINDOW]
            return carry

        lax.fori_loop(0, N_QBLOCKS, blk, 0)

    vmem = pl.BlockSpec(memory_space=pltpu.VMEM)
    return pl.pallas_call(
        body, name="attn_fwd",
        in_specs=[pl.BlockSpec(memory_space=pltpu.SMEM), vmem, vmem, vmem]
        + [pl.BlockSpec(memory_space=pl.ANY)] * len(deps), out_specs=vmem,
        out_shape=jax.ShapeDtypeStruct((SEQ, ATTN_WIDTH), F32),
        scratch_shapes=[pltpu.VMEM((SEQ + 2 * WINDOW, KV_WIDTH), BF16)] * 2,
        compiler_params=_cparams(),
    )(sinks, qt, k, v, *deps)


def _attn_bwd(qt, k, v, sinks, dot_):
    def body(sk_ref, qt_ref, k_ref, v_ref, dot_ref, dqt_ref, dk_ref, dv_ref, dsk_ref,
             dsk_acc, kp_ref, vp_ref, dkp_ref, dvp_ref):
        _pad_window(k_ref, kp_ref)
        _pad_window(v_ref, vp_ref)
        dkp_ref[...] = jnp.zeros_like(dkp_ref)
        dvp_ref[...] = jnp.zeros_like(dvp_ref)
        dsk_acc[...] = jnp.zeros_like(dsk_acc)

        def blk(n, carry):
            r0 = pl.multiple_of(n * WINDOW, WINDOW)
            dist = _window_masks(n)
            for gi in range(KV_HEADS):
                gcols = slice(gi * HEAD_DIM, (gi + 1) * HEAD_DIM)
                kw = kp_ref[pl.ds(r0, 3 * WINDOW), gcols]
                vw = vp_ref[pl.ds(r0, 3 * WINDOW), gcols]
                qgt = _group_cols(qt_ref, r0, gi)
                dogt = _group_cols(dot_ref, r0, gi)
                pr, psink = _group_probs(qgt, kw, dist, gi, sk_ref)
                dp = _dot(vw, dogt)
                delta = jnp.sum(pr * dp, axis=0, keepdims=True)
                ds = (pr * (dp - delta)).astype(BF16)
                dsk_acc[gi:gi + 1, :] += -(psink * delta)
                dqgt = _dot_tn(kw, ds) * SCALE
                for hh in range(GROUP):
                    h = gi * GROUP + hh
                    dqt_ref[h * HEAD_DIM:(h + 1) * HEAD_DIM, pl.ds(r0, WINDOW)] = dqgt[:, hh * WINDOW:(hh + 1) * WINDOW]
                dkp_ref[pl.ds(r0, 3 * WINDOW), gcols] += _dot_nt(ds, qgt) * SCALE
                dvp_ref[pl.ds(r0, 3 * WINDOW), gcols] += _dot_nt(pr.astype(BF16), dogt)
            return carry

        lax.fori_loop(0, N_QBLOCKS, blk, 0)
        for h in range(ATTN_HEADS):
            gi, hh = divmod(h, GROUP)
            dsk_ref[:, h:h + 1] = jnp.sum(dsk_acc[gi:gi + 1, hh * WINDOW:(hh + 1) * WINDOW], axis=1, keepdims=True)
        dk_ref[...] = dkp_ref[WINDOW:WINDOW + SEQ, :]
        dv_ref[...] = dvp_ref[WINDOW:WINDOW + SEQ, :]

    vmem = pl.BlockSpec(memory_space=pltpu.VMEM)
    padded = (SEQ + 2 * WINDOW, KV_WIDTH)
    return pl.pallas_call(
        body, name="attn_bwd",
        in_specs=[pl.BlockSpec(memory_space=pltpu.SMEM), vmem, vmem, vmem, vmem],
        out_specs=[vmem, vmem, vmem, vmem],
        out_shape=[jax.ShapeDtypeStruct((ATTN_WIDTH, SEQ), F32),
                   jax.ShapeDtypeStruct((SEQ, KV_WIDTH), F32), jax.ShapeDtypeStruct((SEQ, KV_WIDTH), F32),
                   jax.ShapeDtypeStruct((1, ATTN_HEADS), F32)],
        scratch_shapes=[pltpu.VMEM((KV_HEADS, GROUP * WINDOW), F32), pltpu.VMEM(padded, BF16),
                        pltpu.VMEM(padded, BF16), pltpu.VMEM(padded, F32), pltpu.VMEM(padded, F32)],
        compiler_params=_cparams(),
    )(sinks, qt, k, v, dot_)


HALF_LANES = LANES // 2
BLOCK_ROWS = 32


def _embed_block(bt, q):
    z = jnp.zeros((16, HALF_LANES), bt.dtype)
    blk = jnp.concatenate([jnp.concatenate([bt[:16], z], axis=1), jnp.concatenate([z, bt[16:]], axis=1)], axis=0)
    parts = [jnp.zeros((BLOCK_ROWS * q, LANES), bt.dtype)] if q else []
    parts.append(blk)
    if q < 3:
        parts.append(jnp.zeros((BLOCK_ROWS * (3 - q), LANES), bt.dtype))
    return jnp.concatenate(parts, axis=0)


def _extract_block(m, q):
    blk = m[BLOCK_ROWS * q:BLOCK_ROWS * (q + 1)]
    return jnp.concatenate([blk[:16, :HALF_LANES], blk[16:, HALF_LANES:]], axis=0)


def _ssm_prep(lam_re, lam_im, log_dt, bt_re, bt_im, c_re, c_im):
    nb = 2 * N_LANE_BLOCKS

    def body(lr_ref, li_ref, ldt_ref, btr_ref, bti_ref, ctr_ref, cti_ref, ar_ref, ai_ref, bb_ref, cc_ref):
        lr = jnp.minimum(lr_ref[...], LAMBDA_RE_MAX)
        li = li_ref[...]
        dt = jnp.exp(ldt_ref[...])
        mag = jnp.exp(lr * dt)
        ar = mag * jnp.cos(li * dt)
        ai = mag * jnp.sin(li * dt)
        den = lr * lr + li * li
        cr = ((ar - 1.0) * lr + ai * li) / den
        ci = (ai * lr - (ar - 1.0) * li) / den
        ar_ref[...] = ar
        ai_ref[...] = ai
        for i in range(nb):
            q = i % 4
            rows = slice(BLOCK_ROWS * i, BLOCK_ROWS * (i + 1))
            br = _embed_block(btr_ref[rows, :], q)
            bi = _embed_block(bti_ref[rows, :], q)
            cri, cii = cr[i:i + 1, :], ci[i:i + 1, :]
            bb_ref[i] = jnp.concatenate([cri * br - cii * bi, cri * bi + cii * br], axis=1).astype(BF16)
            cc_ref[i] = jnp.concatenate([_embed_block(ctr_ref[rows, :], q).T,
                                         -_embed_block(cti_ref[rows, :], q).T], axis=0).astype(BF16)

    return pl.pallas_call(
        body, name="ssm_prep",
        out_shape=[jax.ShapeDtypeStruct((nb, LANES), F32), jax.ShapeDtypeStruct((nb, LANES), F32),
                   jax.ShapeDtypeStruct((nb, LANES, 2 * LANES), BF16),
                   jax.ShapeDtypeStruct((nb, 2 * LANES, LANES), BF16)],
        compiler_params=_cparams(),
    )(lam_re, lam_im, log_dt, bt_re, bt_im, c_re, c_im)


def _ssm_prep_bwd(lam_re, lam_im, log_dt, bt_re, bt_im, dar, dai, dbb, dcc):
    nb = 2 * N_LANE_BLOCKS

    def body(lr_ref, li_ref, ldt_ref, btr_ref, bti_ref, dar_ref, dai_ref, dbb_ref, dcc_ref,
             glr_ref, gli_ref, gdt_ref, gbr_ref, gbi_ref, gcre_ref, gcim_ref, gcr_s, gci_s):
        lam = lr_ref[...]
        lr = jnp.minimum(lam, LAMBDA_RE_MAX)
        li = li_ref[...]
        dt = jnp.exp(ldt_ref[...])
        mag = jnp.exp(lr * dt)
        cs = jnp.cos(li * dt)
        sn = jnp.sin(li * dt)
        ar = mag * cs
        ai = mag * sn
        den = lr * lr + li * li
        nr = (ar - 1.0) * lr + ai * li
        ni = ai * lr - (ar - 1.0) * li
        cr = nr / den
        ci = ni / den
        for i in range(nb):
            q = i % 4
            rows = slice(BLOCK_ROWS * i, BLOCK_ROWS * (i + 1))
            br = _embed_block(btr_ref[rows, :], q)
            bi = _embed_block(bti_ref[rows, :], q)
            gbbr = dbb_ref[i, :, :LANES]
            gbbi = dbb_ref[i, :, LANES:]
            cri, cii = cr[i:i + 1, :], ci[i:i + 1, :]
            gcr_s[i:i + 1, :] = jnp.sum(gbbr * br + gbbi * bi, axis=0, keepdims=True)
            gci_s[i:i + 1, :] = jnp.sum(gbbi * br - gbbr * bi, axis=0, keepdims=True)
            gbr_ref[rows, :] = _extract_block(cri * gbbr + cii * gbbi, q)
            gbi_ref[rows, :] = _extract_block(cri * gbbi - cii * gbbr, q)
            gcre_ref[rows, :] = _extract_block(dcc_ref[i, :LANES, :].T, q)
            gcim_ref[rows, :] = -_extract_block(dcc_ref[i, LANES:, :].T, q)
        g_cr = gcr_s[...]
        g_ci = gci_s[...]
        g_nr = g_cr / den
        g_ni = g_ci / den
        g_den = -(g_cr * nr + g_ci * ni) / (den * den)
        g_ar = dar_ref[...] + g_nr * lr - g_ni * li
        g_ai = dai_ref[...] + g_nr * li + g_ni * lr
        g_lr = g_nr * (ar - 1.0) + g_ni * ai + g_den * 2.0 * lr
        g_li = g_nr * ai - g_ni * (ar - 1.0) + g_den * 2.0 * li
        g_mag = g_ar * cs + g_ai * sn
        g_th = (g_ai * cs - g_ar * sn) * mag
        g_lr = g_lr + g_mag * mag * dt
        g_li = g_li + g_th * dt
        g_dt = g_mag * mag * lr + g_th * li
        glr_ref[...] = jnp.where(lam < LAMBDA_RE_MAX, g_lr, 0.0)
        gli_ref[...] = g_li
        gl = g_dt * dt
        half = LANES // 2
        gdt_ref[:, 0:1] = jnp.sum(gl[:, :half], axis=1, keepdims=True)
        gdt_ref[:, 1:2] = jnp.sum(gl[:, half:], axis=1, keepdims=True)

    rows_shape = jax.ShapeDtypeStruct((nb * BLOCK_ROWS, HALF_LANES), F32)
    return pl.pallas_call(
        body, name="ssm_prep_bwd",
        out_shape=[jax.ShapeDtypeStruct((nb, LANES), F32), jax.ShapeDtypeStruct((nb, LANES), F32),
                   jax.ShapeDtypeStruct((nb, 2), F32), rows_shape, rows_shape, rows_shape, rows_shape],
        scratch_shapes=[pltpu.VMEM((nb, LANES), F32), pltpu.VMEM((nb, LANES), F32)],
        compiler_params=_cparams(),
    )(lam_re, lam_im, log_dt, bt_re, bt_im, dar, dai, dbb, dcc)


def _cmul(ar, ai, br, bi):
    return ar * br - ai * bi, ar * bi + ai * br


def _interleave_rows(src_ref, dst_ref):
    def step(j, carry):
        dst_ref[pl.ds(pl.multiple_of(j * 8, 8), 8), :] = src_ref[pl.ds(j, 8, stride=SCAN_CHUNK), :]
        return carry
    lax.fori_loop(0, SCAN_CHUNK, step, 0, unroll=4)


def _deinterleave_rows(src_ref, dst_ref):
    def step(j, carry):
        dst_ref[pl.ds(j, 8, stride=SCAN_CHUNK), :] = src_ref[pl.ds(pl.multiple_of(j * 8, 8), 8), :]
        return carry
    lax.fori_loop(0, SCAN_CHUNK, step, 0, unroll=4)


def _scan_inplace(re_ref, im_ref, a_re, a_im, reverse):
    nq = len(a_re)
    ch = SCAN_CHUNK
    ab_re = [jnp.broadcast_to(a, (8, LANES)) for a in a_re]
    ab_im = [jnp.broadcast_to(a, (8, LANES)) for a in a_im]

    def rows(j):
        jj = (ch - 1 - j) if reverse else j
        return pl.ds(pl.multiple_of(jj * 8, 8), 8)

    def sweep(init, store):
        def step(j, st):
            out = []
            r = rows(j)
            for qi in range(nq):
                xr, xi = st[2 * qi], st[2 * qi + 1]
                pr, pi = _cmul(ab_re[qi], ab_im[qi], xr, xi)
                xr = pr + re_ref[qi, r, :]
                xi = pi + im_ref[qi, r, :]
                if store:
                    re_ref[qi, r, :] = xr
                    im_ref[qi, r, :] = xi
                out += [xr, xi]
            return tuple(out)
        return lax.fori_loop(0, ch, step, tuple(init), unroll=2)

    zeros = [jnp.zeros((8, LANES), F32)] * (2 * nq)
    finals = sweep(zeros, store=False)

    row_id = lax.broadcasted_iota(jnp.int32, (8, LANES), 0)
    carries = []
    for qi in range(nq):
        pr, pi = ab_re[qi], ab_im[qi]
        for _ in range(8):
            pr, pi = _cmul(pr, pi, pr, pi)
        fr, fi = finals[2 * qi], finals[2 * qi + 1]
        sr = jnp.zeros((8, LANES), F32)
        si = jnp.zeros((8, LANES), F32)
        for _ in range(7):
            tr, ti = _cmul(pr, pi, sr, si)
            tr, ti = tr + fr, ti + fi
            if reverse:
                sr = jnp.where(row_id == 7, 0.0, pltpu.roll(tr, 7, axis=0))
                si = jnp.where(row_id == 7, 0.0, pltpu.roll(ti, 7, axis=0))
            else:
                sr = jnp.where(row_id == 0, 0.0, pltpu.roll(tr, 1, axis=0))
                si = jnp.where(row_id == 0, 0.0, pltpu.roll(ti, 1, axis=0))
        carries += [sr, si]
    sweep(carries, store=True)


SSM_Q = 4


def _ssm_fwd(u, are, aim, bb, cc, dskip, after=None):
    nq = SSM_Q
    deps = [] if after is None else [after]

    def body(u_ref, ar_ref, ai_ref, bb_ref, cc_ref, d_ref, *rest):
        y_ref, xr_ref, xi_ref, sre, sim, up, yp = rest[len(deps):]
        _interleave_rows(u_ref, up)
        uf = up[...]
        ub = uf.astype(BF16)
        yp[...] = d_ref[...] * uf
        for d in range(2):
            for qi in range(nq):
                sre[qi] = _dot(ub, bb_ref[d, qi, :, :LANES])
                sim[qi] = _dot(ub, bb_ref[d, qi, :, LANES:])
            _scan_inplace(sre, sim, [ar_ref[d, qi] for qi in range(nq)], [ai_ref[d, qi] for qi in range(nq)],
                          reverse=(d == 1))
            for qi in range(nq):
                xrb = sre[qi].astype(BF16)
                xib = sim[qi].astype(BF16)
                xr_ref[d, qi] = xrb
                xi_ref[d, qi] = xib
                yp[...] += _dot(xrb, cc_ref[d, qi, :LANES, :]) + _dot(xib, cc_ref[d, qi, LANES:, :])
        _deinterleave_rows(yp, y_ref)

    blk4 = lambda k: (0, k, 0, 0)
    return pl.pallas_call(
        body, name="ssm_fwd", grid=(SSM_WIDTH // LANES,),
        in_specs=[pl.BlockSpec((SEQ, LANES), lambda k: (0, k)),
                  pl.BlockSpec((2, nq, 1, LANES), blk4), pl.BlockSpec((2, nq, 1, LANES), blk4),
                  pl.BlockSpec((2, nq, LANES, 2 * LANES), blk4), pl.BlockSpec((2, nq, 2 * LANES, LANES), blk4),
                  pl.BlockSpec((1, LANES), lambda k: (0, k))] + [pl.BlockSpec(memory_space=pl.ANY)] * len(deps),
        out_specs=[pl.BlockSpec((SEQ, LANES), lambda k: (0, k)),
                   pl.BlockSpec((2, nq, SEQ, LANES), blk4), pl.BlockSpec((2, nq, SEQ, LANES), blk4)],
        out_shape=[jax.ShapeDtypeStruct((SEQ, SSM_WIDTH), F32),
                   jax.ShapeDtypeStruct((2, N_LANE_BLOCKS, SEQ, LANES), BF16),
                   jax.ShapeDtypeStruct((2, N_LANE_BLOCKS, SEQ, LANES), BF16)],
        scratch_shapes=[pltpu.VMEM((nq, SEQ, LANES), F32), pltpu.VMEM((nq, SEQ, LANES), F32),
                        pltpu.VMEM((SEQ, LANES), F32), pltpu.VMEM((SEQ, LANES), F32)],
        compiler_params=_cparams(("parallel",)),
    )(u, are, aim, bb, cc, dskip, *deps)


def _ssm_bwd(dy, u, xr, xi, are, aim, bb, cc, dskip, after=None):
    nq = SSM_Q
    body_rows = SEQ - 8
    deps = [] if after is None else [after]

    def body(dy_ref, u_ref, xr_ref, xi_ref, ar_ref, ai_ref, bb_ref, cc_ref, d_ref, *rest):
        du_ref, dd_ref, dcc_ref, dbb_ref, dar_ref, dai_ref, sre, sim, up, dyp, dup = rest[len(deps):]
        _interleave_rows(u_ref, up)
        _interleave_rows(dy_ref, dyp)
        dyf = dyp[...]
        uf = up[...]
        dyb = dyf.astype(BF16)
        ub = uf.astype(BF16)
        dd_ref[...] = jnp.sum(dyf * uf, axis=0, keepdims=True)
        dup[...] = d_ref[...] * dyf
        row8 = lax.broadcasted_iota(jnp.int32, (8, LANES), 0)
        for d in range(2):
            for qi in range(nq):
                dx = _dot_nt(dyb, cc_ref[d, qi])
                sre[qi] = dx[:, :LANES]
                sim[qi] = dx[:, LANES:]
                dcc_ref[d, qi] = _dot_tn(jnp.concatenate([xr_ref[d, qi], xi_ref[d, qi]], axis=1), dyb)
            _scan_inplace(sre, sim, [ar_ref[d, qi] for qi in range(nq)], [-ai_ref[d, qi] for qi in range(nq)],
                          reverse=(d == 0))
            for qi in range(nq):
                gr = sre[qi]
                gi = sim[qi]
                xrf = xr_ref[d, qi].astype(F32)
                xif = xi_ref[d, qi].astype(F32)
                if d == 0:
                    g_main_r, g_main_i = gr[8:], gi[8:]
                    x_main_r, x_main_i = xrf[:body_rows], xif[:body_rows]
                    g_edge_r, g_edge_i = gr[:8], gi[:8]
                    x_edge_r = jnp.where(row8 == 0, 0.0, pltpu.roll(xrf[body_rows:], 1, axis=0))
                    x_edge_i = jnp.where(row8 == 0, 0.0, pltpu.roll(xif[body_rows:], 1, axis=0))
                else:
                    g_main_r, g_main_i = gr[:body_rows], gi[:body_rows]
                    x_main_r, x_main_i = xrf[8:], xif[8:]
                    g_edge_r, g_edge_i = gr[body_rows:], gi[body_rows:]
                    x_edge_r = jnp.where(row8 == 7, 0.0, pltpu.roll(xrf[:8], 7, axis=0))
                    x_edge_i = jnp.where(row8 == 7, 0.0, pltpu.roll(xif[:8], 7, axis=0))
                dar_ref[d, qi] = (jnp.sum(g_main_r * x_main_r + g_main_i * x_main_i, axis=0, keepdims=True)
                                  + jnp.sum(g_edge_r * x_edge_r + g_edge_i * x_edge_i, axis=0, keepdims=True))
                dai_ref[d, qi] = (jnp.sum(g_main_i * x_main_r - g_main_r * x_main_i, axis=0, keepdims=True)
                                  + jnp.sum(g_edge_i * x_edge_r - g_edge_r * x_edge_i, axis=0, keepdims=True))
                gb = jnp.concatenate([gr, gi], axis=1).astype(BF16)
                dup[...] += _dot_nt(gb, bb_ref[d, qi])
                dbb_ref[d, qi] = _dot_tn(ub, gb)
        _deinterleave_rows(dup, du_ref)

    blk4 = lambda k: (0, k, 0, 0)
    col = lambda k: (0, k)
    bb_spec = pl.BlockSpec((2, nq, LANES, 2 * LANES), blk4)
    cc_spec = pl.BlockSpec((2, nq, 2 * LANES, LANES), blk4)
    a_spec = pl.BlockSpec((2, nq, 1, LANES), blk4)
    x_spec = pl.BlockSpec((2, nq, SEQ, LANES), blk4)
    a_shape = jax.ShapeDtypeStruct((2, N_LANE_BLOCKS, 1, LANES), F32)
    return pl.pallas_call(
        body, name="ssm_bwd", grid=(SSM_WIDTH // LANES,),
        in_specs=[pl.BlockSpec((SEQ, LANES), col), pl.BlockSpec((SEQ, LANES), col), x_spec, x_spec,
                  a_spec, a_spec, bb_spec, cc_spec, pl.BlockSpec((1, LANES), col)]
        + [pl.BlockSpec(memory_space=pl.ANY)] * len(deps),
        out_specs=[pl.BlockSpec((SEQ, LANES), col), pl.BlockSpec((1, LANES), col),
                   cc_spec, bb_spec, a_spec, a_spec],
        out_shape=[jax.ShapeDtypeStruct((SEQ, SSM_WIDTH), F32), jax.ShapeDtypeStruct((1, SSM_WIDTH), F32),
                   jax.ShapeDtypeStruct((2, N_LANE_BLOCKS, 2 * LANES, LANES), F32),
                   jax.ShapeDtypeStruct((2, N_LANE_BLOCKS, LANES, 2 * LANES), F32), a_shape, a_shape],
        scratch_shapes=[pltpu.VMEM((nq, SEQ, LANES), F32), pltpu.VMEM((nq, SEQ, LANES), F32),
                        pltpu.VMEM((SEQ, LANES), F32), pltpu.VMEM((SEQ, LANES), F32), pltpu.VMEM((SEQ, LANES), F32)],
        compiler_params=_cparams(("parallel",)),
    )(dy, u, xr, xi, are, aim, bb, cc, dskip, *deps)


GELU_C = 0.7978845608028654
GELU_K = 0.044715


def _gelu(y):
    return 0.5 * y * (1.0 + jnp.tanh(GELU_C * (y + GELU_K * y * y * y)))


def _gelu_grad(y):
    t = jnp.tanh(GELU_C * (y + GELU_K * y * y * y))
    return 0.5 * (1.0 + t) + 0.5 * y * (1.0 - t * t) * GELU_C * (1.0 + 3.0 * GELU_K * y * y)


def _mixout_fwd(o, y, glu_w, glu_b, gan, gsn, wout, x1):
    tm = MIX_TM

    def body(o_ref, y_ref, gw_ref, gb_ref, gan_ref, gsn_ref, w_ref, x1_ref, x2_ref, mx_ref):
        yg = _gelu(y_ref[...])
        z = _dot(yg.astype(BF16), gw_ref[...]) + gb_ref[...]
        so = yg * _sigmoid(z)
        na = _rms_fwd(o_ref[...], gan_ref[...])
        ns = _rms_fwd(so, gsn_ref[...])
        mixed = jnp.concatenate([na, ns], axis=-1).astype(BF16)
        mx_ref[...] = mixed
        x2_ref[...] = x1_ref[...] + _dot(mixed, w_ref[...])

    row = lambda i: (i, 0)
    const = lambda i: (0, 0)
    return pl.pallas_call(
        body, name="mixout_fwd", grid=(SEQ // tm,),
        in_specs=[pl.BlockSpec((tm, ATTN_WIDTH), row), pl.BlockSpec((tm, SSM_WIDTH), row),
                  pl.BlockSpec((SSM_WIDTH, SSM_WIDTH), const), pl.BlockSpec((1, SSM_WIDTH), const),
                  pl.BlockSpec((1, ATTN_WIDTH), const), pl.BlockSpec((1, SSM_WIDTH), const),
                  pl.BlockSpec((D_MODEL, D_MODEL), const), pl.BlockSpec((tm, D_MODEL), row)],
        out_specs=[pl.BlockSpec((tm, D_MODEL), row), pl.BlockSpec((tm, D_MODEL), row)],
        out_shape=[jax.ShapeDtypeStruct((SEQ, D_MODEL), F32), jax.ShapeDtypeStruct((SEQ, D_MODEL), BF16)],
        compiler_params=_cparams(("parallel",)),
    )(o, y, glu_w, glu_b, gan, gsn, wout, x1)


def _mixout_bwd(dx2, o, y, glu_w, glu_b, gan, gsn, wout):
    tm = MIX_TM

    def body(dx2_ref, o_ref, y_ref, gw_ref, gb_ref, gan_ref, gsn_ref, w_ref,
             do_ref, dy_ref, dz_ref, yg_ref, dxb_ref, dgan_ref, dgsn_ref, dgb_ref):
        i = pl.program_id(0)
        dxb = dx2_ref[...].astype(BF16)
        dxb_ref[...] = dxb
        dmixed = _dot_nt(dxb, w_ref[...])
        do, dgan = _rms_bwd(dmixed[:, :ATTN_WIDTH], o_ref[...], gan_ref[...])
        do_ref[...] = do.T
        yv = y_ref[...]
        yg = _gelu(yv)
        ygb = yg.astype(BF16)
        yg_ref[...] = ygb
        sg = _sigmoid(_dot(ygb, gw_ref[...]) + gb_ref[...])
        dso, dgsn = _rms_bwd(dmixed[:, ATTN_WIDTH:], yg * sg, gsn_ref[...])
        dz = dso * yg * sg * (1.0 - sg)
        dzb = dz.astype(BF16)
        dz_ref[...] = dzb
        dyg = dso * sg + _dot_nt(dzb, gw_ref[...])
        dy_ref[...] = dyg * _gelu_grad(yv)
        dgb = jnp.sum(dz, axis=0, keepdims=True)

        @pl.when(i == 0)
        def _():
            dgan_ref[...] = dgan
            dgsn_ref[...] = dgsn
            dgb_ref[...] = dgb

        @pl.when(i != 0)
        def _():
            dgan_ref[...] += dgan
            dgsn_ref[...] += dgsn
            dgb_ref[...] += dgb

    row = lambda i: (i, 0)
    const = lambda i: (0, 0)
    return pl.pallas_call(
        body, name="mixout_bwd", grid=(SEQ // tm,),
        in_specs=[pl.BlockSpec((tm, D_MODEL), row), pl.BlockSpec((tm, ATTN_WIDTH), row),
                  pl.BlockSpec((tm, SSM_WIDTH), row),
                  pl.BlockSpec((SSM_WIDTH, SSM_WIDTH), const), pl.BlockSpec((1, SSM_WIDTH), const),
                  pl.BlockSpec((1, ATTN_WIDTH), const), pl.BlockSpec((1, SSM_WIDTH), const),
                  pl.BlockSpec((D_MODEL, D_MODEL), const)],
        out_specs=[pl.BlockSpec((ATTN_WIDTH, tm), lambda i: (0, i)), pl.BlockSpec((tm, SSM_WIDTH), row),
                   pl.BlockSpec((tm, SSM_WIDTH), row), pl.BlockSpec((tm, SSM_WIDTH), row),
                   pl.BlockSpec((tm, D_MODEL), row),
                   pl.BlockSpec((1, ATTN_WIDTH), const), pl.BlockSpec((1, SSM_WIDTH), const),
                   pl.BlockSpec((1, SSM_WIDTH), const)],
        out_shape=[jax.ShapeDtypeStruct((ATTN_WIDTH, SEQ), F32), jax.ShapeDtypeStruct((SEQ, SSM_WIDTH), F32),
                   jax.ShapeDtypeStruct((SEQ, SSM_WIDTH), BF16), jax.ShapeDtypeStruct((SEQ, SSM_WIDTH), BF16),
                   jax.ShapeDtypeStruct((SEQ, D_MODEL), BF16),
                   jax.ShapeDtypeStruct((1, ATTN_WIDTH), F32), jax.ShapeDtypeStruct((1, SSM_WIDTH), F32),
                   jax.ShapeDtypeStruct((1, SSM_WIDTH), F32)],
        compiler_params=_cparams(("arbitrary",)),
    )(dx2, o, y, glu_w, glu_b, gan, gsn, wout)


def _local_step(x, target, w, p, late_weights, early_grads, after=None, midway=None):
    x1, h1, a1, b1 = _ffn_fwd(x, p["norm_ffn1"], w["wgt1"], w["wut1"], w["wd1"], "ffn1_fwd", after=after)
    h2, q, k, v, u = _mixin_fwd(x1, p["norm_mix"], w["wint"])

    lam_re = p["ssm_lambda_re"].reshape(2 * N_LANE_BLOCKS, LANES)
    lam_im = p["ssm_lambda_im"].reshape(2 * N_LANE_BLOCKS, LANES)
    log_dt = jnp.repeat(p["ssm_log_dt"].reshape(2, 32), 64, axis=-1).reshape(2 * N_LANE_BLOCKS, LANES)
    a_re, a_im, bb, cc = _ssm_prep(lam_re, lam_im, log_dt, p["ssm_b_re"], p["ssm_b_im"],
                                   p["ssm_c_re"], p["ssm_c_im"])
    shape_a = (2, N_LANE_BLOCKS, 1, LANES)
    a_re4, a_im4 = a_re.reshape(shape_a), a_im.reshape(shape_a)
    bb4 = bb.reshape(2, N_LANE_BLOCKS, LANES, 2 * LANES)
    cc4 = cc.reshape(2, N_LANE_BLOCKS, 2 * LANES, LANES)
    dskip = p["ssm_d"].T.reshape(1, SSM_WIDTH)
    y, xr, xi = _ssm_fwd(u, a_re4, a_im4, bb4, cc4, dskip)
    o = _attn_fwd(q, k, v, p["attn_sinks"], after=None if midway is None else midway(y))

    w2 = late_weights(o)
    x2, mixed = _mixout_fwd(o, y, w2["glu"], p["ssm_glu_b"], p["attn_out_norm"], p["ssm_out_norm"], w2["wout"], x1)
    dx3, h3, a3, b3, loss, d_final = _ffn_fwd(x2, p["norm_ffn2"], w2["wgt2"], w2["wut2"], w2["wd2"], "ffn2_fwd",
                                              head=(p["final_norm"], target))
    dx2, da3, db3, s3, df3, d_n2 = _ffn_bwd_act(dx3, x2, p["norm_ffn2"], a3, b3, w2["wgt2"], w2["wut2"], w2["wd2"],
                                                "ffn2_bwd_act")
    g_wgt2, g_wut2, g_wd2 = _mm_tn([(da3, h3), (db3, h3), (s3, df3)], "ffn2_bwd_w")

    do, dy, dz, ygb, dx2b, d_gan, d_gsn, d_glub = _mixout_bwd(
        dx2, o, y, w2["glu"], p["ssm_glu_b"], p["attn_out_norm"], p["ssm_out_norm"], w2["wout"])
    (g_wout,) = _mm_tn([(mixed, dx2b)], "wout_bwd_w")
    (g_glu,) = _mm_tn([(ygb, dz)], "glu_bwd_w")
    sent = early_grads(dict(glu=g_glu, wout=g_wout, wgt2=g_wgt2, wut2=g_wut2, wd2=g_wd2))

    du, d_dskip, dcc, dbb, dar, dai = _ssm_bwd(dy, u, xr, xi, a_re4, a_im4, bb4, cc4, dskip, after=sent)
    nb = 2 * N_LANE_BLOCKS
    g_lre, g_lim, g_ldt, g_btr, g_bti, g_cre, g_cim = _ssm_prep_bwd(
        lam_re, lam_im, log_dt, p["ssm_b_re"], p["ssm_b_im"], dar.reshape(nb, LANES), dai.reshape(nb, LANES),
        dbb.reshape(nb, LANES, 2 * LANES), dcc.reshape(nb, 2 * LANES, LANES))

    dq, dk, dv, d_sinks = _attn_bwd(q, k, v, p["attn_sinks"], do)
    dx1, dproj, d_nmix = _mixin_bwd(dq, dk, dv, du, w["wint"], x1, p["norm_mix"], dx2)
    (g_wint,) = _mm_tn([(dproj, h2)], "win_bwd_w")

    dx0, da1, db1, s1, df1, d_n1 = _ffn_bwd_act(dx1, x, p["norm_ffn1"], a1, b1, w["wgt1"], w["wut1"], w["wd1"],
                                                "ffn1_bwd_act")
    g_wgt1, g_wut1, g_wd1 = _mm_tn([(da1, h1), (db1, h1), (s1, df1)], "ffn1_bwd_w")

    big = dict(wgt1=g_wgt1, wut1=g_wut1, wd1=g_wd1, wint=g_wint)
    small = dict(
        norm_ffn1=d_n1, norm_mix=d_nmix, attn_sinks=d_sinks,
        ssm_lambda_re=g_lre.reshape(64, 64), ssm_lambda_im=g_lim.reshape(64, 64),
        ssm_log_dt=g_ldt.reshape(2, 32), ssm_b_re=g_btr, ssm_b_im=g_bti, ssm_c_re=g_cre, ssm_c_im=g_cim,
        ssm_d=d_dskip.reshape(32, 16).T, ssm_glu_b=d_glub, attn_out_norm=d_gan, ssm_out_norm=d_gsn,
        norm_ffn2=d_n2, final_norm=d_final, loss=loss)
    return loss, dx0, big, small


BIG = dict(
    wgt1=("ffn1_w_gate", 352, 1024, True), wut1=("ffn1_w_up", 352, 1024, True), wd1=("ffn1_w_down", 352, 1024, False),
    wint=("w_in", 160, 1024, True), glu=("ssm_glu_w", 64, 512, False), wout=("w_out", 128, 1024, False),
    wgt2=("ffn2_w_gate", 352, 1024, True), wut2=("ffn2_w_up", 352, 1024, True), wd2=("ffn2_w_down", 352, 1024, False))

SMALL = dict(
    norm_ffn1=(1, 1024), norm_mix=(1, 1024), attn_sinks=(1, 8), ssm_lambda_re=(64, 64), ssm_lambda_im=(64, 64),
    ssm_log_dt=(2, 32), ssm_b_re=(1024, 64), ssm_b_im=(1024, 64), ssm_c_re=(1024, 64), ssm_c_im=(1024, 64),
    ssm_d=(16, 32), ssm_glu_b=(1, 512), attn_out_norm=(1, 512), ssm_out_norm=(1, 512), norm_ffn2=(1, 1024),
    final_norm=(1, 1024), loss=(1, 128))
SMALL_TRANSPOSED = ("ssm_b_re", "ssm_b_im", "ssm_d")
SMALL_PARAMS = tuple(n for n in SMALL if n != "loss")

SMALL_PAIRS = (("ssm_lambda_re", "ssm_lambda_im"), ("ssm_c_re", "ssm_c_im"), ("ssm_b_re", "ssm_b_im"))
SMALL_VECS = ("norm_ffn1", "norm_mix", "norm_ffn2", "final_norm", "ssm_glu_b", "attn_out_norm", "ssm_out_norm")
SMALL_TILES = ("ssm_log_dt", "attn_sinks", "ssm_d", "loss")


def _small_offsets():
    off, table = 0, {}
    for re, im in SMALL_PAIRS:
        table[re] = table[im] = off
        off += SMALL[re][0]
    for n in SMALL_VECS:
        table[n] = off
        off += SMALL[n][1] // LANES
    for n in SMALL_TILES:
        off = -(-off // 8) * 8
        table[n] = off
        off += SMALL[n][0]
    return table, off


SMALL_OFFSET, SMALL_USED_ROWS = _small_offsets()
SMALL_ROWS = -(-SMALL_USED_ROWS // (8 * N_DEV)) * 8 * N_DEV


def _cast_shards(shards):
    names = list(BIG)

    def body(*refs):
        ins, outs = refs[:len(names)], refs[len(names):]
        for idx in range(len(names)):
            outs[idx][...] = ins[idx][...].astype(BF16)

    return pl.pallas_call(
        body, name="cast_shards",
        out_shape=[jax.ShapeDtypeStruct((BIG[n][1], BIG[n][2]), BF16) for n in names],
        compiler_params=_cparams(),
    )(*[shards[n] for n in names])


def _peer(x, y, c, r):
    px = 1 - x if r & 4 else x
    py = 1 - y if r & 2 else y
    pc = 1 - c if r & 1 else c
    return px, py, pc


FIRST_GROUP = ("wgt1", "wut1", "wd1", "wint")
LATE_GROUP = ("glu", "wout", "wgt2", "wut2", "wd2")
N_PEERS = N_DEV - 1
ANY_SPEC = pl.BlockSpec(memory_space=pl.ANY)
HBM_SPEC = pl.BlockSpec(memory_space=pltpu.HBM)
SEM_SPEC = pl.BlockSpec(memory_space=pltpu.SEMAPHORE)
DATAFLOW_EFFECT = pltpu.SideEffectType.DATAFLOW_SIDE_EFFECTING


def _mesh_pos():
    x, y, c = lax.axis_index("x"), lax.axis_index("y"), lax.axis_index("c")
    return x, y, c, 4 * x + 2 * y + c


def _gather_first(first, late):
    nf, nl = len(first), len(late)

    def body(*refs):
        f_in, l_in = refs[:nf], refs[nf:nf + nl]
        f_out, l_out = refs[nf + nl:2 * nf + nl], refs[2 * nf + nl:2 * (nf + nl)]
        send_sems, recv_sems, local_sems = refs[2 * (nf + nl):]
        x, y, c, me = _mesh_pos()
        sibling = (x, y, 1 - c)
        chips = [(x, 1 - y), (1 - x, y), (1 - x, 1 - y)]

        def idx(px, py, pc):
            return 4 * px + 2 * py + pc

        def copy(k, s, block, to, src=None):
            slot = f_out[k].at[block]
            return pltpu.make_async_remote_copy(
                src_ref=slot if src is None else src, dst_ref=slot, send_sem=send_sems.at[k, s],
                recv_sem=recv_sems.at[k, s], device_id=to, device_id_type=MESH_ID)

        local = []
        for k in range(nf + nl):
            src, dst = (f_in[k], f_out[k]) if k < nf else (l_in[k - nf], l_out[k - nf])
            mine = pltpu.make_async_copy(src, dst.at[me], local_sems.at[k])
            mine.start()
            local.append(mine)
        sends = []
        for j, chip in enumerate(chips):
            for k in range(nf):
                sends.append(copy(k, 1 + j, me, (*chip, c), src=f_in[k]))
                sends[-1].start()
        for k in range(nf):
            sends.append(copy(k, 0, me, sibling, src=f_in[k]))
            sends[-1].start()
        for j, chip in enumerate(chips):
            for k in range(nf):
                copy(k, 1 + j, idx(*chip, c), (*chip, c)).wait_recv()
                sends.append(copy(k, 4 + j, idx(*chip, c), sibling))
                sends[-1].start()
        for k in range(nf):
            copy(k, 0, idx(*sibling), sibling).wait_recv()
        for j, chip in enumerate(chips):
            for k in range(nf):
                copy(k, 4 + j, idx(*chip, 1 - c), sibling).wait_recv()
        for cp in sends:
            cp.wait_send()
        for cp in local:
            cp.wait()

    return pl.pallas_call(
        body, name="gather_first",
        in_specs=[ANY_SPEC] * (nf + nl), out_specs=[ANY_SPEC] * (nf + nl),
        out_shape=[jax.ShapeDtypeStruct((N_DEV,) + s.shape, s.dtype) for s in list(first) + list(late)],
        scratch_shapes=[pltpu.SemaphoreType.DMA((nf, N_PEERS)), pltpu.SemaphoreType.DMA((nf, N_PEERS)),
                        pltpu.SemaphoreType.DMA((nf + nl,))],
        compiler_params=pltpu.CompilerParams(has_side_effects=True),
    )(*first, *late)


def _split_copy(src_refs, land_refs, send_sems, recv_sems, k, r, pos, scatter, receiving):
    x, y, c, me = pos
    px, py, pc = _peer(x, y, c, r)
    peer_idx = 4 * px + 2 * py + pc
    if scatter:
        src, dst = src_refs[k].at[peer_idx], land_refs[k].at[r - 1]
    else:
        src, dst = src_refs[k], land_refs[k].at[peer_idx if receiving else me]
    return pltpu.make_async_remote_copy(
        src_ref=src, dst_ref=dst, send_sem=send_sems.at[k * N_PEERS + r - 1],
        recv_sem=recv_sems.at[k * N_PEERS + r - 1], device_id=(px, py, pc), device_id_type=MESH_ID)


def _split_start(name, srcs, lands, scatter):
    n = len(srcs)

    def body(*refs):
        src_refs, land_refs = refs[:n], refs[n:2 * n]
        send_sems, recv_sems = refs[2 * n], refs[2 * n + 1]
        token = refs[-1]
        pos = _mesh_pos()
        for k in range(n):
            for r in range(1, N_DEV):
                _split_copy(src_refs, land_refs, send_sems, recv_sems, k, r, pos, scatter, False).start()
        token[...] = jnp.zeros_like(token)

    thru = [pltpu.HBM(a.shape, a.dtype) for a in list(srcs) + list(lands)]
    outs = pl.pallas_call(
        body, name=name,
        in_specs=[HBM_SPEC] * (2 * n),
        out_specs=[SEM_SPEC, SEM_SPEC] + [HBM_SPEC] * (2 * n) + [pl.BlockSpec(memory_space=pltpu.VMEM)],
        out_shape=[pltpu.SemaphoreType.DMA((n * N_PEERS,)), pltpu.SemaphoreType.DMA((n * N_PEERS,))] + thru
        + [jax.ShapeDtypeStruct((8, LANES), F32)],
        input_output_aliases={i: 2 + i for i in range(2 * n)},
        compiler_params=pltpu.CompilerParams(has_side_effects=DATAFLOW_EFFECT),
    )(*[pltpu.with_memory_space_constraint(a, pltpu.HBM) for a in list(srcs) + list(lands)])
    return outs[0], outs[1], outs[2:2 + n], outs[2 + n:2 + 2 * n], outs[-1]


def _split_wait(name, send_sems, recv_sems, srcs, lands, scatter, after):
    n = len(srcs)

    def body(*refs):
        src_refs, land_refs = refs[:n], refs[n:2 * n]
        send, recv = refs[2 * n], refs[2 * n + 1]
        pos = _mesh_pos()
        for k in range(n):
            for r in range(1, N_DEV):
                cp = _split_copy(src_refs, land_refs, send, recv, k, r, pos, scatter, True)
                cp.wait_send()
                cp.wait_recv()

    thru = [pltpu.HBM(a.shape, a.dtype) for a in list(srcs) + list(lands)]
    outs = pl.pallas_call(
        body, name=name,
        in_specs=[HBM_SPEC] * (2 * n) + [SEM_SPEC, SEM_SPEC, ANY_SPEC],
        out_specs=[HBM_SPEC] * (2 * n), out_shape=thru,
        input_output_aliases={i: i for i in range(2 * n)},
        compiler_params=pltpu.CompilerParams(has_side_effects=DATAFLOW_EFFECT),
    )(*srcs, *lands, send_sems, recv_sems, after)
    return outs[:n], outs[n:]


def _late_copy(passing, src_refs, land_refs, send_sems, recv_sems, k, s, pos, receiving):
    x, y, c, me = pos
    chips = [(x, 1 - y), (1 - x, y), (1 - x, 1 - y)]
    sibling = (x, y, 1 - c)

    def idx(dev):
        return 4 * dev[0] + 2 * dev[1] + dev[2]

    if passing:
        to = sibling
        block = idx((*chips[s], 1 - c)) if receiving else idx((*chips[s], c))
        src = dst = land_refs[k].at[block]
        sem = k * 3 + s
    else:
        to = sibling if s == 0 else (*chips[s - 1], c)
        src, dst = src_refs[k], land_refs[k].at[idx(to) if receiving else me]
        sem = k * 4 + s
    return pltpu.make_async_remote_copy(src_ref=src, dst_ref=dst, send_sem=send_sems.at[sem],
                                        recv_sem=recv_sems.at[sem], device_id=to, device_id_type=MESH_ID)


def _late_gather_call(name, stage, srcs, lands, sems, after=None):
    n = len(srcs)
    n_sem_in = len(sems)
    has_after = after is not None

    def body(*refs):
        src_refs, land_refs = refs[:n], refs[n:2 * n]
        sem_in = refs[2 * n:2 * n + n_sem_in]
        outs = refs[2 * n + n_sem_in + (1 if has_after else 0):]
        pos = _mesh_pos()
        if stage == 0:
            own_send, own_recv = outs[0], outs[1]
            for s in (1, 2, 3, 0):
                for k in range(n):
                    _late_copy(False, src_refs, land_refs, own_send, own_recv, k, s, pos, False).start()
            outs[-1][...] = jnp.zeros_like(outs[-1])
        elif stage == 1:
            own_recv = sem_in[1]
            pass_send, pass_recv = outs[0], outs[1]
            for s in range(3):
                for k in range(n):
                    _late_copy(False, src_refs, land_refs, sem_in[0], own_recv, k, s + 1, pos, True).wait_recv()
                    _late_copy(True, src_refs, land_refs, pass_send, pass_recv, k, s, pos, False).start()
            outs[-1][...] = jnp.zeros_like(outs[-1])
        else:
            own_send, own_recv, pass_send, pass_recv = sem_in
            for k in range(n):
                _late_copy(False, src_refs, land_refs, own_send, own_recv, k, 0, pos, True).wait_recv()
                for s in range(4):
                    _late_copy(False, src_refs, land_refs, own_send, own_recv, k, s, pos, False).wait_send()
                for s in range(3):
                    cp = _late_copy(True, src_refs, land_refs, pass_send, pass_recv, k, s, pos, True)
                    cp.wait_recv()
                    cp.wait_send()

    thru = [pltpu.HBM(a.shape, a.dtype) for a in list(srcs) + list(lands)]
    new_sems = [[pltpu.SemaphoreType.DMA((n * 4,))] * 2, [pltpu.SemaphoreType.DMA((n * 3,))] * 2, []][stage]
    extra = [] if stage == 2 else [jax.ShapeDtypeStruct((8, LANES), F32)]
    outs = pl.pallas_call(
        body, name=name,
        in_specs=[HBM_SPEC] * (2 * n) + [SEM_SPEC] * n_sem_in + [ANY_SPEC] * has_after,
        out_specs=[SEM_SPEC] * len(new_sems) + [HBM_SPEC] * (2 * n) + [pl.BlockSpec(memory_space=pltpu.VMEM)] * len(extra),
        out_shape=new_sems + thru + extra,
        input_output_aliases={i: len(new_sems) + i for i in range(2 * n)},
        compiler_params=pltpu.CompilerParams(has_side_effects=DATAFLOW_EFFECT),
    )(*[pltpu.with_memory_space_constraint(a, pltpu.HBM) for a in list(srcs) + list(lands)], *sems,
      *([after] if has_after else []))
    ns = len(new_sems)
    return list(outs[:ns]), outs[ns:ns + n], outs[ns + n:ns + 2 * n], (outs[-1] if extra else None)


N_SEND_SLOTS = 3


def _exchange_last(grads, small_packed):
    ng = len(grads)
    ch = SMALL_ROWS // N_DEV
    max_rows = max(g.shape[1] for g in grads)
    cols = grads[0].shape[2]

    def body(*refs):
        g_in, s_in = refs[:ng], refs[ng]
        outs = refs[ng + 1:]
        own_out, land, stage = outs[:ng], outs[ng:2 * ng], outs[2 * ng:3 * ng]
        s_red, s_stage = outs[3 * ng], outs[3 * ng + 1]
        (va, vb, vo, vs, sm_in, sm_out, d2d_send, d2d_recv, ici_send, ici_recv, s1_send, s1_recv, s2_send, s2_recv,
         local_sems) = outs[3 * ng + 2:]
        x, y, c, me = _mesh_pos()
        sibling = (x, y, 1 - c)
        chips = [(x, y), (x, 1 - y), (1 - x, y), (1 - x, 1 - y)]

        def idx(chip, core):
            return 4 * chip[0] + 2 * chip[1] + core

        def d2d(k, j):
            return pltpu.make_async_remote_copy(
                src_ref=g_in[k].at[idx(chips[j], 1 - c)], dst_ref=stage[k].at[j], send_sem=d2d_send.at[k, j],
                recv_sem=d2d_recv.at[k, j], device_id=sibling, device_id_type=MESH_ID)

        def ici(k, j, slot):
            rows = g_in[k].shape[1]
            return pltpu.make_async_remote_copy(
                src_ref=vo.at[slot, pl.ds(0, rows)], dst_ref=land[k].at[j - 1], send_sem=ici_send.at[k, j - 1],
                recv_sem=ici_recv.at[k, j - 1], device_id=(*chips[j], c), device_id_type=MESH_ID)

        def small_scatter(r):
            px, py, pc = _peer(x, y, c, r)
            return pltpu.make_async_remote_copy(
                src_ref=s_in.at[pl.ds(pl.multiple_of((4 * px + 2 * py + pc) * ch, 8), ch)], dst_ref=s_stage.at[me],
                send_sem=s1_send.at[r - 1], recv_sem=s1_recv.at[r - 1], device_id=(px, py, pc), device_id_type=MESH_ID)

        def small_gather(r):
            return pltpu.make_async_remote_copy(
                src_ref=sm_out, dst_ref=s_red.at[me], send_sem=s2_send.at[r - 1], recv_sem=s2_recv.at[r - 1],
                device_id=_peer(x, y, c, r), device_id_type=MESH_ID)

        for r in range(1, N_DEV):
            small_scatter(r).start()
        mine = pltpu.make_async_copy(s_in.at[pl.ds(pl.multiple_of(me * ch, 8), ch)], s_stage.at[me], local_sems.at[0])
        mine.start()
        pairs = [(k, j) for k in range(ng) for j in (1, 2, 3)] + [(k, 0) for k in range(ng)]
        for k, j in pairs:
            d2d(k, j).start()

        for r in range(1, N_DEV):
            small_scatter(r).wait_recv()
        mine.wait()
        load = pltpu.make_async_copy(s_stage, sm_in, local_sems.at[1])
        load.start()
        load.wait()
        total = sm_in[0]
        for i in range(1, N_DEV):
            total = total + sm_in[i]
        sm_out[...] = total
        for r in range(1, N_DEV):
            small_gather(r).start()
        keep = pltpu.make_async_copy(sm_out, s_red.at[me], local_sems.at[2])
        keep.start()

        in_flight = {}
        for i, (k, j) in enumerate(pairs):
            slot = i % N_SEND_SLOTS
            rows = g_in[k].shape[1]
            if slot in in_flight:
                in_flight.pop(slot).wait_send()
            d2d(k, j).wait_recv()
            la = pltpu.make_async_copy(g_in[k].at[idx(chips[j], c)], va.at[pl.ds(0, rows)], local_sems.at[3])
            lb = pltpu.make_async_copy(stage[k].at[j], vb.at[pl.ds(0, rows)], local_sems.at[4])
            la.start()
            lb.start()
            la.wait()
            lb.wait()
            total = va[pl.ds(0, rows)].astype(F32) + vb[pl.ds(0, rows)].astype(F32)
            if j == 0:
                vs[pl.ds(0, rows)] = total
                st = pltpu.make_async_copy(vs.at[pl.ds(0, rows)], own_out[k], local_sems.at[5])
                st.start()
                st.wait()
            else:
                vo[slot, pl.ds(0, rows)] = total.astype(BF16)
                cp = ici(k, j, slot)
                cp.start()
                in_flight[slot] = cp
        for cp in in_flight.values():
            cp.wait_send()

        for j in (1, 2, 3, 0):
            for k in range(ng):
                d2d(k, j).wait_send()
        for j in (1, 2, 3):
            for k in range(ng):
                ici(k, j, 0).wait_recv()
        for r in range(1, N_DEV):
            small_scatter(r).wait_send()
            small_gather(r).wait_send()
            small_gather(r).wait_recv()
        keep.wait()

    out_shape = [jax.ShapeDtypeStruct(g.shape[1:], F32) for g in grads]
    out_shape += [jax.ShapeDtypeStruct((3,) + g.shape[1:], BF16) for g in grads]
    out_shape += [jax.ShapeDtypeStruct((4,) + g.shape[1:], BF16) for g in grads]
    out_shape += [jax.ShapeDtypeStruct((N_DEV, ch, LANES), F32), jax.ShapeDtypeStruct((N_DEV, ch, LANES), F32)]
    outs = pl.pallas_call(
        body, name="exchange_last",
        in_specs=[ANY_SPEC] * (ng + 1), out_specs=[ANY_SPEC] * len(out_shape), out_shape=out_shape,
        scratch_shapes=[pltpu.VMEM((max_rows, cols), BF16), pltpu.VMEM((max_rows, cols), BF16),
                        pltpu.VMEM((N_SEND_SLOTS, max_rows, cols), BF16), pltpu.VMEM((max_rows, cols), F32),
                        pltpu.VMEM((N_DEV, ch, LANES), F32), pltpu.VMEM((ch, LANES), F32),
                        pltpu.SemaphoreType.DMA((ng, 4)), pltpu.SemaphoreType.DMA((ng, 4)),
                        pltpu.SemaphoreType.DMA((ng, 3)), pltpu.SemaphoreType.DMA((ng, 3)),
                        pltpu.SemaphoreType.DMA((N_PEERS,)), pltpu.SemaphoreType.DMA((N_PEERS,)),
                        pltpu.SemaphoreType.DMA((N_PEERS,)), pltpu.SemaphoreType.DMA((N_PEERS,)),
                        pltpu.SemaphoreType.DMA((6,))],
        compiler_params=pltpu.CompilerParams(has_side_effects=True, vmem_limit_bytes=VMEM_LIMIT),
    )(*grads, small_packed)
    return outs[:ng], outs[ng:2 * ng], outs[3 * ng].reshape(SMALL_ROWS, LANES)


def _adamw_math(w, g, m, v):
    m2 = ADAM_B1 * m + (1.0 - ADAM_B1) * g
    v2 = ADAM_B2 * v + (1.0 - ADAM_B2) * (g * g)
    m_hat = m2 / (1.0 - ADAM_B1 ** ADAM_STEP)
    v_hat = v2 / (1.0 - ADAM_B2 ** ADAM_STEP)
    delta = -ADAM_LR * (m_hat / (jnp.sqrt(v_hat) + ADAM_EPS) + ADAM_WD * w)
    return delta, m2, v2


ADAM_ROW_TILES = 2


def _adamw_big(own, parts, w, m, v, name):
    shape = w.shape
    own_is_blocks = own.ndim == 3
    tr = shape[0] // ADAM_ROW_TILES
    n_parts = parts.shape[0]

    def body(own_ref, p_ref, w_ref, m_ref, v_ref, g_ref, d_ref, m2_ref, v2_ref, own_s, sem):
        rows = pl.ds(pl.multiple_of(pl.program_id(0) * tr, 16), tr)
        if own_is_blocks:
            cp = pltpu.make_async_copy(own_ref.at[_mesh_pos()[3], rows], own_s, sem)
        else:
            cp = pltpu.make_async_copy(own_ref.at[rows], own_s, sem)
        cp.start()
        cp.wait()
        g = own_s[...].astype(F32)
        for i in range(n_parts):
            g = g + p_ref[i].astype(F32)
        delta, m2, v2 = _adamw_math(w_ref[...], g, m_ref[...], v_ref[...])
        g_ref[...] = g
        d_ref[...] = delta
        m2_ref[...] = m2
        v2_ref[...] = v2

    tile = pl.BlockSpec((tr, shape[1]), lambda i: (i, 0))
    return pl.pallas_call(
        body, name=name, grid=(ADAM_ROW_TILES,),
        in_specs=[ANY_SPEC, pl.BlockSpec((n_parts, tr, shape[1]), lambda i: (0, i, 0)), tile, tile, tile],
        out_specs=[tile] * 4, out_shape=[jax.ShapeDtypeStruct(shape, F32)] * 4,
        scratch_shapes=[pltpu.VMEM((tr, shape[1]), own.dtype), pltpu.SemaphoreType.DMA(())],
        compiler_params=_cparams(("arbitrary",)),
    )(own, parts, w, m, v)


def _pack_small(grads):
    names = list(SMALL)

    def body(*refs):
        ins, out = dict(zip(names, refs[:-1])), refs[-1]
        out[...] = jnp.zeros_like(out)
        for re, im in SMALL_PAIRS:
            off, rows = SMALL_OFFSET[re], SMALL[re][0]
            out[off:off + rows, :] = jnp.concatenate([ins[re][...], ins[im][...]], axis=1)
        for n in SMALL_VECS:
            off, vec = SMALL_OFFSET[n], ins[n][...]
            for i in range(SMALL[n][1] // LANES):
                out[off + i:off + i + 1, :] = vec[:, i * LANES:(i + 1) * LANES]
        for n in SMALL_TILES:
            off, (rows, cols) = SMALL_OFFSET[n], SMALL[n]
            out[off:off + rows, 0:cols] = ins[n][...]

    return pl.pallas_call(
        body, name="pack_small", out_shape=jax.ShapeDtypeStruct((SMALL_ROWS, LANES), F32),
        compiler_params=_cparams(),
    )(*[grads[n] for n in names])


def _unpack_small_ref(g_ref, n):
    off, (rows, cols) = SMALL_OFFSET[n], SMALL[n]
    for re, im in SMALL_PAIRS:
        if n == re:
            return g_ref[off:off + rows, 0:HALF_LANES]
        if n == im:
            return g_ref[off:off + rows, HALF_LANES:LANES]
    if n in SMALL_VECS:
        return jnp.concatenate([g_ref[off + i:off + i + 1, :] for i in range(cols // LANES)], axis=1)
    return g_ref[off:off + rows, 0:cols]


def _adamw_small(g_packed, w, m, v):
    names = list(SMALL_PARAMS)
    n = len(names)

    def body(g_ref, *refs):
        w_refs, m_refs, v_refs, outs = refs[:n], refs[n:2 * n], refs[2 * n:3 * n], refs[3 * n:]
        for idx, name in enumerate(names):
            g = _unpack_small_ref(g_ref, name)
            delta, m2, v2 = _adamw_math(w_refs[idx][...], g, m_refs[idx][...], v_refs[idx][...])
            outs[4 * idx][...] = g
            outs[4 * idx + 1][...] = delta
            outs[4 * idx + 2][...] = m2
            outs[4 * idx + 3][...] = v2
        outs[4 * n][...] = _unpack_small_ref(g_ref, "loss")

    outs = pl.pallas_call(
        body, name="adamw_small",
        out_shape=[jax.ShapeDtypeStruct(SMALL[name], F32) for name in names for _ in range(4)]
        + [jax.ShapeDtypeStruct(SMALL["loss"], F32)],
        compiler_params=_cparams(),
    )(g_packed, *[w[k] for k in names], *[m[k] for k in names], *[v[k] for k in names])
    return {name: outs[4 * idx:4 * idx + 4] for idx, name in enumerate(names)}, outs[4 * n]


WEIGHT_NAMES = ['norm_ffn1', 'ffn1_w_gate', 'ffn1_w_up', 'ffn1_w_down', 'norm_mix', 'w_in', 'attn_sinks',
                'ssm_lambda_re', 'ssm_lambda_im', 'ssm_log_dt', 'ssm_b_re', 'ssm_b_im', 'ssm_c_re', 'ssm_c_im',
                'ssm_d', 'ssm_glu_w', 'ssm_glu_b', 'attn_out_norm', 'ssm_out_norm', 'w_out', 'norm_ffn2',
                'ffn2_w_gate', 'ffn2_w_up', 'ffn2_w_down', 'final_norm']


def kernel(x, norm_ffn1, ffn1_w_gate, ffn1_w_up, ffn1_w_down, norm_mix, w_in, attn_sinks, ssm_lambda_re, ssm_lambda_im, ssm_log_dt, ssm_b_re, ssm_b_im, ssm_c_re, ssm_c_im, ssm_d, ssm_glu_w, ssm_glu_b, attn_out_norm, ssm_out_norm, w_out, norm_ffn2, ffn2_w_gate, ffn2_w_up, ffn2_w_down, final_norm, loss_target, m_norm_ffn1, m_ffn1_w_gate, m_ffn1_w_up, m_ffn1_w_down, m_norm_mix, m_w_in, m_attn_sinks, m_ssm_lambda_re, m_ssm_lambda_im, m_ssm_log_dt, m_ssm_b_re, m_ssm_b_im, m_ssm_c_re, m_ssm_c_im, m_ssm_d, m_ssm_glu_w, m_ssm_glu_b, m_attn_out_norm, m_ssm_out_norm, m_w_out, m_norm_ffn2, m_ffn2_w_gate, m_ffn2_w_up, m_ffn2_w_down, m_final_norm, v_norm_ffn1, v_ffn1_w_gate, v_ffn1_w_up, v_ffn1_w_down, v_norm_mix, v_w_in, v_attn_sinks, v_ssm_lambda_re, v_ssm_lambda_im, v_ssm_log_dt, v_ssm_b_re, v_ssm_b_im, v_ssm_c_re, v_ssm_c_im, v_ssm_d, v_ssm_glu_w, v_ssm_glu_b, v_attn_out_norm, v_ssm_out_norm, v_w_out, v_norm_ffn2, v_ffn2_w_gate, v_ffn2_w_up, v_ffn2_w_down, v_final_norm):
    args = dict(locals())
    weights = {n: args[n] for n in WEIGHT_NAMES}
    moms = {n: args["m_" + n] for n in WEIGHT_NAMES}
    vars_ = {n: args["v_" + n] for n in WEIGHT_NAMES}

    def shard2d(a, k):
        a = a.reshape(a.shape[-2], a.shape[-1])
        return a.T if BIG[k][3] else a

    def shard_master(a, k):
        return (a.T if BIG[k][3] else a).reshape(weights[BIG[k][0]].shape)

    def blocks(g, k):
        return g.reshape(N_DEV, BIG[k][1], BIG[k][2])

    def full(g, k):
        return g.reshape(N_DEV * BIG[k][1], BIG[k][2])

    shards = dict(zip(BIG, _cast_shards({k: shard2d(weights[BIG[k][0]], k) for k in BIG})))
    nf = len(FIRST_GROUP)
    got = _gather_first([shards[k] for k in FIRST_GROUP], [shards[k] for k in LATE_GROUP])
    w_first = {k: full(g, k) for k, g in zip(FIRST_GROUP, got[:nf])}
    late = {}
    late["own_sems"], late["srcs"], late["lands"], w_token = _late_gather_call(
        "gather_late_start", 0, [shards[k] for k in LATE_GROUP], got[nf:], [])

    def late_pass(dep):
        late["pass_sems"], late["srcs"], late["lands"], token = _late_gather_call(
            "gather_late_pass", 1, late["srcs"], late["lands"], late["own_sems"], after=dep)
        return token

    def late_weights(dep):
        _, _, lands, _ = _late_gather_call("gather_late_wait", 2, late["srcs"], late["lands"],
                                           late["own_sems"] + late["pass_sems"], after=dep)
        return {k: full(g, k) for k, g in zip(LATE_GROUP, lands)}

    early = {}

    def early_grads(g):
        srcs = [blocks(g[k], k) for k in LATE_GROUP]
        lands = [lax.empty((N_PEERS, BIG[k][1], BIG[k][2]), BF16) for k in LATE_GROUP]
        early["send"], early["recv"], early["srcs"], early["lands"], token = _split_start(
            "grads_late_start", srcs, lands, scatter=True)
        return token

    def small2d(a, n):
        if n in SMALL_TRANSPOSED:
            a = jnp.swapaxes(a, -1, -2)
        return a.reshape(SMALL[n])

    def small_master(a, n):
        if n in SMALL_TRANSPOSED:
            shape = weights[n].shape
            return jnp.swapaxes(a.reshape(shape[:-2] + (shape[-1], shape[-2])), -1, -2)
        return a.reshape(weights[n].shape)

    small_p = {n: small2d(weights[n], n) for n in SMALL_PARAMS}
    _, grad_x, g_first, g_small = _local_step(
        x.reshape(SEQ, D_MODEL), loss_target.reshape(SEQ, D_MODEL), w_first, small_p, late_weights, early_grads,
        after=w_token, midway=late_pass)

    own_sums, first_parts, small_grad = _exchange_last([blocks(g_first[k], k) for k in FIRST_GROUP],
                                                       _pack_small(g_small))
    own_late, late_parts = _split_wait("grads_late_wait", early["send"], early["recv"], early["srcs"],
                                       early["lands"], True, small_grad)
    own = dict(zip(FIRST_GROUP + LATE_GROUP, list(own_sums) + list(own_late)))
    parts = dict(zip(FIRST_GROUP + LATE_GROUP, list(first_parts) + list(late_parts)))
    outs = {}
    for k in BIG:
        n = BIG[k][0]
        outs[n] = [shard_master(o, k) for o in
                   _adamw_big(own[k], parts[k], shard2d(weights[n], k), shard2d(moms[n], k), shard2d(vars_[n], k),
                              "adamw_" + n)]
    small_out, loss_row = _adamw_small(small_grad, small_p, {n: small2d(moms[n], n) for n in SMALL_PARAMS},
                                       {n: small2d(vars_[n], n) for n in SMALL_PARAMS})
    for n in SMALL_PARAMS:
        outs[n] = [small_master(o, n) for o in small_out[n]]

    result = [loss_row[0, 0], grad_x.reshape(x.shape)]
    for i in range(4):
        result += [outs[n][i] for n in WEIGHT_NAMES]
    return tuple(result)
```

```python
import functools

import jax
import jax.numpy as jnp
from jax import lax
from jax.experimental import pallas as pl
from jax.experimental.pallas import tpu as pltpu

F32 = jnp.float32
BF16 = jnp.bfloat16

N_DEV = 8
SEQ = 2048
D_MODEL = 1024
D_FF = 2816
ATTN_HEADS = 8
KV_HEADS = 2
HEAD_DIM = 64
ATTN_WIDTH = 512
KV_WIDTH = 128
WINDOW = 128
SSM_WIDTH = 512
IN_WIDTH = 1280
EPS = 1e-6
MASKED_DISTANCE = 1e33
LAMBDA_RE_MAX = -1e-4
LANES = 128
N_LANE_BLOCKS = 16
SCAN_CHUNK = SEQ // 8

ADAM_LR = 0.001
ADAM_B1 = 0.9
ADAM_B2 = 0.999
ADAM_EPS = 1e-08
ADAM_WD = 0.01
ADAM_STEP = 10

VMEM_LIMIT = 60 * 1024 * 1024
MESH_ID = pl.DeviceIdType.MESH


def _cparams(sem=None):
    return pltpu.CompilerParams(dimension_semantics=sem, vmem_limit_bytes=VMEM_LIMIT)


def _dot(a, b):
    return jnp.dot(a, b, preferred_element_type=F32)


def _dot_nt(a, b):
    return lax.dot_general(a, b, (((1,), (1,)), ((), ())), preferred_element_type=F32)


def _dot_tn(a, b):
    return lax.dot_general(a, b, (((0,), (0,)), ((), ())), preferred_element_type=F32)


def _rms_fwd(x, g):
    r = lax.rsqrt(jnp.mean(x * x, axis=-1, keepdims=True) + EPS)
    return x * r * g


def _rms_bwd(dh, x, g):
    r = lax.rsqrt(jnp.mean(x * x, axis=-1, keepdims=True) + EPS)
    xh = x * r
    dg = jnp.sum(dh * xh, axis=0, keepdims=True)
    dxh = dh * g
    dx = r * (dxh - xh * jnp.mean(dxh * xh, axis=-1, keepdims=True))
    return dx, dg


def _sigmoid(x):
    return 1.0 / (1.0 + jnp.exp(-x))


FFN_TM = 512
FFN_TF = 1408


def _ffn_fwd(x, g, wgt, wut, wd, name, after=None, head=None):
    tm, tf = FFN_TM, FFN_TF
    nj = D_FF // tf
    deps = [] if after is None else [after]
    n_in = len(deps) + (2 if head else 0)

    def body(x_ref, g_ref, wg_ref, wu_ref, wd_ref, *rest):
        i = pl.program_id(0)
        j = pl.program_id(1)
        if head:
            gf_ref, t_ref = rest[len(deps):n_in]
            xo_ref, h_ref, a_ref, b_ref, loss_ref, dgf_ref, h_s, acc = rest[n_in:]
        else:
            xo_ref, h_ref, a_ref, b_ref, h_s, acc = rest[n_in:]

        @pl.when(j == 0)
        def _():
            h = _rms_fwd(x_ref[...], g_ref[...]).astype(BF16)
            h_s[...] = h
            h_ref[...] = h
            acc[...] = jnp.zeros_like(acc)

        h = h_s[...]
        a = _dot_nt(h, wg_ref[...])
        b = _dot_nt(h, wu_ref[...])
        a_ref[...] = a.astype(BF16)
        b_ref[...] = b.astype(BF16)
        s = (a * _sigmoid(a) * b).astype(BF16)
        acc[...] += _dot(s, wd_ref[...])

        @pl.when(j == nj - 1)
        def _():
            xo = x_ref[...] + 0.5 * acc[...]
            if not head:
                xo_ref[...] = xo
                return
            gf = gf_ref[...]
            err = _rms_fwd(xo, gf) - t_ref[...]
            part = jnp.broadcast_to(0.5 * jnp.sum(err * err) / D_MODEL, (1, LANES))
            dx, dgf = _rms_bwd(err * (1.0 / D_MODEL), xo, gf)
            xo_ref[...] = dx

            @pl.when(i == 0)
            def _():
                loss_ref[...] = part
                dgf_ref[...] = dgf

            @pl.when(i != 0)
            def _():
                loss_ref[...] += part
                dgf_ref[...] += dgf

    row = lambda i, j: (i, 0)
    const = lambda i, j: (0, 0)
    head_in = [pl.BlockSpec((1, D_MODEL), const), pl.BlockSpec((tm, D_MODEL), row)] if head else []
    head_out = [pl.BlockSpec((1, LANES), const), pl.BlockSpec((1, D_MODEL), const)] if head else []
    head_shape = [jax.ShapeDtypeStruct((1, LANES), F32), jax.ShapeDtypeStruct((1, D_MODEL), F32)] if head else []
    return pl.pallas_call(
        body, name=name, grid=(SEQ // tm, nj),
        in_specs=[pl.BlockSpec((tm, D_MODEL), row), pl.BlockSpec((1, D_MODEL), const),
                  pl.BlockSpec((tf, D_MODEL), lambda i, j: (j, 0)),
                  pl.BlockSpec((tf, D_MODEL), lambda i, j: (j, 0)),
                  pl.BlockSpec((tf, D_MODEL), lambda i, j: (j, 0))] + [pl.BlockSpec(memory_space=pl.ANY)] * len(deps)
        + head_in,
        out_specs=[pl.BlockSpec((tm, D_MODEL), row), pl.BlockSpec((tm, D_MODEL), row),
                   pl.BlockSpec((tm, tf), lambda i, j: (i, j)),
                   pl.BlockSpec((tm, tf), lambda i, j: (i, j))] + head_out,
        out_shape=[jax.ShapeDtypeStruct((SEQ, D_MODEL), F32), jax.ShapeDtypeStruct((SEQ, D_MODEL), BF16),
                   jax.ShapeDtypeStruct((SEQ, D_FF), BF16), jax.ShapeDtypeStruct((SEQ, D_FF), BF16)] + head_shape,
        scratch_shapes=[pltpu.VMEM((tm, D_MODEL), BF16), pltpu.VMEM((tm, D_MODEL), F32)],
        compiler_params=_cparams(("arbitrary" if head else "parallel", "arbitrary")),
    )(x, g, wgt, wut, wd, *deps, *(head or ()))


def _ffn_bwd_act(dxo, x, g, a, b, wgt, wut, wd, name):
    tm, tf = FFN_TM, FFN_TF
    nj = D_FF // tf

    def body(dxo_ref, x_ref, g_ref, a_ref, b_ref, wg_ref, wu_ref, wd_ref,
             dx_ref, da_ref, db_ref, s_ref, df_ref, dg_ref, df_s, acc):
        i = pl.program_id(0)
        j = pl.program_id(1)

        @pl.when(j == 0)
        def _():
            df = (0.5 * dxo_ref[...]).astype(BF16)
            df_s[...] = df
            df_ref[...] = df
            acc[...] = jnp.zeros_like(acc)

        ds = _dot_nt(df_s[...], wd_ref[...])
        av = a_ref[...].astype(F32)
        bv = b_ref[...].astype(F32)
        sig = _sigmoid(av)
        sl = av * sig
        s_ref[...] = (sl * bv).astype(BF16)
        db = (ds * sl).astype(BF16)
        da = (ds * bv * (sig * (1.0 + av * (1.0 - sig)))).astype(BF16)
        da_ref[...] = da
        db_ref[...] = db
        acc[...] += _dot(da, wg_ref[...]) + _dot(db, wu_ref[...])

        @pl.when(j == nj - 1)
        def _():
            dx, dg = _rms_bwd(acc[...], x_ref[...], g_ref[...])
            dx_ref[...] = dxo_ref[...] + dx

            @pl.when(i == 0)
            def _():
                dg_ref[...] = dg

            @pl.when(i != 0)
            def _():
                dg_ref[...] += dg

    row = lambda i, j: (i, 0)
    col = lambda i, j: (j, 0)
    tile = lambda i, j: (i, j)
    return pl.pallas_call(
        body, name=name, grid=(SEQ // tm, nj),
        in_specs=[pl.BlockSpec((tm, D_MODEL), row), pl.BlockSpec((tm, D_MODEL), row),
                  pl.BlockSpec((1, D_MODEL), lambda i, j: (0, 0)),
                  pl.BlockSpec((tm, tf), tile), pl.BlockSpec((tm, tf), tile),
                  pl.BlockSpec((tf, D_MODEL), col), pl.BlockSpec((tf, D_MODEL), col), pl.BlockSpec((tf, D_MODEL), col)],
        out_specs=[pl.BlockSpec((tm, D_MODEL), row),
                   pl.BlockSpec((tm, tf), tile), pl.BlockSpec((tm, tf), tile), pl.BlockSpec((tm, tf), tile),
                   pl.BlockSpec((tm, D_MODEL), row),
                   pl.BlockSpec((1, D_MODEL), lambda i, j: (0, 0))],
        out_shape=[jax.ShapeDtypeStruct((SEQ, D_MODEL), F32),
                   jax.ShapeDtypeStruct((SEQ, D_FF), BF16), jax.ShapeDtypeStruct((SEQ, D_FF), BF16),
                   jax.ShapeDtypeStruct((SEQ, D_FF), BF16),
                   jax.ShapeDtypeStruct((SEQ, D_MODEL), BF16),
                   jax.ShapeDtypeStruct((1, D_MODEL), F32)],
        scratch_shapes=[pltpu.VMEM((tm, D_MODEL), BF16), pltpu.VMEM((tm, D_MODEL), F32)],
        compiler_params=_cparams(("arbitrary", "arbitrary")),
    )(dxo, x, g, a, b, wgt, wut, wd)


def _mm_tn(pairs, name, tmm=256):
    m = pairs[0][0].shape[1]
    n_pairs = len(pairs)

    def body(*refs):
        ins, outs = refs[:2 * n_pairs], refs[2 * n_pairs:]
        for p in range(n_pairs):
            outs[p][...] = _dot_tn(ins[2 * p][...], ins[2 * p + 1][...]).astype(BF16)

    in_specs, out_specs, out_shape, args = [], [], [], []
    for a, b in pairs:
        n = b.shape[1]
        in_specs += [pl.BlockSpec((SEQ, tmm), lambda i: (0, i)), pl.BlockSpec((SEQ, n), lambda i: (0, 0))]
        out_specs.append(pl.BlockSpec((tmm, n), lambda i: (i, 0)))
        out_shape.append(jax.ShapeDtypeStruct((m, n), BF16))
        args += [a, b]
    return pl.pallas_call(body, name=name, grid=(m // tmm,), in_specs=in_specs, out_specs=out_specs,
                          out_shape=out_shape, compiler_params=_cparams(("parallel",)))(*args)


MIX_TM = 512


def _mixin_fwd(x, g, wint):
    tm = MIX_TM

    def body(x_ref, g_ref, w_ref, h_ref, q_ref, k_ref, v_ref, u_ref):
        h = _rms_fwd(x_ref[...], g_ref[...]).astype(BF16)
        h_ref[...] = h
        proj = _dot_nt(h, w_ref[...])
        q_ref[...] = proj[:, :ATTN_WIDTH].T
        k_ref[...] = proj[:, ATTN_WIDTH:ATTN_WIDTH + KV_WIDTH]
        v_ref[...] = proj[:, ATTN_WIDTH + KV_WIDTH:ATTN_WIDTH + 2 * KV_WIDTH]
        u_ref[...] = proj[:, ATTN_WIDTH + 2 * KV_WIDTH:]

    row = lambda i: (i, 0)
    return pl.pallas_call(
        body, name="mixin_fwd", grid=(SEQ // tm,),
        in_specs=[pl.BlockSpec((tm, D_MODEL), row), pl.BlockSpec((1, D_MODEL), lambda i: (0, 0)),
                  pl.BlockSpec((IN_WIDTH, D_MODEL), lambda i: (0, 0))],
        out_specs=[pl.BlockSpec((tm, D_MODEL), row), pl.BlockSpec((ATTN_WIDTH, tm), lambda i: (0, i)),
                   pl.BlockSpec((tm, KV_WIDTH), row), pl.BlockSpec((tm, KV_WIDTH), row),
                   pl.BlockSpec((tm, SSM_WIDTH), row)],
        out_shape=[jax.ShapeDtypeStruct((SEQ, D_MODEL), BF16), jax.ShapeDtypeStruct((ATTN_WIDTH, SEQ), F32),
                   jax.ShapeDtypeStruct((SEQ, KV_WIDTH), F32), jax.ShapeDtypeStruct((SEQ, KV_WIDTH), F32),
                   jax.ShapeDtypeStruct((SEQ, SSM_WIDTH), F32)],
        compiler_params=_cparams(("parallel",)),
    )(x, g, wint)


def _mixin_bwd(dqt, dk, dv, du, wint, x, g, dres):
    tm = MIX_TM

    def body(dq_ref, dk_ref, dv_ref, du_ref, w_ref, x_ref, g_ref, dres_ref, dx_ref, dp_ref, dg_ref):
        i = pl.program_id(0)
        dp = jnp.concatenate([dq_ref[...].T, dk_ref[...], dv_ref[...], du_ref[...]], axis=-1).astype(BF16)
        dp_ref[...] = dp
        dh = _dot(dp, w_ref[...])
        dx, dg = _rms_bwd(dh, x_ref[...], g_ref[...])
        dx_ref[...] = dres_ref[...] + dx

        @pl.when(i == 0)
        def _():
            dg_ref[...] = dg

        @pl.when(i != 0)
        def _():
            dg_ref[...] += dg

    row = lambda i: (i, 0)
    const = lambda i: (0, 0)
    return pl.pallas_call(
        body, name="mixin_bwd", grid=(SEQ // tm,),
        in_specs=[pl.BlockSpec((ATTN_WIDTH, tm), lambda i: (0, i)), pl.BlockSpec((tm, KV_WIDTH), row),
                  pl.BlockSpec((tm, KV_WIDTH), row), pl.BlockSpec((tm, SSM_WIDTH), row),
                  pl.BlockSpec((IN_WIDTH, D_MODEL), const), pl.BlockSpec((tm, D_MODEL), row),
                  pl.BlockSpec((1, D_MODEL), const), pl.BlockSpec((tm, D_MODEL), row)],
        out_specs=[pl.BlockSpec((tm, D_MODEL), row), pl.BlockSpec((tm, IN_WIDTH), row),
                   pl.BlockSpec((1, D_MODEL), const)],
        out_shape=[jax.ShapeDtypeStruct((SEQ, D_MODEL), F32), jax.ShapeDtypeStruct((SEQ, IN_WIDTH), BF16),
                   jax.ShapeDtypeStruct((1, D_MODEL), F32)],
        compiler_params=_cparams(("arbitrary",)),
    )(dqt, dk, dv, du, wint, x, g, dres)


N_QBLOCKS = SEQ // WINDOW
GROUP = ATTN_HEADS // KV_HEADS
SCALE = HEAD_DIM ** -0.5


def _alibi_slope(h):
    return 2.0 ** (-8.0 * (h + 1) / ATTN_HEADS)


def _window_masks(n):
    s_idx = lax.broadcasted_iota(jnp.int32, (3 * WINDOW, WINDOW), 0)
    t_idx = lax.broadcasted_iota(jnp.int32, (3 * WINDOW, WINDOW), 1)
    absrel = jnp.abs(s_idx - WINDOW - t_idx)
    key_pos = n * WINDOW - WINDOW + s_idx
    valid = (absrel <= WINDOW) & (key_pos >= 0) & (key_pos < SEQ)
    return jnp.where(valid, absrel.astype(F32), MASKED_DISTANCE)


def _group_cols(ref, r0, gi):
    return jnp.concatenate(
        [ref[(gi * GROUP + hh) * HEAD_DIM:(gi * GROUP + hh + 1) * HEAD_DIM, pl.ds(r0, WINDOW)].astype(BF16)
         for hh in range(GROUP)], axis=1)


def _group_probs(qgt, kw, dist, gi, sk_ref):
    bias = jnp.concatenate([-_alibi_slope(gi * GROUP + hh) * dist for hh in range(GROUP)], axis=1)
    sink = jnp.concatenate([jnp.full((1, WINDOW), sk_ref[0, gi * GROUP + hh], F32) for hh in range(GROUP)], axis=1)
    s = _dot(kw, qgt) * SCALE + bias
    m = jnp.maximum(jnp.max(s, axis=0, keepdims=True), sink)
    p = jnp.exp(s - m)
    ps = jnp.exp(sink - m)
    inv = 1.0 / (jnp.sum(p, axis=0, keepdims=True) + ps)
    return p * inv, ps * inv


def _pad_window(src_ref, dst_ref):
    zeros = jnp.zeros((WINDOW, KV_WIDTH), BF16)
    dst_ref[0:WINDOW, :] = zeros
    dst_ref[WINDOW + SEQ:, :] = zeros
    dst_ref[WINDOW:WINDOW + SEQ, :] = src_ref[...].astype(BF16)


def _attn_fwd(qt, k, v, sinks, after=None):
    deps = [] if after is None else [after]

    def body(sk_ref, qt_ref, k_ref, v_ref, *rest):
        o_ref, kp_ref, vp_ref = rest[len(deps):]
        _pad_window(k_ref, kp_ref)
        _pad_window(v_ref, vp_ref)

        def blk(n, carry):
            r0 = pl.multiple_of(n * WINDOW, WINDOW)
            dist = _window_masks(n)
            for gi in range(KV_HEADS):
                kw = kp_ref[pl.ds(r0, 3 * WINDOW), gi * HEAD_DIM:(gi + 1) * HEAD_DIM]
                vw = vp_ref[pl.ds(r0, 3 * WINDOW), gi * HEAD_DIM:(gi + 1) * HEAD_DIM]
                pr, _ = _group_probs(_group_cols(qt_ref, r0, gi), kw, dist, gi, sk_ref)
                og = _dot_tn(pr.astype(BF16), vw)
                for hh in range(GROUP):
                    h = gi * GROUP + hh
                    o_ref[pl.ds(r0, WINDOW), h * HEAD_DIM:(h + 1) * HEAD_DIM] = og[hh * WINDOW:(hh + 1) * WINDOW]
            return carry

        lax.fori_loop(0, N_QBLOCKS, blk, 0)

    vmem = pl.BlockSpec(memory_space=pltpu.VMEM)
    return pl.pallas_call(
        body, name="attn_fwd",
        in_specs=[pl.BlockSpec(memory_space=pltpu.SMEM), vmem, vmem, vmem]
        + [pl.BlockSpec(memory_space=pl.ANY)] * len(deps), out_specs=vmem,
        out_shape=jax.ShapeDtypeStruct((SEQ, ATTN_WIDTH), F32),
        scratch_shapes=[pltpu.VMEM((SEQ + 2 * WINDOW, KV_WIDTH), BF16)] * 2,
        compiler_params=_cparams(),
    )(sinks, qt, k, v, *deps)


def _attn_bwd(qt, k, v, sinks, dot_):
    def body(sk_ref, qt_ref, k_ref, v_ref, dot_ref, dqt_ref, dk_ref, dv_ref, dsk_ref,
             dsk_acc, kp_ref, vp_ref, dkp_ref, dvp_ref):
        _pad_window(k_ref, kp_ref)
        _pad_window(v_ref, vp_ref)
        dkp_ref[...] = jnp.zeros_like(dkp_ref)
        dvp_ref[...] = jnp.zeros_like(dvp_ref)
        dsk_acc[...] = jnp.zeros_like(dsk_acc)

        def blk(n, carry):
            r0 = pl.multiple_of(n * WINDOW, WINDOW)
            dist = _window_masks(n)
            for gi in range(KV_HEADS):
                gcols = slice(gi * HEAD_DIM, (gi + 1) * HEAD_DIM)
                kw = kp_ref[pl.ds(r0, 3 * WINDOW), gcols]
                vw = vp_ref[pl.ds(r0, 3 * WINDOW), gcols]
                qgt = _group_cols(qt_ref, r0, gi)
                dogt = _group_cols(dot_ref, r0, gi)
                pr, psink = _group_probs(qgt, kw, dist, gi, sk_ref)
                dp = _dot(vw, dogt)
                delta = jnp.sum(pr * dp, axis=0, keepdims=True)
                ds = (pr * (dp - delta)).astype(BF16)
                dsk_acc[gi:gi + 1, :] += -(psink * delta)
                dqgt = _dot_tn(kw, ds) * SCALE
                for hh in range(GROUP):
                    h = gi * GROUP + hh
                    dqt_ref[h * HEAD_DIM:(h + 1) * HEAD_DIM, pl.ds(r0, WINDOW)] = dqgt[:, hh * WINDOW:(hh + 1) * WINDOW]
                dkp_ref[pl.ds(r0, 3 * WINDOW), gcols] += _dot_nt(ds, qgt) * SCALE
                dvp_ref[pl.ds(r0, 3 * WINDOW), gcols] += _dot_nt(pr.astype(BF16), dogt)
            return carry

        lax.fori_loop(0, N_QBLOCKS, blk, 0)
        for h in range(ATTN_HEADS):
            gi, hh = divmod(h, GROUP)
            dsk_ref[:, h:h + 1] = jnp.sum(dsk_acc[gi:gi + 1, hh * WINDOW:(hh + 1) * WINDOW], axis=1, keepdims=True)
        dk_ref[...] = dkp_ref[WINDOW:WINDOW + SEQ, :]
        dv_ref[...] = dvp_ref[WINDOW:WINDOW + SEQ, :]

    vmem = pl.BlockSpec(memory_space=pltpu.VMEM)
    padded = (SEQ + 2 * WINDOW, KV_WIDTH)
    return pl.pallas_call(
        body, name="attn_bwd",
        in_specs=[pl.BlockSpec(memory_space=pltpu.SMEM), vmem, vmem, vmem, vmem],
        out_specs=[vmem, vmem, vmem, vmem],
        out_shape=[jax.ShapeDtypeStruct((ATTN_WIDTH, SEQ), F32),
                   jax.ShapeDtypeStruct((SEQ, KV_WIDTH), F32), jax.ShapeDtypeStruct((SEQ, KV_WIDTH), F32),
                   jax.ShapeDtypeStruct((1, ATTN_HEADS), F32)],
        scratch_shapes=[pltpu.VMEM((KV_HEADS, GROUP * WINDOW), F32), pltpu.VMEM(padded, BF16),
                        pltpu.VMEM(padded, BF16), pltpu.VMEM(padded, F32), pltpu.VMEM(padded, F32)],
        compiler_params=_cparams(),
    )(sinks, qt, k, v, dot_)


HALF_LANES = LANES // 2
BLOCK_ROWS = 32


def _embed_block(bt, q):
    z = jnp.zeros((16, HALF_LANES), bt.dtype)
    blk = jnp.concatenate([jnp.concatenate([bt[:16], z], axis=1), jnp.concatenate([z, bt[16:]], axis=1)], axis=0)
    parts = [jnp.zeros((BLOCK_ROWS * q, LANES), bt.dtype)] if q else []
    parts.append(blk)
    if q < 3:
        parts.append(jnp.zeros((BLOCK_ROWS * (3 - q), LANES), bt.dtype))
    return jnp.concatenate(parts, axis=0)


def _extract_block(m, q):
    blk = m[BLOCK_ROWS * q:BLOCK_ROWS * (q + 1)]
    return jnp.concatenate([blk[:16, :HALF_LANES], blk[16:, HALF_LANES:]], axis=0)


def _ssm_prep(lam_re, lam_im, log_dt, bt_re, bt_im, c_re, c_im):
    nb = 2 * N_LANE_BLOCKS

    def body(lr_ref, li_ref, ldt_ref, btr_ref, bti_ref, ctr_ref, cti_ref, ar_ref, ai_ref, bb_ref, cc_ref):
        lr = jnp.minimum(lr_ref[...], LAMBDA_RE_MAX)
        li = li_ref[...]
        dt = jnp.exp(ldt_ref[...])
        mag = jnp.exp(lr * dt)
        ar = mag * jnp.cos(li * dt)
        ai = mag * jnp.sin(li * dt)
        den = lr * lr + li * li
        cr = ((ar - 1.0) * lr + ai * li) / den
        ci = (ai * lr - (ar - 1.0) * li) / den
        ar_ref[...] = ar
        ai_ref[...] = ai
        for i in range(nb):
            q = i % 4
            rows = slice(BLOCK_ROWS * i, BLOCK_ROWS * (i + 1))
            br = _embed_block(btr_ref[rows, :], q)
            bi = _embed_block(bti_ref[rows, :], q)
            cri, cii = cr[i:i + 1, :], ci[i:i + 1, :]
            bb_ref[i] = jnp.concatenate([cri * br - cii * bi, cri * bi + cii * br], axis=1).astype(BF16)
            cc_ref[i] = jnp.concatenate([_embed_block(ctr_ref[rows, :], q).T,
                                         -_embed_block(cti_ref[rows, :], q).T], axis=0).astype(BF16)

    return pl.pallas_call(
        body, name="ssm_prep",
        out_shape=[jax.ShapeDtypeStruct((nb, LANES), F32), jax.ShapeDtypeStruct((nb, LANES), F32),
                   jax.ShapeDtypeStruct((nb, LANES, 2 * LANES), BF16),
                   jax.ShapeDtypeStruct((nb, 2 * LANES, LANES), BF16)],
        compiler_params=_cparams(),
    )(lam_re, lam_im, log_dt, bt_re, bt_im, c_re, c_im)


def _ssm_prep_bwd(lam_re, lam_im, log_dt, bt_re, bt_im, dar, dai, dbb, dcc):
    nb = 2 * N_LANE_BLOCKS

    def body(lr_ref, li_ref, ldt_ref, btr_ref, bti_ref, dar_ref, dai_ref, dbb_ref, dcc_ref,
             glr_ref, gli_ref, gdt_ref, gbr_ref, gbi_ref, gcre_ref, gcim_ref, gcr_s, gci_s):
        lam = lr_ref[...]
        lr = jnp.minimum(lam, LAMBDA_RE_MAX)
        li = li_ref[...]
        dt = jnp.exp(ldt_ref[...])
        mag = jnp.exp(lr * dt)
        cs = jnp.cos(li * dt)
        sn = jnp.sin(li * dt)
        ar = mag * cs
        ai = mag * sn
        den = lr * lr + li * li
        nr = (ar - 1.0) * lr + ai * li
        ni = ai * lr - (ar - 1.0) * li
        cr = nr / den
        ci = ni / den
        for i in range(nb):
            q = i % 4
            rows = slice(BLOCK_ROWS * i, BLOCK_ROWS * (i + 1))
            br = _embed_block(btr_ref[rows, :], q)
            bi = _embed_block(bti_ref[rows, :], q)
            gbbr = dbb_ref[i, :, :LANES]
            gbbi = dbb_ref[i, :, LANES:]
            cri, cii = cr[i:i + 1, :], ci[i:i + 1, :]
            gcr_s[i:i + 1, :] = jnp.sum(gbbr * br + gbbi * bi, axis=0, keepdims=True)
            gci_s[i:i + 1, :] = jnp.sum(gbbi * br - gbbr * bi, axis=0, keepdims=True)
            gbr_ref[rows, :] = _extract_block(cri * gbbr + cii * gbbi, q)
            gbi_ref[rows, :] = _extract_block(cri * gbbi - cii * gbbr, q)
            gcre_ref[rows, :] = _extract_block(dcc_ref[i, :LANES, :].T, q)
            gcim_ref[rows, :] = -_extract_block(dcc_ref[i, LANES:, :].T, q)
        g_cr = gcr_s[...]
        g_ci = gci_s[...]
        g_nr = g_cr / den
        g_ni = g_ci / den
        g_den = -(g_cr * nr + g_ci * ni) / (den * den)
        g_ar = dar_ref[...] + g_nr * lr - g_ni * li
        g_ai = dai_ref[...] + g_nr * li + g_ni * lr
        g_lr = g_nr * (ar - 1.0) + g_ni * ai + g_den * 2.0 * lr
        g_li = g_nr * ai - g_ni * (ar - 1.0) + g_den * 2.0 * li
        g_mag = g_ar * cs + g_ai * sn
        g_th = (g_ai * cs - g_ar * sn) * mag
        g_lr = g_lr + g_mag * mag * dt
        g_li = g_li + g_th * dt
        g_dt = g_mag * mag * lr + g_th * li
        glr_ref[...] = jnp.where(lam < LAMBDA_RE_MAX, g_lr, 0.0)
        gli_ref[...] = g_li
        gl = g_dt * dt
        half = LANES // 2
        gdt_ref[:, 0:1] = jnp.sum(gl[:, :half], axis=1, keepdims=True)
        gdt_ref[:, 1:2] = jnp.sum(gl[:, half:], axis=1, keepdims=True)

    rows_shape = jax.ShapeDtypeStruct((nb * BLOCK_ROWS, HALF_LANES), F32)
    return pl.pallas_call(
        body, name="ssm_prep_bwd",
        out_shape=[jax.ShapeDtypeStruct((nb, LANES), F32), jax.ShapeDtypeStruct((nb, LANES), F32),
                   jax.ShapeDtypeStruct((nb, 2), F32), rows_shape, rows_shape, rows_shape, rows_shape],
        scratch_shapes=[pltpu.VMEM((nb, LANES), F32), pltpu.VMEM((nb, LANES), F32)],
        compiler_params=_cparams(),
    )(lam_re, lam_im, log_dt, bt_re, bt_im, dar, dai, dbb, dcc)


def _cmul(ar, ai, br, bi):
    return ar * br - ai * bi, ar * bi + ai * br


def _interleave_rows(src_ref, dst_ref):
    def step(j, carry):
        dst_ref[pl.ds(pl.multiple_of(j * 8, 8), 8), :] = src_ref[pl.ds(j, 8, stride=SCAN_CHUNK), :]
        return carry
    lax.fori_loop(0, SCAN_CHUNK, step, 0, unroll=4)


def _deinterleave_rows(src_ref, dst_ref):
    def step(j, carry):
        dst_ref[pl.ds(j, 8, stride=SCAN_CHUNK), :] = src_ref[pl.ds(pl.multiple_of(j * 8, 8), 8), :]
        return carry
    lax.fori_loop(0, SCAN_CHUNK, step, 0, unroll=4)


def _scan_inplace(re_ref, im_ref, a_re, a_im, reverse):
    nq = len(a_re)
    ch = SCAN_CHUNK
    ab_re = [jnp.broadcast_to(a, (8, LANES)) for a in a_re]
    ab_im = [jnp.broadcast_to(a, (8, LANES)) for a in a_im]

    def rows(j):
        jj = (ch - 1 - j) if reverse else j
        return pl.ds(pl.multiple_of(jj * 8, 8), 8)

    def sweep(init, store):
        def step(j, st):
            out = []
            r = rows(j)
            for qi in range(nq):
                xr, xi = st[2 * qi], st[2 * qi + 1]
                pr, pi = _cmul(ab_re[qi], ab_im[qi], xr, xi)
                xr = pr + re_ref[qi, r, :]
                xi = pi + im_ref[qi, r, :]
                if store:
                    re_ref[qi, r, :] = xr
                    im_ref[qi, r, :] = xi
                out += [xr, xi]
            return tuple(out)
        return lax.fori_loop(0, ch, step, tuple(init), unroll=2)

    zeros = [jnp.zeros((8, LANES), F32)] * (2 * nq)
    finals = sweep(zeros, store=False)

    row_id = lax.broadcasted_iota(jnp.int32, (8, LANES), 0)
    carries = []
    for qi in range(nq):
        pr, pi = ab_re[qi], ab_im[qi]
        for _ in range(8):
            pr, pi = _cmul(pr, pi, pr, pi)
        fr, fi = finals[2 * qi], finals[2 * qi + 1]
        sr = jnp.zeros((8, LANES), F32)
        si = jnp.zeros((8, LANES), F32)
        for _ in range(7):
            tr, ti = _cmul(pr, pi, sr, si)
            tr, ti = tr + fr, ti + fi
            if reverse:
                sr = jnp.where(row_id == 7, 0.0, pltpu.roll(tr, 7, axis=0))
                si = jnp.where(row_id == 7, 0.0, pltpu.roll(ti, 7, axis=0))
            else:
                sr = jnp.where(row_id == 0, 0.0, pltpu.roll(tr, 1, axis=0))
                si = jnp.where(row_id == 0, 0.0, pltpu.roll(ti, 1, axis=0))
        carries += [sr, si]
    sweep(carries, store=True)


SSM_Q = 4


def _ssm_fwd(u, are, aim, bb, cc, dskip, after=None):
    nq = SSM_Q
    deps = [] if after is None else [after]

    def body(u_ref, ar_ref, ai_ref, bb_ref, cc_ref, d_ref, *rest):
        y_ref, xr_ref, xi_ref, sre, sim, up, yp = rest[len(deps):]
        _interleave_rows(u_ref, up)
        uf = up[...]
        ub = uf.astype(BF16)
        yp[...] = d_ref[...] * uf
        for d in range(2):
            for qi in range(nq):
                sre[qi] = _dot(ub, bb_ref[d, qi, :, :LANES])
                sim[qi] = _dot(ub, bb_ref[d, qi, :, LANES:])
            _scan_inplace(sre, sim, [ar_ref[d, qi] for qi in range(nq)], [ai_ref[d, qi] for qi in range(nq)],
                          reverse=(d == 1))
            for qi in range(nq):
                xrb = sre[qi].astype(BF16)
                xib = sim[qi].astype(BF16)
                xr_ref[d, qi] = xrb
                xi_ref[d, qi] = xib
                yp[...] += _dot(xrb, cc_ref[d, qi, :LANES, :]) + _dot(xib, cc_ref[d, qi, LANES:, :])
        _deinterleave_rows(yp, y_ref)

    blk4 = lambda k: (0, k, 0, 0)
    return pl.pallas_call(
        body, name="ssm_fwd", grid=(SSM_WIDTH // LANES,),
        in_specs=[pl.BlockSpec((SEQ, LANES), lambda k: (0, k)),
                  pl.BlockSpec((2, nq, 1, LANES), blk4), pl.BlockSpec((2, nq, 1, LANES), blk4),
                  pl.BlockSpec((2, nq, LANES, 2 * LANES), blk4), pl.BlockSpec((2, nq, 2 * LANES, LANES), blk4),
                  pl.BlockSpec((1, LANES), lambda k: (0, k))] + [pl.BlockSpec(memory_space=pl.ANY)] * len(deps),
        out_specs=[pl.BlockSpec((SEQ, LANES), lambda k: (0, k)),
                   pl.BlockSpec((2, nq, SEQ, LANES), blk4), pl.BlockSpec((2, nq, SEQ, LANES), blk4)],
        out_shape=[jax.ShapeDtypeStruct((SEQ, SSM_WIDTH), F32),
                   jax.ShapeDtypeStruct((2, N_LANE_BLOCKS, SEQ, LANES), BF16),
                   jax.ShapeDtypeStruct((2, N_LANE_BLOCKS, SEQ, LANES), BF16)],
        scratch_shapes=[pltpu.VMEM((nq, SEQ, LANES), F32), pltpu.VMEM((nq, SEQ, LANES), F32),
                        pltpu.VMEM((SEQ, LANES), F32), pltpu.VMEM((SEQ, LANES), F32)],
        compiler_params=_cparams(("parallel",)),
    )(u, are, aim, bb, cc, dskip, *deps)


def _ssm_bwd(dy, u, xr, xi, are, aim, bb, cc, dskip, after=None):
    nq = SSM_Q
    body_rows = SEQ - 8
    deps = [] if after is None else [after]

    def body(dy_ref, u_ref, xr_ref, xi_ref, ar_ref, ai_ref, bb_ref, cc_ref, d_ref, *rest):
        du_ref, dd_ref, dcc_ref, dbb_ref, dar_ref, dai_ref, sre, sim, up, dyp, dup = rest[len(deps):]
        _interleave_rows(u_ref, up)
        _interleave_rows(dy_ref, dyp)
        dyf = dyp[...]
        uf = up[...]
        dyb = dyf.astype(BF16)
        ub = uf.astype(BF16)
        dd_ref[...] = jnp.sum(dyf * uf, axis=0, keepdims=True)
        dup[...] = d_ref[...] * dyf
        row8 = lax.broadcasted_iota(jnp.int32, (8, LANES), 0)
        for d in range(2):
            for qi in range(nq):
                dx = _dot_nt(dyb, cc_ref[d, qi])
                sre[qi] = dx[:, :LANES]
                sim[qi] = dx[:, LANES:]
                dcc_ref[d, qi] = _dot_tn(jnp.concatenate([xr_ref[d, qi], xi_ref[d, qi]], axis=1), dyb)
            _scan_inplace(sre, sim, [ar_ref[d, qi] for qi in range(nq)], [-ai_ref[d, qi] for qi in range(nq)],
                          reverse=(d == 0))
            for qi in range(nq):
                gr = sre[qi]
                gi = sim[qi]
                xrf = xr_ref[d, qi].astype(F32)
                xif = xi_ref[d, qi].astype(F32)
                if d == 0:
                    g_main_r, g_main_i = gr[8:], gi[8:]
                    x_main_r, x_main_i = xrf[:body_rows], xif[:body_rows]
                    g_edge_r, g_edge_i = gr[:8], gi[:8]
                    x_edge_r = jnp.where(row8 == 0, 0.0, pltpu.roll(xrf[body_rows:], 1, axis=0))
                    x_edge_i = jnp.where(row8 == 0, 0.0, pltpu.roll(xif[body_rows:], 1, axis=0))
                else:
                    g_main_r, g_main_i = gr[:body_rows], gi[:body_rows]
                    x_main_r, x_main_i = xrf[8:], xif[8:]
                    g_edge_r, g_edge_i = gr[body_rows:], gi[body_rows:]
                    x_edge_r = jnp.where(row8 == 7, 0.0, pltpu.roll(xrf[:8], 7, axis=0))
                    x_edge_i = jnp.where(row8 == 7, 0.0, pltpu.roll(xif[:8], 7, axis=0))
                dar_ref[d, qi] = (jnp.sum(g_main_r * x_main_r + g_main_i * x_main_i, axis=0, keepdims=True)
                                  + jnp.sum(g_edge_r * x_edge_r + g_edge_i * x_edge_i, axis=0, keepdims=True))
                dai_ref[d, qi] = (jnp.sum(g_main_i * x_main_r - g_main_r * x_main_i, axis=0, keepdims=True)
                                  + jnp.sum(g_edge_i * x_edge_r - g_edge_r * x_edge_i, axis=0, keepdims=True))
                gb = jnp.concatenate([gr, gi], axis=1).astype(BF16)
                dup[...] += _dot_nt(gb, bb_ref[d, qi])
                dbb_ref[d, qi] = _dot_tn(ub, gb)
        _deinterleave_rows(dup, du_ref)

    blk4 = lambda k: (0, k, 0, 0)
    col = lambda k: (0, k)
    bb_spec = pl.BlockSpec((2, nq, LANES, 2 * LANES), blk4)
    cc_spec = pl.BlockSpec((2, nq, 2 * LANES, LANES), blk4)
    a_spec = pl.BlockSpec((2, nq, 1, LANES), blk4)
    x_spec = pl.BlockSpec((2, nq, SEQ, LANES), blk4)
    a_shape = jax.ShapeDtypeStruct((2, N_LANE_BLOCKS, 1, LANES), F32)
    return pl.pallas_call(
        body, name="ssm_bwd", grid=(SSM_WIDTH // LANES,),
        in_specs=[pl.BlockSpec((SEQ, LANES), col), pl.BlockSpec((SEQ, LANES), col), x_spec, x_spec,
                  a_spec, a_spec, bb_spec, cc_spec, pl.BlockSpec((1, LANES), col)]
        + [pl.BlockSpec(memory_space=pl.ANY)] * len(deps),
        out_specs=[pl.BlockSpec((SEQ, LANES), col), pl.BlockSpec((1, LANES), col),
                   cc_spec, bb_spec, a_spec, a_spec],
        out_shape=[jax.ShapeDtypeStruct((SEQ, SSM_WIDTH), F32), jax.ShapeDtypeStruct((1, SSM_WIDTH), F32),
                   jax.ShapeDtypeStruct((2, N_LANE_BLOCKS, 2 * LANES, LANES), F32),
                   jax.ShapeDtypeStruct((2, N_LANE_BLOCKS, LANES, 2 * LANES), F32), a_shape, a_shape],
        scratch_shapes=[pltpu.VMEM((nq, SEQ, LANES), F32), pltpu.VMEM((nq, SEQ, LANES), F32),
                        pltpu.VMEM((SEQ, LANES), F32), pltpu.VMEM((SEQ, LANES), F32), pltpu.VMEM((SEQ, LANES), F32)],
        compiler_params=_cparams(("parallel",)),
    )(dy, u, xr, xi, are, aim, bb, cc, dskip, *deps)


GELU_C = 0.7978845608028654
GELU_K = 0.044715


def _gelu(y):
    return 0.5 * y * (1.0 + jnp.tanh(GELU_C * (y + GELU_K * y * y * y)))


def _gelu_grad(y):
    t = jnp.tanh(GELU_C * (y + GELU_K * y * y * y))
    return 0.5 * (1.0 + t) + 0.5 * y * (1.0 - t * t) * GELU_C * (1.0 + 3.0 * GELU_K * y * y)


def _mixout_fwd(o, y, glu_w, glu_b, gan, gsn, wout, x1):
    tm = MIX_TM

    def body(o_ref, y_ref, gw_ref, gb_ref, gan_ref, gsn_ref, w_ref, x1_ref, x2_ref, mx_ref):
        yg = _gelu(y_ref[...])
        z = _dot(yg.astype(BF16), gw_ref[...]) + gb_ref[...]
        so = yg * _sigmoid(z)
        na = _rms_fwd(o_ref[...], gan_ref[...])
        ns = _rms_fwd(so, gsn_ref[...])
        mixed = jnp.concatenate([na, ns], axis=-1).astype(BF16)
        mx_ref[...] = mixed
        x2_ref[...] = x1_ref[...] + _dot(mixed, w_ref[...])

    row = lambda i: (i, 0)
    const = lambda i: (0, 0)
    return pl.pallas_call(
        body, name="mixout_fwd", grid=(SEQ // tm,),
        in_specs=[pl.BlockSpec((tm, ATTN_WIDTH), row), pl.BlockSpec((tm, SSM_WIDTH), row),
                  pl.BlockSpec((SSM_WIDTH, SSM_WIDTH), const), pl.BlockSpec((1, SSM_WIDTH), const),
                  pl.BlockSpec((1, ATTN_WIDTH), const), pl.BlockSpec((1, SSM_WIDTH), const),
                  pl.BlockSpec((D_MODEL, D_MODEL), const), pl.BlockSpec((tm, D_MODEL), row)],
        out_specs=[pl.BlockSpec((tm, D_MODEL), row), pl.BlockSpec((tm, D_MODEL), row)],
        out_shape=[jax.ShapeDtypeStruct((SEQ, D_MODEL), F32), jax.ShapeDtypeStruct((SEQ, D_MODEL), BF16)],
        compiler_params=_cparams(("parallel",)),
    )(o, y, glu_w, glu_b, gan, gsn, wout, x1)


def _mixout_bwd(dx2, o, y, glu_w, glu_b, gan, gsn, wout):
    tm = MIX_TM

    def body(dx2_ref, o_ref, y_ref, gw_ref, gb_ref, gan_ref, gsn_ref, w_ref,
             do_ref, dy_ref, dz_ref, yg_ref, dxb_ref, dgan_ref, dgsn_ref, dgb_ref):
        i = pl.program_id(0)
        dxb = dx2_ref[...].astype(BF16)
        dxb_ref[...] = dxb
        dmixed = _dot_nt(dxb, w_ref[...])
        do, dgan = _rms_bwd(dmixed[:, :ATTN_WIDTH], o_ref[...], gan_ref[...])
        do_ref[...] = do.T
        yv = y_ref[...]
        yg = _gelu(yv)
        ygb = yg.astype(BF16)
        yg_ref[...] = ygb
        sg = _sigmoid(_dot(ygb, gw_ref[...]) + gb_ref[...])
        dso, dgsn = _rms_bwd(dmixed[:, ATTN_WIDTH:], yg * sg, gsn_ref[...])
        dz = dso * yg * sg * (1.0 - sg)
        dzb = dz.astype(BF16)
        dz_ref[...] = dzb
        dyg = dso * sg + _dot_nt(dzb, gw_ref[...])
        dy_ref[...] = dyg * _gelu_grad(yv)
        dgb = jnp.sum(dz, axis=0, keepdims=True)

        @pl.when(i == 0)
        def _():
            dgan_ref[...] = dgan
            dgsn_ref[...] = dgsn
            dgb_ref[...] = dgb

        @pl.when(i != 0)
        def _():
            dgan_ref[...] += dgan
            dgsn_ref[...] += dgsn
            dgb_ref[...] += dgb

    row = lambda i: (i, 0)
    const = lambda i: (0, 0)
    return pl.pallas_call(
        body, name="mixout_bwd", grid=(SEQ // tm,),
        in_specs=[pl.BlockSpec((tm, D_MODEL), row), pl.BlockSpec((tm, ATTN_WIDTH), row),
                  pl.BlockSpec((tm, SSM_WIDTH), row),
                  pl.BlockSpec((SSM_WIDTH, SSM_WIDTH), const), pl.BlockSpec((1, SSM_WIDTH), const),
                  pl.BlockSpec((1, ATTN_WIDTH), const), pl.BlockSpec((1, SSM_WIDTH), const),
                  pl.BlockSpec((D_MODEL, D_MODEL), const)],
        out_specs=[pl.BlockSpec((ATTN_WIDTH, tm), lambda i: (0, i)), pl.BlockSpec((tm, SSM_WIDTH), row),
                   pl.BlockSpec((tm, SSM_WIDTH), row), pl.BlockSpec((tm, SSM_WIDTH), row),
                   pl.BlockSpec((tm, D_MODEL), row),
                   pl.BlockSpec((1, ATTN_WIDTH), const), pl.BlockSpec((1, SSM_WIDTH), const),
                   pl.BlockSpec((1, SSM_WIDTH), const)],
        out_shape=[jax.ShapeDtypeStruct((ATTN_WIDTH, SEQ), F32), jax.ShapeDtypeStruct((SEQ, SSM_WIDTH), F32),
                   jax.ShapeDtypeStruct((SEQ, SSM_WIDTH), BF16), jax.ShapeDtypeStruct((SEQ, SSM_WIDTH), BF16),
                   jax.ShapeDtypeStruct((SEQ, D_MODEL), BF16),
                   jax.ShapeDtypeStruct((1, ATTN_WIDTH), F32), jax.ShapeDtypeStruct((1, SSM_WIDTH), F32),
                   jax.ShapeDtypeStruct((1, SSM_WIDTH), F32)],
        compiler_params=_cparams(("arbitrary",)),
    )(dx2, o, y, glu_w, glu_b, gan, gsn, wout)


def _local_step(x, target, w, p, late_weights, early_grads, after=None, midway=None):
    x1, h1, a1, b1 = _ffn_fwd(x, p["norm_ffn1"], w["wgt1"], w["wut1"], w["wd1"], "ffn1_fwd", after=after)
    h2, q, k, v, u = _mixin_fwd(x1, p["norm_mix"], w["wint"])

    lam_re = p["ssm_lambda_re"].reshape(2 * N_LANE_BLOCKS, LANES)
    lam_im = p["ssm_lambda_im"].reshape(2 * N_LANE_BLOCKS, LANES)
    log_dt = jnp.repeat(p["ssm_log_dt"].reshape(2, 32), 64, axis=-1).reshape(2 * N_LANE_BLOCKS, LANES)
    a_re, a_im, bb, cc = _ssm_prep(lam_re, lam_im, log_dt, p["ssm_b_re"], p["ssm_b_im"],
                                   p["ssm_c_re"], p["ssm_c_im"])
    shape_a = (2, N_LANE_BLOCKS, 1, LANES)
    a_re4, a_im4 = a_re.reshape(shape_a), a_im.reshape(shape_a)
    bb4 = bb.reshape(2, N_LANE_BLOCKS, LANES, 2 * LANES)
    cc4 = cc.reshape(2, N_LANE_BLOCKS, 2 * LANES, LANES)
    dskip = p["ssm_d"].T.reshape(1, SSM_WIDTH)
    y, xr, xi = _ssm_fwd(u, a_re4, a_im4, bb4, cc4, dskip)
    o = _attn_fwd(q, k, v, p["attn_sinks"], after=None if midway is None else midway(y))

    w2 = late_weights(o)
    x2, mixed = _mixout_fwd(o, y, w2["glu"], p["ssm_glu_b"], p["attn_out_norm"], p["ssm_out_norm"], w2["wout"], x1)
    dx3, h3, a3, b3, loss, d_final = _ffn_fwd(x2, p["norm_ffn2"], w2["wgt2"], w2["wut2"], w2["wd2"], "ffn2_fwd",
                                              head=(p["final_norm"], target))
    dx2, da3, db3, s3, df3, d_n2 = _ffn_bwd_act(dx3, x2, p["norm_ffn2"], a3, b3, w2["wgt2"], w2["wut2"], w2["wd2"],
                                                "ffn2_bwd_act")
    g_wgt2, g_wut2, g_wd2 = _mm_tn([(da3, h3), (db3, h3), (s3, df3)], "ffn2_bwd_w")

    do, dy, dz, ygb, dx2b, d_gan, d_gsn, d_glub = _mixout_bwd(
        dx2, o, y, w2["glu"], p["ssm_glu_b"], p["attn_out_norm"], p["ssm_out_norm"], w2["wout"])
    (g_wout,) = _mm_tn([(mixed, dx2b)], "wout_bwd_w")
    (g_glu,) = _mm_tn([(ygb, dz)], "glu_bwd_w")
    sent = early_grads(dict(glu=g_glu, wout=g_wout, wgt2=g_wgt2, wut2=g_wut2, wd2=g_wd2))

    du, d_dskip, dcc, dbb, dar, dai = _ssm_bwd(dy, u, xr, xi, a_re4, a_im4, bb4, cc4, dskip, after=sent)
    nb = 2 * N_LANE_BLOCKS
    g_lre, g_lim, g_ldt, g_btr, g_bti, g_cre, g_cim = _ssm_prep_bwd(
        lam_re, lam_im, log_dt, p["ssm_b_re"], p["ssm_b_im"], dar.reshape(nb, LANES), dai.reshape(nb, LANES),
        dbb.reshape(nb, LANES, 2 * LANES), dcc.reshape(nb, 2 * LANES, LANES))

    dq, dk, dv, d_sinks = _attn_bwd(q, k, v, p["attn_sinks"], do)
    dx1, dproj, d_nmix = _mixin_bwd(dq, dk, dv, du, w["wint"], x1, p["norm_mix"], dx2)
    (g_wint,) = _mm_tn([(dproj, h2)], "win_bwd_w")

    dx0, da1, db1, s1, df1, d_n1 = _ffn_bwd_act(dx1, x, p["norm_ffn1"], a1, b1, w["wgt1"], w["wut1"], w["wd1"],
                                                "ffn1_bwd_act")
    g_wgt1, g_wut1, g_wd1 = _mm_tn([(da1, h1), (db1, h1), (s1, df1)], "ffn1_bwd_w")

    big = dict(wgt1=g_wgt1, wut1=g_wut1, wd1=g_wd1, wint=g_wint)
    small = dict(
        norm_ffn1=d_n1, norm_mix=d_nmix, attn_sinks=d_sinks,
        ssm_lambda_re=g_lre.reshape(64, 64), ssm_lambda_im=g_lim.reshape(64, 64),
        ssm_log_dt=g_ldt.reshape(2, 32), ssm_b_re=g_btr, ssm_b_im=g_bti, ssm_c_re=g_cre, ssm_c_im=g_cim,
        ssm_d=d_dskip.reshape(32, 16).T, ssm_glu_b=d_glub, attn_out_norm=d_gan, ssm_out_norm=d_gsn,
        norm_ffn2=d_n2, final_norm=d_final, loss=loss)
    return loss, dx0, big, small


BIG = dict(
    wgt1=("ffn1_w_gate", 352, 1024, True), wut1=("ffn1_w_up", 352, 1024, True), wd1=("ffn1_w_down", 352, 1024, False),
    wint=("w_in", 160, 1024, True), glu=("ssm_glu_w", 64, 512, False), wout=("w_out", 128, 1024, False),
    wgt2=("ffn2_w_gate", 352, 1024, True), wut2=("ffn2_w_up", 352, 1024, True), wd2=("ffn2_w_down", 352, 1024, False))

SMALL = dict(
    norm_ffn1=(1, 1024), norm_mix=(1, 1024), attn_sinks=(1, 8), ssm_lambda_re=(64, 64), ssm_lambda_im=(64, 64),
    ssm_log_dt=(2, 32), ssm_b_re=(1024, 64), ssm_b_im=(1024, 64), ssm_c_re=(1024, 64), ssm_c_im=(1024, 64),
    ssm_d=(16, 32), ssm_glu_b=(1, 512), attn_out_norm=(1, 512), ssm_out_norm=(1, 512), norm_ffn2=(1, 1024),
    final_norm=(1, 1024), loss=(1, 128))
SMALL_TRANSPOSED = ("ssm_b_re", "ssm_b_im", "ssm_d")
SMALL_PARAMS = tuple(n for n in SMALL if n != "loss")

SMALL_PAIRS = (("ssm_lambda_re", "ssm_lambda_im"), ("ssm_c_re", "ssm_c_im"), ("ssm_b_re", "ssm_b_im"))
SMALL_VECS = ("norm_ffn1", "norm_mix", "norm_ffn2", "final_norm", "ssm_glu_b", "attn_out_norm", "ssm_out_norm")
SMALL_TILES = ("ssm_log_dt", "attn_sinks", "ssm_d", "loss")


def _small_offsets():
    off, table = 0, {}
    for re, im in SMALL_PAIRS:
        table[re] = table[im] = off
        off += SMALL[re][0]
    for n in SMALL_VECS:
        table[n] = off
        off += SMALL[n][1] // LANES
    for n in SMALL_TILES:
        off = -(-off // 8) * 8
        table[n] = off
        off += SMALL[n][0]
    return table, off


SMALL_OFFSET, SMALL_USED_ROWS = _small_offsets()
SMALL_ROWS = -(-SMALL_USED_ROWS // (8 * N_DEV)) * 8 * N_DEV


def _cast_shards(shards):
    names = list(BIG)

    def body(*refs):
        ins, outs = refs[:len(names)], refs[len(names):]
        for idx in range(len(names)):
            outs[idx][...] = ins[idx][...].astype(BF16)

    return pl.pallas_call(
        body, name="cast_shards",
        out_shape=[jax.ShapeDtypeStruct((BIG[n][1], BIG[n][2]), BF16) for n in names],
        compiler_params=_cparams(),
    )(*[shards[n] for n in names])


def _peer(x, y, c, r):
    px = 1 - x if r & 4 else x
    py = 1 - y if r & 2 else y
    pc = 1 - c if r & 1 else c
    return px, py, pc


FIRST_GROUP = ("wgt1", "wut1", "wd1", "wint")
LATE_GROUP = ("glu", "wout", "wgt2", "wut2", "wd2")
N_PEERS = N_DEV - 1
ANY_SPEC = pl.BlockSpec(memory_space=pl.ANY)
HBM_SPEC = pl.BlockSpec(memory_space=pltpu.HBM)
SEM_SPEC = pl.BlockSpec(memory_space=pltpu.SEMAPHORE)
DATAFLOW_EFFECT = pltpu.SideEffectType.DATAFLOW_SIDE_EFFECTING


def _mesh_pos():
    x, y, c = lax.axis_index("x"), lax.axis_index("y"), lax.axis_index("c")
    return x, y, c, 4 * x + 2 * y + c


def _gather_first(first, late):
    nf, nl = len(first), len(late)

    def body(*refs):
        f_in, l_in = refs[:nf], refs[nf:nf + nl]
        f_out, l_out = refs[nf + nl:2 * nf + nl], refs[2 * nf + nl:2 * (nf + nl)]
        send_sems, recv_sems, local_sems = refs[2 * (nf + nl):]
        x, y, c, me = _mesh_pos()
        sibling = (x, y, 1 - c)
        chips = [(x, 1 - y), (1 - x, y), (1 - x, 1 - y)]

        def idx(px, py, pc):
            return 4 * px + 2 * py + pc

        def copy(k, s, block, to, src=None):
            slot = f_out[k].at[block]
            return pltpu.make_async_remote_copy(
                src_ref=slot if src is None else src, dst_ref=slot, send_sem=send_sems.at[k, s],
                recv_sem=recv_sems.at[k, s], device_id=to, device_id_type=MESH_ID)

        local = []
        for k in range(nf + nl):
            src, dst = (f_in[k], f_out[k]) if k < nf else (l_in[k - nf], l_out[k - nf])
            mine = pltpu.make_async_copy(src, dst.at[me], local_sems.at[k])
            mine.start()
            local.append(mine)
        sends = []
        for j, chip in enumerate(chips):
            for k in range(nf):
                sends.append(copy(k, 1 + j, me, (*chip, c), src=f_in[k]))
                sends[-1].start()
        for k in range(nf):
            sends.append(copy(k, 0, me, sibling, src=f_in[k]))
            sends[-1].start()
        for j, chip in enumerate(chips):
            for k in range(nf):
                copy(k, 1 + j, idx(*chip, c), (*chip, c)).wait_recv()
                sends.append(copy(k, 4 + j, idx(*chip, c), sibling))
                sends[-1].start()
        for k in range(nf):
            copy(k, 0, idx(*sibling), sibling).wait_recv()
        for j, chip in enumerate(chips):
            for k in range(nf):
                copy(k, 4 + j, idx(*chip, 1 - c), sibling).wait_recv()
        for cp in sends:
            cp.wait_send()
        for cp in local:
            cp.wait()

    return pl.pallas_call(
        body, name="gather_first",
        in_specs=[ANY_SPEC] * (nf + nl), out_specs=[ANY_SPEC] * (nf + nl),
        out_shape=[jax.ShapeDtypeStruct((N_DEV,) + s.shape, s.dtype) for s in list(first) + list(late)],
        scratch_shapes=[pltpu.SemaphoreType.DMA((nf, N_PEERS)), pltpu.SemaphoreType.DMA((nf, N_PEERS)),
                        pltpu.SemaphoreType.DMA((nf + nl,))],
        compiler_params=pltpu.CompilerParams(has_side_effects=True),
    )(*first, *late)


def _split_copy(src_refs, land_refs, send_sems, recv_sems, k, r, pos, scatter, receiving):
    x, y, c, me = pos
    px, py, pc = _peer(x, y, c, r)
    peer_idx = 4 * px + 2 * py + pc
    if scatter:
        src, dst = src_refs[k].at[peer_idx], land_refs[k].at[r - 1]
    else:
        src, dst = src_refs[k], land_refs[k].at[peer_idx if receiving else me]
    return pltpu.make_async_remote_copy(
        src_ref=src, dst_ref=dst, send_sem=send_sems.at[k * N_PEERS + r - 1],
        recv_sem=recv_sems.at[k * N_PEERS + r - 1], device_id=(px, py, pc), device_id_type=MESH_ID)


def _split_start(name, srcs, lands, scatter):
    n = len(srcs)

    def body(*refs):
        src_refs, land_refs = refs[:n], refs[n:2 * n]
        send_sems, recv_sems = refs[2 * n], refs[2 * n + 1]
        token = refs[-1]
        pos = _mesh_pos()
        for k in range(n):
            for r in range(1, N_DEV):
                _split_copy(src_refs, land_refs, send_sems, recv_sems, k, r, pos, scatter, False).start()
        token[...] = jnp.zeros_like(token)

    thru = [pltpu.HBM(a.shape, a.dtype) for a in list(srcs) + list(lands)]
    outs = pl.pallas_call(
        body, name=name,
        in_specs=[HBM_SPEC] * (2 * n),
        out_specs=[SEM_SPEC, SEM_SPEC] + [HBM_SPEC] * (2 * n) + [pl.BlockSpec(memory_space=pltpu.VMEM)],
        out_shape=[pltpu.SemaphoreType.DMA((n * N_PEERS,)), pltpu.SemaphoreType.DMA((n * N_PEERS,))] + thru
        + [jax.ShapeDtypeStruct((8, LANES), F32)],
        input_output_aliases={i: 2 + i for i in range(2 * n)},
        compiler_params=pltpu.CompilerParams(has_side_effects=DATAFLOW_EFFECT),
    )(*[pltpu.with_memory_space_constraint(a, pltpu.HBM) for a in list(srcs) + list(lands)])
    return outs[0], outs[1], outs[2:2 + n], outs[2 + n:2 + 2 * n], outs[-1]


def _split_wait(name, send_sems, recv_sems, srcs, lands, scatter, after):
    n = len(srcs)

    def body(*refs):
        src_refs, land_refs = refs[:n], refs[n:2 * n]
        send, recv = refs[2 * n], refs[2 * n + 1]
        pos = _mesh_pos()
        for k in range(n):
            for r in range(1, N_DEV):
                cp = _split_copy(src_refs, land_refs, send, recv, k, r, pos, scatter, True)
                cp.wait_send()
                cp.wait_recv()

    thru = [pltpu.HBM(a.shape, a.dtype) for a in list(srcs) + list(lands)]
    outs = pl.pallas_call(
        body, name=name,
        in_specs=[HBM_SPEC] * (2 * n) + [SEM_SPEC, SEM_SPEC, ANY_SPEC],
        out_specs=[HBM_SPEC] * (2 * n), out_shape=thru,
        input_output_aliases={i: i for i in range(2 * n)},
        compiler_params=pltpu.CompilerParams(has_side_effects=DATAFLOW_EFFECT),
    )(*srcs, *lands, send_sems, recv_sems, after)
    return outs[:n], outs[n:]


def _late_copy(passing, src_refs, land_refs, send_sems, recv_sems, k, s, pos, receiving):
    x, y, c, me = pos
    chips = [(x, 1 - y), (1 - x, y), (1 - x, 1 - y)]
    sibling = (x, y, 1 - c)

    def idx(dev):
        return 4 * dev[0] + 2 * dev[1] + dev[2]

    if passing:
        to = sibling
        block = idx((*chips[s], 1 - c)) if receiving else idx((*chips[s], c))
        src = dst = land_refs[k].at[block]
        sem = k * 3 + s
    else:
        to = sibling if s == 0 else (*chips[s - 1], c)
        src, dst = src_refs[k], land_refs[k].at[idx(to) if receiving else me]
        sem = k * 4 + s
    return pltpu.make_async_remote_copy(src_ref=src, dst_ref=dst, send_sem=send_sems.at[sem],
                                        recv_sem=recv_sems.at[sem], device_id=to, device_id_type=MESH_ID)


def _late_gather_call(name, stage, srcs, lands, sems, after=None):
    n = len(srcs)
    n_sem_in = len(sems)
    has_after = after is not None

    def body(*refs):
        src_refs, land_refs = refs[:n], refs[n:2 * n]
        sem_in = refs[2 * n:2 * n + n_sem_in]
        outs = refs[2 * n + n_sem_in + (1 if has_after else 0):]
        pos = _mesh_pos()
        if stage == 0:
            own_send, own_recv = outs[0], outs[1]
            for s in (1, 2, 3, 0):
                for k in range(n):
                    _late_copy(False, src_refs, land_refs, own_send, own_recv, k, s, pos, False).start()
            outs[-1][...] = jnp.zeros_like(outs[-1])
        elif stage == 1:
            own_recv = sem_in[1]
            pass_send, pass_recv = outs[0], outs[1]
            for s in range(3):
                for k in range(n):
                    _late_copy(False, src_refs, land_refs, sem_in[0], own_recv, k, s + 1, pos, True).wait_recv()
                    _late_copy(True, src_refs, land_refs, pass_send, pass_recv, k, s, pos, False).start()
            outs[-1][...] = jnp.zeros_like(outs[-1])
        else:
            own_send, own_recv, pass_send, pass_recv = sem_in
            for k in range(n):
                _late_copy(False, src_refs, land_refs, own_send, own_recv, k, 0, pos, True).wait_recv()
                for s in range(4):
                    _late_copy(False, src_refs, land_refs, own_send, own_recv, k, s, pos, False).wait_send()
                for s in range(3):
                    cp = _late_copy(True, src_refs, land_refs, pass_send, pass_recv, k, s, pos, True)
                    cp.wait_recv()
                    cp.wait_send()

    thru = [pltpu.HBM(a.shape, a.dtype) for a in list(srcs) + list(lands)]
    new_sems = [[pltpu.SemaphoreType.DMA((n * 4,))] * 2, [pltpu.SemaphoreType.DMA((n * 3,))] * 2, []][stage]
    extra = [] if stage == 2 else [jax.ShapeDtypeStruct((8, LANES), F32)]
    outs = pl.pallas_call(
        body, name=name,
        in_specs=[HBM_SPEC] * (2 * n) + [SEM_SPEC] * n_sem_in + [ANY_SPEC] * has_after,
        out_specs=[SEM_SPEC] * len(new_sems) + [HBM_SPEC] * (2 * n) + [pl.BlockSpec(memory_space=pltpu.VMEM)] * len(extra),
        out_shape=new_sems + thru + extra,
        input_output_aliases={i: len(new_sems) + i for i in range(2 * n)},
        compiler_params=pltpu.CompilerParams(has_side_effects=DATAFLOW_EFFECT),
    )(*[pltpu.with_memory_space_constraint(a, pltpu.HBM) for a in list(srcs) + list(lands)], *sems,
      *([after] if has_after else []))
    ns = len(new_sems)
    return list(outs[:ns]), outs[ns:ns + n], outs[ns + n:ns + 2 * n], (outs[-1] if extra else None)


N_SEND_SLOTS = 3


def _exchange_last(grads, small_packed):
    ng = len(grads)
    ch = SMALL_ROWS // N_DEV
    max_rows = max(g.shape[1] for g in grads)
    cols = grads[0].shape[2]

    def body(*refs):
        g_in, s_in = refs[:ng], refs[ng]
        outs = refs[ng + 1:]
        own_out, land, stage = outs[:ng], outs[ng:2 * ng], outs[2 * ng:3 * ng]
        s_red, s_stage = outs[3 * ng], outs[3 * ng + 1]
        (va, vb, vo, vs, sm_in, sm_out, d2d_send, d2d_recv, ici_send, ici_recv, s1_send, s1_recv, s2_send, s2_recv,
         local_sems) = outs[3 * ng + 2:]
        x, y, c, me = _mesh_pos()
        sibling = (x, y, 1 - c)
        chips = [(x, y), (x, 1 - y), (1 - x, y), (1 - x, 1 - y)]

        def idx(chip, core):
            return 4 * chip[0] + 2 * chip[1] + core

        def d2d(k, j):
            return pltpu.make_async_remote_copy(
                src_ref=g_in[k].at[idx(chips[j], 1 - c)], dst_ref=stage[k].at[j], send_sem=d2d_send.at[k, j],
                recv_sem=d2d_recv.at[k, j], device_id=sibling, device_id_type=MESH_ID)

        def ici(k, j, slot):
            rows = g_in[k].shape[1]
            return pltpu.make_async_remote_copy(
                src_ref=vo.at[slot, pl.ds(0, rows)], dst_ref=land[k].at[j - 1], send_sem=ici_send.at[k, j - 1],
                recv_sem=ici_recv.at[k, j - 1], device_id=(*chips[j], c), device_id_type=MESH_ID)

        def small_scatter(r):
            px, py, pc = _peer(x, y, c, r)
            return pltpu.make_async_remote_copy(
                src_ref=s_in.at[pl.ds(pl.multiple_of((4 * px + 2 * py + pc) * ch, 8), ch)], dst_ref=s_stage.at[me],
                send_sem=s1_send.at[r - 1], recv_sem=s1_recv.at[r - 1], device_id=(px, py, pc), device_id_type=MESH_ID)

        def small_gather(r):
            return pltpu.make_async_remote_copy(
                src_ref=sm_out, dst_ref=s_red.at[me], send_sem=s2_send.at[r - 1], recv_sem=s2_recv.at[r - 1],
                device_id=_peer(x, y, c, r), device_id_type=MESH_ID)

        for r in range(1, N_DEV):
            small_scatter(r).start()
        mine = pltpu.make_async_copy(s_in.at[pl.ds(pl.multiple_of(me * ch, 8), ch)], s_stage.at[me], local_sems.at[0])
        mine.start()
        pairs = [(k, j) for k in range(ng) for j in (1, 2, 3)] + [(k, 0) for k in range(ng)]
        for k, j in pairs:
            d2d(k, j).start()

        def reduce_small():
            for r in range(1, N_DEV):
                small_scatter(r).wait_recv()
            mine.wait()
            load = pltpu.make_async_copy(s_stage, sm_in, local_sems.at[1])
            load.start()
            load.wait()
            total = sm_in[0]
            for i in range(1, N_DEV):
                total = total + sm_in[i]
            sm_out[...] = total
            for r in range(1, N_DEV):
                small_gather(r).start()
            keep = pltpu.make_async_copy(sm_out, s_red.at[me], local_sems.at[2])
            keep.start()
            return keep

        in_flight = {}
        for i, (k, j) in enumerate(pairs):
            if i == N_SEND_SLOTS:
                keep = reduce_small()
            slot = i % N_SEND_SLOTS
            rows = g_in[k].shape[1]
            if slot in in_flight:
                in_flight.pop(slot).wait_send()
            d2d(k, j).wait_recv()
            la = pltpu.make_async_copy(g_in[k].at[idx(chips[j], c)], va.at[pl.ds(0, rows)], local_sems.at[3])
            lb = pltpu.make_async_copy(stage[k].at[j], vb.at[pl.ds(0, rows)], local_sems.at[4])
            la.start()
            lb.start()
            la.wait()
            lb.wait()
            total = va[pl.ds(0, rows)].astype(F32) + vb[pl.ds(0, rows)].astype(F32)
            if j == 0:
                vs[pl.ds(0, rows)] = total
                st = pltpu.make_async_copy(vs.at[pl.ds(0, rows)], own_out[k], local_sems.at[5])
                st.start()
                st.wait()
            else:
                vo[slot, pl.ds(0, rows)] = total.astype(BF16)
                cp = ici(k, j, slot)
                cp.start()
                in_flight[slot] = cp
        for cp in in_flight.values():
            cp.wait_send()

        for j in (1, 2, 3, 0):
            for k in range(ng):
                d2d(k, j).wait_send()
        for j in (1, 2, 3):
            for k in range(ng):
                ici(k, j, 0).wait_recv()
        for r in range(1, N_DEV):
            small_scatter(r).wait_send()
            small_gather(r).wait_send()
            small_gather(r).wait_recv()
        keep.wait()

    out_shape = [jax.ShapeDtypeStruct(g.shape[1:], F32) for g in grads]
    out_shape += [jax.ShapeDtypeStruct((3,) + g.shape[1:], BF16) for g in grads]
    out_shape += [jax.ShapeDtypeStruct((4,) + g.shape[1:], BF16) for g in grads]
    out_shape += [jax.ShapeDtypeStruct((N_DEV, ch, LANES), F32), jax.ShapeDtypeStruct((N_DEV, ch, LANES), F32)]
    outs = pl.pallas_call(
        body, name="exchange_last",
        in_specs=[ANY_SPEC] * (ng + 1), out_specs=[ANY_SPEC] * len(out_shape), out_shape=out_shape,
        scratch_shapes=[pltpu.VMEM((max_rows, cols), BF16), pltpu.VMEM((max_rows, cols), BF16),
                        pltpu.VMEM((N_SEND_SLOTS, max_rows, cols), BF16), pltpu.VMEM((max_rows, cols), F32),
                        pltpu.VMEM((N_DEV, ch, LANES), F32), pltpu.VMEM((ch, LANES), F32),
                        pltpu.SemaphoreType.DMA((ng, 4)), pltpu.SemaphoreType.DMA((ng, 4)),
                        pltpu.SemaphoreType.DMA((ng, 3)), pltpu.SemaphoreType.DMA((ng, 3)),
                        pltpu.SemaphoreType.DMA((N_PEERS,)), pltpu.SemaphoreType.DMA((N_PEERS,)),
                        pltpu.SemaphoreType.DMA((N_PEERS,)), pltpu.SemaphoreType.DMA((N_PEERS,)),
                        pltpu.SemaphoreType.DMA((6,))],
        compiler_params=pltpu.CompilerParams(has_side_effects=True, vmem_limit_bytes=VMEM_LIMIT),
    )(*grads, small_packed)
    return outs[:ng], outs[ng:2 * ng], outs[3 * ng].reshape(SMALL_ROWS, LANES)


def _adamw_math(w, g, m, v):
    m2 = ADAM_B1 * m + (1.0 - ADAM_B1) * g
    v2 = ADAM_B2 * v + (1.0 - ADAM_B2) * (g * g)
    m_hat = m2 / (1.0 - ADAM_B1 ** ADAM_STEP)
    v_hat = v2 / (1.0 - ADAM_B2 ** ADAM_STEP)
    delta = -ADAM_LR * (m_hat / (jnp.sqrt(v_hat) + ADAM_EPS) + ADAM_WD * w)
    return delta, m2, v2


ADAM_ROW_TILES = 2


def _adamw_big(own, parts, w, m, v, name):
    shape = w.shape
    own_is_blocks = own.ndim == 3
    tr = shape[0] // ADAM_ROW_TILES
    n_parts = parts.shape[0]

    def body(own_ref, p_ref, w_ref, m_ref, v_ref, g_ref, d_ref, m2_ref, v2_ref, own_s, sem):
        rows = pl.ds(pl.multiple_of(pl.program_id(0) * tr, 16), tr)
        if own_is_blocks:
            cp = pltpu.make_async_copy(own_ref.at[_mesh_pos()[3], rows], own_s, sem)
        else:
            cp = pltpu.make_async_copy(own_ref.at[rows], own_s, sem)
        cp.start()
        cp.wait()
        g = own_s[...].astype(F32)
        for i in range(n_parts):
            g = g + p_ref[i].astype(F32)
        delta, m2, v2 = _adamw_math(w_ref[...], g, m_ref[...], v_ref[...])
        g_ref[...] = g
        d_ref[...] = delta
        m2_ref[...] = m2
        v2_ref[...] = v2

    tile = pl.BlockSpec((tr, shape[1]), lambda i: (i, 0))
    return pl.pallas_call(
        body, name=name, grid=(ADAM_ROW_TILES,),
        in_specs=[ANY_SPEC, pl.BlockSpec((n_parts, tr, shape[1]), lambda i: (0, i, 0)), tile, tile, tile],
        out_specs=[tile] * 4, out_shape=[jax.ShapeDtypeStruct(shape, F32)] * 4,
        scratch_shapes=[pltpu.VMEM((tr, shape[1]), own.dtype), pltpu.SemaphoreType.DMA(())],
        compiler_params=_cparams(("arbitrary",)),
    )(own, parts, w, m, v)


def _pack_small(grads):
    names = list(SMALL)

    def body(*refs):
        ins, out = dict(zip(names, refs[:-1])), refs[-1]
        out[...] = jnp.zeros_like(out)
        for re, im in SMALL_PAIRS:
            off, rows = SMALL_OFFSET[re], SMALL[re][0]
            out[off:off + rows, :] = jnp.concatenate([ins[re][...], ins[im][...]], axis=1)
        for n in SMALL_VECS:
            off, vec = SMALL_OFFSET[n], ins[n][...]
            for i in range(SMALL[n][1] // LANES):
                out[off + i:off + i + 1, :] = vec[:, i * LANES:(i + 1) * LANES]
        for n in SMALL_TILES:
            off, (rows, cols) = SMALL_OFFSET[n], SMALL[n]
            out[off:off + rows, 0:cols] = ins[n][...]

    return pl.pallas_call(
        body, name="pack_small", out_shape=jax.ShapeDtypeStruct((SMALL_ROWS, LANES), F32),
        compiler_params=_cparams(),
    )(*[grads[n] for n in names])


def _unpack_small_ref(g_ref, n):
    off, (rows, cols) = SMALL_OFFSET[n], SMALL[n]
    for re, im in SMALL_PAIRS:
        if n == re:
            return g_ref[off:off + rows, 0:HALF_LANES]
        if n == im:
            return g_ref[off:off + rows, HALF_LANES:LANES]
    if n in SMALL_VECS:
        return jnp.concatenate([g_ref[off + i:off + i + 1, :] for i in range(cols // LANES)], axis=1)
    return g_ref[off:off + rows, 0:cols]


def _adamw_small(g_packed, w, m, v):
    names = list(SMALL_PARAMS)
    n = len(names)

    def body(g_ref, *refs):
        w_refs, m_refs, v_refs, outs = refs[:n], refs[n:2 * n], refs[2 * n:3 * n], refs[3 * n:]
        for idx, name in enumerate(names):
            g = _unpack_small_ref(g_ref, name)
            delta, m2, v2 = _adamw_math(w_refs[idx][...], g, m_refs[idx][...], v_refs[idx][...])
            outs[4 * idx][...] = g
            outs[4 * idx + 1][...] = delta
            outs[4 * idx + 2][...] = m2
            outs[4 * idx + 3][...] = v2
        outs[4 * n][...] = _unpack_small_ref(g_ref, "loss")

    outs = pl.pallas_call(
        body, name="adamw_small",
        out_shape=[jax.ShapeDtypeStruct(SMALL[name], F32) for name in names for _ in range(4)]
        + [jax.ShapeDtypeStruct(SMALL["loss"], F32)],
        compiler_params=_cparams(),
    )(g_packed, *[w[k] for k in names], *[m[k] for k in names], *[v[k] for k in names])
    return {name: outs[4 * idx:4 * idx + 4] for idx, name in enumerate(names)}, outs[4 * n]


WEIGHT_NAMES = ['norm_ffn1', 'ffn1_w_gate', 'ffn1_w_up', 'ffn1_w_down', 'norm_mix', 'w_in', 'attn_sinks',
                'ssm_lambda_re', 'ssm_lambda_im', 'ssm_log_dt', 'ssm_b_re', 'ssm_b_im', 'ssm_c_re', 'ssm_c_im',
                'ssm_d', 'ssm_glu_w', 'ssm_glu_b', 'attn_out_norm', 'ssm_out_norm', 'w_out', 'norm_ffn2',
                'ffn2_w_gate', 'ffn2_w_up', 'ffn2_w_down', 'final_norm']


def kernel(x, norm_ffn1, ffn1_w_gate, ffn1_w_up, ffn1_w_down, norm_mix, w_in, attn_sinks, ssm_lambda_re, ssm_lambda_im, ssm_log_dt, ssm_b_re, ssm_b_im, ssm_c_re, ssm_c_im, ssm_d, ssm_glu_w, ssm_glu_b, attn_out_norm, ssm_out_norm, w_out, norm_ffn2, ffn2_w_gate, ffn2_w_up, ffn2_w_down, final_norm, loss_target, m_norm_ffn1, m_ffn1_w_gate, m_ffn1_w_up, m_ffn1_w_down, m_norm_mix, m_w_in, m_attn_sinks, m_ssm_lambda_re, m_ssm_lambda_im, m_ssm_log_dt, m_ssm_b_re, m_ssm_b_im, m_ssm_c_re, m_ssm_c_im, m_ssm_d, m_ssm_glu_w, m_ssm_glu_b, m_attn_out_norm, m_ssm_out_norm, m_w_out, m_norm_ffn2, m_ffn2_w_gate, m_ffn2_w_up, m_ffn2_w_down, m_final_norm, v_norm_ffn1, v_ffn1_w_gate, v_ffn1_w_up, v_ffn1_w_down, v_norm_mix, v_w_in, v_attn_sinks, v_ssm_lambda_re, v_ssm_lambda_im, v_ssm_log_dt, v_ssm_b_re, v_ssm_b_im, v_ssm_c_re, v_ssm_c_im, v_ssm_d, v_ssm_glu_w, v_ssm_glu_b, v_attn_out_norm, v_ssm_out_norm, v_w_out, v_norm_ffn2, v_ffn2_w_gate, v_ffn2_w_up, v_ffn2_w_down, v_final_norm):
    args = dict(locals())
    weights = {n: args[n] for n in WEIGHT_NAMES}
    moms = {n: args["m_" + n] for n in WEIGHT_NAMES}
    vars_ = {n: args["v_" + n] for n in WEIGHT_NAMES}

    def shard2d(a, k):
        a = a.reshape(a.shape[-2], a.shape[-1])
        return a.T if BIG[k][3] else a

    def shard_master(a, k):
        return (a.T if BIG[k][3] else a).reshape(weights[BIG[k][0]].shape)

    def blocks(g, k):
        return g.reshape(N_DEV, BIG[k][1], BIG[k][2])

    def full(g, k):
        return g.reshape(N_DEV * BIG[k][1], BIG[k][2])

    shards = dict(zip(BIG, _cast_shards({k: shard2d(weights[BIG[k][0]], k) for k in BIG})))
    nf = len(FIRST_GROUP)
    got = _gather_first([shards[k] for k in FIRST_GROUP], [shards[k] for k in LATE_GROUP])
    w_first = {k: full(g, k) for k, g in zip(FIRST_GROUP, got[:nf])}
    late = {}
    late["own_sems"], late["srcs"], late["lands"], w_token = _late_gather_call(
        "gather_late_start", 0, [shards[k] for k in LATE_GROUP], got[nf:], [])

    def late_pass(dep):
        late["pass_sems"], late["srcs"], late["lands"], token = _late_gather_call(
            "gather_late_pass", 1, late["srcs"], late["lands"], late["own_sems"], after=dep)
        return token

    def late_weights(dep):
        _, _, lands, _ = _late_gather_call("gather_late_wait", 2, late["srcs"], late["lands"],
                                           late["own_sems"] + late["pass_sems"], after=dep)
        return {k: full(g, k) for k, g in zip(LATE_GROUP, lands)}

    early = {}

    def early_grads(g):
        srcs = [blocks(g[k], k) for k in LATE_GROUP]
        lands = [lax.empty((N_PEERS, BIG[k][1], BIG[k][2]), BF16) for k in LATE_GROUP]
        early["send"], early["recv"], early["srcs"], early["lands"], token = _split_start(
            "grads_late_start", srcs, lands, scatter=True)
        return token

    def small2d(a, n):
        if n in SMALL_TRANSPOSED:
            a = jnp.swapaxes(a, -1, -2)
        return a.reshape(SMALL[n])

    def small_master(a, n):
        if n in SMALL_TRANSPOSED:
            shape = weights[n].shape
            return jnp.swapaxes(a.reshape(shape[:-2] + (shape[-1], shape[-2])), -1, -2)
        return a.reshape(weights[n].shape)

    small_p = {n: small2d(weights[n], n) for n in SMALL_PARAMS}
    _, grad_x, g_first, g_small = _local_step(
        x.reshape(SEQ, D_MODEL), loss_target.reshape(SEQ, D_MODEL), w_first, small_p, late_weights, early_grads,
        after=w_token, midway=late_pass)

    own_sums, first_parts, small_grad = _exchange_last([blocks(g_first[k], k) for k in FIRST_GROUP],
                                                       _pack_small(g_small))
    own_late, late_parts = _split_wait("grads_late_wait", early["send"], early["recv"], early["srcs"],
                                       early["lands"], True, small_grad)
    own = dict(zip(FIRST_GROUP + LATE_GROUP, list(own_sums) + list(own_late)))
    parts = dict(zip(FIRST_GROUP + LATE_GROUP, list(first_parts) + list(late_parts)))
    outs = {}
    for k in BIG:
        n = BIG[k][0]
        outs[n] = [shard_master(o, k) for o in
                   _adamw_big(own[k], parts[k], shard2d(weights[n], k), shard2d(moms[n], k), shard2d(vars_[n], k),
                              "adamw_" + n)]
    small_out, loss_row = _adamw_small(small_grad, small_p, {n: small2d(moms[n], n) for n in SMALL_PARAMS},
                                       {n: small2d(vars_[n], n) for n in SMALL_PARAMS})
    for n in SMALL_PARAMS:
        outs[n] = [small_master(o, n) for o in small_out[n]]

    result = [loss_row[0, 0], grad_x.reshape(x.shape)]
    for i in range(4):
        result += [outs[n][i] for n in WEIGHT_NAMES]
    return tuple(result)
```

```python
import functools

import jax
import jax.numpy as jnp
from jax import lax
from jax.experimental import pallas as pl
from jax.experimental.pallas import tpu as pltpu

F32 = jnp.float32
BF16 = jnp.bfloat16

N_DEV = 8
SEQ = 2048
D_MODEL = 1024
D_FF = 2816
ATTN_HEADS = 8
KV_HEADS = 2
HEAD_DIM = 64
ATTN_WIDTH = 512
KV_WIDTH = 128
WINDOW = 128
SSM_WIDTH = 512
IN_WIDTH = 1280
EPS = 1e-6
MASKED_DISTANCE = 1e33
LAMBDA_RE_MAX = -1e-4
LANES = 128
N_LANE_BLOCKS = 16
SCAN_CHUNK = SEQ // 8

ADAM_LR = 0.001
ADAM_B1 = 0.9
ADAM_B2 = 0.999
ADAM_EPS = 1e-08
ADAM_WD = 0.01
ADAM_STEP = 10

VMEM_LIMIT = 60 * 1024 * 1024
MESH_ID = pl.DeviceIdType.MESH


def _cparams(sem=None):
    return pltpu.CompilerParams(dimension_semantics=sem, vmem_limit_bytes=VMEM_LIMIT)


def _dot(a, b):
    return jnp.dot(a, b, preferred_element_type=F32)


def _dot_nt(a, b):
    return lax.dot_general(a, b, (((1,), (1,)), ((), ())), preferred_element_type=F32)


def _dot_tn(a, b):
    return lax.dot_general(a, b, (((0,), (0,)), ((), ())), preferred_element_type=F32)


def _rms_fwd(x, g):
    r = lax.rsqrt(jnp.mean(x * x, axis=-1, keepdims=True) + EPS)
    return x * r * g


def _rms_bwd(dh, x, g):
    r = lax.rsqrt(jnp.mean(x * x, axis=-1, keepdims=True) + EPS)
    xh = x * r
    dg = jnp.sum(dh * xh, axis=0, keepdims=True)
    dxh = dh * g
    dx = r * (dxh - xh * jnp.mean(dxh * xh, axis=-1, keepdims=True))
    return dx, dg


def _sigmoid(x):
    return 1.0 / (1.0 + jnp.exp(-x))


FFN_TM = 512
FFN_TF = 1408


def _ffn_fwd(x, g, wgt, wut, wd, name, after=None, head=None):
    tm, tf = FFN_TM // 2, D_FF
    nj = D_FF // tf
    deps = [] if after is None else [after]
    n_in = len(deps) + (2 if head else 0)

    def body(x_ref, g_ref, wg_ref, wu_ref, wd_ref, *rest):
        i = pl.program_id(0)
        j = pl.program_id(1)
        if head:
            gf_ref, t_ref = rest[len(deps):n_in]
            xo_ref, h_ref, a_ref, b_ref, loss_ref, dgf_ref, h_s, acc = rest[n_in:]
        else:
            xo_ref, h_ref, a_ref, b_ref, h_s, acc = rest[n_in:]

        @pl.when(j == 0)
        def _():
            h = _rms_fwd(x_ref[...], g_ref[...]).astype(BF16)
            h_s[...] = h
            h_ref[...] = h
            acc[...] = jnp.zeros_like(acc)

        h = h_s[...]
        a = _dot_nt(h, wg_ref[...])
        b = _dot_nt(h, wu_ref[...])
        a_ref[...] = a.astype(BF16)
        b_ref[...] = b.astype(BF16)
        s = (a * _sigmoid(a) * b).astype(BF16)
        acc[...] += _dot(s, wd_ref[...])

        @pl.when(j == nj - 1)
        def _():
            xo = x_ref[...] + 0.5 * acc[...]
            if not head:
                xo_ref[...] = xo
                return
            gf = gf_ref[...]
            err = _rms_fwd(xo, gf) - t_ref[...]
            part = jnp.broadcast_to(0.5 * jnp.sum(err * err) / D_MODEL, (1, LANES))
            dx, dgf = _rms_bwd(err * (1.0 / D_MODEL), xo, gf)
            xo_ref[...] = dx

            @pl.when(i == 0)
            def _():
                loss_ref[...] = part
                dgf_ref[...] = dgf

            @pl.when(i != 0)
            def _():
                loss_ref[...] += part
                dgf_ref[...] += dgf

    row = lambda i, j: (i, 0)
    const = lambda i, j: (0, 0)
    head_in = [pl.BlockSpec((1, D_MODEL), const), pl.BlockSpec((tm, D_MODEL), row)] if head else []
    head_out = [pl.BlockSpec((1, LANES), const), pl.BlockSpec((1, D_MODEL), const)] if head else []
    head_shape = [jax.ShapeDtypeStruct((1, LANES), F32), jax.ShapeDtypeStruct((1, D_MODEL), F32)] if head else []
    return pl.pallas_call(
        body, name=name, grid=(SEQ // tm, nj),
        in_specs=[pl.BlockSpec((tm, D_MODEL), row), pl.BlockSpec((1, D_MODEL), const),
                  pl.BlockSpec((tf, D_MODEL), lambda i, j: (j, 0)),
                  pl.BlockSpec((tf, D_MODEL), lambda i, j: (j, 0)),
                  pl.BlockSpec((tf, D_MODEL), lambda i, j: (j, 0))] + [pl.BlockSpec(memory_space=pl.ANY)] * len(deps)
        + head_in,
        out_specs=[pl.BlockSpec((tm, D_MODEL), row), pl.BlockSpec((tm, D_MODEL), row),
                   pl.BlockSpec((tm, tf), lambda i, j: (i, j)),
                   pl.BlockSpec((tm, tf), lambda i, j: (i, j))] + head_out,
        out_shape=[jax.ShapeDtypeStruct((SEQ, D_MODEL), F32), jax.ShapeDtypeStruct((SEQ, D_MODEL), BF16),
                   jax.ShapeDtypeStruct((SEQ, D_FF), BF16), jax.ShapeDtypeStruct((SEQ, D_FF), BF16)] + head_shape,
        scratch_shapes=[pltpu.VMEM((tm, D_MODEL), BF16), pltpu.VMEM((tm, D_MODEL), F32)],
        compiler_params=_cparams(("arbitrary" if head else "parallel", "arbitrary")),
    )(x, g, wgt, wut, wd, *deps, *(head or ()))


def _ffn_bwd_act(dxo, x, g, a, b, wgt, wut, wd, name):
    tm, tf = FFN_TM, FFN_TF
    nj = D_FF // tf

    def body(dxo_ref, x_ref, g_ref, a_ref, b_ref, wg_ref, wu_ref, wd_ref,
             dx_ref, da_ref, db_ref, s_ref, df_ref, dg_ref, df_s, acc):
        i = pl.program_id(0)
        j = pl.program_id(1)

        @pl.when(j == 0)
        def _():
            df = (0.5 * dxo_ref[...]).astype(BF16)
            df_s[...] = df
            df_ref[...] = df
            acc[...] = jnp.zeros_like(acc)

        ds = _dot_nt(df_s[...], wd_ref[...])
        av = a_ref[...].astype(F32)
        bv = b_ref[...].astype(F32)
        sig = _sigmoid(av)
        sl = av * sig
        s_ref[...] = (sl * bv).astype(BF16)
        db = (ds * sl).astype(BF16)
        da = (ds * bv * (sig * (1.0 + av * (1.0 - sig)))).astype(BF16)
        da_ref[...] = da
        db_ref[...] = db
        acc[...] += _dot(da, wg_ref[...]) + _dot(db, wu_ref[...])

        @pl.when(j == nj - 1)
        def _():
            dx, dg = _rms_bwd(acc[...], x_ref[...], g_ref[...])
            dx_ref[...] = dxo_ref[...] + dx

            @pl.when(i == 0)
            def _():
                dg_ref[...] = dg

            @pl.when(i != 0)
            def _():
                dg_ref[...] += dg

    row = lambda i, j: (i, 0)
    col = lambda i, j: (j, 0)
    tile = lambda i, j: (i, j)
    return pl.pallas_call(
        body, name=name, grid=(SEQ // tm, nj),
        in_specs=[pl.BlockSpec((tm, D_MODEL), row), pl.BlockSpec((tm, D_MODEL), row),
                  pl.BlockSpec((1, D_MODEL), lambda i, j: (0, 0)),
                  pl.BlockSpec((tm, tf), tile), pl.BlockSpec((tm, tf), tile),
                  pl.BlockSpec((tf, D_MODEL), col), pl.BlockSpec((tf, D_MODEL), col), pl.BlockSpec((tf, D_MODEL), col)],
        out_specs=[pl.BlockSpec((tm, D_MODEL), row),
                   pl.BlockSpec((tm, tf), tile), pl.BlockSpec((tm, tf), tile), pl.BlockSpec((tm, tf), tile),
                   pl.BlockSpec((tm, D_MODEL), row),
                   pl.BlockSpec((1, D_MODEL), lambda i, j: (0, 0))],
        out_shape=[jax.ShapeDtypeStruct((SEQ, D_MODEL), F32),
                   jax.ShapeDtypeStruct((SEQ, D_FF), BF16), jax.ShapeDtypeStruct((SEQ, D_FF), BF16),
                   jax.ShapeDtypeStruct((SEQ, D_FF), BF16),
                   jax.ShapeDtypeStruct((SEQ, D_MODEL), BF16),
                   jax.ShapeDtypeStruct((1, D_MODEL), F32)],
        scratch_shapes=[pltpu.VMEM((tm, D_MODEL), BF16), pltpu.VMEM((tm, D_MODEL), F32)],
        compiler_params=_cparams(("arbitrary", "arbitrary")),
    )(dxo, x, g, a, b, wgt, wut, wd)


def _mm_tn(pairs, name, tmm=256):
    m = pairs[0][0].shape[1]
    n_pairs = len(pairs)

    def body(*refs):
        ins, outs = refs[:2 * n_pairs], refs[2 * n_pairs:]
        for p in range(n_pairs):
            outs[p][...] = _dot_tn(ins[2 * p][...], ins[2 * p + 1][...]).astype(BF16)

    in_specs, out_specs, out_shape, args = [], [], [], []
    for a, b in pairs:
        n = b.shape[1]
        in_specs += [pl.BlockSpec((SEQ, tmm), lambda i: (0, i)), pl.BlockSpec((SEQ, n), lambda i: (0, 0))]
        out_specs.append(pl.BlockSpec((tmm, n), lambda i: (i, 0)))
        out_shape.append(jax.ShapeDtypeStruct((m, n), BF16))
        args += [a, b]
    return pl.pallas_call(body, name=name, grid=(m // tmm,), in_specs=in_specs, out_specs=out_specs,
                          out_shape=out_shape, compiler_params=_cparams(("parallel",)))(*args)


MIX_TM = 512


def _mixin_fwd(x, g, wint):
    tm = MIX_TM

    def body(x_ref, g_ref, w_ref, h_ref, q_ref, k_ref, v_ref, u_ref):
        h = _rms_fwd(x_ref[...], g_ref[...]).astype(BF16)
        h_ref[...] = h
        proj = _dot_nt(h, w_ref[...])
        q_ref[...] = proj[:, :ATTN_WIDTH].T
        k_ref[...] = proj[:, ATTN_WIDTH:ATTN_WIDTH + KV_WIDTH]
        v_ref[...] = proj[:, ATTN_WIDTH + KV_WIDTH:ATTN_WIDTH + 2 * KV_WIDTH]
        u_ref[...] = proj[:, ATTN_WIDTH + 2 * KV_WIDTH:]

    row = lambda i: (i, 0)
    return pl.pallas_call(
        body, name="mixin_fwd", grid=(SEQ // tm,),
        in_specs=[pl.BlockSpec((tm, D_MODEL), row), pl.BlockSpec((1, D_MODEL), lambda i: (0, 0)),
                  pl.BlockSpec((IN_WIDTH, D_MODEL), lambda i: (0, 0))],
        out_specs=[pl.BlockSpec((tm, D_MODEL), row), pl.BlockSpec((ATTN_WIDTH, tm), lambda i: (0, i)),
                   pl.BlockSpec((tm, KV_WIDTH), row), pl.BlockSpec((tm, KV_WIDTH), row),
                   pl.BlockSpec((tm, SSM_WIDTH), row)],
        out_shape=[jax.ShapeDtypeStruct((SEQ, D_MODEL), BF16), jax.ShapeDtypeStruct((ATTN_WIDTH, SEQ), F32),
                   jax.ShapeDtypeStruct((SEQ, KV_WIDTH), F32), jax.ShapeDtypeStruct((SEQ, KV_WIDTH), F32),
                   jax.ShapeDtypeStruct((SEQ, SSM_WIDTH), F32)],
        compiler_params=_cparams(("parallel",)),
    )(x, g, wint)


def _mixin_bwd(dqt, dk, dv, du, wint, x, g, dres):
    tm = MIX_TM

    def body(dq_ref, dk_ref, dv_ref, du_ref, w_ref, x_ref, g_ref, dres_ref, dx_ref, dp_ref, dg_ref):
        i = pl.program_id(0)
        dp = jnp.concatenate([dq_ref[...].T, dk_ref[...], dv_ref[...], du_ref[...]], axis=-1).astype(BF16)
        dp_ref[...] = dp
        dh = _dot(dp, w_ref[...])
        dx, dg = _rms_bwd(dh, x_ref[...], g_ref[...])
        dx_ref[...] = dres_ref[...] + dx

        @pl.when(i == 0)
        def _():
            dg_ref[...] = dg

        @pl.when(i != 0)
        def _():
            dg_ref[...] += dg

    row = lambda i: (i, 0)
    const = lambda i: (0, 0)
    return pl.pallas_call(
        body, name="mixin_bwd", grid=(SEQ // tm,),
        in_specs=[pl.BlockSpec((ATTN_WIDTH, tm), lambda i: (0, i)), pl.BlockSpec((tm, KV_WIDTH), row),
                  pl.BlockSpec((tm, KV_WIDTH), row), pl.BlockSpec((tm, SSM_WIDTH), row),
                  pl.BlockSpec((IN_WIDTH, D_MODEL), const), pl.BlockSpec((tm, D_MODEL), row),
                  pl.BlockSpec((1, D_MODEL), const), pl.BlockSpec((tm, D_MODEL), row)],
        out_specs=[pl.BlockSpec((tm, D_MODEL), row), pl.BlockSpec((tm, IN_WIDTH), row),
                   pl.BlockSpec((1, D_MODEL), const)],
        out_shape=[jax.ShapeDtypeStruct((SEQ, D_MODEL), F32), jax.ShapeDtypeStruct((SEQ, IN_WIDTH), BF16),
                   jax.ShapeDtypeStruct((1, D_MODEL), F32)],
        compiler_params=_cparams(("arbitrary",)),
    )(dqt, dk, dv, du, wint, x, g, dres)


N_QBLOCKS = SEQ // WINDOW
GROUP = ATTN_HEADS // KV_HEADS
SCALE = HEAD_DIM ** -0.5


def _alibi_slope(h):
    return 2.0 ** (-8.0 * (h + 1) / ATTN_HEADS)


def _window_masks(n):
    s_idx = lax.broadcasted_iota(jnp.int32, (3 * WINDOW, WINDOW), 0)
    t_idx = lax.broadcasted_iota(jnp.int32, (3 * WINDOW, WINDOW), 1)
    absrel = jnp.abs(s_idx - WINDOW - t_idx)
    key_pos = n * WINDOW - WINDOW + s_idx
    valid = (absrel <= WINDOW) & (key_pos >= 0) & (key_pos < SEQ)
    return jnp.where(valid, absrel.astype(F32), MASKED_DISTANCE)


def _group_cols(ref, r0, gi):
    return jnp.concatenate(
        [ref[(gi * GROUP + hh) * HEAD_DIM:(gi * GROUP + hh + 1) * HEAD_DIM, pl.ds(r0, WINDOW)].astype(BF16)
         for hh in range(GROUP)], axis=1)


def _group_probs(qgt, kw, dist, gi, sk_ref):
    bias = jnp.concatenate([-_alibi_slope(gi * GROUP + hh) * dist for hh in range(GROUP)], axis=1)
    sink = jnp.concatenate([jnp.full((1, WINDOW), sk_ref[0, gi * GROUP + hh], F32) for hh in range(GROUP)], axis=1)
    s = _dot(kw, qgt) * SCALE + bias
    m = jnp.maximum(jnp.max(s, axis=0, keepdims=True), sink)
    p = jnp.exp(s - m)
    ps = jnp.exp(sink - m)
    inv = 1.0 / (jnp.sum(p, axis=0, keepdims=True) + ps)
    return p * inv, ps * inv


def _pad_window(src_ref, dst_ref):
    zeros = jnp.zeros((WINDOW, KV_WIDTH), BF16)
    dst_ref[0:WINDOW, :] = zeros
    dst_ref[WINDOW + SEQ:, :] = zeros
    dst_ref[WINDOW:WINDOW + SEQ, :] = src_ref[...].astype(BF16)


def _attn_fwd(qt, k, v, sinks, after=None):
    deps = [] if after is None else [after]

    def body(sk_ref, qt_ref, k_ref, v_ref, *rest):
        o_ref, kp_ref, vp_ref = rest[len(deps):]
        _pad_window(k_ref, kp_ref)
        _pad_window(v_ref, vp_ref)

        def blk(n, carry):
            r0 = pl.multiple_of(n * WINDOW, WINDOW)
            dist = _window_masks(n)
            for gi in range(KV_HEADS):
                kw = kp_ref[pl.ds(r0, 3 * WINDOW), gi * HEAD_DIM:(gi + 1) * HEAD_DIM]
                vw = vp_ref[pl.ds(r0, 3 * WINDOW), gi * HEAD_DIM:(gi + 1) * HEAD_DIM]
                pr, _ = _group_probs(_group_cols(qt_ref, r0, gi), kw, dist, gi, sk_ref)
                og = _dot_tn(pr.astype(BF16), vw)
                for hh in range(GROUP):
                    h = gi * GROUP + hh
                    o_ref[pl.ds(r0, WINDOW), h * HEAD_DIM:(h + 1) * HEAD_DIM] = og[hh * WINDOW:(hh + 1) * WINDOW]
            return carry

        lax.fori_loop(0, N_QBLOCKS, blk, 0)

    vmem = pl.BlockSpec(memory_space=pltpu.VMEM)
    return pl.pallas_call(
        body, name="attn_fwd",
        in_specs=[pl.BlockSpec(memory_space=pltpu.SMEM), vmem, vmem, vmem]
        + [pl.BlockSpec(memory_space=pl.ANY)] * len(deps), out_specs=vmem,
        out_shape=jax.ShapeDtypeStruct((SEQ, ATTN_WIDTH), F32),
        scratch_shapes=[pltpu.VMEM((SEQ + 2 * WINDOW, KV_WIDTH), BF16)] * 2,
        compiler_params=_cparams(),
    )(sinks, qt, k, v, *deps)


def _attn_bwd(qt, k, v, sinks, dot_):
    def body(sk_ref, qt_ref, k_ref, v_ref, dot_ref, dqt_ref, dk_ref, dv_ref, dsk_ref,
             dsk_acc, kp_ref, vp_ref, dkp_ref, dvp_ref):
        _pad_window(k_ref, kp_ref)
        _pad_window(v_ref, vp_ref)
        dkp_ref[...] = jnp.zeros_like(dkp_ref)
        dvp_ref[...] = jnp.zeros_like(dvp_ref)
        dsk_acc[...] = jnp.zeros_like(dsk_acc)

        def blk(n, carry):
            r0 = pl.multiple_of(n * WINDOW, WINDOW)
            dist = _window_masks(n)
            for gi in range(KV_HEADS):
                gcols = slice(gi * HEAD_DIM, (gi + 1) * HEAD_DIM)
                kw = kp_ref[pl.ds(r0, 3 * WINDOW), gcols]
                vw = vp_ref[pl.ds(r0, 3 * WINDOW), gcols]
                qgt = _group_cols(qt_ref, r0, gi)
                dogt = _group_cols(dot_ref, r0, gi)
                pr, psink = _group_probs(qgt, kw, dist, gi, sk_ref)
                dp = _dot(vw, dogt)
                delta = jnp.sum(pr * dp, axis=0, keepdims=True)
                ds = (pr * (dp - delta)).astype(BF16)
                dsk_acc[gi:gi + 1, :] += -(psink * delta)
                dqgt = _dot_tn(kw, ds) * SCALE
                for hh in range(GROUP):
                    h = gi * GROUP + hh
                    dqt_ref[h * HEAD_DIM:(h + 1) * HEAD_DIM, pl.ds(r0, WINDOW)] = dqgt[:, hh * WINDOW:(hh + 1) * WINDOW]
                dkp_ref[pl.ds(r0, 3 * WINDOW), gcols] += _dot_nt(ds, qgt) * SCALE
                dvp_ref[pl.ds(r0, 3 * WINDOW), gcols] += _dot_nt(pr.astype(BF16), dogt)
            return carry

        lax.fori_loop(0, N_QBLOCKS, blk, 0)
        for h in range(ATTN_HEADS):
            gi, hh = divmod(h, GROUP)
            dsk_ref[:, h:h + 1] = jnp.sum(dsk_acc[gi:gi + 1, hh * WINDOW:(hh + 1) * WINDOW], axis=1, keepdims=True)
        dk_ref[...] = dkp_ref[WINDOW:WINDOW + SEQ, :]
        dv_ref[...] = dvp_ref[WINDOW:WINDOW + SEQ, :]

    vmem = pl.BlockSpec(memory_space=pltpu.VMEM)
    padded = (SEQ + 2 * WINDOW, KV_WIDTH)
    return pl.pallas_call(
        body, name="attn_bwd",
        in_specs=[pl.BlockSpec(memory_space=pltpu.SMEM), vmem, vmem, vmem, vmem],
        out_specs=[vmem, vmem, vmem, vmem],
        out_shape=[jax.ShapeDtypeStruct((ATTN_WIDTH, SEQ), F32),
                   jax.ShapeDtypeStruct((SEQ, KV_WIDTH), F32), jax.ShapeDtypeStruct((SEQ, KV_WIDTH), F32),
                   jax.ShapeDtypeStruct((1, ATTN_HEADS), F32)],
        scratch_shapes=[pltpu.VMEM((KV_HEADS, GROUP * WINDOW), F32), pltpu.VMEM(padded, BF16),
                        pltpu.VMEM(padded, BF16), pltpu.VMEM(padded, F32), pltpu.VMEM(padded, F32)],
        compiler_params=_cparams(),
    )(sinks, qt, k, v, dot_)


HALF_LANES = LANES // 2
BLOCK_ROWS = 32


def _embed_block(bt, q):
    z = jnp.zeros((16, HALF_LANES), bt.dtype)
    blk = jnp.concatenate([jnp.concatenate([bt[:16], z], axis=1), jnp.concatenate([z, bt[16:]], axis=1)], axis=0)
    parts = [jnp.zeros((BLOCK_ROWS * q, LANES), bt.dtype)] if q else []
    parts.append(blk)
    if q < 3:
        parts.append(jnp.zeros((BLOCK_ROWS * (3 - q), LANES), bt.dtype))
    return jnp.concatenate(parts, axis=0)


def _extract_block(m, q):
    blk = m[BLOCK_ROWS * q:BLOCK_ROWS * (q + 1)]
    return jnp.concatenate([blk[:16, :HALF_LANES], blk[16:, HALF_LANES:]], axis=0)


def _ssm_prep(lam_re, lam_im, log_dt, bt_re, bt_im, c_re, c_im):
    nb = 2 * N_LANE_BLOCKS

    def body(lr_ref, li_ref, ldt_ref, btr_ref, bti_ref, ctr_ref, cti_ref, ar_ref, ai_ref, bb_ref, cc_ref):
        lr = jnp.minimum(lr_ref[...], LAMBDA_RE_MAX)
        li = li_ref[...]
        dt = jnp.exp(ldt_ref[...])
        mag = jnp.exp(lr * dt)
        ar = mag * jnp.cos(li * dt)
        ai = mag * jnp.sin(li * dt)
        den = lr * lr + li * li
        cr = ((ar - 1.0) * lr + ai * li) / den
        ci = (ai * lr - (ar - 1.0) * li) / den
        ar_ref[...] = ar
        ai_ref[...] = ai
        for i in range(nb):
            q = i % 4
            rows = slice(BLOCK_ROWS * i, BLOCK_ROWS * (i + 1))
            br = _embed_block(btr_ref[rows, :], q)
            bi = _embed_block(bti_ref[rows, :], q)
            cri, cii = cr[i:i + 1, :], ci[i:i + 1, :]
            bb_ref[i] = jnp.concatenate([cri * br - cii * bi, cri * bi + cii * br], axis=1).astype(BF16)
            cc_ref[i] = jnp.concatenate([_embed_block(ctr_ref[rows, :], q).T,
                                         -_embed_block(cti_ref[rows, :], q).T], axis=0).astype(BF16)

    return pl.pallas_call(
        body, name="ssm_prep",
        out_shape=[jax.ShapeDtypeStruct((nb, LANES), F32), jax.ShapeDtypeStruct((nb, LANES), F32),
                   jax.ShapeDtypeStruct((nb, LANES, 2 * LANES), BF16),
                   jax.ShapeDtypeStruct((nb, 2 * LANES, LANES), BF16)],
        compiler_params=_cparams(),
    )(lam_re, lam_im, log_dt, bt_re, bt_im, c_re, c_im)


def _ssm_prep_bwd(lam_re, lam_im, log_dt, bt_re, bt_im, dar, dai, dbb, dcc):
    nb = 2 * N_LANE_BLOCKS

    def body(lr_ref, li_ref, ldt_ref, btr_ref, bti_ref, dar_ref, dai_ref, dbb_ref, dcc_ref,
             glr_ref, gli_ref, gdt_ref, gbr_ref, gbi_ref, gcre_ref, gcim_ref, gcr_s, gci_s):
        lam = lr_ref[...]
        lr = jnp.minimum(lam, LAMBDA_RE_MAX)
        li = li_ref[...]
        dt = jnp.exp(ldt_ref[...])
        mag = jnp.exp(lr * dt)
        cs = jnp.cos(li * dt)
        sn = jnp.sin(li * dt)
        ar = mag * cs
        ai = mag * sn
        den = lr * lr + li * li
        nr = (ar - 1.0) * lr + ai * li
        ni = ai * lr - (ar - 1.0) * li
        cr = nr / den
        ci = ni / den
        for i in range(nb):
            q = i % 4
            rows = slice(BLOCK_ROWS * i, BLOCK_ROWS * (i + 1))
            br = _embed_block(btr_ref[rows, :], q)
            bi = _embed_block(bti_ref[rows, :], q)
            gbbr = dbb_ref[i, :, :LANES]
            gbbi = dbb_ref[i, :, LANES:]
            cri, cii = cr[i:i + 1, :], ci[i:i + 1, :]
            gcr_s[i:i + 1, :] = jnp.sum(gbbr * br + gbbi * bi, axis=0, keepdims=True)
            gci_s[i:i + 1, :] = jnp.sum(gbbi * br - gbbr * bi, axis=0, keepdims=True)
            gbr_ref[rows, :] = _extract_block(cri * gbbr + cii * gbbi, q)
            gbi_ref[rows, :] = _extract_block(cri * gbbi - cii * gbbr, q)
            gcre_ref[rows, :] = _extract_block(dcc_ref[i, :LANES, :].T, q)
            gcim_ref[rows, :] = -_extract_block(dcc_ref[i, LANES:, :].T, q)
        g_cr = gcr_s[...]
        g_ci = gci_s[...]
        g_nr = g_cr / den
        g_ni = g_ci / den
        g_den = -(g_cr * nr + g_ci * ni) / (den * den)
        g_ar = dar_ref[...] + g_nr * lr - g_ni * li
        g_ai = dai_ref[...] + g_nr * li + g_ni * lr
        g_lr = g_nr * (ar - 1.0) + g_ni * ai + g_den * 2.0 * lr
        g_li = g_nr * ai - g_ni * (ar - 1.0) + g_den * 2.0 * li
        g_mag = g_ar * cs + g_ai * sn
        g_th = (g_ai * cs - g_ar * sn) * mag
        g_lr = g_lr + g_mag * mag * dt
        g_li = g_li + g_th * dt
        g_dt = g_mag * mag * lr + g_th * li
        glr_ref[...] = jnp.where(lam < LAMBDA_RE_MAX, g_lr, 0.0)
        gli_ref[...] = g_li
        gl = g_dt * dt
        half = LANES // 2
        gdt_ref[:, 0:1] = jnp.sum(gl[:, :half], axis=1, keepdims=True)
        gdt_ref[:, 1:2] = jnp.sum(gl[:, half:], axis=1, keepdims=True)

    rows_shape = jax.ShapeDtypeStruct((nb * BLOCK_ROWS, HALF_LANES), F32)
    return pl.pallas_call(
        body, name="ssm_prep_bwd",
        out_shape=[jax.ShapeDtypeStruct((nb, LANES), F32), jax.ShapeDtypeStruct((nb, LANES), F32),
                   jax.ShapeDtypeStruct((nb, 2), F32), rows_shape, rows_shape, rows_shape, rows_shape],
        scratch_shapes=[pltpu.VMEM((nb, LANES), F32), pltpu.VMEM((nb, LANES), F32)],
        compiler_params=_cparams(),
    )(lam_re, lam_im, log_dt, bt_re, bt_im, dar, dai, dbb, dcc)


def _cmul(ar, ai, br, bi):
    return ar * br - ai * bi, ar * bi + ai * br


def _interleave_rows(src_ref, dst_ref):
    def step(j, carry):
        dst_ref[pl.ds(pl.multiple_of(j * 8, 8), 8), :] = src_ref[pl.ds(j, 8, stride=SCAN_CHUNK), :]
        return carry
    lax.fori_loop(0, SCAN_CHUNK, step, 0, unroll=4)


def _deinterleave_rows(src_ref, dst_ref):
    def step(j, carry):
        dst_ref[pl.ds(j, 8, stride=SCAN_CHUNK), :] = src_ref[pl.ds(pl.multiple_of(j * 8, 8), 8), :]
        return carry
    lax.fori_loop(0, SCAN_CHUNK, step, 0, unroll=4)


def _scan_inplace(re_ref, im_ref, a_re, a_im, reverse):
    nq = len(a_re)
    ch = SCAN_CHUNK
    ab_re = [jnp.broadcast_to(a, (8, LANES)) for a in a_re]
    ab_im = [jnp.broadcast_to(a, (8, LANES)) for a in a_im]

    def rows(j):
        jj = (ch - 1 - j) if reverse else j
        return pl.ds(pl.multiple_of(jj * 8, 8), 8)

    def sweep(init, store):
        def step(j, st):
            out = []
            r = rows(j)
            for qi in range(nq):
                xr, xi = st[2 * qi], st[2 * qi + 1]
                pr, pi = _cmul(ab_re[qi], ab_im[qi], xr, xi)
                xr = pr + re_ref[qi, r, :]
                xi = pi + im_ref[qi, r, :]
                if store:
                    re_ref[qi, r, :] = xr
                    im_ref[qi, r, :] = xi
                out += [xr, xi]
            return tuple(out)
        return lax.fori_loop(0, ch, step, tuple(init), unroll=2)

    zeros = [jnp.zeros((8, LANES), F32)] * (2 * nq)
    finals = sweep(zeros, store=False)

    row_id = lax.broadcasted_iota(jnp.int32, (8, LANES), 0)
    carries = []
    for qi in range(nq):
        pr, pi = ab_re[qi], ab_im[qi]
        for _ in range(8):
            pr, pi = _cmul(pr, pi, pr, pi)
        fr, fi = finals[2 * qi], finals[2 * qi + 1]
        sr = jnp.zeros((8, LANES), F32)
        si = jnp.zeros((8, LANES), F32)
        for _ in range(7):
            tr, ti = _cmul(pr, pi, sr, si)
            tr, ti = tr + fr, ti + fi
            if reverse:
                sr = jnp.where(row_id == 7, 0.0, pltpu.roll(tr, 7, axis=0))
                si = jnp.where(row_id == 7, 0.0, pltpu.roll(ti, 7, axis=0))
            else:
                sr = jnp.where(row_id == 0, 0.0, pltpu.roll(tr, 1, axis=0))
                si = jnp.where(row_id == 0, 0.0, pltpu.roll(ti, 1, axis=0))
        carries += [sr, si]
    sweep(carries, store=True)


SSM_Q = 4


def _ssm_fwd(u, are, aim, bb, cc, dskip, after=None):
    nq = SSM_Q
    deps = [] if after is None else [after]

    def body(u_ref, ar_ref, ai_ref, bb_ref, cc_ref, d_ref, *rest):
        y_ref, xr_ref, xi_ref, sre, sim, up, yp = rest[len(deps):]
        _interleave_rows(u_ref, up)
        uf = up[...]
        ub = uf.astype(BF16)
        yp[...] = d_ref[...] * uf
        for d in range(2):
            for qi in range(nq):
                sre[qi] = _dot(ub, bb_ref[d, qi, :, :LANES])
                sim[qi] = _dot(ub, bb_ref[d, qi, :, LANES:])
            _scan_inplace(sre, sim, [ar_ref[d, qi] for qi in range(nq)], [ai_ref[d, qi] for qi in range(nq)],
                          reverse=(d == 1))
            for qi in range(nq):
                xrb = sre[qi].astype(BF16)
                xib = sim[qi].astype(BF16)
                xr_ref[d, qi] = xrb
                xi_ref[d, qi] = xib
                yp[...] += _dot(xrb, cc_ref[d, qi, :LANES, :]) + _dot(xib, cc_ref[d, qi, LANES:, :])
        _deinterleave_rows(yp, y_ref)

    blk4 = lambda k: (0, k, 0, 0)
    return pl.pallas_call(
        body, name="ssm_fwd", grid=(SSM_WIDTH // LANES,),
        in_specs=[pl.BlockSpec((SEQ, LANES), lambda k: (0, k)),
                  pl.BlockSpec((2, nq, 1, LANES), blk4), pl.BlockSpec((2, nq, 1, LANES), blk4),
                  pl.BlockSpec((2, nq, LANES, 2 * LANES), blk4), pl.BlockSpec((2, nq, 2 * LANES, LANES), blk4),
                  pl.BlockSpec((1, LANES), lambda k: (0, k))] + [pl.BlockSpec(memory_space=pl.ANY)] * len(deps),
        out_specs=[pl.BlockSpec((SEQ, LANES), lambda k: (0, k)),
                   pl.BlockSpec((2, nq, SEQ, LANES), blk4), pl.BlockSpec((2, nq, SEQ, LANES), blk4)],
        out_shape=[jax.ShapeDtypeStruct((SEQ, SSM_WIDTH), F32),
                   jax.ShapeDtypeStruct((2, N_LANE_BLOCKS, SEQ, LANES), BF16),
                   jax.ShapeDtypeStruct((2, N_LANE_BLOCKS, SEQ, LANES), BF16)],
        scratch_shapes=[pltpu.VMEM((nq, SEQ, LANES), F32), pltpu.VMEM((nq, SEQ, LANES), F32),
                        pltpu.VMEM((SEQ, LANES), F32), pltpu.VMEM((SEQ, LANES), F32)],
        compiler_params=_cparams(("parallel",)),
    )(u, are, aim, bb, cc, dskip, *deps)


def _ssm_bwd(dy, u, xr, xi, are, aim, bb, cc, dskip, after=None):
    nq = SSM_Q
    body_rows = SEQ - 8
    deps = [] if after is None else [after]

    def body(dy_ref, u_ref, xr_ref, xi_ref, ar_ref, ai_ref, bb_ref, cc_ref, d_ref, *rest):
        du_ref, dd_ref, dcc_ref, dbb_ref, dar_ref, dai_ref, sre, sim, up, dyp, dup = rest[len(deps):]
        _interleave_rows(u_ref, up)
        _interleave_rows(dy_ref, dyp)
        dyf = dyp[...]
        uf = up[...]
        dyb = dyf.astype(BF16)
        ub = uf.astype(BF16)
        dd_ref[...] = jnp.sum(dyf * uf, axis=0, keepdims=True)
        dup[...] = d_ref[...] * dyf
        row8 = lax.broadcasted_iota(jnp.int32, (8, LANES), 0)
        for d in range(2):
            for qi in range(nq):
                dx = _dot_nt(dyb, cc_ref[d, qi])
                sre[qi] = dx[:, :LANES]
                sim[qi] = dx[:, LANES:]
                dcc_ref[d, qi] = _dot_tn(jnp.concatenate([xr_ref[d, qi], xi_ref[d, qi]], axis=1), dyb)
            _scan_inplace(sre, sim, [ar_ref[d, qi] for qi in range(nq)], [-ai_ref[d, qi] for qi in range(nq)],
                          reverse=(d == 0))
            for qi in range(nq):
                gr = sre[qi]
                gi = sim[qi]
                xrf = xr_ref[d, qi].astype(F32)
                xif = xi_ref[d, qi].astype(F32)
                if d == 0:
                    g_main_r, g_main_i = gr[8:], gi[8:]
                    x_main_r, x_main_i = xrf[:body_rows], xif[:body_rows]
                    g_edge_r, g_edge_i = gr[:8], gi[:8]
                    x_edge_r = jnp.where(row8 == 0, 0.0, pltpu.roll(xrf[body_rows:], 1, axis=0))
                    x_edge_i = jnp.where(row8 == 0, 0.0, pltpu.roll(xif[body_rows:], 1, axis=0))
                else:
                    g_main_r, g_main_i = gr[:body_rows], gi[:body_rows]
                    x_main_r, x_main_i = xrf[8:], xif[8:]
                    g_edge_r, g_edge_i = gr[body_rows:], gi[body_rows:]
                    x_edge_r = jnp.where(row8 == 7, 0.0, pltpu.roll(xrf[:8], 7, axis=0))
                    x_edge_i = jnp.where(row8 == 7, 0.0, pltpu.roll(xif[:8], 7, axis=0))
                dar_ref[d, qi] = (jnp.sum(g_main_r * x_main_r + g_main_i * x_main_i, axis=0, keepdims=True)
                                  + jnp.sum(g_edge_r * x_edge_r + g_edge_i * x_edge_i, axis=0, keepdims=True))
                dai_ref[d, qi] = (jnp.sum(g_main_i * x_main_r - g_main_r * x_main_i, axis=0, keepdims=True)
                                  + jnp.sum(g_edge_i * x_edge_r - g_edge_r * x_edge_i, axis=0, keepdims=True))
                gb = jnp.concatenate([gr, gi], axis=1).astype(BF16)
                dup[...] += _dot_nt(gb, bb_ref[d, qi])
                dbb_ref[d, qi] = _dot_tn(ub, gb)
        _deinterleave_rows(dup, du_ref)

    blk4 = lambda k: (0, k, 0, 0)
    col = lambda k: (0, k)
    bb_spec = pl.BlockSpec((2, nq, LANES, 2 * LANES), blk4)
    cc_spec = pl.BlockSpec((2, nq, 2 * LANES, LANES), blk4)
    a_spec = pl.BlockSpec((2, nq, 1, LANES), blk4)
    x_spec = pl.BlockSpec((2, nq, SEQ, LANES), blk4)
    a_shape = jax.ShapeDtypeStruct((2, N_LANE_BLOCKS, 1, LANES), F32)
    return pl.pallas_call(
        body, name="ssm_bwd", grid=(SSM_WIDTH // LANES,),
        in_specs=[pl.BlockSpec((SEQ, LANES), col), pl.BlockSpec((SEQ, LANES), col), x_spec, x_spec,
                  a_spec, a_spec, bb_spec, cc_spec, pl.BlockSpec((1, LANES), col)]
        + [pl.BlockSpec(memory_space=pl.ANY)] * len(deps),
        out_specs=[pl.BlockSpec((SEQ, LANES), col), pl.BlockSpec((1, LANES), col),
                   cc_spec, bb_spec, a_spec, a_spec],
        out_shape=[jax.ShapeDtypeStruct((SEQ, SSM_WIDTH), F32), jax.ShapeDtypeStruct((1, SSM_WIDTH), F32),
                   jax.ShapeDtypeStruct((2, N_LANE_BLOCKS, 2 * LANES, LANES), F32),
                   jax.ShapeDtypeStruct((2, N_LANE_BLOCKS, LANES, 2 * LANES), F32), a_shape, a_shape],
        scratch_shapes=[pltpu.VMEM((nq, SEQ, LANES), F32), pltpu.VMEM((nq, SEQ, LANES), F32),
                        pltpu.VMEM((SEQ, LANES), F32), pltpu.VMEM((SEQ, LANES), F32), pltpu.VMEM((SEQ, LANES), F32)],
        compiler_params=_cparams(("parallel",)),
    )(dy, u, xr, xi, are, aim, bb, cc, dskip, *deps)


GELU_C = 0.7978845608028654
GELU_K = 0.044715


def _gelu(y):
    return 0.5 * y * (1.0 + jnp.tanh(GELU_C * (y + GELU_K * y * y * y)))


def _gelu_grad(y):
    t = jnp.tanh(GELU_C * (y + GELU_K * y * y * y))
    return 0.5 * (1.0 + t) + 0.5 * y * (1.0 - t * t) * GELU_C * (1.0 + 3.0 * GELU_K * y * y)


def _mixout_fwd(o, y, glu_w, glu_b, gan, gsn, wout, x1):
    tm = MIX_TM

    def body(o_ref, y_ref, gw_ref, gb_ref, gan_ref, gsn_ref, w_ref, x1_ref, x2_ref, mx_ref):
        yg = _gelu(y_ref[...])
        z = _dot(yg.astype(BF16), gw_ref[...]) + gb_ref[...]
        so = yg * _sigmoid(z)
        na = _rms_fwd(o_ref[...], gan_ref[...])
        ns = _rms_fwd(so, gsn_ref[...])
        mixed = jnp.concatenate([na, ns], axis=-1).astype(BF16)
        mx_ref[...] = mixed
        x2_ref[...] = x1_ref[...] + _dot(mixed, w_ref[...])

    row = lambda i: (i, 0)
    const = lambda i: (0, 0)
    return pl.pallas_call(
        body, name="mixout_fwd", grid=(SEQ // tm,),
        in_specs=[pl.BlockSpec((tm, ATTN_WIDTH), row), pl.BlockSpec((tm, SSM_WIDTH), row),
                  pl.BlockSpec((SSM_WIDTH, SSM_WIDTH), const), pl.BlockSpec((1, SSM_WIDTH), const),
                  pl.BlockSpec((1, ATTN_WIDTH), const), pl.BlockSpec((1, SSM_WIDTH), const),
                  pl.BlockSpec((D_MODEL, D_MODEL), const), pl.BlockSpec((tm, D_MODEL), row)],
        out_specs=[pl.BlockSpec((tm, D_MODEL), row), pl.BlockSpec((tm, D_MODEL), row)],
        out_shape=[jax.ShapeDtypeStruct((SEQ, D_MODEL), F32), jax.ShapeDtypeStruct((SEQ, D_MODEL), BF16)],
        compiler_params=_cparams(("parallel",)),
    )(o, y, glu_w, glu_b, gan, gsn, wout, x1)


def _mixout_bwd(dx2, o, y, glu_w, glu_b, gan, gsn, wout):
    tm = MIX_TM

    def body(dx2_ref, o_ref, y_ref, gw_ref, gb_ref, gan_ref, gsn_ref, w_ref,
             do_ref, dy_ref, dz_ref, yg_ref, dxb_ref, dgan_ref, dgsn_ref, dgb_ref):
        i = pl.program_id(0)
        dxb = dx2_ref[...].astype(BF16)
        dxb_ref[...] = dxb
        dmixed = _dot_nt(dxb, w_ref[...])
        do, dgan = _rms_bwd(dmixed[:, :ATTN_WIDTH], o_ref[...], gan_ref[...])
        do_ref[...] = do.T
        yv = y_ref[...]
        yg = _gelu(yv)
        ygb = yg.astype(BF16)
        yg_ref[...] = ygb
        sg = _sigmoid(_dot(ygb, gw_ref[...]) + gb_ref[...])
        dso, dgsn = _rms_bwd(dmixed[:, ATTN_WIDTH:], yg * sg, gsn_ref[...])
        dz = dso * yg * sg * (1.0 - sg)
        dzb = dz.astype(BF16)
        dz_ref[...] = dzb
        dyg = dso * sg + _dot_nt(dzb, gw_ref[...])
        dy_ref[...] = dyg * _gelu_grad(yv)
        dgb = jnp.sum(dz, axis=0, keepdims=True)

        @pl.when(i == 0)
        def _():
            dgan_ref[...] = dgan
            dgsn_ref[...] = dgsn
            dgb_ref[...] = dgb

        @pl.when(i != 0)
        def _():
            dgan_ref[...] += dgan
            dgsn_ref[...] += dgsn
            dgb_ref[...] += dgb

    row = lambda i: (i, 0)
    const = lambda i: (0, 0)
    return pl.pallas_call(
        body, name="mixout_bwd", grid=(SEQ // tm,),
        in_specs=[pl.BlockSpec((tm, D_MODEL), row), pl.BlockSpec((tm, ATTN_WIDTH), row),
                  pl.BlockSpec((tm, SSM_WIDTH), row),
                  pl.BlockSpec((SSM_WIDTH, SSM_WIDTH), const), pl.BlockSpec((1, SSM_WIDTH), const),
                  pl.BlockSpec((1, ATTN_WIDTH), const), pl.BlockSpec((1, SSM_WIDTH), const),
                  pl.BlockSpec((D_MODEL, D_MODEL), const)],
        out_specs=[pl.BlockSpec((ATTN_WIDTH, tm), lambda i: (0, i)), pl.BlockSpec((tm, SSM_WIDTH), row),
                   pl.BlockSpec((tm, SSM_WIDTH), row), pl.BlockSpec((tm, SSM_WIDTH), row),
                   pl.BlockSpec((tm, D_MODEL), row),
                   pl.BlockSpec((1, ATTN_WIDTH), const), pl.BlockSpec((1, SSM_WIDTH), const),
                   pl.BlockSpec((1, SSM_WIDTH), const)],
        out_shape=[jax.ShapeDtypeStruct((ATTN_WIDTH, SEQ), F32), jax.ShapeDtypeStruct((SEQ, SSM_WIDTH), F32),
                   jax.ShapeDtypeStruct((SEQ, SSM_WIDTH), BF16), jax.ShapeDtypeStruct((SEQ, SSM_WIDTH), BF16),
                   jax.ShapeDtypeStruct((SEQ, D_MODEL), BF16),
                   jax.ShapeDtypeStruct((1, ATTN_WIDTH), F32), jax.ShapeDtypeStruct((1, SSM_WIDTH), F32),
                   jax.ShapeDtypeStruct((1, SSM_WIDTH), F32)],
        compiler_params=_cparams(("arbitrary",)),
    )(dx2, o, y, glu_w, glu_b, gan, gsn, wout)


def _local_step(x, target, w, p, late_weights, early_grads, after=None, midway=None):
    x1, h1, a1, b1 = _ffn_fwd(x, p["norm_ffn1"], w["wgt1"], w["wut1"], w["wd1"], "ffn1_fwd", after=after)
    h2, q, k, v, u = _mixin_fwd(x1, p["norm_mix"], w["wint"])

    lam_re = p["ssm_lambda_re"].reshape(2 * N_LANE_BLOCKS, LANES)
    lam_im = p["ssm_lambda_im"].reshape(2 * N_LANE_BLOCKS, LANES)
    log_dt = jnp.repeat(p["ssm_log_dt"].reshape(2, 32), 64, axis=-1).reshape(2 * N_LANE_BLOCKS, LANES)
    a_re, a_im, bb, cc = _ssm_prep(lam_re, lam_im, log_dt, p["ssm_b_re"], p["ssm_b_im"],
                                   p["ssm_c_re"], p["ssm_c_im"])
    shape_a = (2, N_LANE_BLOCKS, 1, LANES)
    a_re4, a_im4 = a_re.reshape(shape_a), a_im.reshape(shape_a)
    bb4 = bb.reshape(2, N_LANE_BLOCKS, LANES, 2 * LANES)
    cc4 = cc.reshape(2, N_LANE_BLOCKS, 2 * LANES, LANES)
    dskip = p["ssm_d"].T.reshape(1, SSM_WIDTH)
    y, xr, xi = _ssm_fwd(u, a_re4, a_im4, bb4, cc4, dskip)
    o = _attn_fwd(q, k, v, p["attn_sinks"], after=None if midway is None else midway(y))

    w2 = late_weights(o)
    x2, mixed = _mixout_fwd(o, y, w2["glu"], p["ssm_glu_b"], p["attn_out_norm"], p["ssm_out_norm"], w2["wout"], x1)
    dx3, h3, a3, b3, loss, d_final = _ffn_fwd(x2, p["norm_ffn2"], w2["wgt2"], w2["wut2"], w2["wd2"], "ffn2_fwd",
                                              head=(p["final_norm"], target))
    dx2, da3, db3, s3, df3, d_n2 = _ffn_bwd_act(dx3, x2, p["norm_ffn2"], a3, b3, w2["wgt2"], w2["wut2"], w2["wd2"],
                                                "ffn2_bwd_act")
    g_wgt2, g_wut2, g_wd2 = _mm_tn([(da3, h3), (db3, h3), (s3, df3)], "ffn2_bwd_w")

    do, dy, dz, ygb, dx2b, d_gan, d_gsn, d_glub = _mixout_bwd(
        dx2, o, y, w2["glu"], p["ssm_glu_b"], p["attn_out_norm"], p["ssm_out_norm"], w2["wout"])
    (g_wout,) = _mm_tn([(mixed, dx2b)], "wout_bwd_w")
    (g_glu,) = _mm_tn([(ygb, dz)], "glu_bwd_w")
    sent = early_grads(dict(glu=g_glu, wout=g_wout, wgt2=g_wgt2, wut2=g_wut2, wd2=g_wd2))

    du, d_dskip, dcc, dbb, dar, dai = _ssm_bwd(dy, u, xr, xi, a_re4, a_im4, bb4, cc4, dskip, after=sent)
    nb = 2 * N_LANE_BLOCKS
    g_lre, g_lim, g_ldt, g_btr, g_bti, g_cre, g_cim = _ssm_prep_bwd(
        lam_re, lam_im, log_dt, p["ssm_b_re"], p["ssm_b_im"], dar.reshape(nb, LANES), dai.reshape(nb, LANES),
        dbb.reshape(nb, LANES, 2 * LANES), dcc.reshape(nb, 2 * LANES, LANES))

    dq, dk, dv, d_sinks = _attn_bwd(q, k, v, p["attn_sinks"], do)
    dx1, dproj, d_nmix = _mixin_bwd(dq, dk, dv, du, w["wint"], x1, p["norm_mix"], dx2)
    (g_wint,) = _mm_tn([(dproj, h2)], "win_bwd_w")

    dx0, da1, db1, s1, df1, d_n1 = _ffn_bwd_act(dx1, x, p["norm_ffn1"], a1, b1, w["wgt1"], w["wut1"], w["wd1"],
                                                "ffn1_bwd_act")
    g_wgt1, g_wut1, g_wd1 = _mm_tn([(da1, h1), (db1, h1), (s1, df1)], "ffn1_bwd_w")

    big = dict(wgt1=g_wgt1, wut1=g_wut1, wd1=g_wd1, wint=g_wint)
    small = dict(
        norm_ffn1=d_n1, norm_mix=d_nmix, attn_sinks=d_sinks,
        ssm_lambda_re=g_lre.reshape(64, 64), ssm_lambda_im=g_lim.reshape(64, 64),
        ssm_log_dt=g_ldt.reshape(2, 32), ssm_b_re=g_btr, ssm_b_im=g_bti, ssm_c_re=g_cre, ssm_c_im=g_cim,
        ssm_d=d_dskip.reshape(32, 16).T, ssm_glu_b=d_glub, attn_out_norm=d_gan, ssm_out_norm=d_gsn,
        norm_ffn2=d_n2, final_norm=d_final, loss=loss)
    return loss, dx0, big, small


BIG = dict(
    wgt1=("ffn1_w_gate", 352, 1024, True), wut1=("ffn1_w_up", 352, 1024, True), wd1=("ffn1_w_down", 352, 1024, False),
    wint=("w_in", 160, 1024, True), glu=("ssm_glu_w", 64, 512, False), wout=("w_out", 128, 1024, False),
    wgt2=("ffn2_w_gate", 352, 1024, True), wut2=("ffn2_w_up", 352, 1024, True), wd2=("ffn2_w_down", 352, 1024, False))

SMALL = dict(
    norm_ffn1=(1, 1024), norm_mix=(1, 1024), attn_sinks=(1, 8), ssm_lambda_re=(64, 64), ssm_lambda_im=(64, 64),
    ssm_log_dt=(2, 32), ssm_b_re=(1024, 64), ssm_b_im=(1024, 64), ssm_c_re=(1024, 64), ssm_c_im=(1024, 64),
    ssm_d=(16, 32), ssm_glu_b=(1, 512), attn_out_norm=(1, 512), ssm_out_norm=(1, 512), norm_ffn2=(1, 1024),
    final_norm=(1, 1024), loss=(1, 128))
SMALL_TRANSPOSED = ("ssm_b_re", "ssm_b_im", "ssm_d")
SMALL_PARAMS = tuple(n for n in SMALL if n != "loss")

SMALL_PAIRS = (("ssm_lambda_re", "ssm_lambda_im"), ("ssm_c_re", "ssm_c_im"), ("ssm_b_re", "ssm_b_im"))
SMALL_VECS = ("norm_ffn1", "norm_mix", "norm_ffn2", "final_norm", "ssm_glu_b", "attn_out_norm", "ssm_out_norm")
SMALL_TILES = ("ssm_log_dt", "attn_sinks", "ssm_d", "loss")


def _small_offsets():
    off, table = 0, {}
    for re, im in SMALL_PAIRS:
        table[re] = table[im] = off
        off += SMALL[re][0]
    for n in SMALL_VECS:
        table[n] = off
        off += SMALL[n][1] // LANES
    for n in SMALL_TILES:
        off = -(-off // 8) * 8
        table[n] = off
        off += SMALL[n][0]
    return table, off


SMALL_OFFSET, SMALL_USED_ROWS = _small_offsets()
SMALL_ROWS = -(-SMALL_USED_ROWS // (8 * N_DEV)) * 8 * N_DEV


def _cast_shards(shards):
    names = list(BIG)

    def body(*refs):
        ins, outs = refs[:len(names)], refs[len(names):]
        for idx in range(len(names)):
            outs[idx][...] = ins[idx][...].astype(BF16)

    return pl.pallas_call(
        body, name="cast_shards",
        out_shape=[jax.ShapeDtypeStruct((BIG[n][1], BIG[n][2]), BF16) for n in names],
        compiler_params=_cparams(),
    )(*[shards[n] for n in names])


def _peer(x, y, c, r):
    px = 1 - x if r & 4 else x
    py = 1 - y if r & 2 else y
    pc = 1 - c if r & 1 else c
    return px, py, pc


FIRST_GROUP = ("wgt1", "wut1", "wd1", "wint")
LATE_GROUP = ("glu", "wout", "wgt2", "wut2", "wd2")
N_PEERS = N_DEV - 1
ANY_SPEC = pl.BlockSpec(memory_space=pl.ANY)
HBM_SPEC = pl.BlockSpec(memory_space=pltpu.HBM)
SEM_SPEC = pl.BlockSpec(memory_space=pltpu.SEMAPHORE)
DATAFLOW_EFFECT = pltpu.SideEffectType.DATAFLOW_SIDE_EFFECTING


def _mesh_pos():
    x, y, c = lax.axis_index("x"), lax.axis_index("y"), lax.axis_index("c")
    return x, y, c, 4 * x + 2 * y + c


def _gather_first(first, late):
    nf, nl = len(first), len(late)

    def body(*refs):
        f_in, l_in = refs[:nf], refs[nf:nf + nl]
        f_out, l_out = refs[nf + nl:2 * nf + nl], refs[2 * nf + nl:2 * (nf + nl)]
        send_sems, recv_sems, local_sems = refs[2 * (nf + nl):]
        x, y, c, me = _mesh_pos()
        sibling = (x, y, 1 - c)
        chips = [(x, 1 - y), (1 - x, y), (1 - x, 1 - y)]

        def idx(px, py, pc):
            return 4 * px + 2 * py + pc

        def copy(k, s, block, to, src=None):
            slot = f_out[k].at[block]
            return pltpu.make_async_remote_copy(
                src_ref=slot if src is None else src, dst_ref=slot, send_sem=send_sems.at[k, s],
                recv_sem=recv_sems.at[k, s], device_id=to, device_id_type=MESH_ID)

        local = []
        for k in range(nf + nl):
            src, dst = (f_in[k], f_out[k]) if k < nf else (l_in[k - nf], l_out[k - nf])
            mine = pltpu.make_async_copy(src, dst.at[me], local_sems.at[k])
            mine.start()
            local.append(mine)
        sends = []
        for j, chip in enumerate(chips):
            for k in range(nf):
                sends.append(copy(k, 1 + j, me, (*chip, c), src=f_in[k]))
                sends[-1].start()
        for k in range(nf):
            sends.append(copy(k, 0, me, sibling, src=f_in[k]))
            sends[-1].start()
        for j, chip in enumerate(chips):
            for k in range(nf):
                copy(k, 1 + j, idx(*chip, c), (*chip, c)).wait_recv()
                sends.append(copy(k, 4 + j, idx(*chip, c), sibling))
                sends[-1].start()
        for k in range(nf):
            copy(k, 0, idx(*sibling), sibling).wait_recv()
        for j, chip in enumerate(chips):
            for k in range(nf):
                copy(k, 4 + j, idx(*chip, 1 - c), sibling).wait_recv()
        for cp in sends:
            cp.wait_send()
        for cp in local:
            cp.wait()

    return pl.pallas_call(
        body, name="gather_first",
        in_specs=[ANY_SPEC] * (nf + nl), out_specs=[ANY_SPEC] * (nf + nl),
        out_shape=[jax.ShapeDtypeStruct((N_DEV,) + s.shape, s.dtype) for s in list(first) + list(late)],
        scratch_shapes=[pltpu.SemaphoreType.DMA((nf, N_PEERS)), pltpu.SemaphoreType.DMA((nf, N_PEERS)),
                        pltpu.SemaphoreType.DMA((nf + nl,))],
        compiler_params=pltpu.CompilerParams(has_side_effects=True),
    )(*first, *late)


def _split_copy(src_refs, land_refs, send_sems, recv_sems, k, r, pos, scatter, receiving):
    x, y, c, me = pos
    px, py, pc = _peer(x, y, c, r)
    peer_idx = 4 * px + 2 * py + pc
    if scatter:
        src, dst = src_refs[k].at[peer_idx], land_refs[k].at[r - 1]
    else:
        src, dst = src_refs[k], land_refs[k].at[peer_idx if receiving else me]
    return pltpu.make_async_remote_copy(
        src_ref=src, dst_ref=dst, send_sem=send_sems.at[k * N_PEERS + r - 1],
        recv_sem=recv_sems.at[k * N_PEERS + r - 1], device_id=(px, py, pc), device_id_type=MESH_ID)


def _split_start(name, srcs, lands, scatter):
    n = len(srcs)

    def body(*refs):
        src_refs, land_refs = refs[:n], refs[n:2 * n]
        send_sems, recv_sems = refs[2 * n], refs[2 * n + 1]
        token = refs[-1]
        pos = _mesh_pos()
        for k in range(n):
            for r in range(1, N_DEV):
                _split_copy(src_refs, land_refs, send_sems, recv_sems, k, r, pos, scatter, False).start()
        token[...] = jnp.zeros_like(token)

    thru = [pltpu.HBM(a.shape, a.dtype) for a in list(srcs) + list(lands)]
    outs = pl.pallas_call(
        body, name=name,
        in_specs=[HBM_SPEC] * (2 * n),
        out_specs=[SEM_SPEC, SEM_SPEC] + [HBM_SPEC] * (2 * n) + [pl.BlockSpec(memory_space=pltpu.VMEM)],
        out_shape=[pltpu.SemaphoreType.DMA((n * N_PEERS,)), pltpu.SemaphoreType.DMA((n * N_PEERS,))] + thru
        + [jax.ShapeDtypeStruct((8, LANES), F32)],
        input_output_aliases={i: 2 + i for i in range(2 * n)},
        compiler_params=pltpu.CompilerParams(has_side_effects=DATAFLOW_EFFECT),
    )(*[pltpu.with_memory_space_constraint(a, pltpu.HBM) for a in list(srcs) + list(lands)])
    return outs[0], outs[1], outs[2:2 + n], outs[2 + n:2 + 2 * n], outs[-1]


def _split_wait(name, send_sems, recv_sems, srcs, lands, scatter, after):
    n = len(srcs)

    def body(*refs):
        src_refs, land_refs = refs[:n], refs[n:2 * n]
        send, recv = refs[2 * n], refs[2 * n + 1]
        pos = _mesh_pos()
        for k in range(n):
            for r in range(1, N_DEV):
                cp = _split_copy(src_refs, land_refs, send, recv, k, r, pos, scatter, True)
                cp.wait_send()
                cp.wait_recv()

    thru = [pltpu.HBM(a.shape, a.dtype) for a in list(srcs) + list(lands)]
    outs = pl.pallas_call(
        body, name=name,
        in_specs=[HBM_SPEC] * (2 * n) + [SEM_SPEC, SEM_SPEC, ANY_SPEC],
        out_specs=[HBM_SPEC] * (2 * n), out_shape=thru,
        input_output_aliases={i: i for i in range(2 * n)},
        compiler_params=pltpu.CompilerParams(has_side_effects=DATAFLOW_EFFECT),
    )(*srcs, *lands, send_sems, recv_sems, after)
    return outs[:n], outs[n:]


def _late_copy(passing, src_refs, land_refs, send_sems, recv_sems, k, s, pos, receiving):
    x, y, c, me = pos
    chips = [(x, 1 - y), (1 - x, y), (1 - x, 1 - y)]
    sibling = (x, y, 1 - c)

    def idx(dev):
        return 4 * dev[0] + 2 * dev[1] + dev[2]

    if passing:
        to = sibling
        block = idx((*chips[s], 1 - c)) if receiving else idx((*chips[s], c))
        src = dst = land_refs[k].at[block]
        sem = k * 3 + s
    else:
        to = sibling if s == 0 else (*chips[s - 1], c)
        src, dst = src_refs[k], land_refs[k].at[idx(to) if receiving else me]
        sem = k * 4 + s
    return pltpu.make_async_remote_copy(src_ref=src, dst_ref=dst, send_sem=send_sems.at[sem],
                                        recv_sem=recv_sems.at[sem], device_id=to, device_id_type=MESH_ID)


def _late_gather_call(name, stage, srcs, lands, sems, after=None):
    n = len(srcs)
    n_sem_in = len(sems)
    has_after = after is not None

    def body(*refs):
        src_refs, land_refs = refs[:n], refs[n:2 * n]
        sem_in = refs[2 * n:2 * n + n_sem_in]
        outs = refs[2 * n + n_sem_in + (1 if has_after else 0):]
        pos = _mesh_pos()
        if stage == 0:
            own_send, own_recv = outs[0], outs[1]
            for s in (1, 2, 3, 0):
                for k in range(n):
                    _late_copy(False, src_refs, land_refs, own_send, own_recv, k, s, pos, False).start()
            outs[-1][...] = jnp.zeros_like(outs[-1])
        elif stage == 1:
            own_recv = sem_in[1]
            pass_send, pass_recv = outs[0], outs[1]
            for s in range(3):
                for k in range(n):
                    _late_copy(False, src_refs, land_refs, sem_in[0], own_recv, k, s + 1, pos, True).wait_recv()
                    _late_copy(True, src_refs, land_refs, pass_send, pass_recv, k, s, pos, False).start()
            outs[-1][...] = jnp.zeros_like(outs[-1])
        else:
            own_send, own_recv, pass_send, pass_recv = sem_in
            for k in range(n):
                _late_copy(False, src_refs, land_refs, own_send, own_recv, k, 0, pos, True).wait_recv()
                for s in range(4):
                    _late_copy(False, src_refs, land_refs, own_send, own_recv, k, s, pos, False).wait_send()
                for s in range(3):
                    cp = _late_copy(True, src_refs, land_refs, pass_send, pass_recv, k, s, pos, True)
                    cp.wait_recv()
                    cp.wait_send()

    thru = [pltpu.HBM(a.shape, a.dtype) for a in list(srcs) + list(lands)]
    new_sems = [[pltpu.SemaphoreType.DMA((n * 4,))] * 2, [pltpu.SemaphoreType.DMA((n * 3,))] * 2, []][stage]
    extra = [] if stage == 2 else [jax.ShapeDtypeStruct((8, LANES), F32)]
    outs = pl.pallas_call(
        body, name=name,
        in_specs=[HBM_SPEC] * (2 * n) + [SEM_SPEC] * n_sem_in + [ANY_SPEC] * has_after,
        out_specs=[SEM_SPEC] * len(new_sems) + [HBM_SPEC] * (2 * n) + [pl.BlockSpec(memory_space=pltpu.VMEM)] * len(extra),
        out_shape=new_sems + thru + extra,
        input_output_aliases={i: len(new_sems) + i for i in range(2 * n)},
        compiler_params=pltpu.CompilerParams(has_side_effects=DATAFLOW_EFFECT),
    )(*[pltpu.with_memory_space_constraint(a, pltpu.HBM) for a in list(srcs) + list(lands)], *sems,
      *([after] if has_after else []))
    ns = len(new_sems)
    return list(outs[:ns]), outs[ns:ns + n], outs[ns + n:ns + 2 * n], (outs[-1] if extra else None)


N_SEND_SLOTS = 3


def _exchange_last(grads, small_packed):
    ng = len(grads)
    ch = SMALL_ROWS // N_DEV
    max_rows = max(g.shape[1] for g in grads)
    cols = grads[0].shape[2]

    def body(*refs):
        g_in, s_in = refs[:ng], refs[ng]
        outs = refs[ng + 1:]
        own_out, land, stage = outs[:ng], outs[ng:2 * ng], outs[2 * ng:3 * ng]
        s_red, s_stage = outs[3 * ng], outs[3 * ng + 1]
        (va, vb, vo, vs, sm_in, sm_out, d2d_send, d2d_recv, ici_send, ici_recv, s1_send, s1_recv, s2_send, s2_recv,
         local_sems) = outs[3 * ng + 2:]
        x, y, c, me = _mesh_pos()
        sibling = (x, y, 1 - c)
        chips = [(x, y), (x, 1 - y), (1 - x, y), (1 - x, 1 - y)]

        def idx(chip, core):
            return 4 * chip[0] + 2 * chip[1] + core

        def d2d(k, j):
            return pltpu.make_async_remote_copy(
                src_ref=g_in[k].at[idx(chips[j], 1 - c)], dst_ref=stage[k].at[j], send_sem=d2d_send.at[k, j],
                recv_sem=d2d_recv.at[k, j], device_id=sibling, device_id_type=MESH_ID)

        def ici(k, j, slot):
            rows = g_in[k].shape[1]
            return pltpu.make_async_remote_copy(
                src_ref=vo.at[slot, pl.ds(0, rows)], dst_ref=land[k].at[j - 1], send_sem=ici_send.at[k, j - 1],
                recv_sem=ici_recv.at[k, j - 1], device_id=(*chips[j], c), device_id_type=MESH_ID)

        def small_scatter(r):
            px, py, pc = _peer(x, y, c, r)
            return pltpu.make_async_remote_copy(
                src_ref=s_in.at[pl.ds(pl.multiple_of((4 * px + 2 * py + pc) * ch, 8), ch)], dst_ref=s_stage.at[me],
                send_sem=s1_send.at[r - 1], recv_sem=s1_recv.at[r - 1], device_id=(px, py, pc), device_id_type=MESH_ID)

        def small_gather(r):
            return pltpu.make_async_remote_copy(
                src_ref=sm_out, dst_ref=s_red.at[me], send_sem=s2_send.at[r - 1], recv_sem=s2_recv.at[r - 1],
                device_id=_peer(x, y, c, r), device_id_type=MESH_ID)

        for r in range(1, N_DEV):
            small_scatter(r).start()
        mine = pltpu.make_async_copy(s_in.at[pl.ds(pl.multiple_of(me * ch, 8), ch)], s_stage.at[me], local_sems.at[0])
        mine.start()
        pairs = [(k, j) for k in range(ng) for j in (1, 2, 3)] + [(k, 0) for k in range(ng)]
        for k, j in pairs:
            d2d(k, j).start()

        def reduce_small():
            for r in range(1, N_DEV):
                small_scatter(r).wait_recv()
            mine.wait()
            load = pltpu.make_async_copy(s_stage, sm_in, local_sems.at[1])
            load.start()
            load.wait()
            total = sm_in[0]
            for i in range(1, N_DEV):
                total = total + sm_in[i]
            sm_out[...] = total
            for r in range(1, N_DEV):
                small_gather(r).start()
            keep = pltpu.make_async_copy(sm_out, s_red.at[me], local_sems.at[2])
            keep.start()
            return keep

        in_flight = {}
        for i, (k, j) in enumerate(pairs):
            if i == N_SEND_SLOTS:
                keep = reduce_small()
            slot = i % N_SEND_SLOTS
            rows = g_in[k].shape[1]
            if slot in in_flight:
                in_flight.pop(slot).wait_send()
            d2d(k, j).wait_recv()
            la = pltpu.make_async_copy(g_in[k].at[idx(chips[j], c)], va.at[pl.ds(0, rows)], local_sems.at[3])
            lb = pltpu.make_async_copy(stage[k].at[j], vb.at[pl.ds(0, rows)], local_sems.at[4])
            la.start()
            lb.start()
            la.wait()
            lb.wait()
            total = va[pl.ds(0, rows)].astype(F32) + vb[pl.ds(0, rows)].astype(F32)
            if j == 0:
                vs[pl.ds(0, rows)] = total
                st = pltpu.make_async_copy(vs.at[pl.ds(0, rows)], own_out[k], local_sems.at[5])
                st.start()
                st.wait()
            else:
                vo[slot, pl.ds(0, rows)] = total.astype(BF16)
                cp = ici(k, j, slot)
                cp.start()
                in_flight[slot] = cp
        for cp in in_flight.values():
            cp.wait_send()

        for j in (1, 2, 3, 0):
            for k in range(ng):
                d2d(k, j).wait_send()
        for j in (1, 2, 3):
            for k in range(ng):
                ici(k, j, 0).wait_recv()
        for r in range(1, N_DEV):
            small_scatter(r).wait_send()
            small_gather(r).wait_send()
            small_gather(r).wait_recv()
        keep.wait()

    out_shape = [jax.ShapeDtypeStruct(g.shape[1:], F32) for g in grads]
    out_shape += [jax.ShapeDtypeStruct((3,) + g.shape[1:], BF16) for g in grads]
    out_shape += [jax.ShapeDtypeStruct((4,) + g.shape[1:], BF16) for g in grads]
    out_shape += [jax.ShapeDtypeStruct((N_DEV, ch, LANES), F32), jax.ShapeDtypeStruct((N_DEV, ch, LANES), F32)]
    outs = pl.pallas_call(
        body, name="exchange_last",
        in_specs=[ANY_SPEC] * (ng + 1), out_specs=[ANY_SPEC] * len(out_shape), out_shape=out_shape,
        scratch_shapes=[pltpu.VMEM((max_rows, cols), BF16), pltpu.VMEM((max_rows, cols), BF16),
                        pltpu.VMEM((N_SEND_SLOTS, max_rows, cols), BF16), pltpu.VMEM((max_rows, cols), F32),
                        pltpu.VMEM((N_DEV, ch, LANES), F32), pltpu.VMEM((ch, LANES), F32),
                        pltpu.SemaphoreType.DMA((ng, 4)), pltpu.SemaphoreType.DMA((ng, 4)),
                        pltpu.SemaphoreType.DMA((ng, 3)), pltpu.SemaphoreType.DMA((ng, 3)),
                        pltpu.SemaphoreType.DMA((N_PEERS,)), pltpu.SemaphoreType.DMA((N_PEERS,)),
                        pltpu.SemaphoreType.DMA((N_PEERS,)), pltpu.SemaphoreType.DMA((N_PEERS,)),
                        pltpu.SemaphoreType.DMA((6,))],
        compiler_params=pltpu.CompilerParams(has_side_effects=True, vmem_limit_bytes=VMEM_LIMIT),
    )(*grads, small_packed)
    return outs[:ng], outs[ng:2 * ng], outs[3 * ng].reshape(SMALL_ROWS, LANES)


def _adamw_math(w, g, m, v):
    m2 = ADAM_B1 * m + (1.0 - ADAM_B1) * g
    v2 = ADAM_B2 * v + (1.0 - ADAM_B2) * (g * g)
    m_hat = m2 / (1.0 - ADAM_B1 ** ADAM_STEP)
    v_hat = v2 / (1.0 - ADAM_B2 ** ADAM_STEP)
    delta = -ADAM_LR * (m_hat / (jnp.sqrt(v_hat) + ADAM_EPS) + ADAM_WD * w)
    return delta, m2, v2


ADAM_ROW_TILES = 2


def _adamw_big(own, parts, w, m, v, name):
    shape = w.shape
    own_is_blocks = own.ndim == 3
    tr = shape[0] // ADAM_ROW_TILES
    n_parts = parts.shape[0]

    def body(own_ref, p_ref, w_ref, m_ref, v_ref, g_ref, d_ref, m2_ref, v2_ref, own_s, sem):
        rows = pl.ds(pl.multiple_of(pl.program_id(0) * tr, 16), tr)
        if own_is_blocks:
            cp = pltpu.make_async_copy(own_ref.at[_mesh_pos()[3], rows], own_s, sem)
        else:
            cp = pltpu.make_async_copy(own_ref.at[rows], own_s, sem)
        cp.start()
        cp.wait()
        g = own_s[...].astype(F32)
        for i in range(n_parts):
            g = g + p_ref[i].astype(F32)
        delta, m2, v2 = _adamw_math(w_ref[...], g, m_ref[...], v_ref[...])
        g_ref[...] = g
        d_ref[...] = delta
        m2_ref[...] = m2
        v2_ref[...] = v2

    tile = pl.BlockSpec((tr, shape[1]), lambda i: (i, 0))
    return pl.pallas_call(
        body, name=name, grid=(ADAM_ROW_TILES,),
        in_specs=[ANY_SPEC, pl.BlockSpec((n_parts, tr, shape[1]), lambda i: (0, i, 0)), tile, tile, tile],
        out_specs=[tile] * 4, out_shape=[jax.ShapeDtypeStruct(shape, F32)] * 4,
        scratch_shapes=[pltpu.VMEM((tr, shape[1]), own.dtype), pltpu.SemaphoreType.DMA(())],
        compiler_params=_cparams(("arbitrary",)),
    )(own, parts, w, m, v)


def _pack_small(grads):
    names = list(SMALL)

    def body(*refs):
        ins, out = dict(zip(names, refs[:-1])), refs[-1]
        out[...] = jnp.zeros_like(out)
        for re, im in SMALL_PAIRS:
            off, rows = SMALL_OFFSET[re], SMALL[re][0]
            out[off:off + rows, :] = jnp.concatenate([ins[re][...], ins[im][...]], axis=1)
        for n in SMALL_VECS:
            off, vec = SMALL_OFFSET[n], ins[n][...]
            for i in range(SMALL[n][1] // LANES):
                out[off + i:off + i + 1, :] = vec[:, i * LANES:(i + 1) * LANES]
        for n in SMALL_TILES:
            off, (rows, cols) = SMALL_OFFSET[n], SMALL[n]
            out[off:off + rows, 0:cols] = ins[n][...]

    return pl.pallas_call(
        body, name="pack_small", out_shape=jax.ShapeDtypeStruct((SMALL_ROWS, LANES), F32),
        compiler_params=_cparams(),
    )(*[grads[n] for n in names])


def _unpack_small_ref(g_ref, n):
    off, (rows, cols) = SMALL_OFFSET[n], SMALL[n]
    for re, im in SMALL_PAIRS:
        if n == re:
            return g_ref[off:off + rows, 0:HALF_LANES]
        if n == im:
            return g_ref[off:off + rows, HALF_LANES:LANES]
    if n in SMALL_VECS:
        return jnp.concatenate([g_ref[off + i:off + i + 1, :] for i in range(cols // LANES)], axis=1)
    return g_ref[off:off + rows, 0:cols]


def _adamw_small(g_packed, w, m, v):
    names = list(SMALL_PARAMS)
    n = len(names)

    def body(g_ref, *refs):
        w_refs, m_refs, v_refs, outs = refs[:n], refs[n:2 * n], refs[2 * n:3 * n], refs[3 * n:]
        for idx, name in enumerate(names):
            g = _unpack_small_ref(g_ref, name)
            delta, m2, v2 = _adamw_math(w_refs[idx][...], g, m_refs[idx][...], v_refs[idx][...])
            outs[4 * idx][...] = g
            outs[4 * idx + 1][...] = delta
            outs[4 * idx + 2][...] = m2
            outs[4 * idx + 3][...] = v2
        outs[4 * n][...] = _unpack_small_ref(g_ref, "loss")

    outs = pl.pallas_call(
        body, name="adamw_small",
        out_shape=[jax.ShapeDtypeStruct(SMALL[name], F32) for name in names for _ in range(4)]
        + [jax.ShapeDtypeStruct(SMALL["loss"], F32)],
        compiler_params=_cparams(),
    )(g_packed, *[w[k] for k in names], *[m[k] for k in names], *[v[k] for k in names])
    return {name: outs[4 * idx:4 * idx + 4] for idx, name in enumerate(names)}, outs[4 * n]


WEIGHT_NAMES = ['norm_ffn1', 'ffn1_w_gate', 'ffn1_w_up', 'ffn1_w_down', 'norm_mix', 'w_in', 'attn_sinks',
                'ssm_lambda_re', 'ssm_lambda_im', 'ssm_log_dt', 'ssm_b_re', 'ssm_b_im', 'ssm_c_re', 'ssm_c_im',
                'ssm_d', 'ssm_glu_w', 'ssm_glu_b', 'attn_out_norm', 'ssm_out_norm', 'w_out', 'norm_ffn2',
                'ffn2_w_gate', 'ffn2_w_up', 'ffn2_w_down', 'final_norm']


def kernel(x, norm_ffn1, ffn1_w_gate, ffn1_w_up, ffn1_w_down, norm_mix, w_in, attn_sinks, ssm_lambda_re, ssm_lambda_im, ssm_log_dt, ssm_b_re, ssm_b_im, ssm_c_re, ssm_c_im, ssm_d, ssm_glu_w, ssm_glu_b, attn_out_norm, ssm_out_norm, w_out, norm_ffn2, ffn2_w_gate, ffn2_w_up, ffn2_w_down, final_norm, loss_target, m_norm_ffn1, m_ffn1_w_gate, m_ffn1_w_up, m_ffn1_w_down, m_norm_mix, m_w_in, m_attn_sinks, m_ssm_lambda_re, m_ssm_lambda_im, m_ssm_log_dt, m_ssm_b_re, m_ssm_b_im, m_ssm_c_re, m_ssm_c_im, m_ssm_d, m_ssm_glu_w, m_ssm_glu_b, m_attn_out_norm, m_ssm_out_norm, m_w_out, m_norm_ffn2, m_ffn2_w_gate, m_ffn2_w_up, m_ffn2_w_down, m_final_norm, v_norm_ffn1, v_ffn1_w_gate, v_ffn1_w_up, v_ffn1_w_down, v_norm_mix, v_w_in, v_attn_sinks, v_ssm_lambda_re, v_ssm_lambda_im, v_ssm_log_dt, v_ssm_b_re, v_ssm_b_im, v_ssm_c_re, v_ssm_c_im, v_ssm_d, v_ssm_glu_w, v_ssm_glu_b, v_attn_out_norm, v_ssm_out_norm, v_w_out, v_norm_ffn2, v_ffn2_w_gate, v_ffn2_w_up, v_ffn2_w_down, v_final_norm):
    args = dict(locals())
    weights = {n: args[n] for n in WEIGHT_NAMES}
    moms = {n: args["m_" + n] for n in WEIGHT_NAMES}
    vars_ = {n: args["v_" + n] for n in WEIGHT_NAMES}

    def shard2d(a, k):
        a = a.reshape(a.shape[-2], a.shape[-1])
        return a.T if BIG[k][3] else a

    def shard_master(a, k):
        return (a.T if BIG[k][3] else a).reshape(weights[BIG[k][0]].shape)

    def blocks(g, k):
        return g.reshape(N_DEV, BIG[k][1], BIG[k][2])

    def full(g, k):
        return g.reshape(N_DEV * BIG[k][1], BIG[k][2])

    shards = dict(zip(BIG, _cast_shards({k: shard2d(weights[BIG[k][0]], k) for k in BIG})))
    nf = len(FIRST_GROUP)
    got = _gather_first([shards[k] for k in FIRST_GROUP], [shards[k] for k in LATE_GROUP])
    w_first = {k: full(g, k) for k, g in zip(FIRST_GROUP, got[:nf])}
    late = {}
    late["own_sems"], late["srcs"], late["lands"], w_token = _late_gather_call(
        "gather_late_start", 0, [shards[k] for k in LATE_GROUP], got[nf:], [])

    def late_pass(dep):
        late["pass_sems"], late["srcs"], late["lands"], token = _late_gather_call(
            "gather_late_pass", 1, late["srcs"], late["lands"], late["own_sems"], after=dep)
        return token

    def late_weights(dep):
        _, _, lands, _ = _late_gather_call("gather_late_wait", 2, late["srcs"], late["lands"],
                                           late["own_sems"] + late["pass_sems"], after=dep)
        return {k: full(g, k) for k, g in zip(LATE_GROUP, lands)}

    early = {}

    def early_grads(g):
        srcs = [blocks(g[k], k) for k in LATE_GROUP]
        lands = [lax.empty((N_PEERS, BIG[k][1], BIG[k][2]), BF16) for k in LATE_GROUP]
        early["send"], early["recv"], early["srcs"], early["lands"], token = _split_start(
            "grads_late_start", srcs, lands, scatter=True)
        return token

    def small2d(a, n):
        if n in SMALL_TRANSPOSED:
            a = jnp.swapaxes(a, -1, -2)
        return a.reshape(SMALL[n])

    def small_master(a, n):
        if n in SMALL_TRANSPOSED:
            shape = weights[n].shape
            return jnp.swapaxes(a.reshape(shape[:-2] + (shape[-1], shape[-2])), -1, -2)
        return a.reshape(weights[n].shape)

    small_p = {n: small2d(weights[n], n) for n in SMALL_PARAMS}
    _, grad_x, g_first, g_small = _local_step(
        x.reshape(SEQ, D_MODEL), loss_target.reshape(SEQ, D_MODEL), w_first, small_p, late_weights, early_grads,
        after=w_token, midway=late_pass)

    own_sums, first_parts, small_grad = _exchange_last([blocks(g_first[k], k) for k in FIRST_GROUP],
                                                       _pack_small(g_small))
    own_late, late_parts = _split_wait("grads_late_wait", early["send"], early["recv"], early["srcs"],
                                       early["lands"], True, small_grad)
    own = dict(zip(FIRST_GROUP + LATE_GROUP, list(own_sums) + list(own_late)))
    parts = dict(zip(FIRST_GROUP + LATE_GROUP, list(first_parts) + list(late_parts)))
    outs = {}
    for k in BIG:
        n = BIG[k][0]
        outs[n] = [shard_master(o, k) for o in
                   _adamw_big(own[k], parts[k], shard2d(weights[n], k), shard2d(moms[n], k), shard2d(vars_[n], k),
                              "adamw_" + n)]
    small_out, loss_row = _adamw_small(small_grad, small_p, {n: small2d(moms[n], n) for n in SMALL_PARAMS},
                                       {n: small2d(vars_[n], n) for n in SMALL_PARAMS})
    for n in SMALL_PARAMS:
        outs[n] = [small_master(o, n) for o in small_out[n]]

    result = [loss_row[0, 0], grad_x.reshape(x.shape)]
    for i in range(4):
        result += [outs[n][i] for n in WEIGHT_NAMES]
    return tuple(result)
```

```python
import functools

import jax
import jax.numpy as jnp
from jax import lax
from jax.experimental import pallas as pl
from jax.experimental.pallas import tpu as pltpu

F32 = jnp.float32
BF16 = jnp.bfloat16

N_DEV = 8
SEQ = 2048
D_MODEL = 1024
D_FF = 2816
ATTN_HEADS = 8
KV_HEADS = 2
HEAD_DIM = 64
ATTN_WIDTH = 512
KV_WIDTH = 128
WINDOW = 128
SSM_WIDTH = 512
IN_WIDTH = 1280
EPS = 1e-6
MASKED_DISTANCE = 1e33
LAMBDA_RE_MAX = -1e-4
LANES = 128
N_LANE_BLOCKS = 16
SCAN_CHUNK = SEQ // 8

ADAM_LR = 0.001
ADAM_B1 = 0.9
ADAM_B2 = 0.999
ADAM_EPS = 1e-08
ADAM_WD = 0.01
ADAM_STEP = 10

VMEM_LIMIT = 60 * 1024 * 1024
MESH_ID = pl.DeviceIdType.MESH


def _cparams(sem=None):
    return pltpu.CompilerParams(dimension_semantics=sem, vmem_limit_bytes=VMEM_LIMIT)


def _dot(a, b):
    return jnp.dot(a, b, preferred_element_type=F32)


def _dot_nt(a, b):
    return lax.dot_general(a, b, (((1,), (1,)), ((), ())), preferred_element_type=F32)


def _dot_tn(a, b):
    return lax.dot_general(a, b, (((0,), (0,)), ((), ())), preferred_element_type=F32)


def _rms_fwd(x, g):
    r = lax.rsqrt(jnp.mean(x * x, axis=-1, keepdims=True) + EPS)
    return x * r * g


def _rms_bwd(dh, x, g):
    r = lax.rsqrt(jnp.mean(x * x, axis=-1, keepdims=True) + EPS)
    xh = x * r
    dg = jnp.sum(dh * xh, axis=0, keepdims=True)
    dxh = dh * g
    dx = r * (dxh - xh * jnp.mean(dxh * xh, axis=-1, keepdims=True))
    return dx, dg


def _sigmoid(x):
    return 1.0 / (1.0 + jnp.exp(-x))


FFN_TM = 512
FFN_TF = 1408


def _ffn_fwd(x, g, wgt, wut, wd, name, after=None, head=None):
    tm, tf = FFN_TM // 2, D_FF
    nj = D_FF // tf
    deps = [] if after is None else [after]
    n_in = len(deps) + (2 if head else 0)

    def body(x_ref, g_ref, wg_ref, wu_ref, wd_ref, *rest):
        i = pl.program_id(0)
        j = pl.program_id(1)
        if head:
            gf_ref, t_ref = rest[len(deps):n_in]
            xo_ref, h_ref, a_ref, b_ref, loss_ref, dgf_ref, h_s, acc = rest[n_in:]
        else:
            xo_ref, h_ref, a_ref, b_ref, h_s, acc = rest[n_in:]

        @pl.when(j == 0)
        def _():
            h = _rms_fwd(x_ref[...], g_ref[...]).astype(BF16)
            h_s[...] = h
            h_ref[...] = h
            acc[...] = jnp.zeros_like(acc)

        h = h_s[...]
        a = _dot_nt(h, wg_ref[...])
        b = _dot_nt(h, wu_ref[...])
        a_ref[...] = a.astype(BF16)
        b_ref[...] = b.astype(BF16)
        s = (a * _sigmoid(a) * b).astype(BF16)
        acc[...] += _dot(s, wd_ref[...])

        @pl.when(j == nj - 1)
        def _():
            xo = x_ref[...] + 0.5 * acc[...]
            if not head:
                xo_ref[...] = xo
                return
            gf = gf_ref[...]
            err = _rms_fwd(xo, gf) - t_ref[...]
            part = jnp.broadcast_to(0.5 * jnp.sum(err * err) / D_MODEL, (1, LANES))
            dx, dgf = _rms_bwd(err * (1.0 / D_MODEL), xo, gf)
            xo_ref[...] = dx

            @pl.when(i == 0)
            def _():
                loss_ref[...] = part
                dgf_ref[...] = dgf

            @pl.when(i != 0)
            def _():
                loss_ref[...] += part
                dgf_ref[...] += dgf

    row = lambda i, j: (i, 0)
    const = lambda i, j: (0, 0)
    head_in = [pl.BlockSpec((1, D_MODEL), const), pl.BlockSpec((tm, D_MODEL), row)] if head else []
    head_out = [pl.BlockSpec((1, LANES), const), pl.BlockSpec((1, D_MODEL), const)] if head else []
    head_shape = [jax.ShapeDtypeStruct((1, LANES), F32), jax.ShapeDtypeStruct((1, D_MODEL), F32)] if head else []
    return pl.pallas_call(
        body, name=name, grid=(SEQ // tm, nj),
        in_specs=[pl.BlockSpec((tm, D_MODEL), row), pl.BlockSpec((1, D_MODEL), const),
                  pl.BlockSpec((tf, D_MODEL), lambda i, j: (j, 0)),
                  pl.BlockSpec((tf, D_MODEL), lambda i, j: (j, 0)),
                  pl.BlockSpec((tf, D_MODEL), lambda i, j: (j, 0))] + [pl.BlockSpec(memory_space=pl.ANY)] * len(deps)
        + head_in,
        out_specs=[pl.BlockSpec((tm, D_MODEL), row), pl.BlockSpec((tm, D_MODEL), row),
                   pl.BlockSpec((tm, tf), lambda i, j: (i, j)),
                   pl.BlockSpec((tm, tf), lambda i, j: (i, j))] + head_out,
        out_shape=[jax.ShapeDtypeStruct((SEQ, D_MODEL), F32), jax.ShapeDtypeStruct((SEQ, D_MODEL), BF16),
                   jax.ShapeDtypeStruct((SEQ, D_FF), BF16), jax.ShapeDtypeStruct((SEQ, D_FF), BF16)] + head_shape,
        scratch_shapes=[pltpu.VMEM((tm, D_MODEL), BF16), pltpu.VMEM((tm, D_MODEL), F32)],
        compiler_params=_cparams(("arbitrary" if head else "parallel", "arbitrary")),
    )(x, g, wgt, wut, wd, *deps, *(head or ()))


def _ffn_bwd_act(dxo, x, g, a, b, wgt, wut, wd, name):
    tm, tf = FFN_TM // 2, D_FF
    nj = D_FF // tf
    resident = pl.Buffered(1)

    def body(dxo_ref, x_ref, g_ref, a_ref, b_ref, wg_ref, wu_ref, wd_ref,
             dx_ref, da_ref, db_ref, s_ref, df_ref, dg_ref, df_s, acc):
        i = pl.program_id(0)
        j = pl.program_id(1)

        @pl.when(j == 0)
        def _():
            df = (0.5 * dxo_ref[...]).astype(BF16)
            df_s[...] = df
            df_ref[...] = df
            acc[...] = jnp.zeros_like(acc)

        ds = _dot_nt(df_s[...], wd_ref[...])
        av = a_ref[...].astype(F32)
        bv = b_ref[...].astype(F32)
        sig = _sigmoid(av)
        sl = av * sig
        s_ref[...] = (sl * bv).astype(BF16)
        db = (ds * sl).astype(BF16)
        da = (ds * bv * (sig * (1.0 + av * (1.0 - sig)))).astype(BF16)
        da_ref[...] = da
        db_ref[...] = db
        acc[...] += _dot(da, wg_ref[...]) + _dot(db, wu_ref[...])

        @pl.when(j == nj - 1)
        def _():
            dx, dg = _rms_bwd(acc[...], x_ref[...], g_ref[...])
            dx_ref[...] = dxo_ref[...] + dx

            @pl.when(i == 0)
            def _():
                dg_ref[...] = dg

            @pl.when(i != 0)
            def _():
                dg_ref[...] += dg

    row = lambda i, j: (i, 0)
    col = lambda i, j: (j, 0)
    tile = lambda i, j: (i, j)
    return pl.pallas_call(
        body, name=name, grid=(SEQ // tm, nj),
        in_specs=[pl.BlockSpec((tm, D_MODEL), row), pl.BlockSpec((tm, D_MODEL), row),
                  pl.BlockSpec((1, D_MODEL), lambda i, j: (0, 0)),
                  pl.BlockSpec((tm, tf), tile), pl.BlockSpec((tm, tf), tile),
                  pl.BlockSpec((tf, D_MODEL), col, pipeline_mode=resident),
                  pl.BlockSpec((tf, D_MODEL), col, pipeline_mode=resident),
                  pl.BlockSpec((tf, D_MODEL), col, pipeline_mode=resident)],
        out_specs=[pl.BlockSpec((tm, D_MODEL), row),
                   pl.BlockSpec((tm, tf), tile), pl.BlockSpec((tm, tf), tile), pl.BlockSpec((tm, tf), tile),
                   pl.BlockSpec((tm, D_MODEL), row),
                   pl.BlockSpec((1, D_MODEL), lambda i, j: (0, 0))],
        out_shape=[jax.ShapeDtypeStruct((SEQ, D_MODEL), F32),
                   jax.ShapeDtypeStruct((SEQ, D_FF), BF16), jax.ShapeDtypeStruct((SEQ, D_FF), BF16),
                   jax.ShapeDtypeStruct((SEQ, D_FF), BF16),
                   jax.ShapeDtypeStruct((SEQ, D_MODEL), BF16),
                   jax.ShapeDtypeStruct((1, D_MODEL), F32)],
        scratch_shapes=[pltpu.VMEM((tm, D_MODEL), BF16), pltpu.VMEM((tm, D_MODEL), F32)],
        compiler_params=_cparams(("arbitrary", "arbitrary")),
    )(dxo, x, g, a, b, wgt, wut, wd)


def _mm_tn(pairs, name, tmm=256):
    m = pairs[0][0].shape[1]
    n_pairs = len(pairs)

    def body(*refs):
        ins, outs = refs[:2 * n_pairs], refs[2 * n_pairs:]
        for p in range(n_pairs):
            outs[p][...] = _dot_tn(ins[2 * p][...], ins[2 * p + 1][...]).astype(BF16)

    in_specs, out_specs, out_shape, args = [], [], [], []
    for a, b in pairs:
        n = b.shape[1]
        in_specs += [pl.BlockSpec((SEQ, tmm), lambda i: (0, i)), pl.BlockSpec((SEQ, n), lambda i: (0, 0))]
        out_specs.append(pl.BlockSpec((tmm, n), lambda i: (i, 0)))
        out_shape.append(jax.ShapeDtypeStruct((m, n), BF16))
        args += [a, b]
    return pl.pallas_call(body, name=name, grid=(m // tmm,), in_specs=in_specs, out_specs=out_specs,
                          out_shape=out_shape, compiler_params=_cparams(("parallel",)))(*args)


MIX_TM = 512


def _mixin_fwd(x, g, wint):
    tm = MIX_TM

    def body(x_ref, g_ref, w_ref, h_ref, q_ref, k_ref, v_ref, u_ref):
        h = _rms_fwd(x_ref[...], g_ref[...]).astype(BF16)
        h_ref[...] = h
        proj = _dot_nt(h, w_ref[...])
        q_ref[...] = proj[:, :ATTN_WIDTH].T
        k_ref[...] = proj[:, ATTN_WIDTH:ATTN_WIDTH + KV_WIDTH]
        v_ref[...] = proj[:, ATTN_WIDTH + KV_WIDTH:ATTN_WIDTH + 2 * KV_WIDTH]
        u_ref[...] = proj[:, ATTN_WIDTH + 2 * KV_WIDTH:]

    row = lambda i: (i, 0)
    return pl.pallas_call(
        body, name="mixin_fwd", grid=(SEQ // tm,),
        in_specs=[pl.BlockSpec((tm, D_MODEL), row), pl.BlockSpec((1, D_MODEL), lambda i: (0, 0)),
                  pl.BlockSpec((IN_WIDTH, D_MODEL), lambda i: (0, 0))],
        out_specs=[pl.BlockSpec((tm, D_MODEL), row), pl.BlockSpec((ATTN_WIDTH, tm), lambda i: (0, i)),
                   pl.BlockSpec((tm, KV_WIDTH), row), pl.BlockSpec((tm, KV_WIDTH), row),
                   pl.BlockSpec((tm, SSM_WIDTH), row)],
        out_shape=[jax.ShapeDtypeStruct((SEQ, D_MODEL), BF16), jax.ShapeDtypeStruct((ATTN_WIDTH, SEQ), F32),
                   jax.ShapeDtypeStruct((SEQ, KV_WIDTH), F32), jax.ShapeDtypeStruct((SEQ, KV_WIDTH), F32),
                   jax.ShapeDtypeStruct((SEQ, SSM_WIDTH), F32)],
        compiler_params=_cparams(("parallel",)),
    )(x, g, wint)


def _mixin_bwd(dqt, dk, dv, du, wint, x, g, dres):
    tm = MIX_TM

    def body(dq_ref, dk_ref, dv_ref, du_ref, w_ref, x_ref, g_ref, dres_ref, dx_ref, dp_ref, dg_ref):
        i = pl.program_id(0)
        dp = jnp.concatenate([dq_ref[...].T, dk_ref[...], dv_ref[...], du_ref[...]], axis=-1).astype(BF16)
        dp_ref[...] = dp
        dh = _dot(dp, w_ref[...])
        dx, dg = _rms_bwd(dh, x_ref[...], g_ref[...])
        dx_ref[...] = dres_ref[...] + dx

        @pl.when(i == 0)
        def _():
            dg_ref[...] = dg

        @pl.when(i != 0)
        def _():
            dg_ref[...] += dg

    row = lambda i: (i, 0)
    const = lambda i: (0, 0)
    return pl.pallas_call(
        body, name="mixin_bwd", grid=(SEQ // tm,),
        in_specs=[pl.BlockSpec((ATTN_WIDTH, tm), lambda i: (0, i)), pl.BlockSpec((tm, KV_WIDTH), row),
                  pl.BlockSpec((tm, KV_WIDTH), row), pl.BlockSpec((tm, SSM_WIDTH), row),
                  pl.BlockSpec((IN_WIDTH, D_MODEL), const), pl.BlockSpec((tm, D_MODEL), row),
                  pl.BlockSpec((1, D_MODEL), const), pl.BlockSpec((tm, D_MODEL), row)],
        out_specs=[pl.BlockSpec((tm, D_MODEL), row), pl.BlockSpec((tm, IN_WIDTH), row),
                   pl.BlockSpec((1, D_MODEL), const)],
        out_shape=[jax.ShapeDtypeStruct((SEQ, D_MODEL), F32), jax.ShapeDtypeStruct((SEQ, IN_WIDTH), BF16),
                   jax.ShapeDtypeStruct((1, D_MODEL), F32)],
        compiler_params=_cparams(("arbitrary",)),
    )(dqt, dk, dv, du, wint, x, g, dres)


N_QBLOCKS = SEQ // WINDOW
GROUP = ATTN_HEADS // KV_HEADS
SCALE = HEAD_DIM ** -0.5


def _alibi_slope(h):
    return 2.0 ** (-8.0 * (h + 1) / ATTN_HEADS)


def _window_masks(n):
    s_idx = lax.broadcasted_iota(jnp.int32, (3 * WINDOW, WINDOW), 0)
    t_idx = lax.broadcasted_iota(jnp.int32, (3 * WINDOW, WINDOW), 1)
    absrel = jnp.abs(s_idx - WINDOW - t_idx)
    key_pos = n * WINDOW - WINDOW + s_idx
    valid = (absrel <= WINDOW) & (key_pos >= 0) & (key_pos < SEQ)
    return jnp.where(valid, absrel.astype(F32), MASKED_DISTANCE)


def _group_cols(ref, r0, gi):
    return jnp.concatenate(
        [ref[(gi * GROUP + hh) * HEAD_DIM:(gi * GROUP + hh + 1) * HEAD_DIM, pl.ds(r0, WINDOW)].astype(BF16)
         for hh in range(GROUP)], axis=1)


def _group_probs(qgt, kw, dist, gi, sk_ref):
    bias = jnp.concatenate([-_alibi_slope(gi * GROUP + hh) * dist for hh in range(GROUP)], axis=1)
    sink = jnp.concatenate([jnp.full((1, WINDOW), sk_ref[0, gi * GROUP + hh], F32) for hh in range(GROUP)], axis=1)
    s = _dot(kw, qgt) * SCALE + bias
    m = jnp.maximum(jnp.max(s, axis=0, keepdims=True), sink)
    p = jnp.exp(s - m)
    ps = jnp.exp(sink - m)
    inv = 1.0 / (jnp.sum(p, axis=0, keepdims=True) + ps)
    return p * inv, ps * inv


def _pad_window(src_ref, dst_ref):
    zeros = jnp.zeros((WINDOW, KV_WIDTH), BF16)
    dst_ref[0:WINDOW, :] = zeros
    dst_ref[WINDOW + SEQ:, :] = zeros
    dst_ref[WINDOW:WINDOW + SEQ, :] = src_ref[...].astype(BF16)


def _attn_fwd(qt, k, v, sinks, after=None):
    deps = [] if after is None else [after]

    def body(sk_ref, qt_ref, k_ref, v_ref, *rest):
        o_ref, kp_ref, vp_ref = rest[len(deps):]
        _pad_window(k_ref, kp_ref)
        _pad_window(v_ref, vp_ref)

        def blk(n, carry):
            r0 = pl.multiple_of(n * WINDOW, WINDOW)
            dist = _window_masks(n)
            for gi in range(KV_HEADS):
                kw = kp_ref[pl.ds(r0, 3 * WINDOW), gi * HEAD_DIM:(gi + 1) * HEAD_DIM]
                vw = vp_ref[pl.ds(r0, 3 * WINDOW), gi * HEAD_DIM:(gi + 1) * HEAD_DIM]
                pr, _ = _group_probs(_group_cols(qt_ref, r0, gi), kw, dist, gi, sk_ref)
                og = _dot_tn(pr.astype(BF16), vw)
                for hh in range(GROUP):
                    h = gi * GROUP + hh
                    o_ref[pl.ds(r0, WINDOW), h * HEAD_DIM:(h + 1) * HEAD_DIM] = og[hh * WINDOW:(hh + 1) * WINDOW]
            return carry

        lax.fori_loop(0, N_QBLOCKS, blk, 0)

    vmem = pl.BlockSpec(memory_space=pltpu.VMEM)
    return pl.pallas_call(
        body, name="attn_fwd",
        in_specs=[pl.BlockSpec(memory_space=pltpu.SMEM), vmem, vmem, vmem]
        + [pl.BlockSpec(memory_space=pl.ANY)] * len(deps), out_specs=vmem,
        out_shape=jax.ShapeDtypeStruct((SEQ, ATTN_WIDTH), F32),
        scratch_shapes=[pltpu.VMEM((SEQ + 2 * WINDOW, KV_WIDTH), BF16)] * 2,
        compiler_params=_cparams(),
    )(sinks, qt, k, v, *deps)


def _attn_bwd(qt, k, v, sinks, dot_):
    def body(sk_ref, qt_ref, k_ref, v_ref, dot_ref, dqt_ref, dk_ref, dv_ref, dsk_ref,
             dsk_acc, kp_ref, vp_ref, dkp_ref, dvp_ref):
        _pad_window(k_ref, kp_ref)
        _pad_window(v_ref, vp_ref)
        dkp_ref[...] = jnp.zeros_like(dkp_ref)
        dvp_ref[...] = jnp.zeros_like(dvp_ref)
        dsk_acc[...] = jnp.zeros_like(dsk_acc)

        def blk(n, carry):
            r0 = pl.multiple_of(n * WINDOW, WINDOW)
            dist = _window_masks(n)
            for gi in range(KV_HEADS):
                gcols = slice(gi * HEAD_DIM, (gi + 1) * HEAD_DIM)
                kw = kp_ref[pl.ds(r0, 3 * WINDOW), gcols]
                vw = vp_ref[pl.ds(r0, 3 * WINDOW), gcols]
                qgt = _group_cols(qt_ref, r0, gi)
                dogt = _group_cols(dot_ref, r0, gi)
                pr, psink = _group_probs(qgt, kw, dist, gi, sk_ref)
                dp = _dot(vw, dogt)
                delta = jnp.sum(pr * dp, axis=0, keepdims=True)
                ds = (pr * (dp - delta)).astype(BF16)
                dsk_acc[gi:gi + 1, :] += -(psink * delta)
                dqgt = _dot_tn(kw, ds) * SCALE
                for hh in range(GROUP):
                    h = gi * GROUP + hh
                    dqt_ref[h * HEAD_DIM:(h + 1) * HEAD_DIM, pl.ds(r0, WINDOW)] = dqgt[:, hh * WINDOW:(hh + 1) * WINDOW]
                dkp_ref[pl.ds(r0, 3 * WINDOW), gcols] += _dot_nt(ds, qgt) * SCALE
                dvp_ref[pl.ds(r0, 3 * WINDOW), gcols] += _dot_nt(pr.astype(BF16), dogt)
            return carry

        lax.fori_loop(0, N_QBLOCKS, blk, 0)
        for h in range(ATTN_HEADS):
            gi, hh = divmod(h, GROUP)
            dsk_ref[:, h:h + 1] = jnp.sum(dsk_acc[gi:gi + 1, hh * WINDOW:(hh + 1) * WINDOW], axis=1, keepdims=True)
        dk_ref[...] = dkp_ref[WINDOW:WINDOW + SEQ, :]
        dv_ref[...] = dvp_ref[WINDOW:WINDOW + SEQ, :]

    vmem = pl.BlockSpec(memory_space=pltpu.VMEM)
    padded = (SEQ + 2 * WINDOW, KV_WIDTH)
    return pl.pallas_call(
        body, name="attn_bwd",
        in_specs=[pl.BlockSpec(memory_space=pltpu.SMEM), vmem, vmem, vmem, vmem],
        out_specs=[vmem, vmem, vmem, vmem],
        out_shape=[jax.ShapeDtypeStruct((ATTN_WIDTH, SEQ), F32),
                   jax.ShapeDtypeStruct((SEQ, KV_WIDTH), F32), jax.ShapeDtypeStruct((SEQ, KV_WIDTH), F32),
                   jax.ShapeDtypeStruct((1, ATTN_HEADS), F32)],
        scratch_shapes=[pltpu.VMEM((KV_HEADS, GROUP * WINDOW), F32), pltpu.VMEM(padded, BF16),
                        pltpu.VMEM(padded, BF16), pltpu.VMEM(padded, F32), pltpu.VMEM(padded, F32)],
        compiler_params=_cparams(),
    )(sinks, qt, k, v, dot_)


HALF_LANES = LANES // 2
BLOCK_ROWS = 32


def _embed_block(bt, q):
    z = jnp.zeros((16, HALF_LANES), bt.dtype)
    blk = jnp.concatenate([jnp.concatenate([bt[:16], z], axis=1), jnp.concatenate([z, bt[16:]], axis=1)], axis=0)
    parts = [jnp.zeros((BLOCK_ROWS * q, LANES), bt.dtype)] if q else []
    parts.append(blk)
    if q < 3:
        parts.append(jnp.zeros((BLOCK_ROWS * (3 - q), LANES), bt.dtype))
    return jnp.concatenate(parts, axis=0)


def _extract_block(m, q):
    blk = m[BLOCK_ROWS * q:BLOCK_ROWS * (q + 1)]
    return jnp.concatenate([blk[:16, :HALF_LANES], blk[16:, HALF_LANES:]], axis=0)


def _ssm_prep(lam_re, lam_im, log_dt, bt_re, bt_im, c_re, c_im):
    nb = 2 * N_LANE_BLOCKS

    def body(lr_ref, li_ref, ldt_ref, btr_ref, bti_ref, ctr_ref, cti_ref, ar_ref, ai_ref, bb_ref, cc_ref):
        lr = jnp.minimum(lr_ref[...], LAMBDA_RE_MAX)
        li = li_ref[...]
        dt = jnp.exp(ldt_ref[...])
        mag = jnp.exp(lr * dt)
        ar = mag * jnp.cos(li * dt)
        ai = mag * jnp.sin(li * dt)
        den = lr * lr + li * li
        cr = ((ar - 1.0) * lr + ai * li) / den
        ci = (ai * lr - (ar - 1.0) * li) / den
        ar_ref[...] = ar
        ai_ref[...] = ai
        for i in range(nb):
            q = i % 4
            rows = slice(BLOCK_ROWS * i, BLOCK_ROWS * (i + 1))
            br = _embed_block(btr_ref[rows, :], q)
            bi = _embed_block(bti_ref[rows, :], q)
            cri, cii = cr[i:i + 1, :], ci[i:i + 1, :]
            bb_ref[i] = jnp.concatenate([cri * br - cii * bi, cri * bi + cii * br], axis=1).astype(BF16)
            cc_ref[i] = jnp.concatenate([_embed_block(ctr_ref[rows, :], q).T,
                                         -_embed_block(cti_ref[rows, :], q).T], axis=0).astype(BF16)

    return pl.pallas_call(
        body, name="ssm_prep",
        out_shape=[jax.ShapeDtypeStruct((nb, LANES), F32), jax.ShapeDtypeStruct((nb, LANES), F32),
                   jax.ShapeDtypeStruct((nb, LANES, 2 * LANES), BF16),
                   jax.ShapeDtypeStruct((nb, 2 * LANES, LANES), BF16)],
        compiler_params=_cparams(),
    )(lam_re, lam_im, log_dt, bt_re, bt_im, c_re, c_im)


def _ssm_prep_bwd(lam_re, lam_im, log_dt, bt_re, bt_im, dar, dai, dbb, dcc):
    nb = 2 * N_LANE_BLOCKS

    def body(lr_ref, li_ref, ldt_ref, btr_ref, bti_ref, dar_ref, dai_ref, dbb_ref, dcc_ref,
             glr_ref, gli_ref, gdt_ref, gbr_ref, gbi_ref, gcre_ref, gcim_ref, gcr_s, gci_s):
        lam = lr_ref[...]
        lr = jnp.minimum(lam, LAMBDA_RE_MAX)
        li = li_ref[...]
        dt = jnp.exp(ldt_ref[...])
        mag = jnp.exp(lr * dt)
        cs = jnp.cos(li * dt)
        sn = jnp.sin(li * dt)
        ar = mag * cs
        ai = mag * sn
        den = lr * lr + li * li
        nr = (ar - 1.0) * lr + ai * li
        ni = ai * lr - (ar - 1.0) * li
        cr = nr / den
        ci = ni / den
        for i in range(nb):
            q = i % 4
            rows = slice(BLOCK_ROWS * i, BLOCK_ROWS * (i + 1))
            br = _embed_block(btr_ref[rows, :], q)
            bi = _embed_block(bti_ref[rows, :], q)
            gbbr = dbb_ref[i, :, :LANES]
            gbbi = dbb_ref[i, :, LANES:]
            cri, cii = cr[i:i + 1, :], ci[i:i + 1, :]
            gcr_s[i:i + 1, :] = jnp.sum(gbbr * br + gbbi * bi, axis=0, keepdims=True)
            gci_s[i:i + 1, :] = jnp.sum(gbbi * br - gbbr * bi, axis=0, keepdims=True)
            gbr_ref[rows, :] = _extract_block(cri * gbbr + cii * gbbi, q)
            gbi_ref[rows, :] = _extract_block(cri * gbbi - cii * gbbr, q)
            gcre_ref[rows, :] = _extract_block(dcc_ref[i, :LANES, :].T, q)
            gcim_ref[rows, :] = -_extract_block(dcc_ref[i, LANES:, :].T, q)
        g_cr = gcr_s[...]
        g_ci = gci_s[...]
        g_nr = g_cr / den
        g_ni = g_ci / den
        g_den = -(g_cr * nr + g_ci * ni) / (den * den)
        g_ar = dar_ref[...] + g_nr * lr - g_ni * li
        g_ai = dai_ref[...] + g_nr * li + g_ni * lr
        g_lr = g_nr * (ar - 1.0) + g_ni * ai + g_den * 2.0 * lr
        g_li = g_nr * ai - g_ni * (ar - 1.0) + g_den * 2.0 * li
        g_mag = g_ar * cs + g_ai * sn
        g_th = (g_ai * cs - g_ar * sn) * mag
        g_lr = g_lr + g_mag * mag * dt
        g_li = g_li + g_th * dt
        g_dt = g_mag * mag * lr + g_th * li
        glr_ref[...] = jnp.where(lam < LAMBDA_RE_MAX, g_lr, 0.0)
        gli_ref[...] = g_li
        gl = g_dt * dt
        half = LANES // 2
        gdt_ref[:, 0:1] = jnp.sum(gl[:, :half], axis=1, keepdims=True)
        gdt_ref[:, 1:2] = jnp.sum(gl[:, half:], axis=1, keepdims=True)

    rows_shape = jax.ShapeDtypeStruct((nb * BLOCK_ROWS, HALF_LANES), F32)
    return pl.pallas_call(
        body, name="ssm_prep_bwd",
        out_shape=[jax.ShapeDtypeStruct((nb, LANES), F32), jax.ShapeDtypeStruct((nb, LANES), F32),
                   jax.ShapeDtypeStruct((nb, 2), F32), rows_shape, rows_shape, rows_shape, rows_shape],
        scratch_shapes=[pltpu.VMEM((nb, LANES), F32), pltpu.VMEM((nb, LANES), F32)],
        compiler_params=_cparams(),
    )(lam_re, lam_im, log_dt, bt_re, bt_im, dar, dai, dbb, dcc)


def _cmul(ar, ai, br, bi):
    return ar * br - ai * bi, ar * bi + ai * br


def _interleave_rows(src_ref, dst_ref):
    def step(j, carry):
        dst_ref[pl.ds(pl.multiple_of(j * 8, 8), 8), :] = src_ref[pl.ds(j, 8, stride=SCAN_CHUNK), :]
        return carry
    lax.fori_loop(0, SCAN_CHUNK, step, 0, unroll=4)


def _deinterleave_rows(src_ref, dst_ref):
    def step(j, carry):
        dst_ref[pl.ds(j, 8, stride=SCAN_CHUNK), :] = src_ref[pl.ds(pl.multiple_of(j * 8, 8), 8), :]
        return carry
    lax.fori_loop(0, SCAN_CHUNK, step, 0, unroll=4)


def _scan_inplace(re_ref, im_ref, a_re, a_im, reverse):
    nq = len(a_re)
    ch = SCAN_CHUNK
    ab_re = [jnp.broadcast_to(a, (8, LANES)) for a in a_re]
    ab_im = [jnp.broadcast_to(a, (8, LANES)) for a in a_im]

    def rows(j):
        jj = (ch - 1 - j) if reverse else j
        return pl.ds(pl.multiple_of(jj * 8, 8), 8)

    def sweep(init, store):
        def step(j, st):
            out = []
            r = rows(j)
            for qi in range(nq):
                xr, xi = st[2 * qi], st[2 * qi + 1]
                pr, pi = _cmul(ab_re[qi], ab_im[qi], xr, xi)
                xr = pr + re_ref[qi, r, :]
                xi = pi + im_ref[qi, r, :]
                if store:
                    re_ref[qi, r, :] = xr
                    im_ref[qi, r, :] = xi
                out += [xr, xi]
            return tuple(out)
        return lax.fori_loop(0, ch, step, tuple(init), unroll=2)

    zeros = [jnp.zeros((8, LANES), F32)] * (2 * nq)
    finals = sweep(zeros, store=False)

    row_id = lax.broadcasted_iota(jnp.int32, (8, LANES), 0)
    carries = []
    for qi in range(nq):
        pr, pi = ab_re[qi], ab_im[qi]
        for _ in range(8):
            pr, pi = _cmul(pr, pi, pr, pi)
        fr, fi = finals[2 * qi], finals[2 * qi + 1]
        sr = jnp.zeros((8, LANES), F32)
        si = jnp.zeros((8, LANES), F32)
        for _ in range(7):
            tr, ti = _cmul(pr, pi, sr, si)
            tr, ti = tr + fr, ti + fi
            if reverse:
                sr = jnp.where(row_id == 7, 0.0, pltpu.roll(tr, 7, axis=0))
                si = jnp.where(row_id == 7, 0.0, pltpu.roll(ti, 7, axis=0))
            else:
                sr = jnp.where(row_id == 0, 0.0, pltpu.roll(tr, 1, axis=0))
                si = jnp.where(row_id == 0, 0.0, pltpu.roll(ti, 1, axis=0))
        carries += [sr, si]
    sweep(carries, store=True)


SSM_Q = 4


def _ssm_fwd(u, are, aim, bb, cc, dskip, after=None):
    nq = SSM_Q
    deps = [] if after is None else [after]

    def body(u_ref, ar_ref, ai_ref, bb_ref, cc_ref, d_ref, *rest):
        y_ref, xr_ref, xi_ref, sre, sim, up, yp = rest[len(deps):]
        _interleave_rows(u_ref, up)
        uf = up[...]
        ub = uf.astype(BF16)
        yp[...] = d_ref[...] * uf
        for d in range(2):
            for qi in range(nq):
                sre[qi] = _dot(ub, bb_ref[d, qi, :, :LANES])
                sim[qi] = _dot(ub, bb_ref[d, qi, :, LANES:])
            _scan_inplace(sre, sim, [ar_ref[d, qi] for qi in range(nq)], [ai_ref[d, qi] for qi in range(nq)],
                          reverse=(d == 1))
            for qi in range(nq):
                xrb = sre[qi].astype(BF16)
                xib = sim[qi].astype(BF16)
                xr_ref[d, qi] = xrb
                xi_ref[d, qi] = xib
                yp[...] += _dot(xrb, cc_ref[d, qi, :LANES, :]) + _dot(xib, cc_ref[d, qi, LANES:, :])
        _deinterleave_rows(yp, y_ref)

    blk4 = lambda k: (0, k, 0, 0)
    return pl.pallas_call(
        body, name="ssm_fwd", grid=(SSM_WIDTH // LANES,),
        in_specs=[pl.BlockSpec((SEQ, LANES), lambda k: (0, k)),
                  pl.BlockSpec((2, nq, 1, LANES), blk4), pl.BlockSpec((2, nq, 1, LANES), blk4),
                  pl.BlockSpec((2, nq, LANES, 2 * LANES), blk4), pl.BlockSpec((2, nq, 2 * LANES, LANES), blk4),
                  pl.BlockSpec((1, LANES), lambda k: (0, k))] + [pl.BlockSpec(memory_space=pl.ANY)] * len(deps),
        out_specs=[pl.BlockSpec((SEQ, LANES), lambda k: (0, k)),
                   pl.BlockSpec((2, nq, SEQ, LANES), blk4), pl.BlockSpec((2, nq, SEQ, LANES), blk4)],
        out_shape=[jax.ShapeDtypeStruct((SEQ, SSM_WIDTH), F32),
                   jax.ShapeDtypeStruct((2, N_LANE_BLOCKS, SEQ, LANES), BF16),
                   jax.ShapeDtypeStruct((2, N_LANE_BLOCKS, SEQ, LANES), BF16)],
        scratch_shapes=[pltpu.VMEM((nq, SEQ, LANES), F32), pltpu.VMEM((nq, SEQ, LANES), F32),
                        pltpu.VMEM((SEQ, LANES), F32), pltpu.VMEM((SEQ, LANES), F32)],
        compiler_params=_cparams(("parallel",)),
    )(u, are, aim, bb, cc, dskip, *deps)


def _ssm_bwd(dy, u, xr, xi, are, aim, bb, cc, dskip, after=None):
    nq = SSM_Q
    body_rows = SEQ - 8
    deps = [] if after is None else [after]

    def body(dy_ref, u_ref, xr_ref, xi_ref, ar_ref, ai_ref, bb_ref, cc_ref, d_ref, *rest):
        du_ref, dd_ref, dcc_ref, dbb_ref, dar_ref, dai_ref, sre, sim, up, dyp, dup = rest[len(deps):]
        _interleave_rows(u_ref, up)
        _interleave_rows(dy_ref, dyp)
        dyf = dyp[...]
        uf = up[...]
        dyb = dyf.astype(BF16)
        ub = uf.astype(BF16)
        dd_ref[...] = jnp.sum(dyf * uf, axis=0, keepdims=True)
        dup[...] = d_ref[...] * dyf
        row8 = lax.broadcasted_iota(jnp.int32, (8, LANES), 0)
        for d in range(2):
            for qi in range(nq):
                dx = _dot_nt(dyb, cc_ref[d, qi])
                sre[qi] = dx[:, :LANES]
                sim[qi] = dx[:, LANES:]
                dcc_ref[d, qi] = _dot_tn(jnp.concatenate([xr_ref[d, qi], xi_ref[d, qi]], axis=1), dyb)
            _scan_inplace(sre, sim, [ar_ref[d, qi] for qi in range(nq)], [-ai_ref[d, qi] for qi in range(nq)],
                          reverse=(d == 0))
            for qi in range(nq):
                gr = sre[qi]
                gi = sim[qi]
                xrf = xr_ref[d, qi].astype(F32)
                xif = xi_ref[d, qi].astype(F32)
                if d == 0:
                    g_main_r, g_main_i = gr[8:], gi[8:]
                    x_main_r, x_main_i = xrf[:body_rows], xif[:body_rows]
                    g_edge_r, g_edge_i = gr[:8], gi[:8]
                    x_edge_r = jnp.where(row8 == 0, 0.0, pltpu.roll(xrf[body_rows:], 1, axis=0))
                    x_edge_i = jnp.where(row8 == 0, 0.0, pltpu.roll(xif[body_rows:], 1, axis=0))
                else:
                    g_main_r, g_main_i = gr[:body_rows], gi[:body_rows]
                    x_main_r, x_main_i = xrf[8:], xif[8:]
                    g_edge_r, g_edge_i = gr[body_rows:], gi[body_rows:]
                    x_edge_r = jnp.where(row8 == 7, 0.0, pltpu.roll(xrf[:8], 7, axis=0))
                    x_edge_i = jnp.where(row8 == 7, 0.0, pltpu.roll(xif[:8], 7, axis=0))
                dar_ref[d, qi] = (jnp.sum(g_main_r * x_main_r + g_main_i * x_main_i, axis=0, keepdims=True)
                                  + jnp.sum(g_edge_r * x_edge_r + g_edge_i * x_edge_i, axis=0, keepdims=True))
                dai_ref[d, qi] = (jnp.sum(g_main_i * x_main_r - g_main_r * x_main_i, axis=0, keepdims=True)
                                  + jnp.sum(g_edge_i * x_edge_r - g_edge_r * x_edge_i, axis=0, keepdims=True))
                gb = jnp.concatenate([gr, gi], axis=1).astype(BF16)
                dup[...] += _dot_nt(gb, bb_ref[d, qi])
                dbb_ref[d, qi] = _dot_tn(ub, gb)
        _deinterleave_rows(dup, du_ref)

    blk4 = lambda k: (0, k, 0, 0)
    col = lambda k: (0, k)
    bb_spec = pl.BlockSpec((2, nq, LANES, 2 * LANES), blk4)
    cc_spec = pl.BlockSpec((2, nq, 2 * LANES, LANES), blk4)
    a_spec = pl.BlockSpec((2, nq, 1, LANES), blk4)
    x_spec = pl.BlockSpec((2, nq, SEQ, LANES), blk4)
    a_shape = jax.ShapeDtypeStruct((2, N_LANE_BLOCKS, 1, LANES), F32)
    return pl.pallas_call(
        body, name="ssm_bwd", grid=(SSM_WIDTH // LANES,),
        in_specs=[pl.BlockSpec((SEQ, LANES), col), pl.BlockSpec((SEQ, LANES), col), x_spec, x_spec,
                  a_spec, a_spec, bb_spec, cc_spec, pl.BlockSpec((1, LANES), col)]
        + [pl.BlockSpec(memory_space=pl.ANY)] * len(deps),
        out_specs=[pl.BlockSpec((SEQ, LANES), col), pl.BlockSpec((1, LANES), col),
                   cc_spec, bb_spec, a_spec, a_spec],
        out_shape=[jax.ShapeDtypeStruct((SEQ, SSM_WIDTH), F32), jax.ShapeDtypeStruct((1, SSM_WIDTH), F32),
                   jax.ShapeDtypeStruct((2, N_LANE_BLOCKS, 2 * LANES, LANES), F32),
                   jax.ShapeDtypeStruct((2, N_LANE_BLOCKS, LANES, 2 * LANES), F32), a_shape, a_shape],
        scratch_shapes=[pltpu.VMEM((nq, SEQ, LANES), F32), pltpu.VMEM((nq, SEQ, LANES), F32),
                        pltpu.VMEM((SEQ, LANES), F32), pltpu.VMEM((SEQ, LANES), F32), pltpu.VMEM((SEQ, LANES), F32)],
        compiler_params=_cparams(("parallel",)),
    )(dy, u, xr, xi, are, aim, bb, cc, dskip, *deps)


GELU_C = 0.7978845608028654
GELU_K = 0.044715


def _gelu(y):
    return 0.5 * y * (1.0 + jnp.tanh(GELU_C * (y + GELU_K * y * y * y)))


def _gelu_grad(y):
    t = jnp.tanh(GELU_C * (y + GELU_K * y * y * y))
    return 0.5 * (1.0 + t) + 0.5 * y * (1.0 - t * t) * GELU_C * (1.0 + 3.0 * GELU_K * y * y)


def _mixout_fwd(o, y, glu_w, glu_b, gan, gsn, wout, x1):
    tm = MIX_TM

    def body(o_ref, y_ref, gw_ref, gb_ref, gan_ref, gsn_ref, w_ref, x1_ref, x2_ref, mx_ref):
        yg = _gelu(y_ref[...])
        z = _dot(yg.astype(BF16), gw_ref[...]) + gb_ref[...]
        so = yg * _sigmoid(z)
        na = _rms_fwd(o_ref[...], gan_ref[...])
        ns = _rms_fwd(so, gsn_ref[...])
        mixed = jnp.concatenate([na, ns], axis=-1).astype(BF16)
        mx_ref[...] = mixed
        x2_ref[...] = x1_ref[...] + _dot(mixed, w_ref[...])

    row = lambda i: (i, 0)
    const = lambda i: (0, 0)
    return pl.pallas_call(
        body, name="mixout_fwd", grid=(SEQ // tm,),
        in_specs=[pl.BlockSpec((tm, ATTN_WIDTH), row), pl.BlockSpec((tm, SSM_WIDTH), row),
                  pl.BlockSpec((SSM_WIDTH, SSM_WIDTH), const), pl.BlockSpec((1, SSM_WIDTH), const),
                  pl.BlockSpec((1, ATTN_WIDTH), const), pl.BlockSpec((1, SSM_WIDTH), const),
                  pl.BlockSpec((D_MODEL, D_MODEL), const), pl.BlockSpec((tm, D_MODEL), row)],
        out_specs=[pl.BlockSpec((tm, D_MODEL), row), pl.BlockSpec((tm, D_MODEL), row)],
        out_shape=[jax.ShapeDtypeStruct((SEQ, D_MODEL), F32), jax.ShapeDtypeStruct((SEQ, D_MODEL), BF16)],
        compiler_params=_cparams(("parallel",)),
    )(o, y, glu_w, glu_b, gan, gsn, wout, x1)


def _mixout_bwd(dx2, o, y, glu_w, glu_b, gan, gsn, wout):
    tm = MIX_TM

    def body(dx2_ref, o_ref, y_ref, gw_ref, gb_ref, gan_ref, gsn_ref, w_ref,
             do_ref, dy_ref, dz_ref, yg_ref, dxb_ref, dgan_ref, dgsn_ref, dgb_ref):
        i = pl.program_id(0)
        dxb = dx2_ref[...].astype(BF16)
        dxb_ref[...] = dxb
        dmixed = _dot_nt(dxb, w_ref[...])
        do, dgan = _rms_bwd(dmixed[:, :ATTN_WIDTH], o_ref[...], gan_ref[...])
        do_ref[...] = do.T
        yv = y_ref[...]
        yg = _gelu(yv)
        ygb = yg.astype(BF16)
        yg_ref[...] = ygb
        sg = _sigmoid(_dot(ygb, gw_ref[...]) + gb_ref[...])
        dso, dgsn = _rms_bwd(dmixed[:, ATTN_WIDTH:], yg * sg, gsn_ref[...])
        dz = dso * yg * sg * (1.0 - sg)
        dzb = dz.astype(BF16)
        dz_ref[...] = dzb
        dyg = dso * sg + _dot_nt(dzb, gw_ref[...])
        dy_ref[...] = dyg * _gelu_grad(yv)
        dgb = jnp.sum(dz, axis=0, keepdims=True)

        @pl.when(i == 0)
        def _():
            dgan_ref[...] = dgan
            dgsn_ref[...] = dgsn
            dgb_ref[...] = dgb

        @pl.when(i != 0)
        def _():
            dgan_ref[...] += dgan
            dgsn_ref[...] += dgsn
            dgb_ref[...] += dgb

    row = lambda i: (i, 0)
    const = lambda i: (0, 0)
    return pl.pallas_call(
        body, name="mixout_bwd", grid=(SEQ // tm,),
        in_specs=[pl.BlockSpec((tm, D_MODEL), row), pl.BlockSpec((tm, ATTN_WIDTH), row),
                  pl.BlockSpec((tm, SSM_WIDTH), row),
                  pl.BlockSpec((SSM_WIDTH, SSM_WIDTH), const), pl.BlockSpec((1, SSM_WIDTH), const),
                  pl.BlockSpec((1, ATTN_WIDTH), const), pl.BlockSpec((1, SSM_WIDTH), const),
                  pl.BlockSpec((D_MODEL, D_MODEL), const)],
        out_specs=[pl.BlockSpec((ATTN_WIDTH, tm), lambda i: (0, i)), pl.BlockSpec((tm, SSM_WIDTH), row),
                   pl.BlockSpec((tm, SSM_WIDTH), row), pl.BlockSpec((tm, SSM_WIDTH), row),
                   pl.BlockSpec((tm, D_MODEL), row),
                   pl.BlockSpec((1, ATTN_WIDTH), const), pl.BlockSpec((1, SSM_WIDTH), const),
                   pl.BlockSpec((1, SSM_WIDTH), const)],
        out_shape=[jax.ShapeDtypeStruct((ATTN_WIDTH, SEQ), F32), jax.ShapeDtypeStruct((SEQ, SSM_WIDTH), F32),
                   jax.ShapeDtypeStruct((SEQ, SSM_WIDTH), BF16), jax.ShapeDtypeStruct((SEQ, SSM_WIDTH), BF16),
                   jax.ShapeDtypeStruct((SEQ, D_MODEL), BF16),
                   jax.ShapeDtypeStruct((1, ATTN_WIDTH), F32), jax.ShapeDtypeStruct((1, SSM_WIDTH), F32),
                   jax.ShapeDtypeStruct((1, SSM_WIDTH), F32)],
        compiler_params=_cparams(("arbitrary",)),
    )(dx2, o, y, glu_w, glu_b, gan, gsn, wout)


def _local_step(x, target, w, p, late_weights, early_grads, after=None, midway=None):
    x1, h1, a1, b1 = _ffn_fwd(x, p["norm_ffn1"], w["wgt1"], w["wut1"], w["wd1"], "ffn1_fwd", after=after)
    h2, q, k, v, u = _mixin_fwd(x1, p["norm_mix"], w["wint"])

    lam_re = p["ssm_lambda_re"].reshape(2 * N_LANE_BLOCKS, LANES)
    lam_im = p["ssm_lambda_im"].reshape(2 * N_LANE_BLOCKS, LANES)
    log_dt = jnp.repeat(p["ssm_log_dt"].reshape(2, 32), 64, axis=-1).reshape(2 * N_LANE_BLOCKS, LANES)
    a_re, a_im, bb, cc = _ssm_prep(lam_re, lam_im, log_dt, p["ssm_b_re"], p["ssm_b_im"],
                                   p["ssm_c_re"], p["ssm_c_im"])
    shape_a = (2, N_LANE_BLOCKS, 1, LANES)
    a_re4, a_im4 = a_re.reshape(shape_a), a_im.reshape(shape_a)
    bb4 = bb.reshape(2, N_LANE_BLOCKS, LANES, 2 * LANES)
    cc4 = cc.reshape(2, N_LANE_BLOCKS, 2 * LANES, LANES)
    dskip = p["ssm_d"].T.reshape(1, SSM_WIDTH)
    y, xr, xi = _ssm_fwd(u, a_re4, a_im4, bb4, cc4, dskip)
    o = _attn_fwd(q, k, v, p["attn_sinks"], after=None if midway is None else midway(y))

    w2 = late_weights(o)
    x2, mixed = _mixout_fwd(o, y, w2["glu"], p["ssm_glu_b"], p["attn_out_norm"], p["ssm_out_norm"], w2["wout"], x1)
    dx3, h3, a3, b3, loss, d_final = _ffn_fwd(x2, p["norm_ffn2"], w2["wgt2"], w2["wut2"], w2["wd2"], "ffn2_fwd",
                                              head=(p["final_norm"], target))
    dx2, da3, db3, s3, df3, d_n2 = _ffn_bwd_act(dx3, x2, p["norm_ffn2"], a3, b3, w2["wgt2"], w2["wut2"], w2["wd2"],
                                                "ffn2_bwd_act")
    g_wgt2, g_wut2, g_wd2 = _mm_tn([(da3, h3), (db3, h3), (s3, df3)], "ffn2_bwd_w")

    do, dy, dz, ygb, dx2b, d_gan, d_gsn, d_glub = _mixout_bwd(
        dx2, o, y, w2["glu"], p["ssm_glu_b"], p["attn_out_norm"], p["ssm_out_norm"], w2["wout"])
    (g_wout,) = _mm_tn([(mixed, dx2b)], "wout_bwd_w")
    (g_glu,) = _mm_tn([(ygb, dz)], "glu_bwd_w")
    sent = early_grads(dict(glu=g_glu, wout=g_wout, wgt2=g_wgt2, wut2=g_wut2, wd2=g_wd2))

    du, d_dskip, dcc, dbb, dar, dai = _ssm_bwd(dy, u, xr, xi, a_re4, a_im4, bb4, cc4, dskip, after=sent)
    nb = 2 * N_LANE_BLOCKS
    g_lre, g_lim, g_ldt, g_btr, g_bti, g_cre, g_cim = _ssm_prep_bwd(
        lam_re, lam_im, log_dt, p["ssm_b_re"], p["ssm_b_im"], dar.reshape(nb, LANES), dai.reshape(nb, LANES),
        dbb.reshape(nb, LANES, 2 * LANES), dcc.reshape(nb, 2 * LANES, LANES))

    dq, dk, dv, d_sinks = _attn_bwd(q, k, v, p["attn_sinks"], do)
    dx1, dproj, d_nmix = _mixin_bwd(dq, dk, dv, du, w["wint"], x1, p["norm_mix"], dx2)
    (g_wint,) = _mm_tn([(dproj, h2)], "win_bwd_w")

    dx0, da1, db1, s1, df1, d_n1 = _ffn_bwd_act(dx1, x, p["norm_ffn1"], a1, b1, w["wgt1"], w["wut1"], w["wd1"],
                                                "ffn1_bwd_act")
    g_wgt1, g_wut1, g_wd1 = _mm_tn([(da1, h1), (db1, h1), (s1, df1)], "ffn1_bwd_w")

    big = dict(wgt1=g_wgt1, wut1=g_wut1, wd1=g_wd1, wint=g_wint)
    small = dict(
        norm_ffn1=d_n1, norm_mix=d_nmix, attn_sinks=d_sinks,
        ssm_lambda_re=g_lre.reshape(64, 64), ssm_lambda_im=g_lim.reshape(64, 64),
        ssm_log_dt=g_ldt.reshape(2, 32), ssm_b_re=g_btr, ssm_b_im=g_bti, ssm_c_re=g_cre, ssm_c_im=g_cim,
        ssm_d=d_dskip.reshape(32, 16).T, ssm_glu_b=d_glub, attn_out_norm=d_gan, ssm_out_norm=d_gsn,
        norm_ffn2=d_n2, final_norm=d_final, loss=loss)
    return loss, dx0, big, small


BIG = dict(
    wgt1=("ffn1_w_gate", 352, 1024, True), wut1=("ffn1_w_up", 352, 1024, True), wd1=("ffn1_w_down", 352, 1024, False),
    wint=("w_in", 160, 1024, True), glu=("ssm_glu_w", 64, 512, False), wout=("w_out", 128, 1024, False),
    wgt2=("ffn2_w_gate", 352, 1024, True), wut2=("ffn2_w_up", 352, 1024, True), wd2=("ffn2_w_down", 352, 1024, False))

SMALL = dict(
    norm_ffn1=(1, 1024), norm_mix=(1, 1024), attn_sinks=(1, 8), ssm_lambda_re=(64, 64), ssm_lambda_im=(64, 64),
    ssm_log_dt=(2, 32), ssm_b_re=(1024, 64), ssm_b_im=(1024, 64), ssm_c_re=(1024, 64), ssm_c_im=(1024, 64),
    ssm_d=(16, 32), ssm_glu_b=(1, 512), attn_out_norm=(1, 512), ssm_out_norm=(1, 512), norm_ffn2=(1, 1024),
    final_norm=(1, 1024), loss=(1, 128))
SMALL_TRANSPOSED = ("ssm_b_re", "ssm_b_im", "ssm_d")
SMALL_PARAMS = tuple(n for n in SMALL if n != "loss")

SMALL_PAIRS = (("ssm_lambda_re", "ssm_lambda_im"), ("ssm_c_re", "ssm_c_im"), ("ssm_b_re", "ssm_b_im"))
SMALL_VECS = ("norm_ffn1", "norm_mix", "norm_ffn2", "final_norm", "ssm_glu_b", "attn_out_norm", "ssm_out_norm")
SMALL_TILES = ("ssm_log_dt", "attn_sinks", "ssm_d", "loss")


def _small_offsets():
    off, table = 0, {}
    for re, im in SMALL_PAIRS:
        table[re] = table[im] = off
        off += SMALL[re][0]
    for n in SMALL_VECS:
        table[n] = off
        off += SMALL[n][1] // LANES
    for n in SMALL_TILES:
        off = -(-off // 8) * 8
        table[n] = off
        off += SMALL[n][0]
    return table, off


SMALL_OFFSET, SMALL_USED_ROWS = _small_offsets()
SMALL_ROWS = -(-SMALL_USED_ROWS // (8 * N_DEV)) * 8 * N_DEV


def _cast_shards(shards):
    names = list(BIG)

    def body(*refs):
        ins, outs = refs[:len(names)], refs[len(names):]
        for idx in range(len(names)):
            outs[idx][...] = ins[idx][...].astype(BF16)

    return pl.pallas_call(
        body, name="cast_shards",
        out_shape=[jax.ShapeDtypeStruct((BIG[n][1], BIG[n][2]), BF16) for n in names],
        compiler_params=_cparams(),
    )(*[shards[n] for n in names])


def _peer(x, y, c, r):
    px = 1 - x if r & 4 else x
    py = 1 - y if r & 2 else y
    pc = 1 - c if r & 1 else c
    return px, py, pc


FIRST_GROUP = ("wgt1", "wut1", "wd1", "wint")
LATE_GROUP = ("glu", "wout", "wgt2", "wut2", "wd2")
N_PEERS = N_DEV - 1
ANY_SPEC = pl.BlockSpec(memory_space=pl.ANY)
HBM_SPEC = pl.BlockSpec(memory_space=pltpu.HBM)
SEM_SPEC = pl.BlockSpec(memory_space=pltpu.SEMAPHORE)
DATAFLOW_EFFECT = pltpu.SideEffectType.DATAFLOW_SIDE_EFFECTING


def _mesh_pos():
    x, y, c = lax.axis_index("x"), lax.axis_index("y"), lax.axis_index("c")
    return x, y, c, 4 * x + 2 * y + c


def _gather_first(first, late):
    nf, nl = len(first), len(late)

    def body(*refs):
        f_in, l_in = refs[:nf], refs[nf:nf + nl]
        f_out, l_out = refs[nf + nl:2 * nf + nl], refs[2 * nf + nl:2 * (nf + nl)]
        send_sems, recv_sems, local_sems = refs[2 * (nf + nl):]
        x, y, c, me = _mesh_pos()
        sibling = (x, y, 1 - c)
        chips = [(x, 1 - y), (1 - x, y), (1 - x, 1 - y)]

        def idx(px, py, pc):
            return 4 * px + 2 * py + pc

        def copy(k, s, block, to, src=None):
            slot = f_out[k].at[block]
            return pltpu.make_async_remote_copy(
                src_ref=slot if src is None else src, dst_ref=slot, send_sem=send_sems.at[k, s],
                recv_sem=recv_sems.at[k, s], device_id=to, device_id_type=MESH_ID)

        local = []
        for k in range(nf + nl):
            src, dst = (f_in[k], f_out[k]) if k < nf else (l_in[k - nf], l_out[k - nf])
            mine = pltpu.make_async_copy(src, dst.at[me], local_sems.at[k])
            mine.start()
            local.append(mine)
        sends = []
        for j, chip in enumerate(chips):
            for k in range(nf):
                sends.append(copy(k, 1 + j, me, (*chip, c), src=f_in[k]))
                sends[-1].start()
        for k in range(nf):
            sends.append(copy(k, 0, me, sibling, src=f_in[k]))
            sends[-1].start()
        for j, chip in enumerate(chips):
            for k in range(nf):
                copy(k, 1 + j, idx(*chip, c), (*chip, c)).wait_recv()
                sends.append(copy(k, 4 + j, idx(*chip, c), sibling))
                sends[-1].start()
        for k in range(nf):
            copy(k, 0, idx(*sibling), sibling).wait_recv()
        for j, chip in enumerate(chips):
            for k in range(nf):
                copy(k, 4 + j, idx(*chip, 1 - c), sibling).wait_recv()
        for cp in sends:
            cp.wait_send()
        for cp in local:
            cp.wait()

    return pl.pallas_call(
        body, name="gather_first",
        in_specs=[ANY_SPEC] * (nf + nl), out_specs=[ANY_SPEC] * (nf + nl),
        out_shape=[jax.ShapeDtypeStruct((N_DEV,) + s.shape, s.dtype) for s in list(first) + list(late)],
        scratch_shapes=[pltpu.SemaphoreType.DMA((nf, N_PEERS)), pltpu.SemaphoreType.DMA((nf, N_PEERS)),
                        pltpu.SemaphoreType.DMA((nf + nl,))],
        compiler_params=pltpu.CompilerParams(has_side_effects=True),
    )(*first, *late)


def _split_copy(src_refs, land_refs, send_sems, recv_sems, k, r, pos, scatter, receiving):
    x, y, c, me = pos
    px, py, pc = _peer(x, y, c, r)
    peer_idx = 4 * px + 2 * py + pc
    if scatter:
        src, dst = src_refs[k].at[peer_idx], land_refs[k].at[r - 1]
    else:
        src, dst = src_refs[k], land_refs[k].at[peer_idx if receiving else me]
    return pltpu.make_async_remote_copy(
        src_ref=src, dst_ref=dst, send_sem=send_sems.at[k * N_PEERS + r - 1],
        recv_sem=recv_sems.at[k * N_PEERS + r - 1], device_id=(px, py, pc), device_id_type=MESH_ID)


def _split_start(name, srcs, lands, scatter):
    n = len(srcs)

    def body(*refs):
        src_refs, land_refs = refs[:n], refs[n:2 * n]
        send_sems, recv_sems = refs[2 * n], refs[2 * n + 1]
        token = refs[-1]
        pos = _mesh_pos()
        for k in range(n):
            for r in range(1, N_DEV):
                _split_copy(src_refs, land_refs, send_sems, recv_sems, k, r, pos, scatter, False).start()
        token[...] = jnp.zeros_like(token)

    thru = [pltpu.HBM(a.shape, a.dtype) for a in list(srcs) + list(lands)]
    outs = pl.pallas_call(
        body, name=name,
        in_specs=[HBM_SPEC] * (2 * n),
        out_specs=[SEM_SPEC, SEM_SPEC] + [HBM_SPEC] * (2 * n) + [pl.BlockSpec(memory_space=pltpu.VMEM)],
        out_shape=[pltpu.SemaphoreType.DMA((n * N_PEERS,)), pltpu.SemaphoreType.DMA((n * N_PEERS,))] + thru
        + [jax.ShapeDtypeStruct((8, LANES), F32)],
        input_output_aliases={i: 2 + i for i in range(2 * n)},
        compiler_params=pltpu.CompilerParams(has_side_effects=DATAFLOW_EFFECT),
    )(*[pltpu.with_memory_space_constraint(a, pltpu.HBM) for a in list(srcs) + list(lands)])
    return outs[0], outs[1], outs[2:2 + n], outs[2 + n:2 + 2 * n], outs[-1]


def _split_wait(name, send_sems, recv_sems, srcs, lands, scatter, after):
    n = len(srcs)

    def body(*refs):
        src_refs, land_refs = refs[:n], refs[n:2 * n]
        send, recv = refs[2 * n], refs[2 * n + 1]
        pos = _mesh_pos()
        for k in range(n):
            for r in range(1, N_DEV):
                cp = _split_copy(src_refs, land_refs, send, recv, k, r, pos, scatter, True)
                cp.wait_send()
                cp.wait_recv()

    thru = [pltpu.HBM(a.shape, a.dtype) for a in list(srcs) + list(lands)]
    outs = pl.pallas_call(
        body, name=name,
        in_specs=[HBM_SPEC] * (2 * n) + [SEM_SPEC, SEM_SPEC, ANY_SPEC],
        out_specs=[HBM_SPEC] * (2 * n), out_shape=thru,
        input_output_aliases={i: i for i in range(2 * n)},
        compiler_params=pltpu.CompilerParams(has_side_effects=DATAFLOW_EFFECT),
    )(*srcs, *lands, send_sems, recv_sems, after)
    return outs[:n], outs[n:]


def _late_copy(passing, src_refs, land_refs, send_sems, recv_sems, k, s, pos, receiving):
    x, y, c, me = pos
    chips = [(x, 1 - y), (1 - x, y), (1 - x, 1 - y)]
    sibling = (x, y, 1 - c)

    def idx(dev):
        return 4 * dev[0] + 2 * dev[1] + dev[2]

    if passing:
        to = sibling
        block = idx((*chips[s], 1 - c)) if receiving else idx((*chips[s], c))
        src = dst = land_refs[k].at[block]
        sem = k * 3 + s
    else:
        to = sibling if s == 0 else (*chips[s - 1], c)
        src, dst = src_refs[k], land_refs[k].at[idx(to) if receiving else me]
        sem = k * 4 + s
    return pltpu.make_async_remote_copy(src_ref=src, dst_ref=dst, send_sem=send_sems.at[sem],
                                        recv_sem=recv_sems.at[sem], device_id=to, device_id_type=MESH_ID)


def _late_gather_call(name, stage, srcs, lands, sems, after=None):
    n = len(srcs)
    n_sem_in = len(sems)
    has_after = after is not None

    def body(*refs):
        src_refs, land_refs = refs[:n], refs[n:2 * n]
        sem_in = refs[2 * n:2 * n + n_sem_in]
        outs = refs[2 * n + n_sem_in + (1 if has_after else 0):]
        pos = _mesh_pos()
        if stage == 0:
            own_send, own_recv = outs[0], outs[1]
            for s in (1, 2, 3, 0):
                for k in range(n):
                    _late_copy(False, src_refs, land_refs, own_send, own_recv, k, s, pos, False).start()
            outs[-1][...] = jnp.zeros_like(outs[-1])
        elif stage == 1:
            own_recv = sem_in[1]
            pass_send, pass_recv = outs[0], outs[1]
            for s in range(3):
                for k in range(n):
                    _late_copy(False, src_refs, land_refs, sem_in[0], own_recv, k, s + 1, pos, True).wait_recv()
                    _late_copy(True, src_refs, land_refs, pass_send, pass_recv, k, s, pos, False).start()
            outs[-1][...] = jnp.zeros_like(outs[-1])
        else:
            own_send, own_recv, pass_send, pass_recv = sem_in
            for k in range(n):
                _late_copy(False, src_refs, land_refs, own_send, own_recv, k, 0, pos, True).wait_recv()
                for s in range(4):
                    _late_copy(False, src_refs, land_refs, own_send, own_recv, k, s, pos, False).wait_send()
                for s in range(3):
                    cp = _late_copy(True, src_refs, land_refs, pass_send, pass_recv, k, s, pos, True)
                    cp.wait_recv()
                    cp.wait_send()

    thru = [pltpu.HBM(a.shape, a.dtype) for a in list(srcs) + list(lands)]
    new_sems = [[pltpu.SemaphoreType.DMA((n * 4,))] * 2, [pltpu.SemaphoreType.DMA((n * 3,))] * 2, []][stage]
    extra = [] if stage == 2 else [jax.ShapeDtypeStruct((8, LANES), F32)]
    outs = pl.pallas_call(
        body, name=name,
        in_specs=[HBM_SPEC] * (2 * n) + [SEM_SPEC] * n_sem_in + [ANY_SPEC] * has_after,
        out_specs=[SEM_SPEC] * len(new_sems) + [HBM_SPEC] * (2 * n) + [pl.BlockSpec(memory_space=pltpu.VMEM)] * len(extra),
        out_shape=new_sems + thru + extra,
        input_output_aliases={i: len(new_sems) + i for i in range(2 * n)},
        compiler_params=pltpu.CompilerParams(has_side_effects=DATAFLOW_EFFECT),
    )(*[pltpu.with_memory_space_constraint(a, pltpu.HBM) for a in list(srcs) + list(lands)], *sems,
      *([after] if has_after else []))
    ns = len(new_sems)
    return list(outs[:ns]), outs[ns:ns + n], outs[ns + n:ns + 2 * n], (outs[-1] if extra else None)


N_SEND_SLOTS = 3


def _exchange_last(grads, small_packed):
    ng = len(grads)
    ch = SMALL_ROWS // N_DEV
    max_rows = max(g.shape[1] for g in grads)
    cols = grads[0].shape[2]

    def body(*refs):
        g_in, s_in = refs[:ng], refs[ng]
        outs = refs[ng + 1:]
        own_out, land, stage = outs[:ng], outs[ng:2 * ng], outs[2 * ng:3 * ng]
        s_red, s_stage = outs[3 * ng], outs[3 * ng + 1]
        (va, vb, vo, vs, sm_in, sm_out, d2d_send, d2d_recv, ici_send, ici_recv, s1_send, s1_recv, s2_send, s2_recv,
         local_sems) = outs[3 * ng + 2:]
        x, y, c, me = _mesh_pos()
        sibling = (x, y, 1 - c)
        chips = [(x, y), (x, 1 - y), (1 - x, y), (1 - x, 1 - y)]

        def idx(chip, core):
            return 4 * chip[0] + 2 * chip[1] + core

        def d2d(k, j):
            return pltpu.make_async_remote_copy(
                src_ref=g_in[k].at[idx(chips[j], 1 - c)], dst_ref=stage[k].at[j], send_sem=d2d_send.at[k, j],
                recv_sem=d2d_recv.at[k, j], device_id=sibling, device_id_type=MESH_ID)

        def ici(k, j, slot):
            rows = g_in[k].shape[1]
            return pltpu.make_async_remote_copy(
                src_ref=vo.at[slot, pl.ds(0, rows)], dst_ref=land[k].at[j - 1], send_sem=ici_send.at[k, j - 1],
                recv_sem=ici_recv.at[k, j - 1], device_id=(*chips[j], c), device_id_type=MESH_ID)

        def small_scatter(r):
            px, py, pc = _peer(x, y, c, r)
            return pltpu.make_async_remote_copy(
                src_ref=s_in.at[pl.ds(pl.multiple_of((4 * px + 2 * py + pc) * ch, 8), ch)], dst_ref=s_stage.at[me],
                send_sem=s1_send.at[r - 1], recv_sem=s1_recv.at[r - 1], device_id=(px, py, pc), device_id_type=MESH_ID)

        def small_gather(r):
            return pltpu.make_async_remote_copy(
                src_ref=sm_out, dst_ref=s_red.at[me], send_sem=s2_send.at[r - 1], recv_sem=s2_recv.at[r - 1],
                device_id=_peer(x, y, c, r), device_id_type=MESH_ID)

        for r in range(1, N_DEV):
            small_scatter(r).start()
        mine = pltpu.make_async_copy(s_in.at[pl.ds(pl.multiple_of(me * ch, 8), ch)], s_stage.at[me], local_sems.at[0])
        mine.start()
        pairs = [(k, j) for k in range(ng) for j in (1, 2, 3)] + [(k, 0) for k in range(ng)]
        for k, j in pairs:
            d2d(k, j).start()

        def reduce_small():
            for r in range(1, N_DEV):
                small_scatter(r).wait_recv()
            mine.wait()
            load = pltpu.make_async_copy(s_stage, sm_in, local_sems.at[1])
            load.start()
            load.wait()
            total = sm_in[0]
            for i in range(1, N_DEV):
                total = total + sm_in[i]
            sm_out[...] = total
            for r in range(1, N_DEV):
                small_gather(r).start()
            keep = pltpu.make_async_copy(sm_out, s_red.at[me], local_sems.at[2])
            keep.start()
            return keep

        in_flight = {}
        for i, (k, j) in enumerate(pairs):
            if i == N_SEND_SLOTS:
                keep = reduce_small()
            slot = i % N_SEND_SLOTS
            rows = g_in[k].shape[1]
            if slot in in_flight:
                in_flight.pop(slot).wait_send()
            d2d(k, j).wait_recv()
            la = pltpu.make_async_copy(g_in[k].at[idx(chips[j], c)], va.at[pl.ds(0, rows)], local_sems.at[3])
            lb = pltpu.make_async_copy(stage[k].at[j], vb.at[pl.ds(0, rows)], local_sems.at[4])
            la.start()
            lb.start()
            la.wait()
            lb.wait()
            total = va[pl.ds(0, rows)].astype(F32) + vb[pl.ds(0, rows)].astype(F32)
            if j == 0:
                vs[pl.ds(0, rows)] = total
                st = pltpu.make_async_copy(vs.at[pl.ds(0, rows)], own_out[k], local_sems.at[5])
                st.start()
                st.wait()
            else:
                vo[slot, pl.ds(0, rows)] = total.astype(BF16)
                cp = ici(k, j, slot)
                cp.start()
                in_flight[slot] = cp
        for cp in in_flight.values():
            cp.wait_send()

        for j in (1, 2, 3, 0):
            for k in range(ng):
                d2d(k, j).wait_send()
        for j in (1, 2, 3):
            for k in range(ng):
                ici(k, j, 0).wait_recv()
        for r in range(1, N_DEV):
            small_scatter(r).wait_send()
            small_gather(r).wait_send()
            small_gather(r).wait_recv()
        keep.wait()

    out_shape = [jax.ShapeDtypeStruct(g.shape[1:], F32) for g in grads]
    out_shape += [jax.ShapeDtypeStruct((3,) + g.shape[1:], BF16) for g in grads]
    out_shape += [jax.ShapeDtypeStruct((4,) + g.shape[1:], BF16) for g in grads]
    out_shape += [jax.ShapeDtypeStruct((N_DEV, ch, LANES), F32), jax.ShapeDtypeStruct((N_DEV, ch, LANES), F32)]
    outs = pl.pallas_call(
        body, name="exchange_last",
        in_specs=[ANY_SPEC] * (ng + 1), out_specs=[ANY_SPEC] * len(out_shape), out_shape=out_shape,
        scratch_shapes=[pltpu.VMEM((max_rows, cols), BF16), pltpu.VMEM((max_rows, cols), BF16),
                        pltpu.VMEM((N_SEND_SLOTS, max_rows, cols), BF16), pltpu.VMEM((max_rows, cols), F32),
                        pltpu.VMEM((N_DEV, ch, LANES), F32), pltpu.VMEM((ch, LANES), F32),
                        pltpu.SemaphoreType.DMA((ng, 4)), pltpu.SemaphoreType.DMA((ng, 4)),
                        pltpu.SemaphoreType.DMA((ng, 3)), pltpu.SemaphoreType.DMA((ng, 3)),
                        pltpu.SemaphoreType.DMA((N_PEERS,)), pltpu.SemaphoreType.DMA((N_PEERS,)),
                        pltpu.SemaphoreType.DMA((N_PEERS,)), pltpu.SemaphoreType.DMA((N_PEERS,)),
                        pltpu.SemaphoreType.DMA((6,))],
        compiler_params=pltpu.CompilerParams(has_side_effects=True, vmem_limit_bytes=VMEM_LIMIT),
    )(*grads, small_packed)
    return outs[:ng], outs[ng:2 * ng], outs[3 * ng].reshape(SMALL_ROWS, LANES)


def _adamw_math(w, g, m, v):
    m2 = ADAM_B1 * m + (1.0 - ADAM_B1) * g
    v2 = ADAM_B2 * v + (1.0 - ADAM_B2) * (g * g)
    m_hat = m2 / (1.0 - ADAM_B1 ** ADAM_STEP)
    v_hat = v2 / (1.0 - ADAM_B2 ** ADAM_STEP)
    delta = -ADAM_LR * (m_hat / (jnp.sqrt(v_hat) + ADAM_EPS) + ADAM_WD * w)
    return delta, m2, v2


ADAM_ROW_TILES = 2


def _adamw_big(own, parts, w, m, v, name):
    shape = w.shape
    own_is_blocks = own.ndim == 3
    tr = shape[0] // ADAM_ROW_TILES
    n_parts = parts.shape[0]

    def body(own_ref, p_ref, w_ref, m_ref, v_ref, g_ref, d_ref, m2_ref, v2_ref, own_s, sem):
        rows = pl.ds(pl.multiple_of(pl.program_id(0) * tr, 16), tr)
        if own_is_blocks:
            cp = pltpu.make_async_copy(own_ref.at[_mesh_pos()[3], rows], own_s, sem)
        else:
            cp = pltpu.make_async_copy(own_ref.at[rows], own_s, sem)
        cp.start()
        cp.wait()
        g = own_s[...].astype(F32)
        for i in range(n_parts):
            g = g + p_ref[i].astype(F32)
        delta, m2, v2 = _adamw_math(w_ref[...], g, m_ref[...], v_ref[...])
        g_ref[...] = g
        d_ref[...] = delta
        m2_ref[...] = m2
        v2_ref[...] = v2

    tile = pl.BlockSpec((tr, shape[1]), lambda i: (i, 0))
    return pl.pallas_call(
        body, name=name, grid=(ADAM_ROW_TILES,),
        in_specs=[ANY_SPEC, pl.BlockSpec((n_parts, tr, shape[1]), lambda i: (0, i, 0)), tile, tile, tile],
        out_specs=[tile] * 4, out_shape=[jax.ShapeDtypeStruct(shape, F32)] * 4,
        scratch_shapes=[pltpu.VMEM((tr, shape[1]), own.dtype), pltpu.SemaphoreType.DMA(())],
        compiler_params=_cparams(("arbitrary",)),
    )(own, parts, w, m, v)


def _pack_small(grads):
    names = list(SMALL)

    def body(*refs):
        ins, out = dict(zip(names, refs[:-1])), refs[-1]
        out[...] = jnp.zeros_like(out)
        for re, im in SMALL_PAIRS:
            off, rows = SMALL_OFFSET[re], SMALL[re][0]
            out[off:off + rows, :] = jnp.concatenate([ins[re][...], ins[im][...]], axis=1)
        for n in SMALL_VECS:
            off, vec = SMALL_OFFSET[n], ins[n][...]
            for i in range(SMALL[n][1] // LANES):
                out[off + i:off + i + 1, :] = vec[:, i * LANES:(i + 1) * LANES]
        for n in SMALL_TILES:
            off, (rows, cols) = SMALL_OFFSET[n], SMALL[n]
            out[off:off + rows, 0:cols] = ins[n][...]

    return pl.pallas_call(
        body, name="pack_small", out_shape=jax.ShapeDtypeStruct((SMALL_ROWS, LANES), F32),
        compiler_params=_cparams(),
    )(*[grads[n] for n in names])


def _unpack_small_ref(g_ref, n):
    off, (rows, cols) = SMALL_OFFSET[n], SMALL[n]
    for re, im in SMALL_PAIRS:
        if n == re:
            return g_ref[off:off + rows, 0:HALF_LANES]
        if n == im:
            return g_ref[off:off + rows, HALF_LANES:LANES]
    if n in SMALL_VECS:
        return jnp.concatenate([g_ref[off + i:off + i + 1, :] for i in range(cols // LANES)], axis=1)
    return g_ref[off:off + rows, 0:cols]


def _adamw_small(g_packed, w, m, v):
    names = list(SMALL_PARAMS)
    n = len(names)

    def body(g_ref, *refs):
        w_refs, m_refs, v_refs, outs = refs[:n], refs[n:2 * n], refs[2 * n:3 * n], refs[3 * n:]
        for idx, name in enumerate(names):
            g = _unpack_small_ref(g_ref, name)
            delta, m2, v2 = _adamw_math(w_refs[idx][...], g, m_refs[idx][...], v_refs[idx][...])
            outs[4 * idx][...] = g
            outs[4 * idx + 1][...] = delta
            outs[4 * idx + 2][...] = m2
            outs[4 * idx + 3][...] = v2
        outs[4 * n][...] = _unpack_small_ref(g_ref, "loss")

    outs = pl.pallas_call(
        body, name="adamw_small",
        out_shape=[jax.ShapeDtypeStruct(SMALL[name], F32) for name in names for _ in range(4)]
        + [jax.ShapeDtypeStruct(SMALL["loss"], F32)],
        compiler_params=_cparams(),
    )(g_packed, *[w[k] for k in names], *[m[k] for k in names], *[v[k] for k in names])
    return {name: outs[4 * idx:4 * idx + 4] for idx, name in enumerate(names)}, outs[4 * n]


WEIGHT_NAMES = ['norm_ffn1', 'ffn1_w_gate', 'ffn1_w_up', 'ffn1_w_down', 'norm_mix', 'w_in', 'attn_sinks',
                'ssm_lambda_re', 'ssm_lambda_im', 'ssm_log_dt', 'ssm_b_re', 'ssm_b_im', 'ssm_c_re', 'ssm_c_im',
                'ssm_d', 'ssm_glu_w', 'ssm_glu_b', 'attn_out_norm', 'ssm_out_norm', 'w_out', 'norm_ffn2',
                'ffn2_w_gate', 'ffn2_w_up', 'ffn2_w_down', 'final_norm']


def kernel(x, norm_ffn1, ffn1_w_gate, ffn1_w_up, ffn1_w_down, norm_mix, w_in, attn_sinks, ssm_lambda_re, ssm_lambda_im, ssm_log_dt, ssm_b_re, ssm_b_im, ssm_c_re, ssm_c_im, ssm_d, ssm_glu_w, ssm_glu_b, attn_out_norm, ssm_out_norm, w_out, norm_ffn2, ffn2_w_gate, ffn2_w_up, ffn2_w_down, final_norm, loss_target, m_norm_ffn1, m_ffn1_w_gate, m_ffn1_w_up, m_ffn1_w_down, m_norm_mix, m_w_in, m_attn_sinks, m_ssm_lambda_re, m_ssm_lambda_im, m_ssm_log_dt, m_ssm_b_re, m_ssm_b_im, m_ssm_c_re, m_ssm_c_im, m_ssm_d, m_ssm_glu_w, m_ssm_glu_b, m_attn_out_norm, m_ssm_out_norm, m_w_out, m_norm_ffn2, m_ffn2_w_gate, m_ffn2_w_up, m_ffn2_w_down, m_final_norm, v_norm_ffn1, v_ffn1_w_gate, v_ffn1_w_up, v_ffn1_w_down, v_norm_mix, v_w_in, v_attn_sinks, v_ssm_lambda_re, v_ssm_lambda_im, v_ssm_log_dt, v_ssm_b_re, v_ssm_b_im, v_ssm_c_re, v_ssm_c_im, v_ssm_d, v_ssm_glu_w, v_ssm_glu_b, v_attn_out_norm, v_ssm_out_norm, v_w_out, v_norm_ffn2, v_ffn2_w_gate, v_ffn2_w_up, v_ffn2_w_down, v_final_norm):
    args = dict(locals())
    weights = {n: args[n] for n in WEIGHT_NAMES}
    moms = {n: args["m_" + n] for n in WEIGHT_NAMES}
    vars_ = {n: args["v_" + n] for n in WEIGHT_NAMES}

    def shard2d(a, k):
        a = a.reshape(a.shape[-2], a.shape[-1])
        return a.T if BIG[k][3] else a

    def shard_master(a, k):
        return (a.T if BIG[k][3] else a).reshape(weights[BIG[k][0]].shape)

    def blocks(g, k):
        return g.reshape(N_DEV, BIG[k][1], BIG[k][2])

    def full(g, k):
        return g.reshape(N_DEV * BIG[k][1], BIG[k][2])

    shards = dict(zip(BIG, _cast_shards({k: shard2d(weights[BIG[k][0]], k) for k in BIG})))
    nf = len(FIRST_GROUP)
    got = _gather_first([shards[k] for k in FIRST_GROUP], [shards[k] for k in LATE_GROUP])
    w_first = {k: full(g, k) for k, g in zip(FIRST_GROUP, got[:nf])}
    late = {}
    late["own_sems"], late["srcs"], late["lands"], w_token = _late_gather_call(
        "gather_late_start", 0, [shards[k] for k in LATE_GROUP], got[nf:], [])

    def late_pass(dep):
        late["pass_sems"], late["srcs"], late["lands"], token = _late_gather_call(
            "gather_late_pass", 1, late["srcs"], late["lands"], late["own_sems"], after=dep)
        return token

    def late_weights(dep):
        _, _, lands, _ = _late_gather_call("gather_late_wait", 2, late["srcs"], late["lands"],
                                           late["own_sems"] + late["pass_sems"], after=dep)
        return {k: full(g, k) for k, g in zip(LATE_GROUP, lands)}

    early = {}

    def early_grads(g):
        srcs = [blocks(g[k], k) for k in LATE_GROUP]
        lands = [lax.empty((N_PEERS, BIG[k][1], BIG[k][2]), BF16) for k in LATE_GROUP]
        early["send"], early["recv"], early["srcs"], early["lands"], token = _split_start(
            "grads_late_start", srcs, lands, scatter=True)
        return token

    def small2d(a, n):
        if n in SMALL_TRANSPOSED:
            a = jnp.swapaxes(a, -1, -2)
        return a.reshape(SMALL[n])

    def small_master(a, n):
        if n in SMALL_TRANSPOSED:
            shape = weights[n].shape
            return jnp.swapaxes(a.reshape(shape[:-2] + (shape[-1], shape[-2])), -1, -2)
        return a.reshape(weights[n].shape)

    small_p = {n: small2d(weights[n], n) for n in SMALL_PARAMS}
    _, grad_x, g_first, g_small = _local_step(
        x.reshape(SEQ, D_MODEL), loss_target.reshape(SEQ, D_MODEL), w_first, small_p, late_weights, early_grads,
        after=w_token, midway=late_pass)

    own_sums, first_parts, small_grad = _exchange_last([blocks(g_first[k], k) for k in FIRST_GROUP],
                                                       _pack_small(g_small))
    own_late, late_parts = _split_wait("grads_late_wait", early["send"], early["recv"], early["srcs"],
                                       early["lands"], True, small_grad)
    own = dict(zip(FIRST_GROUP + LATE_GROUP, list(own_sums) + list(own_late)))
    parts = dict(zip(FIRST_GROUP + LATE_GROUP, list(first_parts) + list(late_parts)))
    outs = {}
    for k in BIG:
        n = BIG[k][0]
        outs[n] = [shard_master(o, k) for o in
                   _adamw_big(own[k], parts[k], shard2d(weights[n], k), shard2d(moms[n], k), shard2d(vars_[n], k),
                              "adamw_" + n)]
    small_out, loss_row = _adamw_small(small_grad, small_p, {n: small2d(moms[n], n) for n in SMALL_PARAMS},
                                       {n: small2d(vars_[n], n) for n in SMALL_PARAMS})
    for n in SMALL_PARAMS:
        outs[n] = [small_master(o, n) for o in small_out[n]]

    result = [loss_row[0, 0], grad_x.reshape(x.shape)]
    for i in range(4):
        result += [outs[n][i] for n in WEIGHT_NAMES]
    return tuple(result)
```

```python
import functools

import jax
import jax.numpy as jnp
from jax import lax
from jax.experimental import pallas as pl
from jax.experimental.pallas import tpu as pltpu

F32 = jnp.float32
BF16 = jnp.bfloat16

N_DEV = 8
SEQ = 2048
D_MODEL = 1024
D_FF = 2816
ATTN_HEADS = 8
KV_HEADS = 2
HEAD_DIM = 64
ATTN_WIDTH = 512
KV_WIDTH = 128
WINDOW = 128
SSM_WIDTH = 512
IN_WIDTH = 1280
EPS = 1e-6
MASKED_DISTANCE = 1e33
LAMBDA_RE_MAX = -1e-4
LANES = 128
N_LANE_BLOCKS = 16
SCAN_CHUNK = SEQ // 8

ADAM_LR = 0.001
ADAM_B1 = 0.9
ADAM_B2 = 0.999
ADAM_EPS = 1e-08
ADAM_WD = 0.01
ADAM_STEP = 10

VMEM_LIMIT = 60 * 1024 * 1024
MESH_ID = pl.DeviceIdType.MESH


def _cparams(sem=None):
    return pltpu.CompilerParams(dimension_semantics=sem, vmem_limit_bytes=VMEM_LIMIT)


def _dot(a, b):
    return jnp.dot(a, b, preferred_element_type=F32)


def _dot_nt(a, b):
    return lax.dot_general(a, b, (((1,), (1,)), ((), ())), preferred_element_type=F32)


def _dot_tn(a, b):
    return lax.dot_general(a, b, (((0,), (0,)), ((), ())), preferred_element_type=F32)


def _rms_fwd(x, g):
    r = lax.rsqrt(jnp.mean(x * x, axis=-1, keepdims=True) + EPS)
    return x * r * g


def _rms_bwd(dh, x, g):
    r = lax.rsqrt(jnp.mean(x * x, axis=-1, keepdims=True) + EPS)
    xh = x * r
    dg = jnp.sum(dh * xh, axis=0, keepdims=True)
    dxh = dh * g
    dx = r * (dxh - xh * jnp.mean(dxh * xh, axis=-1, keepdims=True))
    return dx, dg


def _sigmoid(x):
    return 1.0 / (1.0 + jnp.exp(-x))


FFN_TM = 512
FFN_TF = 1408


def _ffn_fwd(x, g, wgt, wut, wd, name, after=None, head=None):
    tm, tf = FFN_TM, D_FF
    nj = D_FF // tf
    deps = [] if after is None else [after]
    n_in = len(deps) + (2 if head else 0)

    def body(x_ref, g_ref, wg_ref, wu_ref, wd_ref, *rest):
        i = pl.program_id(0)
        j = pl.program_id(1)
        if head:
            gf_ref, t_ref = rest[len(deps):n_in]
            xo_ref, h_ref, a_ref, b_ref, loss_ref, dgf_ref, h_s, acc = rest[n_in:]
        else:
            xo_ref, h_ref, a_ref, b_ref, h_s, acc = rest[n_in:]

        @pl.when(j == 0)
        def _():
            h = _rms_fwd(x_ref[...], g_ref[...]).astype(BF16)
            h_s[...] = h
            h_ref[...] = h
            acc[...] = jnp.zeros_like(acc)

        h = h_s[...]
        a = _dot_nt(h, wg_ref[...])
        b = _dot_nt(h, wu_ref[...])
        a_ref[...] = a.astype(BF16)
        b_ref[...] = b.astype(BF16)
        s = (a * _sigmoid(a) * b).astype(BF16)
        acc[...] += _dot(s, wd_ref[...])

        @pl.when(j == nj - 1)
        def _():
            xo = x_ref[...] + 0.5 * acc[...]
            if not head:
                xo_ref[...] = xo
                return
            gf = gf_ref[...]
            err = _rms_fwd(xo, gf) - t_ref[...]
            part = jnp.broadcast_to(0.5 * jnp.sum(err * err) / D_MODEL, (1, LANES))
            dx, dgf = _rms_bwd(err * (1.0 / D_MODEL), xo, gf)
            xo_ref[...] = dx

            @pl.when(i == 0)
            def _():
                loss_ref[...] = part
                dgf_ref[...] = dgf

            @pl.when(i != 0)
            def _():
                loss_ref[...] += part
                dgf_ref[...] += dgf

    row = lambda i, j: (i, 0)
    const = lambda i, j: (0, 0)
    head_in = [pl.BlockSpec((1, D_MODEL), const), pl.BlockSpec((tm, D_MODEL), row)] if head else []
    head_out = [pl.BlockSpec((1, LANES), const), pl.BlockSpec((1, D_MODEL), const)] if head else []
    head_shape = [jax.ShapeDtypeStruct((1, LANES), F32), jax.ShapeDtypeStruct((1, D_MODEL), F32)] if head else []
    return pl.pallas_call(
        body, name=name, grid=(SEQ // tm, nj),
        in_specs=[pl.BlockSpec((tm, D_MODEL), row), pl.BlockSpec((1, D_MODEL), const),
                  pl.BlockSpec((tf, D_MODEL), lambda i, j: (j, 0), pipeline_mode=pl.Buffered(1)),
                  pl.BlockSpec((tf, D_MODEL), lambda i, j: (j, 0), pipeline_mode=pl.Buffered(1)),
                  pl.BlockSpec((tf, D_MODEL), lambda i, j: (j, 0), pipeline_mode=pl.Buffered(1))]
        + [pl.BlockSpec(memory_space=pl.ANY)] * len(deps) + head_in,
        out_specs=[pl.BlockSpec((tm, D_MODEL), row), pl.BlockSpec((tm, D_MODEL), row),
                   pl.BlockSpec((tm, tf), lambda i, j: (i, j)),
                   pl.BlockSpec((tm, tf), lambda i, j: (i, j))] + head_out,
        out_shape=[jax.ShapeDtypeStruct((SEQ, D_MODEL), F32), jax.ShapeDtypeStruct((SEQ, D_MODEL), BF16),
                   jax.ShapeDtypeStruct((SEQ, D_FF), BF16), jax.ShapeDtypeStruct((SEQ, D_FF), BF16)] + head_shape,
        scratch_shapes=[pltpu.VMEM((tm, D_MODEL), BF16), pltpu.VMEM((tm, D_MODEL), F32)],
        compiler_params=_cparams(("arbitrary" if head else "parallel", "arbitrary")),
    )(x, g, wgt, wut, wd, *deps, *(head or ()))


def _ffn_bwd_act(dxo, x, g, a, b, wgt, wut, wd, name):
    tm, tf = FFN_TM // 2, D_FF
    nj = D_FF // tf
    resident = pl.Buffered(1)

    def body(dxo_ref, x_ref, g_ref, a_ref, b_ref, wg_ref, wu_ref, wd_ref,
             dx_ref, da_ref, db_ref, s_ref, df_ref, dg_ref, df_s, acc):
        i = pl.program_id(0)
        j = pl.program_id(1)

        @pl.when(j == 0)
        def _():
            df = (0.5 * dxo_ref[...]).astype(BF16)
            df_s[...] = df
            df_ref[...] = df
            acc[...] = jnp.zeros_like(acc)

        ds = _dot_nt(df_s[...], wd_ref[...])
        av = a_ref[...].astype(F32)
        bv = b_ref[...].astype(F32)
        sig = _sigmoid(av)
        sl = av * sig
        s_ref[...] = (sl * bv).astype(BF16)
        db = (ds * sl).astype(BF16)
        da = (ds * bv * (sig * (1.0 + av * (1.0 - sig)))).astype(BF16)
        da_ref[...] = da
        db_ref[...] = db
        acc[...] += _dot(da, wg_ref[...]) + _dot(db, wu_ref[...])

        @pl.when(j == nj - 1)
        def _():
            dx, dg = _rms_bwd(acc[...], x_ref[...], g_ref[...])
            dx_ref[...] = dxo_ref[...] + dx

            @pl.when(i == 0)
            def _():
                dg_ref[...] = dg

            @pl.when(i != 0)
            def _():
                dg_ref[...] += dg

    row = lambda i, j: (i, 0)
    col = lambda i, j: (j, 0)
    tile = lambda i, j: (i, j)
    return pl.pallas_call(
        body, name=name, grid=(SEQ // tm, nj),
        in_specs=[pl.BlockSpec((tm, D_MODEL), row), pl.BlockSpec((tm, D_MODEL), row),
                  pl.BlockSpec((1, D_MODEL), lambda i, j: (0, 0)),
                  pl.BlockSpec((tm, tf), tile), pl.BlockSpec((tm, tf), tile),
                  pl.BlockSpec((tf, D_MODEL), col, pipeline_mode=resident),
                  pl.BlockSpec((tf, D_MODEL), col, pipeline_mode=resident),
                  pl.BlockSpec((tf, D_MODEL), col, pipeline_mode=resident)],
        out_specs=[pl.BlockSpec((tm, D_MODEL), row),
                   pl.BlockSpec((tm, tf), tile), pl.BlockSpec((tm, tf), tile), pl.BlockSpec((tm, tf), tile),
                   pl.BlockSpec((tm, D_MODEL), row),
                   pl.BlockSpec((1, D_MODEL), lambda i, j: (0, 0))],
        out_shape=[jax.ShapeDtypeStruct((SEQ, D_MODEL), F32),
                   jax.ShapeDtypeStruct((SEQ, D_FF), BF16), jax.ShapeDtypeStruct((SEQ, D_FF), BF16),
                   jax.ShapeDtypeStruct((SEQ, D_FF), BF16),
                   jax.ShapeDtypeStruct((SEQ, D_MODEL), BF16),
                   jax.ShapeDtypeStruct((1, D_MODEL), F32)],
        scratch_shapes=[pltpu.VMEM((tm, D_MODEL), BF16), pltpu.VMEM((tm, D_MODEL), F32)],
        compiler_params=_cparams(("arbitrary", "arbitrary")),
    )(dxo, x, g, a, b, wgt, wut, wd)


def _mm_tn(pairs, name, tmm=256):
    m = pairs[0][0].shape[1]
    n_pairs = len(pairs)

    def body(*refs):
        ins, outs = refs[:2 * n_pairs], refs[2 * n_pairs:]
        for p in range(n_pairs):
            outs[p][...] = _dot_tn(ins[2 * p][...], ins[2 * p + 1][...]).astype(BF16)

    in_specs, out_specs, out_shape, args = [], [], [], []
    for a, b in pairs:
        n = b.shape[1]
        in_specs += [pl.BlockSpec((SEQ, tmm), lambda i: (0, i)), pl.BlockSpec((SEQ, n), lambda i: (0, 0))]
        out_specs.append(pl.BlockSpec((tmm, n), lambda i: (i, 0)))
        out_shape.append(jax.ShapeDtypeStruct((m, n), BF16))
        args += [a, b]
    return pl.pallas_call(body, name=name, grid=(m // tmm,), in_specs=in_specs, out_specs=out_specs,
                          out_shape=out_shape, compiler_params=_cparams(("parallel",)))(*args)


MIX_TM = 512


def _mixin_fwd(x, g, wint):
    tm = MIX_TM

    def body(x_ref, g_ref, w_ref, h_ref, q_ref, k_ref, v_ref, u_ref):
        h = _rms_fwd(x_ref[...], g_ref[...]).astype(BF16)
        h_ref[...] = h
        proj = _dot_nt(h, w_ref[...])
        q_ref[...] = proj[:, :ATTN_WIDTH].T
        k_ref[...] = proj[:, ATTN_WIDTH:ATTN_WIDTH + KV_WIDTH]
        v_ref[...] = proj[:, ATTN_WIDTH + KV_WIDTH:ATTN_WIDTH + 2 * KV_WIDTH]
        u_ref[...] = proj[:, ATTN_WIDTH + 2 * KV_WIDTH:]

    row = lambda i: (i, 0)
    return pl.pallas_call(
        body, name="mixin_fwd", grid=(SEQ // tm,),
        in_specs=[pl.BlockSpec((tm, D_MODEL), row), pl.BlockSpec((1, D_MODEL), lambda i: (0, 0)),
                  pl.BlockSpec((IN_WIDTH, D_MODEL), lambda i: (0, 0))],
        out_specs=[pl.BlockSpec((tm, D_MODEL), row), pl.BlockSpec((ATTN_WIDTH, tm), lambda i: (0, i)),
                   pl.BlockSpec((tm, KV_WIDTH), row), pl.BlockSpec((tm, KV_WIDTH), row),
                   pl.BlockSpec((tm, SSM_WIDTH), row)],
        out_shape=[jax.ShapeDtypeStruct((SEQ, D_MODEL), BF16), jax.ShapeDtypeStruct((ATTN_WIDTH, SEQ), F32),
                   jax.ShapeDtypeStruct((SEQ, KV_WIDTH), F32), jax.ShapeDtypeStruct((SEQ, KV_WIDTH), F32),
                   jax.ShapeDtypeStruct((SEQ, SSM_WIDTH), F32)],
        compiler_params=_cparams(("parallel",)),
    )(x, g, wint)


def _mixin_bwd(dqt, dk, dv, du, wint, x, g, dres):
    tm = MIX_TM

    def body(dq_ref, dk_ref, dv_ref, du_ref, w_ref, x_ref, g_ref, dres_ref, dx_ref, dp_ref, dg_ref):
        i = pl.program_id(0)
        dp = jnp.concatenate([dq_ref[...].T, dk_ref[...], dv_ref[...], du_ref[...]], axis=-1).astype(BF16)
        dp_ref[...] = dp
        dh = _dot(dp, w_ref[...])
        dx, dg = _rms_bwd(dh, x_ref[...], g_ref[...])
        dx_ref[...] = dres_ref[...] + dx

        @pl.when(i == 0)
        def _():
            dg_ref[...] = dg

        @pl.when(i != 0)
        def _():
            dg_ref[...] += dg

    row = lambda i: (i, 0)
    const = lambda i: (0, 0)
    return pl.pallas_call(
        body, name="mixin_bwd", grid=(SEQ // tm,),
        in_specs=[pl.BlockSpec((ATTN_WIDTH, tm), lambda i: (0, i)), pl.BlockSpec((tm, KV_WIDTH), row),
                  pl.BlockSpec((tm, KV_WIDTH), row), pl.BlockSpec((tm, SSM_WIDTH), row),
                  pl.BlockSpec((IN_WIDTH, D_MODEL), const), pl.BlockSpec((tm, D_MODEL), row),
                  pl.BlockSpec((1, D_MODEL), const), pl.BlockSpec((tm, D_MODEL), row)],
        out_specs=[pl.BlockSpec((tm, D_MODEL), row), pl.BlockSpec((tm, IN_WIDTH), row),
                   pl.BlockSpec((1, D_MODEL), const)],
        out_shape=[jax.ShapeDtypeStruct((SEQ, D_MODEL), F32), jax.ShapeDtypeStruct((SEQ, IN_WIDTH), BF16),
                   jax.ShapeDtypeStruct((1, D_MODEL), F32)],
        compiler_params=_cparams(("arbitrary",)),
    )(dqt, dk, dv, du, wint, x, g, dres)


N_QBLOCKS = SEQ // WINDOW
GROUP = ATTN_HEADS // KV_HEADS
SCALE = HEAD_DIM ** -0.5


def _alibi_slope(h):
    return 2.0 ** (-8.0 * (h + 1) / ATTN_HEADS)


def _window_masks(n):
    s_idx = lax.broadcasted_iota(jnp.int32, (3 * WINDOW, WINDOW), 0)
    t_idx = lax.broadcasted_iota(jnp.int32, (3 * WINDOW, WINDOW), 1)
    absrel = jnp.abs(s_idx - WINDOW - t_idx)
    key_pos = n * WINDOW - WINDOW + s_idx
    valid = (absrel <= WINDOW) & (key_pos >= 0) & (key_pos < SEQ)
    return jnp.where(valid, absrel.astype(F32), MASKED_DISTANCE)


def _group_cols(ref, r0, gi):
    return jnp.concatenate(
        [ref[(gi * GROUP + hh) * HEAD_DIM:(gi * GROUP + hh + 1) * HEAD_DIM, pl.ds(r0, WINDOW)].astype(BF16)
         for hh in range(GROUP)], axis=1)


def _group_probs(qgt, kw, dist, gi, sk_ref):
    bias = jnp.concatenate([-_alibi_slope(gi * GROUP + hh) * dist for hh in range(GROUP)], axis=1)
    sink = jnp.concatenate([jnp.full((1, WINDOW), sk_ref[0, gi * GROUP + hh], F32) for hh in range(GROUP)], axis=1)
    s = _dot(kw, qgt) * SCALE + bias
    m = jnp.maximum(jnp.max(s, axis=0, keepdims=True), sink)
    p = jnp.exp(s - m)
    ps = jnp.exp(sink - m)
    inv = 1.0 / (jnp.sum(p, axis=0, keepdims=True) + ps)
    return p * inv, ps * inv


def _pad_window(src_ref, dst_ref):
    zeros = jnp.zeros((WINDOW, KV_WIDTH), BF16)
    dst_ref[0:WINDOW, :] = zeros
    dst_ref[WINDOW + SEQ:, :] = zeros
    dst_ref[WINDOW:WINDOW + SEQ, :] = src_ref[...].astype(BF16)


def _attn_fwd(qt, k, v, sinks, after=None):
    deps = [] if after is None else [after]

    def body(sk_ref, qt_ref, k_ref, v_ref, *rest):
        o_ref, kp_ref, vp_ref = rest[len(deps):]
        _pad_window(k_ref, kp_ref)
        _pad_window(v_ref, vp_ref)

        def blk(n, carry):
            r0 = pl.multiple_of(n * WINDOW, WINDOW)
            dist = _window_masks(n)
            for gi in range(KV_HEADS):
                kw = kp_ref[pl.ds(r0, 3 * WINDOW), gi * HEAD_DIM:(gi + 1) * HEAD_DIM]
                vw = vp_ref[pl.ds(r0, 3 * WINDOW), gi * HEAD_DIM:(gi + 1) * HEAD_DIM]
                pr, _ = _group_probs(_group_cols(qt_ref, r0, gi), kw, dist, gi, sk_ref)
                og = _dot_tn(pr.astype(BF16), vw)
                for hh in range(GROUP):
                    h = gi * GROUP + hh
                    o_ref[pl.ds(r0, WINDOW), h * HEAD_DIM:(h + 1) * HEAD_DIM] = og[hh * WINDOW:(hh + 1) * WINDOW]
            return carry

        lax.fori_loop(0, N_QBLOCKS, blk, 0)

    vmem = pl.BlockSpec(memory_space=pltpu.VMEM)
    return pl.pallas_call(
        body, name="attn_fwd",
        in_specs=[pl.BlockSpec(memory_space=pltpu.SMEM), vmem, vmem, vmem]
        + [pl.BlockSpec(memory_space=pl.ANY)] * len(deps), out_specs=vmem,
        out_shape=jax.ShapeDtypeStruct((SEQ, ATTN_WIDTH), F32),
        scratch_shapes=[pltpu.VMEM((SEQ + 2 * WINDOW, KV_WIDTH), BF16)] * 2,
        compiler_params=_cparams(),
    )(sinks, qt, k, v, *deps)


def _attn_bwd(qt, k, v, sinks, dot_):
    def body(sk_ref, qt_ref, k_ref, v_ref, dot_ref, dqt_ref, dk_ref, dv_ref, dsk_ref,
             dsk_acc, kp_ref, vp_ref, dkp_ref, dvp_ref):
        _pad_window(k_ref, kp_ref)
        _pad_window(v_ref, vp_ref)
        dkp_ref[...] = jnp.zeros_like(dkp_ref)
        dvp_ref[...] = jnp.zeros_like(dvp_ref)
        dsk_acc[...] = jnp.zeros_like(dsk_acc)

        def blk(n, carry):
            r0 = pl.multiple_of(n * WINDOW, WINDOW)
            dist = _window_masks(n)
            for gi in range(KV_HEADS):
                gcols = slice(gi * HEAD_DIM, (gi + 1) * HEAD_DIM)
                kw = kp_ref[pl.ds(r0, 3 * WINDOW), gcols]
                vw = vp_ref[pl.ds(r0, 3 * WINDOW), gcols]
                qgt = _group_cols(qt_ref, r0, gi)
                dogt = _group_cols(dot_ref, r0, gi)
                pr, psink = _group_probs(qgt, kw, dist, gi, sk_ref)
                dp = _dot(vw, dogt)
                delta = jnp.sum(pr * dp, axis=0, keepdims=True)
                ds = (pr * (dp - delta)).astype(BF16)
                dsk_acc[gi:gi + 1, :] += -(psink * delta)
                dqgt = _dot_tn(kw, ds) * SCALE
                for hh in range(GROUP):
                    h = gi * GROUP + hh
                    dqt_ref[h * HEAD_DIM:(h + 1) * HEAD_DIM, pl.ds(r0, WINDOW)] = dqgt[:, hh * WINDOW:(hh + 1) * WINDOW]
                dkp_ref[pl.ds(r0, 3 * WINDOW), gcols] += _dot_nt(ds, qgt) * SCALE
                dvp_ref[pl.ds(r0, 3 * WINDOW), gcols] += _dot_nt(pr.astype(BF16), dogt)
            return carry

        lax.fori_loop(0, N_QBLOCKS, blk, 0)
        for h in range(ATTN_HEADS):
            gi, hh = divmod(h, GROUP)
            dsk_ref[:, h:h + 1] = jnp.sum(dsk_acc[gi:gi + 1, hh * WINDOW:(hh + 1) * WINDOW], axis=1, keepdims=True)
        dk_ref[...] = dkp_ref[WINDOW:WINDOW + SEQ, :]
        dv_ref[...] = dvp_ref[WINDOW:WINDOW + SEQ, :]

    vmem = pl.BlockSpec(memory_space=pltpu.VMEM)
    padded = (SEQ + 2 * WINDOW, KV_WIDTH)
    return pl.pallas_call(
        body, name="attn_bwd",
        in_specs=[pl.BlockSpec(memory_space=pltpu.SMEM), vmem, vmem, vmem, vmem],
        out_specs=[vmem, vmem, vmem, vmem],
        out_shape=[jax.ShapeDtypeStruct((ATTN_WIDTH, SEQ), F32),
                   jax.ShapeDtypeStruct((SEQ, KV_WIDTH), F32), jax.ShapeDtypeStruct((SEQ, KV_WIDTH), F32),
                   jax.ShapeDtypeStruct((1, ATTN_HEADS), F32)],
        scratch_shapes=[pltpu.VMEM((KV_HEADS, GROUP * WINDOW), F32), pltpu.VMEM(padded, BF16),
                        pltpu.VMEM(padded, BF16), pltpu.VMEM(padded, F32), pltpu.VMEM(padded, F32)],
        compiler_params=_cparams(),
    )(sinks, qt, k, v, dot_)


HALF_LANES = LANES // 2
BLOCK_ROWS = 32


def _embed_block(bt, q):
    z = jnp.zeros((16, HALF_LANES), bt.dtype)
    blk = jnp.concatenate([jnp.concatenate([bt[:16], z], axis=1), jnp.concatenate([z, bt[16:]], axis=1)], axis=0)
    parts = [jnp.zeros((BLOCK_ROWS * q, LANES), bt.dtype)] if q else []
    parts.append(blk)
    if q < 3:
        parts.append(jnp.zeros((BLOCK_ROWS * (3 - q), LANES), bt.dtype))
    return jnp.concatenate(parts, axis=0)


def _extract_block(m, q):
    blk = m[BLOCK_ROWS * q:BLOCK_ROWS * (q + 1)]
    return jnp.concatenate([blk[:16, :HALF_LANES], blk[16:, HALF_LANES:]], axis=0)


def _ssm_prep(lam_re, lam_im, log_dt, bt_re, bt_im, c_re, c_im):
    nb = 2 * N_LANE_BLOCKS

    def body(lr_ref, li_ref, ldt_ref, btr_ref, bti_ref, ctr_ref, cti_ref, ar_ref, ai_ref, bb_ref, cc_ref):
        lr = jnp.minimum(lr_ref[...], LAMBDA_RE_MAX)
        li = li_ref[...]
        dt = jnp.exp(ldt_ref[...])
        mag = jnp.exp(lr * dt)
        ar = mag * jnp.cos(li * dt)
        ai = mag * jnp.sin(li * dt)
        den = lr * lr + li * li
        cr = ((ar - 1.0) * lr + ai * li) / den
        ci = (ai * lr - (ar - 1.0) * li) / den
        ar_ref[...] = ar
        ai_ref[...] = ai
        for i in range(nb):
            q = i % 4
            rows = slice(BLOCK_ROWS * i, BLOCK_ROWS * (i + 1))
            br = _embed_block(btr_ref[rows, :], q)
            bi = _embed_block(bti_ref[rows, :], q)
            cri, cii = cr[i:i + 1, :], ci[i:i + 1, :]
            bb_ref[i] = jnp.concatenate([cri * br - cii * bi, cri * bi + cii * br], axis=1).astype(BF16)
            cc_ref[i] = jnp.concatenate([_embed_block(ctr_ref[rows, :], q).T,
                                         -_embed_block(cti_ref[rows, :], q).T], axis=0).astype(BF16)

    return pl.pallas_call(
        body, name="ssm_prep",
        out_shape=[jax.ShapeDtypeStruct((nb, LANES), F32), jax.ShapeDtypeStruct((nb, LANES), F32),
                   jax.ShapeDtypeStruct((nb, LANES, 2 * LANES), BF16),
                   jax.ShapeDtypeStruct((nb, 2 * LANES, LANES), BF16)],
        compiler_params=_cparams(),
    )(lam_re, lam_im, log_dt, bt_re, bt_im, c_re, c_im)


def _ssm_prep_bwd(lam_re, lam_im, log_dt, bt_re, bt_im, dar, dai, dbb, dcc):
    nb = 2 * N_LANE_BLOCKS

    def body(lr_ref, li_ref, ldt_ref, btr_ref, bti_ref, dar_ref, dai_ref, dbb_ref, dcc_ref,
             glr_ref, gli_ref, gdt_ref, gbr_ref, gbi_ref, gcre_ref, gcim_ref, gcr_s, gci_s):
        lam = lr_ref[...]
        lr = jnp.minimum(lam, LAMBDA_RE_MAX)
        li = li_ref[...]
        dt = jnp.exp(ldt_ref[...])
        mag = jnp.exp(lr * dt)
        cs = jnp.cos(li * dt)
        sn = jnp.sin(li * dt)
        ar = mag * cs
        ai = mag * sn
        den = lr * lr + li * li
        nr = (ar - 1.0) * lr + ai * li
        ni = ai * lr - (ar - 1.0) * li
        cr = nr / den
        ci = ni / den
        for i in range(nb):
            q = i % 4
            rows = slice(BLOCK_ROWS * i, BLOCK_ROWS * (i + 1))
            br = _embed_block(btr_ref[rows, :], q)
            bi = _embed_block(bti_ref[rows, :], q)
            gbbr = dbb_ref[i, :, :LANES]
            gbbi = dbb_ref[i, :, LANES:]
            cri, cii = cr[i:i + 1, :], ci[i:i + 1, :]
            gcr_s[i:i + 1, :] = jnp.sum(gbbr * br + gbbi * bi, axis=0, keepdims=True)
            gci_s[i:i + 1, :] = jnp.sum(gbbi * br - gbbr * bi, axis=0, keepdims=True)
            gbr_ref[rows, :] = _extract_block(cri * gbbr + cii * gbbi, q)
            gbi_ref[rows, :] = _extract_block(cri * gbbi - cii * gbbr, q)
            gcre_ref[rows, :] = _extract_block(dcc_ref[i, :LANES, :].T, q)
            gcim_ref[rows, :] = -_extract_block(dcc_ref[i, LANES:, :].T, q)
        g_cr = gcr_s[...]
        g_ci = gci_s[...]
        g_nr = g_cr / den
        g_ni = g_ci / den
        g_den = -(g_cr * nr + g_ci * ni) / (den * den)
        g_ar = dar_ref[...] + g_nr * lr - g_ni * li
        g_ai = dai_ref[...] + g_nr * li + g_ni * lr
        g_lr = g_nr * (ar - 1.0) + g_ni * ai + g_den * 2.0 * lr
        g_li = g_nr * ai - g_ni * (ar - 1.0) + g_den * 2.0 * li
        g_mag = g_ar * cs + g_ai * sn
        g_th = (g_ai * cs - g_ar * sn) * mag
        g_lr = g_lr + g_mag * mag * dt
        g_li = g_li + g_th * dt
        g_dt = g_mag * mag * lr + g_th * li
        glr_ref[...] = jnp.where(lam < LAMBDA_RE_MAX, g_lr, 0.0)
        gli_ref[...] = g_li
        gl = g_dt * dt
        half = LANES // 2
        gdt_ref[:, 0:1] = jnp.sum(gl[:, :half], axis=1, keepdims=True)
        gdt_ref[:, 1:2] = jnp.sum(gl[:, half:], axis=1, keepdims=True)

    rows_shape = jax.ShapeDtypeStruct((nb * BLOCK_ROWS, HALF_LANES), F32)
    return pl.pallas_call(
        body, name="ssm_prep_bwd",
        out_shape=[jax.ShapeDtypeStruct((nb, LANES), F32), jax.ShapeDtypeStruct((nb, LANES), F32),
                   jax.ShapeDtypeStruct((nb, 2), F32), rows_shape, rows_shape, rows_shape, rows_shape],
        scratch_shapes=[pltpu.VMEM((nb, LANES), F32), pltpu.VMEM((nb, LANES), F32)],
        compiler_params=_cparams(),
    )(lam_re, lam_im, log_dt, bt_re, bt_im, dar, dai, dbb, dcc)


def _cmul(ar, ai, br, bi):
    return ar * br - ai * bi, ar * bi + ai * br


def _interleave_rows(src_ref, dst_ref):
    def step(j, carry):
        dst_ref[pl.ds(pl.multiple_of(j * 8, 8), 8), :] = src_ref[pl.ds(j, 8, stride=SCAN_CHUNK), :]
        return carry
    lax.fori_loop(0, SCAN_CHUNK, step, 0, unroll=4)


def _deinterleave_rows(src_ref, dst_ref):
    def step(j, carry):
        dst_ref[pl.ds(j, 8, stride=SCAN_CHUNK), :] = src_ref[pl.ds(pl.multiple_of(j * 8, 8), 8), :]
        return carry
    lax.fori_loop(0, SCAN_CHUNK, step, 0, unroll=4)


def _scan_inplace(re_ref, im_ref, a_re, a_im, reverse):
    nq = len(a_re)
    ch = SCAN_CHUNK
    ab_re = [jnp.broadcast_to(a, (8, LANES)) for a in a_re]
    ab_im = [jnp.broadcast_to(a, (8, LANES)) for a in a_im]

    def rows(j):
        jj = (ch - 1 - j) if reverse else j
        return pl.ds(pl.multiple_of(jj * 8, 8), 8)

    def sweep(init, store):
        def step(j, st):
            out = []
            r = rows(j)
            for qi in range(nq):
                xr, xi = st[2 * qi], st[2 * qi + 1]
                pr, pi = _cmul(ab_re[qi], ab_im[qi], xr, xi)
                xr = pr + re_ref[qi, r, :]
                xi = pi + im_ref[qi, r, :]
                if store:
                    re_ref[qi, r, :] = xr
                    im_ref[qi, r, :] = xi
                out += [xr, xi]
            return tuple(out)
        return lax.fori_loop(0, ch, step, tuple(init), unroll=2)

    zeros = [jnp.zeros((8, LANES), F32)] * (2 * nq)
    finals = sweep(zeros, store=False)

    row_id = lax.broadcasted_iota(jnp.int32, (8, LANES), 0)
    carries = []
    for qi in range(nq):
        pr, pi = ab_re[qi], ab_im[qi]
        for _ in range(8):
            pr, pi = _cmul(pr, pi, pr, pi)
        fr, fi = finals[2 * qi], finals[2 * qi + 1]
        sr = jnp.zeros((8, LANES), F32)
        si = jnp.zeros((8, LANES), F32)
        for _ in range(7):
            tr, ti = _cmul(pr, pi, sr, si)
            tr, ti = tr + fr, ti + fi
            if reverse:
                sr = jnp.where(row_id == 7, 0.0, pltpu.roll(tr, 7, axis=0))
                si = jnp.where(row_id == 7, 0.0, pltpu.roll(ti, 7, axis=0))
            else:
                sr = jnp.where(row_id == 0, 0.0, pltpu.roll(tr, 1, axis=0))
                si = jnp.where(row_id == 0, 0.0, pltpu.roll(ti, 1, axis=0))
        carries += [sr, si]
    sweep(carries, store=True)


SSM_Q = 4


def _ssm_fwd(u, are, aim, bb, cc, dskip, after=None):
    nq = SSM_Q
    deps = [] if after is None else [after]

    def body(u_ref, ar_ref, ai_ref, bb_ref, cc_ref, d_ref, *rest):
        y_ref, xr_ref, xi_ref, sre, sim, up, yp = rest[len(deps):]
        _interleave_rows(u_ref, up)
        uf = up[...]
        ub = uf.astype(BF16)
        yp[...] = d_ref[...] * uf
        for d in range(2):
            for qi in range(nq):
                sre[qi] = _dot(ub, bb_ref[d, qi, :, :LANES])
                sim[qi] = _dot(ub, bb_ref[d, qi, :, LANES:])
            _scan_inplace(sre, sim, [ar_ref[d, qi] for qi in range(nq)], [ai_ref[d, qi] for qi in range(nq)],
                          reverse=(d == 1))
            for qi in range(nq):
                xrb = sre[qi].astype(BF16)
                xib = sim[qi].astype(BF16)
                xr_ref[d, qi] = xrb
                xi_ref[d, qi] = xib
                yp[...] += _dot(xrb, cc_ref[d, qi, :LANES, :]) + _dot(xib, cc_ref[d, qi, LANES:, :])
        _deinterleave_rows(yp, y_ref)

    blk4 = lambda k: (0, k, 0, 0)
    return pl.pallas_call(
        body, name="ssm_fwd", grid=(SSM_WIDTH // LANES,),
        in_specs=[pl.BlockSpec((SEQ, LANES), lambda k: (0, k)),
                  pl.BlockSpec((2, nq, 1, LANES), blk4), pl.BlockSpec((2, nq, 1, LANES), blk4),
                  pl.BlockSpec((2, nq, LANES, 2 * LANES), blk4), pl.BlockSpec((2, nq, 2 * LANES, LANES), blk4),
                  pl.BlockSpec((1, LANES), lambda k: (0, k))] + [pl.BlockSpec(memory_space=pl.ANY)] * len(deps),
        out_specs=[pl.BlockSpec((SEQ, LANES), lambda k: (0, k)),
                   pl.BlockSpec((2, nq, SEQ, LANES), blk4), pl.BlockSpec((2, nq, SEQ, LANES), blk4)],
        out_shape=[jax.ShapeDtypeStruct((SEQ, SSM_WIDTH), F32),
                   jax.ShapeDtypeStruct((2, N_LANE_BLOCKS, SEQ, LANES), BF16),
                   jax.ShapeDtypeStruct((2, N_LANE_BLOCKS, SEQ, LANES), BF16)],
        scratch_shapes=[pltpu.VMEM((nq, SEQ, LANES), F32), pltpu.VMEM((nq, SEQ, LANES), F32),
                        pltpu.VMEM((SEQ, LANES), F32), pltpu.VMEM((SEQ, LANES), F32)],
        compiler_params=_cparams(("parallel",)),
    )(u, are, aim, bb, cc, dskip, *deps)


def _ssm_bwd(dy, u, xr, xi, are, aim, bb, cc, dskip, after=None):
    nq = SSM_Q
    body_rows = SEQ - 8
    deps = [] if after is None else [after]

    def body(dy_ref, u_ref, xr_ref, xi_ref, ar_ref, ai_ref, bb_ref, cc_ref, d_ref, *rest):
        du_ref, dd_ref, dcc_ref, dbb_ref, dar_ref, dai_ref, sre, sim, up, dyp, dup = rest[len(deps):]
        _interleave_rows(u_ref, up)
        _interleave_rows(dy_ref, dyp)
        dyf = dyp[...]
        uf = up[...]
        dyb = dyf.astype(BF16)
        ub = uf.astype(BF16)
        dd_ref[...] = jnp.sum(dyf * uf, axis=0, keepdims=True)
        dup[...] = d_ref[...] * dyf
        row8 = lax.broadcasted_iota(jnp.int32, (8, LANES), 0)
        for d in range(2):
            for qi in range(nq):
                dx = _dot_nt(dyb, cc_ref[d, qi])
                sre[qi] = dx[:, :LANES]
                sim[qi] = dx[:, LANES:]
                dcc_ref[d, qi] = _dot_tn(jnp.concatenate([xr_ref[d, qi], xi_ref[d, qi]], axis=1), dyb)
            _scan_inplace(sre, sim, [ar_ref[d, qi] for qi in range(nq)], [-ai_ref[d, qi] for qi in range(nq)],
                          reverse=(d == 0))
            for qi in range(nq):
                gr = sre[qi]
                gi = sim[qi]
                xrf = xr_ref[d, qi].astype(F32)
                xif = xi_ref[d, qi].astype(F32)
                if d == 0:
                    g_main_r, g_main_i = gr[8:], gi[8:]
                    x_main_r, x_main_i = xrf[:body_rows], xif[:body_rows]
                    g_edge_r, g_edge_i = gr[:8], gi[:8]
                    x_edge_r = jnp.where(row8 == 0, 0.0, pltpu.roll(xrf[body_rows:], 1, axis=0))
                    x_edge_i = jnp.where(row8 == 0, 0.0, pltpu.roll(xif[body_rows:], 1, axis=0))
                else:
                    g_main_r, g_main_i = gr[:body_rows], gi[:body_rows]
                    x_main_r, x_main_i = xrf[8:], xif[8:]
                    g_edge_r, g_edge_i = gr[body_rows:], gi[body_rows:]
                    x_edge_r = jnp.where(row8 == 7, 0.0, pltpu.roll(xrf[:8], 7, axis=0))
                    x_edge_i = jnp.where(row8 == 7, 0.0, pltpu.roll(xif[:8], 7, axis=0))
                dar_ref[d, qi] = (jnp.sum(g_main_r * x_main_r + g_main_i * x_main_i, axis=0, keepdims=True)
                                  + jnp.sum(g_edge_r * x_edge_r + g_edge_i * x_edge_i, axis=0, keepdims=True))
                dai_ref[d, qi] = (jnp.sum(g_main_i * x_main_r - g_main_r * x_main_i, axis=0, keepdims=True)
                                  + jnp.sum(g_edge_i * x_edge_r - g_edge_r * x_edge_i, axis=0, keepdims=True))
                gb = jnp.concatenate([gr, gi], axis=1).astype(BF16)
                dup[...] += _dot_nt(gb, bb_ref[d, qi])
                dbb_ref[d, qi] = _dot_tn(ub, gb)
        _deinterleave_rows(dup, du_ref)

    blk4 = lambda k: (0, k, 0, 0)
    col = lambda k: (0, k)
    bb_spec = pl.BlockSpec((2, nq, LANES, 2 * LANES), blk4)
    cc_spec = pl.BlockSpec((2, nq, 2 * LANES, LANES), blk4)
    a_spec = pl.BlockSpec((2, nq, 1, LANES), blk4)
    x_spec = pl.BlockSpec((2, nq, SEQ, LANES), blk4)
    a_shape = jax.ShapeDtypeStruct((2, N_LANE_BLOCKS, 1, LANES), F32)
    return pl.pallas_call(
        body, name="ssm_bwd", grid=(SSM_WIDTH // LANES,),
        in_specs=[pl.BlockSpec((SEQ, LANES), col), pl.BlockSpec((SEQ, LANES), col), x_spec, x_spec,
                  a_spec, a_spec, bb_spec, cc_spec, pl.BlockSpec((1, LANES), col)]
        + [pl.BlockSpec(memory_space=pl.ANY)] * len(deps),
        out_specs=[pl.BlockSpec((SEQ, LANES), col), pl.BlockSpec((1, LANES), col),
                   cc_spec, bb_spec, a_spec, a_spec],
        out_shape=[jax.ShapeDtypeStruct((SEQ, SSM_WIDTH), F32), jax.ShapeDtypeStruct((1, SSM_WIDTH), F32),
                   jax.ShapeDtypeStruct((2, N_LANE_BLOCKS, 2 * LANES, LANES), F32),
                   jax.ShapeDtypeStruct((2, N_LANE_BLOCKS, LANES, 2 * LANES), F32), a_shape, a_shape],
        scratch_shapes=[pltpu.VMEM((nq, SEQ, LANES), F32), pltpu.VMEM((nq, SEQ, LANES), F32),
                        pltpu.VMEM((SEQ, LANES), F32), pltpu.VMEM((SEQ, LANES), F32), pltpu.VMEM((SEQ, LANES), F32)],
        compiler_params=_cparams(("parallel",)),
    )(dy, u, xr, xi, are, aim, bb, cc, dskip, *deps)


GELU_C = 0.7978845608028654
GELU_K = 0.044715


def _gelu(y):
    return 0.5 * y * (1.0 + jnp.tanh(GELU_C * (y + GELU_K * y * y * y)))


def _gelu_grad(y):
    t = jnp.tanh(GELU_C * (y + GELU_K * y * y * y))
    return 0.5 * (1.0 + t) + 0.5 * y * (1.0 - t * t) * GELU_C * (1.0 + 3.0 * GELU_K * y * y)


def _mixout_fwd(o, y, glu_w, glu_b, gan, gsn, wout, x1):
    tm = MIX_TM

    def body(o_ref, y_ref, gw_ref, gb_ref, gan_ref, gsn_ref, w_ref, x1_ref, x2_ref, mx_ref):
        yg = _gelu(y_ref[...])
        z = _dot(yg.astype(BF16), gw_ref[...]) + gb_ref[...]
        so = yg * _sigmoid(z)
        na = _rms_fwd(o_ref[...], gan_ref[...])
        ns = _rms_fwd(so, gsn_ref[...])
        mixed = jnp.concatenate([na, ns], axis=-1).astype(BF16)
        mx_ref[...] = mixed
        x2_ref[...] = x1_ref[...] + _dot(mixed, w_ref[...])

    row = lambda i: (i, 0)
    const = lambda i: (0, 0)
    return pl.pallas_call(
        body, name="mixout_fwd", grid=(SEQ // tm,),
        in_specs=[pl.BlockSpec((tm, ATTN_WIDTH), row), pl.BlockSpec((tm, SSM_WIDTH), row),
                  pl.BlockSpec((SSM_WIDTH, SSM_WIDTH), const), pl.BlockSpec((1, SSM_WIDTH), const),
                  pl.BlockSpec((1, ATTN_WIDTH), const), pl.BlockSpec((1, SSM_WIDTH), const),
                  pl.BlockSpec((D_MODEL, D_MODEL), const), pl.BlockSpec((tm, D_MODEL), row)],
        out_specs=[pl.BlockSpec((tm, D_MODEL), row), pl.BlockSpec((tm, D_MODEL), row)],
        out_shape=[jax.ShapeDtypeStruct((SEQ, D_MODEL), F32), jax.ShapeDtypeStruct((SEQ, D_MODEL), BF16)],
        compiler_params=_cparams(("parallel",)),
    )(o, y, glu_w, glu_b, gan, gsn, wout, x1)


def _mixout_bwd(dx2, o, y, glu_w, glu_b, gan, gsn, wout):
    tm = MIX_TM

    def body(dx2_ref, o_ref, y_ref, gw_ref, gb_ref, gan_ref, gsn_ref, w_ref,
             do_ref, dy_ref, dz_ref, yg_ref, dxb_ref, dgan_ref, dgsn_ref, dgb_ref):
        i = pl.program_id(0)
        dxb = dx2_ref[...].astype(BF16)
        dxb_ref[...] = dxb
        dmixed = _dot_nt(dxb, w_ref[...])
        do, dgan = _rms_bwd(dmixed[:, :ATTN_WIDTH], o_ref[...], gan_ref[...])
        do_ref[...] = do.T
        yv = y_ref[...]
        yg = _gelu(yv)
        ygb = yg.astype(BF16)
        yg_ref[...] = ygb
        sg = _sigmoid(_dot(ygb, gw_ref[...]) + gb_ref[...])
        dso, dgsn = _rms_bwd(dmixed[:, ATTN_WIDTH:], yg * sg, gsn_ref[...])
        dz = dso * yg * sg * (1.0 - sg)
        dzb = dz.astype(BF16)
        dz_ref[...] = dzb
        dyg = dso * sg + _dot_nt(dzb, gw_ref[...])
        dy_ref[...] = dyg * _gelu_grad(yv)
        dgb = jnp.sum(dz, axis=0, keepdims=True)

        @pl.when(i == 0)
        def _():
            dgan_ref[...] = dgan
            dgsn_ref[...] = dgsn
            dgb_ref[...] = dgb

        @pl.when(i != 0)
        def _():
            dgan_ref[...] += dgan
            dgsn_ref[...] += dgsn
            dgb_ref[...] += dgb

    row = lambda i: (i, 0)
    const = lambda i: (0, 0)
    return pl.pallas_call(
        body, name="mixout_bwd", grid=(SEQ // tm,),
        in_specs=[pl.BlockSpec((tm, D_MODEL), row), pl.BlockSpec((tm, ATTN_WIDTH), row),
                  pl.BlockSpec((tm, SSM_WIDTH), row),
                  pl.BlockSpec((SSM_WIDTH, SSM_WIDTH), const), pl.BlockSpec((1, SSM_WIDTH), const),
                  pl.BlockSpec((1, ATTN_WIDTH), const), pl.BlockSpec((1, SSM_WIDTH), const),
                  pl.BlockSpec((D_MODEL, D_MODEL), const)],
        out_specs=[pl.BlockSpec((ATTN_WIDTH, tm), lambda i: (0, i)), pl.BlockSpec((tm, SSM_WIDTH), row),
                   pl.BlockSpec((tm, SSM_WIDTH), row), pl.BlockSpec((tm, SSM_WIDTH), row),
                   pl.BlockSpec((tm, D_MODEL), row),
                   pl.BlockSpec((1, ATTN_WIDTH), const), pl.BlockSpec((1, SSM_WIDTH), const),
                   pl.BlockSpec((1, SSM_WIDTH), const)],
        out_shape=[jax.ShapeDtypeStruct((ATTN_WIDTH, SEQ), F32), jax.ShapeDtypeStruct((SEQ, SSM_WIDTH), F32),
                   jax.ShapeDtypeStruct((SEQ, SSM_WIDTH), BF16), jax.ShapeDtypeStruct((SEQ, SSM_WIDTH), BF16),
                   jax.ShapeDtypeStruct((SEQ, D_MODEL), BF16),
                   jax.ShapeDtypeStruct((1, ATTN_WIDTH), F32), jax.ShapeDtypeStruct((1, SSM_WIDTH), F32),
                   jax.ShapeDtypeStruct((1, SSM_WIDTH), F32)],
        compiler_params=_cparams(("arbitrary",)),
    )(dx2, o, y, glu_w, glu_b, gan, gsn, wout)


def _local_step(x, target, w, p, late_weights, early_grads, after=None, midway=None):
    x1, h1, a1, b1 = _ffn_fwd(x, p["norm_ffn1"], w["wgt1"], w["wut1"], w["wd1"], "ffn1_fwd", after=after)
    h2, q, k, v, u = _mixin_fwd(x1, p["norm_mix"], w["wint"])

    lam_re = p["ssm_lambda_re"].reshape(2 * N_LANE_BLOCKS, LANES)
    lam_im = p["ssm_lambda_im"].reshape(2 * N_LANE_BLOCKS, LANES)
    log_dt = jnp.repeat(p["ssm_log_dt"].reshape(2, 32), 64, axis=-1).reshape(2 * N_LANE_BLOCKS, LANES)
    a_re, a_im, bb, cc = _ssm_prep(lam_re, lam_im, log_dt, p["ssm_b_re"], p["ssm_b_im"],
                                   p["ssm_c_re"], p["ssm_c_im"])
    shape_a = (2, N_LANE_BLOCKS, 1, LANES)
    a_re4, a_im4 = a_re.reshape(shape_a), a_im.reshape(shape_a)
    bb4 = bb.reshape(2, N_LANE_BLOCKS, LANES, 2 * LANES)
    cc4 = cc.reshape(2, N_LANE_BLOCKS, 2 * LANES, LANES)
    dskip = p["ssm_d"].T.reshape(1, SSM_WIDTH)
    y, xr, xi = _ssm_fwd(u, a_re4, a_im4, bb4, cc4, dskip)
    o = _attn_fwd(q, k, v, p["attn_sinks"], after=None if midway is None else midway(y))

    w2 = late_weights(o)
    x2, mixed = _mixout_fwd(o, y, w2["glu"], p["ssm_glu_b"], p["attn_out_norm"], p["ssm_out_norm"], w2["wout"], x1)
    dx3, h3, a3, b3, loss, d_final = _ffn_fwd(x2, p["norm_ffn2"], w2["wgt2"], w2["wut2"], w2["wd2"], "ffn2_fwd",
                                              head=(p["final_norm"], target))
    dx2, da3, db3, s3, df3, d_n2 = _ffn_bwd_act(dx3, x2, p["norm_ffn2"], a3, b3, w2["wgt2"], w2["wut2"], w2["wd2"],
                                                "ffn2_bwd_act")
    g_wgt2, g_wut2, g_wd2 = _mm_tn([(da3, h3), (db3, h3), (s3, df3)], "ffn2_bwd_w")

    do, dy, dz, ygb, dx2b, d_gan, d_gsn, d_glub = _mixout_bwd(
        dx2, o, y, w2["glu"], p["ssm_glu_b"], p["attn_out_norm"], p["ssm_out_norm"], w2["wout"])
    (g_wout,) = _mm_tn([(mixed, dx2b)], "wout_bwd_w")
    (g_glu,) = _mm_tn([(ygb, dz)], "glu_bwd_w")
    sent = early_grads(dict(glu=g_glu, wout=g_wout, wgt2=g_wgt2, wut2=g_wut2, wd2=g_wd2))

    du, d_dskip, dcc, dbb, dar, dai = _ssm_bwd(dy, u, xr, xi, a_re4, a_im4, bb4, cc4, dskip, after=sent)
    nb = 2 * N_LANE_BLOCKS
    g_lre, g_lim, g_ldt, g_btr, g_bti, g_cre, g_cim = _ssm_prep_bwd(
        lam_re, lam_im, log_dt, p["ssm_b_re"], p["ssm_b_im"], dar.reshape(nb, LANES), dai.reshape(nb, LANES),
        dbb.reshape(nb, LANES, 2 * LANES), dcc.reshape(nb, 2 * LANES, LANES))

    dq, dk, dv, d_sinks = _attn_bwd(q, k, v, p["attn_sinks"], do)
    dx1, dproj, d_nmix = _mixin_bwd(dq, dk, dv, du, w["wint"], x1, p["norm_mix"], dx2)
    (g_wint,) = _mm_tn([(dproj, h2)], "win_bwd_w")

    dx0, da1, db1, s1, df1, d_n1 = _ffn_bwd_act(dx1, x, p["norm_ffn1"], a1, b1, w["wgt1"], w["wut1"], w["wd1"],
                                                "ffn1_bwd_act")
    g_wgt1, g_wut1, g_wd1 = _mm_tn([(da1, h1), (db1, h1), (s1, df1)], "ffn1_bwd_w")

    big = dict(wgt1=g_wgt1, wut1=g_wut1, wd1=g_wd1, wint=g_wint)
    small = dict(
        norm_ffn1=d_n1, norm_mix=d_nmix, attn_sinks=d_sinks,
        ssm_lambda_re=g_lre.reshape(64, 64), ssm_lambda_im=g_lim.reshape(64, 64),
        ssm_log_dt=g_ldt.reshape(2, 32), ssm_b_re=g_btr, ssm_b_im=g_bti, ssm_c_re=g_cre, ssm_c_im=g_cim,
        ssm_d=d_dskip.reshape(32, 16).T, ssm_glu_b=d_glub, attn_out_norm=d_gan, ssm_out_norm=d_gsn,
        norm_ffn2=d_n2, final_norm=d_final, loss=loss)
    return loss, dx0, big, small


BIG = dict(
    wgt1=("ffn1_w_gate", 352, 1024, True), wut1=("ffn1_w_up", 352, 1024, True), wd1=("ffn1_w_down", 352, 1024, False),
    wint=("w_in", 160, 1024, True), glu=("ssm_glu_w", 64, 512, False), wout=("w_out", 128, 1024, False),
    wgt2=("ffn2_w_gate", 352, 1024, True), wut2=("ffn2_w_up", 352, 1024, True), wd2=("ffn2_w_down", 352, 1024, False))

SMALL = dict(
    norm_ffn1=(1, 1024), norm_mix=(1, 1024), attn_sinks=(1, 8), ssm_lambda_re=(64, 64), ssm_lambda_im=(64, 64),
    ssm_log_dt=(2, 32), ssm_b_re=(1024, 64), ssm_b_im=(1024, 64), ssm_c_re=(1024, 64), ssm_c_im=(1024, 64),
    ssm_d=(16, 32), ssm_glu_b=(1, 512), attn_out_norm=(1, 512), ssm_out_norm=(1, 512), norm_ffn2=(1, 1024),
    final_norm=(1, 1024), loss=(1, 128))
SMALL_TRANSPOSED = ("ssm_b_re", "ssm_b_im", "ssm_d")
SMALL_PARAMS = tuple(n for n in SMALL if n != "loss")

SMALL_PAIRS = (("ssm_lambda_re", "ssm_lambda_im"), ("ssm_c_re", "ssm_c_im"), ("ssm_b_re", "ssm_b_im"))
SMALL_VECS = ("norm_ffn1", "norm_mix", "norm_ffn2", "final_norm", "ssm_glu_b", "attn_out_norm", "ssm_out_norm")
SMALL_TILES = ("ssm_log_dt", "attn_sinks", "ssm_d", "loss")


def _small_offsets():
    off, table = 0, {}
    for re, im in SMALL_PAIRS:
        table[re] = table[im] = off
        off += SMALL[re][0]
    for n in SMALL_VECS:
        table[n] = off
        off += SMALL[n][1] // LANES
    for n in SMALL_TILES:
        off = -(-off // 8) * 8
        table[n] = off
        off += SMALL[n][0]
    return table, off


SMALL_OFFSET, SMALL_USED_ROWS = _small_offsets()
SMALL_ROWS = -(-SMALL_USED_ROWS // (8 * N_DEV)) * 8 * N_DEV


def _cast_shards(shards):
    names = list(BIG)

    def body(*refs):
        ins, outs = refs[:len(names)], refs[len(names):]
        for idx in range(len(names)):
            outs[idx][...] = ins[idx][...].astype(BF16)

    return pl.pallas_call(
        body, name="cast_shards",
        out_shape=[jax.ShapeDtypeStruct((BIG[n][1], BIG[n][2]), BF16) for n in names],
        compiler_params=_cparams(),
    )(*[shards[n] for n in names])


def _peer(x, y, c, r):
    px = 1 - x if r & 4 else x
    py = 1 - y if r & 2 else y
    pc = 1 - c if r & 1 else c
    return px, py, pc


FIRST_GROUP = ("wgt1", "wut1", "wd1", "wint")
LATE_GROUP = ("glu", "wout", "wgt2", "wut2", "wd2")
N_PEERS = N_DEV - 1
ANY_SPEC = pl.BlockSpec(memory_space=pl.ANY)
HBM_SPEC = pl.BlockSpec(memory_space=pltpu.HBM)
SEM_SPEC = pl.BlockSpec(memory_space=pltpu.SEMAPHORE)
DATAFLOW_EFFECT = pltpu.SideEffectType.DATAFLOW_SIDE_EFFECTING


def _mesh_pos():
    x, y, c = lax.axis_index("x"), lax.axis_index("y"), lax.axis_index("c")
    return x, y, c, 4 * x + 2 * y + c


def _gather_first(first, late):
    nf, nl = len(first), len(late)

    def body(*refs):
        f_in, l_in = refs[:nf], refs[nf:nf + nl]
        f_out, l_out = refs[nf + nl:2 * nf + nl], refs[2 * nf + nl:2 * (nf + nl)]
        send_sems, recv_sems, local_sems = refs[2 * (nf + nl):]
        x, y, c, me = _mesh_pos()
        sibling = (x, y, 1 - c)
        chips = [(x, 1 - y), (1 - x, y), (1 - x, 1 - y)]

        def idx(px, py, pc):
            return 4 * px + 2 * py + pc

        def copy(k, s, block, to, src=None):
            slot = f_out[k].at[block]
            return pltpu.make_async_remote_copy(
                src_ref=slot if src is None else src, dst_ref=slot, send_sem=send_sems.at[k, s],
                recv_sem=recv_sems.at[k, s], device_id=to, device_id_type=MESH_ID)

        local = []
        for k in range(nf + nl):
            src, dst = (f_in[k], f_out[k]) if k < nf else (l_in[k - nf], l_out[k - nf])
            mine = pltpu.make_async_copy(src, dst.at[me], local_sems.at[k])
            mine.start()
            local.append(mine)
        sends = []
        for j, chip in enumerate(chips):
            for k in range(nf):
                sends.append(copy(k, 1 + j, me, (*chip, c), src=f_in[k]))
                sends[-1].start()
        for k in range(nf):
            sends.append(copy(k, 0, me, sibling, src=f_in[k]))
            sends[-1].start()
        for j, chip in enumerate(chips):
            for k in range(nf):
                copy(k, 1 + j, idx(*chip, c), (*chip, c)).wait_recv()
                sends.append(copy(k, 4 + j, idx(*chip, c), sibling))
                sends[-1].start()
        for k in range(nf):
            copy(k, 0, idx(*sibling), sibling).wait_recv()
        for j, chip in enumerate(chips):
            for k in range(nf):
                copy(k, 4 + j, idx(*chip, 1 - c), sibling).wait_recv()
        for cp in sends:
            cp.wait_send()
        for cp in local:
            cp.wait()

    return pl.pallas_call(
        body, name="gather_first",
        in_specs=[ANY_SPEC] * (nf + nl), out_specs=[ANY_SPEC] * (nf + nl),
        out_shape=[jax.ShapeDtypeStruct((N_DEV,) + s.shape, s.dtype) for s in list(first) + list(late)],
        scratch_shapes=[pltpu.SemaphoreType.DMA((nf, N_PEERS)), pltpu.SemaphoreType.DMA((nf, N_PEERS)),
                        pltpu.SemaphoreType.DMA((nf + nl,))],
        compiler_params=pltpu.CompilerParams(has_side_effects=True),
    )(*first, *late)


def _split_copy(src_refs, land_refs, send_sems, recv_sems, k, r, pos, scatter, receiving):
    x, y, c, me = pos
    px, py, pc = _peer(x, y, c, r)
    peer_idx = 4 * px + 2 * py + pc
    if scatter:
        src, dst = src_refs[k].at[peer_idx], land_refs[k].at[r - 1]
    else:
        src, dst = src_refs[k], land_refs[k].at[peer_idx if receiving else me]
    return pltpu.make_async_remote_copy(
        src_ref=src, dst_ref=dst, send_sem=send_sems.at[k * N_PEERS + r - 1],
        recv_sem=recv_sems.at[k * N_PEERS + r - 1], device_id=(px, py, pc), device_id_type=MESH_ID)


def _split_start(name, srcs, lands, scatter):
    n = len(srcs)

    def body(*refs):
        src_refs, land_refs = refs[:n], refs[n:2 * n]
        send_sems, recv_sems = refs[2 * n], refs[2 * n + 1]
        token = refs[-1]
        pos = _mesh_pos()
        for k in range(n):
            for r in range(1, N_DEV):
                _split_copy(src_refs, land_refs, send_sems, recv_sems, k, r, pos, scatter, False).start()
        token[...] = jnp.zeros_like(token)

    thru = [pltpu.HBM(a.shape, a.dtype) for a in list(srcs) + list(lands)]
    outs = pl.pallas_call(
        body, name=name,
        in_specs=[HBM_SPEC] * (2 * n),
        out_specs=[SEM_SPEC, SEM_SPEC] + [HBM_SPEC] * (2 * n) + [pl.BlockSpec(memory_space=pltpu.VMEM)],
        out_shape=[pltpu.SemaphoreType.DMA((n * N_PEERS,)), pltpu.SemaphoreType.DMA((n * N_PEERS,))] + thru
        + [jax.ShapeDtypeStruct((8, LANES), F32)],
        input_output_aliases={i: 2 + i for i in range(2 * n)},
        compiler_params=pltpu.CompilerParams(has_side_effects=DATAFLOW_EFFECT),
    )(*[pltpu.with_memory_space_constraint(a, pltpu.HBM) for a in list(srcs) + list(lands)])
    return outs[0], outs[1], outs[2:2 + n], outs[2 + n:2 + 2 * n], outs[-1]


def _split_wait(name, send_sems, recv_sems, srcs, lands, scatter, after):
    n = len(srcs)

    def body(*refs):
        src_refs, land_refs = refs[:n], refs[n:2 * n]
        send, recv = refs[2 * n], refs[2 * n + 1]
        pos = _mesh_pos()
        for k in range(n):
            for r in range(1, N_DEV):
                cp = _split_copy(src_refs, land_refs, send, recv, k, r, pos, scatter, True)
                cp.wait_send()
                cp.wait_recv()

    thru = [pltpu.HBM(a.shape, a.dtype) for a in list(srcs) + list(lands)]
    outs = pl.pallas_call(
        body, name=name,
        in_specs=[HBM_SPEC] * (2 * n) + [SEM_SPEC, SEM_SPEC, ANY_SPEC],
        out_specs=[HBM_SPEC] * (2 * n), out_shape=thru,
        input_output_aliases={i: i for i in range(2 * n)},
        compiler_params=pltpu.CompilerParams(has_side_effects=DATAFLOW_EFFECT),
    )(*srcs, *lands, send_sems, recv_sems, after)
    return outs[:n], outs[n:]


def _late_copy(passing, src_refs, land_refs, send_sems, recv_sems, k, s, pos, receiving):
    x, y, c, me = pos
    chips = [(x, 1 - y), (1 - x, y), (1 - x, 1 - y)]
    sibling = (x, y, 1 - c)

    def idx(dev):
        return 4 * dev[0] + 2 * dev[1] + dev[2]

    if passing:
        to = sibling
        block = idx((*chips[s], 1 - c)) if receiving else idx((*chips[s], c))
        src = dst = land_refs[k].at[block]
        sem = k * 3 + s
    else:
        to = sibling if s == 0 else (*chips[s - 1], c)
        src, dst = src_refs[k], land_refs[k].at[idx(to) if receiving else me]
        sem = k * 4 + s
    return pltpu.make_async_remote_copy(src_ref=src, dst_ref=dst, send_sem=send_sems.at[sem],
                                        recv_sem=recv_sems.at[sem], device_id=to, device_id_type=MESH_ID)


def _late_gather_call(name, stage, srcs, lands, sems, after=None):
    n = len(srcs)
    n_sem_in = len(sems)
    has_after = after is not None

    def body(*refs):
        src_refs, land_refs = refs[:n], refs[n:2 * n]
        sem_in = refs[2 * n:2 * n + n_sem_in]
        outs = refs[2 * n + n_sem_in + (1 if has_after else 0):]
        pos = _mesh_pos()
        if stage == 0:
            own_send, own_recv = outs[0], outs[1]
            for s in (1, 2, 3, 0):
                for k in range(n):
                    _late_copy(False, src_refs, land_refs, own_send, own_recv, k, s, pos, False).start()
            outs[-1][...] = jnp.zeros_like(outs[-1])
        elif stage == 1:
            own_recv = sem_in[1]
            pass_send, pass_recv = outs[0], outs[1]
            for s in range(3):
                for k in range(n):
                    _late_copy(False, src_refs, land_refs, sem_in[0], own_recv, k, s + 1, pos, True).wait_recv()
                    _late_copy(True, src_refs, land_refs, pass_send, pass_recv, k, s, pos, False).start()
            outs[-1][...] = jnp.zeros_like(outs[-1])
        else:
            own_send, own_recv, pass_send, pass_recv = sem_in
            for k in range(n):
                _late_copy(False, src_refs, land_refs, own_send, own_recv, k, 0, pos, True).wait_recv()
                for s in range(4):
                    _late_copy(False, src_refs, land_refs, own_send, own_recv, k, s, pos, False).wait_send()
                for s in range(3):
                    cp = _late_copy(True, src_refs, land_refs, pass_send, pass_recv, k, s, pos, True)
                    cp.wait_recv()
                    cp.wait_send()

    thru = [pltpu.HBM(a.shape, a.dtype) for a in list(srcs) + list(lands)]
    new_sems = [[pltpu.SemaphoreType.DMA((n * 4,))] * 2, [pltpu.SemaphoreType.DMA((n * 3,))] * 2, []][stage]
    extra = [] if stage == 2 else [jax.ShapeDtypeStruct((8, LANES), F32)]
    outs = pl.pallas_call(
        body, name=name,
        in_specs=[HBM_SPEC] * (2 * n) + [SEM_SPEC] * n_sem_in + [ANY_SPEC] * has_after,
        out_specs=[SEM_SPEC] * len(new_sems) + [HBM_SPEC] * (2 * n) + [pl.BlockSpec(memory_space=pltpu.VMEM)] * len(extra),
        out_shape=new_sems + thru + extra,
        input_output_aliases={i: len(new_sems) + i for i in range(2 * n)},
        compiler_params=pltpu.CompilerParams(has_side_effects=DATAFLOW_EFFECT),
    )(*[pltpu.with_memory_space_constraint(a, pltpu.HBM) for a in list(srcs) + list(lands)], *sems,
      *([after] if has_after else []))
    ns = len(new_sems)
    return list(outs[:ns]), outs[ns:ns + n], outs[ns + n:ns + 2 * n], (outs[-1] if extra else None)


N_SEND_SLOTS = 3


def _exchange_last(grads, small_packed):
    ng = len(grads)
    ch = SMALL_ROWS // N_DEV
    max_rows = max(g.shape[1] for g in grads)
    cols = grads[0].shape[2]

    def body(*refs):
        g_in, s_in = refs[:ng], refs[ng]
        outs = refs[ng + 1:]
        own_out, land, stage = outs[:ng], outs[ng:2 * ng], outs[2 * ng:3 * ng]
        s_red, s_stage = outs[3 * ng], outs[3 * ng + 1]
        (va, vb, vo, vs, sm_in, sm_out, d2d_send, d2d_recv, ici_send, ici_recv, s1_send, s1_recv, s2_send, s2_recv,
         local_sems) = outs[3 * ng + 2:]
        x, y, c, me = _mesh_pos()
        sibling = (x, y, 1 - c)
        chips = [(x, y), (x, 1 - y), (1 - x, y), (1 - x, 1 - y)]

        def idx(chip, core):
            return 4 * chip[0] + 2 * chip[1] + core

        def d2d(k, j):
            return pltpu.make_async_remote_copy(
                src_ref=g_in[k].at[idx(chips[j], 1 - c)], dst_ref=stage[k].at[j], send_sem=d2d_send.at[k, j],
                recv_sem=d2d_recv.at[k, j], device_id=sibling, device_id_type=MESH_ID)

        def ici(k, j, slot):
            rows = g_in[k].shape[1]
            return pltpu.make_async_remote_copy(
                src_ref=vo.at[slot, pl.ds(0, rows)], dst_ref=land[k].at[j - 1], send_sem=ici_send.at[k, j - 1],
                recv_sem=ici_recv.at[k, j - 1], device_id=(*chips[j], c), device_id_type=MESH_ID)

        def small_scatter(r):
            px, py, pc = _peer(x, y, c, r)
            return pltpu.make_async_remote_copy(
                src_ref=s_in.at[pl.ds(pl.multiple_of((4 * px + 2 * py + pc) * ch, 8), ch)], dst_ref=s_stage.at[me],
                send_sem=s1_send.at[r - 1], recv_sem=s1_recv.at[r - 1], device_id=(px, py, pc), device_id_type=MESH_ID)

        def small_gather(r):
            return pltpu.make_async_remote_copy(
                src_ref=sm_out, dst_ref=s_red.at[me], send_sem=s2_send.at[r - 1], recv_sem=s2_recv.at[r - 1],
                device_id=_peer(x, y, c, r), device_id_type=MESH_ID)

        for r in range(1, N_DEV):
            small_scatter(r).start()
        mine = pltpu.make_async_copy(s_in.at[pl.ds(pl.multiple_of(me * ch, 8), ch)], s_stage.at[me], local_sems.at[0])
        mine.start()
        pairs = [(k, j) for k in range(ng) for j in (1, 2, 3)] + [(k, 0) for k in range(ng)]
        for k, j in pairs:
            d2d(k, j).start()

        def reduce_small():
            for r in range(1, N_DEV):
                small_scatter(r).wait_recv()
            mine.wait()
            load = pltpu.make_async_copy(s_stage, sm_in, local_sems.at[1])
            load.start()
            load.wait()
            total = sm_in[0]
            for i in range(1, N_DEV):
                total = total + sm_in[i]
            sm_out[...] = total
            for r in range(1, N_DEV):
                small_gather(r).start()
            keep = pltpu.make_async_copy(sm_out, s_red.at[me], local_sems.at[2])
            keep.start()
            return keep

        in_flight = {}
        for i, (k, j) in enumerate(pairs):
            if i == N_SEND_SLOTS:
                keep = reduce_small()
            slot = i % N_SEND_SLOTS
            rows = g_in[k].shape[1]
            if slot in in_flight:
                in_flight.pop(slot).wait_send()
            d2d(k, j).wait_recv()
            la = pltpu.make_async_copy(g_in[k].at[idx(chips[j], c)], va.at[pl.ds(0, rows)], local_sems.at[3])
            lb = pltpu.make_async_copy(stage[k].at[j], vb.at[pl.ds(0, rows)], local_sems.at[4])
            la.start()
            lb.start()
            la.wait()
            lb.wait()
            total = va[pl.ds(0, rows)].astype(F32) + vb[pl.ds(0, rows)].astype(F32)
            if j == 0:
                vs[pl.ds(0, rows)] = total
                st = pltpu.make_async_copy(vs.at[pl.ds(0, rows)], own_out[k], local_sems.at[5])
                st.start()
                st.wait()
            else:
                vo[slot, pl.ds(0, rows)] = total.astype(BF16)
                cp = ici(k, j, slot)
                cp.start()
                in_flight[slot] = cp
        for cp in in_flight.values():
            cp.wait_send()

        for j in (1, 2, 3, 0):
            for k in range(ng):
                d2d(k, j).wait_send()
        for j in (1, 2, 3):
            for k in range(ng):
                ici(k, j, 0).wait_recv()
        for r in range(1, N_DEV):
            small_scatter(r).wait_send()
            small_gather(r).wait_send()
            small_gather(r).wait_recv()
        keep.wait()

    out_shape = [jax.ShapeDtypeStruct(g.shape[1:], F32) for g in grads]
    out_shape += [jax.ShapeDtypeStruct((3,) + g.shape[1:], BF16) for g in grads]
    out_shape += [jax.ShapeDtypeStruct((4,) + g.shape[1:], BF16) for g in grads]
    out_shape += [jax.ShapeDtypeStruct((N_DEV, ch, LANES), F32), jax.ShapeDtypeStruct((N_DEV, ch, LANES), F32)]
    outs = pl.pallas_call(
        body, name="exchange_last",
        in_specs=[ANY_SPEC] * (ng + 1), out_specs=[ANY_SPEC] * len(out_shape), out_shape=out_shape,
        scratch_shapes=[pltpu.VMEM((max_rows, cols), BF16), pltpu.VMEM((max_rows, cols), BF16),
                        pltpu.VMEM((N_SEND_SLOTS, max_rows, cols), BF16), pltpu.VMEM((max_rows, cols), F32),
                        pltpu.VMEM((N_DEV, ch, LANES), F32), pltpu.VMEM((ch, LANES), F32),
                        pltpu.SemaphoreType.DMA((ng, 4)), pltpu.SemaphoreType.DMA((ng, 4)),
                        pltpu.SemaphoreType.DMA((ng, 3)), pltpu.SemaphoreType.DMA((ng, 3)),
                        pltpu.SemaphoreType.DMA((N_PEERS,)), pltpu.SemaphoreType.DMA((N_PEERS,)),
                        pltpu.SemaphoreType.DMA((N_PEERS,)), pltpu.SemaphoreType.DMA((N_PEERS,)),
                        pltpu.SemaphoreType.DMA((6,))],
        compiler_params=pltpu.CompilerParams(has_side_effects=True, vmem_limit_bytes=VMEM_LIMIT),
    )(*grads, small_packed)
    return outs[:ng], outs[ng:2 * ng], outs[3 * ng].reshape(SMALL_ROWS, LANES)


def _adamw_math(w, g, m, v):
    m2 = ADAM_B1 * m + (1.0 - ADAM_B1) * g
    v2 = ADAM_B2 * v + (1.0 - ADAM_B2) * (g * g)
    m_hat = m2 / (1.0 - ADAM_B1 ** ADAM_STEP)
    v_hat = v2 / (1.0 - ADAM_B2 ** ADAM_STEP)
    delta = -ADAM_LR * (m_hat / (jnp.sqrt(v_hat) + ADAM_EPS) + ADAM_WD * w)
    return delta, m2, v2


ADAM_ROW_TILES = 2


def _adamw_big(own, parts, w, m, v, name):
    shape = w.shape
    own_is_blocks = own.ndim == 3
    tr = shape[0] // ADAM_ROW_TILES
    n_parts = parts.shape[0]

    def body(own_ref, p_ref, w_ref, m_ref, v_ref, g_ref, d_ref, m2_ref, v2_ref, own_s, sem):
        rows = pl.ds(pl.multiple_of(pl.program_id(0) * tr, 16), tr)
        if own_is_blocks:
            cp = pltpu.make_async_copy(own_ref.at[_mesh_pos()[3], rows], own_s, sem)
        else:
            cp = pltpu.make_async_copy(own_ref.at[rows], own_s, sem)
        cp.start()
        cp.wait()
        g = own_s[...].astype(F32)
        for i in range(n_parts):
            g = g + p_ref[i].astype(F32)
        delta, m2, v2 = _adamw_math(w_ref[...], g, m_ref[...], v_ref[...])
        g_ref[...] = g
        d_ref[...] = delta
        m2_ref[...] = m2
        v2_ref[...] = v2

    tile = pl.BlockSpec((tr, shape[1]), lambda i: (i, 0))
    return pl.pallas_call(
        body, name=name, grid=(ADAM_ROW_TILES,),
        in_specs=[ANY_SPEC, pl.BlockSpec((n_parts, tr, shape[1]), lambda i: (0, i, 0)), tile, tile, tile],
        out_specs=[tile] * 4, out_shape=[jax.ShapeDtypeStruct(shape, F32)] * 4,
        scratch_shapes=[pltpu.VMEM((tr, shape[1]), own.dtype), pltpu.SemaphoreType.DMA(())],
        compiler_params=_cparams(("arbitrary",)),
    )(own, parts, w, m, v)


def _pack_small(grads):
    names = list(SMALL)

    def body(*refs):
        ins, out = dict(zip(names, refs[:-1])), refs[-1]
        out[...] = jnp.zeros_like(out)
        for re, im in SMALL_PAIRS:
            off, rows = SMALL_OFFSET[re], SMALL[re][0]
            out[off:off + rows, :] = jnp.concatenate([ins[re][...], ins[im][...]], axis=1)
        for n in SMALL_VECS:
            off, vec = SMALL_OFFSET[n], ins[n][...]
            for i in range(SMALL[n][1] // LANES):
                out[off + i:off + i + 1, :] = vec[:, i * LANES:(i + 1) * LANES]
        for n in SMALL_TILES:
            off, (rows, cols) = SMALL_OFFSET[n], SMALL[n]
            out[off:off + rows, 0:cols] = ins[n][...]

    return pl.pallas_call(
        body, name="pack_small", out_shape=jax.ShapeDtypeStruct((SMALL_ROWS, LANES), F32),
        compiler_params=_cparams(),
    )(*[grads[n] for n in names])


def _unpack_small_ref(g_ref, n):
    off, (rows, cols) = SMALL_OFFSET[n], SMALL[n]
    for re, im in SMALL_PAIRS:
        if n == re:
            return g_ref[off:off + rows, 0:HALF_LANES]
        if n == im:
            return g_ref[off:off + rows, HALF_LANES:LANES]
    if n in SMALL_VECS:
        return jnp.concatenate([g_ref[off + i:off + i + 1, :] for i in range(cols // LANES)], axis=1)
    return g_ref[off:off + rows, 0:cols]


def _adamw_small(g_packed, w, m, v):
    names = list(SMALL_PARAMS)
    n = len(names)

    def body(g_ref, *refs):
        w_refs, m_refs, v_refs, outs = refs[:n], refs[n:2 * n], refs[2 * n:3 * n], refs[3 * n:]
        for idx, name in enumerate(names):
            g = _unpack_small_ref(g_ref, name)
            delta, m2, v2 = _adamw_math(w_refs[idx][...], g, m_refs[idx][...], v_refs[idx][...])
            outs[4 * idx][...] = g
            outs[4 * idx + 1][...] = delta
            outs[4 * idx + 2][...] = m2
            outs[4 * idx + 3][...] = v2
        outs[4 * n][...] = _unpack_small_ref(g_ref, "loss")

    outs = pl.pallas_call(
        body, name="adamw_small",
        out_shape=[jax.ShapeDtypeStruct(SMALL[name], F32) for name in names for _ in range(4)]
        + [jax.ShapeDtypeStruct(SMALL["loss"], F32)],
        compiler_params=_cparams(),
    )(g_packed, *[w[k] for k in names], *[m[k] for k in names], *[v[k] for k in names])
    return {name: outs[4 * idx:4 * idx + 4] for idx, name in enumerate(names)}, outs[4 * n]


WEIGHT_NAMES = ['norm_ffn1', 'ffn1_w_gate', 'ffn1_w_up', 'ffn1_w_down', 'norm_mix', 'w_in', 'attn_sinks',
                'ssm_lambda_re', 'ssm_lambda_im', 'ssm_log_dt', 'ssm_b_re', 'ssm_b_im', 'ssm_c_re', 'ssm_c_im',
                'ssm_d', 'ssm_glu_w', 'ssm_glu_b', 'attn_out_norm', 'ssm_out_norm', 'w_out', 'norm_ffn2',
                'ffn2_w_gate', 'ffn2_w_up', 'ffn2_w_down', 'final_norm']


def kernel(x, norm_ffn1, ffn1_w_gate, ffn1_w_up, ffn1_w_down, norm_mix, w_in, attn_sinks, ssm_lambda_re, ssm_lambda_im, ssm_log_dt, ssm_b_re, ssm_b_im, ssm_c_re, ssm_c_im, ssm_d, ssm_glu_w, ssm_glu_b, attn_out_norm, ssm_out_norm, w_out, norm_ffn2, ffn2_w_gate, ffn2_w_up, ffn2_w_down, final_norm, loss_target, m_norm_ffn1, m_ffn1_w_gate, m_ffn1_w_up, m_ffn1_w_down, m_norm_mix, m_w_in, m_attn_sinks, m_ssm_lambda_re, m_ssm_lambda_im, m_ssm_log_dt, m_ssm_b_re, m_ssm_b_im, m_ssm_c_re, m_ssm_c_im, m_ssm_d, m_ssm_glu_w, m_ssm_glu_b, m_attn_out_norm, m_ssm_out_norm, m_w_out, m_norm_ffn2, m_ffn2_w_gate, m_ffn2_w_up, m_ffn2_w_down, m_final_norm, v_norm_ffn1, v_ffn1_w_gate, v_ffn1_w_up, v_ffn1_w_down, v_norm_mix, v_w_in, v_attn_sinks, v_ssm_lambda_re, v_ssm_lambda_im, v_ssm_log_dt, v_ssm_b_re, v_ssm_b_im, v_ssm_c_re, v_ssm_c_im, v_ssm_d, v_ssm_glu_w, v_ssm_glu_b, v_attn_out_norm, v_ssm_out_norm, v_w_out, v_norm_ffn2, v_ffn2_w_gate, v_ffn2_w_up, v_ffn2_w_down, v_final_norm):
    args = dict(locals())
    weights = {n: args[n] for n in WEIGHT_NAMES}
    moms = {n: args["m_" + n] for n in WEIGHT_NAMES}
    vars_ = {n: args["v_" + n] for n in WEIGHT_NAMES}

    def shard2d(a, k):
        a = a.reshape(a.shape[-2], a.shape[-1])
        return a.T if BIG[k][3] else a

    def shard_master(a, k):
        return (a.T if BIG[k][3] else a).reshape(weights[BIG[k][0]].shape)

    def blocks(g, k):
        return g.reshape(N_DEV, BIG[k][1], BIG[k][2])

    def full(g, k):
        return g.reshape(N_DEV * BIG[k][1], BIG[k][2])

    shards = dict(zip(BIG, _cast_shards({k: shard2d(weights[BIG[k][0]], k) for k in BIG})))
    nf = len(FIRST_GROUP)
    got = _gather_first([shards[k] for k in FIRST_GROUP], [shards[k] for k in LATE_GROUP])
    w_first = {k: full(g, k) for k, g in zip(FIRST_GROUP, got[:nf])}
    late = {}
    late["own_sems"], late["srcs"], late["lands"], w_token = _late_gather_call(
        "gather_late_start", 0, [shards[k] for k in LATE_GROUP], got[nf:], [])

    def late_pass(dep):
        late["pass_sems"], late["srcs"], late["lands"], token = _late_gather_call(
            "gather_late_pass", 1, late["srcs"], late["lands"], late["own_sems"], after=dep)
        return token

    def late_weights(dep):
        _, _, lands, _ = _late_gather_call("gather_late_wait", 2, late["srcs"], late["lands"],
                                           late["own_sems"] + late["pass_sems"], after=dep)
        return {k: full(g, k) for k, g in zip(LATE_GROUP, lands)}

    early = {}

    def early_grads(g):
        srcs = [blocks(g[k], k) for k in LATE_GROUP]
        lands = [lax.empty((N_PEERS, BIG[k][1], BIG[k][2]), BF16) for k in LATE_GROUP]
        early["send"], early["recv"], early["srcs"], early["lands"], token = _split_start(
            "grads_late_start", srcs, lands, scatter=True)
        return token

    def small2d(a, n):
        if n in SMALL_TRANSPOSED:
            a = jnp.swapaxes(a, -1, -2)
        return a.reshape(SMALL[n])

    def small_master(a, n):
        if n in SMALL_TRANSPOSED:
            shape = weights[n].shape
            return jnp.swapaxes(a.reshape(shape[:-2] + (shape[-1], shape[-2])), -1, -2)
        return a.reshape(weights[n].shape)

    small_p = {n: small2d(weights[n], n) for n in SMALL_PARAMS}
    _, grad_x, g_first, g_small = _local_step(
        x.reshape(SEQ, D_MODEL), loss_target.reshape(SEQ, D_MODEL), w_first, small_p, late_weights, early_grads,
        after=w_token, midway=late_pass)

    own_sums, first_parts, small_grad = _exchange_last([blocks(g_first[k], k) for k in FIRST_GROUP],
                                                       _pack_small(g_small))
    own_late, late_parts = _split_wait("grads_late_wait", early["send"], early["recv"], early["srcs"],
                                       early["lands"], True, small_grad)
    own = dict(zip(FIRST_GROUP + LATE_GROUP, list(own_sums) + list(own_late)))
    parts = dict(zip(FIRST_GROUP + LATE_GROUP, list(first_parts) + list(late_parts)))
    outs = {}
    for k in BIG:
        n = BIG[k][0]
        outs[n] = [shard_master(o, k) for o in
                   _adamw_big(own[k], parts[k], shard2d(weights[n], k), shard2d(moms[n], k), shard2d(vars_[n], k),
                              "adamw_" + n)]
    small_out, loss_row = _adamw_small(small_grad, small_p, {n: small2d(moms[n], n) for n in SMALL_PARAMS},
                                       {n: small2d(vars_[n], n) for n in SMALL_PARAMS})
    for n in SMALL_PARAMS:
        outs[n] = [small_master(o, n) for o in small_out[n]]

    result = [loss_row[0, 0], grad_x.reshape(x.shape)]
    for i in range(4):
        result += [outs[n][i] for n in WEIGHT_NAMES]
    return tuple(result)
```

```python
import functools

import jax
import jax.numpy as jnp
from jax import lax
from jax.experimental import pallas as pl
from jax.experimental.pallas import tpu as pltpu

F32 = jnp.float32
BF16 = jnp.bfloat16

N_DEV = 8
SEQ = 2048
D_MODEL = 1024
D_FF = 2816
ATTN_HEADS = 8
KV_HEADS = 2
HEAD_DIM = 64
ATTN_WIDTH = 512
KV_WIDTH = 128
WINDOW = 128
SSM_WIDTH = 512
IN_WIDTH = 1280
EPS = 1e-6
MASKED_DISTANCE = 1e33
LAMBDA_RE_MAX = -1e-4
LANES = 128
N_LANE_BLOCKS = 16
SCAN_CHUNK = SEQ // 8

ADAM_LR = 0.001
ADAM_B1 = 0.9
ADAM_B2 = 0.999
ADAM_EPS = 1e-08
ADAM_WD = 0.01
ADAM_STEP = 10

VMEM_LIMIT = 60 * 1024 * 1024
MESH_ID = pl.DeviceIdType.MESH


def _cparams(sem=None):
    return pltpu.CompilerParams(dimension_semantics=sem, vmem_limit_bytes=VMEM_LIMIT)


def _dot(a, b):
    return jnp.dot(a, b, preferred_element_type=F32)


def _dot_nt(a, b):
    return lax.dot_general(a, b, (((1,), (1,)), ((), ())), preferred_element_type=F32)


def _dot_tn(a, b):
    return lax.dot_general(a, b, (((0,), (0,)), ((), ())), preferred_element_type=F32)


def _rms_fwd(x, g):
    r = lax.rsqrt(jnp.mean(x * x, axis=-1, keepdims=True) + EPS)
    return x * r * g


def _rms_bwd(dh, x, g):
    r = lax.rsqrt(jnp.mean(x * x, axis=-1, keepdims=True) + EPS)
    xh = x * r
    dg = jnp.sum(dh * xh, axis=0, keepdims=True)
    dxh = dh * g
    dx = r * (dxh - xh * jnp.mean(dxh * xh, axis=-1, keepdims=True))
    return dx, dg


def _sigmoid(x):
    return 1.0 / (1.0 + jnp.exp(-x))


FFN_TM = 512
FFN_TF = 1408


def _ffn_fwd(x, g, wgt, wut, wd, name, after=None, head=None):
    tm, tf = FFN_TM // 2, D_FF
    nj = D_FF // tf
    deps = [] if after is None else [after]
    n_in = len(deps) + (2 if head else 0)

    def body(x_ref, g_ref, wg_ref, wu_ref, wd_ref, *rest):
        i = pl.program_id(0)
        j = pl.program_id(1)
        if head:
            gf_ref, t_ref = rest[len(deps):n_in]
            xo_ref, h_ref, a_ref, b_ref, loss_ref, dgf_ref, h_s, acc = rest[n_in:]
        else:
            xo_ref, h_ref, a_ref, b_ref, h_s, acc = rest[n_in:]

        @pl.when(j == 0)
        def _():
            h = _rms_fwd(x_ref[...], g_ref[...]).astype(BF16)
            h_s[...] = h
            h_ref[...] = h
            acc[...] = jnp.zeros_like(acc)

        h = h_s[...]
        a = _dot_nt(h, wg_ref[...])
        b = _dot_nt(h, wu_ref[...])
        a_ref[...] = a.astype(BF16)
        b_ref[...] = b.astype(BF16)
        s = (a * _sigmoid(a) * b).astype(BF16)
        acc[...] += _dot(s, wd_ref[...])

        @pl.when(j == nj - 1)
        def _():
            xo = x_ref[...] + 0.5 * acc[...]
            if not head:
                xo_ref[...] = xo
                return
            gf = gf_ref[...]
            err = _rms_fwd(xo, gf) - t_ref[...]
            part = jnp.broadcast_to(0.5 * jnp.sum(err * err) / D_MODEL, (1, LANES))
            dx, dgf = _rms_bwd(err * (1.0 / D_MODEL), xo, gf)
            xo_ref[...] = dx

            @pl.when(i == 0)
            def _():
                loss_ref[...] = part
                dgf_ref[...] = dgf

            @pl.when(i != 0)
            def _():
                loss_ref[...] += part
                dgf_ref[...] += dgf

    row = lambda i, j: (i, 0)
    const = lambda i, j: (0, 0)
    head_in = [pl.BlockSpec((1, D_MODEL), const), pl.BlockSpec((tm, D_MODEL), row)] if head else []
    head_out = [pl.BlockSpec((1, LANES), const), pl.BlockSpec((1, D_MODEL), const)] if head else []
    head_shape = [jax.ShapeDtypeStruct((1, LANES), F32), jax.ShapeDtypeStruct((1, D_MODEL), F32)] if head else []
    return pl.pallas_call(
        body, name=name, grid=(SEQ // tm, nj),
        in_specs=[pl.BlockSpec((tm, D_MODEL), row), pl.BlockSpec((1, D_MODEL), const),
                  pl.BlockSpec((tf, D_MODEL), lambda i, j: (j, 0)),
                  pl.BlockSpec((tf, D_MODEL), lambda i, j: (j, 0)),
                  pl.BlockSpec((tf, D_MODEL), lambda i, j: (j, 0))] + [pl.BlockSpec(memory_space=pl.ANY)] * len(deps)
        + head_in,
        out_specs=[pl.BlockSpec((tm, D_MODEL), row), pl.BlockSpec((tm, D_MODEL), row),
                   pl.BlockSpec((tm, tf), lambda i, j: (i, j)),
                   pl.BlockSpec((tm, tf), lambda i, j: (i, j))] + head_out,
        out_shape=[jax.ShapeDtypeStruct((SEQ, D_MODEL), F32), jax.ShapeDtypeStruct((SEQ, D_MODEL), BF16),
                   jax.ShapeDtypeStruct((SEQ, D_FF), BF16), jax.ShapeDtypeStruct((SEQ, D_FF), BF16)] + head_shape,
        scratch_shapes=[pltpu.VMEM((tm, D_MODEL), BF16), pltpu.VMEM((tm, D_MODEL), F32)],
        compiler_params=_cparams(("arbitrary" if head else "parallel", "arbitrary")),
    )(x, g, wgt, wut, wd, *deps, *(head or ()))


def _ffn_bwd_act(dxo, x, g, a, b, wgt, wut, wd, name):
    tm, tf = FFN_TM // 2, D_FF
    nj = D_FF // tf
    resident = pl.Buffered(1)

    def body(dxo_ref, x_ref, g_ref, a_ref, b_ref, wg_ref, wu_ref, wd_ref,
             dx_ref, da_ref, db_ref, s_ref, df_ref, dg_ref, df_s, acc):
        i = pl.program_id(0)
        j = pl.program_id(1)

        @pl.when(j == 0)
        def _():
            df = (0.5 * dxo_ref[...]).astype(BF16)
            df_s[...] = df
            df_ref[...] = df
            acc[...] = jnp.zeros_like(acc)

        ds = _dot_nt(df_s[...], wd_ref[...])
        av = a_ref[...].astype(F32)
        bv = b_ref[...].astype(F32)
        sig = _sigmoid(av)
        sl = av * sig
        s_ref[...] = (sl * bv).astype(BF16)
        db = (ds * sl).astype(BF16)
        da = (ds * bv * (sig * (1.0 + av * (1.0 - sig)))).astype(BF16)
        da_ref[...] = da
        db_ref[...] = db
        acc[...] += _dot(da, wg_ref[...]) + _dot(db, wu_ref[...])

        @pl.when(j == nj - 1)
        def _():
            dx, dg = _rms_bwd(acc[...], x_ref[...], g_ref[...])
            dx_ref[...] = dxo_ref[...] + dx

            @pl.when(i == 0)
            def _():
                dg_ref[...] = dg

            @pl.when(i != 0)
            def _():
                dg_ref[...] += dg

    row = lambda i, j: (i, 0)
    col = lambda i, j: (j, 0)
    tile = lambda i, j: (i, j)
    return pl.pallas_call(
        body, name=name, grid=(SEQ // tm, nj),
        in_specs=[pl.BlockSpec((tm, D_MODEL), row), pl.BlockSpec((tm, D_MODEL), row),
                  pl.BlockSpec((1, D_MODEL), lambda i, j: (0, 0)),
                  pl.BlockSpec((tm, tf), tile), pl.BlockSpec((tm, tf), tile),
                  pl.BlockSpec((tf, D_MODEL), col, pipeline_mode=resident),
                  pl.BlockSpec((tf, D_MODEL), col, pipeline_mode=resident),
                  pl.BlockSpec((tf, D_MODEL), col, pipeline_mode=resident)],
        out_specs=[pl.BlockSpec((tm, D_MODEL), row),
                   pl.BlockSpec((tm, tf), tile), pl.BlockSpec((tm, tf), tile), pl.BlockSpec((tm, tf), tile),
                   pl.BlockSpec((tm, D_MODEL), row),
                   pl.BlockSpec((1, D_MODEL), lambda i, j: (0, 0))],
        out_shape=[jax.ShapeDtypeStruct((SEQ, D_MODEL), F32),
                   jax.ShapeDtypeStruct((SEQ, D_FF), BF16), jax.ShapeDtypeStruct((SEQ, D_FF), BF16),
                   jax.ShapeDtypeStruct((SEQ, D_FF), BF16),
                   jax.ShapeDtypeStruct((SEQ, D_MODEL), BF16),
                   jax.ShapeDtypeStruct((1, D_MODEL), F32)],
        scratch_shapes=[pltpu.VMEM((tm, D_MODEL), BF16), pltpu.VMEM((tm, D_MODEL), F32)],
        compiler_params=_cparams(("arbitrary", "arbitrary")),
    )(dxo, x, g, a, b, wgt, wut, wd)


def _mm_tn(pairs, name, tmm=256):
    m = pairs[0][0].shape[1]
    n_pairs = len(pairs)

    def body(*refs):
        ins, outs = refs[:2 * n_pairs], refs[2 * n_pairs:]
        for p in range(n_pairs):
            outs[p][...] = _dot_tn(ins[2 * p][...], ins[2 * p + 1][...]).astype(BF16)

    in_specs, out_specs, out_shape, args = [], [], [], []
    for a, b in pairs:
        n = b.shape[1]
        in_specs += [pl.BlockSpec((SEQ, tmm), lambda i: (0, i)), pl.BlockSpec((SEQ, n), lambda i: (0, 0))]
        out_specs.append(pl.BlockSpec((tmm, n), lambda i: (i, 0)))
        out_shape.append(jax.ShapeDtypeStruct((m, n), BF16))
        args += [a, b]
    return pl.pallas_call(body, name=name, grid=(m // tmm,), in_specs=in_specs, out_specs=out_specs,
                          out_shape=out_shape, compiler_params=_cparams(("parallel",)))(*args)


MIX_TM = 512


def _mixin_fwd(x, g, wint):
    tm = MIX_TM

    def body(x_ref, g_ref, w_ref, h_ref, q_ref, k_ref, v_ref, u_ref):
        h = _rms_fwd(x_ref[...], g_ref[...]).astype(BF16)
        h_ref[...] = h
        proj = _dot_nt(h, w_ref[...])
        q_ref[...] = proj[:, :ATTN_WIDTH].T
        k_ref[...] = proj[:, ATTN_WIDTH:ATTN_WIDTH + KV_WIDTH]
        v_ref[...] = proj[:, ATTN_WIDTH + KV_WIDTH:ATTN_WIDTH + 2 * KV_WIDTH]
        u_ref[...] = proj[:, ATTN_WIDTH + 2 * KV_WIDTH:]

    row = lambda i: (i, 0)
    return pl.pallas_call(
        body, name="mixin_fwd", grid=(SEQ // tm,),
        in_specs=[pl.BlockSpec((tm, D_MODEL), row), pl.BlockSpec((1, D_MODEL), lambda i: (0, 0)),
                  pl.BlockSpec((IN_WIDTH, D_MODEL), lambda i: (0, 0))],
        out_specs=[pl.BlockSpec((tm, D_MODEL), row), pl.BlockSpec((ATTN_WIDTH, tm), lambda i: (0, i)),
                   pl.BlockSpec((tm, KV_WIDTH), row), pl.BlockSpec((tm, KV_WIDTH), row),
                   pl.BlockSpec((tm, SSM_WIDTH), row)],
        out_shape=[jax.ShapeDtypeStruct((SEQ, D_MODEL), BF16), jax.ShapeDtypeStruct((ATTN_WIDTH, SEQ), F32),
                   jax.ShapeDtypeStruct((SEQ, KV_WIDTH), F32), jax.ShapeDtypeStruct((SEQ, KV_WIDTH), F32),
                   jax.ShapeDtypeStruct((SEQ, SSM_WIDTH), F32)],
        compiler_params=_cparams(("parallel",)),
    )(x, g, wint)


def _mixin_bwd(dqt, dk, dv, du, wint, x, g, dres):
    tm = MIX_TM

    def body(dq_ref, dk_ref, dv_ref, du_ref, w_ref, x_ref, g_ref, dres_ref, dx_ref, dp_ref, dg_ref):
        i = pl.program_id(0)
        dp = jnp.concatenate([dq_ref[...].T, dk_ref[...], dv_ref[...], du_ref[...]], axis=-1).astype(BF16)
        dp_ref[...] = dp
        dh = _dot(dp, w_ref[...])
        dx, dg = _rms_bwd(dh, x_ref[...], g_ref[...])
        dx_ref[...] = dres_ref[...] + dx

        @pl.when(i == 0)
        def _():
            dg_ref[...] = dg

        @pl.when(i != 0)
        def _():
            dg_ref[...] += dg

    row = lambda i: (i, 0)
    const = lambda i: (0, 0)
    return pl.pallas_call(
        body, name="mixin_bwd", grid=(SEQ // tm,),
        in_specs=[pl.BlockSpec((ATTN_WIDTH, tm), lambda i: (0, i)), pl.BlockSpec((tm, KV_WIDTH), row),
                  pl.BlockSpec((tm, KV_WIDTH), row), pl.BlockSpec((tm, SSM_WIDTH), row),
                  pl.BlockSpec((IN_WIDTH, D_MODEL), const), pl.BlockSpec((tm, D_MODEL), row),
                  pl.BlockSpec((1, D_MODEL), const), pl.BlockSpec((tm, D_MODEL), row)],
        out_specs=[pl.BlockSpec((tm, D_MODEL), row), pl.BlockSpec((tm, IN_WIDTH), row),
                   pl.BlockSpec((1, D_MODEL), const)],
        out_shape=[jax.ShapeDtypeStruct((SEQ, D_MODEL), F32), jax.ShapeDtypeStruct((SEQ, IN_WIDTH), BF16),
                   jax.ShapeDtypeStruct((1, D_MODEL), F32)],
        compiler_params=_cparams(("arbitrary",)),
    )(dqt, dk, dv, du, wint, x, g, dres)


N_QBLOCKS = SEQ // WINDOW
GROUP = ATTN_HEADS // KV_HEADS
SCALE = HEAD_DIM ** -0.5


def _alibi_slope(h):
    return 2.0 ** (-8.0 * (h + 1) / ATTN_HEADS)


def _window_masks(n):
    s_idx = lax.broadcasted_iota(jnp.int32, (3 * WINDOW, WINDOW), 0)
    t_idx = lax.broadcasted_iota(jnp.int32, (3 * WINDOW, WINDOW), 1)
    absrel = jnp.abs(s_idx - WINDOW - t_idx)
    key_pos = n * WINDOW - WINDOW + s_idx
    valid = (absrel <= WINDOW) & (key_pos >= 0) & (key_pos < SEQ)
    return jnp.where(valid, absrel.astype(F32), MASKED_DISTANCE)


def _group_cols(ref, r0, gi):
    return jnp.concatenate(
        [ref[(gi * GROUP + hh) * HEAD_DIM:(gi * GROUP + hh + 1) * HEAD_DIM, pl.ds(r0, WINDOW)].astype(BF16)
         for hh in range(GROUP)], axis=1)


def _group_probs(qgt, kw, dist, gi, sk_ref):
    bias = jnp.concatenate([-_alibi_slope(gi * GROUP + hh) * dist for hh in range(GROUP)], axis=1)
    sink = jnp.concatenate([jnp.full((1, WINDOW), sk_ref[0, gi * GROUP + hh], F32) for hh in range(GROUP)], axis=1)
    s = _dot(kw, qgt) * SCALE + bias
    m = jnp.maximum(jnp.max(s, axis=0, keepdims=True), sink)
    p = jnp.exp(s - m)
    ps = jnp.exp(sink - m)
    inv = 1.0 / (jnp.sum(p, axis=0, keepdims=True) + ps)
    return p * inv, ps * inv


def _pad_window(src_ref, dst_ref):
    zeros = jnp.zeros((WINDOW, KV_WIDTH), BF16)
    dst_ref[0:WINDOW, :] = zeros
    dst_ref[WINDOW + SEQ:, :] = zeros
    dst_ref[WINDOW:WINDOW + SEQ, :] = src_ref[...].astype(BF16)


def _attn_fwd(qt, k, v, sinks, after=None):
    deps = [] if after is None else [after]

    def body(sk_ref, qt_ref, k_ref, v_ref, *rest):
        o_ref, kp_ref, vp_ref = rest[len(deps):]
        _pad_window(k_ref, kp_ref)
        _pad_window(v_ref, vp_ref)

        def blk(n, carry):
            r0 = pl.multiple_of(n * WINDOW, WINDOW)
            dist = _window_masks(n)
            for gi in range(KV_HEADS):
                kw = kp_ref[pl.ds(r0, 3 * WINDOW), gi * HEAD_DIM:(gi + 1) * HEAD_DIM]
                vw = vp_ref[pl.ds(r0, 3 * WINDOW), gi * HEAD_DIM:(gi + 1) * HEAD_DIM]
                pr, _ = _group_probs(_group_cols(qt_ref, r0, gi), kw, dist, gi, sk_ref)
                og = _dot_tn(pr.astype(BF16), vw)
                for hh in range(GROUP):
                    h = gi * GROUP + hh
                    o_ref[pl.ds(r0, WINDOW), h * HEAD_DIM:(h + 1) * HEAD_DIM] = og[hh * WINDOW:(hh + 1) * WINDOW]
            return carry

        lax.fori_loop(0, N_QBLOCKS, blk, 0)

    vmem = pl.BlockSpec(memory_space=pltpu.VMEM)
    return pl.pallas_call(
        body, name="attn_fwd",
        in_specs=[pl.BlockSpec(memory_space=pltpu.SMEM), vmem, vmem, vmem]
        + [pl.BlockSpec(memory_space=pl.ANY)] * len(deps), out_specs=vmem,
        out_shape=jax.ShapeDtypeStruct((SEQ, ATTN_WIDTH), F32),
        scratch_shapes=[pltpu.VMEM((SEQ + 2 * WINDOW, KV_WIDTH), BF16)] * 2,
        compiler_params=_cparams(),
    )(sinks, qt, k, v, *deps)


def _attn_bwd(qt, k, v, sinks, dot_):
    def body(sk_ref, qt_ref, k_ref, v_ref, dot_ref, dqt_ref, dk_ref, dv_ref, dsk_ref,
             dsk_acc, kp_ref, vp_ref, dkp_ref, dvp_ref):
        _pad_window(k_ref, kp_ref)
        _pad_window(v_ref, vp_ref)
        dkp_ref[...] = jnp.zeros_like(dkp_ref)
        dvp_ref[...] = jnp.zeros_like(dvp_ref)
        dsk_acc[...] = jnp.zeros_like(dsk_acc)

        def blk(n, carry):
            r0 = pl.multiple_of(n * WINDOW, WINDOW)
            dist = _window_masks(n)
            for gi in range(KV_HEADS):
                gcols = slice(gi * HEAD_DIM, (gi + 1) * HEAD_DIM)
                kw = kp_ref[pl.ds(r0, 3 * WINDOW), gcols]
                vw = vp_ref[pl.ds(r0, 3 * WINDOW), gcols]
                qgt = _group_cols(qt_ref, r0, gi)
                dogt = _group_cols(dot_ref, r0, gi)
                pr, psink = _group_probs(qgt, kw, dist, gi, sk_ref)
                dp = _dot(vw, dogt)
                delta = jnp.sum(pr * dp, axis=0, keepdims=True)
                ds = (pr * (dp - delta)).astype(BF16)
                dsk_acc[gi:gi + 1, :] += -(psink * delta)
                dqgt = _dot_tn(kw, ds) * SCALE
                for hh in range(GROUP):
                    h = gi * GROUP + hh
                    dqt_ref[h * HEAD_DIM:(h + 1) * HEAD_DIM, pl.ds(r0, WINDOW)] = dqgt[:, hh * WINDOW:(hh + 1) * WINDOW]
                dkp_ref[pl.ds(r0, 3 * WINDOW), gcols] += _dot_nt(ds, qgt) * SCALE
                dvp_ref[pl.ds(r0, 3 * WINDOW), gcols] += _dot_nt(pr.astype(BF16), dogt)
            return carry

        lax.fori_loop(0, N_QBLOCKS, blk, 0)
        for h in range(ATTN_HEADS):
            gi, hh = divmod(h, GROUP)
            dsk_ref[:, h:h + 1] = jnp.sum(dsk_acc[gi:gi + 1, hh * WINDOW:(hh + 1) * WINDOW], axis=1, keepdims=True)
        dk_ref[...] = dkp_ref[WINDOW:WINDOW + SEQ, :]
        dv_ref[...] = dvp_ref[WINDOW:WINDOW + SEQ, :]

    vmem = pl.BlockSpec(memory_space=pltpu.VMEM)
    padded = (SEQ + 2 * WINDOW, KV_WIDTH)
    return pl.pallas_call(
        body, name="attn_bwd",
        in_specs=[pl.BlockSpec(memory_space=pltpu.SMEM), vmem, vmem, vmem, vmem],
        out_specs=[vmem, vmem, vmem, vmem],
        out_shape=[jax.ShapeDtypeStruct((ATTN_WIDTH, SEQ), F32),
                   jax.ShapeDtypeStruct((SEQ, KV_WIDTH), F32), jax.ShapeDtypeStruct((SEQ, KV_WIDTH), F32),
                   jax.ShapeDtypeStruct((1, ATTN_HEADS), F32)],
        scratch_shapes=[pltpu.VMEM((KV_HEADS, GROUP * WINDOW), F32), pltpu.VMEM(padded, BF16),
                        pltpu.VMEM(padded, BF16), pltpu.VMEM(padded, F32), pltpu.VMEM(padded, F32)],
        compiler_params=_cparams(),
    )(sinks, qt, k, v, dot_)


HALF_LANES = LANES // 2
BLOCK_ROWS = 32


def _embed_block(bt, q):
    z = jnp.zeros((16, HALF_LANES), bt.dtype)
    blk = jnp.concatenate([jnp.concatenate([bt[:16], z], axis=1), jnp.concatenate([z, bt[16:]], axis=1)], axis=0)
    parts = [jnp.zeros((BLOCK_ROWS * q, LANES), bt.dtype)] if q else []
    parts.append(blk)
    if q < 3:
        parts.append(jnp.zeros((BLOCK_ROWS * (3 - q), LANES), bt.dtype))
    return jnp.concatenate(parts, axis=0)


def _extract_block(m, q):
    blk = m[BLOCK_ROWS * q:BLOCK_ROWS * (q + 1)]
    return jnp.concatenate([blk[:16, :HALF_LANES], blk[16:, HALF_LANES:]], axis=0)


def _ssm_prep(lam_re, lam_im, log_dt, bt_re, bt_im, c_re, c_im):
    nb = 2 * N_LANE_BLOCKS

    def body(lr_ref, li_ref, ldt_ref, btr_ref, bti_ref, ctr_ref, cti_ref, ar_ref, ai_ref, bb_ref, cc_ref):
        lr = jnp.minimum(lr_ref[...], LAMBDA_RE_MAX)
        li = li_ref[...]
        dt = jnp.exp(ldt_ref[...])
        mag = jnp.exp(lr * dt)
        ar = mag * jnp.cos(li * dt)
        ai = mag * jnp.sin(li * dt)
        den = lr * lr + li * li
        cr = ((ar - 1.0) * lr + ai * li) / den
        ci = (ai * lr - (ar - 1.0) * li) / den
        ar_ref[...] = ar
        ai_ref[...] = ai
        for i in range(nb):
            q = i % 4
            rows = slice(BLOCK_ROWS * i, BLOCK_ROWS * (i + 1))
            br = _embed_block(btr_ref[rows, :], q)
            bi = _embed_block(bti_ref[rows, :], q)
            cri, cii = cr[i:i + 1, :], ci[i:i + 1, :]
            bb_ref[i] = jnp.concatenate([cri * br - cii * bi, cri * bi + cii * br], axis=1).astype(BF16)
            cc_ref[i] = jnp.concatenate([_embed_block(ctr_ref[rows, :], q).T,
                                         -_embed_block(cti_ref[rows, :], q).T], axis=0).astype(BF16)

    return pl.pallas_call(
        body, name="ssm_prep",
        out_shape=[jax.ShapeDtypeStruct((nb, LANES), F32), jax.ShapeDtypeStruct((nb, LANES), F32),
                   jax.ShapeDtypeStruct((nb, LANES, 2 * LANES), BF16),
                   jax.ShapeDtypeStruct((nb, 2 * LANES, LANES), BF16)],
        compiler_params=_cparams(),
    )(lam_re, lam_im, log_dt, bt_re, bt_im, c_re, c_im)


def _ssm_prep_bwd(lam_re, lam_im, log_dt, bt_re, bt_im, dar, dai, dbb, dcc):
    nb = 2 * N_LANE_BLOCKS

    def body(lr_ref, li_ref, ldt_ref, btr_ref, bti_ref, dar_ref, dai_ref, dbb_ref, dcc_ref,
             glr_ref, gli_ref, gdt_ref, gbr_ref, gbi_ref, gcre_ref, gcim_ref, gcr_s, gci_s):
        lam = lr_ref[...]
        lr = jnp.minimum(lam, LAMBDA_RE_MAX)
        li = li_ref[...]
        dt = jnp.exp(ldt_ref[...])
        mag = jnp.exp(lr * dt)
        cs = jnp.cos(li * dt)
        sn = jnp.sin(li * dt)
        ar = mag * cs
        ai = mag * sn
        den = lr * lr + li * li
        nr = (ar - 1.0) * lr + ai * li
        ni = ai * lr - (ar - 1.0) * li
        cr = nr / den
        ci = ni / den
        for i in range(nb):
            q = i % 4
            rows = slice(BLOCK_ROWS * i, BLOCK_ROWS * (i + 1))
            br = _embed_block(btr_ref[rows, :], q)
            bi = _embed_block(bti_ref[rows, :], q)
            gbbr = dbb_ref[i, :, :LANES]
            gbbi = dbb_ref[i, :, LANES:]
            cri, cii = cr[i:i + 1, :], ci[i:i + 1, :]
            gcr_s[i:i + 1, :] = jnp.sum(gbbr * br + gbbi * bi, axis=0, keepdims=True)
            gci_s[i:i + 1, :] = jnp.sum(gbbi * br - gbbr * bi, axis=0, keepdims=True)
            gbr_ref[rows, :] = _extract_block(cri * gbbr + cii * gbbi, q)
            gbi_ref[rows, :] = _extract_block(cri * gbbi - cii * gbbr, q)
            gcre_ref[rows, :] = _extract_block(dcc_ref[i, :LANES, :].T, q)
            gcim_ref[rows, :] = -_extract_block(dcc_ref[i, LANES:, :].T, q)
        g_cr = gcr_s[...]
        g_ci = gci_s[...]
        g_nr = g_cr / den
        g_ni = g_ci / den
        g_den = -(g_cr * nr + g_ci * ni) / (den * den)
        g_ar = dar_ref[...] + g_nr * lr - g_ni * li
        g_ai = dai_ref[...] + g_nr * li + g_ni * lr
        g_lr = g_nr * (ar - 1.0) + g_ni * ai + g_den * 2.0 * lr
        g_li = g_nr * ai - g_ni * (ar - 1.0) + g_den * 2.0 * li
        g_mag = g_ar * cs + g_ai * sn
        g_th = (g_ai * cs - g_ar * sn) * mag
        g_lr = g_lr + g_mag * mag * dt
        g_li = g_li + g_th * dt
        g_dt = g_mag * mag * lr + g_th * li
        glr_ref[...] = jnp.where(lam < LAMBDA_RE_MAX, g_lr, 0.0)
        gli_ref[...] = g_li
        gl = g_dt * dt
        half = LANES // 2
        gdt_ref[:, 0:1] = jnp.sum(gl[:, :half], axis=1, keepdims=True)
        gdt_ref[:, 1:2] = jnp.sum(gl[:, half:], axis=1, keepdims=True)

    rows_shape = jax.ShapeDtypeStruct((nb * BLOCK_ROWS, HALF_LANES), F32)
    return pl.pallas_call(
        body, name="ssm_prep_bwd",
        out_shape=[jax.ShapeDtypeStruct((nb, LANES), F32), jax.ShapeDtypeStruct((nb, LANES), F32),
                   jax.ShapeDtypeStruct((nb, 2), F32), rows_shape, rows_shape, rows_shape, rows_shape],
        scratch_shapes=[pltpu.VMEM((nb, LANES), F32), pltpu.VMEM((nb, LANES), F32)],
        compiler_params=_cparams(),
    )(lam_re, lam_im, log_dt, bt_re, bt_im, dar, dai, dbb, dcc)


def _cmul(ar, ai, br, bi):
    return ar * br - ai * bi, ar * bi + ai * br


def _interleave_rows(src_ref, dst_ref):
    def step(j, carry):
        dst_ref[pl.ds(pl.multiple_of(j * 8, 8), 8), :] = src_ref[pl.ds(j, 8, stride=SCAN_CHUNK), :]
        return carry
    lax.fori_loop(0, SCAN_CHUNK, step, 0, unroll=4)


def _deinterleave_rows(src_ref, dst_ref):
    def step(j, carry):
        dst_ref[pl.ds(j, 8, stride=SCAN_CHUNK), :] = src_ref[pl.ds(pl.multiple_of(j * 8, 8), 8), :]
        return carry
    lax.fori_loop(0, SCAN_CHUNK, step, 0, unroll=4)


def _scan_inplace(re_ref, im_ref, a_re, a_im, reverse):
    nq = len(a_re)
    ch = SCAN_CHUNK
    ab_re = [jnp.broadcast_to(a, (8, LANES)) for a in a_re]
    ab_im = [jnp.broadcast_to(a, (8, LANES)) for a in a_im]

    def rows(j):
        jj = (ch - 1 - j) if reverse else j
        return pl.ds(pl.multiple_of(jj * 8, 8), 8)

    def sweep(init, store):
        def step(j, st):
            out = []
            r = rows(j)
            for qi in range(nq):
                xr, xi = st[2 * qi], st[2 * qi + 1]
                pr, pi = _cmul(ab_re[qi], ab_im[qi], xr, xi)
                xr = pr + re_ref[qi, r, :]
                xi = pi + im_ref[qi, r, :]
                if store:
                    re_ref[qi, r, :] = xr
                    im_ref[qi, r, :] = xi
                out += [xr, xi]
            return tuple(out)
        return lax.fori_loop(0, ch, step, tuple(init), unroll=2)

    zeros = [jnp.zeros((8, LANES), F32)] * (2 * nq)
    finals = sweep(zeros, store=False)

    row_id = lax.broadcasted_iota(jnp.int32, (8, LANES), 0)
    carries = []
    for qi in range(nq):
        pr, pi = ab_re[qi], ab_im[qi]
        for _ in range(8):
            pr, pi = _cmul(pr, pi, pr, pi)
        fr, fi = finals[2 * qi], finals[2 * qi + 1]
        sr = jnp.zeros((8, LANES), F32)
        si = jnp.zeros((8, LANES), F32)
        for _ in range(7):
            tr, ti = _cmul(pr, pi, sr, si)
            tr, ti = tr + fr, ti + fi
            if reverse:
                sr = jnp.where(row_id == 7, 0.0, pltpu.roll(tr, 7, axis=0))
                si = jnp.where(row_id == 7, 0.0, pltpu.roll(ti, 7, axis=0))
            else:
                sr = jnp.where(row_id == 0, 0.0, pltpu.roll(tr, 1, axis=0))
                si = jnp.where(row_id == 0, 0.0, pltpu.roll(ti, 1, axis=0))
        carries += [sr, si]
    sweep(carries, store=True)


SSM_Q = 4


def _ssm_fwd(u, are, aim, bb, cc, dskip, after=None):
    nq = SSM_Q
    deps = [] if after is None else [after]

    def body(u_ref, ar_ref, ai_ref, bb_ref, cc_ref, d_ref, *rest):
        y_ref, xr_ref, xi_ref, sre, sim, up, yp = rest[len(deps):]
        _interleave_rows(u_ref, up)
        uf = up[...]
        ub = uf.astype(BF16)
        yp[...] = d_ref[...] * uf
        for d in range(2):
            for qi in range(nq):
                sre[qi] = _dot(ub, bb_ref[d, qi, :, :LANES])
                sim[qi] = _dot(ub, bb_ref[d, qi, :, LANES:])
            _scan_inplace(sre, sim, [ar_ref[d, qi] for qi in range(nq)], [ai_ref[d, qi] for qi in range(nq)],
                          reverse=(d == 1))
            for qi in range(nq):
                xrb = sre[qi].astype(BF16)
                xib = sim[qi].astype(BF16)
                xr_ref[d, qi] = xrb
                xi_ref[d, qi] = xib
                yp[...] += _dot(xrb, cc_ref[d, qi, :LANES, :]) + _dot(xib, cc_ref[d, qi, LANES:, :])
        _deinterleave_rows(yp, y_ref)

    blk4 = lambda k: (0, k, 0, 0)
    return pl.pallas_call(
        body, name="ssm_fwd", grid=(SSM_WIDTH // LANES,),
        in_specs=[pl.BlockSpec((SEQ, LANES), lambda k: (0, k)),
                  pl.BlockSpec((2, nq, 1, LANES), blk4), pl.BlockSpec((2, nq, 1, LANES), blk4),
                  pl.BlockSpec((2, nq, LANES, 2 * LANES), blk4), pl.BlockSpec((2, nq, 2 * LANES, LANES), blk4),
                  pl.BlockSpec((1, LANES), lambda k: (0, k))] + [pl.BlockSpec(memory_space=pl.ANY)] * len(deps),
        out_specs=[pl.BlockSpec((SEQ, LANES), lambda k: (0, k)),
                   pl.BlockSpec((2, nq, SEQ, LANES), blk4), pl.BlockSpec((2, nq, SEQ, LANES), blk4)],
        out_shape=[jax.ShapeDtypeStruct((SEQ, SSM_WIDTH), F32),
                   jax.ShapeDtypeStruct((2, N_LANE_BLOCKS, SEQ, LANES), BF16),
                   jax.ShapeDtypeStruct((2, N_LANE_BLOCKS, SEQ, LANES), BF16)],
        scratch_shapes=[pltpu.VMEM((nq, SEQ, LANES), F32), pltpu.VMEM((nq, SEQ, LANES), F32),
                        pltpu.VMEM((SEQ, LANES), F32), pltpu.VMEM((SEQ, LANES), F32)],
        compiler_params=_cparams(("parallel",)),
    )(u, are, aim, bb, cc, dskip, *deps)


def _ssm_bwd(dy, u, xr, xi, are, aim, bb, cc, dskip, after=None):
    nq = SSM_Q
    body_rows = SEQ - 8
    deps = [] if after is None else [after]

    def body(dy_ref, u_ref, xr_ref, xi_ref, ar_ref, ai_ref, bb_ref, cc_ref, d_ref, *rest):
        du_ref, dd_ref, dcc_ref, dbb_ref, dar_ref, dai_ref, sre, sim, up, dyp, dup = rest[len(deps):]
        _interleave_rows(u_ref, up)
        _interleave_rows(dy_ref, dyp)
        dyf = dyp[...]
        uf = up[...]
        dyb = dyf.astype(BF16)
        ub = uf.astype(BF16)
        dd_ref[...] = jnp.sum(dyf * uf, axis=0, keepdims=True)
        dup[...] = d_ref[...] * dyf
        row8 = lax.broadcasted_iota(jnp.int32, (8, LANES), 0)
        for d in range(2):
            for qi in range(nq):
                dx = _dot_nt(dyb, cc_ref[d, qi])
                sre[qi] = dx[:, :LANES]
                sim[qi] = dx[:, LANES:]
                dcc_ref[d, qi] = _dot_tn(jnp.concatenate([xr_ref[d, qi], xi_ref[d, qi]], axis=1), dyb)
            _scan_inplace(sre, sim, [ar_ref[d, qi] for qi in range(nq)], [-ai_ref[d, qi] for qi in range(nq)],
                          reverse=(d == 0))
            for qi in range(nq):
                gr = sre[qi]
                gi = sim[qi]
                xrf = xr_ref[d, qi].astype(F32)
                xif = xi_ref[d, qi].astype(F32)
                if d == 0:
                    g_main_r, g_main_i = gr[8:], gi[8:]
                    x_main_r, x_main_i = xrf[:body_rows], xif[:body_rows]
                    g_edge_r, g_edge_i = gr[:8], gi[:8]
                    x_edge_r = jnp.where(row8 == 0, 0.0, pltpu.roll(xrf[body_rows:], 1, axis=0))
                    x_edge_i = jnp.where(row8 == 0, 0.0, pltpu.roll(xif[body_rows:], 1, axis=0))
                else:
                    g_main_r, g_main_i = gr[:body_rows], gi[:body_rows]
                    x_main_r, x_main_i = xrf[8:], xif[8:]
                    g_edge_r, g_edge_i = gr[body_rows:], gi[body_rows:]
                    x_edge_r = jnp.where(row8 == 7, 0.0, pltpu.roll(xrf[:8], 7, axis=0))
                    x_edge_i = jnp.where(row8 == 7, 0.0, pltpu.roll(xif[:8], 7, axis=0))
                dar_ref[d, qi] = (jnp.sum(g_main_r * x_main_r + g_main_i * x_main_i, axis=0, keepdims=True)
                                  + jnp.sum(g_edge_r * x_edge_r + g_edge_i * x_edge_i, axis=0, keepdims=True))
                dai_ref[d, qi] = (jnp.sum(g_main_i * x_main_r - g_main_r * x_main_i, axis=0, keepdims=True)
                                  + jnp.sum(g_edge_i * x_edge_r - g_edge_r * x_edge_i, axis=0, keepdims=True))
                gb = jnp.concatenate([gr, gi], axis=1).astype(BF16)
                dup[...] += _dot_nt(gb, bb_ref[d, qi])
                dbb_ref[d, qi] = _dot_tn(ub, gb)
        _deinterleave_rows(dup, du_ref)

    blk4 = lambda k: (0, k, 0, 0)
    col = lambda k: (0, k)
    bb_spec = pl.BlockSpec((2, nq, LANES, 2 * LANES), blk4)
    cc_spec = pl.BlockSpec((2, nq, 2 * LANES, LANES), blk4)
    a_spec = pl.BlockSpec((2, nq, 1, LANES), blk4)
    x_spec = pl.BlockSpec((2, nq, SEQ, LANES), blk4)
    a_shape = jax.ShapeDtypeStruct((2, N_LANE_BLOCKS, 1, LANES), F32)
    return pl.pallas_call(
        body, name="ssm_bwd", grid=(SSM_WIDTH // LANES,),
        in_specs=[pl.BlockSpec((SEQ, LANES), col), pl.BlockSpec((SEQ, LANES), col), x_spec, x_spec,
                  a_spec, a_spec, bb_spec, cc_spec, pl.BlockSpec((1, LANES), col)]
        + [pl.BlockSpec(memory_space=pl.ANY)] * len(deps),
        out_specs=[pl.BlockSpec((SEQ, LANES), col), pl.BlockSpec((1, LANES), col),
                   cc_spec, bb_spec, a_spec, a_spec],
        out_shape=[jax.ShapeDtypeStruct((SEQ, SSM_WIDTH), F32), jax.ShapeDtypeStruct((1, SSM_WIDTH), F32),
                   jax.ShapeDtypeStruct((2, N_LANE_BLOCKS, 2 * LANES, LANES), F32),
                   jax.ShapeDtypeStruct((2, N_LANE_BLOCKS, LANES, 2 * LANES), F32), a_shape, a_shape],
        scratch_shapes=[pltpu.VMEM((nq, SEQ, LANES), F32), pltpu.VMEM((nq, SEQ, LANES), F32),
                        pltpu.VMEM((SEQ, LANES), F32), pltpu.VMEM((SEQ, LANES), F32), pltpu.VMEM((SEQ, LANES), F32)],
        compiler_params=_cparams(("parallel",)),
    )(dy, u, xr, xi, are, aim, bb, cc, dskip, *deps)


GELU_C = 0.7978845608028654
GELU_K = 0.044715


def _gelu(y):
    return 0.5 * y * (1.0 + jnp.tanh(GELU_C * (y + GELU_K * y * y * y)))


def _gelu_grad(y):
    t = jnp.tanh(GELU_C * (y + GELU_K * y * y * y))
    return 0.5 * (1.0 + t) + 0.5 * y * (1.0 - t * t) * GELU_C * (1.0 + 3.0 * GELU_K * y * y)


def _mixout_fwd(o, y, glu_w, glu_b, gan, gsn, wout, x1):
    tm = MIX_TM

    def body(o_ref, y_ref, gw_ref, gb_ref, gan_ref, gsn_ref, w_ref, x1_ref, x2_ref, mx_ref):
        yg = _gelu(y_ref[...])
        z = _dot(yg.astype(BF16), gw_ref[...]) + gb_ref[...]
        so = yg * _sigmoid(z)
        na = _rms_fwd(o_ref[...], gan_ref[...])
        ns = _rms_fwd(so, gsn_ref[...])
        mixed = jnp.concatenate([na, ns], axis=-1).astype(BF16)
        mx_ref[...] = mixed
        x2_ref[...] = x1_ref[...] + _dot(mixed, w_ref[...])

    row = lambda i: (i, 0)
    const = lambda i: (0, 0)
    return pl.pallas_call(
        body, name="mixout_fwd", grid=(SEQ // tm,),
        in_specs=[pl.BlockSpec((tm, ATTN_WIDTH), row), pl.BlockSpec((tm, SSM_WIDTH), row),
                  pl.BlockSpec((SSM_WIDTH, SSM_WIDTH), const), pl.BlockSpec((1, SSM_WIDTH), const),
                  pl.BlockSpec((1, ATTN_WIDTH), const), pl.BlockSpec((1, SSM_WIDTH), const),
                  pl.BlockSpec((D_MODEL, D_MODEL), const), pl.BlockSpec((tm, D_MODEL), row)],
        out_specs=[pl.BlockSpec((tm, D_MODEL), row), pl.BlockSpec((tm, D_MODEL), row)],
        out_shape=[jax.ShapeDtypeStruct((SEQ, D_MODEL), F32), jax.ShapeDtypeStruct((SEQ, D_MODEL), BF16)],
        compiler_params=_cparams(("parallel",)),
    )(o, y, glu_w, glu_b, gan, gsn, wout, x1)


def _mixout_bwd(dx2, o, y, glu_w, glu_b, gan, gsn, wout):
    tm = MIX_TM

    def body(dx2_ref, o_ref, y_ref, gw_ref, gb_ref, gan_ref, gsn_ref, w_ref,
             do_ref, dy_ref, dz_ref, yg_ref, dxb_ref, dgan_ref, dgsn_ref, dgb_ref):
        i = pl.program_id(0)
        dxb = dx2_ref[...].astype(BF16)
        dxb_ref[...] = dxb
        dmixed = _dot_nt(dxb, w_ref[...])
        do, dgan = _rms_bwd(dmixed[:, :ATTN_WIDTH], o_ref[...], gan_ref[...])
        do_ref[...] = do.T
        yv = y_ref[...]
        yg = _gelu(yv)
        ygb = yg.astype(BF16)
        yg_ref[...] = ygb
        sg = _sigmoid(_dot(ygb, gw_ref[...]) + gb_ref[...])
        dso, dgsn = _rms_bwd(dmixed[:, ATTN_WIDTH:], yg * sg, gsn_ref[...])
        dz = dso * yg * sg * (1.0 - sg)
        dzb = dz.astype(BF16)
        dz_ref[...] = dzb
        dyg = dso * sg + _dot_nt(dzb, gw_ref[...])
        dy_ref[...] = dyg * _gelu_grad(yv)
        dgb = jnp.sum(dz, axis=0, keepdims=True)

        @pl.when(i == 0)
        def _():
            dgan_ref[...] = dgan
            dgsn_ref[...] = dgsn
            dgb_ref[...] = dgb

        @pl.when(i != 0)
        def _():
            dgan_ref[...] += dgan
            dgsn_ref[...] += dgsn
            dgb_ref[...] += dgb

    row = lambda i: (i, 0)
    const = lambda i: (0, 0)
    return pl.pallas_call(
        body, name="mixout_bwd", grid=(SEQ // tm,),
        in_specs=[pl.BlockSpec((tm, D_MODEL), row), pl.BlockSpec((tm, ATTN_WIDTH), row),
                  pl.BlockSpec((tm, SSM_WIDTH), row),
                  pl.BlockSpec((SSM_WIDTH, SSM_WIDTH), const), pl.BlockSpec((1, SSM_WIDTH), const),
                  pl.BlockSpec((1, ATTN_WIDTH), const), pl.BlockSpec((1, SSM_WIDTH), const),
                  pl.BlockSpec((D_MODEL, D_MODEL), const)],
        out_specs=[pl.BlockSpec((ATTN_WIDTH, tm), lambda i: (0, i)), pl.BlockSpec((tm, SSM_WIDTH), row),
                   pl.BlockSpec((tm, SSM_WIDTH), row), pl.BlockSpec((tm, SSM_WIDTH), row),
                   pl.BlockSpec((tm, D_MODEL), row),
                   pl.BlockSpec((1, ATTN_WIDTH), const), pl.BlockSpec((1, SSM_WIDTH), const),
                   pl.BlockSpec((1, SSM_WIDTH), const)],
        out_shape=[jax.ShapeDtypeStruct((ATTN_WIDTH, SEQ), F32), jax.ShapeDtypeStruct((SEQ, SSM_WIDTH), F32),
                   jax.ShapeDtypeStruct((SEQ, SSM_WIDTH), BF16), jax.ShapeDtypeStruct((SEQ, SSM_WIDTH), BF16),
                   jax.ShapeDtypeStruct((SEQ, D_MODEL), BF16),
                   jax.ShapeDtypeStruct((1, ATTN_WIDTH), F32), jax.ShapeDtypeStruct((1, SSM_WIDTH), F32),
                   jax.ShapeDtypeStruct((1, SSM_WIDTH), F32)],
        compiler_params=_cparams(("arbitrary",)),
    )(dx2, o, y, glu_w, glu_b, gan, gsn, wout)


def _local_step(x, target, w, p, late_weights, early_grads, after=None, midway=None):
    x1, h1, a1, b1 = _ffn_fwd(x, p["norm_ffn1"], w["wgt1"], w["wut1"], w["wd1"], "ffn1_fwd", after=after)
    h2, q, k, v, u = _mixin_fwd(x1, p["norm_mix"], w["wint"])

    lam_re = p["ssm_lambda_re"].reshape(2 * N_LANE_BLOCKS, LANES)
    lam_im = p["ssm_lambda_im"].reshape(2 * N_LANE_BLOCKS, LANES)
    log_dt = jnp.repeat(p["ssm_log_dt"].reshape(2, 32), 64, axis=-1).reshape(2 * N_LANE_BLOCKS, LANES)
    a_re, a_im, bb, cc = _ssm_prep(lam_re, lam_im, log_dt, p["ssm_b_re"], p["ssm_b_im"],
                                   p["ssm_c_re"], p["ssm_c_im"])
    shape_a = (2, N_LANE_BLOCKS, 1, LANES)
    a_re4, a_im4 = a_re.reshape(shape_a), a_im.reshape(shape_a)
    bb4 = bb.reshape(2, N_LANE_BLOCKS, LANES, 2 * LANES)
    cc4 = cc.reshape(2, N_LANE_BLOCKS, 2 * LANES, LANES)
    dskip = p["ssm_d"].T.reshape(1, SSM_WIDTH)
    y, xr, xi = _ssm_fwd(u, a_re4, a_im4, bb4, cc4, dskip)
    o = _attn_fwd(q, k, v, p["attn_sinks"], after=None if midway is None else midway(y))

    w2 = late_weights(o)
    x2, mixed = _mixout_fwd(o, y, w2["glu"], p["ssm_glu_b"], p["attn_out_norm"], p["ssm_out_norm"], w2["wout"], x1)
    dx3, h3, a3, b3, loss, d_final = _ffn_fwd(x2, p["norm_ffn2"], w2["wgt2"], w2["wut2"], w2["wd2"], "ffn2_fwd",
                                              head=(p["final_norm"], target))
    dx2, da3, db3, s3, df3, d_n2 = _ffn_bwd_act(dx3, x2, p["norm_ffn2"], a3, b3, w2["wgt2"], w2["wut2"], w2["wd2"],
                                                "ffn2_bwd_act")
    g_wgt2, g_wut2, g_wd2 = _mm_tn([(da3, h3), (db3, h3), (s3, df3)], "ffn2_bwd_w")

    do, dy, dz, ygb, dx2b, d_gan, d_gsn, d_glub = _mixout_bwd(
        dx2, o, y, w2["glu"], p["ssm_glu_b"], p["attn_out_norm"], p["ssm_out_norm"], w2["wout"])
    (g_wout,) = _mm_tn([(mixed, dx2b)], "wout_bwd_w")
    (g_glu,) = _mm_tn([(ygb, dz)], "glu_bwd_w")
    sent = early_grads(dict(glu=g_glu, wout=g_wout, wgt2=g_wgt2, wut2=g_wut2, wd2=g_wd2))

    du, d_dskip, dcc, dbb, dar, dai = _ssm_bwd(dy, u, xr, xi, a_re4, a_im4, bb4, cc4, dskip, after=sent)
    nb = 2 * N_LANE_BLOCKS
    g_lre, g_lim, g_ldt, g_btr, g_bti, g_cre, g_cim = _ssm_prep_bwd(
        lam_re, lam_im, log_dt, p["ssm_b_re"], p["ssm_b_im"], dar.reshape(nb, LANES), dai.reshape(nb, LANES),
        dbb.reshape(nb, LANES, 2 * LANES), dcc.reshape(nb, 2 * LANES, LANES))

    dq, dk, dv, d_sinks = _attn_bwd(q, k, v, p["attn_sinks"], do)
    dx1, dproj, d_nmix = _mixin_bwd(dq, dk, dv, du, w["wint"], x1, p["norm_mix"], dx2)
    (g_wint,) = _mm_tn([(dproj, h2)], "win_bwd_w")

    dx0, da1, db1, s1, df1, d_n1 = _ffn_bwd_act(dx1, x, p["norm_ffn1"], a1, b1, w["wgt1"], w["wut1"], w["wd1"],
                                                "ffn1_bwd_act")
    g_wgt1, g_wut1, g_wd1 = _mm_tn([(da1, h1), (db1, h1), (s1, df1)], "ffn1_bwd_w")

    big = dict(wgt1=g_wgt1, wut1=g_wut1, wd1=g_wd1, wint=g_wint)
    small = dict(
        norm_ffn1=d_n1, norm_mix=d_nmix, attn_sinks=d_sinks,
        ssm_lambda_re=g_lre.reshape(64, 64), ssm_lambda_im=g_lim.reshape(64, 64),
        ssm_log_dt=g_ldt.reshape(2, 32), ssm_b_re=g_btr, ssm_b_im=g_bti, ssm_c_re=g_cre, ssm_c_im=g_cim,
        ssm_d=d_dskip.reshape(32, 16).T, ssm_glu_b=d_glub, attn_out_norm=d_gan, ssm_out_norm=d_gsn,
        norm_ffn2=d_n2, final_norm=d_final, loss=loss)
    return loss, dx0, big, small


BIG = dict(
    wgt1=("ffn1_w_gate", 352, 1024, True), wut1=("ffn1_w_up", 352, 1024, True), wd1=("ffn1_w_down", 352, 1024, False),
    wint=("w_in", 160, 1024, True), glu=("ssm_glu_w", 64, 512, False), wout=("w_out", 128, 1024, False),
    wgt2=("ffn2_w_gate", 352, 1024, True), wut2=("ffn2_w_up", 352, 1024, True), wd2=("ffn2_w_down", 352, 1024, False))

SMALL = dict(
    norm_ffn1=(1, 1024), norm_mix=(1, 1024), attn_sinks=(1, 8), ssm_lambda_re=(64, 64), ssm_lambda_im=(64, 64),
    ssm_log_dt=(2, 32), ssm_b_re=(1024, 64), ssm_b_im=(1024, 64), ssm_c_re=(1024, 64), ssm_c_im=(1024, 64),
    ssm_d=(16, 32), ssm_glu_b=(1, 512), attn_out_norm=(1, 512), ssm_out_norm=(1, 512), norm_ffn2=(1, 1024),
    final_norm=(1, 1024), loss=(1, 128))
SMALL_TRANSPOSED = ("ssm_b_re", "ssm_b_im", "ssm_d")
SMALL_PARAMS = tuple(n for n in SMALL if n != "loss")

SMALL_PAIRS = (("ssm_lambda_re", "ssm_lambda_im"), ("ssm_c_re", "ssm_c_im"), ("ssm_b_re", "ssm_b_im"))
SMALL_VECS = ("norm_ffn1", "norm_mix", "norm_ffn2", "final_norm", "ssm_glu_b", "attn_out_norm", "ssm_out_norm")
SMALL_TILES = ("ssm_log_dt", "attn_sinks", "ssm_d", "loss")


def _small_offsets():
    off, table = 0, {}
    for re, im in SMALL_PAIRS:
        table[re] = table[im] = off
        off += SMALL[re][0]
    for n in SMALL_VECS:
        table[n] = off
        off += SMALL[n][1] // LANES
    for n in SMALL_TILES:
        off = -(-off // 8) * 8
        table[n] = off
        off += SMALL[n][0]
    return table, off


SMALL_OFFSET, SMALL_USED_ROWS = _small_offsets()
SMALL_ROWS = -(-SMALL_USED_ROWS // (8 * N_DEV)) * 8 * N_DEV


def _cast_shards(shards):
    names = list(BIG)

    def body(*refs):
        ins, outs = refs[:len(names)], refs[len(names):]
        for idx in range(len(names)):
            outs[idx][...] = ins[idx][...].astype(BF16)

    return pl.pallas_call(
        body, name="cast_shards",
        out_shape=[jax.ShapeDtypeStruct((BIG[n][1], BIG[n][2]), BF16) for n in names],
        compiler_params=_cparams(),
    )(*[shards[n] for n in names])


def _peer(x, y, c, r):
    px = 1 - x if r & 4 else x
    py = 1 - y if r & 2 else y
    pc = 1 - c if r & 1 else c
    return px, py, pc


FIRST_GROUP = ("wgt1", "wut1", "wd1", "wint")
LATE_GROUP = ("glu", "wout", "wgt2", "wut2", "wd2")
N_PEERS = N_DEV - 1
ANY_SPEC = pl.BlockSpec(memory_space=pl.ANY)
HBM_SPEC = pl.BlockSpec(memory_space=pltpu.HBM)
SEM_SPEC = pl.BlockSpec(memory_space=pltpu.SEMAPHORE)
DATAFLOW_EFFECT = pltpu.SideEffectType.DATAFLOW_SIDE_EFFECTING


def _mesh_pos():
    x, y, c = lax.axis_index("x"), lax.axis_index("y"), lax.axis_index("c")
    return x, y, c, 4 * x + 2 * y + c


def _gather_first(first, late):
    nf, nl = len(first), len(late)

    def body(*refs):
        f_in, l_in = refs[:nf], refs[nf:nf + nl]
        f_out, l_out = refs[nf + nl:2 * nf + nl], refs[2 * nf + nl:2 * (nf + nl)]
        send_sems, recv_sems, local_sems = refs[2 * (nf + nl):]
        x, y, c, me = _mesh_pos()
        sibling = (x, y, 1 - c)
        chips = [(x, 1 - y), (1 - x, y), (1 - x, 1 - y)]

        def idx(px, py, pc):
            return 4 * px + 2 * py + pc

        def copy(k, s, block, to, src=None):
            slot = f_out[k].at[block]
            return pltpu.make_async_remote_copy(
                src_ref=slot if src is None else src, dst_ref=slot, send_sem=send_sems.at[k, s],
                recv_sem=recv_sems.at[k, s], device_id=to, device_id_type=MESH_ID)

        local = []
        for k in range(nf + nl):
            src, dst = (f_in[k], f_out[k]) if k < nf else (l_in[k - nf], l_out[k - nf])
            mine = pltpu.make_async_copy(src, dst.at[me], local_sems.at[k])
            mine.start()
            local.append(mine)
        sends = []
        for j, chip in enumerate(chips):
            for k in range(nf):
                sends.append(copy(k, 1 + j, me, (*chip, c), src=f_in[k]))
                sends[-1].start()
        for k in range(nf):
            sends.append(copy(k, 0, me, sibling, src=f_in[k]))
            sends[-1].start()
        for j, chip in enumerate(chips):
            for k in range(nf):
                copy(k, 1 + j, idx(*chip, c), (*chip, c)).wait_recv()
                sends.append(copy(k, 4 + j, idx(*chip, c), sibling))
                sends[-1].start()
        for k in range(nf):
            copy(k, 0, idx(*sibling), sibling).wait_recv()
        for j, chip in enumerate(chips):
            for k in range(nf):
                copy(k, 4 + j, idx(*chip, 1 - c), sibling).wait_recv()
        for cp in sends:
            cp.wait_send()
        for cp in local:
            cp.wait()

    return pl.pallas_call(
        body, name="gather_first",
        in_specs=[ANY_SPEC] * (nf + nl), out_specs=[ANY_SPEC] * (nf + nl),
        out_shape=[jax.ShapeDtypeStruct((N_DEV,) + s.shape, s.dtype) for s in list(first) + list(late)],
        scratch_shapes=[pltpu.SemaphoreType.DMA((nf, N_PEERS)), pltpu.SemaphoreType.DMA((nf, N_PEERS)),
                        pltpu.SemaphoreType.DMA((nf + nl,))],
        compiler_params=pltpu.CompilerParams(has_side_effects=True),
    )(*first, *late)


def _split_copy(src_refs, land_refs, send_sems, recv_sems, k, r, pos, scatter, receiving):
    x, y, c, me = pos
    px, py, pc = _peer(x, y, c, r)
    peer_idx = 4 * px + 2 * py + pc
    if scatter:
        src, dst = src_refs[k].at[peer_idx], land_refs[k].at[r - 1]
    else:
        src, dst = src_refs[k], land_refs[k].at[peer_idx if receiving else me]
    return pltpu.make_async_remote_copy(
        src_ref=src, dst_ref=dst, send_sem=send_sems.at[k * N_PEERS + r - 1],
        recv_sem=recv_sems.at[k * N_PEERS + r - 1], device_id=(px, py, pc), device_id_type=MESH_ID)


def _split_start(name, srcs, lands, scatter):
    n = len(srcs)

    def body(*refs):
        src_refs, land_refs = refs[:n], refs[n:2 * n]
        send_sems, recv_sems = refs[2 * n], refs[2 * n + 1]
        token = refs[-1]
        pos = _mesh_pos()
        for k in range(n):
            for r in range(1, N_DEV):
                _split_copy(src_refs, land_refs, send_sems, recv_sems, k, r, pos, scatter, False).start()
        token[...] = jnp.zeros_like(token)

    thru = [pltpu.HBM(a.shape, a.dtype) for a in list(srcs) + list(lands)]
    outs = pl.pallas_call(
        body, name=name,
        in_specs=[HBM_SPEC] * (2 * n),
        out_specs=[SEM_SPEC, SEM_SPEC] + [HBM_SPEC] * (2 * n) + [pl.BlockSpec(memory_space=pltpu.VMEM)],
        out_shape=[pltpu.SemaphoreType.DMA((n * N_PEERS,)), pltpu.SemaphoreType.DMA((n * N_PEERS,))] + thru
        + [jax.ShapeDtypeStruct((8, LANES), F32)],
        input_output_aliases={i: 2 + i for i in range(2 * n)},
        compiler_params=pltpu.CompilerParams(has_side_effects=DATAFLOW_EFFECT),
    )(*[pltpu.with_memory_space_constraint(a, pltpu.HBM) for a in list(srcs) + list(lands)])
    return outs[0], outs[1], outs[2:2 + n], outs[2 + n:2 + 2 * n], outs[-1]


def _split_wait(name, send_sems, recv_sems, srcs, lands, scatter, after):
    n = len(srcs)

    def body(*refs):
        src_refs, land_refs = refs[:n], refs[n:2 * n]
        send, recv = refs[2 * n], refs[2 * n + 1]
        pos = _mesh_pos()
        for k in range(n):
            for r in range(1, N_DEV):
                cp = _split_copy(src_refs, land_refs, send, recv, k, r, pos, scatter, True)
                cp.wait_send()
                cp.wait_recv()

    thru = [pltpu.HBM(a.shape, a.dtype) for a in list(srcs) + list(lands)]
    outs = pl.pallas_call(
        body, name=name,
        in_specs=[HBM_SPEC] * (2 * n) + [SEM_SPEC, SEM_SPEC, ANY_SPEC],
        out_specs=[HBM_SPEC] * (2 * n), out_shape=thru,
        input_output_aliases={i: i for i in range(2 * n)},
        compiler_params=pltpu.CompilerParams(has_side_effects=DATAFLOW_EFFECT),
    )(*srcs, *lands, send_sems, recv_sems, after)
    return outs[:n], outs[n:]


def _late_copy(passing, src_refs, land_refs, send_sems, recv_sems, k, s, pos, receiving):
    x, y, c, me = pos
    chips = [(x, 1 - y), (1 - x, y), (1 - x, 1 - y)]
    sibling = (x, y, 1 - c)

    def idx(dev):
        return 4 * dev[0] + 2 * dev[1] + dev[2]

    if passing:
        to = sibling
        block = idx((*chips[s], 1 - c)) if receiving else idx((*chips[s], c))
        src = dst = land_refs[k].at[block]
        sem = k * 3 + s
    else:
        to = sibling if s == 0 else (*chips[s - 1], c)
        src, dst = src_refs[k], land_refs[k].at[idx(to) if receiving else me]
        sem = k * 4 + s
    return pltpu.make_async_remote_copy(src_ref=src, dst_ref=dst, send_sem=send_sems.at[sem],
                                        recv_sem=recv_sems.at[sem], device_id=to, device_id_type=MESH_ID)


def _late_gather_call(name, stage, srcs, lands, sems, after=None):
    n = len(srcs)
    n_sem_in = len(sems)
    has_after = after is not None

    def body(*refs):
        src_refs, land_refs = refs[:n], refs[n:2 * n]
        sem_in = refs[2 * n:2 * n + n_sem_in]
        outs = refs[2 * n + n_sem_in + (1 if has_after else 0):]
        pos = _mesh_pos()
        if stage == 0:
            own_send, own_recv = outs[0], outs[1]
            for s in (1, 2, 3, 0):
                for k in range(n):
                    _late_copy(False, src_refs, land_refs, own_send, own_recv, k, s, pos, False).start()
            outs[-1][...] = jnp.zeros_like(outs[-1])
        elif stage == 1:
            own_recv = sem_in[1]
            pass_send, pass_recv = outs[0], outs[1]
            for s in range(3):
                for k in range(n):
                    _late_copy(False, src_refs, land_refs, sem_in[0], own_recv, k, s + 1, pos, True).wait_recv()
                    _late_copy(True, src_refs, land_refs, pass_send, pass_recv, k, s, pos, False).start()
            outs[-1][...] = jnp.zeros_like(outs[-1])
        else:
            own_send, own_recv, pass_send, pass_recv = sem_in
            for k in range(n):
                _late_copy(False, src_refs, land_refs, own_send, own_recv, k, 0, pos, True).wait_recv()
                for s in range(4):
                    _late_copy(False, src_refs, land_refs, own_send, own_recv, k, s, pos, False).wait_send()
                for s in range(3):
                    cp = _late_copy(True, src_refs, land_refs, pass_send, pass_recv, k, s, pos, True)
                    cp.wait_recv()
                    cp.wait_send()

    thru = [pltpu.HBM(a.shape, a.dtype) for a in list(srcs) + list(lands)]
    new_sems = [[pltpu.SemaphoreType.DMA((n * 4,))] * 2, [pltpu.SemaphoreType.DMA((n * 3,))] * 2, []][stage]
    extra = [] if stage == 2 else [jax.ShapeDtypeStruct((8, LANES), F32)]
    outs = pl.pallas_call(
        body, name=name,
        in_specs=[HBM_SPEC] * (2 * n) + [SEM_SPEC] * n_sem_in + [ANY_SPEC] * has_after,
        out_specs=[SEM_SPEC] * len(new_sems) + [HBM_SPEC] * (2 * n) + [pl.BlockSpec(memory_space=pltpu.VMEM)] * len(extra),
        out_shape=new_sems + thru + extra,
        input_output_aliases={i: len(new_sems) + i for i in range(2 * n)},
        compiler_params=pltpu.CompilerParams(has_side_effects=DATAFLOW_EFFECT),
    )(*[pltpu.with_memory_space_constraint(a, pltpu.HBM) for a in list(srcs) + list(lands)], *sems,
      *([after] if has_after else []))
    ns = len(new_sems)
    return list(outs[:ns]), outs[ns:ns + n], outs[ns + n:ns + 2 * n], (outs[-1] if extra else None)


N_SEND_SLOTS = 3


def _exchange_last(grads, small_packed):
    ng = len(grads)
    ch = SMALL_ROWS // N_DEV
    max_rows = max(g.shape[1] for g in grads)
    cols = grads[0].shape[2]

    def body(*refs):
        g_in, s_in = refs[:ng], refs[ng]
        outs = refs[ng + 1:]
        own_out, land, stage = outs[:ng], outs[ng:2 * ng], outs[2 * ng:3 * ng]
        s_red, s_stage = outs[3 * ng], outs[3 * ng + 1]
        (va, vb, vo, vs, sm_in, sm_out, d2d_send, d2d_recv, ici_send, ici_recv, s1_send, s1_recv, s2_send, s2_recv,
         local_sems) = outs[3 * ng + 2:]
        x, y, c, me = _mesh_pos()
        sibling = (x, y, 1 - c)
        chips = [(x, y), (x, 1 - y), (1 - x, y), (1 - x, 1 - y)]

        def idx(chip, core):
            return 4 * chip[0] + 2 * chip[1] + core

        def d2d(k, j):
            return pltpu.make_async_remote_copy(
                src_ref=g_in[k].at[idx(chips[j], 1 - c)], dst_ref=stage[k].at[j], send_sem=d2d_send.at[k, j],
                recv_sem=d2d_recv.at[k, j], device_id=sibling, device_id_type=MESH_ID)

        def ici(k, j, slot):
            rows = g_in[k].shape[1]
            return pltpu.make_async_remote_copy(
                src_ref=vo.at[slot, pl.ds(0, rows)], dst_ref=land[k].at[j - 1], send_sem=ici_send.at[k, j - 1],
                recv_sem=ici_recv.at[k, j - 1], device_id=(*chips[j], c), device_id_type=MESH_ID)

        def small_scatter(r):
            px, py, pc = _peer(x, y, c, r)
            return pltpu.make_async_remote_copy(
                src_ref=s_in.at[pl.ds(pl.multiple_of((4 * px + 2 * py + pc) * ch, 8), ch)], dst_ref=s_stage.at[me],
                send_sem=s1_send.at[r - 1], recv_sem=s1_recv.at[r - 1], device_id=(px, py, pc), device_id_type=MESH_ID)

        def small_gather(r):
            return pltpu.make_async_remote_copy(
                src_ref=sm_out, dst_ref=s_red.at[me], send_sem=s2_send.at[r - 1], recv_sem=s2_recv.at[r - 1],
                device_id=_peer(x, y, c, r), device_id_type=MESH_ID)

        for r in range(1, N_DEV):
            small_scatter(r).start()
        mine = pltpu.make_async_copy(s_in.at[pl.ds(pl.multiple_of(me * ch, 8), ch)], s_stage.at[me], local_sems.at[0])
        mine.start()
        pairs = [(k, j) for k in range(ng) for j in (1, 2, 3)] + [(k, 0) for k in range(ng)]
        for k, j in pairs:
            d2d(k, j).start()

        def reduce_small():
            for r in range(1, N_DEV):
                small_scatter(r).wait_recv()
            mine.wait()
            load = pltpu.make_async_copy(s_stage, sm_in, local_sems.at[1])
            load.start()
            load.wait()
            total = sm_in[0]
            for i in range(1, N_DEV):
                total = total + sm_in[i]
            sm_out[...] = total
            for r in range(1, N_DEV):
                small_gather(r).start()
            keep = pltpu.make_async_copy(sm_out, s_red.at[me], local_sems.at[2])
            keep.start()
            return keep

        in_flight = {}
        for i, (k, j) in enumerate(pairs):
            if i == N_SEND_SLOTS:
                keep = reduce_small()
            slot = i % N_SEND_SLOTS
            rows = g_in[k].shape[1]
            if slot in in_flight:
                in_flight.pop(slot).wait_send()
            d2d(k, j).wait_recv()
            la = pltpu.make_async_copy(g_in[k].at[idx(chips[j], c)], va.at[pl.ds(0, rows)], local_sems.at[3])
            lb = pltpu.make_async_copy(stage[k].at[j], vb.at[pl.ds(0, rows)], local_sems.at[4])
            la.start()
            lb.start()
            la.wait()
            lb.wait()
            total = va[pl.ds(0, rows)].astype(F32) + vb[pl.ds(0, rows)].astype(F32)
            if j == 0:
                vs[pl.ds(0, rows)] = total
                st = pltpu.make_async_copy(vs.at[pl.ds(0, rows)], own_out[k], local_sems.at[5])
                st.start()
                st.wait()
            else:
                vo[slot, pl.ds(0, rows)] = total.astype(BF16)
                cp = ici(k, j, slot)
                cp.start()
                in_flight[slot] = cp
        for cp in in_flight.values():
            cp.wait_send()

        for j in (1, 2, 3, 0):
            for k in range(ng):
                d2d(k, j).wait_send()
        for j in (1, 2, 3):
            for k in range(ng):
                ici(k, j, 0).wait_recv()
        for r in range(1, N_DEV):
            small_scatter(r).wait_send()
            small_gather(r).wait_send()
            small_gather(r).wait_recv()
        keep.wait()

    out_shape = [jax.ShapeDtypeStruct(g.shape[1:], F32) for g in grads]
    out_shape += [jax.ShapeDtypeStruct((3,) + g.shape[1:], BF16) for g in grads]
    out_shape += [jax.ShapeDtypeStruct((4,) + g.shape[1:], BF16) for g in grads]
    out_shape += [jax.ShapeDtypeStruct((N_DEV, ch, LANES), F32), jax.ShapeDtypeStruct((N_DEV, ch, LANES), F32)]
    outs = pl.pallas_call(
        body, name="exchange_last",
        in_specs=[ANY_SPEC] * (ng + 1), out_specs=[ANY_SPEC] * len(out_shape), out_shape=out_shape,
        scratch_shapes=[pltpu.VMEM((max_rows, cols), BF16), pltpu.VMEM((max_rows, cols), BF16),
                        pltpu.VMEM((N_SEND_SLOTS, max_rows, cols), BF16), pltpu.VMEM((max_rows, cols), F32),
                        pltpu.VMEM((N_DEV, ch, LANES), F32), pltpu.VMEM((ch, LANES), F32),
                        pltpu.SemaphoreType.DMA((ng, 4)), pltpu.SemaphoreType.DMA((ng, 4)),
                        pltpu.SemaphoreType.DMA((ng, 3)), pltpu.SemaphoreType.DMA((ng, 3)),
                        pltpu.SemaphoreType.DMA((N_PEERS,)), pltpu.SemaphoreType.DMA((N_PEERS,)),
                        pltpu.SemaphoreType.DMA((N_PEERS,)), pltpu.SemaphoreType.DMA((N_PEERS,)),
                        pltpu.SemaphoreType.DMA((6,))],
        compiler_params=pltpu.CompilerParams(has_side_effects=True, vmem_limit_bytes=VMEM_LIMIT),
    )(*grads, small_packed)
    return outs[:ng], outs[ng:2 * ng], outs[3 * ng].reshape(SMALL_ROWS, LANES)


def _adamw_math(w, g, m, v):
    m2 = ADAM_B1 * m + (1.0 - ADAM_B1) * g
    v2 = ADAM_B2 * v + (1.0 - ADAM_B2) * (g * g)
    m_hat = m2 / (1.0 - ADAM_B1 ** ADAM_STEP)
    v_hat = v2 / (1.0 - ADAM_B2 ** ADAM_STEP)
    delta = -ADAM_LR * (m_hat / (jnp.sqrt(v_hat) + ADAM_EPS) + ADAM_WD * w)
    return delta, m2, v2


ADAM_ROW_TILES = 2


def _adamw_big(own, parts, w, m, v, name):
    shape = w.shape
    own_is_blocks = own.ndim == 3
    tr = shape[0] // ADAM_ROW_TILES
    n_parts = parts.shape[0]

    def body(own_ref, p_ref, w_ref, m_ref, v_ref, g_ref, d_ref, m2_ref, v2_ref, own_s, sem):
        if own_is_blocks:
            rows = pl.ds(pl.multiple_of(pl.program_id(0) * tr, 16), tr)
            cp = pltpu.make_async_copy(own_ref.at[_mesh_pos()[3], rows], own_s, sem)
            cp.start()
            cp.wait()
            g = own_s[...].astype(F32)
        else:
            g = own_ref[...]
        for i in range(n_parts):
            g = g + p_ref[i].astype(F32)
        delta, m2, v2 = _adamw_math(w_ref[...], g, m_ref[...], v_ref[...])
        g_ref[...] = g
        d_ref[...] = delta
        m2_ref[...] = m2
        v2_ref[...] = v2

    tile = pl.BlockSpec((tr, shape[1]), lambda i: (i, 0))
    return pl.pallas_call(
        body, name=name, grid=(ADAM_ROW_TILES,),
        in_specs=[ANY_SPEC if own_is_blocks else tile,
                  pl.BlockSpec((n_parts, tr, shape[1]), lambda i: (0, i, 0)), tile, tile, tile],
        out_specs=[tile] * 4, out_shape=[jax.ShapeDtypeStruct(shape, F32)] * 4,
        scratch_shapes=[pltpu.VMEM((tr, shape[1]), own.dtype), pltpu.SemaphoreType.DMA(())],
        compiler_params=_cparams(("arbitrary",)),
    )(own, parts, w, m, v)


def _pack_small(grads):
    names = list(SMALL)

    def body(*refs):
        ins, out = dict(zip(names, refs[:-1])), refs[-1]
        out[...] = jnp.zeros_like(out)
        for re, im in SMALL_PAIRS:
            off, rows = SMALL_OFFSET[re], SMALL[re][0]
            out[off:off + rows, :] = jnp.concatenate([ins[re][...], ins[im][...]], axis=1)
        for n in SMALL_VECS:
            off, vec = SMALL_OFFSET[n], ins[n][...]
            for i in range(SMALL[n][1] // LANES):
                out[off + i:off + i + 1, :] = vec[:, i * LANES:(i + 1) * LANES]
        for n in SMALL_TILES:
            off, (rows, cols) = SMALL_OFFSET[n], SMALL[n]
            out[off:off + rows, 0:cols] = ins[n][...]

    return pl.pallas_call(
        body, name="pack_small", out_shape=jax.ShapeDtypeStruct((SMALL_ROWS, LANES), F32),
        compiler_params=_cparams(),
    )(*[grads[n] for n in names])


def _unpack_small_ref(g_ref, n):
    off, (rows, cols) = SMALL_OFFSET[n], SMALL[n]
    for re, im in SMALL_PAIRS:
        if n == re:
            return g_ref[off:off + rows, 0:HALF_LANES]
        if n == im:
            return g_ref[off:off + rows, HALF_LANES:LANES]
    if n in SMALL_VECS:
        return jnp.concatenate([g_ref[off + i:off + i + 1, :] for i in range(cols // LANES)], axis=1)
    return g_ref[off:off + rows, 0:cols]


def _adamw_small(g_packed, w, m, v):
    names = list(SMALL_PARAMS)
    n = len(names)

    def body(g_ref, *refs):
        w_refs, m_refs, v_refs, outs = refs[:n], refs[n:2 * n], refs[2 * n:3 * n], refs[3 * n:]
        for idx, name in enumerate(names):
            g = _unpack_small_ref(g_ref, name)
            delta, m2, v2 = _adamw_math(w_refs[idx][...], g, m_refs[idx][...], v_refs[idx][...])
            outs[4 * idx][...] = g
            outs[4 * idx + 1][...] = delta
            outs[4 * idx + 2][...] = m2
            outs[4 * idx + 3][...] = v2
        outs[4 * n][...] = _unpack_small_ref(g_ref, "loss")

    outs = pl.pallas_call(
        body, name="adamw_small",
        out_shape=[jax.ShapeDtypeStruct(SMALL[name], F32) for name in names for _ in range(4)]
        + [jax.ShapeDtypeStruct(SMALL["loss"], F32)],
        compiler_params=_cparams(),
    )(g_packed, *[w[k] for k in names], *[m[k] for k in names], *[v[k] for k in names])
    return {name: outs[4 * idx:4 * idx + 4] for idx, name in enumerate(names)}, outs[4 * n]


WEIGHT_NAMES = ['norm_ffn1', 'ffn1_w_gate', 'ffn1_w_up', 'ffn1_w_down', 'norm_mix', 'w_in', 'attn_sinks',
                'ssm_lambda_re', 'ssm_lambda_im', 'ssm_log_dt', 'ssm_b_re', 'ssm_b_im', 'ssm_c_re', 'ssm_c_im',
                'ssm_d', 'ssm_glu_w', 'ssm_glu_b', 'attn_out_norm', 'ssm_out_norm', 'w_out', 'norm_ffn2',
                'ffn2_w_gate', 'ffn2_w_up', 'ffn2_w_down', 'final_norm']


def kernel(x, norm_ffn1, ffn1_w_gate, ffn1_w_up, ffn1_w_down, norm_mix, w_in, attn_sinks, ssm_lambda_re, ssm_lambda_im, ssm_log_dt, ssm_b_re, ssm_b_im, ssm_c_re, ssm_c_im, ssm_d, ssm_glu_w, ssm_glu_b, attn_out_norm, ssm_out_norm, w_out, norm_ffn2, ffn2_w_gate, ffn2_w_up, ffn2_w_down, final_norm, loss_target, m_norm_ffn1, m_ffn1_w_gate, m_ffn1_w_up, m_ffn1_w_down, m_norm_mix, m_w_in, m_attn_sinks, m_ssm_lambda_re, m_ssm_lambda_im, m_ssm_log_dt, m_ssm_b_re, m_ssm_b_im, m_ssm_c_re, m_ssm_c_im, m_ssm_d, m_ssm_glu_w, m_ssm_glu_b, m_attn_out_norm, m_ssm_out_norm, m_w_out, m_norm_ffn2, m_ffn2_w_gate, m_ffn2_w_up, m_ffn2_w_down, m_final_norm, v_norm_ffn1, v_ffn1_w_gate, v_ffn1_w_up, v_ffn1_w_down, v_norm_mix, v_w_in, v_attn_sinks, v_ssm_lambda_re, v_ssm_lambda_im, v_ssm_log_dt, v_ssm_b_re, v_ssm_b_im, v_ssm_c_re, v_ssm_c_im, v_ssm_d, v_ssm_glu_w, v_ssm_glu_b, v_attn_out_norm, v_ssm_out_norm, v_w_out, v_norm_ffn2, v_ffn2_w_gate, v_ffn2_w_up, v_ffn2_w_down, v_final_norm):
    args = dict(locals())
    weights = {n: args[n] for n in WEIGHT_NAMES}
    moms = {n: args["m_" + n] for n in WEIGHT_NAMES}
    vars_ = {n: args["v_" + n] for n in WEIGHT_NAMES}

    def shard2d(a, k):
        a = a.reshape(a.shape[-2], a.shape[-1])
        return a.T if BIG[k][3] else a

    def shard_master(a, k):
        return (a.T if BIG[k][3] else a).reshape(weights[BIG[k][0]].shape)

    def blocks(g, k):
        return g.reshape(N_DEV, BIG[k][1], BIG[k][2])

    def full(g, k):
        return g.reshape(N_DEV * BIG[k][1], BIG[k][2])

    shards = dict(zip(BIG, _cast_shards({k: shard2d(weights[BIG[k][0]], k) for k in BIG})))
    nf = len(FIRST_GROUP)
    got = _gather_first([shards[k] for k in FIRST_GROUP], [shards[k] for k in LATE_GROUP])
    w_first = {k: full(g, k) for k, g in zip(FIRST_GROUP, got[:nf])}
    late = {}
    late["own_sems"], late["srcs"], late["lands"], w_token = _late_gather_call(
        "gather_late_start", 0, [shards[k] for k in LATE_GROUP], got[nf:], [])

    def late_pass(dep):
        late["pass_sems"], late["srcs"], late["lands"], token = _late_gather_call(
            "gather_late_pass", 1, late["srcs"], late["lands"], late["own_sems"], after=dep)
        return token

    def late_weights(dep):
        _, _, lands, _ = _late_gather_call("gather_late_wait", 2, late["srcs"], late["lands"],
                                           late["own_sems"] + late["pass_sems"], after=dep)
        return {k: full(g, k) for k, g in zip(LATE_GROUP, lands)}

    early = {}

    def early_grads(g):
        srcs = [blocks(g[k], k) for k in LATE_GROUP]
        lands = [lax.empty((N_PEERS, BIG[k][1], BIG[k][2]), BF16) for k in LATE_GROUP]
        early["send"], early["recv"], early["srcs"], early["lands"], token = _split_start(
            "grads_late_start", srcs, lands, scatter=True)
        return token

    def small2d(a, n):
        if n in SMALL_TRANSPOSED:
            a = jnp.swapaxes(a, -1, -2)
        return a.reshape(SMALL[n])

    def small_master(a, n):
        if n in SMALL_TRANSPOSED:
            shape = weights[n].shape
            return jnp.swapaxes(a.reshape(shape[:-2] + (shape[-1], shape[-2])), -1, -2)
        return a.reshape(weights[n].shape)

    small_p = {n: small2d(weights[n], n) for n in SMALL_PARAMS}
    _, grad_x, g_first, g_small = _local_step(
        x.reshape(SEQ, D_MODEL), loss_target.reshape(SEQ, D_MODEL), w_first, small_p, late_weights, early_grads,
        after=w_token, midway=late_pass)

    own_sums, first_parts, small_grad = _exchange_last([blocks(g_first[k], k) for k in FIRST_GROUP],
                                                       _pack_small(g_small))
    own_late, late_parts = _split_wait("grads_late_wait", early["send"], early["recv"], early["srcs"],
                                       early["lands"], True, small_grad)
    own = dict(zip(FIRST_GROUP + LATE_GROUP, list(own_sums) + list(own_late)))
    parts = dict(zip(FIRST_GROUP + LATE_GROUP, list(first_parts) + list(late_parts)))
    outs = {}
    for k in BIG:
        n = BIG[k][0]
        outs[n] = [shard_master(o, k) for o in
                   _adamw_big(own[k], parts[k], shard2d(weights[n], k), shard2d(moms[n], k), shard2d(vars_[n], k),
                              "adamw_" + n)]
    small_out, loss_row = _adamw_small(small_grad, small_p, {n: small2d(moms[n], n) for n in SMALL_PARAMS},
                                       {n: small2d(vars_[n], n) for n in SMALL_PARAMS})
    for n in SMALL_PARAMS:
        outs[n] = [small_master(o, n) for o in small_out[n]]

    result = [loss_row[0, 0], grad_x.reshape(x.shape)]
    for i in range(4):
        result += [outs[n][i] for n in WEIGHT_NAMES]
    return tuple(result)
```

```python
import functools

import jax
import jax.numpy as jnp
from jax import lax
from jax.experimental import pallas as pl
from jax.experimental.pallas import tpu as pltpu

F32 = jnp.float32
BF16 = jnp.bfloat16

N_DEV = 8
SEQ = 2048
D_MODEL = 1024
D_FF = 2816
ATTN_HEADS = 8
KV_HEADS = 2
HEAD_DIM = 64
ATTN_WIDTH = 512
KV_WIDTH = 128
WINDOW = 128
SSM_WIDTH = 512
IN_WIDTH = 1280
EPS = 1e-6
MASKED_DISTANCE = 1e33
LAMBDA_RE_MAX = -1e-4
LANES = 128
N_LANE_BLOCKS = 16
SCAN_CHUNK = SEQ // 8

ADAM_LR = 0.001
ADAM_B1 = 0.9
ADAM_B2 = 0.999
ADAM_EPS = 1e-08
ADAM_WD = 0.01
ADAM_STEP = 10

VMEM_LIMIT = 60 * 1024 * 1024
MESH_ID = pl.DeviceIdType.MESH


def _cparams(sem=None):
    return pltpu.CompilerParams(dimension_semantics=sem, vmem_limit_bytes=VMEM_LIMIT)


def _dot(a, b):
    return jnp.dot(a, b, preferred_element_type=F32)


def _dot_nt(a, b):
    return lax.dot_general(a, b, (((1,), (1,)), ((), ())), preferred_element_type=F32)


def _dot_tn(a, b):
    return lax.dot_general(a, b, (((0,), (0,)), ((), ())), preferred_element_type=F32)


def _rms_fwd(x, g):
    r = lax.rsqrt(jnp.mean(x * x, axis=-1, keepdims=True) + EPS)
    return x * r * g


def _rms_bwd(dh, x, g):
    r = lax.rsqrt(jnp.mean(x * x, axis=-1, keepdims=True) + EPS)
    xh = x * r
    dg = jnp.sum(dh * xh, axis=0, keepdims=True)
    dxh = dh * g
    dx = r * (dxh - xh * jnp.mean(dxh * xh, axis=-1, keepdims=True))
    return dx, dg


def _sigmoid(x):
    return 1.0 / (1.0 + jnp.exp(-x))


FFN_TM = 512
FFN_TF = 1408


def _ffn_fwd(x, g, wgt, wut, wd, name, after=None, head=None):
    tm, tf = FFN_TM // 2, D_FF
    nj = D_FF // tf
    deps = [] if after is None else [after]
    n_in = len(deps) + (2 if head else 0)

    def body(x_ref, g_ref, wg_ref, wu_ref, wd_ref, *rest):
        i = pl.program_id(0)
        j = pl.program_id(1)
        if head:
            gf_ref, t_ref = rest[len(deps):n_in]
            xo_ref, h_ref, a_ref, b_ref, loss_ref, dgf_ref, h_s, acc = rest[n_in:]
        else:
            xo_ref, h_ref, a_ref, b_ref, h_s, acc = rest[n_in:]

        @pl.when(j == 0)
        def _():
            h = _rms_fwd(x_ref[...], g_ref[...]).astype(BF16)
            h_s[...] = h
            h_ref[...] = h
            acc[...] = jnp.zeros_like(acc)

        h = h_s[...]
        a = _dot_nt(h, wg_ref[...])
        b = _dot_nt(h, wu_ref[...])
        a_ref[...] = a.astype(BF16)
        b_ref[...] = b.astype(BF16)
        s = (a * _sigmoid(a) * b).astype(BF16)
        acc[...] += _dot(s, wd_ref[...])

        @pl.when(j == nj - 1)
        def _():
            xo = x_ref[...] + 0.5 * acc[...]
            if not head:
                xo_ref[...] = xo
                return
            gf = gf_ref[...]
            err = _rms_fwd(xo, gf) - t_ref[...]
            part = jnp.broadcast_to(0.5 * jnp.sum(err * err) / D_MODEL, (1, LANES))
            dx, dgf = _rms_bwd(err * (1.0 / D_MODEL), xo, gf)
            xo_ref[...] = dx

            @pl.when(i == 0)
            def _():
                loss_ref[...] = part
                dgf_ref[...] = dgf

            @pl.when(i != 0)
            def _():
                loss_ref[...] += part
                dgf_ref[...] += dgf

    row = lambda i, j: (i, 0)
    const = lambda i, j: (0, 0)
    head_in = [pl.BlockSpec((1, D_MODEL), const), pl.BlockSpec((tm, D_MODEL), row)] if head else []
    head_out = [pl.BlockSpec((1, LANES), const), pl.BlockSpec((1, D_MODEL), const)] if head else []
    head_shape = [jax.ShapeDtypeStruct((1, LANES), F32), jax.ShapeDtypeStruct((1, D_MODEL), F32)] if head else []
    return pl.pallas_call(
        body, name=name, grid=(SEQ // tm, nj),
        in_specs=[pl.BlockSpec((tm, D_MODEL), row), pl.BlockSpec((1, D_MODEL), const),
                  pl.BlockSpec((tf, D_MODEL), lambda i, j: (j, 0)),
                  pl.BlockSpec((tf, D_MODEL), lambda i, j: (j, 0)),
                  pl.BlockSpec((tf, D_MODEL), lambda i, j: (j, 0))] + [pl.BlockSpec(memory_space=pl.ANY)] * len(deps)
        + head_in,
        out_specs=[pl.BlockSpec((tm, D_MODEL), row), pl.BlockSpec((tm, D_MODEL), row),
                   pl.BlockSpec((tm, tf), lambda i, j: (i, j)),
                   pl.BlockSpec((tm, tf), lambda i, j: (i, j))] + head_out,
        out_shape=[jax.ShapeDtypeStruct((SEQ, D_MODEL), F32), jax.ShapeDtypeStruct((SEQ, D_MODEL), BF16),
                   jax.ShapeDtypeStruct((SEQ, D_FF), BF16), jax.ShapeDtypeStruct((SEQ, D_FF), BF16)] + head_shape,
        scratch_shapes=[pltpu.VMEM((tm, D_MODEL), BF16), pltpu.VMEM((tm, D_MODEL), F32)],
        compiler_params=_cparams(("arbitrary" if head else "parallel", "arbitrary")),
    )(x, g, wgt, wut, wd, *deps, *(head or ()))


def _ffn_bwd_act(dxo, x, g, a, b, wgt, wut, wd, name):
    tm, tf = FFN_TM // 2, D_FF
    nj = D_FF // tf
    resident = pl.Buffered(1)

    def body(dxo_ref, x_ref, g_ref, a_ref, b_ref, wg_ref, wu_ref, wd_ref,
             dx_ref, da_ref, db_ref, s_ref, df_ref, dg_ref, df_s, acc):
        i = pl.program_id(0)
        j = pl.program_id(1)

        @pl.when(j == 0)
        def _():
            df = (0.5 * dxo_ref[...]).astype(BF16)
            df_s[...] = df
            df_ref[...] = df
            acc[...] = jnp.zeros_like(acc)

        ds = _dot_nt(df_s[...], wd_ref[...])
        av = a_ref[...].astype(F32)
        bv = b_ref[...].astype(F32)
        sig = _sigmoid(av)
        sl = av * sig
        s_ref[...] = (sl * bv).astype(BF16)
        db = (ds * sl).astype(BF16)
        da = (ds * bv * (sig * (1.0 + av * (1.0 - sig)))).astype(BF16)
        da_ref[...] = da
        db_ref[...] = db
        acc[...] += _dot(da, wg_ref[...]) + _dot(db, wu_ref[...])

        @pl.when(j == nj - 1)
        def _():
            dx, dg = _rms_bwd(acc[...], x_ref[...], g_ref[...])
            dx_ref[...] = dxo_ref[...] + dx

            @pl.when(i == 0)
            def _():
                dg_ref[...] = dg

            @pl.when(i != 0)
            def _():
                dg_ref[...] += dg

    row = lambda i, j: (i, 0)
    col = lambda i, j: (j, 0)
    tile = lambda i, j: (i, j)
    return pl.pallas_call(
        body, name=name, grid=(SEQ // tm, nj),
        in_specs=[pl.BlockSpec((tm, D_MODEL), row), pl.BlockSpec((tm, D_MODEL), row),
                  pl.BlockSpec((1, D_MODEL), lambda i, j: (0, 0)),
                  pl.BlockSpec((tm, tf), tile), pl.BlockSpec((tm, tf), tile),
                  pl.BlockSpec((tf, D_MODEL), col, pipeline_mode=resident),
                  pl.BlockSpec((tf, D_MODEL), col, pipeline_mode=resident),
                  pl.BlockSpec((tf, D_MODEL), col, pipeline_mode=resident)],
        out_specs=[pl.BlockSpec((tm, D_MODEL), row),
                   pl.BlockSpec((tm, tf), tile), pl.BlockSpec((tm, tf), tile), pl.BlockSpec((tm, tf), tile),
                   pl.BlockSpec((tm, D_MODEL), row),
                   pl.BlockSpec((1, D_MODEL), lambda i, j: (0, 0))],
        out_shape=[jax.ShapeDtypeStruct((SEQ, D_MODEL), F32),
                   jax.ShapeDtypeStruct((SEQ, D_FF), BF16), jax.ShapeDtypeStruct((SEQ, D_FF), BF16),
                   jax.ShapeDtypeStruct((SEQ, D_FF), BF16),
                   jax.ShapeDtypeStruct((SEQ, D_MODEL), BF16),
                   jax.ShapeDtypeStruct((1, D_MODEL), F32)],
        scratch_shapes=[pltpu.VMEM((tm, D_MODEL), BF16), pltpu.VMEM((tm, D_MODEL), F32)],
        compiler_params=_cparams(("arbitrary", "arbitrary")),
    )(dxo, x, g, a, b, wgt, wut, wd)


def _mm_tn(pairs, name, tmm=256):
    m = pairs[0][0].shape[1]
    n_pairs = len(pairs)

    def body(*refs):
        ins, outs = refs[:2 * n_pairs], refs[2 * n_pairs:]
        for p in range(n_pairs):
            outs[p][...] = _dot_tn(ins[2 * p][...], ins[2 * p + 1][...]).astype(BF16)

    in_specs, out_specs, out_shape, args = [], [], [], []
    for a, b in pairs:
        n = b.shape[1]
        in_specs += [pl.BlockSpec((SEQ, tmm), lambda i: (0, i)), pl.BlockSpec((SEQ, n), lambda i: (0, 0))]
        out_specs.append(pl.BlockSpec((tmm, n), lambda i: (i, 0)))
        out_shape.append(jax.ShapeDtypeStruct((m, n), BF16))
        args += [a, b]
    return pl.pallas_call(body, name=name, grid=(m // tmm,), in_specs=in_specs, out_specs=out_specs,
                          out_shape=out_shape, compiler_params=_cparams(("parallel",)))(*args)


MIX_TM = 512


def _mixin_fwd(x, g, wint):
    tm = MIX_TM

    def body(x_ref, g_ref, w_ref, h_ref, q_ref, k_ref, v_ref, u_ref):
        h = _rms_fwd(x_ref[...], g_ref[...]).astype(BF16)
        h_ref[...] = h
        proj = _dot_nt(h, w_ref[...])
        q_ref[...] = proj[:, :ATTN_WIDTH].T
        k_ref[...] = proj[:, ATTN_WIDTH:ATTN_WIDTH + KV_WIDTH]
        v_ref[...] = proj[:, ATTN_WIDTH + KV_WIDTH:ATTN_WIDTH + 2 * KV_WIDTH]
        u_ref[...] = proj[:, ATTN_WIDTH + 2 * KV_WIDTH:]

    row = lambda i: (i, 0)
    return pl.pallas_call(
        body, name="mixin_fwd", grid=(SEQ // tm,),
        in_specs=[pl.BlockSpec((tm, D_MODEL), row), pl.BlockSpec((1, D_MODEL), lambda i: (0, 0)),
                  pl.BlockSpec((IN_WIDTH, D_MODEL), lambda i: (0, 0))],
        out_specs=[pl.BlockSpec((tm, D_MODEL), row), pl.BlockSpec((ATTN_WIDTH, tm), lambda i: (0, i)),
                   pl.BlockSpec((tm, KV_WIDTH), row), pl.BlockSpec((tm, KV_WIDTH), row),
                   pl.BlockSpec((tm, SSM_WIDTH), row)],
        out_shape=[jax.ShapeDtypeStruct((SEQ, D_MODEL), BF16), jax.ShapeDtypeStruct((ATTN_WIDTH, SEQ), F32),
                   jax.ShapeDtypeStruct((SEQ, KV_WIDTH), F32), jax.ShapeDtypeStruct((SEQ, KV_WIDTH), F32),
                   jax.ShapeDtypeStruct((SEQ, SSM_WIDTH), F32)],
        compiler_params=_cparams(("parallel",)),
    )(x, g, wint)


def _mixin_bwd(dqt, dk, dv, du, wint, x, g, dres):
    tm = MIX_TM

    def body(dq_ref, dk_ref, dv_ref, du_ref, w_ref, x_ref, g_ref, dres_ref, dx_ref, dp_ref, dg_ref):
        i = pl.program_id(0)
        dp = jnp.concatenate([dq_ref[...].T, dk_ref[...], dv_ref[...], du_ref[...]], axis=-1).astype(BF16)
        dp_ref[...] = dp
        dh = _dot(dp, w_ref[...])
        dx, dg = _rms_bwd(dh, x_ref[...], g_ref[...])
        dx_ref[...] = dres_ref[...] + dx

        @pl.when(i == 0)
        def _():
            dg_ref[...] = dg

        @pl.when(i != 0)
        def _():
            dg_ref[...] += dg

    row = lambda i: (i, 0)
    const = lambda i: (0, 0)
    return pl.pallas_call(
        body, name="mixin_bwd", grid=(SEQ // tm,),
        in_specs=[pl.BlockSpec((ATTN_WIDTH, tm), lambda i: (0, i)), pl.BlockSpec((tm, KV_WIDTH), row),
                  pl.BlockSpec((tm, KV_WIDTH), row), pl.BlockSpec((tm, SSM_WIDTH), row),
                  pl.BlockSpec((IN_WIDTH, D_MODEL), const), pl.BlockSpec((tm, D_MODEL), row),
                  pl.BlockSpec((1, D_MODEL), const), pl.BlockSpec((tm, D_MODEL), row)],
        out_specs=[pl.BlockSpec((tm, D_MODEL), row), pl.BlockSpec((tm, IN_WIDTH), row),
                   pl.BlockSpec((1, D_MODEL), const)],
        out_shape=[jax.ShapeDtypeStruct((SEQ, D_MODEL), F32), jax.ShapeDtypeStruct((SEQ, IN_WIDTH), BF16),
                   jax.ShapeDtypeStruct((1, D_MODEL), F32)],
        compiler_params=_cparams(("arbitrary",)),
    )(dqt, dk, dv, du, wint, x, g, dres)


N_QBLOCKS = SEQ // WINDOW
GROUP = ATTN_HEADS // KV_HEADS
SCALE = HEAD_DIM ** -0.5


def _alibi_slope(h):
    return 2.0 ** (-8.0 * (h + 1) / ATTN_HEADS)


def _window_masks(n):
    s_idx = lax.broadcasted_iota(jnp.int32, (3 * WINDOW, WINDOW), 0)
    t_idx = lax.broadcasted_iota(jnp.int32, (3 * WINDOW, WINDOW), 1)
    absrel = jnp.abs(s_idx - WINDOW - t_idx)
    key_pos = n * WINDOW - WINDOW + s_idx
    valid = (absrel <= WINDOW) & (key_pos >= 0) & (key_pos < SEQ)
    return jnp.where(valid, absrel.astype(F32), MASKED_DISTANCE)


def _group_cols(ref, r0, gi):
    return jnp.concatenate(
        [ref[(gi * GROUP + hh) * HEAD_DIM:(gi * GROUP + hh + 1) * HEAD_DIM, pl.ds(r0, WINDOW)].astype(BF16)
         for hh in range(GROUP)], axis=1)


def _group_probs(qgt, kw, dist, gi, sk_ref):
    bias = jnp.concatenate([-_alibi_slope(gi * GROUP + hh) * dist for hh in range(GROUP)], axis=1)
    sink = jnp.concatenate([jnp.full((1, WINDOW), sk_ref[0, gi * GROUP + hh], F32) for hh in range(GROUP)], axis=1)
    s = _dot(kw, qgt) * SCALE + bias
    m = jnp.maximum(jnp.max(s, axis=0, keepdims=True), sink)
    p = jnp.exp(s - m)
    ps = jnp.exp(sink - m)
    inv = 1.0 / (jnp.sum(p, axis=0, keepdims=True) + ps)
    return p * inv, ps * inv


def _pad_window(src_ref, dst_ref):
    zeros = jnp.zeros((WINDOW, KV_WIDTH), BF16)
    dst_ref[0:WINDOW, :] = zeros
    dst_ref[WINDOW + SEQ:, :] = zeros
    dst_ref[WINDOW:WINDOW + SEQ, :] = src_ref[...].astype(BF16)


def _attn_fwd(qt, k, v, sinks, after=None):
    deps = [] if after is None else [after]

    def body(sk_ref, qt_ref, k_ref, v_ref, *rest):
        o_ref, kp_ref, vp_ref = rest[len(deps):]
        _pad_window(k_ref, kp_ref)
        _pad_window(v_ref, vp_ref)

        def blk(n, carry):
            r0 = pl.multiple_of(n * WINDOW, WINDOW)
            dist = _window_masks(n)
            for gi in range(KV_HEADS):
                kw = kp_ref[pl.ds(r0, 3 * WINDOW), gi * HEAD_DIM:(gi + 1) * HEAD_DIM]
                vw = vp_ref[pl.ds(r0, 3 * WINDOW), gi * HEAD_DIM:(gi + 1) * HEAD_DIM]
                pr, _ = _group_probs(_group_cols(qt_ref, r0, gi), kw, dist, gi, sk_ref)
                og = _dot_tn(pr.astype(BF16), vw)
                for hh in range(GROUP):
                    h = gi * GROUP + hh
                    o_ref[pl.ds(r0, WINDOW), h * HEAD_DIM:(h + 1) * HEAD_DIM] = og[hh * WINDOW:(hh + 1) * WINDOW]
            return carry

        lax.fori_loop(0, N_QBLOCKS, blk, 0)

    vmem = pl.BlockSpec(memory_space=pltpu.VMEM)
    return pl.pallas_call(
        body, name="attn_fwd",
        in_specs=[pl.BlockSpec(memory_space=pltpu.SMEM), vmem, vmem, vmem]
        + [pl.BlockSpec(memory_space=pl.ANY)] * len(deps), out_specs=vmem,
        out_shape=jax.ShapeDtypeStruct((SEQ, ATTN_WIDTH), F32),
        scratch_shapes=[pltpu.VMEM((SEQ + 2 * WINDOW, KV_WIDTH), BF16)] * 2,
        compiler_params=_cparams(),
    )(sinks, qt, k, v, *deps)


def _attn_bwd(qt, k, v, sinks, dot_):
    def body(sk_ref, qt_ref, k_ref, v_ref, dot_ref, dqt_ref, dk_ref, dv_ref, dsk_ref,
             dsk_acc, kp_ref, vp_ref, dkp_ref, dvp_ref):
        _pad_window(k_ref, kp_ref)
        _pad_window(v_ref, vp_ref)
        dkp_ref[...] = jnp.zeros_like(dkp_ref)
        dvp_ref[...] = jnp.zeros_like(dvp_ref)
        dsk_acc[...] = jnp.zeros_like(dsk_acc)

        def blk(n, carry):
            r0 = pl.multiple_of(n * WINDOW, WINDOW)
            dist = _window_masks(n)
            for gi in range(KV_HEADS):
                gcols = slice(gi * HEAD_DIM, (gi + 1) * HEAD_DIM)
                kw = kp_ref[pl.ds(r0, 3 * WINDOW), gcols]
                vw = vp_ref[pl.ds(r0, 3 * WINDOW), gcols]
                qgt = _group_cols(qt_ref, r0, gi)
                dogt = _group_cols(dot_ref, r0, gi)
                pr, psink = _group_probs(qgt, kw, dist, gi, sk_ref)
                dp = _dot(vw, dogt)
                delta = jnp.sum(pr * dp, axis=0, keepdims=True)
                ds = (pr * (dp - delta)).astype(BF16)
                dsk_acc[gi:gi + 1, :] += -(psink * delta)
                dqgt = _dot_tn(kw, ds) * SCALE
                for hh in range(GROUP):
                    h = gi * GROUP + hh
                    dqt_ref[h * HEAD_DIM:(h + 1) * HEAD_DIM, pl.ds(r0, WINDOW)] = dqgt[:, hh * WINDOW:(hh + 1) * WINDOW]
                dkp_ref[pl.ds(r0, 3 * WINDOW), gcols] += _dot_nt(ds, qgt) * SCALE
                dvp_ref[pl.ds(r0, 3 * WINDOW), gcols] += _dot_nt(pr.astype(BF16), dogt)
            return carry

        lax.fori_loop(0, N_QBLOCKS, blk, 0)
        for h in range(ATTN_HEADS):
            gi, hh = divmod(h, GROUP)
            dsk_ref[:, h:h + 1] = jnp.sum(dsk_acc[gi:gi + 1, hh * WINDOW:(hh + 1) * WINDOW], axis=1, keepdims=True)
        dk_ref[...] = dkp_ref[WINDOW:WINDOW + SEQ, :]
        dv_ref[...] = dvp_ref[WINDOW:WINDOW + SEQ, :]

    vmem = pl.BlockSpec(memory_space=pltpu.VMEM)
    padded = (SEQ + 2 * WINDOW, KV_WIDTH)
    return pl.pallas_call(
        body, name="attn_bwd",
        in_specs=[pl.BlockSpec(memory_space=pltpu.SMEM), vmem, vmem, vmem, vmem],
        out_specs=[vmem, vmem, vmem, vmem],
        out_shape=[jax.ShapeDtypeStruct((ATTN_WIDTH, SEQ), F32),
                   jax.ShapeDtypeStruct((SEQ, KV_WIDTH), F32), jax.ShapeDtypeStruct((SEQ, KV_WIDTH), F32),
                   jax.ShapeDtypeStruct((1, ATTN_HEADS), F32)],
        scratch_shapes=[pltpu.VMEM((KV_HEADS, GROUP * WINDOW), F32), pltpu.VMEM(padded, BF16),
                        pltpu.VMEM(padded, BF16), pltpu.VMEM(padded, F32), pltpu.VMEM(padded, F32)],
        compiler_params=_cparams(),
    )(sinks, qt, k, v, dot_)


HALF_LANES = LANES // 2
BLOCK_ROWS = 32


def _embed_block(bt, q):
    z = jnp.zeros((16, HALF_LANES), bt.dtype)
    blk = jnp.concatenate([jnp.concatenate([bt[:16], z], axis=1), jnp.concatenate([z, bt[16:]], axis=1)], axis=0)
    parts = [jnp.zeros((BLOCK_ROWS * q, LANES), bt.dtype)] if q else []
    parts.append(blk)
    if q < 3:
        parts.append(jnp.zeros((BLOCK_ROWS * (3 - q), LANES), bt.dtype))
    return jnp.concatenate(parts, axis=0)


def _extract_block(m, q):
    blk = m[BLOCK_ROWS * q:BLOCK_ROWS * (q + 1)]
    return jnp.concatenate([blk[:16, :HALF_LANES], blk[16:, HALF_LANES:]], axis=0)


def _ssm_prep(lam_re, lam_im, log_dt, bt_re, bt_im, c_re, c_im):
    nb = 2 * N_LANE_BLOCKS

    def body(lr_ref, li_ref, ldt_ref, btr_ref, bti_ref, ctr_ref, cti_ref, ar_ref, ai_ref, bb_ref, cc_ref):
        lr = jnp.minimum(lr_ref[...], LAMBDA_RE_MAX)
        li = li_ref[...]
        dt = jnp.exp(ldt_ref[...])
        mag = jnp.exp(lr * dt)
        ar = mag * jnp.cos(li * dt)
        ai = mag * jnp.sin(li * dt)
        den = lr * lr + li * li
        cr = ((ar - 1.0) * lr + ai * li) / den
        ci = (ai * lr - (ar - 1.0) * li) / den
        ar_ref[...] = ar
        ai_ref[...] = ai
        for i in range(nb):
            q = i % 4
            rows = slice(BLOCK_ROWS * i, BLOCK_ROWS * (i + 1))
            br = _embed_block(btr_ref[rows, :], q)
            bi = _embed_block(bti_ref[rows, :], q)
            cri, cii = cr[i:i + 1, :], ci[i:i + 1, :]
            bb_ref[i] = jnp.concatenate([cri * br - cii * bi, cri * bi + cii * br], axis=1).astype(BF16)
            cc_ref[i] = jnp.concatenate([_embed_block(ctr_ref[rows, :], q).T,
                                         -_embed_block(cti_ref[rows, :], q).T], axis=0).astype(BF16)

    return pl.pallas_call(
        body, name="ssm_prep",
        out_shape=[jax.ShapeDtypeStruct((nb, LANES), F32), jax.ShapeDtypeStruct((nb, LANES), F32),
                   jax.ShapeDtypeStruct((nb, LANES, 2 * LANES), BF16),
                   jax.ShapeDtypeStruct((nb, 2 * LANES, LANES), BF16)],
        compiler_params=_cparams(),
    )(lam_re, lam_im, log_dt, bt_re, bt_im, c_re, c_im)


def _ssm_prep_bwd(lam_re, lam_im, log_dt, bt_re, bt_im, dar, dai, dbb, dcc):
    nb = 2 * N_LANE_BLOCKS

    def body(lr_ref, li_ref, ldt_ref, btr_ref, bti_ref, dar_ref, dai_ref, dbb_ref, dcc_ref,
             glr_ref, gli_ref, gdt_ref, gbr_ref, gbi_ref, gcre_ref, gcim_ref, gcr_s, gci_s):
        lam = lr_ref[...]
        lr = jnp.minimum(lam, LAMBDA_RE_MAX)
        li = li_ref[...]
        dt = jnp.exp(ldt_ref[...])
        mag = jnp.exp(lr * dt)
        cs = jnp.cos(li * dt)
        sn = jnp.sin(li * dt)
        ar = mag * cs
        ai = mag * sn
        den = lr * lr + li * li
        nr = (ar - 1.0) * lr + ai * li
        ni = ai * lr - (ar - 1.0) * li
        cr = nr / den
        ci = ni / den
        for i in range(nb):
            q = i % 4
            rows = slice(BLOCK_ROWS * i, BLOCK_ROWS * (i + 1))
            br = _embed_block(btr_ref[rows, :], q)
            bi = _embed_block(bti_ref[rows, :], q)
            gbbr = dbb_ref[i, :, :LANES]
            gbbi = dbb_ref[i, :, LANES:]
            cri, cii = cr[i:i + 1, :], ci[i:i + 1, :]
            gcr_s[i:i + 1, :] = jnp.sum(gbbr * br + gbbi * bi, axis=0, keepdims=True)
            gci_s[i:i + 1, :] = jnp.sum(gbbi * br - gbbr * bi, axis=0, keepdims=True)
            gbr_ref[rows, :] = _extract_block(cri * gbbr + cii * gbbi, q)
            gbi_ref[rows, :] = _extract_block(cri * gbbi - cii * gbbr, q)
            gcre_ref[rows, :] = _extract_block(dcc_ref[i, :LANES, :].T, q)
            gcim_ref[rows, :] = -_extract_block(dcc_ref[i, LANES:, :].T, q)
        g_cr = gcr_s[...]
        g_ci = gci_s[...]
        g_nr = g_cr / den
        g_ni = g_ci / den
        g_den = -(g_cr * nr + g_ci * ni) / (den * den)
        g_ar = dar_ref[...] + g_nr * lr - g_ni * li
        g_ai = dai_ref[...] + g_nr * li + g_ni * lr
        g_lr = g_nr * (ar - 1.0) + g_ni * ai + g_den * 2.0 * lr
        g_li = g_nr * ai - g_ni * (ar - 1.0) + g_den * 2.0 * li
        g_mag = g_ar * cs + g_ai * sn
        g_th = (g_ai * cs - g_ar * sn) * mag
        g_lr = g_lr + g_mag * mag * dt
        g_li = g_li + g_th * dt
        g_dt = g_mag * mag * lr + g_th * li
        glr_ref[...] = jnp.where(lam < LAMBDA_RE_MAX, g_lr, 0.0)
        gli_ref[...] = g_li
        gl = g_dt * dt
        half = LANES // 2
        gdt_ref[:, 0:1] = jnp.sum(gl[:, :half], axis=1, keepdims=True)
        gdt_ref[:, 1:2] = jnp.sum(gl[:, half:], axis=1, keepdims=True)

    rows_shape = jax.ShapeDtypeStruct((nb * BLOCK_ROWS, HALF_LANES), F32)
    return pl.pallas_call(
        body, name="ssm_prep_bwd",
        out_shape=[jax.ShapeDtypeStruct((nb, LANES), F32), jax.ShapeDtypeStruct((nb, LANES), F32),
                   jax.ShapeDtypeStruct((nb, 2), F32), rows_shape, rows_shape, rows_shape, rows_shape],
        scratch_shapes=[pltpu.VMEM((nb, LANES), F32), pltpu.VMEM((nb, LANES), F32)],
        compiler_params=_cparams(),
    )(lam_re, lam_im, log_dt, bt_re, bt_im, dar, dai, dbb, dcc)


def _cmul(ar, ai, br, bi):
    return ar * br - ai * bi, ar * bi + ai * br


def _interleave_rows(src_ref, dst_ref):
    def step(j, carry):
        dst_ref[pl.ds(pl.multiple_of(j * 8, 8), 8), :] = src_ref[pl.ds(j, 8, stride=SCAN_CHUNK), :]
        return carry
    lax.fori_loop(0, SCAN_CHUNK, step, 0, unroll=4)


def _deinterleave_rows(src_ref, dst_ref):
    def step(j, carry):
        dst_ref[pl.ds(j, 8, stride=SCAN_CHUNK), :] = src_ref[pl.ds(pl.multiple_of(j * 8, 8), 8), :]
        return carry
    lax.fori_loop(0, SCAN_CHUNK, step, 0, unroll=4)


def _scan_inplace(re_ref, im_ref, a_re, a_im, reverse):
    nq = len(a_re)
    ch = SCAN_CHUNK
    ab_re = [jnp.broadcast_to(a, (8, LANES)) for a in a_re]
    ab_im = [jnp.broadcast_to(a, (8, LANES)) for a in a_im]

    def rows(j):
        jj = (ch - 1 - j) if reverse else j
        return pl.ds(pl.multiple_of(jj * 8, 8), 8)

    def sweep(init, store):
        def step(j, st):
            out = []
            r = rows(j)
            for qi in range(nq):
                xr, xi = st[2 * qi], st[2 * qi + 1]
                pr, pi = _cmul(ab_re[qi], ab_im[qi], xr, xi)
                xr = pr + re_ref[qi, r, :]
                xi = pi + im_ref[qi, r, :]
                if store:
                    re_ref[qi, r, :] = xr
                    im_ref[qi, r, :] = xi
                out += [xr, xi]
            return tuple(out)
        return lax.fori_loop(0, ch, step, tuple(init), unroll=2)

    zeros = [jnp.zeros((8, LANES), F32)] * (2 * nq)
    finals = sweep(zeros, store=False)

    row_id = lax.broadcasted_iota(jnp.int32, (8, LANES), 0)
    carries = []
    for qi in range(nq):
        pr, pi = ab_re[qi], ab_im[qi]
        for _ in range(8):
            pr, pi = _cmul(pr, pi, pr, pi)
        fr, fi = finals[2 * qi], finals[2 * qi + 1]
        sr = jnp.zeros((8, LANES), F32)
        si = jnp.zeros((8, LANES), F32)
        for _ in range(7):
            tr, ti = _cmul(pr, pi, sr, si)
            tr, ti = tr + fr, ti + fi
            if reverse:
                sr = jnp.where(row_id == 7, 0.0, pltpu.roll(tr, 7, axis=0))
                si = jnp.where(row_id == 7, 0.0, pltpu.roll(ti, 7, axis=0))
            else:
                sr = jnp.where(row_id == 0, 0.0, pltpu.roll(tr, 1, axis=0))
                si = jnp.where(row_id == 0, 0.0, pltpu.roll(ti, 1, axis=0))
        carries += [sr, si]
    sweep(carries, store=True)


SSM_Q = 4


def _ssm_fwd(u, are, aim, bb, cc, dskip, after=None):
    nq = SSM_Q
    deps = [] if after is None else [after]

    def body(u_ref, ar_ref, ai_ref, bb_ref, cc_ref, d_ref, *rest):
        y_ref, xr_ref, xi_ref, sre, sim, up, yp = rest[len(deps):]
        _interleave_rows(u_ref, up)
        uf = up[...]
        ub = uf.astype(BF16)
        yp[...] = d_ref[...] * uf
        for d in range(2):
            for qi in range(nq):
                sre[qi] = _dot(ub, bb_ref[d, qi, :, :LANES])
                sim[qi] = _dot(ub, bb_ref[d, qi, :, LANES:])
            _scan_inplace(sre, sim, [ar_ref[d, qi] for qi in range(nq)], [ai_ref[d, qi] for qi in range(nq)],
                          reverse=(d == 1))
            for qi in range(nq):
                xrb = sre[qi].astype(BF16)
                xib = sim[qi].astype(BF16)
                xr_ref[d, qi] = xrb
                xi_ref[d, qi] = xib
                yp[...] += _dot(xrb, cc_ref[d, qi, :LANES, :]) + _dot(xib, cc_ref[d, qi, LANES:, :])
        _deinterleave_rows(yp, y_ref)

    blk4 = lambda k: (0, k, 0, 0)
    return pl.pallas_call(
        body, name="ssm_fwd", grid=(SSM_WIDTH // LANES,),
        in_specs=[pl.BlockSpec((SEQ, LANES), lambda k: (0, k)),
                  pl.BlockSpec((2, nq, 1, LANES), blk4), pl.BlockSpec((2, nq, 1, LANES), blk4),
                  pl.BlockSpec((2, nq, LANES, 2 * LANES), blk4), pl.BlockSpec((2, nq, 2 * LANES, LANES), blk4),
                  pl.BlockSpec((1, LANES), lambda k: (0, k))] + [pl.BlockSpec(memory_space=pl.ANY)] * len(deps),
        out_specs=[pl.BlockSpec((SEQ, LANES), lambda k: (0, k)),
                   pl.BlockSpec((2, nq, SEQ, LANES), blk4), pl.BlockSpec((2, nq, SEQ, LANES), blk4)],
        out_shape=[jax.ShapeDtypeStruct((SEQ, SSM_WIDTH), F32),
                   jax.ShapeDtypeStruct((2, N_LANE_BLOCKS, SEQ, LANES), BF16),
                   jax.ShapeDtypeStruct((2, N_LANE_BLOCKS, SEQ, LANES), BF16)],
        scratch_shapes=[pltpu.VMEM((nq, SEQ, LANES), F32), pltpu.VMEM((nq, SEQ, LANES), F32),
                        pltpu.VMEM((SEQ, LANES), F32), pltpu.VMEM((SEQ, LANES), F32)],
        compiler_params=_cparams(("parallel",)),
    )(u, are, aim, bb, cc, dskip, *deps)


def _ssm_bwd(dy, u, xr, xi, are, aim, bb, cc, dskip, after=None):
    nq = SSM_Q
    body_rows = SEQ - 8
    deps = [] if after is None else [after]

    def body(dy_ref, u_ref, xr_ref, xi_ref, ar_ref, ai_ref, bb_ref, cc_ref, d_ref, *rest):
        du_ref, dd_ref, dcc_ref, dbb_ref, dar_ref, dai_ref, sre, sim, up, dyp, dup = rest[len(deps):]
        _interleave_rows(u_ref, up)
        _interleave_rows(dy_ref, dyp)
        dyf = dyp[...]
        uf = up[...]
        dyb = dyf.astype(BF16)
        ub = uf.astype(BF16)
        dd_ref[...] = jnp.sum(dyf * uf, axis=0, keepdims=True)
        dup[...] = d_ref[...] * dyf
        row8 = lax.broadcasted_iota(jnp.int32, (8, LANES), 0)
        for d in range(2):
            for qi in range(nq):
                dx = _dot_nt(dyb, cc_ref[d, qi])
                sre[qi] = dx[:, :LANES]
                sim[qi] = dx[:, LANES:]
                dcc_ref[d, qi] = _dot_tn(jnp.concatenate([xr_ref[d, qi], xi_ref[d, qi]], axis=1), dyb)
            _scan_inplace(sre, sim, [ar_ref[d, qi] for qi in range(nq)], [-ai_ref[d, qi] for qi in range(nq)],
                          reverse=(d == 0))
            for qi in range(nq):
                gr = sre[qi]
                gi = sim[qi]
                xrf = xr_ref[d, qi].astype(F32)
                xif = xi_ref[d, qi].astype(F32)
                if d == 0:
                    g_main_r, g_main_i = gr[8:], gi[8:]
                    x_main_r, x_main_i = xrf[:body_rows], xif[:body_rows]
                    g_edge_r, g_edge_i = gr[:8], gi[:8]
                    x_edge_r = jnp.where(row8 == 0, 0.0, pltpu.roll(xrf[body_rows:], 1, axis=0))
                    x_edge_i = jnp.where(row8 == 0, 0.0, pltpu.roll(xif[body_rows:], 1, axis=0))
                else:
                    g_main_r, g_main_i = gr[:body_rows], gi[:body_rows]
                    x_main_r, x_main_i = xrf[8:], xif[8:]
                    g_edge_r, g_edge_i = gr[body_rows:], gi[body_rows:]
                    x_edge_r = jnp.where(row8 == 7, 0.0, pltpu.roll(xrf[:8], 7, axis=0))
                    x_edge_i = jnp.where(row8 == 7, 0.0, pltpu.roll(xif[:8], 7, axis=0))
                dar_ref[d, qi] = (jnp.sum(g_main_r * x_main_r + g_main_i * x_main_i, axis=0, keepdims=True)
                                  + jnp.sum(g_edge_r * x_edge_r + g_edge_i * x_edge_i, axis=0, keepdims=True))
                dai_ref[d, qi] = (jnp.sum(g_main_i * x_main_r - g_main_r * x_main_i, axis=0, keepdims=True)
                                  + jnp.sum(g_edge_i * x_edge_r - g_edge_r * x_edge_i, axis=0, keepdims=True))
                gb = jnp.concatenate([gr, gi], axis=1).astype(BF16)
                dup[...] += _dot_nt(gb, bb_ref[d, qi])
                dbb_ref[d, qi] = _dot_tn(ub, gb)
        _deinterleave_rows(dup, du_ref)

    blk4 = lambda k: (0, k, 0, 0)
    col = lambda k: (0, k)
    bb_spec = pl.BlockSpec((2, nq, LANES, 2 * LANES), blk4)
    cc_spec = pl.BlockSpec((2, nq, 2 * LANES, LANES), blk4)
    a_spec = pl.BlockSpec((2, nq, 1, LANES), blk4)
    x_spec = pl.BlockSpec((2, nq, SEQ, LANES), blk4)
    a_shape = jax.ShapeDtypeStruct((2, N_LANE_BLOCKS, 1, LANES), F32)
    return pl.pallas_call(
        body, name="ssm_bwd", grid=(SSM_WIDTH // LANES,),
        in_specs=[pl.BlockSpec((SEQ, LANES), col), pl.BlockSpec((SEQ, LANES), col), x_spec, x_spec,
                  a_spec, a_spec, bb_spec, cc_spec, pl.BlockSpec((1, LANES), col)]
        + [pl.BlockSpec(memory_space=pl.ANY)] * len(deps),
        out_specs=[pl.BlockSpec((SEQ, LANES), col), pl.BlockSpec((1, LANES), col),
                   cc_spec, bb_spec, a_spec, a_spec],
        out_shape=[jax.ShapeDtypeStruct((SEQ, SSM_WIDTH), F32), jax.ShapeDtypeStruct((1, SSM_WIDTH), F32),
                   jax.ShapeDtypeStruct((2, N_LANE_BLOCKS, 2 * LANES, LANES), F32),
                   jax.ShapeDtypeStruct((2, N_LANE_BLOCKS, LANES, 2 * LANES), F32), a_shape, a_shape],
        scratch_shapes=[pltpu.VMEM((nq, SEQ, LANES), F32), pltpu.VMEM((nq, SEQ, LANES), F32),
                        pltpu.VMEM((SEQ, LANES), F32), pltpu.VMEM((SEQ, LANES), F32), pltpu.VMEM((SEQ, LANES), F32)],
        compiler_params=_cparams(("parallel",)),
    )(dy, u, xr, xi, are, aim, bb, cc, dskip, *deps)


GELU_C = 0.7978845608028654
GELU_K = 0.044715


def _gelu(y):
    return 0.5 * y * (1.0 + jnp.tanh(GELU_C * (y + GELU_K * y * y * y)))


def _gelu_grad(y):
    t = jnp.tanh(GELU_C * (y + GELU_K * y * y * y))
    return 0.5 * (1.0 + t) + 0.5 * y * (1.0 - t * t) * GELU_C * (1.0 + 3.0 * GELU_K * y * y)


def _mixout_fwd(o, y, glu_w, glu_b, gan, gsn, wout, x1):
    tm = MIX_TM

    def body(o_ref, y_ref, gw_ref, gb_ref, gan_ref, gsn_ref, w_ref, x1_ref, x2_ref, mx_ref):
        yg = _gelu(y_ref[...])
        z = _dot(yg.astype(BF16), gw_ref[...]) + gb_ref[...]
        so = yg * _sigmoid(z)
        na = _rms_fwd(o_ref[...], gan_ref[...])
        ns = _rms_fwd(so, gsn_ref[...])
        mixed = jnp.concatenate([na, ns], axis=-1).astype(BF16)
        mx_ref[...] = mixed
        x2_ref[...] = x1_ref[...] + _dot(mixed, w_ref[...])

    row = lambda i: (i, 0)
    const = lambda i: (0, 0)
    return pl.pallas_call(
        body, name="mixout_fwd", grid=(SEQ // tm,),
        in_specs=[pl.BlockSpec((tm, ATTN_WIDTH), row), pl.BlockSpec((tm, SSM_WIDTH), row),
                  pl.BlockSpec((SSM_WIDTH, SSM_WIDTH), const), pl.BlockSpec((1, SSM_WIDTH), const),
                  pl.BlockSpec((1, ATTN_WIDTH), const), pl.BlockSpec((1, SSM_WIDTH), const),
                  pl.BlockSpec((D_MODEL, D_MODEL), const), pl.BlockSpec((tm, D_MODEL), row)],
        out_specs=[pl.BlockSpec((tm, D_MODEL), row), pl.BlockSpec((tm, D_MODEL), row)],
        out_shape=[jax.ShapeDtypeStruct((SEQ, D_MODEL), F32), jax.ShapeDtypeStruct((SEQ, D_MODEL), BF16)],
        compiler_params=_cparams(("parallel",)),
    )(o, y, glu_w, glu_b, gan, gsn, wout, x1)


def _mixout_bwd(dx2, o, y, glu_w, glu_b, gan, gsn, wout):
    tm = MIX_TM

    def body(dx2_ref, o_ref, y_ref, gw_ref, gb_ref, gan_ref, gsn_ref, w_ref,
             do_ref, dy_ref, dz_ref, yg_ref, dxb_ref, dgan_ref, dgsn_ref, dgb_ref):
        i = pl.program_id(0)
        dxb = dx2_ref[...].astype(BF16)
        dxb_ref[...] = dxb
        dmixed = _dot_nt(dxb, w_ref[...])
        do, dgan = _rms_bwd(dmixed[:, :ATTN_WIDTH], o_ref[...], gan_ref[...])
        do_ref[...] = do.T
        yv = y_ref[...]
        yg = _gelu(yv)
        ygb = yg.astype(BF16)
        yg_ref[...] = ygb
        sg = _sigmoid(_dot(ygb, gw_ref[...]) + gb_ref[...])
        dso, dgsn = _rms_bwd(dmixed[:, ATTN_WIDTH:], yg * sg, gsn_ref[...])
        dz = dso * yg * sg * (1.0 - sg)
        dzb = dz.astype(BF16)
        dz_ref[...] = dzb
        dyg = dso * sg + _dot_nt(dzb, gw_ref[...])
        dy_ref[...] = dyg * _gelu_grad(yv)
        dgb = jnp.sum(dz, axis=0, keepdims=True)

        @pl.when(i == 0)
        def _():
            dgan_ref[...] = dgan
            dgsn_ref[...] = dgsn
            dgb_ref[...] = dgb

        @pl.when(i != 0)
        def _():
            dgan_ref[...] += dgan
            dgsn_ref[...] += dgsn
            dgb_ref[...] += dgb

    row = lambda i: (i, 0)
    const = lambda i: (0, 0)
    return pl.pallas_call(
        body, name="mixout_bwd", grid=(SEQ // tm,),
        in_specs=[pl.BlockSpec((tm, D_MODEL), row), pl.BlockSpec((tm, ATTN_WIDTH), row),
                  pl.BlockSpec((tm, SSM_WIDTH), row),
                  pl.BlockSpec((SSM_WIDTH, SSM_WIDTH), const), pl.BlockSpec((1, SSM_WIDTH), const),
                  pl.BlockSpec((1, ATTN_WIDTH), const), pl.BlockSpec((1, SSM_WIDTH), const),
                  pl.BlockSpec((D_MODEL, D_MODEL), const)],
        out_specs=[pl.BlockSpec((ATTN_WIDTH, tm), lambda i: (0, i)), pl.BlockSpec((tm, SSM_WIDTH), row),
                   pl.BlockSpec((tm, SSM_WIDTH), row), pl.BlockSpec((tm, SSM_WIDTH), row),
                   pl.BlockSpec((tm, D_MODEL), row),
                   pl.BlockSpec((1, ATTN_WIDTH), const), pl.BlockSpec((1, SSM_WIDTH), const),
                   pl.BlockSpec((1, SSM_WIDTH), const)],
        out_shape=[jax.ShapeDtypeStruct((ATTN_WIDTH, SEQ), F32), jax.ShapeDtypeStruct((SEQ, SSM_WIDTH), F32),
                   jax.ShapeDtypeStruct((SEQ, SSM_WIDTH), BF16), jax.ShapeDtypeStruct((SEQ, SSM_WIDTH), BF16),
                   jax.ShapeDtypeStruct((SEQ, D_MODEL), BF16),
                   jax.ShapeDtypeStruct((1, ATTN_WIDTH), F32), jax.ShapeDtypeStruct((1, SSM_WIDTH), F32),
                   jax.ShapeDtypeStruct((1, SSM_WIDTH), F32)],
        compiler_params=_cparams(("arbitrary",)),
    )(dx2, o, y, glu_w, glu_b, gan, gsn, wout)


def _local_step(x, target, w, p, late_weights, early_grads, after=None, midway=None):
    x1, h1, a1, b1 = _ffn_fwd(x, p["norm_ffn1"], w["wgt1"], w["wut1"], w["wd1"], "ffn1_fwd", after=after)
    h2, q, k, v, u = _mixin_fwd(x1, p["norm_mix"], w["wint"])

    lam_re = p["ssm_lambda_re"].reshape(2 * N_LANE_BLOCKS, LANES)
    lam_im = p["ssm_lambda_im"].reshape(2 * N_LANE_BLOCKS, LANES)
    log_dt = jnp.repeat(p["ssm_log_dt"].reshape(2, 32), 64, axis=-1).reshape(2 * N_LANE_BLOCKS, LANES)
    a_re, a_im, bb, cc = _ssm_prep(lam_re, lam_im, log_dt, p["ssm_b_re"], p["ssm_b_im"],
                                   p["ssm_c_re"], p["ssm_c_im"])
    shape_a = (2, N_LANE_BLOCKS, 1, LANES)
    a_re4, a_im4 = a_re.reshape(shape_a), a_im.reshape(shape_a)
    bb4 = bb.reshape(2, N_LANE_BLOCKS, LANES, 2 * LANES)
    cc4 = cc.reshape(2, N_LANE_BLOCKS, 2 * LANES, LANES)
    dskip = p["ssm_d"].T.reshape(1, SSM_WIDTH)
    y, xr, xi = _ssm_fwd(u, a_re4, a_im4, bb4, cc4, dskip)
    o = _attn_fwd(q, k, v, p["attn_sinks"], after=None if midway is None else midway(y))

    w2 = late_weights(o)
    x2, mixed = _mixout_fwd(o, y, w2["glu"], p["ssm_glu_b"], p["attn_out_norm"], p["ssm_out_norm"], w2["wout"], x1)
    dx3, h3, a3, b3, loss, d_final = _ffn_fwd(x2, p["norm_ffn2"], w2["wgt2"], w2["wut2"], w2["wd2"], "ffn2_fwd",
                                              head=(p["final_norm"], target))
    dx2, da3, db3, s3, df3, d_n2 = _ffn_bwd_act(dx3, x2, p["norm_ffn2"], a3, b3, w2["wgt2"], w2["wut2"], w2["wd2"],
                                                "ffn2_bwd_act")
    g_wgt2, g_wut2, g_wd2 = _mm_tn([(da3, h3), (db3, h3), (s3, df3)], "ffn2_bwd_w")

    do, dy, dz, ygb, dx2b, d_gan, d_gsn, d_glub = _mixout_bwd(
        dx2, o, y, w2["glu"], p["ssm_glu_b"], p["attn_out_norm"], p["ssm_out_norm"], w2["wout"])
    (g_wout,) = _mm_tn([(mixed, dx2b)], "wout_bwd_w")
    (g_glu,) = _mm_tn([(ygb, dz)], "glu_bwd_w")
    sent = early_grads(dict(glu=g_glu, wout=g_wout, wgt2=g_wgt2, wut2=g_wut2, wd2=g_wd2))

    du, d_dskip, dcc, dbb, dar, dai = _ssm_bwd(dy, u, xr, xi, a_re4, a_im4, bb4, cc4, dskip, after=sent)
    nb = 2 * N_LANE_BLOCKS
    g_lre, g_lim, g_ldt, g_btr, g_bti, g_cre, g_cim = _ssm_prep_bwd(
        lam_re, lam_im, log_dt, p["ssm_b_re"], p["ssm_b_im"], dar.reshape(nb, LANES), dai.reshape(nb, LANES),
        dbb.reshape(nb, LANES, 2 * LANES), dcc.reshape(nb, 2 * LANES, LANES))

    dq, dk, dv, d_sinks = _attn_bwd(q, k, v, p["attn_sinks"], do)
    dx1, dproj, d_nmix = _mixin_bwd(dq, dk, dv, du, w["wint"], x1, p["norm_mix"], dx2)
    (g_wint,) = _mm_tn([(dproj, h2)], "win_bwd_w")

    dx0, da1, db1, s1, df1, d_n1 = _ffn_bwd_act(dx1, x, p["norm_ffn1"], a1, b1, w["wgt1"], w["wut1"], w["wd1"],
                                                "ffn1_bwd_act")
    g_wgt1, g_wut1, g_wd1 = _mm_tn([(da1, h1), (db1, h1), (s1, df1)], "ffn1_bwd_w")

    big = dict(wgt1=g_wgt1, wut1=g_wut1, wd1=g_wd1, wint=g_wint)
    small = dict(
        norm_ffn1=d_n1, norm_mix=d_nmix, attn_sinks=d_sinks,
        ssm_lambda_re=g_lre.reshape(64, 64), ssm_lambda_im=g_lim.reshape(64, 64),
        ssm_log_dt=g_ldt.reshape(2, 32), ssm_b_re=g_btr, ssm_b_im=g_bti, ssm_c_re=g_cre, ssm_c_im=g_cim,
        ssm_d=d_dskip.reshape(32, 16).T, ssm_glu_b=d_glub, attn_out_norm=d_gan, ssm_out_norm=d_gsn,
        norm_ffn2=d_n2, final_norm=d_final, loss=loss)
    return loss, dx0, big, small


BIG = dict(
    wgt1=("ffn1_w_gate", 352, 1024, True), wut1=("ffn1_w_up", 352, 1024, True), wd1=("ffn1_w_down", 352, 1024, False),
    wint=("w_in", 160, 1024, True), glu=("ssm_glu_w", 64, 512, False), wout=("w_out", 128, 1024, False),
    wgt2=("ffn2_w_gate", 352, 1024, True), wut2=("ffn2_w_up", 352, 1024, True), wd2=("ffn2_w_down", 352, 1024, False))

SMALL = dict(
    norm_ffn1=(1, 1024), norm_mix=(1, 1024), attn_sinks=(1, 8), ssm_lambda_re=(64, 64), ssm_lambda_im=(64, 64),
    ssm_log_dt=(2, 32), ssm_b_re=(1024, 64), ssm_b_im=(1024, 64), ssm_c_re=(1024, 64), ssm_c_im=(1024, 64),
    ssm_d=(16, 32), ssm_glu_b=(1, 512), attn_out_norm=(1, 512), ssm_out_norm=(1, 512), norm_ffn2=(1, 1024),
    final_norm=(1, 1024), loss=(1, 128))
SMALL_TRANSPOSED = ("ssm_b_re", "ssm_b_im", "ssm_d")
SMALL_PARAMS = tuple(n for n in SMALL if n != "loss")

SMALL_PAIRS = (("ssm_lambda_re", "ssm_lambda_im"), ("ssm_c_re", "ssm_c_im"), ("ssm_b_re", "ssm_b_im"))
SMALL_VECS = ("norm_ffn1", "norm_mix", "norm_ffn2", "final_norm", "ssm_glu_b", "attn_out_norm", "ssm_out_norm")
SMALL_TILES = ("ssm_log_dt", "attn_sinks", "ssm_d", "loss")


def _small_offsets():
    off, table = 0, {}
    for re, im in SMALL_PAIRS:
        table[re] = table[im] = off
        off += SMALL[re][0]
    for n in SMALL_VECS:
        table[n] = off
        off += SMALL[n][1] // LANES
    for n in SMALL_TILES:
        off = -(-off // 8) * 8
        table[n] = off
        off += SMALL[n][0]
    return table, off


SMALL_OFFSET, SMALL_USED_ROWS = _small_offsets()
SMALL_ROWS = -(-SMALL_USED_ROWS // (8 * N_DEV)) * 8 * N_DEV


def _cast_shards(shards):
    names = list(BIG)

    def body(*refs):
        ins, outs = refs[:len(names)], refs[len(names):]
        for idx in range(len(names)):
            outs[idx][...] = ins[idx][...].astype(BF16)

    return pl.pallas_call(
        body, name="cast_shards",
        out_shape=[jax.ShapeDtypeStruct((BIG[n][1], BIG[n][2]), BF16) for n in names],
        compiler_params=_cparams(),
    )(*[shards[n] for n in names])


def _peer(x, y, c, r):
    px = 1 - x if r & 4 else x
    py = 1 - y if r & 2 else y
    pc = 1 - c if r & 1 else c
    return px, py, pc


FIRST_GROUP = ("wgt1", "wut1", "wd1", "wint")
LATE_GROUP = ("glu", "wout", "wgt2", "wut2", "wd2")
N_PEERS = N_DEV - 1
ANY_SPEC = pl.BlockSpec(memory_space=pl.ANY)
HBM_SPEC = pl.BlockSpec(memory_space=pltpu.HBM)
SEM_SPEC = pl.BlockSpec(memory_space=pltpu.SEMAPHORE)
DATAFLOW_EFFECT = pltpu.SideEffectType.DATAFLOW_SIDE_EFFECTING


def _mesh_pos():
    x, y, c = lax.axis_index("x"), lax.axis_index("y"), lax.axis_index("c")
    return x, y, c, 4 * x + 2 * y + c


def _gather_first(first, late):
    nf, nl = len(first), len(late)

    def body(*refs):
        f_in, l_in = refs[:nf], refs[nf:nf + nl]
        f_out, l_out = refs[nf + nl:2 * nf + nl], refs[2 * nf + nl:2 * (nf + nl)]
        send_sems, recv_sems, local_sems = refs[2 * (nf + nl):]
        x, y, c, me = _mesh_pos()
        sibling = (x, y, 1 - c)
        chips = [(x, 1 - y), (1 - x, y), (1 - x, 1 - y)]

        def idx(px, py, pc):
            return 4 * px + 2 * py + pc

        def copy(k, s, block, to, src=None):
            slot = f_out[k].at[block]
            return pltpu.make_async_remote_copy(
                src_ref=slot if src is None else src, dst_ref=slot, send_sem=send_sems.at[k, s],
                recv_sem=recv_sems.at[k, s], device_id=to, device_id_type=MESH_ID)

        local = []
        for k in range(nf + nl):
            src, dst = (f_in[k], f_out[k]) if k < nf else (l_in[k - nf], l_out[k - nf])
            mine = pltpu.make_async_copy(src, dst.at[me], local_sems.at[k])
            mine.start()
            local.append(mine)
        sends = []
        for j, chip in enumerate(chips):
            for k in range(nf):
                sends.append(copy(k, 1 + j, me, (*chip, c), src=f_in[k]))
                sends[-1].start()
        for k in range(nf):
            sends.append(copy(k, 0, me, sibling, src=f_in[k]))
            sends[-1].start()
        for j, chip in enumerate(chips):
            for k in range(nf):
                copy(k, 1 + j, idx(*chip, c), (*chip, c)).wait_recv()
                sends.append(copy(k, 4 + j, idx(*chip, c), sibling))
                sends[-1].start()
        for k in range(nf):
            copy(k, 0, idx(*sibling), sibling).wait_recv()
        for j, chip in enumerate(chips):
            for k in range(nf):
                copy(k, 4 + j, idx(*chip, 1 - c), sibling).wait_recv()
        for cp in sends:
            cp.wait_send()
        for cp in local:
            cp.wait()

    return pl.pallas_call(
        body, name="gather_first",
        in_specs=[ANY_SPEC] * (nf + nl), out_specs=[ANY_SPEC] * (nf + nl),
        out_shape=[jax.ShapeDtypeStruct((N_DEV,) + s.shape, s.dtype) for s in list(first) + list(late)],
        scratch_shapes=[pltpu.SemaphoreType.DMA((nf, N_PEERS)), pltpu.SemaphoreType.DMA((nf, N_PEERS)),
                        pltpu.SemaphoreType.DMA((nf + nl,))],
        compiler_params=pltpu.CompilerParams(has_side_effects=True),
    )(*first, *late)


def _split_copy(src_refs, land_refs, send_sems, recv_sems, k, r, pos, scatter, receiving):
    x, y, c, me = pos
    px, py, pc = _peer(x, y, c, r)
    peer_idx = 4 * px + 2 * py + pc
    if scatter:
        src, dst = src_refs[k].at[peer_idx], land_refs[k].at[r - 1]
    else:
        src, dst = src_refs[k], land_refs[k].at[peer_idx if receiving else me]
    return pltpu.make_async_remote_copy(
        src_ref=src, dst_ref=dst, send_sem=send_sems.at[k * N_PEERS + r - 1],
        recv_sem=recv_sems.at[k * N_PEERS + r - 1], device_id=(px, py, pc), device_id_type=MESH_ID)


def _split_start(name, srcs, lands, scatter):
    n = len(srcs)

    def body(*refs):
        src_refs, land_refs = refs[:n], refs[n:2 * n]
        send_sems, recv_sems = refs[2 * n], refs[2 * n + 1]
        token = refs[-1]
        pos = _mesh_pos()
        for k in range(n):
            for r in range(1, N_DEV):
                _split_copy(src_refs, land_refs, send_sems, recv_sems, k, r, pos, scatter, False).start()
        token[...] = jnp.zeros_like(token)

    thru = [pltpu.HBM(a.shape, a.dtype) for a in list(srcs) + list(lands)]
    outs = pl.pallas_call(
        body, name=name,
        in_specs=[HBM_SPEC] * (2 * n),
        out_specs=[SEM_SPEC, SEM_SPEC] + [HBM_SPEC] * (2 * n) + [pl.BlockSpec(memory_space=pltpu.VMEM)],
        out_shape=[pltpu.SemaphoreType.DMA((n * N_PEERS,)), pltpu.SemaphoreType.DMA((n * N_PEERS,))] + thru
        + [jax.ShapeDtypeStruct((8, LANES), F32)],
        input_output_aliases={i: 2 + i for i in range(2 * n)},
        compiler_params=pltpu.CompilerParams(has_side_effects=DATAFLOW_EFFECT),
    )(*[pltpu.with_memory_space_constraint(a, pltpu.HBM) for a in list(srcs) + list(lands)])
    return outs[0], outs[1], outs[2:2 + n], outs[2 + n:2 + 2 * n], outs[-1]


def _split_wait(name, send_sems, recv_sems, srcs, lands, scatter, after):
    n = len(srcs)

    def body(*refs):
        src_refs, land_refs = refs[:n], refs[n:2 * n]
        send, recv = refs[2 * n], refs[2 * n + 1]
        pos = _mesh_pos()
        for k in range(n):
            for r in range(1, N_DEV):
                cp = _split_copy(src_refs, land_refs, send, recv, k, r, pos, scatter, True)
                cp.wait_send()
                cp.wait_recv()

    thru = [pltpu.HBM(a.shape, a.dtype) for a in list(srcs) + list(lands)]
    outs = pl.pallas_call(
        body, name=name,
        in_specs=[HBM_SPEC] * (2 * n) + [SEM_SPEC, SEM_SPEC, ANY_SPEC],
        out_specs=[HBM_SPEC] * (2 * n), out_shape=thru,
        input_output_aliases={i: i for i in range(2 * n)},
        compiler_params=pltpu.CompilerParams(has_side_effects=DATAFLOW_EFFECT),
    )(*srcs, *lands, send_sems, recv_sems, after)
    return outs[:n], outs[n:]


def _late_copy(passing, src_refs, land_refs, send_sems, recv_sems, k, s, pos, receiving):
    x, y, c, me = pos
    chips = [(x, 1 - y), (1 - x, y), (1 - x, 1 - y)]
    sibling = (x, y, 1 - c)

    def idx(dev):
        return 4 * dev[0] + 2 * dev[1] + dev[2]

    if passing:
        to = sibling
        block = idx((*chips[s], 1 - c)) if receiving else idx((*chips[s], c))
        src = dst = land_refs[k].at[block]
        sem = k * 3 + s
    else:
        to = sibling if s == 0 else (*chips[s - 1], c)
        src, dst = src_refs[k], land_refs[k].at[idx(to) if receiving else me]
        sem = k * 4 + s
    return pltpu.make_async_remote_copy(src_ref=src, dst_ref=dst, send_sem=send_sems.at[sem],
                                        recv_sem=recv_sems.at[sem], device_id=to, device_id_type=MESH_ID)


def _late_gather_call(name, stage, srcs, lands, sems, after=None):
    n = len(srcs)
    n_sem_in = len(sems)
    has_after = after is not None

    def body(*refs):
        src_refs, land_refs = refs[:n], refs[n:2 * n]
        sem_in = refs[2 * n:2 * n + n_sem_in]
        outs = refs[2 * n + n_sem_in + (1 if has_after else 0):]
        pos = _mesh_pos()
        if stage == 0:
            own_send, own_recv = outs[0], outs[1]
            for s in (1, 2, 3, 0):
                for k in range(n):
                    _late_copy(False, src_refs, land_refs, own_send, own_recv, k, s, pos, False).start()
            outs[-1][...] = jnp.zeros_like(outs[-1])
        elif stage == 1:
            own_recv = sem_in[1]
            pass_send, pass_recv = outs[0], outs[1]
            for s in range(3):
                for k in range(n):
                    _late_copy(False, src_refs, land_refs, sem_in[0], own_recv, k, s + 1, pos, True).wait_recv()
                    _late_copy(True, src_refs, land_refs, pass_send, pass_recv, k, s, pos, False).start()
            outs[-1][...] = jnp.zeros_like(outs[-1])
        else:
            own_send, own_recv, pass_send, pass_recv = sem_in
            for k in range(n):
                _late_copy(False, src_refs, land_refs, own_send, own_recv, k, 0, pos, True).wait_recv()
                for s in range(4):
                    _late_copy(False, src_refs, land_refs, own_send, own_recv, k, s, pos, False).wait_send()
                for s in range(3):
                    cp = _late_copy(True, src_refs, land_refs, pass_send, pass_recv, k, s, pos, True)
                    cp.wait_recv()
                    cp.wait_send()

    thru = [pltpu.HBM(a.shape, a.dtype) for a in list(srcs) + list(lands)]
    new_sems = [[pltpu.SemaphoreType.DMA((n * 4,))] * 2, [pltpu.SemaphoreType.DMA((n * 3,))] * 2, []][stage]
    extra = [] if stage == 2 else [jax.ShapeDtypeStruct((8, LANES), F32)]
    outs = pl.pallas_call(
        body, name=name,
        in_specs=[HBM_SPEC] * (2 * n) + [SEM_SPEC] * n_sem_in + [ANY_SPEC] * has_after,
        out_specs=[SEM_SPEC] * len(new_sems) + [HBM_SPEC] * (2 * n) + [pl.BlockSpec(memory_space=pltpu.VMEM)] * len(extra),
        out_shape=new_sems + thru + extra,
        input_output_aliases={i: len(new_sems) + i for i in range(2 * n)},
        compiler_params=pltpu.CompilerParams(has_side_effects=DATAFLOW_EFFECT),
    )(*[pltpu.with_memory_space_constraint(a, pltpu.HBM) for a in list(srcs) + list(lands)], *sems,
      *([after] if has_after else []))
    ns = len(new_sems)
    return list(outs[:ns]), outs[ns:ns + n], outs[ns + n:ns + 2 * n], (outs[-1] if extra else None)


N_SEND_SLOTS = 3


def _exchange_last(grads, small_packed):
    ng = len(grads)
    ch = SMALL_ROWS // N_DEV
    max_rows = max(g.shape[1] for g in grads)
    cols = grads[0].shape[2]

    def body(*refs):
        g_in, s_in = refs[:ng], refs[ng]
        outs = refs[ng + 1:]
        own_out, land, stage = outs[:ng], outs[ng:2 * ng], outs[2 * ng:3 * ng]
        s_red, s_stage = outs[3 * ng], outs[3 * ng + 1]
        (va, vb, vo, vs, sm_in, sm_out, d2d_send, d2d_recv, ici_send, ici_recv, s1_send, s1_recv, s2_send, s2_recv,
         local_sems) = outs[3 * ng + 2:]
        x, y, c, me = _mesh_pos()
        sibling = (x, y, 1 - c)
        chips = [(x, y), (x, 1 - y), (1 - x, y), (1 - x, 1 - y)]

        def idx(chip, core):
            return 4 * chip[0] + 2 * chip[1] + core

        def d2d(k, j):
            return pltpu.make_async_remote_copy(
                src_ref=g_in[k].at[idx(chips[j], 1 - c)], dst_ref=stage[k].at[j], send_sem=d2d_send.at[k, j],
                recv_sem=d2d_recv.at[k, j], device_id=sibling, device_id_type=MESH_ID)

        def ici(k, j, slot):
            rows = g_in[k].shape[1]
            return pltpu.make_async_remote_copy(
                src_ref=vo.at[slot, pl.ds(0, rows)], dst_ref=land[k].at[j - 1], send_sem=ici_send.at[k, j - 1],
                recv_sem=ici_recv.at[k, j - 1], device_id=(*chips[j], c), device_id_type=MESH_ID)

        def small_scatter(r):
            px, py, pc = _peer(x, y, c, r)
            return pltpu.make_async_remote_copy(
                src_ref=s_in.at[pl.ds(pl.multiple_of((4 * px + 2 * py + pc) * ch, 8), ch)], dst_ref=s_stage.at[me],
                send_sem=s1_send.at[r - 1], recv_sem=s1_recv.at[r - 1], device_id=(px, py, pc), device_id_type=MESH_ID)

        def small_gather(r):
            return pltpu.make_async_remote_copy(
                src_ref=sm_out, dst_ref=s_red.at[me], send_sem=s2_send.at[r - 1], recv_sem=s2_recv.at[r - 1],
                device_id=_peer(x, y, c, r), device_id_type=MESH_ID)

        for r in range(1, N_DEV):
            small_scatter(r).start()
        mine = pltpu.make_async_copy(s_in.at[pl.ds(pl.multiple_of(me * ch, 8), ch)], s_stage.at[me], local_sems.at[0])
        mine.start()
        pairs = [(k, j) for k in range(ng) for j in (1, 2, 3)] + [(k, 0) for k in range(ng)]
        for k, j in pairs:
            d2d(k, j).start()

        def reduce_small():
            for r in range(1, N_DEV):
                small_scatter(r).wait_recv()
            mine.wait()
            load = pltpu.make_async_copy(s_stage, sm_in, local_sems.at[1])
            load.start()
            load.wait()
            total = sm_in[0]
            for i in range(1, N_DEV):
                total = total + sm_in[i]
            sm_out[...] = total
            for r in range(1, N_DEV):
                small_gather(r).start()
            keep = pltpu.make_async_copy(sm_out, s_red.at[me], local_sems.at[2])
            keep.start()
            return keep

        in_flight = {}
        for i, (k, j) in enumerate(pairs):
            if i == N_SEND_SLOTS:
                keep = reduce_small()
            slot = i % N_SEND_SLOTS
            rows = g_in[k].shape[1]
            if slot in in_flight:
                in_flight.pop(slot).wait_send()
            d2d(k, j).wait_recv()
            la = pltpu.make_async_copy(g_in[k].at[idx(chips[j], c)], va.at[pl.ds(0, rows)], local_sems.at[3])
            lb = pltpu.make_async_copy(stage[k].at[j], vb.at[pl.ds(0, rows)], local_sems.at[4])
            la.start()
            lb.start()
            la.wait()
            lb.wait()
            total = va[pl.ds(0, rows)].astype(F32) + vb[pl.ds(0, rows)].astype(F32)
            if j == 0:
                vs[pl.ds(0, rows)] = total
                st = pltpu.make_async_copy(vs.at[pl.ds(0, rows)], own_out[k], local_sems.at[5])
                st.start()
                st.wait()
            else:
                vo[slot, pl.ds(0, rows)] = total.astype(BF16)
                cp = ici(k, j, slot)
                cp.start()
                in_flight[slot] = cp
        for cp in in_flight.values():
            cp.wait_send()

        for j in (1, 2, 3, 0):
            for k in range(ng):
                d2d(k, j).wait_send()
        for j in (1, 2, 3):
            for k in range(ng):
                ici(k, j, 0).wait_recv()
        for r in range(1, N_DEV):
            small_scatter(r).wait_send()
            small_gather(r).wait_send()
            small_gather(r).wait_recv()
        keep.wait()

    out_shape = [jax.ShapeDtypeStruct(g.shape[1:], F32) for g in grads]
    out_shape += [jax.ShapeDtypeStruct((3,) + g.shape[1:], BF16) for g in grads]
    out_shape += [jax.ShapeDtypeStruct((4,) + g.shape[1:], BF16) for g in grads]
    out_shape += [jax.ShapeDtypeStruct((N_DEV, ch, LANES), F32), jax.ShapeDtypeStruct((N_DEV, ch, LANES), F32)]
    outs = pl.pallas_call(
        body, name="exchange_last",
        in_specs=[ANY_SPEC] * (ng + 1), out_specs=[ANY_SPEC] * len(out_shape), out_shape=out_shape,
        scratch_shapes=[pltpu.VMEM((max_rows, cols), BF16), pltpu.VMEM((max_rows, cols), BF16),
                        pltpu.VMEM((N_SEND_SLOTS, max_rows, cols), BF16), pltpu.VMEM((max_rows, cols), F32),
                        pltpu.VMEM((N_DEV, ch, LANES), F32), pltpu.VMEM((ch, LANES), F32),
                        pltpu.SemaphoreType.DMA((ng, 4)), pltpu.SemaphoreType.DMA((ng, 4)),
                        pltpu.SemaphoreType.DMA((ng, 3)), pltpu.SemaphoreType.DMA((ng, 3)),
                        pltpu.SemaphoreType.DMA((N_PEERS,)), pltpu.SemaphoreType.DMA((N_PEERS,)),
                        pltpu.SemaphoreType.DMA((N_PEERS,)), pltpu.SemaphoreType.DMA((N_PEERS,)),
                        pltpu.SemaphoreType.DMA((6,))],
        compiler_params=pltpu.CompilerParams(has_side_effects=True, vmem_limit_bytes=VMEM_LIMIT),
    )(*grads, small_packed)
    return outs[:ng], outs[ng:2 * ng], outs[3 * ng].reshape(SMALL_ROWS, LANES)


def _adamw_math(w, g, m, v):
    m2 = ADAM_B1 * m + (1.0 - ADAM_B1) * g
    v2 = ADAM_B2 * v + (1.0 - ADAM_B2) * (g * g)
    m_hat = m2 / (1.0 - ADAM_B1 ** ADAM_STEP)
    v_hat = v2 / (1.0 - ADAM_B2 ** ADAM_STEP)
    delta = -ADAM_LR * (m_hat / (jnp.sqrt(v_hat) + ADAM_EPS) + ADAM_WD * w)
    return delta, m2, v2


ADAM_ROW_TILES = 2


def _adamw_big(own, parts, w, m, v, name):
    shape = w.shape
    own_is_blocks = own.ndim == 3
    tr = shape[0] // ADAM_ROW_TILES
    n_parts = parts.shape[0]

    def body(me_ref, own_ref, p_ref, w_ref, m_ref, v_ref, g_ref, d_ref, m2_ref, v2_ref):
        g = own_ref[...].astype(F32)
        for i in range(n_parts):
            g = g + p_ref[i].astype(F32)
        delta, m2, v2 = _adamw_math(w_ref[...], g, m_ref[...], v_ref[...])
        g_ref[...] = g
        d_ref[...] = delta
        m2_ref[...] = m2
        v2_ref[...] = v2

    me = (4 * lax.axis_index("x") + 2 * lax.axis_index("y") + lax.axis_index("c")).astype(jnp.int32).reshape(1)
    tile = pl.BlockSpec((tr, shape[1]), lambda i, me_ref: (i, 0))
    if own_is_blocks:
        own_spec = pl.BlockSpec((None, tr, shape[1]), lambda i, me_ref: (me_ref[0], i, 0))
    else:
        own_spec = tile
    grid_spec = pltpu.PrefetchScalarGridSpec(
        num_scalar_prefetch=1, grid=(ADAM_ROW_TILES,),
        in_specs=[own_spec, pl.BlockSpec((n_parts, tr, shape[1]), lambda i, me_ref: (0, i, 0)), tile, tile, tile],
        out_specs=[tile] * 4)
    return pl.pallas_call(
        body, name=name, grid_spec=grid_spec, out_shape=[jax.ShapeDtypeStruct(shape, F32)] * 4,
        compiler_params=_cparams(("arbitrary",)),
    )(me, own, parts, w, m, v)


def _pack_small(grads):
    names = list(SMALL)

    def body(*refs):
        ins, out = dict(zip(names, refs[:-1])), refs[-1]
        out[...] = jnp.zeros_like(out)
        for re, im in SMALL_PAIRS:
            off, rows = SMALL_OFFSET[re], SMALL[re][0]
            out[off:off + rows, :] = jnp.concatenate([ins[re][...], ins[im][...]], axis=1)
        for n in SMALL_VECS:
            off, vec = SMALL_OFFSET[n], ins[n][...]
            for i in range(SMALL[n][1] // LANES):
                out[off + i:off + i + 1, :] = vec[:, i * LANES:(i + 1) * LANES]
        for n in SMALL_TILES:
            off, (rows, cols) = SMALL_OFFSET[n], SMALL[n]
            out[off:off + rows, 0:cols] = ins[n][...]

    return pl.pallas_call(
        body, name="pack_small", out_shape=jax.ShapeDtypeStruct((SMALL_ROWS, LANES), F32),
        compiler_params=_cparams(),
    )(*[grads[n] for n in names])


def _unpack_small_ref(g_ref, n):
    off, (rows, cols) = SMALL_OFFSET[n], SMALL[n]
    for re, im in SMALL_PAIRS:
        if n == re:
            return g_ref[off:off + rows, 0:HALF_LANES]
        if n == im:
            return g_ref[off:off + rows, HALF_LANES:LANES]
    if n in SMALL_VECS:
        return jnp.concatenate([g_ref[off + i:off + i + 1, :] for i in range(cols // LANES)], axis=1)
    return g_ref[off:off + rows, 0:cols]


def _adamw_small(g_packed, w, m, v):
    names = list(SMALL_PARAMS)
    n = len(names)

    def body(g_ref, *refs):
        w_refs, m_refs, v_refs, outs = refs[:n], refs[n:2 * n], refs[2 * n:3 * n], refs[3 * n:]
        for idx, name in enumerate(names):
            g = _unpack_small_ref(g_ref, name)
            delta, m2, v2 = _adamw_math(w_refs[idx][...], g, m_refs[idx][...], v_refs[idx][...])
            outs[4 * idx][...] = g
            outs[4 * idx + 1][...] = delta
            outs[4 * idx + 2][...] = m2
            outs[4 * idx + 3][...] = v2
        outs[4 * n][...] = _unpack_small_ref(g_ref, "loss")

    outs = pl.pallas_call(
        body, name="adamw_small",
        out_shape=[jax.ShapeDtypeStruct(SMALL[name], F32) for name in names for _ in range(4)]
        + [jax.ShapeDtypeStruct(SMALL["loss"], F32)],
        compiler_params=_cparams(),
    )(g_packed, *[w[k] for k in names], *[m[k] for k in names], *[v[k] for k in names])
    return {name: outs[4 * idx:4 * idx + 4] for idx, name in enumerate(names)}, outs[4 * n]


WEIGHT_NAMES = ['norm_ffn1', 'ffn1_w_gate', 'ffn1_w_up', 'ffn1_w_down', 'norm_mix', 'w_in', 'attn_sinks',
                'ssm_lambda_re', 'ssm_lambda_im', 'ssm_log_dt', 'ssm_b_re', 'ssm_b_im', 'ssm_c_re', 'ssm_c_im',
                'ssm_d', 'ssm_glu_w', 'ssm_glu_b', 'attn_out_norm', 'ssm_out_norm', 'w_out', 'norm_ffn2',
                'ffn2_w_gate', 'ffn2_w_up', 'ffn2_w_down', 'final_norm']


def kernel(x, norm_ffn1, ffn1_w_gate, ffn1_w_up, ffn1_w_down, norm_mix, w_in, attn_sinks, ssm_lambda_re, ssm_lambda_im, ssm_log_dt, ssm_b_re, ssm_b_im, ssm_c_re, ssm_c_im, ssm_d, ssm_glu_w, ssm_glu_b, attn_out_norm, ssm_out_norm, w_out, norm_ffn2, ffn2_w_gate, ffn2_w_up, ffn2_w_down, final_norm, loss_target, m_norm_ffn1, m_ffn1_w_gate, m_ffn1_w_up, m_ffn1_w_down, m_norm_mix, m_w_in, m_attn_sinks, m_ssm_lambda_re, m_ssm_lambda_im, m_ssm_log_dt, m_ssm_b_re, m_ssm_b_im, m_ssm_c_re, m_ssm_c_im, m_ssm_d, m_ssm_glu_w, m_ssm_glu_b, m_attn_out_norm, m_ssm_out_norm, m_w_out, m_norm_ffn2, m_ffn2_w_gate, m_ffn2_w_up, m_ffn2_w_down, m_final_norm, v_norm_ffn1, v_ffn1_w_gate, v_ffn1_w_up, v_ffn1_w_down, v_norm_mix, v_w_in, v_attn_sinks, v_ssm_lambda_re, v_ssm_lambda_im, v_ssm_log_dt, v_ssm_b_re, v_ssm_b_im, v_ssm_c_re, v_ssm_c_im, v_ssm_d, v_ssm_glu_w, v_ssm_glu_b, v_attn_out_norm, v_ssm_out_norm, v_w_out, v_norm_ffn2, v_ffn2_w_gate, v_ffn2_w_up, v_ffn2_w_down, v_final_norm):
    args = dict(locals())
    weights = {n: args[n] for n in WEIGHT_NAMES}
    moms = {n: args["m_" + n] for n in WEIGHT_NAMES}
    vars_ = {n: args["v_" + n] for n in WEIGHT_NAMES}

    def shard2d(a, k):
        a = a.reshape(a.shape[-2], a.shape[-1])
        return a.T if BIG[k][3] else a

    def shard_master(a, k):
        return (a.T if BIG[k][3] else a).reshape(weights[BIG[k][0]].shape)

    def blocks(g, k):
        return g.reshape(N_DEV, BIG[k][1], BIG[k][2])

    def full(g, k):
        return g.reshape(N_DEV * BIG[k][1], BIG[k][2])

    shards = dict(zip(BIG, _cast_shards({k: shard2d(weights[BIG[k][0]], k) for k in BIG})))
    nf = len(FIRST_GROUP)
    got = _gather_first([shards[k] for k in FIRST_GROUP], [shards[k] for k in LATE_GROUP])
    w_first = {k: full(g, k) for k, g in zip(FIRST_GROUP, got[:nf])}
    late = {}
    late["own_sems"], late["srcs"], late["lands"], w_token = _late_gather_call(
        "gather_late_start", 0, [shards[k] for k in LATE_GROUP], got[nf:], [])

    def late_pass(dep):
        late["pass_sems"], late["srcs"], late["lands"], token = _late_gather_call(
            "gather_late_pass", 1, late["srcs"], late["lands"], late["own_sems"], after=dep)
        return token

    def late_weights(dep):
        _, _, lands, _ = _late_gather_call("gather_late_wait", 2, late["srcs"], late["lands"],
                                           late["own_sems"] + late["pass_sems"], after=dep)
        return {k: full(g, k) for k, g in zip(LATE_GROUP, lands)}

    early = {}

    def early_grads(g):
        srcs = [blocks(g[k], k) for k in LATE_GROUP]
        lands = [lax.empty((N_PEERS, BIG[k][1], BIG[k][2]), BF16) for k in LATE_GROUP]
        early["send"], early["recv"], early["srcs"], early["lands"], token = _split_start(
            "grads_late_start", srcs, lands, scatter=True)
        return token

    def small2d(a, n):
        if n in SMALL_TRANSPOSED:
            a = jnp.swapaxes(a, -1, -2)
        return a.reshape(SMALL[n])

    def small_master(a, n):
        if n in SMALL_TRANSPOSED:
            shape = weights[n].shape
            return jnp.swapaxes(a.reshape(shape[:-2] + (shape[-1], shape[-2])), -1, -2)
        return a.reshape(weights[n].shape)

    small_p = {n: small2d(weights[n], n) for n in SMALL_PARAMS}
    _, grad_x, g_first, g_small = _local_step(
        x.reshape(SEQ, D_MODEL), loss_target.reshape(SEQ, D_MODEL), w_first, small_p, late_weights, early_grads,
        after=w_token, midway=late_pass)

    own_sums, first_parts, small_grad = _exchange_last([blocks(g_first[k], k) for k in FIRST_GROUP],
                                                       _pack_small(g_small))
    own_late, late_parts = _split_wait("grads_late_wait", early["send"], early["recv"], early["srcs"],
                                       early["lands"], True, small_grad)
    own = dict(zip(FIRST_GROUP + LATE_GROUP, list(own_sums) + list(own_late)))
    parts = dict(zip(FIRST_GROUP + LATE_GROUP, list(first_parts) + list(late_parts)))
    outs = {}
    for k in BIG:
        n = BIG[k][0]
        outs[n] = [shard_master(o, k) for o in
                   _adamw_big(own[k], parts[k], shard2d(weights[n], k), shard2d(moms[n], k), shard2d(vars_[n], k),
                              "adamw_" + n)]
    small_out, loss_row = _adamw_small(small_grad, small_p, {n: small2d(moms[n], n) for n in SMALL_PARAMS},
                                       {n: small2d(vars_[n], n) for n in SMALL_PARAMS})
    for n in SMALL_PARAMS:
        outs[n] = [small_master(o, n) for o in small_out[n]]

    result = [loss_row[0, 0], grad_x.reshape(x.shape)]
    for i in range(4):
        result += [outs[n][i] for n in WEIGHT_NAMES]
    return tuple(result)
```

```python
import functools

import jax
import jax.numpy as jnp
from jax import lax
from jax.experimental import pallas as pl
from jax.experimental.pallas import tpu as pltpu

F32 = jnp.float32
BF16 = jnp.bfloat16

N_DEV = 8
SEQ = 2048
D_MODEL = 1024
D_FF = 2816
ATTN_HEADS = 8
KV_HEADS = 2
HEAD_DIM = 64
ATTN_WIDTH = 512
KV_WIDTH = 128
WINDOW = 128
SSM_WIDTH = 512
IN_WIDTH = 1280
EPS = 1e-6
MASKED_DISTANCE = 1e33
LAMBDA_RE_MAX = -1e-4
LANES = 128
N_LANE_BLOCKS = 16
SCAN_CHUNK = SEQ // 8

ADAM_LR = 0.001
ADAM_B1 = 0.9
ADAM_B2 = 0.999
ADAM_EPS = 1e-08
ADAM_WD = 0.01
ADAM_STEP = 10

VMEM_LIMIT = 60 * 1024 * 1024
MESH_ID = pl.DeviceIdType.MESH


def _cparams(sem=None):
    return pltpu.CompilerParams(dimension_semantics=sem, vmem_limit_bytes=VMEM_LIMIT)


def _dot(a, b):
    return jnp.dot(a, b, preferred_element_type=F32)


def _dot_nt(a, b):
    return lax.dot_general(a, b, (((1,), (1,)), ((), ())), preferred_element_type=F32)


def _dot_tn(a, b):
    return lax.dot_general(a, b, (((0,), (0,)), ((), ())), preferred_element_type=F32)


def _rms_fwd(x, g):
    r = lax.rsqrt(jnp.mean(x * x, axis=-1, keepdims=True) + EPS)
    return x * r * g


def _rms_bwd(dh, x, g):
    r = lax.rsqrt(jnp.mean(x * x, axis=-1, keepdims=True) + EPS)
    xh = x * r
    dg = jnp.sum(dh * xh, axis=0, keepdims=True)
    dxh = dh * g
    dx = r * (dxh - xh * jnp.mean(dxh * xh, axis=-1, keepdims=True))
    return dx, dg


def _sigmoid(x):
    return 1.0 / (1.0 + jnp.exp(-x))


FFN_TM = 512
FFN_TF = 1408


def _ffn_fwd(x, g, wgt, wut, wd, name, after=None, head=None):
    tm, tf = FFN_TM // 2, D_FF
    nj = D_FF // tf
    deps = [] if after is None else [after]
    n_in = len(deps) + (2 if head else 0)

    def body(x_ref, g_ref, wg_ref, wu_ref, wd_ref, *rest):
        i = pl.program_id(0)
        j = pl.program_id(1)
        if head:
            gf_ref, t_ref = rest[len(deps):n_in]
            xo_ref, h_ref, a_ref, b_ref, loss_ref, dgf_ref, h_s, acc = rest[n_in:]
        else:
            xo_ref, h_ref, a_ref, b_ref, h_s, acc = rest[n_in:]

        @pl.when(j == 0)
        def _():
            h = _rms_fwd(x_ref[...], g_ref[...]).astype(BF16)
            h_s[...] = h
            h_ref[...] = h
            acc[...] = jnp.zeros_like(acc)

        h = h_s[...]
        a = _dot_nt(h, wg_ref[...])
        b = _dot_nt(h, wu_ref[...])
        a_ref[...] = a.astype(BF16)
        b_ref[...] = b.astype(BF16)
        s = (a * _sigmoid(a) * b).astype(BF16)
        acc[...] += _dot(s, wd_ref[...])

        @pl.when(j == nj - 1)
        def _():
            xo = x_ref[...] + 0.5 * acc[...]
            if not head:
                xo_ref[...] = xo
                return
            gf = gf_ref[...]
            err = _rms_fwd(xo, gf) - t_ref[...]
            part = jnp.broadcast_to(0.5 * jnp.sum(err * err) / D_MODEL, (1, LANES))
            dx, dgf = _rms_bwd(err * (1.0 / D_MODEL), xo, gf)
            xo_ref[...] = dx

            @pl.when(i == 0)
            def _():
                loss_ref[...] = part
                dgf_ref[...] = dgf

            @pl.when(i != 0)
            def _():
                loss_ref[...] += part
                dgf_ref[...] += dgf

    row = lambda i, j: (i, 0)
    const = lambda i, j: (0, 0)
    head_in = [pl.BlockSpec((1, D_MODEL), const), pl.BlockSpec((tm, D_MODEL), row)] if head else []
    head_out = [pl.BlockSpec((1, LANES), const), pl.BlockSpec((1, D_MODEL), const)] if head else []
    head_shape = [jax.ShapeDtypeStruct((1, LANES), F32), jax.ShapeDtypeStruct((1, D_MODEL), F32)] if head else []
    return pl.pallas_call(
        body, name=name, grid=(SEQ // tm, nj),
        in_specs=[pl.BlockSpec((tm, D_MODEL), row), pl.BlockSpec((1, D_MODEL), const),
                  pl.BlockSpec((tf, D_MODEL), lambda i, j: (j, 0)),
                  pl.BlockSpec((tf, D_MODEL), lambda i, j: (j, 0)),
                  pl.BlockSpec((tf, D_MODEL), lambda i, j: (j, 0))] + [pl.BlockSpec(memory_space=pl.ANY)] * len(deps)
        + head_in,
        out_specs=[pl.BlockSpec((tm, D_MODEL), row), pl.BlockSpec((tm, D_MODEL), row),
                   pl.BlockSpec((tm, tf), lambda i, j: (i, j)),
                   pl.BlockSpec((tm, tf), lambda i, j: (i, j))] + head_out,
        out_shape=[jax.ShapeDtypeStruct((SEQ, D_MODEL), F32), jax.ShapeDtypeStruct((SEQ, D_MODEL), BF16),
                   jax.ShapeDtypeStruct((SEQ, D_FF), BF16), jax.ShapeDtypeStruct((SEQ, D_FF), BF16)] + head_shape,
        scratch_shapes=[pltpu.VMEM((tm, D_MODEL), BF16), pltpu.VMEM((tm, D_MODEL), F32)],
        compiler_params=_cparams(("arbitrary" if head else "parallel", "arbitrary")),
    )(x, g, wgt, wut, wd, *deps, *(head or ()))


def _ffn_bwd_act(dxo, x, g, a, b, wgt, wut, wd, name):
    tm, tf = FFN_TM // 2, D_FF
    nj = D_FF // tf

    def body(dxo_ref, x_ref, g_ref, a_ref, b_ref, wg_hbm, wu_hbm, wd_hbm,
             dx_ref, da_ref, db_ref, s_ref, df_ref, dg_ref, df_s, acc, wg_ref, wu_ref, wd_ref, w_sems):
        i = pl.program_id(0)
        j = pl.program_id(1)
        first = (i == 0) & (j == 0)
        load_wd = pltpu.make_async_copy(wd_hbm, wd_ref, w_sems.at[0])
        load_wg = pltpu.make_async_copy(wg_hbm, wg_ref, w_sems.at[1])
        load_wu = pltpu.make_async_copy(wu_hbm, wu_ref, w_sems.at[2])

        @pl.when(first)
        def _():
            load_wd.start()
            load_wg.start()
            load_wu.start()

        @pl.when(j == 0)
        def _():
            df = (0.5 * dxo_ref[...]).astype(BF16)
            df_s[...] = df
            df_ref[...] = df
            acc[...] = jnp.zeros_like(acc)

        @pl.when(first)
        def _():
            load_wd.wait()

        ds = _dot_nt(df_s[...], wd_ref[...])
        av = a_ref[...].astype(F32)
        bv = b_ref[...].astype(F32)
        sig = _sigmoid(av)
        sl = av * sig
        s_ref[...] = (sl * bv).astype(BF16)
        db = (ds * sl).astype(BF16)
        da = (ds * bv * (sig * (1.0 + av * (1.0 - sig)))).astype(BF16)
        da_ref[...] = da
        db_ref[...] = db

        @pl.when(first)
        def _():
            load_wg.wait()
            load_wu.wait()

        acc[...] += _dot(da, wg_ref[...]) + _dot(db, wu_ref[...])

        @pl.when(j == nj - 1)
        def _():
            dx, dg = _rms_bwd(acc[...], x_ref[...], g_ref[...])
            dx_ref[...] = dxo_ref[...] + dx

            @pl.when(i == 0)
            def _():
                dg_ref[...] = dg

            @pl.when(i != 0)
            def _():
                dg_ref[...] += dg

    row = lambda i, j: (i, 0)
    col = lambda i, j: (j, 0)
    tile = lambda i, j: (i, j)
    return pl.pallas_call(
        body, name=name, grid=(SEQ // tm, nj),
        in_specs=[pl.BlockSpec((tm, D_MODEL), row), pl.BlockSpec((tm, D_MODEL), row),
                  pl.BlockSpec((1, D_MODEL), lambda i, j: (0, 0)),
                  pl.BlockSpec((tm, tf), tile), pl.BlockSpec((tm, tf), tile),
                  pl.BlockSpec(memory_space=pl.ANY), pl.BlockSpec(memory_space=pl.ANY),
                  pl.BlockSpec(memory_space=pl.ANY)],
        out_specs=[pl.BlockSpec((tm, D_MODEL), row),
                   pl.BlockSpec((tm, tf), tile), pl.BlockSpec((tm, tf), tile), pl.BlockSpec((tm, tf), tile),
                   pl.BlockSpec((tm, D_MODEL), row),
                   pl.BlockSpec((1, D_MODEL), lambda i, j: (0, 0))],
        out_shape=[jax.ShapeDtypeStruct((SEQ, D_MODEL), F32),
                   jax.ShapeDtypeStruct((SEQ, D_FF), BF16), jax.ShapeDtypeStruct((SEQ, D_FF), BF16),
                   jax.ShapeDtypeStruct((SEQ, D_FF), BF16),
                   jax.ShapeDtypeStruct((SEQ, D_MODEL), BF16),
                   jax.ShapeDtypeStruct((1, D_MODEL), F32)],
        scratch_shapes=[pltpu.VMEM((tm, D_MODEL), BF16), pltpu.VMEM((tm, D_MODEL), F32),
                        pltpu.VMEM((D_FF, D_MODEL), BF16), pltpu.VMEM((D_FF, D_MODEL), BF16),
                        pltpu.VMEM((D_FF, D_MODEL), BF16), pltpu.SemaphoreType.DMA((3,))],
        compiler_params=_cparams(("arbitrary", "arbitrary")),
    )(dxo, x, g, a, b, wgt, wut, wd)


def _mm_tn(pairs, name, tmm=256):
    m = pairs[0][0].shape[1]
    n_pairs = len(pairs)

    def body(*refs):
        ins, outs = refs[:2 * n_pairs], refs[2 * n_pairs:]
        for p in range(n_pairs):
            outs[p][...] = _dot_tn(ins[2 * p][...], ins[2 * p + 1][...]).astype(BF16)

    in_specs, out_specs, out_shape, args = [], [], [], []
    for a, b in pairs:
        n = b.shape[1]
        in_specs += [pl.BlockSpec((SEQ, tmm), lambda i: (0, i)), pl.BlockSpec((SEQ, n), lambda i: (0, 0))]
        out_specs.append(pl.BlockSpec((tmm, n), lambda i: (i, 0)))
        out_shape.append(jax.ShapeDtypeStruct((m, n), BF16))
        args += [a, b]
    return pl.pallas_call(body, name=name, grid=(m // tmm,), in_specs=in_specs, out_specs=out_specs,
                          out_shape=out_shape, compiler_params=_cparams(("parallel",)))(*args)


MIX_TM = 512


def _mixin_fwd(x, g, wint):
    tm = MIX_TM

    def body(x_ref, g_ref, w_ref, h_ref, q_ref, k_ref, v_ref, u_ref):
        h = _rms_fwd(x_ref[...], g_ref[...]).astype(BF16)
        h_ref[...] = h
        proj = _dot_nt(h, w_ref[...])
        q_ref[...] = proj[:, :ATTN_WIDTH].T
        k_ref[...] = proj[:, ATTN_WIDTH:ATTN_WIDTH + KV_WIDTH]
        v_ref[...] = proj[:, ATTN_WIDTH + KV_WIDTH:ATTN_WIDTH + 2 * KV_WIDTH]
        u_ref[...] = proj[:, ATTN_WIDTH + 2 * KV_WIDTH:]

    row = lambda i: (i, 0)
    return pl.pallas_call(
        body, name="mixin_fwd", grid=(SEQ // tm,),
        in_specs=[pl.BlockSpec((tm, D_MODEL), row), pl.BlockSpec((1, D_MODEL), lambda i: (0, 0)),
                  pl.BlockSpec((IN_WIDTH, D_MODEL), lambda i: (0, 0))],
        out_specs=[pl.BlockSpec((tm, D_MODEL), row), pl.BlockSpec((ATTN_WIDTH, tm), lambda i: (0, i)),
                   pl.BlockSpec((tm, KV_WIDTH), row), pl.BlockSpec((tm, KV_WIDTH), row),
                   pl.BlockSpec((tm, SSM_WIDTH), row)],
        out_shape=[jax.ShapeDtypeStruct((SEQ, D_MODEL), BF16), jax.ShapeDtypeStruct((ATTN_WIDTH, SEQ), F32),
                   jax.ShapeDtypeStruct((SEQ, KV_WIDTH), F32), jax.ShapeDtypeStruct((SEQ, KV_WIDTH), F32),
                   jax.ShapeDtypeStruct((SEQ, SSM_WIDTH), F32)],
        compiler_params=_cparams(("parallel",)),
    )(x, g, wint)


def _mixin_bwd(dqt, dk, dv, du, wint, x, g, dres):
    tm = MIX_TM

    def body(dq_ref, dk_ref, dv_ref, du_ref, w_ref, x_ref, g_ref, dres_ref, dx_ref, dp_ref, dg_ref):
        i = pl.program_id(0)
        dp = jnp.concatenate([dq_ref[...].T, dk_ref[...], dv_ref[...], du_ref[...]], axis=-1).astype(BF16)
        dp_ref[...] = dp
        dh = _dot(dp, w_ref[...])
        dx, dg = _rms_bwd(dh, x_ref[...], g_ref[...])
        dx_ref[...] = dres_ref[...] + dx

        @pl.when(i == 0)
        def _():
            dg_ref[...] = dg

        @pl.when(i != 0)
        def _():
            dg_ref[...] += dg

    row = lambda i: (i, 0)
    const = lambda i: (0, 0)
    return pl.pallas_call(
        body, name="mixin_bwd", grid=(SEQ // tm,),
        in_specs=[pl.BlockSpec((ATTN_WIDTH, tm), lambda i: (0, i)), pl.BlockSpec((tm, KV_WIDTH), row),
                  pl.BlockSpec((tm, KV_WIDTH), row), pl.BlockSpec((tm, SSM_WIDTH), row),
                  pl.BlockSpec((IN_WIDTH, D_MODEL), const), pl.BlockSpec((tm, D_MODEL), row),
                  pl.BlockSpec((1, D_MODEL), const), pl.BlockSpec((tm, D_MODEL), row)],
        out_specs=[pl.BlockSpec((tm, D_MODEL), row), pl.BlockSpec((tm, IN_WIDTH), row),
                   pl.BlockSpec((1, D_MODEL), const)],
        out_shape=[jax.ShapeDtypeStruct((SEQ, D_MODEL), F32), jax.ShapeDtypeStruct((SEQ, IN_WIDTH), BF16),
                   jax.ShapeDtypeStruct((1, D_MODEL), F32)],
        compiler_params=_cparams(("arbitrary",)),
    )(dqt, dk, dv, du, wint, x, g, dres)


N_QBLOCKS = SEQ // WINDOW
GROUP = ATTN_HEADS // KV_HEADS
SCALE = HEAD_DIM ** -0.5


def _alibi_slope(h):
    return 2.0 ** (-8.0 * (h + 1) / ATTN_HEADS)


def _window_masks(n):
    s_idx = lax.broadcasted_iota(jnp.int32, (3 * WINDOW, WINDOW), 0)
    t_idx = lax.broadcasted_iota(jnp.int32, (3 * WINDOW, WINDOW), 1)
    absrel = jnp.abs(s_idx - WINDOW - t_idx)
    key_pos = n * WINDOW - WINDOW + s_idx
    valid = (absrel <= WINDOW) & (key_pos >= 0) & (key_pos < SEQ)
    return jnp.where(valid, absrel.astype(F32), MASKED_DISTANCE)


def _group_cols(ref, r0, gi):
    return jnp.concatenate(
        [ref[(gi * GROUP + hh) * HEAD_DIM:(gi * GROUP + hh + 1) * HEAD_DIM, pl.ds(r0, WINDOW)].astype(BF16)
         for hh in range(GROUP)], axis=1)


def _group_probs(qgt, kw, dist, gi, sk_ref):
    bias = jnp.concatenate([-_alibi_slope(gi * GROUP + hh) * dist for hh in range(GROUP)], axis=1)
    sink = jnp.concatenate([jnp.full((1, WINDOW), sk_ref[0, gi * GROUP + hh], F32) for hh in range(GROUP)], axis=1)
    s = _dot(kw, qgt) * SCALE + bias
    m = jnp.maximum(jnp.max(s, axis=0, keepdims=True), sink)
    p = jnp.exp(s - m)
    ps = jnp.exp(sink - m)
    inv = 1.0 / (jnp.sum(p, axis=0, keepdims=True) + ps)
    return p * inv, ps * inv


def _pad_window(src_ref, dst_ref):
    zeros = jnp.zeros((WINDOW, KV_WIDTH), BF16)
    dst_ref[0:WINDOW, :] = zeros
    dst_ref[WINDOW + SEQ:, :] = zeros
    dst_ref[WINDOW:WINDOW + SEQ, :] = src_ref[...].astype(BF16)


def _attn_fwd(qt, k, v, sinks, after=None):
    deps = [] if after is None else [after]

    def body(sk_ref, qt_ref, k_ref, v_ref, *rest):
        o_ref, kp_ref, vp_ref = rest[len(deps):]
        _pad_window(k_ref, kp_ref)
        _pad_window(v_ref, vp_ref)

        def blk(n, carry):
            r0 = pl.multiple_of(n * WINDOW, WINDOW)
            dist = _window_masks(n)
            for gi in range(KV_HEADS):
                kw = kp_ref[pl.ds(r0, 3 * WINDOW), gi * HEAD_DIM:(gi + 1) * HEAD_DIM]
                vw = vp_ref[pl.ds(r0, 3 * WINDOW), gi * HEAD_DIM:(gi + 1) * HEAD_DIM]
                pr, _ = _group_probs(_group_cols(qt_ref, r0, gi), kw, dist, gi, sk_ref)
                og = _dot_tn(pr.astype(BF16), vw)
                for hh in range(GROUP):
                    h = gi * GROUP + hh
                    o_ref[pl.ds(r0, WINDOW), h * HEAD_DIM:(h + 1) * HEAD_DIM] = og[hh * WINDOW:(hh + 1) * WINDOW]
            return carry

        lax.fori_loop(0, N_QBLOCKS, blk, 0)

    vmem = pl.BlockSpec(memory_space=pltpu.VMEM)
    return pl.pallas_call(
        body, name="attn_fwd",
        in_specs=[pl.BlockSpec(memory_space=pltpu.SMEM), vmem, vmem, vmem]
        + [pl.BlockSpec(memory_space=pl.ANY)] * len(deps), out_specs=vmem,
        out_shape=jax.ShapeDtypeStruct((SEQ, ATTN_WIDTH), F32),
        scratch_shapes=[pltpu.VMEM((SEQ + 2 * WINDOW, KV_WIDTH), BF16)] * 2,
        compiler_params=_cparams(),
    )(sinks, qt, k, v, *deps)


def _attn_bwd(qt, k, v, sinks, dot_):
    def body(sk_ref, qt_ref, k_ref, v_ref, dot_ref, dqt_ref, dk_ref, dv_ref, dsk_ref,
             dsk_acc, kp_ref, vp_ref, dkp_ref, dvp_ref):
        _pad_window(k_ref, kp_ref)
        _pad_window(v_ref, vp_ref)
        dkp_ref[...] = jnp.zeros_like(dkp_ref)
        dvp_ref[...] = jnp.zeros_like(dvp_ref)
        dsk_acc[...] = jnp.zeros_like(dsk_acc)

        def blk(n, carry):
            r0 = pl.multiple_of(n * WINDOW, WINDOW)
            dist = _window_masks(n)
            for gi in range(KV_HEADS):
                gcols = slice(gi * HEAD_DIM, (gi + 1) * HEAD_DIM)
                kw = kp_ref[pl.ds(r0, 3 * WINDOW), gcols]
                vw = vp_ref[pl.ds(r0, 3 * WINDOW), gcols]
                qgt = _group_cols(qt_ref, r0, gi)
                dogt = _group_cols(dot_ref, r0, gi)
                pr, psink = _group_probs(qgt, kw, dist, gi, sk_ref)
                dp = _dot(vw, dogt)
                delta = jnp.sum(pr * dp, axis=0, keepdims=True)
                ds = (pr * (dp - delta)).astype(BF16)
                dsk_acc[gi:gi + 1, :] += -(psink * delta)
                dqgt = _dot_tn(kw, ds) * SCALE
                for hh in range(GROUP):
                    h = gi * GROUP + hh
                    dqt_ref[h * HEAD_DIM:(h + 1) * HEAD_DIM, pl.ds(r0, WINDOW)] = dqgt[:, hh * WINDOW:(hh + 1) * WINDOW]
                dkp_ref[pl.ds(r0, 3 * WINDOW), gcols] += _dot_nt(ds, qgt) * SCALE
                dvp_ref[pl.ds(r0, 3 * WINDOW), gcols] += _dot_nt(pr.astype(BF16), dogt)
            return carry

        lax.fori_loop(0, N_QBLOCKS, blk, 0)
        for h in range(ATTN_HEADS):
            gi, hh = divmod(h, GROUP)
            dsk_ref[:, h:h + 1] = jnp.sum(dsk_acc[gi:gi + 1, hh * WINDOW:(hh + 1) * WINDOW], axis=1, keepdims=True)
        dk_ref[...] = dkp_ref[WINDOW:WINDOW + SEQ, :]
        dv_ref[...] = dvp_ref[WINDOW:WINDOW + SEQ, :]

    vmem = pl.BlockSpec(memory_space=pltpu.VMEM)
    padded = (SEQ + 2 * WINDOW, KV_WIDTH)
    return pl.pallas_call(
        body, name="attn_bwd",
        in_specs=[pl.BlockSpec(memory_space=pltpu.SMEM), vmem, vmem, vmem, vmem],
        out_specs=[vmem, vmem, vmem, vmem],
        out_shape=[jax.ShapeDtypeStruct((ATTN_WIDTH, SEQ), F32),
                   jax.ShapeDtypeStruct((SEQ, KV_WIDTH), F32), jax.ShapeDtypeStruct((SEQ, KV_WIDTH), F32),
                   jax.ShapeDtypeStruct((1, ATTN_HEADS), F32)],
        scratch_shapes=[pltpu.VMEM((KV_HEADS, GROUP * WINDOW), F32), pltpu.VMEM(padded, BF16),
                        pltpu.VMEM(padded, BF16), pltpu.VMEM(padded, F32), pltpu.VMEM(padded, F32)],
        compiler_params=_cparams(),
    )(sinks, qt, k, v, dot_)


HALF_LANES = LANES // 2
BLOCK_ROWS = 32


def _embed_block(bt, q):
    z = jnp.zeros((16, HALF_LANES), bt.dtype)
    blk = jnp.concatenate([jnp.concatenate([bt[:16], z], axis=1), jnp.concatenate([z, bt[16:]], axis=1)], axis=0)
    parts = [jnp.zeros((BLOCK_ROWS * q, LANES), bt.dtype)] if q else []
    parts.append(blk)
    if q < 3:
        parts.append(jnp.zeros((BLOCK_ROWS * (3 - q), LANES), bt.dtype))
    return jnp.concatenate(parts, axis=0)


def _extract_block(m, q):
    blk = m[BLOCK_ROWS * q:BLOCK_ROWS * (q + 1)]
    return jnp.concatenate([blk[:16, :HALF_LANES], blk[16:, HALF_LANES:]], axis=0)


def _ssm_prep(lam_re, lam_im, log_dt, bt_re, bt_im, c_re, c_im):
    nb = 2 * N_LANE_BLOCKS

    def body(lr_ref, li_ref, ldt_ref, btr_ref, bti_ref, ctr_ref, cti_ref, ar_ref, ai_ref, bb_ref, cc_ref):
        lr = jnp.minimum(lr_ref[...], LAMBDA_RE_MAX)
        li = li_ref[...]
        dt = jnp.exp(ldt_ref[...])
        mag = jnp.exp(lr * dt)
        ar = mag * jnp.cos(li * dt)
        ai = mag * jnp.sin(li * dt)
        den = lr * lr + li * li
        cr = ((ar - 1.0) * lr + ai * li) / den
        ci = (ai * lr - (ar - 1.0) * li) / den
        ar_ref[...] = ar
        ai_ref[...] = ai
        for i in range(nb):
            q = i % 4
            rows = slice(BLOCK_ROWS * i, BLOCK_ROWS * (i + 1))
            br = _embed_block(btr_ref[rows, :], q)
            bi = _embed_block(bti_ref[rows, :], q)
            cri, cii = cr[i:i + 1, :], ci[i:i + 1, :]
            bb_ref[i] = jnp.concatenate([cri * br - cii * bi, cri * bi + cii * br], axis=1).astype(BF16)
            cc_ref[i] = jnp.concatenate([_embed_block(ctr_ref[rows, :], q).T,
                                         -_embed_block(cti_ref[rows, :], q).T], axis=0).astype(BF16)

    return pl.pallas_call(
        body, name="ssm_prep",
        out_shape=[jax.ShapeDtypeStruct((nb, LANES), F32), jax.ShapeDtypeStruct((nb, LANES), F32),
                   jax.ShapeDtypeStruct((nb, LANES, 2 * LANES), BF16),
                   jax.ShapeDtypeStruct((nb, 2 * LANES, LANES), BF16)],
        compiler_params=_cparams(),
    )(lam_re, lam_im, log_dt, bt_re, bt_im, c_re, c_im)


def _ssm_prep_bwd(lam_re, lam_im, log_dt, bt_re, bt_im, dar, dai, dbb, dcc):
    nb = 2 * N_LANE_BLOCKS

    def body(lr_ref, li_ref, ldt_ref, btr_ref, bti_ref, dar_ref, dai_ref, dbb_ref, dcc_ref,
             glr_ref, gli_ref, gdt_ref, gbr_ref, gbi_ref, gcre_ref, gcim_ref, gcr_s, gci_s):
        lam = lr_ref[...]
        lr = jnp.minimum(lam, LAMBDA_RE_MAX)
        li = li_ref[...]
        dt = jnp.exp(ldt_ref[...])
        mag = jnp.exp(lr * dt)
        cs = jnp.cos(li * dt)
        sn = jnp.sin(li * dt)
        ar = mag * cs
        ai = mag * sn
        den = lr * lr + li * li
        nr = (ar - 1.0) * lr + ai * li
        ni = ai * lr - (ar - 1.0) * li
        cr = nr / den
        ci = ni / den
        for i in range(nb):
            q = i % 4
            rows = slice(BLOCK_ROWS * i, BLOCK_ROWS * (i + 1))
            br = _embed_block(btr_ref[rows, :], q)
            bi = _embed_block(bti_ref[rows, :], q)
            gbbr = dbb_ref[i, :, :LANES]
            gbbi = dbb_ref[i, :, LANES:]
            cri, cii = cr[i:i + 1, :], ci[i:i + 1, :]
            gcr_s[i:i + 1, :] = jnp.sum(gbbr * br + gbbi * bi, axis=0, keepdims=True)
            gci_s[i:i + 1, :] = jnp.sum(gbbi * br - gbbr * bi, axis=0, keepdims=True)
            gbr_ref[rows, :] = _extract_block(cri * gbbr + cii * gbbi, q)
            gbi_ref[rows, :] = _extract_block(cri * gbbi - cii * gbbr, q)
            gcre_ref[rows, :] = _extract_block(dcc_ref[i, :LANES, :].T, q)
            gcim_ref[rows, :] = -_extract_block(dcc_ref[i, LANES:, :].T, q)
        g_cr = gcr_s[...]
        g_ci = gci_s[...]
        g_nr = g_cr / den
        g_ni = g_ci / den
        g_den = -(g_cr * nr + g_ci * ni) / (den * den)
        g_ar = dar_ref[...] + g_nr * lr - g_ni * li
        g_ai = dai_ref[...] + g_nr * li + g_ni * lr
        g_lr = g_nr * (ar - 1.0) + g_ni * ai + g_den * 2.0 * lr
        g_li = g_nr * ai - g_ni * (ar - 1.0) + g_den * 2.0 * li
        g_mag = g_ar * cs + g_ai * sn
        g_th = (g_ai * cs - g_ar * sn) * mag
        g_lr = g_lr + g_mag * mag * dt
        g_li = g_li + g_th * dt
        g_dt = g_mag * mag * lr + g_th * li
        glr_ref[...] = jnp.where(lam < LAMBDA_RE_MAX, g_lr, 0.0)
        gli_ref[...] = g_li
        gl = g_dt * dt
        half = LANES // 2
        gdt_ref[:, 0:1] = jnp.sum(gl[:, :half], axis=1, keepdims=True)
        gdt_ref[:, 1:2] = jnp.sum(gl[:, half:], axis=1, keepdims=True)

    rows_shape = jax.ShapeDtypeStruct((nb * BLOCK_ROWS, HALF_LANES), F32)
    return pl.pallas_call(
        body, name="ssm_prep_bwd",
        out_shape=[jax.ShapeDtypeStruct((nb, LANES), F32), jax.ShapeDtypeStruct((nb, LANES), F32),
                   jax.ShapeDtypeStruct((nb, 2), F32), rows_shape, rows_shape, rows_shape, rows_shape],
        scratch_shapes=[pltpu.VMEM((nb, LANES), F32), pltpu.VMEM((nb, LANES), F32)],
        compiler_params=_cparams(),
    )(lam_re, lam_im, log_dt, bt_re, bt_im, dar, dai, dbb, dcc)


def _cmul(ar, ai, br, bi):
    return ar * br - ai * bi, ar * bi + ai * br


def _interleave_rows(src_ref, dst_ref):
    def step(j, carry):
        dst_ref[pl.ds(pl.multiple_of(j * 8, 8), 8), :] = src_ref[pl.ds(j, 8, stride=SCAN_CHUNK), :]
        return carry
    lax.fori_loop(0, SCAN_CHUNK, step, 0, unroll=4)


def _deinterleave_rows(src_ref, dst_ref):
    def step(j, carry):
        dst_ref[pl.ds(j, 8, stride=SCAN_CHUNK), :] = src_ref[pl.ds(pl.multiple_of(j * 8, 8), 8), :]
        return carry
    lax.fori_loop(0, SCAN_CHUNK, step, 0, unroll=4)


def _scan_inplace(re_ref, im_ref, a_re, a_im, reverse):
    nq = len(a_re)
    ch = SCAN_CHUNK
    ab_re = [jnp.broadcast_to(a, (8, LANES)) for a in a_re]
    ab_im = [jnp.broadcast_to(a, (8, LANES)) for a in a_im]

    def rows(j):
        jj = (ch - 1 - j) if reverse else j
        return pl.ds(pl.multiple_of(jj * 8, 8), 8)

    def sweep(init, store):
        def step(j, st):
            out = []
            r = rows(j)
            for qi in range(nq):
                xr, xi = st[2 * qi], st[2 * qi + 1]
                pr, pi = _cmul(ab_re[qi], ab_im[qi], xr, xi)
                xr = pr + re_ref[qi, r, :]
                xi = pi + im_ref[qi, r, :]
                if store:
                    re_ref[qi, r, :] = xr
                    im_ref[qi, r, :] = xi
                out += [xr, xi]
            return tuple(out)
        return lax.fori_loop(0, ch, step, tuple(init), unroll=2)

    zeros = [jnp.zeros((8, LANES), F32)] * (2 * nq)
    finals = sweep(zeros, store=False)

    row_id = lax.broadcasted_iota(jnp.int32, (8, LANES), 0)
    carries = []
    for qi in range(nq):
        pr, pi = ab_re[qi], ab_im[qi]
        for _ in range(8):
            pr, pi = _cmul(pr, pi, pr, pi)
        fr, fi = finals[2 * qi], finals[2 * qi + 1]
        sr = jnp.zeros((8, LANES), F32)
        si = jnp.zeros((8, LANES), F32)
        for _ in range(7):
            tr, ti = _cmul(pr, pi, sr, si)
            tr, ti = tr + fr, ti + fi
            if reverse:
                sr = jnp.where(row_id == 7, 0.0, pltpu.roll(tr, 7, axis=0))
                si = jnp.where(row_id == 7, 0.0, pltpu.roll(ti, 7, axis=0))
            else:
                sr = jnp.where(row_id == 0, 0.0, pltpu.roll(tr, 1, axis=0))
                si = jnp.where(row_id == 0, 0.0, pltpu.roll(ti, 1, axis=0))
        carries += [sr, si]
    sweep(carries, store=True)


SSM_Q = 4


def _ssm_fwd(u, are, aim, bb, cc, dskip, after=None):
    nq = SSM_Q
    deps = [] if after is None else [after]

    def body(u_ref, ar_ref, ai_ref, bb_ref, cc_ref, d_ref, *rest):
        y_ref, xr_ref, xi_ref, sre, sim, up, yp = rest[len(deps):]
        _interleave_rows(u_ref, up)
        uf = up[...]
        ub = uf.astype(BF16)
        yp[...] = d_ref[...] * uf
        for d in range(2):
            for qi in range(nq):
                sre[qi] = _dot(ub, bb_ref[d, qi, :, :LANES])
                sim[qi] = _dot(ub, bb_ref[d, qi, :, LANES:])
            _scan_inplace(sre, sim, [ar_ref[d, qi] for qi in range(nq)], [ai_ref[d, qi] for qi in range(nq)],
                          reverse=(d == 1))
            for qi in range(nq):
                xrb = sre[qi].astype(BF16)
                xib = sim[qi].astype(BF16)
                xr_ref[d, qi] = xrb
                xi_ref[d, qi] = xib
                yp[...] += _dot(xrb, cc_ref[d, qi, :LANES, :]) + _dot(xib, cc_ref[d, qi, LANES:, :])
        _deinterleave_rows(yp, y_ref)

    blk4 = lambda k: (0, k, 0, 0)
    return pl.pallas_call(
        body, name="ssm_fwd", grid=(SSM_WIDTH // LANES,),
        in_specs=[pl.BlockSpec((SEQ, LANES), lambda k: (0, k)),
                  pl.BlockSpec((2, nq, 1, LANES), blk4), pl.BlockSpec((2, nq, 1, LANES), blk4),
                  pl.BlockSpec((2, nq, LANES, 2 * LANES), blk4), pl.BlockSpec((2, nq, 2 * LANES, LANES), blk4),
                  pl.BlockSpec((1, LANES), lambda k: (0, k))] + [pl.BlockSpec(memory_space=pl.ANY)] * len(deps),
        out_specs=[pl.BlockSpec((SEQ, LANES), lambda k: (0, k)),
                   pl.BlockSpec((2, nq, SEQ, LANES), blk4), pl.BlockSpec((2, nq, SEQ, LANES), blk4)],
        out_shape=[jax.ShapeDtypeStruct((SEQ, SSM_WIDTH), F32),
                   jax.ShapeDtypeStruct((2, N_LANE_BLOCKS, SEQ, LANES), BF16),
                   jax.ShapeDtypeStruct((2, N_LANE_BLOCKS, SEQ, LANES), BF16)],
        scratch_shapes=[pltpu.VMEM((nq, SEQ, LANES), F32), pltpu.VMEM((nq, SEQ, LANES), F32),
                        pltpu.VMEM((SEQ, LANES), F32), pltpu.VMEM((SEQ, LANES), F32)],
        compiler_params=_cparams(("parallel",)),
    )(u, are, aim, bb, cc, dskip, *deps)


def _ssm_bwd(dy, u, xr, xi, are, aim, bb, cc, dskip, after=None):
    nq = SSM_Q
    body_rows = SEQ - 8
    deps = [] if after is None else [after]

    def body(dy_ref, u_ref, xr_ref, xi_ref, ar_ref, ai_ref, bb_ref, cc_ref, d_ref, *rest):
        du_ref, dd_ref, dcc_ref, dbb_ref, dar_ref, dai_ref, sre, sim, up, dyp, dup = rest[len(deps):]
        _interleave_rows(u_ref, up)
        _interleave_rows(dy_ref, dyp)
        dyf = dyp[...]
        uf = up[...]
        dyb = dyf.astype(BF16)
        ub = uf.astype(BF16)
        dd_ref[...] = jnp.sum(dyf * uf, axis=0, keepdims=True)
        dup[...] = d_ref[...] * dyf
        row8 = lax.broadcasted_iota(jnp.int32, (8, LANES), 0)
        for d in range(2):
            for qi in range(nq):
                dx = _dot_nt(dyb, cc_ref[d, qi])
                sre[qi] = dx[:, :LANES]
                sim[qi] = dx[:, LANES:]
                dcc_ref[d, qi] = _dot_tn(jnp.concatenate([xr_ref[d, qi], xi_ref[d, qi]], axis=1), dyb)
            _scan_inplace(sre, sim, [ar_ref[d, qi] for qi in range(nq)], [-ai_ref[d, qi] for qi in range(nq)],
                          reverse=(d == 0))
            for qi in range(nq):
                gr = sre[qi]
                gi = sim[qi]
                xrf = xr_ref[d, qi].astype(F32)
                xif = xi_ref[d, qi].astype(F32)
                if d == 0:
                    g_main_r, g_main_i = gr[8:], gi[8:]
                    x_main_r, x_main_i = xrf[:body_rows], xif[:body_rows]
                    g_edge_r, g_edge_i = gr[:8], gi[:8]
                    x_edge_r = jnp.where(row8 == 0, 0.0, pltpu.roll(xrf[body_rows:], 1, axis=0))
                    x_edge_i = jnp.where(row8 == 0, 0.0, pltpu.roll(xif[body_rows:], 1, axis=0))
                else:
                    g_main_r, g_main_i = gr[:body_rows], gi[:body_rows]
                    x_main_r, x_main_i = xrf[8:], xif[8:]
                    g_edge_r, g_edge_i = gr[body_rows:], gi[body_rows:]
                    x_edge_r = jnp.where(row8 == 7, 0.0, pltpu.roll(xrf[:8], 7, axis=0))
                    x_edge_i = jnp.where(row8 == 7, 0.0, pltpu.roll(xif[:8], 7, axis=0))
                dar_ref[d, qi] = (jnp.sum(g_main_r * x_main_r + g_main_i * x_main_i, axis=0, keepdims=True)
                                  + jnp.sum(g_edge_r * x_edge_r + g_edge_i * x_edge_i, axis=0, keepdims=True))
                dai_ref[d, qi] = (jnp.sum(g_main_i * x_main_r - g_main_r * x_main_i, axis=0, keepdims=True)
                                  + jnp.sum(g_edge_i * x_edge_r - g_edge_r * x_edge_i, axis=0, keepdims=True))
                gb = jnp.concatenate([gr, gi], axis=1).astype(BF16)
                dup[...] += _dot_nt(gb, bb_ref[d, qi])
                dbb_ref[d, qi] = _dot_tn(ub, gb)
        _deinterleave_rows(dup, du_ref)

    blk4 = lambda k: (0, k, 0, 0)
    col = lambda k: (0, k)
    bb_spec = pl.BlockSpec((2, nq, LANES, 2 * LANES), blk4)
    cc_spec = pl.BlockSpec((2, nq, 2 * LANES, LANES), blk4)
    a_spec = pl.BlockSpec((2, nq, 1, LANES), blk4)
    x_spec = pl.BlockSpec((2, nq, SEQ, LANES), blk4)
    a_shape = jax.ShapeDtypeStruct((2, N_LANE_BLOCKS, 1, LANES), F32)
    return pl.pallas_call(
        body, name="ssm_bwd", grid=(SSM_WIDTH // LANES,),
        in_specs=[pl.BlockSpec((SEQ, LANES), col), pl.BlockSpec((SEQ, LANES), col), x_spec, x_spec,
                  a_spec, a_spec, bb_spec, cc_spec, pl.BlockSpec((1, LANES), col)]
        + [pl.BlockSpec(memory_space=pl.ANY)] * len(deps),
        out_specs=[pl.BlockSpec((SEQ, LANES), col), pl.BlockSpec((1, LANES), col),
                   cc_spec, bb_spec, a_spec, a_spec],
        out_shape=[jax.ShapeDtypeStruct((SEQ, SSM_WIDTH), F32), jax.ShapeDtypeStruct((1, SSM_WIDTH), F32),
                   jax.ShapeDtypeStruct((2, N_LANE_BLOCKS, 2 * LANES, LANES), F32),
                   jax.ShapeDtypeStruct((2, N_LANE_BLOCKS, LANES, 2 * LANES), F32), a_shape, a_shape],
        scratch_shapes=[pltpu.VMEM((nq, SEQ, LANES), F32), pltpu.VMEM((nq, SEQ, LANES), F32),
                        pltpu.VMEM((SEQ, LANES), F32), pltpu.VMEM((SEQ, LANES), F32), pltpu.VMEM((SEQ, LANES), F32)],
        compiler_params=_cparams(("parallel",)),
    )(dy, u, xr, xi, are, aim, bb, cc, dskip, *deps)


GELU_C = 0.7978845608028654
GELU_K = 0.044715


def _gelu(y):
    return 0.5 * y * (1.0 + jnp.tanh(GELU_C * (y + GELU_K * y * y * y)))


def _gelu_grad(y):
    t = jnp.tanh(GELU_C * (y + GELU_K * y * y * y))
    return 0.5 * (1.0 + t) + 0.5 * y * (1.0 - t * t) * GELU_C * (1.0 + 3.0 * GELU_K * y * y)


def _mixout_fwd(o, y, glu_w, glu_b, gan, gsn, wout, x1):
    tm = MIX_TM

    def body(o_ref, y_ref, gw_ref, gb_ref, gan_ref, gsn_ref, w_ref, x1_ref, x2_ref, mx_ref):
        yg = _gelu(y_ref[...])
        z = _dot(yg.astype(BF16), gw_ref[...]) + gb_ref[...]
        so = yg * _sigmoid(z)
        na = _rms_fwd(o_ref[...], gan_ref[...])
        ns = _rms_fwd(so, gsn_ref[...])
        mixed = jnp.concatenate([na, ns], axis=-1).astype(BF16)
        mx_ref[...] = mixed
        x2_ref[...] = x1_ref[...] + _dot(mixed, w_ref[...])

    row = lambda i: (i, 0)
    const = lambda i: (0, 0)
    return pl.pallas_call(
        body, name="mixout_fwd", grid=(SEQ // tm,),
        in_specs=[pl.BlockSpec((tm, ATTN_WIDTH), row), pl.BlockSpec((tm, SSM_WIDTH), row),
                  pl.BlockSpec((SSM_WIDTH, SSM_WIDTH), const), pl.BlockSpec((1, SSM_WIDTH), const),
                  pl.BlockSpec((1, ATTN_WIDTH), const), pl.BlockSpec((1, SSM_WIDTH), const),
                  pl.BlockSpec((D_MODEL, D_MODEL), const), pl.BlockSpec((tm, D_MODEL), row)],
        out_specs=[pl.BlockSpec((tm, D_MODEL), row), pl.BlockSpec((tm, D_MODEL), row)],
        out_shape=[jax.ShapeDtypeStruct((SEQ, D_MODEL), F32), jax.ShapeDtypeStruct((SEQ, D_MODEL), BF16)],
        compiler_params=_cparams(("parallel",)),
    )(o, y, glu_w, glu_b, gan, gsn, wout, x1)


def _mixout_bwd(dx2, o, y, glu_w, glu_b, gan, gsn, wout):
    tm = MIX_TM

    def body(dx2_ref, o_ref, y_ref, gw_ref, gb_ref, gan_ref, gsn_ref, w_ref,
             do_ref, dy_ref, dz_ref, yg_ref, dxb_ref, dgan_ref, dgsn_ref, dgb_ref):
        i = pl.program_id(0)
        dxb = dx2_ref[...].astype(BF16)
        dxb_ref[...] = dxb
        dmixed = _dot_nt(dxb, w_ref[...])
        do, dgan = _rms_bwd(dmixed[:, :ATTN_WIDTH], o_ref[...], gan_ref[...])
        do_ref[...] = do.T
        yv = y_ref[...]
        yg = _gelu(yv)
        ygb = yg.astype(BF16)
        yg_ref[...] = ygb
        sg = _sigmoid(_dot(ygb, gw_ref[...]) + gb_ref[...])
        dso, dgsn = _rms_bwd(dmixed[:, ATTN_WIDTH:], yg * sg, gsn_ref[...])
        dz = dso * yg * sg * (1.0 - sg)
        dzb = dz.astype(BF16)
        dz_ref[...] = dzb
        dyg = dso * sg + _dot_nt(dzb, gw_ref[...])
        dy_ref[...] = dyg * _gelu_grad(yv)
        dgb = jnp.sum(dz, axis=0, keepdims=True)

        @pl.when(i == 0)
        def _():
            dgan_ref[...] = dgan
            dgsn_ref[...] = dgsn
            dgb_ref[...] = dgb

        @pl.when(i != 0)
        def _():
            dgan_ref[...] += dgan
            dgsn_ref[...] += dgsn
            dgb_ref[...] += dgb

    row = lambda i: (i, 0)
    const = lambda i: (0, 0)
    return pl.pallas_call(
        body, name="mixout_bwd", grid=(SEQ // tm,),
        in_specs=[pl.BlockSpec((tm, D_MODEL), row), pl.BlockSpec((tm, ATTN_WIDTH), row),
                  pl.BlockSpec((tm, SSM_WIDTH), row),
                  pl.BlockSpec((SSM_WIDTH, SSM_WIDTH), const), pl.BlockSpec((1, SSM_WIDTH), const),
                  pl.BlockSpec((1, ATTN_WIDTH), const), pl.BlockSpec((1, SSM_WIDTH), const),
                  pl.BlockSpec((D_MODEL, D_MODEL), const)],
        out_specs=[pl.BlockSpec((ATTN_WIDTH, tm), lambda i: (0, i)), pl.BlockSpec((tm, SSM_WIDTH), row),
                   pl.BlockSpec((tm, SSM_WIDTH), row), pl.BlockSpec((tm, SSM_WIDTH), row),
                   pl.BlockSpec((tm, D_MODEL), row),
                   pl.BlockSpec((1, ATTN_WIDTH), const), pl.BlockSpec((1, SSM_WIDTH), const),
                   pl.BlockSpec((1, SSM_WIDTH), const)],
        out_shape=[jax.ShapeDtypeStruct((ATTN_WIDTH, SEQ), F32), jax.ShapeDtypeStruct((SEQ, SSM_WIDTH), F32),
                   jax.ShapeDtypeStruct((SEQ, SSM_WIDTH), BF16), jax.ShapeDtypeStruct((SEQ, SSM_WIDTH), BF16),
                   jax.ShapeDtypeStruct((SEQ, D_MODEL), BF16),
                   jax.ShapeDtypeStruct((1, ATTN_WIDTH), F32), jax.ShapeDtypeStruct((1, SSM_WIDTH), F32),
                   jax.ShapeDtypeStruct((1, SSM_WIDTH), F32)],
        compiler_params=_cparams(("arbitrary",)),
    )(dx2, o, y, glu_w, glu_b, gan, gsn, wout)


def _local_step(x, target, w, p, late_weights, early_grads, after=None, midway=None):
    x1, h1, a1, b1 = _ffn_fwd(x, p["norm_ffn1"], w["wgt1"], w["wut1"], w["wd1"], "ffn1_fwd", after=after)
    h2, q, k, v, u = _mixin_fwd(x1, p["norm_mix"], w["wint"])

    lam_re = p["ssm_lambda_re"].reshape(2 * N_LANE_BLOCKS, LANES)
    lam_im = p["ssm_lambda_im"].reshape(2 * N_LANE_BLOCKS, LANES)
    log_dt = jnp.repeat(p["ssm_log_dt"].reshape(2, 32), 64, axis=-1).reshape(2 * N_LANE_BLOCKS, LANES)
    a_re, a_im, bb, cc = _ssm_prep(lam_re, lam_im, log_dt, p["ssm_b_re"], p["ssm_b_im"],
                                   p["ssm_c_re"], p["ssm_c_im"])
    shape_a = (2, N_LANE_BLOCKS, 1, LANES)
    a_re4, a_im4 = a_re.reshape(shape_a), a_im.reshape(shape_a)
    bb4 = bb.reshape(2, N_LANE_BLOCKS, LANES, 2 * LANES)
    cc4 = cc.reshape(2, N_LANE_BLOCKS, 2 * LANES, LANES)
    dskip = p["ssm_d"].T.reshape(1, SSM_WIDTH)
    y, xr, xi = _ssm_fwd(u, a_re4, a_im4, bb4, cc4, dskip)
    o = _attn_fwd(q, k, v, p["attn_sinks"], after=None if midway is None else midway(y))

    w2 = late_weights(o)
    x2, mixed = _mixout_fwd(o, y, w2["glu"], p["ssm_glu_b"], p["attn_out_norm"], p["ssm_out_norm"], w2["wout"], x1)
    dx3, h3, a3, b3, loss, d_final = _ffn_fwd(x2, p["norm_ffn2"], w2["wgt2"], w2["wut2"], w2["wd2"], "ffn2_fwd",
                                              head=(p["final_norm"], target))
    dx2, da3, db3, s3, df3, d_n2 = _ffn_bwd_act(dx3, x2, p["norm_ffn2"], a3, b3, w2["wgt2"], w2["wut2"], w2["wd2"],
                                                "ffn2_bwd_act")
    g_wgt2, g_wut2, g_wd2 = _mm_tn([(da3, h3), (db3, h3), (s3, df3)], "ffn2_bwd_w")

    do, dy, dz, ygb, dx2b, d_gan, d_gsn, d_glub = _mixout_bwd(
        dx2, o, y, w2["glu"], p["ssm_glu_b"], p["attn_out_norm"], p["ssm_out_norm"], w2["wout"])
    (g_wout,) = _mm_tn([(mixed, dx2b)], "wout_bwd_w")
    (g_glu,) = _mm_tn([(ygb, dz)], "glu_bwd_w")
    sent = early_grads(dict(glu=g_glu, wout=g_wout, wgt2=g_wgt2, wut2=g_wut2, wd2=g_wd2))

    du, d_dskip, dcc, dbb, dar, dai = _ssm_bwd(dy, u, xr, xi, a_re4, a_im4, bb4, cc4, dskip, after=sent)
    nb = 2 * N_LANE_BLOCKS
    g_lre, g_lim, g_ldt, g_btr, g_bti, g_cre, g_cim = _ssm_prep_bwd(
        lam_re, lam_im, log_dt, p["ssm_b_re"], p["ssm_b_im"], dar.reshape(nb, LANES), dai.reshape(nb, LANES),
        dbb.reshape(nb, LANES, 2 * LANES), dcc.reshape(nb, 2 * LANES, LANES))

    dq, dk, dv, d_sinks = _attn_bwd(q, k, v, p["attn_sinks"], do)
    dx1, dproj, d_nmix = _mixin_bwd(dq, dk, dv, du, w["wint"], x1, p["norm_mix"], dx2)
    (g_wint,) = _mm_tn([(dproj, h2)], "win_bwd_w")

    dx0, da1, db1, s1, df1, d_n1 = _ffn_bwd_act(dx1, x, p["norm_ffn1"], a1, b1, w["wgt1"], w["wut1"], w["wd1"],
                                                "ffn1_bwd_act")
    g_wgt1, g_wut1, g_wd1 = _mm_tn([(da1, h1), (db1, h1), (s1, df1)], "ffn1_bwd_w")

    big = dict(wgt1=g_wgt1, wut1=g_wut1, wd1=g_wd1, wint=g_wint)
    small = dict(
        norm_ffn1=d_n1, norm_mix=d_nmix, attn_sinks=d_sinks,
        ssm_lambda_re=g_lre.reshape(64, 64), ssm_lambda_im=g_lim.reshape(64, 64),
        ssm_log_dt=g_ldt.reshape(2, 32), ssm_b_re=g_btr, ssm_b_im=g_bti, ssm_c_re=g_cre, ssm_c_im=g_cim,
        ssm_d=d_dskip.reshape(32, 16).T, ssm_glu_b=d_glub, attn_out_norm=d_gan, ssm_out_norm=d_gsn,
        norm_ffn2=d_n2, final_norm=d_final, loss=loss)
    return loss, dx0, big, small


BIG = dict(
    wgt1=("ffn1_w_gate", 352, 1024, True), wut1=("ffn1_w_up", 352, 1024, True), wd1=("ffn1_w_down", 352, 1024, False),
    wint=("w_in", 160, 1024, True), glu=("ssm_glu_w", 64, 512, False), wout=("w_out", 128, 1024, False),
    wgt2=("ffn2_w_gate", 352, 1024, True), wut2=("ffn2_w_up", 352, 1024, True), wd2=("ffn2_w_down", 352, 1024, False))

SMALL = dict(
    norm_ffn1=(1, 1024), norm_mix=(1, 1024), attn_sinks=(1, 8), ssm_lambda_re=(64, 64), ssm_lambda_im=(64, 64),
    ssm_log_dt=(2, 32), ssm_b_re=(1024, 64), ssm_b_im=(1024, 64), ssm_c_re=(1024, 64), ssm_c_im=(1024, 64),
    ssm_d=(16, 32), ssm_glu_b=(1, 512), attn_out_norm=(1, 512), ssm_out_norm=(1, 512), norm_ffn2=(1, 1024),
    final_norm=(1, 1024), loss=(1, 128))
SMALL_TRANSPOSED = ("ssm_b_re", "ssm_b_im", "ssm_d")
SMALL_PARAMS = tuple(n for n in SMALL if n != "loss")

SMALL_PAIRS = (("ssm_lambda_re", "ssm_lambda_im"), ("ssm_c_re", "ssm_c_im"), ("ssm_b_re", "ssm_b_im"))
SMALL_VECS = ("norm_ffn1", "norm_mix", "norm_ffn2", "final_norm", "ssm_glu_b", "attn_out_norm", "ssm_out_norm")
SMALL_TILES = ("ssm_log_dt", "attn_sinks", "ssm_d", "loss")


def _small_offsets():
    off, table = 0, {}
    for re, im in SMALL_PAIRS:
        table[re] = table[im] = off
        off += SMALL[re][0]
    for n in SMALL_VECS:
        table[n] = off
        off += SMALL[n][1] // LANES
    for n in SMALL_TILES:
        off = -(-off // 8) * 8
        table[n] = off
        off += SMALL[n][0]
    return table, off


SMALL_OFFSET, SMALL_USED_ROWS = _small_offsets()
SMALL_ROWS = -(-SMALL_USED_ROWS // (8 * N_DEV)) * 8 * N_DEV


def _cast_shards(shards):
    names = list(BIG)

    def body(*refs):
        ins, outs = refs[:len(names)], refs[len(names):]
        for idx in range(len(names)):
            outs[idx][...] = ins[idx][...].astype(BF16)

    return pl.pallas_call(
        body, name="cast_shards",
        out_shape=[jax.ShapeDtypeStruct((BIG[n][1], BIG[n][2]), BF16) for n in names],
        compiler_params=_cparams(),
    )(*[shards[n] for n in names])


def _peer(x, y, c, r):
    px = 1 - x if r & 4 else x
    py = 1 - y if r & 2 else y
    pc = 1 - c if r & 1 else c
    return px, py, pc


FIRST_GROUP = ("wgt1", "wut1", "wd1", "wint")
LATE_GROUP = ("glu", "wout", "wgt2", "wut2", "wd2")
N_PEERS = N_DEV - 1
ANY_SPEC = pl.BlockSpec(memory_space=pl.ANY)
HBM_SPEC = pl.BlockSpec(memory_space=pltpu.HBM)
SEM_SPEC = pl.BlockSpec(memory_space=pltpu.SEMAPHORE)
DATAFLOW_EFFECT = pltpu.SideEffectType.DATAFLOW_SIDE_EFFECTING


def _mesh_pos():
    x, y, c = lax.axis_index("x"), lax.axis_index("y"), lax.axis_index("c")
    return x, y, c, 4 * x + 2 * y + c


def _gather_first(first, late):
    nf, nl = len(first), len(late)

    def body(*refs):
        f_in, l_in = refs[:nf], refs[nf:nf + nl]
        f_out, l_out = refs[nf + nl:2 * nf + nl], refs[2 * nf + nl:2 * (nf + nl)]
        send_sems, recv_sems, local_sems = refs[2 * (nf + nl):]
        x, y, c, me = _mesh_pos()
        sibling = (x, y, 1 - c)
        chips = [(x, 1 - y), (1 - x, y), (1 - x, 1 - y)]

        def idx(px, py, pc):
            return 4 * px + 2 * py + pc

        def copy(k, s, block, to, src=None):
            slot = f_out[k].at[block]
            return pltpu.make_async_remote_copy(
                src_ref=slot if src is None else src, dst_ref=slot, send_sem=send_sems.at[k, s],
                recv_sem=recv_sems.at[k, s], device_id=to, device_id_type=MESH_ID)

        local = []
        for k in range(nf + nl):
            src, dst = (f_in[k], f_out[k]) if k < nf else (l_in[k - nf], l_out[k - nf])
            mine = pltpu.make_async_copy(src, dst.at[me], local_sems.at[k])
            mine.start()
            local.append(mine)
        sends = []
        for j, chip in enumerate(chips):
            for k in range(nf):
                sends.append(copy(k, 1 + j, me, (*chip, c), src=f_in[k]))
                sends[-1].start()
        for k in range(nf):
            sends.append(copy(k, 0, me, sibling, src=f_in[k]))
            sends[-1].start()
        for j, chip in enumerate(chips):
            for k in range(nf):
                copy(k, 1 + j, idx(*chip, c), (*chip, c)).wait_recv()
                sends.append(copy(k, 4 + j, idx(*chip, c), sibling))
                sends[-1].start()
        for k in range(nf):
            copy(k, 0, idx(*sibling), sibling).wait_recv()
        for j, chip in enumerate(chips):
            for k in range(nf):
                copy(k, 4 + j, idx(*chip, 1 - c), sibling).wait_recv()
        for cp in sends:
            cp.wait_send()
        for cp in local:
            cp.wait()

    return pl.pallas_call(
        body, name="gather_first",
        in_specs=[ANY_SPEC] * (nf + nl), out_specs=[ANY_SPEC] * (nf + nl),
        out_shape=[jax.ShapeDtypeStruct((N_DEV,) + s.shape, s.dtype) for s in list(first) + list(late)],
        scratch_shapes=[pltpu.SemaphoreType.DMA((nf, N_PEERS)), pltpu.SemaphoreType.DMA((nf, N_PEERS)),
                        pltpu.SemaphoreType.DMA((nf + nl,))],
        compiler_params=pltpu.CompilerParams(has_side_effects=True),
    )(*first, *late)


def _split_copy(src_refs, land_refs, send_sems, recv_sems, k, r, pos, scatter, receiving):
    x, y, c, me = pos
    px, py, pc = _peer(x, y, c, r)
    peer_idx = 4 * px + 2 * py + pc
    if scatter:
        src, dst = src_refs[k].at[peer_idx], land_refs[k].at[r - 1]
    else:
        src, dst = src_refs[k], land_refs[k].at[peer_idx if receiving else me]
    return pltpu.make_async_remote_copy(
        src_ref=src, dst_ref=dst, send_sem=send_sems.at[k * N_PEERS + r - 1],
        recv_sem=recv_sems.at[k * N_PEERS + r - 1], device_id=(px, py, pc), device_id_type=MESH_ID)


def _split_start(name, srcs, lands, scatter):
    n = len(srcs)

    def body(*refs):
        src_refs, land_refs = refs[:n], refs[n:2 * n]
        send_sems, recv_sems = refs[2 * n], refs[2 * n + 1]
        token = refs[-1]
        pos = _mesh_pos()
        for k in range(n):
            for r in range(1, N_DEV):
                _split_copy(src_refs, land_refs, send_sems, recv_sems, k, r, pos, scatter, False).start()
        token[...] = jnp.zeros_like(token)

    thru = [pltpu.HBM(a.shape, a.dtype) for a in list(srcs) + list(lands)]
    outs = pl.pallas_call(
        body, name=name,
        in_specs=[HBM_SPEC] * (2 * n),
        out_specs=[SEM_SPEC, SEM_SPEC] + [HBM_SPEC] * (2 * n) + [pl.BlockSpec(memory_space=pltpu.VMEM)],
        out_shape=[pltpu.SemaphoreType.DMA((n * N_PEERS,)), pltpu.SemaphoreType.DMA((n * N_PEERS,))] + thru
        + [jax.ShapeDtypeStruct((8, LANES), F32)],
        input_output_aliases={i: 2 + i for i in range(2 * n)},
        compiler_params=pltpu.CompilerParams(has_side_effects=DATAFLOW_EFFECT),
    )(*[pltpu.with_memory_space_constraint(a, pltpu.HBM) for a in list(srcs) + list(lands)])
    return outs[0], outs[1], outs[2:2 + n], outs[2 + n:2 + 2 * n], outs[-1]


def _split_wait(name, send_sems, recv_sems, srcs, lands, scatter, after):
    n = len(srcs)

    def body(*refs):
        src_refs, land_refs = refs[:n], refs[n:2 * n]
        send, recv = refs[2 * n], refs[2 * n + 1]
        pos = _mesh_pos()
        for k in range(n):
            for r in range(1, N_DEV):
                cp = _split_copy(src_refs, land_refs, send, recv, k, r, pos, scatter, True)
                cp.wait_send()
                cp.wait_recv()

    thru = [pltpu.HBM(a.shape, a.dtype) for a in list(srcs) + list(lands)]
    outs = pl.pallas_call(
        body, name=name,
        in_specs=[HBM_SPEC] * (2 * n) + [SEM_SPEC, SEM_SPEC, ANY_SPEC],
        out_specs=[HBM_SPEC] * (2 * n), out_shape=thru,
        input_output_aliases={i: i for i in range(2 * n)},
        compiler_params=pltpu.CompilerParams(has_side_effects=DATAFLOW_EFFECT),
    )(*srcs, *lands, send_sems, recv_sems, after)
    return outs[:n], outs[n:]


def _late_copy(passing, src_refs, land_refs, send_sems, recv_sems, k, s, pos, receiving):
    x, y, c, me = pos
    chips = [(x, 1 - y), (1 - x, y), (1 - x, 1 - y)]
    sibling = (x, y, 1 - c)

    def idx(dev):
        return 4 * dev[0] + 2 * dev[1] + dev[2]

    if passing:
        to = sibling
        block = idx((*chips[s], 1 - c)) if receiving else idx((*chips[s], c))
        src = dst = land_refs[k].at[block]
        sem = k * 3 + s
    else:
        to = sibling if s == 0 else (*chips[s - 1], c)
        src, dst = src_refs[k], land_refs[k].at[idx(to) if receiving else me]
        sem = k * 4 + s
    return pltpu.make_async_remote_copy(src_ref=src, dst_ref=dst, send_sem=send_sems.at[sem],
                                        recv_sem=recv_sems.at[sem], device_id=to, device_id_type=MESH_ID)


def _late_gather_call(name, stage, srcs, lands, sems, after=None):
    n = len(srcs)
    n_sem_in = len(sems)
    has_after = after is not None

    def body(*refs):
        src_refs, land_refs = refs[:n], refs[n:2 * n]
        sem_in = refs[2 * n:2 * n + n_sem_in]
        outs = refs[2 * n + n_sem_in + (1 if has_after else 0):]
        pos = _mesh_pos()
        if stage == 0:
            own_send, own_recv = outs[0], outs[1]
            for s in (1, 2, 3, 0):
                for k in range(n):
                    _late_copy(False, src_refs, land_refs, own_send, own_recv, k, s, pos, False).start()
            outs[-1][...] = jnp.zeros_like(outs[-1])
        elif stage == 1:
            own_recv = sem_in[1]
            pass_send, pass_recv = outs[0], outs[1]
            for s in range(3):
                for k in range(n):
                    _late_copy(False, src_refs, land_refs, sem_in[0], own_recv, k, s + 1, pos, True).wait_recv()
                    _late_copy(True, src_refs, land_refs, pass_send, pass_recv, k, s, pos, False).start()
            outs[-1][...] = jnp.zeros_like(outs[-1])
        else:
            own_send, own_recv, pass_send, pass_recv = sem_in
            for k in range(n):
                _late_copy(False, src_refs, land_refs, own_send, own_recv, k, 0, pos, True).wait_recv()
                for s in range(4):
                    _late_copy(False, src_refs, land_refs, own_send, own_recv, k, s, pos, False).wait_send()
                for s in range(3):
                    cp = _late_copy(True, src_refs, land_refs, pass_send, pass_recv, k, s, pos, True)
                    cp.wait_recv()
                    cp.wait_send()

    thru = [pltpu.HBM(a.shape, a.dtype) for a in list(srcs) + list(lands)]
    new_sems = [[pltpu.SemaphoreType.DMA((n * 4,))] * 2, [pltpu.SemaphoreType.DMA((n * 3,))] * 2, []][stage]
    extra = [] if stage == 2 else [jax.ShapeDtypeStruct((8, LANES), F32)]
    outs = pl.pallas_call(
        body, name=name,
        in_specs=[HBM_SPEC] * (2 * n) + [SEM_SPEC] * n_sem_in + [ANY_SPEC] * has_after,
        out_specs=[SEM_SPEC] * len(new_sems) + [HBM_SPEC] * (2 * n) + [pl.BlockSpec(memory_space=pltpu.VMEM)] * len(extra),
        out_shape=new_sems + thru + extra,
        input_output_aliases={i: len(new_sems) + i for i in range(2 * n)},
        compiler_params=pltpu.CompilerParams(has_side_effects=DATAFLOW_EFFECT),
    )(*[pltpu.with_memory_space_constraint(a, pltpu.HBM) for a in list(srcs) + list(lands)], *sems,
      *([after] if has_after else []))
    ns = len(new_sems)
    return list(outs[:ns]), outs[ns:ns + n], outs[ns + n:ns + 2 * n], (outs[-1] if extra else None)


N_SEND_SLOTS = 3


def _exchange_last(grads, small_packed):
    ng = len(grads)
    ch = SMALL_ROWS // N_DEV
    max_rows = max(g.shape[1] for g in grads)
    cols = grads[0].shape[2]

    def body(*refs):
        g_in, s_in = refs[:ng], refs[ng]
        outs = refs[ng + 1:]
        own_out, land, stage = outs[:ng], outs[ng:2 * ng], outs[2 * ng:3 * ng]
        s_red, s_stage = outs[3 * ng], outs[3 * ng + 1]
        (va, vb, vo, vs, sm_in, sm_out, d2d_send, d2d_recv, ici_send, ici_recv, s1_send, s1_recv, s2_send, s2_recv,
         local_sems) = outs[3 * ng + 2:]
        x, y, c, me = _mesh_pos()
        sibling = (x, y, 1 - c)
        chips = [(x, y), (x, 1 - y), (1 - x, y), (1 - x, 1 - y)]

        def idx(chip, core):
            return 4 * chip[0] + 2 * chip[1] + core

        def d2d(k, j):
            return pltpu.make_async_remote_copy(
                src_ref=g_in[k].at[idx(chips[j], 1 - c)], dst_ref=stage[k].at[j], send_sem=d2d_send.at[k, j],
                recv_sem=d2d_recv.at[k, j], device_id=sibling, device_id_type=MESH_ID)

        def ici(k, j, slot):
            rows = g_in[k].shape[1]
            return pltpu.make_async_remote_copy(
                src_ref=vo.at[slot, pl.ds(0, rows)], dst_ref=land[k].at[j - 1], send_sem=ici_send.at[k, j - 1],
                recv_sem=ici_recv.at[k, j - 1], device_id=(*chips[j], c), device_id_type=MESH_ID)

        def small_scatter(r):
            px, py, pc = _peer(x, y, c, r)
            return pltpu.make_async_remote_copy(
                src_ref=s_in.at[pl.ds(pl.multiple_of((4 * px + 2 * py + pc) * ch, 8), ch)], dst_ref=s_stage.at[me],
                send_sem=s1_send.at[r - 1], recv_sem=s1_recv.at[r - 1], device_id=(px, py, pc), device_id_type=MESH_ID)

        def small_gather(r):
            return pltpu.make_async_remote_copy(
                src_ref=sm_out, dst_ref=s_red.at[me], send_sem=s2_send.at[r - 1], recv_sem=s2_recv.at[r - 1],
                device_id=_peer(x, y, c, r), device_id_type=MESH_ID)

        for r in range(1, N_DEV):
            small_scatter(r).start()
        mine = pltpu.make_async_copy(s_in.at[pl.ds(pl.multiple_of(me * ch, 8), ch)], s_stage.at[me], local_sems.at[0])
        mine.start()
        pairs = [(k, j) for k in range(ng) for j in (1, 2, 3)] + [(k, 0) for k in range(ng)]
        for k, j in pairs:
            d2d(k, j).start()

        def reduce_small():
            for r in range(1, N_DEV):
                small_scatter(r).wait_recv()
            mine.wait()
            load = pltpu.make_async_copy(s_stage, sm_in, local_sems.at[1])
            load.start()
            load.wait()
            total = sm_in[0]
            for i in range(1, N_DEV):
                total = total + sm_in[i]
            sm_out[...] = total
            for r in range(1, N_DEV):
                small_gather(r).start()
            keep = pltpu.make_async_copy(sm_out, s_red.at[me], local_sems.at[2])
            keep.start()
            return keep

        in_flight = {}
        for i, (k, j) in enumerate(pairs):
            if i == N_SEND_SLOTS:
                keep = reduce_small()
            slot = i % N_SEND_SLOTS
            rows = g_in[k].shape[1]
            if slot in in_flight:
                in_flight.pop(slot).wait_send()
            d2d(k, j).wait_recv()
            la = pltpu.make_async_copy(g_in[k].at[idx(chips[j], c)], va.at[pl.ds(0, rows)], local_sems.at[3])
            lb = pltpu.make_async_copy(stage[k].at[j], vb.at[pl.ds(0, rows)], local_sems.at[4])
            la.start()
            lb.start()
            la.wait()
            lb.wait()
            total = va[pl.ds(0, rows)].astype(F32) + vb[pl.ds(0, rows)].astype(F32)
            if j == 0:
                vs[pl.ds(0, rows)] = total
                st = pltpu.make_async_copy(vs.at[pl.ds(0, rows)], own_out[k], local_sems.at[5])
                st.start()
                st.wait()
            else:
                vo[slot, pl.ds(0, rows)] = total.astype(BF16)
                cp = ici(k, j, slot)
                cp.start()
                in_flight[slot] = cp
        for cp in in_flight.values():
            cp.wait_send()

        for j in (1, 2, 3, 0):
            for k in range(ng):
                d2d(k, j).wait_send()
        for j in (1, 2, 3):
            for k in range(ng):
                ici(k, j, 0).wait_recv()
        for r in range(1, N_DEV):
            small_scatter(r).wait_send()
            small_gather(r).wait_send()
            small_gather(r).wait_recv()
        keep.wait()

    out_shape = [jax.ShapeDtypeStruct(g.shape[1:], F32) for g in grads]
    out_shape += [jax.ShapeDtypeStruct((3,) + g.shape[1:], BF16) for g in grads]
    out_shape += [jax.ShapeDtypeStruct((4,) + g.shape[1:], BF16) for g in grads]
    out_shape += [jax.ShapeDtypeStruct((N_DEV, ch, LANES), F32), jax.ShapeDtypeStruct((N_DEV, ch, LANES), F32)]
    outs = pl.pallas_call(
        body, name="exchange_last",
        in_specs=[ANY_SPEC] * (ng + 1), out_specs=[ANY_SPEC] * len(out_shape), out_shape=out_shape,
        scratch_shapes=[pltpu.VMEM((max_rows, cols), BF16), pltpu.VMEM((max_rows, cols), BF16),
                        pltpu.VMEM((N_SEND_SLOTS, max_rows, cols), BF16), pltpu.VMEM((max_rows, cols), F32),
                        pltpu.VMEM((N_DEV, ch, LANES), F32), pltpu.VMEM((ch, LANES), F32),
                        pltpu.SemaphoreType.DMA((ng, 4)), pltpu.SemaphoreType.DMA((ng, 4)),
                        pltpu.SemaphoreType.DMA((ng, 3)), pltpu.SemaphoreType.DMA((ng, 3)),
                        pltpu.SemaphoreType.DMA((N_PEERS,)), pltpu.SemaphoreType.DMA((N_PEERS,)),
                        pltpu.SemaphoreType.DMA((N_PEERS,)), pltpu.SemaphoreType.DMA((N_PEERS,)),
                        pltpu.SemaphoreType.DMA((6,))],
        compiler_params=pltpu.CompilerParams(has_side_effects=True, vmem_limit_bytes=VMEM_LIMIT),
    )(*grads, small_packed)
    return outs[:ng], outs[ng:2 * ng], outs[3 * ng].reshape(SMALL_ROWS, LANES)


def _adamw_math(w, g, m, v):
    m2 = ADAM_B1 * m + (1.0 - ADAM_B1) * g
    v2 = ADAM_B2 * v + (1.0 - ADAM_B2) * (g * g)
    m_hat = m2 / (1.0 - ADAM_B1 ** ADAM_STEP)
    v_hat = v2 / (1.0 - ADAM_B2 ** ADAM_STEP)
    delta = -ADAM_LR * (m_hat / (jnp.sqrt(v_hat) + ADAM_EPS) + ADAM_WD * w)
    return delta, m2, v2


ADAM_ROW_TILES = 2


def _adamw_big(own, parts, w, m, v, name):
    shape = w.shape
    own_is_blocks = own.ndim == 3
    tr = shape[0] // ADAM_ROW_TILES
    n_parts = parts.shape[0]

    def body(me_ref, own_ref, p_ref, w_ref, m_ref, v_ref, g_ref, d_ref, m2_ref, v2_ref):
        g = own_ref[...].astype(F32)
        for i in range(n_parts):
            g = g + p_ref[i].astype(F32)
        delta, m2, v2 = _adamw_math(w_ref[...], g, m_ref[...], v_ref[...])
        g_ref[...] = g
        d_ref[...] = delta
        m2_ref[...] = m2
        v2_ref[...] = v2

    me = (4 * lax.axis_index("x") + 2 * lax.axis_index("y") + lax.axis_index("c")).astype(jnp.int32).reshape(1)
    tile = pl.BlockSpec((tr, shape[1]), lambda i, me_ref: (i, 0))
    if own_is_blocks:
        own_spec = pl.BlockSpec((None, tr, shape[1]), lambda i, me_ref: (me_ref[0], i, 0))
    else:
        own_spec = tile
    grid_spec = pltpu.PrefetchScalarGridSpec(
        num_scalar_prefetch=1, grid=(ADAM_ROW_TILES,),
        in_specs=[own_spec, pl.BlockSpec((n_parts, tr, shape[1]), lambda i, me_ref: (0, i, 0)), tile, tile, tile],
        out_specs=[tile] * 4)
    return pl.pallas_call(
        body, name=name, grid_spec=grid_spec, out_shape=[jax.ShapeDtypeStruct(shape, F32)] * 4,
        compiler_params=_cparams(("arbitrary",)),
    )(me, own, parts, w, m, v)


def _pack_small(grads):
    names = list(SMALL)

    def body(*refs):
        ins, out = dict(zip(names, refs[:-1])), refs[-1]
        out[...] = jnp.zeros_like(out)
        for re, im in SMALL_PAIRS:
            off, rows = SMALL_OFFSET[re], SMALL[re][0]
            out[off:off + rows, :] = jnp.concatenate([ins[re][...], ins[im][...]], axis=1)
        for n in SMALL_VECS:
            off, vec = SMALL_OFFSET[n], ins[n][...]
            for i in range(SMALL[n][1] // LANES):
                out[off + i:off + i + 1, :] = vec[:, i * LANES:(i + 1) * LANES]
        for n in SMALL_TILES:
            off, (rows, cols) = SMALL_OFFSET[n], SMALL[n]
            out[off:off + rows, 0:cols] = ins[n][...]

    return pl.pallas_call(
        body, name="pack_small", out_shape=jax.ShapeDtypeStruct((SMALL_ROWS, LANES), F32),
        compiler_params=_cparams(),
    )(*[grads[n] for n in names])


def _unpack_small_ref(g_ref, n):
    off, (rows, cols) = SMALL_OFFSET[n], SMALL[n]
    for re, im in SMALL_PAIRS:
        if n == re:
            return g_ref[off:off + rows, 0:HALF_LANES]
        if n == im:
            return g_ref[off:off + rows, HALF_LANES:LANES]
    if n in SMALL_VECS:
        return jnp.concatenate([g_ref[off + i:off + i + 1, :] for i in range(cols // LANES)], axis=1)
    return g_ref[off:off + rows, 0:cols]


def _adamw_small(g_packed, w, m, v):
    names = list(SMALL_PARAMS)
    n = len(names)

    def body(g_ref, *refs):
        w_refs, m_refs, v_refs, outs = refs[:n], refs[n:2 * n], refs[2 * n:3 * n], refs[3 * n:]
        for idx, name in enumerate(names):
            g = _unpack_small_ref(g_ref, name)
            delta, m2, v2 = _adamw_math(w_refs[idx][...], g, m_refs[idx][...], v_refs[idx][...])
            outs[4 * idx][...] = g
            outs[4 * idx + 1][...] = delta
            outs[4 * idx + 2][...] = m2
            outs[4 * idx + 3][...] = v2
        outs[4 * n][...] = _unpack_small_ref(g_ref, "loss")

    outs = pl.pallas_call(
        body, name="adamw_small",
        out_shape=[jax.ShapeDtypeStruct(SMALL[name], F32) for name in names for _ in range(4)]
        + [jax.ShapeDtypeStruct(SMALL["loss"], F32)],
        compiler_params=_cparams(),
    )(g_packed, *[w[k] for k in names], *[m[k] for k in names], *[v[k] for k in names])
    return {name: outs[4 * idx:4 * idx + 4] for idx, name in enumerate(names)}, outs[4 * n]


WEIGHT_NAMES = ['norm_ffn1', 'ffn1_w_gate', 'ffn1_w_up', 'ffn1_w_down', 'norm_mix', 'w_in', 'attn_sinks',
                'ssm_lambda_re', 'ssm_lambda_im', 'ssm_log_dt', 'ssm_b_re', 'ssm_b_im', 'ssm_c_re', 'ssm_c_im',
                'ssm_d', 'ssm_glu_w', 'ssm_glu_b', 'attn_out_norm', 'ssm_out_norm', 'w_out', 'norm_ffn2',
                'ffn2_w_gate', 'ffn2_w_up', 'ffn2_w_down', 'final_norm']


def kernel(x, norm_ffn1, ffn1_w_gate, ffn1_w_up, ffn1_w_down, norm_mix, w_in, attn_sinks, ssm_lambda_re, ssm_lambda_im, ssm_log_dt, ssm_b_re, ssm_b_im, ssm_c_re, ssm_c_im, ssm_d, ssm_glu_w, ssm_glu_b, attn_out_norm, ssm_out_norm, w_out, norm_ffn2, ffn2_w_gate, ffn2_w_up, ffn2_w_down, final_norm, loss_target, m_norm_ffn1, m_ffn1_w_gate, m_ffn1_w_up, m_ffn1_w_down, m_norm_mix, m_w_in, m_attn_sinks, m_ssm_lambda_re, m_ssm_lambda_im, m_ssm_log_dt, m_ssm_b_re, m_ssm_b_im, m_ssm_c_re, m_ssm_c_im, m_ssm_d, m_ssm_glu_w, m_ssm_glu_b, m_attn_out_norm, m_ssm_out_norm, m_w_out, m_norm_ffn2, m_ffn2_w_gate, m_ffn2_w_up, m_ffn2_w_down, m_final_norm, v_norm_ffn1, v_ffn1_w_gate, v_ffn1_w_up, v_ffn1_w_down, v_norm_mix, v_w_in, v_attn_sinks, v_ssm_lambda_re, v_ssm_lambda_im, v_ssm_log_dt, v_ssm_b_re, v_ssm_b_im, v_ssm_c_re, v_ssm_c_im, v_ssm_d, v_ssm_glu_w, v_ssm_glu_b, v_attn_out_norm, v_ssm_out_norm, v_w_out, v_norm_ffn2, v_ffn2_w_gate, v_ffn2_w_up, v_ffn2_w_down, v_final_norm):
    args = dict(locals())
    weights = {n: args[n] for n in WEIGHT_NAMES}
    moms = {n: args["m_" + n] for n in WEIGHT_NAMES}
    vars_ = {n: args["v_" + n] for n in WEIGHT_NAMES}

    def shard2d(a, k):
        a = a.reshape(a.shape[-2], a.shape[-1])
        return a.T if BIG[k][3] else a

    def shard_master(a, k):
        return (a.T if BIG[k][3] else a).reshape(weights[BIG[k][0]].shape)

    def blocks(g, k):
        return g.reshape(N_DEV, BIG[k][1], BIG[k][2])

    def full(g, k):
        return g.reshape(N_DEV * BIG[k][1], BIG[k][2])

    shards = dict(zip(BIG, _cast_shards({k: shard2d(weights[BIG[k][0]], k) for k in BIG})))
    nf = len(FIRST_GROUP)
    got = _gather_first([shards[k] for k in FIRST_GROUP], [shards[k] for k in LATE_GROUP])
    w_first = {k: full(g, k) for k, g in zip(FIRST_GROUP, got[:nf])}
    late = {}
    late["own_sems"], late["srcs"], late["lands"], w_token = _late_gather_call(
        "gather_late_start", 0, [shards[k] for k in LATE_GROUP], got[nf:], [])

    def late_pass(dep):
        late["pass_sems"], late["srcs"], late["lands"], token = _late_gather_call(
            "gather_late_pass", 1, late["srcs"], late["lands"], late["own_sems"], after=dep)
        return token

    def late_weights(dep):
        _, _, lands, _ = _late_gather_call("gather_late_wait", 2, late["srcs"], late["lands"],
                                           late["own_sems"] + late["pass_sems"], after=dep)
        return {k: full(g, k) for k, g in zip(LATE_GROUP, lands)}

    early = {}

    def early_grads(g):
        srcs = [blocks(g[k], k) for k in LATE_GROUP]
        lands = [lax.empty((N_PEERS, BIG[k][1], BIG[k][2]), BF16) for k in LATE_GROUP]
        early["send"], early["recv"], early["srcs"], early["lands"], token = _split_start(
            "grads_late_start", srcs, lands, scatter=True)
        return token

    def small2d(a, n):
        if n in SMALL_TRANSPOSED:
            a = jnp.swapaxes(a, -1, -2)
        return a.reshape(SMALL[n])

    def small_master(a, n):
        if n in SMALL_TRANSPOSED:
            shape = weights[n].shape
            return jnp.swapaxes(a.reshape(shape[:-2] + (shape[-1], shape[-2])), -1, -2)
        return a.reshape(weights[n].shape)

    small_p = {n: small2d(weights[n], n) for n in SMALL_PARAMS}
    _, grad_x, g_first, g_small = _local_step(
        x.reshape(SEQ, D_MODEL), loss_target.reshape(SEQ, D_MODEL), w_first, small_p, late_weights, early_grads,
        after=w_token, midway=late_pass)

    own_sums, first_parts, small_grad = _exchange_last([blocks(g_first[k], k) for k in FIRST_GROUP],
                                                       _pack_small(g_small))
    own_late, late_parts = _split_wait("grads_late_wait", early["send"], early["recv"], early["srcs"],
                                       early["lands"], True, small_grad)
    own = dict(zip(FIRST_GROUP + LATE_GROUP, list(own_sums) + list(own_late)))
    parts = dict(zip(FIRST_GROUP + LATE_GROUP, list(first_parts) + list(late_parts)))
    outs = {}
    for k in BIG:
        n = BIG[k][0]
        outs[n] = [shard_master(o, k) for o in
                   _adamw_big(own[k], parts[k], shard2d(weights[n], k), shard2d(moms[n], k), shard2d(vars_[n], k),
                              "adamw_" + n)]
    small_out, loss_row = _adamw_small(small_grad, small_p, {n: small2d(moms[n], n) for n in SMALL_PARAMS},
                                       {n: small2d(vars_[n], n) for n in SMALL_PARAMS})
    for n in SMALL_PARAMS:
        outs[n] = [small_master(o, n) for o in small_out[n]]

    result = [loss_row[0, 0], grad_x.reshape(x.shape)]
    for i in range(4):
        result += [outs[n][i] for n in WEIGHT_NAMES]
    return tuple(result)
```
